```python
import math
import jax, jax.numpy as jnp
from jax import lax
import numpy as np

D_MODEL = 1024
BATCH = 8
SEQ = 4096
DEPTH = 1

GMLP_WIDTH = D_MODEL
GMLP_GROUPS = 8
GMLP_GROUP_DIM = GMLP_WIDTH // GMLP_GROUPS
GMLP_CHUNK = 128
HGRN_HEADS = 8
HGRN_DK = 128
HGRN_DV = D_MODEL // HGRN_HEADS
HGRN_KEY_WIDTH = HGRN_HEADS * HGRN_DK
HGRN_VAL_WIDTH = HGRN_HEADS * HGRN_DV
HGRN_CHUNK = 64
HGRN_SCALE = HGRN_DK ** -0.5
N_BRANCHES = 2
D_FF = -(-(8 * D_MODEL) // (3 * 256)) * 256
IN_SIZES = (GMLP_WIDTH, GMLP_WIDTH, HGRN_KEY_WIDTH, HGRN_KEY_WIDTH,
            HGRN_VAL_WIDTH, HGRN_VAL_WIDTH, D_MODEL, D_MODEL)
IN_WIDTH = sum(IN_SIZES)
NORM_EPS = 1e-6

kernel_name = "gmlp_hgrn2_gated_hybrid_block"


def _split_points(sizes):
    pts, acc = [], 0
    for s in sizes[:-1]:
        acc += s
        pts.append(acc)
    return pts


def rms_norm(x, gain):
    xf = x.astype(jnp.float32)
    y = xf * lax.rsqrt(jnp.mean(xf * xf, axis=-1, keepdims=True) + NORM_EPS)
    return (y * gain.astype(jnp.float32)).astype(x.dtype)


def layer_norm(x, gain, bias):
    xf = x.astype(jnp.float32)
    mu = jnp.mean(xf, axis=-1, keepdims=True)
    var = jnp.mean(jnp.square(xf - mu), axis=-1, keepdims=True)
    y = (xf - mu) * lax.rsqrt(var + NORM_EPS)
    return (y * gain.astype(jnp.float32) + bias.astype(jnp.float32)).astype(x.dtype)


def gmlp_spatial_gating(u, v, ln_g, ln_b, w_s, b_s):
    B, S, _ = v.shape
    n_chunks = S // GMLP_CHUNK
    v = layer_norm(v, ln_g, ln_b)
    vc = v.reshape(B, n_chunks, GMLP_CHUNK, GMLP_GROUPS, GMLP_GROUP_DIM)
    causal = jnp.tril(jnp.ones((GMLP_CHUNK, GMLP_CHUNK), dtype=bool))
    w = jnp.where(causal, w_s, jnp.zeros((), w_s.dtype)).astype(v.dtype)
    mixed = jnp.einsum('gts,bnsgd->bntgd', w, vc) + b_s.T.astype(v.dtype)[:, :, None]
    return u * mixed.reshape(B, S, GMLP_WIDTH)


def hgrn2_chunkwise(q, f_logit, v, lb):
    B, S, _ = q.shape
    C = HGRN_CHUNK
    n_chunks = S // C
    f32 = jnp.float32
    lb = lb.astype(f32)
    f = lb + (1.0 - lb) * jax.nn.sigmoid(f_logit.astype(f32))
    k = 1.0 - f
    log_f = jnp.log(f)

    def to_chunks(t, d):
        return t.reshape(B, n_chunks, C, HGRN_HEADS, d).transpose(0, 3, 1, 2, 4)

    qc = to_chunks(q.astype(f32), HGRN_DK) * HGRN_SCALE
    kc = to_chunks(k, HGRN_DK)
    vc = to_chunks(v.astype(f32), HGRN_DV)
    A = jnp.cumsum(to_chunks(log_f, HGRN_DK), axis=3)
    A_mid = A[:, :, :, C // 2 - 1:C // 2, :]
    A_last = A[:, :, :, C - 1:C, :]

    q_in = qc * jnp.exp(A - A_mid)
    k_in = kc * jnp.exp(A_mid - A)
    causal = jnp.tril(jnp.ones((C, C), dtype=bool))
    scores = jnp.where(causal, jnp.einsum('bhctk,bhcsk->bhcts', q_in, k_in), 0.0)
    o_intra = jnp.einsum('bhcts,bhcsv->bhctv', scores, vc)

    dS = jnp.einsum('bhcsk,bhcsv->bhckv', kc * jnp.exp(A_last - A), vc)
    decay = jnp.exp(A_last[:, :, :, 0, :])

    def step(S_prev, xs):
        d, ds = xs
        return d[..., None] * S_prev + ds, S_prev

    S0 = jnp.zeros((B, HGRN_HEADS, HGRN_DK, HGRN_DV), f32)
    _, S_before = lax.scan(step, S0, (jnp.moveaxis(decay, 2, 0), jnp.moveaxis(dS, 2, 0)))
    S_before = jnp.moveaxis(S_before, 0, 2)
    o_inter = jnp.einsum('bhctk,bhckv->bhctv', qc * jnp.exp(A), S_before)

    o = o_intra + o_inter
    return o.transpose(0, 2, 3, 1, 4).reshape(B, S, HGRN_HEADS, HGRN_DV)


def _fwd_setup_inputs(seed: int = 0) -> dict:
    key = jax.random.key(seed)
    ks = jax.random.split(key, 16)
    f32 = jnp.float32

    def normal(k, shape, scale):
        return jax.random.normal(k, shape, f32) * scale

    return {
        "x": normal(ks[0], (BATCH, SEQ, D_MODEL), 1.0),
        "norm_mix_g": 1.0 + normal(ks[1], (DEPTH, D_MODEL), 0.05),
        "w_in": normal(ks[2], (DEPTH, D_MODEL, IN_WIDTH), D_MODEL ** -0.5),
        "gmlp_ln_g": 1.0 + normal(ks[3], (DEPTH, GMLP_WIDTH), 0.05),
        "gmlp_ln_b": normal(ks[4], (DEPTH, GMLP_WIDTH), 0.02),
        "gmlp_w_s": normal(ks[5], (DEPTH, GMLP_GROUPS, GMLP_CHUNK, GMLP_CHUNK), GMLP_CHUNK ** -0.5),
        "gmlp_b_s": 1.0 + normal(ks[6], (DEPTH, GMLP_GROUPS, GMLP_CHUNK), 0.1),
        "hgrn_lb_table": normal(ks[7], (DEPTH + 1, HGRN_KEY_WIDTH), 0.5),
        "hgrn_norm_g": 1.0 + normal(ks[8], (DEPTH, HGRN_DV * HGRN_HEADS), 0.05),
        "w_branch_a": normal(ks[9], (DEPTH, GMLP_WIDTH, D_MODEL), GMLP_WIDTH ** -0.5),
        "w_branch_b": normal(ks[10], (DEPTH, HGRN_VAL_WIDTH, D_MODEL), HGRN_VAL_WIDTH ** -0.5),
        "w_out": normal(ks[11], (DEPTH, D_MODEL, D_MODEL), D_MODEL ** -0.5),
        "norm_ffn_g": 1.0 + normal(ks[12], (DEPTH, D_MODEL), 0.05),
        "w_gate_up": normal(ks[13], (DEPTH, D_MODEL, 2 * D_FF), D_MODEL ** -0.5),
        "w_down": normal(ks[14], (DEPTH, D_FF, D_MODEL), D_FF ** -0.5),
        "norm_final_g": 1.0 + normal(ks[15], (D_MODEL,), 0.05),
    }


def _fwd_reference(x, norm_mix_g, w_in, gmlp_ln_g, gmlp_ln_b, gmlp_w_s, gmlp_b_s,
              hgrn_lb_table, hgrn_norm_g, w_branch_a, w_branch_b, w_out,
              norm_ffn_g, w_gate_up, w_down, norm_final_g):
    B, S, _ = x.shape
    split_pts = _split_points(IN_SIZES)
    lb_all = jnp.cumsum(jax.nn.softmax(hgrn_lb_table.astype(jnp.float32), axis=0), axis=0)

    for l in range(DEPTH):
        h = rms_norm(x, norm_mix_g[l])
        proj = jnp.einsum('bsd,de->bse', h, w_in[l])
        u, v, q, f_logit, i_val, g_out, gate_a, gate_b = jnp.split(proj, split_pts, axis=-1)

        a = gmlp_spatial_gating(jax.nn.gelu(u), jax.nn.gelu(v), gmlp_ln_g[l], gmlp_ln_b[l],
                                gmlp_w_s[l], gmlp_b_s[l])
        y_a = jnp.einsum('bse,ed->bsd', a, w_branch_a[l])

        o = hgrn2_chunkwise(q, f_logit, i_val, lb_all[l])
        o = o * lax.rsqrt(jnp.mean(o * o, axis=-1, keepdims=True) + NORM_EPS)
        o = (o.reshape(B, S, HGRN_VAL_WIDTH) * hgrn_norm_g[l].astype(jnp.float32)).astype(x.dtype)
        o = o * jax.nn.silu(g_out)
        y_b = jnp.einsum('bse,ed->bsd', o, w_branch_b[l])

        merged = jax.nn.sigmoid(gate_a) * y_a + jax.nn.sigmoid(gate_b) * y_b
        x = x + jnp.einsum('bsd,de->bse', merged, w_out[l])

        h = rms_norm(x, norm_ffn_g[l])
        gu = jnp.einsum('bsd,df->bsf', h, w_gate_up[l])
        gate, up = jnp.split(gu, [D_FF], axis=-1)
        x = x + jnp.einsum('bsf,fd->bsd', jax.nn.silu(gate) * up, w_down[l])

    return rms_norm(x, norm_final_g)


import jax as _jax
import jax.numpy as _jnp

TWIN_FORMAT = 'train_step'
FWD_PARAMS = ['x', 'norm_mix_g', 'w_in', 'gmlp_ln_g', 'gmlp_ln_b', 'gmlp_w_s', 'gmlp_b_s', 'hgrn_lb_table', 'hgrn_norm_g', 'w_branch_a', 'w_branch_b', 'w_out', 'norm_ffn_g', 'w_gate_up', 'w_down', 'norm_final_g']
TWIN_WEIGHTS = ['norm_mix_g', 'w_in', 'gmlp_ln_g', 'gmlp_ln_b', 'gmlp_w_s', 'gmlp_b_s', 'hgrn_lb_table', 'hgrn_norm_g', 'w_branch_a', 'w_branch_b', 'w_out', 'norm_ffn_g', 'w_gate_up', 'w_down', 'norm_final_g']
TWIN_DIFF_INPUT = 'x'
TWIN_INPUTS = ['x', 'norm_mix_g', 'w_in', 'gmlp_ln_g', 'gmlp_ln_b', 'gmlp_w_s', 'gmlp_b_s', 'hgrn_lb_table', 'hgrn_norm_g', 'w_branch_a', 'w_branch_b', 'w_out', 'norm_ffn_g', 'w_gate_up', 'w_down', 'norm_final_g', 'loss_target', 'm_norm_mix_g', 'm_w_in', 'm_gmlp_ln_g', 'm_gmlp_ln_b', 'm_gmlp_w_s', 'm_gmlp_b_s', 'm_hgrn_lb_table', 'm_hgrn_norm_g', 'm_w_branch_a', 'm_w_branch_b', 'm_w_out', 'm_norm_ffn_g', 'm_w_gate_up', 'm_w_down', 'm_norm_final_g', 'v_norm_mix_g', 'v_w_in', 'v_gmlp_ln_g', 'v_gmlp_ln_b', 'v_gmlp_w_s', 'v_gmlp_b_s', 'v_hgrn_lb_table', 'v_hgrn_norm_g', 'v_w_branch_a', 'v_w_branch_b', 'v_w_out', 'v_norm_ffn_g', 'v_w_gate_up', 'v_w_down', 'v_norm_final_g']
TWIN_OUTPUTS = ['loss', 'grad_x', 'grad_norm_mix_g', 'grad_w_in', 'grad_gmlp_ln_g', 'grad_gmlp_ln_b', 'grad_gmlp_w_s', 'grad_gmlp_b_s', 'grad_hgrn_lb_table', 'grad_hgrn_norm_g', 'grad_w_branch_a', 'grad_w_branch_b', 'grad_w_out', 'grad_norm_ffn_g', 'grad_w_gate_up', 'grad_w_down', 'grad_norm_final_g', 'delta_norm_mix_g', 'delta_w_in', 'delta_gmlp_ln_g', 'delta_gmlp_ln_b', 'delta_gmlp_w_s', 'delta_gmlp_b_s', 'delta_hgrn_lb_table', 'delta_hgrn_norm_g', 'delta_w_branch_a', 'delta_w_branch_b', 'delta_w_out', 'delta_norm_ffn_g', 'delta_w_gate_up', 'delta_w_down', 'delta_norm_final_g', 'new_m_norm_mix_g', 'new_m_w_in', 'new_m_gmlp_ln_g', 'new_m_gmlp_ln_b', 'new_m_gmlp_w_s', 'new_m_gmlp_b_s', 'new_m_hgrn_lb_table', 'new_m_hgrn_norm_g', 'new_m_w_branch_a', 'new_m_w_branch_b', 'new_m_w_out', 'new_m_norm_ffn_g', 'new_m_w_gate_up', 'new_m_w_down', 'new_m_norm_final_g', 'new_v_norm_mix_g', 'new_v_w_in', 'new_v_gmlp_ln_g', 'new_v_gmlp_ln_b', 'new_v_gmlp_w_s', 'new_v_gmlp_b_s', 'new_v_hgrn_lb_table', 'new_v_hgrn_norm_g', 'new_v_w_branch_a', 'new_v_w_branch_b', 'new_v_w_out', 'new_v_norm_ffn_g', 'new_v_w_gate_up', 'new_v_w_down', 'new_v_norm_final_g']
TWIN_LEAF_KINDS = {'loss': 'loss', 'grad_x': 'grad_x', 'grad_norm_mix_g': 'grad_w', 'grad_w_in': 'grad_w', 'grad_gmlp_ln_g': 'grad_w', 'grad_gmlp_ln_b': 'grad_w', 'grad_gmlp_w_s': 'grad_w', 'grad_gmlp_b_s': 'grad_w', 'grad_hgrn_lb_table': 'grad_w', 'grad_hgrn_norm_g': 'grad_w', 'grad_w_branch_a': 'grad_w', 'grad_w_branch_b': 'grad_w', 'grad_w_out': 'grad_w', 'grad_norm_ffn_g': 'grad_w', 'grad_w_gate_up': 'grad_w', 'grad_w_down': 'grad_w', 'grad_norm_final_g': 'grad_w', 'delta_norm_mix_g': 'delta_w', 'delta_w_in': 'delta_w', 'delta_gmlp_ln_g': 'delta_w', 'delta_gmlp_ln_b': 'delta_w', 'delta_gmlp_w_s': 'delta_w', 'delta_gmlp_b_s': 'delta_w', 'delta_hgrn_lb_table': 'delta_w', 'delta_hgrn_norm_g': 'delta_w', 'delta_w_branch_a': 'delta_w', 'delta_w_branch_b': 'delta_w', 'delta_w_out': 'delta_w', 'delta_norm_ffn_g': 'delta_w', 'delta_w_gate_up': 'delta_w', 'delta_w_down': 'delta_w', 'delta_norm_final_g': 'delta_w', 'new_m_norm_mix_g': 'new_m', 'new_m_w_in': 'new_m', 'new_m_gmlp_ln_g': 'new_m', 'new_m_gmlp_ln_b': 'new_m', 'new_m_gmlp_w_s': 'new_m', 'new_m_gmlp_b_s': 'new_m', 'new_m_hgrn_lb_table': 'new_m', 'new_m_hgrn_norm_g': 'new_m', 'new_m_w_branch_a': 'new_m', 'new_m_w_branch_b': 'new_m', 'new_m_w_out': 'new_m', 'new_m_norm_ffn_g': 'new_m', 'new_m_w_gate_up': 'new_m', 'new_m_w_down': 'new_m', 'new_m_norm_final_g': 'new_m', 'new_v_norm_mix_g': 'new_v', 'new_v_w_in': 'new_v', 'new_v_gmlp_ln_g': 'new_v', 'new_v_gmlp_ln_b': 'new_v', 'new_v_gmlp_w_s': 'new_v', 'new_v_gmlp_b_s': 'new_v', 'new_v_hgrn_lb_table': 'new_v', 'new_v_hgrn_norm_g': 'new_v', 'new_v_w_branch_a': 'new_v', 'new_v_w_branch_b': 'new_v', 'new_v_w_out': 'new_v', 'new_v_norm_ffn_g': 'new_v', 'new_v_w_gate_up': 'new_v', 'new_v_w_down': 'new_v', 'new_v_norm_final_g': 'new_v'}


def _forward(args):
    return _fwd_reference(*[args[k] for k in FWD_PARAMS])


def _output_shape():
    out = _jax.eval_shape(lambda: _forward(_fwd_setup_inputs(0)))
    return out.shape, out.dtype

N_MICROBATCH = 1
ADAM_LR = 0.001
ADAM_B1 = 0.9
ADAM_B2 = 0.999
ADAM_EPS = 1e-08
ADAM_WD = 0.01
ADAM_STEP = 10
PER_EXAMPLE_BATCH_AXIS = {'x': 0, 'loss_target': 0}
SHARED_INPUTS = []
_WEIGHT_DTYPES = {'norm_mix_g': _jnp.float32, 'w_in': _jnp.float32, 'gmlp_ln_g': _jnp.float32, 'gmlp_ln_b': _jnp.float32, 'gmlp_w_s': _jnp.float32, 'gmlp_b_s': _jnp.float32, 'hgrn_lb_table': _jnp.float32, 'hgrn_norm_g': _jnp.float32, 'w_branch_a': _jnp.float32, 'w_branch_b': _jnp.float32, 'w_out': _jnp.float32, 'norm_ffn_g': _jnp.float32, 'w_gate_up': _jnp.float32, 'w_down': _jnp.float32, 'norm_final_g': _jnp.float32}
MOMENT_SCALE = {'norm_mix_g': 1.687266e-01, 'w_in': 5.576355e-02, 'gmlp_ln_g': 4.576127e-02, 'gmlp_ln_b': 4.212769e-02, 'gmlp_w_s': 4.377184e-02, 'gmlp_b_s': 6.414666e-02, 'hgrn_lb_table': 3.311495e-02, 'hgrn_norm_g': 5.880355e-02, 'w_branch_a': 8.016152e-02, 'w_branch_b': 5.641502e-02, 'w_out': 9.828740e-02, 'norm_ffn_g': 1.388463e-01, 'w_gate_up': 5.129314e-02, 'w_down': 8.426746e-02, 'norm_final_g': 3.195608e+01}


def _to_microbatches(a, axis):
    t = _jnp.moveaxis(a, axis, 0)
    t = t.reshape((N_MICROBATCH, t.shape[0] // N_MICROBATCH) + t.shape[1:])
    return _jnp.moveaxis(t, 1, axis + 1)


def setup_inputs(seed: int = 0) -> dict:
    inp = _fwd_setup_inputs(seed)
    key = _jax.random.fold_in(_jax.random.key(seed), 7919)
    shape, _ = _output_shape()
    out = dict(inp)
    out["loss_target"] = _jax.random.normal(_jax.random.fold_in(key, 0), shape, _jnp.float32)
    for i, name in enumerate(TWIN_WEIGHTS):
        w = inp[name].astype(_jnp.float32)
        if MOMENT_SCALE is None:
            s = _jnp.sqrt(_jnp.mean(_jnp.square(w)) + 1e-30)
        else:
            s = MOMENT_SCALE[name]
        km, kv = _jax.random.split(_jax.random.fold_in(key, i + 1))
        out[name] = w
        out["m_" + name] = s * _jax.random.normal(km, w.shape, _jnp.float32)
        out["v_" + name] = (s * s) * _jax.random.uniform(kv, w.shape, _jnp.float32, 0.5, 1.5)
    if N_MICROBATCH > 1:
        for name, axis in PER_EXAMPLE_BATCH_AXIS.items():
            out[name] = _to_microbatches(out[name], axis)
    return {'x': out['x'], 'norm_mix_g': out['norm_mix_g'], 'w_in': out['w_in'], 'gmlp_ln_g': out['gmlp_ln_g'], 'gmlp_ln_b': out['gmlp_ln_b'], 'gmlp_w_s': out['gmlp_w_s'], 'gmlp_b_s': out['gmlp_b_s'], 'hgrn_lb_table': out['hgrn_lb_table'], 'hgrn_norm_g': out['hgrn_norm_g'], 'w_branch_a': out['w_branch_a'], 'w_branch_b': out['w_branch_b'], 'w_out': out['w_out'], 'norm_ffn_g': out['norm_ffn_g'], 'w_gate_up': out['w_gate_up'], 'w_down': out['w_down'], 'norm_final_g': out['norm_final_g'], 'loss_target': out['loss_target'], 'm_norm_mix_g': out['m_norm_mix_g'], 'm_w_in': out['m_w_in'], 'm_gmlp_ln_g': out['m_gmlp_ln_g'], 'm_gmlp_ln_b': out['m_gmlp_ln_b'], 'm_gmlp_w_s': out['m_gmlp_w_s'], 'm_gmlp_b_s': out['m_gmlp_b_s'], 'm_hgrn_lb_table': out['m_hgrn_lb_table'], 'm_hgrn_norm_g': out['m_hgrn_norm_g'], 'm_w_branch_a': out['m_w_branch_a'], 'm_w_branch_b': out['m_w_branch_b'], 'm_w_out': out['m_w_out'], 'm_norm_ffn_g': out['m_norm_ffn_g'], 'm_w_gate_up': out['m_w_gate_up'], 'm_w_down': out['m_w_down'], 'm_norm_final_g': out['m_norm_final_g'], 'v_norm_mix_g': out['v_norm_mix_g'], 'v_w_in': out['v_w_in'], 'v_gmlp_ln_g': out['v_gmlp_ln_g'], 'v_gmlp_ln_b': out['v_gmlp_ln_b'], 'v_gmlp_w_s': out['v_gmlp_w_s'], 'v_gmlp_b_s': out['v_gmlp_b_s'], 'v_hgrn_lb_table': out['v_hgrn_lb_table'], 'v_hgrn_norm_g': out['v_hgrn_norm_g'], 'v_w_branch_a': out['v_w_branch_a'], 'v_w_branch_b': out['v_w_branch_b'], 'v_w_out': out['v_w_out'], 'v_norm_ffn_g': out['v_norm_ffn_g'], 'v_w_gate_up': out['v_w_gate_up'], 'v_w_down': out['v_w_down'], 'v_norm_final_g': out['v_norm_final_g']}


def _loss(weights, diff, rest, loss_target):
    with _jax.named_scope("forward"):
        args = {**rest, TWIN_DIFF_INPUT: diff, **{k: w.astype(_WEIGHT_DTYPES[k]) for k, w in weights.items()}}
        y = _forward(args)
    with _jax.named_scope("loss_head"):
        err = _jnp.square(y.astype(_jnp.float32) - loss_target)
        return 0.5 * _jnp.sum(_jnp.mean(err, axis=-1)) if err.ndim else 0.5 * err


def _adamw(w, g, m, v):
    m = ADAM_B1 * m + (1.0 - ADAM_B1) * g
    v = ADAM_B2 * v + (1.0 - ADAM_B2) * _jnp.square(g)
    m_hat = m / (1.0 - ADAM_B1 ** ADAM_STEP)
    v_hat = v / (1.0 - ADAM_B2 ** ADAM_STEP)
    delta = -ADAM_LR * (m_hat / (_jnp.sqrt(v_hat) + ADAM_EPS) + ADAM_WD * w)
    return delta, m, v


def reference(x, norm_mix_g, w_in, gmlp_ln_g, gmlp_ln_b, gmlp_w_s, gmlp_b_s, hgrn_lb_table, hgrn_norm_g, w_branch_a, w_branch_b, w_out, norm_ffn_g, w_gate_up, w_down, norm_final_g, loss_target, m_norm_mix_g, m_w_in, m_gmlp_ln_g, m_gmlp_ln_b, m_gmlp_w_s, m_gmlp_b_s, m_hgrn_lb_table, m_hgrn_norm_g, m_w_branch_a, m_w_branch_b, m_w_out, m_norm_ffn_g, m_w_gate_up, m_w_down, m_norm_final_g, v_norm_mix_g, v_w_in, v_gmlp_ln_g, v_gmlp_ln_b, v_gmlp_w_s, v_gmlp_b_s, v_hgrn_lb_table, v_hgrn_norm_g, v_w_branch_a, v_w_branch_b, v_w_out, v_norm_ffn_g, v_w_gate_up, v_w_down, v_norm_final_g):
    given = dict(x=x, norm_mix_g=norm_mix_g, w_in=w_in, gmlp_ln_g=gmlp_ln_g, gmlp_ln_b=gmlp_ln_b, gmlp_w_s=gmlp_w_s, gmlp_b_s=gmlp_b_s, hgrn_lb_table=hgrn_lb_table, hgrn_norm_g=hgrn_norm_g, w_branch_a=w_branch_a, w_branch_b=w_branch_b, w_out=w_out, norm_ffn_g=norm_ffn_g, w_gate_up=w_gate_up, w_down=w_down, norm_final_g=norm_final_g, loss_target=loss_target, m_norm_mix_g=m_norm_mix_g, m_w_in=m_w_in, m_gmlp_ln_g=m_gmlp_ln_g, m_gmlp_ln_b=m_gmlp_ln_b, m_gmlp_w_s=m_gmlp_w_s, m_gmlp_b_s=m_gmlp_b_s, m_hgrn_lb_table=m_hgrn_lb_table, m_hgrn_norm_g=m_hgrn_norm_g, m_w_branch_a=m_w_branch_a, m_w_branch_b=m_w_branch_b, m_w_out=m_w_out, m_norm_ffn_g=m_norm_ffn_g, m_w_gate_up=m_w_gate_up, m_w_down=m_w_down, m_norm_final_g=m_norm_final_g, v_norm_mix_g=v_norm_mix_g, v_w_in=v_w_in, v_gmlp_ln_g=v_gmlp_ln_g, v_gmlp_ln_b=v_gmlp_ln_b, v_gmlp_w_s=v_gmlp_w_s, v_gmlp_b_s=v_gmlp_b_s, v_hgrn_lb_table=v_hgrn_lb_table, v_hgrn_norm_g=v_hgrn_norm_g, v_w_branch_a=v_w_branch_a, v_w_branch_b=v_w_branch_b, v_w_out=v_w_out, v_norm_ffn_g=v_norm_ffn_g, v_w_gate_up=v_w_gate_up, v_w_down=v_w_down, v_norm_final_g=v_norm_final_g)
    weights = {n: given[n] for n in TWIN_WEIGHTS}
    shared = {n: given[n] for n in SHARED_INPUTS}
    per_example = {n: given[n] for n in ['x']}
    grad_fn = _jax.value_and_grad(_loss, argnums=(0, 1))

    def one_microbatch(ex, loss_target):
        ex = dict(ex)
        diff = ex.pop(TWIN_DIFF_INPUT)
        return grad_fn(weights, diff, {**shared, **ex}, loss_target)

    if N_MICROBATCH == 1:
        loss, (grad_w, grad_x) = one_microbatch(per_example, given["loss_target"])
    else:
        def body(carry, xs):
            loss_sum, grad_sum = carry
            l_k, (gw_k, gx_k) = one_microbatch(xs[0], xs[1])
            with _jax.named_scope("update"):
                return (loss_sum + l_k, _jax.tree.map(_jnp.add, grad_sum, gw_k)), gx_k

        init = (_jnp.zeros((), _jnp.float32), _jax.tree.map(_jnp.zeros_like, weights))
        (loss, grad_w), grad_x = _jax.lax.scan(body, init, (per_example, given["loss_target"]))
    with _jax.named_scope("update"):
        delta_w, new_m, new_v = {}, {}, {}
        for n in TWIN_WEIGHTS:
            delta_w[n], new_m[n], new_v[n] = _adamw(weights[n], grad_w[n], given["m_" + n], given["v_" + n])
    return (loss, grad_x, *[grad_w[n] for n in TWIN_WEIGHTS], *[delta_w[n] for n in TWIN_WEIGHTS],
            *[new_m[n] for n in TWIN_WEIGHTS], *[new_v[n] for n in TWIN_WEIGHTS])
```

```python
import functools
import math

import jax
import jax.numpy as jnp
from jax import lax
from jax.experimental import pallas as pl
from jax.experimental.pallas import tpu as pltpu

F32 = jnp.float32
BF16 = jnp.bfloat16
SDS = jax.ShapeDtypeStruct
MESH = pl.DeviceIdType.MESH
ANY = pl.BlockSpec(memory_space=pl.ANY)

D = 1024
NIN = 8
NG = 8
GCH = 128
NH = 8
HD = 128
HCH = 64
FF = 2816
FFS = 1408
NCHIP = 4
EPS = 1e-6
QSCALE = HD ** -0.5
GELU_C0 = math.sqrt(2.0 / math.pi)
GELU_C1 = 0.044715
LR, B1, B2, AEPS, WD, STEP = 0.001, 0.9, 0.999, 1e-08, 0.01, 10
VMEM_LIMIT_V7X = 56 * 1024 * 1024
SP_ROWS = 144


def _cparams(**kw):
    return pltpu.CompilerParams(vmem_limit_bytes=VMEM_LIMIT_V7X, **kw)


def _mm(a, b):
    return jnp.dot(a, b, preferred_element_type=F32)


def _mm_nt(a, b):
    return lax.dot_general(a, b, (((1,), (1,)), ((), ())), preferred_element_type=F32)


def _mm_tn(a, b):
    return lax.dot_general(a, b, (((0,), (0,)), ((), ())), preferred_element_type=F32)


def _rows8(x):
    r, c = x.shape
    return jnp.sum(x.reshape(r // 8, 8, c), axis=0)


def _mean(x):
    return jnp.mean(x, axis=-1, keepdims=True)


def _sigmoid(x):
    return 1.0 / (1.0 + jnp.exp(-x))


def _gelu(x):
    t = jnp.tanh(GELU_C0 * (x + GELU_C1 * x * x * x))
    return 0.5 * x * (1.0 + t), t


def _gelu_grad(x, t):
    return 0.5 * (1.0 + t) + 0.5 * x * (1.0 - t * t) * (GELU_C0 * (1.0 + 3.0 * GELU_C1 * x * x))


def _orig_group(m):
    return jnp.where(m < 6, (m + 2) % 6, m)


def _proj_fwd(x, g_mix, w_in4):
    T = x.shape[0]
    tm = min(512, T)

    def body(x_ref, g_ref, w_ref, proj_ref, h_ref, hs):
        @pl.when(pl.program_id(1) == 0)
        def _():
            xv = x_ref[...]
            r = lax.rsqrt(_mean(xv * xv) + EPS)
            hb = (xv * r * g_ref[...]).astype(BF16)
            hs[...] = hb
            h_ref[...] = hb
        proj_ref[...] = _mm(hs[...], w_ref[...])

    return pl.pallas_call(
        body, name="proj_fwd", grid=(T // tm, NIN),
        in_specs=[pl.BlockSpec((tm, D), lambda i, j: (i, 0)),
                  pl.BlockSpec((1, D), lambda i, j: (0, 0)),
                  pl.BlockSpec((None, D, D), lambda i, j: (j // 2, 0, j % 2))],
        out_specs=[pl.BlockSpec((tm, D), lambda i, j: (i, j)),
                   pl.BlockSpec((tm, D), lambda i, j: (i, 0))],
        out_shape=[SDS((T, NIN * D), F32), SDS((T, D), BF16)],
        scratch_shapes=[pltpu.VMEM((tm, D), BF16)],
        compiler_params=_cparams(),
    )(x, g_mix, w_in4)


def _layer_norm_stats(gv):
    mu = _mean(gv)
    xc = gv - mu
    rs = lax.rsqrt(_mean(xc * xc) + EPS)
    return xc * rs, rs


def _gmlp_fwd(proj, ln_g, ln_b, wm, b_t, w_a):
    T = proj.shape[0]
    tm = min(256, T)

    def body(u_ref, v_ref, lg_ref, lb_ref, wm_ref, bt_ref, wa_ref, a_ref, ya_ref, a_s):
        gu, _ = _gelu(u_ref[...])
        gv, _ = _gelu(v_ref[...])
        vhat, _ = _layer_norm_stats(gv)
        vnb = (vhat * lg_ref[...] + lb_ref[...]).astype(BF16)
        for ch in range(tm // GCH):
            rows = slice(GCH * ch, GCH * (ch + 1))
            for g in range(NG):
                cols = slice(128 * g, 128 * (g + 1))
                mixed = _mm(wm_ref[g], vnb[rows, cols]) + bt_ref[:, g:g + 1]
                a_s[rows, cols] = gu[rows, cols] * mixed
        ab = a_s[...].astype(BF16)
        a_ref[...] = ab
        ya_ref[...] = _mm(ab, wa_ref[...])

    row = lambda i: (0, 0)
    return pl.pallas_call(
        body, name="gmlp_fwd", grid=(T // tm,),
        in_specs=[pl.BlockSpec((tm, D), lambda i: (i, 0)), pl.BlockSpec((tm, D), lambda i: (i, 1)),
                  pl.BlockSpec((1, D), row), pl.BlockSpec((1, D), row),
                  pl.BlockSpec((NG, GCH, GCH), lambda i: (0, 0, 0)), pl.BlockSpec((GCH, NG), row),
                  pl.BlockSpec((D, D), row)],
        out_specs=[pl.BlockSpec((tm, D), lambda i: (i, 0)), pl.BlockSpec((tm, D), lambda i: (i, 0))],
        out_shape=[SDS((T, D), BF16), SDS((T, D), F32)],
        scratch_shapes=[pltpu.VMEM((tm, D), F32)],
        compiler_params=_cparams(),
    )(proj, proj, ln_g, ln_b, wm, b_t, w_a)


def _cumsum64(x, row):
    for s in (1, 2, 4, 8, 16, 32):
        x = x + jnp.where(row >= s, pltpu.roll(x, s, 0), 0.0)
    return x


def _revcumsum64(x, row):
    n = x.shape[0]
    for s in (1, 2, 4, 8, 16, 32):
        x = x + jnp.where(row < HCH - s, pltpu.roll(x, n - s, 0), 0.0)
    return x


def _seg_sum(x):
    n, c = x.shape
    s = jnp.sum(x.reshape(n // HCH, HCH, c), axis=1, keepdims=True)
    return jnp.broadcast_to(s, (n // HCH, HCH, c)).reshape(n, c)


def _hgrn_gates(fl, lbv, row):
    s = _sigmoid(fl)
    f = lbv + (1.0 - lbv) * s
    a = _cumsum64(jnp.log(f), row)
    a_mid = _seg_sum(jnp.where(row == HCH // 2 - 1, a, 0.0))
    a_last = _seg_sum(jnp.where(row == HCH - 1, a, 0.0))
    return s, f, a, a_mid, a_last


def _hgrn_fwd(proj, lb_table, norm_g):
    T = proj.shape[0]
    tb = min(512, T)
    nc = tb // HCH

    def body(q_ref, fl_ref, v_ref, g_ref, lbt_ref, gn_ref, o_ref, ob_ref, stb_ref, st_s, o_s):
        @pl.when(pl.program_id(1) == 0)
        def _():
            st_s[...] = jnp.zeros_like(st_s)

        row = lax.broadcasted_iota(jnp.int32, (tb, HD), 0) & (HCH - 1)
        lbv = _sigmoid(lbt_ref[0:1, :] - lbt_ref[1:2, :])
        _, f, a, a_mid, a_last = _hgrn_gates(fl_ref[...], lbv, row)
        k = 1.0 - f
        qs = q_ref[...] * QSCALE
        q_in = (qs * jnp.exp(a - a_mid)).astype(BF16)
        k_in = (k * jnp.exp(a_mid - a)).astype(BF16)
        q_a = (qs * jnp.exp(a)).astype(BF16)
        k_d = (k * jnp.exp(a_last - a)).astype(BF16)
        dec = jnp.exp(a_last)
        vb = v_ref[...].astype(BF16)
        tri = (lax.broadcasted_iota(jnp.int32, (HCH, HCH), 0)
               >= lax.broadcasted_iota(jnp.int32, (HCH, HCH), 1))
        for c in range(nc):
            sl = slice(HCH * c, HCH * (c + 1))
            st = st_s[...]
            stb_ref[c] = st
            sc = jnp.where(tri, _mm_nt(q_in[sl], k_in[sl]), 0.0)
            o_s[sl, :] = _mm(sc.astype(BF16), vb[sl]) + _mm_nt(q_a[sl], st.astype(BF16))
            d64 = dec[sl]
            st_s[...] = st * jnp.concatenate([d64, d64], axis=0) + _mm_tn(vb[sl], k_d[sl])
        o = o_s[...]
        r = lax.rsqrt(_mean(o * o) + EPS)
        g = g_ref[...]
        o_ref[...] = o
        ob_ref[...] = (o * r * gn_ref[...] * (g * _sigmoid(g))).astype(BF16)

    def col(off):
        return pl.BlockSpec((tb, HD), lambda h, cb: (cb, off * NH + h))

    return pl.pallas_call(
        body, name="hgrn_fwd", grid=(NH, T // tb),
        in_specs=[col(2), col(3), col(4), col(5),
                  pl.BlockSpec((2, HD), lambda h, cb: (0, h)), pl.BlockSpec((1, HD), lambda h, cb: (0, h))],
        out_specs=[pl.BlockSpec((tb, HD), lambda h, cb: (cb, h)), pl.BlockSpec((tb, HD), lambda h, cb: (cb, h)),
                   pl.BlockSpec((None, nc, HD, HD), lambda h, cb: (h, cb, 0, 0))],
        out_shape=[SDS((T, D), F32), SDS((T, D), BF16), SDS((NH, T // HCH, HD, HD), F32)],
        scratch_shapes=[pltpu.VMEM((HD, HD), F32), pltpu.VMEM((tb, HD), F32)],
        compiler_params=_cparams(),
    )(proj, proj, proj, proj, lb_table, norm_g)


def _merge_fwd(x, y_a, ob, proj, w_b, w_out):
    T = x.shape[0]
    tm = min(512, T)

    def body(x_ref, ya_ref, ob_ref, ga_ref, gb_ref, wb_ref, wo_ref, yb_ref, mg_ref, x1_ref):
        yb = _mm(ob_ref[...], wb_ref[...])
        merged = (_sigmoid(ga_ref[...]) * ya_ref[...] + _sigmoid(gb_ref[...]) * yb).astype(BF16)
        yb_ref[...] = yb
        mg_ref[...] = merged
        x1_ref[...] = x_ref[...] + _mm(merged, wo_ref[...])

    t = lambda i: (i, 0)
    w = lambda i: (0, 0)
    return pl.pallas_call(
        body, name="merge_fwd", grid=(T // tm,),
        in_specs=[pl.BlockSpec((tm, D), t), pl.BlockSpec((tm, D), t), pl.BlockSpec((tm, D), t),
                  pl.BlockSpec((tm, D), lambda i: (i, 6)), pl.BlockSpec((tm, D), lambda i: (i, 7)),
                  pl.BlockSpec((D, D), w), pl.BlockSpec((D, D), w)],
        out_specs=[pl.BlockSpec((tm, D), t)] * 3,
        out_shape=[SDS((T, D), F32), SDS((T, D), BF16), SDS((T, D), F32)],
        compiler_params=_cparams(),
    )(x, y_a, ob, proj, proj, w_b, w_out)


def _ffn_fwd_bwd(x1, target, g_ffn, g_fin, w_gu4, w_down):
    T = x1.shape[0]
    tm = min(256, T)
    inv_d = 1.0 / D

    def body(x1_ref, tg_ref, gf_ref, gn_ref, wgu_ref, wd_ref,
             act_ref, dx2b_ref, h2b_ref, dgu_ref, dx1_ref, dx1b_ref, acc_ref):
        @pl.when(pl.program_id(0) == 0)
        def _():
            acc_ref[...] = jnp.zeros_like(acc_ref)

        x1v = x1_ref[...]
        gf = gf_ref[...]
        gn = gn_ref[...]
        rr1 = lax.rsqrt(_mean(x1v * x1v) + EPS)
        x1n = x1v * rr1
        h2b = (x1n * gf).astype(BF16)
        h2b_ref[...] = h2b
        p = [_mm(h2b, wgu_ref[k]) for k in range(NCHIP)]
        sg = [_sigmoid(p[0]), _sigmoid(p[1])]
        si = [p[0] * sg[0], p[1] * sg[1]]
        x2 = x1v
        for k in range(2):
            actk = (si[k] * p[2 + k]).astype(BF16)
            act_ref[:, FFS * k:FFS * (k + 1)] = actk
            x2 = x2 + _mm(actk, wd_ref[FFS * k:FFS * (k + 1), :])
        rr2 = lax.rsqrt(_mean(x2 * x2) + EPS)
        x2n = x2 * rr2
        e = x2n * gn - tg_ref[...]
        acc_ref[0] += _rows8(e * e) * (0.5 * inv_d)
        dy = e * inv_d
        acc_ref[1] += _rows8(dy * x2n)
        dxn = dy * gn
        dx2 = rr2 * (dxn - x2n * _mean(dxn * x2n))
        dx2b = dx2.astype(BF16)
        dx2b_ref[...] = dx2b
        dh2 = None
        for k in range(2):
            dact = _mm_nt(dx2b, wd_ref[FFS * k:FFS * (k + 1), :])
            dgate = (dact * p[2 + k] * (sg[k] * (1.0 + p[k] * (1.0 - sg[k])))).astype(BF16)
            dup = (dact * si[k]).astype(BF16)
            dgu_ref[k] = dgate
            dgu_ref[2 + k] = dup
            part = _mm_nt(dgate, wgu_ref[k]) + _mm_nt(dup, wgu_ref[2 + k])
            dh2 = part if dh2 is None else dh2 + part
        acc_ref[2] += _rows8(dh2 * x1n)
        dxn1 = dh2 * gf
        dx1 = dx2 + rr1 * (dxn1 - x1n * _mean(dxn1 * x1n))
        dx1_ref[...] = dx1
        dx1b_ref[...] = dx1.astype(BF16)

    t = lambda i: (i, 0)
    w = lambda i: (0, 0)
    one = pl.Buffered(1)
    return pl.pallas_call(
        body, name="ffn_fwd_bwd", grid=(T // tm,),
        in_specs=[pl.BlockSpec((tm, D), t), pl.BlockSpec((tm, D), t),
                  pl.BlockSpec((1, D), w), pl.BlockSpec((1, D), w),
                  pl.BlockSpec((NCHIP, D, FFS), lambda i: (0, 0, 0), pipeline_mode=one),
                  pl.BlockSpec((FF, D), w, pipeline_mode=one)],
        out_specs=[pl.BlockSpec((tm, FF), t), pl.BlockSpec((tm, D), t), pl.BlockSpec((tm, D), t),
                   pl.BlockSpec((NCHIP, tm, FFS), lambda i: (0, i, 0)),
                   pl.BlockSpec((tm, D), t), pl.BlockSpec((tm, D), t),
                   pl.BlockSpec((3, 8, D), lambda i: (0, 0, 0))],
        out_shape=[SDS((T, FF), BF16), SDS((T, D), BF16), SDS((T, D), BF16),
                   SDS((NCHIP, T, FFS), BF16), SDS((T, D), F32), SDS((T, D), BF16),
                   SDS((3, 8, D), F32)],
        compiler_params=_cparams(),
    )(x1, target, g_ffn, g_fin, w_gu4, w_down)


def _merge_bwd(dx1b, y_a, y_b, proj, w_out, w_a, w_b):
    T = dx1b.shape[0]
    tm = min(512, T)

    def body(dx_ref, ya_ref, yb_ref, ga_ref, gb_ref, wo_ref, wa_ref, wb_ref,
             dya_ref, dyb_ref, da_ref, dob_ref, dp_ref):
        dm = _mm_nt(dx_ref[...], wo_ref[...])
        sa = _sigmoid(ga_ref[...])
        sb = _sigmoid(gb_ref[...])
        dya = (dm * sa).astype(BF16)
        dyb = (dm * sb).astype(BF16)
        dya_ref[...] = dya
        dyb_ref[...] = dyb
        dp_ref[0] = (dm * ya_ref[...] * sa * (1.0 - sa)).astype(BF16)
        dp_ref[1] = (dm * yb_ref[...] * sb * (1.0 - sb)).astype(BF16)
        da_ref[...] = _mm_nt(dya, wa_ref[...])
        dob_ref[...] = _mm_nt(dyb, wb_ref[...])

    t = lambda i: (i, 0)
    w = lambda i: (0, 0)
    return pl.pallas_call(
        body, name="merge_bwd", grid=(T // tm,),
        in_specs=[pl.BlockSpec((tm, D), t), pl.BlockSpec((tm, D), t), pl.BlockSpec((tm, D), t),
                  pl.BlockSpec((tm, D), lambda i: (i, 6)), pl.BlockSpec((tm, D), lambda i: (i, 7)),
                  pl.BlockSpec((D, D), w), pl.BlockSpec((D, D), w), pl.BlockSpec((D, D), w)],
        out_specs=[pl.BlockSpec((tm, D), t)] * 4 + [pl.BlockSpec((2, tm, D), lambda i: (3, i, 0))],
        out_shape=[SDS((T, D), BF16), SDS((T, D), BF16), SDS((T, D), F32), SDS((T, D), F32),
                   SDS((NIN, T, D), BF16)],
        compiler_params=_cparams(),
    )(dx1b, y_a, y_b, proj, proj, w_out, w_a, w_b)


def _hgrn_bwd(dproj, dob, o_raw, proj, st_before, lb_table, norm_g):
    T = dob.shape[0]
    tb = min(512, T)
    nc = tb // HCH
    nb = T // tb

    def body(dp_in, dob_ref, o_ref, q_ref, fl_ref, v_ref, g_ref, stb_ref, lbt_ref, gn_ref,
             dp_ref, acc_ref, dst_s, dqin_s, dqa_s, dkin_s, dkd_s, dv_s, ddec_s):
        del dp_in

        @pl.when(pl.program_id(1) == 0)
        def _():
            dst_s[...] = jnp.zeros_like(dst_s)
            acc_ref[...] = jnp.zeros_like(acc_ref)

        row = lax.broadcasted_iota(jnp.int32, (tb, HD), 0) & (HCH - 1)
        gn = gn_ref[...]
        lbv = _sigmoid(lbt_ref[0:1, :] - lbt_ref[1:2, :])
        o = o_ref[...]
        r = lax.rsqrt(_mean(o * o) + EPS)
        on = o * r
        g = g_ref[...]
        sgm = _sigmoid(g)
        dob_v = dob_ref[...]
        dp_ref[3] = (dob_v * on * gn * (sgm * (1.0 + g * (1.0 - sgm)))).astype(BF16)
        do_n = dob_v * (g * sgm)
        acc_ref[1] += _rows8(do_n * on)
        dxn = do_n * gn
        do = (r * (dxn - on * _mean(dxn * on))).astype(BF16)
        s, f, a, a_mid, a_last = _hgrn_gates(fl_ref[...], lbv, row)
        k = 1.0 - f
        qs = q_ref[...] * QSCALE
        e_q = jnp.exp(a - a_mid)
        e_k = jnp.exp(a_mid - a)
        e_a = jnp.exp(a)
        e_l = jnp.exp(a_last - a)
        dec = jnp.exp(a_last)
        q_in = qs * e_q
        k_in = k * e_k
        q_a = qs * e_a
        k_d = k * e_l
        q_inb, k_inb, q_ab, k_db = (z.astype(BF16) for z in (q_in, k_in, q_a, k_d))
        vb = v_ref[...].astype(BF16)
        tri = (lax.broadcasted_iota(jnp.int32, (HCH, HCH), 0)
               >= lax.broadcasted_iota(jnp.int32, (HCH, HCH), 1))
        for c in reversed(range(nc)):
            sl = slice(HCH * c, HCH * (c + 1))
            stp = stb_ref[c]
            dst = dst_s[...]
            dstb = dst.astype(BF16)
            do_c = do[sl]
            dqa_s[sl, :] = _mm(do_c, stp.astype(BF16))
            dkd_s[sl, :] = _mm(vb[sl], dstb)
            ddec_s[sl, :] = jnp.broadcast_to(jnp.sum(dst * stp, axis=0, keepdims=True), (HCH, HD))
            sc = jnp.where(tri, _mm_nt(q_inb[sl], k_inb[sl]), 0.0).astype(BF16)
            dsc = jnp.where(tri, _mm_nt(do_c, vb[sl]), 0.0).astype(BF16)
            dv_s[sl, :] = _mm_nt(k_db[sl], dstb) + _mm_tn(sc, do_c)
            dqin_s[sl, :] = _mm(dsc, k_inb[sl])
            dkin_s[sl, :] = _mm_tn(dsc, q_inb[sl])
            d64 = dec[sl]
            dst_s[...] = dst * jnp.concatenate([d64, d64], axis=0) + _mm_tn(do_c, q_ab[sl])
        dq_in = dqin_s[...]
        dq_a = dqa_s[...]
        dk_in = dkin_s[...]
        dk_d = dkd_s[...]
        dp_ref[0] = ((dq_in * e_q + dq_a * e_a) * QSCALE).astype(BF16)
        dp_ref[2] = dv_s[...].astype(BF16)
        tq = dq_in * q_in
        tk = dk_in * k_in
        td = dk_d * k_d
        d_a = tq + dq_a * q_a - tk - td
        d_a = d_a + jnp.where(row == HCH // 2 - 1, _seg_sum(tk - tq), 0.0)
        d_a = d_a + jnp.where(row == HCH - 1, _seg_sum(td) + ddec_s[...] * dec, 0.0)
        dlf = _revcumsum64(d_a, row)
        df = dlf / f - (dk_in * e_k + dk_d * e_l)
        dp_ref[1] = (df * (1.0 - lbv) * s * (1.0 - s)).astype(BF16)
        acc_ref[0] += _rows8(df * (1.0 - s))

    def col(off):
        return pl.BlockSpec((tb, HD), lambda h, cb: (nb - 1 - cb, off * NH + h))

    hb = lambda h, cb: (nb - 1 - cb, h)
    return pl.pallas_call(
        body, name="hgrn_bwd", grid=(NH, nb),
        in_specs=[ANY, pl.BlockSpec((tb, HD), hb), pl.BlockSpec((tb, HD), hb),
                  col(2), col(3), col(4), col(5),
                  pl.BlockSpec((None, nc, HD, HD), lambda h, cb: (h, nb - 1 - cb, 0, 0)),
                  pl.BlockSpec((2, HD), lambda h, cb: (0, h)), pl.BlockSpec((1, HD), lambda h, cb: (0, h))],
        out_specs=[pl.BlockSpec((4, tb, HD), lambda h, cb: (0, nb - 1 - cb, h)),
                   pl.BlockSpec((2, 8, HD), lambda h, cb: (0, 0, h))],
        out_shape=[SDS(dproj.shape, BF16), SDS((2, 8, D), F32)],
        scratch_shapes=[pltpu.VMEM((HD, HD), F32)] + [pltpu.VMEM((tb, HD), F32)] * 6,
        input_output_aliases={0: 0},
        compiler_params=_cparams(),
    )(dproj, dob, o_raw, proj, proj, proj, proj, st_before, lb_table, norm_g)


def _gmlp_bwd(dproj, da, proj, ln_g, ln_b, wm, wm_t, b_t):
    T = da.shape[0]
    tm = min(256, T)

    def body(dp_in, da_ref, u_ref, v_ref, lg_ref, lb_ref, wm_ref, wmt_ref, bt_ref,
             dp_ref, acc_ref, dws_ref, dmix_ref, du_s, dvn_s):
        del dp_in

        @pl.when(pl.program_id(0) == 0)
        def _():
            acc_ref[...] = jnp.zeros_like(acc_ref)
            dws_ref[...] = jnp.zeros_like(dws_ref)
            dmix_ref[...] = jnp.zeros_like(dmix_ref)

        u = u_ref[...]
        v = v_ref[...]
        lg = lg_ref[...]
        gu, t_u = _gelu(u)
        gv, t_v = _gelu(v)
        vhat, rs = _layer_norm_stats(gv)
        vnb = (vhat * lg + lb_ref[...]).astype(BF16)
        da_v = da_ref[...]
        for ch in range(tm // GCH):
            rows = slice(GCH * ch, GCH * (ch + 1))
            for g in range(NG):
                cols = slice(128 * g, 128 * (g + 1))
                vng = vnb[rows, cols]
                mixed = _mm(wm_ref[g], vng) + bt_ref[:, g:g + 1]
                dag = da_v[rows, cols]
                dmx = dag * gu[rows, cols]
                du_s[rows, cols] = dag * mixed
                dmxb = dmx.astype(BF16)
                dws_ref[:, cols] += _mm_nt(dmxb, vng)
                dmix_ref[:, cols] += dmx
                dvn_s[rows, cols] = _mm(wmt_ref[g], dmxb)
        dp_ref[0] = (du_s[...] * _gelu_grad(u, t_u)).astype(BF16)
        dvn = dvn_s[...]
        acc_ref[0] += _rows8(dvn * vhat)
        acc_ref[1] += _rows8(dvn)
        dvh = dvn * lg
        dgv = rs * (dvh - _mean(dvh) - vhat * _mean(dvh * vhat))
        dp_ref[1] = (dgv * _gelu_grad(v, t_v)).astype(BF16)

    row = lambda i: (0, 0)
    w3 = lambda i: (0, 0, 0)
    return pl.pallas_call(
        body, name="gmlp_bwd", grid=(T // tm,),
        in_specs=[ANY, pl.BlockSpec((tm, D), lambda i: (i, 0)),
                  pl.BlockSpec((tm, D), lambda i: (i, 0)), pl.BlockSpec((tm, D), lambda i: (i, 1)),
                  pl.BlockSpec((1, D), row), pl.BlockSpec((1, D), row),
                  pl.BlockSpec((NG, GCH, GCH), w3), pl.BlockSpec((NG, GCH, GCH), w3),
                  pl.BlockSpec((GCH, NG), row)],
        out_specs=[pl.BlockSpec((2, tm, D), lambda i: (2, i, 0)),
                   pl.BlockSpec((2, 8, D), w3), pl.BlockSpec((GCH, D), row), pl.BlockSpec((GCH, D), row)],
        out_shape=[SDS(dproj.shape, BF16), SDS((2, 8, D), F32), SDS((GCH, D), F32), SDS((GCH, D), F32)],
        scratch_shapes=[pltpu.VMEM((tm, D), F32), pltpu.VMEM((tm, D), F32)],
        input_output_aliases={0: 0},
        compiler_params=_cparams(),
    )(dproj, da, proj, proj, ln_g, ln_b, wm, wm_t, b_t)


def _proj_bwd(dproj, w_in4, x, dx1, g_mix):
    T = x.shape[0]
    tm = min(256, T)
    order = (2, 3, 4, 5, 0, 1, 6, 7)

    def body(dp_ref, w_ref, x_ref, dx1_ref, g_ref, gx_ref, acc_ref):
        @pl.when(pl.program_id(0) == 0)
        def _():
            acc_ref[...] = jnp.zeros_like(acc_ref)

        dh = None
        for m, og in enumerate(order):
            part = _mm_nt(dp_ref[m], w_ref[og // 2, :, D * (og % 2):D * (og % 2 + 1)])
            dh = part if dh is None else dh + part
        xv = x_ref[...]
        r = lax.rsqrt(_mean(xv * xv) + EPS)
        xn = xv * r
        acc_ref[...] += _rows8(dh * xn)
        dxn = dh * g_ref[...]
        gx_ref[...] = dx1_ref[...] + r * (dxn - xn * _mean(dxn * xn))

    t = lambda i: (i, 0)
    return pl.pallas_call(
        body, name="proj_bwd", grid=(T // tm,),
        in_specs=[pl.BlockSpec((NIN, tm, D), lambda i: (0, i, 0)),
                  pl.BlockSpec((NCHIP, D, 2 * D), lambda i: (0, 0, 0), pipeline_mode=pl.Buffered(1)),
                  pl.BlockSpec((tm, D), t), pl.BlockSpec((tm, D), t), pl.BlockSpec((1, D), lambda i: (0, 0))],
        out_specs=[pl.BlockSpec((tm, D), t), pl.BlockSpec((8, D), lambda i: (0, 0))],
        out_shape=[SDS((T, D), F32), SDS((8, D), F32)],
        compiler_params=_cparams(),
    )(dproj, w_in4, x, dx1, g_mix)


def _dw_call(name, a, b, a_spec, b_spec, o_spec, out_shape, nblk, tt):
    T = a.shape[-2]

    def body(a_ref, b_ref, o_ref):
        @pl.when(pl.program_id(1) == 0)
        def _():
            o_ref[...] = jnp.zeros_like(o_ref)
        o_ref[...] += _mm_tn(a_ref[...], b_ref[...])

    return pl.pallas_call(
        body, name=name, grid=(nblk, T // tt),
        in_specs=[a_spec, b_spec], out_specs=o_spec, out_shape=out_shape,
        compiler_params=_cparams(),
    )(a, b)


def _weight_grads(hb, dproj, ab, dya, obb, dyb, mgb, dx1b, h2b, dgu4, act, dx2b):
    T = hb.shape[0]
    tt = min(512, T)
    g_in = _dw_call(
        "dw_in", hb, dproj,
        pl.BlockSpec((tt, D), lambda m, t: (t, 0)),
        pl.BlockSpec((None, tt, D), lambda m, t: (m, t, 0)),
        pl.BlockSpec((None, D, D), lambda m, t: (_orig_group(m) // 2, 0, _orig_group(m) % 2)),
        SDS((NCHIP, D, 2 * D), F32), NIN, tt)
    g_gu = _dw_call(
        "dw_gate_up", h2b, dgu4,
        pl.BlockSpec((tt, D), lambda k, t: (t, 0)),
        pl.BlockSpec((None, tt, FFS), lambda k, t: (k, t, 0)),
        pl.BlockSpec((None, D, FFS), lambda k, t: (k, 0, 0)),
        SDS((NCHIP, D, FFS), F32), NCHIP, tt)
    g_down = _dw_call(
        "dw_down", act, dx2b,
        pl.BlockSpec((tt, FFS), lambda k, t: (t, k)),
        pl.BlockSpec((tt, D), lambda k, t: (t, 0)),
        pl.BlockSpec((FFS, D), lambda k, t: (k, 0)),
        SDS((FF, D), F32), 2, tt)

    def square(name, a, b):
        return _dw_call(
            name, a, b,
            pl.BlockSpec((tt, D), lambda k, t: (t, 0)), pl.BlockSpec((tt, D), lambda k, t: (t, 0)),
            pl.BlockSpec((D, D), lambda k, t: (0, 0)), SDS((D, D), F32), 1, tt)

    g_a = square("dw_branch_a", ab, dya)
    g_b = square("dw_branch_b", obb, dyb)
    g_out = square("dw_out", mgb, dx1b)
    return (g_in, g_gu, g_a.reshape(NCHIP, D // NCHIP, D), g_b.reshape(NCHIP, D // NCHIP, D),
            g_out.reshape(NCHIP, D // NCHIP, D), g_down.reshape(NCHIP, FF // NCHIP, D))


def _place():
    x, y, c = lax.axis_index("x"), lax.axis_index("y"), lax.axis_index("c")
    return x, y, c, 2 * x + y


def _chip_at(x, y, s):
    return x ^ (s >> 1), y ^ (s & 1)


def _gather_weights(shards):
    n = len(shards)

    def body(*refs):
        src = refs[:n]
        out = refs[n:2 * n]
        send_sem, recv_sem, fsend_sem, frecv_sem, loc_sem = refs[2 * n:]
        x, y, c, j = _place()
        sib = (x, y, 1 - c)
        copies = []
        for w in range(n):
            half = src[w].shape[0] // 2
            own = pltpu.make_async_copy(src[w], out[w].at[j], loc_sem.at[w])
            own.start()
            copies.append(own)
            mine = pl.ds(pl.multiple_of(c * half, 16), half)
            for s in range(1, NCHIP):
                cx, cy = _chip_at(x, y, s)
                cp = pltpu.make_async_remote_copy(
                    src_ref=src[w].at[mine], dst_ref=out[w].at[j, mine],
                    send_sem=send_sem.at[w, s - 1], recv_sem=recv_sem.at[w, s - 1],
                    device_id=(cx, cy, c), device_id_type=MESH)
                cp.start()
                copies.append(cp)
        fwd = []
        for s in range(1, NCHIP):
            for w in range(n):
                half = src[w].shape[0] // 2
                mine = pl.ds(pl.multiple_of(c * half, 16), half)
                landed = out[w].at[j ^ s, mine]
                pltpu.make_async_remote_copy(
                    src_ref=landed, dst_ref=landed, send_sem=send_sem.at[w, s - 1],
                    recv_sem=recv_sem.at[w, s - 1], device_id=sib, device_id_type=MESH).wait_recv()
                cp = pltpu.make_async_remote_copy(
                    src_ref=landed, dst_ref=landed, send_sem=fsend_sem.at[w, s - 1],
                    recv_sem=frecv_sem.at[w, s - 1], device_id=sib, device_id_type=MESH)
                cp.start()
                fwd.append(cp)
        for s in range(1, NCHIP):
            for w in range(n):
                half = src[w].shape[0] // 2
                theirs = out[w].at[j ^ s, pl.ds(pl.multiple_of((1 - c) * half, 16), half)]
                pltpu.make_async_remote_copy(
                    src_ref=theirs, dst_ref=theirs, send_sem=fsend_sem.at[w, s - 1],
                    recv_sem=frecv_sem.at[w, s - 1], device_id=sib, device_id_type=MESH).wait_recv()
        for w in range(n):
            copies[w * NCHIP].wait()
            for s in range(1, NCHIP):
                copies[w * NCHIP + s].wait_send()
        for cp in fwd:
            cp.wait_send()

    return pl.pallas_call(
        body, name="gather_weights",
        in_specs=[ANY] * n, out_specs=[ANY] * n,
        out_shape=[SDS((NCHIP,) + s.shape, s.dtype) for s in shards],
        scratch_shapes=[pltpu.SemaphoreType.DMA((n, NCHIP - 1))] * 4 + [pltpu.SemaphoreType.DMA((n,))],
        compiler_params=pltpu.CompilerParams(has_side_effects=True),
    )(*shards)


def _pair_exchange(grads):
    n = len(grads)

    def body(*refs):
        src = refs[:n]
        out = refs[n:2 * n]
        send_sem, recv_sem = refs[2 * n:]
        x, y, c, _ = _place()
        cps = []
        for w in range(n):
            half = src[w].shape[1] // 2
            theirs = pl.ds(pl.multiple_of((1 - c) * half, 8), half)
            cp = pltpu.make_async_remote_copy(
                src_ref=src[w].at[:, theirs, :], dst_ref=out[w], send_sem=send_sem.at[w],
                recv_sem=recv_sem.at[w], device_id=(x, y, 1 - c), device_id_type=MESH)
            cp.start()
            cps.append(cp)
        for cp in cps:
            cp.wait()

    return pl.pallas_call(
        body, name="rs_pair_exchange",
        in_specs=[ANY] * n, out_specs=[ANY] * n,
        out_shape=[SDS((NCHIP, g.shape[1] // 2, g.shape[2]), F32) for g in grads],
        scratch_shapes=[pltpu.SemaphoreType.DMA((n,))] * 2,
        compiler_params=pltpu.CompilerParams(has_side_effects=True),
    )(*grads)


def _row_tile(rows):
    return 176 if rows % 176 == 0 and rows % 128 else 128


def _pair_sum(name, place, g, sib):
    half, cols = sib.shape[1], sib.shape[2]
    tr = _row_tile(half)
    nt = half // tr

    def body(pc_ref, g_ref, s_ref, own_ref, out_ref):
        del pc_ref
        v = g_ref[...] + s_ref[...]
        out_ref[...] = v.astype(BF16)

        @pl.when(pl.program_id(1) == 0)
        def _():
            own_ref[...] = v

    return pl.pallas_call(
        body, name=name,
        grid_spec=pltpu.PrefetchScalarGridSpec(
            num_scalar_prefetch=1, grid=(nt, NCHIP),
            in_specs=[pl.BlockSpec((None, tr, cols), lambda i, s, pc: (pc[0] ^ s, pc[1] * nt + i, 0)),
                      pl.BlockSpec((None, tr, cols), lambda i, s, pc: (pc[0] ^ s, i, 0))],
            out_specs=[pl.BlockSpec((tr, cols), lambda i, s, pc: (i, 0)),
                       pl.BlockSpec((None, tr, cols), lambda i, s, pc: (s, i, 0))]),
        out_shape=[SDS((half, cols), F32), SDS((NCHIP, half, cols), BF16)],
        compiler_params=_cparams(),
    )(place, g, sib)


def _chip_exchange(parts):
    n = len(parts)

    def body(*refs):
        src = refs[:n]
        out = refs[n:2 * n]
        send_sem, recv_sem = refs[2 * n:]
        x, y, c, _ = _place()
        cps = []
        for w in range(n):
            for s in range(1, NCHIP):
                cx, cy = _chip_at(x, y, s)
                cp = pltpu.make_async_remote_copy(
                    src_ref=src[w].at[s], dst_ref=out[w].at[s - 1], send_sem=send_sem.at[w, s - 1],
                    recv_sem=recv_sem.at[w, s - 1], device_id=(cx, cy, c), device_id_type=MESH)
                cp.start()
                cps.append(cp)
        for cp in cps:
            cp.wait()

    return pl.pallas_call(
        body, name="rs_chip_exchange",
        in_specs=[ANY] * n, out_specs=[ANY] * n,
        out_shape=[SDS((NCHIP - 1,) + p.shape[1:], BF16) for p in parts],
        scratch_shapes=[pltpu.SemaphoreType.DMA((n, NCHIP - 1))] * 2,
        compiler_params=pltpu.CompilerParams(has_side_effects=True),
    )(*parts)


def _chip_sum(name, own, rem):
    half, cols = own.shape
    tr = _row_tile(half)

    def body(own_ref, rem_ref, out_ref):
        out_ref[...] = ((own_ref[...] + rem_ref[0].astype(F32)) + rem_ref[1].astype(F32)) + rem_ref[2].astype(F32)

    return pl.pallas_call(
        body, name=name, grid=(half // tr,),
        in_specs=[pl.BlockSpec((tr, cols), lambda i: (i, 0)),
                  pl.BlockSpec((NCHIP - 1, tr, cols), lambda i: (0, i, 0))],
        out_specs=pl.BlockSpec((tr, cols), lambda i: (i, 0)),
        out_shape=SDS((half, cols), F32),
        compiler_params=_cparams(),
    )(own, rem)


def _share_halves(halves):
    n = len(halves)

    def body(*refs):
        src = refs[:n]
        out = refs[n:2 * n]
        send_sem, recv_sem, loc_sem = refs[2 * n:]
        x, y, c, _ = _place()
        cps = []
        for w in range(n):
            half = src[w].shape[0]
            mine = pl.ds(pl.multiple_of(c * half, 8), half)
            loc = pltpu.make_async_copy(src[w], out[w].at[mine], loc_sem.at[w])
            loc.start()
            cp = pltpu.make_async_remote_copy(
                src_ref=src[w], dst_ref=out[w].at[mine], send_sem=send_sem.at[w],
                recv_sem=recv_sem.at[w], device_id=(x, y, 1 - c), device_id_type=MESH)
            cp.start()
            cps.append((loc, cp))
        for w in range(n):
            half = src[w].shape[0]
            theirs = out[w].at[pl.ds(pl.multiple_of((1 - c) * half, 8), half)]
            loc, cp = cps[w]
            loc.wait()
            cp.wait_send()
            pltpu.make_async_remote_copy(
                src_ref=src[w], dst_ref=theirs, send_sem=send_sem.at[w], recv_sem=recv_sem.at[w],
                device_id=(x, y, 1 - c), device_id_type=MESH).wait_recv()

    return pl.pallas_call(
        body, name="rs_share_halves",
        in_specs=[ANY] * n, out_specs=[ANY] * n,
        out_shape=[SDS((2 * h.shape[0], h.shape[1]), F32) for h in halves],
        scratch_shapes=[pltpu.SemaphoreType.DMA((n,))] * 3,
        compiler_params=pltpu.CompilerParams(has_side_effects=True),
    )(*halves)


def _adamw_math(w, g, m, v):
    m = B1 * m + (1.0 - B1) * g
    v = B2 * v + (1.0 - B2) * (g * g)
    m_hat = m / (1.0 - B1 ** STEP)
    v_hat = v / (1.0 - B2 ** STEP)
    delta = -LR * (m_hat / (jnp.sqrt(v_hat) + AEPS) + WD * w)
    return delta, m, v


def _adamw(name, w, g, m, v):
    rows, cols = w.shape
    tr = 352 if rows % 352 == 0 else 256

    def body(w_ref, g_ref, m_ref, v_ref, d_ref, mo_ref, vo_ref):
        d, mn, vn = _adamw_math(w_ref[...], g_ref[...], m_ref[...], v_ref[...])
        d_ref[...] = d
        mo_ref[...] = mn
        vo_ref[...] = vn

    spec = pl.BlockSpec((tr, cols), lambda i: (i, 0))
    return pl.pallas_call(
        body, name=name, grid=(rows // tr,),
        in_specs=[spec] * 4, out_specs=[spec] * 3, out_shape=[SDS((rows, cols), F32)] * 3,
        compiler_params=_cparams(),
    )(w, g, m, v)


def _small_allreduce_adamw(sp, w, m, v):
    shape = sp.shape

    def body(sp_ref, w_ref, m_ref, v_ref, g_ref, d_ref, mo_ref, vo_ref,
             sib_s, pair_s, chip_s, send_sem, recv_sem):
        x, y, c, j = _place()
        cp = pltpu.make_async_remote_copy(
            src_ref=sp_ref, dst_ref=sib_s, send_sem=send_sem.at[0], recv_sem=recv_sem.at[0],
            device_id=(x, y, 1 - c), device_id_type=MESH)
        cp.start()
        cp.wait()
        pair_s[...] = sp_ref[...] + sib_s[...]
        cps = []
        for s in range(1, NCHIP):
            cx, cy = _chip_at(x, y, s)
            cp = pltpu.make_async_remote_copy(
                src_ref=pair_s, dst_ref=chip_s.at[s], send_sem=send_sem.at[s], recv_sem=recv_sem.at[s],
                device_id=(cx, cy, c), device_id_type=MESH)
            cp.start()
            cps.append(cp)
        chip_s[0] = pair_s[...]
        for cp in cps:
            cp.wait()
        tot = chip_s[j]
        for k in range(1, NCHIP):
            tot = tot + chip_s[k ^ j]
        g_ref[...] = tot
        d, mn, vn = _adamw_math(w_ref[...], tot, m_ref[...], v_ref[...])
        d_ref[...] = d
        mo_ref[...] = mn
        vo_ref[...] = vn

    vm = pl.BlockSpec(memory_space=pltpu.VMEM)
    return pl.pallas_call(
        body, name="small_allreduce_adamw",
        in_specs=[vm] * 4, out_specs=[vm] * 4, out_shape=[SDS(shape, F32)] * 4,
        scratch_shapes=[pltpu.VMEM(shape, F32), pltpu.VMEM(shape, F32), pltpu.VMEM((NCHIP,) + shape, F32),
                        pltpu.SemaphoreType.DMA((NCHIP,)), pltpu.SemaphoreType.DMA((NCHIP,))],
        compiler_params=pltpu.CompilerParams(has_side_effects=True),
    )(sp, w, m, v)


def _pack_small(first, mix, ln_g, ln_b, b_s, lbt, hn, ffn, fin, w_s):
    rows = [first.reshape(1, D), mix.reshape(1, D), ln_g.reshape(1, D), ln_b.reshape(1, D),
            b_s.reshape(1, D), lbt.reshape(2, D), hn.reshape(1, D), ffn.reshape(1, D), fin.reshape(1, D),
            jnp.zeros((6, D), F32)]
    return jnp.concatenate(rows + [w_s.reshape(NG, GCH, GCH).transpose(1, 0, 2).reshape(GCH, D)], axis=0)


def _unpack_small(p):
    w_s = p[16:].reshape(GCH, NG, GCH).transpose(1, 0, 2).reshape(1, NG, GCH, GCH)
    return dict(norm_mix_g=p[1:2], gmlp_ln_g=p[2:3], gmlp_ln_b=p[3:4], gmlp_b_s=p[4].reshape(1, NG, GCH),
                hgrn_lb_table=p[5:7], hgrn_norm_g=p[7:8], norm_ffn_g=p[8:9], norm_final_g=p[9],
                gmlp_w_s=w_s)


SMALL = ("norm_mix_g", "gmlp_ln_g", "gmlp_ln_b", "gmlp_w_s", "gmlp_b_s", "hgrn_lb_table", "hgrn_norm_g",
         "norm_ffn_g", "norm_final_g")
BIG = ("w_in", "w_gate_up", "w_branch_a", "w_branch_b", "w_out", "w_down")
ORDER = ("norm_mix_g", "w_in", "gmlp_ln_g", "gmlp_ln_b", "gmlp_w_s", "gmlp_b_s", "hgrn_lb_table",
         "hgrn_norm_g", "w_branch_a", "w_branch_b", "w_out", "norm_ffn_g", "w_gate_up", "w_down",
         "norm_final_g")


def kernel(x, norm_mix_g, w_in, gmlp_ln_g, gmlp_ln_b, gmlp_w_s, gmlp_b_s, hgrn_lb_table, hgrn_norm_g, w_branch_a, w_branch_b, w_out, norm_ffn_g, w_gate_up, w_down, norm_final_g, loss_target, m_norm_mix_g, m_w_in, m_gmlp_ln_g, m_gmlp_ln_b, m_gmlp_w_s, m_gmlp_b_s, m_hgrn_lb_table, m_hgrn_norm_g, m_w_branch_a, m_w_branch_b, m_w_out, m_norm_ffn_g, m_w_gate_up, m_w_down, m_norm_final_g, v_norm_mix_g, v_w_in, v_gmlp_ln_g, v_gmlp_ln_b, v_gmlp_w_s, v_gmlp_b_s, v_hgrn_lb_table, v_hgrn_norm_g, v_w_branch_a, v_w_branch_b, v_w_out, v_norm_ffn_g, v_w_gate_up, v_w_down, v_norm_final_g):
    args = dict(locals())
    T = x.shape[1]
    xs = x.reshape(T, D)
    target = loss_target.reshape(T, D)
    big = {n: args[n].reshape(args[n].shape[1:]) for n in BIG}
    big_m = {n: args["m_" + n].reshape(args[n].shape[1:]) for n in BIG}
    big_v = {n: args["v_" + n].reshape(args[n].shape[1:]) for n in BIG}

    gathered = _gather_weights([big[n].astype(BF16) for n in BIG])
    w_in4, w_gu4, w_a4, w_b4, w_out4, w_down4 = gathered
    w_a, w_b, w_o = (w.reshape(D, D) for w in (w_a4, w_b4, w_out4))
    w_dn = w_down4.reshape(FF, D)
    tril = jnp.tril(jnp.ones((GCH, GCH), bool))
    wm = jnp.where(tril, gmlp_w_s[0], 0.0).astype(BF16)
    wm_t = jnp.swapaxes(wm, 1, 2)
    b_t = gmlp_b_s[0].T

    proj, hb = _proj_fwd(xs, norm_mix_g, w_in4)
    ab, y_a = _gmlp_fwd(proj, gmlp_ln_g, gmlp_ln_b, wm, b_t, w_a)
    o_raw, obb, st_before = _hgrn_fwd(proj, hgrn_lb_table, hgrn_norm_g)
    y_b, mgb, x1 = _merge_fwd(xs, y_a, obb, proj, w_b, w_o)
    act, dx2b, h2b, dgu4, dx1, dx1b, acc_ffn = _ffn_fwd_bwd(
        x1, target, norm_ffn_g, norm_final_g.reshape(1, D), w_gu4, w_dn)

    dya, dyb, da, dob, dproj = _merge_bwd(dx1b, y_a, y_b, proj, w_o, w_a, w_b)
    dproj, acc_hgrn = _hgrn_bwd(dproj, dob, o_raw, proj, st_before, hgrn_lb_table, hgrn_norm_g)
    dproj, acc_ln, dws, dmix = _gmlp_bwd(dproj, da, proj, gmlp_ln_g, gmlp_ln_b, wm, wm_t, b_t)
    grad_x, acc_mix = _proj_bwd(dproj, w_in4, xs, dx1, norm_mix_g)
    grads = _weight_grads(hb, dproj, ab, dya, obb, dyb, mgb, dx1b, h2b, dgu4, act, dx2b)

    x_i, y_i, c_i = lax.axis_index("x"), lax.axis_index("y"), lax.axis_index("c")
    place = jnp.stack([2 * x_i + y_i, c_i]).astype(jnp.int32)
    sib = _pair_exchange(list(grads))
    owns, parts = zip(*[_pair_sum("rs_pair_sum_" + n, place, g, s) for n, g, s in zip(BIG, grads, sib)])
    rems = _chip_exchange(list(parts))
    halves = [_chip_sum("rs_chip_sum_" + n, o, r) for n, o, r in zip(BIG, owns, rems)]
    fulls = _share_halves(halves)
    out = {}
    for n, g in zip(BIG, fulls):
        d, mn, vn = _adamw("adamw_" + n, big[n], g, big_m[n], big_v[n])
        shp = args[n].shape
        out[n] = (g.reshape(shp), d.reshape(shp), mn.reshape(shp), vn.reshape(shp))

    lbv = jax.nn.sigmoid(hgrn_lb_table[0] - hgrn_lb_table[1])
    d_t0 = jnp.sum(acc_hgrn[0], axis=0) * lbv * (1.0 - lbv)
    loss_row = jnp.zeros((D,), F32).at[0].set(jnp.sum(acc_ffn[0]))
    dws_m = jnp.where(tril[:, None, :], dws.reshape(GCH, NG, GCH), 0.0).transpose(1, 0, 2)
    db_s = jnp.sum(dmix.reshape(GCH, NG, GCH), axis=-1).T
    sp = _pack_small(loss_row, jnp.sum(acc_mix, 0), jnp.sum(acc_ln[0], 0), jnp.sum(acc_ln[1], 0), db_s,
                     jnp.stack([d_t0, -d_t0]), jnp.sum(acc_hgrn[1], 0), jnp.sum(acc_ffn[2], 0),
                     jnp.sum(acc_ffn[1], 0), dws_m)
    zero = jnp.zeros((D,), F32)

    def pack(prefix):
        a = lambda n: args[prefix + n]
        return _pack_small(zero, a("norm_mix_g"), a("gmlp_ln_g"), a("gmlp_ln_b"), a("gmlp_b_s"),
                           a("hgrn_lb_table"), a("hgrn_norm_g"), a("norm_ffn_g"), a("norm_final_g"),
                           a("gmlp_w_s"))

    packed = _small_allreduce_adamw(sp, pack(""), pack("m_"), pack("v_"))
    loss = packed[0][0, 0]
    small = [_unpack_small(p) for p in packed]
    for n in SMALL:
        out[n] = tuple(s[n] for s in small)
    return (loss, grad_x.reshape(x.shape), *[out[n][0] for n in ORDER], *[out[n][1] for n in ORDER],
            *[out[n][2] for n in ORDER], *[out[n][3] for n in ORDER])
```

```python
import functools
import math

import jax
import jax.numpy as jnp
from jax import lax
from jax.experimental import pallas as pl
from jax.experimental.pallas import tpu as pltpu

F32 = jnp.float32
BF16 = jnp.bfloat16
SDS = jax.ShapeDtypeStruct
MESH = pl.DeviceIdType.MESH
ANY = pl.BlockSpec(memory_space=pl.ANY)

D = 1024
NIN = 8
NG = 8
GCH = 128
NH = 8
HD = 128
HCH = 64
FF = 2816
FFS = 1408
NCHIP = 4
EPS = 1e-6
QSCALE = HD ** -0.5
GELU_C0 = math.sqrt(2.0 / math.pi)
GELU_C1 = 0.044715
LR, B1, B2, AEPS, WD, STEP = 0.001, 0.9, 0.999, 1e-08, 0.01, 10
VMEM_LIMIT_V7X = 56 * 1024 * 1024
SP_ROWS = 144


def _cparams(**kw):
    return pltpu.CompilerParams(vmem_limit_bytes=VMEM_LIMIT_V7X, **kw)


def _mm(a, b):
    return jnp.dot(a, b, preferred_element_type=F32)


def _mm_nt(a, b):
    return lax.dot_general(a, b, (((1,), (1,)), ((), ())), preferred_element_type=F32)


def _mm_tn(a, b):
    return lax.dot_general(a, b, (((0,), (0,)), ((), ())), preferred_element_type=F32)


def _rows8(x):
    r, c = x.shape
    return jnp.sum(x.reshape(r // 8, 8, c), axis=0)


def _mean(x):
    return jnp.mean(x, axis=-1, keepdims=True)


def _sigmoid(x):
    return 1.0 / (1.0 + jnp.exp(-x))


def _gelu(x):
    t = jnp.tanh(GELU_C0 * (x + GELU_C1 * x * x * x))
    return 0.5 * x * (1.0 + t), t


def _gelu_grad(x, t):
    return 0.5 * (1.0 + t) + 0.5 * x * (1.0 - t * t) * (GELU_C0 * (1.0 + 3.0 * GELU_C1 * x * x))


def _orig_group(m):
    return jnp.where(m < 6, (m + 2) % 6, m)


def _proj_fwd(x, g_mix, w_in4, later):
    T = x.shape[0]
    tm = min(512, T)
    n = len(later)
    ni = T // tm

    def body(x_ref, g_ref, w_ref, *rest):
        proj_ref, h_ref = rest[n:n + 2]
        gathered = rest[n + 2:2 * n + 2]
        hs = rest[2 * n + 2]
        sems = rest[2 * n + 3:]
        i, j = pl.program_id(0), pl.program_id(1)

        @pl.when((i == 0) & (j == 0))
        def _():
            _gather_start(gathered, sems)

        @pl.when(j == 0)
        def _():
            xv = x_ref[...]
            r = lax.rsqrt(_mean(xv * xv) + EPS)
            hb = (xv * r * g_ref[...]).astype(BF16)
            hs[...] = hb
            h_ref[...] = hb
        proj_ref[...] = _mm(hs[...], w_ref[...])

        @pl.when((i == ni - 1) & (j == NIN - 1))
        def _():
            _gather_finish(gathered, sems)

    return pl.pallas_call(
        body, name="proj_fwd", grid=(ni, NIN),
        in_specs=[pl.BlockSpec((tm, D), lambda i, j: (i, 0)),
                  pl.BlockSpec((1, D), lambda i, j: (0, 0)),
                  pl.BlockSpec((None, D, D), lambda i, j: (j // 2, 0, j % 2))] + [ANY] * n,
        out_specs=[pl.BlockSpec((tm, D), lambda i, j: (i, j)),
                   pl.BlockSpec((tm, D), lambda i, j: (i, 0))] + [ANY] * n,
        out_shape=[SDS((T, NIN * D), F32), SDS((T, D), BF16)] + [SDS(a.shape, a.dtype) for a in later],
        scratch_shapes=[pltpu.VMEM((tm, D), BF16)] + [pltpu.SemaphoreType.DMA((n, NCHIP - 1))] * 4,
        input_output_aliases={3 + k: 2 + k for k in range(n)},
        compiler_params=_cparams(has_side_effects=True),
    )(x, g_mix, w_in4, *later)


def _layer_norm_stats(gv):
    mu = _mean(gv)
    xc = gv - mu
    rs = lax.rsqrt(_mean(xc * xc) + EPS)
    return xc * rs, rs


def _gmlp_fwd(proj, ln_g, ln_b, wm, b_t, w_a):
    T = proj.shape[0]
    tm = min(256, T)

    def body(u_ref, v_ref, lg_ref, lb_ref, wm_ref, bt_ref, wa_ref, a_ref, ya_ref, a_s):
        gu, _ = _gelu(u_ref[...])
        gv, _ = _gelu(v_ref[...])
        vhat, _ = _layer_norm_stats(gv)
        vnb = (vhat * lg_ref[...] + lb_ref[...]).astype(BF16)
        for ch in range(tm // GCH):
            rows = slice(GCH * ch, GCH * (ch + 1))
            for g in range(NG):
                cols = slice(128 * g, 128 * (g + 1))
                mixed = _mm(wm_ref[g], vnb[rows, cols]) + bt_ref[:, g:g + 1]
                a_s[rows, cols] = gu[rows, cols] * mixed
        ab = a_s[...].astype(BF16)
        a_ref[...] = ab
        ya_ref[...] = _mm(ab, wa_ref[...])

    row = lambda i: (0, 0)
    return pl.pallas_call(
        body, name="gmlp_fwd", grid=(T // tm,),
        in_specs=[pl.BlockSpec((tm, D), lambda i: (i, 0)), pl.BlockSpec((tm, D), lambda i: (i, 1)),
                  pl.BlockSpec((1, D), row), pl.BlockSpec((1, D), row),
                  pl.BlockSpec((NG, GCH, GCH), lambda i: (0, 0, 0)), pl.BlockSpec((GCH, NG), row),
                  pl.BlockSpec((D, D), row)],
        out_specs=[pl.BlockSpec((tm, D), lambda i: (i, 0)), pl.BlockSpec((tm, D), lambda i: (i, 0))],
        out_shape=[SDS((T, D), BF16), SDS((T, D), F32)],
        scratch_shapes=[pltpu.VMEM((tm, D), F32)],
        compiler_params=_cparams(),
    )(proj, proj, ln_g, ln_b, wm, b_t, w_a)


def _cumsum64(x, row):
    for s in (1, 2, 4, 8, 16, 32):
        x = x + jnp.where(row >= s, pltpu.roll(x, s, 0), 0.0)
    return x


def _revcumsum64(x, row):
    n = x.shape[0]
    for s in (1, 2, 4, 8, 16, 32):
        x = x + jnp.where(row < HCH - s, pltpu.roll(x, n - s, 0), 0.0)
    return x


def _seg_sum(x):
    n, c = x.shape
    s = jnp.sum(x.reshape(n // HCH, HCH, c), axis=1, keepdims=True)
    return jnp.broadcast_to(s, (n // HCH, HCH, c)).reshape(n, c)


def _hgrn_gates(fl, lbv, row):
    s = _sigmoid(fl)
    f = lbv + (1.0 - lbv) * s
    a = _cumsum64(jnp.log(f), row)
    a_mid = _seg_sum(jnp.where(row == HCH // 2 - 1, a, 0.0))
    a_last = _seg_sum(jnp.where(row == HCH - 1, a, 0.0))
    return s, f, a, a_mid, a_last


def _hgrn_fwd(proj, lb_table, norm_g):
    T = proj.shape[0]
    tb = min(512, T)
    nc = tb // HCH

    def body(q_ref, fl_ref, v_ref, g_ref, lbt_ref, gn_ref, o_ref, ob_ref, stb_ref, st_s, o_s):
        @pl.when(pl.program_id(1) == 0)
        def _():
            st_s[...] = jnp.zeros_like(st_s)

        row = lax.broadcasted_iota(jnp.int32, (tb, HD), 0) & (HCH - 1)
        lbv = _sigmoid(lbt_ref[0:1, :] - lbt_ref[1:2, :])
        _, f, a, a_mid, a_last = _hgrn_gates(fl_ref[...], lbv, row)
        k = 1.0 - f
        qs = q_ref[...] * QSCALE
        q_in = (qs * jnp.exp(a - a_mid)).astype(BF16)
        k_in = (k * jnp.exp(a_mid - a)).astype(BF16)
        q_a = (qs * jnp.exp(a)).astype(BF16)
        k_d = (k * jnp.exp(a_last - a)).astype(BF16)
        dec = jnp.exp(a_last)
        vb = v_ref[...].astype(BF16)
        tri = (lax.broadcasted_iota(jnp.int32, (HCH, HCH), 0)
               >= lax.broadcasted_iota(jnp.int32, (HCH, HCH), 1))
        for c in range(nc):
            sl = slice(HCH * c, HCH * (c + 1))
            st = st_s[...]
            stb_ref[c] = st
            sc = jnp.where(tri, _mm_nt(q_in[sl], k_in[sl]), 0.0)
            o_s[sl, :] = _mm(sc.astype(BF16), vb[sl]) + _mm_nt(q_a[sl], st.astype(BF16))
            d64 = dec[sl]
            st_s[...] = st * jnp.concatenate([d64, d64], axis=0) + _mm_tn(vb[sl], k_d[sl])
        o = o_s[...]
        r = lax.rsqrt(_mean(o * o) + EPS)
        g = g_ref[...]
        o_ref[...] = o
        ob_ref[...] = (o * r * gn_ref[...] * (g * _sigmoid(g))).astype(BF16)

    def col(off):
        return pl.BlockSpec((tb, HD), lambda h, cb: (cb, off * NH + h))

    return pl.pallas_call(
        body, name="hgrn_fwd", grid=(NH, T // tb),
        in_specs=[col(2), col(3), col(4), col(5),
                  pl.BlockSpec((2, HD), lambda h, cb: (0, h)), pl.BlockSpec((1, HD), lambda h, cb: (0, h))],
        out_specs=[pl.BlockSpec((tb, HD), lambda h, cb: (cb, h)), pl.BlockSpec((tb, HD), lambda h, cb: (cb, h)),
                   pl.BlockSpec((None, nc, HD, HD), lambda h, cb: (h, cb, 0, 0))],
        out_shape=[SDS((T, D), F32), SDS((T, D), BF16), SDS((NH, T // HCH, HD, HD), F32)],
        scratch_shapes=[pltpu.VMEM((HD, HD), F32), pltpu.VMEM((tb, HD), F32)],
        compiler_params=_cparams(),
    )(proj, proj, proj, proj, lb_table, norm_g)


def _merge_fwd(x, y_a, ob, proj, w_b, w_out):
    T = x.shape[0]
    tm = min(512, T)

    def body(x_ref, ya_ref, ob_ref, ga_ref, gb_ref, wb_ref, wo_ref, yb_ref, mg_ref, x1_ref):
        yb = _mm(ob_ref[...], wb_ref[...])
        merged = (_sigmoid(ga_ref[...]) * ya_ref[...] + _sigmoid(gb_ref[...]) * yb).astype(BF16)
        yb_ref[...] = yb
        mg_ref[...] = merged
        x1_ref[...] = x_ref[...] + _mm(merged, wo_ref[...])

    t = lambda i: (i, 0)
    w = lambda i: (0, 0)
    return pl.pallas_call(
        body, name="merge_fwd", grid=(T // tm,),
        in_specs=[pl.BlockSpec((tm, D), t), pl.BlockSpec((tm, D), t), pl.BlockSpec((tm, D), t),
                  pl.BlockSpec((tm, D), lambda i: (i, 6)), pl.BlockSpec((tm, D), lambda i: (i, 7)),
                  pl.BlockSpec((D, D), w), pl.BlockSpec((D, D), w)],
        out_specs=[pl.BlockSpec((tm, D), t)] * 3,
        out_shape=[SDS((T, D), F32), SDS((T, D), BF16), SDS((T, D), F32)],
        compiler_params=_cparams(),
    )(x, y_a, ob, proj, proj, w_b, w_out)


def _ffn_fwd_bwd(x1, target, g_ffn, g_fin, w_gu4, w_down):
    T = x1.shape[0]
    tm = min(256, T)
    inv_d = 1.0 / D

    def body(x1_ref, tg_ref, gf_ref, gn_ref, wgu_ref, wd_ref,
             act_ref, dx2b_ref, h2b_ref, dgu_ref, dx1_ref, dx1b_ref, acc_ref):
        @pl.when(pl.program_id(0) == 0)
        def _():
            acc_ref[...] = jnp.zeros_like(acc_ref)

        x1v = x1_ref[...]
        gf = gf_ref[...]
        gn = gn_ref[...]
        rr1 = lax.rsqrt(_mean(x1v * x1v) + EPS)
        x1n = x1v * rr1
        h2b = (x1n * gf).astype(BF16)
        h2b_ref[...] = h2b
        p = [_mm(h2b, wgu_ref[k]) for k in range(NCHIP)]
        sg = [_sigmoid(p[0]), _sigmoid(p[1])]
        si = [p[0] * sg[0], p[1] * sg[1]]
        x2 = x1v
        for k in range(2):
            actk = (si[k] * p[2 + k]).astype(BF16)
            act_ref[:, FFS * k:FFS * (k + 1)] = actk
            x2 = x2 + _mm(actk, wd_ref[FFS * k:FFS * (k + 1), :])
        rr2 = lax.rsqrt(_mean(x2 * x2) + EPS)
        x2n = x2 * rr2
        e = x2n * gn - tg_ref[...]
        acc_ref[0] += _rows8(e * e) * (0.5 * inv_d)
        dy = e * inv_d
        acc_ref[1] += _rows8(dy * x2n)
        dxn = dy * gn
        dx2 = rr2 * (dxn - x2n * _mean(dxn * x2n))
        dx2b = dx2.astype(BF16)
        dx2b_ref[...] = dx2b
        dh2 = None
        for k in range(2):
            dact = _mm_nt(dx2b, wd_ref[FFS * k:FFS * (k + 1), :])
            dgate = (dact * p[2 + k] * (sg[k] * (1.0 + p[k] * (1.0 - sg[k])))).astype(BF16)
            dup = (dact * si[k]).astype(BF16)
            dgu_ref[k] = dgate
            dgu_ref[2 + k] = dup
            part = _mm_nt(dgate, wgu_ref[k]) + _mm_nt(dup, wgu_ref[2 + k])
            dh2 = part if dh2 is None else dh2 + part
        acc_ref[2] += _rows8(dh2 * x1n)
        dxn1 = dh2 * gf
        dx1 = dx2 + rr1 * (dxn1 - x1n * _mean(dxn1 * x1n))
        dx1_ref[...] = dx1
        dx1b_ref[...] = dx1.astype(BF16)

    t = lambda i: (i, 0)
    w = lambda i: (0, 0)
    one = pl.Buffered(1)
    return pl.pallas_call(
        body, name="ffn_fwd_bwd", grid=(T // tm,),
        in_specs=[pl.BlockSpec((tm, D), t), pl.BlockSpec((tm, D), t),
                  pl.BlockSpec((1, D), w), pl.BlockSpec((1, D), w),
                  pl.BlockSpec((NCHIP, D, FFS), lambda i: (0, 0, 0), pipeline_mode=one),
                  pl.BlockSpec((FF, D), w, pipeline_mode=one)],
        out_specs=[pl.BlockSpec((tm, FF), t), pl.BlockSpec((tm, D), t), pl.BlockSpec((tm, D), t),
                   pl.BlockSpec((NCHIP, tm, FFS), lambda i: (0, i, 0)),
                   pl.BlockSpec((tm, D), t), pl.BlockSpec((tm, D), t),
                   pl.BlockSpec((3, 8, D), lambda i: (0, 0, 0))],
        out_shape=[SDS((T, FF), BF16), SDS((T, D), BF16), SDS((T, D), BF16),
                   SDS((NCHIP, T, FFS), BF16), SDS((T, D), F32), SDS((T, D), BF16),
                   SDS((3, 8, D), F32)],
        compiler_params=_cparams(),
    )(x1, target, g_ffn, g_fin, w_gu4, w_down)


def _merge_bwd(dx1b, y_a, y_b, proj, w_out, w_a, w_b):
    T = dx1b.shape[0]
    tm = min(512, T)

    def body(dx_ref, ya_ref, yb_ref, ga_ref, gb_ref, wo_ref, wa_ref, wb_ref,
             dya_ref, dyb_ref, da_ref, dob_ref, dp_ref):
        dm = _mm_nt(dx_ref[...], wo_ref[...])
        sa = _sigmoid(ga_ref[...])
        sb = _sigmoid(gb_ref[...])
        dya = (dm * sa).astype(BF16)
        dyb = (dm * sb).astype(BF16)
        dya_ref[...] = dya
        dyb_ref[...] = dyb
        dp_ref[0] = (dm * ya_ref[...] * sa * (1.0 - sa)).astype(BF16)
        dp_ref[1] = (dm * yb_ref[...] * sb * (1.0 - sb)).astype(BF16)
        da_ref[...] = _mm_nt(dya, wa_ref[...])
        dob_ref[...] = _mm_nt(dyb, wb_ref[...])

    t = lambda i: (i, 0)
    w = lambda i: (0, 0)
    return pl.pallas_call(
        body, name="merge_bwd", grid=(T // tm,),
        in_specs=[pl.BlockSpec((tm, D), t), pl.BlockSpec((tm, D), t), pl.BlockSpec((tm, D), t),
                  pl.BlockSpec((tm, D), lambda i: (i, 6)), pl.BlockSpec((tm, D), lambda i: (i, 7)),
                  pl.BlockSpec((D, D), w), pl.BlockSpec((D, D), w), pl.BlockSpec((D, D), w)],
        out_specs=[pl.BlockSpec((tm, D), t)] * 4 + [pl.BlockSpec((2, tm, D), lambda i: (3, i, 0))],
        out_shape=[SDS((T, D), BF16), SDS((T, D), BF16), SDS((T, D), F32), SDS((T, D), F32),
                   SDS((NIN, T, D), BF16)],
        compiler_params=_cparams(),
    )(dx1b, y_a, y_b, proj, proj, w_out, w_a, w_b)


def _hgrn_bwd(dproj, dob, o_raw, proj, st_before, lb_table, norm_g):
    T = dob.shape[0]
    tb = min(512, T)
    nc = tb // HCH
    nb = T // tb

    def body(dp_in, dob_ref, o_ref, q_ref, fl_ref, v_ref, g_ref, stb_ref, lbt_ref, gn_ref,
             dp_ref, acc_ref, dst_s, dqin_s, dqa_s, dkin_s, dkd_s, dv_s, ddec_s):
        del dp_in

        @pl.when(pl.program_id(1) == 0)
        def _():
            dst_s[...] = jnp.zeros_like(dst_s)
            acc_ref[...] = jnp.zeros_like(acc_ref)

        row = lax.broadcasted_iota(jnp.int32, (tb, HD), 0) & (HCH - 1)
        gn = gn_ref[...]
        lbv = _sigmoid(lbt_ref[0:1, :] - lbt_ref[1:2, :])
        o = o_ref[...]
        r = lax.rsqrt(_mean(o * o) + EPS)
        on = o * r
        g = g_ref[...]
        sgm = _sigmoid(g)
        dob_v = dob_ref[...]
        dp_ref[3] = (dob_v * on * gn * (sgm * (1.0 + g * (1.0 - sgm)))).astype(BF16)
        do_n = dob_v * (g * sgm)
        acc_ref[1] += _rows8(do_n * on)
        dxn = do_n * gn
        do = (r * (dxn - on * _mean(dxn * on))).astype(BF16)
        s, f, a, a_mid, a_last = _hgrn_gates(fl_ref[...], lbv, row)
        k = 1.0 - f
        qs = q_ref[...] * QSCALE
        e_q = jnp.exp(a - a_mid)
        e_k = jnp.exp(a_mid - a)
        e_a = jnp.exp(a)
        e_l = jnp.exp(a_last - a)
        dec = jnp.exp(a_last)
        q_in = qs * e_q
        k_in = k * e_k
        q_a = qs * e_a
        k_d = k * e_l
        q_inb, k_inb, q_ab, k_db = (z.astype(BF16) for z in (q_in, k_in, q_a, k_d))
        vb = v_ref[...].astype(BF16)
        tri = (lax.broadcasted_iota(jnp.int32, (HCH, HCH), 0)
               >= lax.broadcasted_iota(jnp.int32, (HCH, HCH), 1))
        for c in reversed(range(nc)):
            sl = slice(HCH * c, HCH * (c + 1))
            stp = stb_ref[c]
            dst = dst_s[...]
            dstb = dst.astype(BF16)
            do_c = do[sl]
            dqa_s[sl, :] = _mm(do_c, stp.astype(BF16))
            dkd_s[sl, :] = _mm(vb[sl], dstb)
            ddec_s[sl, :] = jnp.broadcast_to(jnp.sum(dst * stp, axis=0, keepdims=True), (HCH, HD))
            sc = jnp.where(tri, _mm_nt(q_inb[sl], k_inb[sl]), 0.0).astype(BF16)
            dsc = jnp.where(tri, _mm_nt(do_c, vb[sl]), 0.0).astype(BF16)
            dv_s[sl, :] = _mm_nt(k_db[sl], dstb) + _mm_tn(sc, do_c)
            dqin_s[sl, :] = _mm(dsc, k_inb[sl])
            dkin_s[sl, :] = _mm_tn(dsc, q_inb[sl])
            d64 = dec[sl]
            dst_s[...] = dst * jnp.concatenate([d64, d64], axis=0) + _mm_tn(do_c, q_ab[sl])
        dq_in = dqin_s[...]
        dq_a = dqa_s[...]
        dk_in = dkin_s[...]
        dk_d = dkd_s[...]
        dp_ref[0] = ((dq_in * e_q + dq_a * e_a) * QSCALE).astype(BF16)
        dp_ref[2] = dv_s[...].astype(BF16)
        tq = dq_in * q_in
        tk = dk_in * k_in
        td = dk_d * k_d
        d_a = tq + dq_a * q_a - tk - td
        d_a = d_a + jnp.where(row == HCH // 2 - 1, _seg_sum(tk - tq), 0.0)
        d_a = d_a + jnp.where(row == HCH - 1, _seg_sum(td) + ddec_s[...] * dec, 0.0)
        dlf = _revcumsum64(d_a, row)
        df = dlf / f - (dk_in * e_k + dk_d * e_l)
        dp_ref[1] = (df * (1.0 - lbv) * s * (1.0 - s)).astype(BF16)
        acc_ref[0] += _rows8(df * (1.0 - s))

    def col(off):
        return pl.BlockSpec((tb, HD), lambda h, cb: (nb - 1 - cb, off * NH + h))

    hb = lambda h, cb: (nb - 1 - cb, h)
    return pl.pallas_call(
        body, name="hgrn_bwd", grid=(NH, nb),
        in_specs=[ANY, pl.BlockSpec((tb, HD), hb), pl.BlockSpec((tb, HD), hb),
                  col(2), col(3), col(4), col(5),
                  pl.BlockSpec((None, nc, HD, HD), lambda h, cb: (h, nb - 1 - cb, 0, 0)),
                  pl.BlockSpec((2, HD), lambda h, cb: (0, h)), pl.BlockSpec((1, HD), lambda h, cb: (0, h))],
        out_specs=[pl.BlockSpec((4, tb, HD), lambda h, cb: (0, nb - 1 - cb, h)),
                   pl.BlockSpec((2, 8, HD), lambda h, cb: (0, 0, h))],
        out_shape=[SDS(dproj.shape, BF16), SDS((2, 8, D), F32)],
        scratch_shapes=[pltpu.VMEM((HD, HD), F32)] + [pltpu.VMEM((tb, HD), F32)] * 6,
        input_output_aliases={0: 0},
        compiler_params=_cparams(),
    )(dproj, dob, o_raw, proj, proj, proj, proj, st_before, lb_table, norm_g)


def _gmlp_bwd(dproj, da, proj, ln_g, ln_b, wm, wm_t, b_t):
    T = da.shape[0]
    tm = min(256, T)

    def body(dp_in, da_ref, u_ref, v_ref, lg_ref, lb_ref, wm_ref, wmt_ref, bt_ref,
             dp_ref, acc_ref, dws_ref, dmix_ref, du_s, dvn_s):
        del dp_in

        @pl.when(pl.program_id(0) == 0)
        def _():
            acc_ref[...] = jnp.zeros_like(acc_ref)
            dws_ref[...] = jnp.zeros_like(dws_ref)
            dmix_ref[...] = jnp.zeros_like(dmix_ref)

        u = u_ref[...]
        v = v_ref[...]
        lg = lg_ref[...]
        gu, t_u = _gelu(u)
        gv, t_v = _gelu(v)
        vhat, rs = _layer_norm_stats(gv)
        vnb = (vhat * lg + lb_ref[...]).astype(BF16)
        da_v = da_ref[...]
        for ch in range(tm // GCH):
            rows = slice(GCH * ch, GCH * (ch + 1))
            for g in range(NG):
                cols = slice(128 * g, 128 * (g + 1))
                vng = vnb[rows, cols]
                mixed = _mm(wm_ref[g], vng) + bt_ref[:, g:g + 1]
                dag = da_v[rows, cols]
                dmx = dag * gu[rows, cols]
                du_s[rows, cols] = dag * mixed
                dmxb = dmx.astype(BF16)
                dws_ref[:, cols] += _mm_nt(dmxb, vng)
                dmix_ref[:, cols] += dmx
                dvn_s[rows, cols] = _mm(wmt_ref[g], dmxb)
        dp_ref[0] = (du_s[...] * _gelu_grad(u, t_u)).astype(BF16)
        dvn = dvn_s[...]
        acc_ref[0] += _rows8(dvn * vhat)
        acc_ref[1] += _rows8(dvn)
        dvh = dvn * lg
        dgv = rs * (dvh - _mean(dvh) - vhat * _mean(dvh * vhat))
        dp_ref[1] = (dgv * _gelu_grad(v, t_v)).astype(BF16)

    row = lambda i: (0, 0)
    w3 = lambda i: (0, 0, 0)
    return pl.pallas_call(
        body, name="gmlp_bwd", grid=(T // tm,),
        in_specs=[ANY, pl.BlockSpec((tm, D), lambda i: (i, 0)),
                  pl.BlockSpec((tm, D), lambda i: (i, 0)), pl.BlockSpec((tm, D), lambda i: (i, 1)),
                  pl.BlockSpec((1, D), row), pl.BlockSpec((1, D), row),
                  pl.BlockSpec((NG, GCH, GCH), w3), pl.BlockSpec((NG, GCH, GCH), w3),
                  pl.BlockSpec((GCH, NG), row)],
        out_specs=[pl.BlockSpec((2, tm, D), lambda i: (2, i, 0)),
                   pl.BlockSpec((2, 8, D), w3), pl.BlockSpec((GCH, D), row), pl.BlockSpec((GCH, D), row)],
        out_shape=[SDS(dproj.shape, BF16), SDS((2, 8, D), F32), SDS((GCH, D), F32), SDS((GCH, D), F32)],
        scratch_shapes=[pltpu.VMEM((tm, D), F32), pltpu.VMEM((tm, D), F32)],
        input_output_aliases={0: 0},
        compiler_params=_cparams(),
    )(dproj, da, proj, proj, ln_g, ln_b, wm, wm_t, b_t)


def _proj_bwd(dproj, w_in4, x, dx1, g_mix):
    T = x.shape[0]
    tm = min(256, T)
    order = (2, 3, 4, 5, 0, 1, 6, 7)

    def body(dp_ref, w_ref, x_ref, dx1_ref, g_ref, gx_ref, acc_ref):
        @pl.when(pl.program_id(0) == 0)
        def _():
            acc_ref[...] = jnp.zeros_like(acc_ref)

        dh = None
        for m, og in enumerate(order):
            part = _mm_nt(dp_ref[m], w_ref[og // 2, :, D * (og % 2):D * (og % 2 + 1)])
            dh = part if dh is None else dh + part
        xv = x_ref[...]
        r = lax.rsqrt(_mean(xv * xv) + EPS)
        xn = xv * r
        acc_ref[...] += _rows8(dh * xn)
        dxn = dh * g_ref[...]
        gx_ref[...] = dx1_ref[...] + r * (dxn - xn * _mean(dxn * xn))

    t = lambda i: (i, 0)
    return pl.pallas_call(
        body, name="proj_bwd", grid=(T // tm,),
        in_specs=[pl.BlockSpec((NIN, tm, D), lambda i: (0, i, 0)),
                  pl.BlockSpec((NCHIP, D, 2 * D), lambda i: (0, 0, 0), pipeline_mode=pl.Buffered(1)),
                  pl.BlockSpec((tm, D), t), pl.BlockSpec((tm, D), t), pl.BlockSpec((1, D), lambda i: (0, 0))],
        out_specs=[pl.BlockSpec((tm, D), t), pl.BlockSpec((8, D), lambda i: (0, 0))],
        out_shape=[SDS((T, D), F32), SDS((8, D), F32)],
        compiler_params=_cparams(),
    )(dproj, w_in4, x, dx1, g_mix)


def _dw_call(name, a, b, a_spec, b_spec, o_spec, out_shape, nblk, tt):
    T = a.shape[-2]

    def body(a_ref, b_ref, o_ref):
        @pl.when(pl.program_id(1) == 0)
        def _():
            o_ref[...] = jnp.zeros_like(o_ref)
        o_ref[...] += _mm_tn(a_ref[...], b_ref[...])

    return pl.pallas_call(
        body, name=name, grid=(nblk, T // tt),
        in_specs=[a_spec, b_spec], out_specs=o_spec, out_shape=out_shape,
        compiler_params=_cparams(),
    )(a, b)


def _weight_grads(hb, dproj, ab, dya, obb, dyb, mgb, dx1b, h2b, dgu4, act, dx2b):
    T = hb.shape[0]
    tt = min(512, T)
    g_in = _dw_call(
        "dw_in", hb, dproj,
        pl.BlockSpec((tt, D), lambda m, t: (t, 0)),
        pl.BlockSpec((None, tt, D), lambda m, t: (m, t, 0)),
        pl.BlockSpec((None, D, D), lambda m, t: (_orig_group(m) // 2, 0, _orig_group(m) % 2)),
        SDS((NCHIP, D, 2 * D), F32), NIN, tt)
    g_gu = _dw_call(
        "dw_gate_up", h2b, dgu4,
        pl.BlockSpec((tt, D), lambda k, t: (t, 0)),
        pl.BlockSpec((None, tt, FFS), lambda k, t: (k, t, 0)),
        pl.BlockSpec((None, D, FFS), lambda k, t: (k, 0, 0)),
        SDS((NCHIP, D, FFS), F32), NCHIP, tt)
    g_down = _dw_call(
        "dw_down", act, dx2b,
        pl.BlockSpec((tt, FFS), lambda k, t: (t, k)),
        pl.BlockSpec((tt, D), lambda k, t: (t, 0)),
        pl.BlockSpec((FFS, D), lambda k, t: (k, 0)),
        SDS((FF, D), F32), 2, tt)

    def square(name, a, b):
        return _dw_call(
            name, a, b,
            pl.BlockSpec((tt, D), lambda k, t: (t, 0)), pl.BlockSpec((tt, D), lambda k, t: (t, 0)),
            pl.BlockSpec((D, D), lambda k, t: (0, 0)), SDS((D, D), F32), 1, tt)

    g_a = square("dw_branch_a", ab, dya)
    g_b = square("dw_branch_b", obb, dyb)
    g_out = square("dw_out", mgb, dx1b)
    return (g_in, g_gu, g_a.reshape(NCHIP, D // NCHIP, D), g_b.reshape(NCHIP, D // NCHIP, D),
            g_out.reshape(NCHIP, D // NCHIP, D), g_down.reshape(NCHIP, FF // NCHIP, D))


def _place():
    x, y, c = lax.axis_index("x"), lax.axis_index("y"), lax.axis_index("c")
    return x, y, c, 2 * x + y


def _chip_at(x, y, s):
    return x ^ (s >> 1), y ^ (s & 1)


def _cast_shard(name, place, w):
    rows, cols = w.shape
    tr = 352 if rows % 352 == 0 else 256

    def body(pc_ref, w_ref, o_ref):
        del pc_ref
        o_ref[...] = w_ref[...].astype(BF16)

    return pl.pallas_call(
        body, name=name,
        grid_spec=pltpu.PrefetchScalarGridSpec(
            num_scalar_prefetch=1, grid=(rows // tr,),
            in_specs=[pl.BlockSpec((tr, cols), lambda i, pc: (i, 0))],
            out_specs=pl.BlockSpec((None, tr, cols), lambda i, pc: (pc[0], i, 0))),
        out_shape=SDS((NCHIP, rows, cols), BF16),
        compiler_params=_cparams(),
    )(place, w)


def _sibling_copy(ref, send_sem, recv_sem):
    x, y, c, _ = _place()
    return pltpu.make_async_remote_copy(src_ref=ref, dst_ref=ref, send_sem=send_sem, recv_sem=recv_sem,
                                        device_id=(x, y, 1 - c), device_id_type=MESH)


def _half_rows(arr, slot, core):
    half = arr.shape[1] // 2
    return arr.at[slot, pl.ds(pl.multiple_of(core * half, 16), half)]


def _gather_start(arrs, sems):
    send_sem, recv_sem = sems[0], sems[1]
    x, y, c, j = _place()
    for w, arr in enumerate(arrs):
        mine = _half_rows(arr, j, c)
        for s in range(1, NCHIP):
            cx, cy = _chip_at(x, y, s)
            pltpu.make_async_remote_copy(
                src_ref=mine, dst_ref=mine, send_sem=send_sem.at[w, s - 1], recv_sem=recv_sem.at[w, s - 1],
                device_id=(cx, cy, c), device_id_type=MESH).start()


def _gather_finish(arrs, sems):
    send_sem, recv_sem, fsend_sem, frecv_sem = sems
    _, _, c, j = _place()
    for s in range(1, NCHIP):
        for w, arr in enumerate(arrs):
            landed = _half_rows(arr, j ^ s, c)
            _sibling_copy(landed, send_sem.at[w, s - 1], recv_sem.at[w, s - 1]).wait_recv()
            _sibling_copy(landed, fsend_sem.at[w, s - 1], frecv_sem.at[w, s - 1]).start()
    for s in range(1, NCHIP):
        for w, arr in enumerate(arrs):
            theirs = _half_rows(arr, j ^ s, 1 - c)
            _sibling_copy(theirs, fsend_sem.at[w, s - 1], frecv_sem.at[w, s - 1]).wait_recv()
    for s in range(1, NCHIP):
        for w, arr in enumerate(arrs):
            _sibling_copy(_half_rows(arr, j, c), send_sem.at[w, s - 1], recv_sem.at[w, s - 1]).wait_send()
            _sibling_copy(_half_rows(arr, j ^ s, c), fsend_sem.at[w, s - 1], frecv_sem.at[w, s - 1]).wait_send()


def _gather_weights(arrs):
    n = len(arrs)

    def body(*refs):
        outs = refs[n:2 * n]
        sems = refs[2 * n:]
        _gather_start(outs, sems)
        _gather_finish(outs, sems)

    return pl.pallas_call(
        body, name="gather_weights",
        in_specs=[ANY] * n, out_specs=[ANY] * n,
        out_shape=[SDS(a.shape, a.dtype) for a in arrs],
        scratch_shapes=[pltpu.SemaphoreType.DMA((n, NCHIP - 1))] * 4,
        input_output_aliases={k: k for k in range(n)},
        compiler_params=pltpu.CompilerParams(has_side_effects=True),
    )(*arrs)


def _pair_exchange(grads):
    n = len(grads)

    def body(*refs):
        src = refs[:n]
        out = refs[n:2 * n]
        send_sem, recv_sem = refs[2 * n:]
        x, y, c, _ = _place()
        cps = []
        for w in range(n):
            half = src[w].shape[1] // 2
            theirs = pl.ds(pl.multiple_of((1 - c) * half, 8), half)
            cp = pltpu.make_async_remote_copy(
                src_ref=src[w].at[:, theirs, :], dst_ref=out[w], send_sem=send_sem.at[w],
                recv_sem=recv_sem.at[w], device_id=(x, y, 1 - c), device_id_type=MESH)
            cp.start()
            cps.append(cp)
        for cp in cps:
            cp.wait()

    return pl.pallas_call(
        body, name="rs_pair_exchange",
        in_specs=[ANY] * n, out_specs=[ANY] * n,
        out_shape=[SDS((NCHIP, g.shape[1] // 2, g.shape[2]), F32) for g in grads],
        scratch_shapes=[pltpu.SemaphoreType.DMA((n,))] * 2,
        compiler_params=pltpu.CompilerParams(has_side_effects=True),
    )(*grads)


def _row_tile(rows):
    return 176 if rows % 176 == 0 and rows % 128 else 128


def _pair_sum(name, place, g, sib):
    half, cols = sib.shape[1], sib.shape[2]
    tr = _row_tile(half)
    nt = half // tr

    def body(pc_ref, g_ref, s_ref, own_ref, out_ref):
        del pc_ref
        v = g_ref[...] + s_ref[...]
        out_ref[...] = v.astype(BF16)

        @pl.when(pl.program_id(1) == 0)
        def _():
            own_ref[...] = v

    return pl.pallas_call(
        body, name=name,
        grid_spec=pltpu.PrefetchScalarGridSpec(
            num_scalar_prefetch=1, grid=(nt, NCHIP),
            in_specs=[pl.BlockSpec((None, tr, cols), lambda i, s, pc: (pc[0] ^ s, pc[1] * nt + i, 0)),
                      pl.BlockSpec((None, tr, cols), lambda i, s, pc: (pc[0] ^ s, i, 0))],
            out_specs=[pl.BlockSpec((tr, cols), lambda i, s, pc: (i, 0)),
                       pl.BlockSpec((None, tr, cols), lambda i, s, pc: (s, i, 0))]),
        out_shape=[SDS((half, cols), F32), SDS((NCHIP, half, cols), BF16)],
        compiler_params=_cparams(),
    )(place, g, sib)


def _chip_exchange(parts):
    n = len(parts)

    def body(*refs):
        src = refs[:n]
        out = refs[n:2 * n]
        send_sem, recv_sem = refs[2 * n:]
        x, y, c, _ = _place()
        cps = []
        for w in range(n):
            for s in range(1, NCHIP):
                cx, cy = _chip_at(x, y, s)
                cp = pltpu.make_async_remote_copy(
                    src_ref=src[w].at[s], dst_ref=out[w].at[s - 1], send_sem=send_sem.at[w, s - 1],
                    recv_sem=recv_sem.at[w, s - 1], device_id=(cx, cy, c), device_id_type=MESH)
                cp.start()
                cps.append(cp)
        for cp in cps:
            cp.wait()

    return pl.pallas_call(
        body, name="rs_chip_exchange",
        in_specs=[ANY] * n, out_specs=[ANY] * n,
        out_shape=[SDS((NCHIP - 1,) + p.shape[1:], BF16) for p in parts],
        scratch_shapes=[pltpu.SemaphoreType.DMA((n, NCHIP - 1))] * 2,
        compiler_params=pltpu.CompilerParams(has_side_effects=True),
    )(*parts)


def _chip_sum(name, own, rem):
    half, cols = own.shape
    tr = _row_tile(half)

    def body(own_ref, rem_ref, out_ref):
        out_ref[...] = ((own_ref[...] + rem_ref[0].astype(F32)) + rem_ref[1].astype(F32)) + rem_ref[2].astype(F32)

    return pl.pallas_call(
        body, name=name, grid=(half // tr,),
        in_specs=[pl.BlockSpec((tr, cols), lambda i: (i, 0)),
                  pl.BlockSpec((NCHIP - 1, tr, cols), lambda i: (0, i, 0))],
        out_specs=pl.BlockSpec((tr, cols), lambda i: (i, 0)),
        out_shape=SDS((half, cols), F32),
        compiler_params=_cparams(),
    )(own, rem)


def _share_halves(halves):
    n = len(halves)

    def body(*refs):
        src = refs[:n]
        out = refs[n:2 * n]
        send_sem, recv_sem = refs[2 * n:]
        x, y, c, _ = _place()
        cps = []
        for w in range(n):
            cp = pltpu.make_async_remote_copy(
                src_ref=src[w], dst_ref=out[w], send_sem=send_sem.at[w], recv_sem=recv_sem.at[w],
                device_id=(x, y, 1 - c), device_id_type=MESH)
            cp.start()
            cps.append(cp)
        for cp in cps:
            cp.wait()

    return pl.pallas_call(
        body, name="rs_share_halves",
        in_specs=[ANY] * n, out_specs=[ANY] * n,
        out_shape=[SDS(h.shape, F32) for h in halves],
        scratch_shapes=[pltpu.SemaphoreType.DMA((n,))] * 2,
        compiler_params=pltpu.CompilerParams(has_side_effects=True),
    )(*halves)


def _adamw_math(w, g, m, v):
    m = B1 * m + (1.0 - B1) * g
    v = B2 * v + (1.0 - B2) * (g * g)
    m_hat = m / (1.0 - B1 ** STEP)
    v_hat = v / (1.0 - B2 ** STEP)
    delta = -LR * (m_hat / (jnp.sqrt(v_hat) + AEPS) + WD * w)
    return delta, m, v


def _adamw(name, place, w, own, sib, m, v):
    rows, cols = w.shape
    half = rows // 2
    tr = 352 if half % 352 == 0 else min(256, half)
    nt = half // tr

    def body(pc_ref, w_ref, own_ref, sib_ref, m_ref, v_ref, g_ref, d_ref, mo_ref, vo_ref):
        g = jnp.where(pl.program_id(0) == pc_ref[1], own_ref[...], sib_ref[...])
        d, mn, vn = _adamw_math(w_ref[...], g, m_ref[...], v_ref[...])
        g_ref[...] = g
        d_ref[...] = d
        mo_ref[...] = mn
        vo_ref[...] = vn

    full = pl.BlockSpec((tr, cols), lambda h, i, pc: (h * nt + i, 0))
    part = pl.BlockSpec((tr, cols), lambda h, i, pc: (i, 0))
    return pl.pallas_call(
        body, name=name,
        grid_spec=pltpu.PrefetchScalarGridSpec(
            num_scalar_prefetch=1, grid=(2, nt),
            in_specs=[full, part, part, full, full], out_specs=[full] * 4),
        out_shape=[SDS((rows, cols), F32)] * 4,
        compiler_params=_cparams(),
    )(place, w, own, sib, m, v)


def _small_allreduce_adamw(sp, w, m, v):
    shape = sp.shape

    def body(sp_ref, w_ref, m_ref, v_ref, g_ref, d_ref, mo_ref, vo_ref,
             sib_s, pair_s, chip_s, send_sem, recv_sem):
        x, y, c, j = _place()
        cp = pltpu.make_async_remote_copy(
            src_ref=sp_ref, dst_ref=sib_s, send_sem=send_sem.at[0], recv_sem=recv_sem.at[0],
            device_id=(x, y, 1 - c), device_id_type=MESH)
        cp.start()
        cp.wait()
        pair_s[...] = sp_ref[...] + sib_s[...]
        cps = []
        for s in range(1, NCHIP):
            cx, cy = _chip_at(x, y, s)
            cp = pltpu.make_async_remote_copy(
                src_ref=pair_s, dst_ref=chip_s.at[s], send_sem=send_sem.at[s], recv_sem=recv_sem.at[s],
                device_id=(cx, cy, c), device_id_type=MESH)
            cp.start()
            cps.append(cp)
        chip_s[0] = pair_s[...]
        for cp in cps:
            cp.wait()
        tot = chip_s[j]
        for k in range(1, NCHIP):
            tot = tot + chip_s[k ^ j]
        g_ref[...] = tot
        d, mn, vn = _adamw_math(w_ref[...], tot, m_ref[...], v_ref[...])
        d_ref[...] = d
        mo_ref[...] = mn
        vo_ref[...] = vn

    vm = pl.BlockSpec(memory_space=pltpu.VMEM)
    return pl.pallas_call(
        body, name="small_allreduce_adamw",
        in_specs=[vm] * 4, out_specs=[vm] * 4, out_shape=[SDS(shape, F32)] * 4,
        scratch_shapes=[pltpu.VMEM(shape, F32), pltpu.VMEM(shape, F32), pltpu.VMEM((NCHIP,) + shape, F32),
                        pltpu.SemaphoreType.DMA((NCHIP,)), pltpu.SemaphoreType.DMA((NCHIP,))],
        compiler_params=pltpu.CompilerParams(has_side_effects=True),
    )(sp, w, m, v)


def _pack_small(first, mix, ln_g, ln_b, b_s, lbt, hn, ffn, fin, w_s):
    rows = [first.reshape(1, D), mix.reshape(1, D), ln_g.reshape(1, D), ln_b.reshape(1, D),
            b_s.reshape(1, D), lbt.reshape(2, D), hn.reshape(1, D), ffn.reshape(1, D), fin.reshape(1, D),
            jnp.zeros((6, D), F32)]
    return jnp.concatenate(rows + [w_s.reshape(NG, GCH, GCH).transpose(1, 0, 2).reshape(GCH, D)], axis=0)


def _unpack_small(p):
    w_s = p[16:].reshape(GCH, NG, GCH).transpose(1, 0, 2).reshape(1, NG, GCH, GCH)
    return dict(norm_mix_g=p[1:2], gmlp_ln_g=p[2:3], gmlp_ln_b=p[3:4], gmlp_b_s=p[4].reshape(1, NG, GCH),
                hgrn_lb_table=p[5:7], hgrn_norm_g=p[7:8], norm_ffn_g=p[8:9], norm_final_g=p[9],
                gmlp_w_s=w_s)


SMALL = ("norm_mix_g", "gmlp_ln_g", "gmlp_ln_b", "gmlp_w_s", "gmlp_b_s", "hgrn_lb_table", "hgrn_norm_g",
         "norm_ffn_g", "norm_final_g")
BIG = ("w_in", "w_gate_up", "w_branch_a", "w_branch_b", "w_out", "w_down")
ORDER = ("norm_mix_g", "w_in", "gmlp_ln_g", "gmlp_ln_b", "gmlp_w_s", "gmlp_b_s", "hgrn_lb_table",
         "hgrn_norm_g", "w_branch_a", "w_branch_b", "w_out", "norm_ffn_g", "w_gate_up", "w_down",
         "norm_final_g")


def kernel(x, norm_mix_g, w_in, gmlp_ln_g, gmlp_ln_b, gmlp_w_s, gmlp_b_s, hgrn_lb_table, hgrn_norm_g, w_branch_a, w_branch_b, w_out, norm_ffn_g, w_gate_up, w_down, norm_final_g, loss_target, m_norm_mix_g, m_w_in, m_gmlp_ln_g, m_gmlp_ln_b, m_gmlp_w_s, m_gmlp_b_s, m_hgrn_lb_table, m_hgrn_norm_g, m_w_branch_a, m_w_branch_b, m_w_out, m_norm_ffn_g, m_w_gate_up, m_w_down, m_norm_final_g, v_norm_mix_g, v_w_in, v_gmlp_ln_g, v_gmlp_ln_b, v_gmlp_w_s, v_gmlp_b_s, v_hgrn_lb_table, v_hgrn_norm_g, v_w_branch_a, v_w_branch_b, v_w_out, v_norm_ffn_g, v_w_gate_up, v_w_down, v_norm_final_g):
    args = dict(locals())
    T = x.shape[1]
    xs = x.reshape(T, D)
    target = loss_target.reshape(T, D)
    big = {n: args[n].reshape(args[n].shape[1:]) for n in BIG}
    big_m = {n: args["m_" + n].reshape(args[n].shape[1:]) for n in BIG}
    big_v = {n: args["v_" + n].reshape(args[n].shape[1:]) for n in BIG}

    x_i, y_i, c_i = lax.axis_index("x"), lax.axis_index("y"), lax.axis_index("c")
    place = jnp.stack([2 * x_i + y_i, c_i]).astype(jnp.int32)
    cast = {n: _cast_shard("cast_" + n, place, big[n]) for n in BIG}
    (w_in4,) = _gather_weights([cast["w_in"]])
    tril = jnp.tril(jnp.ones((GCH, GCH), bool))
    wm = jnp.where(tril, gmlp_w_s[0], 0.0).astype(BF16)
    wm_t = jnp.swapaxes(wm, 1, 2)
    b_t = gmlp_b_s[0].T

    proj, hb, w_gu4, w_a4, w_b4, w_out4, w_down4 = _proj_fwd(
        xs, norm_mix_g, w_in4, [cast[n] for n in BIG[1:]])
    w_a, w_b, w_o = (w.reshape(D, D) for w in (w_a4, w_b4, w_out4))
    w_dn = w_down4.reshape(FF, D)
    ab, y_a = _gmlp_fwd(proj, gmlp_ln_g, gmlp_ln_b, wm, b_t, w_a)
    o_raw, obb, st_before = _hgrn_fwd(proj, hgrn_lb_table, hgrn_norm_g)
    y_b, mgb, x1 = _merge_fwd(xs, y_a, obb, proj, w_b, w_o)
    act, dx2b, h2b, dgu4, dx1, dx1b, acc_ffn = _ffn_fwd_bwd(
        x1, target, norm_ffn_g, norm_final_g.reshape(1, D), w_gu4, w_dn)

    dya, dyb, da, dob, dproj = _merge_bwd(dx1b, y_a, y_b, proj, w_o, w_a, w_b)
    dproj, acc_hgrn = _hgrn_bwd(dproj, dob, o_raw, proj, st_before, hgrn_lb_table, hgrn_norm_g)
    dproj, acc_ln, dws, dmix = _gmlp_bwd(dproj, da, proj, gmlp_ln_g, gmlp_ln_b, wm, wm_t, b_t)
    grad_x, acc_mix = _proj_bwd(dproj, w_in4, xs, dx1, norm_mix_g)
    grads = _weight_grads(hb, dproj, ab, dya, obb, dyb, mgb, dx1b, h2b, dgu4, act, dx2b)

    sib = _pair_exchange(list(grads))
    owns, parts = zip(*[_pair_sum("rs_pair_sum_" + n, place, g, s) for n, g, s in zip(BIG, grads, sib)])
    rems = _chip_exchange(list(parts))
    halves = [_chip_sum("rs_chip_sum_" + n, o, r) for n, o, r in zip(BIG, owns, rems)]
    sib_halves = _share_halves(halves)
    out = {}
    for n, own, sibh in zip(BIG, halves, sib_halves):
        g, d, mn, vn = _adamw("adamw_" + n, place, big[n], own, sibh, big_m[n], big_v[n])
        shp = args[n].shape
        out[n] = (g.reshape(shp), d.reshape(shp), mn.reshape(shp), vn.reshape(shp))

    lbv = jax.nn.sigmoid(hgrn_lb_table[0] - hgrn_lb_table[1])
    d_t0 = jnp.sum(acc_hgrn[0], axis=0) * lbv * (1.0 - lbv)
    loss_row = jnp.zeros((D,), F32).at[0].set(jnp.sum(acc_ffn[0]))
    dws_m = jnp.where(tril[:, None, :], dws.reshape(GCH, NG, GCH), 0.0).transpose(1, 0, 2)
    db_s = jnp.sum(dmix.reshape(GCH, NG, GCH), axis=-1).T
    sp = _pack_small(loss_row, jnp.sum(acc_mix, 0), jnp.sum(acc_ln[0], 0), jnp.sum(acc_ln[1], 0), db_s,
                     jnp.stack([d_t0, -d_t0]), jnp.sum(acc_hgrn[1], 0), jnp.sum(acc_ffn[2], 0),
                     jnp.sum(acc_ffn[1], 0), dws_m)
    zero = jnp.zeros((D,), F32)

    def pack(prefix):
        a = lambda n: args[prefix + n]
        return _pack_small(zero, a("norm_mix_g"), a("gmlp_ln_g"), a("gmlp_ln_b"), a("gmlp_b_s"),
                           a("hgrn_lb_table"), a("hgrn_norm_g"), a("norm_ffn_g"), a("norm_final_g"),
                           a("gmlp_w_s"))

    packed = _small_allreduce_adamw(sp, pack(""), pack("m_"), pack("v_"))
    loss = packed[0][0, 0]
    small = [_unpack_small(p) for p in packed]
    for n in SMALL:
        out[n] = tuple(s[n] for s in small)
    return (loss, grad_x.reshape(x.shape), *[out[n][0] for n in ORDER], *[out[n][1] for n in ORDER],
            *[out[n][2] for n in ORDER], *[out[n][3] for n in ORDER])
```

```python
import functools
import math

import jax
import jax.numpy as jnp
from jax import lax
from jax.experimental import pallas as pl
from jax.experimental.pallas import tpu as pltpu

F32 = jnp.float32
BF16 = jnp.bfloat16
SDS = jax.ShapeDtypeStruct
MESH = pl.DeviceIdType.MESH
ANY = pl.BlockSpec(memory_space=pl.ANY)

D = 1024
NIN = 8
NG = 8
GCH = 128
NH = 8
HD = 128
HCH = 64
FF = 2816
FFS = 1408
NCHIP = 4
EPS = 1e-6
QSCALE = HD ** -0.5
GELU_C0 = math.sqrt(2.0 / math.pi)
GELU_C1 = 0.044715
LR, B1, B2, AEPS, WD, STEP = 0.001, 0.9, 0.999, 1e-08, 0.01, 10
VMEM_LIMIT_V7X = 56 * 1024 * 1024
SP_ROWS = 144


def _cparams(**kw):
    return pltpu.CompilerParams(vmem_limit_bytes=VMEM_LIMIT_V7X, **kw)


def _mm(a, b):
    return jnp.dot(a, b, preferred_element_type=F32)


def _mm_nt(a, b):
    return lax.dot_general(a, b, (((1,), (1,)), ((), ())), preferred_element_type=F32)


def _mm_tn(a, b):
    return lax.dot_general(a, b, (((0,), (0,)), ((), ())), preferred_element_type=F32)


def _rows8(x):
    r, c = x.shape
    return jnp.sum(x.reshape(r // 8, 8, c), axis=0)


def _mean(x):
    return jnp.mean(x, axis=-1, keepdims=True)


def _sigmoid(x):
    return 1.0 / (1.0 + jnp.exp(-x))


def _gelu(x):
    t = jnp.tanh(GELU_C0 * (x + GELU_C1 * x * x * x))
    return 0.5 * x * (1.0 + t), t


def _gelu_grad(x, t):
    return 0.5 * (1.0 + t) + 0.5 * x * (1.0 - t * t) * (GELU_C0 * (1.0 + 3.0 * GELU_C1 * x * x))


def _orig_group(m):
    return jnp.where(m < 6, (m + 2) % 6, m)


def _proj_fwd(x, g_mix, w_in4, job=None):
    T = x.shape[0]
    tm = min(512, T)

    def body(x_ref, g_ref, w_ref, proj_ref, h_ref, hs):
        @pl.when(pl.program_id(1) == 0)
        def _():
            xv = x_ref[...]
            r = lax.rsqrt(_mean(xv * xv) + EPS)
            hb = (xv * r * g_ref[...]).astype(BF16)
            hs[...] = hb
            h_ref[...] = hb
        proj_ref[...] = _mm(hs[...], w_ref[...])

    return _call(
        body, name="proj_fwd", grid=(T // tm, NIN),
        in_specs=[pl.BlockSpec((tm, D), lambda i, j: (i, 0)),
                  pl.BlockSpec((1, D), lambda i, j: (0, 0)),
                  pl.BlockSpec((None, D, D), lambda i, j: (j // 2, 0, j % 2))],
        out_specs=[pl.BlockSpec((tm, D), lambda i, j: (i, j)),
                   pl.BlockSpec((tm, D), lambda i, j: (i, 0))],
        out_shape=[SDS((T, NIN * D), F32), SDS((T, D), BF16)],
        scratch_shapes=[pltpu.VMEM((tm, D), BF16)],
        args=(x, g_mix, w_in4), job=job)


def _layer_norm_stats(gv):
    mu = _mean(gv)
    xc = gv - mu
    rs = lax.rsqrt(_mean(xc * xc) + EPS)
    return xc * rs, rs


def _gmlp_fwd(proj, ln_g, ln_b, wm, b_t, w_a):
    T = proj.shape[0]
    tm = min(256, T)

    def body(u_ref, v_ref, lg_ref, lb_ref, wm_ref, bt_ref, wa_ref, a_ref, ya_ref, a_s):
        gu, _ = _gelu(u_ref[...])
        gv, _ = _gelu(v_ref[...])
        vhat, _ = _layer_norm_stats(gv)
        vnb = (vhat * lg_ref[...] + lb_ref[...]).astype(BF16)
        for ch in range(tm // GCH):
            rows = slice(GCH * ch, GCH * (ch + 1))
            for g in range(NG):
                cols = slice(128 * g, 128 * (g + 1))
                mixed = _mm(wm_ref[g], vnb[rows, cols]) + bt_ref[:, g:g + 1]
                a_s[rows, cols] = gu[rows, cols] * mixed
        ab = a_s[...].astype(BF16)
        a_ref[...] = ab
        ya_ref[...] = _mm(ab, wa_ref[...])

    row = lambda i: (0, 0)
    return pl.pallas_call(
        body, name="gmlp_fwd", grid=(T // tm,),
        in_specs=[pl.BlockSpec((tm, D), lambda i: (i, 0)), pl.BlockSpec((tm, D), lambda i: (i, 1)),
                  pl.BlockSpec((1, D), row), pl.BlockSpec((1, D), row),
                  pl.BlockSpec((NG, GCH, GCH), lambda i: (0, 0, 0)), pl.BlockSpec((GCH, NG), row),
                  pl.BlockSpec((D, D), row)],
        out_specs=[pl.BlockSpec((tm, D), lambda i: (i, 0)), pl.BlockSpec((tm, D), lambda i: (i, 0))],
        out_shape=[SDS((T, D), BF16), SDS((T, D), F32)],
        scratch_shapes=[pltpu.VMEM((tm, D), F32)],
        compiler_params=_cparams(),
    )(proj, proj, ln_g, ln_b, wm, b_t, w_a)


def _cumsum64(x, row):
    for s in (1, 2, 4, 8, 16, 32):
        x = x + jnp.where(row >= s, pltpu.roll(x, s, 0), 0.0)
    return x


def _revcumsum64(x, row):
    n = x.shape[0]
    for s in (1, 2, 4, 8, 16, 32):
        x = x + jnp.where(row < HCH - s, pltpu.roll(x, n - s, 0), 0.0)
    return x


def _seg_sum(x):
    n, c = x.shape
    s = jnp.sum(x.reshape(n // HCH, HCH, c), axis=1, keepdims=True)
    return jnp.broadcast_to(s, (n // HCH, HCH, c)).reshape(n, c)


def _hgrn_gates(fl, lbv, row):
    s = _sigmoid(fl)
    f = lbv + (1.0 - lbv) * s
    a = _cumsum64(jnp.log(f), row)
    a_mid = _seg_sum(jnp.where(row == HCH // 2 - 1, a, 0.0))
    a_last = _seg_sum(jnp.where(row == HCH - 1, a, 0.0))
    return s, f, a, a_mid, a_last


def _hgrn_fwd(proj, lb_table, norm_g):
    T = proj.shape[0]
    tb = min(512, T)
    nc = tb // HCH

    def body(q_ref, fl_ref, v_ref, g_ref, lbt_ref, gn_ref, o_ref, ob_ref, stb_ref, st_s, o_s):
        @pl.when(pl.program_id(1) == 0)
        def _():
            st_s[...] = jnp.zeros_like(st_s)

        row = lax.broadcasted_iota(jnp.int32, (tb, HD), 0) & (HCH - 1)
        lbv = _sigmoid(lbt_ref[0:1, :] - lbt_ref[1:2, :])
        _, f, a, a_mid, a_last = _hgrn_gates(fl_ref[...], lbv, row)
        k = 1.0 - f
        qs = q_ref[...] * QSCALE
        q_in = (qs * jnp.exp(a - a_mid)).astype(BF16)
        k_in = (k * jnp.exp(a_mid - a)).astype(BF16)
        q_a = (qs * jnp.exp(a)).astype(BF16)
        k_d = (k * jnp.exp(a_last - a)).astype(BF16)
        dec = jnp.exp(a_last)
        vb = v_ref[...].astype(BF16)
        tri = (lax.broadcasted_iota(jnp.int32, (HCH, HCH), 0)
               >= lax.broadcasted_iota(jnp.int32, (HCH, HCH), 1))
        for c in range(nc):
            sl = slice(HCH * c, HCH * (c + 1))
            st = st_s[...]
            stb_ref[c] = st
            sc = jnp.where(tri, _mm_nt(q_in[sl], k_in[sl]), 0.0)
            o_s[sl, :] = _mm(sc.astype(BF16), vb[sl]) + _mm_nt(q_a[sl], st.astype(BF16))
            d64 = dec[sl]
            st_s[...] = st * jnp.concatenate([d64, d64], axis=0) + _mm_tn(vb[sl], k_d[sl])
        o = o_s[...]
        r = lax.rsqrt(_mean(o * o) + EPS)
        g = g_ref[...]
        o_ref[...] = o
        ob_ref[...] = (o * r * gn_ref[...] * (g * _sigmoid(g))).astype(BF16)

    def col(off):
        return pl.BlockSpec((tb, HD), lambda h, cb: (cb, off * NH + h))

    return pl.pallas_call(
        body, name="hgrn_fwd", grid=(NH, T // tb),
        in_specs=[col(2), col(3), col(4), col(5),
                  pl.BlockSpec((2, HD), lambda h, cb: (0, h)), pl.BlockSpec((1, HD), lambda h, cb: (0, h))],
        out_specs=[pl.BlockSpec((tb, HD), lambda h, cb: (cb, h)), pl.BlockSpec((tb, HD), lambda h, cb: (cb, h)),
                   pl.BlockSpec((None, nc, HD, HD), lambda h, cb: (h, cb, 0, 0))],
        out_shape=[SDS((T, D), F32), SDS((T, D), BF16), SDS((NH, T // HCH, HD, HD), F32)],
        scratch_shapes=[pltpu.VMEM((HD, HD), F32), pltpu.VMEM((tb, HD), F32)],
        compiler_params=_cparams(),
    )(proj, proj, proj, proj, lb_table, norm_g)


def _merge_fwd(x, y_a, ob, proj, w_b, w_out):
    T = x.shape[0]
    tm = min(512, T)

    def body(x_ref, ya_ref, ob_ref, ga_ref, gb_ref, wb_ref, wo_ref, yb_ref, mg_ref, x1_ref):
        yb = _mm(ob_ref[...], wb_ref[...])
        merged = (_sigmoid(ga_ref[...]) * ya_ref[...] + _sigmoid(gb_ref[...]) * yb).astype(BF16)
        yb_ref[...] = yb
        mg_ref[...] = merged
        x1_ref[...] = x_ref[...] + _mm(merged, wo_ref[...])

    t = lambda i: (i, 0)
    w = lambda i: (0, 0)
    return pl.pallas_call(
        body, name="merge_fwd", grid=(T // tm,),
        in_specs=[pl.BlockSpec((tm, D), t), pl.BlockSpec((tm, D), t), pl.BlockSpec((tm, D), t),
                  pl.BlockSpec((tm, D), lambda i: (i, 6)), pl.BlockSpec((tm, D), lambda i: (i, 7)),
                  pl.BlockSpec((D, D), w), pl.BlockSpec((D, D), w)],
        out_specs=[pl.BlockSpec((tm, D), t)] * 3,
        out_shape=[SDS((T, D), F32), SDS((T, D), BF16), SDS((T, D), F32)],
        compiler_params=_cparams(),
    )(x, y_a, ob, proj, proj, w_b, w_out)


def _ffn_fwd_bwd(x1, target, g_ffn, g_fin, w_gu4, w_down):
    T = x1.shape[0]
    tm = min(256, T)
    inv_d = 1.0 / D

    def body(x1_ref, tg_ref, gf_ref, gn_ref, wgu_ref, wd_ref,
             act_ref, dx2b_ref, h2b_ref, dgu_ref, dx1_ref, dx1b_ref, acc_ref):
        @pl.when(pl.program_id(0) == 0)
        def _():
            acc_ref[...] = jnp.zeros_like(acc_ref)

        x1v = x1_ref[...]
        gf = gf_ref[...]
        gn = gn_ref[...]
        rr1 = lax.rsqrt(_mean(x1v * x1v) + EPS)
        x1n = x1v * rr1
        h2b = (x1n * gf).astype(BF16)
        h2b_ref[...] = h2b
        p = [_mm(h2b, wgu_ref[k]) for k in range(NCHIP)]
        sg = [_sigmoid(p[0]), _sigmoid(p[1])]
        si = [p[0] * sg[0], p[1] * sg[1]]
        x2 = x1v
        for k in range(2):
            actk = (si[k] * p[2 + k]).astype(BF16)
            act_ref[:, FFS * k:FFS * (k + 1)] = actk
            x2 = x2 + _mm(actk, wd_ref[FFS * k:FFS * (k + 1), :])
        rr2 = lax.rsqrt(_mean(x2 * x2) + EPS)
        x2n = x2 * rr2
        e = x2n * gn - tg_ref[...]
        acc_ref[0] += _rows8(e * e) * (0.5 * inv_d)
        dy = e * inv_d
        acc_ref[1] += _rows8(dy * x2n)
        dxn = dy * gn
        dx2 = rr2 * (dxn - x2n * _mean(dxn * x2n))
        dx2b = dx2.astype(BF16)
        dx2b_ref[...] = dx2b
        dh2 = None
        for k in range(2):
            dact = _mm_nt(dx2b, wd_ref[FFS * k:FFS * (k + 1), :])
            dgate = (dact * p[2 + k] * (sg[k] * (1.0 + p[k] * (1.0 - sg[k])))).astype(BF16)
            dup = (dact * si[k]).astype(BF16)
            dgu_ref[k] = dgate
            dgu_ref[2 + k] = dup
            part = _mm_nt(dgate, wgu_ref[k]) + _mm_nt(dup, wgu_ref[2 + k])
            dh2 = part if dh2 is None else dh2 + part
        acc_ref[2] += _rows8(dh2 * x1n)
        dxn1 = dh2 * gf
        dx1 = dx2 + rr1 * (dxn1 - x1n * _mean(dxn1 * x1n))
        dx1_ref[...] = dx1
        dx1b_ref[...] = dx1.astype(BF16)

    t = lambda i: (i, 0)
    w = lambda i: (0, 0)
    one = pl.Buffered(1)
    return pl.pallas_call(
        body, name="ffn_fwd_bwd", grid=(T // tm,),
        in_specs=[pl.BlockSpec((tm, D), t), pl.BlockSpec((tm, D), t),
                  pl.BlockSpec((1, D), w), pl.BlockSpec((1, D), w),
                  pl.BlockSpec((NCHIP, D, FFS), lambda i: (0, 0, 0), pipeline_mode=one),
                  pl.BlockSpec((FF, D), w, pipeline_mode=one)],
        out_specs=[pl.BlockSpec((tm, FF), t), pl.BlockSpec((tm, D), t), pl.BlockSpec((tm, D), t),
                   pl.BlockSpec((NCHIP, tm, FFS), lambda i: (0, i, 0)),
                   pl.BlockSpec((tm, D), t), pl.BlockSpec((tm, D), t),
                   pl.BlockSpec((3, 8, D), lambda i: (0, 0, 0))],
        out_shape=[SDS((T, FF), BF16), SDS((T, D), BF16), SDS((T, D), BF16),
                   SDS((NCHIP, T, FFS), BF16), SDS((T, D), F32), SDS((T, D), BF16),
                   SDS((3, 8, D), F32)],
        compiler_params=_cparams(),
    )(x1, target, g_ffn, g_fin, w_gu4, w_down)


def _merge_bwd(dx1b, y_a, y_b, proj, w_out, w_a, w_b, job=None):
    T = dx1b.shape[0]
    tm = min(512, T)

    def body(dx_ref, ya_ref, yb_ref, ga_ref, gb_ref, wo_ref, wa_ref, wb_ref,
             dya_ref, dyb_ref, da_ref, dob_ref, dp_ref):
        dm = _mm_nt(dx_ref[...], wo_ref[...])
        sa = _sigmoid(ga_ref[...])
        sb = _sigmoid(gb_ref[...])
        dya = (dm * sa).astype(BF16)
        dyb = (dm * sb).astype(BF16)
        dya_ref[...] = dya
        dyb_ref[...] = dyb
        dp_ref[0] = (dm * ya_ref[...] * sa * (1.0 - sa)).astype(BF16)
        dp_ref[1] = (dm * yb_ref[...] * sb * (1.0 - sb)).astype(BF16)
        da_ref[...] = _mm_nt(dya, wa_ref[...])
        dob_ref[...] = _mm_nt(dyb, wb_ref[...])

    t = lambda i: (i, 0)
    w = lambda i: (0, 0)
    return _call(
        body, name="merge_bwd", grid=(T // tm,),
        in_specs=[pl.BlockSpec((tm, D), t), pl.BlockSpec((tm, D), t), pl.BlockSpec((tm, D), t),
                  pl.BlockSpec((tm, D), lambda i: (i, 6)), pl.BlockSpec((tm, D), lambda i: (i, 7)),
                  pl.BlockSpec((D, D), w), pl.BlockSpec((D, D), w), pl.BlockSpec((D, D), w)],
        out_specs=[pl.BlockSpec((tm, D), t)] * 4 + [pl.BlockSpec((2, tm, D), lambda i: (3, i, 0))],
        out_shape=[SDS((T, D), BF16), SDS((T, D), BF16), SDS((T, D), F32), SDS((T, D), F32),
                   SDS((NIN, T, D), BF16)],
        args=(dx1b, y_a, y_b, proj, proj, w_out, w_a, w_b), job=job)


def _hgrn_bwd(dproj, dob, o_raw, proj, st_before, lb_table, norm_g, job=None):
    T = dob.shape[0]
    tb = min(512, T)
    nc = tb // HCH
    nb = T // tb

    def body(dp_in, dob_ref, o_ref, q_ref, fl_ref, v_ref, g_ref, stb_ref, lbt_ref, gn_ref,
             dp_ref, acc_ref, dst_s, dqin_s, dqa_s, dkin_s, dkd_s, dv_s, ddec_s):
        del dp_in

        @pl.when(pl.program_id(1) == 0)
        def _():
            dst_s[...] = jnp.zeros_like(dst_s)
            acc_ref[...] = jnp.zeros_like(acc_ref)

        row = lax.broadcasted_iota(jnp.int32, (tb, HD), 0) & (HCH - 1)
        gn = gn_ref[...]
        lbv = _sigmoid(lbt_ref[0:1, :] - lbt_ref[1:2, :])
        o = o_ref[...]
        r = lax.rsqrt(_mean(o * o) + EPS)
        on = o * r
        g = g_ref[...]
        sgm = _sigmoid(g)
        dob_v = dob_ref[...]
        dp_ref[3] = (dob_v * on * gn * (sgm * (1.0 + g * (1.0 - sgm)))).astype(BF16)
        do_n = dob_v * (g * sgm)
        acc_ref[1] += _rows8(do_n * on)
        dxn = do_n * gn
        do = (r * (dxn - on * _mean(dxn * on))).astype(BF16)
        s, f, a, a_mid, a_last = _hgrn_gates(fl_ref[...], lbv, row)
        k = 1.0 - f
        qs = q_ref[...] * QSCALE
        e_q = jnp.exp(a - a_mid)
        e_k = jnp.exp(a_mid - a)
        e_a = jnp.exp(a)
        e_l = jnp.exp(a_last - a)
        dec = jnp.exp(a_last)
        q_in = qs * e_q
        k_in = k * e_k
        q_a = qs * e_a
        k_d = k * e_l
        q_inb, k_inb, q_ab, k_db = (z.astype(BF16) for z in (q_in, k_in, q_a, k_d))
        vb = v_ref[...].astype(BF16)
        tri = (lax.broadcasted_iota(jnp.int32, (HCH, HCH), 0)
               >= lax.broadcasted_iota(jnp.int32, (HCH, HCH), 1))
        for c in reversed(range(nc)):
            sl = slice(HCH * c, HCH * (c + 1))
            stp = stb_ref[c]
            dst = dst_s[...]
            dstb = dst.astype(BF16)
            do_c = do[sl]
            dqa_s[sl, :] = _mm(do_c, stp.astype(BF16))
            dkd_s[sl, :] = _mm(vb[sl], dstb)
            ddec_s[sl, :] = jnp.broadcast_to(jnp.sum(dst * stp, axis=0, keepdims=True), (HCH, HD))
            sc = jnp.where(tri, _mm_nt(q_inb[sl], k_inb[sl]), 0.0).astype(BF16)
            dsc = jnp.where(tri, _mm_nt(do_c, vb[sl]), 0.0).astype(BF16)
            dv_s[sl, :] = _mm_nt(k_db[sl], dstb) + _mm_tn(sc, do_c)
            dqin_s[sl, :] = _mm(dsc, k_inb[sl])
            dkin_s[sl, :] = _mm_tn(dsc, q_inb[sl])
            d64 = dec[sl]
            dst_s[...] = dst * jnp.concatenate([d64, d64], axis=0) + _mm_tn(do_c, q_ab[sl])
        dq_in = dqin_s[...]
        dq_a = dqa_s[...]
        dk_in = dkin_s[...]
        dk_d = dkd_s[...]
        dp_ref[0] = ((dq_in * e_q + dq_a * e_a) * QSCALE).astype(BF16)
        dp_ref[2] = dv_s[...].astype(BF16)
        tq = dq_in * q_in
        tk = dk_in * k_in
        td = dk_d * k_d
        d_a = tq + dq_a * q_a - tk - td
        d_a = d_a + jnp.where(row == HCH // 2 - 1, _seg_sum(tk - tq), 0.0)
        d_a = d_a + jnp.where(row == HCH - 1, _seg_sum(td) + ddec_s[...] * dec, 0.0)
        dlf = _revcumsum64(d_a, row)
        df = dlf / f - (dk_in * e_k + dk_d * e_l)
        dp_ref[1] = (df * (1.0 - lbv) * s * (1.0 - s)).astype(BF16)
        acc_ref[0] += _rows8(df * (1.0 - s))

    def col(off):
        return pl.BlockSpec((tb, HD), lambda h, cb: (nb - 1 - cb, off * NH + h))

    hb = lambda h, cb: (nb - 1 - cb, h)
    return _call(
        body, name="hgrn_bwd", grid=(NH, nb), job=job,
        args=(dproj, dob, o_raw, proj, proj, proj, proj, st_before, lb_table, norm_g),
        in_specs=[ANY, pl.BlockSpec((tb, HD), hb), pl.BlockSpec((tb, HD), hb),
                  col(2), col(3), col(4), col(5),
                  pl.BlockSpec((None, nc, HD, HD), lambda h, cb: (h, nb - 1 - cb, 0, 0)),
                  pl.BlockSpec((2, HD), lambda h, cb: (0, h)), pl.BlockSpec((1, HD), lambda h, cb: (0, h))],
        out_specs=[pl.BlockSpec((4, tb, HD), lambda h, cb: (0, nb - 1 - cb, h)),
                   pl.BlockSpec((2, 8, HD), lambda h, cb: (0, 0, h))],
        out_shape=[SDS(dproj.shape, BF16), SDS((2, 8, D), F32)],
        scratch_shapes=[pltpu.VMEM((HD, HD), F32)] + [pltpu.VMEM((tb, HD), F32)] * 6,
        aliases={0: 0})


def _gmlp_bwd(dproj, da, proj, ln_g, ln_b, wm, wm_t, b_t):
    T = da.shape[0]
    tm = min(256, T)

    def body(dp_in, da_ref, u_ref, v_ref, lg_ref, lb_ref, wm_ref, wmt_ref, bt_ref,
             dp_ref, acc_ref, dws_ref, dmix_ref, du_s, dvn_s):
        del dp_in

        @pl.when(pl.program_id(0) == 0)
        def _():
            acc_ref[...] = jnp.zeros_like(acc_ref)
            dws_ref[...] = jnp.zeros_like(dws_ref)
            dmix_ref[...] = jnp.zeros_like(dmix_ref)

        u = u_ref[...]
        v = v_ref[...]
        lg = lg_ref[...]
        gu, t_u = _gelu(u)
        gv, t_v = _gelu(v)
        vhat, rs = _layer_norm_stats(gv)
        vnb = (vhat * lg + lb_ref[...]).astype(BF16)
        da_v = da_ref[...]
        for ch in range(tm // GCH):
            rows = slice(GCH * ch, GCH * (ch + 1))
            for g in range(NG):
                cols = slice(128 * g, 128 * (g + 1))
                vng = vnb[rows, cols]
                mixed = _mm(wm_ref[g], vng) + bt_ref[:, g:g + 1]
                dag = da_v[rows, cols]
                dmx = dag * gu[rows, cols]
                du_s[rows, cols] = dag * mixed
                dmxb = dmx.astype(BF16)
                dws_ref[:, cols] += _mm_nt(dmxb, vng)
                dmix_ref[:, cols] += dmx
                dvn_s[rows, cols] = _mm(wmt_ref[g], dmxb)
        dp_ref[0] = (du_s[...] * _gelu_grad(u, t_u)).astype(BF16)
        dvn = dvn_s[...]
        acc_ref[0] += _rows8(dvn * vhat)
        acc_ref[1] += _rows8(dvn)
        dvh = dvn * lg
        dgv = rs * (dvh - _mean(dvh) - vhat * _mean(dvh * vhat))
        dp_ref[1] = (dgv * _gelu_grad(v, t_v)).astype(BF16)

    row = lambda i: (0, 0)
    w3 = lambda i: (0, 0, 0)
    return pl.pallas_call(
        body, name="gmlp_bwd", grid=(T // tm,),
        in_specs=[ANY, pl.BlockSpec((tm, D), lambda i: (i, 0)),
                  pl.BlockSpec((tm, D), lambda i: (i, 0)), pl.BlockSpec((tm, D), lambda i: (i, 1)),
                  pl.BlockSpec((1, D), row), pl.BlockSpec((1, D), row),
                  pl.BlockSpec((NG, GCH, GCH), w3), pl.BlockSpec((NG, GCH, GCH), w3),
                  pl.BlockSpec((GCH, NG), row)],
        out_specs=[pl.BlockSpec((2, tm, D), lambda i: (2, i, 0)),
                   pl.BlockSpec((2, 8, D), w3), pl.BlockSpec((GCH, D), row), pl.BlockSpec((GCH, D), row)],
        out_shape=[SDS(dproj.shape, BF16), SDS((2, 8, D), F32), SDS((GCH, D), F32), SDS((GCH, D), F32)],
        scratch_shapes=[pltpu.VMEM((tm, D), F32), pltpu.VMEM((tm, D), F32)],
        input_output_aliases={0: 0},
        compiler_params=_cparams(),
    )(dproj, da, proj, proj, ln_g, ln_b, wm, wm_t, b_t)


def _proj_bwd(dproj, w_in4, x, dx1, g_mix, job=None):
    T = x.shape[0]
    tm = min(256, T)
    order = (2, 3, 4, 5, 0, 1, 6, 7)

    def body(dp_ref, w_ref, x_ref, dx1_ref, g_ref, gx_ref, acc_ref):
        @pl.when(pl.program_id(0) == 0)
        def _():
            acc_ref[...] = jnp.zeros_like(acc_ref)

        dh = None
        for m, og in enumerate(order):
            part = _mm_nt(dp_ref[m], w_ref[og // 2, :, D * (og % 2):D * (og % 2 + 1)])
            dh = part if dh is None else dh + part
        xv = x_ref[...]
        r = lax.rsqrt(_mean(xv * xv) + EPS)
        xn = xv * r
        acc_ref[...] += _rows8(dh * xn)
        dxn = dh * g_ref[...]
        gx_ref[...] = dx1_ref[...] + r * (dxn - xn * _mean(dxn * xn))

    t = lambda i: (i, 0)
    return _call(
        body, name="proj_bwd", grid=(T // tm,),
        in_specs=[pl.BlockSpec((NIN, tm, D), lambda i: (0, i, 0)),
                  pl.BlockSpec((NCHIP, D, 2 * D), lambda i: (0, 0, 0), pipeline_mode=pl.Buffered(1)),
                  pl.BlockSpec((tm, D), t), pl.BlockSpec((tm, D), t), pl.BlockSpec((1, D), lambda i: (0, 0))],
        out_specs=[pl.BlockSpec((tm, D), t), pl.BlockSpec((8, D), lambda i: (0, 0))],
        out_shape=[SDS((T, D), F32), SDS((8, D), F32)],
        args=(dproj, w_in4, x, dx1, g_mix), job=job)


def _dw_call(name, a, b, a_spec, b_spec, o_spec, out_shape, nblk, tt, job=None):
    T = a.shape[-2]

    def body(a_ref, b_ref, o_ref):
        @pl.when(pl.program_id(1) == 0)
        def _():
            o_ref[...] = jnp.zeros_like(o_ref)
        o_ref[...] += _mm_tn(a_ref[...], b_ref[...])

    (out,), job_out = _call(
        body, name=name, grid=(nblk, T // tt), in_specs=[a_spec, b_spec], out_specs=[o_spec],
        out_shape=[out_shape], args=(a, b), job=job)
    return out, job_out


def _dw_in(hb, dproj, job=None):
    tt = min(512, hb.shape[0])
    return _dw_call(
        "dw_in", hb, dproj,
        pl.BlockSpec((tt, D), lambda m, t: (t, 0)),
        pl.BlockSpec((None, tt, D), lambda m, t: (m, t, 0)),
        pl.BlockSpec((None, D, D), lambda m, t: (_orig_group(m) // 2, 0, _orig_group(m) % 2)),
        SDS((NCHIP, D, 2 * D), F32), NIN, tt, job)


def _dw_gate_up(h2b, dgu4, job=None):
    tt = min(512, h2b.shape[0])
    return _dw_call(
        "dw_gate_up", h2b, dgu4,
        pl.BlockSpec((tt, D), lambda k, t: (t, 0)),
        pl.BlockSpec((None, tt, FFS), lambda k, t: (k, t, 0)),
        pl.BlockSpec((None, D, FFS), lambda k, t: (k, 0, 0)),
        SDS((NCHIP, D, FFS), F32), NCHIP, tt, job)


def _dw_down(act, dx2b, job=None):
    tt = min(512, act.shape[0])
    g, job_out = _dw_call(
        "dw_down", act, dx2b,
        pl.BlockSpec((tt, FFS), lambda k, t: (t, k)),
        pl.BlockSpec((tt, D), lambda k, t: (t, 0)),
        pl.BlockSpec((FFS, D), lambda k, t: (k, 0)),
        SDS((FF, D), F32), 2, tt, job)
    return g.reshape(NCHIP, FF // NCHIP, D), job_out


def _dw_square(name, a, b, job=None):
    tt = min(512, a.shape[0])
    g, job_out = _dw_call(
        name, a, b,
        pl.BlockSpec((tt, D), lambda k, t: (t, 0)), pl.BlockSpec((tt, D), lambda k, t: (t, 0)),
        pl.BlockSpec((D, D), lambda k, t: (0, 0)), SDS((D, D), F32), 1, tt, job)
    return g.reshape(NCHIP, D // NCHIP, D), job_out


def _place():
    x, y, c = lax.axis_index("x"), lax.axis_index("y"), lax.axis_index("c")
    return x, y, c, 2 * x + y


def _chip_at(x, y, s):
    return x ^ (s >> 1), y ^ (s & 1)


class _Job:
    def __init__(self, ins, out_shapes, sems, start, finish, aliases=None):
        self.ins, self.out_shapes, self.sems = list(ins), list(out_shapes), list(sems)
        self.start, self.finish, self.aliases = start, finish, dict(aliases or {})


def _join_jobs(*jobs):
    def cut(refs, sizes):
        out, at = [], 0
        for n in sizes:
            out.append(refs[at:at + n])
            at += n
        return out

    ni = [len(j.ins) for j in jobs]
    no = [len(j.out_shapes) for j in jobs]
    ns = [len(j.sems) for j in jobs]

    def run(which):
        def go(ins, outs, sems):
            for j, a, b, c in zip(jobs, cut(ins, ni), cut(outs, no), cut(sems, ns)):
                getattr(j, which)(a, b, c)
        return go

    aliases = {}
    for k, j in enumerate(jobs):
        for a, b in j.aliases.items():
            aliases[sum(ni[:k]) + a] = sum(no[:k]) + b
    return _Job([a for j in jobs for a in j.ins], [o for j in jobs for o in j.out_shapes],
                [s for j in jobs for s in j.sems], run("start"), run("finish"), aliases)


def _call(body, *, name, grid, in_specs, out_specs, out_shape, args, scratch_shapes=(), aliases=None, job=None):
    n_in, n_out, n_scr = len(in_specs), len(out_specs), len(scratch_shapes)
    aliases = dict(aliases or {})
    if job is None:
        res = pl.pallas_call(
            body, name=name, grid=grid, in_specs=list(in_specs), out_specs=list(out_specs),
            out_shape=list(out_shape), scratch_shapes=list(scratch_shapes), input_output_aliases=aliases,
            compiler_params=_cparams())(*args)
        return list(res), []
    ji, jo = len(job.ins), len(job.out_shapes)

    def wrapped(*refs):
        ins, jin = refs[:n_in], refs[n_in:n_in + ji]
        o0 = n_in + ji
        outs, jout = refs[o0:o0 + n_out], refs[o0 + n_out:o0 + n_out + jo]
        s0 = o0 + n_out + jo
        scr, jsem = refs[s0:s0 + n_scr], refs[s0 + n_scr:]
        ids = [pl.program_id(a) for a in range(len(grid))]
        first = functools.reduce(jnp.logical_and, [i == 0 for i in ids])
        last = functools.reduce(jnp.logical_and, [i == g - 1 for i, g in zip(ids, grid)])

        @pl.when(first)
        def _():
            job.start(jin, jout, jsem)

        body(*ins, *outs, *scr)

        @pl.when(last)
        def _():
            job.finish(jin, jout, jsem)

    for a, b in job.aliases.items():
        aliases[n_in + a] = n_out + b
    res = pl.pallas_call(
        wrapped, name=name, grid=grid, in_specs=list(in_specs) + [ANY] * ji,
        out_specs=list(out_specs) + [ANY] * jo, out_shape=list(out_shape) + job.out_shapes,
        scratch_shapes=list(scratch_shapes) + job.sems, input_output_aliases=aliases,
        compiler_params=_cparams(has_side_effects=True))(*args, *job.ins)
    return list(res[:n_out]), list(res[n_out:])


def _run_job(job, name):
    ji, jo = len(job.ins), len(job.out_shapes)

    def body(*refs):
        jin, jout, jsem = refs[:ji], refs[ji:ji + jo], refs[ji + jo:]
        job.start(jin, jout, jsem)
        job.finish(jin, jout, jsem)

    return list(pl.pallas_call(
        body, name=name, in_specs=[ANY] * ji, out_specs=[ANY] * jo, out_shape=job.out_shapes,
        scratch_shapes=job.sems, input_output_aliases=job.aliases,
        compiler_params=pltpu.CompilerParams(has_side_effects=True))(*job.ins))


def _cast_shard(name, place, w):
    rows, cols = w.shape
    tr = 352 if rows % 352 == 0 else 256

    def body(pc_ref, w_ref, o_ref):
        del pc_ref
        o_ref[...] = w_ref[...].astype(BF16)

    return pl.pallas_call(
        body, name=name,
        grid_spec=pltpu.PrefetchScalarGridSpec(
            num_scalar_prefetch=1, grid=(rows // tr,),
            in_specs=[pl.BlockSpec((tr, cols), lambda i, pc: (i, 0))],
            out_specs=pl.BlockSpec((None, tr, cols), lambda i, pc: (pc[0], i, 0))),
        out_shape=SDS((NCHIP, rows, cols), BF16),
        compiler_params=_cparams(),
    )(place, w)


def _sibling_copy(ref, send_sem, recv_sem):
    x, y, c, _ = _place()
    return pltpu.make_async_remote_copy(src_ref=ref, dst_ref=ref, send_sem=send_sem, recv_sem=recv_sem,
                                        device_id=(x, y, 1 - c), device_id_type=MESH)


def _half_rows(arr, slot, core):
    half = arr.shape[1] // 2
    return arr.at[slot, pl.ds(pl.multiple_of(core * half, 16), half)]


def _gather_start(arrs, sems):
    send_sem, recv_sem = sems[0], sems[1]
    x, y, c, j = _place()
    for w, arr in enumerate(arrs):
        mine = _half_rows(arr, j, c)
        for s in range(1, NCHIP):
            cx, cy = _chip_at(x, y, s)
            pltpu.make_async_remote_copy(
                src_ref=mine, dst_ref=mine, send_sem=send_sem.at[w, s - 1], recv_sem=recv_sem.at[w, s - 1],
                device_id=(cx, cy, c), device_id_type=MESH).start()


def _gather_finish(arrs, sems):
    send_sem, recv_sem, fsend_sem, frecv_sem = sems
    _, _, c, j = _place()
    for s in range(1, NCHIP):
        for w, arr in enumerate(arrs):
            landed = _half_rows(arr, j ^ s, c)
            _sibling_copy(landed, send_sem.at[w, s - 1], recv_sem.at[w, s - 1]).wait_recv()
            _sibling_copy(landed, fsend_sem.at[w, s - 1], frecv_sem.at[w, s - 1]).start()
    for s in range(1, NCHIP):
        for w, arr in enumerate(arrs):
            theirs = _half_rows(arr, j ^ s, 1 - c)
            _sibling_copy(theirs, fsend_sem.at[w, s - 1], frecv_sem.at[w, s - 1]).wait_recv()
    for s in range(1, NCHIP):
        for w, arr in enumerate(arrs):
            _sibling_copy(_half_rows(arr, j, c), send_sem.at[w, s - 1], recv_sem.at[w, s - 1]).wait_send()
            _sibling_copy(_half_rows(arr, j ^ s, c), fsend_sem.at[w, s - 1], frecv_sem.at[w, s - 1]).wait_send()


def _gather_job(arrs):
    n = len(arrs)
    return _Job(arrs, [SDS(a.shape, a.dtype) for a in arrs], [pltpu.SemaphoreType.DMA((n, NCHIP - 1))] * 4,
                lambda ins, outs, sems: _gather_start(outs, sems),
                lambda ins, outs, sems: _gather_finish(outs, sems), {k: k for k in range(n)})


def _exchange_job(arrs, out_shapes, n, copies):
    def start(ins, outs, sems):
        for cp in copies(ins, outs, sems[0], sems[1]):
            cp.start()

    def finish(ins, outs, sems):
        for cp in copies(ins, outs, sems[0], sems[1]):
            cp.wait()

    return _Job(arrs, out_shapes, [pltpu.SemaphoreType.DMA((n,))] * 2, start, finish)


def _pair_exchange_job(grads):
    def copies(ins, outs, send_sem, recv_sem):
        x, y, c, _ = _place()
        res = []
        for w in range(len(grads)):
            half = ins[w].shape[1] // 2
            theirs = pl.ds(pl.multiple_of((1 - c) * half, 8), half)
            res.append(pltpu.make_async_remote_copy(
                src_ref=ins[w].at[:, theirs, :], dst_ref=outs[w], send_sem=send_sem.at[w],
                recv_sem=recv_sem.at[w], device_id=(x, y, 1 - c), device_id_type=MESH))
        return res

    return _exchange_job(grads, [SDS((NCHIP, g.shape[1] // 2, g.shape[2]), F32) for g in grads],
                         len(grads), copies)


def _row_tile(rows):
    return 176 if rows % 176 == 0 and rows % 128 else 128


def _pair_sum(name, place, g, sib):
    half, cols = sib.shape[1], sib.shape[2]
    tr = _row_tile(half)
    nt = half // tr

    def body(pc_ref, g_ref, s_ref, own_ref, out_ref):
        del pc_ref
        v = g_ref[...] + s_ref[...]
        out_ref[...] = v.astype(BF16)

        @pl.when(pl.program_id(1) == 0)
        def _():
            own_ref[...] = v

    return pl.pallas_call(
        body, name=name,
        grid_spec=pltpu.PrefetchScalarGridSpec(
            num_scalar_prefetch=1, grid=(nt, NCHIP),
            in_specs=[pl.BlockSpec((None, tr, cols), lambda i, s, pc: (pc[0] ^ s, pc[1] * nt + i, 0)),
                      pl.BlockSpec((None, tr, cols), lambda i, s, pc: (pc[0] ^ s, i, 0))],
            out_specs=[pl.BlockSpec((tr, cols), lambda i, s, pc: (i, 0)),
                       pl.BlockSpec((None, tr, cols), lambda i, s, pc: (s, i, 0))]),
        out_shape=[SDS((half, cols), F32), SDS((NCHIP, half, cols), BF16)],
        compiler_params=_cparams(),
    )(place, g, sib)


def _chip_exchange_job(parts):
    def copies(ins, outs, send_sem, recv_sem):
        x, y, c, _ = _place()
        res = []
        for w in range(len(parts)):
            for s in range(1, NCHIP):
                cx, cy = _chip_at(x, y, s)
                k = w * (NCHIP - 1) + s - 1
                res.append(pltpu.make_async_remote_copy(
                    src_ref=ins[w].at[s], dst_ref=outs[w].at[s - 1], send_sem=send_sem.at[k],
                    recv_sem=recv_sem.at[k], device_id=(cx, cy, c), device_id_type=MESH))
        return res

    return _exchange_job(parts, [SDS((NCHIP - 1,) + p.shape[1:], BF16) for p in parts],
                         len(parts) * (NCHIP - 1), copies)


def _chip_sum(name, own, rem):
    half, cols = own.shape
    tr = _row_tile(half)

    def body(own_ref, rem_ref, out_ref):
        out_ref[...] = ((own_ref[...] + rem_ref[0].astype(F32)) + rem_ref[1].astype(F32)) + rem_ref[2].astype(F32)

    return pl.pallas_call(
        body, name=name, grid=(half // tr,),
        in_specs=[pl.BlockSpec((tr, cols), lambda i: (i, 0)),
                  pl.BlockSpec((NCHIP - 1, tr, cols), lambda i: (0, i, 0))],
        out_specs=pl.BlockSpec((tr, cols), lambda i: (i, 0)),
        out_shape=SDS((half, cols), F32),
        compiler_params=_cparams(),
    )(own, rem)


def _share_halves_job(halves):
    def copies(ins, outs, send_sem, recv_sem):
        x, y, c, _ = _place()
        return [pltpu.make_async_remote_copy(
            src_ref=ins[w], dst_ref=outs[w], send_sem=send_sem.at[w], recv_sem=recv_sem.at[w],
            device_id=(x, y, 1 - c), device_id_type=MESH) for w in range(len(halves))]

    return _exchange_job(halves, [SDS(h.shape, F32) for h in halves], len(halves), copies)


def _adamw_math(w, g, m, v):
    m = B1 * m + (1.0 - B1) * g
    v = B2 * v + (1.0 - B2) * (g * g)
    m_hat = m / (1.0 - B1 ** STEP)
    v_hat = v / (1.0 - B2 ** STEP)
    delta = -LR * (m_hat / (jnp.sqrt(v_hat) + AEPS) + WD * w)
    return delta, m, v


def _adamw(name, place, w, own, sib, m, v):
    rows, cols = w.shape
    half = rows // 2
    tr = 352 if half % 352 == 0 else min(256, half)
    nt = half // tr

    def body(pc_ref, w_ref, own_ref, sib_ref, m_ref, v_ref, g_ref, d_ref, mo_ref, vo_ref):
        g = jnp.where(pl.program_id(0) == pc_ref[1], own_ref[...], sib_ref[...])
        d, mn, vn = _adamw_math(w_ref[...], g, m_ref[...], v_ref[...])
        g_ref[...] = g
        d_ref[...] = d
        mo_ref[...] = mn
        vo_ref[...] = vn

    full = pl.BlockSpec((tr, cols), lambda h, i, pc: (h * nt + i, 0))
    part = pl.BlockSpec((tr, cols), lambda h, i, pc: (i, 0))
    return pl.pallas_call(
        body, name=name,
        grid_spec=pltpu.PrefetchScalarGridSpec(
            num_scalar_prefetch=1, grid=(2, nt),
            in_specs=[full, part, part, full, full], out_specs=[full] * 4),
        out_shape=[SDS((rows, cols), F32)] * 4,
        compiler_params=_cparams(),
    )(place, w, own, sib, m, v)


def _small_allreduce_adamw(sp, w, m, v):
    shape = sp.shape

    def body(sp_ref, w_ref, m_ref, v_ref, g_ref, d_ref, mo_ref, vo_ref,
             sib_s, pair_s, chip_s, send_sem, recv_sem):
        x, y, c, j = _place()
        cp = pltpu.make_async_remote_copy(
            src_ref=sp_ref, dst_ref=sib_s, send_sem=send_sem.at[0], recv_sem=recv_sem.at[0],
            device_id=(x, y, 1 - c), device_id_type=MESH)
        cp.start()
        cp.wait()
        pair_s[...] = sp_ref[...] + sib_s[...]
        cps = []
        for s in range(1, NCHIP):
            cx, cy = _chip_at(x, y, s)
            cp = pltpu.make_async_remote_copy(
                src_ref=pair_s, dst_ref=chip_s.at[s], send_sem=send_sem.at[s], recv_sem=recv_sem.at[s],
                device_id=(cx, cy, c), device_id_type=MESH)
            cp.start()
            cps.append(cp)
        chip_s[0] = pair_s[...]
        for cp in cps:
            cp.wait()
        tot = chip_s[j]
        for k in range(1, NCHIP):
            tot = tot + chip_s[k ^ j]
        g_ref[...] = tot
        d, mn, vn = _adamw_math(w_ref[...], tot, m_ref[...], v_ref[...])
        d_ref[...] = d
        mo_ref[...] = mn
        vo_ref[...] = vn

    vm = pl.BlockSpec(memory_space=pltpu.VMEM)
    return pl.pallas_call(
        body, name="small_allreduce_adamw",
        in_specs=[vm] * 4, out_specs=[vm] * 4, out_shape=[SDS(shape, F32)] * 4,
        scratch_shapes=[pltpu.VMEM(shape, F32), pltpu.VMEM(shape, F32), pltpu.VMEM((NCHIP,) + shape, F32),
                        pltpu.SemaphoreType.DMA((NCHIP,)), pltpu.SemaphoreType.DMA((NCHIP,))],
        compiler_params=pltpu.CompilerParams(has_side_effects=True),
    )(sp, w, m, v)


def _pack_small(first, mix, ln_g, ln_b, b_s, lbt, hn, ffn, fin, w_s):
    rows = [first.reshape(1, D), mix.reshape(1, D), ln_g.reshape(1, D), ln_b.reshape(1, D),
            b_s.reshape(1, D), lbt.reshape(2, D), hn.reshape(1, D), ffn.reshape(1, D), fin.reshape(1, D),
            jnp.zeros((6, D), F32)]
    return jnp.concatenate(rows + [w_s.reshape(NG, GCH, GCH).transpose(1, 0, 2).reshape(GCH, D)], axis=0)


def _unpack_small(p):
    w_s = p[16:].reshape(GCH, NG, GCH).transpose(1, 0, 2).reshape(1, NG, GCH, GCH)
    return dict(norm_mix_g=p[1:2], gmlp_ln_g=p[2:3], gmlp_ln_b=p[3:4], gmlp_b_s=p[4].reshape(1, NG, GCH),
                hgrn_lb_table=p[5:7], hgrn_norm_g=p[7:8], norm_ffn_g=p[8:9], norm_final_g=p[9],
                gmlp_w_s=w_s)


SMALL = ("norm_mix_g", "gmlp_ln_g", "gmlp_ln_b", "gmlp_w_s", "gmlp_b_s", "hgrn_lb_table", "hgrn_norm_g",
         "norm_ffn_g", "norm_final_g")
BIG = ("w_in", "w_gate_up", "w_branch_a", "w_branch_b", "w_out", "w_down")
ORDER = ("norm_mix_g", "w_in", "gmlp_ln_g", "gmlp_ln_b", "gmlp_w_s", "gmlp_b_s", "hgrn_lb_table",
         "hgrn_norm_g", "w_branch_a", "w_branch_b", "w_out", "norm_ffn_g", "w_gate_up", "w_down",
         "norm_final_g")


def kernel(x, norm_mix_g, w_in, gmlp_ln_g, gmlp_ln_b, gmlp_w_s, gmlp_b_s, hgrn_lb_table, hgrn_norm_g, w_branch_a, w_branch_b, w_out, norm_ffn_g, w_gate_up, w_down, norm_final_g, loss_target, m_norm_mix_g, m_w_in, m_gmlp_ln_g, m_gmlp_ln_b, m_gmlp_w_s, m_gmlp_b_s, m_hgrn_lb_table, m_hgrn_norm_g, m_w_branch_a, m_w_branch_b, m_w_out, m_norm_ffn_g, m_w_gate_up, m_w_down, m_norm_final_g, v_norm_mix_g, v_w_in, v_gmlp_ln_g, v_gmlp_ln_b, v_gmlp_w_s, v_gmlp_b_s, v_hgrn_lb_table, v_hgrn_norm_g, v_w_branch_a, v_w_branch_b, v_w_out, v_norm_ffn_g, v_w_gate_up, v_w_down, v_norm_final_g):
    args = dict(locals())
    T = x.shape[1]
    xs = x.reshape(T, D)
    target = loss_target.reshape(T, D)
    big = {n: args[n].reshape(args[n].shape[1:]) for n in BIG}
    big_m = {n: args["m_" + n].reshape(args[n].shape[1:]) for n in BIG}
    big_v = {n: args["v_" + n].reshape(args[n].shape[1:]) for n in BIG}

    x_i, y_i, c_i = lax.axis_index("x"), lax.axis_index("y"), lax.axis_index("c")
    place = jnp.stack([2 * x_i + y_i, c_i]).astype(jnp.int32)
    cast = {n: _cast_shard("cast_" + n, place, big[n]) for n in BIG}
    (w_in4,) = _run_job(_gather_job([cast["w_in"]]), "gather_w_in")
    tril = jnp.tril(jnp.ones((GCH, GCH), bool))
    wm = jnp.where(tril, gmlp_w_s[0], 0.0).astype(BF16)
    wm_t = jnp.swapaxes(wm, 1, 2)
    b_t = gmlp_b_s[0].T

    (proj, hb), (w_gu4, w_a4, w_b4, w_out4, w_down4) = _proj_fwd(
        xs, norm_mix_g, w_in4, _gather_job([cast[n] for n in BIG[1:]]))
    w_a, w_b, w_o = (w.reshape(D, D) for w in (w_a4, w_b4, w_out4))
    w_dn = w_down4.reshape(FF, D)
    ab, y_a = _gmlp_fwd(proj, gmlp_ln_g, gmlp_ln_b, wm, b_t, w_a)
    o_raw, obb, st_before = _hgrn_fwd(proj, hgrn_lb_table, hgrn_norm_g)
    y_b, mgb, x1 = _merge_fwd(xs, y_a, obb, proj, w_b, w_o)
    act, dx2b, h2b, dgu4, dx1, dx1b, acc_ffn = _ffn_fwd_bwd(
        x1, target, norm_ffn_g, norm_final_g.reshape(1, D), w_gu4, w_dn)

    grads, owns, parts, halves, sibh = {}, {}, {}, {}, {}

    def pair_sums(names, sibs):
        for n, s in zip(names, sibs):
            owns[n], parts[n] = _pair_sum("rs_pair_sum_" + n, place, grads[n], s)

    def chip_sums(names, got):
        for n, r in zip(names, got):
            halves[n] = _chip_sum("rs_chip_sum_" + n, owns[n], r)

    ffn, mix = ("w_gate_up", "w_down"), ("w_branch_a", "w_branch_b", "w_out")
    grads["w_gate_up"], _ = _dw_gate_up(h2b, dgu4)
    grads["w_down"], _ = _dw_down(act, dx2b)
    (dya, dyb, da, dob, dproj), got = _merge_bwd(
        dx1b, y_a, y_b, proj, w_o, w_a, w_b, job=_pair_exchange_job([grads[n] for n in ffn]))
    pair_sums(ffn, got)
    grads["w_branch_a"], _ = _dw_square("dw_branch_a", ab, dya)
    grads["w_branch_b"], _ = _dw_square("dw_branch_b", obb, dyb)
    grads["w_out"], _ = _dw_square("dw_out", mgb, dx1b)
    (dproj, acc_hgrn), got = _hgrn_bwd(
        dproj, dob, o_raw, proj, st_before, hgrn_lb_table, hgrn_norm_g,
        job=_join_jobs(_chip_exchange_job([parts[n] for n in ffn]), _pair_exchange_job([grads[n] for n in mix])))
    chip_sums(ffn, got[:2])
    pair_sums(mix, got[2:])
    dproj, acc_ln, dws, dmix = _gmlp_bwd(dproj, da, proj, gmlp_ln_g, gmlp_ln_b, wm, wm_t, b_t)
    (grad_x, acc_mix), got = _proj_bwd(
        dproj, w_in4, xs, dx1, norm_mix_g,
        job=_join_jobs(_share_halves_job([halves[n] for n in ffn]), _chip_exchange_job([parts[n] for n in mix])))
    sibh.update(zip(ffn, got[:2]))
    chip_sums(mix, got[2:])
    grads["w_in"], got = _dw_in(hb, dproj, job=_share_halves_job([halves[n] for n in mix]))
    sibh.update(zip(mix, got))
    pair_sums(("w_in",), _run_job(_pair_exchange_job([grads["w_in"]]), "rs_pair_exchange_w_in"))
    chip_sums(("w_in",), _run_job(_chip_exchange_job([parts["w_in"]]), "rs_chip_exchange_w_in"))
    (sibh["w_in"],) = _run_job(_share_halves_job([halves["w_in"]]), "rs_share_halves_w_in")
    out = {}
    for n in BIG:
        g, d, mn, vn = _adamw("adamw_" + n, place, big[n], halves[n], sibh[n], big_m[n], big_v[n])
        shp = args[n].shape
        out[n] = (g.reshape(shp), d.reshape(shp), mn.reshape(shp), vn.reshape(shp))

    lbv = jax.nn.sigmoid(hgrn_lb_table[0] - hgrn_lb_table[1])
    d_t0 = jnp.sum(acc_hgrn[0], axis=0) * lbv * (1.0 - lbv)
    loss_row = jnp.zeros((D,), F32).at[0].set(jnp.sum(acc_ffn[0]))
    dws_m = jnp.where(tril[:, None, :], dws.reshape(GCH, NG, GCH), 0.0).transpose(1, 0, 2)
    db_s = jnp.sum(dmix.reshape(GCH, NG, GCH), axis=-1).T
    sp = _pack_small(loss_row, jnp.sum(acc_mix, 0), jnp.sum(acc_ln[0], 0), jnp.sum(acc_ln[1], 0), db_s,
                     jnp.stack([d_t0, -d_t0]), jnp.sum(acc_hgrn[1], 0), jnp.sum(acc_ffn[2], 0),
                     jnp.sum(acc_ffn[1], 0), dws_m)
    zero = jnp.zeros((D,), F32)

    def pack(prefix):
        a = lambda n: args[prefix + n]
        return _pack_small(zero, a("norm_mix_g"), a("gmlp_ln_g"), a("gmlp_ln_b"), a("gmlp_b_s"),
                           a("hgrn_lb_table"), a("hgrn_norm_g"), a("norm_ffn_g"), a("norm_final_g"),
                           a("gmlp_w_s"))

    packed = _small_allreduce_adamw(sp, pack(""), pack("m_"), pack("v_"))
    loss = packed[0][0, 0]
    small = [_unpack_small(p) for p in packed]
    for n in SMALL:
        out[n] = tuple(s[n] for s in small)
    return (loss, grad_x.reshape(x.shape), *[out[n][0] for n in ORDER], *[out[n][1] for n in ORDER],
            *[out[n][2] for n in ORDER], *[out[n][3] for n in ORDER])
```

```python
import functools
import math

import jax
import jax.numpy as jnp
from jax import lax
from jax.experimental import pallas as pl
from jax.experimental.pallas import tpu as pltpu

F32 = jnp.float32
BF16 = jnp.bfloat16
SDS = jax.ShapeDtypeStruct
MESH = pl.DeviceIdType.MESH
ANY = pl.BlockSpec(memory_space=pl.ANY)

D = 1024
NIN = 8
NG = 8
GCH = 128
NH = 8
HD = 128
HCH = 64
FF = 2816
FFS = 1408
NCHIP = 4
EPS = 1e-6
QSCALE = HD ** -0.5
GELU_C0 = math.sqrt(2.0 / math.pi)
GELU_C1 = 0.044715
LR, B1, B2, AEPS, WD, STEP = 0.001, 0.9, 0.999, 1e-08, 0.01, 10
VMEM_LIMIT_V7X = 56 * 1024 * 1024
SP_ROWS = 144


def _cparams(**kw):
    return pltpu.CompilerParams(vmem_limit_bytes=VMEM_LIMIT_V7X, **kw)


def _mm(a, b):
    return jnp.dot(a, b, preferred_element_type=F32)


def _mm_nt(a, b):
    return lax.dot_general(a, b, (((1,), (1,)), ((), ())), preferred_element_type=F32)


def _mm_tn(a, b):
    return lax.dot_general(a, b, (((0,), (0,)), ((), ())), preferred_element_type=F32)


def _rows8(x):
    r, c = x.shape
    return jnp.sum(x.reshape(r // 8, 8, c), axis=0)


def _mean(x):
    return jnp.mean(x, axis=-1, keepdims=True)


def _sigmoid(x):
    return 1.0 / (1.0 + jnp.exp(-x))


def _gelu(x):
    t = jnp.tanh(GELU_C0 * (x + GELU_C1 * x * x * x))
    return 0.5 * x * (1.0 + t), t


def _gelu_grad(x, t):
    return 0.5 * (1.0 + t) + 0.5 * x * (1.0 - t * t) * (GELU_C0 * (1.0 + 3.0 * GELU_C1 * x * x))


def _orig_group(m):
    return jnp.where(m < 6, (m + 2) % 6, m)


def _proj_fwd(place, x, g_mix, w_in4, later):
    T = x.shape[0]
    tm = min(512, T)
    ni = T // tm
    n = len(later)

    def body(pc_ref, x_ref, g_ref, *rest):
        proj_ref, h_ref, w_all = rest[1 + n:4 + n]
        gathered = rest[4 + n:4 + 2 * n]
        hs, wbuf, wsem = rest[4 + 2 * n:7 + 2 * n]
        w_sems, later_sems = rest[7 + 2 * n:11 + 2 * n], rest[11 + 2 * n:]
        jp, i = pl.program_id(0), pl.program_id(1)
        s, blk = jp // 2, jp % 2

        @pl.when((jp == 0) & (i == 0))
        def _():
            _gather_start([w_all], w_sems)
            _gather_start(gathered, later_sems)

        @pl.when(jp == 0)
        def _():
            xv = x_ref[...]
            r = lax.rsqrt(_mean(xv * xv) + EPS)
            hb = (xv * r * g_ref[...]).astype(BF16)
            hs[i] = hb
            h_ref[...] = hb

        for dist in range(1, NCHIP):
            @pl.when((jp == 2 * dist) & (i == 0))
            def _():
                _gather_land([w_all], w_sems, dist)

        @pl.when(i == 0)
        def _():
            cols = pl.ds(pl.multiple_of(blk * D, 128), D)
            cp = pltpu.make_async_copy(w_all.at[pc_ref[0] ^ s, :, cols], wbuf, wsem)
            cp.start()
            cp.wait()

        proj_ref[...] = _mm(hs[i], wbuf[...])

        @pl.when((jp == NIN - 1) & (i == ni - 1))
        def _():
            _gather_drain([w_all], w_sems)
            _gather_finish(gathered, later_sems)

    tile = lambda jp, i, pc: (jnp.where(jp == 0, i, ni - 1), 0)
    sem4 = lambda k: [pltpu.SemaphoreType.DMA((k, NCHIP - 1))] * 4
    res = pl.pallas_call(
        body, name="proj_fwd",
        grid_spec=pltpu.PrefetchScalarGridSpec(
            num_scalar_prefetch=1, grid=(NIN, ni),
            in_specs=[pl.BlockSpec((tm, D), tile), pl.BlockSpec((1, D), lambda jp, i, pc: (0, 0))] + [ANY] * (1 + n),
            out_specs=[pl.BlockSpec((tm, D), lambda jp, i, pc: (i, 2 * (pc[0] ^ (jp // 2)) + jp % 2)),
                       pl.BlockSpec((tm, D), tile)] + [ANY] * (1 + n),
            scratch_shapes=[pltpu.VMEM((ni, tm, D), BF16), pltpu.VMEM((D, D), BF16), pltpu.SemaphoreType.DMA]
            + sem4(1) + sem4(n)),
        out_shape=[SDS((T, NIN * D), F32), SDS((T, D), BF16), SDS(w_in4.shape, BF16)]
        + [SDS(a.shape, a.dtype) for a in later],
        input_output_aliases={3 + k: 2 + k for k in range(1 + n)},
        compiler_params=_cparams(has_side_effects=True),
    )(place, x, g_mix, w_in4, *later)
    return res[:2], res[2], res[3:]


def _layer_norm_stats(gv):
    mu = _mean(gv)
    xc = gv - mu
    rs = lax.rsqrt(_mean(xc * xc) + EPS)
    return xc * rs, rs


def _gmlp_fwd(proj, ln_g, ln_b, wm, b_t, w_a):
    T = proj.shape[0]
    tm = min(256, T)

    def body(u_ref, v_ref, lg_ref, lb_ref, wm_ref, bt_ref, wa_ref, a_ref, ya_ref, a_s):
        gu, _ = _gelu(u_ref[...])
        gv, _ = _gelu(v_ref[...])
        vhat, _ = _layer_norm_stats(gv)
        vnb = (vhat * lg_ref[...] + lb_ref[...]).astype(BF16)
        for ch in range(tm // GCH):
            rows = slice(GCH * ch, GCH * (ch + 1))
            for g in range(NG):
                cols = slice(128 * g, 128 * (g + 1))
                mixed = _mm(wm_ref[g], vnb[rows, cols]) + bt_ref[:, g:g + 1]
                a_s[rows, cols] = gu[rows, cols] * mixed
        ab = a_s[...].astype(BF16)
        a_ref[...] = ab
        ya_ref[...] = _mm(ab, wa_ref[...])

    row = lambda i: (0, 0)
    return pl.pallas_call(
        body, name="gmlp_fwd", grid=(T // tm,),
        in_specs=[pl.BlockSpec((tm, D), lambda i: (i, 0)), pl.BlockSpec((tm, D), lambda i: (i, 1)),
                  pl.BlockSpec((1, D), row), pl.BlockSpec((1, D), row),
                  pl.BlockSpec((NG, GCH, GCH), lambda i: (0, 0, 0)), pl.BlockSpec((GCH, NG), row),
                  pl.BlockSpec((D, D), row)],
        out_specs=[pl.BlockSpec((tm, D), lambda i: (i, 0)), pl.BlockSpec((tm, D), lambda i: (i, 0))],
        out_shape=[SDS((T, D), BF16), SDS((T, D), F32)],
        scratch_shapes=[pltpu.VMEM((tm, D), F32)],
        compiler_params=_cparams(),
    )(proj, proj, ln_g, ln_b, wm, b_t, w_a)


def _cumsum64(x, row):
    for s in (1, 2, 4, 8, 16, 32):
        x = x + jnp.where(row >= s, pltpu.roll(x, s, 0), 0.0)
    return x


def _revcumsum64(x, row):
    n = x.shape[0]
    for s in (1, 2, 4, 8, 16, 32):
        x = x + jnp.where(row < HCH - s, pltpu.roll(x, n - s, 0), 0.0)
    return x


def _seg_sum(x):
    n, c = x.shape
    s = jnp.sum(x.reshape(n // HCH, HCH, c), axis=1, keepdims=True)
    return jnp.broadcast_to(s, (n // HCH, HCH, c)).reshape(n, c)


def _hgrn_gates(fl, lbv, row):
    s = _sigmoid(fl)
    f = lbv + (1.0 - lbv) * s
    a = _cumsum64(jnp.log(f), row)
    a_mid = _seg_sum(jnp.where(row == HCH // 2 - 1, a, 0.0))
    a_last = _seg_sum(jnp.where(row == HCH - 1, a, 0.0))
    return s, f, a, a_mid, a_last


def _hgrn_fwd(proj, lb_table, norm_g):
    T = proj.shape[0]
    tb = min(512, T)
    nc = tb // HCH

    def body(q_ref, fl_ref, v_ref, g_ref, lbt_ref, gn_ref, o_ref, ob_ref, stb_ref, st_s, o_s):
        @pl.when(pl.program_id(1) == 0)
        def _():
            st_s[...] = jnp.zeros_like(st_s)

        row = lax.broadcasted_iota(jnp.int32, (tb, HD), 0) & (HCH - 1)
        lbv = _sigmoid(lbt_ref[0:1, :] - lbt_ref[1:2, :])
        _, f, a, a_mid, a_last = _hgrn_gates(fl_ref[...], lbv, row)
        k = 1.0 - f
        qs = q_ref[...] * QSCALE
        q_in = (qs * jnp.exp(a - a_mid)).astype(BF16)
        k_in = (k * jnp.exp(a_mid - a)).astype(BF16)
        q_a = (qs * jnp.exp(a)).astype(BF16)
        k_d = (k * jnp.exp(a_last - a)).astype(BF16)
        dec = jnp.exp(a_last)
        vb = v_ref[...].astype(BF16)
        tri = (lax.broadcasted_iota(jnp.int32, (HCH, HCH), 0)
               >= lax.broadcasted_iota(jnp.int32, (HCH, HCH), 1))
        for c in range(nc):
            sl = slice(HCH * c, HCH * (c + 1))
            st = st_s[...]
            stb_ref[c] = st
            sc = jnp.where(tri, _mm_nt(q_in[sl], k_in[sl]), 0.0)
            o_s[sl, :] = _mm(sc.astype(BF16), vb[sl]) + _mm_nt(q_a[sl], st.astype(BF16))
            d64 = dec[sl]
            st_s[...] = st * jnp.concatenate([d64, d64], axis=0) + _mm_tn(vb[sl], k_d[sl])
        o = o_s[...]
        r = lax.rsqrt(_mean(o * o) + EPS)
        g = g_ref[...]
        o_ref[...] = o
        ob_ref[...] = (o * r * gn_ref[...] * (g * _sigmoid(g))).astype(BF16)

    def col(off):
        return pl.BlockSpec((tb, HD), lambda h, cb: (cb, off * NH + h))

    return pl.pallas_call(
        body, name="hgrn_fwd", grid=(NH, T // tb),
        in_specs=[col(2), col(3), col(4), col(5),
                  pl.BlockSpec((2, HD), lambda h, cb: (0, h)), pl.BlockSpec((1, HD), lambda h, cb: (0, h))],
        out_specs=[pl.BlockSpec((tb, HD), lambda h, cb: (cb, h)), pl.BlockSpec((tb, HD), lambda h, cb: (cb, h)),
                   pl.BlockSpec((None, nc, HD, HD), lambda h, cb: (h, cb, 0, 0))],
        out_shape=[SDS((T, D), F32), SDS((T, D), BF16), SDS((NH, T // HCH, HD, HD), F32)],
        scratch_shapes=[pltpu.VMEM((HD, HD), F32), pltpu.VMEM((tb, HD), F32)],
        compiler_params=_cparams(),
    )(proj, proj, proj, proj, lb_table, norm_g)


def _merge_fwd(x, y_a, ob, proj, w_b, w_out):
    T = x.shape[0]
    tm = min(512, T)

    def body(x_ref, ya_ref, ob_ref, ga_ref, gb_ref, wb_ref, wo_ref, yb_ref, mg_ref, x1_ref):
        yb = _mm(ob_ref[...], wb_ref[...])
        merged = (_sigmoid(ga_ref[...]) * ya_ref[...] + _sigmoid(gb_ref[...]) * yb).astype(BF16)
        yb_ref[...] = yb
        mg_ref[...] = merged
        x1_ref[...] = x_ref[...] + _mm(merged, wo_ref[...])

    t = lambda i: (i, 0)
    w = lambda i: (0, 0)
    return pl.pallas_call(
        body, name="merge_fwd", grid=(T // tm,),
        in_specs=[pl.BlockSpec((tm, D), t), pl.BlockSpec((tm, D), t), pl.BlockSpec((tm, D), t),
                  pl.BlockSpec((tm, D), lambda i: (i, 6)), pl.BlockSpec((tm, D), lambda i: (i, 7)),
                  pl.BlockSpec((D, D), w), pl.BlockSpec((D, D), w)],
        out_specs=[pl.BlockSpec((tm, D), t)] * 3,
        out_shape=[SDS((T, D), F32), SDS((T, D), BF16), SDS((T, D), F32)],
        compiler_params=_cparams(),
    )(x, y_a, ob, proj, proj, w_b, w_out)


def _ffn_fwd_bwd(x1, target, g_ffn, g_fin, w_gu4, w_down):
    T = x1.shape[0]
    tm = min(256, T)
    inv_d = 1.0 / D

    def body(x1_ref, tg_ref, gf_ref, gn_ref, wgu_ref, wd_ref,
             act_ref, dx2b_ref, h2b_ref, dgu_ref, dx1_ref, dx1b_ref, acc_ref):
        @pl.when(pl.program_id(0) == 0)
        def _():
            acc_ref[...] = jnp.zeros_like(acc_ref)

        x1v = x1_ref[...]
        gf = gf_ref[...]
        gn = gn_ref[...]
        rr1 = lax.rsqrt(_mean(x1v * x1v) + EPS)
        x1n = x1v * rr1
        h2b = (x1n * gf).astype(BF16)
        h2b_ref[...] = h2b
        p = [_mm(h2b, wgu_ref[k]) for k in range(NCHIP)]
        sg = [_sigmoid(p[0]), _sigmoid(p[1])]
        si = [p[0] * sg[0], p[1] * sg[1]]
        x2 = x1v
        for k in range(2):
            actk = (si[k] * p[2 + k]).astype(BF16)
            act_ref[:, FFS * k:FFS * (k + 1)] = actk
            x2 = x2 + _mm(actk, wd_ref[FFS * k:FFS * (k + 1), :])
        rr2 = lax.rsqrt(_mean(x2 * x2) + EPS)
        x2n = x2 * rr2
        e = x2n * gn - tg_ref[...]
        acc_ref[0] += _rows8(e * e) * (0.5 * inv_d)
        dy = e * inv_d
        acc_ref[1] += _rows8(dy * x2n)
        dxn = dy * gn
        dx2 = rr2 * (dxn - x2n * _mean(dxn * x2n))
        dx2b = dx2.astype(BF16)
        dx2b_ref[...] = dx2b
        dh2 = None
        for k in range(2):
            dact = _mm_nt(dx2b, wd_ref[FFS * k:FFS * (k + 1), :])
            dgate = (dact * p[2 + k] * (sg[k] * (1.0 + p[k] * (1.0 - sg[k])))).astype(BF16)
            dup = (dact * si[k]).astype(BF16)
            dgu_ref[k] = dgate
            dgu_ref[2 + k] = dup
            part = _mm_nt(dgate, wgu_ref[k]) + _mm_nt(dup, wgu_ref[2 + k])
            dh2 = part if dh2 is None else dh2 + part
        acc_ref[2] += _rows8(dh2 * x1n)
        dxn1 = dh2 * gf
        dx1 = dx2 + rr1 * (dxn1 - x1n * _mean(dxn1 * x1n))
        dx1_ref[...] = dx1
        dx1b_ref[...] = dx1.astype(BF16)

    t = lambda i: (i, 0)
    w = lambda i: (0, 0)
    one = pl.Buffered(1)
    return pl.pallas_call(
        body, name="ffn_fwd_bwd", grid=(T // tm,),
        in_specs=[pl.BlockSpec((tm, D), t), pl.BlockSpec((tm, D), t),
                  pl.BlockSpec((1, D), w), pl.BlockSpec((1, D), w),
                  pl.BlockSpec((NCHIP, D, FFS), lambda i: (0, 0, 0), pipeline_mode=one),
                  pl.BlockSpec((FF, D), w, pipeline_mode=one)],
        out_specs=[pl.BlockSpec((tm, FF), t), pl.BlockSpec((tm, D), t), pl.BlockSpec((tm, D), t),
                   pl.BlockSpec((NCHIP, tm, FFS), lambda i: (0, i, 0)),
                   pl.BlockSpec((tm, D), t), pl.BlockSpec((tm, D), t),
                   pl.BlockSpec((3, 8, D), lambda i: (0, 0, 0))],
        out_shape=[SDS((T, FF), BF16), SDS((T, D), BF16), SDS((T, D), BF16),
                   SDS((NCHIP, T, FFS), BF16), SDS((T, D), F32), SDS((T, D), BF16),
                   SDS((3, 8, D), F32)],
        compiler_params=_cparams(),
    )(x1, target, g_ffn, g_fin, w_gu4, w_down)


def _merge_bwd(dx1b, y_a, y_b, proj, w_out, w_a, w_b, job=None):
    T = dx1b.shape[0]
    tm = min(512, T)

    def body(dx_ref, ya_ref, yb_ref, ga_ref, gb_ref, wo_ref, wa_ref, wb_ref,
             dya_ref, dyb_ref, da_ref, dob_ref, dp_ref):
        dm = _mm_nt(dx_ref[...], wo_ref[...])
        sa = _sigmoid(ga_ref[...])
        sb = _sigmoid(gb_ref[...])
        dya = (dm * sa).astype(BF16)
        dyb = (dm * sb).astype(BF16)
        dya_ref[...] = dya
        dyb_ref[...] = dyb
        dp_ref[0] = (dm * ya_ref[...] * sa * (1.0 - sa)).astype(BF16)
        dp_ref[1] = (dm * yb_ref[...] * sb * (1.0 - sb)).astype(BF16)
        da_ref[...] = _mm_nt(dya, wa_ref[...])
        dob_ref[...] = _mm_nt(dyb, wb_ref[...])

    t = lambda i: (i, 0)
    w = lambda i: (0, 0)
    return _call(
        body, name="merge_bwd", grid=(T // tm,),
        in_specs=[pl.BlockSpec((tm, D), t), pl.BlockSpec((tm, D), t), pl.BlockSpec((tm, D), t),
                  pl.BlockSpec((tm, D), lambda i: (i, 6)), pl.BlockSpec((tm, D), lambda i: (i, 7)),
                  pl.BlockSpec((D, D), w), pl.BlockSpec((D, D), w), pl.BlockSpec((D, D), w)],
        out_specs=[pl.BlockSpec((tm, D), t)] * 4 + [pl.BlockSpec((2, tm, D), lambda i: (3, i, 0))],
        out_shape=[SDS((T, D), BF16), SDS((T, D), BF16), SDS((T, D), F32), SDS((T, D), F32),
                   SDS((NIN, T, D), BF16)],
        args=(dx1b, y_a, y_b, proj, proj, w_out, w_a, w_b), job=job)


def _hgrn_bwd(dproj, dob, o_raw, proj, st_before, lb_table, norm_g, job=None):
    T = dob.shape[0]
    tb = min(512, T)
    nc = tb // HCH
    nb = T // tb

    def body(dp_in, dob_ref, o_ref, q_ref, fl_ref, v_ref, g_ref, stb_ref, lbt_ref, gn_ref,
             dp_ref, acc_ref, dst_s, dqin_s, dqa_s, dkin_s, dkd_s, dv_s, ddec_s):
        del dp_in

        @pl.when(pl.program_id(1) == 0)
        def _():
            dst_s[...] = jnp.zeros_like(dst_s)
            acc_ref[...] = jnp.zeros_like(acc_ref)

        row = lax.broadcasted_iota(jnp.int32, (tb, HD), 0) & (HCH - 1)
        gn = gn_ref[...]
        lbv = _sigmoid(lbt_ref[0:1, :] - lbt_ref[1:2, :])
        o = o_ref[...]
        r = lax.rsqrt(_mean(o * o) + EPS)
        on = o * r
        g = g_ref[...]
        sgm = _sigmoid(g)
        dob_v = dob_ref[...]
        dp_ref[3] = (dob_v * on * gn * (sgm * (1.0 + g * (1.0 - sgm)))).astype(BF16)
        do_n = dob_v * (g * sgm)
        acc_ref[1] += _rows8(do_n * on)
        dxn = do_n * gn
        do = (r * (dxn - on * _mean(dxn * on))).astype(BF16)
        s, f, a, a_mid, a_last = _hgrn_gates(fl_ref[...], lbv, row)
        k = 1.0 - f
        qs = q_ref[...] * QSCALE
        e_q = jnp.exp(a - a_mid)
        e_k = jnp.exp(a_mid - a)
        e_a = jnp.exp(a)
        e_l = jnp.exp(a_last - a)
        dec = jnp.exp(a_last)
        q_in = qs * e_q
        k_in = k * e_k
        q_a = qs * e_a
        k_d = k * e_l
        q_inb, k_inb, q_ab, k_db = (z.astype(BF16) for z in (q_in, k_in, q_a, k_d))
        vb = v_ref[...].astype(BF16)
        tri = (lax.broadcasted_iota(jnp.int32, (HCH, HCH), 0)
               >= lax.broadcasted_iota(jnp.int32, (HCH, HCH), 1))
        for c in reversed(range(nc)):
            sl = slice(HCH * c, HCH * (c + 1))
            stp = stb_ref[c]
            dst = dst_s[...]
            dstb = dst.astype(BF16)
            do_c = do[sl]
            dqa_s[sl, :] = _mm(do_c, stp.astype(BF16))
            dkd_s[sl, :] = _mm(vb[sl], dstb)
            ddec_s[sl, :] = jnp.broadcast_to(jnp.sum(dst * stp, axis=0, keepdims=True), (HCH, HD))
            sc = jnp.where(tri, _mm_nt(q_inb[sl], k_inb[sl]), 0.0).astype(BF16)
            dsc = jnp.where(tri, _mm_nt(do_c, vb[sl]), 0.0).astype(BF16)
            dv_s[sl, :] = _mm_nt(k_db[sl], dstb) + _mm_tn(sc, do_c)
            dqin_s[sl, :] = _mm(dsc, k_inb[sl])
            dkin_s[sl, :] = _mm_tn(dsc, q_inb[sl])
            d64 = dec[sl]
            dst_s[...] = dst * jnp.concatenate([d64, d64], axis=0) + _mm_tn(do_c, q_ab[sl])
        dq_in = dqin_s[...]
        dq_a = dqa_s[...]
        dk_in = dkin_s[...]
        dk_d = dkd_s[...]
        dp_ref[0] = ((dq_in * e_q + dq_a * e_a) * QSCALE).astype(BF16)
        dp_ref[2] = dv_s[...].astype(BF16)
        tq = dq_in * q_in
        tk = dk_in * k_in
        td = dk_d * k_d
        d_a = tq + dq_a * q_a - tk - td
        d_a = d_a + jnp.where(row == HCH // 2 - 1, _seg_sum(tk - tq), 0.0)
        d_a = d_a + jnp.where(row == HCH - 1, _seg_sum(td) + ddec_s[...] * dec, 0.0)
        dlf = _revcumsum64(d_a, row)
        df = dlf / f - (dk_in * e_k + dk_d * e_l)
        dp_ref[1] = (df * (1.0 - lbv) * s * (1.0 - s)).astype(BF16)
        acc_ref[0] += _rows8(df * (1.0 - s))

    def col(off):
        return pl.BlockSpec((tb, HD), lambda h, cb: (nb - 1 - cb, off * NH + h))

    hb = lambda h, cb: (nb - 1 - cb, h)
    return _call(
        body, name="hgrn_bwd", grid=(NH, nb), job=job,
        args=(dproj, dob, o_raw, proj, proj, proj, proj, st_before, lb_table, norm_g),
        in_specs=[ANY, pl.BlockSpec((tb, HD), hb), pl.BlockSpec((tb, HD), hb),
                  col(2), col(3), col(4), col(5),
                  pl.BlockSpec((None, nc, HD, HD), lambda h, cb: (h, nb - 1 - cb, 0, 0)),
                  pl.BlockSpec((2, HD), lambda h, cb: (0, h)), pl.BlockSpec((1, HD), lambda h, cb: (0, h))],
        out_specs=[pl.BlockSpec((4, tb, HD), lambda h, cb: (0, nb - 1 - cb, h)),
                   pl.BlockSpec((2, 8, HD), lambda h, cb: (0, 0, h))],
        out_shape=[SDS(dproj.shape, BF16), SDS((2, 8, D), F32)],
        scratch_shapes=[pltpu.VMEM((HD, HD), F32)] + [pltpu.VMEM((tb, HD), F32)] * 6,
        aliases={0: 0})


def _gmlp_bwd(dproj, da, proj, ln_g, ln_b, wm, wm_t, b_t):
    T = da.shape[0]
    tm = min(256, T)

    def body(dp_in, da_ref, u_ref, v_ref, lg_ref, lb_ref, wm_ref, wmt_ref, bt_ref,
             dp_ref, acc_ref, dws_ref, dmix_ref, du_s, dvn_s):
        del dp_in

        @pl.when(pl.program_id(0) == 0)
        def _():
            acc_ref[...] = jnp.zeros_like(acc_ref)
            dws_ref[...] = jnp.zeros_like(dws_ref)
            dmix_ref[...] = jnp.zeros_like(dmix_ref)

        u = u_ref[...]
        v = v_ref[...]
        lg = lg_ref[...]
        gu, t_u = _gelu(u)
        gv, t_v = _gelu(v)
        vhat, rs = _layer_norm_stats(gv)
        vnb = (vhat * lg + lb_ref[...]).astype(BF16)
        da_v = da_ref[...]
        for ch in range(tm // GCH):
            rows = slice(GCH * ch, GCH * (ch + 1))
            for g in range(NG):
                cols = slice(128 * g, 128 * (g + 1))
                vng = vnb[rows, cols]
                mixed = _mm(wm_ref[g], vng) + bt_ref[:, g:g + 1]
                dag = da_v[rows, cols]
                dmx = dag * gu[rows, cols]
                du_s[rows, cols] = dag * mixed
                dmxb = dmx.astype(BF16)
                dws_ref[:, cols] += _mm_nt(dmxb, vng)
                dmix_ref[:, cols] += dmx
                dvn_s[rows, cols] = _mm(wmt_ref[g], dmxb)
        dp_ref[0] = (du_s[...] * _gelu_grad(u, t_u)).astype(BF16)
        dvn = dvn_s[...]
        acc_ref[0] += _rows8(dvn * vhat)
        acc_ref[1] += _rows8(dvn)
        dvh = dvn * lg
        dgv = rs * (dvh - _mean(dvh) - vhat * _mean(dvh * vhat))
        dp_ref[1] = (dgv * _gelu_grad(v, t_v)).astype(BF16)

    row = lambda i: (0, 0)
    w3 = lambda i: (0, 0, 0)
    return pl.pallas_call(
        body, name="gmlp_bwd", grid=(T // tm,),
        in_specs=[ANY, pl.BlockSpec((tm, D), lambda i: (i, 0)),
                  pl.BlockSpec((tm, D), lambda i: (i, 0)), pl.BlockSpec((tm, D), lambda i: (i, 1)),
                  pl.BlockSpec((1, D), row), pl.BlockSpec((1, D), row),
                  pl.BlockSpec((NG, GCH, GCH), w3), pl.BlockSpec((NG, GCH, GCH), w3),
                  pl.BlockSpec((GCH, NG), row)],
        out_specs=[pl.BlockSpec((2, tm, D), lambda i: (2, i, 0)),
                   pl.BlockSpec((2, 8, D), w3), pl.BlockSpec((GCH, D), row), pl.BlockSpec((GCH, D), row)],
        out_shape=[SDS(dproj.shape, BF16), SDS((2, 8, D), F32), SDS((GCH, D), F32), SDS((GCH, D), F32)],
        scratch_shapes=[pltpu.VMEM((tm, D), F32), pltpu.VMEM((tm, D), F32)],
        input_output_aliases={0: 0},
        compiler_params=_cparams(),
    )(dproj, da, proj, proj, ln_g, ln_b, wm, wm_t, b_t)


def _proj_bwd(dproj, w_in4, x, dx1, g_mix, job=None):
    T = x.shape[0]
    tm = min(256, T)
    order = (2, 3, 4, 5, 0, 1, 6, 7)

    def body(dp_ref, w_ref, x_ref, dx1_ref, g_ref, gx_ref, acc_ref):
        @pl.when(pl.program_id(0) == 0)
        def _():
            acc_ref[...] = jnp.zeros_like(acc_ref)

        dh = None
        for m, og in enumerate(order):
            part = _mm_nt(dp_ref[m], w_ref[og // 2, :, D * (og % 2):D * (og % 2 + 1)])
            dh = part if dh is None else dh + part
        xv = x_ref[...]
        r = lax.rsqrt(_mean(xv * xv) + EPS)
        xn = xv * r
        acc_ref[...] += _rows8(dh * xn)
        dxn = dh * g_ref[...]
        gx_ref[...] = dx1_ref[...] + r * (dxn - xn * _mean(dxn * xn))

    t = lambda i: (i, 0)
    return _call(
        body, name="proj_bwd", grid=(T // tm,),
        in_specs=[pl.BlockSpec((NIN, tm, D), lambda i: (0, i, 0)),
                  pl.BlockSpec((NCHIP, D, 2 * D), lambda i: (0, 0, 0), pipeline_mode=pl.Buffered(1)),
                  pl.BlockSpec((tm, D), t), pl.BlockSpec((tm, D), t), pl.BlockSpec((1, D), lambda i: (0, 0))],
        out_specs=[pl.BlockSpec((tm, D), t), pl.BlockSpec((8, D), lambda i: (0, 0))],
        out_shape=[SDS((T, D), F32), SDS((8, D), F32)],
        args=(dproj, w_in4, x, dx1, g_mix), job=job)


def _dw_call(name, a, b, a_spec, b_spec, o_spec, out_shape, nblk, tt, job=None):
    T = a.shape[-2]

    def body(a_ref, b_ref, o_ref):
        @pl.when(pl.program_id(1) == 0)
        def _():
            o_ref[...] = jnp.zeros_like(o_ref)
        o_ref[...] += _mm_tn(a_ref[...], b_ref[...])

    (out,), job_out = _call(
        body, name=name, grid=(nblk, T // tt), in_specs=[a_spec, b_spec], out_specs=[o_spec],
        out_shape=[out_shape], args=(a, b), job=job)
    return out, job_out


def _dw_in(hb, dproj, job=None):
    tt = min(512, hb.shape[0])
    return _dw_call(
        "dw_in", hb, dproj,
        pl.BlockSpec((tt, D), lambda m, t: (t, 0)),
        pl.BlockSpec((None, tt, D), lambda m, t: (m, t, 0)),
        pl.BlockSpec((None, D, D), lambda m, t: (_orig_group(m) // 2, 0, _orig_group(m) % 2)),
        SDS((NCHIP, D, 2 * D), F32), NIN, tt, job)


def _dw_gate_up(h2b, dgu4, job=None):
    tt = min(512, h2b.shape[0])
    return _dw_call(
        "dw_gate_up", h2b, dgu4,
        pl.BlockSpec((tt, D), lambda k, t: (t, 0)),
        pl.BlockSpec((None, tt, FFS), lambda k, t: (k, t, 0)),
        pl.BlockSpec((None, D, FFS), lambda k, t: (k, 0, 0)),
        SDS((NCHIP, D, FFS), F32), NCHIP, tt, job)


def _dw_down(act, dx2b, job=None):
    tt = min(512, act.shape[0])
    g, job_out = _dw_call(
        "dw_down", act, dx2b,
        pl.BlockSpec((tt, FFS), lambda k, t: (t, k)),
        pl.BlockSpec((tt, D), lambda k, t: (t, 0)),
        pl.BlockSpec((FFS, D), lambda k, t: (k, 0)),
        SDS((FF, D), F32), 2, tt, job)
    return g.reshape(NCHIP, FF // NCHIP, D), job_out


def _dw_square(name, a, b, job=None):
    tt = min(512, a.shape[0])
    g, job_out = _dw_call(
        name, a, b,
        pl.BlockSpec((tt, D), lambda k, t: (t, 0)), pl.BlockSpec((tt, D), lambda k, t: (t, 0)),
        pl.BlockSpec((D, D), lambda k, t: (0, 0)), SDS((D, D), F32), 1, tt, job)
    return g.reshape(NCHIP, D // NCHIP, D), job_out


def _place():
    x, y, c = lax.axis_index("x"), lax.axis_index("y"), lax.axis_index("c")
    return x, y, c, 2 * x + y


def _chip_at(x, y, s):
    return x ^ (s >> 1), y ^ (s & 1)


class _Job:
    def __init__(self, ins, out_shapes, sems, start, finish, aliases=None):
        self.ins, self.out_shapes, self.sems = list(ins), list(out_shapes), list(sems)
        self.start, self.finish, self.aliases = start, finish, dict(aliases or {})


def _join_jobs(*jobs):
    def cut(refs, sizes):
        out, at = [], 0
        for n in sizes:
            out.append(refs[at:at + n])
            at += n
        return out

    ni = [len(j.ins) for j in jobs]
    no = [len(j.out_shapes) for j in jobs]
    ns = [len(j.sems) for j in jobs]

    def run(which):
        def go(ins, outs, sems):
            for j, a, b, c in zip(jobs, cut(ins, ni), cut(outs, no), cut(sems, ns)):
                getattr(j, which)(a, b, c)
        return go

    aliases = {}
    for k, j in enumerate(jobs):
        for a, b in j.aliases.items():
            aliases[sum(ni[:k]) + a] = sum(no[:k]) + b
    return _Job([a for j in jobs for a in j.ins], [o for j in jobs for o in j.out_shapes],
                [s for j in jobs for s in j.sems], run("start"), run("finish"), aliases)


def _call(body, *, name, grid, in_specs, out_specs, out_shape, args, scratch_shapes=(), aliases=None, job=None):
    n_in, n_out, n_scr = len(in_specs), len(out_specs), len(scratch_shapes)
    aliases = dict(aliases or {})
    if job is None:
        res = pl.pallas_call(
            body, name=name, grid=grid, in_specs=list(in_specs), out_specs=list(out_specs),
            out_shape=list(out_shape), scratch_shapes=list(scratch_shapes), input_output_aliases=aliases,
            compiler_params=_cparams())(*args)
        return list(res), []
    ji, jo = len(job.ins), len(job.out_shapes)

    def wrapped(*refs):
        ins, jin = refs[:n_in], refs[n_in:n_in + ji]
        o0 = n_in + ji
        outs, jout = refs[o0:o0 + n_out], refs[o0 + n_out:o0 + n_out + jo]
        s0 = o0 + n_out + jo
        scr, jsem = refs[s0:s0 + n_scr], refs[s0 + n_scr:]
        ids = [pl.program_id(a) for a in range(len(grid))]
        first = functools.reduce(jnp.logical_and, [i == 0 for i in ids])
        last = functools.reduce(jnp.logical_and, [i == g - 1 for i, g in zip(ids, grid)])

        @pl.when(first)
        def _():
            job.start(jin, jout, jsem)

        body(*ins, *outs, *scr)

        @pl.when(last)
        def _():
            job.finish(jin, jout, jsem)

    for a, b in job.aliases.items():
        aliases[n_in + a] = n_out + b
    res = pl.pallas_call(
        wrapped, name=name, grid=grid, in_specs=list(in_specs) + [ANY] * ji,
        out_specs=list(out_specs) + [ANY] * jo, out_shape=list(out_shape) + job.out_shapes,
        scratch_shapes=list(scratch_shapes) + job.sems, input_output_aliases=aliases,
        compiler_params=_cparams(has_side_effects=True))(*args, *job.ins)
    return list(res[:n_out]), list(res[n_out:])


def _run_job(job, name):
    ji, jo = len(job.ins), len(job.out_shapes)

    def body(*refs):
        jin, jout, jsem = refs[:ji], refs[ji:ji + jo], refs[ji + jo:]
        job.start(jin, jout, jsem)
        job.finish(jin, jout, jsem)

    return list(pl.pallas_call(
        body, name=name, in_specs=[ANY] * ji, out_specs=[ANY] * jo, out_shape=job.out_shapes,
        scratch_shapes=job.sems, input_output_aliases=job.aliases,
        compiler_params=pltpu.CompilerParams(has_side_effects=True))(*job.ins))


def _cast_shard(name, place, w):
    rows, cols = w.shape
    tr = 352 if rows % 352 == 0 else 256

    def body(pc_ref, w_ref, o_ref):
        del pc_ref
        o_ref[...] = w_ref[...].astype(BF16)

    return pl.pallas_call(
        body, name=name,
        grid_spec=pltpu.PrefetchScalarGridSpec(
            num_scalar_prefetch=1, grid=(rows // tr,),
            in_specs=[pl.BlockSpec((tr, cols), lambda i, pc: (i, 0))],
            out_specs=pl.BlockSpec((None, tr, cols), lambda i, pc: (pc[0], i, 0))),
        out_shape=SDS((NCHIP, rows, cols), BF16),
        compiler_params=_cparams(),
    )(place, w)


def _sibling_copy(ref, send_sem, recv_sem):
    x, y, c, _ = _place()
    return pltpu.make_async_remote_copy(src_ref=ref, dst_ref=ref, send_sem=send_sem, recv_sem=recv_sem,
                                        device_id=(x, y, 1 - c), device_id_type=MESH)


def _half_rows(arr, slot, core):
    half = arr.shape[1] // 2
    return arr.at[slot, pl.ds(pl.multiple_of(core * half, 16), half)]


def _gather_start(arrs, sems):
    send_sem, recv_sem = sems[0], sems[1]
    x, y, c, j = _place()
    for w, arr in enumerate(arrs):
        mine = _half_rows(arr, j, c)
        for s in range(1, NCHIP):
            cx, cy = _chip_at(x, y, s)
            pltpu.make_async_remote_copy(
                src_ref=mine, dst_ref=mine, send_sem=send_sem.at[w, s - 1], recv_sem=recv_sem.at[w, s - 1],
                device_id=(cx, cy, c), device_id_type=MESH).start()


def _gather_land(arrs, sems, s):
    send_sem, recv_sem, fsend_sem, frecv_sem = sems
    _, _, c, j = _place()
    for w, arr in enumerate(arrs):
        landed = _half_rows(arr, j ^ s, c)
        _sibling_copy(landed, send_sem.at[w, s - 1], recv_sem.at[w, s - 1]).wait_recv()
        _sibling_copy(landed, fsend_sem.at[w, s - 1], frecv_sem.at[w, s - 1]).start()
    for w, arr in enumerate(arrs):
        theirs = _half_rows(arr, j ^ s, 1 - c)
        _sibling_copy(theirs, fsend_sem.at[w, s - 1], frecv_sem.at[w, s - 1]).wait_recv()


def _gather_drain(arrs, sems):
    send_sem, recv_sem, fsend_sem, frecv_sem = sems
    _, _, c, j = _place()
    for s in range(1, NCHIP):
        for w, arr in enumerate(arrs):
            _sibling_copy(_half_rows(arr, j, c), send_sem.at[w, s - 1], recv_sem.at[w, s - 1]).wait_send()
            _sibling_copy(_half_rows(arr, j ^ s, c), fsend_sem.at[w, s - 1], frecv_sem.at[w, s - 1]).wait_send()


def _gather_finish(arrs, sems):
    for s in range(1, NCHIP):
        _gather_land(arrs, sems, s)
    _gather_drain(arrs, sems)


def _exchange_job(arrs, out_shapes, n, copies):
    def start(ins, outs, sems):
        for cp in copies(ins, outs, sems[0], sems[1]):
            cp.start()

    def finish(ins, outs, sems):
        for cp in copies(ins, outs, sems[0], sems[1]):
            cp.wait()

    return _Job(arrs, out_shapes, [pltpu.SemaphoreType.DMA((n,))] * 2, start, finish)


def _pair_exchange_job(grads):
    def copies(ins, outs, send_sem, recv_sem):
        x, y, c, _ = _place()
        res = []
        for w in range(len(grads)):
            half = ins[w].shape[1] // 2
            theirs = pl.ds(pl.multiple_of((1 - c) * half, 8), half)
            res.append(pltpu.make_async_remote_copy(
                src_ref=ins[w].at[:, theirs, :], dst_ref=outs[w], send_sem=send_sem.at[w],
                recv_sem=recv_sem.at[w], device_id=(x, y, 1 - c), device_id_type=MESH))
        return res

    return _exchange_job(grads, [SDS((NCHIP, g.shape[1] // 2, g.shape[2]), F32) for g in grads],
                         len(grads), copies)


def _row_tile(rows):
    return 176 if rows % 176 == 0 and rows % 128 else 128


def _pair_sum(name, place, g, sib):
    half, cols = sib.shape[1], sib.shape[2]
    tr = _row_tile(half)
    nt = half // tr

    def body(pc_ref, g_ref, s_ref, own_ref, out_ref):
        del pc_ref
        v = g_ref[...] + s_ref[...]
        out_ref[...] = v.astype(BF16)

        @pl.when(pl.program_id(1) == 0)
        def _():
            own_ref[...] = v

    return pl.pallas_call(
        body, name=name,
        grid_spec=pltpu.PrefetchScalarGridSpec(
            num_scalar_prefetch=1, grid=(nt, NCHIP),
            in_specs=[pl.BlockSpec((None, tr, cols), lambda i, s, pc: (pc[0] ^ s, pc[1] * nt + i, 0)),
                      pl.BlockSpec((None, tr, cols), lambda i, s, pc: (pc[0] ^ s, i, 0))],
            out_specs=[pl.BlockSpec((tr, cols), lambda i, s, pc: (i, 0)),
                       pl.BlockSpec((None, tr, cols), lambda i, s, pc: (s, i, 0))]),
        out_shape=[SDS((half, cols), F32), SDS((NCHIP, half, cols), BF16)],
        compiler_params=_cparams(),
    )(place, g, sib)


def _chip_exchange_job(parts):
    def copies(ins, outs, send_sem, recv_sem):
        x, y, c, _ = _place()
        res = []
        for w in range(len(parts)):
            for s in range(1, NCHIP):
                cx, cy = _chip_at(x, y, s)
                k = w * (NCHIP - 1) + s - 1
                res.append(pltpu.make_async_remote_copy(
                    src_ref=ins[w].at[s], dst_ref=outs[w].at[s - 1], send_sem=send_sem.at[k],
                    recv_sem=recv_sem.at[k], device_id=(cx, cy, c), device_id_type=MESH))
        return res

    return _exchange_job(parts, [SDS((NCHIP - 1,) + p.shape[1:], BF16) for p in parts],
                         len(parts) * (NCHIP - 1), copies)


def _chip_sum(name, own, rem):
    half, cols = own.shape
    tr = _row_tile(half)

    def body(own_ref, rem_ref, out_ref):
        out_ref[...] = ((own_ref[...] + rem_ref[0].astype(F32)) + rem_ref[1].astype(F32)) + rem_ref[2].astype(F32)

    return pl.pallas_call(
        body, name=name, grid=(half // tr,),
        in_specs=[pl.BlockSpec((tr, cols), lambda i: (i, 0)),
                  pl.BlockSpec((NCHIP - 1, tr, cols), lambda i: (0, i, 0))],
        out_specs=pl.BlockSpec((tr, cols), lambda i: (i, 0)),
        out_shape=SDS((half, cols), F32),
        compiler_params=_cparams(),
    )(own, rem)


def _share_halves_job(halves):
    def copies(ins, outs, send_sem, recv_sem):
        x, y, c, _ = _place()
        return [pltpu.make_async_remote_copy(
            src_ref=ins[w], dst_ref=outs[w], send_sem=send_sem.at[w], recv_sem=recv_sem.at[w],
            device_id=(x, y, 1 - c), device_id_type=MESH) for w in range(len(halves))]

    return _exchange_job(halves, [SDS(h.shape, F32) for h in halves], len(halves), copies)


def _adamw_math(w, g, m, v):
    m = B1 * m + (1.0 - B1) * g
    v = B2 * v + (1.0 - B2) * (g * g)
    m_hat = m / (1.0 - B1 ** STEP)
    v_hat = v / (1.0 - B2 ** STEP)
    delta = -LR * (m_hat / (jnp.sqrt(v_hat) + AEPS) + WD * w)
    return delta, m, v


def _adamw(name, place, w, own, sib, m, v):
    rows, cols = w.shape
    half = rows // 2
    tr = 352 if half % 352 == 0 else min(256, half)
    nt = half // tr

    def body(pc_ref, w_ref, own_ref, sib_ref, m_ref, v_ref, g_ref, d_ref, mo_ref, vo_ref):
        g = jnp.where(pl.program_id(0) == pc_ref[1], own_ref[...], sib_ref[...])
        d, mn, vn = _adamw_math(w_ref[...], g, m_ref[...], v_ref[...])
        g_ref[...] = g
        d_ref[...] = d
        mo_ref[...] = mn
        vo_ref[...] = vn

    full = pl.BlockSpec((tr, cols), lambda h, i, pc: (h * nt + i, 0))
    part = pl.BlockSpec((tr, cols), lambda h, i, pc: (i, 0))
    return pl.pallas_call(
        body, name=name,
        grid_spec=pltpu.PrefetchScalarGridSpec(
            num_scalar_prefetch=1, grid=(2, nt),
            in_specs=[full, part, part, full, full], out_specs=[full] * 4),
        out_shape=[SDS((rows, cols), F32)] * 4,
        compiler_params=_cparams(),
    )(place, w, own, sib, m, v)


def _small_allreduce_adamw(sp, w, m, v):
    shape = sp.shape

    def body(sp_ref, w_ref, m_ref, v_ref, g_ref, d_ref, mo_ref, vo_ref,
             sib_s, pair_s, chip_s, send_sem, recv_sem):
        x, y, c, j = _place()
        cp = pltpu.make_async_remote_copy(
            src_ref=sp_ref, dst_ref=sib_s, send_sem=send_sem.at[0], recv_sem=recv_sem.at[0],
            device_id=(x, y, 1 - c), device_id_type=MESH)
        cp.start()
        cp.wait()
        pair_s[...] = sp_ref[...] + sib_s[...]
        cps = []
        for s in range(1, NCHIP):
            cx, cy = _chip_at(x, y, s)
            cp = pltpu.make_async_remote_copy(
                src_ref=pair_s, dst_ref=chip_s.at[s], send_sem=send_sem.at[s], recv_sem=recv_sem.at[s],
                device_id=(cx, cy, c), device_id_type=MESH)
            cp.start()
            cps.append(cp)
        chip_s[0] = pair_s[...]
        for cp in cps:
            cp.wait()
        tot = chip_s[j]
        for k in range(1, NCHIP):
            tot = tot + chip_s[k ^ j]
        g_ref[...] = tot
        d, mn, vn = _adamw_math(w_ref[...], tot, m_ref[...], v_ref[...])
        d_ref[...] = d
        mo_ref[...] = mn
        vo_ref[...] = vn

    vm = pl.BlockSpec(memory_space=pltpu.VMEM)
    return pl.pallas_call(
        body, name="small_allreduce_adamw",
        in_specs=[vm] * 4, out_specs=[vm] * 4, out_shape=[SDS(shape, F32)] * 4,
        scratch_shapes=[pltpu.VMEM(shape, F32), pltpu.VMEM(shape, F32), pltpu.VMEM((NCHIP,) + shape, F32),
                        pltpu.SemaphoreType.DMA((NCHIP,)), pltpu.SemaphoreType.DMA((NCHIP,))],
        compiler_params=pltpu.CompilerParams(has_side_effects=True),
    )(sp, w, m, v)


def _pack_small(first, mix, ln_g, ln_b, b_s, lbt, hn, ffn, fin, w_s):
    rows = [first.reshape(1, D), mix.reshape(1, D), ln_g.reshape(1, D), ln_b.reshape(1, D),
            b_s.reshape(1, D), lbt.reshape(2, D), hn.reshape(1, D), ffn.reshape(1, D), fin.reshape(1, D),
            jnp.zeros((6, D), F32)]
    return jnp.concatenate(rows + [w_s.reshape(NG, GCH, GCH).transpose(1, 0, 2).reshape(GCH, D)], axis=0)


def _unpack_small(p):
    w_s = p[16:].reshape(GCH, NG, GCH).transpose(1, 0, 2).reshape(1, NG, GCH, GCH)
    return dict(norm_mix_g=p[1:2], gmlp_ln_g=p[2:3], gmlp_ln_b=p[3:4], gmlp_b_s=p[4].reshape(1, NG, GCH),
                hgrn_lb_table=p[5:7], hgrn_norm_g=p[7:8], norm_ffn_g=p[8:9], norm_final_g=p[9],
                gmlp_w_s=w_s)


SMALL = ("norm_mix_g", "gmlp_ln_g", "gmlp_ln_b", "gmlp_w_s", "gmlp_b_s", "hgrn_lb_table", "hgrn_norm_g",
         "norm_ffn_g", "norm_final_g")
BIG = ("w_in", "w_gate_up", "w_branch_a", "w_branch_b", "w_out", "w_down")
ORDER = ("norm_mix_g", "w_in", "gmlp_ln_g", "gmlp_ln_b", "gmlp_w_s", "gmlp_b_s", "hgrn_lb_table",
         "hgrn_norm_g", "w_branch_a", "w_branch_b", "w_out", "norm_ffn_g", "w_gate_up", "w_down",
         "norm_final_g")


def kernel(x, norm_mix_g, w_in, gmlp_ln_g, gmlp_ln_b, gmlp_w_s, gmlp_b_s, hgrn_lb_table, hgrn_norm_g, w_branch_a, w_branch_b, w_out, norm_ffn_g, w_gate_up, w_down, norm_final_g, loss_target, m_norm_mix_g, m_w_in, m_gmlp_ln_g, m_gmlp_ln_b, m_gmlp_w_s, m_gmlp_b_s, m_hgrn_lb_table, m_hgrn_norm_g, m_w_branch_a, m_w_branch_b, m_w_out, m_norm_ffn_g, m_w_gate_up, m_w_down, m_norm_final_g, v_norm_mix_g, v_w_in, v_gmlp_ln_g, v_gmlp_ln_b, v_gmlp_w_s, v_gmlp_b_s, v_hgrn_lb_table, v_hgrn_norm_g, v_w_branch_a, v_w_branch_b, v_w_out, v_norm_ffn_g, v_w_gate_up, v_w_down, v_norm_final_g):
    args = dict(locals())
    T = x.shape[1]
    xs = x.reshape(T, D)
    target = loss_target.reshape(T, D)
    big = {n: args[n].reshape(args[n].shape[1:]) for n in BIG}
    big_m = {n: args["m_" + n].reshape(args[n].shape[1:]) for n in BIG}
    big_v = {n: args["v_" + n].reshape(args[n].shape[1:]) for n in BIG}

    x_i, y_i, c_i = lax.axis_index("x"), lax.axis_index("y"), lax.axis_index("c")
    place = jnp.stack([2 * x_i + y_i, c_i]).astype(jnp.int32)
    cast = {n: _cast_shard("cast_" + n, place, big[n]) for n in BIG}
    tril = jnp.tril(jnp.ones((GCH, GCH), bool))
    wm = jnp.where(tril, gmlp_w_s[0], 0.0).astype(BF16)
    wm_t = jnp.swapaxes(wm, 1, 2)
    b_t = gmlp_b_s[0].T

    (proj, hb), w_in4, (w_gu4, w_a4, w_b4, w_out4, w_down4) = _proj_fwd(
        place, xs, norm_mix_g, cast["w_in"], [cast[n] for n in BIG[1:]])
    w_a, w_b, w_o = (w.reshape(D, D) for w in (w_a4, w_b4, w_out4))
    w_dn = w_down4.reshape(FF, D)
    ab, y_a = _gmlp_fwd(proj, gmlp_ln_g, gmlp_ln_b, wm, b_t, w_a)
    o_raw, obb, st_before = _hgrn_fwd(proj, hgrn_lb_table, hgrn_norm_g)
    y_b, mgb, x1 = _merge_fwd(xs, y_a, obb, proj, w_b, w_o)
    act, dx2b, h2b, dgu4, dx1, dx1b, acc_ffn = _ffn_fwd_bwd(
        x1, target, norm_ffn_g, norm_final_g.reshape(1, D), w_gu4, w_dn)

    grads, owns, parts, halves, sibh = {}, {}, {}, {}, {}

    def pair_sums(names, sibs):
        for n, s in zip(names, sibs):
            owns[n], parts[n] = _pair_sum("rs_pair_sum_" + n, place, grads[n], s)

    def chip_sums(names, got):
        for n, r in zip(names, got):
            halves[n] = _chip_sum("rs_chip_sum_" + n, owns[n], r)

    ffn, mix = ("w_gate_up", "w_down"), ("w_branch_a", "w_branch_b", "w_out")
    grads["w_gate_up"], _ = _dw_gate_up(h2b, dgu4)
    grads["w_down"], _ = _dw_down(act, dx2b)
    (dya, dyb, da, dob, dproj), got = _merge_bwd(
        dx1b, y_a, y_b, proj, w_o, w_a, w_b, job=_pair_exchange_job([grads[n] for n in ffn]))
    pair_sums(ffn, got)
    grads["w_branch_a"], _ = _dw_square("dw_branch_a", ab, dya)
    grads["w_branch_b"], _ = _dw_square("dw_branch_b", obb, dyb)
    grads["w_out"], _ = _dw_square("dw_out", mgb, dx1b)
    (dproj, acc_hgrn), got = _hgrn_bwd(
        dproj, dob, o_raw, proj, st_before, hgrn_lb_table, hgrn_norm_g,
        job=_join_jobs(_chip_exchange_job([parts[n] for n in ffn]), _pair_exchange_job([grads[n] for n in mix])))
    chip_sums(ffn, got[:2])
    pair_sums(mix, got[2:])
    dproj, acc_ln, dws, dmix = _gmlp_bwd(dproj, da, proj, gmlp_ln_g, gmlp_ln_b, wm, wm_t, b_t)
    (grad_x, acc_mix), got = _proj_bwd(
        dproj, w_in4, xs, dx1, norm_mix_g,
        job=_join_jobs(_share_halves_job([halves[n] for n in ffn]), _chip_exchange_job([parts[n] for n in mix])))
    sibh.update(zip(ffn, got[:2]))
    chip_sums(mix, got[2:])
    grads["w_in"], got = _dw_in(hb, dproj, job=_share_halves_job([halves[n] for n in mix]))
    sibh.update(zip(mix, got))
    pair_sums(("w_in",), _run_job(_pair_exchange_job([grads["w_in"]]), "rs_pair_exchange_w_in"))
    chip_sums(("w_in",), _run_job(_chip_exchange_job([parts["w_in"]]), "rs_chip_exchange_w_in"))
    (sibh["w_in"],) = _run_job(_share_halves_job([halves["w_in"]]), "rs_share_halves_w_in")
    out = {}
    for n in BIG:
        g, d, mn, vn = _adamw("adamw_" + n, place, big[n], halves[n], sibh[n], big_m[n], big_v[n])
        shp = args[n].shape
        out[n] = (g.reshape(shp), d.reshape(shp), mn.reshape(shp), vn.reshape(shp))

    lbv = jax.nn.sigmoid(hgrn_lb_table[0] - hgrn_lb_table[1])
    d_t0 = jnp.sum(acc_hgrn[0], axis=0) * lbv * (1.0 - lbv)
    loss_row = jnp.zeros((D,), F32).at[0].set(jnp.sum(acc_ffn[0]))
    dws_m = jnp.where(tril[:, None, :], dws.reshape(GCH, NG, GCH), 0.0).transpose(1, 0, 2)
    db_s = jnp.sum(dmix.reshape(GCH, NG, GCH), axis=-1).T
    sp = _pack_small(loss_row, jnp.sum(acc_mix, 0), jnp.sum(acc_ln[0], 0), jnp.sum(acc_ln[1], 0), db_s,
                     jnp.stack([d_t0, -d_t0]), jnp.sum(acc_hgrn[1], 0), jnp.sum(acc_ffn[2], 0),
                     jnp.sum(acc_ffn[1], 0), dws_m)
    zero = jnp.zeros((D,), F32)

    def pack(prefix):
        a = lambda n: args[prefix + n]
        return _pack_small(zero, a("norm_mix_g"), a("gmlp_ln_g"), a("gmlp_ln_b"), a("gmlp_b_s"),
                           a("hgrn_lb_table"), a("hgrn_norm_g"), a("norm_ffn_g"), a("norm_final_g"),
                           a("gmlp_w_s"))

    packed = _small_allreduce_adamw(sp, pack(""), pack("m_"), pack("v_"))
    loss = packed[0][0, 0]
    small = [_unpack_small(p) for p in packed]
    for n in SMALL:
        out[n] = tuple(s[n] for s in small)
    return (loss, grad_x.reshape(x.shape), *[out[n][0] for n in ORDER], *[out[n][1] for n in ORDER],
            *[out[n][2] for n in ORDER], *[out[n][3] for n in ORDER])
```

```python
import functools
import math

import jax
import jax.numpy as jnp
from jax import lax
from jax.experimental import pallas as pl
from jax.experimental.pallas import tpu as pltpu

F32 = jnp.float32
BF16 = jnp.bfloat16
SDS = jax.ShapeDtypeStruct
MESH = pl.DeviceIdType.MESH
ANY = pl.BlockSpec(memory_space=pl.ANY)

D = 1024
NIN = 8
NG = 8
GCH = 128
NH = 8
HD = 128
HCH = 64
HGRN_HB = 2
HW = HGRN_HB * HD
DW_TOKENS = 2048
FF = 2816
FFS = 1408
NCHIP = 4
EPS = 1e-6
QSCALE = HD ** -0.5
GELU_C0 = math.sqrt(2.0 / math.pi)
GELU_C1 = 0.044715
LR, B1, B2, AEPS, WD, STEP = 0.001, 0.9, 0.999, 1e-08, 0.01, 10
VMEM_LIMIT_V7X = 56 * 1024 * 1024
SP_ROWS = 144


def _cparams(**kw):
    return pltpu.CompilerParams(vmem_limit_bytes=VMEM_LIMIT_V7X, **kw)


def _mm(a, b):
    return jnp.dot(a, b, preferred_element_type=F32)


def _mm_nt(a, b):
    return lax.dot_general(a, b, (((1,), (1,)), ((), ())), preferred_element_type=F32)


def _mm_tn(a, b):
    return lax.dot_general(a, b, (((0,), (0,)), ((), ())), preferred_element_type=F32)


def _rows8(x):
    r, c = x.shape
    return jnp.sum(x.reshape(r // 8, 8, c), axis=0)


def _mean(x):
    return jnp.mean(x, axis=-1, keepdims=True)


def _sigmoid(x):
    return 1.0 / (1.0 + jnp.exp(-x))


def _gelu(x):
    t = jnp.tanh(GELU_C0 * (x + GELU_C1 * x * x * x))
    return 0.5 * x * (1.0 + t), t


def _gelu_grad(x, t):
    return 0.5 * (1.0 + t) + 0.5 * x * (1.0 - t * t) * (GELU_C0 * (1.0 + 3.0 * GELU_C1 * x * x))


def _orig_group(m):
    return jnp.where(m < 6, (m + 2) % 6, m)


def _proj_fwd(place, x, g_mix, w_in4, later):
    T = x.shape[0]
    tm = min(512, T)
    ni = T // tm
    n = len(later)

    def body(pc_ref, x_ref, g_ref, *rest):
        proj_ref, h_ref, w_all = rest[1 + n:4 + n]
        gathered = rest[4 + n:4 + 2 * n]
        hs, wbuf, wsem = rest[4 + 2 * n:7 + 2 * n]
        w_sems, later_sems = rest[7 + 2 * n:11 + 2 * n], rest[11 + 2 * n:]
        jp, i = pl.program_id(0), pl.program_id(1)
        s, blk = jp // 2, jp % 2

        @pl.when((jp == 0) & (i == 0))
        def _():
            _gather_start([w_all], w_sems)
            _gather_start(gathered, later_sems)

        @pl.when(jp == 0)
        def _():
            xv = x_ref[...]
            r = lax.rsqrt(_mean(xv * xv) + EPS)
            hb = (xv * r * g_ref[...]).astype(BF16)
            hs[i] = hb
            h_ref[...] = hb

        for dist in range(1, NCHIP):
            @pl.when((jp == 2 * dist) & (i == 0))
            def _():
                _gather_land([w_all], w_sems, dist)

        @pl.when(i == 0)
        def _():
            cols = pl.ds(pl.multiple_of(blk * D, 128), D)
            cp = pltpu.make_async_copy(w_all.at[pc_ref[0] ^ s, :, cols], wbuf, wsem)
            cp.start()
            cp.wait()

        proj_ref[...] = _mm(hs[i], wbuf[...])

        @pl.when((jp == NIN - 1) & (i == ni - 1))
        def _():
            _gather_drain([w_all], w_sems)
            _gather_finish(gathered, later_sems)

    tile = lambda jp, i, pc: (jnp.where(jp == 0, i, ni - 1), 0)
    sem4 = lambda k: [pltpu.SemaphoreType.DMA((k, NCHIP - 1))] * 4
    res = pl.pallas_call(
        body, name="proj_fwd",
        grid_spec=pltpu.PrefetchScalarGridSpec(
            num_scalar_prefetch=1, grid=(NIN, ni),
            in_specs=[pl.BlockSpec((tm, D), tile), pl.BlockSpec((1, D), lambda jp, i, pc: (0, 0))] + [ANY] * (1 + n),
            out_specs=[pl.BlockSpec((tm, D), lambda jp, i, pc: (i, 2 * (pc[0] ^ (jp // 2)) + jp % 2)),
                       pl.BlockSpec((tm, D), tile)] + [ANY] * (1 + n),
            scratch_shapes=[pltpu.VMEM((ni, tm, D), BF16), pltpu.VMEM((D, D), BF16), pltpu.SemaphoreType.DMA]
            + sem4(1) + sem4(n)),
        out_shape=[SDS((T, NIN * D), F32), SDS((T, D), BF16), SDS(w_in4.shape, BF16)]
        + [SDS(a.shape, a.dtype) for a in later],
        input_output_aliases={3 + k: 2 + k for k in range(1 + n)},
        compiler_params=_cparams(has_side_effects=True),
    )(place, x, g_mix, w_in4, *later)
    return res[:2], res[2], res[3:]


def _layer_norm_stats(gv):
    mu = _mean(gv)
    xc = gv - mu
    rs = lax.rsqrt(_mean(xc * xc) + EPS)
    return xc * rs, rs


def _gmlp_fwd(proj, ln_g, ln_b, wm, b_t, w_a, job=None):
    T = proj.shape[0]
    tm = min(256, T)

    def body(u_ref, v_ref, lg_ref, lb_ref, wm_ref, bt_ref, wa_ref, a_ref, ya_ref, a_s):
        gu, _ = _gelu(u_ref[...])
        gv, _ = _gelu(v_ref[...])
        vhat, _ = _layer_norm_stats(gv)
        vnb = (vhat * lg_ref[...] + lb_ref[...]).astype(BF16)
        for ch in range(tm // GCH):
            rows = slice(GCH * ch, GCH * (ch + 1))
            for g in range(NG):
                cols = slice(128 * g, 128 * (g + 1))
                mixed = _mm(wm_ref[g], vnb[rows, cols]) + bt_ref[:, g:g + 1]
                a_s[rows, cols] = gu[rows, cols] * mixed
        ab = a_s[...].astype(BF16)
        a_ref[...] = ab
        ya_ref[...] = _mm(ab, wa_ref[...])

    row = lambda i: (0, 0)
    return _call(
        body, name="gmlp_fwd", grid=(T // tm,), job=job, args=(proj, proj, ln_g, ln_b, wm, b_t, w_a),
        in_specs=[pl.BlockSpec((tm, D), lambda i: (i, 0)), pl.BlockSpec((tm, D), lambda i: (i, 1)),
                  pl.BlockSpec((1, D), row), pl.BlockSpec((1, D), row),
                  pl.BlockSpec((NG, GCH, GCH), lambda i: (0, 0, 0)), pl.BlockSpec((GCH, NG), row),
                  pl.BlockSpec((D, D), row)],
        out_specs=[pl.BlockSpec((tm, D), lambda i: (i, 0)), pl.BlockSpec((tm, D), lambda i: (i, 0))],
        out_shape=[SDS((T, D), BF16), SDS((T, D), F32)],
        scratch_shapes=[pltpu.VMEM((tm, D), F32)])


def _cumsum64(x, row):
    for s in (1, 2, 4, 8, 16, 32):
        x = x + jnp.where(row >= s, pltpu.roll(x, s, 0), 0.0)
    return x


def _revcumsum64(x, row):
    n = x.shape[0]
    for s in (1, 2, 4, 8, 16, 32):
        x = x + jnp.where(row < HCH - s, pltpu.roll(x, n - s, 0), 0.0)
    return x


def _head_mean(x):
    parts = [jnp.broadcast_to(_mean(x[:, HD * h:HD * (h + 1)]), (x.shape[0], HD)) for h in range(x.shape[1] // HD)]
    return jnp.concatenate(parts, axis=1)


def _seg_sum(x):
    n, c = x.shape
    s = jnp.sum(x.reshape(n // HCH, HCH, c), axis=1, keepdims=True)
    return jnp.broadcast_to(s, (n // HCH, HCH, c)).reshape(n, c)


def _hgrn_gates(fl, lbv, row):
    s = _sigmoid(fl)
    f = lbv + (1.0 - lbv) * s
    a = _cumsum64(jnp.log(f), row)
    a_mid = _seg_sum(jnp.where(row == HCH // 2 - 1, a, 0.0))
    a_last = _seg_sum(jnp.where(row == HCH - 1, a, 0.0))
    return s, f, a, a_mid, a_last


def _hgrn_fwd(proj, lb_table, norm_g, job=None):
    T = proj.shape[0]
    tb = min(512, T)
    nc = tb // HCH

    def body(q_ref, fl_ref, v_ref, g_ref, lbt_ref, gn_ref, o_ref, ob_ref, stb_ref, st_s, o_s):
        @pl.when(pl.program_id(1) == 0)
        def _():
            st_s[...] = jnp.zeros_like(st_s)

        row = lax.broadcasted_iota(jnp.int32, (tb, HW), 0) & (HCH - 1)
        lbv = _sigmoid(lbt_ref[0:1, :] - lbt_ref[1:2, :])
        _, f, a, a_mid, a_last = _hgrn_gates(fl_ref[...], lbv, row)
        k = 1.0 - f
        qs = q_ref[...] * QSCALE
        q_in = (qs * jnp.exp(a - a_mid)).astype(BF16)
        k_in = (k * jnp.exp(a_mid - a)).astype(BF16)
        q_a = (qs * jnp.exp(a)).astype(BF16)
        k_d = (k * jnp.exp(a_last - a)).astype(BF16)
        dec = jnp.exp(a_last)
        vb = v_ref[...].astype(BF16)
        tri = (lax.broadcasted_iota(jnp.int32, (HCH, HCH), 0)
               >= lax.broadcasted_iota(jnp.int32, (HCH, HCH), 1))
        for c in range(nc):
            sl = slice(HCH * c, HCH * (c + 1))
            for hh in range(HGRN_HB):
                hs = slice(HD * hh, HD * (hh + 1))
                st = st_s[hh]
                stb_ref[hh, c] = st
                sc = jnp.where(tri, _mm_nt(q_in[sl, hs], k_in[sl, hs]), 0.0)
                o_s[sl, hs] = _mm(sc.astype(BF16), vb[sl, hs]) + _mm_nt(q_a[sl, hs], st.astype(BF16))
                d64 = dec[sl, hs]
                st_s[hh] = st * jnp.concatenate([d64, d64], axis=0) + _mm_tn(vb[sl, hs], k_d[sl, hs])
        o = o_s[...]
        r = lax.rsqrt(_head_mean(o * o) + EPS)
        g = g_ref[...]
        o_ref[...] = o
        ob_ref[...] = (o * r * gn_ref[...] * (g * _sigmoid(g))).astype(BF16)

    def col(off):
        return pl.BlockSpec((tb, HW), lambda h, cb: (cb, off * (NH // HGRN_HB) + h))

    return _call(
        body, name="hgrn_fwd", grid=(NH // HGRN_HB, T // tb), job=job,
        args=(proj, proj, proj, proj, lb_table, norm_g),
        in_specs=[col(2), col(3), col(4), col(5),
                  pl.BlockSpec((2, HW), lambda h, cb: (0, h)), pl.BlockSpec((1, HW), lambda h, cb: (0, h))],
        out_specs=[pl.BlockSpec((tb, HW), lambda h, cb: (cb, h)), pl.BlockSpec((tb, HW), lambda h, cb: (cb, h)),
                   pl.BlockSpec((HGRN_HB, nc, HD, HD), lambda h, cb: (h, cb, 0, 0))],
        out_shape=[SDS((T, D), F32), SDS((T, D), BF16), SDS((NH, T // HCH, HD, HD), F32)],
        scratch_shapes=[pltpu.VMEM((HGRN_HB, HD, HD), F32), pltpu.VMEM((tb, HW), F32)])


def _merge_fwd(x, y_a, ob, proj, w_b, w_out):
    T = x.shape[0]
    tm = min(512, T)

    def body(x_ref, ya_ref, ob_ref, ga_ref, gb_ref, wb_ref, wo_ref, yb_ref, mg_ref, x1_ref):
        yb = _mm(ob_ref[...], wb_ref[...])
        merged = (_sigmoid(ga_ref[...]) * ya_ref[...] + _sigmoid(gb_ref[...]) * yb).astype(BF16)
        yb_ref[...] = yb
        mg_ref[...] = merged
        x1_ref[...] = x_ref[...] + _mm(merged, wo_ref[...])

    t = lambda i: (i, 0)
    w = lambda i: (0, 0)
    return pl.pallas_call(
        body, name="merge_fwd", grid=(T // tm,),
        in_specs=[pl.BlockSpec((tm, D), t), pl.BlockSpec((tm, D), t), pl.BlockSpec((tm, D), t),
                  pl.BlockSpec((tm, D), lambda i: (i, 6)), pl.BlockSpec((tm, D), lambda i: (i, 7)),
                  pl.BlockSpec((D, D), w), pl.BlockSpec((D, D), w)],
        out_specs=[pl.BlockSpec((tm, D), t)] * 3,
        out_shape=[SDS((T, D), F32), SDS((T, D), BF16), SDS((T, D), F32)],
        compiler_params=_cparams(),
    )(x, y_a, ob, proj, proj, w_b, w_out)


def _ffn_fwd_bwd(x1, target, g_ffn, g_fin, w_gu4, w_down):
    T = x1.shape[0]
    tm = min(256, T)
    inv_d = 1.0 / D

    def body(x1_ref, tg_ref, gf_ref, gn_ref, wgu_ref, wd_ref,
             act_ref, dx2b_ref, h2b_ref, dgu_ref, dx1_ref, dx1b_ref, acc_ref):
        @pl.when(pl.program_id(0) == 0)
        def _():
            acc_ref[...] = jnp.zeros_like(acc_ref)

        x1v = x1_ref[...]
        gf = gf_ref[...]
        gn = gn_ref[...]
        rr1 = lax.rsqrt(_mean(x1v * x1v) + EPS)
        x1n = x1v * rr1
        h2b = (x1n * gf).astype(BF16)
        h2b_ref[...] = h2b
        p = [_mm(h2b, wgu_ref[k]) for k in range(NCHIP)]
        sg = [_sigmoid(p[0]), _sigmoid(p[1])]
        si = [p[0] * sg[0], p[1] * sg[1]]
        x2 = x1v
        for k in range(2):
            actk = (si[k] * p[2 + k]).astype(BF16)
            act_ref[:, FFS * k:FFS * (k + 1)] = actk
            x2 = x2 + _mm(actk, wd_ref[FFS * k:FFS * (k + 1), :])
        rr2 = lax.rsqrt(_mean(x2 * x2) + EPS)
        x2n = x2 * rr2
        e = x2n * gn - tg_ref[...]
        acc_ref[0] += _rows8(e * e) * (0.5 * inv_d)
        dy = e * inv_d
        acc_ref[1] += _rows8(dy * x2n)
        dxn = dy * gn
        dx2 = rr2 * (dxn - x2n * _mean(dxn * x2n))
        dx2b = dx2.astype(BF16)
        dx2b_ref[...] = dx2b
        dh2 = None
        for k in range(2):
            dact = _mm_nt(dx2b, wd_ref[FFS * k:FFS * (k + 1), :])
            dgate = (dact * p[2 + k] * (sg[k] * (1.0 + p[k] * (1.0 - sg[k])))).astype(BF16)
            dup = (dact * si[k]).astype(BF16)
            dgu_ref[k] = dgate
            dgu_ref[2 + k] = dup
            part = _mm_nt(dgate, wgu_ref[k]) + _mm_nt(dup, wgu_ref[2 + k])
            dh2 = part if dh2 is None else dh2 + part
        acc_ref[2] += _rows8(dh2 * x1n)
        dxn1 = dh2 * gf
        dx1 = dx2 + rr1 * (dxn1 - x1n * _mean(dxn1 * x1n))
        dx1_ref[...] = dx1
        dx1b_ref[...] = dx1.astype(BF16)

    t = lambda i: (i, 0)
    w = lambda i: (0, 0)
    one = pl.Buffered(1)
    return pl.pallas_call(
        body, name="ffn_fwd_bwd", grid=(T // tm,),
        in_specs=[pl.BlockSpec((tm, D), t), pl.BlockSpec((tm, D), t),
                  pl.BlockSpec((1, D), w), pl.BlockSpec((1, D), w),
                  pl.BlockSpec((NCHIP, D, FFS), lambda i: (0, 0, 0), pipeline_mode=one),
                  pl.BlockSpec((FF, D), w, pipeline_mode=one)],
        out_specs=[pl.BlockSpec((tm, FF), t), pl.BlockSpec((tm, D), t), pl.BlockSpec((tm, D), t),
                   pl.BlockSpec((NCHIP, tm, FFS), lambda i: (0, i, 0)),
                   pl.BlockSpec((tm, D), t), pl.BlockSpec((tm, D), t),
                   pl.BlockSpec((3, 8, D), lambda i: (0, 0, 0))],
        out_shape=[SDS((T, FF), BF16), SDS((T, D), BF16), SDS((T, D), BF16),
                   SDS((NCHIP, T, FFS), BF16), SDS((T, D), F32), SDS((T, D), BF16),
                   SDS((3, 8, D), F32)],
        compiler_params=_cparams(),
    )(x1, target, g_ffn, g_fin, w_gu4, w_down)


def _merge_bwd(dx1b, y_a, y_b, proj, w_out, w_a, w_b, job=None):
    T = dx1b.shape[0]
    tm = min(512, T)

    def body(dx_ref, ya_ref, yb_ref, ga_ref, gb_ref, wo_ref, wa_ref, wb_ref,
             dya_ref, dyb_ref, da_ref, dob_ref, dp_ref):
        dm = _mm_nt(dx_ref[...], wo_ref[...])
        sa = _sigmoid(ga_ref[...])
        sb = _sigmoid(gb_ref[...])
        dya = (dm * sa).astype(BF16)
        dyb = (dm * sb).astype(BF16)
        dya_ref[...] = dya
        dyb_ref[...] = dyb
        dp_ref[0] = (dm * ya_ref[...] * sa * (1.0 - sa)).astype(BF16)
        dp_ref[1] = (dm * yb_ref[...] * sb * (1.0 - sb)).astype(BF16)
        da_ref[...] = _mm_nt(dya, wa_ref[...])
        dob_ref[...] = _mm_nt(dyb, wb_ref[...])

    t = lambda i: (i, 0)
    w = lambda i: (0, 0)
    return _call(
        body, name="merge_bwd", grid=(T // tm,),
        in_specs=[pl.BlockSpec((tm, D), t), pl.BlockSpec((tm, D), t), pl.BlockSpec((tm, D), t),
                  pl.BlockSpec((tm, D), lambda i: (i, 6)), pl.BlockSpec((tm, D), lambda i: (i, 7)),
                  pl.BlockSpec((D, D), w), pl.BlockSpec((D, D), w), pl.BlockSpec((D, D), w)],
        out_specs=[pl.BlockSpec((tm, D), t)] * 4 + [pl.BlockSpec((2, tm, D), lambda i: (3, i, 0))],
        out_shape=[SDS((T, D), BF16), SDS((T, D), BF16), SDS((T, D), F32), SDS((T, D), F32),
                   SDS((NIN, T, D), BF16)],
        args=(dx1b, y_a, y_b, proj, proj, w_out, w_a, w_b), job=job)


def _hgrn_bwd(dproj, dob, o_raw, proj, st_before, lb_table, norm_g, job=None):
    T = dob.shape[0]
    tb = min(512, T)
    nc = tb // HCH
    nb = T // tb

    def body(dp_in, dob_ref, o_ref, q_ref, fl_ref, v_ref, g_ref, stb_ref, lbt_ref, gn_ref,
             dp_ref, acc_ref, dst_s, dqin_s, dqa_s, dkin_s, dkd_s, dv_s, ddec_s):
        del dp_in

        @pl.when(pl.program_id(1) == 0)
        def _():
            dst_s[...] = jnp.zeros_like(dst_s)
            acc_ref[...] = jnp.zeros_like(acc_ref)

        row = lax.broadcasted_iota(jnp.int32, (tb, HW), 0) & (HCH - 1)
        gn = gn_ref[...]
        lbv = _sigmoid(lbt_ref[0:1, :] - lbt_ref[1:2, :])
        o = o_ref[...]
        r = lax.rsqrt(_head_mean(o * o) + EPS)
        on = o * r
        g = g_ref[...]
        sgm = _sigmoid(g)
        dob_v = dob_ref[...]
        dp_ref[3] = (dob_v * on * gn * (sgm * (1.0 + g * (1.0 - sgm)))).astype(BF16)
        do_n = dob_v * (g * sgm)
        acc_ref[1] += _rows8(do_n * on)
        dxn = do_n * gn
        do = (r * (dxn - on * _head_mean(dxn * on))).astype(BF16)
        s, f, a, a_mid, a_last = _hgrn_gates(fl_ref[...], lbv, row)
        k = 1.0 - f
        qs = q_ref[...] * QSCALE
        e_q = jnp.exp(a - a_mid)
        e_k = jnp.exp(a_mid - a)
        e_a = jnp.exp(a)
        e_l = jnp.exp(a_last - a)
        dec = jnp.exp(a_last)
        q_in = qs * e_q
        k_in = k * e_k
        q_a = qs * e_a
        k_d = k * e_l
        q_inb, k_inb, q_ab, k_db = (z.astype(BF16) for z in (q_in, k_in, q_a, k_d))
        vb = v_ref[...].astype(BF16)
        tri = (lax.broadcasted_iota(jnp.int32, (HCH, HCH), 0)
               >= lax.broadcasted_iota(jnp.int32, (HCH, HCH), 1))
        for c in reversed(range(nc)):
            sl = slice(HCH * c, HCH * (c + 1))
            for hh in range(HGRN_HB):
                hs = slice(HD * hh, HD * (hh + 1))
                stp = stb_ref[hh, c]
                dst = dst_s[hh]
                dstb = dst.astype(BF16)
                do_c = do[sl, hs]
                v_c = vb[sl, hs]
                dqa_s[sl, hs] = _mm(do_c, stp.astype(BF16))
                dkd_s[sl, hs] = _mm(v_c, dstb)
                ddec_s[sl, hs] = jnp.broadcast_to(jnp.sum(dst * stp, axis=0, keepdims=True), (HCH, HD))
                sc = jnp.where(tri, _mm_nt(q_inb[sl, hs], k_inb[sl, hs]), 0.0).astype(BF16)
                dsc = jnp.where(tri, _mm_nt(do_c, v_c), 0.0).astype(BF16)
                dv_s[sl, hs] = _mm_nt(k_db[sl, hs], dstb) + _mm_tn(sc, do_c)
                dqin_s[sl, hs] = _mm(dsc, k_inb[sl, hs])
                dkin_s[sl, hs] = _mm_tn(dsc, q_inb[sl, hs])
                d64 = dec[sl, hs]
                dst_s[hh] = dst * jnp.concatenate([d64, d64], axis=0) + _mm_tn(do_c, q_ab[sl, hs])
        dq_in = dqin_s[...]
        dq_a = dqa_s[...]
        dk_in = dkin_s[...]
        dk_d = dkd_s[...]
        dp_ref[0] = ((dq_in * e_q + dq_a * e_a) * QSCALE).astype(BF16)
        dp_ref[2] = dv_s[...].astype(BF16)
        tq = dq_in * q_in
        tk = dk_in * k_in
        td = dk_d * k_d
        d_a = tq + dq_a * q_a - tk - td
        d_a = d_a + jnp.where(row == HCH // 2 - 1, _seg_sum(tk - tq), 0.0)
        d_a = d_a + jnp.where(row == HCH - 1, _seg_sum(td) + ddec_s[...] * dec, 0.0)
        dlf = _revcumsum64(d_a, row)
        df = dlf / f - (dk_in * e_k + dk_d * e_l)
        dp_ref[1] = (df * (1.0 - lbv) * s * (1.0 - s)).astype(BF16)
        acc_ref[0] += _rows8(df * (1.0 - s))

    def col(off):
        return pl.BlockSpec((tb, HW), lambda h, cb: (nb - 1 - cb, off * (NH // HGRN_HB) + h))

    hb = lambda h, cb: (nb - 1 - cb, h)
    return _call(
        body, name="hgrn_bwd", grid=(NH // HGRN_HB, nb), job=job,
        args=(dproj, dob, o_raw, proj, proj, proj, proj, st_before, lb_table, norm_g),
        in_specs=[ANY, pl.BlockSpec((tb, HW), hb), pl.BlockSpec((tb, HW), hb),
                  col(2), col(3), col(4), col(5),
                  pl.BlockSpec((HGRN_HB, nc, HD, HD), lambda h, cb: (h, nb - 1 - cb, 0, 0)),
                  pl.BlockSpec((2, HW), lambda h, cb: (0, h)), pl.BlockSpec((1, HW), lambda h, cb: (0, h))],
        out_specs=[pl.BlockSpec((4, tb, HW), lambda h, cb: (0, nb - 1 - cb, h)),
                   pl.BlockSpec((2, 8, HW), lambda h, cb: (0, 0, h))],
        out_shape=[SDS(dproj.shape, BF16), SDS((2, 8, D), F32)],
        scratch_shapes=[pltpu.VMEM((HGRN_HB, HD, HD), F32)] + [pltpu.VMEM((tb, HW), F32)] * 6,
        aliases={0: 0})


def _gmlp_bwd(dproj, da, proj, ln_g, ln_b, wm, wm_t, b_t):
    T = da.shape[0]
    tm = min(256, T)

    def body(dp_in, da_ref, u_ref, v_ref, lg_ref, lb_ref, wm_ref, wmt_ref, bt_ref,
             dp_ref, acc_ref, dws_ref, dmix_ref, du_s, dvn_s):
        del dp_in

        @pl.when(pl.program_id(0) == 0)
        def _():
            acc_ref[...] = jnp.zeros_like(acc_ref)
            dws_ref[...] = jnp.zeros_like(dws_ref)
            dmix_ref[...] = jnp.zeros_like(dmix_ref)

        u = u_ref[...]
        v = v_ref[...]
        lg = lg_ref[...]
        gu, t_u = _gelu(u)
        gv, t_v = _gelu(v)
        vhat, rs = _layer_norm_stats(gv)
        vnb = (vhat * lg + lb_ref[...]).astype(BF16)
        da_v = da_ref[...]
        for ch in range(tm // GCH):
            rows = slice(GCH * ch, GCH * (ch + 1))
            for g in range(NG):
                cols = slice(128 * g, 128 * (g + 1))
                vng = vnb[rows, cols]
                mixed = _mm(wm_ref[g], vng) + bt_ref[:, g:g + 1]
                dag = da_v[rows, cols]
                dmx = dag * gu[rows, cols]
                du_s[rows, cols] = dag * mixed
                dmxb = dmx.astype(BF16)
                dws_ref[:, cols] += _mm_nt(dmxb, vng)
                dmix_ref[:, cols] += dmx
                dvn_s[rows, cols] = _mm(wmt_ref[g], dmxb)
        dp_ref[0] = (du_s[...] * _gelu_grad(u, t_u)).astype(BF16)
        dvn = dvn_s[...]
        acc_ref[0] += _rows8(dvn * vhat)
        acc_ref[1] += _rows8(dvn)
        dvh = dvn * lg
        dgv = rs * (dvh - _mean(dvh) - vhat * _mean(dvh * vhat))
        dp_ref[1] = (dgv * _gelu_grad(v, t_v)).astype(BF16)

    row = lambda i: (0, 0)
    w3 = lambda i: (0, 0, 0)
    return pl.pallas_call(
        body, name="gmlp_bwd", grid=(T // tm,),
        in_specs=[ANY, pl.BlockSpec((tm, D), lambda i: (i, 0)),
                  pl.BlockSpec((tm, D), lambda i: (i, 0)), pl.BlockSpec((tm, D), lambda i: (i, 1)),
                  pl.BlockSpec((1, D), row), pl.BlockSpec((1, D), row),
                  pl.BlockSpec((NG, GCH, GCH), w3), pl.BlockSpec((NG, GCH, GCH), w3),
                  pl.BlockSpec((GCH, NG), row)],
        out_specs=[pl.BlockSpec((2, tm, D), lambda i: (2, i, 0)),
                   pl.BlockSpec((2, 8, D), w3), pl.BlockSpec((GCH, D), row), pl.BlockSpec((GCH, D), row)],
        out_shape=[SDS(dproj.shape, BF16), SDS((2, 8, D), F32), SDS((GCH, D), F32), SDS((GCH, D), F32)],
        scratch_shapes=[pltpu.VMEM((tm, D), F32), pltpu.VMEM((tm, D), F32)],
        input_output_aliases={0: 0},
        compiler_params=_cparams(),
    )(dproj, da, proj, proj, ln_g, ln_b, wm, wm_t, b_t)


def _proj_bwd(dproj, w_in4, x, dx1, g_mix, job=None):
    T = x.shape[0]
    tm = min(256, T)
    order = (2, 3, 4, 5, 0, 1, 6, 7)

    def body(dp_ref, w_ref, x_ref, dx1_ref, g_ref, gx_ref, acc_ref):
        @pl.when(pl.program_id(0) == 0)
        def _():
            acc_ref[...] = jnp.zeros_like(acc_ref)

        dh = None
        for m, og in enumerate(order):
            part = _mm_nt(dp_ref[m], w_ref[og // 2, :, D * (og % 2):D * (og % 2 + 1)])
            dh = part if dh is None else dh + part
        xv = x_ref[...]
        r = lax.rsqrt(_mean(xv * xv) + EPS)
        xn = xv * r
        acc_ref[...] += _rows8(dh * xn)
        dxn = dh * g_ref[...]
        gx_ref[...] = dx1_ref[...] + r * (dxn - xn * _mean(dxn * xn))

    t = lambda i: (i, 0)
    return _call(
        body, name="proj_bwd", grid=(T // tm,),
        in_specs=[pl.BlockSpec((NIN, tm, D), lambda i: (0, i, 0)),
                  pl.BlockSpec((NCHIP, D, 2 * D), lambda i: (0, 0, 0), pipeline_mode=pl.Buffered(1)),
                  pl.BlockSpec((tm, D), t), pl.BlockSpec((tm, D), t), pl.BlockSpec((1, D), lambda i: (0, 0))],
        out_specs=[pl.BlockSpec((tm, D), t), pl.BlockSpec((8, D), lambda i: (0, 0))],
        out_shape=[SDS((T, D), F32), SDS((8, D), F32)],
        args=(dproj, w_in4, x, dx1, g_mix), job=job)


def _dw_call(name, a, b, a_spec, b_spec, o_spec, out_shape, nblk, tt, job=None):
    T = a.shape[-2]

    def body(a_ref, b_ref, o_ref):
        @pl.when(pl.program_id(1) == 0)
        def _():
            o_ref[...] = jnp.zeros_like(o_ref)
        o_ref[...] += _mm_tn(a_ref[...], b_ref[...])

    (out,), job_out = _call(
        body, name=name, grid=(nblk, T // tt), in_specs=[a_spec, b_spec], out_specs=[o_spec],
        out_shape=[out_shape], args=(a, b), job=job)
    return out, job_out


def _dw_in(hb, dproj, job=None):
    tt = min(DW_TOKENS, hb.shape[0])
    return _dw_call(
        "dw_in", hb, dproj,
        pl.BlockSpec((tt, D), lambda m, t: (t, 0)),
        pl.BlockSpec((None, tt, D), lambda m, t: (m, t, 0)),
        pl.BlockSpec((None, D, D), lambda m, t: (_orig_group(m) // 2, 0, _orig_group(m) % 2)),
        SDS((NCHIP, D, 2 * D), F32), NIN, tt, job)


def _dw_gate_up(h2b, dgu4, job=None):
    tt = min(DW_TOKENS, h2b.shape[0])
    return _dw_call(
        "dw_gate_up", h2b, dgu4,
        pl.BlockSpec((tt, D), lambda k, t: (t, 0)),
        pl.BlockSpec((None, tt, FFS), lambda k, t: (k, t, 0)),
        pl.BlockSpec((None, D, FFS), lambda k, t: (k, 0, 0)),
        SDS((NCHIP, D, FFS), F32), NCHIP, tt, job)


def _dw_down(act, dx2b, job=None):
    tt = min(DW_TOKENS, act.shape[0])
    g, job_out = _dw_call(
        "dw_down", act, dx2b,
        pl.BlockSpec((tt, FFS), lambda k, t: (t, k)),
        pl.BlockSpec((tt, D), lambda k, t: (t, 0)),
        pl.BlockSpec((FFS, D), lambda k, t: (k, 0)),
        SDS((FF, D), F32), 2, tt, job)
    return g.reshape(NCHIP, FF // NCHIP, D), job_out


def _dw_square(name, a, b, job=None):
    tt = min(DW_TOKENS, a.shape[0])
    g, job_out = _dw_call(
        name, a, b,
        pl.BlockSpec((tt, D), lambda k, t: (t, 0)), pl.BlockSpec((tt, D), lambda k, t: (t, 0)),
        pl.BlockSpec((D, D), lambda k, t: (0, 0)), SDS((D, D), F32), 1, tt, job)
    return g.reshape(NCHIP, D // NCHIP, D), job_out


def _place():
    x, y, c = lax.axis_index("x"), lax.axis_index("y"), lax.axis_index("c")
    return x, y, c, 2 * x + y


def _chip_at(x, y, s):
    return x ^ (s >> 1), y ^ (s & 1)


class _Job:
    def __init__(self, ins, out_shapes, sems, start, finish, aliases=None):
        self.ins, self.out_shapes, self.sems = list(ins), list(out_shapes), list(sems)
        self.start, self.finish, self.aliases = start, finish, dict(aliases or {})


def _join_jobs(*jobs):
    def cut(refs, sizes):
        out, at = [], 0
        for n in sizes:
            out.append(refs[at:at + n])
            at += n
        return out

    ni = [len(j.ins) for j in jobs]
    no = [len(j.out_shapes) for j in jobs]
    ns = [len(j.sems) for j in jobs]

    def run(which):
        def go(ins, outs, sems):
            for j, a, b, c in zip(jobs, cut(ins, ni), cut(outs, no), cut(sems, ns)):
                getattr(j, which)(a, b, c)
        return go

    aliases = {}
    for k, j in enumerate(jobs):
        for a, b in j.aliases.items():
            aliases[sum(ni[:k]) + a] = sum(no[:k]) + b
    return _Job([a for j in jobs for a in j.ins], [o for j in jobs for o in j.out_shapes],
                [s for j in jobs for s in j.sems], run("start"), run("finish"), aliases)


def _call(body, *, name, grid, in_specs, out_specs, out_shape, args, scratch_shapes=(), aliases=None, job=None):
    n_in, n_out, n_scr = len(in_specs), len(out_specs), len(scratch_shapes)
    aliases = dict(aliases or {})
    if job is None:
        res = pl.pallas_call(
            body, name=name, grid=grid, in_specs=list(in_specs), out_specs=list(out_specs),
            out_shape=list(out_shape), scratch_shapes=list(scratch_shapes), input_output_aliases=aliases,
            compiler_params=_cparams())(*args)
        return list(res), []
    ji, jo = len(job.ins), len(job.out_shapes)

    def wrapped(*refs):
        ins, jin = refs[:n_in], refs[n_in:n_in + ji]
        o0 = n_in + ji
        outs, jout = refs[o0:o0 + n_out], refs[o0 + n_out:o0 + n_out + jo]
        s0 = o0 + n_out + jo
        scr, jsem = refs[s0:s0 + n_scr], refs[s0 + n_scr:]
        ids = [pl.program_id(a) for a in range(len(grid))]
        first = functools.reduce(jnp.logical_and, [i == 0 for i in ids])
        last = functools.reduce(jnp.logical_and, [i == g - 1 for i, g in zip(ids, grid)])

        @pl.when(first)
        def _():
            job.start(jin, jout, jsem)

        body(*ins, *outs, *scr)

        @pl.when(last)
        def _():
            job.finish(jin, jout, jsem)

    for a, b in job.aliases.items():
        aliases[n_in + a] = n_out + b
    res = pl.pallas_call(
        wrapped, name=name, grid=grid, in_specs=list(in_specs) + [ANY] * ji,
        out_specs=list(out_specs) + [ANY] * jo, out_shape=list(out_shape) + job.out_shapes,
        scratch_shapes=list(scratch_shapes) + job.sems, input_output_aliases=aliases,
        compiler_params=_cparams(has_side_effects=True))(*args, *job.ins)
    return list(res[:n_out]), list(res[n_out:])


def _run_job(job, name):
    ji, jo = len(job.ins), len(job.out_shapes)

    def body(*refs):
        jin, jout, jsem = refs[:ji], refs[ji:ji + jo], refs[ji + jo:]
        job.start(jin, jout, jsem)
        job.finish(jin, jout, jsem)

    return list(pl.pallas_call(
        body, name=name, in_specs=[ANY] * ji, out_specs=[ANY] * jo, out_shape=job.out_shapes,
        scratch_shapes=job.sems, input_output_aliases=job.aliases,
        compiler_params=pltpu.CompilerParams(has_side_effects=True))(*job.ins))


def _cast_shard(name, place, w):
    rows, cols = w.shape
    tr = 352 if rows % 352 == 0 else 256

    def body(pc_ref, w_ref, o_ref):
        del pc_ref
        o_ref[...] = w_ref[...].astype(BF16)

    return pl.pallas_call(
        body, name=name,
        grid_spec=pltpu.PrefetchScalarGridSpec(
            num_scalar_prefetch=1, grid=(rows // tr,),
            in_specs=[pl.BlockSpec((tr, cols), lambda i, pc: (i, 0))],
            out_specs=pl.BlockSpec((None, tr, cols), lambda i, pc: (pc[0], i, 0))),
        out_shape=SDS((NCHIP, rows, cols), BF16),
        compiler_params=_cparams(),
    )(place, w)


def _sibling_copy(ref, send_sem, recv_sem):
    x, y, c, _ = _place()
    return pltpu.make_async_remote_copy(src_ref=ref, dst_ref=ref, send_sem=send_sem, recv_sem=recv_sem,
                                        device_id=(x, y, 1 - c), device_id_type=MESH)


def _half_rows(arr, slot, core):
    half = arr.shape[1] // 2
    return arr.at[slot, pl.ds(pl.multiple_of(core * half, 16), half)]


def _gather_start(arrs, sems):
    send_sem, recv_sem = sems[0], sems[1]
    x, y, c, j = _place()
    for w, arr in enumerate(arrs):
        mine = _half_rows(arr, j, c)
        for s in range(1, NCHIP):
            cx, cy = _chip_at(x, y, s)
            pltpu.make_async_remote_copy(
                src_ref=mine, dst_ref=mine, send_sem=send_sem.at[w, s - 1], recv_sem=recv_sem.at[w, s - 1],
                device_id=(cx, cy, c), device_id_type=MESH).start()


def _gather_land(arrs, sems, s):
    send_sem, recv_sem, fsend_sem, frecv_sem = sems
    _, _, c, j = _place()
    for w, arr in enumerate(arrs):
        landed = _half_rows(arr, j ^ s, c)
        _sibling_copy(landed, send_sem.at[w, s - 1], recv_sem.at[w, s - 1]).wait_recv()
        _sibling_copy(landed, fsend_sem.at[w, s - 1], frecv_sem.at[w, s - 1]).start()
    for w, arr in enumerate(arrs):
        theirs = _half_rows(arr, j ^ s, 1 - c)
        _sibling_copy(theirs, fsend_sem.at[w, s - 1], frecv_sem.at[w, s - 1]).wait_recv()


def _gather_drain(arrs, sems):
    send_sem, recv_sem, fsend_sem, frecv_sem = sems
    _, _, c, j = _place()
    for s in range(1, NCHIP):
        for w, arr in enumerate(arrs):
            _sibling_copy(_half_rows(arr, j, c), send_sem.at[w, s - 1], recv_sem.at[w, s - 1]).wait_send()
            _sibling_copy(_half_rows(arr, j ^ s, c), fsend_sem.at[w, s - 1], frecv_sem.at[w, s - 1]).wait_send()


def _gather_finish(arrs, sems):
    for s in range(1, NCHIP):
        _gather_land(arrs, sems, s)
    _gather_drain(arrs, sems)


def _gather_job(arrs):
    n = len(arrs)
    return _Job(arrs, [SDS(a.shape, a.dtype) for a in arrs], [pltpu.SemaphoreType.DMA((n, NCHIP - 1))] * 4,
                lambda ins, outs, sems: _gather_start(outs, sems),
                lambda ins, outs, sems: _gather_finish(outs, sems), {k: k for k in range(n)})


def _exchange_job(arrs, out_shapes, n, copies):
    def start(ins, outs, sems):
        for cp in copies(ins, outs, sems[0], sems[1]):
            cp.start()

    def finish(ins, outs, sems):
        for cp in copies(ins, outs, sems[0], sems[1]):
            cp.wait()

    return _Job(arrs, out_shapes, [pltpu.SemaphoreType.DMA((n,))] * 2, start, finish)


def _pair_exchange_job(grads):
    def copies(ins, outs, send_sem, recv_sem):
        x, y, c, _ = _place()
        res = []
        for w in range(len(grads)):
            half = ins[w].shape[1] // 2
            theirs = pl.ds(pl.multiple_of((1 - c) * half, 8), half)
            res.append(pltpu.make_async_remote_copy(
                src_ref=ins[w].at[:, theirs, :], dst_ref=outs[w], send_sem=send_sem.at[w],
                recv_sem=recv_sem.at[w], device_id=(x, y, 1 - c), device_id_type=MESH))
        return res

    return _exchange_job(grads, [SDS((NCHIP, g.shape[1] // 2, g.shape[2]), F32) for g in grads],
                         len(grads), copies)


def _row_tile(rows):
    return 176 if rows % 176 == 0 and rows % 128 else 128


def _pair_sum(name, place, g, sib):
    half, cols = sib.shape[1], sib.shape[2]
    tr = _row_tile(half)
    nt = half // tr

    def body(pc_ref, g_ref, s_ref, own_ref, out_ref):
        del pc_ref
        v = g_ref[...] + s_ref[...]
        out_ref[...] = v.astype(BF16)

        @pl.when(pl.program_id(1) == 0)
        def _():
            own_ref[...] = v

    return pl.pallas_call(
        body, name=name,
        grid_spec=pltpu.PrefetchScalarGridSpec(
            num_scalar_prefetch=1, grid=(nt, NCHIP),
            in_specs=[pl.BlockSpec((None, tr, cols), lambda i, s, pc: (pc[0] ^ s, pc[1] * nt + i, 0)),
                      pl.BlockSpec((None, tr, cols), lambda i, s, pc: (pc[0] ^ s, i, 0))],
            out_specs=[pl.BlockSpec((tr, cols), lambda i, s, pc: (i, 0)),
                       pl.BlockSpec((None, tr, cols), lambda i, s, pc: (s, i, 0))]),
        out_shape=[SDS((half, cols), F32), SDS((NCHIP, half, cols), BF16)],
        compiler_params=_cparams(),
    )(place, g, sib)


def _chip_exchange_job(parts):
    def copies(ins, outs, send_sem, recv_sem):
        x, y, c, _ = _place()
        res = []
        for w in range(len(parts)):
            for s in range(1, NCHIP):
                cx, cy = _chip_at(x, y, s)
                k = w * (NCHIP - 1) + s - 1
                res.append(pltpu.make_async_remote_copy(
                    src_ref=ins[w].at[s], dst_ref=outs[w].at[s - 1], send_sem=send_sem.at[k],
                    recv_sem=recv_sem.at[k], device_id=(cx, cy, c), device_id_type=MESH))
        return res

    return _exchange_job(parts, [SDS((NCHIP - 1,) + p.shape[1:], BF16) for p in parts],
                         len(parts) * (NCHIP - 1), copies)


def _chip_sum(name, own, rem):
    half, cols = own.shape
    tr = _row_tile(half)

    def body(own_ref, rem_ref, out_ref):
        out_ref[...] = ((own_ref[...] + rem_ref[0].astype(F32)) + rem_ref[1].astype(F32)) + rem_ref[2].astype(F32)

    return pl.pallas_call(
        body, name=name, grid=(half // tr,),
        in_specs=[pl.BlockSpec((tr, cols), lambda i: (i, 0)),
                  pl.BlockSpec((NCHIP - 1, tr, cols), lambda i: (0, i, 0))],
        out_specs=pl.BlockSpec((tr, cols), lambda i: (i, 0)),
        out_shape=SDS((half, cols), F32),
        compiler_params=_cparams(),
    )(own, rem)


def _share_halves_job(halves):
    def copies(ins, outs, send_sem, recv_sem):
        x, y, c, _ = _place()
        return [pltpu.make_async_remote_copy(
            src_ref=ins[w], dst_ref=outs[w], send_sem=send_sem.at[w], recv_sem=recv_sem.at[w],
            device_id=(x, y, 1 - c), device_id_type=MESH) for w in range(len(halves))]

    return _exchange_job(halves, [SDS(h.shape, F32) for h in halves], len(halves), copies)


def _adamw_math(w, g, m, v):
    m = B1 * m + (1.0 - B1) * g
    v = B2 * v + (1.0 - B2) * (g * g)
    m_hat = m / (1.0 - B1 ** STEP)
    v_hat = v / (1.0 - B2 ** STEP)
    delta = -LR * (m_hat / (jnp.sqrt(v_hat) + AEPS) + WD * w)
    return delta, m, v


def _adamw(name, place, w, own, sib, m, v):
    rows, cols = w.shape
    half = rows // 2
    tr = 352 if half % 352 == 0 else min(256, half)
    nt = half // tr

    def body(pc_ref, w_ref, own_ref, sib_ref, m_ref, v_ref, g_ref, d_ref, mo_ref, vo_ref):
        g = jnp.where(pl.program_id(0) == pc_ref[1], own_ref[...], sib_ref[...])
        d, mn, vn = _adamw_math(w_ref[...], g, m_ref[...], v_ref[...])
        g_ref[...] = g
        d_ref[...] = d
        mo_ref[...] = mn
        vo_ref[...] = vn

    full = pl.BlockSpec((tr, cols), lambda h, i, pc: (h * nt + i, 0))
    part = pl.BlockSpec((tr, cols), lambda h, i, pc: (i, 0))
    return pl.pallas_call(
        body, name=name,
        grid_spec=pltpu.PrefetchScalarGridSpec(
            num_scalar_prefetch=1, grid=(2, nt),
            in_specs=[full, part, part, full, full], out_specs=[full] * 4),
        out_shape=[SDS((rows, cols), F32)] * 4,
        compiler_params=_cparams(),
    )(place, w, own, sib, m, v)


def _small_allreduce_adamw(sp, w, m, v):
    shape = sp.shape

    def body(sp_ref, w_ref, m_ref, v_ref, g_ref, d_ref, mo_ref, vo_ref,
             sib_s, pair_s, chip_s, send_sem, recv_sem):
        x, y, c, j = _place()
        cp = pltpu.make_async_remote_copy(
            src_ref=sp_ref, dst_ref=sib_s, send_sem=send_sem.at[0], recv_sem=recv_sem.at[0],
            device_id=(x, y, 1 - c), device_id_type=MESH)
        cp.start()
        cp.wait()
        pair_s[...] = sp_ref[...] + sib_s[...]
        cps = []
        for s in range(1, NCHIP):
            cx, cy = _chip_at(x, y, s)
            cp = pltpu.make_async_remote_copy(
                src_ref=pair_s, dst_ref=chip_s.at[s], send_sem=send_sem.at[s], recv_sem=recv_sem.at[s],
                device_id=(cx, cy, c), device_id_type=MESH)
            cp.start()
            cps.append(cp)
        chip_s[0] = pair_s[...]
        for cp in cps:
            cp.wait()
        tot = chip_s[j]
        for k in range(1, NCHIP):
            tot = tot + chip_s[k ^ j]
        g_ref[...] = tot
        d, mn, vn = _adamw_math(w_ref[...], tot, m_ref[...], v_ref[...])
        d_ref[...] = d
        mo_ref[...] = mn
        vo_ref[...] = vn

    vm = pl.BlockSpec(memory_space=pltpu.VMEM)
    return pl.pallas_call(
        body, name="small_allreduce_adamw",
        in_specs=[vm] * 4, out_specs=[vm] * 4, out_shape=[SDS(shape, F32)] * 4,
        scratch_shapes=[pltpu.VMEM(shape, F32), pltpu.VMEM(shape, F32), pltpu.VMEM((NCHIP,) + shape, F32),
                        pltpu.SemaphoreType.DMA((NCHIP,)), pltpu.SemaphoreType.DMA((NCHIP,))],
        compiler_params=pltpu.CompilerParams(has_side_effects=True),
    )(sp, w, m, v)


def _pack_small(first, mix, ln_g, ln_b, b_s, lbt, hn, ffn, fin, w_s):
    rows = [first.reshape(1, D), mix.reshape(1, D), ln_g.reshape(1, D), ln_b.reshape(1, D),
            b_s.reshape(1, D), lbt.reshape(2, D), hn.reshape(1, D), ffn.reshape(1, D), fin.reshape(1, D),
            jnp.zeros((6, D), F32)]
    return jnp.concatenate(rows + [w_s.reshape(NG, GCH, GCH).transpose(1, 0, 2).reshape(GCH, D)], axis=0)


def _unpack_small(p):
    w_s = p[16:].reshape(GCH, NG, GCH).transpose(1, 0, 2).reshape(1, NG, GCH, GCH)
    return dict(norm_mix_g=p[1:2], gmlp_ln_g=p[2:3], gmlp_ln_b=p[3:4], gmlp_b_s=p[4].reshape(1, NG, GCH),
                hgrn_lb_table=p[5:7], hgrn_norm_g=p[7:8], norm_ffn_g=p[8:9], norm_final_g=p[9],
                gmlp_w_s=w_s)


SMALL = ("norm_mix_g", "gmlp_ln_g", "gmlp_ln_b", "gmlp_w_s", "gmlp_b_s", "hgrn_lb_table", "hgrn_norm_g",
         "norm_ffn_g", "norm_final_g")
BIG = ("w_in", "w_gate_up", "w_branch_a", "w_branch_b", "w_out", "w_down")
ORDER = ("norm_mix_g", "w_in", "gmlp_ln_g", "gmlp_ln_b", "gmlp_w_s", "gmlp_b_s", "hgrn_lb_table",
         "hgrn_norm_g", "w_branch_a", "w_branch_b", "w_out", "norm_ffn_g", "w_gate_up", "w_down",
         "norm_final_g")


def kernel(x, norm_mix_g, w_in, gmlp_ln_g, gmlp_ln_b, gmlp_w_s, gmlp_b_s, hgrn_lb_table, hgrn_norm_g, w_branch_a, w_branch_b, w_out, norm_ffn_g, w_gate_up, w_down, norm_final_g, loss_target, m_norm_mix_g, m_w_in, m_gmlp_ln_g, m_gmlp_ln_b, m_gmlp_w_s, m_gmlp_b_s, m_hgrn_lb_table, m_hgrn_norm_g, m_w_branch_a, m_w_branch_b, m_w_out, m_norm_ffn_g, m_w_gate_up, m_w_down, m_norm_final_g, v_norm_mix_g, v_w_in, v_gmlp_ln_g, v_gmlp_ln_b, v_gmlp_w_s, v_gmlp_b_s, v_hgrn_lb_table, v_hgrn_norm_g, v_w_branch_a, v_w_branch_b, v_w_out, v_norm_ffn_g, v_w_gate_up, v_w_down, v_norm_final_g):
    args = dict(locals())
    T = x.shape[1]
    xs = x.reshape(T, D)
    target = loss_target.reshape(T, D)
    big = {n: args[n].reshape(args[n].shape[1:]) for n in BIG}
    big_m = {n: args["m_" + n].reshape(args[n].shape[1:]) for n in BIG}
    big_v = {n: args["v_" + n].reshape(args[n].shape[1:]) for n in BIG}

    x_i, y_i, c_i = lax.axis_index("x"), lax.axis_index("y"), lax.axis_index("c")
    place = jnp.stack([2 * x_i + y_i, c_i]).astype(jnp.int32)
    cast = {n: _cast_shard("cast_" + n, place, big[n]) for n in BIG}
    tril = jnp.tril(jnp.ones((GCH, GCH), bool))
    wm = jnp.where(tril, gmlp_w_s[0], 0.0).astype(BF16)
    wm_t = jnp.swapaxes(wm, 1, 2)
    b_t = gmlp_b_s[0].T

    (proj, hb), w_in4, (w_a4,) = _proj_fwd(place, xs, norm_mix_g, cast["w_in"], [cast["w_branch_a"]])
    (ab, y_a), (w_b4, w_out4) = _gmlp_fwd(
        proj, gmlp_ln_g, gmlp_ln_b, wm, b_t, w_a4.reshape(D, D),
        job=_gather_job([cast["w_branch_b"], cast["w_out"]]))
    (o_raw, obb, st_before), (w_gu4, w_down4) = _hgrn_fwd(
        proj, hgrn_lb_table, hgrn_norm_g, job=_gather_job([cast["w_gate_up"], cast["w_down"]]))
    w_a, w_b, w_o = (w.reshape(D, D) for w in (w_a4, w_b4, w_out4))
    w_dn = w_down4.reshape(FF, D)
    y_b, mgb, x1 = _merge_fwd(xs, y_a, obb, proj, w_b, w_o)
    act, dx2b, h2b, dgu4, dx1, dx1b, acc_ffn = _ffn_fwd_bwd(
        x1, target, norm_ffn_g, norm_final_g.reshape(1, D), w_gu4, w_dn)

    grads, owns, parts, halves, sibh = {}, {}, {}, {}, {}

    def pair_sums(names, sibs):
        for n, s in zip(names, sibs):
            owns[n], parts[n] = _pair_sum("rs_pair_sum_" + n, place, grads[n], s)

    def chip_sums(names, got):
        for n, r in zip(names, got):
            halves[n] = _chip_sum("rs_chip_sum_" + n, owns[n], r)

    ffn, mix = ("w_gate_up", "w_down"), ("w_branch_a", "w_branch_b", "w_out")
    grads["w_gate_up"], _ = _dw_gate_up(h2b, dgu4)
    grads["w_down"], _ = _dw_down(act, dx2b)
    (dya, dyb, da, dob, dproj), got = _merge_bwd(
        dx1b, y_a, y_b, proj, w_o, w_a, w_b, job=_pair_exchange_job([grads[n] for n in ffn]))
    pair_sums(ffn, got)
    grads["w_branch_a"], _ = _dw_square("dw_branch_a", ab, dya)
    grads["w_branch_b"], _ = _dw_square("dw_branch_b", obb, dyb)
    grads["w_out"], _ = _dw_square("dw_out", mgb, dx1b)
    (dproj, acc_hgrn), got = _hgrn_bwd(
        dproj, dob, o_raw, proj, st_before, hgrn_lb_table, hgrn_norm_g,
        job=_join_jobs(_chip_exchange_job([parts[n] for n in ffn]), _pair_exchange_job([grads[n] for n in mix])))
    chip_sums(ffn, got[:2])
    pair_sums(mix, got[2:])
    dproj, acc_ln, dws, dmix = _gmlp_bwd(dproj, da, proj, gmlp_ln_g, gmlp_ln_b, wm, wm_t, b_t)
    (grad_x, acc_mix), got = _proj_bwd(
        dproj, w_in4, xs, dx1, norm_mix_g,
        job=_join_jobs(_share_halves_job([halves[n] for n in ffn]), _chip_exchange_job([parts[n] for n in mix])))
    sibh.update(zip(ffn, got[:2]))
    chip_sums(mix, got[2:])
    grads["w_in"], got = _dw_in(hb, dproj, job=_share_halves_job([halves[n] for n in mix]))
    sibh.update(zip(mix, got))
    pair_sums(("w_in",), _run_job(_pair_exchange_job([grads["w_in"]]), "rs_pair_exchange_w_in"))
    chip_sums(("w_in",), _run_job(_chip_exchange_job([parts["w_in"]]), "rs_chip_exchange_w_in"))
    (sibh["w_in"],) = _run_job(_share_halves_job([halves["w_in"]]), "rs_share_halves_w_in")
    out = {}
    for n in BIG:
        g, d, mn, vn = _adamw("adamw_" + n, place, big[n], halves[n], sibh[n], big_m[n], big_v[n])
        shp = args[n].shape
        out[n] = (g.reshape(shp), d.reshape(shp), mn.reshape(shp), vn.reshape(shp))

    lbv = jax.nn.sigmoid(hgrn_lb_table[0] - hgrn_lb_table[1])
    d_t0 = jnp.sum(acc_hgrn[0], axis=0) * lbv * (1.0 - lbv)
    loss_row = jnp.zeros((D,), F32).at[0].set(jnp.sum(acc_ffn[0]))
    dws_m = jnp.where(tril[:, None, :], dws.reshape(GCH, NG, GCH), 0.0).transpose(1, 0, 2)
    db_s = jnp.sum(dmix.reshape(GCH, NG, GCH), axis=-1).T
    sp = _pack_small(loss_row, jnp.sum(acc_mix, 0), jnp.sum(acc_ln[0], 0), jnp.sum(acc_ln[1], 0), db_s,
                     jnp.stack([d_t0, -d_t0]), jnp.sum(acc_hgrn[1], 0), jnp.sum(acc_ffn[2], 0),
                     jnp.sum(acc_ffn[1], 0), dws_m)
    zero = jnp.zeros((D,), F32)

    def pack(prefix):
        a = lambda n: args[prefix + n]
        return _pack_small(zero, a("norm_mix_g"), a("gmlp_ln_g"), a("gmlp_ln_b"), a("gmlp_b_s"),
                           a("hgrn_lb_table"), a("hgrn_norm_g"), a("norm_ffn_g"), a("norm_final_g"),
                           a("gmlp_w_s"))

    packed = _small_allreduce_adamw(sp, pack(""), pack("m_"), pack("v_"))
    loss = packed[0][0, 0]
    small = [_unpack_small(p) for p in packed]
    for n in SMALL:
        out[n] = tuple(s[n] for s in small)
    return (loss, grad_x.reshape(x.shape), *[out[n][0] for n in ORDER], *[out[n][1] for n in ORDER],
            *[out[n][2] for n in ORDER], *[out[n][3] for n in ORDER])
```

```python
import functools
import math

import jax
import jax.numpy as jnp
from jax import lax
from jax.experimental import pallas as pl
from jax.experimental.pallas import tpu as pltpu

F32 = jnp.float32
BF16 = jnp.bfloat16
SDS = jax.ShapeDtypeStruct
MESH = pl.DeviceIdType.MESH
ANY = pl.BlockSpec(memory_space=pl.ANY)

D = 1024
NIN = 8
NG = 8
GCH = 128
NH = 8
HD = 128
HCH = 64
HGRN_HB = 2
HW = HGRN_HB * HD
DW_TOKENS = 2048
FF = 2816
FFS = 1408
NCHIP = 4
EPS = 1e-6
QSCALE = HD ** -0.5
GELU_C0 = math.sqrt(2.0 / math.pi)
GELU_C1 = 0.044715
LR, B1, B2, AEPS, WD, STEP = 0.001, 0.9, 0.999, 1e-08, 0.01, 10
VMEM_LIMIT_V7X = 56 * 1024 * 1024
SP_ROWS = 144


def _cparams(**kw):
    return pltpu.CompilerParams(vmem_limit_bytes=VMEM_LIMIT_V7X, **kw)


def _mm(a, b):
    return jnp.dot(a, b, preferred_element_type=F32)


def _mm_nt(a, b):
    return lax.dot_general(a, b, (((1,), (1,)), ((), ())), preferred_element_type=F32)


def _mm_tn(a, b):
    return lax.dot_general(a, b, (((0,), (0,)), ((), ())), preferred_element_type=F32)


def _rows8(x):
    r, c = x.shape
    return jnp.sum(x.reshape(r // 8, 8, c), axis=0)


def _mean(x):
    return jnp.mean(x, axis=-1, keepdims=True)


def _sigmoid(x):
    return 1.0 / (1.0 + jnp.exp(-x))


def _gelu(x):
    t = jnp.tanh(GELU_C0 * (x + GELU_C1 * x * x * x))
    return 0.5 * x * (1.0 + t), t


def _gelu_grad(x, t):
    return 0.5 * (1.0 + t) + 0.5 * x * (1.0 - t * t) * (GELU_C0 * (1.0 + 3.0 * GELU_C1 * x * x))


def _orig_group(m):
    return jnp.where(m < 6, (m + 2) % 6, m)


def _proj_fwd(place, x, g_mix, w_in4, later):
    T = x.shape[0]
    tm = min(512, T)
    ni = T // tm
    n = len(later)

    def body(pc_ref, x_ref, g_ref, *rest):
        proj_ref, h_ref, w_all = rest[1 + n:4 + n]
        gathered = rest[4 + n:4 + 2 * n]
        hs, wbuf, wsem = rest[4 + 2 * n:7 + 2 * n]
        w_sems, later_sems = rest[7 + 2 * n:13 + 2 * n], rest[13 + 2 * n:]
        jp, i = pl.program_id(0), pl.program_id(1)
        s, blk = jp // 2, jp % 2

        @pl.when((jp == 0) & (i == 0))
        def _():
            _gather_start([w_all], w_sems)
            _gather_start(gathered, later_sems)

        @pl.when(jp == 0)
        def _():
            xv = x_ref[...]
            r = lax.rsqrt(_mean(xv * xv) + EPS)
            hb = (xv * r * g_ref[...]).astype(BF16)
            hs[i] = hb
            h_ref[...] = hb

        @pl.when((jp == 2) & (i == 0))
        def _():
            _gather_land([w_all], w_sems, 1)
            _gather_land([w_all], w_sems, 2)

        @pl.when((jp == 6) & (i == 0))
        def _():
            _gather_land([w_all], w_sems, 3)

        @pl.when(i == 0)
        def _():
            cols = pl.ds(pl.multiple_of(blk * D, 128), D)
            cp = pltpu.make_async_copy(w_all.at[pc_ref[0] ^ s, :, cols], wbuf, wsem)
            cp.start()
            cp.wait()

        proj_ref[...] = _mm(hs[i], wbuf[...])

        @pl.when((jp == NIN - 1) & (i == ni - 1))
        def _():
            _gather_drain([w_all], w_sems)
            _gather_finish(gathered, later_sems)

    tile = lambda jp, i, pc: (jnp.where(jp == 0, i, ni - 1), 0)
    res = pl.pallas_call(
        body, name="proj_fwd",
        grid_spec=pltpu.PrefetchScalarGridSpec(
            num_scalar_prefetch=1, grid=(NIN, ni),
            in_specs=[pl.BlockSpec((tm, D), tile), pl.BlockSpec((1, D), lambda jp, i, pc: (0, 0))] + [ANY] * (1 + n),
            out_specs=[pl.BlockSpec((tm, D), lambda jp, i, pc: (i, 2 * (pc[0] ^ (jp // 2)) + jp % 2)),
                       pl.BlockSpec((tm, D), tile)] + [ANY] * (1 + n),
            scratch_shapes=[pltpu.VMEM((ni, tm, D), BF16), pltpu.VMEM((D, D), BF16), pltpu.SemaphoreType.DMA]
            + _gather_sems(1) + _gather_sems(n)),
        out_shape=[SDS((T, NIN * D), F32), SDS((T, D), BF16), SDS(w_in4.shape, BF16)]
        + [SDS(a.shape, a.dtype) for a in later],
        input_output_aliases={3 + k: 2 + k for k in range(1 + n)},
        compiler_params=_cparams(has_side_effects=True),
    )(place, x, g_mix, w_in4, *later)
    return res[:2], res[2], res[3:]


def _layer_norm_stats(gv):
    mu = _mean(gv)
    xc = gv - mu
    rs = lax.rsqrt(_mean(xc * xc) + EPS)
    return xc * rs, rs


def _gmlp_fwd(proj, ln_g, ln_b, wm, b_t, w_a, job=None):
    T = proj.shape[0]
    tm = min(256, T)

    def body(u_ref, v_ref, lg_ref, lb_ref, wm_ref, bt_ref, wa_ref, a_ref, ya_ref, a_s):
        gu, _ = _gelu(u_ref[...])
        gv, _ = _gelu(v_ref[...])
        vhat, _ = _layer_norm_stats(gv)
        vnb = (vhat * lg_ref[...] + lb_ref[...]).astype(BF16)
        for ch in range(tm // GCH):
            rows = slice(GCH * ch, GCH * (ch + 1))
            for g in range(NG):
                cols = slice(128 * g, 128 * (g + 1))
                mixed = _mm(wm_ref[g], vnb[rows, cols]) + bt_ref[:, g:g + 1]
                a_s[rows, cols] = gu[rows, cols] * mixed
        ab = a_s[...].astype(BF16)
        a_ref[...] = ab
        ya_ref[...] = _mm(ab, wa_ref[...])

    row = lambda i: (0, 0)
    return _call(
        body, name="gmlp_fwd", grid=(T // tm,), job=job, args=(proj, proj, ln_g, ln_b, wm, b_t, w_a),
        in_specs=[pl.BlockSpec((tm, D), lambda i: (i, 0)), pl.BlockSpec((tm, D), lambda i: (i, 1)),
                  pl.BlockSpec((1, D), row), pl.BlockSpec((1, D), row),
                  pl.BlockSpec((NG, GCH, GCH), lambda i: (0, 0, 0)), pl.BlockSpec((GCH, NG), row),
                  pl.BlockSpec((D, D), row)],
        out_specs=[pl.BlockSpec((tm, D), lambda i: (i, 0)), pl.BlockSpec((tm, D), lambda i: (i, 0))],
        out_shape=[SDS((T, D), BF16), SDS((T, D), F32)],
        scratch_shapes=[pltpu.VMEM((tm, D), F32)])


def _cumsum64(x, row):
    for s in (1, 2, 4, 8, 16, 32):
        x = x + jnp.where(row >= s, pltpu.roll(x, s, 0), 0.0)
    return x


def _revcumsum64(x, row):
    n = x.shape[0]
    for s in (1, 2, 4, 8, 16, 32):
        x = x + jnp.where(row < HCH - s, pltpu.roll(x, n - s, 0), 0.0)
    return x


def _head_mean(x):
    parts = [jnp.broadcast_to(_mean(x[:, HD * h:HD * (h + 1)]), (x.shape[0], HD)) for h in range(x.shape[1] // HD)]
    return jnp.concatenate(parts, axis=1)


def _seg_sum(x):
    n, c = x.shape
    s = jnp.sum(x.reshape(n // HCH, HCH, c), axis=1, keepdims=True)
    return jnp.broadcast_to(s, (n // HCH, HCH, c)).reshape(n, c)


def _hgrn_gates(fl, lbv, row):
    s = _sigmoid(fl)
    f = lbv + (1.0 - lbv) * s
    a = _cumsum64(jnp.log(f), row)
    a_mid = _seg_sum(jnp.where(row == HCH // 2 - 1, a, 0.0))
    a_last = _seg_sum(jnp.where(row == HCH - 1, a, 0.0))
    return s, f, a, a_mid, a_last


def _hgrn_fwd(proj, lb_table, norm_g, job=None):
    T = proj.shape[0]
    tb = min(512, T)
    nc = tb // HCH

    def body(q_ref, fl_ref, v_ref, g_ref, lbt_ref, gn_ref, o_ref, ob_ref, stb_ref, st_s, o_s):
        @pl.when(pl.program_id(1) == 0)
        def _():
            st_s[...] = jnp.zeros_like(st_s)

        row = lax.broadcasted_iota(jnp.int32, (tb, HW), 0) & (HCH - 1)
        lbv = _sigmoid(lbt_ref[0:1, :] - lbt_ref[1:2, :])
        _, f, a, a_mid, a_last = _hgrn_gates(fl_ref[...], lbv, row)
        k = 1.0 - f
        qs = q_ref[...] * QSCALE
        q_in = (qs * jnp.exp(a - a_mid)).astype(BF16)
        k_in = (k * jnp.exp(a_mid - a)).astype(BF16)
        q_a = (qs * jnp.exp(a)).astype(BF16)
        k_d = (k * jnp.exp(a_last - a)).astype(BF16)
        dec = jnp.exp(a_last)
        vb = v_ref[...].astype(BF16)
        tri = (lax.broadcasted_iota(jnp.int32, (HCH, HCH), 0)
               >= lax.broadcasted_iota(jnp.int32, (HCH, HCH), 1))
        for c in range(nc):
            sl = slice(HCH * c, HCH * (c + 1))
            for hh in range(HGRN_HB):
                hs = slice(HD * hh, HD * (hh + 1))
                st = st_s[hh]
                stb_ref[hh, c] = st
                sc = jnp.where(tri, _mm_nt(q_in[sl, hs], k_in[sl, hs]), 0.0)
                o_s[sl, hs] = _mm(sc.astype(BF16), vb[sl, hs]) + _mm_nt(q_a[sl, hs], st.astype(BF16))
                d64 = dec[sl, hs]
                st_s[hh] = st * jnp.concatenate([d64, d64], axis=0) + _mm_tn(vb[sl, hs], k_d[sl, hs])
        o = o_s[...]
        r = lax.rsqrt(_head_mean(o * o) + EPS)
        g = g_ref[...]
        o_ref[...] = o
        ob_ref[...] = (o * r * gn_ref[...] * (g * _sigmoid(g))).astype(BF16)

    def col(off):
        return pl.BlockSpec((tb, HW), lambda h, cb: (cb, off * (NH // HGRN_HB) + h))

    return _call(
        body, name="hgrn_fwd", grid=(NH // HGRN_HB, T // tb), job=job,
        args=(proj, proj, proj, proj, lb_table, norm_g),
        in_specs=[col(2), col(3), col(4), col(5),
                  pl.BlockSpec((2, HW), lambda h, cb: (0, h)), pl.BlockSpec((1, HW), lambda h, cb: (0, h))],
        out_specs=[pl.BlockSpec((tb, HW), lambda h, cb: (cb, h)), pl.BlockSpec((tb, HW), lambda h, cb: (cb, h)),
                   pl.BlockSpec((HGRN_HB, nc, HD, HD), lambda h, cb: (h, cb, 0, 0))],
        out_shape=[SDS((T, D), F32), SDS((T, D), BF16), SDS((NH, T // HCH, HD, HD), F32)],
        scratch_shapes=[pltpu.VMEM((HGRN_HB, HD, HD), F32), pltpu.VMEM((tb, HW), F32)])


def _merge_fwd(x, y_a, ob, proj, w_b, w_out):
    T = x.shape[0]
    tm = min(512, T)

    def body(x_ref, ya_ref, ob_ref, ga_ref, gb_ref, wb_ref, wo_ref, yb_ref, mg_ref, x1_ref):
        yb = _mm(ob_ref[...], wb_ref[...])
        merged = (_sigmoid(ga_ref[...]) * ya_ref[...] + _sigmoid(gb_ref[...]) * yb).astype(BF16)
        yb_ref[...] = yb
        mg_ref[...] = merged
        x1_ref[...] = x_ref[...] + _mm(merged, wo_ref[...])

    t = lambda i: (i, 0)
    w = lambda i: (0, 0)
    return pl.pallas_call(
        body, name="merge_fwd", grid=(T // tm,),
        in_specs=[pl.BlockSpec((tm, D), t), pl.BlockSpec((tm, D), t), pl.BlockSpec((tm, D), t),
                  pl.BlockSpec((tm, D), lambda i: (i, 6)), pl.BlockSpec((tm, D), lambda i: (i, 7)),
                  pl.BlockSpec((D, D), w), pl.BlockSpec((D, D), w)],
        out_specs=[pl.BlockSpec((tm, D), t)] * 3,
        out_shape=[SDS((T, D), F32), SDS((T, D), BF16), SDS((T, D), F32)],
        compiler_params=_cparams(),
    )(x, y_a, ob, proj, proj, w_b, w_out)


def _ffn_fwd_bwd(x1, target, g_ffn, g_fin, w_gu4, w_down):
    T = x1.shape[0]
    tm = min(256, T)
    inv_d = 1.0 / D

    def body(x1_ref, tg_ref, gf_ref, gn_ref, wgu_ref, wd_ref,
             act_ref, dx2b_ref, h2b_ref, dgu_ref, dx1_ref, dx1b_ref, acc_ref):
        @pl.when(pl.program_id(0) == 0)
        def _():
            acc_ref[...] = jnp.zeros_like(acc_ref)

        x1v = x1_ref[...]
        gf = gf_ref[...]
        gn = gn_ref[...]
        rr1 = lax.rsqrt(_mean(x1v * x1v) + EPS)
        x1n = x1v * rr1
        h2b = (x1n * gf).astype(BF16)
        h2b_ref[...] = h2b
        p = [_mm(h2b, wgu_ref[k]) for k in range(NCHIP)]
        sg = [_sigmoid(p[0]), _sigmoid(p[1])]
        si = [p[0] * sg[0], p[1] * sg[1]]
        x2 = x1v
        for k in range(2):
            actk = (si[k] * p[2 + k]).astype(BF16)
            act_ref[:, FFS * k:FFS * (k + 1)] = actk
            x2 = x2 + _mm(actk, wd_ref[FFS * k:FFS * (k + 1), :])
        rr2 = lax.rsqrt(_mean(x2 * x2) + EPS)
        x2n = x2 * rr2
        e = x2n * gn - tg_ref[...]
        acc_ref[0] += _rows8(e * e) * (0.5 * inv_d)
        dy = e * inv_d
        acc_ref[1] += _rows8(dy * x2n)
        dxn = dy * gn
        dx2 = rr2 * (dxn - x2n * _mean(dxn * x2n))
        dx2b = dx2.astype(BF16)
        dx2b_ref[...] = dx2b
        dh2 = None
        for k in range(2):
            dact = _mm_nt(dx2b, wd_ref[FFS * k:FFS * (k + 1), :])
            dgate = (dact * p[2 + k] * (sg[k] * (1.0 + p[k] * (1.0 - sg[k])))).astype(BF16)
            dup = (dact * si[k]).astype(BF16)
            dgu_ref[k] = dgate
            dgu_ref[2 + k] = dup
            part = _mm_nt(dgate, wgu_ref[k]) + _mm_nt(dup, wgu_ref[2 + k])
            dh2 = part if dh2 is None else dh2 + part
        acc_ref[2] += _rows8(dh2 * x1n)
        dxn1 = dh2 * gf
        dx1 = dx2 + rr1 * (dxn1 - x1n * _mean(dxn1 * x1n))
        dx1_ref[...] = dx1
        dx1b_ref[...] = dx1.astype(BF16)

    t = lambda i: (i, 0)
    w = lambda i: (0, 0)
    one = pl.Buffered(1)
    return pl.pallas_call(
        body, name="ffn_fwd_bwd", grid=(T // tm,),
        in_specs=[pl.BlockSpec((tm, D), t), pl.BlockSpec((tm, D), t),
                  pl.BlockSpec((1, D), w), pl.BlockSpec((1, D), w),
                  pl.BlockSpec((NCHIP, D, FFS), lambda i: (0, 0, 0), pipeline_mode=one),
                  pl.BlockSpec((FF, D), w, pipeline_mode=one)],
        out_specs=[pl.BlockSpec((tm, FF), t), pl.BlockSpec((tm, D), t), pl.BlockSpec((tm, D), t),
                   pl.BlockSpec((NCHIP, tm, FFS), lambda i: (0, i, 0)),
                   pl.BlockSpec((tm, D), t), pl.BlockSpec((tm, D), t),
                   pl.BlockSpec((3, 8, D), lambda i: (0, 0, 0))],
        out_shape=[SDS((T, FF), BF16), SDS((T, D), BF16), SDS((T, D), BF16),
                   SDS((NCHIP, T, FFS), BF16), SDS((T, D), F32), SDS((T, D), BF16),
                   SDS((3, 8, D), F32)],
        compiler_params=_cparams(),
    )(x1, target, g_ffn, g_fin, w_gu4, w_down)


def _merge_bwd(dx1b, y_a, y_b, proj, w_out, w_a, w_b, job=None):
    T = dx1b.shape[0]
    tm = min(512, T)

    def body(dx_ref, ya_ref, yb_ref, ga_ref, gb_ref, wo_ref, wa_ref, wb_ref,
             dya_ref, dyb_ref, da_ref, dob_ref, dp_ref):
        dm = _mm_nt(dx_ref[...], wo_ref[...])
        sa = _sigmoid(ga_ref[...])
        sb = _sigmoid(gb_ref[...])
        dya = (dm * sa).astype(BF16)
        dyb = (dm * sb).astype(BF16)
        dya_ref[...] = dya
        dyb_ref[...] = dyb
        dp_ref[0] = (dm * ya_ref[...] * sa * (1.0 - sa)).astype(BF16)
        dp_ref[1] = (dm * yb_ref[...] * sb * (1.0 - sb)).astype(BF16)
        da_ref[...] = _mm_nt(dya, wa_ref[...])
        dob_ref[...] = _mm_nt(dyb, wb_ref[...])

    t = lambda i: (i, 0)
    w = lambda i: (0, 0)
    return _call(
        body, name="merge_bwd", grid=(T // tm,),
        in_specs=[pl.BlockSpec((tm, D), t), pl.BlockSpec((tm, D), t), pl.BlockSpec((tm, D), t),
                  pl.BlockSpec((tm, D), lambda i: (i, 6)), pl.BlockSpec((tm, D), lambda i: (i, 7)),
                  pl.BlockSpec((D, D), w), pl.BlockSpec((D, D), w), pl.BlockSpec((D, D), w)],
        out_specs=[pl.BlockSpec((tm, D), t)] * 4 + [pl.BlockSpec((2, tm, D), lambda i: (3, i, 0))],
        out_shape=[SDS((T, D), BF16), SDS((T, D), BF16), SDS((T, D), F32), SDS((T, D), F32),
                   SDS((NIN, T, D), BF16)],
        args=(dx1b, y_a, y_b, proj, proj, w_out, w_a, w_b), job=job)


def _hgrn_bwd(dproj, dob, o_raw, proj, st_before, lb_table, norm_g, job=None):
    T = dob.shape[0]
    tb = min(512, T)
    nc = tb // HCH
    nb = T // tb

    def body(dp_in, dob_ref, o_ref, q_ref, fl_ref, v_ref, g_ref, stb_ref, lbt_ref, gn_ref,
             dp_ref, acc_ref, dst_s, dqin_s, dqa_s, dkin_s, dkd_s, dv_s, ddec_s):
        del dp_in

        @pl.when(pl.program_id(1) == 0)
        def _():
            dst_s[...] = jnp.zeros_like(dst_s)
            acc_ref[...] = jnp.zeros_like(acc_ref)

        row = lax.broadcasted_iota(jnp.int32, (tb, HW), 0) & (HCH - 1)
        gn = gn_ref[...]
        lbv = _sigmoid(lbt_ref[0:1, :] - lbt_ref[1:2, :])
        o = o_ref[...]
        r = lax.rsqrt(_head_mean(o * o) + EPS)
        on = o * r
        g = g_ref[...]
        sgm = _sigmoid(g)
        dob_v = dob_ref[...]
        dp_ref[3] = (dob_v * on * gn * (sgm * (1.0 + g * (1.0 - sgm)))).astype(BF16)
        do_n = dob_v * (g * sgm)
        acc_ref[1] += _rows8(do_n * on)
        dxn = do_n * gn
        do = (r * (dxn - on * _head_mean(dxn * on))).astype(BF16)
        s, f, a, a_mid, a_last = _hgrn_gates(fl_ref[...], lbv, row)
        k = 1.0 - f
        qs = q_ref[...] * QSCALE
        e_q = jnp.exp(a - a_mid)
        e_k = jnp.exp(a_mid - a)
        e_a = jnp.exp(a)
        e_l = jnp.exp(a_last - a)
        dec = jnp.exp(a_last)
        q_in = qs * e_q
        k_in = k * e_k
        q_a = qs * e_a
        k_d = k * e_l
        q_inb, k_inb, q_ab, k_db = (z.astype(BF16) for z in (q_in, k_in, q_a, k_d))
        vb = v_ref[...].astype(BF16)
        tri = (lax.broadcasted_iota(jnp.int32, (HCH, HCH), 0)
               >= lax.broadcasted_iota(jnp.int32, (HCH, HCH), 1))
        for c in reversed(range(nc)):
            sl = slice(HCH * c, HCH * (c + 1))
            for hh in range(HGRN_HB):
                hs = slice(HD * hh, HD * (hh + 1))
                stp = stb_ref[hh, c]
                dst = dst_s[hh]
                dstb = dst.astype(BF16)
                do_c = do[sl, hs]
                v_c = vb[sl, hs]
                dqa_s[sl, hs] = _mm(do_c, stp.astype(BF16))
                dkd_s[sl, hs] = _mm(v_c, dstb)
                ddec_s[sl, hs] = jnp.broadcast_to(jnp.sum(dst * stp, axis=0, keepdims=True), (HCH, HD))
                sc = jnp.where(tri, _mm_nt(q_inb[sl, hs], k_inb[sl, hs]), 0.0).astype(BF16)
                dsc = jnp.where(tri, _mm_nt(do_c, v_c), 0.0).astype(BF16)
                dv_s[sl, hs] = _mm_nt(k_db[sl, hs], dstb) + _mm_tn(sc, do_c)
                dqin_s[sl, hs] = _mm(dsc, k_inb[sl, hs])
                dkin_s[sl, hs] = _mm_tn(dsc, q_inb[sl, hs])
                d64 = dec[sl, hs]
                dst_s[hh] = dst * jnp.concatenate([d64, d64], axis=0) + _mm_tn(do_c, q_ab[sl, hs])
        dq_in = dqin_s[...]
        dq_a = dqa_s[...]
        dk_in = dkin_s[...]
        dk_d = dkd_s[...]
        dp_ref[0] = ((dq_in * e_q + dq_a * e_a) * QSCALE).astype(BF16)
        dp_ref[2] = dv_s[...].astype(BF16)
        tq = dq_in * q_in
        tk = dk_in * k_in
        td = dk_d * k_d
        d_a = tq + dq_a * q_a - tk - td
        d_a = d_a + jnp.where(row == HCH // 2 - 1, _seg_sum(tk - tq), 0.0)
        d_a = d_a + jnp.where(row == HCH - 1, _seg_sum(td) + ddec_s[...] * dec, 0.0)
        dlf = _revcumsum64(d_a, row)
        df = dlf / f - (dk_in * e_k + dk_d * e_l)
        dp_ref[1] = (df * (1.0 - lbv) * s * (1.0 - s)).astype(BF16)
        acc_ref[0] += _rows8(df * (1.0 - s))

    def col(off):
        return pl.BlockSpec((tb, HW), lambda h, cb: (nb - 1 - cb, off * (NH // HGRN_HB) + h))

    hb = lambda h, cb: (nb - 1 - cb, h)
    return _call(
        body, name="hgrn_bwd", grid=(NH // HGRN_HB, nb), job=job,
        args=(dproj, dob, o_raw, proj, proj, proj, proj, st_before, lb_table, norm_g),
        in_specs=[ANY, pl.BlockSpec((tb, HW), hb), pl.BlockSpec((tb, HW), hb),
                  col(2), col(3), col(4), col(5),
                  pl.BlockSpec((HGRN_HB, nc, HD, HD), lambda h, cb: (h, nb - 1 - cb, 0, 0)),
                  pl.BlockSpec((2, HW), lambda h, cb: (0, h)), pl.BlockSpec((1, HW), lambda h, cb: (0, h))],
        out_specs=[pl.BlockSpec((4, tb, HW), lambda h, cb: (0, nb - 1 - cb, h)),
                   pl.BlockSpec((2, 8, HW), lambda h, cb: (0, 0, h))],
        out_shape=[SDS(dproj.shape, BF16), SDS((2, 8, D), F32)],
        scratch_shapes=[pltpu.VMEM((HGRN_HB, HD, HD), F32)] + [pltpu.VMEM((tb, HW), F32)] * 6,
        aliases={0: 0})


def _gmlp_bwd(dproj, da, proj, ln_g, ln_b, wm, wm_t, b_t):
    T = da.shape[0]
    tm = min(256, T)

    def body(dp_in, da_ref, u_ref, v_ref, lg_ref, lb_ref, wm_ref, wmt_ref, bt_ref,
             dp_ref, acc_ref, dws_ref, dmix_ref, du_s, dvn_s):
        del dp_in

        @pl.when(pl.program_id(0) == 0)
        def _():
            acc_ref[...] = jnp.zeros_like(acc_ref)
            dws_ref[...] = jnp.zeros_like(dws_ref)
            dmix_ref[...] = jnp.zeros_like(dmix_ref)

        u = u_ref[...]
        v = v_ref[...]
        lg = lg_ref[...]
        gu, t_u = _gelu(u)
        gv, t_v = _gelu(v)
        vhat, rs = _layer_norm_stats(gv)
        vnb = (vhat * lg + lb_ref[...]).astype(BF16)
        da_v = da_ref[...]
        for ch in range(tm // GCH):
            rows = slice(GCH * ch, GCH * (ch + 1))
            for g in range(NG):
                cols = slice(128 * g, 128 * (g + 1))
                vng = vnb[rows, cols]
                mixed = _mm(wm_ref[g], vng) + bt_ref[:, g:g + 1]
                dag = da_v[rows, cols]
                dmx = dag * gu[rows, cols]
                du_s[rows, cols] = dag * mixed
                dmxb = dmx.astype(BF16)
                dws_ref[:, cols] += _mm_nt(dmxb, vng)
                dmix_ref[:, cols] += dmx
                dvn_s[rows, cols] = _mm(wmt_ref[g], dmxb)
        dp_ref[0] = (du_s[...] * _gelu_grad(u, t_u)).astype(BF16)
        dvn = dvn_s[...]
        acc_ref[0] += _rows8(dvn * vhat)
        acc_ref[1] += _rows8(dvn)
        dvh = dvn * lg
        dgv = rs * (dvh - _mean(dvh) - vhat * _mean(dvh * vhat))
        dp_ref[1] = (dgv * _gelu_grad(v, t_v)).astype(BF16)

    row = lambda i: (0, 0)
    w3 = lambda i: (0, 0, 0)
    return pl.pallas_call(
        body, name="gmlp_bwd", grid=(T // tm,),
        in_specs=[ANY, pl.BlockSpec((tm, D), lambda i: (i, 0)),
                  pl.BlockSpec((tm, D), lambda i: (i, 0)), pl.BlockSpec((tm, D), lambda i: (i, 1)),
                  pl.BlockSpec((1, D), row), pl.BlockSpec((1, D), row),
                  pl.BlockSpec((NG, GCH, GCH), w3), pl.BlockSpec((NG, GCH, GCH), w3),
                  pl.BlockSpec((GCH, NG), row)],
        out_specs=[pl.BlockSpec((2, tm, D), lambda i: (2, i, 0)),
                   pl.BlockSpec((2, 8, D), w3), pl.BlockSpec((GCH, D), row), pl.BlockSpec((GCH, D), row)],
        out_shape=[SDS(dproj.shape, BF16), SDS((2, 8, D), F32), SDS((GCH, D), F32), SDS((GCH, D), F32)],
        scratch_shapes=[pltpu.VMEM((tm, D), F32), pltpu.VMEM((tm, D), F32)],
        input_output_aliases={0: 0},
        compiler_params=_cparams(),
    )(dproj, da, proj, proj, ln_g, ln_b, wm, wm_t, b_t)


def _proj_bwd(dproj, w_in4, x, dx1, g_mix, job=None):
    T = x.shape[0]
    tm = min(256, T)
    order = (2, 3, 4, 5, 0, 1, 6, 7)

    def body(dp_ref, w_ref, x_ref, dx1_ref, g_ref, gx_ref, acc_ref):
        @pl.when(pl.program_id(0) == 0)
        def _():
            acc_ref[...] = jnp.zeros_like(acc_ref)

        dh = None
        for m, og in enumerate(order):
            part = _mm_nt(dp_ref[m], w_ref[og // 2, :, D * (og % 2):D * (og % 2 + 1)])
            dh = part if dh is None else dh + part
        xv = x_ref[...]
        r = lax.rsqrt(_mean(xv * xv) + EPS)
        xn = xv * r
        acc_ref[...] += _rows8(dh * xn)
        dxn = dh * g_ref[...]
        gx_ref[...] = dx1_ref[...] + r * (dxn - xn * _mean(dxn * xn))

    t = lambda i: (i, 0)
    return _call(
        body, name="proj_bwd", grid=(T // tm,),
        in_specs=[pl.BlockSpec((NIN, tm, D), lambda i: (0, i, 0)),
                  pl.BlockSpec((NCHIP, D, 2 * D), lambda i: (0, 0, 0), pipeline_mode=pl.Buffered(1)),
                  pl.BlockSpec((tm, D), t), pl.BlockSpec((tm, D), t), pl.BlockSpec((1, D), lambda i: (0, 0))],
        out_specs=[pl.BlockSpec((tm, D), t), pl.BlockSpec((8, D), lambda i: (0, 0))],
        out_shape=[SDS((T, D), F32), SDS((8, D), F32)],
        args=(dproj, w_in4, x, dx1, g_mix), job=job)


def _dw_call(name, a, b, a_spec, b_spec, o_spec, out_shape, nblk, tt, job=None):
    T = a.shape[-2]

    def body(a_ref, b_ref, o_ref):
        @pl.when(pl.program_id(1) == 0)
        def _():
            o_ref[...] = jnp.zeros_like(o_ref)
        o_ref[...] += _mm_tn(a_ref[...], b_ref[...])

    (out,), job_out = _call(
        body, name=name, grid=(nblk, T // tt), in_specs=[a_spec, b_spec], out_specs=[o_spec],
        out_shape=[out_shape], args=(a, b), job=job)
    return out, job_out


def _dw_in(hb, dproj, job=None):
    tt = min(DW_TOKENS, hb.shape[0])
    return _dw_call(
        "dw_in", hb, dproj,
        pl.BlockSpec((tt, D), lambda m, t: (t, 0)),
        pl.BlockSpec((None, tt, D), lambda m, t: (m, t, 0)),
        pl.BlockSpec((None, D, D), lambda m, t: (_orig_group(m) // 2, 0, _orig_group(m) % 2)),
        SDS((NCHIP, D, 2 * D), F32), NIN, tt, job)


def _dw_gate_up(h2b, dgu4, job=None):
    tt = min(DW_TOKENS, h2b.shape[0])
    return _dw_call(
        "dw_gate_up", h2b, dgu4,
        pl.BlockSpec((tt, D), lambda k, t: (t, 0)),
        pl.BlockSpec((None, tt, FFS), lambda k, t: (k, t, 0)),
        pl.BlockSpec((None, D, FFS), lambda k, t: (k, 0, 0)),
        SDS((NCHIP, D, FFS), F32), NCHIP, tt, job)


def _dw_down(act, dx2b, job=None):
    tt = min(DW_TOKENS, act.shape[0])
    g, job_out = _dw_call(
        "dw_down", act, dx2b,
        pl.BlockSpec((tt, FFS), lambda k, t: (t, k)),
        pl.BlockSpec((tt, D), lambda k, t: (t, 0)),
        pl.BlockSpec((FFS, D), lambda k, t: (k, 0)),
        SDS((FF, D), F32), 2, tt, job)
    return g.reshape(NCHIP, FF // NCHIP, D), job_out


def _dw_square(name, a, b, job=None):
    tt = min(DW_TOKENS, a.shape[0])
    g, job_out = _dw_call(
        name, a, b,
        pl.BlockSpec((tt, D), lambda k, t: (t, 0)), pl.BlockSpec((tt, D), lambda k, t: (t, 0)),
        pl.BlockSpec((D, D), lambda k, t: (0, 0)), SDS((D, D), F32), 1, tt, job)
    return g.reshape(NCHIP, D // NCHIP, D), job_out


def _place():
    x, y, c = lax.axis_index("x"), lax.axis_index("y"), lax.axis_index("c")
    return x, y, c, 2 * x + y


def _chip_at(x, y, s):
    return x ^ (s >> 1), y ^ (s & 1)


class _Job:
    def __init__(self, ins, out_shapes, sems, start, finish, aliases=None):
        self.ins, self.out_shapes, self.sems = list(ins), list(out_shapes), list(sems)
        self.start, self.finish, self.aliases = start, finish, dict(aliases or {})


def _join_jobs(*jobs):
    def cut(refs, sizes):
        out, at = [], 0
        for n in sizes:
            out.append(refs[at:at + n])
            at += n
        return out

    ni = [len(j.ins) for j in jobs]
    no = [len(j.out_shapes) for j in jobs]
    ns = [len(j.sems) for j in jobs]

    def run(which):
        def go(ins, outs, sems):
            for j, a, b, c in zip(jobs, cut(ins, ni), cut(outs, no), cut(sems, ns)):
                getattr(j, which)(a, b, c)
        return go

    aliases = {}
    for k, j in enumerate(jobs):
        for a, b in j.aliases.items():
            aliases[sum(ni[:k]) + a] = sum(no[:k]) + b
    return _Job([a for j in jobs for a in j.ins], [o for j in jobs for o in j.out_shapes],
                [s for j in jobs for s in j.sems], run("start"), run("finish"), aliases)


def _call(body, *, name, grid, in_specs, out_specs, out_shape, args, scratch_shapes=(), aliases=None, job=None):
    n_in, n_out, n_scr = len(in_specs), len(out_specs), len(scratch_shapes)
    aliases = dict(aliases or {})
    if job is None:
        res = pl.pallas_call(
            body, name=name, grid=grid, in_specs=list(in_specs), out_specs=list(out_specs),
            out_shape=list(out_shape), scratch_shapes=list(scratch_shapes), input_output_aliases=aliases,
            compiler_params=_cparams())(*args)
        return list(res), []
    ji, jo = len(job.ins), len(job.out_shapes)

    def wrapped(*refs):
        ins, jin = refs[:n_in], refs[n_in:n_in + ji]
        o0 = n_in + ji
        outs, jout = refs[o0:o0 + n_out], refs[o0 + n_out:o0 + n_out + jo]
        s0 = o0 + n_out + jo
        scr, jsem = refs[s0:s0 + n_scr], refs[s0 + n_scr:]
        ids = [pl.program_id(a) for a in range(len(grid))]
        first = functools.reduce(jnp.logical_and, [i == 0 for i in ids])
        last = functools.reduce(jnp.logical_and, [i == g - 1 for i, g in zip(ids, grid)])

        @pl.when(first)
        def _():
            job.start(jin, jout, jsem)

        body(*ins, *outs, *scr)

        @pl.when(last)
        def _():
            job.finish(jin, jout, jsem)

    for a, b in job.aliases.items():
        aliases[n_in + a] = n_out + b
    res = pl.pallas_call(
        wrapped, name=name, grid=grid, in_specs=list(in_specs) + [ANY] * ji,
        out_specs=list(out_specs) + [ANY] * jo, out_shape=list(out_shape) + job.out_shapes,
        scratch_shapes=list(scratch_shapes) + job.sems, input_output_aliases=aliases,
        compiler_params=_cparams(has_side_effects=True))(*args, *job.ins)
    return list(res[:n_out]), list(res[n_out:])


def _run_job(job, name):
    ji, jo = len(job.ins), len(job.out_shapes)

    def body(*refs):
        jin, jout, jsem = refs[:ji], refs[ji:ji + jo], refs[ji + jo:]
        job.start(jin, jout, jsem)
        job.finish(jin, jout, jsem)

    return list(pl.pallas_call(
        body, name=name, in_specs=[ANY] * ji, out_specs=[ANY] * jo, out_shape=job.out_shapes,
        scratch_shapes=job.sems, input_output_aliases=job.aliases,
        compiler_params=pltpu.CompilerParams(has_side_effects=True))(*job.ins))


def _cast_shard(name, place, w):
    rows, cols = w.shape
    tr = 352 if rows % 352 == 0 else 256

    def body(pc_ref, w_ref, o_ref):
        del pc_ref
        o_ref[...] = w_ref[...].astype(BF16)

    return pl.pallas_call(
        body, name=name,
        grid_spec=pltpu.PrefetchScalarGridSpec(
            num_scalar_prefetch=1, grid=(rows // tr,),
            in_specs=[pl.BlockSpec((tr, cols), lambda i, pc: (i, 0))],
            out_specs=pl.BlockSpec((None, tr, cols), lambda i, pc: (pc[0], i, 0))),
        out_shape=SDS((NCHIP, rows, cols), BF16),
        compiler_params=_cparams(),
    )(place, w)


def _sibling_copy(ref, send_sem, recv_sem):
    x, y, c, _ = _place()
    return pltpu.make_async_remote_copy(src_ref=ref, dst_ref=ref, send_sem=send_sem, recv_sem=recv_sem,
                                        device_id=(x, y, 1 - c), device_id_type=MESH)


def _half_rows(arr, slot, core):
    half = arr.shape[1] // 2
    return arr.at[slot, pl.ds(pl.multiple_of(core * half, 16), half)]


def _quarter_rows(arr, slot, core, q):
    quarter = arr.shape[1] // 4
    return arr.at[slot, pl.ds(pl.multiple_of((2 * core + q) * quarter, 16), quarter)]


def _chip_copy(ref, dist, send_sem, recv_sem):
    x, y, c, _ = _place()
    cx, cy = _chip_at(x, y, dist)
    return pltpu.make_async_remote_copy(src_ref=ref, dst_ref=ref, send_sem=send_sem, recv_sem=recv_sem,
                                        device_id=(cx, cy, c), device_id_type=MESH)


def _gather_sems(n):
    dma = pltpu.SemaphoreType.DMA
    return [dma((n, 2))] * 4 + [dma((n, 4))] * 2


def _gather_start(arrs, sems):
    dsend, drecv = sems[0], sems[1]
    _, _, c, j = _place()
    for w, arr in enumerate(arrs):
        for dist in (1, 2):
            _chip_copy(_half_rows(arr, j, c), dist, dsend.at[w, dist - 1], drecv.at[w, dist - 1]).start()


def _gather_land(arrs, sems, dist):
    dsend, drecv, rsend, rrecv, fsend, frecv = sems
    _, _, c, j = _place()
    if dist < 3:
        other = 3 - dist
        for w, arr in enumerate(arrs):
            landed = _half_rows(arr, j ^ dist, c)
            _chip_copy(landed, dist, dsend.at[w, dist - 1], drecv.at[w, dist - 1]).wait_recv()
            relay = _quarter_rows(arr, j ^ dist, c, other - 1)
            _chip_copy(relay, other, rsend.at[w, other - 1], rrecv.at[w, other - 1]).start()
            _sibling_copy(landed, fsend.at[w, dist - 1], frecv.at[w, dist - 1]).start()
        for w, arr in enumerate(arrs):
            theirs = _half_rows(arr, j ^ dist, 1 - c)
            _sibling_copy(theirs, fsend.at[w, dist - 1], frecv.at[w, dist - 1]).wait_recv()
    else:
        for w, arr in enumerate(arrs):
            for via in (1, 2):
                piece = _quarter_rows(arr, j ^ 3, c, via - 1)
                _chip_copy(piece, via, rsend.at[w, via - 1], rrecv.at[w, via - 1]).wait_recv()
                _sibling_copy(piece, fsend.at[w, 1 + via], frecv.at[w, 1 + via]).start()
        for w, arr in enumerate(arrs):
            for via in (1, 2):
                theirs = _quarter_rows(arr, j ^ 3, 1 - c, via - 1)
                _sibling_copy(theirs, fsend.at[w, 1 + via], frecv.at[w, 1 + via]).wait_recv()


def _gather_drain(arrs, sems):
    dsend, drecv, rsend, rrecv, fsend, frecv = sems
    _, _, c, j = _place()
    for w, arr in enumerate(arrs):
        for dist in (1, 2):
            other = 3 - dist
            _chip_copy(_half_rows(arr, j, c), dist, dsend.at[w, dist - 1], drecv.at[w, dist - 1]).wait_send()
            _chip_copy(_quarter_rows(arr, j ^ dist, c, other - 1), other,
                       rsend.at[w, other - 1], rrecv.at[w, other - 1]).wait_send()
            _sibling_copy(_half_rows(arr, j ^ dist, c), fsend.at[w, dist - 1], frecv.at[w, dist - 1]).wait_send()
            _sibling_copy(_quarter_rows(arr, j ^ 3, c, dist - 1),
                          fsend.at[w, 1 + dist], frecv.at[w, 1 + dist]).wait_send()


def _gather_finish(arrs, sems):
    for dist in (1, 2, 3):
        _gather_land(arrs, sems, dist)
    _gather_drain(arrs, sems)


def _gather_job(arrs):
    n = len(arrs)
    return _Job(arrs, [SDS(a.shape, a.dtype) for a in arrs], _gather_sems(n),
                lambda ins, outs, sems: _gather_start(outs, sems),
                lambda ins, outs, sems: _gather_finish(outs, sems), {k: k for k in range(n)})


def _exchange_job(arrs, out_shapes, n, copies):
    def start(ins, outs, sems):
        for cp in copies(ins, outs, sems[0], sems[1]):
            cp.start()

    def finish(ins, outs, sems):
        for cp in copies(ins, outs, sems[0], sems[1]):
            cp.wait()

    return _Job(arrs, out_shapes, [pltpu.SemaphoreType.DMA((n,))] * 2, start, finish)


def _pair_exchange_job(grads):
    def copies(ins, outs, send_sem, recv_sem):
        x, y, c, _ = _place()
        res = []
        for w in range(len(grads)):
            half = ins[w].shape[1] // 2
            theirs = pl.ds(pl.multiple_of((1 - c) * half, 8), half)
            res.append(pltpu.make_async_remote_copy(
                src_ref=ins[w].at[:, theirs, :], dst_ref=outs[w], send_sem=send_sem.at[w],
                recv_sem=recv_sem.at[w], device_id=(x, y, 1 - c), device_id_type=MESH))
        return res

    return _exchange_job(grads, [SDS((NCHIP, g.shape[1] // 2, g.shape[2]), F32) for g in grads],
                         len(grads), copies)


def _row_tile(rows):
    return 176 if rows % 176 == 0 and rows % 128 else 128


def _pair_sum(name, place, g, sib):
    half, cols = sib.shape[1], sib.shape[2]
    tr = _row_tile(half)
    nt = half // tr

    def body(pc_ref, g_ref, s_ref, own_ref, out_ref):
        del pc_ref
        v = g_ref[...] + s_ref[...]
        out_ref[...] = v.astype(BF16)

        @pl.when(pl.program_id(1) == 0)
        def _():
            own_ref[...] = v

    return pl.pallas_call(
        body, name=name,
        grid_spec=pltpu.PrefetchScalarGridSpec(
            num_scalar_prefetch=1, grid=(nt, NCHIP),
            in_specs=[pl.BlockSpec((None, tr, cols), lambda i, s, pc: (pc[0] ^ s, pc[1] * nt + i, 0)),
                      pl.BlockSpec((None, tr, cols), lambda i, s, pc: (pc[0] ^ s, i, 0))],
            out_specs=[pl.BlockSpec((tr, cols), lambda i, s, pc: (i, 0)),
                       pl.BlockSpec((None, tr, cols), lambda i, s, pc: (s, i, 0))]),
        out_shape=[SDS((half, cols), F32), SDS((NCHIP, half, cols), BF16)],
        compiler_params=_cparams(),
    )(place, g, sib)


def _chip_exchange_job(parts):
    def copies(ins, outs, send_sem, recv_sem):
        x, y, c, _ = _place()
        res = []
        for w in range(len(parts)):
            for s in range(1, NCHIP):
                cx, cy = _chip_at(x, y, s)
                k = w * (NCHIP - 1) + s - 1
                res.append(pltpu.make_async_remote_copy(
                    src_ref=ins[w].at[s], dst_ref=outs[w].at[s - 1], send_sem=send_sem.at[k],
                    recv_sem=recv_sem.at[k], device_id=(cx, cy, c), device_id_type=MESH))
        return res

    return _exchange_job(parts, [SDS((NCHIP - 1,) + p.shape[1:], BF16) for p in parts],
                         len(parts) * (NCHIP - 1), copies)


def _chip_sum(name, own, rem):
    half, cols = own.shape
    tr = _row_tile(half)

    def body(own_ref, rem_ref, out_ref):
        out_ref[...] = ((own_ref[...] + rem_ref[0].astype(F32)) + rem_ref[1].astype(F32)) + rem_ref[2].astype(F32)

    return pl.pallas_call(
        body, name=name, grid=(half // tr,),
        in_specs=[pl.BlockSpec((tr, cols), lambda i: (i, 0)),
                  pl.BlockSpec((NCHIP - 1, tr, cols), lambda i: (0, i, 0))],
        out_specs=pl.BlockSpec((tr, cols), lambda i: (i, 0)),
        out_shape=SDS((half, cols), F32),
        compiler_params=_cparams(),
    )(own, rem)


def _share_halves_job(halves):
    def copies(ins, outs, send_sem, recv_sem):
        x, y, c, _ = _place()
        return [pltpu.make_async_remote_copy(
            src_ref=ins[w], dst_ref=outs[w], send_sem=send_sem.at[w], recv_sem=recv_sem.at[w],
            device_id=(x, y, 1 - c), device_id_type=MESH) for w in range(len(halves))]

    return _exchange_job(halves, [SDS(h.shape, F32) for h in halves], len(halves), copies)


def _adamw_math(w, g, m, v):
    m = B1 * m + (1.0 - B1) * g
    v = B2 * v + (1.0 - B2) * (g * g)
    m_hat = m / (1.0 - B1 ** STEP)
    v_hat = v / (1.0 - B2 ** STEP)
    delta = -LR * (m_hat / (jnp.sqrt(v_hat) + AEPS) + WD * w)
    return delta, m, v


def _adamw(name, place, w, own, sib, m, v):
    rows, cols = w.shape
    half = rows // 2
    tr = 352 if half % 352 == 0 else min(256, half)
    nt = half // tr

    def body(pc_ref, w_ref, own_ref, sib_ref, m_ref, v_ref, g_ref, d_ref, mo_ref, vo_ref):
        g = jnp.where(pl.program_id(0) == pc_ref[1], own_ref[...], sib_ref[...])
        d, mn, vn = _adamw_math(w_ref[...], g, m_ref[...], v_ref[...])
        g_ref[...] = g
        d_ref[...] = d
        mo_ref[...] = mn
        vo_ref[...] = vn

    full = pl.BlockSpec((tr, cols), lambda h, i, pc: (h * nt + i, 0))
    part = pl.BlockSpec((tr, cols), lambda h, i, pc: (i, 0))
    return pl.pallas_call(
        body, name=name,
        grid_spec=pltpu.PrefetchScalarGridSpec(
            num_scalar_prefetch=1, grid=(2, nt),
            in_specs=[full, part, part, full, full], out_specs=[full] * 4),
        out_shape=[SDS((rows, cols), F32)] * 4,
        compiler_params=_cparams(),
    )(place, w, own, sib, m, v)


def _small_allreduce_adamw(sp, w, m, v):
    shape = sp.shape

    def body(sp_ref, w_ref, m_ref, v_ref, g_ref, d_ref, mo_ref, vo_ref,
             sib_s, pair_s, chip_s, send_sem, recv_sem):
        x, y, c, j = _place()
        cp = pltpu.make_async_remote_copy(
            src_ref=sp_ref, dst_ref=sib_s, send_sem=send_sem.at[0], recv_sem=recv_sem.at[0],
            device_id=(x, y, 1 - c), device_id_type=MESH)
        cp.start()
        cp.wait()
        pair_s[...] = sp_ref[...] + sib_s[...]
        cps = []
        for s in range(1, NCHIP):
            cx, cy = _chip_at(x, y, s)
            cp = pltpu.make_async_remote_copy(
                src_ref=pair_s, dst_ref=chip_s.at[s], send_sem=send_sem.at[s], recv_sem=recv_sem.at[s],
                device_id=(cx, cy, c), device_id_type=MESH)
            cp.start()
            cps.append(cp)
        chip_s[0] = pair_s[...]
        for cp in cps:
            cp.wait()
        tot = chip_s[j]
        for k in range(1, NCHIP):
            tot = tot + chip_s[k ^ j]
        g_ref[...] = tot
        d, mn, vn = _adamw_math(w_ref[...], tot, m_ref[...], v_ref[...])
        d_ref[...] = d
        mo_ref[...] = mn
        vo_ref[...] = vn

    vm = pl.BlockSpec(memory_space=pltpu.VMEM)
    return pl.pallas_call(
        body, name="small_allreduce_adamw",
        in_specs=[vm] * 4, out_specs=[vm] * 4, out_shape=[SDS(shape, F32)] * 4,
        scratch_shapes=[pltpu.VMEM(shape, F32), pltpu.VMEM(shape, F32), pltpu.VMEM((NCHIP,) + shape, F32),
                        pltpu.SemaphoreType.DMA((NCHIP,)), pltpu.SemaphoreType.DMA((NCHIP,))],
        compiler_params=pltpu.CompilerParams(has_side_effects=True),
    )(sp, w, m, v)


def _pack_small(first, mix, ln_g, ln_b, b_s, lbt, hn, ffn, fin, w_s):
    rows = [first.reshape(1, D), mix.reshape(1, D), ln_g.reshape(1, D), ln_b.reshape(1, D),
            b_s.reshape(1, D), lbt.reshape(2, D), hn.reshape(1, D), ffn.reshape(1, D), fin.reshape(1, D),
            jnp.zeros((6, D), F32)]
    return jnp.concatenate(rows + [w_s.reshape(NG, GCH, GCH).transpose(1, 0, 2).reshape(GCH, D)], axis=0)


def _unpack_small(p):
    w_s = p[16:].reshape(GCH, NG, GCH).transpose(1, 0, 2).reshape(1, NG, GCH, GCH)
    return dict(norm_mix_g=p[1:2], gmlp_ln_g=p[2:3], gmlp_ln_b=p[3:4], gmlp_b_s=p[4].reshape(1, NG, GCH),
                hgrn_lb_table=p[5:7], hgrn_norm_g=p[7:8], norm_ffn_g=p[8:9], norm_final_g=p[9],
                gmlp_w_s=w_s)


SMALL = ("norm_mix_g", "gmlp_ln_g", "gmlp_ln_b", "gmlp_w_s", "gmlp_b_s", "hgrn_lb_table", "hgrn_norm_g",
         "norm_ffn_g", "norm_final_g")
BIG = ("w_in", "w_gate_up", "w_branch_a", "w_branch_b", "w_out", "w_down")
ORDER = ("norm_mix_g", "w_in", "gmlp_ln_g", "gmlp_ln_b", "gmlp_w_s", "gmlp_b_s", "hgrn_lb_table",
         "hgrn_norm_g", "w_branch_a", "w_branch_b", "w_out", "norm_ffn_g", "w_gate_up", "w_down",
         "norm_final_g")


def kernel(x, norm_mix_g, w_in, gmlp_ln_g, gmlp_ln_b, gmlp_w_s, gmlp_b_s, hgrn_lb_table, hgrn_norm_g, w_branch_a, w_branch_b, w_out, norm_ffn_g, w_gate_up, w_down, norm_final_g, loss_target, m_norm_mix_g, m_w_in, m_gmlp_ln_g, m_gmlp_ln_b, m_gmlp_w_s, m_gmlp_b_s, m_hgrn_lb_table, m_hgrn_norm_g, m_w_branch_a, m_w_branch_b, m_w_out, m_norm_ffn_g, m_w_gate_up, m_w_down, m_norm_final_g, v_norm_mix_g, v_w_in, v_gmlp_ln_g, v_gmlp_ln_b, v_gmlp_w_s, v_gmlp_b_s, v_hgrn_lb_table, v_hgrn_norm_g, v_w_branch_a, v_w_branch_b, v_w_out, v_norm_ffn_g, v_w_gate_up, v_w_down, v_norm_final_g):
    args = dict(locals())
    T = x.shape[1]
    xs = x.reshape(T, D)
    target = loss_target.reshape(T, D)
    big = {n: args[n].reshape(args[n].shape[1:]) for n in BIG}
    big_m = {n: args["m_" + n].reshape(args[n].shape[1:]) for n in BIG}
    big_v = {n: args["v_" + n].reshape(args[n].shape[1:]) for n in BIG}

    x_i, y_i, c_i = lax.axis_index("x"), lax.axis_index("y"), lax.axis_index("c")
    place = jnp.stack([2 * x_i + y_i, c_i]).astype(jnp.int32)
    cast = {n: _cast_shard("cast_" + n, place, big[n]) for n in BIG}
    tril = jnp.tril(jnp.ones((GCH, GCH), bool))
    wm = jnp.where(tril, gmlp_w_s[0], 0.0).astype(BF16)
    wm_t = jnp.swapaxes(wm, 1, 2)
    b_t = gmlp_b_s[0].T

    (proj, hb), w_in4, (w_a4,) = _proj_fwd(place, xs, norm_mix_g, cast["w_in"], [cast["w_branch_a"]])
    (ab, y_a), (w_b4, w_out4) = _gmlp_fwd(
        proj, gmlp_ln_g, gmlp_ln_b, wm, b_t, w_a4.reshape(D, D),
        job=_gather_job([cast["w_branch_b"], cast["w_out"]]))
    (o_raw, obb, st_before), (w_gu4, w_down4) = _hgrn_fwd(
        proj, hgrn_lb_table, hgrn_norm_g, job=_gather_job([cast["w_gate_up"], cast["w_down"]]))
    w_a, w_b, w_o = (w.reshape(D, D) for w in (w_a4, w_b4, w_out4))
    w_dn = w_down4.reshape(FF, D)
    y_b, mgb, x1 = _merge_fwd(xs, y_a, obb, proj, w_b, w_o)
    act, dx2b, h2b, dgu4, dx1, dx1b, acc_ffn = _ffn_fwd_bwd(
        x1, target, norm_ffn_g, norm_final_g.reshape(1, D), w_gu4, w_dn)

    grads, owns, parts, halves, sibh = {}, {}, {}, {}, {}

    def pair_sums(names, sibs):
        for n, s in zip(names, sibs):
            owns[n], parts[n] = _pair_sum("rs_pair_sum_" + n, place, grads[n], s)

    def chip_sums(names, got):
        for n, r in zip(names, got):
            halves[n] = _chip_sum("rs_chip_sum_" + n, owns[n], r)

    ffn, mix = ("w_gate_up", "w_down"), ("w_branch_a", "w_branch_b", "w_out")
    grads["w_gate_up"], _ = _dw_gate_up(h2b, dgu4)
    grads["w_down"], _ = _dw_down(act, dx2b)
    (dya, dyb, da, dob, dproj), got = _merge_bwd(
        dx1b, y_a, y_b, proj, w_o, w_a, w_b, job=_pair_exchange_job([grads[n] for n in ffn]))
    pair_sums(ffn, got)
    grads["w_branch_a"], _ = _dw_square("dw_branch_a", ab, dya)
    grads["w_branch_b"], _ = _dw_square("dw_branch_b", obb, dyb)
    grads["w_out"], _ = _dw_square("dw_out", mgb, dx1b)
    (dproj, acc_hgrn), got = _hgrn_bwd(
        dproj, dob, o_raw, proj, st_before, hgrn_lb_table, hgrn_norm_g,
        job=_join_jobs(_chip_exchange_job([parts[n] for n in ffn]), _pair_exchange_job([grads[n] for n in mix])))
    chip_sums(ffn, got[:2])
    pair_sums(mix, got[2:])
    dproj, acc_ln, dws, dmix = _gmlp_bwd(dproj, da, proj, gmlp_ln_g, gmlp_ln_b, wm, wm_t, b_t)
    (grad_x, acc_mix), got = _proj_bwd(
        dproj, w_in4, xs, dx1, norm_mix_g,
        job=_join_jobs(_share_halves_job([halves[n] for n in ffn]), _chip_exchange_job([parts[n] for n in mix])))
    sibh.update(zip(ffn, got[:2]))
    chip_sums(mix, got[2:])
    grads["w_in"], got = _dw_in(hb, dproj, job=_share_halves_job([halves[n] for n in mix]))
    sibh.update(zip(mix, got))
    pair_sums(("w_in",), _run_job(_pair_exchange_job([grads["w_in"]]), "rs_pair_exchange_w_in"))
    chip_sums(("w_in",), _run_job(_chip_exchange_job([parts["w_in"]]), "rs_chip_exchange_w_in"))
    (sibh["w_in"],) = _run_job(_share_halves_job([halves["w_in"]]), "rs_share_halves_w_in")
    out = {}
    for n in BIG:
        g, d, mn, vn = _adamw("adamw_" + n, place, big[n], halves[n], sibh[n], big_m[n], big_v[n])
        shp = args[n].shape
        out[n] = (g.reshape(shp), d.reshape(shp), mn.reshape(shp), vn.reshape(shp))

    lbv = jax.nn.sigmoid(hgrn_lb_table[0] - hgrn_lb_table[1])
    d_t0 = jnp.sum(acc_hgrn[0], axis=0) * lbv * (1.0 - lbv)
    loss_row = jnp.zeros((D,), F32).at[0].set(jnp.sum(acc_ffn[0]))
    dws_m = jnp.where(tril[:, None, :], dws.reshape(GCH, NG, GCH), 0.0).transpose(1, 0, 2)
    db_s = jnp.sum(dmix.reshape(GCH, NG, GCH), axis=-1).T
    sp = _pack_small(loss_row, jnp.sum(acc_mix, 0), jnp.sum(acc_ln[0], 0), jnp.sum(acc_ln[1], 0), db_s,
                     jnp.stack([d_t0, -d_t0]), jnp.sum(acc_hgrn[1], 0), jnp.sum(acc_ffn[2], 0),
                     jnp.sum(acc_ffn[1], 0), dws_m)
    zero = jnp.zeros((D,), F32)

    def pack(prefix):
        a = lambda n: args[prefix + n]
        return _pack_small(zero, a("norm_mix_g"), a("gmlp_ln_g"), a("gmlp_ln_b"), a("gmlp_b_s"),
                           a("hgrn_lb_table"), a("hgrn_norm_g"), a("norm_ffn_g"), a("norm_final_g"),
                           a("gmlp_w_s"))

    packed = _small_allreduce_adamw(sp, pack(""), pack("m_"), pack("v_"))
    loss = packed[0][0, 0]
    small = [_unpack_small(p) for p in packed]
    for n in SMALL:
        out[n] = tuple(s[n] for s in small)
    return (loss, grad_x.reshape(x.shape), *[out[n][0] for n in ORDER], *[out[n][1] for n in ORDER],
            *[out[n][2] for n in ORDER], *[out[n][3] for n in ORDER])
```

```python
import functools
import math

import jax
import jax.numpy as jnp
from jax import lax
from jax.experimental import pallas as pl
from jax.experimental.pallas import tpu as pltpu

F32 = jnp.float32
BF16 = jnp.bfloat16
SDS = jax.ShapeDtypeStruct
MESH = pl.DeviceIdType.MESH
ANY = pl.BlockSpec(memory_space=pl.ANY)

D = 1024
NIN = 8
NG = 8
GCH = 128
NH = 8
HD = 128
HCH = 64
HGRN_HB = 2
HW = HGRN_HB * HD
DW_TOKENS = 2048
FF = 2816
FFS = 1408
NCHIP = 4
EPS = 1e-6
QSCALE = HD ** -0.5
GELU_C0 = math.sqrt(2.0 / math.pi)
GELU_C1 = 0.044715
LR, B1, B2, AEPS, WD, STEP = 0.001, 0.9, 0.999, 1e-08, 0.01, 10
VMEM_LIMIT_V7X = 56 * 1024 * 1024
SP_ROWS = 144


def _cparams(**kw):
    return pltpu.CompilerParams(vmem_limit_bytes=VMEM_LIMIT_V7X, **kw)


def _mm(a, b):
    return jnp.dot(a, b, preferred_element_type=F32)


def _mm_nt(a, b):
    return lax.dot_general(a, b, (((1,), (1,)), ((), ())), preferred_element_type=F32)


def _mm_tn(a, b):
    return lax.dot_general(a, b, (((0,), (0,)), ((), ())), preferred_element_type=F32)


def _rows8(x):
    r, c = x.shape
    return jnp.sum(x.reshape(r // 8, 8, c), axis=0)


def _mean(x):
    return jnp.mean(x, axis=-1, keepdims=True)


def _sigmoid(x):
    return 1.0 / (1.0 + jnp.exp(-x))


def _gelu(x):
    t = jnp.tanh(GELU_C0 * (x + GELU_C1 * x * x * x))
    return 0.5 * x * (1.0 + t), t


def _gelu_grad(x, t):
    return 0.5 * (1.0 + t) + 0.5 * x * (1.0 - t * t) * (GELU_C0 * (1.0 + 3.0 * GELU_C1 * x * x))


def _orig_group(m):
    return jnp.where(m < 6, (m + 2) % 6, m)


def _proj_fwd(place, x, g_mix, w_in4, later):
    T = x.shape[0]
    tm = min(512, T)
    ni = T // tm
    n = len(later)

    def body(pc_ref, x_ref, g_ref, *rest):
        proj_ref, h_ref, w_all = rest[1 + n:4 + n]
        gathered = rest[4 + n:4 + 2 * n]
        hs, wbuf, wsem = rest[4 + 2 * n:7 + 2 * n]
        w_sems, later_sems = rest[7 + 2 * n:13 + 2 * n], rest[13 + 2 * n:]
        jp, i = pl.program_id(0), pl.program_id(1)
        s, blk = jp // 2, jp % 2

        @pl.when((jp == 0) & (i == 0))
        def _():
            _gather_start([w_all], w_sems)
            _gather_start(gathered, later_sems)

        @pl.when(jp == 0)
        def _():
            xv = x_ref[...]
            r = lax.rsqrt(_mean(xv * xv) + EPS)
            hb = (xv * r * g_ref[...]).astype(BF16)
            hs[i] = hb
            h_ref[...] = hb

        @pl.when((jp == 2) & (i == 0))
        def _():
            _gather_land([w_all], w_sems, 1)
            _gather_land([w_all], w_sems, 2)

        @pl.when((jp == 4) & (i == 0))
        def _():
            _gather_neighbours(gathered, later_sems)

        @pl.when((jp == 6) & (i == 0))
        def _():
            _gather_land([w_all], w_sems, 3)

        @pl.when(i == 0)
        def _():
            cols = pl.ds(pl.multiple_of(blk * D, 128), D)
            cp = pltpu.make_async_copy(w_all.at[pc_ref[0] ^ s, :, cols], wbuf, wsem)
            cp.start()
            cp.wait()

        proj_ref[...] = _mm(hs[i], wbuf[...])

        @pl.when((jp == NIN - 1) & (i == ni - 1))
        def _():
            _gather_drain([w_all], w_sems)
            _gather_finish(gathered, later_sems)

    tile = lambda jp, i, pc: (jnp.where(jp == 0, i, ni - 1), 0)
    res = pl.pallas_call(
        body, name="proj_fwd",
        grid_spec=pltpu.PrefetchScalarGridSpec(
            num_scalar_prefetch=1, grid=(NIN, ni),
            in_specs=[pl.BlockSpec((tm, D), tile), pl.BlockSpec((1, D), lambda jp, i, pc: (0, 0))] + [ANY] * (1 + n),
            out_specs=[pl.BlockSpec((tm, D), lambda jp, i, pc: (i, 2 * (pc[0] ^ (jp // 2)) + jp % 2)),
                       pl.BlockSpec((tm, D), tile)] + [ANY] * (1 + n),
            scratch_shapes=[pltpu.VMEM((ni, tm, D), BF16), pltpu.VMEM((D, D), BF16), pltpu.SemaphoreType.DMA]
            + _gather_sems(1) + _gather_sems(n)),
        out_shape=[SDS((T, NIN * D), F32), SDS((T, D), BF16), SDS(w_in4.shape, BF16)]
        + [SDS(a.shape, a.dtype) for a in later],
        input_output_aliases={3 + k: 2 + k for k in range(1 + n)},
        compiler_params=_cparams(has_side_effects=True),
    )(place, x, g_mix, w_in4, *later)
    return res[:2], res[2], res[3:]


def _layer_norm_stats(gv):
    mu = _mean(gv)
    xc = gv - mu
    rs = lax.rsqrt(_mean(xc * xc) + EPS)
    return xc * rs, rs


def _gmlp_fwd(proj, ln_g, ln_b, wm, b_t, w_a, job=None):
    T = proj.shape[0]
    tm = min(256, T)

    def body(u_ref, v_ref, lg_ref, lb_ref, wm_ref, bt_ref, wa_ref, a_ref, ya_ref, a_s):
        gu, _ = _gelu(u_ref[...])
        gv, _ = _gelu(v_ref[...])
        vhat, _ = _layer_norm_stats(gv)
        vnb = (vhat * lg_ref[...] + lb_ref[...]).astype(BF16)
        for ch in range(tm // GCH):
            rows = slice(GCH * ch, GCH * (ch + 1))
            for g in range(NG):
                cols = slice(128 * g, 128 * (g + 1))
                mixed = _mm(wm_ref[g], vnb[rows, cols]) + bt_ref[:, g:g + 1]
                a_s[rows, cols] = gu[rows, cols] * mixed
        ab = a_s[...].astype(BF16)
        a_ref[...] = ab
        ya_ref[...] = _mm(ab, wa_ref[...])

    row = lambda i: (0, 0)
    return _call(
        body, name="gmlp_fwd", grid=(T // tm,), job=job, args=(proj, proj, ln_g, ln_b, wm, b_t, w_a),
        in_specs=[pl.BlockSpec((tm, D), lambda i: (i, 0)), pl.BlockSpec((tm, D), lambda i: (i, 1)),
                  pl.BlockSpec((1, D), row), pl.BlockSpec((1, D), row),
                  pl.BlockSpec((NG, GCH, GCH), lambda i: (0, 0, 0)), pl.BlockSpec((GCH, NG), row),
                  pl.BlockSpec((D, D), row)],
        out_specs=[pl.BlockSpec((tm, D), lambda i: (i, 0)), pl.BlockSpec((tm, D), lambda i: (i, 0))],
        out_shape=[SDS((T, D), BF16), SDS((T, D), F32)],
        scratch_shapes=[pltpu.VMEM((tm, D), F32)])


def _cumsum64(x, row):
    for s in (1, 2, 4, 8, 16, 32):
        x = x + jnp.where(row >= s, pltpu.roll(x, s, 0), 0.0)
    return x


def _revcumsum64(x, row):
    n = x.shape[0]
    for s in (1, 2, 4, 8, 16, 32):
        x = x + jnp.where(row < HCH - s, pltpu.roll(x, n - s, 0), 0.0)
    return x


def _head_mean(x):
    parts = [jnp.broadcast_to(_mean(x[:, HD * h:HD * (h + 1)]), (x.shape[0], HD)) for h in range(x.shape[1] // HD)]
    return jnp.concatenate(parts, axis=1)


def _seg_sum(x):
    n, c = x.shape
    s = jnp.sum(x.reshape(n // HCH, HCH, c), axis=1, keepdims=True)
    return jnp.broadcast_to(s, (n // HCH, HCH, c)).reshape(n, c)


def _hgrn_gates(fl, lbv, row):
    s = _sigmoid(fl)
    f = lbv + (1.0 - lbv) * s
    a = _cumsum64(jnp.log(f), row)
    a_mid = _seg_sum(jnp.where(row == HCH // 2 - 1, a, 0.0))
    a_last = _seg_sum(jnp.where(row == HCH - 1, a, 0.0))
    return s, f, a, a_mid, a_last


def _hgrn_fwd(proj, lb_table, norm_g, job=None):
    T = proj.shape[0]
    tb = min(512, T)
    nc = tb // HCH

    def body(q_ref, fl_ref, v_ref, g_ref, lbt_ref, gn_ref, o_ref, ob_ref, stb_ref, st_s, o_s):
        @pl.when(pl.program_id(1) == 0)
        def _():
            st_s[...] = jnp.zeros_like(st_s)

        row = lax.broadcasted_iota(jnp.int32, (tb, HW), 0) & (HCH - 1)
        lbv = _sigmoid(lbt_ref[0:1, :] - lbt_ref[1:2, :])
        _, f, a, a_mid, a_last = _hgrn_gates(fl_ref[...], lbv, row)
        k = 1.0 - f
        qs = q_ref[...] * QSCALE
        q_in = (qs * jnp.exp(a - a_mid)).astype(BF16)
        k_in = (k * jnp.exp(a_mid - a)).astype(BF16)
        q_a = (qs * jnp.exp(a)).astype(BF16)
        k_d = (k * jnp.exp(a_last - a)).astype(BF16)
        dec = jnp.exp(a_last)
        vb = v_ref[...].astype(BF16)
        tri = (lax.broadcasted_iota(jnp.int32, (HCH, HCH), 0)
               >= lax.broadcasted_iota(jnp.int32, (HCH, HCH), 1))
        for c in range(nc):
            sl = slice(HCH * c, HCH * (c + 1))
            for hh in range(HGRN_HB):
                hs = slice(HD * hh, HD * (hh + 1))
                st = st_s[hh]
                stb_ref[hh, c] = st
                sc = jnp.where(tri, _mm_nt(q_in[sl, hs], k_in[sl, hs]), 0.0)
                o_s[sl, hs] = _mm(sc.astype(BF16), vb[sl, hs]) + _mm_nt(q_a[sl, hs], st.astype(BF16))
                d64 = dec[sl, hs]
                st_s[hh] = st * jnp.concatenate([d64, d64], axis=0) + _mm_tn(vb[sl, hs], k_d[sl, hs])
        o = o_s[...]
        r = lax.rsqrt(_head_mean(o * o) + EPS)
        g = g_ref[...]
        o_ref[...] = o
        ob_ref[...] = (o * r * gn_ref[...] * (g * _sigmoid(g))).astype(BF16)

    def col(off):
        return pl.BlockSpec((tb, HW), lambda h, cb: (cb, off * (NH // HGRN_HB) + h))

    return _call(
        body, name="hgrn_fwd", grid=(NH // HGRN_HB, T // tb), job=job,
        args=(proj, proj, proj, proj, lb_table, norm_g),
        in_specs=[col(2), col(3), col(4), col(5),
                  pl.BlockSpec((2, HW), lambda h, cb: (0, h)), pl.BlockSpec((1, HW), lambda h, cb: (0, h))],
        out_specs=[pl.BlockSpec((tb, HW), lambda h, cb: (cb, h)), pl.BlockSpec((tb, HW), lambda h, cb: (cb, h)),
                   pl.BlockSpec((HGRN_HB, nc, HD, HD), lambda h, cb: (h, cb, 0, 0))],
        out_shape=[SDS((T, D), F32), SDS((T, D), BF16), SDS((NH, T // HCH, HD, HD), F32)],
        scratch_shapes=[pltpu.VMEM((HGRN_HB, HD, HD), F32), pltpu.VMEM((tb, HW), F32)])


def _merge_fwd(x, y_a, ob, proj, w_b, w_out):
    T = x.shape[0]
    tm = min(512, T)

    def body(x_ref, ya_ref, ob_ref, ga_ref, gb_ref, wb_ref, wo_ref, yb_ref, mg_ref, x1_ref):
        yb = _mm(ob_ref[...], wb_ref[...])
        merged = (_sigmoid(ga_ref[...]) * ya_ref[...] + _sigmoid(gb_ref[...]) * yb).astype(BF16)
        yb_ref[...] = yb
        mg_ref[...] = merged
        x1_ref[...] = x_ref[...] + _mm(merged, wo_ref[...])

    t = lambda i: (i, 0)
    w = lambda i: (0, 0)
    return pl.pallas_call(
        body, name="merge_fwd", grid=(T // tm,),
        in_specs=[pl.BlockSpec((tm, D), t), pl.BlockSpec((tm, D), t), pl.BlockSpec((tm, D), t),
                  pl.BlockSpec((tm, D), lambda i: (i, 6)), pl.BlockSpec((tm, D), lambda i: (i, 7)),
                  pl.BlockSpec((D, D), w), pl.BlockSpec((D, D), w)],
        out_specs=[pl.BlockSpec((tm, D), t)] * 3,
        out_shape=[SDS((T, D), F32), SDS((T, D), BF16), SDS((T, D), F32)],
        compiler_params=_cparams(),
    )(x, y_a, ob, proj, proj, w_b, w_out)


def _ffn_fwd_bwd(x1, target, g_ffn, g_fin, w_gu4, w_down):
    T = x1.shape[0]
    tm = min(256, T)
    inv_d = 1.0 / D

    def body(x1_ref, tg_ref, gf_ref, gn_ref, wgu_ref, wd_ref,
             act_ref, dx2b_ref, h2b_ref, dgu_ref, dx1_ref, dx1b_ref, acc_ref):
        @pl.when(pl.program_id(0) == 0)
        def _():
            acc_ref[...] = jnp.zeros_like(acc_ref)

        x1v = x1_ref[...]
        gf = gf_ref[...]
        gn = gn_ref[...]
        rr1 = lax.rsqrt(_mean(x1v * x1v) + EPS)
        x1n = x1v * rr1
        h2b = (x1n * gf).astype(BF16)
        h2b_ref[...] = h2b
        p = [_mm(h2b, wgu_ref[k]) for k in range(NCHIP)]
        sg = [_sigmoid(p[0]), _sigmoid(p[1])]
        si = [p[0] * sg[0], p[1] * sg[1]]
        x2 = x1v
        for k in range(2):
            actk = (si[k] * p[2 + k]).astype(BF16)
            act_ref[:, FFS * k:FFS * (k + 1)] = actk
            x2 = x2 + _mm(actk, wd_ref[FFS * k:FFS * (k + 1), :])
        rr2 = lax.rsqrt(_mean(x2 * x2) + EPS)
        x2n = x2 * rr2
        e = x2n * gn - tg_ref[...]
        acc_ref[0] += _rows8(e * e) * (0.5 * inv_d)
        dy = e * inv_d
        acc_ref[1] += _rows8(dy * x2n)
        dxn = dy * gn
        dx2 = rr2 * (dxn - x2n * _mean(dxn * x2n))
        dx2b = dx2.astype(BF16)
        dx2b_ref[...] = dx2b
        dh2 = None
        for k in range(2):
            dact = _mm_nt(dx2b, wd_ref[FFS * k:FFS * (k + 1), :])
            dgate = (dact * p[2 + k] * (sg[k] * (1.0 + p[k] * (1.0 - sg[k])))).astype(BF16)
            dup = (dact * si[k]).astype(BF16)
            dgu_ref[k] = dgate
            dgu_ref[2 + k] = dup
            part = _mm_nt(dgate, wgu_ref[k]) + _mm_nt(dup, wgu_ref[2 + k])
            dh2 = part if dh2 is None else dh2 + part
        acc_ref[2] += _rows8(dh2 * x1n)
        dxn1 = dh2 * gf
        dx1 = dx2 + rr1 * (dxn1 - x1n * _mean(dxn1 * x1n))
        dx1_ref[...] = dx1
        dx1b_ref[...] = dx1.astype(BF16)

    t = lambda i: (i, 0)
    w = lambda i: (0, 0)
    one = pl.Buffered(1)
    return pl.pallas_call(
        body, name="ffn_fwd_bwd", grid=(T // tm,),
        in_specs=[pl.BlockSpec((tm, D), t), pl.BlockSpec((tm, D), t),
                  pl.BlockSpec((1, D), w), pl.BlockSpec((1, D), w),
                  pl.BlockSpec((NCHIP, D, FFS), lambda i: (0, 0, 0), pipeline_mode=one),
                  pl.BlockSpec((FF, D), w, pipeline_mode=one)],
        out_specs=[pl.BlockSpec((tm, FF), t), pl.BlockSpec((tm, D), t), pl.BlockSpec((tm, D), t),
                   pl.BlockSpec((NCHIP, tm, FFS), lambda i: (0, i, 0)),
                   pl.BlockSpec((tm, D), t), pl.BlockSpec((tm, D), t),
                   pl.BlockSpec((3, 8, D), lambda i: (0, 0, 0))],
        out_shape=[SDS((T, FF), BF16), SDS((T, D), BF16), SDS((T, D), BF16),
                   SDS((NCHIP, T, FFS), BF16), SDS((T, D), F32), SDS((T, D), BF16),
                   SDS((3, 8, D), F32)],
        compiler_params=_cparams(),
    )(x1, target, g_ffn, g_fin, w_gu4, w_down)


def _merge_bwd(dx1b, y_a, y_b, proj, w_out, w_a, w_b, job=None):
    T = dx1b.shape[0]
    tm = min(512, T)

    def body(dx_ref, ya_ref, yb_ref, ga_ref, gb_ref, wo_ref, wa_ref, wb_ref,
             dya_ref, dyb_ref, da_ref, dob_ref, dp_ref):
        dm = _mm_nt(dx_ref[...], wo_ref[...])
        sa = _sigmoid(ga_ref[...])
        sb = _sigmoid(gb_ref[...])
        dya = (dm * sa).astype(BF16)
        dyb = (dm * sb).astype(BF16)
        dya_ref[...] = dya
        dyb_ref[...] = dyb
        dp_ref[0] = (dm * ya_ref[...] * sa * (1.0 - sa)).astype(BF16)
        dp_ref[1] = (dm * yb_ref[...] * sb * (1.0 - sb)).astype(BF16)
        da_ref[...] = _mm_nt(dya, wa_ref[...])
        dob_ref[...] = _mm_nt(dyb, wb_ref[...])

    t = lambda i: (i, 0)
    w = lambda i: (0, 0)
    return _call(
        body, name="merge_bwd", grid=(T // tm,),
        in_specs=[pl.BlockSpec((tm, D), t), pl.BlockSpec((tm, D), t), pl.BlockSpec((tm, D), t),
                  pl.BlockSpec((tm, D), lambda i: (i, 6)), pl.BlockSpec((tm, D), lambda i: (i, 7)),
                  pl.BlockSpec((D, D), w), pl.BlockSpec((D, D), w), pl.BlockSpec((D, D), w)],
        out_specs=[pl.BlockSpec((tm, D), t)] * 4 + [pl.BlockSpec((2, tm, D), lambda i: (3, i, 0))],
        out_shape=[SDS((T, D), BF16), SDS((T, D), BF16), SDS((T, D), F32), SDS((T, D), F32),
                   SDS((NIN, T, D), BF16)],
        args=(dx1b, y_a, y_b, proj, proj, w_out, w_a, w_b), job=job)


def _hgrn_bwd(dproj, dob, o_raw, proj, st_before, lb_table, norm_g, job=None):
    T = dob.shape[0]
    tb = min(512, T)
    nc = tb // HCH
    nb = T // tb

    def body(dp_in, dob_ref, o_ref, q_ref, fl_ref, v_ref, g_ref, stb_ref, lbt_ref, gn_ref,
             dp_ref, acc_ref, dst_s, dqin_s, dqa_s, dkin_s, dkd_s, dv_s, ddec_s):
        del dp_in

        @pl.when(pl.program_id(1) == 0)
        def _():
            dst_s[...] = jnp.zeros_like(dst_s)
            acc_ref[...] = jnp.zeros_like(acc_ref)

        row = lax.broadcasted_iota(jnp.int32, (tb, HW), 0) & (HCH - 1)
        gn = gn_ref[...]
        lbv = _sigmoid(lbt_ref[0:1, :] - lbt_ref[1:2, :])
        o = o_ref[...]
        r = lax.rsqrt(_head_mean(o * o) + EPS)
        on = o * r
        g = g_ref[...]
        sgm = _sigmoid(g)
        dob_v = dob_ref[...]
        dp_ref[3] = (dob_v * on * gn * (sgm * (1.0 + g * (1.0 - sgm)))).astype(BF16)
        do_n = dob_v * (g * sgm)
        acc_ref[1] += _rows8(do_n * on)
        dxn = do_n * gn
        do = (r * (dxn - on * _head_mean(dxn * on))).astype(BF16)
        s, f, a, a_mid, a_last = _hgrn_gates(fl_ref[...], lbv, row)
        k = 1.0 - f
        qs = q_ref[...] * QSCALE
        e_q = jnp.exp(a - a_mid)
        e_k = jnp.exp(a_mid - a)
        e_a = jnp.exp(a)
        e_l = jnp.exp(a_last - a)
        dec = jnp.exp(a_last)
        q_in = qs * e_q
        k_in = k * e_k
        q_a = qs * e_a
        k_d = k * e_l
        q_inb, k_inb, q_ab, k_db = (z.astype(BF16) for z in (q_in, k_in, q_a, k_d))
        vb = v_ref[...].astype(BF16)
        tri = (lax.broadcasted_iota(jnp.int32, (HCH, HCH), 0)
               >= lax.broadcasted_iota(jnp.int32, (HCH, HCH), 1))
        for c in reversed(range(nc)):
            sl = slice(HCH * c, HCH * (c + 1))
            for hh in range(HGRN_HB):
                hs = slice(HD * hh, HD * (hh + 1))
                stp = stb_ref[hh, c]
                dst = dst_s[hh]
                dstb = dst.astype(BF16)
                do_c = do[sl, hs]
                v_c = vb[sl, hs]
                dqa_s[sl, hs] = _mm(do_c, stp.astype(BF16))
                dkd_s[sl, hs] = _mm(v_c, dstb)
                ddec_s[sl, hs] = jnp.broadcast_to(jnp.sum(dst * stp, axis=0, keepdims=True), (HCH, HD))
                sc = jnp.where(tri, _mm_nt(q_inb[sl, hs], k_inb[sl, hs]), 0.0).astype(BF16)
                dsc = jnp.where(tri, _mm_nt(do_c, v_c), 0.0).astype(BF16)
                dv_s[sl, hs] = _mm_nt(k_db[sl, hs], dstb) + _mm_tn(sc, do_c)
                dqin_s[sl, hs] = _mm(dsc, k_inb[sl, hs])
                dkin_s[sl, hs] = _mm_tn(dsc, q_inb[sl, hs])
                d64 = dec[sl, hs]
                dst_s[hh] = dst * jnp.concatenate([d64, d64], axis=0) + _mm_tn(do_c, q_ab[sl, hs])
        dq_in = dqin_s[...]
        dq_a = dqa_s[...]
        dk_in = dkin_s[...]
        dk_d = dkd_s[...]
        dp_ref[0] = ((dq_in * e_q + dq_a * e_a) * QSCALE).astype(BF16)
        dp_ref[2] = dv_s[...].astype(BF16)
        tq = dq_in * q_in
        tk = dk_in * k_in
        td = dk_d * k_d
        d_a = tq + dq_a * q_a - tk - td
        d_a = d_a + jnp.where(row == HCH // 2 - 1, _seg_sum(tk - tq), 0.0)
        d_a = d_a + jnp.where(row == HCH - 1, _seg_sum(td) + ddec_s[...] * dec, 0.0)
        dlf = _revcumsum64(d_a, row)
        df = dlf / f - (dk_in * e_k + dk_d * e_l)
        dp_ref[1] = (df * (1.0 - lbv) * s * (1.0 - s)).astype(BF16)
        acc_ref[0] += _rows8(df * (1.0 - s))

    def col(off):
        return pl.BlockSpec((tb, HW), lambda h, cb: (nb - 1 - cb, off * (NH // HGRN_HB) + h))

    hb = lambda h, cb: (nb - 1 - cb, h)
    return _call(
        body, name="hgrn_bwd", grid=(NH // HGRN_HB, nb), job=job,
        args=(dproj, dob, o_raw, proj, proj, proj, proj, st_before, lb_table, norm_g),
        in_specs=[ANY, pl.BlockSpec((tb, HW), hb), pl.BlockSpec((tb, HW), hb),
                  col(2), col(3), col(4), col(5),
                  pl.BlockSpec((HGRN_HB, nc, HD, HD), lambda h, cb: (h, nb - 1 - cb, 0, 0)),
                  pl.BlockSpec((2, HW), lambda h, cb: (0, h)), pl.BlockSpec((1, HW), lambda h, cb: (0, h))],
        out_specs=[pl.BlockSpec((4, tb, HW), lambda h, cb: (0, nb - 1 - cb, h)),
                   pl.BlockSpec((2, 8, HW), lambda h, cb: (0, 0, h))],
        out_shape=[SDS(dproj.shape, BF16), SDS((2, 8, D), F32)],
        scratch_shapes=[pltpu.VMEM((HGRN_HB, HD, HD), F32)] + [pltpu.VMEM((tb, HW), F32)] * 6,
        aliases={0: 0})


def _gmlp_bwd(dproj, da, proj, ln_g, ln_b, wm, wm_t, b_t):
    T = da.shape[0]
    tm = min(256, T)

    def body(dp_in, da_ref, u_ref, v_ref, lg_ref, lb_ref, wm_ref, wmt_ref, bt_ref,
             dp_ref, acc_ref, dws_ref, dmix_ref, du_s, dvn_s):
        del dp_in

        @pl.when(pl.program_id(0) == 0)
        def _():
            acc_ref[...] = jnp.zeros_like(acc_ref)
            dws_ref[...] = jnp.zeros_like(dws_ref)
            dmix_ref[...] = jnp.zeros_like(dmix_ref)

        u = u_ref[...]
        v = v_ref[...]
        lg = lg_ref[...]
        gu, t_u = _gelu(u)
        gv, t_v = _gelu(v)
        vhat, rs = _layer_norm_stats(gv)
        vnb = (vhat * lg + lb_ref[...]).astype(BF16)
        da_v = da_ref[...]
        for ch in range(tm // GCH):
            rows = slice(GCH * ch, GCH * (ch + 1))
            for g in range(NG):
                cols = slice(128 * g, 128 * (g + 1))
                vng = vnb[rows, cols]
                mixed = _mm(wm_ref[g], vng) + bt_ref[:, g:g + 1]
                dag = da_v[rows, cols]
                dmx = dag * gu[rows, cols]
                du_s[rows, cols] = dag * mixed
                dmxb = dmx.astype(BF16)
                dws_ref[:, cols] += _mm_nt(dmxb, vng)
                dmix_ref[:, cols] += dmx
                dvn_s[rows, cols] = _mm(wmt_ref[g], dmxb)
        dp_ref[0] = (du_s[...] * _gelu_grad(u, t_u)).astype(BF16)
        dvn = dvn_s[...]
        acc_ref[0] += _rows8(dvn * vhat)
        acc_ref[1] += _rows8(dvn)
        dvh = dvn * lg
        dgv = rs * (dvh - _mean(dvh) - vhat * _mean(dvh * vhat))
        dp_ref[1] = (dgv * _gelu_grad(v, t_v)).astype(BF16)

    row = lambda i: (0, 0)
    w3 = lambda i: (0, 0, 0)
    return pl.pallas_call(
        body, name="gmlp_bwd", grid=(T // tm,),
        in_specs=[ANY, pl.BlockSpec((tm, D), lambda i: (i, 0)),
                  pl.BlockSpec((tm, D), lambda i: (i, 0)), pl.BlockSpec((tm, D), lambda i: (i, 1)),
                  pl.BlockSpec((1, D), row), pl.BlockSpec((1, D), row),
                  pl.BlockSpec((NG, GCH, GCH), w3), pl.BlockSpec((NG, GCH, GCH), w3),
                  pl.BlockSpec((GCH, NG), row)],
        out_specs=[pl.BlockSpec((2, tm, D), lambda i: (2, i, 0)),
                   pl.BlockSpec((2, 8, D), w3), pl.BlockSpec((GCH, D), row), pl.BlockSpec((GCH, D), row)],
        out_shape=[SDS(dproj.shape, BF16), SDS((2, 8, D), F32), SDS((GCH, D), F32), SDS((GCH, D), F32)],
        scratch_shapes=[pltpu.VMEM((tm, D), F32), pltpu.VMEM((tm, D), F32)],
        input_output_aliases={0: 0},
        compiler_params=_cparams(),
    )(dproj, da, proj, proj, ln_g, ln_b, wm, wm_t, b_t)


def _proj_bwd(dproj, w_in4, x, dx1, g_mix, job=None):
    T = x.shape[0]
    tm = min(256, T)
    order = (2, 3, 4, 5, 0, 1, 6, 7)

    def body(dp_ref, w_ref, x_ref, dx1_ref, g_ref, gx_ref, acc_ref):
        @pl.when(pl.program_id(0) == 0)
        def _():
            acc_ref[...] = jnp.zeros_like(acc_ref)

        dh = None
        for m, og in enumerate(order):
            part = _mm_nt(dp_ref[m], w_ref[og // 2, :, D * (og % 2):D * (og % 2 + 1)])
            dh = part if dh is None else dh + part
        xv = x_ref[...]
        r = lax.rsqrt(_mean(xv * xv) + EPS)
        xn = xv * r
        acc_ref[...] += _rows8(dh * xn)
        dxn = dh * g_ref[...]
        gx_ref[...] = dx1_ref[...] + r * (dxn - xn * _mean(dxn * xn))

    t = lambda i: (i, 0)
    return _call(
        body, name="proj_bwd", grid=(T // tm,),
        in_specs=[pl.BlockSpec((NIN, tm, D), lambda i: (0, i, 0)),
                  pl.BlockSpec((NCHIP, D, 2 * D), lambda i: (0, 0, 0), pipeline_mode=pl.Buffered(1)),
                  pl.BlockSpec((tm, D), t), pl.BlockSpec((tm, D), t), pl.BlockSpec((1, D), lambda i: (0, 0))],
        out_specs=[pl.BlockSpec((tm, D), t), pl.BlockSpec((8, D), lambda i: (0, 0))],
        out_shape=[SDS((T, D), F32), SDS((8, D), F32)],
        args=(dproj, w_in4, x, dx1, g_mix), job=job)


def _dw_call(name, a, b, a_spec, b_spec, o_spec, out_shape, nblk, tt, job=None, prefetch=None):
    T = a.shape[-2]

    def body(*refs):
        a_ref, b_ref, o_ref = refs[-3:]

        @pl.when(pl.program_id(1) == 0)
        def _():
            o_ref[...] = jnp.zeros_like(o_ref)
        o_ref[...] += _mm_tn(a_ref[...], b_ref[...])

    (out,), job_out = _call(
        body, name=name, grid=(nblk, T // tt), in_specs=[a_spec, b_spec], out_specs=[o_spec],
        out_shape=[out_shape], args=(a, b), job=job, prefetch=prefetch)
    return out, job_out


def _dw_in_half(name, place, hb, dproj, mine, job=None):
    tt = min(DW_TOKENS, hb.shape[0])
    half = (lambda pc: pc[1]) if mine else (lambda pc: 1 - pc[1])
    return _dw_call(
        name, hb, dproj,
        pl.BlockSpec((tt, D // 2), lambda m, t, pc: (t, half(pc))),
        pl.BlockSpec((None, tt, D), lambda m, t, pc: (m, t, 0)),
        pl.BlockSpec((None, D // 2, D), lambda m, t, pc: (_orig_group(m) // 2, 0, _orig_group(m) % 2)),
        SDS((NCHIP, D // 2, 2 * D), F32), NIN, tt, job, place)


def _dw_gate_up(h2b, dgu4, job=None):
    tt = min(DW_TOKENS, h2b.shape[0])
    return _dw_call(
        "dw_gate_up", h2b, dgu4,
        pl.BlockSpec((tt, D), lambda k, t: (t, 0)),
        pl.BlockSpec((None, tt, FFS), lambda k, t: (k, t, 0)),
        pl.BlockSpec((None, D, FFS), lambda k, t: (k, 0, 0)),
        SDS((NCHIP, D, FFS), F32), NCHIP, tt, job)


def _dw_down(act, dx2b, job=None):
    tt = min(DW_TOKENS, act.shape[0])
    g, job_out = _dw_call(
        "dw_down", act, dx2b,
        pl.BlockSpec((tt, FFS), lambda k, t: (t, k)),
        pl.BlockSpec((tt, D), lambda k, t: (t, 0)),
        pl.BlockSpec((FFS, D), lambda k, t: (k, 0)),
        SDS((FF, D), F32), 2, tt, job)
    return g.reshape(NCHIP, FF // NCHIP, D), job_out


def _dw_square(name, a, b, job=None):
    tt = min(DW_TOKENS, a.shape[0])
    g, job_out = _dw_call(
        name, a, b,
        pl.BlockSpec((tt, D), lambda k, t: (t, 0)), pl.BlockSpec((tt, D), lambda k, t: (t, 0)),
        pl.BlockSpec((D, D), lambda k, t: (0, 0)), SDS((D, D), F32), 1, tt, job)
    return g.reshape(NCHIP, D // NCHIP, D), job_out


def _place():
    x, y, c = lax.axis_index("x"), lax.axis_index("y"), lax.axis_index("c")
    return x, y, c, 2 * x + y


def _chip_at(x, y, s):
    return x ^ (s >> 1), y ^ (s & 1)


class _Job:
    def __init__(self, ins, out_shapes, sems, start, finish, aliases=None, mid=None):
        self.ins, self.out_shapes, self.sems = list(ins), list(out_shapes), list(sems)
        self.start, self.finish, self.aliases = start, finish, dict(aliases or {})
        self.mid = mid if mid is not None else (lambda ins, outs, sems: None)


def _join_jobs(*jobs):
    def cut(refs, sizes):
        out, at = [], 0
        for n in sizes:
            out.append(refs[at:at + n])
            at += n
        return out

    ni = [len(j.ins) for j in jobs]
    no = [len(j.out_shapes) for j in jobs]
    ns = [len(j.sems) for j in jobs]

    def run(which):
        def go(ins, outs, sems):
            for j, a, b, c in zip(jobs, cut(ins, ni), cut(outs, no), cut(sems, ns)):
                getattr(j, which)(a, b, c)
        return go

    aliases = {}
    for k, j in enumerate(jobs):
        for a, b in j.aliases.items():
            aliases[sum(ni[:k]) + a] = sum(no[:k]) + b
    return _Job([a for j in jobs for a in j.ins], [o for j in jobs for o in j.out_shapes],
                [s for j in jobs for s in j.sems], run("start"), run("finish"), aliases, run("mid"))


def _call(body, *, name, grid, in_specs, out_specs, out_shape, args, scratch_shapes=(), aliases=None,
          job=None, prefetch=None):
    n_in, n_out, n_scr = len(in_specs), len(out_specs), len(scratch_shapes)
    npf = 0 if prefetch is None else 1
    job = job if job is not None else _Job([], [], [], lambda *a: None, lambda *a: None)
    ji, jo = len(job.ins), len(job.out_shapes)
    steps = math.prod(grid)

    def wrapped(*refs):
        pf, refs = refs[:npf], refs[npf:]
        ins, jin = refs[:n_in], refs[n_in:n_in + ji]
        o0 = n_in + ji
        outs, jout = refs[o0:o0 + n_out], refs[o0 + n_out:o0 + n_out + jo]
        s0 = o0 + n_out + jo
        scr, jsem = refs[s0:s0 + n_scr], refs[s0 + n_scr:]
        step = functools.reduce(lambda acc, ag: acc * ag[1] + pl.program_id(ag[0]), enumerate(grid), 0)
        if ji or jo:
            @pl.when(step == 0)
            def _():
                job.start(jin, jout, jsem)

        body(*pf, *ins, *outs, *scr)

        if ji or jo:
            @pl.when(step == steps // 2)
            def _():
                job.mid(jin, jout, jsem)

            @pl.when(step == steps - 1)
            def _():
                job.finish(jin, jout, jsem)

    io = {npf + a: b for a, b in dict(aliases or {}).items()}
    io.update({npf + n_in + a: n_out + b for a, b in job.aliases.items()})
    kw = dict(in_specs=list(in_specs) + [ANY] * ji, out_specs=list(out_specs) + [ANY] * jo,
              scratch_shapes=list(scratch_shapes) + job.sems)
    if npf:
        kw = dict(grid_spec=pltpu.PrefetchScalarGridSpec(num_scalar_prefetch=1, grid=grid, **kw))
    else:
        kw["grid"] = grid
    res = pl.pallas_call(
        wrapped, name=name, out_shape=list(out_shape) + job.out_shapes, input_output_aliases=io,
        compiler_params=_cparams(has_side_effects=bool(ji or jo)), **kw,
    )(*(() if prefetch is None else (prefetch,)), *args, *job.ins)
    return list(res[:n_out]), list(res[n_out:])


def _run_job(job, name):
    ji, jo = len(job.ins), len(job.out_shapes)

    def body(*refs):
        jin, jout, jsem = refs[:ji], refs[ji:ji + jo], refs[ji + jo:]
        job.start(jin, jout, jsem)
        job.finish(jin, jout, jsem)

    return list(pl.pallas_call(
        body, name=name, in_specs=[ANY] * ji, out_specs=[ANY] * jo, out_shape=job.out_shapes,
        scratch_shapes=job.sems, input_output_aliases=job.aliases,
        compiler_params=pltpu.CompilerParams(has_side_effects=True))(*job.ins))


def _cast_shard(name, place, w):
    rows, cols = w.shape
    tr = 352 if rows % 352 == 0 else 256

    def body(pc_ref, w_ref, o_ref):
        del pc_ref
        o_ref[...] = w_ref[...].astype(BF16)

    return pl.pallas_call(
        body, name=name,
        grid_spec=pltpu.PrefetchScalarGridSpec(
            num_scalar_prefetch=1, grid=(rows // tr,),
            in_specs=[pl.BlockSpec((tr, cols), lambda i, pc: (i, 0))],
            out_specs=pl.BlockSpec((None, tr, cols), lambda i, pc: (pc[0], i, 0))),
        out_shape=SDS((NCHIP, rows, cols), BF16),
        compiler_params=_cparams(),
    )(place, w)


def _sibling_copy(ref, send_sem, recv_sem):
    x, y, c, _ = _place()
    return pltpu.make_async_remote_copy(src_ref=ref, dst_ref=ref, send_sem=send_sem, recv_sem=recv_sem,
                                        device_id=(x, y, 1 - c), device_id_type=MESH)


def _half_rows(arr, slot, core):
    half = arr.shape[1] // 2
    return arr.at[slot, pl.ds(pl.multiple_of(core * half, 16), half)]


def _quarter_rows(arr, slot, core, q):
    quarter = arr.shape[1] // 4
    return arr.at[slot, pl.ds(pl.multiple_of((2 * core + q) * quarter, 16), quarter)]


def _chip_copy(ref, dist, send_sem, recv_sem):
    x, y, c, _ = _place()
    cx, cy = _chip_at(x, y, dist)
    return pltpu.make_async_remote_copy(src_ref=ref, dst_ref=ref, send_sem=send_sem, recv_sem=recv_sem,
                                        device_id=(cx, cy, c), device_id_type=MESH)


def _gather_sems(n):
    dma = pltpu.SemaphoreType.DMA
    return [dma((n, 2))] * 4 + [dma((n, 4))] * 2


def _gather_start(arrs, sems):
    dsend, drecv = sems[0], sems[1]
    _, _, c, j = _place()
    for w, arr in enumerate(arrs):
        for dist in (1, 2):
            _chip_copy(_half_rows(arr, j, c), dist, dsend.at[w, dist - 1], drecv.at[w, dist - 1]).start()


def _gather_land(arrs, sems, dist):
    dsend, drecv, rsend, rrecv, fsend, frecv = sems
    _, _, c, j = _place()
    if dist < 3:
        other = 3 - dist
        for w, arr in enumerate(arrs):
            landed = _half_rows(arr, j ^ dist, c)
            _chip_copy(landed, dist, dsend.at[w, dist - 1], drecv.at[w, dist - 1]).wait_recv()
            relay = _quarter_rows(arr, j ^ dist, c, other - 1)
            _chip_copy(relay, other, rsend.at[w, other - 1], rrecv.at[w, other - 1]).start()
            _sibling_copy(landed, fsend.at[w, dist - 1], frecv.at[w, dist - 1]).start()
        for w, arr in enumerate(arrs):
            theirs = _half_rows(arr, j ^ dist, 1 - c)
            _sibling_copy(theirs, fsend.at[w, dist - 1], frecv.at[w, dist - 1]).wait_recv()
    else:
        for w, arr in enumerate(arrs):
            for via in (1, 2):
                piece = _quarter_rows(arr, j ^ 3, c, via - 1)
                _chip_copy(piece, via, rsend.at[w, via - 1], rrecv.at[w, via - 1]).wait_recv()
                _sibling_copy(piece, fsend.at[w, 1 + via], frecv.at[w, 1 + via]).start()
        for w, arr in enumerate(arrs):
            for via in (1, 2):
                theirs = _quarter_rows(arr, j ^ 3, 1 - c, via - 1)
                _sibling_copy(theirs, fsend.at[w, 1 + via], frecv.at[w, 1 + via]).wait_recv()


def _gather_drain(arrs, sems):
    dsend, drecv, rsend, rrecv, fsend, frecv = sems
    _, _, c, j = _place()
    for w, arr in enumerate(arrs):
        for dist in (1, 2):
            other = 3 - dist
            _chip_copy(_half_rows(arr, j, c), dist, dsend.at[w, dist - 1], drecv.at[w, dist - 1]).wait_send()
            _chip_copy(_quarter_rows(arr, j ^ dist, c, other - 1), other,
                       rsend.at[w, other - 1], rrecv.at[w, other - 1]).wait_send()
            _sibling_copy(_half_rows(arr, j ^ dist, c), fsend.at[w, dist - 1], frecv.at[w, dist - 1]).wait_send()
            _sibling_copy(_quarter_rows(arr, j ^ 3, c, dist - 1),
                          fsend.at[w, 1 + dist], frecv.at[w, 1 + dist]).wait_send()


def _gather_neighbours(arrs, sems):
    _gather_land(arrs, sems, 1)
    _gather_land(arrs, sems, 2)


def _gather_finish(arrs, sems):
    _gather_land(arrs, sems, 3)
    _gather_drain(arrs, sems)


def _gather_job(arrs):
    n = len(arrs)
    return _Job(arrs, [SDS(a.shape, a.dtype) for a in arrs], _gather_sems(n),
                lambda ins, outs, sems: _gather_start(outs, sems),
                lambda ins, outs, sems: _gather_finish(outs, sems), {k: k for k in range(n)},
                mid=lambda ins, outs, sems: _gather_neighbours(outs, sems))


def _exchange_job(arrs, out_shapes, n, copies):
    def start(ins, outs, sems):
        for cp in copies(ins, outs, sems[0], sems[1]):
            cp.start()

    def finish(ins, outs, sems):
        for cp in copies(ins, outs, sems[0], sems[1]):
            cp.wait()

    return _Job(arrs, out_shapes, [pltpu.SemaphoreType.DMA((n,))] * 2, start, finish)


def _pair_exchange_job(grads):
    def copies(ins, outs, send_sem, recv_sem):
        x, y, c, _ = _place()
        res = []
        for w in range(len(grads)):
            half = ins[w].shape[1] // 2
            theirs = pl.ds(pl.multiple_of((1 - c) * half, 8), half)
            res.append(pltpu.make_async_remote_copy(
                src_ref=ins[w].at[:, theirs, :], dst_ref=outs[w], send_sem=send_sem.at[w],
                recv_sem=recv_sem.at[w], device_id=(x, y, 1 - c), device_id_type=MESH))
        return res

    return _exchange_job(grads, [SDS((NCHIP, g.shape[1] // 2, g.shape[2]), F32) for g in grads],
                         len(grads), copies)


def _row_tile(rows):
    return 176 if rows % 176 == 0 and rows % 128 else 128


def _pair_sum(name, place, g, sib):
    half, cols = sib.shape[1], sib.shape[2]
    tr = _row_tile(half)
    nt = half // tr
    mine = nt if g.shape[1] == 2 * half else 0

    def body(pc_ref, g_ref, s_ref, own_ref, out_ref):
        del pc_ref
        v = g_ref[...] + s_ref[...]
        out_ref[...] = v.astype(BF16)

        @pl.when(pl.program_id(1) == 0)
        def _():
            own_ref[...] = v

    return pl.pallas_call(
        body, name=name,
        grid_spec=pltpu.PrefetchScalarGridSpec(
            num_scalar_prefetch=1, grid=(nt, NCHIP),
            in_specs=[pl.BlockSpec((None, tr, cols), lambda i, s, pc: (pc[0] ^ s, pc[1] * mine + i, 0)),
                      pl.BlockSpec((None, tr, cols), lambda i, s, pc: (pc[0] ^ s, i, 0))],
            out_specs=[pl.BlockSpec((tr, cols), lambda i, s, pc: (i, 0)),
                       pl.BlockSpec((None, tr, cols), lambda i, s, pc: (s, i, 0))]),
        out_shape=[SDS((half, cols), F32), SDS((NCHIP, half, cols), BF16)],
        compiler_params=_cparams(),
    )(place, g, sib)


def _chip_exchange_job(parts):
    def copies(ins, outs, send_sem, recv_sem):
        x, y, c, _ = _place()
        res = []
        for w in range(len(parts)):
            for s in range(1, NCHIP):
                cx, cy = _chip_at(x, y, s)
                k = w * (NCHIP - 1) + s - 1
                res.append(pltpu.make_async_remote_copy(
                    src_ref=ins[w].at[s], dst_ref=outs[w].at[s - 1], send_sem=send_sem.at[k],
                    recv_sem=recv_sem.at[k], device_id=(cx, cy, c), device_id_type=MESH))
        return res

    return _exchange_job(parts, [SDS((NCHIP - 1,) + p.shape[1:], BF16) for p in parts],
                         len(parts) * (NCHIP - 1), copies)


def _chip_sum(name, own, rem):
    half, cols = own.shape
    tr = _row_tile(half)

    def body(own_ref, rem_ref, out_ref):
        out_ref[...] = ((own_ref[...] + rem_ref[0].astype(F32)) + rem_ref[1].astype(F32)) + rem_ref[2].astype(F32)

    return pl.pallas_call(
        body, name=name, grid=(half // tr,),
        in_specs=[pl.BlockSpec((tr, cols), lambda i: (i, 0)),
                  pl.BlockSpec((NCHIP - 1, tr, cols), lambda i: (0, i, 0))],
        out_specs=pl.BlockSpec((tr, cols), lambda i: (i, 0)),
        out_shape=SDS((half, cols), F32),
        compiler_params=_cparams(),
    )(own, rem)


def _share_halves_job(halves):
    def copies(ins, outs, send_sem, recv_sem):
        x, y, c, _ = _place()
        return [pltpu.make_async_remote_copy(
            src_ref=ins[w], dst_ref=outs[w], send_sem=send_sem.at[w], recv_sem=recv_sem.at[w],
            device_id=(x, y, 1 - c), device_id_type=MESH) for w in range(len(halves))]

    return _exchange_job(halves, [SDS(h.shape, F32) for h in halves], len(halves), copies)


def _adamw_math(w, g, m, v):
    m = B1 * m + (1.0 - B1) * g
    v = B2 * v + (1.0 - B2) * (g * g)
    m_hat = m / (1.0 - B1 ** STEP)
    v_hat = v / (1.0 - B2 ** STEP)
    delta = -LR * (m_hat / (jnp.sqrt(v_hat) + AEPS) + WD * w)
    return delta, m, v


def _adamw(name, place, w, own, sib, m, v):
    rows, cols = w.shape
    half = rows // 2
    tr = 352 if half % 352 == 0 else min(256, half)
    nt = half // tr

    def body(pc_ref, w_ref, own_ref, sib_ref, m_ref, v_ref, g_ref, d_ref, mo_ref, vo_ref):
        g = jnp.where(pl.program_id(0) == pc_ref[1], own_ref[...], sib_ref[...])
        d, mn, vn = _adamw_math(w_ref[...], g, m_ref[...], v_ref[...])
        g_ref[...] = g
        d_ref[...] = d
        mo_ref[...] = mn
        vo_ref[...] = vn

    full = pl.BlockSpec((tr, cols), lambda h, i, pc: (h * nt + i, 0))
    part = pl.BlockSpec((tr, cols), lambda h, i, pc: (i, 0))
    return pl.pallas_call(
        body, name=name,
        grid_spec=pltpu.PrefetchScalarGridSpec(
            num_scalar_prefetch=1, grid=(2, nt),
            in_specs=[full, part, part, full, full], out_specs=[full] * 4),
        out_shape=[SDS((rows, cols), F32)] * 4,
        compiler_params=_cparams(),
    )(place, w, own, sib, m, v)


def _small_allreduce_adamw(sp, w, m, v):
    shape = sp.shape

    def body(sp_ref, w_ref, m_ref, v_ref, g_ref, d_ref, mo_ref, vo_ref,
             sib_s, pair_s, chip_s, send_sem, recv_sem):
        x, y, c, j = _place()
        cp = pltpu.make_async_remote_copy(
            src_ref=sp_ref, dst_ref=sib_s, send_sem=send_sem.at[0], recv_sem=recv_sem.at[0],
            device_id=(x, y, 1 - c), device_id_type=MESH)
        cp.start()
        cp.wait()
        pair_s[...] = sp_ref[...] + sib_s[...]
        cps = []
        for s in range(1, NCHIP):
            cx, cy = _chip_at(x, y, s)
            cp = pltpu.make_async_remote_copy(
                src_ref=pair_s, dst_ref=chip_s.at[s], send_sem=send_sem.at[s], recv_sem=recv_sem.at[s],
                device_id=(cx, cy, c), device_id_type=MESH)
            cp.start()
            cps.append(cp)
        chip_s[0] = pair_s[...]
        for cp in cps:
            cp.wait()
        tot = chip_s[j]
        for k in range(1, NCHIP):
            tot = tot + chip_s[k ^ j]
        g_ref[...] = tot
        d, mn, vn = _adamw_math(w_ref[...], tot, m_ref[...], v_ref[...])
        d_ref[...] = d
        mo_ref[...] = mn
        vo_ref[...] = vn

    vm = pl.BlockSpec(memory_space=pltpu.VMEM)
    return pl.pallas_call(
        body, name="small_allreduce_adamw",
        in_specs=[vm] * 4, out_specs=[vm] * 4, out_shape=[SDS(shape, F32)] * 4,
        scratch_shapes=[pltpu.VMEM(shape, F32), pltpu.VMEM(shape, F32), pltpu.VMEM((NCHIP,) + shape, F32),
                        pltpu.SemaphoreType.DMA((NCHIP,)), pltpu.SemaphoreType.DMA((NCHIP,))],
        compiler_params=pltpu.CompilerParams(has_side_effects=True),
    )(sp, w, m, v)


def _pack_small(first, mix, ln_g, ln_b, b_s, lbt, hn, ffn, fin, w_s):
    rows = [first.reshape(1, D), mix.reshape(1, D), ln_g.reshape(1, D), ln_b.reshape(1, D),
            b_s.reshape(1, D), lbt.reshape(2, D), hn.reshape(1, D), ffn.reshape(1, D), fin.reshape(1, D),
            jnp.zeros((6, D), F32)]
    return jnp.concatenate(rows + [w_s.reshape(NG, GCH, GCH).transpose(1, 0, 2).reshape(GCH, D)], axis=0)


def _unpack_small(p):
    w_s = p[16:].reshape(GCH, NG, GCH).transpose(1, 0, 2).reshape(1, NG, GCH, GCH)
    return dict(norm_mix_g=p[1:2], gmlp_ln_g=p[2:3], gmlp_ln_b=p[3:4], gmlp_b_s=p[4].reshape(1, NG, GCH),
                hgrn_lb_table=p[5:7], hgrn_norm_g=p[7:8], norm_ffn_g=p[8:9], norm_final_g=p[9],
                gmlp_w_s=w_s)


SMALL = ("norm_mix_g", "gmlp_ln_g", "gmlp_ln_b", "gmlp_w_s", "gmlp_b_s", "hgrn_lb_table", "hgrn_norm_g",
         "norm_ffn_g", "norm_final_g")
BIG = ("w_in", "w_gate_up", "w_branch_a", "w_branch_b", "w_out", "w_down")
ORDER = ("norm_mix_g", "w_in", "gmlp_ln_g", "gmlp_ln_b", "gmlp_w_s", "gmlp_b_s", "hgrn_lb_table",
         "hgrn_norm_g", "w_branch_a", "w_branch_b", "w_out", "norm_ffn_g", "w_gate_up", "w_down",
         "norm_final_g")


def kernel(x, norm_mix_g, w_in, gmlp_ln_g, gmlp_ln_b, gmlp_w_s, gmlp_b_s, hgrn_lb_table, hgrn_norm_g, w_branch_a, w_branch_b, w_out, norm_ffn_g, w_gate_up, w_down, norm_final_g, loss_target, m_norm_mix_g, m_w_in, m_gmlp_ln_g, m_gmlp_ln_b, m_gmlp_w_s, m_gmlp_b_s, m_hgrn_lb_table, m_hgrn_norm_g, m_w_branch_a, m_w_branch_b, m_w_out, m_norm_ffn_g, m_w_gate_up, m_w_down, m_norm_final_g, v_norm_mix_g, v_w_in, v_gmlp_ln_g, v_gmlp_ln_b, v_gmlp_w_s, v_gmlp_b_s, v_hgrn_lb_table, v_hgrn_norm_g, v_w_branch_a, v_w_branch_b, v_w_out, v_norm_ffn_g, v_w_gate_up, v_w_down, v_norm_final_g):
    args = dict(locals())
    T = x.shape[1]
    xs = x.reshape(T, D)
    target = loss_target.reshape(T, D)
    big = {n: args[n].reshape(args[n].shape[1:]) for n in BIG}
    big_m = {n: args["m_" + n].reshape(args[n].shape[1:]) for n in BIG}
    big_v = {n: args["v_" + n].reshape(args[n].shape[1:]) for n in BIG}

    x_i, y_i, c_i = lax.axis_index("x"), lax.axis_index("y"), lax.axis_index("c")
    place = jnp.stack([2 * x_i + y_i, c_i]).astype(jnp.int32)
    cast = {n: _cast_shard("cast_" + n, place, big[n]) for n in BIG}
    tril = jnp.tril(jnp.ones((GCH, GCH), bool))
    wm = jnp.where(tril, gmlp_w_s[0], 0.0).astype(BF16)
    wm_t = jnp.swapaxes(wm, 1, 2)
    b_t = gmlp_b_s[0].T

    (proj, hb), w_in4, (w_a4,) = _proj_fwd(place, xs, norm_mix_g, cast["w_in"], [cast["w_branch_a"]])
    (ab, y_a), (w_b4, w_out4) = _gmlp_fwd(
        proj, gmlp_ln_g, gmlp_ln_b, wm, b_t, w_a4.reshape(D, D),
        job=_gather_job([cast["w_branch_b"], cast["w_out"]]))
    (o_raw, obb, st_before), (w_gu4, w_down4) = _hgrn_fwd(
        proj, hgrn_lb_table, hgrn_norm_g, job=_gather_job([cast["w_gate_up"], cast["w_down"]]))
    w_a, w_b, w_o = (w.reshape(D, D) for w in (w_a4, w_b4, w_out4))
    w_dn = w_down4.reshape(FF, D)
    y_b, mgb, x1 = _merge_fwd(xs, y_a, obb, proj, w_b, w_o)
    act, dx2b, h2b, dgu4, dx1, dx1b, acc_ffn = _ffn_fwd_bwd(
        x1, target, norm_ffn_g, norm_final_g.reshape(1, D), w_gu4, w_dn)

    grads, owns, parts, halves, sibh = {}, {}, {}, {}, {}

    def pair_sums(names, sibs):
        for n, s in zip(names, sibs):
            owns[n], parts[n] = _pair_sum("rs_pair_sum_" + n, place, grads[n], s)

    def chip_sums(names, got):
        for n, r in zip(names, got):
            halves[n] = _chip_sum("rs_chip_sum_" + n, owns[n], r)

    ffn, mix = ("w_gate_up", "w_down"), ("w_branch_a", "w_branch_b", "w_out")
    grads["w_gate_up"], _ = _dw_gate_up(h2b, dgu4)
    grads["w_down"], _ = _dw_down(act, dx2b)
    (dya, dyb, da, dob, dproj), got = _merge_bwd(
        dx1b, y_a, y_b, proj, w_o, w_a, w_b, job=_pair_exchange_job([grads[n] for n in ffn]))
    pair_sums(ffn, got)
    grads["w_branch_a"], _ = _dw_square("dw_branch_a", ab, dya)
    grads["w_branch_b"], _ = _dw_square("dw_branch_b", obb, dyb)
    grads["w_out"], _ = _dw_square("dw_out", mgb, dx1b)
    (dproj, acc_hgrn), got = _hgrn_bwd(
        dproj, dob, o_raw, proj, st_before, hgrn_lb_table, hgrn_norm_g,
        job=_join_jobs(_chip_exchange_job([parts[n] for n in ffn]), _pair_exchange_job([grads[n] for n in mix])))
    chip_sums(ffn, got[:2])
    pair_sums(mix, got[2:])
    dproj, acc_ln, dws, dmix = _gmlp_bwd(dproj, da, proj, gmlp_ln_g, gmlp_ln_b, wm, wm_t, b_t)
    for_sibling, got = _dw_in_half(
        "dw_in_sibling_half", place, hb, dproj, False,
        job=_join_jobs(_share_halves_job([halves[n] for n in ffn]), _chip_exchange_job([parts[n] for n in mix])))
    sibh.update(zip(ffn, got[:2]))
    chip_sums(mix, got[2:])
    grads["w_in"], got = _dw_in_half(
        "dw_in_own_half", place, hb, dproj, True, job=_share_halves_job([for_sibling]))
    pair_sums(("w_in",), got)
    (grad_x, acc_mix), got = _proj_bwd(
        dproj, w_in4, xs, dx1, norm_mix_g,
        job=_join_jobs(_chip_exchange_job([parts["w_in"]]), _share_halves_job([halves[n] for n in mix])))
    chip_sums(("w_in",), got[:1])
    sibh.update(zip(mix, got[1:]))
    (sibh["w_in"],) = _run_job(_share_halves_job([halves["w_in"]]), "rs_share_halves_w_in")
    out = {}
    for n in BIG:
        g, d, mn, vn = _adamw("adamw_" + n, place, big[n], halves[n], sibh[n], big_m[n], big_v[n])
        shp = args[n].shape
        out[n] = (g.reshape(shp), d.reshape(shp), mn.reshape(shp), vn.reshape(shp))

    lbv = jax.nn.sigmoid(hgrn_lb_table[0] - hgrn_lb_table[1])
    d_t0 = jnp.sum(acc_hgrn[0], axis=0) * lbv * (1.0 - lbv)
    loss_row = jnp.zeros((D,), F32).at[0].set(jnp.sum(acc_ffn[0]))
    dws_m = jnp.where(tril[:, None, :], dws.reshape(GCH, NG, GCH), 0.0).transpose(1, 0, 2)
    db_s = jnp.sum(dmix.reshape(GCH, NG, GCH), axis=-1).T
    sp = _pack_small(loss_row, jnp.sum(acc_mix, 0), jnp.sum(acc_ln[0], 0), jnp.sum(acc_ln[1], 0), db_s,
                     jnp.stack([d_t0, -d_t0]), jnp.sum(acc_hgrn[1], 0), jnp.sum(acc_ffn[2], 0),
                     jnp.sum(acc_ffn[1], 0), dws_m)
    zero = jnp.zeros((D,), F32)

    def pack(prefix):
        a = lambda n: args[prefix + n]
        return _pack_small(zero, a("norm_mix_g"), a("gmlp_ln_g"), a("gmlp_ln_b"), a("gmlp_b_s"),
                           a("hgrn_lb_table"), a("hgrn_norm_g"), a("norm_ffn_g"), a("norm_final_g"),
                           a("gmlp_w_s"))

    packed = _small_allreduce_adamw(sp, pack(""), pack("m_"), pack("v_"))
    loss = packed[0][0, 0]
    small = [_unpack_small(p) for p in packed]
    for n in SMALL:
        out[n] = tuple(s[n] for s in small)
    return (loss, grad_x.reshape(x.shape), *[out[n][0] for n in ORDER], *[out[n][1] for n in ORDER],
            *[out[n][2] for n in ORDER], *[out[n][3] for n in ORDER])
```

```python
import functools
import math

import jax
import jax.numpy as jnp
from jax import lax
from jax.experimental import pallas as pl
from jax.experimental.pallas import tpu as pltpu

F32 = jnp.float32
BF16 = jnp.bfloat16
SDS = jax.ShapeDtypeStruct
MESH = pl.DeviceIdType.MESH
ANY = pl.BlockSpec(memory_space=pl.ANY)

D = 1024
NIN = 8
NG = 8
GCH = 128
NH = 8
HD = 128
HCH = 64
HGRN_HB = 4
HW = HGRN_HB * HD
DW_TOKENS = 2048
FF = 2816
FFS = 1408
NCHIP = 4
EPS = 1e-6
QSCALE = HD ** -0.5
GELU_C0 = math.sqrt(2.0 / math.pi)
GELU_C1 = 0.044715
LR, B1, B2, AEPS, WD, STEP = 0.001, 0.9, 0.999, 1e-08, 0.01, 10
VMEM_LIMIT_V7X = 56 * 1024 * 1024
SP_ROWS = 144


def _cparams(**kw):
    return pltpu.CompilerParams(vmem_limit_bytes=VMEM_LIMIT_V7X, **kw)


def _mm(a, b):
    return jnp.dot(a, b, preferred_element_type=F32)


def _mm_nt(a, b):
    return lax.dot_general(a, b, (((1,), (1,)), ((), ())), preferred_element_type=F32)


def _mm_tn(a, b):
    return lax.dot_general(a, b, (((0,), (0,)), ((), ())), preferred_element_type=F32)


def _rows8(x):
    r, c = x.shape
    return jnp.sum(x.reshape(r // 8, 8, c), axis=0)


def _mean(x):
    return jnp.mean(x, axis=-1, keepdims=True)


def _sigmoid(x):
    return 1.0 / (1.0 + jnp.exp(-x))


def _gelu(x):
    t = jnp.tanh(GELU_C0 * (x + GELU_C1 * x * x * x))
    return 0.5 * x * (1.0 + t), t


def _gelu_grad(x, t):
    return 0.5 * (1.0 + t) + 0.5 * x * (1.0 - t * t) * (GELU_C0 * (1.0 + 3.0 * GELU_C1 * x * x))


def _component_of(group):
    return jnp.where(group < 6, (group + 4) % 6, group)


def _proj_fwd(place, x, g_mix, w_in4, later):
    T = x.shape[0]
    tm = min(512, T)
    ni = T // tm
    n = len(later)

    def body(pc_ref, x_ref, g_ref, *rest):
        proj_ref, h_ref, w_all = rest[1 + n:4 + n]
        gathered = rest[4 + n:4 + 2 * n]
        hs, wbuf, wsem = rest[4 + 2 * n:7 + 2 * n]
        w_sems, later_sems = rest[7 + 2 * n:13 + 2 * n], rest[13 + 2 * n:]
        jp, i = pl.program_id(0), pl.program_id(1)
        s, blk = jp // 2, jp % 2

        @pl.when((jp == 0) & (i == 0))
        def _():
            _gather_start([w_all], w_sems)
            _gather_start(gathered, later_sems)

        @pl.when(jp == 0)
        def _():
            xv = x_ref[...]
            r = lax.rsqrt(_mean(xv * xv) + EPS)
            hb = (xv * r * g_ref[...]).astype(BF16)
            hs[i] = hb
            h_ref[...] = hb

        @pl.when((jp == 2) & (i == 0))
        def _():
            _gather_land([w_all], w_sems, 1)
            _gather_land([w_all], w_sems, 2)

        @pl.when((jp == 4) & (i == 0))
        def _():
            _gather_neighbours(gathered, later_sems)

        @pl.when((jp == 6) & (i == 0))
        def _():
            _gather_land([w_all], w_sems, 3)

        @pl.when(i == 0)
        def _():
            cols = pl.ds(pl.multiple_of(blk * D, 128), D)
            cp = pltpu.make_async_copy(w_all.at[pc_ref[0] ^ s, :, cols], wbuf, wsem)
            cp.start()
            cp.wait()

        proj_ref[...] = _mm(hs[i], wbuf[...])

        @pl.when((jp == NIN - 1) & (i == ni - 1))
        def _():
            _gather_drain([w_all], w_sems)
            _gather_finish(gathered, later_sems)

    tile = lambda jp, i, pc: (jnp.where(jp == 0, i, ni - 1), 0)
    res = pl.pallas_call(
        body, name="proj_fwd",
        grid_spec=pltpu.PrefetchScalarGridSpec(
            num_scalar_prefetch=1, grid=(NIN, ni),
            in_specs=[pl.BlockSpec((tm, D), tile), pl.BlockSpec((1, D), lambda jp, i, pc: (0, 0))] + [ANY] * (1 + n),
            out_specs=[pl.BlockSpec((None, tm, D), lambda jp, i, pc: (2 * (pc[0] ^ (jp // 2)) + jp % 2, i, 0)),
                       pl.BlockSpec((tm, D), tile)] + [ANY] * (1 + n),
            scratch_shapes=[pltpu.VMEM((ni, tm, D), BF16), pltpu.VMEM((D, D), BF16), pltpu.SemaphoreType.DMA]
            + _gather_sems(1) + _gather_sems(n)),
        out_shape=[SDS((NIN, T, D), F32), SDS((T, D), BF16), SDS(w_in4.shape, BF16)]
        + [SDS(a.shape, a.dtype) for a in later],
        input_output_aliases={3 + k: 2 + k for k in range(1 + n)},
        compiler_params=_cparams(has_side_effects=True),
    )(place, x, g_mix, w_in4, *later)
    return res[:2], res[2], res[3:]


def _layer_norm_stats(gv):
    mu = _mean(gv)
    xc = gv - mu
    rs = lax.rsqrt(_mean(xc * xc) + EPS)
    return xc * rs, rs


def _gmlp_fwd(proj, ln_g, ln_b, wm, b_t, w_a, job=None):
    T = proj.shape[1]
    tm = min(256, T)

    def body(u_ref, v_ref, lg_ref, lb_ref, wm_ref, bt_ref, wa_ref, a_ref, ya_ref, a_s):
        gu, _ = _gelu(u_ref[...])
        gv, _ = _gelu(v_ref[...])
        vhat, _ = _layer_norm_stats(gv)
        vnb = (vhat * lg_ref[...] + lb_ref[...]).astype(BF16)
        for ch in range(tm // GCH):
            rows = slice(GCH * ch, GCH * (ch + 1))
            for g in range(NG):
                cols = slice(128 * g, 128 * (g + 1))
                mixed = _mm(wm_ref[g], vnb[rows, cols]) + bt_ref[:, g:g + 1]
                a_s[rows, cols] = gu[rows, cols] * mixed
        ab = a_s[...].astype(BF16)
        a_ref[...] = ab
        ya_ref[...] = _mm(ab, wa_ref[...])

    row = lambda i: (0, 0)
    return _call(
        body, name="gmlp_fwd", grid=(T // tm,), job=job, args=(proj, proj, ln_g, ln_b, wm, b_t, w_a),
        in_specs=[pl.BlockSpec((None, tm, D), lambda i: (0, i, 0)), pl.BlockSpec((None, tm, D), lambda i: (1, i, 0)),
                  pl.BlockSpec((1, D), row), pl.BlockSpec((1, D), row),
                  pl.BlockSpec((NG, GCH, GCH), lambda i: (0, 0, 0)), pl.BlockSpec((GCH, NG), row),
                  pl.BlockSpec((D, D), row)],
        out_specs=[pl.BlockSpec((tm, D), lambda i: (i, 0)), pl.BlockSpec((tm, D), lambda i: (i, 0))],
        out_shape=[SDS((T, D), BF16), SDS((T, D), F32)],
        scratch_shapes=[pltpu.VMEM((tm, D), F32)])


def _cumsum64(x, row):
    for s in (1, 2, 4, 8, 16, 32):
        x = x + jnp.where(row >= s, pltpu.roll(x, s, 0), 0.0)
    return x


def _revcumsum64(x, row):
    n = x.shape[0]
    for s in (1, 2, 4, 8, 16, 32):
        x = x + jnp.where(row < HCH - s, pltpu.roll(x, n - s, 0), 0.0)
    return x


def _head_mean(x):
    parts = [jnp.broadcast_to(_mean(x[:, HD * h:HD * (h + 1)]), (x.shape[0], HD)) for h in range(x.shape[1] // HD)]
    return jnp.concatenate(parts, axis=1)


def _seg_sum(x):
    n, c = x.shape
    s = jnp.sum(x.reshape(n // HCH, HCH, c), axis=1, keepdims=True)
    return jnp.broadcast_to(s, (n // HCH, HCH, c)).reshape(n, c)


def _hgrn_gates(fl, lbv, row):
    s = _sigmoid(fl)
    f = lbv + (1.0 - lbv) * s
    a = _cumsum64(jnp.log(f), row)
    a_mid = _seg_sum(jnp.where(row == HCH // 2 - 1, a, 0.0))
    a_last = _seg_sum(jnp.where(row == HCH - 1, a, 0.0))
    return s, f, a, a_mid, a_last


def _hgrn_fwd(proj, lb_table, norm_g, job=None):
    T = proj.shape[1]
    tb = min(512, T)
    nc = tb // HCH

    def body(q_ref, fl_ref, v_ref, g_ref, lbt_ref, gn_ref, o_ref, ob_ref, stb_ref, st_s, o_s):
        @pl.when(pl.program_id(1) == 0)
        def _():
            st_s[...] = jnp.zeros_like(st_s)

        row = lax.broadcasted_iota(jnp.int32, (tb, HW), 0) & (HCH - 1)
        lbv = _sigmoid(lbt_ref[0:1, :] - lbt_ref[1:2, :])
        _, f, a, a_mid, a_last = _hgrn_gates(fl_ref[...], lbv, row)
        k = 1.0 - f
        qs = q_ref[...] * QSCALE
        q_in = (qs * jnp.exp(a - a_mid)).astype(BF16)
        k_in = (k * jnp.exp(a_mid - a)).astype(BF16)
        q_a = (qs * jnp.exp(a)).astype(BF16)
        k_d = (k * jnp.exp(a_last - a)).astype(BF16)
        dec = jnp.exp(a_last)
        vb = v_ref[...].astype(BF16)
        tri = (lax.broadcasted_iota(jnp.int32, (HCH, HCH), 0)
               >= lax.broadcasted_iota(jnp.int32, (HCH, HCH), 1))
        for c in range(nc):
            sl = slice(HCH * c, HCH * (c + 1))
            for hh in range(HGRN_HB):
                hs = slice(HD * hh, HD * (hh + 1))
                st = st_s[hh]
                stb_ref[hh, c] = st
                sc = jnp.where(tri, _mm_nt(q_in[sl, hs], k_in[sl, hs]), 0.0)
                o_s[sl, hs] = _mm(sc.astype(BF16), vb[sl, hs]) + _mm_nt(q_a[sl, hs], st.astype(BF16))
                d64 = dec[sl, hs]
                st_s[hh] = st * jnp.concatenate([d64, d64], axis=0) + _mm_tn(vb[sl, hs], k_d[sl, hs])
        o = o_s[...]
        r = lax.rsqrt(_head_mean(o * o) + EPS)
        g = g_ref[...]
        o_ref[...] = o
        ob_ref[...] = (o * r * gn_ref[...] * (g * _sigmoid(g))).astype(BF16)

    def col(off):
        return pl.BlockSpec((None, tb, HW), lambda h, cb: (off, cb, h))

    return _call(
        body, name="hgrn_fwd", grid=(NH // HGRN_HB, T // tb), job=job,
        args=(proj, proj, proj, proj, lb_table, norm_g),
        in_specs=[col(2), col(3), col(4), col(5),
                  pl.BlockSpec((2, HW), lambda h, cb: (0, h)), pl.BlockSpec((1, HW), lambda h, cb: (0, h))],
        out_specs=[pl.BlockSpec((tb, HW), lambda h, cb: (cb, h)), pl.BlockSpec((tb, HW), lambda h, cb: (cb, h)),
                   pl.BlockSpec((HGRN_HB, nc, HD, HD), lambda h, cb: (h, cb, 0, 0))],
        out_shape=[SDS((T, D), F32), SDS((T, D), BF16), SDS((NH, T // HCH, HD, HD), F32)],
        scratch_shapes=[pltpu.VMEM((HGRN_HB, HD, HD), F32), pltpu.VMEM((tb, HW), F32)])


def _merge_fwd(x, y_a, ob, proj, w_b, w_out):
    T = x.shape[0]
    tm = min(512, T)

    def body(x_ref, ya_ref, ob_ref, ga_ref, gb_ref, wb_ref, wo_ref, yb_ref, mg_ref, x1_ref):
        yb = _mm(ob_ref[...], wb_ref[...])
        merged = (_sigmoid(ga_ref[...]) * ya_ref[...] + _sigmoid(gb_ref[...]) * yb).astype(BF16)
        yb_ref[...] = yb
        mg_ref[...] = merged
        x1_ref[...] = x_ref[...] + _mm(merged, wo_ref[...])

    t = lambda i: (i, 0)
    w = lambda i: (0, 0)
    return pl.pallas_call(
        body, name="merge_fwd", grid=(T // tm,),
        in_specs=[pl.BlockSpec((tm, D), t), pl.BlockSpec((tm, D), t), pl.BlockSpec((tm, D), t),
                  pl.BlockSpec((None, tm, D), lambda i: (6, i, 0)), pl.BlockSpec((None, tm, D), lambda i: (7, i, 0)),
                  pl.BlockSpec((D, D), w), pl.BlockSpec((D, D), w)],
        out_specs=[pl.BlockSpec((tm, D), t)] * 3,
        out_shape=[SDS((T, D), F32), SDS((T, D), BF16), SDS((T, D), F32)],
        compiler_params=_cparams(),
    )(x, y_a, ob, proj, proj, w_b, w_out)


def _ffn_fwd_bwd(x1, target, g_ffn, g_fin, w_gu4, w_down):
    T = x1.shape[0]
    tm = min(256, T)
    inv_d = 1.0 / D

    def body(x1_ref, tg_ref, gf_ref, gn_ref, wgu_ref, wd_ref,
             act_ref, dx2b_ref, h2b_ref, dgu_ref, dx1_ref, dx1b_ref, acc_ref):
        @pl.when(pl.program_id(0) == 0)
        def _():
            acc_ref[...] = jnp.zeros_like(acc_ref)

        x1v = x1_ref[...]
        gf = gf_ref[...]
        gn = gn_ref[...]
        rr1 = lax.rsqrt(_mean(x1v * x1v) + EPS)
        x1n = x1v * rr1
        h2b = (x1n * gf).astype(BF16)
        h2b_ref[...] = h2b
        p = [_mm(h2b, wgu_ref[k]) for k in range(NCHIP)]
        sg = [_sigmoid(p[0]), _sigmoid(p[1])]
        si = [p[0] * sg[0], p[1] * sg[1]]
        x2 = x1v
        for k in range(2):
            actk = (si[k] * p[2 + k]).astype(BF16)
            act_ref[:, FFS * k:FFS * (k + 1)] = actk
            x2 = x2 + _mm(actk, wd_ref[FFS * k:FFS * (k + 1), :])
        rr2 = lax.rsqrt(_mean(x2 * x2) + EPS)
        x2n = x2 * rr2
        e = x2n * gn - tg_ref[...]
        acc_ref[0] += _rows8(e * e) * (0.5 * inv_d)
        dy = e * inv_d
        acc_ref[1] += _rows8(dy * x2n)
        dxn = dy * gn
        dx2 = rr2 * (dxn - x2n * _mean(dxn * x2n))
        dx2b = dx2.astype(BF16)
        dx2b_ref[...] = dx2b
        dh2 = None
        for k in range(2):
            dact = _mm_nt(dx2b, wd_ref[FFS * k:FFS * (k + 1), :])
            dgate = (dact * p[2 + k] * (sg[k] * (1.0 + p[k] * (1.0 - sg[k])))).astype(BF16)
            dup = (dact * si[k]).astype(BF16)
            dgu_ref[k] = dgate
            dgu_ref[2 + k] = dup
            part = _mm_nt(dgate, wgu_ref[k]) + _mm_nt(dup, wgu_ref[2 + k])
            dh2 = part if dh2 is None else dh2 + part
        acc_ref[2] += _rows8(dh2 * x1n)
        dxn1 = dh2 * gf
        dx1 = dx2 + rr1 * (dxn1 - x1n * _mean(dxn1 * x1n))
        dx1_ref[...] = dx1
        dx1b_ref[...] = dx1.astype(BF16)

    t = lambda i: (i, 0)
    w = lambda i: (0, 0)
    one = pl.Buffered(1)
    return pl.pallas_call(
        body, name="ffn_fwd_bwd", grid=(T // tm,),
        in_specs=[pl.BlockSpec((tm, D), t), pl.BlockSpec((tm, D), t),
                  pl.BlockSpec((1, D), w), pl.BlockSpec((1, D), w),
                  pl.BlockSpec((NCHIP, D, FFS), lambda i: (0, 0, 0), pipeline_mode=one),
                  pl.BlockSpec((FF, D), w, pipeline_mode=one)],
        out_specs=[pl.BlockSpec((tm, FF), t), pl.BlockSpec((tm, D), t), pl.BlockSpec((tm, D), t),
                   pl.BlockSpec((NCHIP, tm, FFS), lambda i: (0, i, 0)),
                   pl.BlockSpec((tm, D), t), pl.BlockSpec((tm, D), t),
                   pl.BlockSpec((3, 8, D), lambda i: (0, 0, 0))],
        out_shape=[SDS((T, FF), BF16), SDS((T, D), BF16), SDS((T, D), BF16),
                   SDS((NCHIP, T, FFS), BF16), SDS((T, D), F32), SDS((T, D), BF16),
                   SDS((3, 8, D), F32)],
        compiler_params=_cparams(),
    )(x1, target, g_ffn, g_fin, w_gu4, w_down)


def _merge_bwd(dx1b, y_a, y_b, proj, w_out, w_a, w_b, job=None):
    T = dx1b.shape[0]
    tm = min(512, T)

    def body(dx_ref, ya_ref, yb_ref, ga_ref, gb_ref, wo_ref, wa_ref, wb_ref,
             dya_ref, dyb_ref, da_ref, dob_ref, dp_ref):
        dm = _mm_nt(dx_ref[...], wo_ref[...])
        sa = _sigmoid(ga_ref[...])
        sb = _sigmoid(gb_ref[...])
        dya = (dm * sa).astype(BF16)
        dyb = (dm * sb).astype(BF16)
        dya_ref[...] = dya
        dyb_ref[...] = dyb
        dp_ref[0] = (dm * ya_ref[...] * sa * (1.0 - sa)).astype(BF16)
        dp_ref[1] = (dm * yb_ref[...] * sb * (1.0 - sb)).astype(BF16)
        da_ref[...] = _mm_nt(dya, wa_ref[...])
        dob_ref[...] = _mm_nt(dyb, wb_ref[...])

    t = lambda i: (i, 0)
    w = lambda i: (0, 0)
    return _call(
        body, name="merge_bwd", grid=(T // tm,),
        in_specs=[pl.BlockSpec((tm, D), t), pl.BlockSpec((tm, D), t), pl.BlockSpec((tm, D), t),
                  pl.BlockSpec((None, tm, D), lambda i: (6, i, 0)), pl.BlockSpec((None, tm, D), lambda i: (7, i, 0)),
                  pl.BlockSpec((D, D), w), pl.BlockSpec((D, D), w), pl.BlockSpec((D, D), w)],
        out_specs=[pl.BlockSpec((tm, D), t)] * 4 + [pl.BlockSpec((2, tm, D), lambda i: (3, i, 0))],
        out_shape=[SDS((T, D), BF16), SDS((T, D), BF16), SDS((T, D), F32), SDS((T, D), F32),
                   SDS((NIN, T, D), BF16)],
        args=(dx1b, y_a, y_b, proj, proj, w_out, w_a, w_b), job=job)


def _hgrn_bwd(dproj, dob, o_raw, proj, st_before, lb_table, norm_g, job=None):
    T = dob.shape[0]
    tb = min(512, T)
    nc = tb // HCH
    nb = T // tb

    def body(dp_in, dob_ref, o_ref, q_ref, fl_ref, v_ref, g_ref, stb_ref, lbt_ref, gn_ref,
             dp_ref, acc_ref, dst_s, dqin_s, dqa_s, dkin_s, dkd_s, dv_s, ddec_s):
        del dp_in

        @pl.when(pl.program_id(1) == 0)
        def _():
            dst_s[...] = jnp.zeros_like(dst_s)
            acc_ref[...] = jnp.zeros_like(acc_ref)

        row = lax.broadcasted_iota(jnp.int32, (tb, HW), 0) & (HCH - 1)
        gn = gn_ref[...]
        lbv = _sigmoid(lbt_ref[0:1, :] - lbt_ref[1:2, :])
        o = o_ref[...]
        r = lax.rsqrt(_head_mean(o * o) + EPS)
        on = o * r
        g = g_ref[...]
        sgm = _sigmoid(g)
        dob_v = dob_ref[...]
        dp_ref[3] = (dob_v * on * gn * (sgm * (1.0 + g * (1.0 - sgm)))).astype(BF16)
        do_n = dob_v * (g * sgm)
        acc_ref[1] += _rows8(do_n * on)
        dxn = do_n * gn
        do = (r * (dxn - on * _head_mean(dxn * on))).astype(BF16)
        s, f, a, a_mid, a_last = _hgrn_gates(fl_ref[...], lbv, row)
        k = 1.0 - f
        qs = q_ref[...] * QSCALE
        e_q = jnp.exp(a - a_mid)
        e_k = jnp.exp(a_mid - a)
        e_a = jnp.exp(a)
        e_l = jnp.exp(a_last - a)
        dec = jnp.exp(a_last)
        q_in = qs * e_q
        k_in = k * e_k
        q_a = qs * e_a
        k_d = k * e_l
        q_inb, k_inb, q_ab, k_db = (z.astype(BF16) for z in (q_in, k_in, q_a, k_d))
        vb = v_ref[...].astype(BF16)
        tri = (lax.broadcasted_iota(jnp.int32, (HCH, HCH), 0)
               >= lax.broadcasted_iota(jnp.int32, (HCH, HCH), 1))
        for c in reversed(range(nc)):
            sl = slice(HCH * c, HCH * (c + 1))
            for hh in range(HGRN_HB):
                hs = slice(HD * hh, HD * (hh + 1))
                stp = stb_ref[hh, c]
                dst = dst_s[hh]
                dstb = dst.astype(BF16)
                do_c = do[sl, hs]
                v_c = vb[sl, hs]
                dqa_s[sl, hs] = _mm(do_c, stp.astype(BF16))
                dkd_s[sl, hs] = _mm(v_c, dstb)
                ddec_s[sl, hs] = jnp.broadcast_to(jnp.sum(dst * stp, axis=0, keepdims=True), (HCH, HD))
                sc = jnp.where(tri, _mm_nt(q_inb[sl, hs], k_inb[sl, hs]), 0.0).astype(BF16)
                dsc = jnp.where(tri, _mm_nt(do_c, v_c), 0.0).astype(BF16)
                dv_s[sl, hs] = _mm_nt(k_db[sl, hs], dstb) + _mm_tn(sc, do_c)
                dqin_s[sl, hs] = _mm(dsc, k_inb[sl, hs])
                dkin_s[sl, hs] = _mm_tn(dsc, q_inb[sl, hs])
                d64 = dec[sl, hs]
                dst_s[hh] = dst * jnp.concatenate([d64, d64], axis=0) + _mm_tn(do_c, q_ab[sl, hs])
        dq_in = dqin_s[...]
        dq_a = dqa_s[...]
        dk_in = dkin_s[...]
        dk_d = dkd_s[...]
        dp_ref[0] = ((dq_in * e_q + dq_a * e_a) * QSCALE).astype(BF16)
        dp_ref[2] = dv_s[...].astype(BF16)
        tq = dq_in * q_in
        tk = dk_in * k_in
        td = dk_d * k_d
        d_a = tq + dq_a * q_a - tk - td
        d_a = d_a + jnp.where(row == HCH // 2 - 1, _seg_sum(tk - tq), 0.0)
        d_a = d_a + jnp.where(row == HCH - 1, _seg_sum(td) + ddec_s[...] * dec, 0.0)
        dlf = _revcumsum64(d_a, row)
        df = dlf / f - (dk_in * e_k + dk_d * e_l)
        dp_ref[1] = (df * (1.0 - lbv) * s * (1.0 - s)).astype(BF16)
        acc_ref[0] += _rows8(df * (1.0 - s))

    def col(off):
        return pl.BlockSpec((None, tb, HW), lambda h, cb: (off, nb - 1 - cb, h))

    hb = lambda h, cb: (nb - 1 - cb, h)
    return _call(
        body, name="hgrn_bwd", grid=(NH // HGRN_HB, nb), job=job,
        args=(dproj, dob, o_raw, proj, proj, proj, proj, st_before, lb_table, norm_g),
        in_specs=[ANY, pl.BlockSpec((tb, HW), hb), pl.BlockSpec((tb, HW), hb),
                  col(2), col(3), col(4), col(5),
                  pl.BlockSpec((HGRN_HB, nc, HD, HD), lambda h, cb: (h, nb - 1 - cb, 0, 0)),
                  pl.BlockSpec((2, HW), lambda h, cb: (0, h)), pl.BlockSpec((1, HW), lambda h, cb: (0, h))],
        out_specs=[pl.BlockSpec((4, tb, HW), lambda h, cb: (0, nb - 1 - cb, h)),
                   pl.BlockSpec((2, 8, HW), lambda h, cb: (0, 0, h))],
        out_shape=[SDS(dproj.shape, BF16), SDS((2, 8, D), F32)],
        scratch_shapes=[pltpu.VMEM((HGRN_HB, HD, HD), F32)] + [pltpu.VMEM((tb, HW), F32)] * 6,
        aliases={0: 0})


def _gmlp_bwd(dproj, da, proj, ln_g, ln_b, wm, wm_t, b_t):
    T = da.shape[0]
    tm = min(256, T)

    def body(dp_in, da_ref, u_ref, v_ref, lg_ref, lb_ref, wm_ref, wmt_ref, bt_ref,
             dp_ref, acc_ref, dws_ref, dmix_ref, du_s, dvn_s):
        del dp_in

        @pl.when(pl.program_id(0) == 0)
        def _():
            acc_ref[...] = jnp.zeros_like(acc_ref)
            dws_ref[...] = jnp.zeros_like(dws_ref)
            dmix_ref[...] = jnp.zeros_like(dmix_ref)

        u = u_ref[...]
        v = v_ref[...]
        lg = lg_ref[...]
        gu, t_u = _gelu(u)
        gv, t_v = _gelu(v)
        vhat, rs = _layer_norm_stats(gv)
        vnb = (vhat * lg + lb_ref[...]).astype(BF16)
        da_v = da_ref[...]
        for ch in range(tm // GCH):
            rows = slice(GCH * ch, GCH * (ch + 1))
            for g in range(NG):
                cols = slice(128 * g, 128 * (g + 1))
                vng = vnb[rows, cols]
                mixed = _mm(wm_ref[g], vng) + bt_ref[:, g:g + 1]
                dag = da_v[rows, cols]
                dmx = dag * gu[rows, cols]
                du_s[rows, cols] = dag * mixed
                dmxb = dmx.astype(BF16)
                dws_ref[:, cols] += _mm_nt(dmxb, vng)
                dmix_ref[:, cols] += dmx
                dvn_s[rows, cols] = _mm(wmt_ref[g], dmxb)
        dp_ref[0] = (du_s[...] * _gelu_grad(u, t_u)).astype(BF16)
        dvn = dvn_s[...]
        acc_ref[0] += _rows8(dvn * vhat)
        acc_ref[1] += _rows8(dvn)
        dvh = dvn * lg
        dgv = rs * (dvh - _mean(dvh) - vhat * _mean(dvh * vhat))
        dp_ref[1] = (dgv * _gelu_grad(v, t_v)).astype(BF16)

    row = lambda i: (0, 0)
    w3 = lambda i: (0, 0, 0)
    return pl.pallas_call(
        body, name="gmlp_bwd", grid=(T // tm,),
        in_specs=[ANY, pl.BlockSpec((tm, D), lambda i: (i, 0)),
                  pl.BlockSpec((None, tm, D), lambda i: (0, i, 0)), pl.BlockSpec((None, tm, D), lambda i: (1, i, 0)),
                  pl.BlockSpec((1, D), row), pl.BlockSpec((1, D), row),
                  pl.BlockSpec((NG, GCH, GCH), w3), pl.BlockSpec((NG, GCH, GCH), w3),
                  pl.BlockSpec((GCH, NG), row)],
        out_specs=[pl.BlockSpec((2, tm, D), lambda i: (2, i, 0)),
                   pl.BlockSpec((2, 8, D), w3), pl.BlockSpec((GCH, D), row), pl.BlockSpec((GCH, D), row)],
        out_shape=[SDS(dproj.shape, BF16), SDS((2, 8, D), F32), SDS((GCH, D), F32), SDS((GCH, D), F32)],
        scratch_shapes=[pltpu.VMEM((tm, D), F32), pltpu.VMEM((tm, D), F32)],
        input_output_aliases={0: 0},
        compiler_params=_cparams(),
    )(dproj, da, proj, proj, ln_g, ln_b, wm, wm_t, b_t)


def _proj_bwd(dproj, w_in4, x, dx1, g_mix, job=None):
    T = x.shape[0]
    tm = min(256, T)
    order = (2, 3, 4, 5, 0, 1, 6, 7)

    def body(dp_ref, w_ref, x_ref, dx1_ref, g_ref, gx_ref, acc_ref):
        @pl.when(pl.program_id(0) == 0)
        def _():
            acc_ref[...] = jnp.zeros_like(acc_ref)

        dh = None
        for m, og in enumerate(order):
            part = _mm_nt(dp_ref[m], w_ref[og // 2, :, D * (og % 2):D * (og % 2 + 1)])
            dh = part if dh is None else dh + part
        xv = x_ref[...]
        r = lax.rsqrt(_mean(xv * xv) + EPS)
        xn = xv * r
        acc_ref[...] += _rows8(dh * xn)
        dxn = dh * g_ref[...]
        gx_ref[...] = dx1_ref[...] + r * (dxn - xn * _mean(dxn * xn))

    t = lambda i: (i, 0)
    return _call(
        body, name="proj_bwd", grid=(T // tm,),
        in_specs=[pl.BlockSpec((NIN, tm, D), lambda i: (0, i, 0)),
                  pl.BlockSpec((NCHIP, D, 2 * D), lambda i: (0, 0, 0), pipeline_mode=pl.Buffered(1)),
                  pl.BlockSpec((tm, D), t), pl.BlockSpec((tm, D), t), pl.BlockSpec((1, D), lambda i: (0, 0))],
        out_specs=[pl.BlockSpec((tm, D), t), pl.BlockSpec((8, D), lambda i: (0, 0))],
        out_shape=[SDS((T, D), F32), SDS((8, D), F32)],
        args=(dproj, w_in4, x, dx1, g_mix), job=job)


def _dw_call(name, a, b, a_spec, b_spec, o_spec, out_shape, nblk, tt, job=None, prefetch=None):
    T = a.shape[-2]

    def body(*refs):
        a_ref, b_ref, o_ref = refs[-3:]

        @pl.when(pl.program_id(1) == 0)
        def _():
            o_ref[...] = jnp.zeros_like(o_ref)
        o_ref[...] += _mm_tn(a_ref[...], b_ref[...])

    (out,), job_out = _call(
        body, name=name, grid=(nblk, T // tt), in_specs=[a_spec, b_spec], out_specs=[o_spec],
        out_shape=[out_shape], args=(a, b), job=job, prefetch=prefetch)
    return out, job_out


def _dw_in_half(name, place, hb, dproj, mine, job=None):
    tt = min(DW_TOKENS, hb.shape[0])

    def comp(k, pc):
        return _component_of(2 * k + (pc[1] if mine else 1 - pc[1]))

    return _dw_call(
        name, hb, dproj,
        pl.BlockSpec((tt, D), lambda k, t, pc: (t, 0)),
        pl.BlockSpec((None, tt, D), lambda k, t, pc: (comp(k, pc), t, 0)),
        pl.BlockSpec((None, D, D), lambda k, t, pc: (k, 0, 0)),
        SDS((NCHIP, D, D), F32), NCHIP, tt, job, place)


def _dw_gate_up(h2b, dgu4, job=None):
    tt = min(DW_TOKENS, h2b.shape[0])
    return _dw_call(
        "dw_gate_up", h2b, dgu4,
        pl.BlockSpec((tt, D), lambda k, t: (t, 0)),
        pl.BlockSpec((None, tt, FFS), lambda k, t: (k, t, 0)),
        pl.BlockSpec((None, D, FFS), lambda k, t: (k, 0, 0)),
        SDS((NCHIP, D, FFS), F32), NCHIP, tt, job)


def _dw_down(act, dx2b, job=None):
    tt = min(DW_TOKENS, act.shape[0])
    g, job_out = _dw_call(
        "dw_down", act, dx2b,
        pl.BlockSpec((tt, FFS), lambda k, t: (t, k)),
        pl.BlockSpec((tt, D), lambda k, t: (t, 0)),
        pl.BlockSpec((FFS, D), lambda k, t: (k, 0)),
        SDS((FF, D), F32), 2, tt, job)
    return g.reshape(NCHIP, FF // NCHIP, D), job_out


def _dw_square(name, a, b, job=None):
    tt = min(DW_TOKENS, a.shape[0])
    g, job_out = _dw_call(
        name, a, b,
        pl.BlockSpec((tt, D), lambda k, t: (t, 0)), pl.BlockSpec((tt, D), lambda k, t: (t, 0)),
        pl.BlockSpec((D, D), lambda k, t: (0, 0)), SDS((D, D), F32), 1, tt, job)
    return g.reshape(NCHIP, D // NCHIP, D), job_out


def _place():
    x, y, c = lax.axis_index("x"), lax.axis_index("y"), lax.axis_index("c")
    return x, y, c, 2 * x + y


def _chip_at(x, y, s):
    return x ^ (s >> 1), y ^ (s & 1)


class _Job:
    def __init__(self, ins, out_shapes, sems, start, finish, aliases=None, mid=None):
        self.ins, self.out_shapes, self.sems = list(ins), list(out_shapes), list(sems)
        self.start, self.finish, self.aliases = start, finish, dict(aliases or {})
        self.mid = mid if mid is not None else (lambda ins, outs, sems: None)


def _join_jobs(*jobs):
    def cut(refs, sizes):
        out, at = [], 0
        for n in sizes:
            out.append(refs[at:at + n])
            at += n
        return out

    ni = [len(j.ins) for j in jobs]
    no = [len(j.out_shapes) for j in jobs]
    ns = [len(j.sems) for j in jobs]

    def run(which):
        def go(ins, outs, sems):
            for j, a, b, c in zip(jobs, cut(ins, ni), cut(outs, no), cut(sems, ns)):
                getattr(j, which)(a, b, c)
        return go

    aliases = {}
    for k, j in enumerate(jobs):
        for a, b in j.aliases.items():
            aliases[sum(ni[:k]) + a] = sum(no[:k]) + b
    return _Job([a for j in jobs for a in j.ins], [o for j in jobs for o in j.out_shapes],
                [s for j in jobs for s in j.sems], run("start"), run("finish"), aliases, run("mid"))


def _call(body, *, name, grid, in_specs, out_specs, out_shape, args, scratch_shapes=(), aliases=None,
          job=None, prefetch=None):
    n_in, n_out, n_scr = len(in_specs), len(out_specs), len(scratch_shapes)
    npf = 0 if prefetch is None else 1
    job = job if job is not None else _Job([], [], [], lambda *a: None, lambda *a: None)
    ji, jo = len(job.ins), len(job.out_shapes)
    steps = math.prod(grid)

    def wrapped(*refs):
        pf, refs = refs[:npf], refs[npf:]
        ins, jin = refs[:n_in], refs[n_in:n_in + ji]
        o0 = n_in + ji
        outs, jout = refs[o0:o0 + n_out], refs[o0 + n_out:o0 + n_out + jo]
        s0 = o0 + n_out + jo
        scr, jsem = refs[s0:s0 + n_scr], refs[s0 + n_scr:]
        step = functools.reduce(lambda acc, ag: acc * ag[1] + pl.program_id(ag[0]), enumerate(grid), 0)
        if ji or jo:
            @pl.when(step == 0)
            def _():
                job.start(jin, jout, jsem)

        body(*pf, *ins, *outs, *scr)

        if ji or jo:
            @pl.when(step == steps // 2)
            def _():
                job.mid(jin, jout, jsem)

            @pl.when(step == steps - 1)
            def _():
                job.finish(jin, jout, jsem)

    io = {npf + a: b for a, b in dict(aliases or {}).items()}
    io.update({npf + n_in + a: n_out + b for a, b in job.aliases.items()})
    kw = dict(in_specs=list(in_specs) + [ANY] * ji, out_specs=list(out_specs) + [ANY] * jo,
              scratch_shapes=list(scratch_shapes) + job.sems)
    if npf:
        kw = dict(grid_spec=pltpu.PrefetchScalarGridSpec(num_scalar_prefetch=1, grid=grid, **kw))
    else:
        kw["grid"] = grid
    res = pl.pallas_call(
        wrapped, name=name, out_shape=list(out_shape) + job.out_shapes, input_output_aliases=io,
        compiler_params=_cparams(has_side_effects=bool(ji or jo)), **kw,
    )(*(() if prefetch is None else (prefetch,)), *args, *job.ins)
    return list(res[:n_out]), list(res[n_out:])


def _run_job(job, name):
    ji, jo = len(job.ins), len(job.out_shapes)

    def body(*refs):
        jin, jout, jsem = refs[:ji], refs[ji:ji + jo], refs[ji + jo:]
        job.start(jin, jout, jsem)
        job.finish(jin, jout, jsem)

    return list(pl.pallas_call(
        body, name=name, in_specs=[ANY] * ji, out_specs=[ANY] * jo, out_shape=job.out_shapes,
        scratch_shapes=job.sems, input_output_aliases=job.aliases,
        compiler_params=pltpu.CompilerParams(has_side_effects=True))(*job.ins))


def _cast_shard(name, place, w):
    rows, cols = w.shape
    tr = 352 if rows % 352 == 0 else 256

    def body(pc_ref, w_ref, o_ref):
        del pc_ref
        o_ref[...] = w_ref[...].astype(BF16)

    return pl.pallas_call(
        body, name=name,
        grid_spec=pltpu.PrefetchScalarGridSpec(
            num_scalar_prefetch=1, grid=(rows // tr,),
            in_specs=[pl.BlockSpec((tr, cols), lambda i, pc: (i, 0))],
            out_specs=pl.BlockSpec((None, tr, cols), lambda i, pc: (pc[0], i, 0))),
        out_shape=SDS((NCHIP, rows, cols), BF16),
        compiler_params=_cparams(),
    )(place, w)


def _sibling_copy(ref, send_sem, recv_sem):
    x, y, c, _ = _place()
    return pltpu.make_async_remote_copy(src_ref=ref, dst_ref=ref, send_sem=send_sem, recv_sem=recv_sem,
                                        device_id=(x, y, 1 - c), device_id_type=MESH)


def _half_rows(arr, slot, core):
    half = arr.shape[1] // 2
    return arr.at[slot, pl.ds(pl.multiple_of(core * half, 16), half)]


def _quarter_rows(arr, slot, core, q):
    quarter = arr.shape[1] // 4
    return arr.at[slot, pl.ds(pl.multiple_of((2 * core + q) * quarter, 16), quarter)]


def _chip_copy(ref, dist, send_sem, recv_sem):
    x, y, c, _ = _place()
    cx, cy = _chip_at(x, y, dist)
    return pltpu.make_async_remote_copy(src_ref=ref, dst_ref=ref, send_sem=send_sem, recv_sem=recv_sem,
                                        device_id=(cx, cy, c), device_id_type=MESH)


def _gather_sems(n):
    dma = pltpu.SemaphoreType.DMA
    return [dma((n, 2))] * 4 + [dma((n, 4))] * 2


def _gather_start(arrs, sems):
    dsend, drecv = sems[0], sems[1]
    _, _, c, j = _place()
    for w, arr in enumerate(arrs):
        for dist in (1, 2):
            _chip_copy(_half_rows(arr, j, c), dist, dsend.at[w, dist - 1], drecv.at[w, dist - 1]).start()


def _gather_land(arrs, sems, dist):
    dsend, drecv, rsend, rrecv, fsend, frecv = sems
    _, _, c, j = _place()
    if dist < 3:
        other = 3 - dist
        for w, arr in enumerate(arrs):
            landed = _half_rows(arr, j ^ dist, c)
            _chip_copy(landed, dist, dsend.at[w, dist - 1], drecv.at[w, dist - 1]).wait_recv()
            relay = _quarter_rows(arr, j ^ dist, c, other - 1)
            _chip_copy(relay, other, rsend.at[w, other - 1], rrecv.at[w, other - 1]).start()
            _sibling_copy(landed, fsend.at[w, dist - 1], frecv.at[w, dist - 1]).start()
        for w, arr in enumerate(arrs):
            theirs = _half_rows(arr, j ^ dist, 1 - c)
            _sibling_copy(theirs, fsend.at[w, dist - 1], frecv.at[w, dist - 1]).wait_recv()
    else:
        for w, arr in enumerate(arrs):
            for via in (1, 2):
                piece = _quarter_rows(arr, j ^ 3, c, via - 1)
                _chip_copy(piece, via, rsend.at[w, via - 1], rrecv.at[w, via - 1]).wait_recv()
                _sibling_copy(piece, fsend.at[w, 1 + via], frecv.at[w, 1 + via]).start()
        for w, arr in enumerate(arrs):
            for via in (1, 2):
                theirs = _quarter_rows(arr, j ^ 3, 1 - c, via - 1)
                _sibling_copy(theirs, fsend.at[w, 1 + via], frecv.at[w, 1 + via]).wait_recv()


def _gather_drain(arrs, sems):
    dsend, drecv, rsend, rrecv, fsend, frecv = sems
    _, _, c, j = _place()
    for w, arr in enumerate(arrs):
        for dist in (1, 2):
            other = 3 - dist
            _chip_copy(_half_rows(arr, j, c), dist, dsend.at[w, dist - 1], drecv.at[w, dist - 1]).wait_send()
            _chip_copy(_quarter_rows(arr, j ^ dist, c, other - 1), other,
                       rsend.at[w, other - 1], rrecv.at[w, other - 1]).wait_send()
            _sibling_copy(_half_rows(arr, j ^ dist, c), fsend.at[w, dist - 1], frecv.at[w, dist - 1]).wait_send()
            _sibling_copy(_quarter_rows(arr, j ^ 3, c, dist - 1),
                          fsend.at[w, 1 + dist], frecv.at[w, 1 + dist]).wait_send()


def _gather_neighbours(arrs, sems):
    _gather_land(arrs, sems, 1)
    _gather_land(arrs, sems, 2)


def _gather_finish(arrs, sems):
    _gather_land(arrs, sems, 3)
    _gather_drain(arrs, sems)


def _gather_job(arrs):
    n = len(arrs)
    return _Job(arrs, [SDS(a.shape, a.dtype) for a in arrs], _gather_sems(n),
                lambda ins, outs, sems: _gather_start(outs, sems),
                lambda ins, outs, sems: _gather_finish(outs, sems), {k: k for k in range(n)},
                mid=lambda ins, outs, sems: _gather_neighbours(outs, sems))


def _exchange_job(arrs, out_shapes, n, copies):
    def start(ins, outs, sems):
        for cp in copies(ins, outs, sems[0], sems[1]):
            cp.start()

    def finish(ins, outs, sems):
        for cp in copies(ins, outs, sems[0], sems[1]):
            cp.wait()

    return _Job(arrs, out_shapes, [pltpu.SemaphoreType.DMA((n,))] * 2, start, finish)


def _pair_exchange_job(grads):
    def copies(ins, outs, send_sem, recv_sem):
        x, y, c, _ = _place()
        res = []
        for w in range(len(grads)):
            half = ins[w].shape[1] // 2
            theirs = pl.ds(pl.multiple_of((1 - c) * half, 8), half)
            res.append(pltpu.make_async_remote_copy(
                src_ref=ins[w].at[:, theirs, :], dst_ref=outs[w], send_sem=send_sem.at[w],
                recv_sem=recv_sem.at[w], device_id=(x, y, 1 - c), device_id_type=MESH))
        return res

    return _exchange_job(grads, [SDS((NCHIP, g.shape[1] // 2, g.shape[2]), F32) for g in grads],
                         len(grads), copies)


def _row_tile(rows):
    return 176 if rows % 176 == 0 and rows % 128 else 128


def _pair_sum(name, place, g, sib):
    half, cols = sib.shape[1], sib.shape[2]
    tr = _row_tile(half)
    nt = half // tr
    mine = nt if g.shape[1] == 2 * half else 0

    def body(pc_ref, g_ref, s_ref, own_ref, out_ref):
        del pc_ref
        v = g_ref[...] + s_ref[...]
        out_ref[...] = v.astype(BF16)

        @pl.when(pl.program_id(1) == 0)
        def _():
            own_ref[...] = v

    return pl.pallas_call(
        body, name=name,
        grid_spec=pltpu.PrefetchScalarGridSpec(
            num_scalar_prefetch=1, grid=(nt, NCHIP),
            in_specs=[pl.BlockSpec((None, tr, cols), lambda i, s, pc: (pc[0] ^ s, pc[1] * mine + i, 0)),
                      pl.BlockSpec((None, tr, cols), lambda i, s, pc: (pc[0] ^ s, i, 0))],
            out_specs=[pl.BlockSpec((tr, cols), lambda i, s, pc: (i, 0)),
                       pl.BlockSpec((None, tr, cols), lambda i, s, pc: (s, i, 0))]),
        out_shape=[SDS((half, cols), F32), SDS((NCHIP, half, cols), BF16)],
        compiler_params=_cparams(),
    )(place, g, sib)


def _chip_exchange_job(parts):
    def copies(ins, outs, send_sem, recv_sem):
        x, y, c, _ = _place()
        res = []
        for w in range(len(parts)):
            for s in range(1, NCHIP):
                cx, cy = _chip_at(x, y, s)
                k = w * (NCHIP - 1) + s - 1
                res.append(pltpu.make_async_remote_copy(
                    src_ref=ins[w].at[s], dst_ref=outs[w].at[s - 1], send_sem=send_sem.at[k],
                    recv_sem=recv_sem.at[k], device_id=(cx, cy, c), device_id_type=MESH))
        return res

    return _exchange_job(parts, [SDS((NCHIP - 1,) + p.shape[1:], BF16) for p in parts],
                         len(parts) * (NCHIP - 1), copies)


def _chip_sum(name, own, rem):
    half, cols = own.shape
    tr = _row_tile(half)

    def body(own_ref, rem_ref, out_ref):
        out_ref[...] = ((own_ref[...] + rem_ref[0].astype(F32)) + rem_ref[1].astype(F32)) + rem_ref[2].astype(F32)

    return pl.pallas_call(
        body, name=name, grid=(half // tr,),
        in_specs=[pl.BlockSpec((tr, cols), lambda i: (i, 0)),
                  pl.BlockSpec((NCHIP - 1, tr, cols), lambda i: (0, i, 0))],
        out_specs=pl.BlockSpec((tr, cols), lambda i: (i, 0)),
        out_shape=SDS((half, cols), F32),
        compiler_params=_cparams(),
    )(own, rem)


def _share_halves_job(halves):
    def copies(ins, outs, send_sem, recv_sem):
        x, y, c, _ = _place()
        return [pltpu.make_async_remote_copy(
            src_ref=ins[w], dst_ref=outs[w], send_sem=send_sem.at[w], recv_sem=recv_sem.at[w],
            device_id=(x, y, 1 - c), device_id_type=MESH) for w in range(len(halves))]

    return _exchange_job(halves, [SDS(h.shape, F32) for h in halves], len(halves), copies)


def _adamw_math(w, g, m, v):
    m = B1 * m + (1.0 - B1) * g
    v = B2 * v + (1.0 - B2) * (g * g)
    m_hat = m / (1.0 - B1 ** STEP)
    v_hat = v / (1.0 - B2 ** STEP)
    delta = -LR * (m_hat / (jnp.sqrt(v_hat) + AEPS) + WD * w)
    return delta, m, v


def _adamw(name, place, w, own, sib, m, v):
    rows, cols = w.shape
    by_cols = own.shape[0] == rows
    half, pc_cols = (rows, cols // 2) if by_cols else (rows // 2, cols)
    tr = 352 if half % 352 == 0 else min(256, half)
    nt = half // tr

    def body(pc_ref, w_ref, own_ref, sib_ref, m_ref, v_ref, g_ref, d_ref, mo_ref, vo_ref):
        g = jnp.where(pl.program_id(0) == pc_ref[1], own_ref[...], sib_ref[...])
        d, mn, vn = _adamw_math(w_ref[...], g, m_ref[...], v_ref[...])
        g_ref[...] = g
        d_ref[...] = d
        mo_ref[...] = mn
        vo_ref[...] = vn

    full = pl.BlockSpec((tr, pc_cols), (lambda h, i, pc: (i, h)) if by_cols else (lambda h, i, pc: (h * nt + i, 0)))
    part = pl.BlockSpec((tr, pc_cols), lambda h, i, pc: (i, 0))
    return pl.pallas_call(
        body, name=name,
        grid_spec=pltpu.PrefetchScalarGridSpec(
            num_scalar_prefetch=1, grid=(2, nt),
            in_specs=[full, part, part, full, full], out_specs=[full] * 4),
        out_shape=[SDS((rows, cols), F32)] * 4,
        compiler_params=_cparams(),
    )(place, w, own, sib, m, v)


def _small_allreduce_adamw(sp, w, m, v):
    shape = sp.shape

    def body(sp_ref, w_ref, m_ref, v_ref, g_ref, d_ref, mo_ref, vo_ref,
             sib_s, pair_s, chip_s, send_sem, recv_sem):
        x, y, c, j = _place()
        cp = pltpu.make_async_remote_copy(
            src_ref=sp_ref, dst_ref=sib_s, send_sem=send_sem.at[0], recv_sem=recv_sem.at[0],
            device_id=(x, y, 1 - c), device_id_type=MESH)
        cp.start()
        cp.wait()
        pair_s[...] = sp_ref[...] + sib_s[...]
        cps = []
        for s in range(1, NCHIP):
            cx, cy = _chip_at(x, y, s)
            cp = pltpu.make_async_remote_copy(
                src_ref=pair_s, dst_ref=chip_s.at[s], send_sem=send_sem.at[s], recv_sem=recv_sem.at[s],
                device_id=(cx, cy, c), device_id_type=MESH)
            cp.start()
            cps.append(cp)
        chip_s[0] = pair_s[...]
        for cp in cps:
            cp.wait()
        tot = chip_s[j]
        for k in range(1, NCHIP):
            tot = tot + chip_s[k ^ j]
        g_ref[...] = tot
        d, mn, vn = _adamw_math(w_ref[...], tot, m_ref[...], v_ref[...])
        d_ref[...] = d
        mo_ref[...] = mn
        vo_ref[...] = vn

    vm = pl.BlockSpec(memory_space=pltpu.VMEM)
    return pl.pallas_call(
        body, name="small_allreduce_adamw",
        in_specs=[vm] * 4, out_specs=[vm] * 4, out_shape=[SDS(shape, F32)] * 4,
        scratch_shapes=[pltpu.VMEM(shape, F32), pltpu.VMEM(shape, F32), pltpu.VMEM((NCHIP,) + shape, F32),
                        pltpu.SemaphoreType.DMA((NCHIP,)), pltpu.SemaphoreType.DMA((NCHIP,))],
        compiler_params=pltpu.CompilerParams(has_side_effects=True),
    )(sp, w, m, v)


def _pack_small(first, mix, ln_g, ln_b, b_s, lbt, hn, ffn, fin, w_s):
    rows = [first.reshape(1, D), mix.reshape(1, D), ln_g.reshape(1, D), ln_b.reshape(1, D),
            b_s.reshape(1, D), lbt.reshape(2, D), hn.reshape(1, D), ffn.reshape(1, D), fin.reshape(1, D),
            jnp.zeros((6, D), F32)]
    return jnp.concatenate(rows + [w_s.reshape(NG, GCH, GCH).transpose(1, 0, 2).reshape(GCH, D)], axis=0)


def _unpack_small(p):
    w_s = p[16:].reshape(GCH, NG, GCH).transpose(1, 0, 2).reshape(1, NG, GCH, GCH)
    return dict(norm_mix_g=p[1:2], gmlp_ln_g=p[2:3], gmlp_ln_b=p[3:4], gmlp_b_s=p[4].reshape(1, NG, GCH),
                hgrn_lb_table=p[5:7], hgrn_norm_g=p[7:8], norm_ffn_g=p[8:9], norm_final_g=p[9],
                gmlp_w_s=w_s)


SMALL = ("norm_mix_g", "gmlp_ln_g", "gmlp_ln_b", "gmlp_w_s", "gmlp_b_s", "hgrn_lb_table", "hgrn_norm_g",
         "norm_ffn_g", "norm_final_g")
BIG = ("w_in", "w_gate_up", "w_branch_a", "w_branch_b", "w_out", "w_down")
ORDER = ("norm_mix_g", "w_in", "gmlp_ln_g", "gmlp_ln_b", "gmlp_w_s", "gmlp_b_s", "hgrn_lb_table",
         "hgrn_norm_g", "w_branch_a", "w_branch_b", "w_out", "norm_ffn_g", "w_gate_up", "w_down",
         "norm_final_g")


def kernel(x, norm_mix_g, w_in, gmlp_ln_g, gmlp_ln_b, gmlp_w_s, gmlp_b_s, hgrn_lb_table, hgrn_norm_g, w_branch_a, w_branch_b, w_out, norm_ffn_g, w_gate_up, w_down, norm_final_g, loss_target, m_norm_mix_g, m_w_in, m_gmlp_ln_g, m_gmlp_ln_b, m_gmlp_w_s, m_gmlp_b_s, m_hgrn_lb_table, m_hgrn_norm_g, m_w_branch_a, m_w_branch_b, m_w_out, m_norm_ffn_g, m_w_gate_up, m_w_down, m_norm_final_g, v_norm_mix_g, v_w_in, v_gmlp_ln_g, v_gmlp_ln_b, v_gmlp_w_s, v_gmlp_b_s, v_hgrn_lb_table, v_hgrn_norm_g, v_w_branch_a, v_w_branch_b, v_w_out, v_norm_ffn_g, v_w_gate_up, v_w_down, v_norm_final_g):
    args = dict(locals())
    T = x.shape[1]
    xs = x.reshape(T, D)
    target = loss_target.reshape(T, D)
    big = {n: args[n].reshape(args[n].shape[1:]) for n in BIG}
    big_m = {n: args["m_" + n].reshape(args[n].shape[1:]) for n in BIG}
    big_v = {n: args["v_" + n].reshape(args[n].shape[1:]) for n in BIG}

    x_i, y_i, c_i = lax.axis_index("x"), lax.axis_index("y"), lax.axis_index("c")
    place = jnp.stack([2 * x_i + y_i, c_i]).astype(jnp.int32)
    cast = {n: _cast_shard("cast_" + n, place, big[n]) for n in BIG}
    tril = jnp.tril(jnp.ones((GCH, GCH), bool))
    wm = jnp.where(tril, gmlp_w_s[0], 0.0).astype(BF16)
    wm_t = jnp.swapaxes(wm, 1, 2)
    b_t = gmlp_b_s[0].T

    (proj, hb), w_in4, (w_a4,) = _proj_fwd(place, xs, norm_mix_g, cast["w_in"], [cast["w_branch_a"]])
    (ab, y_a), (w_b4, w_out4) = _gmlp_fwd(
        proj, gmlp_ln_g, gmlp_ln_b, wm, b_t, w_a4.reshape(D, D),
        job=_gather_job([cast["w_branch_b"], cast["w_out"]]))
    (o_raw, obb, st_before), (w_gu4, w_down4) = _hgrn_fwd(
        proj, hgrn_lb_table, hgrn_norm_g, job=_gather_job([cast["w_gate_up"], cast["w_down"]]))
    w_a, w_b, w_o = (w.reshape(D, D) for w in (w_a4, w_b4, w_out4))
    w_dn = w_down4.reshape(FF, D)
    y_b, mgb, x1 = _merge_fwd(xs, y_a, obb, proj, w_b, w_o)
    act, dx2b, h2b, dgu4, dx1, dx1b, acc_ffn = _ffn_fwd_bwd(
        x1, target, norm_ffn_g, norm_final_g.reshape(1, D), w_gu4, w_dn)

    grads, owns, parts, halves, sibh = {}, {}, {}, {}, {}

    def pair_sums(names, sibs):
        for n, s in zip(names, sibs):
            owns[n], parts[n] = _pair_sum("rs_pair_sum_" + n, place, grads[n], s)

    def chip_sums(names, got):
        for n, r in zip(names, got):
            halves[n] = _chip_sum("rs_chip_sum_" + n, owns[n], r)

    ffn, mix = ("w_gate_up", "w_down"), ("w_branch_a", "w_branch_b", "w_out")
    grads["w_gate_up"], _ = _dw_gate_up(h2b, dgu4)
    grads["w_down"], _ = _dw_down(act, dx2b)
    (dya, dyb, da, dob, dproj), got = _merge_bwd(
        dx1b, y_a, y_b, proj, w_o, w_a, w_b, job=_pair_exchange_job([grads[n] for n in ffn]))
    pair_sums(ffn, got)
    grads["w_branch_a"], _ = _dw_square("dw_branch_a", ab, dya)
    grads["w_branch_b"], _ = _dw_square("dw_branch_b", obb, dyb)
    grads["w_out"], _ = _dw_square("dw_out", mgb, dx1b)
    (dproj, acc_hgrn), got = _hgrn_bwd(
        dproj, dob, o_raw, proj, st_before, hgrn_lb_table, hgrn_norm_g,
        job=_join_jobs(_chip_exchange_job([parts[n] for n in ffn]), _pair_exchange_job([grads[n] for n in mix])))
    chip_sums(ffn, got[:2])
    pair_sums(mix, got[2:])
    dproj, acc_ln, dws, dmix = _gmlp_bwd(dproj, da, proj, gmlp_ln_g, gmlp_ln_b, wm, wm_t, b_t)
    for_sibling, got = _dw_in_half(
        "dw_in_sibling_half", place, hb, dproj, False,
        job=_join_jobs(_share_halves_job([halves[n] for n in ffn]), _chip_exchange_job([parts[n] for n in mix])))
    sibh.update(zip(ffn, got[:2]))
    chip_sums(mix, got[2:])
    grads["w_in"], got = _dw_in_half(
        "dw_in_own_half", place, hb, dproj, True, job=_share_halves_job([for_sibling]))
    pair_sums(("w_in",), got)
    (grad_x, acc_mix), got = _proj_bwd(
        dproj, w_in4, xs, dx1, norm_mix_g,
        job=_join_jobs(_chip_exchange_job([parts["w_in"]]), _share_halves_job([halves[n] for n in mix])))
    chip_sums(("w_in",), got[:1])
    sibh.update(zip(mix, got[1:]))
    (sibh["w_in"],) = _run_job(_share_halves_job([halves["w_in"]]), "rs_share_halves_w_in")
    out = {}
    for n in BIG:
        g, d, mn, vn = _adamw("adamw_" + n, place, big[n], halves[n], sibh[n], big_m[n], big_v[n])
        shp = args[n].shape
        out[n] = (g.reshape(shp), d.reshape(shp), mn.reshape(shp), vn.reshape(shp))

    lbv = jax.nn.sigmoid(hgrn_lb_table[0] - hgrn_lb_table[1])
    d_t0 = jnp.sum(acc_hgrn[0], axis=0) * lbv * (1.0 - lbv)
    loss_row = jnp.zeros((D,), F32).at[0].set(jnp.sum(acc_ffn[0]))
    dws_m = jnp.where(tril[:, None, :], dws.reshape(GCH, NG, GCH), 0.0).transpose(1, 0, 2)
    db_s = jnp.sum(dmix.reshape(GCH, NG, GCH), axis=-1).T
    sp = _pack_small(loss_row, jnp.sum(acc_mix, 0), jnp.sum(acc_ln[0], 0), jnp.sum(acc_ln[1], 0), db_s,
                     jnp.stack([d_t0, -d_t0]), jnp.sum(acc_hgrn[1], 0), jnp.sum(acc_ffn[2], 0),
                     jnp.sum(acc_ffn[1], 0), dws_m)
    zero = jnp.zeros((D,), F32)

    def pack(prefix):
        a = lambda n: args[prefix + n]
        return _pack_small(zero, a("norm_mix_g"), a("gmlp_ln_g"), a("gmlp_ln_b"), a("gmlp_b_s"),
                           a("hgrn_lb_table"), a("hgrn_norm_g"), a("norm_ffn_g"), a("norm_final_g"),
                           a("gmlp_w_s"))

    packed = _small_allreduce_adamw(sp, pack(""), pack("m_"), pack("v_"))
    loss = packed[0][0, 0]
    small = [_unpack_small(p) for p in packed]
    for n in SMALL:
        out[n] = tuple(s[n] for s in small)
    return (loss, grad_x.reshape(x.shape), *[out[n][0] for n in ORDER], *[out[n][1] for n in ORDER],
            *[out[n][2] for n in ORDER], *[out[n][3] for n in ORDER])
```

```python
import functools
import math

import jax
import jax.numpy as jnp
from jax import lax
from jax.experimental import pallas as pl
from jax.experimental.pallas import tpu as pltpu

F32 = jnp.float32
BF16 = jnp.bfloat16
SDS = jax.ShapeDtypeStruct
MESH = pl.DeviceIdType.MESH
ANY = pl.BlockSpec(memory_space=pl.ANY)

D = 1024
NIN = 8
NG = 8
GCH = 128
NH = 8
HD = 128
HCH = 64
HGRN_HB = 4
HW = HGRN_HB * HD
DW_TOKENS = 2048
ELEMENTWISE_BLOCK_BYTES = 2 * 1024 * 1024
FF = 2816
FFS = 1408
NCHIP = 4
EPS = 1e-6
QSCALE = HD ** -0.5
GELU_C0 = math.sqrt(2.0 / math.pi)
GELU_C1 = 0.044715
LR, B1, B2, AEPS, WD, STEP = 0.001, 0.9, 0.999, 1e-08, 0.01, 10
VMEM_LIMIT_V7X = 56 * 1024 * 1024
SP_ROWS = 144


def _cparams(**kw):
    return pltpu.CompilerParams(vmem_limit_bytes=VMEM_LIMIT_V7X, **kw)


def _mm(a, b):
    return jnp.dot(a, b, preferred_element_type=F32)


def _mm_nt(a, b):
    return lax.dot_general(a, b, (((1,), (1,)), ((), ())), preferred_element_type=F32)


def _mm_tn(a, b):
    return lax.dot_general(a, b, (((0,), (0,)), ((), ())), preferred_element_type=F32)


def _rows8(x):
    r, c = x.shape
    return jnp.sum(x.reshape(r // 8, 8, c), axis=0)


def _mean(x):
    return jnp.mean(x, axis=-1, keepdims=True)


def _sigmoid(x):
    return 1.0 / (1.0 + jnp.exp(-x))


def _gelu(x):
    t = jnp.tanh(GELU_C0 * (x + GELU_C1 * x * x * x))
    return 0.5 * x * (1.0 + t), t


def _gelu_grad(x, t):
    return 0.5 * (1.0 + t) + 0.5 * x * (1.0 - t * t) * (GELU_C0 * (1.0 + 3.0 * GELU_C1 * x * x))


def _component_of(group):
    return jnp.where(group < 6, (group + 4) % 6, group)


def _proj_fwd(place, x, g_mix, w_in4, later):
    T = x.shape[0]
    tm = min(512, T)
    ni = T // tm
    n = len(later)

    def body(pc_ref, x_ref, g_ref, *rest):
        proj_ref, h_ref, w_all = rest[1 + n:4 + n]
        gathered = rest[4 + n:4 + 2 * n]
        hs, wbuf, wsem = rest[4 + 2 * n:7 + 2 * n]
        w_sems, later_sems = rest[7 + 2 * n:13 + 2 * n], rest[13 + 2 * n:]
        jp, i = pl.program_id(0), pl.program_id(1)
        w_cols = [w_all.at[:, :, pl.ds(k * D, D)] for k in range(2)]

        def w_copy(blk):
            cols = pl.ds(pl.multiple_of((blk % 2) * D, 128), D)
            return pltpu.make_async_copy(w_all.at[pc_ref[0] ^ (blk // 2), :, cols], wbuf.at[blk % 2],
                                         wsem.at[blk % 2])

        @pl.when((jp == 0) & (i == 0))
        def _():
            _gather_start(w_cols, w_sems)
            _gather_start(gathered, later_sems)
            w_copy(jp).start()

        @pl.when(i == 0)
        def _():
            w_copy(jp).wait()

        @pl.when(jp == 0)
        def _():
            xv = x_ref[...]
            r = lax.rsqrt(_mean(xv * xv) + EPS)
            hb = (xv * r * g_ref[...]).astype(BF16)
            hs[i] = hb
            h_ref[...] = hb

        proj_ref[...] = _mm(hs[i], wbuf[jp % 2])

        for nxt in range(1, NIN):
            @pl.when((jp == nxt - 1) & (i == ni - 1))
            def _():
                if nxt >= 2:
                    _gather_land([w_cols[nxt % 2]], w_sems, nxt // 2, first=nxt % 2)
                if nxt == 4:
                    _gather_neighbours(gathered, later_sems)
                w_copy(jp + 1).start()

        @pl.when((jp == NIN - 1) & (i == ni - 1))
        def _():
            _gather_drain(w_cols, w_sems)
            _gather_finish(gathered, later_sems)

    tile = lambda jp, i, pc: (jnp.where(jp == 0, i, ni - 1), 0)
    res = pl.pallas_call(
        body, name="proj_fwd",
        grid_spec=pltpu.PrefetchScalarGridSpec(
            num_scalar_prefetch=1, grid=(NIN, ni),
            in_specs=[pl.BlockSpec((tm, D), tile), pl.BlockSpec((1, D), lambda jp, i, pc: (0, 0))] + [ANY] * (1 + n),
            out_specs=[pl.BlockSpec((None, tm, D), lambda jp, i, pc: (2 * (pc[0] ^ (jp // 2)) + jp % 2, i, 0)),
                       pl.BlockSpec((tm, D), tile)] + [ANY] * (1 + n),
            scratch_shapes=[pltpu.VMEM((ni, tm, D), BF16), pltpu.VMEM((2, D, D), BF16),
                            pltpu.SemaphoreType.DMA((2,))] + _gather_sems(2) + _gather_sems(n)),
        out_shape=[SDS((NIN, T, D), F32), SDS((T, D), BF16), SDS(w_in4.shape, BF16)]
        + [SDS(a.shape, a.dtype) for a in later],
        input_output_aliases={3 + k: 2 + k for k in range(1 + n)},
        compiler_params=_cparams(has_side_effects=True),
    )(place, x, g_mix, w_in4, *later)
    return res[:2], res[2], res[3:]


def _layer_norm_stats(gv):
    mu = _mean(gv)
    xc = gv - mu
    rs = lax.rsqrt(_mean(xc * xc) + EPS)
    return xc * rs, rs


def _gmlp_fwd(proj, ln_g, ln_b, wm, b_t, w_a, job=None):
    T = proj.shape[1]
    tm = min(256, T)

    def body(u_ref, v_ref, lg_ref, lb_ref, wm_ref, bt_ref, wa_ref, a_ref, ya_ref, a_s):
        gu, _ = _gelu(u_ref[...])
        gv, _ = _gelu(v_ref[...])
        vhat, _ = _layer_norm_stats(gv)
        vnb = (vhat * lg_ref[...] + lb_ref[...]).astype(BF16)
        for ch in range(tm // GCH):
            rows = slice(GCH * ch, GCH * (ch + 1))
            for g in range(NG):
                cols = slice(128 * g, 128 * (g + 1))
                mixed = _mm(wm_ref[g], vnb[rows, cols]) + bt_ref[:, g:g + 1]
                a_s[rows, cols] = gu[rows, cols] * mixed
        ab = a_s[...].astype(BF16)
        a_ref[...] = ab
        ya_ref[...] = _mm(ab, wa_ref[...])

    row = lambda i: (0, 0)
    return _call(
        body, name="gmlp_fwd", grid=(T // tm,), job=job, args=(proj, proj, ln_g, ln_b, wm, b_t, w_a),
        in_specs=[pl.BlockSpec((None, tm, D), lambda i: (0, i, 0)), pl.BlockSpec((None, tm, D), lambda i: (1, i, 0)),
                  pl.BlockSpec((1, D), row), pl.BlockSpec((1, D), row),
                  pl.BlockSpec((NG, GCH, GCH), lambda i: (0, 0, 0)), pl.BlockSpec((GCH, NG), row),
                  pl.BlockSpec((D, D), row)],
        out_specs=[pl.BlockSpec((tm, D), lambda i: (i, 0)), pl.BlockSpec((tm, D), lambda i: (i, 0))],
        out_shape=[SDS((T, D), BF16), SDS((T, D), F32)],
        scratch_shapes=[pltpu.VMEM((tm, D), F32)])


def _cumsum64(x, row):
    for s in (1, 2, 4, 8, 16, 32):
        x = x + jnp.where(row >= s, pltpu.roll(x, s, 0), 0.0)
    return x


def _revcumsum64(x, row):
    n = x.shape[0]
    for s in (1, 2, 4, 8, 16, 32):
        x = x + jnp.where(row < HCH - s, pltpu.roll(x, n - s, 0), 0.0)
    return x


def _head_mean(x):
    parts = [jnp.broadcast_to(_mean(x[:, HD * h:HD * (h + 1)]), (x.shape[0], HD)) for h in range(x.shape[1] // HD)]
    return jnp.concatenate(parts, axis=1)


def _seg_sum(x):
    n, c = x.shape
    s = jnp.sum(x.reshape(n // HCH, HCH, c), axis=1, keepdims=True)
    return jnp.broadcast_to(s, (n // HCH, HCH, c)).reshape(n, c)


def _hgrn_gates(fl, lbv, row):
    s = _sigmoid(fl)
    f = lbv + (1.0 - lbv) * s
    a = _cumsum64(jnp.log(f), row)
    a_mid = _seg_sum(jnp.where(row == HCH // 2 - 1, a, 0.0))
    a_last = _seg_sum(jnp.where(row == HCH - 1, a, 0.0))
    return s, f, a, a_mid, a_last


def _hgrn_fwd(proj, lb_table, norm_g, job=None):
    T = proj.shape[1]
    tb = min(512, T)
    nc = tb // HCH

    def body(q_ref, fl_ref, v_ref, g_ref, lbt_ref, gn_ref, o_ref, ob_ref, stb_ref, st_s, o_s):
        @pl.when(pl.program_id(1) == 0)
        def _():
            st_s[...] = jnp.zeros_like(st_s)

        row = lax.broadcasted_iota(jnp.int32, (tb, HW), 0) & (HCH - 1)
        lbv = _sigmoid(lbt_ref[0:1, :] - lbt_ref[1:2, :])
        _, f, a, a_mid, a_last = _hgrn_gates(fl_ref[...], lbv, row)
        k = 1.0 - f
        qs = q_ref[...] * QSCALE
        q_in = (qs * jnp.exp(a - a_mid)).astype(BF16)
        k_in = (k * jnp.exp(a_mid - a)).astype(BF16)
        q_a = (qs * jnp.exp(a)).astype(BF16)
        k_d = (k * jnp.exp(a_last - a)).astype(BF16)
        dec = jnp.exp(a_last)
        vb = v_ref[...].astype(BF16)
        tri = (lax.broadcasted_iota(jnp.int32, (HCH, HCH), 0)
               >= lax.broadcasted_iota(jnp.int32, (HCH, HCH), 1))
        for c in range(nc):
            sl = slice(HCH * c, HCH * (c + 1))
            for hh in range(HGRN_HB):
                hs = slice(HD * hh, HD * (hh + 1))
                st = st_s[hh]
                stb_ref[hh, c] = st
                sc = jnp.where(tri, _mm_nt(q_in[sl, hs], k_in[sl, hs]), 0.0)
                o_s[sl, hs] = _mm(sc.astype(BF16), vb[sl, hs]) + _mm_nt(q_a[sl, hs], st.astype(BF16))
                d64 = dec[sl, hs]
                st_s[hh] = st * jnp.concatenate([d64, d64], axis=0) + _mm_tn(vb[sl, hs], k_d[sl, hs])
        o = o_s[...]
        r = lax.rsqrt(_head_mean(o * o) + EPS)
        g = g_ref[...]
        o_ref[...] = o
        ob_ref[...] = (o * r * gn_ref[...] * (g * _sigmoid(g))).astype(BF16)

    def col(off):
        return pl.BlockSpec((None, tb, HW), lambda h, cb: (off, cb, h))

    return _call(
        body, name="hgrn_fwd", grid=(NH // HGRN_HB, T // tb), job=job,
        args=(proj, proj, proj, proj, lb_table, norm_g),
        in_specs=[col(2), col(3), col(4), col(5),
                  pl.BlockSpec((2, HW), lambda h, cb: (0, h)), pl.BlockSpec((1, HW), lambda h, cb: (0, h))],
        out_specs=[pl.BlockSpec((tb, HW), lambda h, cb: (cb, h)), pl.BlockSpec((tb, HW), lambda h, cb: (cb, h)),
                   pl.BlockSpec((HGRN_HB, nc, HD, HD), lambda h, cb: (h, cb, 0, 0))],
        out_shape=[SDS((T, D), F32), SDS((T, D), BF16), SDS((NH, T // HCH, HD, HD), F32)],
        scratch_shapes=[pltpu.VMEM((HGRN_HB, HD, HD), F32), pltpu.VMEM((tb, HW), F32)])


def _merge_fwd(x, y_a, ob, proj, w_b, w_out):
    T = x.shape[0]
    tm = min(512, T)

    def body(x_ref, ya_ref, ob_ref, ga_ref, gb_ref, wb_ref, wo_ref, yb_ref, mg_ref, x1_ref):
        yb = _mm(ob_ref[...], wb_ref[...])
        merged = (_sigmoid(ga_ref[...]) * ya_ref[...] + _sigmoid(gb_ref[...]) * yb).astype(BF16)
        yb_ref[...] = yb
        mg_ref[...] = merged
        x1_ref[...] = x_ref[...] + _mm(merged, wo_ref[...])

    t = lambda i: (i, 0)
    w = lambda i: (0, 0)
    return pl.pallas_call(
        body, name="merge_fwd", grid=(T // tm,),
        in_specs=[pl.BlockSpec((tm, D), t), pl.BlockSpec((tm, D), t), pl.BlockSpec((tm, D), t),
                  pl.BlockSpec((None, tm, D), lambda i: (6, i, 0)), pl.BlockSpec((None, tm, D), lambda i: (7, i, 0)),
                  pl.BlockSpec((D, D), w), pl.BlockSpec((D, D), w)],
        out_specs=[pl.BlockSpec((tm, D), t)] * 3,
        out_shape=[SDS((T, D), F32), SDS((T, D), BF16), SDS((T, D), F32)],
        compiler_params=_cparams(),
    )(x, y_a, ob, proj, proj, w_b, w_out)


def _ffn_fwd_bwd(x1, target, g_ffn, g_fin, w_gu4, w_down):
    T = x1.shape[0]
    tm = min(256, T)
    inv_d = 1.0 / D

    def body(x1_ref, tg_ref, gf_ref, gn_ref, wgu_ref, wd_ref,
             act_ref, dx2b_ref, h2b_ref, dgu_ref, dx1_ref, dx1b_ref, acc_ref):
        @pl.when(pl.program_id(0) == 0)
        def _():
            acc_ref[...] = jnp.zeros_like(acc_ref)

        x1v = x1_ref[...]
        gf = gf_ref[...]
        gn = gn_ref[...]
        rr1 = lax.rsqrt(_mean(x1v * x1v) + EPS)
        x1n = x1v * rr1
        h2b = (x1n * gf).astype(BF16)
        h2b_ref[...] = h2b
        p = [_mm(h2b, wgu_ref[k]) for k in range(NCHIP)]
        sg = [_sigmoid(p[0]), _sigmoid(p[1])]
        si = [p[0] * sg[0], p[1] * sg[1]]
        x2 = x1v
        for k in range(2):
            actk = (si[k] * p[2 + k]).astype(BF16)
            act_ref[:, FFS * k:FFS * (k + 1)] = actk
            x2 = x2 + _mm(actk, wd_ref[FFS * k:FFS * (k + 1), :])
        rr2 = lax.rsqrt(_mean(x2 * x2) + EPS)
        x2n = x2 * rr2
        e = x2n * gn - tg_ref[...]
        acc_ref[0] += _rows8(e * e) * (0.5 * inv_d)
        dy = e * inv_d
        acc_ref[1] += _rows8(dy * x2n)
        dxn = dy * gn
        dx2 = rr2 * (dxn - x2n * _mean(dxn * x2n))
        dx2b = dx2.astype(BF16)
        dx2b_ref[...] = dx2b
        dh2 = None
        for k in range(2):
            dact = _mm_nt(dx2b, wd_ref[FFS * k:FFS * (k + 1), :])
            dgate = (dact * p[2 + k] * (sg[k] * (1.0 + p[k] * (1.0 - sg[k])))).astype(BF16)
            dup = (dact * si[k]).astype(BF16)
            dgu_ref[k] = dgate
            dgu_ref[2 + k] = dup
            part = _mm_nt(dgate, wgu_ref[k]) + _mm_nt(dup, wgu_ref[2 + k])
            dh2 = part if dh2 is None else dh2 + part
        acc_ref[2] += _rows8(dh2 * x1n)
        dxn1 = dh2 * gf
        dx1 = dx2 + rr1 * (dxn1 - x1n * _mean(dxn1 * x1n))
        dx1_ref[...] = dx1
        dx1b_ref[...] = dx1.astype(BF16)

    t = lambda i: (i, 0)
    w = lambda i: (0, 0)
    one = pl.Buffered(1)
    return pl.pallas_call(
        body, name="ffn_fwd_bwd", grid=(T // tm,),
        in_specs=[pl.BlockSpec((tm, D), t), pl.BlockSpec((tm, D), t),
                  pl.BlockSpec((1, D), w), pl.BlockSpec((1, D), w),
                  pl.BlockSpec((NCHIP, D, FFS), lambda i: (0, 0, 0), pipeline_mode=one),
                  pl.BlockSpec((FF, D), w, pipeline_mode=one)],
        out_specs=[pl.BlockSpec((tm, FF), t), pl.BlockSpec((tm, D), t), pl.BlockSpec((tm, D), t),
                   pl.BlockSpec((NCHIP, tm, FFS), lambda i: (0, i, 0)),
                   pl.BlockSpec((tm, D), t), pl.BlockSpec((tm, D), t),
                   pl.BlockSpec((3, 8, D), lambda i: (0, 0, 0))],
        out_shape=[SDS((T, FF), BF16), SDS((T, D), BF16), SDS((T, D), BF16),
                   SDS((NCHIP, T, FFS), BF16), SDS((T, D), F32), SDS((T, D), BF16),
                   SDS((3, 8, D), F32)],
        compiler_params=_cparams(),
    )(x1, target, g_ffn, g_fin, w_gu4, w_down)


def _merge_bwd(dx1b, y_a, y_b, proj, w_out, w_a, w_b, job=None):
    T = dx1b.shape[0]
    tm = min(512, T)

    def body(dx_ref, ya_ref, yb_ref, ga_ref, gb_ref, wo_ref, wa_ref, wb_ref,
             dya_ref, dyb_ref, da_ref, dob_ref, dp_ref):
        dm = _mm_nt(dx_ref[...], wo_ref[...])
        sa = _sigmoid(ga_ref[...])
        sb = _sigmoid(gb_ref[...])
        dya = (dm * sa).astype(BF16)
        dyb = (dm * sb).astype(BF16)
        dya_ref[...] = dya
        dyb_ref[...] = dyb
        dp_ref[0] = (dm * ya_ref[...] * sa * (1.0 - sa)).astype(BF16)
        dp_ref[1] = (dm * yb_ref[...] * sb * (1.0 - sb)).astype(BF16)
        da_ref[...] = _mm_nt(dya, wa_ref[...])
        dob_ref[...] = _mm_nt(dyb, wb_ref[...])

    t = lambda i: (i, 0)
    w = lambda i: (0, 0)
    return _call(
        body, name="merge_bwd", grid=(T // tm,),
        in_specs=[pl.BlockSpec((tm, D), t), pl.BlockSpec((tm, D), t), pl.BlockSpec((tm, D), t),
                  pl.BlockSpec((None, tm, D), lambda i: (6, i, 0)), pl.BlockSpec((None, tm, D), lambda i: (7, i, 0)),
                  pl.BlockSpec((D, D), w), pl.BlockSpec((D, D), w), pl.BlockSpec((D, D), w)],
        out_specs=[pl.BlockSpec((tm, D), t)] * 4 + [pl.BlockSpec((2, tm, D), lambda i: (3, i, 0))],
        out_shape=[SDS((T, D), BF16), SDS((T, D), BF16), SDS((T, D), F32), SDS((T, D), F32),
                   SDS((NIN, T, D), BF16)],
        args=(dx1b, y_a, y_b, proj, proj, w_out, w_a, w_b), job=job)


def _hgrn_bwd(dproj, dob, o_raw, proj, st_before, lb_table, norm_g, job=None):
    T = dob.shape[0]
    tb = min(512, T)
    nc = tb // HCH
    nb = T // tb

    def body(dp_in, dob_ref, o_ref, q_ref, fl_ref, v_ref, g_ref, stb_ref, lbt_ref, gn_ref,
             dp_ref, acc_ref, dst_s, dqin_s, dqa_s, dkin_s, dkd_s, dv_s, ddec_s):
        del dp_in

        @pl.when(pl.program_id(1) == 0)
        def _():
            dst_s[...] = jnp.zeros_like(dst_s)
            acc_ref[...] = jnp.zeros_like(acc_ref)

        row = lax.broadcasted_iota(jnp.int32, (tb, HW), 0) & (HCH - 1)
        gn = gn_ref[...]
        lbv = _sigmoid(lbt_ref[0:1, :] - lbt_ref[1:2, :])
        o = o_ref[...]
        r = lax.rsqrt(_head_mean(o * o) + EPS)
        on = o * r
        g = g_ref[...]
        sgm = _sigmoid(g)
        dob_v = dob_ref[...]
        dp_ref[3] = (dob_v * on * gn * (sgm * (1.0 + g * (1.0 - sgm)))).astype(BF16)
        do_n = dob_v * (g * sgm)
        acc_ref[1] += _rows8(do_n * on)
        dxn = do_n * gn
        do = (r * (dxn - on * _head_mean(dxn * on))).astype(BF16)
        s, f, a, a_mid, a_last = _hgrn_gates(fl_ref[...], lbv, row)
        k = 1.0 - f
        qs = q_ref[...] * QSCALE
        e_q = jnp.exp(a - a_mid)
        e_k = jnp.exp(a_mid - a)
        e_a = jnp.exp(a)
        e_l = jnp.exp(a_last - a)
        dec = jnp.exp(a_last)
        q_in = qs * e_q
        k_in = k * e_k
        q_a = qs * e_a
        k_d = k * e_l
        q_inb, k_inb, q_ab, k_db = (z.astype(BF16) for z in (q_in, k_in, q_a, k_d))
        vb = v_ref[...].astype(BF16)
        tri = (lax.broadcasted_iota(jnp.int32, (HCH, HCH), 0)
               >= lax.broadcasted_iota(jnp.int32, (HCH, HCH), 1))
        for c in reversed(range(nc)):
            sl = slice(HCH * c, HCH * (c + 1))
            for hh in range(HGRN_HB):
                hs = slice(HD * hh, HD * (hh + 1))
                stp = stb_ref[hh, c]
                dst = dst_s[hh]
                dstb = dst.astype(BF16)
                do_c = do[sl, hs]
                v_c = vb[sl, hs]
                dqa_s[sl, hs] = _mm(do_c, stp.astype(BF16))
                dkd_s[sl, hs] = _mm(v_c, dstb)
                ddec_s[sl, hs] = jnp.broadcast_to(jnp.sum(dst * stp, axis=0, keepdims=True), (HCH, HD))
                sc = jnp.where(tri, _mm_nt(q_inb[sl, hs], k_inb[sl, hs]), 0.0).astype(BF16)
                dsc = jnp.where(tri, _mm_nt(do_c, v_c), 0.0).astype(BF16)
                dv_s[sl, hs] = _mm_nt(k_db[sl, hs], dstb) + _mm_tn(sc, do_c)
                dqin_s[sl, hs] = _mm(dsc, k_inb[sl, hs])
                dkin_s[sl, hs] = _mm_tn(dsc, q_inb[sl, hs])
                d64 = dec[sl, hs]
                dst_s[hh] = dst * jnp.concatenate([d64, d64], axis=0) + _mm_tn(do_c, q_ab[sl, hs])
        dq_in = dqin_s[...]
        dq_a = dqa_s[...]
        dk_in = dkin_s[...]
        dk_d = dkd_s[...]
        dp_ref[0] = ((dq_in * e_q + dq_a * e_a) * QSCALE).astype(BF16)
        dp_ref[2] = dv_s[...].astype(BF16)
        tq = dq_in * q_in
        tk = dk_in * k_in
        td = dk_d * k_d
        d_a = tq + dq_a * q_a - tk - td
        d_a = d_a + jnp.where(row == HCH // 2 - 1, _seg_sum(tk - tq), 0.0)
        d_a = d_a + jnp.where(row == HCH - 1, _seg_sum(td) + ddec_s[...] * dec, 0.0)
        dlf = _revcumsum64(d_a, row)
        df = dlf / f - (dk_in * e_k + dk_d * e_l)
        dp_ref[1] = (df * (1.0 - lbv) * s * (1.0 - s)).astype(BF16)
        acc_ref[0] += _rows8(df * (1.0 - s))

    def col(off):
        return pl.BlockSpec((None, tb, HW), lambda h, cb: (off, nb - 1 - cb, h))

    hb = lambda h, cb: (nb - 1 - cb, h)
    return _call(
        body, name="hgrn_bwd", grid=(NH // HGRN_HB, nb), job=job,
        args=(dproj, dob, o_raw, proj, proj, proj, proj, st_before, lb_table, norm_g),
        in_specs=[ANY, pl.BlockSpec((tb, HW), hb), pl.BlockSpec((tb, HW), hb),
                  col(2), col(3), col(4), col(5),
                  pl.BlockSpec((HGRN_HB, nc, HD, HD), lambda h, cb: (h, nb - 1 - cb, 0, 0)),
                  pl.BlockSpec((2, HW), lambda h, cb: (0, h)), pl.BlockSpec((1, HW), lambda h, cb: (0, h))],
        out_specs=[pl.BlockSpec((4, tb, HW), lambda h, cb: (0, nb - 1 - cb, h)),
                   pl.BlockSpec((2, 8, HW), lambda h, cb: (0, 0, h))],
        out_shape=[SDS(dproj.shape, BF16), SDS((2, 8, D), F32)],
        scratch_shapes=[pltpu.VMEM((HGRN_HB, HD, HD), F32)] + [pltpu.VMEM((tb, HW), F32)] * 6,
        aliases={0: 0})


def _gmlp_bwd(dproj, da, proj, ln_g, ln_b, wm, wm_t, b_t):
    T = da.shape[0]
    tm = min(256, T)

    def body(dp_in, da_ref, u_ref, v_ref, lg_ref, lb_ref, wm_ref, wmt_ref, bt_ref,
             dp_ref, acc_ref, dws_ref, dmix_ref, du_s, dvn_s):
        del dp_in

        @pl.when(pl.program_id(0) == 0)
        def _():
            acc_ref[...] = jnp.zeros_like(acc_ref)
            dws_ref[...] = jnp.zeros_like(dws_ref)
            dmix_ref[...] = jnp.zeros_like(dmix_ref)

        u = u_ref[...]
        v = v_ref[...]
        lg = lg_ref[...]
        gu, t_u = _gelu(u)
        gv, t_v = _gelu(v)
        vhat, rs = _layer_norm_stats(gv)
        vnb = (vhat * lg + lb_ref[...]).astype(BF16)
        da_v = da_ref[...]
        for ch in range(tm // GCH):
            rows = slice(GCH * ch, GCH * (ch + 1))
            for g in range(NG):
                cols = slice(128 * g, 128 * (g + 1))
                vng = vnb[rows, cols]
                mixed = _mm(wm_ref[g], vng) + bt_ref[:, g:g + 1]
                dag = da_v[rows, cols]
                dmx = dag * gu[rows, cols]
                du_s[rows, cols] = dag * mixed
                dmxb = dmx.astype(BF16)
                dws_ref[:, cols] += _mm_nt(dmxb, vng)
                dmix_ref[:, cols] += dmx
                dvn_s[rows, cols] = _mm(wmt_ref[g], dmxb)
        dp_ref[0] = (du_s[...] * _gelu_grad(u, t_u)).astype(BF16)
        dvn = dvn_s[...]
        acc_ref[0] += _rows8(dvn * vhat)
        acc_ref[1] += _rows8(dvn)
        dvh = dvn * lg
        dgv = rs * (dvh - _mean(dvh) - vhat * _mean(dvh * vhat))
        dp_ref[1] = (dgv * _gelu_grad(v, t_v)).astype(BF16)

    row = lambda i: (0, 0)
    w3 = lambda i: (0, 0, 0)
    return pl.pallas_call(
        body, name="gmlp_bwd", grid=(T // tm,),
        in_specs=[ANY, pl.BlockSpec((tm, D), lambda i: (i, 0)),
                  pl.BlockSpec((None, tm, D), lambda i: (0, i, 0)), pl.BlockSpec((None, tm, D), lambda i: (1, i, 0)),
                  pl.BlockSpec((1, D), row), pl.BlockSpec((1, D), row),
                  pl.BlockSpec((NG, GCH, GCH), w3), pl.BlockSpec((NG, GCH, GCH), w3),
                  pl.BlockSpec((GCH, NG), row)],
        out_specs=[pl.BlockSpec((2, tm, D), lambda i: (2, i, 0)),
                   pl.BlockSpec((2, 8, D), w3), pl.BlockSpec((GCH, D), row), pl.BlockSpec((GCH, D), row)],
        out_shape=[SDS(dproj.shape, BF16), SDS((2, 8, D), F32), SDS((GCH, D), F32), SDS((GCH, D), F32)],
        scratch_shapes=[pltpu.VMEM((tm, D), F32), pltpu.VMEM((tm, D), F32)],
        input_output_aliases={0: 0},
        compiler_params=_cparams(),
    )(dproj, da, proj, proj, ln_g, ln_b, wm, wm_t, b_t)


def _proj_bwd(dproj, w_in4, x, dx1, g_mix, job=None):
    T = x.shape[0]
    tm = min(256, T)
    order = (2, 3, 4, 5, 0, 1, 6, 7)

    def body(dp_ref, w_ref, x_ref, dx1_ref, g_ref, gx_ref, acc_ref):
        @pl.when(pl.program_id(0) == 0)
        def _():
            acc_ref[...] = jnp.zeros_like(acc_ref)

        dh = None
        for m, og in enumerate(order):
            part = _mm_nt(dp_ref[m], w_ref[og // 2, :, D * (og % 2):D * (og % 2 + 1)])
            dh = part if dh is None else dh + part
        xv = x_ref[...]
        r = lax.rsqrt(_mean(xv * xv) + EPS)
        xn = xv * r
        acc_ref[...] += _rows8(dh * xn)
        dxn = dh * g_ref[...]
        gx_ref[...] = dx1_ref[...] + r * (dxn - xn * _mean(dxn * xn))

    t = lambda i: (i, 0)
    return _call(
        body, name="proj_bwd", grid=(T // tm,),
        in_specs=[pl.BlockSpec((NIN, tm, D), lambda i: (0, i, 0)),
                  pl.BlockSpec((NCHIP, D, 2 * D), lambda i: (0, 0, 0), pipeline_mode=pl.Buffered(1)),
                  pl.BlockSpec((tm, D), t), pl.BlockSpec((tm, D), t), pl.BlockSpec((1, D), lambda i: (0, 0))],
        out_specs=[pl.BlockSpec((tm, D), t), pl.BlockSpec((8, D), lambda i: (0, 0))],
        out_shape=[SDS((T, D), F32), SDS((8, D), F32)],
        args=(dproj, w_in4, x, dx1, g_mix), job=job)


def _dw_call(name, a, b, a_spec, b_spec, o_spec, out_shape, nblk, tt, job=None, prefetch=None):
    T = a.shape[-2]

    def body(*refs):
        a_ref, b_ref, o_ref = refs[-3:]

        @pl.when(pl.program_id(1) == 0)
        def _():
            o_ref[...] = jnp.zeros_like(o_ref)
        o_ref[...] += _mm_tn(a_ref[...], b_ref[...])

    (out,), job_out = _call(
        body, name=name, grid=(nblk, T // tt), in_specs=[a_spec, b_spec], out_specs=[o_spec],
        out_shape=[out_shape], args=(a, b), job=job, prefetch=prefetch)
    return out, job_out


def _dw_in_half(name, place, hb, dproj, mine, job=None):
    tt = min(DW_TOKENS, hb.shape[0])

    def comp(k, pc):
        return _component_of(2 * k + (pc[1] if mine else 1 - pc[1]))

    return _dw_call(
        name, hb, dproj,
        pl.BlockSpec((tt, D), lambda k, t, pc: (t, 0)),
        pl.BlockSpec((None, tt, D), lambda k, t, pc: (comp(k, pc), t, 0)),
        pl.BlockSpec((None, D, D), lambda k, t, pc: (k, 0, 0)),
        SDS((NCHIP, D, D), F32), NCHIP, tt, job, place)


def _dw_gate_up(h2b, dgu4, job=None):
    tt = min(DW_TOKENS, h2b.shape[0])
    return _dw_call(
        "dw_gate_up", h2b, dgu4,
        pl.BlockSpec((tt, D), lambda k, t: (t, 0)),
        pl.BlockSpec((None, tt, FFS), lambda k, t: (k, t, 0)),
        pl.BlockSpec((None, D, FFS), lambda k, t: (k, 0, 0)),
        SDS((NCHIP, D, FFS), F32), NCHIP, tt, job)


def _dw_down(act, dx2b, job=None):
    tt = min(DW_TOKENS, act.shape[0])
    g, job_out = _dw_call(
        "dw_down", act, dx2b,
        pl.BlockSpec((tt, FFS), lambda k, t: (t, k)),
        pl.BlockSpec((tt, D), lambda k, t: (t, 0)),
        pl.BlockSpec((FFS, D), lambda k, t: (k, 0)),
        SDS((FF, D), F32), 2, tt, job)
    return g.reshape(NCHIP, FF // NCHIP, D), job_out


def _dw_square(name, a, b, job=None):
    tt = min(DW_TOKENS, a.shape[0])
    g, job_out = _dw_call(
        name, a, b,
        pl.BlockSpec((tt, D), lambda k, t: (t, 0)), pl.BlockSpec((tt, D), lambda k, t: (t, 0)),
        pl.BlockSpec((D, D), lambda k, t: (0, 0)), SDS((D, D), F32), 1, tt, job)
    return g.reshape(NCHIP, D // NCHIP, D), job_out


def _place():
    x, y, c = lax.axis_index("x"), lax.axis_index("y"), lax.axis_index("c")
    return x, y, c, 2 * x + y


def _chip_at(x, y, s):
    return x ^ (s >> 1), y ^ (s & 1)


class _Job:
    def __init__(self, ins, out_shapes, sems, start, finish, aliases=None, mid=None):
        self.ins, self.out_shapes, self.sems = list(ins), list(out_shapes), list(sems)
        self.start, self.finish, self.aliases = start, finish, dict(aliases or {})
        self.mid = mid if mid is not None else (lambda ins, outs, sems: None)


def _join_jobs(*jobs):
    def cut(refs, sizes):
        out, at = [], 0
        for n in sizes:
            out.append(refs[at:at + n])
            at += n
        return out

    ni = [len(j.ins) for j in jobs]
    no = [len(j.out_shapes) for j in jobs]
    ns = [len(j.sems) for j in jobs]

    def run(which):
        def go(ins, outs, sems):
            for j, a, b, c in zip(jobs, cut(ins, ni), cut(outs, no), cut(sems, ns)):
                getattr(j, which)(a, b, c)
        return go

    aliases = {}
    for k, j in enumerate(jobs):
        for a, b in j.aliases.items():
            aliases[sum(ni[:k]) + a] = sum(no[:k]) + b
    return _Job([a for j in jobs for a in j.ins], [o for j in jobs for o in j.out_shapes],
                [s for j in jobs for s in j.sems], run("start"), run("finish"), aliases, run("mid"))


def _call(body, *, name, grid, in_specs, out_specs, out_shape, args, scratch_shapes=(), aliases=None,
          job=None, prefetch=None):
    n_in, n_out, n_scr = len(in_specs), len(out_specs), len(scratch_shapes)
    npf = 0 if prefetch is None else 1
    job = job if job is not None else _Job([], [], [], lambda *a: None, lambda *a: None)
    ji, jo = len(job.ins), len(job.out_shapes)
    steps = math.prod(grid)

    def wrapped(*refs):
        pf, refs = refs[:npf], refs[npf:]
        ins, jin = refs[:n_in], refs[n_in:n_in + ji]
        o0 = n_in + ji
        outs, jout = refs[o0:o0 + n_out], refs[o0 + n_out:o0 + n_out + jo]
        s0 = o0 + n_out + jo
        scr, jsem = refs[s0:s0 + n_scr], refs[s0 + n_scr:]
        step = functools.reduce(lambda acc, ag: acc * ag[1] + pl.program_id(ag[0]), enumerate(grid), 0)
        if ji or jo:
            @pl.when(step == 0)
            def _():
                job.start(jin, jout, jsem)

        body(*pf, *ins, *outs, *scr)

        if ji or jo:
            @pl.when(step == steps // 2)
            def _():
                job.mid(jin, jout, jsem)

            @pl.when(step == steps - 1)
            def _():
                job.finish(jin, jout, jsem)

    io = {npf + a: b for a, b in dict(aliases or {}).items()}
    io.update({npf + n_in + a: n_out + b for a, b in job.aliases.items()})
    kw = dict(in_specs=list(in_specs) + [ANY] * ji, out_specs=list(out_specs) + [ANY] * jo,
              scratch_shapes=list(scratch_shapes) + job.sems)
    if npf:
        kw = dict(grid_spec=pltpu.PrefetchScalarGridSpec(num_scalar_prefetch=1, grid=grid, **kw))
    else:
        kw["grid"] = grid
    res = pl.pallas_call(
        wrapped, name=name, out_shape=list(out_shape) + job.out_shapes, input_output_aliases=io,
        compiler_params=_cparams(has_side_effects=bool(ji or jo)), **kw,
    )(*(() if prefetch is None else (prefetch,)), *args, *job.ins)
    return list(res[:n_out]), list(res[n_out:])


def _run_job(job, name):
    ji, jo = len(job.ins), len(job.out_shapes)

    def body(*refs):
        jin, jout, jsem = refs[:ji], refs[ji:ji + jo], refs[ji + jo:]
        job.start(jin, jout, jsem)
        job.finish(jin, jout, jsem)

    return list(pl.pallas_call(
        body, name=name, in_specs=[ANY] * ji, out_specs=[ANY] * jo, out_shape=job.out_shapes,
        scratch_shapes=job.sems, input_output_aliases=job.aliases,
        compiler_params=pltpu.CompilerParams(has_side_effects=True))(*job.ins))


def _cast_shard(name, place, w):
    rows, cols = w.shape
    tr = 352 if rows % 352 == 0 else 256

    def body(pc_ref, w_ref, o_ref):
        del pc_ref
        o_ref[...] = w_ref[...].astype(BF16)

    return pl.pallas_call(
        body, name=name,
        grid_spec=pltpu.PrefetchScalarGridSpec(
            num_scalar_prefetch=1, grid=(rows // tr,),
            in_specs=[pl.BlockSpec((tr, cols), lambda i, pc: (i, 0))],
            out_specs=pl.BlockSpec((None, tr, cols), lambda i, pc: (pc[0], i, 0))),
        out_shape=SDS((NCHIP, rows, cols), BF16),
        compiler_params=_cparams(),
    )(place, w)


def _sibling_copy(ref, send_sem, recv_sem):
    x, y, c, _ = _place()
    return pltpu.make_async_remote_copy(src_ref=ref, dst_ref=ref, send_sem=send_sem, recv_sem=recv_sem,
                                        device_id=(x, y, 1 - c), device_id_type=MESH)


def _half_rows(arr, slot, core):
    half = arr.shape[1] // 2
    return arr.at[slot, pl.ds(pl.multiple_of(core * half, 16), half)]


def _quarter_rows(arr, slot, core, q):
    quarter = arr.shape[1] // 4
    return arr.at[slot, pl.ds(pl.multiple_of((2 * core + q) * quarter, 16), quarter)]


def _chip_copy(ref, dist, send_sem, recv_sem):
    x, y, c, _ = _place()
    cx, cy = _chip_at(x, y, dist)
    return pltpu.make_async_remote_copy(src_ref=ref, dst_ref=ref, send_sem=send_sem, recv_sem=recv_sem,
                                        device_id=(cx, cy, c), device_id_type=MESH)


def _gather_sems(n):
    dma = pltpu.SemaphoreType.DMA
    return [dma((n, 2))] * 4 + [dma((n, 4))] * 2


def _gather_start(arrs, sems):
    dsend, drecv = sems[0], sems[1]
    _, _, c, j = _place()
    for w, arr in enumerate(arrs):
        for dist in (1, 2):
            _chip_copy(_half_rows(arr, j, c), dist, dsend.at[w, dist - 1], drecv.at[w, dist - 1]).start()


def _gather_land(arrs, sems, dist, first=0):
    dsend, drecv, rsend, rrecv, fsend, frecv = sems
    _, _, c, j = _place()
    if dist < 3:
        other = 3 - dist
        for w, arr in enumerate(arrs, first):
            landed = _half_rows(arr, j ^ dist, c)
            _chip_copy(landed, dist, dsend.at[w, dist - 1], drecv.at[w, dist - 1]).wait_recv()
            relay = _quarter_rows(arr, j ^ dist, c, other - 1)
            _chip_copy(relay, other, rsend.at[w, other - 1], rrecv.at[w, other - 1]).start()
            _sibling_copy(landed, fsend.at[w, dist - 1], frecv.at[w, dist - 1]).start()
        for w, arr in enumerate(arrs, first):
            theirs = _half_rows(arr, j ^ dist, 1 - c)
            _sibling_copy(theirs, fsend.at[w, dist - 1], frecv.at[w, dist - 1]).wait_recv()
    else:
        for w, arr in enumerate(arrs, first):
            for via in (1, 2):
                piece = _quarter_rows(arr, j ^ 3, c, via - 1)
                _chip_copy(piece, via, rsend.at[w, via - 1], rrecv.at[w, via - 1]).wait_recv()
                _sibling_copy(piece, fsend.at[w, 1 + via], frecv.at[w, 1 + via]).start()
        for w, arr in enumerate(arrs, first):
            for via in (1, 2):
                theirs = _quarter_rows(arr, j ^ 3, 1 - c, via - 1)
                _sibling_copy(theirs, fsend.at[w, 1 + via], frecv.at[w, 1 + via]).wait_recv()


def _gather_drain(arrs, sems):
    dsend, drecv, rsend, rrecv, fsend, frecv = sems
    _, _, c, j = _place()
    for w, arr in enumerate(arrs):
        for dist in (1, 2):
            other = 3 - dist
            _chip_copy(_half_rows(arr, j, c), dist, dsend.at[w, dist - 1], drecv.at[w, dist - 1]).wait_send()
            _chip_copy(_quarter_rows(arr, j ^ dist, c, other - 1), other,
                       rsend.at[w, other - 1], rrecv.at[w, other - 1]).wait_send()
            _sibling_copy(_half_rows(arr, j ^ dist, c), fsend.at[w, dist - 1], frecv.at[w, dist - 1]).wait_send()
            _sibling_copy(_quarter_rows(arr, j ^ 3, c, dist - 1),
                          fsend.at[w, 1 + dist], frecv.at[w, 1 + dist]).wait_send()


def _gather_neighbours(arrs, sems):
    _gather_land(arrs, sems, 1)
    _gather_land(arrs, sems, 2)


def _gather_finish(arrs, sems):
    _gather_land(arrs, sems, 3)
    _gather_drain(arrs, sems)


def _gather_job(arrs):
    n = len(arrs)
    return _Job(arrs, [SDS(a.shape, a.dtype) for a in arrs], _gather_sems(n),
                lambda ins, outs, sems: _gather_start(outs, sems),
                lambda ins, outs, sems: _gather_finish(outs, sems), {k: k for k in range(n)},
                mid=lambda ins, outs, sems: _gather_neighbours(outs, sems))


def _exchange_job(arrs, out_shapes, n, copies):
    def start(ins, outs, sems):
        for cp in copies(ins, outs, sems[0], sems[1]):
            cp.start()

    def finish(ins, outs, sems):
        for cp in copies(ins, outs, sems[0], sems[1]):
            cp.wait()

    return _Job(arrs, out_shapes, [pltpu.SemaphoreType.DMA((n,))] * 2, start, finish)


def _pair_exchange_job(grads):
    def copies(ins, outs, send_sem, recv_sem):
        x, y, c, _ = _place()
        res = []
        for w in range(len(grads)):
            half = ins[w].shape[1] // 2
            theirs = pl.ds(pl.multiple_of((1 - c) * half, 8), half)
            res.append(pltpu.make_async_remote_copy(
                src_ref=ins[w].at[:, theirs, :], dst_ref=outs[w], send_sem=send_sem.at[w],
                recv_sem=recv_sem.at[w], device_id=(x, y, 1 - c), device_id_type=MESH))
        return res

    return _exchange_job(grads, [SDS((NCHIP, g.shape[1] // 2, g.shape[2]), F32) for g in grads],
                         len(grads), copies)


def _row_tile(rows, cols):
    tr = rows
    while tr * cols * 4 > ELEMENTWISE_BLOCK_BYTES and tr % 32 == 0:
        tr //= 2
    return tr


def _pair_sum(name, place, g, sib):
    half, cols = sib.shape[1], sib.shape[2]
    tr = _row_tile(half, cols)
    nt = half // tr
    mine = nt if g.shape[1] == 2 * half else 0

    def body(pc_ref, g_ref, s_ref, own_ref, out_ref):
        del pc_ref
        v = g_ref[...] + s_ref[...]
        out_ref[...] = v.astype(BF16)

        @pl.when(pl.program_id(1) == 0)
        def _():
            own_ref[...] = v

    return pl.pallas_call(
        body, name=name,
        grid_spec=pltpu.PrefetchScalarGridSpec(
            num_scalar_prefetch=1, grid=(nt, NCHIP),
            in_specs=[pl.BlockSpec((None, tr, cols), lambda i, s, pc: (pc[0] ^ s, pc[1] * mine + i, 0)),
                      pl.BlockSpec((None, tr, cols), lambda i, s, pc: (pc[0] ^ s, i, 0))],
            out_specs=[pl.BlockSpec((tr, cols), lambda i, s, pc: (i, 0)),
                       pl.BlockSpec((None, tr, cols), lambda i, s, pc: (s, i, 0))]),
        out_shape=[SDS((half, cols), F32), SDS((NCHIP, half, cols), BF16)],
        compiler_params=_cparams(),
    )(place, g, sib)


def _chip_exchange_job(parts):
    def copies(ins, outs, send_sem, recv_sem):
        x, y, c, _ = _place()
        res = []
        for w in range(len(parts)):
            for s in range(1, NCHIP):
                cx, cy = _chip_at(x, y, s)
                k = w * (NCHIP - 1) + s - 1
                res.append(pltpu.make_async_remote_copy(
                    src_ref=ins[w].at[s], dst_ref=outs[w].at[s - 1], send_sem=send_sem.at[k],
                    recv_sem=recv_sem.at[k], device_id=(cx, cy, c), device_id_type=MESH))
        return res

    return _exchange_job(parts, [SDS((NCHIP - 1,) + p.shape[1:], BF16) for p in parts],
                         len(parts) * (NCHIP - 1), copies)


def _chip_sum(name, own, rem):
    half, cols = own.shape
    tr = _row_tile(half, cols)

    def body(own_ref, rem_ref, out_ref):
        out_ref[...] = ((own_ref[...] + rem_ref[0].astype(F32)) + rem_ref[1].astype(F32)) + rem_ref[2].astype(F32)

    return pl.pallas_call(
        body, name=name, grid=(half // tr,),
        in_specs=[pl.BlockSpec((tr, cols), lambda i: (i, 0)),
                  pl.BlockSpec((NCHIP - 1, tr, cols), lambda i: (0, i, 0))],
        out_specs=pl.BlockSpec((tr, cols), lambda i: (i, 0)),
        out_shape=SDS((half, cols), F32),
        compiler_params=_cparams(),
    )(own, rem)


def _share_halves_job(halves):
    def copies(ins, outs, send_sem, recv_sem):
        x, y, c, _ = _place()
        return [pltpu.make_async_remote_copy(
            src_ref=ins[w], dst_ref=outs[w], send_sem=send_sem.at[w], recv_sem=recv_sem.at[w],
            device_id=(x, y, 1 - c), device_id_type=MESH) for w in range(len(halves))]

    return _exchange_job(halves, [SDS(h.shape, F32) for h in halves], len(halves), copies)


def _adamw_math(w, g, m, v):
    m = B1 * m + (1.0 - B1) * g
    v = B2 * v + (1.0 - B2) * (g * g)
    m_hat = m / (1.0 - B1 ** STEP)
    v_hat = v / (1.0 - B2 ** STEP)
    delta = -LR * (m_hat / (jnp.sqrt(v_hat) + AEPS) + WD * w)
    return delta, m, v


def _adamw(name, place, w, own, sib, m, v):
    rows, cols = w.shape
    by_cols = own.shape[0] == rows
    half, pc_cols = (rows, cols // 2) if by_cols else (rows // 2, cols)
    tr = _row_tile(half, pc_cols)
    nt = half // tr

    def body(pc_ref, w_ref, own_ref, sib_ref, m_ref, v_ref, g_ref, d_ref, mo_ref, vo_ref):
        g = jnp.where(pl.program_id(0) == pc_ref[1], own_ref[...], sib_ref[...])
        d, mn, vn = _adamw_math(w_ref[...], g, m_ref[...], v_ref[...])
        g_ref[...] = g
        d_ref[...] = d
        mo_ref[...] = mn
        vo_ref[...] = vn

    full = pl.BlockSpec((tr, pc_cols), (lambda h, i, pc: (i, h)) if by_cols else (lambda h, i, pc: (h * nt + i, 0)))
    part = pl.BlockSpec((tr, pc_cols), lambda h, i, pc: (i, 0))
    return pl.pallas_call(
        body, name=name,
        grid_spec=pltpu.PrefetchScalarGridSpec(
            num_scalar_prefetch=1, grid=(2, nt),
            in_specs=[full, part, part, full, full], out_specs=[full] * 4),
        out_shape=[SDS((rows, cols), F32)] * 4,
        compiler_params=_cparams(),
    )(place, w, own, sib, m, v)


def _small_allreduce_adamw(sp, w, m, v):
    shape = sp.shape

    def body(sp_ref, w_ref, m_ref, v_ref, g_ref, d_ref, mo_ref, vo_ref,
             sib_s, pair_s, chip_s, send_sem, recv_sem):
        x, y, c, j = _place()
        cp = pltpu.make_async_remote_copy(
            src_ref=sp_ref, dst_ref=sib_s, send_sem=send_sem.at[0], recv_sem=recv_sem.at[0],
            device_id=(x, y, 1 - c), device_id_type=MESH)
        cp.start()
        cp.wait()
        pair_s[...] = sp_ref[...] + sib_s[...]
        cps = []
        for s in range(1, NCHIP):
            cx, cy = _chip_at(x, y, s)
            cp = pltpu.make_async_remote_copy(
                src_ref=pair_s, dst_ref=chip_s.at[s], send_sem=send_sem.at[s], recv_sem=recv_sem.at[s],
                device_id=(cx, cy, c), device_id_type=MESH)
            cp.start()
            cps.append(cp)
        chip_s[0] = pair_s[...]
        for cp in cps:
            cp.wait()
        tot = chip_s[j]
        for k in range(1, NCHIP):
            tot = tot + chip_s[k ^ j]
        g_ref[...] = tot
        d, mn, vn = _adamw_math(w_ref[...], tot, m_ref[...], v_ref[...])
        d_ref[...] = d
        mo_ref[...] = mn
        vo_ref[...] = vn

    vm = pl.BlockSpec(memory_space=pltpu.VMEM)
    return pl.pallas_call(
        body, name="small_allreduce_adamw",
        in_specs=[vm] * 4, out_specs=[vm] * 4, out_shape=[SDS(shape, F32)] * 4,
        scratch_shapes=[pltpu.VMEM(shape, F32), pltpu.VMEM(shape, F32), pltpu.VMEM((NCHIP,) + shape, F32),
                        pltpu.SemaphoreType.DMA((NCHIP,)), pltpu.SemaphoreType.DMA((NCHIP,))],
        compiler_params=pltpu.CompilerParams(has_side_effects=True),
    )(sp, w, m, v)


def _pack_small(first, mix, ln_g, ln_b, b_s, lbt, hn, ffn, fin, w_s):
    rows = [first.reshape(1, D), mix.reshape(1, D), ln_g.reshape(1, D), ln_b.reshape(1, D),
            b_s.reshape(1, D), lbt.reshape(2, D), hn.reshape(1, D), ffn.reshape(1, D), fin.reshape(1, D),
            jnp.zeros((6, D), F32)]
    return jnp.concatenate(rows + [w_s.reshape(NG, GCH, GCH).transpose(1, 0, 2).reshape(GCH, D)], axis=0)


def _unpack_small(p):
    w_s = p[16:].reshape(GCH, NG, GCH).transpose(1, 0, 2).reshape(1, NG, GCH, GCH)
    return dict(norm_mix_g=p[1:2], gmlp_ln_g=p[2:3], gmlp_ln_b=p[3:4], gmlp_b_s=p[4].reshape(1, NG, GCH),
                hgrn_lb_table=p[5:7], hgrn_norm_g=p[7:8], norm_ffn_g=p[8:9], norm_final_g=p[9],
                gmlp_w_s=w_s)


SMALL = ("norm_mix_g", "gmlp_ln_g", "gmlp_ln_b", "gmlp_w_s", "gmlp_b_s", "hgrn_lb_table", "hgrn_norm_g",
         "norm_ffn_g", "norm_final_g")
BIG = ("w_in", "w_gate_up", "w_branch_a", "w_branch_b", "w_out", "w_down")
ORDER = ("norm_mix_g", "w_in", "gmlp_ln_g", "gmlp_ln_b", "gmlp_w_s", "gmlp_b_s", "hgrn_lb_table",
         "hgrn_norm_g", "w_branch_a", "w_branch_b", "w_out", "norm_ffn_g", "w_gate_up", "w_down",
         "norm_final_g")


def kernel(x, norm_mix_g, w_in, gmlp_ln_g, gmlp_ln_b, gmlp_w_s, gmlp_b_s, hgrn_lb_table, hgrn_norm_g, w_branch_a, w_branch_b, w_out, norm_ffn_g, w_gate_up, w_down, norm_final_g, loss_target, m_norm_mix_g, m_w_in, m_gmlp_ln_g, m_gmlp_ln_b, m_gmlp_w_s, m_gmlp_b_s, m_hgrn_lb_table, m_hgrn_norm_g, m_w_branch_a, m_w_branch_b, m_w_out, m_norm_ffn_g, m_w_gate_up, m_w_down, m_norm_final_g, v_norm_mix_g, v_w_in, v_gmlp_ln_g, v_gmlp_ln_b, v_gmlp_w_s, v_gmlp_b_s, v_hgrn_lb_table, v_hgrn_norm_g, v_w_branch_a, v_w_branch_b, v_w_out, v_norm_ffn_g, v_w_gate_up, v_w_down, v_norm_final_g):
    args = dict(locals())
    T = x.shape[1]
    xs = x.reshape(T, D)
    target = loss_target.reshape(T, D)
    big = {n: args[n].reshape(args[n].shape[1:]) for n in BIG}
    big_m = {n: args["m_" + n].reshape(args[n].shape[1:]) for n in BIG}
    big_v = {n: args["v_" + n].reshape(args[n].shape[1:]) for n in BIG}

    x_i, y_i, c_i = lax.axis_index("x"), lax.axis_index("y"), lax.axis_index("c")
    place = jnp.stack([2 * x_i + y_i, c_i]).astype(jnp.int32)
    cast = {n: _cast_shard("cast_" + n, place, big[n]) for n in BIG}
    tril = jnp.tril(jnp.ones((GCH, GCH), bool))
    wm = jnp.where(tril, gmlp_w_s[0], 0.0).astype(BF16)
    wm_t = jnp.swapaxes(wm, 1, 2)
    b_t = gmlp_b_s[0].T

    (proj, hb), w_in4, (w_a4,) = _proj_fwd(place, xs, norm_mix_g, cast["w_in"], [cast["w_branch_a"]])
    (ab, y_a), (w_b4, w_out4) = _gmlp_fwd(
        proj, gmlp_ln_g, gmlp_ln_b, wm, b_t, w_a4.reshape(D, D),
        job=_gather_job([cast["w_branch_b"], cast["w_out"]]))
    (o_raw, obb, st_before), (w_gu4, w_down4) = _hgrn_fwd(
        proj, hgrn_lb_table, hgrn_norm_g, job=_gather_job([cast["w_gate_up"], cast["w_down"]]))
    w_a, w_b, w_o = (w.reshape(D, D) for w in (w_a4, w_b4, w_out4))
    w_dn = w_down4.reshape(FF, D)
    y_b, mgb, x1 = _merge_fwd(xs, y_a, obb, proj, w_b, w_o)
    act, dx2b, h2b, dgu4, dx1, dx1b, acc_ffn = _ffn_fwd_bwd(
        x1, target, norm_ffn_g, norm_final_g.reshape(1, D), w_gu4, w_dn)

    grads, owns, parts, halves, sibh = {}, {}, {}, {}, {}

    def pair_sums(names, sibs):
        for n, s in zip(names, sibs):
            owns[n], parts[n] = _pair_sum("rs_pair_sum_" + n, place, grads[n], s)

    def chip_sums(names, got):
        for n, r in zip(names, got):
            halves[n] = _chip_sum("rs_chip_sum_" + n, owns[n], r)

    ffn, mix = ("w_gate_up", "w_down"), ("w_branch_a", "w_branch_b", "w_out")
    grads["w_gate_up"], _ = _dw_gate_up(h2b, dgu4)
    grads["w_down"], _ = _dw_down(act, dx2b)
    (dya, dyb, da, dob, dproj), got = _merge_bwd(
        dx1b, y_a, y_b, proj, w_o, w_a, w_b, job=_pair_exchange_job([grads[n] for n in ffn]))
    pair_sums(ffn, got)
    grads["w_branch_a"], _ = _dw_square("dw_branch_a", ab, dya)
    grads["w_branch_b"], _ = _dw_square("dw_branch_b", obb, dyb)
    grads["w_out"], _ = _dw_square("dw_out", mgb, dx1b)
    (dproj, acc_hgrn), got = _hgrn_bwd(
        dproj, dob, o_raw, proj, st_before, hgrn_lb_table, hgrn_norm_g,
        job=_join_jobs(_chip_exchange_job([parts[n] for n in ffn]), _pair_exchange_job([grads[n] for n in mix])))
    chip_sums(ffn, got[:2])
    pair_sums(mix, got[2:])
    dproj, acc_ln, dws, dmix = _gmlp_bwd(dproj, da, proj, gmlp_ln_g, gmlp_ln_b, wm, wm_t, b_t)
    for_sibling, got = _dw_in_half(
        "dw_in_sibling_half", place, hb, dproj, False,
        job=_join_jobs(_share_halves_job([halves[n] for n in ffn]), _chip_exchange_job([parts[n] for n in mix])))
    sibh.update(zip(ffn, got[:2]))
    chip_sums(mix, got[2:])
    grads["w_in"], got = _dw_in_half(
        "dw_in_own_half", place, hb, dproj, True, job=_share_halves_job([for_sibling]))
    pair_sums(("w_in",), got)
    (grad_x, acc_mix), got = _proj_bwd(
        dproj, w_in4, xs, dx1, norm_mix_g,
        job=_join_jobs(_chip_exchange_job([parts["w_in"]]), _share_halves_job([halves[n] for n in mix])))
    chip_sums(("w_in",), got[:1])
    sibh.update(zip(mix, got[1:]))
    (sibh["w_in"],) = _run_job(_share_halves_job([halves["w_in"]]), "rs_share_halves_w_in")
    out = {}
    for n in BIG:
        g, d, mn, vn = _adamw("adamw_" + n, place, big[n], halves[n], sibh[n], big_m[n], big_v[n])
        shp = args[n].shape
        out[n] = (g.reshape(shp), d.reshape(shp), mn.reshape(shp), vn.reshape(shp))

    lbv = jax.nn.sigmoid(hgrn_lb_table[0] - hgrn_lb_table[1])
    d_t0 = jnp.sum(acc_hgrn[0], axis=0) * lbv * (1.0 - lbv)
    loss_row = jnp.zeros((D,), F32).at[0].set(jnp.sum(acc_ffn[0]))
    dws_m = jnp.where(tril[:, None, :], dws.reshape(GCH, NG, GCH), 0.0).transpose(1, 0, 2)
    db_s = jnp.sum(dmix.reshape(GCH, NG, GCH), axis=-1).T
    sp = _pack_small(loss_row, jnp.sum(acc_mix, 0), jnp.sum(acc_ln[0], 0), jnp.sum(acc_ln[1], 0), db_s,
                     jnp.stack([d_t0, -d_t0]), jnp.sum(acc_hgrn[1], 0), jnp.sum(acc_ffn[2], 0),
                     jnp.sum(acc_ffn[1], 0), dws_m)
    zero = jnp.zeros((D,), F32)

    def pack(prefix):
        a = lambda n: args[prefix + n]
        return _pack_small(zero, a("norm_mix_g"), a("gmlp_ln_g"), a("gmlp_ln_b"), a("gmlp_b_s"),
                           a("hgrn_lb_table"), a("hgrn_norm_g"), a("norm_ffn_g"), a("norm_final_g"),
                           a("gmlp_w_s"))

    packed = _small_allreduce_adamw(sp, pack(""), pack("m_"), pack("v_"))
    loss = packed[0][0, 0]
    small = [_unpack_small(p) for p in packed]
    for n in SMALL:
        out[n] = tuple(s[n] for s in small)
    return (loss, grad_x.reshape(x.shape), *[out[n][0] for n in ORDER], *[out[n][1] for n in ORDER],
            *[out[n][2] for n in ORDER], *[out[n][3] for n in ORDER])
```

```python
import functools
import math

import jax
import jax.numpy as jnp
from jax import lax
from jax.experimental import pallas as pl
from jax.experimental.pallas import tpu as pltpu

F32 = jnp.float32
BF16 = jnp.bfloat16
SDS = jax.ShapeDtypeStruct
MESH = pl.DeviceIdType.MESH
ANY = pl.BlockSpec(memory_space=pl.ANY)

D = 1024
NIN = 8
NG = 8
GCH = 128
NH = 8
HD = 128
HCH = 64
HGRN_HB = 4
HW = HGRN_HB * HD
DW_TOKENS = 2048
ELEMENTWISE_BLOCK_BYTES = 2 * 1024 * 1024
FF = 2816
FFS = 1408
NCHIP = 4
EPS = 1e-6
QSCALE = HD ** -0.5
GELU_C0 = math.sqrt(2.0 / math.pi)
GELU_C1 = 0.044715
LR, B1, B2, AEPS, WD, STEP = 0.001, 0.9, 0.999, 1e-08, 0.01, 10
VMEM_LIMIT_V7X = 56 * 1024 * 1024
SP_ROWS = 144


def _cparams(**kw):
    return pltpu.CompilerParams(vmem_limit_bytes=VMEM_LIMIT_V7X, **kw)


def _mm(a, b):
    return jnp.dot(a, b, preferred_element_type=F32)


def _mm_nt(a, b):
    return lax.dot_general(a, b, (((1,), (1,)), ((), ())), preferred_element_type=F32)


def _mm_tn(a, b):
    return lax.dot_general(a, b, (((0,), (0,)), ((), ())), preferred_element_type=F32)


def _rows8(x):
    r, c = x.shape
    return jnp.sum(x.reshape(r // 8, 8, c), axis=0)


def _mean(x):
    return jnp.mean(x, axis=-1, keepdims=True)


def _sigmoid(x):
    return 1.0 / (1.0 + jnp.exp(-x))


def _gelu(x):
    t = jnp.tanh(GELU_C0 * (x + GELU_C1 * x * x * x))
    return 0.5 * x * (1.0 + t), t


def _gelu_grad(x, t):
    return 0.5 * (1.0 + t) + 0.5 * x * (1.0 - t * t) * (GELU_C0 * (1.0 + 3.0 * GELU_C1 * x * x))


def _component_of(group):
    return jnp.where(group < 6, (group + 4) % 6, group)


def _proj_fwd(place, x, g_mix, w_in4, later):
    T = x.shape[0]
    tm = min(512, T)
    ni = T // tm
    n = len(later)

    def body(pc_ref, x_ref, g_ref, *rest):
        proj_ref, h_ref, w_all = rest[1 + n:4 + n]
        gathered = rest[4 + n:4 + 2 * n]
        hs, wbuf, wsem, obuf, osem = rest[4 + 2 * n:9 + 2 * n]
        w_sems, later_sems = rest[9 + 2 * n:15 + 2 * n], rest[15 + 2 * n:]
        jp, i = pl.program_id(0), pl.program_id(1)
        w_cols = [w_all.at[:, :, pl.ds(k * D, D)] for k in range(2)]

        def w_copy(blk):
            cols = pl.ds(pl.multiple_of((blk % 2) * D, 128), D)
            return pltpu.make_async_copy(w_all.at[pc_ref[0] ^ (blk // 2), :, cols], wbuf.at[blk % 2],
                                         wsem.at[blk % 2])

        @pl.when((jp == 0) & (i == 0))
        def _():
            _gather_start(w_cols, w_sems)
            _gather_start(gathered, later_sems)
            w_copy(jp).start()

        @pl.when(i == 0)
        def _():
            w_copy(jp).wait()

        @pl.when(jp == 0)
        def _():
            xv = x_ref[...]
            r = lax.rsqrt(_mean(xv * xv) + EPS)
            hb = (xv * r * g_ref[...]).astype(BF16)
            hs[i] = hb
            h_ref[...] = hb

        step = jp * ni + i
        slot = step % 2

        def o_copies(slot_):
            comp = 2 * (pc_ref[0] ^ (jp // 2)) + jp % 2
            return [pltpu.make_async_copy(
                obuf.at[slot_, pl.ds(p * (tm // 2), tm // 2)],
                proj_ref.at[comp, pl.ds(pl.multiple_of(i * tm + p * (tm // 2), 8), tm // 2)],
                osem.at[slot_, p]) for p in range(2)]

        @pl.when(step >= 2)
        def _():
            for cp in o_copies(slot):
                cp.wait()

        obuf[slot] = _mm(hs[i], wbuf[jp % 2])
        for cp in o_copies(slot):
            cp.start()

        @pl.when(step == NIN * ni - 1)
        def _():
            for cp in o_copies(1 - slot) + o_copies(slot):
                cp.wait()

        for nxt in range(1, NIN):
            @pl.when((jp == nxt - 1) & (i == ni - 1))
            def _():
                if nxt >= 2:
                    _gather_land([w_cols[nxt % 2]], w_sems, nxt // 2, first=nxt % 2)
                if nxt == 4:
                    _gather_neighbours(gathered, later_sems)
                w_copy(jp + 1).start()

        @pl.when((jp == NIN - 1) & (i == ni - 1))
        def _():
            _gather_drain(w_cols, w_sems)
            _gather_finish(gathered, later_sems)

    tile = lambda jp, i, pc: (jnp.where(jp == 0, i, ni - 1), 0)
    res = pl.pallas_call(
        body, name="proj_fwd",
        grid_spec=pltpu.PrefetchScalarGridSpec(
            num_scalar_prefetch=1, grid=(NIN, ni),
            in_specs=[pl.BlockSpec((tm, D), tile), pl.BlockSpec((1, D), lambda jp, i, pc: (0, 0))] + [ANY] * (1 + n),
            out_specs=[ANY, pl.BlockSpec((tm, D), tile)] + [ANY] * (1 + n),
            scratch_shapes=[pltpu.VMEM((ni, tm, D), BF16), pltpu.VMEM((2, D, D), BF16),
                            pltpu.SemaphoreType.DMA((2,)), pltpu.VMEM((2, tm, D), F32),
                            pltpu.SemaphoreType.DMA((2, 2))] + _gather_sems(2) + _gather_sems(n)),
        out_shape=[SDS((NIN, T, D), F32), SDS((T, D), BF16), SDS(w_in4.shape, BF16)]
        + [SDS(a.shape, a.dtype) for a in later],
        input_output_aliases={3 + k: 2 + k for k in range(1 + n)},
        compiler_params=_cparams(has_side_effects=True),
    )(place, x, g_mix, w_in4, *later)
    return res[:2], res[2], res[3:]


def _layer_norm_stats(gv):
    mu = _mean(gv)
    xc = gv - mu
    rs = lax.rsqrt(_mean(xc * xc) + EPS)
    return xc * rs, rs


def _gmlp_fwd(proj, ln_g, ln_b, wm, b_t, w_a, job=None):
    T = proj.shape[1]
    tm = min(256, T)

    def body(u_ref, v_ref, lg_ref, lb_ref, wm_ref, bt_ref, wa_ref, a_ref, ya_ref, a_s):
        gu, _ = _gelu(u_ref[...])
        gv, _ = _gelu(v_ref[...])
        vhat, _ = _layer_norm_stats(gv)
        vnb = (vhat * lg_ref[...] + lb_ref[...]).astype(BF16)
        for ch in range(tm // GCH):
            rows = slice(GCH * ch, GCH * (ch + 1))
            for g in range(NG):
                cols = slice(128 * g, 128 * (g + 1))
                mixed = _mm(wm_ref[g], vnb[rows, cols]) + bt_ref[:, g:g + 1]
                a_s[rows, cols] = gu[rows, cols] * mixed
        ab = a_s[...].astype(BF16)
        a_ref[...] = ab
        ya_ref[...] = _mm(ab, wa_ref[...])

    row = lambda i: (0, 0)
    return _call(
        body, name="gmlp_fwd", grid=(T // tm,), job=job, args=(proj, proj, ln_g, ln_b, wm, b_t, w_a),
        in_specs=[pl.BlockSpec((None, tm, D), lambda i: (0, i, 0)), pl.BlockSpec((None, tm, D), lambda i: (1, i, 0)),
                  pl.BlockSpec((1, D), row), pl.BlockSpec((1, D), row),
                  pl.BlockSpec((NG, GCH, GCH), lambda i: (0, 0, 0)), pl.BlockSpec((GCH, NG), row),
                  pl.BlockSpec((D, D), row)],
        out_specs=[pl.BlockSpec((tm, D), lambda i: (i, 0)), pl.BlockSpec((tm, D), lambda i: (i, 0))],
        out_shape=[SDS((T, D), BF16), SDS((T, D), F32)],
        scratch_shapes=[pltpu.VMEM((tm, D), F32)])


def _cumsum64(x, row):
    for s in (1, 2, 4, 8, 16, 32):
        x = x + jnp.where(row >= s, pltpu.roll(x, s, 0), 0.0)
    return x


def _revcumsum64(x, row):
    n = x.shape[0]
    for s in (1, 2, 4, 8, 16, 32):
        x = x + jnp.where(row < HCH - s, pltpu.roll(x, n - s, 0), 0.0)
    return x


def _head_mean(x):
    parts = [jnp.broadcast_to(_mean(x[:, HD * h:HD * (h + 1)]), (x.shape[0], HD)) for h in range(x.shape[1] // HD)]
    return jnp.concatenate(parts, axis=1)


def _seg_sum(x):
    n, c = x.shape
    s = jnp.sum(x.reshape(n // HCH, HCH, c), axis=1, keepdims=True)
    return jnp.broadcast_to(s, (n // HCH, HCH, c)).reshape(n, c)


def _hgrn_gates(fl, lbv, row):
    s = _sigmoid(fl)
    f = lbv + (1.0 - lbv) * s
    a = _cumsum64(jnp.log(f), row)
    a_mid = _seg_sum(jnp.where(row == HCH // 2 - 1, a, 0.0))
    a_last = _seg_sum(jnp.where(row == HCH - 1, a, 0.0))
    return s, f, a, a_mid, a_last


def _hgrn_fwd(proj, lb_table, norm_g, job=None):
    T = proj.shape[1]
    tb = min(512, T)
    nc = tb // HCH

    def body(q_ref, fl_ref, v_ref, g_ref, lbt_ref, gn_ref, o_ref, ob_ref, stb_ref, st_s, o_s):
        @pl.when(pl.program_id(1) == 0)
        def _():
            st_s[...] = jnp.zeros_like(st_s)

        row = lax.broadcasted_iota(jnp.int32, (tb, HW), 0) & (HCH - 1)
        lbv = _sigmoid(lbt_ref[0:1, :] - lbt_ref[1:2, :])
        _, f, a, a_mid, a_last = _hgrn_gates(fl_ref[...], lbv, row)
        k = 1.0 - f
        qs = q_ref[...] * QSCALE
        q_in = (qs * jnp.exp(a - a_mid)).astype(BF16)
        k_in = (k * jnp.exp(a_mid - a)).astype(BF16)
        q_a = (qs * jnp.exp(a)).astype(BF16)
        k_d = (k * jnp.exp(a_last - a)).astype(BF16)
        dec = jnp.exp(a_last)
        vb = v_ref[...].astype(BF16)
        tri = (lax.broadcasted_iota(jnp.int32, (HCH, HCH), 0)
               >= lax.broadcasted_iota(jnp.int32, (HCH, HCH), 1))
        for c in range(nc):
            sl = slice(HCH * c, HCH * (c + 1))
            for hh in range(HGRN_HB):
                hs = slice(HD * hh, HD * (hh + 1))
                st = st_s[hh]
                stb_ref[hh, c] = st
                sc = jnp.where(tri, _mm_nt(q_in[sl, hs], k_in[sl, hs]), 0.0)
                o_s[sl, hs] = _mm(sc.astype(BF16), vb[sl, hs]) + _mm_nt(q_a[sl, hs], st.astype(BF16))
                d64 = dec[sl, hs]
                st_s[hh] = st * jnp.concatenate([d64, d64], axis=0) + _mm_tn(vb[sl, hs], k_d[sl, hs])
        o = o_s[...]
        r = lax.rsqrt(_head_mean(o * o) + EPS)
        g = g_ref[...]
        o_ref[...] = o
        ob_ref[...] = (o * r * gn_ref[...] * (g * _sigmoid(g))).astype(BF16)

    def col(off):
        return pl.BlockSpec((None, tb, HW), lambda h, cb: (off, cb, h))

    return _call(
        body, name="hgrn_fwd", grid=(NH // HGRN_HB, T // tb), job=job,
        args=(proj, proj, proj, proj, lb_table, norm_g),
        in_specs=[col(2), col(3), col(4), col(5),
                  pl.BlockSpec((2, HW), lambda h, cb: (0, h)), pl.BlockSpec((1, HW), lambda h, cb: (0, h))],
        out_specs=[pl.BlockSpec((tb, HW), lambda h, cb: (cb, h)), pl.BlockSpec((tb, HW), lambda h, cb: (cb, h)),
                   pl.BlockSpec((HGRN_HB, nc, HD, HD), lambda h, cb: (h, cb, 0, 0))],
        out_shape=[SDS((T, D), F32), SDS((T, D), BF16), SDS((NH, T // HCH, HD, HD), F32)],
        scratch_shapes=[pltpu.VMEM((HGRN_HB, HD, HD), F32), pltpu.VMEM((tb, HW), F32)])


def _merge_fwd(x, y_a, ob, proj, w_b, w_out, job=None):
    T = x.shape[0]
    tm = min(512, T)

    def body(x_ref, ya_ref, ob_ref, ga_ref, gb_ref, wb_ref, wo_ref, yb_ref, mg_ref, x1_ref):
        yb = _mm(ob_ref[...], wb_ref[...])
        merged = (_sigmoid(ga_ref[...]) * ya_ref[...] + _sigmoid(gb_ref[...]) * yb).astype(BF16)
        yb_ref[...] = yb
        mg_ref[...] = merged
        x1_ref[...] = x_ref[...] + _mm(merged, wo_ref[...])

    t = lambda i: (i, 0)
    w = lambda i: (0, 0)
    return _call(
        body, name="merge_fwd", grid=(T // tm,), job=job, args=(x, y_a, ob, proj, proj, w_b, w_out),
        in_specs=[pl.BlockSpec((tm, D), t), pl.BlockSpec((tm, D), t), pl.BlockSpec((tm, D), t),
                  pl.BlockSpec((None, tm, D), lambda i: (6, i, 0)), pl.BlockSpec((None, tm, D), lambda i: (7, i, 0)),
                  pl.BlockSpec((D, D), w), pl.BlockSpec((D, D), w)],
        out_specs=[pl.BlockSpec((tm, D), t)] * 3,
        out_shape=[SDS((T, D), F32), SDS((T, D), BF16), SDS((T, D), F32)])


def _ffn_fwd_bwd(x1, target, g_ffn, g_fin, w_gu4, w_down):
    T = x1.shape[0]
    tm = min(256, T)
    inv_d = 1.0 / D

    def body(x1_ref, tg_ref, gf_ref, gn_ref, wgu_ref, wd_ref,
             act_ref, dx2b_ref, h2b_ref, dgu_ref, dx1_ref, dx1b_ref, acc_ref):
        @pl.when(pl.program_id(0) == 0)
        def _():
            acc_ref[...] = jnp.zeros_like(acc_ref)

        x1v = x1_ref[...]
        gf = gf_ref[...]
        gn = gn_ref[...]
        rr1 = lax.rsqrt(_mean(x1v * x1v) + EPS)
        x1n = x1v * rr1
        h2b = (x1n * gf).astype(BF16)
        h2b_ref[...] = h2b
        p = [_mm(h2b, wgu_ref[k]) for k in range(NCHIP)]
        sg = [_sigmoid(p[0]), _sigmoid(p[1])]
        si = [p[0] * sg[0], p[1] * sg[1]]
        x2 = x1v
        for k in range(2):
            actk = (si[k] * p[2 + k]).astype(BF16)
            act_ref[:, FFS * k:FFS * (k + 1)] = actk
            x2 = x2 + _mm(actk, wd_ref[FFS * k:FFS * (k + 1), :])
        rr2 = lax.rsqrt(_mean(x2 * x2) + EPS)
        x2n = x2 * rr2
        e = x2n * gn - tg_ref[...]
        acc_ref[0] += _rows8(e * e) * (0.5 * inv_d)
        dy = e * inv_d
        acc_ref[1] += _rows8(dy * x2n)
        dxn = dy * gn
        dx2 = rr2 * (dxn - x2n * _mean(dxn * x2n))
        dx2b = dx2.astype(BF16)
        dx2b_ref[...] = dx2b
        dh2 = None
        for k in range(2):
            dact = _mm_nt(dx2b, wd_ref[FFS * k:FFS * (k + 1), :])
            dgate = (dact * p[2 + k] * (sg[k] * (1.0 + p[k] * (1.0 - sg[k])))).astype(BF16)
            dup = (dact * si[k]).astype(BF16)
            dgu_ref[k] = dgate
            dgu_ref[2 + k] = dup
            part = _mm_nt(dgate, wgu_ref[k]) + _mm_nt(dup, wgu_ref[2 + k])
            dh2 = part if dh2 is None else dh2 + part
        acc_ref[2] += _rows8(dh2 * x1n)
        dxn1 = dh2 * gf
        dx1 = dx2 + rr1 * (dxn1 - x1n * _mean(dxn1 * x1n))
        dx1_ref[...] = dx1
        dx1b_ref[...] = dx1.astype(BF16)

    t = lambda i: (i, 0)
    w = lambda i: (0, 0)
    one = pl.Buffered(1)
    return pl.pallas_call(
        body, name="ffn_fwd_bwd", grid=(T // tm,),
        in_specs=[pl.BlockSpec((tm, D), t), pl.BlockSpec((tm, D), t),
                  pl.BlockSpec((1, D), w), pl.BlockSpec((1, D), w),
                  pl.BlockSpec((NCHIP, D, FFS), lambda i: (0, 0, 0), pipeline_mode=one),
                  pl.BlockSpec((FF, D), w, pipeline_mode=one)],
        out_specs=[pl.BlockSpec((tm, FF), t), pl.BlockSpec((tm, D), t), pl.BlockSpec((tm, D), t),
                   pl.BlockSpec((NCHIP, tm, FFS), lambda i: (0, i, 0)),
                   pl.BlockSpec((tm, D), t), pl.BlockSpec((tm, D), t),
                   pl.BlockSpec((3, 8, D), lambda i: (0, 0, 0))],
        out_shape=[SDS((T, FF), BF16), SDS((T, D), BF16), SDS((T, D), BF16),
                   SDS((NCHIP, T, FFS), BF16), SDS((T, D), F32), SDS((T, D), BF16),
                   SDS((3, 8, D), F32)],
        compiler_params=_cparams(),
    )(x1, target, g_ffn, g_fin, w_gu4, w_down)


def _merge_bwd(dx1b, y_a, y_b, proj, w_out, w_a, w_b, job=None):
    T = dx1b.shape[0]
    tm = min(512, T)

    def body(dx_ref, ya_ref, yb_ref, ga_ref, gb_ref, wo_ref, wa_ref, wb_ref,
             dya_ref, dyb_ref, da_ref, dob_ref, dp_ref):
        dm = _mm_nt(dx_ref[...], wo_ref[...])
        sa = _sigmoid(ga_ref[...])
        sb = _sigmoid(gb_ref[...])
        dya = (dm * sa).astype(BF16)
        dyb = (dm * sb).astype(BF16)
        dya_ref[...] = dya
        dyb_ref[...] = dyb
        dp_ref[0] = (dm * ya_ref[...] * sa * (1.0 - sa)).astype(BF16)
        dp_ref[1] = (dm * yb_ref[...] * sb * (1.0 - sb)).astype(BF16)
        da_ref[...] = _mm_nt(dya, wa_ref[...])
        dob_ref[...] = _mm_nt(dyb, wb_ref[...])

    t = lambda i: (i, 0)
    w = lambda i: (0, 0)
    return _call(
        body, name="merge_bwd", grid=(T // tm,),
        in_specs=[pl.BlockSpec((tm, D), t), pl.BlockSpec((tm, D), t), pl.BlockSpec((tm, D), t),
                  pl.BlockSpec((None, tm, D), lambda i: (6, i, 0)), pl.BlockSpec((None, tm, D), lambda i: (7, i, 0)),
                  pl.BlockSpec((D, D), w), pl.BlockSpec((D, D), w), pl.BlockSpec((D, D), w)],
        out_specs=[pl.BlockSpec((tm, D), t)] * 4 + [pl.BlockSpec((2, tm, D), lambda i: (3, i, 0))],
        out_shape=[SDS((T, D), BF16), SDS((T, D), BF16), SDS((T, D), F32), SDS((T, D), F32),
                   SDS((NIN, T, D), BF16)],
        args=(dx1b, y_a, y_b, proj, proj, w_out, w_a, w_b), job=job)


def _hgrn_bwd(dproj, dob, o_raw, proj, st_before, lb_table, norm_g, job=None):
    T = dob.shape[0]
    tb = min(512, T)
    nc = tb // HCH
    nb = T // tb

    def body(dp_in, dob_ref, o_ref, q_ref, fl_ref, v_ref, g_ref, stb_ref, lbt_ref, gn_ref,
             dp_ref, acc_ref, dst_s, dqin_s, dqa_s, dkin_s, dkd_s, dv_s, ddec_s):
        del dp_in

        @pl.when(pl.program_id(1) == 0)
        def _():
            dst_s[...] = jnp.zeros_like(dst_s)
            acc_ref[...] = jnp.zeros_like(acc_ref)

        row = lax.broadcasted_iota(jnp.int32, (tb, HW), 0) & (HCH - 1)
        gn = gn_ref[...]
        lbv = _sigmoid(lbt_ref[0:1, :] - lbt_ref[1:2, :])
        o = o_ref[...]
        r = lax.rsqrt(_head_mean(o * o) + EPS)
        on = o * r
        g = g_ref[...]
        sgm = _sigmoid(g)
        dob_v = dob_ref[...]
        dp_ref[3] = (dob_v * on * gn * (sgm * (1.0 + g * (1.0 - sgm)))).astype(BF16)
        do_n = dob_v * (g * sgm)
        acc_ref[1] += _rows8(do_n * on)
        dxn = do_n * gn
        do = (r * (dxn - on * _head_mean(dxn * on))).astype(BF16)
        s, f, a, a_mid, a_last = _hgrn_gates(fl_ref[...], lbv, row)
        k = 1.0 - f
        qs = q_ref[...] * QSCALE
        e_q = jnp.exp(a - a_mid)
        e_k = jnp.exp(a_mid - a)
        e_a = jnp.exp(a)
        e_l = jnp.exp(a_last - a)
        dec = jnp.exp(a_last)
        q_in = qs * e_q
        k_in = k * e_k
        q_a = qs * e_a
        k_d = k * e_l
        q_inb, k_inb, q_ab, k_db = (z.astype(BF16) for z in (q_in, k_in, q_a, k_d))
        vb = v_ref[...].astype(BF16)
        tri = (lax.broadcasted_iota(jnp.int32, (HCH, HCH), 0)
               >= lax.broadcasted_iota(jnp.int32, (HCH, HCH), 1))
        for c in reversed(range(nc)):
            sl = slice(HCH * c, HCH * (c + 1))
            for hh in range(HGRN_HB):
                hs = slice(HD * hh, HD * (hh + 1))
                stp = stb_ref[hh, c]
                dst = dst_s[hh]
                dstb = dst.astype(BF16)
                do_c = do[sl, hs]
                v_c = vb[sl, hs]
                dqa_s[sl, hs] = _mm(do_c, stp.astype(BF16))
                dkd_s[sl, hs] = _mm(v_c, dstb)
                ddec_s[sl, hs] = jnp.broadcast_to(jnp.sum(dst * stp, axis=0, keepdims=True), (HCH, HD))
                sc = jnp.where(tri, _mm_nt(q_inb[sl, hs], k_inb[sl, hs]), 0.0).astype(BF16)
                dsc = jnp.where(tri, _mm_nt(do_c, v_c), 0.0).astype(BF16)
                dv_s[sl, hs] = _mm_nt(k_db[sl, hs], dstb) + _mm_tn(sc, do_c)
                dqin_s[sl, hs] = _mm(dsc, k_inb[sl, hs])
                dkin_s[sl, hs] = _mm_tn(dsc, q_inb[sl, hs])
                d64 = dec[sl, hs]
                dst_s[hh] = dst * jnp.concatenate([d64, d64], axis=0) + _mm_tn(do_c, q_ab[sl, hs])
        dq_in = dqin_s[...]
        dq_a = dqa_s[...]
        dk_in = dkin_s[...]
        dk_d = dkd_s[...]
        dp_ref[0] = ((dq_in * e_q + dq_a * e_a) * QSCALE).astype(BF16)
        dp_ref[2] = dv_s[...].astype(BF16)
        tq = dq_in * q_in
        tk = dk_in * k_in
        td = dk_d * k_d
        d_a = tq + dq_a * q_a - tk - td
        d_a = d_a + jnp.where(row == HCH // 2 - 1, _seg_sum(tk - tq), 0.0)
        d_a = d_a + jnp.where(row == HCH - 1, _seg_sum(td) + ddec_s[...] * dec, 0.0)
        dlf = _revcumsum64(d_a, row)
        df = dlf / f - (dk_in * e_k + dk_d * e_l)
        dp_ref[1] = (df * (1.0 - lbv) * s * (1.0 - s)).astype(BF16)
        acc_ref[0] += _rows8(df * (1.0 - s))

    def col(off):
        return pl.BlockSpec((None, tb, HW), lambda h, cb: (off, nb - 1 - cb, h))

    hb = lambda h, cb: (nb - 1 - cb, h)
    return _call(
        body, name="hgrn_bwd", grid=(NH // HGRN_HB, nb), job=job,
        args=(dproj, dob, o_raw, proj, proj, proj, proj, st_before, lb_table, norm_g),
        in_specs=[ANY, pl.BlockSpec((tb, HW), hb), pl.BlockSpec((tb, HW), hb),
                  col(2), col(3), col(4), col(5),
                  pl.BlockSpec((HGRN_HB, nc, HD, HD), lambda h, cb: (h, nb - 1 - cb, 0, 0)),
                  pl.BlockSpec((2, HW), lambda h, cb: (0, h)), pl.BlockSpec((1, HW), lambda h, cb: (0, h))],
        out_specs=[pl.BlockSpec((4, tb, HW), lambda h, cb: (0, nb - 1 - cb, h)),
                   pl.BlockSpec((2, 8, HW), lambda h, cb: (0, 0, h))],
        out_shape=[SDS(dproj.shape, BF16), SDS((2, 8, D), F32)],
        scratch_shapes=[pltpu.VMEM((HGRN_HB, HD, HD), F32)] + [pltpu.VMEM((tb, HW), F32)] * 6,
        aliases={0: 0})


def _gmlp_bwd(dproj, da, proj, ln_g, ln_b, wm, wm_t, b_t):
    T = da.shape[0]
    tm = min(256, T)

    def body(dp_in, da_ref, u_ref, v_ref, lg_ref, lb_ref, wm_ref, wmt_ref, bt_ref,
             dp_ref, acc_ref, dws_ref, dmix_ref, du_s, dvn_s):
        del dp_in

        @pl.when(pl.program_id(0) == 0)
        def _():
            acc_ref[...] = jnp.zeros_like(acc_ref)
            dws_ref[...] = jnp.zeros_like(dws_ref)
            dmix_ref[...] = jnp.zeros_like(dmix_ref)

        u = u_ref[...]
        v = v_ref[...]
        lg = lg_ref[...]
        gu, t_u = _gelu(u)
        gv, t_v = _gelu(v)
        vhat, rs = _layer_norm_stats(gv)
        vnb = (vhat * lg + lb_ref[...]).astype(BF16)
        da_v = da_ref[...]
        for ch in range(tm // GCH):
            rows = slice(GCH * ch, GCH * (ch + 1))
            for g in range(NG):
                cols = slice(128 * g, 128 * (g + 1))
                vng = vnb[rows, cols]
                mixed = _mm(wm_ref[g], vng) + bt_ref[:, g:g + 1]
                dag = da_v[rows, cols]
                dmx = dag * gu[rows, cols]
                du_s[rows, cols] = dag * mixed
                dmxb = dmx.astype(BF16)
                dws_ref[:, cols] += _mm_nt(dmxb, vng)
                dmix_ref[:, cols] += dmx
                dvn_s[rows, cols] = _mm(wmt_ref[g], dmxb)
        dp_ref[0] = (du_s[...] * _gelu_grad(u, t_u)).astype(BF16)
        dvn = dvn_s[...]
        acc_ref[0] += _rows8(dvn * vhat)
        acc_ref[1] += _rows8(dvn)
        dvh = dvn * lg
        dgv = rs * (dvh - _mean(dvh) - vhat * _mean(dvh * vhat))
        dp_ref[1] = (dgv * _gelu_grad(v, t_v)).astype(BF16)

    row = lambda i: (0, 0)
    w3 = lambda i: (0, 0, 0)
    return pl.pallas_call(
        body, name="gmlp_bwd", grid=(T // tm,),
        in_specs=[ANY, pl.BlockSpec((tm, D), lambda i: (i, 0)),
                  pl.BlockSpec((None, tm, D), lambda i: (0, i, 0)), pl.BlockSpec((None, tm, D), lambda i: (1, i, 0)),
                  pl.BlockSpec((1, D), row), pl.BlockSpec((1, D), row),
                  pl.BlockSpec((NG, GCH, GCH), w3), pl.BlockSpec((NG, GCH, GCH), w3),
                  pl.BlockSpec((GCH, NG), row)],
        out_specs=[pl.BlockSpec((2, tm, D), lambda i: (2, i, 0)),
                   pl.BlockSpec((2, 8, D), w3), pl.BlockSpec((GCH, D), row), pl.BlockSpec((GCH, D), row)],
        out_shape=[SDS(dproj.shape, BF16), SDS((2, 8, D), F32), SDS((GCH, D), F32), SDS((GCH, D), F32)],
        scratch_shapes=[pltpu.VMEM((tm, D), F32), pltpu.VMEM((tm, D), F32)],
        input_output_aliases={0: 0},
        compiler_params=_cparams(),
    )(dproj, da, proj, proj, ln_g, ln_b, wm, wm_t, b_t)


def _proj_bwd(dproj, w_in4, x, dx1, g_mix, job=None):
    T = x.shape[0]
    tm = min(256, T)
    order = (2, 3, 4, 5, 0, 1, 6, 7)

    def body(dp_ref, w_ref, x_ref, dx1_ref, g_ref, gx_ref, acc_ref):
        @pl.when(pl.program_id(0) == 0)
        def _():
            acc_ref[...] = jnp.zeros_like(acc_ref)

        dh = None
        for m, og in enumerate(order):
            part = _mm_nt(dp_ref[m], w_ref[og // 2, :, D * (og % 2):D * (og % 2 + 1)])
            dh = part if dh is None else dh + part
        xv = x_ref[...]
        r = lax.rsqrt(_mean(xv * xv) + EPS)
        xn = xv * r
        acc_ref[...] += _rows8(dh * xn)
        dxn = dh * g_ref[...]
        gx_ref[...] = dx1_ref[...] + r * (dxn - xn * _mean(dxn * xn))

    t = lambda i: (i, 0)
    return _call(
        body, name="proj_bwd", grid=(T // tm,),
        in_specs=[pl.BlockSpec((NIN, tm, D), lambda i: (0, i, 0)),
                  pl.BlockSpec((NCHIP, D, 2 * D), lambda i: (0, 0, 0), pipeline_mode=pl.Buffered(1)),
                  pl.BlockSpec((tm, D), t), pl.BlockSpec((tm, D), t), pl.BlockSpec((1, D), lambda i: (0, 0))],
        out_specs=[pl.BlockSpec((tm, D), t), pl.BlockSpec((8, D), lambda i: (0, 0))],
        out_shape=[SDS((T, D), F32), SDS((8, D), F32)],
        args=(dproj, w_in4, x, dx1, g_mix), job=job)


def _dw_call(name, a, b, a_spec, b_spec, o_spec, out_shape, nblk, tt, job=None, prefetch=None):
    T = a.shape[-2]

    def body(*refs):
        a_ref, b_ref, o_ref = refs[-3:]

        @pl.when(pl.program_id(1) == 0)
        def _():
            o_ref[...] = jnp.zeros_like(o_ref)
        o_ref[...] += _mm_tn(a_ref[...], b_ref[...])

    (out,), job_out = _call(
        body, name=name, grid=(nblk, T // tt), in_specs=[a_spec, b_spec], out_specs=[o_spec],
        out_shape=[out_shape], args=(a, b), job=job, prefetch=prefetch)
    return out, job_out


def _dw_in_half(name, place, hb, dproj, mine, job=None):
    tt = min(DW_TOKENS, hb.shape[0])

    def comp(k, pc):
        return _component_of(2 * k + (pc[1] if mine else 1 - pc[1]))

    return _dw_call(
        name, hb, dproj,
        pl.BlockSpec((tt, D), lambda k, t, pc: (t, 0)),
        pl.BlockSpec((None, tt, D), lambda k, t, pc: (comp(k, pc), t, 0)),
        pl.BlockSpec((None, D, D), lambda k, t, pc: (k, 0, 0)),
        SDS((NCHIP, D, D), F32), NCHIP, tt, job, place)


def _dw_gate_up(h2b, dgu4, job=None):
    tt = min(DW_TOKENS, h2b.shape[0])
    return _dw_call(
        "dw_gate_up", h2b, dgu4,
        pl.BlockSpec((tt, D), lambda k, t: (t, 0)),
        pl.BlockSpec((None, tt, FFS), lambda k, t: (k, t, 0)),
        pl.BlockSpec((None, D, FFS), lambda k, t: (k, 0, 0)),
        SDS((NCHIP, D, FFS), F32), NCHIP, tt, job)


def _dw_down(act, dx2b, job=None):
    tt = min(DW_TOKENS, act.shape[0])
    g, job_out = _dw_call(
        "dw_down", act, dx2b,
        pl.BlockSpec((tt, FFS), lambda k, t: (t, k)),
        pl.BlockSpec((tt, D), lambda k, t: (t, 0)),
        pl.BlockSpec((FFS, D), lambda k, t: (k, 0)),
        SDS((FF, D), F32), 2, tt, job)
    return g.reshape(NCHIP, FF // NCHIP, D), job_out


def _dw_square(name, a, b, job=None):
    tt = min(DW_TOKENS, a.shape[0])
    g, job_out = _dw_call(
        name, a, b,
        pl.BlockSpec((tt, D), lambda k, t: (t, 0)), pl.BlockSpec((tt, D), lambda k, t: (t, 0)),
        pl.BlockSpec((D, D), lambda k, t: (0, 0)), SDS((D, D), F32), 1, tt, job)
    return g.reshape(NCHIP, D // NCHIP, D), job_out


def _place():
    x, y, c = lax.axis_index("x"), lax.axis_index("y"), lax.axis_index("c")
    return x, y, c, 2 * x + y


def _chip_at(x, y, s):
    return x ^ (s >> 1), y ^ (s & 1)


class _Job:
    def __init__(self, ins, out_shapes, sems, start, finish, aliases=None, mid=None):
        self.ins, self.out_shapes, self.sems = list(ins), list(out_shapes), list(sems)
        self.start, self.finish, self.aliases = start, finish, dict(aliases or {})
        self.mid = mid if mid is not None else (lambda ins, outs, sems: None)


def _join_jobs(*jobs):
    def cut(refs, sizes):
        out, at = [], 0
        for n in sizes:
            out.append(refs[at:at + n])
            at += n
        return out

    ni = [len(j.ins) for j in jobs]
    no = [len(j.out_shapes) for j in jobs]
    ns = [len(j.sems) for j in jobs]

    def run(which):
        def go(ins, outs, sems):
            for j, a, b, c in zip(jobs, cut(ins, ni), cut(outs, no), cut(sems, ns)):
                getattr(j, which)(a, b, c)
        return go

    aliases = {}
    for k, j in enumerate(jobs):
        for a, b in j.aliases.items():
            aliases[sum(ni[:k]) + a] = sum(no[:k]) + b
    return _Job([a for j in jobs for a in j.ins], [o for j in jobs for o in j.out_shapes],
                [s for j in jobs for s in j.sems], run("start"), run("finish"), aliases, run("mid"))


def _call(body, *, name, grid, in_specs, out_specs, out_shape, args, scratch_shapes=(), aliases=None,
          job=None, prefetch=None):
    n_in, n_out, n_scr = len(in_specs), len(out_specs), len(scratch_shapes)
    npf = 0 if prefetch is None else 1
    job = job if job is not None else _Job([], [], [], lambda *a: None, lambda *a: None)
    ji, jo = len(job.ins), len(job.out_shapes)
    steps = math.prod(grid)

    def wrapped(*refs):
        pf, refs = refs[:npf], refs[npf:]
        ins, jin = refs[:n_in], refs[n_in:n_in + ji]
        o0 = n_in + ji
        outs, jout = refs[o0:o0 + n_out], refs[o0 + n_out:o0 + n_out + jo]
        s0 = o0 + n_out + jo
        scr, jsem = refs[s0:s0 + n_scr], refs[s0 + n_scr:]
        step = functools.reduce(lambda acc, ag: acc * ag[1] + pl.program_id(ag[0]), enumerate(grid), 0)
        if ji or jo:
            @pl.when(step == 0)
            def _():
                job.start(jin, jout, jsem)

        body(*pf, *ins, *outs, *scr)

        if ji or jo:
            @pl.when(step == steps // 2)
            def _():
                job.mid(jin, jout, jsem)

            @pl.when(step == steps - 1)
            def _():
                job.finish(jin, jout, jsem)

    io = {npf + a: b for a, b in dict(aliases or {}).items()}
    io.update({npf + n_in + a: n_out + b for a, b in job.aliases.items()})
    kw = dict(in_specs=list(in_specs) + [ANY] * ji, out_specs=list(out_specs) + [ANY] * jo,
              scratch_shapes=list(scratch_shapes) + job.sems)
    if npf:
        kw = dict(grid_spec=pltpu.PrefetchScalarGridSpec(num_scalar_prefetch=1, grid=grid, **kw))
    else:
        kw["grid"] = grid
    res = pl.pallas_call(
        wrapped, name=name, out_shape=list(out_shape) + job.out_shapes, input_output_aliases=io,
        compiler_params=_cparams(has_side_effects=bool(ji or jo)), **kw,
    )(*(() if prefetch is None else (prefetch,)), *args, *job.ins)
    return list(res[:n_out]), list(res[n_out:])


def _run_job(job, name):
    ji, jo = len(job.ins), len(job.out_shapes)

    def body(*refs):
        jin, jout, jsem = refs[:ji], refs[ji:ji + jo], refs[ji + jo:]
        job.start(jin, jout, jsem)
        job.finish(jin, jout, jsem)

    return list(pl.pallas_call(
        body, name=name, in_specs=[ANY] * ji, out_specs=[ANY] * jo, out_shape=job.out_shapes,
        scratch_shapes=job.sems, input_output_aliases=job.aliases,
        compiler_params=pltpu.CompilerParams(has_side_effects=True))(*job.ins))


def _cast_shard(name, place, w):
    rows, cols = w.shape
    tr = 352 if rows % 352 == 0 else 256

    def body(pc_ref, w_ref, o_ref):
        del pc_ref
        o_ref[...] = w_ref[...].astype(BF16)

    return pl.pallas_call(
        body, name=name,
        grid_spec=pltpu.PrefetchScalarGridSpec(
            num_scalar_prefetch=1, grid=(rows // tr,),
            in_specs=[pl.BlockSpec((tr, cols), lambda i, pc: (i, 0))],
            out_specs=pl.BlockSpec((None, tr, cols), lambda i, pc: (pc[0], i, 0))),
        out_shape=SDS((NCHIP, rows, cols), BF16),
        compiler_params=_cparams(),
    )(place, w)


def _sibling_copy(ref, send_sem, recv_sem):
    x, y, c, _ = _place()
    return pltpu.make_async_remote_copy(src_ref=ref, dst_ref=ref, send_sem=send_sem, recv_sem=recv_sem,
                                        device_id=(x, y, 1 - c), device_id_type=MESH)


def _half_rows(arr, slot, core):
    half = arr.shape[1] // 2
    return arr.at[slot, pl.ds(pl.multiple_of(core * half, 16), half)]


def _quarter_rows(arr, slot, core, q):
    quarter = arr.shape[1] // 4
    return arr.at[slot, pl.ds(pl.multiple_of((2 * core + q) * quarter, 16), quarter)]


def _chip_copy(ref, dist, send_sem, recv_sem):
    x, y, c, _ = _place()
    cx, cy = _chip_at(x, y, dist)
    return pltpu.make_async_remote_copy(src_ref=ref, dst_ref=ref, send_sem=send_sem, recv_sem=recv_sem,
                                        device_id=(cx, cy, c), device_id_type=MESH)


def _gather_sems(n):
    dma = pltpu.SemaphoreType.DMA
    return [dma((n, 2))] * 4 + [dma((n, 4))] * 2


def _gather_start(arrs, sems):
    dsend, drecv = sems[0], sems[1]
    _, _, c, j = _place()
    for w, arr in enumerate(arrs):
        for dist in (1, 2):
            _chip_copy(_half_rows(arr, j, c), dist, dsend.at[w, dist - 1], drecv.at[w, dist - 1]).start()


def _gather_land(arrs, sems, dist, first=0):
    dsend, drecv, rsend, rrecv, fsend, frecv = sems
    _, _, c, j = _place()
    if dist < 3:
        other = 3 - dist
        for w, arr in enumerate(arrs, first):
            landed = _half_rows(arr, j ^ dist, c)
            _chip_copy(landed, dist, dsend.at[w, dist - 1], drecv.at[w, dist - 1]).wait_recv()
            relay = _quarter_rows(arr, j ^ dist, c, other - 1)
            _chip_copy(relay, other, rsend.at[w, other - 1], rrecv.at[w, other - 1]).start()
            _sibling_copy(landed, fsend.at[w, dist - 1], frecv.at[w, dist - 1]).start()
        for w, arr in enumerate(arrs, first):
            theirs = _half_rows(arr, j ^ dist, 1 - c)
            _sibling_copy(theirs, fsend.at[w, dist - 1], frecv.at[w, dist - 1]).wait_recv()
    else:
        for w, arr in enumerate(arrs, first):
            for via in (1, 2):
                piece = _quarter_rows(arr, j ^ 3, c, via - 1)
                _chip_copy(piece, via, rsend.at[w, via - 1], rrecv.at[w, via - 1]).wait_recv()
                _sibling_copy(piece, fsend.at[w, 1 + via], frecv.at[w, 1 + via]).start()
        for w, arr in enumerate(arrs, first):
            for via in (1, 2):
                theirs = _quarter_rows(arr, j ^ 3, 1 - c, via - 1)
                _sibling_copy(theirs, fsend.at[w, 1 + via], frecv.at[w, 1 + via]).wait_recv()


def _gather_drain(arrs, sems):
    dsend, drecv, rsend, rrecv, fsend, frecv = sems
    _, _, c, j = _place()
    for w, arr in enumerate(arrs):
        for dist in (1, 2):
            other = 3 - dist
            _chip_copy(_half_rows(arr, j, c), dist, dsend.at[w, dist - 1], drecv.at[w, dist - 1]).wait_send()
            _chip_copy(_quarter_rows(arr, j ^ dist, c, other - 1), other,
                       rsend.at[w, other - 1], rrecv.at[w, other - 1]).wait_send()
            _sibling_copy(_half_rows(arr, j ^ dist, c), fsend.at[w, dist - 1], frecv.at[w, dist - 1]).wait_send()
            _sibling_copy(_quarter_rows(arr, j ^ 3, c, dist - 1),
                          fsend.at[w, 1 + dist], frecv.at[w, 1 + dist]).wait_send()


def _gather_neighbours(arrs, sems):
    _gather_land(arrs, sems, 1)
    _gather_land(arrs, sems, 2)


def _gather_finish(arrs, sems):
    _gather_land(arrs, sems, 3)
    _gather_drain(arrs, sems)


def _gather_job(arrs):
    n = len(arrs)
    return _Job(arrs, [SDS(a.shape, a.dtype) for a in arrs], _gather_sems(n),
                lambda ins, outs, sems: _gather_start(outs, sems),
                lambda ins, outs, sems: _gather_finish(outs, sems), {k: k for k in range(n)},
                mid=lambda ins, outs, sems: _gather_neighbours(outs, sems))


def _exchange_job(arrs, out_shapes, n, copies):
    def start(ins, outs, sems):
        for cp in copies(ins, outs, sems[0], sems[1]):
            cp.start()

    def finish(ins, outs, sems):
        for cp in copies(ins, outs, sems[0], sems[1]):
            cp.wait()

    return _Job(arrs, out_shapes, [pltpu.SemaphoreType.DMA((n,))] * 2, start, finish)


def _pair_exchange_job(grads):
    def copies(ins, outs, send_sem, recv_sem):
        x, y, c, _ = _place()
        res = []
        for w in range(len(grads)):
            half = ins[w].shape[1] // 2
            theirs = pl.ds(pl.multiple_of((1 - c) * half, 8), half)
            res.append(pltpu.make_async_remote_copy(
                src_ref=ins[w].at[:, theirs, :], dst_ref=outs[w], send_sem=send_sem.at[w],
                recv_sem=recv_sem.at[w], device_id=(x, y, 1 - c), device_id_type=MESH))
        return res

    return _exchange_job(grads, [SDS((NCHIP, g.shape[1] // 2, g.shape[2]), F32) for g in grads],
                         len(grads), copies)


def _row_tile(rows, cols):
    tr = rows
    while tr * cols * 4 > ELEMENTWISE_BLOCK_BYTES and tr % 32 == 0:
        tr //= 2
    return tr


def _pair_sum(name, place, g, sib):
    half, cols = sib.shape[1], sib.shape[2]
    tr = _row_tile(half, cols)
    nt = half // tr
    mine = nt if g.shape[1] == 2 * half else 0

    def body(pc_ref, g_ref, s_ref, own_ref, out_ref):
        del pc_ref
        v = g_ref[...] + s_ref[...]
        out_ref[...] = v.astype(BF16)

        @pl.when(pl.program_id(1) == 0)
        def _():
            own_ref[...] = v

    return pl.pallas_call(
        body, name=name,
        grid_spec=pltpu.PrefetchScalarGridSpec(
            num_scalar_prefetch=1, grid=(nt, NCHIP),
            in_specs=[pl.BlockSpec((None, tr, cols), lambda i, s, pc: (pc[0] ^ s, pc[1] * mine + i, 0)),
                      pl.BlockSpec((None, tr, cols), lambda i, s, pc: (pc[0] ^ s, i, 0))],
            out_specs=[pl.BlockSpec((tr, cols), lambda i, s, pc: (i, 0)),
                       pl.BlockSpec((None, tr, cols), lambda i, s, pc: (s, i, 0))]),
        out_shape=[SDS((half, cols), F32), SDS((NCHIP, half, cols), BF16)],
        compiler_params=_cparams(),
    )(place, g, sib)


def _chip_exchange_job(parts):
    def copies(ins, outs, send_sem, recv_sem):
        x, y, c, _ = _place()
        res = []
        for w in range(len(parts)):
            for s in range(1, NCHIP):
                cx, cy = _chip_at(x, y, s)
                k = w * (NCHIP - 1) + s - 1
                res.append(pltpu.make_async_remote_copy(
                    src_ref=ins[w].at[s], dst_ref=outs[w].at[s - 1], send_sem=send_sem.at[k],
                    recv_sem=recv_sem.at[k], device_id=(cx, cy, c), device_id_type=MESH))
        return res

    return _exchange_job(parts, [SDS((NCHIP - 1,) + p.shape[1:], BF16) for p in parts],
                         len(parts) * (NCHIP - 1), copies)


def _chip_sum(name, own, rem):
    half, cols = own.shape
    tr = _row_tile(half, cols)

    def body(own_ref, rem_ref, out_ref):
        out_ref[...] = ((own_ref[...] + rem_ref[0].astype(F32)) + rem_ref[1].astype(F32)) + rem_ref[2].astype(F32)

    return pl.pallas_call(
        body, name=name, grid=(half // tr,),
        in_specs=[pl.BlockSpec((tr, cols), lambda i: (i, 0)),
                  pl.BlockSpec((NCHIP - 1, tr, cols), lambda i: (0, i, 0))],
        out_specs=pl.BlockSpec((tr, cols), lambda i: (i, 0)),
        out_shape=SDS((half, cols), F32),
        compiler_params=_cparams(),
    )(own, rem)


def _share_halves_job(halves):
    def copies(ins, outs, send_sem, recv_sem):
        x, y, c, _ = _place()
        return [pltpu.make_async_remote_copy(
            src_ref=ins[w], dst_ref=outs[w], send_sem=send_sem.at[w], recv_sem=recv_sem.at[w],
            device_id=(x, y, 1 - c), device_id_type=MESH) for w in range(len(halves))]

    return _exchange_job(halves, [SDS(h.shape, F32) for h in halves], len(halves), copies)


def _adamw_math(w, g, m, v):
    m = B1 * m + (1.0 - B1) * g
    v = B2 * v + (1.0 - B2) * (g * g)
    m_hat = m / (1.0 - B1 ** STEP)
    v_hat = v / (1.0 - B2 ** STEP)
    delta = -LR * (m_hat / (jnp.sqrt(v_hat) + AEPS) + WD * w)
    return delta, m, v


def _adamw(name, place, w, own, sib, m, v):
    rows, cols = w.shape
    by_cols = own.shape[0] == rows
    half, pc_cols = (rows, cols // 2) if by_cols else (rows // 2, cols)
    tr = _row_tile(half, pc_cols)
    nt = half // tr

    def body(pc_ref, w_ref, own_ref, sib_ref, m_ref, v_ref, g_ref, d_ref, mo_ref, vo_ref):
        g = jnp.where(pl.program_id(0) == pc_ref[1], own_ref[...], sib_ref[...])
        d, mn, vn = _adamw_math(w_ref[...], g, m_ref[...], v_ref[...])
        g_ref[...] = g
        d_ref[...] = d
        mo_ref[...] = mn
        vo_ref[...] = vn

    full = pl.BlockSpec((tr, pc_cols), (lambda h, i, pc: (i, h)) if by_cols else (lambda h, i, pc: (h * nt + i, 0)))
    part = pl.BlockSpec((tr, pc_cols), lambda h, i, pc: (i, 0))
    return pl.pallas_call(
        body, name=name,
        grid_spec=pltpu.PrefetchScalarGridSpec(
            num_scalar_prefetch=1, grid=(2, nt),
            in_specs=[full, part, part, full, full], out_specs=[full] * 4),
        out_shape=[SDS((rows, cols), F32)] * 4,
        compiler_params=_cparams(),
    )(place, w, own, sib, m, v)


def _small_allreduce_adamw(sp, w, m, v):
    shape = sp.shape

    def body(sp_ref, w_ref, m_ref, v_ref, g_ref, d_ref, mo_ref, vo_ref,
             sib_s, pair_s, chip_s, send_sem, recv_sem):
        x, y, c, j = _place()
        cp = pltpu.make_async_remote_copy(
            src_ref=sp_ref, dst_ref=sib_s, send_sem=send_sem.at[0], recv_sem=recv_sem.at[0],
            device_id=(x, y, 1 - c), device_id_type=MESH)
        cp.start()
        cp.wait()
        pair_s[...] = sp_ref[...] + sib_s[...]
        cps = []
        for s in range(1, NCHIP):
            cx, cy = _chip_at(x, y, s)
            cp = pltpu.make_async_remote_copy(
                src_ref=pair_s, dst_ref=chip_s.at[s], send_sem=send_sem.at[s], recv_sem=recv_sem.at[s],
                device_id=(cx, cy, c), device_id_type=MESH)
            cp.start()
            cps.append(cp)
        chip_s[0] = pair_s[...]
        for cp in cps:
            cp.wait()
        tot = chip_s[j]
        for k in range(1, NCHIP):
            tot = tot + chip_s[k ^ j]
        g_ref[...] = tot
        d, mn, vn = _adamw_math(w_ref[...], tot, m_ref[...], v_ref[...])
        d_ref[...] = d
        mo_ref[...] = mn
        vo_ref[...] = vn

    vm = pl.BlockSpec(memory_space=pltpu.VMEM)
    return pl.pallas_call(
        body, name="small_allreduce_adamw",
        in_specs=[vm] * 4, out_specs=[vm] * 4, out_shape=[SDS(shape, F32)] * 4,
        scratch_shapes=[pltpu.VMEM(shape, F32), pltpu.VMEM(shape, F32), pltpu.VMEM((NCHIP,) + shape, F32),
                        pltpu.SemaphoreType.DMA((NCHIP,)), pltpu.SemaphoreType.DMA((NCHIP,))],
        compiler_params=pltpu.CompilerParams(has_side_effects=True),
    )(sp, w, m, v)


def _pack_small(first, mix, ln_g, ln_b, b_s, lbt, hn, ffn, fin, w_s):
    rows = [first.reshape(1, D), mix.reshape(1, D), ln_g.reshape(1, D), ln_b.reshape(1, D),
            b_s.reshape(1, D), lbt.reshape(2, D), hn.reshape(1, D), ffn.reshape(1, D), fin.reshape(1, D),
            jnp.zeros((6, D), F32)]
    return jnp.concatenate(rows + [w_s.reshape(NG, GCH, GCH).transpose(1, 0, 2).reshape(GCH, D)], axis=0)


def _unpack_small(p):
    w_s = p[16:].reshape(GCH, NG, GCH).transpose(1, 0, 2).reshape(1, NG, GCH, GCH)
    return dict(norm_mix_g=p[1:2], gmlp_ln_g=p[2:3], gmlp_ln_b=p[3:4], gmlp_b_s=p[4].reshape(1, NG, GCH),
                hgrn_lb_table=p[5:7], hgrn_norm_g=p[7:8], norm_ffn_g=p[8:9], norm_final_g=p[9],
                gmlp_w_s=w_s)


SMALL = ("norm_mix_g", "gmlp_ln_g", "gmlp_ln_b", "gmlp_w_s", "gmlp_b_s", "hgrn_lb_table", "hgrn_norm_g",
         "norm_ffn_g", "norm_final_g")
BIG = ("w_in", "w_gate_up", "w_branch_a", "w_branch_b", "w_out", "w_down")
ORDER = ("norm_mix_g", "w_in", "gmlp_ln_g", "gmlp_ln_b", "gmlp_w_s", "gmlp_b_s", "hgrn_lb_table",
         "hgrn_norm_g", "w_branch_a", "w_branch_b", "w_out", "norm_ffn_g", "w_gate_up", "w_down",
         "norm_final_g")


def kernel(x, norm_mix_g, w_in, gmlp_ln_g, gmlp_ln_b, gmlp_w_s, gmlp_b_s, hgrn_lb_table, hgrn_norm_g, w_branch_a, w_branch_b, w_out, norm_ffn_g, w_gate_up, w_down, norm_final_g, loss_target, m_norm_mix_g, m_w_in, m_gmlp_ln_g, m_gmlp_ln_b, m_gmlp_w_s, m_gmlp_b_s, m_hgrn_lb_table, m_hgrn_norm_g, m_w_branch_a, m_w_branch_b, m_w_out, m_norm_ffn_g, m_w_gate_up, m_w_down, m_norm_final_g, v_norm_mix_g, v_w_in, v_gmlp_ln_g, v_gmlp_ln_b, v_gmlp_w_s, v_gmlp_b_s, v_hgrn_lb_table, v_hgrn_norm_g, v_w_branch_a, v_w_branch_b, v_w_out, v_norm_ffn_g, v_w_gate_up, v_w_down, v_norm_final_g):
    args = dict(locals())
    T = x.shape[1]
    xs = x.reshape(T, D)
    target = loss_target.reshape(T, D)
    big = {n: args[n].reshape(args[n].shape[1:]) for n in BIG}
    big_m = {n: args["m_" + n].reshape(args[n].shape[1:]) for n in BIG}
    big_v = {n: args["v_" + n].reshape(args[n].shape[1:]) for n in BIG}

    x_i, y_i, c_i = lax.axis_index("x"), lax.axis_index("y"), lax.axis_index("c")
    place = jnp.stack([2 * x_i + y_i, c_i]).astype(jnp.int32)
    cast = {n: _cast_shard("cast_" + n, place, big[n]) for n in BIG}
    tril = jnp.tril(jnp.ones((GCH, GCH), bool))
    wm = jnp.where(tril, gmlp_w_s[0], 0.0).astype(BF16)
    wm_t = jnp.swapaxes(wm, 1, 2)
    b_t = gmlp_b_s[0].T

    (proj, hb), w_in4, (w_a4,) = _proj_fwd(place, xs, norm_mix_g, cast["w_in"], [cast["w_branch_a"]])
    (ab, y_a), (w_b4, w_out4) = _gmlp_fwd(
        proj, gmlp_ln_g, gmlp_ln_b, wm, b_t, w_a4.reshape(D, D),
        job=_gather_job([cast["w_branch_b"], cast["w_out"]]))
    (o_raw, obb, st_before), (w_gu4,) = _hgrn_fwd(
        proj, hgrn_lb_table, hgrn_norm_g, job=_gather_job([cast["w_gate_up"]]))
    w_a, w_b, w_o = (w.reshape(D, D) for w in (w_a4, w_b4, w_out4))
    (y_b, mgb, x1), (w_down4,) = _merge_fwd(xs, y_a, obb, proj, w_b, w_o, job=_gather_job([cast["w_down"]]))
    w_dn = w_down4.reshape(FF, D)
    act, dx2b, h2b, dgu4, dx1, dx1b, acc_ffn = _ffn_fwd_bwd(
        x1, target, norm_ffn_g, norm_final_g.reshape(1, D), w_gu4, w_dn)

    grads, owns, parts, halves, sibh = {}, {}, {}, {}, {}

    def pair_sums(names, sibs):
        for n, s in zip(names, sibs):
            owns[n], parts[n] = _pair_sum("rs_pair_sum_" + n, place, grads[n], s)

    def chip_sums(names, got):
        for n, r in zip(names, got):
            halves[n] = _chip_sum("rs_chip_sum_" + n, owns[n], r)

    ffn, mix = ("w_gate_up", "w_down"), ("w_branch_a", "w_branch_b", "w_out")
    grads["w_gate_up"], _ = _dw_gate_up(h2b, dgu4)
    grads["w_down"], _ = _dw_down(act, dx2b)
    (dya, dyb, da, dob, dproj), got = _merge_bwd(
        dx1b, y_a, y_b, proj, w_o, w_a, w_b, job=_pair_exchange_job([grads[n] for n in ffn]))
    pair_sums(ffn, got)
    grads["w_branch_a"], _ = _dw_square("dw_branch_a", ab, dya)
    grads["w_branch_b"], _ = _dw_square("dw_branch_b", obb, dyb)
    grads["w_out"], _ = _dw_square("dw_out", mgb, dx1b)
    (dproj, acc_hgrn), got = _hgrn_bwd(
        dproj, dob, o_raw, proj, st_before, hgrn_lb_table, hgrn_norm_g,
        job=_join_jobs(_chip_exchange_job([parts[n] for n in ffn]), _pair_exchange_job([grads[n] for n in mix])))
    chip_sums(ffn, got[:2])
    pair_sums(mix, got[2:])
    dproj, acc_ln, dws, dmix = _gmlp_bwd(dproj, da, proj, gmlp_ln_g, gmlp_ln_b, wm, wm_t, b_t)
    for_sibling, got = _dw_in_half(
        "dw_in_sibling_half", place, hb, dproj, False,
        job=_join_jobs(_share_halves_job([halves[n] for n in ffn]), _chip_exchange_job([parts[n] for n in mix])))
    sibh.update(zip(ffn, got[:2]))
    chip_sums(mix, got[2:])
    grads["w_in"], got = _dw_in_half(
        "dw_in_own_half", place, hb, dproj, True, job=_share_halves_job([for_sibling]))
    pair_sums(("w_in",), got)
    (grad_x, acc_mix), got = _proj_bwd(
        dproj, w_in4, xs, dx1, norm_mix_g,
        job=_join_jobs(_chip_exchange_job([parts["w_in"]]), _share_halves_job([halves[n] for n in mix])))
    chip_sums(("w_in",), got[:1])
    sibh.update(zip(mix, got[1:]))
    (sibh["w_in"],) = _run_job(_share_halves_job([halves["w_in"]]), "rs_share_halves_w_in")
    out = {}
    for n in BIG:
        g, d, mn, vn = _adamw("adamw_" + n, place, big[n], halves[n], sibh[n], big_m[n], big_v[n])
        shp = args[n].shape
        out[n] = (g.reshape(shp), d.reshape(shp), mn.reshape(shp), vn.reshape(shp))

    lbv = jax.nn.sigmoid(hgrn_lb_table[0] - hgrn_lb_table[1])
    d_t0 = jnp.sum(acc_hgrn[0], axis=0) * lbv * (1.0 - lbv)
    loss_row = jnp.zeros((D,), F32).at[0].set(jnp.sum(acc_ffn[0]))
    dws_m = jnp.where(tril[:, None, :], dws.reshape(GCH, NG, GCH), 0.0).transpose(1, 0, 2)
    db_s = jnp.sum(dmix.reshape(GCH, NG, GCH), axis=-1).T
    sp = _pack_small(loss_row, jnp.sum(acc_mix, 0), jnp.sum(acc_ln[0], 0), jnp.sum(acc_ln[1], 0), db_s,
                     jnp.stack([d_t0, -d_t0]), jnp.sum(acc_hgrn[1], 0), jnp.sum(acc_ffn[2], 0),
                     jnp.sum(acc_ffn[1], 0), dws_m)
    zero = jnp.zeros((D,), F32)

    def pack(prefix):
        a = lambda n: args[prefix + n]
        return _pack_small(zero, a("norm_mix_g"), a("gmlp_ln_g"), a("gmlp_ln_b"), a("gmlp_b_s"),
                           a("hgrn_lb_table"), a("hgrn_norm_g"), a("norm_ffn_g"), a("norm_final_g"),
                           a("gmlp_w_s"))

    packed = _small_allreduce_adamw(sp, pack(""), pack("m_"), pack("v_"))
    loss = packed[0][0, 0]
    small = [_unpack_small(p) for p in packed]
    for n in SMALL:
        out[n] = tuple(s[n] for s in small)
    return (loss, grad_x.reshape(x.shape), *[out[n][0] for n in ORDER], *[out[n][1] for n in ORDER],
            *[out[n][2] for n in ORDER], *[out[n][3] for n in ORDER])
```

```python
import functools
import math

import jax
import jax.numpy as jnp
from jax import lax
from jax.experimental import pallas as pl
from jax.experimental.pallas import tpu as pltpu

F32 = jnp.float32
BF16 = jnp.bfloat16
SDS = jax.ShapeDtypeStruct
MESH = pl.DeviceIdType.MESH
ANY = pl.BlockSpec(memory_space=pl.ANY)

D = 1024
NIN = 8
NG = 8
GCH = 128
NH = 8
HD = 128
HCH = 64
HGRN_HB = 4
HW = HGRN_HB * HD
DW_TOKENS = 2048
ELEMENTWISE_BLOCK_BYTES = 2 * 1024 * 1024
MM_COLS = 256
FF = 2816
FFS = 1408
NCHIP = 4
EPS = 1e-6
QSCALE = HD ** -0.5
GELU_C0 = math.sqrt(2.0 / math.pi)
GELU_C1 = 0.044715
LR, B1, B2, AEPS, WD, STEP = 0.001, 0.9, 0.999, 1e-08, 0.01, 10
VMEM_LIMIT_V7X = 56 * 1024 * 1024
SP_ROWS = 144


def _cparams(**kw):
    return pltpu.CompilerParams(vmem_limit_bytes=VMEM_LIMIT_V7X, **kw)


def _mm(a, b):
    return jnp.dot(a, b, preferred_element_type=F32)


def _mm_nt(a, b):
    return lax.dot_general(a, b, (((1,), (1,)), ((), ())), preferred_element_type=F32)


def _mm_tn(a, b):
    return lax.dot_general(a, b, (((0,), (0,)), ((), ())), preferred_element_type=F32)


def _rows8(x):
    r, c = x.shape
    return jnp.sum(x.reshape(r // 8, 8, c), axis=0)


def _mean(x):
    return jnp.mean(x, axis=-1, keepdims=True)


def _sigmoid(x):
    return 1.0 / (1.0 + jnp.exp(-x))


def _gelu(x):
    t = jnp.tanh(GELU_C0 * (x + GELU_C1 * x * x * x))
    return 0.5 * x * (1.0 + t), t


def _gelu_grad(x, t):
    return 0.5 * (1.0 + t) + 0.5 * x * (1.0 - t * t) * (GELU_C0 * (1.0 + 3.0 * GELU_C1 * x * x))


def _component_of(group):
    return jnp.where(group < 6, (group + 4) % 6, group)


def _proj_fwd(place, x, g_mix, w_in4, later):
    T = x.shape[0]
    tm = min(1024, T)
    ni = T // tm
    n = len(later)

    def body(pc_ref, x_ref, g_ref, *rest):
        proj_ref, h_ref, w_all = rest[1 + n:4 + n]
        gathered = rest[4 + n:4 + 2 * n]
        hs, wbuf, wsem, obuf, osem = rest[4 + 2 * n:9 + 2 * n]
        w_sems, later_sems = rest[9 + 2 * n:15 + 2 * n], rest[15 + 2 * n:]
        jp, i = pl.program_id(0), pl.program_id(1)
        w_cols = [w_all.at[:, :, pl.ds(k * D, D)] for k in range(2)]

        def w_copy(blk):
            cols = pl.ds(pl.multiple_of((blk % 2) * D, 128), D)
            return pltpu.make_async_copy(w_all.at[pc_ref[0] ^ (blk // 2), :, cols], wbuf.at[blk % 2],
                                         wsem.at[blk % 2])

        @pl.when((jp == 0) & (i == 0))
        def _():
            _gather_start(w_cols, w_sems)
            _gather_start(gathered, later_sems)
            w_copy(jp).start()

        @pl.when(i == 0)
        def _():
            w_copy(jp).wait()

        @pl.when(jp == 0)
        def _():
            xv = x_ref[...]
            r = lax.rsqrt(_mean(xv * xv) + EPS)
            hb = (xv * r * g_ref[...]).astype(BF16)
            hs[i] = hb
            h_ref[...] = hb

        step = jp * ni + i
        slot = step % 2

        def o_copies(slot_):
            comp = 2 * (pc_ref[0] ^ (jp // 2)) + jp % 2
            return [pltpu.make_async_copy(
                obuf.at[slot_, pl.ds(p * (tm // 2), tm // 2)],
                proj_ref.at[comp, pl.ds(pl.multiple_of(i * tm + p * (tm // 2), 8), tm // 2)],
                osem.at[slot_, p]) for p in range(2)]

        @pl.when(step >= 2)
        def _():
            for cp in o_copies(slot):
                cp.wait()

        hv = hs[i]
        for k in range(D // MM_COLS):
            cols = slice(MM_COLS * k, MM_COLS * (k + 1))
            obuf[slot, :, cols] = _mm(hv, wbuf[jp % 2, :, cols])
        for cp in o_copies(slot):
            cp.start()

        @pl.when(step == NIN * ni - 1)
        def _():
            for cp in o_copies(1 - slot) + o_copies(slot):
                cp.wait()

        for nxt in range(1, NIN):
            @pl.when((jp == nxt - 1) & (i == ni - 1))
            def _():
                if nxt >= 2:
                    _gather_land([w_cols[nxt % 2]], w_sems, nxt // 2, first=nxt % 2)
                if nxt == 4:
                    _gather_neighbours(gathered, later_sems)
                w_copy(jp + 1).start()

        @pl.when((jp == NIN - 1) & (i == ni - 1))
        def _():
            _gather_drain(w_cols, w_sems)
            _gather_finish(gathered, later_sems)

    tile = lambda jp, i, pc: (jnp.where(jp == 0, i, ni - 1), 0)
    res = pl.pallas_call(
        body, name="proj_fwd",
        grid_spec=pltpu.PrefetchScalarGridSpec(
            num_scalar_prefetch=1, grid=(NIN, ni),
            in_specs=[pl.BlockSpec((tm, D), tile), pl.BlockSpec((1, D), lambda jp, i, pc: (0, 0))] + [ANY] * (1 + n),
            out_specs=[ANY, pl.BlockSpec((tm, D), tile)] + [ANY] * (1 + n),
            scratch_shapes=[pltpu.VMEM((ni, tm, D), BF16), pltpu.VMEM((2, D, D), BF16),
                            pltpu.SemaphoreType.DMA((2,)), pltpu.VMEM((2, tm, D), F32),
                            pltpu.SemaphoreType.DMA((2, 2))] + _gather_sems(2) + _gather_sems(n)),
        out_shape=[SDS((NIN, T, D), F32), SDS((T, D), BF16), SDS(w_in4.shape, BF16)]
        + [SDS(a.shape, a.dtype) for a in later],
        input_output_aliases={3 + k: 2 + k for k in range(1 + n)},
        compiler_params=_cparams(has_side_effects=True),
    )(place, x, g_mix, w_in4, *later)
    return res[:2], res[2], res[3:]


def _layer_norm_stats(gv):
    mu = _mean(gv)
    xc = gv - mu
    rs = lax.rsqrt(_mean(xc * xc) + EPS)
    return xc * rs, rs


def _gmlp_fwd(proj, ln_g, ln_b, wm, b_t, w_a, job=None):
    T = proj.shape[1]
    tm = min(256, T)

    def body(u_ref, v_ref, lg_ref, lb_ref, wm_ref, bt_ref, wa_ref, a_ref, ya_ref, a_s):
        gu, _ = _gelu(u_ref[...])
        gv, _ = _gelu(v_ref[...])
        vhat, _ = _layer_norm_stats(gv)
        vnb = (vhat * lg_ref[...] + lb_ref[...]).astype(BF16)
        for ch in range(tm // GCH):
            rows = slice(GCH * ch, GCH * (ch + 1))
            for g in range(NG):
                cols = slice(128 * g, 128 * (g + 1))
                mixed = _mm(wm_ref[g], vnb[rows, cols]) + bt_ref[:, g:g + 1]
                a_s[rows, cols] = gu[rows, cols] * mixed
        ab = a_s[...].astype(BF16)
        a_ref[...] = ab
        ya_ref[...] = _mm(ab, wa_ref[...])

    row = lambda i: (0, 0)
    return _call(
        body, name="gmlp_fwd", grid=(T // tm,), job=job, args=(proj, proj, ln_g, ln_b, wm, b_t, w_a),
        in_specs=[pl.BlockSpec((None, tm, D), lambda i: (0, i, 0)), pl.BlockSpec((None, tm, D), lambda i: (1, i, 0)),
                  pl.BlockSpec((1, D), row), pl.BlockSpec((1, D), row),
                  pl.BlockSpec((NG, GCH, GCH), lambda i: (0, 0, 0)), pl.BlockSpec((GCH, NG), row),
                  pl.BlockSpec((D, D), row)],
        out_specs=[pl.BlockSpec((tm, D), lambda i: (i, 0)), pl.BlockSpec((tm, D), lambda i: (i, 0))],
        out_shape=[SDS((T, D), BF16), SDS((T, D), F32)],
        scratch_shapes=[pltpu.VMEM((tm, D), F32)])


def _cumsum64(x, row):
    for s in (1, 2, 4, 8, 16, 32):
        x = x + jnp.where(row >= s, pltpu.roll(x, s, 0), 0.0)
    return x


def _revcumsum64(x, row):
    n = x.shape[0]
    for s in (1, 2, 4, 8, 16, 32):
        x = x + jnp.where(row < HCH - s, pltpu.roll(x, n - s, 0), 0.0)
    return x


def _head_mean(x):
    parts = [jnp.broadcast_to(_mean(x[:, HD * h:HD * (h + 1)]), (x.shape[0], HD)) for h in range(x.shape[1] // HD)]
    return jnp.concatenate(parts, axis=1)


def _seg_sum(x):
    n, c = x.shape
    s = jnp.sum(x.reshape(n // HCH, HCH, c), axis=1, keepdims=True)
    return jnp.broadcast_to(s, (n // HCH, HCH, c)).reshape(n, c)


def _hgrn_gates(fl, lbv, row):
    s = _sigmoid(fl)
    f = lbv + (1.0 - lbv) * s
    a = _cumsum64(jnp.log(f), row)
    a_mid = _seg_sum(jnp.where(row == HCH // 2 - 1, a, 0.0))
    a_last = _seg_sum(jnp.where(row == HCH - 1, a, 0.0))
    return s, f, a, a_mid, a_last


def _hgrn_fwd(proj, lb_table, norm_g, job=None):
    T = proj.shape[1]
    tb = min(512, T)
    nc = tb // HCH

    def body(q_ref, fl_ref, v_ref, g_ref, lbt_ref, gn_ref, o_ref, ob_ref, stb_ref, st_s, o_s):
        @pl.when(pl.program_id(1) == 0)
        def _():
            st_s[...] = jnp.zeros_like(st_s)

        row = lax.broadcasted_iota(jnp.int32, (tb, HW), 0) & (HCH - 1)
        lbv = _sigmoid(lbt_ref[0:1, :] - lbt_ref[1:2, :])
        _, f, a, a_mid, a_last = _hgrn_gates(fl_ref[...], lbv, row)
        k = 1.0 - f
        qs = q_ref[...] * QSCALE
        q_in = (qs * jnp.exp(a - a_mid)).astype(BF16)
        k_in = (k * jnp.exp(a_mid - a)).astype(BF16)
        q_a = (qs * jnp.exp(a)).astype(BF16)
        k_d = (k * jnp.exp(a_last - a)).astype(BF16)
        dec = jnp.exp(a_last)
        vb = v_ref[...].astype(BF16)
        tri = (lax.broadcasted_iota(jnp.int32, (HCH, HCH), 0)
               >= lax.broadcasted_iota(jnp.int32, (HCH, HCH), 1))
        for c in range(nc):
            sl = slice(HCH * c, HCH * (c + 1))
            for hh in range(HGRN_HB):
                hs = slice(HD * hh, HD * (hh + 1))
                st = st_s[hh]
                stb_ref[hh, c] = st
                sc = jnp.where(tri, _mm_nt(q_in[sl, hs], k_in[sl, hs]), 0.0)
                o_s[sl, hs] = _mm(sc.astype(BF16), vb[sl, hs]) + _mm_nt(q_a[sl, hs], st.astype(BF16))
                d64 = dec[sl, hs]
                st_s[hh] = st * jnp.concatenate([d64, d64], axis=0) + _mm_tn(vb[sl, hs], k_d[sl, hs])
        o = o_s[...]
        r = lax.rsqrt(_head_mean(o * o) + EPS)
        g = g_ref[...]
        o_ref[...] = o
        ob_ref[...] = (o * r * gn_ref[...] * (g * _sigmoid(g))).astype(BF16)

    def col(off):
        return pl.BlockSpec((None, tb, HW), lambda h, cb: (off, cb, h))

    return _call(
        body, name="hgrn_fwd", grid=(NH // HGRN_HB, T // tb), job=job,
        args=(proj, proj, proj, proj, lb_table, norm_g),
        in_specs=[col(2), col(3), col(4), col(5),
                  pl.BlockSpec((2, HW), lambda h, cb: (0, h)), pl.BlockSpec((1, HW), lambda h, cb: (0, h))],
        out_specs=[pl.BlockSpec((tb, HW), lambda h, cb: (cb, h)), pl.BlockSpec((tb, HW), lambda h, cb: (cb, h)),
                   pl.BlockSpec((HGRN_HB, nc, HD, HD), lambda h, cb: (h, cb, 0, 0))],
        out_shape=[SDS((T, D), F32), SDS((T, D), BF16), SDS((NH, T // HCH, HD, HD), F32)],
        scratch_shapes=[pltpu.VMEM((HGRN_HB, HD, HD), F32), pltpu.VMEM((tb, HW), F32)])


def _merge_fwd(x, y_a, ob, proj, w_b, w_out, job=None):
    T = x.shape[0]
    tm = min(512, T)

    def body(x_ref, ya_ref, ob_ref, ga_ref, gb_ref, wb_ref, wo_ref, yb_ref, mg_ref, x1_ref):
        yb = _mm(ob_ref[...], wb_ref[...])
        merged = (_sigmoid(ga_ref[...]) * ya_ref[...] + _sigmoid(gb_ref[...]) * yb).astype(BF16)
        yb_ref[...] = yb
        mg_ref[...] = merged
        x1_ref[...] = x_ref[...] + _mm(merged, wo_ref[...])

    t = lambda i: (i, 0)
    w = lambda i: (0, 0)
    return _call(
        body, name="merge_fwd", grid=(T // tm,), job=job, args=(x, y_a, ob, proj, proj, w_b, w_out),
        in_specs=[pl.BlockSpec((tm, D), t), pl.BlockSpec((tm, D), t), pl.BlockSpec((tm, D), t),
                  pl.BlockSpec((None, tm, D), lambda i: (6, i, 0)), pl.BlockSpec((None, tm, D), lambda i: (7, i, 0)),
                  pl.BlockSpec((D, D), w), pl.BlockSpec((D, D), w)],
        out_specs=[pl.BlockSpec((tm, D), t)] * 3,
        out_shape=[SDS((T, D), F32), SDS((T, D), BF16), SDS((T, D), F32)])


def _ffn_fwd_bwd(x1, target, g_ffn, g_fin, w_gu4, w_down):
    T = x1.shape[0]
    tm = min(256, T)
    inv_d = 1.0 / D

    def body(x1_ref, tg_ref, gf_ref, gn_ref, wgu_ref, wd_ref,
             act_ref, dx2b_ref, h2b_ref, dgu_ref, dx1_ref, dx1b_ref, acc_ref):
        @pl.when(pl.program_id(0) == 0)
        def _():
            acc_ref[...] = jnp.zeros_like(acc_ref)

        x1v = x1_ref[...]
        gf = gf_ref[...]
        gn = gn_ref[...]
        rr1 = lax.rsqrt(_mean(x1v * x1v) + EPS)
        x1n = x1v * rr1
        h2b = (x1n * gf).astype(BF16)
        h2b_ref[...] = h2b
        p = [_mm(h2b, wgu_ref[k]) for k in range(NCHIP)]
        sg = [_sigmoid(p[0]), _sigmoid(p[1])]
        si = [p[0] * sg[0], p[1] * sg[1]]
        x2 = x1v
        for k in range(2):
            actk = (si[k] * p[2 + k]).astype(BF16)
            act_ref[:, FFS * k:FFS * (k + 1)] = actk
            x2 = x2 + _mm(actk, wd_ref[FFS * k:FFS * (k + 1), :])
        rr2 = lax.rsqrt(_mean(x2 * x2) + EPS)
        x2n = x2 * rr2
        e = x2n * gn - tg_ref[...]
        acc_ref[0] += _rows8(e * e) * (0.5 * inv_d)
        dy = e * inv_d
        acc_ref[1] += _rows8(dy * x2n)
        dxn = dy * gn
        dx2 = rr2 * (dxn - x2n * _mean(dxn * x2n))
        dx2b = dx2.astype(BF16)
        dx2b_ref[...] = dx2b
        dh2 = None
        for k in range(2):
            dact = _mm_nt(dx2b, wd_ref[FFS * k:FFS * (k + 1), :])
            dgate = (dact * p[2 + k] * (sg[k] * (1.0 + p[k] * (1.0 - sg[k])))).astype(BF16)
            dup = (dact * si[k]).astype(BF16)
            dgu_ref[k] = dgate
            dgu_ref[2 + k] = dup
            part = _mm_nt(dgate, wgu_ref[k]) + _mm_nt(dup, wgu_ref[2 + k])
            dh2 = part if dh2 is None else dh2 + part
        acc_ref[2] += _rows8(dh2 * x1n)
        dxn1 = dh2 * gf
        dx1 = dx2 + rr1 * (dxn1 - x1n * _mean(dxn1 * x1n))
        dx1_ref[...] = dx1
        dx1b_ref[...] = dx1.astype(BF16)

    t = lambda i: (i, 0)
    w = lambda i: (0, 0)
    one = pl.Buffered(1)
    return pl.pallas_call(
        body, name="ffn_fwd_bwd", grid=(T // tm,),
        in_specs=[pl.BlockSpec((tm, D), t), pl.BlockSpec((tm, D), t),
                  pl.BlockSpec((1, D), w), pl.BlockSpec((1, D), w),
                  pl.BlockSpec((NCHIP, D, FFS), lambda i: (0, 0, 0), pipeline_mode=one),
                  pl.BlockSpec((FF, D), w, pipeline_mode=one)],
        out_specs=[pl.BlockSpec((tm, FF), t), pl.BlockSpec((tm, D), t), pl.BlockSpec((tm, D), t),
                   pl.BlockSpec((NCHIP, tm, FFS), lambda i: (0, i, 0)),
                   pl.BlockSpec((tm, D), t), pl.BlockSpec((tm, D), t),
                   pl.BlockSpec((3, 8, D), lambda i: (0, 0, 0))],
        out_shape=[SDS((T, FF), BF16), SDS((T, D), BF16), SDS((T, D), BF16),
                   SDS((NCHIP, T, FFS), BF16), SDS((T, D), F32), SDS((T, D), BF16),
                   SDS((3, 8, D), F32)],
        compiler_params=_cparams(),
    )(x1, target, g_ffn, g_fin, w_gu4, w_down)


def _merge_bwd(dx1b, y_a, y_b, proj, w_out, w_a, w_b, job=None):
    T = dx1b.shape[0]
    tm = min(512, T)

    def body(dx_ref, ya_ref, yb_ref, ga_ref, gb_ref, wo_ref, wa_ref, wb_ref,
             dya_ref, dyb_ref, da_ref, dob_ref, dp_ref):
        dm = _mm_nt(dx_ref[...], wo_ref[...])
        sa = _sigmoid(ga_ref[...])
        sb = _sigmoid(gb_ref[...])
        dya = (dm * sa).astype(BF16)
        dyb = (dm * sb).astype(BF16)
        dya_ref[...] = dya
        dyb_ref[...] = dyb
        dp_ref[0] = (dm * ya_ref[...] * sa * (1.0 - sa)).astype(BF16)
        dp_ref[1] = (dm * yb_ref[...] * sb * (1.0 - sb)).astype(BF16)
        da_ref[...] = _mm_nt(dya, wa_ref[...])
        dob_ref[...] = _mm_nt(dyb, wb_ref[...])

    t = lambda i: (i, 0)
    w = lambda i: (0, 0)
    return _call(
        body, name="merge_bwd", grid=(T // tm,),
        in_specs=[pl.BlockSpec((tm, D), t), pl.BlockSpec((tm, D), t), pl.BlockSpec((tm, D), t),
                  pl.BlockSpec((None, tm, D), lambda i: (6, i, 0)), pl.BlockSpec((None, tm, D), lambda i: (7, i, 0)),
                  pl.BlockSpec((D, D), w), pl.BlockSpec((D, D), w), pl.BlockSpec((D, D), w)],
        out_specs=[pl.BlockSpec((tm, D), t)] * 4 + [pl.BlockSpec((2, tm, D), lambda i: (3, i, 0))],
        out_shape=[SDS((T, D), BF16), SDS((T, D), BF16), SDS((T, D), F32), SDS((T, D), F32),
                   SDS((NIN, T, D), BF16)],
        args=(dx1b, y_a, y_b, proj, proj, w_out, w_a, w_b), job=job)


def _hgrn_bwd(dproj, dob, o_raw, proj, st_before, lb_table, norm_g, job=None):
    T = dob.shape[0]
    tb = min(512, T)
    nc = tb // HCH
    nb = T // tb

    def body(dp_in, dob_ref, o_ref, q_ref, fl_ref, v_ref, g_ref, stb_ref, lbt_ref, gn_ref,
             dp_ref, acc_ref, dst_s, dqin_s, dqa_s, dkin_s, dkd_s, dv_s, ddec_s):
        del dp_in

        @pl.when(pl.program_id(1) == 0)
        def _():
            dst_s[...] = jnp.zeros_like(dst_s)
            acc_ref[...] = jnp.zeros_like(acc_ref)

        row = lax.broadcasted_iota(jnp.int32, (tb, HW), 0) & (HCH - 1)
        gn = gn_ref[...]
        lbv = _sigmoid(lbt_ref[0:1, :] - lbt_ref[1:2, :])
        o = o_ref[...]
        r = lax.rsqrt(_head_mean(o * o) + EPS)
        on = o * r
        g = g_ref[...]
        sgm = _sigmoid(g)
        dob_v = dob_ref[...]
        dp_ref[3] = (dob_v * on * gn * (sgm * (1.0 + g * (1.0 - sgm)))).astype(BF16)
        do_n = dob_v * (g * sgm)
        acc_ref[1] += _rows8(do_n * on)
        dxn = do_n * gn
        do = (r * (dxn - on * _head_mean(dxn * on))).astype(BF16)
        s, f, a, a_mid, a_last = _hgrn_gates(fl_ref[...], lbv, row)
        k = 1.0 - f
        qs = q_ref[...] * QSCALE
        e_q = jnp.exp(a - a_mid)
        e_k = jnp.exp(a_mid - a)
        e_a = jnp.exp(a)
        e_l = jnp.exp(a_last - a)
        dec = jnp.exp(a_last)
        q_in = qs * e_q
        k_in = k * e_k
        q_a = qs * e_a
        k_d = k * e_l
        q_inb, k_inb, q_ab, k_db = (z.astype(BF16) for z in (q_in, k_in, q_a, k_d))
        vb = v_ref[...].astype(BF16)
        tri = (lax.broadcasted_iota(jnp.int32, (HCH, HCH), 0)
               >= lax.broadcasted_iota(jnp.int32, (HCH, HCH), 1))
        for c in reversed(range(nc)):
            sl = slice(HCH * c, HCH * (c + 1))
            for hh in range(HGRN_HB):
                hs = slice(HD * hh, HD * (hh + 1))
                stp = stb_ref[hh, c]
                dst = dst_s[hh]
                dstb = dst.astype(BF16)
                do_c = do[sl, hs]
                v_c = vb[sl, hs]
                dqa_s[sl, hs] = _mm(do_c, stp.astype(BF16))
                dkd_s[sl, hs] = _mm(v_c, dstb)
                ddec_s[sl, hs] = jnp.broadcast_to(jnp.sum(dst * stp, axis=0, keepdims=True), (HCH, HD))
                sc = jnp.where(tri, _mm_nt(q_inb[sl, hs], k_inb[sl, hs]), 0.0).astype(BF16)
                dsc = jnp.where(tri, _mm_nt(do_c, v_c), 0.0).astype(BF16)
                dv_s[sl, hs] = _mm_nt(k_db[sl, hs], dstb) + _mm_tn(sc, do_c)
                dqin_s[sl, hs] = _mm(dsc, k_inb[sl, hs])
                dkin_s[sl, hs] = _mm_tn(dsc, q_inb[sl, hs])
                d64 = dec[sl, hs]
                dst_s[hh] = dst * jnp.concatenate([d64, d64], axis=0) + _mm_tn(do_c, q_ab[sl, hs])
        dq_in = dqin_s[...]
        dq_a = dqa_s[...]
        dk_in = dkin_s[...]
        dk_d = dkd_s[...]
        dp_ref[0] = ((dq_in * e_q + dq_a * e_a) * QSCALE).astype(BF16)
        dp_ref[2] = dv_s[...].astype(BF16)
        tq = dq_in * q_in
        tk = dk_in * k_in
        td = dk_d * k_d
        d_a = tq + dq_a * q_a - tk - td
        d_a = d_a + jnp.where(row == HCH // 2 - 1, _seg_sum(tk - tq), 0.0)
        d_a = d_a + jnp.where(row == HCH - 1, _seg_sum(td) + ddec_s[...] * dec, 0.0)
        dlf = _revcumsum64(d_a, row)
        df = dlf / f - (dk_in * e_k + dk_d * e_l)
        dp_ref[1] = (df * (1.0 - lbv) * s * (1.0 - s)).astype(BF16)
        acc_ref[0] += _rows8(df * (1.0 - s))

    def col(off):
        return pl.BlockSpec((None, tb, HW), lambda h, cb: (off, nb - 1 - cb, h))

    hb = lambda h, cb: (nb - 1 - cb, h)
    return _call(
        body, name="hgrn_bwd", grid=(NH // HGRN_HB, nb), job=job,
        args=(dproj, dob, o_raw, proj, proj, proj, proj, st_before, lb_table, norm_g),
        in_specs=[ANY, pl.BlockSpec((tb, HW), hb), pl.BlockSpec((tb, HW), hb),
                  col(2), col(3), col(4), col(5),
                  pl.BlockSpec((HGRN_HB, nc, HD, HD), lambda h, cb: (h, nb - 1 - cb, 0, 0)),
                  pl.BlockSpec((2, HW), lambda h, cb: (0, h)), pl.BlockSpec((1, HW), lambda h, cb: (0, h))],
        out_specs=[pl.BlockSpec((4, tb, HW), lambda h, cb: (0, nb - 1 - cb, h)),
                   pl.BlockSpec((2, 8, HW), lambda h, cb: (0, 0, h))],
        out_shape=[SDS(dproj.shape, BF16), SDS((2, 8, D), F32)],
        scratch_shapes=[pltpu.VMEM((HGRN_HB, HD, HD), F32)] + [pltpu.VMEM((tb, HW), F32)] * 6,
        aliases={0: 0})


def _gmlp_bwd(dproj, da, proj, ln_g, ln_b, wm, wm_t, b_t):
    T = da.shape[0]
    tm = min(256, T)

    def body(dp_in, da_ref, u_ref, v_ref, lg_ref, lb_ref, wm_ref, wmt_ref, bt_ref,
             dp_ref, acc_ref, dws_ref, dmix_ref, du_s, dvn_s):
        del dp_in

        @pl.when(pl.program_id(0) == 0)
        def _():
            acc_ref[...] = jnp.zeros_like(acc_ref)
            dws_ref[...] = jnp.zeros_like(dws_ref)
            dmix_ref[...] = jnp.zeros_like(dmix_ref)

        u = u_ref[...]
        v = v_ref[...]
        lg = lg_ref[...]
        gu, t_u = _gelu(u)
        gv, t_v = _gelu(v)
        vhat, rs = _layer_norm_stats(gv)
        vnb = (vhat * lg + lb_ref[...]).astype(BF16)
        da_v = da_ref[...]
        for ch in range(tm // GCH):
            rows = slice(GCH * ch, GCH * (ch + 1))
            for g in range(NG):
                cols = slice(128 * g, 128 * (g + 1))
                vng = vnb[rows, cols]
                mixed = _mm(wm_ref[g], vng) + bt_ref[:, g:g + 1]
                dag = da_v[rows, cols]
                dmx = dag * gu[rows, cols]
                du_s[rows, cols] = dag * mixed
                dmxb = dmx.astype(BF16)
                dws_ref[:, cols] += _mm_nt(dmxb, vng)
                dmix_ref[:, cols] += dmx
                dvn_s[rows, cols] = _mm(wmt_ref[g], dmxb)
        dp_ref[0] = (du_s[...] * _gelu_grad(u, t_u)).astype(BF16)
        dvn = dvn_s[...]
        acc_ref[0] += _rows8(dvn * vhat)
        acc_ref[1] += _rows8(dvn)
        dvh = dvn * lg
        dgv = rs * (dvh - _mean(dvh) - vhat * _mean(dvh * vhat))
        dp_ref[1] = (dgv * _gelu_grad(v, t_v)).astype(BF16)

    row = lambda i: (0, 0)
    w3 = lambda i: (0, 0, 0)
    return pl.pallas_call(
        body, name="gmlp_bwd", grid=(T // tm,),
        in_specs=[ANY, pl.BlockSpec((tm, D), lambda i: (i, 0)),
                  pl.BlockSpec((None, tm, D), lambda i: (0, i, 0)), pl.BlockSpec((None, tm, D), lambda i: (1, i, 0)),
                  pl.BlockSpec((1, D), row), pl.BlockSpec((1, D), row),
                  pl.BlockSpec((NG, GCH, GCH), w3), pl.BlockSpec((NG, GCH, GCH), w3),
                  pl.BlockSpec((GCH, NG), row)],
        out_specs=[pl.BlockSpec((2, tm, D), lambda i: (2, i, 0)),
                   pl.BlockSpec((2, 8, D), w3), pl.BlockSpec((GCH, D), row), pl.BlockSpec((GCH, D), row)],
        out_shape=[SDS(dproj.shape, BF16), SDS((2, 8, D), F32), SDS((GCH, D), F32), SDS((GCH, D), F32)],
        scratch_shapes=[pltpu.VMEM((tm, D), F32), pltpu.VMEM((tm, D), F32)],
        input_output_aliases={0: 0},
        compiler_params=_cparams(),
    )(dproj, da, proj, proj, ln_g, ln_b, wm, wm_t, b_t)


def _proj_bwd(dproj, w_in4, x, dx1, g_mix, job=None):
    T = x.shape[0]
    tm = min(256, T)
    order = (2, 3, 4, 5, 0, 1, 6, 7)

    def body(dp_ref, w_ref, x_ref, dx1_ref, g_ref, gx_ref, acc_ref):
        @pl.when(pl.program_id(0) == 0)
        def _():
            acc_ref[...] = jnp.zeros_like(acc_ref)

        dh = None
        for m, og in enumerate(order):
            part = _mm_nt(dp_ref[m], w_ref[og // 2, :, D * (og % 2):D * (og % 2 + 1)])
            dh = part if dh is None else dh + part
        xv = x_ref[...]
        r = lax.rsqrt(_mean(xv * xv) + EPS)
        xn = xv * r
        acc_ref[...] += _rows8(dh * xn)
        dxn = dh * g_ref[...]
        gx_ref[...] = dx1_ref[...] + r * (dxn - xn * _mean(dxn * xn))

    t = lambda i: (i, 0)
    return _call(
        body, name="proj_bwd", grid=(T // tm,),
        in_specs=[pl.BlockSpec((NIN, tm, D), lambda i: (0, i, 0)),
                  pl.BlockSpec((NCHIP, D, 2 * D), lambda i: (0, 0, 0), pipeline_mode=pl.Buffered(1)),
                  pl.BlockSpec((tm, D), t), pl.BlockSpec((tm, D), t), pl.BlockSpec((1, D), lambda i: (0, 0))],
        out_specs=[pl.BlockSpec((tm, D), t), pl.BlockSpec((8, D), lambda i: (0, 0))],
        out_shape=[SDS((T, D), F32), SDS((8, D), F32)],
        args=(dproj, w_in4, x, dx1, g_mix), job=job)


def _dw_call(name, a, b, a_spec, b_spec, o_spec, out_shape, nblk, tt, job=None, prefetch=None):
    T = a.shape[-2]

    def body(*refs):
        a_ref, b_ref, o_ref = refs[-3:]

        @pl.when(pl.program_id(1) == 0)
        def _():
            o_ref[...] = jnp.zeros_like(o_ref)
        o_ref[...] += _mm_tn(a_ref[...], b_ref[...])

    (out,), job_out = _call(
        body, name=name, grid=(nblk, T // tt), in_specs=[a_spec, b_spec], out_specs=[o_spec],
        out_shape=[out_shape], args=(a, b), job=job, prefetch=prefetch)
    return out, job_out


def _dw_in_half(name, place, hb, dproj, mine, job=None):
    tt = min(DW_TOKENS, hb.shape[0])

    def comp(k, pc):
        return _component_of(2 * k + (pc[1] if mine else 1 - pc[1]))

    return _dw_call(
        name, hb, dproj,
        pl.BlockSpec((tt, D), lambda k, t, pc: (t, 0)),
        pl.BlockSpec((None, tt, D), lambda k, t, pc: (comp(k, pc), t, 0)),
        pl.BlockSpec((None, D, D), lambda k, t, pc: (k, 0, 0)),
        SDS((NCHIP, D, D), F32), NCHIP, tt, job, place)


def _dw_gate_up(h2b, dgu4, job=None):
    tt = min(DW_TOKENS, h2b.shape[0])
    return _dw_call(
        "dw_gate_up", h2b, dgu4,
        pl.BlockSpec((tt, D), lambda k, t: (t, 0)),
        pl.BlockSpec((None, tt, FFS), lambda k, t: (k, t, 0)),
        pl.BlockSpec((None, D, FFS), lambda k, t: (k, 0, 0)),
        SDS((NCHIP, D, FFS), F32), NCHIP, tt, job)


def _dw_down(act, dx2b, job=None):
    tt = min(DW_TOKENS, act.shape[0])
    g, job_out = _dw_call(
        "dw_down", act, dx2b,
        pl.BlockSpec((tt, FFS), lambda k, t: (t, k)),
        pl.BlockSpec((tt, D), lambda k, t: (t, 0)),
        pl.BlockSpec((FFS, D), lambda k, t: (k, 0)),
        SDS((FF, D), F32), 2, tt, job)
    return g.reshape(NCHIP, FF // NCHIP, D), job_out


def _dw_square(name, a, b, job=None):
    tt = min(DW_TOKENS, a.shape[0])
    g, job_out = _dw_call(
        name, a, b,
        pl.BlockSpec((tt, D), lambda k, t: (t, 0)), pl.BlockSpec((tt, D), lambda k, t: (t, 0)),
        pl.BlockSpec((D, D), lambda k, t: (0, 0)), SDS((D, D), F32), 1, tt, job)
    return g.reshape(NCHIP, D // NCHIP, D), job_out


def _place():
    x, y, c = lax.axis_index("x"), lax.axis_index("y"), lax.axis_index("c")
    return x, y, c, 2 * x + y


def _chip_at(x, y, s):
    return x ^ (s >> 1), y ^ (s & 1)


class _Job:
    def __init__(self, ins, out_shapes, sems, start, finish, aliases=None, mid=None):
        self.ins, self.out_shapes, self.sems = list(ins), list(out_shapes), list(sems)
        self.start, self.finish, self.aliases = start, finish, dict(aliases or {})
        self.mid = mid if mid is not None else (lambda ins, outs, sems: None)


def _join_jobs(*jobs):
    def cut(refs, sizes):
        out, at = [], 0
        for n in sizes:
            out.append(refs[at:at + n])
            at += n
        return out

    ni = [len(j.ins) for j in jobs]
    no = [len(j.out_shapes) for j in jobs]
    ns = [len(j.sems) for j in jobs]

    def run(which):
        def go(ins, outs, sems):
            for j, a, b, c in zip(jobs, cut(ins, ni), cut(outs, no), cut(sems, ns)):
                getattr(j, which)(a, b, c)
        return go

    aliases = {}
    for k, j in enumerate(jobs):
        for a, b in j.aliases.items():
            aliases[sum(ni[:k]) + a] = sum(no[:k]) + b
    return _Job([a for j in jobs for a in j.ins], [o for j in jobs for o in j.out_shapes],
                [s for j in jobs for s in j.sems], run("start"), run("finish"), aliases, run("mid"))


def _call(body, *, name, grid, in_specs, out_specs, out_shape, args, scratch_shapes=(), aliases=None,
          job=None, prefetch=None):
    n_in, n_out, n_scr = len(in_specs), len(out_specs), len(scratch_shapes)
    npf = 0 if prefetch is None else 1
    job = job if job is not None else _Job([], [], [], lambda *a: None, lambda *a: None)
    ji, jo = len(job.ins), len(job.out_shapes)
    steps = math.prod(grid)

    def wrapped(*refs):
        pf, refs = refs[:npf], refs[npf:]
        ins, jin = refs[:n_in], refs[n_in:n_in + ji]
        o0 = n_in + ji
        outs, jout = refs[o0:o0 + n_out], refs[o0 + n_out:o0 + n_out + jo]
        s0 = o0 + n_out + jo
        scr, jsem = refs[s0:s0 + n_scr], refs[s0 + n_scr:]
        step = functools.reduce(lambda acc, ag: acc * ag[1] + pl.program_id(ag[0]), enumerate(grid), 0)
        if ji or jo:
            @pl.when(step == 0)
            def _():
                job.start(jin, jout, jsem)

        body(*pf, *ins, *outs, *scr)

        if ji or jo:
            @pl.when(step == steps // 2)
            def _():
                job.mid(jin, jout, jsem)

            @pl.when(step == steps - 1)
            def _():
                job.finish(jin, jout, jsem)

    io = {npf + a: b for a, b in dict(aliases or {}).items()}
    io.update({npf + n_in + a: n_out + b for a, b in job.aliases.items()})
    kw = dict(in_specs=list(in_specs) + [ANY] * ji, out_specs=list(out_specs) + [ANY] * jo,
              scratch_shapes=list(scratch_shapes) + job.sems)
    if npf:
        kw = dict(grid_spec=pltpu.PrefetchScalarGridSpec(num_scalar_prefetch=1, grid=grid, **kw))
    else:
        kw["grid"] = grid
    res = pl.pallas_call(
        wrapped, name=name, out_shape=list(out_shape) + job.out_shapes, input_output_aliases=io,
        compiler_params=_cparams(has_side_effects=bool(ji or jo)), **kw,
    )(*(() if prefetch is None else (prefetch,)), *args, *job.ins)
    return list(res[:n_out]), list(res[n_out:])


def _run_job(job, name):
    ji, jo = len(job.ins), len(job.out_shapes)

    def body(*refs):
        jin, jout, jsem = refs[:ji], refs[ji:ji + jo], refs[ji + jo:]
        job.start(jin, jout, jsem)
        job.finish(jin, jout, jsem)

    return list(pl.pallas_call(
        body, name=name, in_specs=[ANY] * ji, out_specs=[ANY] * jo, out_shape=job.out_shapes,
        scratch_shapes=job.sems, input_output_aliases=job.aliases,
        compiler_params=pltpu.CompilerParams(has_side_effects=True))(*job.ins))


def _cast_shard(name, place, w):
    rows, cols = w.shape
    tr = 352 if rows % 352 == 0 else 256

    def body(pc_ref, w_ref, o_ref):
        del pc_ref
        o_ref[...] = w_ref[...].astype(BF16)

    return pl.pallas_call(
        body, name=name,
        grid_spec=pltpu.PrefetchScalarGridSpec(
            num_scalar_prefetch=1, grid=(rows // tr,),
            in_specs=[pl.BlockSpec((tr, cols), lambda i, pc: (i, 0))],
            out_specs=pl.BlockSpec((None, tr, cols), lambda i, pc: (pc[0], i, 0))),
        out_shape=SDS((NCHIP, rows, cols), BF16),
        compiler_params=_cparams(),
    )(place, w)


def _sibling_copy(ref, send_sem, recv_sem):
    x, y, c, _ = _place()
    return pltpu.make_async_remote_copy(src_ref=ref, dst_ref=ref, send_sem=send_sem, recv_sem=recv_sem,
                                        device_id=(x, y, 1 - c), device_id_type=MESH)


def _half_rows(arr, slot, core):
    half = arr.shape[1] // 2
    return arr.at[slot, pl.ds(pl.multiple_of(core * half, 16), half)]


def _quarter_rows(arr, slot, core, q):
    quarter = arr.shape[1] // 4
    return arr.at[slot, pl.ds(pl.multiple_of((2 * core + q) * quarter, 16), quarter)]


def _chip_copy(ref, dist, send_sem, recv_sem):
    x, y, c, _ = _place()
    cx, cy = _chip_at(x, y, dist)
    return pltpu.make_async_remote_copy(src_ref=ref, dst_ref=ref, send_sem=send_sem, recv_sem=recv_sem,
                                        device_id=(cx, cy, c), device_id_type=MESH)


def _gather_sems(n):
    dma = pltpu.SemaphoreType.DMA
    return [dma((n, 2))] * 4 + [dma((n, 4))] * 2


def _gather_start(arrs, sems):
    dsend, drecv = sems[0], sems[1]
    _, _, c, j = _place()
    for w, arr in enumerate(arrs):
        for dist in (1, 2):
            _chip_copy(_half_rows(arr, j, c), dist, dsend.at[w, dist - 1], drecv.at[w, dist - 1]).start()


def _gather_land(arrs, sems, dist, first=0):
    dsend, drecv, rsend, rrecv, fsend, frecv = sems
    _, _, c, j = _place()
    if dist < 3:
        other = 3 - dist
        for w, arr in enumerate(arrs, first):
            landed = _half_rows(arr, j ^ dist, c)
            _chip_copy(landed, dist, dsend.at[w, dist - 1], drecv.at[w, dist - 1]).wait_recv()
            relay = _quarter_rows(arr, j ^ dist, c, other - 1)
            _chip_copy(relay, other, rsend.at[w, other - 1], rrecv.at[w, other - 1]).start()
            _sibling_copy(landed, fsend.at[w, dist - 1], frecv.at[w, dist - 1]).start()
        for w, arr in enumerate(arrs, first):
            theirs = _half_rows(arr, j ^ dist, 1 - c)
            _sibling_copy(theirs, fsend.at[w, dist - 1], frecv.at[w, dist - 1]).wait_recv()
    else:
        for w, arr in enumerate(arrs, first):
            for via in (1, 2):
                piece = _quarter_rows(arr, j ^ 3, c, via - 1)
                _chip_copy(piece, via, rsend.at[w, via - 1], rrecv.at[w, via - 1]).wait_recv()
                _sibling_copy(piece, fsend.at[w, 1 + via], frecv.at[w, 1 + via]).start()
        for w, arr in enumerate(arrs, first):
            for via in (1, 2):
                theirs = _quarter_rows(arr, j ^ 3, 1 - c, via - 1)
                _sibling_copy(theirs, fsend.at[w, 1 + via], frecv.at[w, 1 + via]).wait_recv()


def _gather_drain(arrs, sems):
    dsend, drecv, rsend, rrecv, fsend, frecv = sems
    _, _, c, j = _place()
    for w, arr in enumerate(arrs):
        for dist in (1, 2):
            other = 3 - dist
            _chip_copy(_half_rows(arr, j, c), dist, dsend.at[w, dist - 1], drecv.at[w, dist - 1]).wait_send()
            _chip_copy(_quarter_rows(arr, j ^ dist, c, other - 1), other,
                       rsend.at[w, other - 1], rrecv.at[w, other - 1]).wait_send()
            _sibling_copy(_half_rows(arr, j ^ dist, c), fsend.at[w, dist - 1], frecv.at[w, dist - 1]).wait_send()
            _sibling_copy(_quarter_rows(arr, j ^ 3, c, dist - 1),
                          fsend.at[w, 1 + dist], frecv.at[w, 1 + dist]).wait_send()


def _gather_neighbours(arrs, sems):
    _gather_land(arrs, sems, 1)
    _gather_land(arrs, sems, 2)


def _gather_finish(arrs, sems):
    _gather_land(arrs, sems, 3)
    _gather_drain(arrs, sems)


def _gather_job(arrs):
    n = len(arrs)
    return _Job(arrs, [SDS(a.shape, a.dtype) for a in arrs], _gather_sems(n),
                lambda ins, outs, sems: _gather_start(outs, sems),
                lambda ins, outs, sems: _gather_finish(outs, sems), {k: k for k in range(n)},
                mid=lambda ins, outs, sems: _gather_neighbours(outs, sems))


def _exchange_job(arrs, out_shapes, n, copies):
    def start(ins, outs, sems):
        for cp in copies(ins, outs, sems[0], sems[1]):
            cp.start()

    def finish(ins, outs, sems):
        for cp in copies(ins, outs, sems[0], sems[1]):
            cp.wait()

    return _Job(arrs, out_shapes, [pltpu.SemaphoreType.DMA((n,))] * 2, start, finish)


def _pair_exchange_job(grads):
    def copies(ins, outs, send_sem, recv_sem):
        x, y, c, _ = _place()
        res = []
        for w in range(len(grads)):
            half = ins[w].shape[1] // 2
            theirs = pl.ds(pl.multiple_of((1 - c) * half, 8), half)
            res.append(pltpu.make_async_remote_copy(
                src_ref=ins[w].at[:, theirs, :], dst_ref=outs[w], send_sem=send_sem.at[w],
                recv_sem=recv_sem.at[w], device_id=(x, y, 1 - c), device_id_type=MESH))
        return res

    return _exchange_job(grads, [SDS((NCHIP, g.shape[1] // 2, g.shape[2]), F32) for g in grads],
                         len(grads), copies)


def _row_tile(rows, cols):
    tr = rows
    while tr * cols * 4 > ELEMENTWISE_BLOCK_BYTES and tr % 32 == 0:
        tr //= 2
    return tr


def _pair_sum(name, place, g, sib):
    half, cols = sib.shape[1], sib.shape[2]
    tr = _row_tile(half, cols)
    nt = half // tr
    mine = nt if g.shape[1] == 2 * half else 0

    def body(pc_ref, g_ref, s_ref, own_ref, out_ref):
        del pc_ref
        v = g_ref[...] + s_ref[...]
        out_ref[...] = v.astype(BF16)

        @pl.when(pl.program_id(1) == 0)
        def _():
            own_ref[...] = v

    return pl.pallas_call(
        body, name=name,
        grid_spec=pltpu.PrefetchScalarGridSpec(
            num_scalar_prefetch=1, grid=(nt, NCHIP),
            in_specs=[pl.BlockSpec((None, tr, cols), lambda i, s, pc: (pc[0] ^ s, pc[1] * mine + i, 0)),
                      pl.BlockSpec((None, tr, cols), lambda i, s, pc: (pc[0] ^ s, i, 0))],
            out_specs=[pl.BlockSpec((tr, cols), lambda i, s, pc: (i, 0)),
                       pl.BlockSpec((None, tr, cols), lambda i, s, pc: (s, i, 0))]),
        out_shape=[SDS((half, cols), F32), SDS((NCHIP, half, cols), BF16)],
        compiler_params=_cparams(),
    )(place, g, sib)


def _chip_exchange_job(parts):
    def copies(ins, outs, send_sem, recv_sem):
        x, y, c, _ = _place()
        res = []
        for w in range(len(parts)):
            for s in range(1, NCHIP):
                cx, cy = _chip_at(x, y, s)
                k = w * (NCHIP - 1) + s - 1
                res.append(pltpu.make_async_remote_copy(
                    src_ref=ins[w].at[s], dst_ref=outs[w].at[s - 1], send_sem=send_sem.at[k],
                    recv_sem=recv_sem.at[k], device_id=(cx, cy, c), device_id_type=MESH))
        return res

    return _exchange_job(parts, [SDS((NCHIP - 1,) + p.shape[1:], BF16) for p in parts],
                         len(parts) * (NCHIP - 1), copies)


def _chip_sum(name, own, rem):
    half, cols = own.shape
    tr = _row_tile(half, cols)

    def body(own_ref, rem_ref, out_ref):
        out_ref[...] = ((own_ref[...] + rem_ref[0].astype(F32)) + rem_ref[1].astype(F32)) + rem_ref[2].astype(F32)

    return pl.pallas_call(
        body, name=name, grid=(half // tr,),
        in_specs=[pl.BlockSpec((tr, cols), lambda i: (i, 0)),
                  pl.BlockSpec((NCHIP - 1, tr, cols), lambda i: (0, i, 0))],
        out_specs=pl.BlockSpec((tr, cols), lambda i: (i, 0)),
        out_shape=SDS((half, cols), F32),
        compiler_params=_cparams(),
    )(own, rem)


def _share_halves_job(halves):
    def copies(ins, outs, send_sem, recv_sem):
        x, y, c, _ = _place()
        return [pltpu.make_async_remote_copy(
            src_ref=ins[w], dst_ref=outs[w], send_sem=send_sem.at[w], recv_sem=recv_sem.at[w],
            device_id=(x, y, 1 - c), device_id_type=MESH) for w in range(len(halves))]

    return _exchange_job(halves, [SDS(h.shape, F32) for h in halves], len(halves), copies)


def _adamw_math(w, g, m, v):
    m = B1 * m + (1.0 - B1) * g
    v = B2 * v + (1.0 - B2) * (g * g)
    m_hat = m / (1.0 - B1 ** STEP)
    v_hat = v / (1.0 - B2 ** STEP)
    delta = -LR * (m_hat / (jnp.sqrt(v_hat) + AEPS) + WD * w)
    return delta, m, v


def _adamw(name, place, w, own, sib, m, v):
    rows, cols = w.shape
    by_cols = own.shape[0] == rows
    half, pc_cols = (rows, cols // 2) if by_cols else (rows // 2, cols)
    tr = _row_tile(half, pc_cols)
    nt = half // tr

    def body(pc_ref, w_ref, own_ref, sib_ref, m_ref, v_ref, g_ref, d_ref, mo_ref, vo_ref):
        g = jnp.where(pl.program_id(0) == pc_ref[1], own_ref[...], sib_ref[...])
        d, mn, vn = _adamw_math(w_ref[...], g, m_ref[...], v_ref[...])
        g_ref[...] = g
        d_ref[...] = d
        mo_ref[...] = mn
        vo_ref[...] = vn

    full = pl.BlockSpec((tr, pc_cols), (lambda h, i, pc: (i, h)) if by_cols else (lambda h, i, pc: (h * nt + i, 0)))
    part = pl.BlockSpec((tr, pc_cols), lambda h, i, pc: (i, 0))
    return pl.pallas_call(
        body, name=name,
        grid_spec=pltpu.PrefetchScalarGridSpec(
            num_scalar_prefetch=1, grid=(2, nt),
            in_specs=[full, part, part, full, full], out_specs=[full] * 4),
        out_shape=[SDS((rows, cols), F32)] * 4,
        compiler_params=_cparams(),
    )(place, w, own, sib, m, v)


def _small_allreduce_adamw(sp, w, m, v):
    shape = sp.shape

    def body(sp_ref, w_ref, m_ref, v_ref, g_ref, d_ref, mo_ref, vo_ref,
             sib_s, pair_s, chip_s, send_sem, recv_sem):
        x, y, c, j = _place()
        cp = pltpu.make_async_remote_copy(
            src_ref=sp_ref, dst_ref=sib_s, send_sem=send_sem.at[0], recv_sem=recv_sem.at[0],
            device_id=(x, y, 1 - c), device_id_type=MESH)
        cp.start()
        cp.wait()
        pair_s[...] = sp_ref[...] + sib_s[...]
        cps = []
        for s in range(1, NCHIP):
            cx, cy = _chip_at(x, y, s)
            cp = pltpu.make_async_remote_copy(
                src_ref=pair_s, dst_ref=chip_s.at[s], send_sem=send_sem.at[s], recv_sem=recv_sem.at[s],
                device_id=(cx, cy, c), device_id_type=MESH)
            cp.start()
            cps.append(cp)
        chip_s[0] = pair_s[...]
        for cp in cps:
            cp.wait()
        tot = chip_s[j]
        for k in range(1, NCHIP):
            tot = tot + chip_s[k ^ j]
        g_ref[...] = tot
        d, mn, vn = _adamw_math(w_ref[...], tot, m_ref[...], v_ref[...])
        d_ref[...] = d
        mo_ref[...] = mn
        vo_ref[...] = vn

    vm = pl.BlockSpec(memory_space=pltpu.VMEM)
    return pl.pallas_call(
        body, name="small_allreduce_adamw",
        in_specs=[vm] * 4, out_specs=[vm] * 4, out_shape=[SDS(shape, F32)] * 4,
        scratch_shapes=[pltpu.VMEM(shape, F32), pltpu.VMEM(shape, F32), pltpu.VMEM((NCHIP,) + shape, F32),
                        pltpu.SemaphoreType.DMA((NCHIP,)), pltpu.SemaphoreType.DMA((NCHIP,))],
        compiler_params=pltpu.CompilerParams(has_side_effects=True),
    )(sp, w, m, v)


def _pack_small(first, mix, ln_g, ln_b, b_s, lbt, hn, ffn, fin, w_s):
    rows = [first.reshape(1, D), mix.reshape(1, D), ln_g.reshape(1, D), ln_b.reshape(1, D),
            b_s.reshape(1, D), lbt.reshape(2, D), hn.reshape(1, D), ffn.reshape(1, D), fin.reshape(1, D),
            jnp.zeros((6, D), F32)]
    return jnp.concatenate(rows + [w_s.reshape(NG, GCH, GCH).transpose(1, 0, 2).reshape(GCH, D)], axis=0)


def _unpack_small(p):
    w_s = p[16:].reshape(GCH, NG, GCH).transpose(1, 0, 2).reshape(1, NG, GCH, GCH)
    return dict(norm_mix_g=p[1:2], gmlp_ln_g=p[2:3], gmlp_ln_b=p[3:4], gmlp_b_s=p[4].reshape(1, NG, GCH),
                hgrn_lb_table=p[5:7], hgrn_norm_g=p[7:8], norm_ffn_g=p[8:9], norm_final_g=p[9],
                gmlp_w_s=w_s)


SMALL = ("norm_mix_g", "gmlp_ln_g", "gmlp_ln_b", "gmlp_w_s", "gmlp_b_s", "hgrn_lb_table", "hgrn_norm_g",
         "norm_ffn_g", "norm_final_g")
BIG = ("w_in", "w_gate_up", "w_branch_a", "w_branch_b", "w_out", "w_down")
ORDER = ("norm_mix_g", "w_in", "gmlp_ln_g", "gmlp_ln_b", "gmlp_w_s", "gmlp_b_s", "hgrn_lb_table",
         "hgrn_norm_g", "w_branch_a", "w_branch_b", "w_out", "norm_ffn_g", "w_gate_up", "w_down",
         "norm_final_g")


def kernel(x, norm_mix_g, w_in, gmlp_ln_g, gmlp_ln_b, gmlp_w_s, gmlp_b_s, hgrn_lb_table, hgrn_norm_g, w_branch_a, w_branch_b, w_out, norm_ffn_g, w_gate_up, w_down, norm_final_g, loss_target, m_norm_mix_g, m_w_in, m_gmlp_ln_g, m_gmlp_ln_b, m_gmlp_w_s, m_gmlp_b_s, m_hgrn_lb_table, m_hgrn_norm_g, m_w_branch_a, m_w_branch_b, m_w_out, m_norm_ffn_g, m_w_gate_up, m_w_down, m_norm_final_g, v_norm_mix_g, v_w_in, v_gmlp_ln_g, v_gmlp_ln_b, v_gmlp_w_s, v_gmlp_b_s, v_hgrn_lb_table, v_hgrn_norm_g, v_w_branch_a, v_w_branch_b, v_w_out, v_norm_ffn_g, v_w_gate_up, v_w_down, v_norm_final_g):
    args = dict(locals())
    T = x.shape[1]
    xs = x.reshape(T, D)
    target = loss_target.reshape(T, D)
    big = {n: args[n].reshape(args[n].shape[1:]) for n in BIG}
    big_m = {n: args["m_" + n].reshape(args[n].shape[1:]) for n in BIG}
    big_v = {n: args["v_" + n].reshape(args[n].shape[1:]) for n in BIG}

    x_i, y_i, c_i = lax.axis_index("x"), lax.axis_index("y"), lax.axis_index("c")
    place = jnp.stack([2 * x_i + y_i, c_i]).astype(jnp.int32)
    cast = {n: _cast_shard("cast_" + n, place, big[n]) for n in BIG}
    tril = jnp.tril(jnp.ones((GCH, GCH), bool))
    wm = jnp.where(tril, gmlp_w_s[0], 0.0).astype(BF16)
    wm_t = jnp.swapaxes(wm, 1, 2)
    b_t = gmlp_b_s[0].T

    (proj, hb), w_in4, (w_a4,) = _proj_fwd(place, xs, norm_mix_g, cast["w_in"], [cast["w_branch_a"]])
    (ab, y_a), (w_b4, w_out4) = _gmlp_fwd(
        proj, gmlp_ln_g, gmlp_ln_b, wm, b_t, w_a4.reshape(D, D),
        job=_gather_job([cast["w_branch_b"], cast["w_out"]]))
    (o_raw, obb, st_before), (w_gu4,) = _hgrn_fwd(
        proj, hgrn_lb_table, hgrn_norm_g, job=_gather_job([cast["w_gate_up"]]))
    w_a, w_b, w_o = (w.reshape(D, D) for w in (w_a4, w_b4, w_out4))
    (y_b, mgb, x1), (w_down4,) = _merge_fwd(xs, y_a, obb, proj, w_b, w_o, job=_gather_job([cast["w_down"]]))
    w_dn = w_down4.reshape(FF, D)
    act, dx2b, h2b, dgu4, dx1, dx1b, acc_ffn = _ffn_fwd_bwd(
        x1, target, norm_ffn_g, norm_final_g.reshape(1, D), w_gu4, w_dn)

    grads, owns, parts, halves, sibh = {}, {}, {}, {}, {}

    def pair_sums(names, sibs):
        for n, s in zip(names, sibs):
            owns[n], parts[n] = _pair_sum("rs_pair_sum_" + n, place, grads[n], s)

    def chip_sums(names, got):
        for n, r in zip(names, got):
            halves[n] = _chip_sum("rs_chip_sum_" + n, owns[n], r)

    ffn, mix = ("w_gate_up", "w_down"), ("w_branch_a", "w_branch_b", "w_out")
    grads["w_gate_up"], _ = _dw_gate_up(h2b, dgu4)
    grads["w_down"], _ = _dw_down(act, dx2b)
    (dya, dyb, da, dob, dproj), got = _merge_bwd(
        dx1b, y_a, y_b, proj, w_o, w_a, w_b, job=_pair_exchange_job([grads[n] for n in ffn]))
    pair_sums(ffn, got)
    grads["w_branch_a"], _ = _dw_square("dw_branch_a", ab, dya)
    grads["w_branch_b"], _ = _dw_square("dw_branch_b", obb, dyb)
    grads["w_out"], _ = _dw_square("dw_out", mgb, dx1b)
    (dproj, acc_hgrn), got = _hgrn_bwd(
        dproj, dob, o_raw, proj, st_before, hgrn_lb_table, hgrn_norm_g,
        job=_join_jobs(_chip_exchange_job([parts[n] for n in ffn]), _pair_exchange_job([grads[n] for n in mix])))
    chip_sums(ffn, got[:2])
    pair_sums(mix, got[2:])
    dproj, acc_ln, dws, dmix = _gmlp_bwd(dproj, da, proj, gmlp_ln_g, gmlp_ln_b, wm, wm_t, b_t)
    for_sibling, got = _dw_in_half(
        "dw_in_sibling_half", place, hb, dproj, False,
        job=_join_jobs(_share_halves_job([halves[n] for n in ffn]), _chip_exchange_job([parts[n] for n in mix])))
    sibh.update(zip(ffn, got[:2]))
    chip_sums(mix, got[2:])
    grads["w_in"], got = _dw_in_half(
        "dw_in_own_half", place, hb, dproj, True, job=_share_halves_job([for_sibling]))
    pair_sums(("w_in",), got)
    (grad_x, acc_mix), got = _proj_bwd(
        dproj, w_in4, xs, dx1, norm_mix_g,
        job=_join_jobs(_chip_exchange_job([parts["w_in"]]), _share_halves_job([halves[n] for n in mix])))
    chip_sums(("w_in",), got[:1])
    sibh.update(zip(mix, got[1:]))
    (sibh["w_in"],) = _run_job(_share_halves_job([halves["w_in"]]), "rs_share_halves_w_in")
    out = {}
    for n in BIG:
        g, d, mn, vn = _adamw("adamw_" + n, place, big[n], halves[n], sibh[n], big_m[n], big_v[n])
        shp = args[n].shape
        out[n] = (g.reshape(shp), d.reshape(shp), mn.reshape(shp), vn.reshape(shp))

    lbv = jax.nn.sigmoid(hgrn_lb_table[0] - hgrn_lb_table[1])
    d_t0 = jnp.sum(acc_hgrn[0], axis=0) * lbv * (1.0 - lbv)
    loss_row = jnp.zeros((D,), F32).at[0].set(jnp.sum(acc_ffn[0]))
    dws_m = jnp.where(tril[:, None, :], dws.reshape(GCH, NG, GCH), 0.0).transpose(1, 0, 2)
    db_s = jnp.sum(dmix.reshape(GCH, NG, GCH), axis=-1).T
    sp = _pack_small(loss_row, jnp.sum(acc_mix, 0), jnp.sum(acc_ln[0], 0), jnp.sum(acc_ln[1], 0), db_s,
                     jnp.stack([d_t0, -d_t0]), jnp.sum(acc_hgrn[1], 0), jnp.sum(acc_ffn[2], 0),
                     jnp.sum(acc_ffn[1], 0), dws_m)
    zero = jnp.zeros((D,), F32)

    def pack(prefix):
        a = lambda n: args[prefix + n]
        return _pack_small(zero, a("norm_mix_g"), a("gmlp_ln_g"), a("gmlp_ln_b"), a("gmlp_b_s"),
                           a("hgrn_lb_table"), a("hgrn_norm_g"), a("norm_ffn_g"), a("norm_final_g"),
                           a("gmlp_w_s"))

    packed = _small_allreduce_adamw(sp, pack(""), pack("m_"), pack("v_"))
    loss = packed[0][0, 0]
    small = [_unpack_small(p) for p in packed]
    for n in SMALL:
        out[n] = tuple(s[n] for s in small)
    return (loss, grad_x.reshape(x.shape), *[out[n][0] for n in ORDER], *[out[n][1] for n in ORDER],
            *[out[n][2] for n in ORDER], *[out[n][3] for n in ORDER])
```

```python
import functools
import math

import jax
import jax.numpy as jnp
from jax import lax
from jax.experimental import pallas as pl
from jax.experimental.pallas import tpu as pltpu

F32 = jnp.float32
BF16 = jnp.bfloat16
SDS = jax.ShapeDtypeStruct
MESH = pl.DeviceIdType.MESH
ANY = pl.BlockSpec(memory_space=pl.ANY)

D = 1024
NIN = 8
NG = 8
GCH = 128
NH = 8
HD = 128
HCH = 64
HGRN_HB = 4
HW = HGRN_HB * HD
DW_TOKENS = 2048
ELEMENTWISE_BLOCK_BYTES = 2 * 1024 * 1024
MM_COLS = 256
FF = 2816
FFS = 1408
NCHIP = 4
EPS = 1e-6
QSCALE = HD ** -0.5
GELU_C0 = math.sqrt(2.0 / math.pi)
GELU_C1 = 0.044715
LR, B1, B2, AEPS, WD, STEP = 0.001, 0.9, 0.999, 1e-08, 0.01, 10
VMEM_LIMIT_V7X = 56 * 1024 * 1024
SP_ROWS = 144


def _cparams(**kw):
    return pltpu.CompilerParams(vmem_limit_bytes=VMEM_LIMIT_V7X, **kw)


def _mm(a, b):
    return jnp.dot(a, b, preferred_element_type=F32)


def _mm_nt(a, b):
    return lax.dot_general(a, b, (((1,), (1,)), ((), ())), preferred_element_type=F32)


def _mm_tn(a, b):
    return lax.dot_general(a, b, (((0,), (0,)), ((), ())), preferred_element_type=F32)


def _rows8(x):
    r, c = x.shape
    return jnp.sum(x.reshape(r // 8, 8, c), axis=0)


def _mean(x):
    return jnp.mean(x, axis=-1, keepdims=True)


def _sigmoid(x):
    return 1.0 / (1.0 + jnp.exp(-x))


def _gelu(x):
    t = jnp.tanh(GELU_C0 * (x + GELU_C1 * x * x * x))
    return 0.5 * x * (1.0 + t), t


def _gelu_grad(x, t):
    return 0.5 * (1.0 + t) + 0.5 * x * (1.0 - t * t) * (GELU_C0 * (1.0 + 3.0 * GELU_C1 * x * x))


def _component_of(group):
    return jnp.where(group < 6, (group + 4) % 6, group)


def _proj_fwd(place, x, g_mix, w_in4, later):
    T = x.shape[0]
    tm = min(1024, T)
    ni = T // tm
    n = len(later)

    def body(pc_ref, x_ref, g_ref, *rest):
        proj_ref, h_ref, w_all = rest[1 + n:4 + n]
        gathered = rest[4 + n:4 + 2 * n]
        hs, wbuf, wsem, obuf, osem = rest[4 + 2 * n:9 + 2 * n]
        w_sems, later_sems = rest[9 + 2 * n:15 + 2 * n], rest[15 + 2 * n:]
        jp, i = pl.program_id(0), pl.program_id(1)
        w_cols = [w_all.at[:, :, pl.ds(k * D, D)] for k in range(2)]

        def w_copy(blk):
            cols = pl.ds(pl.multiple_of((blk % 2) * D, 128), D)
            return pltpu.make_async_copy(w_all.at[pc_ref[0] ^ (blk // 2), :, cols], wbuf.at[blk % 2],
                                         wsem.at[blk % 2])

        @pl.when((jp == 0) & (i == 0))
        def _():
            _gather_start(w_cols, w_sems)
            _gather_start(gathered, later_sems)
            w_copy(jp).start()

        @pl.when(i == 0)
        def _():
            w_copy(jp).wait()

        @pl.when(jp == 0)
        def _():
            xv = x_ref[...]
            r = lax.rsqrt(_mean(xv * xv) + EPS)
            hb = (xv * r * g_ref[...]).astype(BF16)
            hs[i] = hb
            h_ref[...] = hb

        step = jp * ni + i
        slot = step % 2

        def o_copies(slot_):
            comp = 2 * (pc_ref[0] ^ (jp // 2)) + jp % 2
            return [pltpu.make_async_copy(
                obuf.at[slot_, pl.ds(p * (tm // 2), tm // 2)],
                proj_ref.at[comp, pl.ds(pl.multiple_of(i * tm + p * (tm // 2), 8), tm // 2)],
                osem.at[slot_, p]) for p in range(2)]

        @pl.when(step >= 2)
        def _():
            for cp in o_copies(slot):
                cp.wait()

        hv = hs[i]
        for k in range(D // MM_COLS):
            cols = slice(MM_COLS * k, MM_COLS * (k + 1))
            obuf[slot, :, cols] = _mm(hv, wbuf[jp % 2, :, cols])
        for cp in o_copies(slot):
            cp.start()

        @pl.when(step == NIN * ni - 1)
        def _():
            for cp in o_copies(1 - slot) + o_copies(slot):
                cp.wait()

        for nxt in range(1, NIN):
            @pl.when((jp == nxt - 1) & (i == ni - 1))
            def _():
                if nxt >= 2:
                    _gather_land([w_cols[nxt % 2]], w_sems, nxt // 2, first=nxt % 2)
                if nxt == 4:
                    _gather_neighbours(gathered, later_sems)
                w_copy(jp + 1).start()

        @pl.when((jp == NIN - 1) & (i == ni - 1))
        def _():
            _gather_drain(w_cols, w_sems)
            _gather_finish(gathered, later_sems)

    tile = lambda jp, i, pc: (jnp.where(jp == 0, i, ni - 1), 0)
    res = pl.pallas_call(
        body, name="proj_fwd",
        grid_spec=pltpu.PrefetchScalarGridSpec(
            num_scalar_prefetch=1, grid=(NIN, ni),
            in_specs=[pl.BlockSpec((tm, D), tile), pl.BlockSpec((1, D), lambda jp, i, pc: (0, 0))] + [ANY] * (1 + n),
            out_specs=[ANY, pl.BlockSpec((tm, D), tile)] + [ANY] * (1 + n),
            scratch_shapes=[pltpu.VMEM((ni, tm, D), BF16), pltpu.VMEM((2, D, D), BF16),
                            pltpu.SemaphoreType.DMA((2,)), pltpu.VMEM((2, tm, D), F32),
                            pltpu.SemaphoreType.DMA((2, 2))] + _gather_sems(2) + _gather_sems(n)),
        out_shape=[SDS((NIN, T, D), F32), SDS((T, D), BF16), SDS(w_in4.shape, BF16)]
        + [SDS(a.shape, a.dtype) for a in later],
        input_output_aliases={3 + k: 2 + k for k in range(1 + n)},
        compiler_params=_cparams(has_side_effects=True),
    )(place, x, g_mix, w_in4, *later)
    return res[:2], res[2], res[3:]


def _layer_norm_stats(gv):
    mu = _mean(gv)
    xc = gv - mu
    rs = lax.rsqrt(_mean(xc * xc) + EPS)
    return xc * rs, rs


def _gmlp_fwd(proj, ln_g, ln_b, wm, b_t, job=None):
    T = proj.shape[1]
    tm = min(256, T)

    def body(u_ref, v_ref, lg_ref, lb_ref, wm_ref, bt_ref, a_ref, a_s):
        gu, _ = _gelu(u_ref[...])
        gv, _ = _gelu(v_ref[...])
        vhat, _ = _layer_norm_stats(gv)
        vnb = (vhat * lg_ref[...] + lb_ref[...]).astype(BF16)
        for ch in range(tm // GCH):
            rows = slice(GCH * ch, GCH * (ch + 1))
            for g in range(NG):
                cols = slice(128 * g, 128 * (g + 1))
                mixed = _mm(wm_ref[g], vnb[rows, cols]) + bt_ref[:, g:g + 1]
                a_s[rows, cols] = gu[rows, cols] * mixed
        a_ref[...] = a_s[...].astype(BF16)

    row = lambda i: (0, 0)
    return _call(
        body, name="gmlp_fwd", grid=(T // tm,), job=job, args=(proj, proj, ln_g, ln_b, wm, b_t),
        in_specs=[pl.BlockSpec((None, tm, D), lambda i: (0, i, 0)), pl.BlockSpec((None, tm, D), lambda i: (1, i, 0)),
                  pl.BlockSpec((1, D), row), pl.BlockSpec((1, D), row),
                  pl.BlockSpec((NG, GCH, GCH), lambda i: (0, 0, 0)), pl.BlockSpec((GCH, NG), row)],
        out_specs=[pl.BlockSpec((tm, D), lambda i: (i, 0))],
        out_shape=[SDS((T, D), BF16)],
        scratch_shapes=[pltpu.VMEM((tm, D), F32)])


def _cumsum64(x, row):
    for s in (1, 2, 4, 8, 16, 32):
        x = x + jnp.where(row >= s, pltpu.roll(x, s, 0), 0.0)
    return x


def _revcumsum64(x, row):
    n = x.shape[0]
    for s in (1, 2, 4, 8, 16, 32):
        x = x + jnp.where(row < HCH - s, pltpu.roll(x, n - s, 0), 0.0)
    return x


def _head_mean(x):
    parts = [jnp.broadcast_to(_mean(x[:, HD * h:HD * (h + 1)]), (x.shape[0], HD)) for h in range(x.shape[1] // HD)]
    return jnp.concatenate(parts, axis=1)


def _seg_sum(x):
    n, c = x.shape
    s = jnp.sum(x.reshape(n // HCH, HCH, c), axis=1, keepdims=True)
    return jnp.broadcast_to(s, (n // HCH, HCH, c)).reshape(n, c)


def _hgrn_gates(fl, lbv, row):
    s = _sigmoid(fl)
    f = lbv + (1.0 - lbv) * s
    a = _cumsum64(jnp.log(f), row)
    a_mid = _seg_sum(jnp.where(row == HCH // 2 - 1, a, 0.0))
    a_last = _seg_sum(jnp.where(row == HCH - 1, a, 0.0))
    return s, f, a, a_mid, a_last


def _hgrn_fwd(proj, lb_table, norm_g, job=None):
    T = proj.shape[1]
    tb = min(512, T)
    nc = tb // HCH

    def body(q_ref, fl_ref, v_ref, g_ref, lbt_ref, gn_ref, o_ref, ob_ref, stb_ref, st_s, o_s):
        @pl.when(pl.program_id(1) == 0)
        def _():
            st_s[...] = jnp.zeros_like(st_s)

        row = lax.broadcasted_iota(jnp.int32, (tb, HW), 0) & (HCH - 1)
        lbv = _sigmoid(lbt_ref[0:1, :] - lbt_ref[1:2, :])
        _, f, a, a_mid, a_last = _hgrn_gates(fl_ref[...], lbv, row)
        k = 1.0 - f
        qs = q_ref[...] * QSCALE
        q_in = (qs * jnp.exp(a - a_mid)).astype(BF16)
        k_in = (k * jnp.exp(a_mid - a)).astype(BF16)
        q_a = (qs * jnp.exp(a)).astype(BF16)
        k_d = (k * jnp.exp(a_last - a)).astype(BF16)
        dec = jnp.exp(a_last)
        vb = v_ref[...].astype(BF16)
        tri = (lax.broadcasted_iota(jnp.int32, (HCH, HCH), 0)
               >= lax.broadcasted_iota(jnp.int32, (HCH, HCH), 1))
        for c in range(nc):
            sl = slice(HCH * c, HCH * (c + 1))
            for hh in range(HGRN_HB):
                hs = slice(HD * hh, HD * (hh + 1))
                st = st_s[hh]
                stb_ref[hh, c] = st
                sc = jnp.where(tri, _mm_nt(q_in[sl, hs], k_in[sl, hs]), 0.0)
                o_s[sl, hs] = _mm(sc.astype(BF16), vb[sl, hs]) + _mm_nt(q_a[sl, hs], st.astype(BF16))
                d64 = dec[sl, hs]
                st_s[hh] = st * jnp.concatenate([d64, d64], axis=0) + _mm_tn(vb[sl, hs], k_d[sl, hs])
        o = o_s[...]
        r = lax.rsqrt(_head_mean(o * o) + EPS)
        g = g_ref[...]
        o_ref[...] = o
        ob_ref[...] = (o * r * gn_ref[...] * (g * _sigmoid(g))).astype(BF16)

    def col(off):
        return pl.BlockSpec((None, tb, HW), lambda h, cb: (off, cb, h))

    return _call(
        body, name="hgrn_fwd", grid=(NH // HGRN_HB, T // tb), job=job,
        args=(proj, proj, proj, proj, lb_table, norm_g),
        in_specs=[col(2), col(3), col(4), col(5),
                  pl.BlockSpec((2, HW), lambda h, cb: (0, h)), pl.BlockSpec((1, HW), lambda h, cb: (0, h))],
        out_specs=[pl.BlockSpec((tb, HW), lambda h, cb: (cb, h)), pl.BlockSpec((tb, HW), lambda h, cb: (cb, h)),
                   pl.BlockSpec((HGRN_HB, nc, HD, HD), lambda h, cb: (h, cb, 0, 0))],
        out_shape=[SDS((T, D), F32), SDS((T, D), BF16), SDS((NH, T // HCH, HD, HD), F32)],
        scratch_shapes=[pltpu.VMEM((HGRN_HB, HD, HD), F32), pltpu.VMEM((tb, HW), F32)])


def _merge_fwd(x, ab, ob, proj, w_a, w_b, w_out, job=None):
    T = x.shape[0]
    tm = min(512, T)

    def body(x_ref, ab_ref, ob_ref, ga_ref, gb_ref, wa_ref, wb_ref, wo_ref, mg_ref, x1_ref):
        ya = _mm(ab_ref[...], wa_ref[...])
        yb = _mm(ob_ref[...], wb_ref[...])
        merged = (_sigmoid(ga_ref[...]) * ya + _sigmoid(gb_ref[...]) * yb).astype(BF16)
        mg_ref[...] = merged
        x1_ref[...] = x_ref[...] + _mm(merged, wo_ref[...])

    t = lambda i: (i, 0)
    w = lambda i: (0, 0)
    return _call(
        body, name="merge_fwd", grid=(T // tm,), job=job, args=(x, ab, ob, proj, proj, w_a, w_b, w_out),
        in_specs=[pl.BlockSpec((tm, D), t), pl.BlockSpec((tm, D), t), pl.BlockSpec((tm, D), t),
                  pl.BlockSpec((None, tm, D), lambda i: (6, i, 0)), pl.BlockSpec((None, tm, D), lambda i: (7, i, 0)),
                  pl.BlockSpec((D, D), w), pl.BlockSpec((D, D), w), pl.BlockSpec((D, D), w)],
        out_specs=[pl.BlockSpec((tm, D), t)] * 2,
        out_shape=[SDS((T, D), BF16), SDS((T, D), F32)])


def _ffn_fwd_bwd(x1, target, g_ffn, g_fin, w_gu4, w_down):
    T = x1.shape[0]
    tm = min(256, T)
    inv_d = 1.0 / D

    def body(x1_ref, tg_ref, gf_ref, gn_ref, wgu_ref, wd_ref,
             act_ref, dx2b_ref, h2b_ref, dgu_ref, dx1_ref, dx1b_ref, acc_ref):
        @pl.when(pl.program_id(0) == 0)
        def _():
            acc_ref[...] = jnp.zeros_like(acc_ref)

        x1v = x1_ref[...]
        gf = gf_ref[...]
        gn = gn_ref[...]
        rr1 = lax.rsqrt(_mean(x1v * x1v) + EPS)
        x1n = x1v * rr1
        h2b = (x1n * gf).astype(BF16)
        h2b_ref[...] = h2b
        p = [_mm(h2b, wgu_ref[k]) for k in range(NCHIP)]
        sg = [_sigmoid(p[0]), _sigmoid(p[1])]
        si = [p[0] * sg[0], p[1] * sg[1]]
        x2 = x1v
        for k in range(2):
            actk = (si[k] * p[2 + k]).astype(BF16)
            act_ref[:, FFS * k:FFS * (k + 1)] = actk
            x2 = x2 + _mm(actk, wd_ref[FFS * k:FFS * (k + 1), :])
        rr2 = lax.rsqrt(_mean(x2 * x2) + EPS)
        x2n = x2 * rr2
        e = x2n * gn - tg_ref[...]
        acc_ref[0] += _rows8(e * e) * (0.5 * inv_d)
        dy = e * inv_d
        acc_ref[1] += _rows8(dy * x2n)
        dxn = dy * gn
        dx2 = rr2 * (dxn - x2n * _mean(dxn * x2n))
        dx2b = dx2.astype(BF16)
        dx2b_ref[...] = dx2b
        dh2 = None
        for k in range(2):
            dact = _mm_nt(dx2b, wd_ref[FFS * k:FFS * (k + 1), :])
            dgate = (dact * p[2 + k] * (sg[k] * (1.0 + p[k] * (1.0 - sg[k])))).astype(BF16)
            dup = (dact * si[k]).astype(BF16)
            dgu_ref[k] = dgate
            dgu_ref[2 + k] = dup
            part = _mm_nt(dgate, wgu_ref[k]) + _mm_nt(dup, wgu_ref[2 + k])
            dh2 = part if dh2 is None else dh2 + part
        acc_ref[2] += _rows8(dh2 * x1n)
        dxn1 = dh2 * gf
        dx1 = dx2 + rr1 * (dxn1 - x1n * _mean(dxn1 * x1n))
        dx1_ref[...] = dx1
        dx1b_ref[...] = dx1.astype(BF16)

    t = lambda i: (i, 0)
    w = lambda i: (0, 0)
    one = pl.Buffered(1)
    return pl.pallas_call(
        body, name="ffn_fwd_bwd", grid=(T // tm,),
        in_specs=[pl.BlockSpec((tm, D), t), pl.BlockSpec((tm, D), t),
                  pl.BlockSpec((1, D), w), pl.BlockSpec((1, D), w),
                  pl.BlockSpec((NCHIP, D, FFS), lambda i: (0, 0, 0), pipeline_mode=one),
                  pl.BlockSpec((FF, D), w, pipeline_mode=one)],
        out_specs=[pl.BlockSpec((tm, FF), t), pl.BlockSpec((tm, D), t), pl.BlockSpec((tm, D), t),
                   pl.BlockSpec((NCHIP, tm, FFS), lambda i: (0, i, 0)),
                   pl.BlockSpec((tm, D), t), pl.BlockSpec((tm, D), t),
                   pl.BlockSpec((3, 8, D), lambda i: (0, 0, 0))],
        out_shape=[SDS((T, FF), BF16), SDS((T, D), BF16), SDS((T, D), BF16),
                   SDS((NCHIP, T, FFS), BF16), SDS((T, D), F32), SDS((T, D), BF16),
                   SDS((3, 8, D), F32)],
        compiler_params=_cparams(),
    )(x1, target, g_ffn, g_fin, w_gu4, w_down)


def _merge_bwd(dx1b, ab, ob, proj, w_out, w_a, w_b, job=None):
    T = dx1b.shape[0]
    tm = min(512, T)

    def body(dx_ref, ab_ref, ob_ref, ga_ref, gb_ref, wo_ref, wa_ref, wb_ref, dya_ref, dyb_ref, dp_ref):
        dm = _mm_nt(dx_ref[...], wo_ref[...])
        sa = _sigmoid(ga_ref[...])
        sb = _sigmoid(gb_ref[...])
        dya_ref[...] = (dm * sa).astype(BF16)
        dyb_ref[...] = (dm * sb).astype(BF16)
        dp_ref[0] = (dm * _mm(ab_ref[...], wa_ref[...]) * sa * (1.0 - sa)).astype(BF16)
        dp_ref[1] = (dm * _mm(ob_ref[...], wb_ref[...]) * sb * (1.0 - sb)).astype(BF16)

    t = lambda i: (i, 0)
    w = lambda i: (0, 0)
    return _call(
        body, name="merge_bwd", grid=(T // tm,),
        in_specs=[pl.BlockSpec((tm, D), t), pl.BlockSpec((tm, D), t), pl.BlockSpec((tm, D), t),
                  pl.BlockSpec((None, tm, D), lambda i: (6, i, 0)), pl.BlockSpec((None, tm, D), lambda i: (7, i, 0)),
                  pl.BlockSpec((D, D), w), pl.BlockSpec((D, D), w), pl.BlockSpec((D, D), w)],
        out_specs=[pl.BlockSpec((tm, D), t)] * 2 + [pl.BlockSpec((2, tm, D), lambda i: (3, i, 0))],
        out_shape=[SDS((T, D), BF16), SDS((T, D), BF16), SDS((NIN, T, D), BF16)],
        args=(dx1b, ab, ob, proj, proj, w_out, w_a, w_b), job=job)


def _hgrn_bwd(dproj, dyb, w_b, o_raw, proj, st_before, lb_table, norm_g, job=None):
    T = dyb.shape[0]
    tb = min(512, T)
    nc = tb // HCH
    nb = T // tb

    def body(dp_in, dyb_ref, wb_ref, o_ref, q_ref, fl_ref, v_ref, g_ref, stb_ref, lbt_ref, gn_ref,
             dp_ref, acc_ref, dst_s, dqin_s, dqa_s, dkin_s, dkd_s, dv_s, ddec_s):
        del dp_in

        @pl.when(pl.program_id(1) == 0)
        def _():
            dst_s[...] = jnp.zeros_like(dst_s)
            acc_ref[...] = jnp.zeros_like(acc_ref)

        row = lax.broadcasted_iota(jnp.int32, (tb, HW), 0) & (HCH - 1)
        gn = gn_ref[...]
        lbv = _sigmoid(lbt_ref[0:1, :] - lbt_ref[1:2, :])
        o = o_ref[...]
        r = lax.rsqrt(_head_mean(o * o) + EPS)
        on = o * r
        g = g_ref[...]
        sgm = _sigmoid(g)
        dob_v = _mm_nt(dyb_ref[...], wb_ref[...])
        dp_ref[3] = (dob_v * on * gn * (sgm * (1.0 + g * (1.0 - sgm)))).astype(BF16)
        do_n = dob_v * (g * sgm)
        acc_ref[1] += _rows8(do_n * on)
        dxn = do_n * gn
        do = (r * (dxn - on * _head_mean(dxn * on))).astype(BF16)
        s, f, a, a_mid, a_last = _hgrn_gates(fl_ref[...], lbv, row)
        k = 1.0 - f
        qs = q_ref[...] * QSCALE
        e_q = jnp.exp(a - a_mid)
        e_k = jnp.exp(a_mid - a)
        e_a = jnp.exp(a)
        e_l = jnp.exp(a_last - a)
        dec = jnp.exp(a_last)
        q_in = qs * e_q
        k_in = k * e_k
        q_a = qs * e_a
        k_d = k * e_l
        q_inb, k_inb, q_ab, k_db = (z.astype(BF16) for z in (q_in, k_in, q_a, k_d))
        vb = v_ref[...].astype(BF16)
        tri = (lax.broadcasted_iota(jnp.int32, (HCH, HCH), 0)
               >= lax.broadcasted_iota(jnp.int32, (HCH, HCH), 1))
        for c in reversed(range(nc)):
            sl = slice(HCH * c, HCH * (c + 1))
            for hh in range(HGRN_HB):
                hs = slice(HD * hh, HD * (hh + 1))
                stp = stb_ref[hh, c]
                dst = dst_s[hh]
                dstb = dst.astype(BF16)
                do_c = do[sl, hs]
                v_c = vb[sl, hs]
                dqa_s[sl, hs] = _mm(do_c, stp.astype(BF16))
                dkd_s[sl, hs] = _mm(v_c, dstb)
                ddec_s[sl, hs] = jnp.broadcast_to(jnp.sum(dst * stp, axis=0, keepdims=True), (HCH, HD))
                sc = jnp.where(tri, _mm_nt(q_inb[sl, hs], k_inb[sl, hs]), 0.0).astype(BF16)
                dsc = jnp.where(tri, _mm_nt(do_c, v_c), 0.0).astype(BF16)
                dv_s[sl, hs] = _mm_nt(k_db[sl, hs], dstb) + _mm_tn(sc, do_c)
                dqin_s[sl, hs] = _mm(dsc, k_inb[sl, hs])
                dkin_s[sl, hs] = _mm_tn(dsc, q_inb[sl, hs])
                d64 = dec[sl, hs]
                dst_s[hh] = dst * jnp.concatenate([d64, d64], axis=0) + _mm_tn(do_c, q_ab[sl, hs])
        dq_in = dqin_s[...]
        dq_a = dqa_s[...]
        dk_in = dkin_s[...]
        dk_d = dkd_s[...]
        dp_ref[0] = ((dq_in * e_q + dq_a * e_a) * QSCALE).astype(BF16)
        dp_ref[2] = dv_s[...].astype(BF16)
        tq = dq_in * q_in
        tk = dk_in * k_in
        td = dk_d * k_d
        d_a = tq + dq_a * q_a - tk - td
        d_a = d_a + jnp.where(row == HCH // 2 - 1, _seg_sum(tk - tq), 0.0)
        d_a = d_a + jnp.where(row == HCH - 1, _seg_sum(td) + ddec_s[...] * dec, 0.0)
        dlf = _revcumsum64(d_a, row)
        df = dlf / f - (dk_in * e_k + dk_d * e_l)
        dp_ref[1] = (df * (1.0 - lbv) * s * (1.0 - s)).astype(BF16)
        acc_ref[0] += _rows8(df * (1.0 - s))

    def col(off):
        return pl.BlockSpec((None, tb, HW), lambda h, cb: (off, nb - 1 - cb, h))

    hb = lambda h, cb: (nb - 1 - cb, h)
    return _call(
        body, name="hgrn_bwd", grid=(NH // HGRN_HB, nb), job=job,
        args=(dproj, dyb, w_b, o_raw, proj, proj, proj, proj, st_before, lb_table, norm_g),
        in_specs=[ANY, pl.BlockSpec((tb, D), lambda h, cb: (nb - 1 - cb, 0)),
                  pl.BlockSpec((HW, D), lambda h, cb: (h, 0)), pl.BlockSpec((tb, HW), hb),
                  col(2), col(3), col(4), col(5),
                  pl.BlockSpec((HGRN_HB, nc, HD, HD), lambda h, cb: (h, nb - 1 - cb, 0, 0)),
                  pl.BlockSpec((2, HW), lambda h, cb: (0, h)), pl.BlockSpec((1, HW), lambda h, cb: (0, h))],
        out_specs=[pl.BlockSpec((4, tb, HW), lambda h, cb: (0, nb - 1 - cb, h)),
                   pl.BlockSpec((2, 8, HW), lambda h, cb: (0, 0, h))],
        out_shape=[SDS(dproj.shape, BF16), SDS((2, 8, D), F32)],
        scratch_shapes=[pltpu.VMEM((HGRN_HB, HD, HD), F32)] + [pltpu.VMEM((tb, HW), F32)] * 6,
        aliases={0: 0})


def _gmlp_bwd(dproj, dya, w_a, proj, ln_g, ln_b, wm, wm_t, b_t):
    T = dya.shape[0]
    tm = min(256, T)

    def body(dp_in, dya_ref, wa_ref, u_ref, v_ref, lg_ref, lb_ref, wm_ref, wmt_ref, bt_ref,
             dp_ref, acc_ref, dws_ref, dmix_ref, du_s, dvn_s):
        del dp_in

        @pl.when(pl.program_id(0) == 0)
        def _():
            acc_ref[...] = jnp.zeros_like(acc_ref)
            dws_ref[...] = jnp.zeros_like(dws_ref)
            dmix_ref[...] = jnp.zeros_like(dmix_ref)

        u = u_ref[...]
        v = v_ref[...]
        lg = lg_ref[...]
        gu, t_u = _gelu(u)
        gv, t_v = _gelu(v)
        vhat, rs = _layer_norm_stats(gv)
        vnb = (vhat * lg + lb_ref[...]).astype(BF16)
        da_v = _mm_nt(dya_ref[...], wa_ref[...])
        for ch in range(tm // GCH):
            rows = slice(GCH * ch, GCH * (ch + 1))
            for g in range(NG):
                cols = slice(128 * g, 128 * (g + 1))
                vng = vnb[rows, cols]
                mixed = _mm(wm_ref[g], vng) + bt_ref[:, g:g + 1]
                dag = da_v[rows, cols]
                dmx = dag * gu[rows, cols]
                du_s[rows, cols] = dag * mixed
                dmxb = dmx.astype(BF16)
                dws_ref[:, cols] += _mm_nt(dmxb, vng)
                dmix_ref[:, cols] += dmx
                dvn_s[rows, cols] = _mm(wmt_ref[g], dmxb)
        dp_ref[0] = (du_s[...] * _gelu_grad(u, t_u)).astype(BF16)
        dvn = dvn_s[...]
        acc_ref[0] += _rows8(dvn * vhat)
        acc_ref[1] += _rows8(dvn)
        dvh = dvn * lg
        dgv = rs * (dvh - _mean(dvh) - vhat * _mean(dvh * vhat))
        dp_ref[1] = (dgv * _gelu_grad(v, t_v)).astype(BF16)

    row = lambda i: (0, 0)
    w3 = lambda i: (0, 0, 0)
    return pl.pallas_call(
        body, name="gmlp_bwd", grid=(T // tm,),
        in_specs=[ANY, pl.BlockSpec((tm, D), lambda i: (i, 0)), pl.BlockSpec((D, D), row),
                  pl.BlockSpec((None, tm, D), lambda i: (0, i, 0)), pl.BlockSpec((None, tm, D), lambda i: (1, i, 0)),
                  pl.BlockSpec((1, D), row), pl.BlockSpec((1, D), row),
                  pl.BlockSpec((NG, GCH, GCH), w3), pl.BlockSpec((NG, GCH, GCH), w3),
                  pl.BlockSpec((GCH, NG), row)],
        out_specs=[pl.BlockSpec((2, tm, D), lambda i: (2, i, 0)),
                   pl.BlockSpec((2, 8, D), w3), pl.BlockSpec((GCH, D), row), pl.BlockSpec((GCH, D), row)],
        out_shape=[SDS(dproj.shape, BF16), SDS((2, 8, D), F32), SDS((GCH, D), F32), SDS((GCH, D), F32)],
        scratch_shapes=[pltpu.VMEM((tm, D), F32), pltpu.VMEM((tm, D), F32)],
        input_output_aliases={0: 0},
        compiler_params=_cparams(),
    )(dproj, dya, w_a, proj, proj, ln_g, ln_b, wm, wm_t, b_t)


def _proj_bwd(dproj, w_in4, x, dx1, g_mix, job=None):
    T = x.shape[0]
    tm = min(256, T)
    order = (2, 3, 4, 5, 0, 1, 6, 7)

    def body(dp_ref, w_ref, x_ref, dx1_ref, g_ref, gx_ref, acc_ref):
        @pl.when(pl.program_id(0) == 0)
        def _():
            acc_ref[...] = jnp.zeros_like(acc_ref)

        dh = None
        for m, og in enumerate(order):
            part = _mm_nt(dp_ref[m], w_ref[og // 2, :, D * (og % 2):D * (og % 2 + 1)])
            dh = part if dh is None else dh + part
        xv = x_ref[...]
        r = lax.rsqrt(_mean(xv * xv) + EPS)
        xn = xv * r
        acc_ref[...] += _rows8(dh * xn)
        dxn = dh * g_ref[...]
        gx_ref[...] = dx1_ref[...] + r * (dxn - xn * _mean(dxn * xn))

    t = lambda i: (i, 0)
    return _call(
        body, name="proj_bwd", grid=(T // tm,),
        in_specs=[pl.BlockSpec((NIN, tm, D), lambda i: (0, i, 0)),
                  pl.BlockSpec((NCHIP, D, 2 * D), lambda i: (0, 0, 0), pipeline_mode=pl.Buffered(1)),
                  pl.BlockSpec((tm, D), t), pl.BlockSpec((tm, D), t), pl.BlockSpec((1, D), lambda i: (0, 0))],
        out_specs=[pl.BlockSpec((tm, D), t), pl.BlockSpec((8, D), lambda i: (0, 0))],
        out_shape=[SDS((T, D), F32), SDS((8, D), F32)],
        args=(dproj, w_in4, x, dx1, g_mix), job=job)


def _dw_call(name, a, b, a_spec, b_spec, o_spec, out_shape, nblk, tt, job=None, prefetch=None):
    T = a.shape[-2]

    def body(*refs):
        a_ref, b_ref, o_ref = refs[-3:]

        @pl.when(pl.program_id(1) == 0)
        def _():
            o_ref[...] = jnp.zeros_like(o_ref)
        o_ref[...] += _mm_tn(a_ref[...], b_ref[...])

    (out,), job_out = _call(
        body, name=name, grid=(nblk, T // tt), in_specs=[a_spec, b_spec], out_specs=[o_spec],
        out_shape=[out_shape], args=(a, b), job=job, prefetch=prefetch)
    return out, job_out


def _dw_in_half(name, place, hb, dproj, mine, job=None):
    tt = min(DW_TOKENS, hb.shape[0])

    def comp(k, pc):
        return _component_of(2 * k + (pc[1] if mine else 1 - pc[1]))

    return _dw_call(
        name, hb, dproj,
        pl.BlockSpec((tt, D), lambda k, t, pc: (t, 0)),
        pl.BlockSpec((None, tt, D), lambda k, t, pc: (comp(k, pc), t, 0)),
        pl.BlockSpec((None, D, D), lambda k, t, pc: (k, 0, 0)),
        SDS((NCHIP, D, D), F32), NCHIP, tt, job, place)


def _dw_gate_up(h2b, dgu4, job=None):
    tt = min(DW_TOKENS, h2b.shape[0])
    return _dw_call(
        "dw_gate_up", h2b, dgu4,
        pl.BlockSpec((tt, D), lambda k, t: (t, 0)),
        pl.BlockSpec((None, tt, FFS), lambda k, t: (k, t, 0)),
        pl.BlockSpec((None, D, FFS), lambda k, t: (k, 0, 0)),
        SDS((NCHIP, D, FFS), F32), NCHIP, tt, job)


def _dw_down(act, dx2b, job=None):
    tt = min(DW_TOKENS, act.shape[0])
    g, job_out = _dw_call(
        "dw_down", act, dx2b,
        pl.BlockSpec((tt, FFS), lambda k, t: (t, k)),
        pl.BlockSpec((tt, D), lambda k, t: (t, 0)),
        pl.BlockSpec((FFS, D), lambda k, t: (k, 0)),
        SDS((FF, D), F32), 2, tt, job)
    return g.reshape(NCHIP, FF // NCHIP, D), job_out


def _dw_square(name, a, b, job=None):
    tt = min(DW_TOKENS, a.shape[0])
    g, job_out = _dw_call(
        name, a, b,
        pl.BlockSpec((tt, D), lambda k, t: (t, 0)), pl.BlockSpec((tt, D), lambda k, t: (t, 0)),
        pl.BlockSpec((D, D), lambda k, t: (0, 0)), SDS((D, D), F32), 1, tt, job)
    return g.reshape(NCHIP, D // NCHIP, D), job_out


def _place():
    x, y, c = lax.axis_index("x"), lax.axis_index("y"), lax.axis_index("c")
    return x, y, c, 2 * x + y


def _chip_at(x, y, s):
    return x ^ (s >> 1), y ^ (s & 1)


class _Job:
    def __init__(self, ins, out_shapes, sems, start, finish, aliases=None, mid=None):
        self.ins, self.out_shapes, self.sems = list(ins), list(out_shapes), list(sems)
        self.start, self.finish, self.aliases = start, finish, dict(aliases or {})
        self.mid = mid if mid is not None else (lambda ins, outs, sems: None)


def _join_jobs(*jobs):
    def cut(refs, sizes):
        out, at = [], 0
        for n in sizes:
            out.append(refs[at:at + n])
            at += n
        return out

    ni = [len(j.ins) for j in jobs]
    no = [len(j.out_shapes) for j in jobs]
    ns = [len(j.sems) for j in jobs]

    def run(which):
        def go(ins, outs, sems):
            for j, a, b, c in zip(jobs, cut(ins, ni), cut(outs, no), cut(sems, ns)):
                getattr(j, which)(a, b, c)
        return go

    aliases = {}
    for k, j in enumerate(jobs):
        for a, b in j.aliases.items():
            aliases[sum(ni[:k]) + a] = sum(no[:k]) + b
    return _Job([a for j in jobs for a in j.ins], [o for j in jobs for o in j.out_shapes],
                [s for j in jobs for s in j.sems], run("start"), run("finish"), aliases, run("mid"))


def _call(body, *, name, grid, in_specs, out_specs, out_shape, args, scratch_shapes=(), aliases=None,
          job=None, prefetch=None):
    n_in, n_out, n_scr = len(in_specs), len(out_specs), len(scratch_shapes)
    npf = 0 if prefetch is None else 1
    job = job if job is not None else _Job([], [], [], lambda *a: None, lambda *a: None)
    ji, jo = len(job.ins), len(job.out_shapes)
    steps = math.prod(grid)

    def wrapped(*refs):
        pf, refs = refs[:npf], refs[npf:]
        ins, jin = refs[:n_in], refs[n_in:n_in + ji]
        o0 = n_in + ji
        outs, jout = refs[o0:o0 + n_out], refs[o0 + n_out:o0 + n_out + jo]
        s0 = o0 + n_out + jo
        scr, jsem = refs[s0:s0 + n_scr], refs[s0 + n_scr:]
        step = functools.reduce(lambda acc, ag: acc * ag[1] + pl.program_id(ag[0]), enumerate(grid), 0)
        if ji or jo:
            @pl.when(step == 0)
            def _():
                job.start(jin, jout, jsem)

        body(*pf, *ins, *outs, *scr)

        if ji or jo:
            @pl.when(step == steps // 2)
            def _():
                job.mid(jin, jout, jsem)

            @pl.when(step == steps - 1)
            def _():
                job.finish(jin, jout, jsem)

    io = {npf + a: b for a, b in dict(aliases or {}).items()}
    io.update({npf + n_in + a: n_out + b for a, b in job.aliases.items()})
    kw = dict(in_specs=list(in_specs) + [ANY] * ji, out_specs=list(out_specs) + [ANY] * jo,
              scratch_shapes=list(scratch_shapes) + job.sems)
    if npf:
        kw = dict(grid_spec=pltpu.PrefetchScalarGridSpec(num_scalar_prefetch=1, grid=grid, **kw))
    else:
        kw["grid"] = grid
    res = pl.pallas_call(
        wrapped, name=name, out_shape=list(out_shape) + job.out_shapes, input_output_aliases=io,
        compiler_params=_cparams(has_side_effects=bool(ji or jo)), **kw,
    )(*(() if prefetch is None else (prefetch,)), *args, *job.ins)
    return list(res[:n_out]), list(res[n_out:])


def _run_job(job, name):
    ji, jo = len(job.ins), len(job.out_shapes)

    def body(*refs):
        jin, jout, jsem = refs[:ji], refs[ji:ji + jo], refs[ji + jo:]
        job.start(jin, jout, jsem)
        job.finish(jin, jout, jsem)

    return list(pl.pallas_call(
        body, name=name, in_specs=[ANY] * ji, out_specs=[ANY] * jo, out_shape=job.out_shapes,
        scratch_shapes=job.sems, input_output_aliases=job.aliases,
        compiler_params=pltpu.CompilerParams(has_side_effects=True))(*job.ins))


def _cast_shard(name, place, w):
    rows, cols = w.shape
    tr = 352 if rows % 352 == 0 else 256

    def body(pc_ref, w_ref, o_ref):
        del pc_ref
        o_ref[...] = w_ref[...].astype(BF16)

    return pl.pallas_call(
        body, name=name,
        grid_spec=pltpu.PrefetchScalarGridSpec(
            num_scalar_prefetch=1, grid=(rows // tr,),
            in_specs=[pl.BlockSpec((tr, cols), lambda i, pc: (i, 0))],
            out_specs=pl.BlockSpec((None, tr, cols), lambda i, pc: (pc[0], i, 0))),
        out_shape=SDS((NCHIP, rows, cols), BF16),
        compiler_params=_cparams(),
    )(place, w)


def _sibling_copy(ref, send_sem, recv_sem):
    x, y, c, _ = _place()
    return pltpu.make_async_remote_copy(src_ref=ref, dst_ref=ref, send_sem=send_sem, recv_sem=recv_sem,
                                        device_id=(x, y, 1 - c), device_id_type=MESH)


def _half_rows(arr, slot, core):
    half = arr.shape[1] // 2
    return arr.at[slot, pl.ds(pl.multiple_of(core * half, 16), half)]


def _quarter_rows(arr, slot, core, q):
    quarter = arr.shape[1] // 4
    return arr.at[slot, pl.ds(pl.multiple_of((2 * core + q) * quarter, 16), quarter)]


def _chip_copy(ref, dist, send_sem, recv_sem):
    x, y, c, _ = _place()
    cx, cy = _chip_at(x, y, dist)
    return pltpu.make_async_remote_copy(src_ref=ref, dst_ref=ref, send_sem=send_sem, recv_sem=recv_sem,
                                        device_id=(cx, cy, c), device_id_type=MESH)


def _gather_sems(n):
    dma = pltpu.SemaphoreType.DMA
    return [dma((n, 2))] * 4 + [dma((n, 4))] * 2


def _gather_start(arrs, sems):
    dsend, drecv = sems[0], sems[1]
    _, _, c, j = _place()
    for w, arr in enumerate(arrs):
        for dist in (1, 2):
            _chip_copy(_half_rows(arr, j, c), dist, dsend.at[w, dist - 1], drecv.at[w, dist - 1]).start()


def _gather_land(arrs, sems, dist, first=0):
    dsend, drecv, rsend, rrecv, fsend, frecv = sems
    _, _, c, j = _place()
    if dist < 3:
        other = 3 - dist
        for w, arr in enumerate(arrs, first):
            landed = _half_rows(arr, j ^ dist, c)
            _chip_copy(landed, dist, dsend.at[w, dist - 1], drecv.at[w, dist - 1]).wait_recv()
            relay = _quarter_rows(arr, j ^ dist, c, other - 1)
            _chip_copy(relay, other, rsend.at[w, other - 1], rrecv.at[w, other - 1]).start()
            _sibling_copy(landed, fsend.at[w, dist - 1], frecv.at[w, dist - 1]).start()
        for w, arr in enumerate(arrs, first):
            theirs = _half_rows(arr, j ^ dist, 1 - c)
            _sibling_copy(theirs, fsend.at[w, dist - 1], frecv.at[w, dist - 1]).wait_recv()
    else:
        for w, arr in enumerate(arrs, first):
            for via in (1, 2):
                piece = _quarter_rows(arr, j ^ 3, c, via - 1)
                _chip_copy(piece, via, rsend.at[w, via - 1], rrecv.at[w, via - 1]).wait_recv()
                _sibling_copy(piece, fsend.at[w, 1 + via], frecv.at[w, 1 + via]).start()
        for w, arr in enumerate(arrs, first):
            for via in (1, 2):
                theirs = _quarter_rows(arr, j ^ 3, 1 - c, via - 1)
                _sibling_copy(theirs, fsend.at[w, 1 + via], frecv.at[w, 1 + via]).wait_recv()


def _gather_drain(arrs, sems):
    dsend, drecv, rsend, rrecv, fsend, frecv = sems
    _, _, c, j = _place()
    for w, arr in enumerate(arrs):
        for dist in (1, 2):
            other = 3 - dist
            _chip_copy(_half_rows(arr, j, c), dist, dsend.at[w, dist - 1], drecv.at[w, dist - 1]).wait_send()
            _chip_copy(_quarter_rows(arr, j ^ dist, c, other - 1), other,
                       rsend.at[w, other - 1], rrecv.at[w, other - 1]).wait_send()
            _sibling_copy(_half_rows(arr, j ^ dist, c), fsend.at[w, dist - 1], frecv.at[w, dist - 1]).wait_send()
            _sibling_copy(_quarter_rows(arr, j ^ 3, c, dist - 1),
                          fsend.at[w, 1 + dist], frecv.at[w, 1 + dist]).wait_send()


def _gather_neighbours(arrs, sems):
    _gather_land(arrs, sems, 1)
    _gather_land(arrs, sems, 2)


def _gather_finish(arrs, sems):
    _gather_land(arrs, sems, 3)
    _gather_drain(arrs, sems)


def _gather_job(arrs):
    n = len(arrs)
    return _Job(arrs, [SDS(a.shape, a.dtype) for a in arrs], _gather_sems(n),
                lambda ins, outs, sems: _gather_start(outs, sems),
                lambda ins, outs, sems: _gather_finish(outs, sems), {k: k for k in range(n)},
                mid=lambda ins, outs, sems: _gather_neighbours(outs, sems))


def _exchange_job(arrs, out_shapes, n, copies):
    def start(ins, outs, sems):
        for cp in copies(ins, outs, sems[0], sems[1]):
            cp.start()

    def finish(ins, outs, sems):
        for cp in copies(ins, outs, sems[0], sems[1]):
            cp.wait()

    return _Job(arrs, out_shapes, [pltpu.SemaphoreType.DMA((n,))] * 2, start, finish)


def _pair_exchange_job(grads):
    def copies(ins, outs, send_sem, recv_sem):
        x, y, c, _ = _place()
        res = []
        for w in range(len(grads)):
            half = ins[w].shape[1] // 2
            theirs = pl.ds(pl.multiple_of((1 - c) * half, 8), half)
            res.append(pltpu.make_async_remote_copy(
                src_ref=ins[w].at[:, theirs, :], dst_ref=outs[w], send_sem=send_sem.at[w],
                recv_sem=recv_sem.at[w], device_id=(x, y, 1 - c), device_id_type=MESH))
        return res

    return _exchange_job(grads, [SDS((NCHIP, g.shape[1] // 2, g.shape[2]), F32) for g in grads],
                         len(grads), copies)


def _row_tile(rows, cols):
    tr = rows
    while tr * cols * 4 > ELEMENTWISE_BLOCK_BYTES and tr % 32 == 0:
        tr //= 2
    return tr


def _pair_sum(name, place, g, sib):
    half, cols = sib.shape[1], sib.shape[2]
    tr = _row_tile(half, cols)
    nt = half // tr
    mine = nt if g.shape[1] == 2 * half else 0

    def body(pc_ref, g_ref, s_ref, own_ref, out_ref):
        del pc_ref
        v = g_ref[...] + s_ref[...]
        out_ref[...] = v.astype(BF16)

        @pl.when(pl.program_id(1) == 0)
        def _():
            own_ref[...] = v

    return pl.pallas_call(
        body, name=name,
        grid_spec=pltpu.PrefetchScalarGridSpec(
            num_scalar_prefetch=1, grid=(nt, NCHIP),
            in_specs=[pl.BlockSpec((None, tr, cols), lambda i, s, pc: (pc[0] ^ s, pc[1] * mine + i, 0)),
                      pl.BlockSpec((None, tr, cols), lambda i, s, pc: (pc[0] ^ s, i, 0))],
            out_specs=[pl.BlockSpec((tr, cols), lambda i, s, pc: (i, 0)),
                       pl.BlockSpec((None, tr, cols), lambda i, s, pc: (s, i, 0))]),
        out_shape=[SDS((half, cols), F32), SDS((NCHIP, half, cols), BF16)],
        compiler_params=_cparams(),
    )(place, g, sib)


def _chip_exchange_job(parts):
    def copies(ins, outs, send_sem, recv_sem):
        x, y, c, _ = _place()
        res = []
        for w in range(len(parts)):
            for s in range(1, NCHIP):
                cx, cy = _chip_at(x, y, s)
                k = w * (NCHIP - 1) + s - 1
                res.append(pltpu.make_async_remote_copy(
                    src_ref=ins[w].at[s], dst_ref=outs[w].at[s - 1], send_sem=send_sem.at[k],
                    recv_sem=recv_sem.at[k], device_id=(cx, cy, c), device_id_type=MESH))
        return res

    return _exchange_job(parts, [SDS((NCHIP - 1,) + p.shape[1:], BF16) for p in parts],
                         len(parts) * (NCHIP - 1), copies)


def _chip_sum(name, own, rem):
    half, cols = own.shape
    tr = _row_tile(half, cols)

    def body(own_ref, rem_ref, out_ref):
        out_ref[...] = ((own_ref[...] + rem_ref[0].astype(F32)) + rem_ref[1].astype(F32)) + rem_ref[2].astype(F32)

    return pl.pallas_call(
        body, name=name, grid=(half // tr,),
        in_specs=[pl.BlockSpec((tr, cols), lambda i: (i, 0)),
                  pl.BlockSpec((NCHIP - 1, tr, cols), lambda i: (0, i, 0))],
        out_specs=pl.BlockSpec((tr, cols), lambda i: (i, 0)),
        out_shape=SDS((half, cols), F32),
        compiler_params=_cparams(),
    )(own, rem)


def _share_halves_job(halves):
    def copies(ins, outs, send_sem, recv_sem):
        x, y, c, _ = _place()
        return [pltpu.make_async_remote_copy(
            src_ref=ins[w], dst_ref=outs[w], send_sem=send_sem.at[w], recv_sem=recv_sem.at[w],
            device_id=(x, y, 1 - c), device_id_type=MESH) for w in range(len(halves))]

    return _exchange_job(halves, [SDS(h.shape, F32) for h in halves], len(halves), copies)


def _adamw_math(w, g, m, v):
    m = B1 * m + (1.0 - B1) * g
    v = B2 * v + (1.0 - B2) * (g * g)
    m_hat = m / (1.0 - B1 ** STEP)
    v_hat = v / (1.0 - B2 ** STEP)
    delta = -LR * (m_hat / (jnp.sqrt(v_hat) + AEPS) + WD * w)
    return delta, m, v


def _adamw(name, place, w, own, sib, m, v):
    rows, cols = w.shape
    by_cols = own.shape[0] == rows
    half, pc_cols = (rows, cols // 2) if by_cols else (rows // 2, cols)
    tr = _row_tile(half, pc_cols)
    nt = half // tr

    def body(pc_ref, w_ref, own_ref, sib_ref, m_ref, v_ref, g_ref, d_ref, mo_ref, vo_ref):
        g = jnp.where(pl.program_id(0) == pc_ref[1], own_ref[...], sib_ref[...])
        d, mn, vn = _adamw_math(w_ref[...], g, m_ref[...], v_ref[...])
        g_ref[...] = g
        d_ref[...] = d
        mo_ref[...] = mn
        vo_ref[...] = vn

    full = pl.BlockSpec((tr, pc_cols), (lambda h, i, pc: (i, h)) if by_cols else (lambda h, i, pc: (h * nt + i, 0)))
    part = pl.BlockSpec((tr, pc_cols), lambda h, i, pc: (i, 0))
    return pl.pallas_call(
        body, name=name,
        grid_spec=pltpu.PrefetchScalarGridSpec(
            num_scalar_prefetch=1, grid=(2, nt),
            in_specs=[full, part, part, full, full], out_specs=[full] * 4),
        out_shape=[SDS((rows, cols), F32)] * 4,
        compiler_params=_cparams(),
    )(place, w, own, sib, m, v)


def _small_allreduce_adamw(sp, w, m, v):
    shape = sp.shape

    def body(sp_ref, w_ref, m_ref, v_ref, g_ref, d_ref, mo_ref, vo_ref,
             sib_s, pair_s, chip_s, send_sem, recv_sem):
        x, y, c, j = _place()
        cp = pltpu.make_async_remote_copy(
            src_ref=sp_ref, dst_ref=sib_s, send_sem=send_sem.at[0], recv_sem=recv_sem.at[0],
            device_id=(x, y, 1 - c), device_id_type=MESH)
        cp.start()
        cp.wait()
        pair_s[...] = sp_ref[...] + sib_s[...]
        cps = []
        for s in range(1, NCHIP):
            cx, cy = _chip_at(x, y, s)
            cp = pltpu.make_async_remote_copy(
                src_ref=pair_s, dst_ref=chip_s.at[s], send_sem=send_sem.at[s], recv_sem=recv_sem.at[s],
                device_id=(cx, cy, c), device_id_type=MESH)
            cp.start()
            cps.append(cp)
        chip_s[0] = pair_s[...]
        for cp in cps:
            cp.wait()
        tot = chip_s[j]
        for k in range(1, NCHIP):
            tot = tot + chip_s[k ^ j]
        g_ref[...] = tot
        d, mn, vn = _adamw_math(w_ref[...], tot, m_ref[...], v_ref[...])
        d_ref[...] = d
        mo_ref[...] = mn
        vo_ref[...] = vn

    vm = pl.BlockSpec(memory_space=pltpu.VMEM)
    return pl.pallas_call(
        body, name="small_allreduce_adamw",
        in_specs=[vm] * 4, out_specs=[vm] * 4, out_shape=[SDS(shape, F32)] * 4,
        scratch_shapes=[pltpu.VMEM(shape, F32), pltpu.VMEM(shape, F32), pltpu.VMEM((NCHIP,) + shape, F32),
                        pltpu.SemaphoreType.DMA((NCHIP,)), pltpu.SemaphoreType.DMA((NCHIP,))],
        compiler_params=pltpu.CompilerParams(has_side_effects=True),
    )(sp, w, m, v)


def _pack_small(first, mix, ln_g, ln_b, b_s, lbt, hn, ffn, fin, w_s):
    rows = [first.reshape(1, D), mix.reshape(1, D), ln_g.reshape(1, D), ln_b.reshape(1, D),
            b_s.reshape(1, D), lbt.reshape(2, D), hn.reshape(1, D), ffn.reshape(1, D), fin.reshape(1, D),
            jnp.zeros((6, D), F32)]
    return jnp.concatenate(rows + [w_s.reshape(NG, GCH, GCH).transpose(1, 0, 2).reshape(GCH, D)], axis=0)


def _unpack_small(p):
    w_s = p[16:].reshape(GCH, NG, GCH).transpose(1, 0, 2).reshape(1, NG, GCH, GCH)
    return dict(norm_mix_g=p[1:2], gmlp_ln_g=p[2:3], gmlp_ln_b=p[3:4], gmlp_b_s=p[4].reshape(1, NG, GCH),
                hgrn_lb_table=p[5:7], hgrn_norm_g=p[7:8], norm_ffn_g=p[8:9], norm_final_g=p[9],
                gmlp_w_s=w_s)


SMALL = ("norm_mix_g", "gmlp_ln_g", "gmlp_ln_b", "gmlp_w_s", "gmlp_b_s", "hgrn_lb_table", "hgrn_norm_g",
         "norm_ffn_g", "norm_final_g")
BIG = ("w_in", "w_gate_up", "w_branch_a", "w_branch_b", "w_out", "w_down")
ORDER = ("norm_mix_g", "w_in", "gmlp_ln_g", "gmlp_ln_b", "gmlp_w_s", "gmlp_b_s", "hgrn_lb_table",
         "hgrn_norm_g", "w_branch_a", "w_branch_b", "w_out", "norm_ffn_g", "w_gate_up", "w_down",
         "norm_final_g")


def kernel(x, norm_mix_g, w_in, gmlp_ln_g, gmlp_ln_b, gmlp_w_s, gmlp_b_s, hgrn_lb_table, hgrn_norm_g, w_branch_a, w_branch_b, w_out, norm_ffn_g, w_gate_up, w_down, norm_final_g, loss_target, m_norm_mix_g, m_w_in, m_gmlp_ln_g, m_gmlp_ln_b, m_gmlp_w_s, m_gmlp_b_s, m_hgrn_lb_table, m_hgrn_norm_g, m_w_branch_a, m_w_branch_b, m_w_out, m_norm_ffn_g, m_w_gate_up, m_w_down, m_norm_final_g, v_norm_mix_g, v_w_in, v_gmlp_ln_g, v_gmlp_ln_b, v_gmlp_w_s, v_gmlp_b_s, v_hgrn_lb_table, v_hgrn_norm_g, v_w_branch_a, v_w_branch_b, v_w_out, v_norm_ffn_g, v_w_gate_up, v_w_down, v_norm_final_g):
    args = dict(locals())
    T = x.shape[1]
    xs = x.reshape(T, D)
    target = loss_target.reshape(T, D)
    big = {n: args[n].reshape(args[n].shape[1:]) for n in BIG}
    big_m = {n: args["m_" + n].reshape(args[n].shape[1:]) for n in BIG}
    big_v = {n: args["v_" + n].reshape(args[n].shape[1:]) for n in BIG}

    x_i, y_i, c_i = lax.axis_index("x"), lax.axis_index("y"), lax.axis_index("c")
    place = jnp.stack([2 * x_i + y_i, c_i]).astype(jnp.int32)
    cast = {n: _cast_shard("cast_" + n, place, big[n]) for n in BIG}
    tril = jnp.tril(jnp.ones((GCH, GCH), bool))
    wm = jnp.where(tril, gmlp_w_s[0], 0.0).astype(BF16)
    wm_t = jnp.swapaxes(wm, 1, 2)
    b_t = gmlp_b_s[0].T

    (proj, hb), w_in4, (w_a4,) = _proj_fwd(place, xs, norm_mix_g, cast["w_in"], [cast["w_branch_a"]])
    (ab,), (w_b4, w_out4) = _gmlp_fwd(
        proj, gmlp_ln_g, gmlp_ln_b, wm, b_t, job=_gather_job([cast["w_branch_b"], cast["w_out"]]))
    (o_raw, obb, st_before), (w_gu4,) = _hgrn_fwd(
        proj, hgrn_lb_table, hgrn_norm_g, job=_gather_job([cast["w_gate_up"]]))
    w_a, w_b, w_o = (w.reshape(D, D) for w in (w_a4, w_b4, w_out4))
    (mgb, x1), (w_down4,) = _merge_fwd(xs, ab, obb, proj, w_a, w_b, w_o, job=_gather_job([cast["w_down"]]))
    w_dn = w_down4.reshape(FF, D)
    act, dx2b, h2b, dgu4, dx1, dx1b, acc_ffn = _ffn_fwd_bwd(
        x1, target, norm_ffn_g, norm_final_g.reshape(1, D), w_gu4, w_dn)

    grads, owns, parts, halves, sibh = {}, {}, {}, {}, {}

    def pair_sums(names, sibs):
        for n, s in zip(names, sibs):
            owns[n], parts[n] = _pair_sum("rs_pair_sum_" + n, place, grads[n], s)

    def chip_sums(names, got):
        for n, r in zip(names, got):
            halves[n] = _chip_sum("rs_chip_sum_" + n, owns[n], r)

    ffn, mix = ("w_gate_up", "w_down"), ("w_branch_a", "w_branch_b", "w_out")
    grads["w_gate_up"], _ = _dw_gate_up(h2b, dgu4)
    grads["w_down"], _ = _dw_down(act, dx2b)
    (dya, dyb, dproj), got = _merge_bwd(
        dx1b, ab, obb, proj, w_o, w_a, w_b, job=_pair_exchange_job([grads[n] for n in ffn]))
    pair_sums(ffn, got)
    grads["w_branch_a"], _ = _dw_square("dw_branch_a", ab, dya)
    grads["w_branch_b"], _ = _dw_square("dw_branch_b", obb, dyb)
    grads["w_out"], _ = _dw_square("dw_out", mgb, dx1b)
    (dproj, acc_hgrn), got = _hgrn_bwd(
        dproj, dyb, w_b, o_raw, proj, st_before, hgrn_lb_table, hgrn_norm_g,
        job=_join_jobs(_chip_exchange_job([parts[n] for n in ffn]), _pair_exchange_job([grads[n] for n in mix])))
    chip_sums(ffn, got[:2])
    pair_sums(mix, got[2:])
    dproj, acc_ln, dws, dmix = _gmlp_bwd(dproj, dya, w_a, proj, gmlp_ln_g, gmlp_ln_b, wm, wm_t, b_t)
    for_sibling, got = _dw_in_half(
        "dw_in_sibling_half", place, hb, dproj, False,
        job=_join_jobs(_share_halves_job([halves[n] for n in ffn]), _chip_exchange_job([parts[n] for n in mix])))
    sibh.update(zip(ffn, got[:2]))
    chip_sums(mix, got[2:])
    grads["w_in"], got = _dw_in_half(
        "dw_in_own_half", place, hb, dproj, True, job=_share_halves_job([for_sibling]))
    pair_sums(("w_in",), got)
    (grad_x, acc_mix), got = _proj_bwd(
        dproj, w_in4, xs, dx1, norm_mix_g,
        job=_join_jobs(_chip_exchange_job([parts["w_in"]]), _share_halves_job([halves[n] for n in mix])))
    chip_sums(("w_in",), got[:1])
    sibh.update(zip(mix, got[1:]))
    (sibh["w_in"],) = _run_job(_share_halves_job([halves["w_in"]]), "rs_share_halves_w_in")
    out = {}
    for n in BIG:
        g, d, mn, vn = _adamw("adamw_" + n, place, big[n], halves[n], sibh[n], big_m[n], big_v[n])
        shp = args[n].shape
        out[n] = (g.reshape(shp), d.reshape(shp), mn.reshape(shp), vn.reshape(shp))

    lbv = jax.nn.sigmoid(hgrn_lb_table[0] - hgrn_lb_table[1])
    d_t0 = jnp.sum(acc_hgrn[0], axis=0) * lbv * (1.0 - lbv)
    loss_row = jnp.zeros((D,), F32).at[0].set(jnp.sum(acc_ffn[0]))
    dws_m = jnp.where(tril[:, None, :], dws.reshape(GCH, NG, GCH), 0.0).transpose(1, 0, 2)
    db_s = jnp.sum(dmix.reshape(GCH, NG, GCH), axis=-1).T
    sp = _pack_small(loss_row, jnp.sum(acc_mix, 0), jnp.sum(acc_ln[0], 0), jnp.sum(acc_ln[1], 0), db_s,
                     jnp.stack([d_t0, -d_t0]), jnp.sum(acc_hgrn[1], 0), jnp.sum(acc_ffn[2], 0),
                     jnp.sum(acc_ffn[1], 0), dws_m)
    zero = jnp.zeros((D,), F32)

    def pack(prefix):
        a = lambda n: args[prefix + n]
        return _pack_small(zero, a("norm_mix_g"), a("gmlp_ln_g"), a("gmlp_ln_b"), a("gmlp_b_s"),
                           a("hgrn_lb_table"), a("hgrn_norm_g"), a("norm_ffn_g"), a("norm_final_g"),
                           a("gmlp_w_s"))

    packed = _small_allreduce_adamw(sp, pack(""), pack("m_"), pack("v_"))
    loss = packed[0][0, 0]
    small = [_unpack_small(p) for p in packed]
    for n in SMALL:
        out[n] = tuple(s[n] for s in small)
    return (loss, grad_x.reshape(x.shape), *[out[n][0] for n in ORDER], *[out[n][1] for n in ORDER],
            *[out[n][2] for n in ORDER], *[out[n][3] for n in ORDER])
```

```python
import functools
import math

import jax
import jax.numpy as jnp
from jax import lax
from jax.experimental import pallas as pl
from jax.experimental.pallas import tpu as pltpu

F32 = jnp.float32
BF16 = jnp.bfloat16
SDS = jax.ShapeDtypeStruct
MESH = pl.DeviceIdType.MESH
ANY = pl.BlockSpec(memory_space=pl.ANY)

D = 1024
NIN = 8
NG = 8
GCH = 128
NH = 8
HD = 128
HCH = 64
HGRN_HB = 4
HW = HGRN_HB * HD
DW_TOKENS = 2048
ELEMENTWISE_BLOCK_BYTES = 2 * 1024 * 1024
MM_COLS = 256
FF = 2816
FFS = 1408
NCHIP = 4
EPS = 1e-6
QSCALE = HD ** -0.5
GELU_C0 = math.sqrt(2.0 / math.pi)
GELU_C1 = 0.044715
LR, B1, B2, AEPS, WD, STEP = 0.001, 0.9, 0.999, 1e-08, 0.01, 10
VMEM_LIMIT_V7X = 56 * 1024 * 1024
SP_ROWS = 144


def _cparams(**kw):
    return pltpu.CompilerParams(vmem_limit_bytes=VMEM_LIMIT_V7X, **kw)


def _mm(a, b):
    return jnp.dot(a, b, preferred_element_type=F32)


def _mm_nt(a, b):
    return lax.dot_general(a, b, (((1,), (1,)), ((), ())), preferred_element_type=F32)


def _mm_tn(a, b):
    return lax.dot_general(a, b, (((0,), (0,)), ((), ())), preferred_element_type=F32)


def _rows8(x):
    r, c = x.shape
    return jnp.sum(x.reshape(r // 8, 8, c), axis=0)


def _mean(x):
    return jnp.mean(x, axis=-1, keepdims=True)


def _sigmoid(x):
    return 1.0 / (1.0 + jnp.exp(-x))


def _gelu(x):
    t = jnp.tanh(GELU_C0 * (x + GELU_C1 * x * x * x))
    return 0.5 * x * (1.0 + t), t


def _gelu_grad(x, t):
    return 0.5 * (1.0 + t) + 0.5 * x * (1.0 - t * t) * (GELU_C0 * (1.0 + 3.0 * GELU_C1 * x * x))


def _component_of(group):
    return jnp.where(group < 6, (group + 4) % 6, group)


def _proj_fwd(place, x, g_mix, w_in4, later):
    T = x.shape[0]
    tm = min(1024, T)
    ni = T // tm
    n = len(later)

    def body(pc_ref, x_ref, g_ref, *rest):
        proj_ref, h_ref, w_all = rest[1 + n:4 + n]
        gathered = rest[4 + n:4 + 2 * n]
        hs, wbuf, wsem, obuf, osem = rest[4 + 2 * n:9 + 2 * n]
        w_sems, later_sems = rest[9 + 2 * n:15 + 2 * n], rest[15 + 2 * n:]
        jp, i = pl.program_id(0), pl.program_id(1)
        w_cols = [w_all.at[:, :, pl.ds(k * D, D)] for k in range(2)]

        def w_copy(blk):
            cols = pl.ds(pl.multiple_of((blk % 2) * D, 128), D)
            return pltpu.make_async_copy(w_all.at[pc_ref[0] ^ (blk // 2), :, cols], wbuf.at[blk % 2],
                                         wsem.at[blk % 2])

        @pl.when((jp == 0) & (i == 0))
        def _():
            _gather_start(w_cols, w_sems)
            _gather_start(gathered, later_sems)
            w_copy(jp).start()

        @pl.when(i == 0)
        def _():
            w_copy(jp).wait()

        @pl.when(jp == 0)
        def _():
            xv = x_ref[...]
            r = lax.rsqrt(_mean(xv * xv) + EPS)
            hb = (xv * r * g_ref[...]).astype(BF16)
            hs[i] = hb
            h_ref[...] = hb

        step = jp * ni + i
        slot = step % 2

        def o_copies(slot_):
            comp = 2 * (pc_ref[0] ^ (jp // 2)) + jp % 2
            return [pltpu.make_async_copy(
                obuf.at[slot_, pl.ds(p * (tm // 2), tm // 2)],
                proj_ref.at[comp, pl.ds(pl.multiple_of(i * tm + p * (tm // 2), 8), tm // 2)],
                osem.at[slot_, p]) for p in range(2)]

        @pl.when(step >= 2)
        def _():
            for cp in o_copies(slot):
                cp.wait()

        hv = hs[i]
        for k in range(D // MM_COLS):
            cols = slice(MM_COLS * k, MM_COLS * (k + 1))
            obuf[slot, :, cols] = _mm(hv, wbuf[jp % 2, :, cols])
        for cp in o_copies(slot):
            cp.start()

        @pl.when(step == NIN * ni - 1)
        def _():
            for cp in o_copies(1 - slot) + o_copies(slot):
                cp.wait()

        for nxt in range(1, NIN):
            @pl.when((jp == nxt - 1) & (i == ni - 1))
            def _():
                if nxt >= 2:
                    _gather_land([w_cols[nxt % 2]], w_sems, nxt // 2, first=nxt % 2)
                if nxt == 4:
                    _gather_neighbours(gathered, later_sems)
                w_copy(jp + 1).start()

        @pl.when((jp == NIN - 1) & (i == ni - 1))
        def _():
            _gather_drain(w_cols, w_sems)
            _gather_finish(gathered, later_sems)

    tile = lambda jp, i, pc: (jnp.where(jp == 0, i, ni - 1), 0)
    res = pl.pallas_call(
        body, name="proj_fwd",
        grid_spec=pltpu.PrefetchScalarGridSpec(
            num_scalar_prefetch=1, grid=(NIN, ni),
            in_specs=[pl.BlockSpec((tm, D), tile), pl.BlockSpec((1, D), lambda jp, i, pc: (0, 0))] + [ANY] * (1 + n),
            out_specs=[ANY, pl.BlockSpec((tm, D), tile)] + [ANY] * (1 + n),
            scratch_shapes=[pltpu.VMEM((ni, tm, D), BF16), pltpu.VMEM((2, D, D), BF16),
                            pltpu.SemaphoreType.DMA((2,)), pltpu.VMEM((2, tm, D), F32),
                            pltpu.SemaphoreType.DMA((2, 2))] + _gather_sems(2) + _gather_sems(n)),
        out_shape=[SDS((NIN, T, D), F32), SDS((T, D), BF16), SDS(w_in4.shape, BF16)]
        + [SDS(a.shape, a.dtype) for a in later],
        input_output_aliases={3 + k: 2 + k for k in range(1 + n)},
        compiler_params=_cparams(has_side_effects=True),
    )(place, x, g_mix, w_in4, *later)
    return res[:2], res[2], res[3:]


def _layer_norm_stats(gv):
    mu = _mean(gv)
    xc = gv - mu
    rs = lax.rsqrt(_mean(xc * xc) + EPS)
    return xc * rs, rs


def _gmlp_fwd(proj, ln_g, ln_b, wm, b_t, job=None):
    T = proj.shape[1]
    tm = min(256, T)

    def body(u_ref, v_ref, lg_ref, lb_ref, wm_ref, bt_ref, a_ref, a_s):
        gu, _ = _gelu(u_ref[...])
        gv, _ = _gelu(v_ref[...])
        vhat, _ = _layer_norm_stats(gv)
        vnb = (vhat * lg_ref[...] + lb_ref[...]).astype(BF16)
        for ch in range(tm // GCH):
            rows = slice(GCH * ch, GCH * (ch + 1))
            for g in range(NG):
                cols = slice(128 * g, 128 * (g + 1))
                mixed = _mm(wm_ref[g], vnb[rows, cols]) + bt_ref[:, g:g + 1]
                a_s[rows, cols] = gu[rows, cols] * mixed
        a_ref[...] = a_s[...].astype(BF16)

    row = lambda i: (0, 0)
    return _call(
        body, name="gmlp_fwd", grid=(T // tm,), job=job, args=(proj, proj, ln_g, ln_b, wm, b_t),
        in_specs=[pl.BlockSpec((None, tm, D), lambda i: (0, i, 0)), pl.BlockSpec((None, tm, D), lambda i: (1, i, 0)),
                  pl.BlockSpec((1, D), row), pl.BlockSpec((1, D), row),
                  pl.BlockSpec((NG, GCH, GCH), lambda i: (0, 0, 0)), pl.BlockSpec((GCH, NG), row)],
        out_specs=[pl.BlockSpec((tm, D), lambda i: (i, 0))],
        out_shape=[SDS((T, D), BF16)],
        scratch_shapes=[pltpu.VMEM((tm, D), F32)])


def _cumsum64(x, row):
    for s in (1, 2, 4, 8, 16, 32):
        x = x + jnp.where(row >= s, pltpu.roll(x, s, 0), 0.0)
    return x


def _revcumsum64(x, row):
    n = x.shape[0]
    for s in (1, 2, 4, 8, 16, 32):
        x = x + jnp.where(row < HCH - s, pltpu.roll(x, n - s, 0), 0.0)
    return x


def _head_mean(x):
    parts = [jnp.broadcast_to(_mean(x[:, HD * h:HD * (h + 1)]), (x.shape[0], HD)) for h in range(x.shape[1] // HD)]
    return jnp.concatenate(parts, axis=1)


def _seg_sum(x):
    n, c = x.shape
    s = jnp.sum(x.reshape(n // HCH, HCH, c), axis=1, keepdims=True)
    return jnp.broadcast_to(s, (n // HCH, HCH, c)).reshape(n, c)


def _hgrn_gates(fl, lbv, row):
    s = _sigmoid(fl)
    f = lbv + (1.0 - lbv) * s
    a = _cumsum64(jnp.log(f), row)
    a_mid = _seg_sum(jnp.where(row == HCH // 2 - 1, a, 0.0))
    a_last = _seg_sum(jnp.where(row == HCH - 1, a, 0.0))
    return s, f, a, a_mid, a_last


def _hgrn_fwd(proj, lb_table, norm_g, job=None):
    T = proj.shape[1]
    tb = min(512, T)
    nc = tb // HCH

    def body(q_ref, fl_ref, v_ref, g_ref, lbt_ref, gn_ref, o_ref, ob_ref, stb_ref, st_s, o_s):
        @pl.when(pl.program_id(1) == 0)
        def _():
            st_s[...] = jnp.zeros_like(st_s)

        row = lax.broadcasted_iota(jnp.int32, (tb, HW), 0) & (HCH - 1)
        lbv = _sigmoid(lbt_ref[0:1, :] - lbt_ref[1:2, :])
        _, f, a, a_mid, a_last = _hgrn_gates(fl_ref[...], lbv, row)
        k = 1.0 - f
        qs = q_ref[...] * QSCALE
        q_in = (qs * jnp.exp(a - a_mid)).astype(BF16)
        k_in = (k * jnp.exp(a_mid - a)).astype(BF16)
        q_a = (qs * jnp.exp(a)).astype(BF16)
        k_d = (k * jnp.exp(a_last - a)).astype(BF16)
        dec = jnp.exp(a_last)
        vb = v_ref[...].astype(BF16)
        tri = (lax.broadcasted_iota(jnp.int32, (HCH, HCH), 0)
               >= lax.broadcasted_iota(jnp.int32, (HCH, HCH), 1))
        for c in range(nc):
            sl = slice(HCH * c, HCH * (c + 1))
            for hh in range(HGRN_HB):
                hs = slice(HD * hh, HD * (hh + 1))
                st = st_s[hh]
                stb_ref[hh, c] = st
                sc = jnp.where(tri, _mm_nt(q_in[sl, hs], k_in[sl, hs]), 0.0)
                o_s[sl, hs] = _mm(sc.astype(BF16), vb[sl, hs]) + _mm_nt(q_a[sl, hs], st.astype(BF16))
                d64 = dec[sl, hs]
                st_s[hh] = st * jnp.concatenate([d64, d64], axis=0) + _mm_tn(vb[sl, hs], k_d[sl, hs])
        o = o_s[...]
        r = lax.rsqrt(_head_mean(o * o) + EPS)
        g = g_ref[...]
        o_ref[...] = o
        ob_ref[...] = (o * r * gn_ref[...] * (g * _sigmoid(g))).astype(BF16)

    def col(off):
        return pl.BlockSpec((None, tb, HW), lambda h, cb: (off, cb, h))

    return _call(
        body, name="hgrn_fwd", grid=(NH // HGRN_HB, T // tb), job=job,
        args=(proj, proj, proj, proj, lb_table, norm_g),
        in_specs=[col(2), col(3), col(4), col(5),
                  pl.BlockSpec((2, HW), lambda h, cb: (0, h)), pl.BlockSpec((1, HW), lambda h, cb: (0, h))],
        out_specs=[pl.BlockSpec((tb, HW), lambda h, cb: (cb, h)), pl.BlockSpec((tb, HW), lambda h, cb: (cb, h)),
                   pl.BlockSpec((HGRN_HB, nc, HD, HD), lambda h, cb: (h, cb, 0, 0))],
        out_shape=[SDS((T, D), F32), SDS((T, D), BF16), SDS((NH, T // HCH, HD, HD), F32)],
        scratch_shapes=[pltpu.VMEM((HGRN_HB, HD, HD), F32), pltpu.VMEM((tb, HW), F32)])


def _merge_fwd(x, ab, ob, proj, w_a, w_b, w_out, job=None):
    T = x.shape[0]
    tm = min(512, T)

    def body(x_ref, ab_ref, ob_ref, ga_ref, gb_ref, wa_ref, wb_ref, wo_ref, mg_ref, x1_ref):
        ya = _mm(ab_ref[...], wa_ref[...])
        yb = _mm(ob_ref[...], wb_ref[...])
        merged = (_sigmoid(ga_ref[...]) * ya + _sigmoid(gb_ref[...]) * yb).astype(BF16)
        mg_ref[...] = merged
        x1_ref[...] = x_ref[...] + _mm(merged, wo_ref[...])

    t = lambda i: (i, 0)
    w = lambda i: (0, 0)
    return _call(
        body, name="merge_fwd", grid=(T // tm,), job=job, args=(x, ab, ob, proj, proj, w_a, w_b, w_out),
        in_specs=[pl.BlockSpec((tm, D), t), pl.BlockSpec((tm, D), t), pl.BlockSpec((tm, D), t),
                  pl.BlockSpec((None, tm, D), lambda i: (6, i, 0)), pl.BlockSpec((None, tm, D), lambda i: (7, i, 0)),
                  pl.BlockSpec((D, D), w), pl.BlockSpec((D, D), w), pl.BlockSpec((D, D), w)],
        out_specs=[pl.BlockSpec((tm, D), t)] * 2,
        out_shape=[SDS((T, D), BF16), SDS((T, D), F32)])


def _ffn_fwd_bwd(x1, target, g_ffn, g_fin, w_gu4, w_down):
    T = x1.shape[0]
    tm = min(256, T)
    inv_d = 1.0 / D

    def body(x1_ref, tg_ref, gf_ref, gn_ref, wgu_ref, wd_ref,
             act_ref, dx2b_ref, h2b_ref, dgu_ref, dx1_ref, dx1b_ref, acc_ref):
        @pl.when(pl.program_id(0) == 0)
        def _():
            acc_ref[...] = jnp.zeros_like(acc_ref)

        x1v = x1_ref[...]
        gf = gf_ref[...]
        gn = gn_ref[...]
        rr1 = lax.rsqrt(_mean(x1v * x1v) + EPS)
        x1n = x1v * rr1
        h2b = (x1n * gf).astype(BF16)
        h2b_ref[...] = h2b
        p = [_mm(h2b, wgu_ref[k]) for k in range(NCHIP)]
        sg = [_sigmoid(p[0]), _sigmoid(p[1])]
        si = [p[0] * sg[0], p[1] * sg[1]]
        x2 = x1v
        for k in range(2):
            actk = (si[k] * p[2 + k]).astype(BF16)
            act_ref[:, FFS * k:FFS * (k + 1)] = actk
            x2 = x2 + _mm(actk, wd_ref[FFS * k:FFS * (k + 1), :])
        rr2 = lax.rsqrt(_mean(x2 * x2) + EPS)
        x2n = x2 * rr2
        e = x2n * gn - tg_ref[...]
        acc_ref[0] += _rows8(e * e) * (0.5 * inv_d)
        dy = e * inv_d
        acc_ref[1] += _rows8(dy * x2n)
        dxn = dy * gn
        dx2 = rr2 * (dxn - x2n * _mean(dxn * x2n))
        dx2b = dx2.astype(BF16)
        dx2b_ref[...] = dx2b
        dh2 = None
        for k in range(2):
            dact = _mm_nt(dx2b, wd_ref[FFS * k:FFS * (k + 1), :])
            dgate = (dact * p[2 + k] * (sg[k] * (1.0 + p[k] * (1.0 - sg[k])))).astype(BF16)
            dup = (dact * si[k]).astype(BF16)
            dgu_ref[k] = dgate
            dgu_ref[2 + k] = dup
            part = _mm_nt(dgate, wgu_ref[k]) + _mm_nt(dup, wgu_ref[2 + k])
            dh2 = part if dh2 is None else dh2 + part
        acc_ref[2] += _rows8(dh2 * x1n)
        dxn1 = dh2 * gf
        dx1 = dx2 + rr1 * (dxn1 - x1n * _mean(dxn1 * x1n))
        dx1_ref[...] = dx1
        dx1b_ref[...] = dx1.astype(BF16)

    t = lambda i: (i, 0)
    w = lambda i: (0, 0)
    one = pl.Buffered(1)
    return pl.pallas_call(
        body, name="ffn_fwd_bwd", grid=(T // tm,),
        in_specs=[pl.BlockSpec((tm, D), t), pl.BlockSpec((tm, D), t),
                  pl.BlockSpec((1, D), w), pl.BlockSpec((1, D), w),
                  pl.BlockSpec((NCHIP, D, FFS), lambda i: (0, 0, 0), pipeline_mode=one),
                  pl.BlockSpec((FF, D), w, pipeline_mode=one)],
        out_specs=[pl.BlockSpec((tm, FF), t), pl.BlockSpec((tm, D), t), pl.BlockSpec((tm, D), t),
                   pl.BlockSpec((NCHIP, tm, FFS), lambda i: (0, i, 0)),
                   pl.BlockSpec((tm, D), t), pl.BlockSpec((tm, D), t),
                   pl.BlockSpec((3, 8, D), lambda i: (0, 0, 0))],
        out_shape=[SDS((T, FF), BF16), SDS((T, D), BF16), SDS((T, D), BF16),
                   SDS((NCHIP, T, FFS), BF16), SDS((T, D), F32), SDS((T, D), BF16),
                   SDS((3, 8, D), F32)],
        compiler_params=_cparams(),
    )(x1, target, g_ffn, g_fin, w_gu4, w_down)


def _merge_bwd(dx1b, ab, ob, proj, w_out, w_a, w_b, job=None):
    T = dx1b.shape[0]
    tm = min(512, T)

    def body(dx_ref, ab_ref, ob_ref, ga_ref, gb_ref, wo_ref, wa_ref, wb_ref, dya_ref, dyb_ref, dp_ref):
        dm = _mm_nt(dx_ref[...], wo_ref[...])
        sa = _sigmoid(ga_ref[...])
        sb = _sigmoid(gb_ref[...])
        dya_ref[...] = (dm * sa).astype(BF16)
        dyb_ref[...] = (dm * sb).astype(BF16)
        dp_ref[0] = (dm * _mm(ab_ref[...], wa_ref[...]) * sa * (1.0 - sa)).astype(BF16)
        dp_ref[1] = (dm * _mm(ob_ref[...], wb_ref[...]) * sb * (1.0 - sb)).astype(BF16)

    t = lambda i: (i, 0)
    w = lambda i: (0, 0)
    return _call(
        body, name="merge_bwd", grid=(T // tm,),
        in_specs=[pl.BlockSpec((tm, D), t), pl.BlockSpec((tm, D), t), pl.BlockSpec((tm, D), t),
                  pl.BlockSpec((None, tm, D), lambda i: (6, i, 0)), pl.BlockSpec((None, tm, D), lambda i: (7, i, 0)),
                  pl.BlockSpec((D, D), w), pl.BlockSpec((D, D), w), pl.BlockSpec((D, D), w)],
        out_specs=[pl.BlockSpec((tm, D), t)] * 2 + [pl.BlockSpec((2, tm, D), lambda i: (3, i, 0))],
        out_shape=[SDS((T, D), BF16), SDS((T, D), BF16), SDS((NIN, T, D), BF16)],
        args=(dx1b, ab, ob, proj, proj, w_out, w_a, w_b), job=job)


def _hgrn_bwd(dproj, dyb, w_b, o_raw, proj, st_before, lb_table, norm_g, job=None):
    T = dyb.shape[0]
    tb = min(512, T)
    nc = tb // HCH
    nb = T // tb

    def body(dp_in, dyb_ref, wb_ref, o_ref, q_ref, fl_ref, v_ref, g_ref, stb_ref, lbt_ref, gn_ref,
             dp_ref, acc_ref, dst_s, dqin_s, dqa_s, dkin_s, dkd_s, dv_s, ddec_s):
        del dp_in

        @pl.when(pl.program_id(1) == 0)
        def _():
            dst_s[...] = jnp.zeros_like(dst_s)
            acc_ref[...] = jnp.zeros_like(acc_ref)

        row = lax.broadcasted_iota(jnp.int32, (tb, HW), 0) & (HCH - 1)
        gn = gn_ref[...]
        lbv = _sigmoid(lbt_ref[0:1, :] - lbt_ref[1:2, :])
        o = o_ref[...]
        r = lax.rsqrt(_head_mean(o * o) + EPS)
        on = o * r
        g = g_ref[...]
        sgm = _sigmoid(g)
        dob_v = _mm_nt(dyb_ref[...], wb_ref[...])
        dp_ref[3] = (dob_v * on * gn * (sgm * (1.0 + g * (1.0 - sgm)))).astype(BF16)
        do_n = dob_v * (g * sgm)
        acc_ref[1] += _rows8(do_n * on)
        dxn = do_n * gn
        do = (r * (dxn - on * _head_mean(dxn * on))).astype(BF16)
        s, f, a, a_mid, a_last = _hgrn_gates(fl_ref[...], lbv, row)
        k = 1.0 - f
        qs = q_ref[...] * QSCALE
        e_q = jnp.exp(a - a_mid)
        e_k = jnp.exp(a_mid - a)
        e_a = jnp.exp(a)
        e_l = jnp.exp(a_last - a)
        dec = jnp.exp(a_last)
        q_in = qs * e_q
        k_in = k * e_k
        q_a = qs * e_a
        k_d = k * e_l
        q_inb, k_inb, q_ab, k_db = (z.astype(BF16) for z in (q_in, k_in, q_a, k_d))
        vb = v_ref[...].astype(BF16)
        tri = (lax.broadcasted_iota(jnp.int32, (HCH, HCH), 0)
               >= lax.broadcasted_iota(jnp.int32, (HCH, HCH), 1))
        for c in reversed(range(nc)):
            sl = slice(HCH * c, HCH * (c + 1))
            for hh in range(HGRN_HB):
                hs = slice(HD * hh, HD * (hh + 1))
                stp = stb_ref[hh, c]
                dst = dst_s[hh]
                dstb = dst.astype(BF16)
                do_c = do[sl, hs]
                v_c = vb[sl, hs]
                dqa_s[sl, hs] = _mm(do_c, stp.astype(BF16))
                dkd_s[sl, hs] = _mm(v_c, dstb)
                ddec_s[sl, hs] = jnp.broadcast_to(jnp.sum(dst * stp, axis=0, keepdims=True), (HCH, HD))
                sc = jnp.where(tri, _mm_nt(q_inb[sl, hs], k_inb[sl, hs]), 0.0).astype(BF16)
                dsc = jnp.where(tri, _mm_nt(do_c, v_c), 0.0).astype(BF16)
                dv_s[sl, hs] = _mm_nt(k_db[sl, hs], dstb) + _mm_tn(sc, do_c)
                dqin_s[sl, hs] = _mm(dsc, k_inb[sl, hs])
                dkin_s[sl, hs] = _mm_tn(dsc, q_inb[sl, hs])
                d64 = dec[sl, hs]
                dst_s[hh] = dst * jnp.concatenate([d64, d64], axis=0) + _mm_tn(do_c, q_ab[sl, hs])
        dq_in = dqin_s[...]
        dq_a = dqa_s[...]
        dk_in = dkin_s[...]
        dk_d = dkd_s[...]
        dp_ref[0] = ((dq_in * e_q + dq_a * e_a) * QSCALE).astype(BF16)
        dp_ref[2] = dv_s[...].astype(BF16)
        tq = dq_in * q_in
        tk = dk_in * k_in
        td = dk_d * k_d
        d_a = tq + dq_a * q_a - tk - td
        d_a = d_a + jnp.where(row == HCH // 2 - 1, _seg_sum(tk - tq), 0.0)
        d_a = d_a + jnp.where(row == HCH - 1, _seg_sum(td) + ddec_s[...] * dec, 0.0)
        dlf = _revcumsum64(d_a, row)
        df = dlf / f - (dk_in * e_k + dk_d * e_l)
        dp_ref[1] = (df * (1.0 - lbv) * s * (1.0 - s)).astype(BF16)
        acc_ref[0] += _rows8(df * (1.0 - s))

    def col(off):
        return pl.BlockSpec((None, tb, HW), lambda h, cb: (off, nb - 1 - cb, h))

    hb = lambda h, cb: (nb - 1 - cb, h)
    return _call(
        body, name="hgrn_bwd", grid=(NH // HGRN_HB, nb), job=job,
        args=(dproj, dyb, w_b, o_raw, proj, proj, proj, proj, st_before, lb_table, norm_g),
        in_specs=[ANY, pl.BlockSpec((tb, D), lambda h, cb: (nb - 1 - cb, 0)),
                  pl.BlockSpec((HW, D), lambda h, cb: (h, 0)), pl.BlockSpec((tb, HW), hb),
                  col(2), col(3), col(4), col(5),
                  pl.BlockSpec((HGRN_HB, nc, HD, HD), lambda h, cb: (h, nb - 1 - cb, 0, 0)),
                  pl.BlockSpec((2, HW), lambda h, cb: (0, h)), pl.BlockSpec((1, HW), lambda h, cb: (0, h))],
        out_specs=[pl.BlockSpec((4, tb, HW), lambda h, cb: (0, nb - 1 - cb, h)),
                   pl.BlockSpec((2, 8, HW), lambda h, cb: (0, 0, h))],
        out_shape=[SDS(dproj.shape, BF16), SDS((2, 8, D), F32)],
        scratch_shapes=[pltpu.VMEM((HGRN_HB, HD, HD), F32)] + [pltpu.VMEM((tb, HW), F32)] * 6,
        aliases={0: 0})


def _gmlp_bwd(dproj, dya, w_a, proj, ln_g, ln_b, wm, wm_t, b_t):
    T = dya.shape[0]
    tm = min(256, T)

    def body(dp_in, dya_ref, wa_ref, u_ref, v_ref, lg_ref, lb_ref, wm_ref, wmt_ref, bt_ref,
             dp_ref, acc_ref, dws_ref, dmix_ref, du_s, dvn_s):
        del dp_in

        @pl.when(pl.program_id(0) == 0)
        def _():
            acc_ref[...] = jnp.zeros_like(acc_ref)
            dws_ref[...] = jnp.zeros_like(dws_ref)
            dmix_ref[...] = jnp.zeros_like(dmix_ref)

        u = u_ref[...]
        v = v_ref[...]
        lg = lg_ref[...]
        gu, t_u = _gelu(u)
        gv, t_v = _gelu(v)
        vhat, rs = _layer_norm_stats(gv)
        vnb = (vhat * lg + lb_ref[...]).astype(BF16)
        da_v = _mm_nt(dya_ref[...], wa_ref[...])
        for ch in range(tm // GCH):
            rows = slice(GCH * ch, GCH * (ch + 1))
            for g in range(NG):
                cols = slice(128 * g, 128 * (g + 1))
                vng = vnb[rows, cols]
                mixed = _mm(wm_ref[g], vng) + bt_ref[:, g:g + 1]
                dag = da_v[rows, cols]
                dmx = dag * gu[rows, cols]
                du_s[rows, cols] = dag * mixed
                dmxb = dmx.astype(BF16)
                dws_ref[:, cols] += _mm_nt(dmxb, vng)
                dmix_ref[:, cols] += dmx
                dvn_s[rows, cols] = _mm(wmt_ref[g], dmxb)
        dp_ref[0] = (du_s[...] * _gelu_grad(u, t_u)).astype(BF16)
        dvn = dvn_s[...]
        acc_ref[0] += _rows8(dvn * vhat)
        acc_ref[1] += _rows8(dvn)
        dvh = dvn * lg
        dgv = rs * (dvh - _mean(dvh) - vhat * _mean(dvh * vhat))
        dp_ref[1] = (dgv * _gelu_grad(v, t_v)).astype(BF16)

    row = lambda i: (0, 0)
    w3 = lambda i: (0, 0, 0)
    return pl.pallas_call(
        body, name="gmlp_bwd", grid=(T // tm,),
        in_specs=[ANY, pl.BlockSpec((tm, D), lambda i: (i, 0)), pl.BlockSpec((D, D), row),
                  pl.BlockSpec((None, tm, D), lambda i: (0, i, 0)), pl.BlockSpec((None, tm, D), lambda i: (1, i, 0)),
                  pl.BlockSpec((1, D), row), pl.BlockSpec((1, D), row),
                  pl.BlockSpec((NG, GCH, GCH), w3), pl.BlockSpec((NG, GCH, GCH), w3),
                  pl.BlockSpec((GCH, NG), row)],
        out_specs=[pl.BlockSpec((2, tm, D), lambda i: (2, i, 0)),
                   pl.BlockSpec((2, 8, D), w3), pl.BlockSpec((GCH, D), row), pl.BlockSpec((GCH, D), row)],
        out_shape=[SDS(dproj.shape, BF16), SDS((2, 8, D), F32), SDS((GCH, D), F32), SDS((GCH, D), F32)],
        scratch_shapes=[pltpu.VMEM((tm, D), F32), pltpu.VMEM((tm, D), F32)],
        input_output_aliases={0: 0},
        compiler_params=_cparams(),
    )(dproj, dya, w_a, proj, proj, ln_g, ln_b, wm, wm_t, b_t)


def _proj_bwd(dproj, w_in4, x, dx1, g_mix, job=None):
    T = x.shape[0]
    tm = min(256, T)
    order = (2, 3, 4, 5, 0, 1, 6, 7)

    def body(dp_ref, w_ref, x_ref, dx1_ref, g_ref, gx_ref, acc_ref):
        @pl.when(pl.program_id(0) == 0)
        def _():
            acc_ref[...] = jnp.zeros_like(acc_ref)

        dh = None
        for m, og in enumerate(order):
            part = _mm_nt(dp_ref[m], w_ref[og // 2, :, D * (og % 2):D * (og % 2 + 1)])
            dh = part if dh is None else dh + part
        xv = x_ref[...]
        r = lax.rsqrt(_mean(xv * xv) + EPS)
        xn = xv * r
        acc_ref[...] += _rows8(dh * xn)
        dxn = dh * g_ref[...]
        gx_ref[...] = dx1_ref[...] + r * (dxn - xn * _mean(dxn * xn))

    t = lambda i: (i, 0)
    return _call(
        body, name="proj_bwd", grid=(T // tm,),
        in_specs=[pl.BlockSpec((NIN, tm, D), lambda i: (0, i, 0)),
                  pl.BlockSpec((NCHIP, D, 2 * D), lambda i: (0, 0, 0), pipeline_mode=pl.Buffered(1)),
                  pl.BlockSpec((tm, D), t), pl.BlockSpec((tm, D), t), pl.BlockSpec((1, D), lambda i: (0, 0))],
        out_specs=[pl.BlockSpec((tm, D), t), pl.BlockSpec((8, D), lambda i: (0, 0))],
        out_shape=[SDS((T, D), F32), SDS((8, D), F32)],
        args=(dproj, w_in4, x, dx1, g_mix), job=job)


def _dw_call(name, a, b, a_spec, b_spec, o_spec, out_shape, nblk, tt, job=None, prefetch=None):
    T = a.shape[-2]

    def body(*refs):
        a_ref, b_ref, o_ref = refs[-3:]

        @pl.when(pl.program_id(1) == 0)
        def _():
            o_ref[...] = jnp.zeros_like(o_ref)
        o_ref[...] += _mm_tn(a_ref[...], b_ref[...])

    (out,), job_out = _call(
        body, name=name, grid=(nblk, T // tt), in_specs=[a_spec, b_spec], out_specs=[o_spec],
        out_shape=[out_shape], args=(a, b), job=job, prefetch=prefetch)
    return out, job_out


def _dw_in_half(name, place, hb, dproj, mine, job=None):
    tt = min(DW_TOKENS, hb.shape[0])

    def comp(k, pc):
        return _component_of(2 * k + (pc[1] if mine else 1 - pc[1]))

    return _dw_call(
        name, hb, dproj,
        pl.BlockSpec((tt, D), lambda k, t, pc: (t, 0)),
        pl.BlockSpec((None, tt, D), lambda k, t, pc: (comp(k, pc), t, 0)),
        pl.BlockSpec((None, D, D), lambda k, t, pc: (k, 0, 0)),
        SDS((NCHIP, D, D), F32), NCHIP, tt, job, place)


def _dw_gate_up(h2b, dgu4, job=None):
    tt = min(DW_TOKENS, h2b.shape[0])
    return _dw_call(
        "dw_gate_up", h2b, dgu4,
        pl.BlockSpec((tt, D), lambda k, t: (t, 0)),
        pl.BlockSpec((None, tt, FFS), lambda k, t: (k, t, 0)),
        pl.BlockSpec((None, D, FFS), lambda k, t: (k, 0, 0)),
        SDS((NCHIP, D, FFS), F32), NCHIP, tt, job)


def _dw_down(act, dx2b, job=None):
    tt = min(DW_TOKENS, act.shape[0])
    g, job_out = _dw_call(
        "dw_down", act, dx2b,
        pl.BlockSpec((tt, FFS), lambda k, t: (t, k)),
        pl.BlockSpec((tt, D), lambda k, t: (t, 0)),
        pl.BlockSpec((FFS, D), lambda k, t: (k, 0)),
        SDS((FF, D), F32), 2, tt, job)
    return g.reshape(NCHIP, FF // NCHIP, D), job_out


def _dw_square(name, a, b, job=None):
    tt = min(DW_TOKENS, a.shape[0])
    g, job_out = _dw_call(
        name, a, b,
        pl.BlockSpec((tt, D), lambda k, t: (t, 0)), pl.BlockSpec((tt, D), lambda k, t: (t, 0)),
        pl.BlockSpec((D, D), lambda k, t: (0, 0)), SDS((D, D), F32), 1, tt, job)
    return g.reshape(NCHIP, D // NCHIP, D), job_out


def _place():
    x, y, c = lax.axis_index("x"), lax.axis_index("y"), lax.axis_index("c")
    return x, y, c, 2 * x + y


def _chip_at(x, y, s):
    return x ^ (s >> 1), y ^ (s & 1)


class _Job:
    def __init__(self, ins, out_shapes, sems, start, finish, aliases=None, mid=None):
        self.ins, self.out_shapes, self.sems = list(ins), list(out_shapes), list(sems)
        self.start, self.finish, self.aliases = start, finish, dict(aliases or {})
        self.mid = mid if mid is not None else (lambda ins, outs, sems: None)


def _join_jobs(*jobs):
    def cut(refs, sizes):
        out, at = [], 0
        for n in sizes:
            out.append(refs[at:at + n])
            at += n
        return out

    ni = [len(j.ins) for j in jobs]
    no = [len(j.out_shapes) for j in jobs]
    ns = [len(j.sems) for j in jobs]

    def run(which):
        def go(ins, outs, sems):
            for j, a, b, c in zip(jobs, cut(ins, ni), cut(outs, no), cut(sems, ns)):
                getattr(j, which)(a, b, c)
        return go

    aliases = {}
    for k, j in enumerate(jobs):
        for a, b in j.aliases.items():
            aliases[sum(ni[:k]) + a] = sum(no[:k]) + b
    return _Job([a for j in jobs for a in j.ins], [o for j in jobs for o in j.out_shapes],
                [s for j in jobs for s in j.sems], run("start"), run("finish"), aliases, run("mid"))


def _call(body, *, name, grid, in_specs, out_specs, out_shape, args, scratch_shapes=(), aliases=None,
          job=None, prefetch=None):
    n_in, n_out, n_scr = len(in_specs), len(out_specs), len(scratch_shapes)
    npf = 0 if prefetch is None else 1
    job = job if job is not None else _Job([], [], [], lambda *a: None, lambda *a: None)
    ji, jo = len(job.ins), len(job.out_shapes)
    steps = math.prod(grid)

    def wrapped(*refs):
        pf, refs = refs[:npf], refs[npf:]
        ins, jin = refs[:n_in], refs[n_in:n_in + ji]
        o0 = n_in + ji
        outs, jout = refs[o0:o0 + n_out], refs[o0 + n_out:o0 + n_out + jo]
        s0 = o0 + n_out + jo
        scr, jsem = refs[s0:s0 + n_scr], refs[s0 + n_scr:]
        step = functools.reduce(lambda acc, ag: acc * ag[1] + pl.program_id(ag[0]), enumerate(grid), 0)
        if ji or jo:
            @pl.when(step == 0)
            def _():
                job.start(jin, jout, jsem)

        body(*pf, *ins, *outs, *scr)

        if ji or jo:
            @pl.when(step == steps // 2)
            def _():
                job.mid(jin, jout, jsem)

            @pl.when(step == steps - 1)
            def _():
                job.finish(jin, jout, jsem)

    io = {npf + a: b for a, b in dict(aliases or {}).items()}
    io.update({npf + n_in + a: n_out + b for a, b in job.aliases.items()})
    kw = dict(in_specs=list(in_specs) + [ANY] * ji, out_specs=list(out_specs) + [ANY] * jo,
              scratch_shapes=list(scratch_shapes) + job.sems)
    if npf:
        kw = dict(grid_spec=pltpu.PrefetchScalarGridSpec(num_scalar_prefetch=1, grid=grid, **kw))
    else:
        kw["grid"] = grid
    res = pl.pallas_call(
        wrapped, name=name, out_shape=list(out_shape) + job.out_shapes, input_output_aliases=io,
        compiler_params=_cparams(has_side_effects=bool(ji or jo)), **kw,
    )(*(() if prefetch is None else (prefetch,)), *args, *job.ins)
    return list(res[:n_out]), list(res[n_out:])


def _run_job(job, name):
    ji, jo = len(job.ins), len(job.out_shapes)

    def body(*refs):
        jin, jout, jsem = refs[:ji], refs[ji:ji + jo], refs[ji + jo:]
        job.start(jin, jout, jsem)
        job.finish(jin, jout, jsem)

    return list(pl.pallas_call(
        body, name=name, in_specs=[ANY] * ji, out_specs=[ANY] * jo, out_shape=job.out_shapes,
        scratch_shapes=job.sems, input_output_aliases=job.aliases,
        compiler_params=pltpu.CompilerParams(has_side_effects=True))(*job.ins))


def _cast_shard(name, place, w):
    rows, cols = w.shape
    tr = 352 if rows % 352 == 0 else 256

    def body(pc_ref, w_ref, o_ref):
        del pc_ref
        o_ref[...] = w_ref[...].astype(BF16)

    return pl.pallas_call(
        body, name=name,
        grid_spec=pltpu.PrefetchScalarGridSpec(
            num_scalar_prefetch=1, grid=(rows // tr,),
            in_specs=[pl.BlockSpec((tr, cols), lambda i, pc: (i, 0))],
            out_specs=pl.BlockSpec((None, tr, cols), lambda i, pc: (pc[0], i, 0))),
        out_shape=SDS((NCHIP, rows, cols), BF16),
        compiler_params=_cparams(),
    )(place, w)


def _sibling_copy(ref, send_sem, recv_sem):
    x, y, c, _ = _place()
    return pltpu.make_async_remote_copy(src_ref=ref, dst_ref=ref, send_sem=send_sem, recv_sem=recv_sem,
                                        device_id=(x, y, 1 - c), device_id_type=MESH)


def _half_rows(arr, slot, core):
    half = arr.shape[1] // 2
    return arr.at[slot, pl.ds(pl.multiple_of(core * half, 16), half)]


def _quarter_rows(arr, slot, core, q):
    quarter = arr.shape[1] // 4
    return arr.at[slot, pl.ds(pl.multiple_of((2 * core + q) * quarter, 16), quarter)]


def _chip_copy(ref, dist, send_sem, recv_sem):
    x, y, c, _ = _place()
    cx, cy = _chip_at(x, y, dist)
    return pltpu.make_async_remote_copy(src_ref=ref, dst_ref=ref, send_sem=send_sem, recv_sem=recv_sem,
                                        device_id=(cx, cy, c), device_id_type=MESH)


def _gather_sems(n):
    dma = pltpu.SemaphoreType.DMA
    return [dma((n, 2))] * 4 + [dma((n, 4))] * 2


def _gather_start(arrs, sems):
    dsend, drecv = sems[0], sems[1]
    _, _, c, j = _place()
    for w, arr in enumerate(arrs):
        for dist in (1, 2):
            _chip_copy(_half_rows(arr, j, c), dist, dsend.at[w, dist - 1], drecv.at[w, dist - 1]).start()


def _gather_land(arrs, sems, dist, first=0):
    dsend, drecv, rsend, rrecv, fsend, frecv = sems
    _, _, c, j = _place()
    if dist < 3:
        other = 3 - dist
        for w, arr in enumerate(arrs, first):
            landed = _half_rows(arr, j ^ dist, c)
            _chip_copy(landed, dist, dsend.at[w, dist - 1], drecv.at[w, dist - 1]).wait_recv()
            relay = _quarter_rows(arr, j ^ dist, c, other - 1)
            _chip_copy(relay, other, rsend.at[w, other - 1], rrecv.at[w, other - 1]).start()
            _sibling_copy(landed, fsend.at[w, dist - 1], frecv.at[w, dist - 1]).start()
        for w, arr in enumerate(arrs, first):
            theirs = _half_rows(arr, j ^ dist, 1 - c)
            _sibling_copy(theirs, fsend.at[w, dist - 1], frecv.at[w, dist - 1]).wait_recv()
    else:
        for w, arr in enumerate(arrs, first):
            for via in (1, 2):
                piece = _quarter_rows(arr, j ^ 3, c, via - 1)
                _chip_copy(piece, via, rsend.at[w, via - 1], rrecv.at[w, via - 1]).wait_recv()
                _sibling_copy(piece, fsend.at[w, 1 + via], frecv.at[w, 1 + via]).start()
        for w, arr in enumerate(arrs, first):
            for via in (1, 2):
                theirs = _quarter_rows(arr, j ^ 3, 1 - c, via - 1)
                _sibling_copy(theirs, fsend.at[w, 1 + via], frecv.at[w, 1 + via]).wait_recv()


def _gather_drain(arrs, sems):
    dsend, drecv, rsend, rrecv, fsend, frecv = sems
    _, _, c, j = _place()
    for w, arr in enumerate(arrs):
        for dist in (1, 2):
            other = 3 - dist
            _chip_copy(_half_rows(arr, j, c), dist, dsend.at[w, dist - 1], drecv.at[w, dist - 1]).wait_send()
            _chip_copy(_quarter_rows(arr, j ^ dist, c, other - 1), other,
                       rsend.at[w, other - 1], rrecv.at[w, other - 1]).wait_send()
            _sibling_copy(_half_rows(arr, j ^ dist, c), fsend.at[w, dist - 1], frecv.at[w, dist - 1]).wait_send()
            _sibling_copy(_quarter_rows(arr, j ^ 3, c, dist - 1),
                          fsend.at[w, 1 + dist], frecv.at[w, 1 + dist]).wait_send()


def _gather_neighbours(arrs, sems):
    _gather_land(arrs, sems, 1)
    _gather_land(arrs, sems, 2)


def _gather_finish(arrs, sems):
    _gather_land(arrs, sems, 3)
    _gather_drain(arrs, sems)


def _gather_job(arrs):
    n = len(arrs)
    return _Job(arrs, [SDS(a.shape, a.dtype) for a in arrs], _gather_sems(n),
                lambda ins, outs, sems: _gather_start(outs, sems),
                lambda ins, outs, sems: _gather_finish(outs, sems), {k: k for k in range(n)},
                mid=lambda ins, outs, sems: _gather_neighbours(outs, sems))


def _exchange_job(arrs, out_shapes, n, copies):
    def start(ins, outs, sems):
        for cp in copies(ins, outs, sems[0], sems[1]):
            cp.start()

    def finish(ins, outs, sems):
        for cp in copies(ins, outs, sems[0], sems[1]):
            cp.wait()

    return _Job(arrs, out_shapes, [pltpu.SemaphoreType.DMA((n,))] * 2, start, finish)


def _pair_exchange_job(grads):
    def copies(ins, outs, send_sem, recv_sem):
        x, y, c, _ = _place()
        res = []
        for w in range(len(grads)):
            half = ins[w].shape[1] // 2
            theirs = pl.ds(pl.multiple_of((1 - c) * half, 8), half)
            res.append(pltpu.make_async_remote_copy(
                src_ref=ins[w].at[:, theirs, :], dst_ref=outs[w], send_sem=send_sem.at[w],
                recv_sem=recv_sem.at[w], device_id=(x, y, 1 - c), device_id_type=MESH))
        return res

    return _exchange_job(grads, [SDS((NCHIP, g.shape[1] // 2, g.shape[2]), F32) for g in grads],
                         len(grads), copies)


def _row_tile(rows, cols):
    tr = rows
    while tr * cols * 4 > ELEMENTWISE_BLOCK_BYTES and tr % 32 == 0:
        tr //= 2
    return tr


def _pair_sum(name, place, g, sib):
    half, cols = sib.shape[1], sib.shape[2]
    tr = _row_tile(half, cols)
    nt = half // tr
    mine = nt if g.shape[1] == 2 * half else 0

    def body(pc_ref, g_ref, s_ref, own_ref, out_ref):
        del pc_ref
        v = g_ref[...] + s_ref[...]
        out_ref[...] = v.astype(BF16)

        @pl.when(pl.program_id(1) == 0)
        def _():
            own_ref[...] = v

    return pl.pallas_call(
        body, name=name,
        grid_spec=pltpu.PrefetchScalarGridSpec(
            num_scalar_prefetch=1, grid=(nt, NCHIP),
            in_specs=[pl.BlockSpec((None, tr, cols), lambda i, s, pc: (pc[0] ^ s, pc[1] * mine + i, 0)),
                      pl.BlockSpec((None, tr, cols), lambda i, s, pc: (pc[0] ^ s, i, 0))],
            out_specs=[pl.BlockSpec((tr, cols), lambda i, s, pc: (i, 0)),
                       pl.BlockSpec((None, tr, cols), lambda i, s, pc: (s, i, 0))]),
        out_shape=[SDS((half, cols), F32), SDS((NCHIP, half, cols), BF16)],
        compiler_params=_cparams(),
    )(place, g, sib)


def _chip_exchange_job(parts):
    def copies(ins, outs, send_sem, recv_sem):
        x, y, c, _ = _place()
        res = []
        for w in range(len(parts)):
            for s in range(1, NCHIP):
                cx, cy = _chip_at(x, y, s)
                k = w * (NCHIP - 1) + s - 1
                res.append(pltpu.make_async_remote_copy(
                    src_ref=ins[w].at[s], dst_ref=outs[w].at[s - 1], send_sem=send_sem.at[k],
                    recv_sem=recv_sem.at[k], device_id=(cx, cy, c), device_id_type=MESH))
        return res

    return _exchange_job(parts, [SDS((NCHIP - 1,) + p.shape[1:], BF16) for p in parts],
                         len(parts) * (NCHIP - 1), copies)


def _chip_sum(name, own, rem):
    half, cols = own.shape
    tr = _row_tile(half, cols)

    def body(own_ref, rem_ref, out_ref):
        out_ref[...] = ((own_ref[...] + rem_ref[0].astype(F32)) + rem_ref[1].astype(F32)) + rem_ref[2].astype(F32)

    return pl.pallas_call(
        body, name=name, grid=(half // tr,),
        in_specs=[pl.BlockSpec((tr, cols), lambda i: (i, 0)),
                  pl.BlockSpec((NCHIP - 1, tr, cols), lambda i: (0, i, 0))],
        out_specs=pl.BlockSpec((tr, cols), lambda i: (i, 0)),
        out_shape=SDS((half, cols), F32),
        compiler_params=_cparams(),
    )(own, rem)


def _share_halves_job(halves):
    def copies(ins, outs, send_sem, recv_sem):
        x, y, c, _ = _place()
        return [pltpu.make_async_remote_copy(
            src_ref=ins[w], dst_ref=outs[w], send_sem=send_sem.at[w], recv_sem=recv_sem.at[w],
            device_id=(x, y, 1 - c), device_id_type=MESH) for w in range(len(halves))]

    return _exchange_job(halves, [SDS(h.shape, F32) for h in halves], len(halves), copies)


def _adamw_math(w, g, m, v):
    m = B1 * m + (1.0 - B1) * g
    v = B2 * v + (1.0 - B2) * (g * g)
    m_hat = m / (1.0 - B1 ** STEP)
    v_hat = v / (1.0 - B2 ** STEP)
    delta = -LR * (m_hat / (jnp.sqrt(v_hat) + AEPS) + WD * w)
    return delta, m, v


def _adamw(name, place, w, own, sib, m, v):
    rows, cols = w.shape
    by_cols = own.shape[0] == rows
    half, pc_cols = (rows, cols // 2) if by_cols else (rows // 2, cols)
    tr = _row_tile(half, pc_cols)
    nt = half // tr

    def body(pc_ref, w_ref, own_ref, sib_ref, m_ref, v_ref, g_ref, d_ref, mo_ref, vo_ref):
        g = jnp.where(pl.program_id(0) == pc_ref[1], own_ref[...], sib_ref[...])
        d, mn, vn = _adamw_math(w_ref[...], g, m_ref[...], v_ref[...])
        g_ref[...] = g
        d_ref[...] = d
        mo_ref[...] = mn
        vo_ref[...] = vn

    full = pl.BlockSpec((tr, pc_cols), (lambda h, i, pc: (i, h)) if by_cols else (lambda h, i, pc: (h * nt + i, 0)))
    part = pl.BlockSpec((tr, pc_cols), lambda h, i, pc: (i, 0))
    return pl.pallas_call(
        body, name=name,
        grid_spec=pltpu.PrefetchScalarGridSpec(
            num_scalar_prefetch=1, grid=(2, nt),
            in_specs=[full, part, part, full, full], out_specs=[full] * 4),
        out_shape=[SDS((rows, cols), F32)] * 4,
        compiler_params=_cparams(),
    )(place, w, own, sib, m, v)


def _small_allreduce_adamw(sp, w, m, v):
    shape = sp.shape

    def body(sp_ref, w_ref, m_ref, v_ref, g_ref, d_ref, mo_ref, vo_ref,
             sib_s, pair_s, chip_s, send_sem, recv_sem):
        x, y, c, j = _place()
        cp = pltpu.make_async_remote_copy(
            src_ref=sp_ref, dst_ref=sib_s, send_sem=send_sem.at[0], recv_sem=recv_sem.at[0],
            device_id=(x, y, 1 - c), device_id_type=MESH)
        cp.start()
        cp.wait()
        pair_s[...] = sp_ref[...] + sib_s[...]
        cps = []
        for s in range(1, NCHIP):
            cx, cy = _chip_at(x, y, s)
            cp = pltpu.make_async_remote_copy(
                src_ref=pair_s, dst_ref=chip_s.at[s], send_sem=send_sem.at[s], recv_sem=recv_sem.at[s],
                device_id=(cx, cy, c), device_id_type=MESH)
            cp.start()
            cps.append(cp)
        chip_s[0] = pair_s[...]
        for cp in cps:
            cp.wait()
        tot = chip_s[j]
        for k in range(1, NCHIP):
            tot = tot + chip_s[k ^ j]
        g_ref[...] = tot
        d, mn, vn = _adamw_math(w_ref[...], tot, m_ref[...], v_ref[...])
        d_ref[...] = d
        mo_ref[...] = mn
        vo_ref[...] = vn

    vm = pl.BlockSpec(memory_space=pltpu.VMEM)
    return pl.pallas_call(
        body, name="small_allreduce_adamw",
        in_specs=[vm] * 4, out_specs=[vm] * 4, out_shape=[SDS(shape, F32)] * 4,
        scratch_shapes=[pltpu.VMEM(shape, F32), pltpu.VMEM(shape, F32), pltpu.VMEM((NCHIP,) + shape, F32),
                        pltpu.SemaphoreType.DMA((NCHIP,)), pltpu.SemaphoreType.DMA((NCHIP,))],
        compiler_params=pltpu.CompilerParams(has_side_effects=True),
    )(sp, w, m, v)


def _pack_small(first, mix, ln_g, ln_b, b_s, lbt, hn, ffn, fin, w_s):
    rows = [first.reshape(1, D), mix.reshape(1, D), ln_g.reshape(1, D), ln_b.reshape(1, D),
            b_s.reshape(1, D), lbt.reshape(2, D), hn.reshape(1, D), ffn.reshape(1, D), fin.reshape(1, D),
            jnp.zeros((6, D), F32)]
    return jnp.concatenate(rows + [w_s.reshape(NG, GCH, GCH).transpose(1, 0, 2).reshape(GCH, D)], axis=0)


def _unpack_small(p):
    w_s = p[16:].reshape(GCH, NG, GCH).transpose(1, 0, 2).reshape(1, NG, GCH, GCH)
    return dict(norm_mix_g=p[1:2], gmlp_ln_g=p[2:3], gmlp_ln_b=p[3:4], gmlp_b_s=p[4].reshape(1, NG, GCH),
                hgrn_lb_table=p[5:7], hgrn_norm_g=p[7:8], norm_ffn_g=p[8:9], norm_final_g=p[9],
                gmlp_w_s=w_s)


SMALL = ("norm_mix_g", "gmlp_ln_g", "gmlp_ln_b", "gmlp_w_s", "gmlp_b_s", "hgrn_lb_table", "hgrn_norm_g",
         "norm_ffn_g", "norm_final_g")
BIG = ("w_in", "w_gate_up", "w_branch_a", "w_branch_b", "w_out", "w_down")
ORDER = ("norm_mix_g", "w_in", "gmlp_ln_g", "gmlp_ln_b", "gmlp_w_s", "gmlp_b_s", "hgrn_lb_table",
         "hgrn_norm_g", "w_branch_a", "w_branch_b", "w_out", "norm_ffn_g", "w_gate_up", "w_down",
         "norm_final_g")


def kernel(x, norm_mix_g, w_in, gmlp_ln_g, gmlp_ln_b, gmlp_w_s, gmlp_b_s, hgrn_lb_table, hgrn_norm_g, w_branch_a, w_branch_b, w_out, norm_ffn_g, w_gate_up, w_down, norm_final_g, loss_target, m_norm_mix_g, m_w_in, m_gmlp_ln_g, m_gmlp_ln_b, m_gmlp_w_s, m_gmlp_b_s, m_hgrn_lb_table, m_hgrn_norm_g, m_w_branch_a, m_w_branch_b, m_w_out, m_norm_ffn_g, m_w_gate_up, m_w_down, m_norm_final_g, v_norm_mix_g, v_w_in, v_gmlp_ln_g, v_gmlp_ln_b, v_gmlp_w_s, v_gmlp_b_s, v_hgrn_lb_table, v_hgrn_norm_g, v_w_branch_a, v_w_branch_b, v_w_out, v_norm_ffn_g, v_w_gate_up, v_w_down, v_norm_final_g):
    args = dict(locals())
    T = x.shape[1]
    xs = x.reshape(T, D)
    target = loss_target.reshape(T, D)
    big = {n: args[n].reshape(args[n].shape[1:]) for n in BIG}
    big_m = {n: args["m_" + n].reshape(args[n].shape[1:]) for n in BIG}
    big_v = {n: args["v_" + n].reshape(args[n].shape[1:]) for n in BIG}

    x_i, y_i, c_i = lax.axis_index("x"), lax.axis_index("y"), lax.axis_index("c")
    place = jnp.stack([2 * x_i + y_i, c_i]).astype(jnp.int32)
    cast = {n: _cast_shard("cast_" + n, place, big[n]) for n in BIG}
    tril = jnp.tril(jnp.ones((GCH, GCH), bool))
    wm = jnp.where(tril, gmlp_w_s[0], 0.0).astype(BF16)
    wm_t = jnp.swapaxes(wm, 1, 2)
    b_t = gmlp_b_s[0].T

    (proj, hb), w_in4, (w_a4, w_b4, w_out4, w_down4) = _proj_fwd(
        place, xs, norm_mix_g, cast["w_in"], [cast[n] for n in ("w_branch_a", "w_branch_b", "w_out", "w_down")])
    (ab,), _ = _gmlp_fwd(proj, gmlp_ln_g, gmlp_ln_b, wm, b_t)
    (o_raw, obb, st_before), (w_gu4,) = _hgrn_fwd(
        proj, hgrn_lb_table, hgrn_norm_g, job=_gather_job([cast["w_gate_up"]]))
    w_a, w_b, w_o = (w.reshape(D, D) for w in (w_a4, w_b4, w_out4))
    (mgb, x1), _ = _merge_fwd(xs, ab, obb, proj, w_a, w_b, w_o)
    w_dn = w_down4.reshape(FF, D)
    act, dx2b, h2b, dgu4, dx1, dx1b, acc_ffn = _ffn_fwd_bwd(
        x1, target, norm_ffn_g, norm_final_g.reshape(1, D), w_gu4, w_dn)

    grads, owns, parts, halves, sibh = {}, {}, {}, {}, {}

    def pair_sums(names, sibs):
        for n, s in zip(names, sibs):
            owns[n], parts[n] = _pair_sum("rs_pair_sum_" + n, place, grads[n], s)

    def chip_sums(names, got):
        for n, r in zip(names, got):
            halves[n] = _chip_sum("rs_chip_sum_" + n, owns[n], r)

    ffn, mix = ("w_gate_up", "w_down"), ("w_branch_a", "w_branch_b", "w_out")
    grads["w_gate_up"], _ = _dw_gate_up(h2b, dgu4)
    grads["w_down"], _ = _dw_down(act, dx2b)
    (dya, dyb, dproj), got = _merge_bwd(
        dx1b, ab, obb, proj, w_o, w_a, w_b, job=_pair_exchange_job([grads[n] for n in ffn]))
    pair_sums(ffn, got)
    grads["w_branch_a"], _ = _dw_square("dw_branch_a", ab, dya)
    grads["w_branch_b"], _ = _dw_square("dw_branch_b", obb, dyb)
    grads["w_out"], _ = _dw_square("dw_out", mgb, dx1b)
    (dproj, acc_hgrn), got = _hgrn_bwd(
        dproj, dyb, w_b, o_raw, proj, st_before, hgrn_lb_table, hgrn_norm_g,
        job=_join_jobs(_chip_exchange_job([parts[n] for n in ffn]), _pair_exchange_job([grads[n] for n in mix])))
    chip_sums(ffn, got[:2])
    pair_sums(mix, got[2:])
    dproj, acc_ln, dws, dmix = _gmlp_bwd(dproj, dya, w_a, proj, gmlp_ln_g, gmlp_ln_b, wm, wm_t, b_t)
    for_sibling, got = _dw_in_half(
        "dw_in_sibling_half", place, hb, dproj, False,
        job=_join_jobs(_share_halves_job([halves[n] for n in ffn]), _chip_exchange_job([parts[n] for n in mix])))
    sibh.update(zip(ffn, got[:2]))
    chip_sums(mix, got[2:])
    grads["w_in"], got = _dw_in_half(
        "dw_in_own_half", place, hb, dproj, True, job=_share_halves_job([for_sibling]))
    pair_sums(("w_in",), got)
    (grad_x, acc_mix), got = _proj_bwd(
        dproj, w_in4, xs, dx1, norm_mix_g,
        job=_join_jobs(_chip_exchange_job([parts["w_in"]]), _share_halves_job([halves[n] for n in mix])))
    chip_sums(("w_in",), got[:1])
    sibh.update(zip(mix, got[1:]))
    (sibh["w_in"],) = _run_job(_share_halves_job([halves["w_in"]]), "rs_share_halves_w_in")
    out = {}
    for n in BIG:
        g, d, mn, vn = _adamw("adamw_" + n, place, big[n], halves[n], sibh[n], big_m[n], big_v[n])
        shp = args[n].shape
        out[n] = (g.reshape(shp), d.reshape(shp), mn.reshape(shp), vn.reshape(shp))

    lbv = jax.nn.sigmoid(hgrn_lb_table[0] - hgrn_lb_table[1])
    d_t0 = jnp.sum(acc_hgrn[0], axis=0) * lbv * (1.0 - lbv)
    loss_row = jnp.zeros((D,), F32).at[0].set(jnp.sum(acc_ffn[0]))
    dws_m = jnp.where(tril[:, None, :], dws.reshape(GCH, NG, GCH), 0.0).transpose(1, 0, 2)
    db_s = jnp.sum(dmix.reshape(GCH, NG, GCH), axis=-1).T
    sp = _pack_small(loss_row, jnp.sum(acc_mix, 0), jnp.sum(acc_ln[0], 0), jnp.sum(acc_ln[1], 0), db_s,
                     jnp.stack([d_t0, -d_t0]), jnp.sum(acc_hgrn[1], 0), jnp.sum(acc_ffn[2], 0),
                     jnp.sum(acc_ffn[1], 0), dws_m)
    zero = jnp.zeros((D,), F32)

    def pack(prefix):
        a = lambda n: args[prefix + n]
        return _pack_small(zero, a("norm_mix_g"), a("gmlp_ln_g"), a("gmlp_ln_b"), a("gmlp_b_s"),
                           a("hgrn_lb_table"), a("hgrn_norm_g"), a("norm_ffn_g"), a("norm_final_g"),
                           a("gmlp_w_s"))

    packed = _small_allreduce_adamw(sp, pack(""), pack("m_"), pack("v_"))
    loss = packed[0][0, 0]
    small = [_unpack_small(p) for p in packed]
    for n in SMALL:
        out[n] = tuple(s[n] for s in small)
    return (loss, grad_x.reshape(x.shape), *[out[n][0] for n in ORDER], *[out[n][1] for n in ORDER],
            *[out[n][2] for n in ORDER], *[out[n][3] for n in ORDER])
```

```python
import functools
import math

import jax
import jax.numpy as jnp
from jax import lax
from jax.experimental import pallas as pl
from jax.experimental.pallas import tpu as pltpu

F32 = jnp.float32
BF16 = jnp.bfloat16
SDS = jax.ShapeDtypeStruct
MESH = pl.DeviceIdType.MESH
ANY = pl.BlockSpec(memory_space=pl.ANY)

D = 1024
NIN = 8
NG = 8
GCH = 128
NH = 8
HD = 128
HCH = 64
HGRN_HB = 4
HW = HGRN_HB * HD
DW_TOKENS = 2048
ELEMENTWISE_BLOCK_BYTES = 2 * 1024 * 1024
PROJ_OUT_SLOTS = 4
FF = 2816
FFS = 1408
NCHIP = 4
EPS = 1e-6
QSCALE = HD ** -0.5
GELU_C0 = math.sqrt(2.0 / math.pi)
GELU_C1 = 0.044715
LR, B1, B2, AEPS, WD, STEP = 0.001, 0.9, 0.999, 1e-08, 0.01, 10
VMEM_LIMIT_V7X = 56 * 1024 * 1024
SP_ROWS = 144


def _cparams(**kw):
    return pltpu.CompilerParams(vmem_limit_bytes=VMEM_LIMIT_V7X, **kw)


def _mm(a, b):
    return jnp.dot(a, b, preferred_element_type=F32)


def _mm_nt(a, b):
    return lax.dot_general(a, b, (((1,), (1,)), ((), ())), preferred_element_type=F32)


def _mm_tn(a, b):
    return lax.dot_general(a, b, (((0,), (0,)), ((), ())), preferred_element_type=F32)


def _rows8(x):
    r, c = x.shape
    return jnp.sum(x.reshape(r // 8, 8, c), axis=0)


def _mean(x):
    return jnp.mean(x, axis=-1, keepdims=True)


def _sigmoid(x):
    return 1.0 / (1.0 + jnp.exp(-x))


def _gelu(x):
    t = jnp.tanh(GELU_C0 * (x + GELU_C1 * x * x * x))
    return 0.5 * x * (1.0 + t), t


def _gelu_grad(x, t):
    return 0.5 * (1.0 + t) + 0.5 * x * (1.0 - t * t) * (GELU_C0 * (1.0 + 3.0 * GELU_C1 * x * x))


def _component_of(group):
    return jnp.where(group < 6, (group + 4) % 6, group)


def _proj_fwd(place, x, g_mix, w_in4, later):
    T = x.shape[0]
    tm = min(1024, T)
    ni = T // tm
    n = len(later)

    def body(pc_ref, x_ref, g_ref, *rest):
        proj_ref, h_ref, w_all = rest[1 + n:4 + n]
        gathered = rest[4 + n:4 + 2 * n]
        hs, wbuf, wsem, obuf, osem = rest[4 + 2 * n:9 + 2 * n]
        w_sems, later_sems = rest[9 + 2 * n:15 + 2 * n], rest[15 + 2 * n:]
        jp, i = pl.program_id(0), pl.program_id(1)
        w_cols = [w_all.at[:, :, pl.ds(k * D, D)] for k in range(2)]

        def w_copy(blk):
            cols = pl.ds(pl.multiple_of((blk % 2) * D, 128), D)
            return pltpu.make_async_copy(w_all.at[pc_ref[0] ^ (blk // 2), :, cols], wbuf.at[blk % 2],
                                         wsem.at[blk % 2])

        @pl.when((jp == 0) & (i == 0))
        def _():
            _gather_start(w_cols, w_sems)
            _gather_start(gathered, later_sems)
            w_copy(jp).start()

        @pl.when(i == 0)
        def _():
            w_copy(jp).wait()

        @pl.when(jp == 0)
        def _():
            xv = x_ref[...]
            r = lax.rsqrt(_mean(xv * xv) + EPS)
            hb = (xv * r * g_ref[...]).astype(BF16)
            hs[i] = hb
            h_ref[...] = hb

        step = jp * ni + i
        slot = step % PROJ_OUT_SLOTS

        def o_copy(slot_):
            comp = 2 * (pc_ref[0] ^ (jp // 2)) + jp % 2
            return pltpu.make_async_copy(
                obuf.at[slot_], proj_ref.at[comp, pl.ds(pl.multiple_of(i * tm, 8), tm)], osem.at[slot_])

        @pl.when(step >= PROJ_OUT_SLOTS)
        def _():
            o_copy(slot).wait()

        obuf[slot] = _mm(hs[i], wbuf[jp % 2])
        o_copy(slot).start()

        @pl.when(step == NIN * ni - 1)
        def _():
            for k in range(PROJ_OUT_SLOTS):
                o_copy((slot + 1 + k) % PROJ_OUT_SLOTS).wait()

        for nxt in range(1, NIN):
            @pl.when((jp == nxt - 1) & (i == ni - 1))
            def _():
                if nxt >= 2:
                    _gather_land([w_cols[nxt % 2]], w_sems, nxt // 2, first=nxt % 2)
                if nxt == 4:
                    _gather_neighbours(gathered, later_sems)
                w_copy(jp + 1).start()

        @pl.when((jp == NIN - 1) & (i == ni - 1))
        def _():
            _gather_drain(w_cols, w_sems)
            _gather_finish(gathered, later_sems)

    tile = lambda jp, i, pc: (jnp.where(jp == 0, i, ni - 1), 0)
    res = pl.pallas_call(
        body, name="proj_fwd",
        grid_spec=pltpu.PrefetchScalarGridSpec(
            num_scalar_prefetch=1, grid=(NIN, ni),
            in_specs=[pl.BlockSpec((tm, D), tile), pl.BlockSpec((1, D), lambda jp, i, pc: (0, 0))] + [ANY] * (1 + n),
            out_specs=[ANY, pl.BlockSpec((tm, D), tile)] + [ANY] * (1 + n),
            scratch_shapes=[pltpu.VMEM((ni, tm, D), BF16), pltpu.VMEM((2, D, D), BF16),
                            pltpu.SemaphoreType.DMA((2,)), pltpu.VMEM((PROJ_OUT_SLOTS, tm, D), F32),
                            pltpu.SemaphoreType.DMA((PROJ_OUT_SLOTS,))] + _gather_sems(2) + _gather_sems(n)),
        out_shape=[SDS((NIN, T, D), F32), SDS((T, D), BF16), SDS(w_in4.shape, BF16)]
        + [SDS(a.shape, a.dtype) for a in later],
        input_output_aliases={3 + k: 2 + k for k in range(1 + n)},
        compiler_params=_cparams(has_side_effects=True),
    )(place, x, g_mix, w_in4, *later)
    return res[:2], res[2], res[3:]


def _layer_norm_stats(gv):
    mu = _mean(gv)
    xc = gv - mu
    rs = lax.rsqrt(_mean(xc * xc) + EPS)
    return xc * rs, rs


def _gmlp_fwd(proj, ln_g, ln_b, wm, b_t, job=None):
    T = proj.shape[1]
    tm = min(256, T)

    def body(u_ref, v_ref, lg_ref, lb_ref, wm_ref, bt_ref, a_ref, a_s):
        gu, _ = _gelu(u_ref[...])
        gv, _ = _gelu(v_ref[...])
        vhat, _ = _layer_norm_stats(gv)
        vnb = (vhat * lg_ref[...] + lb_ref[...]).astype(BF16)
        for ch in range(tm // GCH):
            rows = slice(GCH * ch, GCH * (ch + 1))
            for g in range(NG):
                cols = slice(128 * g, 128 * (g + 1))
                mixed = _mm(wm_ref[g], vnb[rows, cols]) + bt_ref[:, g:g + 1]
                a_s[rows, cols] = gu[rows, cols] * mixed
        a_ref[...] = a_s[...].astype(BF16)

    row = lambda i: (0, 0)
    return _call(
        body, name="gmlp_fwd", grid=(T // tm,), job=job, args=(proj, proj, ln_g, ln_b, wm, b_t),
        in_specs=[pl.BlockSpec((None, tm, D), lambda i: (0, i, 0)), pl.BlockSpec((None, tm, D), lambda i: (1, i, 0)),
                  pl.BlockSpec((1, D), row), pl.BlockSpec((1, D), row),
                  pl.BlockSpec((NG, GCH, GCH), lambda i: (0, 0, 0)), pl.BlockSpec((GCH, NG), row)],
        out_specs=[pl.BlockSpec((tm, D), lambda i: (i, 0))],
        out_shape=[SDS((T, D), BF16)],
        scratch_shapes=[pltpu.VMEM((tm, D), F32)])


def _cumsum64(x, row):
    for s in (1, 2, 4, 8, 16, 32):
        x = x + jnp.where(row >= s, pltpu.roll(x, s, 0), 0.0)
    return x


def _revcumsum64(x, row):
    n = x.shape[0]
    for s in (1, 2, 4, 8, 16, 32):
        x = x + jnp.where(row < HCH - s, pltpu.roll(x, n - s, 0), 0.0)
    return x


def _head_mean(x):
    parts = [jnp.broadcast_to(_mean(x[:, HD * h:HD * (h + 1)]), (x.shape[0], HD)) for h in range(x.shape[1] // HD)]
    return jnp.concatenate(parts, axis=1)


def _seg_sum(x):
    n, c = x.shape
    s = jnp.sum(x.reshape(n // HCH, HCH, c), axis=1, keepdims=True)
    return jnp.broadcast_to(s, (n // HCH, HCH, c)).reshape(n, c)


def _hgrn_gates(fl, lbv, row):
    s = _sigmoid(fl)
    f = lbv + (1.0 - lbv) * s
    a = _cumsum64(jnp.log(f), row)
    a_mid = _seg_sum(jnp.where(row == HCH // 2 - 1, a, 0.0))
    a_last = _seg_sum(jnp.where(row == HCH - 1, a, 0.0))
    return s, f, a, a_mid, a_last


def _hgrn_fwd(proj, lb_table, norm_g, job=None):
    T = proj.shape[1]
    tb = min(512, T)
    nc = tb // HCH

    def body(q_ref, fl_ref, v_ref, g_ref, lbt_ref, gn_ref, o_ref, ob_ref, stb_ref, st_s, o_s):
        @pl.when(pl.program_id(1) == 0)
        def _():
            st_s[...] = jnp.zeros_like(st_s)

        row = lax.broadcasted_iota(jnp.int32, (tb, HW), 0) & (HCH - 1)
        lbv = _sigmoid(lbt_ref[0:1, :] - lbt_ref[1:2, :])
        _, f, a, a_mid, a_last = _hgrn_gates(fl_ref[...], lbv, row)
        k = 1.0 - f
        qs = q_ref[...] * QSCALE
        q_in = (qs * jnp.exp(a - a_mid)).astype(BF16)
        k_in = (k * jnp.exp(a_mid - a)).astype(BF16)
        q_a = (qs * jnp.exp(a)).astype(BF16)
        k_d = (k * jnp.exp(a_last - a)).astype(BF16)
        dec = jnp.exp(a_last)
        vb = v_ref[...].astype(BF16)
        tri = (lax.broadcasted_iota(jnp.int32, (HCH, HCH), 0)
               >= lax.broadcasted_iota(jnp.int32, (HCH, HCH), 1))
        for c in range(nc):
            sl = slice(HCH * c, HCH * (c + 1))
            for hh in range(HGRN_HB):
                hs = slice(HD * hh, HD * (hh + 1))
                st = st_s[hh]
                stb_ref[hh, c] = st
                sc = jnp.where(tri, _mm_nt(q_in[sl, hs], k_in[sl, hs]), 0.0)
                o_s[sl, hs] = _mm(sc.astype(BF16), vb[sl, hs]) + _mm_nt(q_a[sl, hs], st.astype(BF16))
                d64 = dec[sl, hs]
                st_s[hh] = st * jnp.concatenate([d64, d64], axis=0) + _mm_tn(vb[sl, hs], k_d[sl, hs])
        o = o_s[...]
        r = lax.rsqrt(_head_mean(o * o) + EPS)
        g = g_ref[...]
        o_ref[...] = o
        ob_ref[...] = (o * r * gn_ref[...] * (g * _sigmoid(g))).astype(BF16)

    def col(off):
        return pl.BlockSpec((None, tb, HW), lambda h, cb: (off, cb, h))

    return _call(
        body, name="hgrn_fwd", grid=(NH // HGRN_HB, T // tb), job=job,
        args=(proj, proj, proj, proj, lb_table, norm_g),
        in_specs=[col(2), col(3), col(4), col(5),
                  pl.BlockSpec((2, HW), lambda h, cb: (0, h)), pl.BlockSpec((1, HW), lambda h, cb: (0, h))],
        out_specs=[pl.BlockSpec((tb, HW), lambda h, cb: (cb, h)), pl.BlockSpec((tb, HW), lambda h, cb: (cb, h)),
                   pl.BlockSpec((HGRN_HB, nc, HD, HD), lambda h, cb: (h, cb, 0, 0))],
        out_shape=[SDS((T, D), F32), SDS((T, D), BF16), SDS((NH, T // HCH, HD, HD), F32)],
        scratch_shapes=[pltpu.VMEM((HGRN_HB, HD, HD), F32), pltpu.VMEM((tb, HW), F32)])


def _merge_fwd(x, ab, ob, proj, w_a, w_b, w_out, job=None):
    T = x.shape[0]
    tm = min(512, T)

    def body(x_ref, ab_ref, ob_ref, ga_ref, gb_ref, wa_ref, wb_ref, wo_ref, mg_ref, x1_ref):
        ya = _mm(ab_ref[...], wa_ref[...])
        yb = _mm(ob_ref[...], wb_ref[...])
        merged = (_sigmoid(ga_ref[...]) * ya + _sigmoid(gb_ref[...]) * yb).astype(BF16)
        mg_ref[...] = merged
        x1_ref[...] = x_ref[...] + _mm(merged, wo_ref[...])

    t = lambda i: (i, 0)
    w = lambda i: (0, 0)
    return _call(
        body, name="merge_fwd", grid=(T // tm,), job=job, args=(x, ab, ob, proj, proj, w_a, w_b, w_out),
        in_specs=[pl.BlockSpec((tm, D), t), pl.BlockSpec((tm, D), t), pl.BlockSpec((tm, D), t),
                  pl.BlockSpec((None, tm, D), lambda i: (6, i, 0)), pl.BlockSpec((None, tm, D), lambda i: (7, i, 0)),
                  pl.BlockSpec((D, D), w), pl.BlockSpec((D, D), w), pl.BlockSpec((D, D), w)],
        out_specs=[pl.BlockSpec((tm, D), t)] * 2,
        out_shape=[SDS((T, D), BF16), SDS((T, D), F32)])


def _ffn_fwd_bwd(x1, target, g_ffn, g_fin, w_gu4, w_down):
    T = x1.shape[0]
    tm = min(256, T)
    inv_d = 1.0 / D

    def body(x1_ref, tg_ref, gf_ref, gn_ref, wgu_ref, wd_ref,
             act_ref, dx2b_ref, h2b_ref, dgu_ref, dx1_ref, dx1b_ref, acc_ref):
        @pl.when(pl.program_id(0) == 0)
        def _():
            acc_ref[...] = jnp.zeros_like(acc_ref)

        x1v = x1_ref[...]
        gf = gf_ref[...]
        gn = gn_ref[...]
        rr1 = lax.rsqrt(_mean(x1v * x1v) + EPS)
        x1n = x1v * rr1
        h2b = (x1n * gf).astype(BF16)
        h2b_ref[...] = h2b
        p = [_mm(h2b, wgu_ref[k]) for k in range(NCHIP)]
        sg = [_sigmoid(p[0]), _sigmoid(p[1])]
        si = [p[0] * sg[0], p[1] * sg[1]]
        x2 = x1v
        for k in range(2):
            actk = (si[k] * p[2 + k]).astype(BF16)
            act_ref[:, FFS * k:FFS * (k + 1)] = actk
            x2 = x2 + _mm(actk, wd_ref[FFS * k:FFS * (k + 1), :])
        rr2 = lax.rsqrt(_mean(x2 * x2) + EPS)
        x2n = x2 * rr2
        e = x2n * gn - tg_ref[...]
        acc_ref[0] += _rows8(e * e) * (0.5 * inv_d)
        dy = e * inv_d
        acc_ref[1] += _rows8(dy * x2n)
        dxn = dy * gn
        dx2 = rr2 * (dxn - x2n * _mean(dxn * x2n))
        dx2b = dx2.astype(BF16)
        dx2b_ref[...] = dx2b
        dh2 = None
        for k in range(2):
            dact = _mm_nt(dx2b, wd_ref[FFS * k:FFS * (k + 1), :])
            dgate = (dact * p[2 + k] * (sg[k] * (1.0 + p[k] * (1.0 - sg[k])))).astype(BF16)
            dup = (dact * si[k]).astype(BF16)
            dgu_ref[k] = dgate
            dgu_ref[2 + k] = dup
            part = _mm_nt(dgate, wgu_ref[k]) + _mm_nt(dup, wgu_ref[2 + k])
            dh2 = part if dh2 is None else dh2 + part
        acc_ref[2] += _rows8(dh2 * x1n)
        dxn1 = dh2 * gf
        dx1 = dx2 + rr1 * (dxn1 - x1n * _mean(dxn1 * x1n))
        dx1_ref[...] = dx1
        dx1b_ref[...] = dx1.astype(BF16)

    t = lambda i: (i, 0)
    w = lambda i: (0, 0)
    one = pl.Buffered(1)
    return pl.pallas_call(
        body, name="ffn_fwd_bwd", grid=(T // tm,),
        in_specs=[pl.BlockSpec((tm, D), t), pl.BlockSpec((tm, D), t),
                  pl.BlockSpec((1, D), w), pl.BlockSpec((1, D), w),
                  pl.BlockSpec((NCHIP, D, FFS), lambda i: (0, 0, 0), pipeline_mode=one),
                  pl.BlockSpec((FF, D), w, pipeline_mode=one)],
        out_specs=[pl.BlockSpec((tm, FF), t), pl.BlockSpec((tm, D), t), pl.BlockSpec((tm, D), t),
                   pl.BlockSpec((NCHIP, tm, FFS), lambda i: (0, i, 0)),
                   pl.BlockSpec((tm, D), t), pl.BlockSpec((tm, D), t),
                   pl.BlockSpec((3, 8, D), lambda i: (0, 0, 0))],
        out_shape=[SDS((T, FF), BF16), SDS((T, D), BF16), SDS((T, D), BF16),
                   SDS((NCHIP, T, FFS), BF16), SDS((T, D), F32), SDS((T, D), BF16),
                   SDS((3, 8, D), F32)],
        compiler_params=_cparams(),
    )(x1, target, g_ffn, g_fin, w_gu4, w_down)


def _merge_bwd(dx1b, ab, ob, proj, w_out, w_a, w_b, job=None):
    T = dx1b.shape[0]
    tm = min(512, T)

    def body(dx_ref, ab_ref, ob_ref, ga_ref, gb_ref, wo_ref, wa_ref, wb_ref, dya_ref, dyb_ref, dp_ref):
        dm = _mm_nt(dx_ref[...], wo_ref[...])
        sa = _sigmoid(ga_ref[...])
        sb = _sigmoid(gb_ref[...])
        dya_ref[...] = (dm * sa).astype(BF16)
        dyb_ref[...] = (dm * sb).astype(BF16)
        dp_ref[0] = (dm * _mm(ab_ref[...], wa_ref[...]) * sa * (1.0 - sa)).astype(BF16)
        dp_ref[1] = (dm * _mm(ob_ref[...], wb_ref[...]) * sb * (1.0 - sb)).astype(BF16)

    t = lambda i: (i, 0)
    w = lambda i: (0, 0)
    return _call(
        body, name="merge_bwd", grid=(T // tm,),
        in_specs=[pl.BlockSpec((tm, D), t), pl.BlockSpec((tm, D), t), pl.BlockSpec((tm, D), t),
                  pl.BlockSpec((None, tm, D), lambda i: (6, i, 0)), pl.BlockSpec((None, tm, D), lambda i: (7, i, 0)),
                  pl.BlockSpec((D, D), w), pl.BlockSpec((D, D), w), pl.BlockSpec((D, D), w)],
        out_specs=[pl.BlockSpec((tm, D), t)] * 2 + [pl.BlockSpec((2, tm, D), lambda i: (3, i, 0))],
        out_shape=[SDS((T, D), BF16), SDS((T, D), BF16), SDS((NIN, T, D), BF16)],
        args=(dx1b, ab, ob, proj, proj, w_out, w_a, w_b), job=job)


def _hgrn_bwd(dproj, dyb, w_b, o_raw, proj, st_before, lb_table, norm_g, job=None):
    T = dyb.shape[0]
    tb = min(512, T)
    nc = tb // HCH
    nb = T // tb

    def body(dp_in, dyb_ref, wb_ref, o_ref, q_ref, fl_ref, v_ref, g_ref, stb_ref, lbt_ref, gn_ref,
             dp_ref, acc_ref, dst_s, dqin_s, dqa_s, dkin_s, dkd_s, dv_s, ddec_s):
        del dp_in

        @pl.when(pl.program_id(1) == 0)
        def _():
            dst_s[...] = jnp.zeros_like(dst_s)
            acc_ref[...] = jnp.zeros_like(acc_ref)

        row = lax.broadcasted_iota(jnp.int32, (tb, HW), 0) & (HCH - 1)
        gn = gn_ref[...]
        lbv = _sigmoid(lbt_ref[0:1, :] - lbt_ref[1:2, :])
        o = o_ref[...]
        r = lax.rsqrt(_head_mean(o * o) + EPS)
        on = o * r
        g = g_ref[...]
        sgm = _sigmoid(g)
        dob_v = _mm_nt(dyb_ref[...], wb_ref[...])
        dp_ref[3] = (dob_v * on * gn * (sgm * (1.0 + g * (1.0 - sgm)))).astype(BF16)
        do_n = dob_v * (g * sgm)
        acc_ref[1] += _rows8(do_n * on)
        dxn = do_n * gn
        do = (r * (dxn - on * _head_mean(dxn * on))).astype(BF16)
        s, f, a, a_mid, a_last = _hgrn_gates(fl_ref[...], lbv, row)
        k = 1.0 - f
        qs = q_ref[...] * QSCALE
        e_q = jnp.exp(a - a_mid)
        e_k = jnp.exp(a_mid - a)
        e_a = jnp.exp(a)
        e_l = jnp.exp(a_last - a)
        dec = jnp.exp(a_last)
        q_in = qs * e_q
        k_in = k * e_k
        q_a = qs * e_a
        k_d = k * e_l
        q_inb, k_inb, q_ab, k_db = (z.astype(BF16) for z in (q_in, k_in, q_a, k_d))
        vb = v_ref[...].astype(BF16)
        tri = (lax.broadcasted_iota(jnp.int32, (HCH, HCH), 0)
               >= lax.broadcasted_iota(jnp.int32, (HCH, HCH), 1))
        for c in reversed(range(nc)):
            sl = slice(HCH * c, HCH * (c + 1))
            for hh in range(HGRN_HB):
                hs = slice(HD * hh, HD * (hh + 1))
                stp = stb_ref[hh, c]
                dst = dst_s[hh]
                dstb = dst.astype(BF16)
                do_c = do[sl, hs]
                v_c = vb[sl, hs]
                dqa_s[sl, hs] = _mm(do_c, stp.astype(BF16))
                dkd_s[sl, hs] = _mm(v_c, dstb)
                ddec_s[sl, hs] = jnp.broadcast_to(jnp.sum(dst * stp, axis=0, keepdims=True), (HCH, HD))
                sc = jnp.where(tri, _mm_nt(q_inb[sl, hs], k_inb[sl, hs]), 0.0).astype(BF16)
                dsc = jnp.where(tri, _mm_nt(do_c, v_c), 0.0).astype(BF16)
                dv_s[sl, hs] = _mm_nt(k_db[sl, hs], dstb) + _mm_tn(sc, do_c)
                dqin_s[sl, hs] = _mm(dsc, k_inb[sl, hs])
                dkin_s[sl, hs] = _mm_tn(dsc, q_inb[sl, hs])
                d64 = dec[sl, hs]
                dst_s[hh] = dst * jnp.concatenate([d64, d64], axis=0) + _mm_tn(do_c, q_ab[sl, hs])
        dq_in = dqin_s[...]
        dq_a = dqa_s[...]
        dk_in = dkin_s[...]
        dk_d = dkd_s[...]
        dp_ref[0] = ((dq_in * e_q + dq_a * e_a) * QSCALE).astype(BF16)
        dp_ref[2] = dv_s[...].astype(BF16)
        tq = dq_in * q_in
        tk = dk_in * k_in
        td = dk_d * k_d
        d_a = tq + dq_a * q_a - tk - td
        d_a = d_a + jnp.where(row == HCH // 2 - 1, _seg_sum(tk - tq), 0.0)
        d_a = d_a + jnp.where(row == HCH - 1, _seg_sum(td) + ddec_s[...] * dec, 0.0)
        dlf = _revcumsum64(d_a, row)
        df = dlf / f - (dk_in * e_k + dk_d * e_l)
        dp_ref[1] = (df * (1.0 - lbv) * s * (1.0 - s)).astype(BF16)
        acc_ref[0] += _rows8(df * (1.0 - s))

    def col(off):
        return pl.BlockSpec((None, tb, HW), lambda h, cb: (off, nb - 1 - cb, h))

    hb = lambda h, cb: (nb - 1 - cb, h)
    return _call(
        body, name="hgrn_bwd", grid=(NH // HGRN_HB, nb), job=job,
        args=(dproj, dyb, w_b, o_raw, proj, proj, proj, proj, st_before, lb_table, norm_g),
        in_specs=[ANY, pl.BlockSpec((tb, D), lambda h, cb: (nb - 1 - cb, 0)),
                  pl.BlockSpec((HW, D), lambda h, cb: (h, 0)), pl.BlockSpec((tb, HW), hb),
                  col(2), col(3), col(4), col(5),
                  pl.BlockSpec((HGRN_HB, nc, HD, HD), lambda h, cb: (h, nb - 1 - cb, 0, 0)),
                  pl.BlockSpec((2, HW), lambda h, cb: (0, h)), pl.BlockSpec((1, HW), lambda h, cb: (0, h))],
        out_specs=[pl.BlockSpec((4, tb, HW), lambda h, cb: (0, nb - 1 - cb, h)),
                   pl.BlockSpec((2, 8, HW), lambda h, cb: (0, 0, h))],
        out_shape=[SDS(dproj.shape, BF16), SDS((2, 8, D), F32)],
        scratch_shapes=[pltpu.VMEM((HGRN_HB, HD, HD), F32)] + [pltpu.VMEM((tb, HW), F32)] * 6,
        aliases={0: 0})


def _gmlp_bwd(dproj, dya, w_a, proj, ln_g, ln_b, wm, wm_t, b_t):
    T = dya.shape[0]
    tm = min(256, T)

    def body(dp_in, dya_ref, wa_ref, u_ref, v_ref, lg_ref, lb_ref, wm_ref, wmt_ref, bt_ref,
             dp_ref, acc_ref, dws_ref, dmix_ref, du_s, dvn_s):
        del dp_in

        @pl.when(pl.program_id(0) == 0)
        def _():
            acc_ref[...] = jnp.zeros_like(acc_ref)
            dws_ref[...] = jnp.zeros_like(dws_ref)
            dmix_ref[...] = jnp.zeros_like(dmix_ref)

        u = u_ref[...]
        v = v_ref[...]
        lg = lg_ref[...]
        gu, t_u = _gelu(u)
        gv, t_v = _gelu(v)
        vhat, rs = _layer_norm_stats(gv)
        vnb = (vhat * lg + lb_ref[...]).astype(BF16)
        da_v = _mm_nt(dya_ref[...], wa_ref[...])
        for ch in range(tm // GCH):
            rows = slice(GCH * ch, GCH * (ch + 1))
            for g in range(NG):
                cols = slice(128 * g, 128 * (g + 1))
                vng = vnb[rows, cols]
                mixed = _mm(wm_ref[g], vng) + bt_ref[:, g:g + 1]
                dag = da_v[rows, cols]
                dmx = dag * gu[rows, cols]
                du_s[rows, cols] = dag * mixed
                dmxb = dmx.astype(BF16)
                dws_ref[:, cols] += _mm_nt(dmxb, vng)
                dmix_ref[:, cols] += dmx
                dvn_s[rows, cols] = _mm(wmt_ref[g], dmxb)
        dp_ref[0] = (du_s[...] * _gelu_grad(u, t_u)).astype(BF16)
        dvn = dvn_s[...]
        acc_ref[0] += _rows8(dvn * vhat)
        acc_ref[1] += _rows8(dvn)
        dvh = dvn * lg
        dgv = rs * (dvh - _mean(dvh) - vhat * _mean(dvh * vhat))
        dp_ref[1] = (dgv * _gelu_grad(v, t_v)).astype(BF16)

    row = lambda i: (0, 0)
    w3 = lambda i: (0, 0, 0)
    return pl.pallas_call(
        body, name="gmlp_bwd", grid=(T // tm,),
        in_specs=[ANY, pl.BlockSpec((tm, D), lambda i: (i, 0)), pl.BlockSpec((D, D), row),
                  pl.BlockSpec((None, tm, D), lambda i: (0, i, 0)), pl.BlockSpec((None, tm, D), lambda i: (1, i, 0)),
                  pl.BlockSpec((1, D), row), pl.BlockSpec((1, D), row),
                  pl.BlockSpec((NG, GCH, GCH), w3), pl.BlockSpec((NG, GCH, GCH), w3),
                  pl.BlockSpec((GCH, NG), row)],
        out_specs=[pl.BlockSpec((2, tm, D), lambda i: (2, i, 0)),
                   pl.BlockSpec((2, 8, D), w3), pl.BlockSpec((GCH, D), row), pl.BlockSpec((GCH, D), row)],
        out_shape=[SDS(dproj.shape, BF16), SDS((2, 8, D), F32), SDS((GCH, D), F32), SDS((GCH, D), F32)],
        scratch_shapes=[pltpu.VMEM((tm, D), F32), pltpu.VMEM((tm, D), F32)],
        input_output_aliases={0: 0},
        compiler_params=_cparams(),
    )(dproj, dya, w_a, proj, proj, ln_g, ln_b, wm, wm_t, b_t)


def _proj_bwd(dproj, w_in4, x, dx1, g_mix, job=None):
    T = x.shape[0]
    tm = min(256, T)
    order = (2, 3, 4, 5, 0, 1, 6, 7)

    def body(dp_ref, w_ref, x_ref, dx1_ref, g_ref, gx_ref, acc_ref):
        @pl.when(pl.program_id(0) == 0)
        def _():
            acc_ref[...] = jnp.zeros_like(acc_ref)

        dh = None
        for m, og in enumerate(order):
            part = _mm_nt(dp_ref[m], w_ref[og // 2, :, D * (og % 2):D * (og % 2 + 1)])
            dh = part if dh is None else dh + part
        xv = x_ref[...]
        r = lax.rsqrt(_mean(xv * xv) + EPS)
        xn = xv * r
        acc_ref[...] += _rows8(dh * xn)
        dxn = dh * g_ref[...]
        gx_ref[...] = dx1_ref[...] + r * (dxn - xn * _mean(dxn * xn))

    t = lambda i: (i, 0)
    return _call(
        body, name="proj_bwd", grid=(T // tm,),
        in_specs=[pl.BlockSpec((NIN, tm, D), lambda i: (0, i, 0)),
                  pl.BlockSpec((NCHIP, D, 2 * D), lambda i: (0, 0, 0), pipeline_mode=pl.Buffered(1)),
                  pl.BlockSpec((tm, D), t), pl.BlockSpec((tm, D), t), pl.BlockSpec((1, D), lambda i: (0, 0))],
        out_specs=[pl.BlockSpec((tm, D), t), pl.BlockSpec((8, D), lambda i: (0, 0))],
        out_shape=[SDS((T, D), F32), SDS((8, D), F32)],
        args=(dproj, w_in4, x, dx1, g_mix), job=job)


def _dw_call(name, a, b, a_spec, b_spec, o_spec, out_shape, nblk, tt, job=None, prefetch=None):
    T = a.shape[-2]

    def body(*refs):
        a_ref, b_ref, o_ref = refs[-3:]

        @pl.when(pl.program_id(1) == 0)
        def _():
            o_ref[...] = jnp.zeros_like(o_ref)
        o_ref[...] += _mm_tn(a_ref[...], b_ref[...])

    (out,), job_out = _call(
        body, name=name, grid=(nblk, T // tt), in_specs=[a_spec, b_spec], out_specs=[o_spec],
        out_shape=[out_shape], args=(a, b), job=job, prefetch=prefetch)
    return out, job_out


def _dw_in_half(name, place, hb, dproj, mine, job=None):
    tt = min(DW_TOKENS, hb.shape[0])

    def comp(k, pc):
        return _component_of(2 * k + (pc[1] if mine else 1 - pc[1]))

    return _dw_call(
        name, hb, dproj,
        pl.BlockSpec((tt, D), lambda k, t, pc: (t, 0)),
        pl.BlockSpec((None, tt, D), lambda k, t, pc: (comp(k, pc), t, 0)),
        pl.BlockSpec((None, D, D), lambda k, t, pc: (k, 0, 0)),
        SDS((NCHIP, D, D), F32), NCHIP, tt, job, place)


def _dw_gate_up(h2b, dgu4, job=None):
    tt = min(DW_TOKENS, h2b.shape[0])
    return _dw_call(
        "dw_gate_up", h2b, dgu4,
        pl.BlockSpec((tt, D), lambda k, t: (t, 0)),
        pl.BlockSpec((None, tt, FFS), lambda k, t: (k, t, 0)),
        pl.BlockSpec((None, D, FFS), lambda k, t: (k, 0, 0)),
        SDS((NCHIP, D, FFS), F32), NCHIP, tt, job)


def _dw_down(act, dx2b, job=None):
    tt = min(DW_TOKENS, act.shape[0])
    g, job_out = _dw_call(
        "dw_down", act, dx2b,
        pl.BlockSpec((tt, FFS), lambda k, t: (t, k)),
        pl.BlockSpec((tt, D), lambda k, t: (t, 0)),
        pl.BlockSpec((FFS, D), lambda k, t: (k, 0)),
        SDS((FF, D), F32), 2, tt, job)
    return g.reshape(NCHIP, FF // NCHIP, D), job_out


def _dw_square(name, a, b, job=None):
    tt = min(DW_TOKENS, a.shape[0])
    g, job_out = _dw_call(
        name, a, b,
        pl.BlockSpec((tt, D), lambda k, t: (t, 0)), pl.BlockSpec((tt, D), lambda k, t: (t, 0)),
        pl.BlockSpec((D, D), lambda k, t: (0, 0)), SDS((D, D), F32), 1, tt, job)
    return g.reshape(NCHIP, D // NCHIP, D), job_out


def _place():
    x, y, c = lax.axis_index("x"), lax.axis_index("y"), lax.axis_index("c")
    return x, y, c, 2 * x + y


def _chip_at(x, y, s):
    return x ^ (s >> 1), y ^ (s & 1)


class _Job:
    def __init__(self, ins, out_shapes, sems, start, finish, aliases=None, mid=None):
        self.ins, self.out_shapes, self.sems = list(ins), list(out_shapes), list(sems)
        self.start, self.finish, self.aliases = start, finish, dict(aliases or {})
        self.mid = mid if mid is not None else (lambda ins, outs, sems: None)


def _join_jobs(*jobs):
    def cut(refs, sizes):
        out, at = [], 0
        for n in sizes:
            out.append(refs[at:at + n])
            at += n
        return out

    ni = [len(j.ins) for j in jobs]
    no = [len(j.out_shapes) for j in jobs]
    ns = [len(j.sems) for j in jobs]

    def run(which):
        def go(ins, outs, sems):
            for j, a, b, c in zip(jobs, cut(ins, ni), cut(outs, no), cut(sems, ns)):
                getattr(j, which)(a, b, c)
        return go

    aliases = {}
    for k, j in enumerate(jobs):
        for a, b in j.aliases.items():
            aliases[sum(ni[:k]) + a] = sum(no[:k]) + b
    return _Job([a for j in jobs for a in j.ins], [o for j in jobs for o in j.out_shapes],
                [s for j in jobs for s in j.sems], run("start"), run("finish"), aliases, run("mid"))


def _call(body, *, name, grid, in_specs, out_specs, out_shape, args, scratch_shapes=(), aliases=None,
          job=None, prefetch=None):
    n_in, n_out, n_scr = len(in_specs), len(out_specs), len(scratch_shapes)
    npf = 0 if prefetch is None else 1
    job = job if job is not None else _Job([], [], [], lambda *a: None, lambda *a: None)
    ji, jo = len(job.ins), len(job.out_shapes)
    steps = math.prod(grid)

    def wrapped(*refs):
        pf, refs = refs[:npf], refs[npf:]
        ins, jin = refs[:n_in], refs[n_in:n_in + ji]
        o0 = n_in + ji
        outs, jout = refs[o0:o0 + n_out], refs[o0 + n_out:o0 + n_out + jo]
        s0 = o0 + n_out + jo
        scr, jsem = refs[s0:s0 + n_scr], refs[s0 + n_scr:]
        step = functools.reduce(lambda acc, ag: acc * ag[1] + pl.program_id(ag[0]), enumerate(grid), 0)
        if ji or jo:
            @pl.when(step == 0)
            def _():
                job.start(jin, jout, jsem)

        body(*pf, *ins, *outs, *scr)

        if ji or jo:
            @pl.when(step == steps // 2)
            def _():
                job.mid(jin, jout, jsem)

            @pl.when(step == steps - 1)
            def _():
                job.finish(jin, jout, jsem)

    io = {npf + a: b for a, b in dict(aliases or {}).items()}
    io.update({npf + n_in + a: n_out + b for a, b in job.aliases.items()})
    kw = dict(in_specs=list(in_specs) + [ANY] * ji, out_specs=list(out_specs) + [ANY] * jo,
              scratch_shapes=list(scratch_shapes) + job.sems)
    if npf:
        kw = dict(grid_spec=pltpu.PrefetchScalarGridSpec(num_scalar_prefetch=1, grid=grid, **kw))
    else:
        kw["grid"] = grid
    res = pl.pallas_call(
        wrapped, name=name, out_shape=list(out_shape) + job.out_shapes, input_output_aliases=io,
        compiler_params=_cparams(has_side_effects=bool(ji or jo)), **kw,
    )(*(() if prefetch is None else (prefetch,)), *args, *job.ins)
    return list(res[:n_out]), list(res[n_out:])


def _run_job(job, name):
    ji, jo = len(job.ins), len(job.out_shapes)

    def body(*refs):
        jin, jout, jsem = refs[:ji], refs[ji:ji + jo], refs[ji + jo:]
        job.start(jin, jout, jsem)
        job.finish(jin, jout, jsem)

    return list(pl.pallas_call(
        body, name=name, in_specs=[ANY] * ji, out_specs=[ANY] * jo, out_shape=job.out_shapes,
        scratch_shapes=job.sems, input_output_aliases=job.aliases,
        compiler_params=pltpu.CompilerParams(has_side_effects=True))(*job.ins))


def _cast_shard(name, place, w):
    rows, cols = w.shape
    tr = 352 if rows % 352 == 0 else 256

    def body(pc_ref, w_ref, o_ref):
        del pc_ref
        o_ref[...] = w_ref[...].astype(BF16)

    return pl.pallas_call(
        body, name=name,
        grid_spec=pltpu.PrefetchScalarGridSpec(
            num_scalar_prefetch=1, grid=(rows // tr,),
            in_specs=[pl.BlockSpec((tr, cols), lambda i, pc: (i, 0))],
            out_specs=pl.BlockSpec((None, tr, cols), lambda i, pc: (pc[0], i, 0))),
        out_shape=SDS((NCHIP, rows, cols), BF16),
        compiler_params=_cparams(),
    )(place, w)


def _sibling_copy(ref, send_sem, recv_sem):
    x, y, c, _ = _place()
    return pltpu.make_async_remote_copy(src_ref=ref, dst_ref=ref, send_sem=send_sem, recv_sem=recv_sem,
                                        device_id=(x, y, 1 - c), device_id_type=MESH)


def _half_rows(arr, slot, core):
    half = arr.shape[1] // 2
    return arr.at[slot, pl.ds(pl.multiple_of(core * half, 16), half)]


def _quarter_rows(arr, slot, core, q):
    quarter = arr.shape[1] // 4
    return arr.at[slot, pl.ds(pl.multiple_of((2 * core + q) * quarter, 16), quarter)]


def _chip_copy(ref, dist, send_sem, recv_sem):
    x, y, c, _ = _place()
    cx, cy = _chip_at(x, y, dist)
    return pltpu.make_async_remote_copy(src_ref=ref, dst_ref=ref, send_sem=send_sem, recv_sem=recv_sem,
                                        device_id=(cx, cy, c), device_id_type=MESH)


def _gather_sems(n):
    dma = pltpu.SemaphoreType.DMA
    return [dma((n, 2))] * 4 + [dma((n, 4))] * 2


def _gather_start(arrs, sems):
    dsend, drecv = sems[0], sems[1]
    _, _, c, j = _place()
    for w, arr in enumerate(arrs):
        for dist in (1, 2):
            _chip_copy(_half_rows(arr, j, c), dist, dsend.at[w, dist - 1], drecv.at[w, dist - 1]).start()


def _gather_land(arrs, sems, dist, first=0):
    dsend, drecv, rsend, rrecv, fsend, frecv = sems
    _, _, c, j = _place()
    if dist < 3:
        other = 3 - dist
        for w, arr in enumerate(arrs, first):
            landed = _half_rows(arr, j ^ dist, c)
            _chip_copy(landed, dist, dsend.at[w, dist - 1], drecv.at[w, dist - 1]).wait_recv()
            relay = _quarter_rows(arr, j ^ dist, c, other - 1)
            _chip_copy(relay, other, rsend.at[w, other - 1], rrecv.at[w, other - 1]).start()
            _sibling_copy(landed, fsend.at[w, dist - 1], frecv.at[w, dist - 1]).start()
        for w, arr in enumerate(arrs, first):
            theirs = _half_rows(arr, j ^ dist, 1 - c)
            _sibling_copy(theirs, fsend.at[w, dist - 1], frecv.at[w, dist - 1]).wait_recv()
    else:
        for w, arr in enumerate(arrs, first):
            for via in (1, 2):
                piece = _quarter_rows(arr, j ^ 3, c, via - 1)
                _chip_copy(piece, via, rsend.at[w, via - 1], rrecv.at[w, via - 1]).wait_recv()
                _sibling_copy(piece, fsend.at[w, 1 + via], frecv.at[w, 1 + via]).start()
        for w, arr in enumerate(arrs, first):
            for via in (1, 2):
                theirs = _quarter_rows(arr, j ^ 3, 1 - c, via - 1)
                _sibling_copy(theirs, fsend.at[w, 1 + via], frecv.at[w, 1 + via]).wait_recv()


def _gather_drain(arrs, sems):
    dsend, drecv, rsend, rrecv, fsend, frecv = sems
    _, _, c, j = _place()
    for w, arr in enumerate(arrs):
        for dist in (1, 2):
            other = 3 - dist
            _chip_copy(_half_rows(arr, j, c), dist, dsend.at[w, dist - 1], drecv.at[w, dist - 1]).wait_send()
            _chip_copy(_quarter_rows(arr, j ^ dist, c, other - 1), other,
                       rsend.at[w, other - 1], rrecv.at[w, other - 1]).wait_send()
            _sibling_copy(_half_rows(arr, j ^ dist, c), fsend.at[w, dist - 1], frecv.at[w, dist - 1]).wait_send()
            _sibling_copy(_quarter_rows(arr, j ^ 3, c, dist - 1),
                          fsend.at[w, 1 + dist], frecv.at[w, 1 + dist]).wait_send()


def _gather_neighbours(arrs, sems):
    _gather_land(arrs, sems, 1)
    _gather_land(arrs, sems, 2)


def _gather_finish(arrs, sems):
    _gather_land(arrs, sems, 3)
    _gather_drain(arrs, sems)


def _gather_job(arrs):
    n = len(arrs)
    return _Job(arrs, [SDS(a.shape, a.dtype) for a in arrs], _gather_sems(n),
                lambda ins, outs, sems: _gather_start(outs, sems),
                lambda ins, outs, sems: _gather_finish(outs, sems), {k: k for k in range(n)},
                mid=lambda ins, outs, sems: _gather_neighbours(outs, sems))


def _exchange_job(arrs, out_shapes, n, copies):
    def start(ins, outs, sems):
        for cp in copies(ins, outs, sems[0], sems[1]):
            cp.start()

    def finish(ins, outs, sems):
        for cp in copies(ins, outs, sems[0], sems[1]):
            cp.wait()

    return _Job(arrs, out_shapes, [pltpu.SemaphoreType.DMA((n,))] * 2, start, finish)


def _pair_exchange_job(grads):
    def copies(ins, outs, send_sem, recv_sem):
        x, y, c, _ = _place()
        res = []
        for w in range(len(grads)):
            half = ins[w].shape[1] // 2
            theirs = pl.ds(pl.multiple_of((1 - c) * half, 8), half)
            res.append(pltpu.make_async_remote_copy(
                src_ref=ins[w].at[:, theirs, :], dst_ref=outs[w], send_sem=send_sem.at[w],
                recv_sem=recv_sem.at[w], device_id=(x, y, 1 - c), device_id_type=MESH))
        return res

    return _exchange_job(grads, [SDS((NCHIP, g.shape[1] // 2, g.shape[2]), F32) for g in grads],
                         len(grads), copies)


def _row_tile(rows, cols):
    tr = rows
    while tr * cols * 4 > ELEMENTWISE_BLOCK_BYTES and tr % 32 == 0:
        tr //= 2
    return tr


def _pair_sum(name, place, g, sib):
    half, cols = sib.shape[1], sib.shape[2]
    tr = _row_tile(half, cols)
    nt = half // tr
    mine = nt if g.shape[1] == 2 * half else 0

    def body(pc_ref, g_ref, s_ref, own_ref, out_ref):
        del pc_ref
        v = g_ref[...] + s_ref[...]
        out_ref[...] = v.astype(BF16)

        @pl.when(pl.program_id(1) == 0)
        def _():
            own_ref[...] = v

    return pl.pallas_call(
        body, name=name,
        grid_spec=pltpu.PrefetchScalarGridSpec(
            num_scalar_prefetch=1, grid=(nt, NCHIP),
            in_specs=[pl.BlockSpec((None, tr, cols), lambda i, s, pc: (pc[0] ^ s, pc[1] * mine + i, 0)),
                      pl.BlockSpec((None, tr, cols), lambda i, s, pc: (pc[0] ^ s, i, 0))],
            out_specs=[pl.BlockSpec((tr, cols), lambda i, s, pc: (i, 0)),
                       pl.BlockSpec((None, tr, cols), lambda i, s, pc: (s, i, 0))]),
        out_shape=[SDS((half, cols), F32), SDS((NCHIP, half, cols), BF16)],
        compiler_params=_cparams(),
    )(place, g, sib)


def _chip_exchange_job(parts):
    def copies(ins, outs, send_sem, recv_sem):
        x, y, c, _ = _place()
        res = []
        for w in range(len(parts)):
            for s in range(1, NCHIP):
                cx, cy = _chip_at(x, y, s)
                k = w * (NCHIP - 1) + s - 1
                res.append(pltpu.make_async_remote_copy(
                    src_ref=ins[w].at[s], dst_ref=outs[w].at[s - 1], send_sem=send_sem.at[k],
                    recv_sem=recv_sem.at[k], device_id=(cx, cy, c), device_id_type=MESH))
        return res

    return _exchange_job(parts, [SDS((NCHIP - 1,) + p.shape[1:], BF16) for p in parts],
                         len(parts) * (NCHIP - 1), copies)


def _chip_sum(name, own, rem):
    half, cols = own.shape
    tr = _row_tile(half, cols)

    def body(own_ref, rem_ref, out_ref):
        out_ref[...] = ((own_ref[...] + rem_ref[0].astype(F32)) + rem_ref[1].astype(F32)) + rem_ref[2].astype(F32)

    return pl.pallas_call(
        body, name=name, grid=(half // tr,),
        in_specs=[pl.BlockSpec((tr, cols), lambda i: (i, 0)),
                  pl.BlockSpec((NCHIP - 1, tr, cols), lambda i: (0, i, 0))],
        out_specs=pl.BlockSpec((tr, cols), lambda i: (i, 0)),
        out_shape=SDS((half, cols), F32),
        compiler_params=_cparams(),
    )(own, rem)


def _share_halves_job(halves):
    def copies(ins, outs, send_sem, recv_sem):
        x, y, c, _ = _place()
        return [pltpu.make_async_remote_copy(
            src_ref=ins[w], dst_ref=outs[w], send_sem=send_sem.at[w], recv_sem=recv_sem.at[w],
            device_id=(x, y, 1 - c), device_id_type=MESH) for w in range(len(halves))]

    return _exchange_job(halves, [SDS(h.shape, F32) for h in halves], len(halves), copies)


def _adamw_math(w, g, m, v):
    m = B1 * m + (1.0 - B1) * g
    v = B2 * v + (1.0 - B2) * (g * g)
    m_hat = m / (1.0 - B1 ** STEP)
    v_hat = v / (1.0 - B2 ** STEP)
    delta = -LR * (m_hat / (jnp.sqrt(v_hat) + AEPS) + WD * w)
    return delta, m, v


def _adamw(name, place, w, own, sib, m, v):
    rows, cols = w.shape
    by_cols = own.shape[0] == rows
    half, pc_cols = (rows, cols // 2) if by_cols else (rows // 2, cols)
    tr = _row_tile(half, pc_cols)
    nt = half // tr

    def body(pc_ref, w_ref, own_ref, sib_ref, m_ref, v_ref, g_ref, d_ref, mo_ref, vo_ref):
        g = jnp.where(pl.program_id(0) == pc_ref[1], own_ref[...], sib_ref[...])
        d, mn, vn = _adamw_math(w_ref[...], g, m_ref[...], v_ref[...])
        g_ref[...] = g
        d_ref[...] = d
        mo_ref[...] = mn
        vo_ref[...] = vn

    full = pl.BlockSpec((tr, pc_cols), (lambda h, i, pc: (i, h)) if by_cols else (lambda h, i, pc: (h * nt + i, 0)))
    part = pl.BlockSpec((tr, pc_cols), lambda h, i, pc: (i, 0))
    return pl.pallas_call(
        body, name=name,
        grid_spec=pltpu.PrefetchScalarGridSpec(
            num_scalar_prefetch=1, grid=(2, nt),
            in_specs=[full, part, part, full, full], out_specs=[full] * 4),
        out_shape=[SDS((rows, cols), F32)] * 4,
        compiler_params=_cparams(),
    )(place, w, own, sib, m, v)


def _small_allreduce_adamw(sp, w, m, v):
    shape = sp.shape

    def body(sp_ref, w_ref, m_ref, v_ref, g_ref, d_ref, mo_ref, vo_ref,
             sib_s, pair_s, chip_s, send_sem, recv_sem):
        x, y, c, j = _place()
        cp = pltpu.make_async_remote_copy(
            src_ref=sp_ref, dst_ref=sib_s, send_sem=send_sem.at[0], recv_sem=recv_sem.at[0],
            device_id=(x, y, 1 - c), device_id_type=MESH)
        cp.start()
        cp.wait()
        pair_s[...] = sp_ref[...] + sib_s[...]
        cps = []
        for s in range(1, NCHIP):
            cx, cy = _chip_at(x, y, s)
            cp = pltpu.make_async_remote_copy(
                src_ref=pair_s, dst_ref=chip_s.at[s], send_sem=send_sem.at[s], recv_sem=recv_sem.at[s],
                device_id=(cx, cy, c), device_id_type=MESH)
            cp.start()
            cps.append(cp)
        chip_s[0] = pair_s[...]
        for cp in cps:
            cp.wait()
        tot = chip_s[j]
        for k in range(1, NCHIP):
            tot = tot + chip_s[k ^ j]
        g_ref[...] = tot
        d, mn, vn = _adamw_math(w_ref[...], tot, m_ref[...], v_ref[...])
        d_ref[...] = d
        mo_ref[...] = mn
        vo_ref[...] = vn

    vm = pl.BlockSpec(memory_space=pltpu.VMEM)
    return pl.pallas_call(
        body, name="small_allreduce_adamw",
        in_specs=[vm] * 4, out_specs=[vm] * 4, out_shape=[SDS(shape, F32)] * 4,
        scratch_shapes=[pltpu.VMEM(shape, F32), pltpu.VMEM(shape, F32), pltpu.VMEM((NCHIP,) + shape, F32),
                        pltpu.SemaphoreType.DMA((NCHIP,)), pltpu.SemaphoreType.DMA((NCHIP,))],
        compiler_params=pltpu.CompilerParams(has_side_effects=True),
    )(sp, w, m, v)


def _pack_small(first, mix, ln_g, ln_b, b_s, lbt, hn, ffn, fin, w_s):
    rows = [first.reshape(1, D), mix.reshape(1, D), ln_g.reshape(1, D), ln_b.reshape(1, D),
            b_s.reshape(1, D), lbt.reshape(2, D), hn.reshape(1, D), ffn.reshape(1, D), fin.reshape(1, D),
            jnp.zeros((6, D), F32)]
    return jnp.concatenate(rows + [w_s.reshape(NG, GCH, GCH).transpose(1, 0, 2).reshape(GCH, D)], axis=0)


def _unpack_small(p):
    w_s = p[16:].reshape(GCH, NG, GCH).transpose(1, 0, 2).reshape(1, NG, GCH, GCH)
    return dict(norm_mix_g=p[1:2], gmlp_ln_g=p[2:3], gmlp_ln_b=p[3:4], gmlp_b_s=p[4].reshape(1, NG, GCH),
                hgrn_lb_table=p[5:7], hgrn_norm_g=p[7:8], norm_ffn_g=p[8:9], norm_final_g=p[9],
                gmlp_w_s=w_s)


SMALL = ("norm_mix_g", "gmlp_ln_g", "gmlp_ln_b", "gmlp_w_s", "gmlp_b_s", "hgrn_lb_table", "hgrn_norm_g",
         "norm_ffn_g", "norm_final_g")
BIG = ("w_in", "w_gate_up", "w_branch_a", "w_branch_b", "w_out", "w_down")
ORDER = ("norm_mix_g", "w_in", "gmlp_ln_g", "gmlp_ln_b", "gmlp_w_s", "gmlp_b_s", "hgrn_lb_table",
         "hgrn_norm_g", "w_branch_a", "w_branch_b", "w_out", "norm_ffn_g", "w_gate_up", "w_down",
         "norm_final_g")


def kernel(x, norm_mix_g, w_in, gmlp_ln_g, gmlp_ln_b, gmlp_w_s, gmlp_b_s, hgrn_lb_table, hgrn_norm_g, w_branch_a, w_branch_b, w_out, norm_ffn_g, w_gate_up, w_down, norm_final_g, loss_target, m_norm_mix_g, m_w_in, m_gmlp_ln_g, m_gmlp_ln_b, m_gmlp_w_s, m_gmlp_b_s, m_hgrn_lb_table, m_hgrn_norm_g, m_w_branch_a, m_w_branch_b, m_w_out, m_norm_ffn_g, m_w_gate_up, m_w_down, m_norm_final_g, v_norm_mix_g, v_w_in, v_gmlp_ln_g, v_gmlp_ln_b, v_gmlp_w_s, v_gmlp_b_s, v_hgrn_lb_table, v_hgrn_norm_g, v_w_branch_a, v_w_branch_b, v_w_out, v_norm_ffn_g, v_w_gate_up, v_w_down, v_norm_final_g):
    args = dict(locals())
    T = x.shape[1]
    xs = x.reshape(T, D)
    target = loss_target.reshape(T, D)
    big = {n: args[n].reshape(args[n].shape[1:]) for n in BIG}
    big_m = {n: args["m_" + n].reshape(args[n].shape[1:]) for n in BIG}
    big_v = {n: args["v_" + n].reshape(args[n].shape[1:]) for n in BIG}

    x_i, y_i, c_i = lax.axis_index("x"), lax.axis_index("y"), lax.axis_index("c")
    place = jnp.stack([2 * x_i + y_i, c_i]).astype(jnp.int32)
    cast = {n: _cast_shard("cast_" + n, place, big[n]) for n in BIG}
    tril = jnp.tril(jnp.ones((GCH, GCH), bool))
    wm = jnp.where(tril, gmlp_w_s[0], 0.0).astype(BF16)
    wm_t = jnp.swapaxes(wm, 1, 2)
    b_t = gmlp_b_s[0].T

    (proj, hb), w_in4, (w_a4, w_b4, w_out4, w_down4) = _proj_fwd(
        place, xs, norm_mix_g, cast["w_in"], [cast[n] for n in ("w_branch_a", "w_branch_b", "w_out", "w_down")])
    (ab,), _ = _gmlp_fwd(proj, gmlp_ln_g, gmlp_ln_b, wm, b_t)
    (o_raw, obb, st_before), (w_gu4,) = _hgrn_fwd(
        proj, hgrn_lb_table, hgrn_norm_g, job=_gather_job([cast["w_gate_up"]]))
    w_a, w_b, w_o = (w.reshape(D, D) for w in (w_a4, w_b4, w_out4))
    (mgb, x1), _ = _merge_fwd(xs, ab, obb, proj, w_a, w_b, w_o)
    w_dn = w_down4.reshape(FF, D)
    act, dx2b, h2b, dgu4, dx1, dx1b, acc_ffn = _ffn_fwd_bwd(
        x1, target, norm_ffn_g, norm_final_g.reshape(1, D), w_gu4, w_dn)

    grads, owns, parts, halves, sibh = {}, {}, {}, {}, {}

    def pair_sums(names, sibs):
        for n, s in zip(names, sibs):
            owns[n], parts[n] = _pair_sum("rs_pair_sum_" + n, place, grads[n], s)

    def chip_sums(names, got):
        for n, r in zip(names, got):
            halves[n] = _chip_sum("rs_chip_sum_" + n, owns[n], r)

    ffn, mix = ("w_gate_up", "w_down"), ("w_branch_a", "w_branch_b", "w_out")
    grads["w_gate_up"], _ = _dw_gate_up(h2b, dgu4)
    grads["w_down"], _ = _dw_down(act, dx2b)
    (dya, dyb, dproj), got = _merge_bwd(
        dx1b, ab, obb, proj, w_o, w_a, w_b, job=_pair_exchange_job([grads[n] for n in ffn]))
    pair_sums(ffn, got)
    grads["w_branch_a"], _ = _dw_square("dw_branch_a", ab, dya)
    grads["w_branch_b"], _ = _dw_square("dw_branch_b", obb, dyb)
    grads["w_out"], _ = _dw_square("dw_out", mgb, dx1b)
    (dproj, acc_hgrn), got = _hgrn_bwd(
        dproj, dyb, w_b, o_raw, proj, st_before, hgrn_lb_table, hgrn_norm_g,
        job=_join_jobs(_chip_exchange_job([parts[n] for n in ffn]), _pair_exchange_job([grads[n] for n in mix])))
    chip_sums(ffn, got[:2])
    pair_sums(mix, got[2:])
    dproj, acc_ln, dws, dmix = _gmlp_bwd(dproj, dya, w_a, proj, gmlp_ln_g, gmlp_ln_b, wm, wm_t, b_t)
    for_sibling, got = _dw_in_half(
        "dw_in_sibling_half", place, hb, dproj, False,
        job=_join_jobs(_share_halves_job([halves[n] for n in ffn]), _chip_exchange_job([parts[n] for n in mix])))
    sibh.update(zip(ffn, got[:2]))
    chip_sums(mix, got[2:])
    grads["w_in"], got = _dw_in_half(
        "dw_in_own_half", place, hb, dproj, True, job=_share_halves_job([for_sibling]))
    pair_sums(("w_in",), got)
    (grad_x, acc_mix), got = _proj_bwd(
        dproj, w_in4, xs, dx1, norm_mix_g,
        job=_join_jobs(_chip_exchange_job([parts["w_in"]]), _share_halves_job([halves[n] for n in mix])))
    chip_sums(("w_in",), got[:1])
    sibh.update(zip(mix, got[1:]))
    (sibh["w_in"],) = _run_job(_share_halves_job([halves["w_in"]]), "rs_share_halves_w_in")
    out = {}
    for n in BIG:
        g, d, mn, vn = _adamw("adamw_" + n, place, big[n], halves[n], sibh[n], big_m[n], big_v[n])
        shp = args[n].shape
        out[n] = (g.reshape(shp), d.reshape(shp), mn.reshape(shp), vn.reshape(shp))

    lbv = jax.nn.sigmoid(hgrn_lb_table[0] - hgrn_lb_table[1])
    d_t0 = jnp.sum(acc_hgrn[0], axis=0) * lbv * (1.0 - lbv)
    loss_row = jnp.zeros((D,), F32).at[0].set(jnp.sum(acc_ffn[0]))
    dws_m = jnp.where(tril[:, None, :], dws.reshape(GCH, NG, GCH), 0.0).transpose(1, 0, 2)
    db_s = jnp.sum(dmix.reshape(GCH, NG, GCH), axis=-1).T
    sp = _pack_small(loss_row, jnp.sum(acc_mix, 0), jnp.sum(acc_ln[0], 0), jnp.sum(acc_ln[1], 0), db_s,
                     jnp.stack([d_t0, -d_t0]), jnp.sum(acc_hgrn[1], 0), jnp.sum(acc_ffn[2], 0),
                     jnp.sum(acc_ffn[1], 0), dws_m)
    zero = jnp.zeros((D,), F32)

    def pack(prefix):
        a = lambda n: args[prefix + n]
        return _pack_small(zero, a("norm_mix_g"), a("gmlp_ln_g"), a("gmlp_ln_b"), a("gmlp_b_s"),
                           a("hgrn_lb_table"), a("hgrn_norm_g"), a("norm_ffn_g"), a("norm_final_g"),
                           a("gmlp_w_s"))

    packed = _small_allreduce_adamw(sp, pack(""), pack("m_"), pack("v_"))
    loss = packed[0][0, 0]
    small = [_unpack_small(p) for p in packed]
    for n in SMALL:
        out[n] = tuple(s[n] for s in small)
    return (loss, grad_x.reshape(x.shape), *[out[n][0] for n in ORDER], *[out[n][1] for n in ORDER],
            *[out[n][2] for n in ORDER], *[out[n][3] for n in ORDER])
```

```python
import functools
import math

import jax
import jax.numpy as jnp
from jax import lax
from jax.experimental import pallas as pl
from jax.experimental.pallas import tpu as pltpu

F32 = jnp.float32
BF16 = jnp.bfloat16
SDS = jax.ShapeDtypeStruct
MESH = pl.DeviceIdType.MESH
ANY = pl.BlockSpec(memory_space=pl.ANY)

D = 1024
NIN = 8
NG = 8
GCH = 128
NH = 8
HD = 128
HCH = 64
HGRN_HB = 4
HW = HGRN_HB * HD
DW_TOKENS = 4096
ELEMENTWISE_BLOCK_BYTES = 2 * 1024 * 1024
PROJ_OUT_SLOTS = 4
FF = 2816
FFS = 1408
NCHIP = 4
EPS = 1e-6
QSCALE = HD ** -0.5
GELU_C0 = math.sqrt(2.0 / math.pi)
GELU_C1 = 0.044715
LR, B1, B2, AEPS, WD, STEP = 0.001, 0.9, 0.999, 1e-08, 0.01, 10
VMEM_LIMIT_V7X = 56 * 1024 * 1024
SP_ROWS = 144


def _cparams(**kw):
    return pltpu.CompilerParams(vmem_limit_bytes=VMEM_LIMIT_V7X, **kw)


def _mm(a, b):
    return jnp.dot(a, b, preferred_element_type=F32)


def _mm_nt(a, b):
    return lax.dot_general(a, b, (((1,), (1,)), ((), ())), preferred_element_type=F32)


def _mm_tn(a, b):
    return lax.dot_general(a, b, (((0,), (0,)), ((), ())), preferred_element_type=F32)


def _rows8(x):
    r, c = x.shape
    return jnp.sum(x.reshape(r // 8, 8, c), axis=0)


def _mean(x):
    return jnp.mean(x, axis=-1, keepdims=True)


def _sigmoid(x):
    return 1.0 / (1.0 + jnp.exp(-x))


def _gelu(x):
    t = jnp.tanh(GELU_C0 * (x + GELU_C1 * x * x * x))
    return 0.5 * x * (1.0 + t), t


def _gelu_grad(x, t):
    return 0.5 * (1.0 + t) + 0.5 * x * (1.0 - t * t) * (GELU_C0 * (1.0 + 3.0 * GELU_C1 * x * x))


def _component_of(group):
    return jnp.where(group < 6, (group + 4) % 6, group)


def _proj_fwd(place, x, g_mix, w_in4, later):
    T = x.shape[0]
    tm = min(1024, T)
    ni = T // tm
    n = len(later)

    def body(pc_ref, x_ref, g_ref, *rest):
        proj_ref, h_ref, w_all = rest[1 + n:4 + n]
        gathered = rest[4 + n:4 + 2 * n]
        hs, wbuf, wsem, obuf, osem = rest[4 + 2 * n:9 + 2 * n]
        w_sems, later_sems = rest[9 + 2 * n:15 + 2 * n], rest[15 + 2 * n:]
        jp, i = pl.program_id(0), pl.program_id(1)
        w_cols = [w_all.at[:, :, pl.ds(k * D, D)] for k in range(2)]

        def w_copy(blk):
            cols = pl.ds(pl.multiple_of((blk % 2) * D, 128), D)
            return pltpu.make_async_copy(w_all.at[pc_ref[0] ^ (blk // 2), :, cols], wbuf.at[blk % 2],
                                         wsem.at[blk % 2])

        @pl.when((jp == 0) & (i == 0))
        def _():
            _gather_start(w_cols, w_sems)
            _gather_start(gathered, later_sems)
            w_copy(jp).start()

        @pl.when(i == 0)
        def _():
            w_copy(jp).wait()

        @pl.when(jp == 0)
        def _():
            xv = x_ref[...]
            r = lax.rsqrt(_mean(xv * xv) + EPS)
            hb = (xv * r * g_ref[...]).astype(BF16)
            hs[i] = hb
            h_ref[...] = hb

        step = jp * ni + i
        slot = step % PROJ_OUT_SLOTS

        def o_copy(slot_):
            comp = 2 * (pc_ref[0] ^ (jp // 2)) + jp % 2
            return pltpu.make_async_copy(
                obuf.at[slot_], proj_ref.at[comp, pl.ds(pl.multiple_of(i * tm, 8), tm)], osem.at[slot_])

        @pl.when(step >= PROJ_OUT_SLOTS)
        def _():
            o_copy(slot).wait()

        obuf[slot] = _mm(hs[i], wbuf[jp % 2])
        o_copy(slot).start()

        @pl.when(step == NIN * ni - 1)
        def _():
            for k in range(PROJ_OUT_SLOTS):
                o_copy((slot + 1 + k) % PROJ_OUT_SLOTS).wait()

        for nxt in range(1, NIN):
            @pl.when((jp == nxt - 1) & (i == ni - 1))
            def _():
                if nxt >= 2:
                    _gather_land([w_cols[nxt % 2]], w_sems, nxt // 2, first=nxt % 2)
                if nxt == 4:
                    _gather_neighbours(gathered, later_sems)
                w_copy(jp + 1).start()

        @pl.when((jp == NIN - 1) & (i == ni - 1))
        def _():
            _gather_drain(w_cols, w_sems)
            _gather_finish(gathered, later_sems)

    tile = lambda jp, i, pc: (jnp.where(jp == 0, i, ni - 1), 0)
    res = pl.pallas_call(
        body, name="proj_fwd",
        grid_spec=pltpu.PrefetchScalarGridSpec(
            num_scalar_prefetch=1, grid=(NIN, ni),
            in_specs=[pl.BlockSpec((tm, D), tile), pl.BlockSpec((1, D), lambda jp, i, pc: (0, 0))] + [ANY] * (1 + n),
            out_specs=[ANY, pl.BlockSpec((tm, D), tile)] + [ANY] * (1 + n),
            scratch_shapes=[pltpu.VMEM((ni, tm, D), BF16), pltpu.VMEM((2, D, D), BF16),
                            pltpu.SemaphoreType.DMA((2,)), pltpu.VMEM((PROJ_OUT_SLOTS, tm, D), F32),
                            pltpu.SemaphoreType.DMA((PROJ_OUT_SLOTS,))] + _gather_sems(2) + _gather_sems(n)),
        out_shape=[SDS((NIN, T, D), F32), SDS((T, D), BF16), SDS(w_in4.shape, BF16)]
        + [SDS(a.shape, a.dtype) for a in later],
        input_output_aliases={3 + k: 2 + k for k in range(1 + n)},
        compiler_params=_cparams(has_side_effects=True),
    )(place, x, g_mix, w_in4, *later)
    return res[:2], res[2], res[3:]


def _layer_norm_stats(gv):
    mu = _mean(gv)
    xc = gv - mu
    rs = lax.rsqrt(_mean(xc * xc) + EPS)
    return xc * rs, rs


def _gmlp_fwd(proj, ln_g, ln_b, wm, b_t, job=None):
    T = proj.shape[1]
    tm = min(256, T)

    def body(u_ref, v_ref, lg_ref, lb_ref, wm_ref, bt_ref, a_ref, a_s):
        gu, _ = _gelu(u_ref[...])
        gv, _ = _gelu(v_ref[...])
        vhat, _ = _layer_norm_stats(gv)
        vnb = (vhat * lg_ref[...] + lb_ref[...]).astype(BF16)
        for ch in range(tm // GCH):
            rows = slice(GCH * ch, GCH * (ch + 1))
            for g in range(NG):
                cols = slice(128 * g, 128 * (g + 1))
                mixed = _mm(wm_ref[g], vnb[rows, cols]) + bt_ref[:, g:g + 1]
                a_s[rows, cols] = gu[rows, cols] * mixed
        a_ref[...] = a_s[...].astype(BF16)

    row = lambda i: (0, 0)
    return _call(
        body, name="gmlp_fwd", grid=(T // tm,), job=job, args=(proj, proj, ln_g, ln_b, wm, b_t),
        in_specs=[pl.BlockSpec((None, tm, D), lambda i: (0, i, 0)), pl.BlockSpec((None, tm, D), lambda i: (1, i, 0)),
                  pl.BlockSpec((1, D), row), pl.BlockSpec((1, D), row),
                  pl.BlockSpec((NG, GCH, GCH), lambda i: (0, 0, 0)), pl.BlockSpec((GCH, NG), row)],
        out_specs=[pl.BlockSpec((tm, D), lambda i: (i, 0))],
        out_shape=[SDS((T, D), BF16)],
        scratch_shapes=[pltpu.VMEM((tm, D), F32)])


def _cumsum64(x, row):
    for s in (1, 2, 4, 8, 16, 32):
        x = x + jnp.where(row >= s, pltpu.roll(x, s, 0), 0.0)
    return x


def _revcumsum64(x, row):
    n = x.shape[0]
    for s in (1, 2, 4, 8, 16, 32):
        x = x + jnp.where(row < HCH - s, pltpu.roll(x, n - s, 0), 0.0)
    return x


def _head_mean(x):
    parts = [jnp.broadcast_to(_mean(x[:, HD * h:HD * (h + 1)]), (x.shape[0], HD)) for h in range(x.shape[1] // HD)]
    return jnp.concatenate(parts, axis=1)


def _seg_sum(x):
    n, c = x.shape
    s = jnp.sum(x.reshape(n // HCH, HCH, c), axis=1, keepdims=True)
    return jnp.broadcast_to(s, (n // HCH, HCH, c)).reshape(n, c)


def _hgrn_gates(fl, lbv, row):
    s = _sigmoid(fl)
    f = lbv + (1.0 - lbv) * s
    a = _cumsum64(jnp.log(f), row)
    a_mid = _seg_sum(jnp.where(row == HCH // 2 - 1, a, 0.0))
    a_last = _seg_sum(jnp.where(row == HCH - 1, a, 0.0))
    return s, f, a, a_mid, a_last


def _hgrn_fwd(proj, lb_table, norm_g, job=None):
    T = proj.shape[1]
    tb = min(512, T)
    nc = tb // HCH

    def body(q_ref, fl_ref, v_ref, g_ref, lbt_ref, gn_ref, o_ref, ob_ref, stb_ref, st_s, o_s):
        @pl.when(pl.program_id(1) == 0)
        def _():
            st_s[...] = jnp.zeros_like(st_s)

        row = lax.broadcasted_iota(jnp.int32, (tb, HW), 0) & (HCH - 1)
        lbv = _sigmoid(lbt_ref[0:1, :] - lbt_ref[1:2, :])
        _, f, a, a_mid, a_last = _hgrn_gates(fl_ref[...], lbv, row)
        k = 1.0 - f
        qs = q_ref[...] * QSCALE
        q_in = (qs * jnp.exp(a - a_mid)).astype(BF16)
        k_in = (k * jnp.exp(a_mid - a)).astype(BF16)
        q_a = (qs * jnp.exp(a)).astype(BF16)
        k_d = (k * jnp.exp(a_last - a)).astype(BF16)
        dec = jnp.exp(a_last)
        vb = v_ref[...].astype(BF16)
        tri = (lax.broadcasted_iota(jnp.int32, (HCH, HCH), 0)
               >= lax.broadcasted_iota(jnp.int32, (HCH, HCH), 1))
        for c in range(nc):
            sl = slice(HCH * c, HCH * (c + 1))
            for hh in range(HGRN_HB):
                hs = slice(HD * hh, HD * (hh + 1))
                st = st_s[hh]
                stb_ref[hh, c] = st
                sc = jnp.where(tri, _mm_nt(q_in[sl, hs], k_in[sl, hs]), 0.0)
                o_s[sl, hs] = _mm(sc.astype(BF16), vb[sl, hs]) + _mm_nt(q_a[sl, hs], st.astype(BF16))
                d64 = dec[sl, hs]
                st_s[hh] = st * jnp.concatenate([d64, d64], axis=0) + _mm_tn(vb[sl, hs], k_d[sl, hs])
        o = o_s[...]
        r = lax.rsqrt(_head_mean(o * o) + EPS)
        g = g_ref[...]
        o_ref[...] = o
        ob_ref[...] = (o * r * gn_ref[...] * (g * _sigmoid(g))).astype(BF16)

    def col(off):
        return pl.BlockSpec((None, tb, HW), lambda h, cb: (off, cb, h))

    return _call(
        body, name="hgrn_fwd", grid=(NH // HGRN_HB, T // tb), job=job,
        args=(proj, proj, proj, proj, lb_table, norm_g),
        in_specs=[col(2), col(3), col(4), col(5),
                  pl.BlockSpec((2, HW), lambda h, cb: (0, h)), pl.BlockSpec((1, HW), lambda h, cb: (0, h))],
        out_specs=[pl.BlockSpec((tb, HW), lambda h, cb: (cb, h)), pl.BlockSpec((tb, HW), lambda h, cb: (cb, h)),
                   pl.BlockSpec((HGRN_HB, nc, HD, HD), lambda h, cb: (h, cb, 0, 0))],
        out_shape=[SDS((T, D), F32), SDS((T, D), BF16), SDS((NH, T // HCH, HD, HD), F32)],
        scratch_shapes=[pltpu.VMEM((HGRN_HB, HD, HD), F32), pltpu.VMEM((tb, HW), F32)])


def _merge_fwd(x, ab, ob, proj, w_a, w_b, w_out, job=None):
    T = x.shape[0]
    tm = min(512, T)

    def body(x_ref, ab_ref, ob_ref, ga_ref, gb_ref, wa_ref, wb_ref, wo_ref, mg_ref, x1_ref):
        ya = _mm(ab_ref[...], wa_ref[...])
        yb = _mm(ob_ref[...], wb_ref[...])
        merged = (_sigmoid(ga_ref[...]) * ya + _sigmoid(gb_ref[...]) * yb).astype(BF16)
        mg_ref[...] = merged
        x1_ref[...] = x_ref[...] + _mm(merged, wo_ref[...])

    t = lambda i: (i, 0)
    w = lambda i: (0, 0)
    return _call(
        body, name="merge_fwd", grid=(T // tm,), job=job, args=(x, ab, ob, proj, proj, w_a, w_b, w_out),
        in_specs=[pl.BlockSpec((tm, D), t), pl.BlockSpec((tm, D), t), pl.BlockSpec((tm, D), t),
                  pl.BlockSpec((None, tm, D), lambda i: (6, i, 0)), pl.BlockSpec((None, tm, D), lambda i: (7, i, 0)),
                  pl.BlockSpec((D, D), w), pl.BlockSpec((D, D), w), pl.BlockSpec((D, D), w)],
        out_specs=[pl.BlockSpec((tm, D), t)] * 2,
        out_shape=[SDS((T, D), BF16), SDS((T, D), F32)])


def _ffn_fwd_bwd(x1, target, g_ffn, g_fin, w_gu4, w_down):
    T = x1.shape[0]
    tm = min(256, T)
    inv_d = 1.0 / D

    def body(x1_ref, tg_ref, gf_ref, gn_ref, wgu_ref, wd_ref,
             act_ref, dx2b_ref, h2b_ref, dgu_ref, dx1_ref, dx1b_ref, acc_ref):
        @pl.when(pl.program_id(0) == 0)
        def _():
            acc_ref[...] = jnp.zeros_like(acc_ref)

        x1v = x1_ref[...]
        gf = gf_ref[...]
        gn = gn_ref[...]
        rr1 = lax.rsqrt(_mean(x1v * x1v) + EPS)
        x1n = x1v * rr1
        h2b = (x1n * gf).astype(BF16)
        h2b_ref[...] = h2b
        p = [_mm(h2b, wgu_ref[k]) for k in range(NCHIP)]
        sg = [_sigmoid(p[0]), _sigmoid(p[1])]
        si = [p[0] * sg[0], p[1] * sg[1]]
        x2 = x1v
        for k in range(2):
            actk = (si[k] * p[2 + k]).astype(BF16)
            act_ref[:, FFS * k:FFS * (k + 1)] = actk
            x2 = x2 + _mm(actk, wd_ref[FFS * k:FFS * (k + 1), :])
        rr2 = lax.rsqrt(_mean(x2 * x2) + EPS)
        x2n = x2 * rr2
        e = x2n * gn - tg_ref[...]
        acc_ref[0] += _rows8(e * e) * (0.5 * inv_d)
        dy = e * inv_d
        acc_ref[1] += _rows8(dy * x2n)
        dxn = dy * gn
        dx2 = rr2 * (dxn - x2n * _mean(dxn * x2n))
        dx2b = dx2.astype(BF16)
        dx2b_ref[...] = dx2b
        dh2 = None
        for k in range(2):
            dact = _mm_nt(dx2b, wd_ref[FFS * k:FFS * (k + 1), :])
            dgate = (dact * p[2 + k] * (sg[k] * (1.0 + p[k] * (1.0 - sg[k])))).astype(BF16)
            dup = (dact * si[k]).astype(BF16)
            dgu_ref[k] = dgate
            dgu_ref[2 + k] = dup
            part = _mm_nt(dgate, wgu_ref[k]) + _mm_nt(dup, wgu_ref[2 + k])
            dh2 = part if dh2 is None else dh2 + part
        acc_ref[2] += _rows8(dh2 * x1n)
        dxn1 = dh2 * gf
        dx1 = dx2 + rr1 * (dxn1 - x1n * _mean(dxn1 * x1n))
        dx1_ref[...] = dx1
        dx1b_ref[...] = dx1.astype(BF16)

    t = lambda i: (i, 0)
    w = lambda i: (0, 0)
    one = pl.Buffered(1)
    return pl.pallas_call(
        body, name="ffn_fwd_bwd", grid=(T // tm,),
        in_specs=[pl.BlockSpec((tm, D), t), pl.BlockSpec((tm, D), t),
                  pl.BlockSpec((1, D), w), pl.BlockSpec((1, D), w),
                  pl.BlockSpec((NCHIP, D, FFS), lambda i: (0, 0, 0), pipeline_mode=one),
                  pl.BlockSpec((FF, D), w, pipeline_mode=one)],
        out_specs=[pl.BlockSpec((tm, FF), t), pl.BlockSpec((tm, D), t), pl.BlockSpec((tm, D), t),
                   pl.BlockSpec((NCHIP, tm, FFS), lambda i: (0, i, 0)),
                   pl.BlockSpec((tm, D), t), pl.BlockSpec((tm, D), t),
                   pl.BlockSpec((3, 8, D), lambda i: (0, 0, 0))],
        out_shape=[SDS((T, FF), BF16), SDS((T, D), BF16), SDS((T, D), BF16),
                   SDS((NCHIP, T, FFS), BF16), SDS((T, D), F32), SDS((T, D), BF16),
                   SDS((3, 8, D), F32)],
        compiler_params=_cparams(),
    )(x1, target, g_ffn, g_fin, w_gu4, w_down)


def _merge_bwd(dx1b, ab, ob, proj, w_out, w_a, w_b, job=None):
    T = dx1b.shape[0]
    tm = min(512, T)

    def body(dx_ref, ab_ref, ob_ref, ga_ref, gb_ref, wo_ref, wa_ref, wb_ref, dya_ref, dyb_ref, dp_ref):
        dm = _mm_nt(dx_ref[...], wo_ref[...])
        sa = _sigmoid(ga_ref[...])
        sb = _sigmoid(gb_ref[...])
        dya_ref[...] = (dm * sa).astype(BF16)
        dyb_ref[...] = (dm * sb).astype(BF16)
        dp_ref[0] = (dm * _mm(ab_ref[...], wa_ref[...]) * sa * (1.0 - sa)).astype(BF16)
        dp_ref[1] = (dm * _mm(ob_ref[...], wb_ref[...]) * sb * (1.0 - sb)).astype(BF16)

    t = lambda i: (i, 0)
    w = lambda i: (0, 0)
    return _call(
        body, name="merge_bwd", grid=(T // tm,),
        in_specs=[pl.BlockSpec((tm, D), t), pl.BlockSpec((tm, D), t), pl.BlockSpec((tm, D), t),
                  pl.BlockSpec((None, tm, D), lambda i: (6, i, 0)), pl.BlockSpec((None, tm, D), lambda i: (7, i, 0)),
                  pl.BlockSpec((D, D), w), pl.BlockSpec((D, D), w), pl.BlockSpec((D, D), w)],
        out_specs=[pl.BlockSpec((tm, D), t)] * 2 + [pl.BlockSpec((2, tm, D), lambda i: (3, i, 0))],
        out_shape=[SDS((T, D), BF16), SDS((T, D), BF16), SDS((NIN, T, D), BF16)],
        args=(dx1b, ab, ob, proj, proj, w_out, w_a, w_b), job=job)


def _hgrn_bwd(dproj, dyb, w_b, o_raw, proj, st_before, lb_table, norm_g, job=None):
    T = dyb.shape[0]
    tb = min(512, T)
    nc = tb // HCH
    nb = T // tb

    def body(dp_in, dyb_ref, wb_ref, o_ref, q_ref, fl_ref, v_ref, g_ref, stb_ref, lbt_ref, gn_ref,
             dp_ref, acc_ref, dst_s, dqin_s, dqa_s, dkin_s, dkd_s, dv_s, ddec_s):
        del dp_in

        @pl.when(pl.program_id(1) == 0)
        def _():
            dst_s[...] = jnp.zeros_like(dst_s)
            acc_ref[...] = jnp.zeros_like(acc_ref)

        row = lax.broadcasted_iota(jnp.int32, (tb, HW), 0) & (HCH - 1)
        gn = gn_ref[...]
        lbv = _sigmoid(lbt_ref[0:1, :] - lbt_ref[1:2, :])
        o = o_ref[...]
        r = lax.rsqrt(_head_mean(o * o) + EPS)
        on = o * r
        g = g_ref[...]
        sgm = _sigmoid(g)
        dob_v = _mm_nt(dyb_ref[...], wb_ref[...])
        dp_ref[3] = (dob_v * on * gn * (sgm * (1.0 + g * (1.0 - sgm)))).astype(BF16)
        do_n = dob_v * (g * sgm)
        acc_ref[1] += _rows8(do_n * on)
        dxn = do_n * gn
        do = (r * (dxn - on * _head_mean(dxn * on))).astype(BF16)
        s, f, a, a_mid, a_last = _hgrn_gates(fl_ref[...], lbv, row)
        k = 1.0 - f
        qs = q_ref[...] * QSCALE
        e_q = jnp.exp(a - a_mid)
        e_k = jnp.exp(a_mid - a)
        e_a = jnp.exp(a)
        e_l = jnp.exp(a_last - a)
        dec = jnp.exp(a_last)
        q_in = qs * e_q
        k_in = k * e_k
        q_a = qs * e_a
        k_d = k * e_l
        q_inb, k_inb, q_ab, k_db = (z.astype(BF16) for z in (q_in, k_in, q_a, k_d))
        vb = v_ref[...].astype(BF16)
        tri = (lax.broadcasted_iota(jnp.int32, (HCH, HCH), 0)
               >= lax.broadcasted_iota(jnp.int32, (HCH, HCH), 1))
        for c in reversed(range(nc)):
            sl = slice(HCH * c, HCH * (c + 1))
            for hh in range(HGRN_HB):
                hs = slice(HD * hh, HD * (hh + 1))
                stp = stb_ref[hh, c]
                dst = dst_s[hh]
                dstb = dst.astype(BF16)
                do_c = do[sl, hs]
                v_c = vb[sl, hs]
                dqa_s[sl, hs] = _mm(do_c, stp.astype(BF16))
                dkd_s[sl, hs] = _mm(v_c, dstb)
                ddec_s[sl, hs] = jnp.broadcast_to(jnp.sum(dst * stp, axis=0, keepdims=True), (HCH, HD))
                sc = jnp.where(tri, _mm_nt(q_inb[sl, hs], k_inb[sl, hs]), 0.0).astype(BF16)
                dsc = jnp.where(tri, _mm_nt(do_c, v_c), 0.0).astype(BF16)
                dv_s[sl, hs] = _mm_nt(k_db[sl, hs], dstb) + _mm_tn(sc, do_c)
                dqin_s[sl, hs] = _mm(dsc, k_inb[sl, hs])
                dkin_s[sl, hs] = _mm_tn(dsc, q_inb[sl, hs])
                d64 = dec[sl, hs]
                dst_s[hh] = dst * jnp.concatenate([d64, d64], axis=0) + _mm_tn(do_c, q_ab[sl, hs])
        dq_in = dqin_s[...]
        dq_a = dqa_s[...]
        dk_in = dkin_s[...]
        dk_d = dkd_s[...]
        dp_ref[0] = ((dq_in * e_q + dq_a * e_a) * QSCALE).astype(BF16)
        dp_ref[2] = dv_s[...].astype(BF16)
        tq = dq_in * q_in
        tk = dk_in * k_in
        td = dk_d * k_d
        d_a = tq + dq_a * q_a - tk - td
        d_a = d_a + jnp.where(row == HCH // 2 - 1, _seg_sum(tk - tq), 0.0)
        d_a = d_a + jnp.where(row == HCH - 1, _seg_sum(td) + ddec_s[...] * dec, 0.0)
        dlf = _revcumsum64(d_a, row)
        df = dlf / f - (dk_in * e_k + dk_d * e_l)
        dp_ref[1] = (df * (1.0 - lbv) * s * (1.0 - s)).astype(BF16)
        acc_ref[0] += _rows8(df * (1.0 - s))

    def col(off):
        return pl.BlockSpec((None, tb, HW), lambda h, cb: (off, nb - 1 - cb, h))

    hb = lambda h, cb: (nb - 1 - cb, h)
    return _call(
        body, name="hgrn_bwd", grid=(NH // HGRN_HB, nb), job=job,
        args=(dproj, dyb, w_b, o_raw, proj, proj, proj, proj, st_before, lb_table, norm_g),
        in_specs=[ANY, pl.BlockSpec((tb, D), lambda h, cb: (nb - 1 - cb, 0)),
                  pl.BlockSpec((HW, D), lambda h, cb: (h, 0)), pl.BlockSpec((tb, HW), hb),
                  col(2), col(3), col(4), col(5),
                  pl.BlockSpec((HGRN_HB, nc, HD, HD), lambda h, cb: (h, nb - 1 - cb, 0, 0)),
                  pl.BlockSpec((2, HW), lambda h, cb: (0, h)), pl.BlockSpec((1, HW), lambda h, cb: (0, h))],
        out_specs=[pl.BlockSpec((4, tb, HW), lambda h, cb: (0, nb - 1 - cb, h)),
                   pl.BlockSpec((2, 8, HW), lambda h, cb: (0, 0, h))],
        out_shape=[SDS(dproj.shape, BF16), SDS((2, 8, D), F32)],
        scratch_shapes=[pltpu.VMEM((HGRN_HB, HD, HD), F32)] + [pltpu.VMEM((tb, HW), F32)] * 6,
        aliases={0: 0})


def _gmlp_bwd(dproj, dya, w_a, proj, ln_g, ln_b, wm, wm_t, b_t):
    T = dya.shape[0]
    tm = min(256, T)

    def body(dp_in, dya_ref, wa_ref, u_ref, v_ref, lg_ref, lb_ref, wm_ref, wmt_ref, bt_ref,
             dp_ref, acc_ref, dws_ref, dmix_ref, du_s, dvn_s):
        del dp_in

        @pl.when(pl.program_id(0) == 0)
        def _():
            acc_ref[...] = jnp.zeros_like(acc_ref)
            dws_ref[...] = jnp.zeros_like(dws_ref)
            dmix_ref[...] = jnp.zeros_like(dmix_ref)

        u = u_ref[...]
        v = v_ref[...]
        lg = lg_ref[...]
        gu, t_u = _gelu(u)
        gv, t_v = _gelu(v)
        vhat, rs = _layer_norm_stats(gv)
        vnb = (vhat * lg + lb_ref[...]).astype(BF16)
        da_v = _mm_nt(dya_ref[...], wa_ref[...])
        for ch in range(tm // GCH):
            rows = slice(GCH * ch, GCH * (ch + 1))
            for g in range(NG):
                cols = slice(128 * g, 128 * (g + 1))
                vng = vnb[rows, cols]
                mixed = _mm(wm_ref[g], vng) + bt_ref[:, g:g + 1]
                dag = da_v[rows, cols]
                dmx = dag * gu[rows, cols]
                du_s[rows, cols] = dag * mixed
                dmxb = dmx.astype(BF16)
                dws_ref[:, cols] += _mm_nt(dmxb, vng)
                dmix_ref[:, cols] += dmx
                dvn_s[rows, cols] = _mm(wmt_ref[g], dmxb)
        dp_ref[0] = (du_s[...] * _gelu_grad(u, t_u)).astype(BF16)
        dvn = dvn_s[...]
        acc_ref[0] += _rows8(dvn * vhat)
        acc_ref[1] += _rows8(dvn)
        dvh = dvn * lg
        dgv = rs * (dvh - _mean(dvh) - vhat * _mean(dvh * vhat))
        dp_ref[1] = (dgv * _gelu_grad(v, t_v)).astype(BF16)

    row = lambda i: (0, 0)
    w3 = lambda i: (0, 0, 0)
    return pl.pallas_call(
        body, name="gmlp_bwd", grid=(T // tm,),
        in_specs=[ANY, pl.BlockSpec((tm, D), lambda i: (i, 0)), pl.BlockSpec((D, D), row),
                  pl.BlockSpec((None, tm, D), lambda i: (0, i, 0)), pl.BlockSpec((None, tm, D), lambda i: (1, i, 0)),
                  pl.BlockSpec((1, D), row), pl.BlockSpec((1, D), row),
                  pl.BlockSpec((NG, GCH, GCH), w3), pl.BlockSpec((NG, GCH, GCH), w3),
                  pl.BlockSpec((GCH, NG), row)],
        out_specs=[pl.BlockSpec((2, tm, D), lambda i: (2, i, 0)),
                   pl.BlockSpec((2, 8, D), w3), pl.BlockSpec((GCH, D), row), pl.BlockSpec((GCH, D), row)],
        out_shape=[SDS(dproj.shape, BF16), SDS((2, 8, D), F32), SDS((GCH, D), F32), SDS((GCH, D), F32)],
        scratch_shapes=[pltpu.VMEM((tm, D), F32), pltpu.VMEM((tm, D), F32)],
        input_output_aliases={0: 0},
        compiler_params=_cparams(),
    )(dproj, dya, w_a, proj, proj, ln_g, ln_b, wm, wm_t, b_t)


def _proj_bwd(dproj, w_in4, x, dx1, g_mix, job=None):
    T = x.shape[0]
    tm = min(512, T)
    order = (2, 3, 4, 5, 0, 1, 6, 7)

    def body(dp_ref, w_ref, x_ref, dx1_ref, g_ref, gx_ref, acc_ref):
        @pl.when(pl.program_id(0) == 0)
        def _():
            acc_ref[...] = jnp.zeros_like(acc_ref)

        dh = None
        for m, og in enumerate(order):
            part = _mm_nt(dp_ref[m], w_ref[og // 2, :, D * (og % 2):D * (og % 2 + 1)])
            dh = part if dh is None else dh + part
        xv = x_ref[...]
        r = lax.rsqrt(_mean(xv * xv) + EPS)
        xn = xv * r
        acc_ref[...] += _rows8(dh * xn)
        dxn = dh * g_ref[...]
        gx_ref[...] = dx1_ref[...] + r * (dxn - xn * _mean(dxn * xn))

    t = lambda i: (i, 0)
    return _call(
        body, name="proj_bwd", grid=(T // tm,),
        in_specs=[pl.BlockSpec((NIN, tm, D), lambda i: (0, i, 0)),
                  pl.BlockSpec((NCHIP, D, 2 * D), lambda i: (0, 0, 0), pipeline_mode=pl.Buffered(1)),
                  pl.BlockSpec((tm, D), t), pl.BlockSpec((tm, D), t), pl.BlockSpec((1, D), lambda i: (0, 0))],
        out_specs=[pl.BlockSpec((tm, D), t), pl.BlockSpec((8, D), lambda i: (0, 0))],
        out_shape=[SDS((T, D), F32), SDS((8, D), F32)],
        args=(dproj, w_in4, x, dx1, g_mix), job=job)


def _dw_call(name, a, b, a_spec, b_spec, o_spec, out_shape, nblk, tt, job=None, prefetch=None):
    T = a.shape[-2]

    def body(*refs):
        a_ref, b_ref, o_ref = refs[-3:]

        @pl.when(pl.program_id(1) == 0)
        def _():
            o_ref[...] = jnp.zeros_like(o_ref)
        o_ref[...] += _mm_tn(a_ref[...], b_ref[...])

    (out,), job_out = _call(
        body, name=name, grid=(nblk, T // tt), in_specs=[a_spec, b_spec], out_specs=[o_spec],
        out_shape=[out_shape], args=(a, b), job=job, prefetch=prefetch)
    return out, job_out


def _dw_in_half(name, place, hb, dproj, mine, job=None):
    tt = min(DW_TOKENS, hb.shape[0])

    def comp(k, pc):
        return _component_of(2 * k + (pc[1] if mine else 1 - pc[1]))

    return _dw_call(
        name, hb, dproj,
        pl.BlockSpec((tt, D), lambda k, t, pc: (t, 0)),
        pl.BlockSpec((None, tt, D), lambda k, t, pc: (comp(k, pc), t, 0)),
        pl.BlockSpec((None, D, D), lambda k, t, pc: (k, 0, 0)),
        SDS((NCHIP, D, D), F32), NCHIP, tt, job, place)


def _dw_gate_up(h2b, dgu4, job=None):
    tt = min(DW_TOKENS, h2b.shape[0])
    return _dw_call(
        "dw_gate_up", h2b, dgu4,
        pl.BlockSpec((tt, D), lambda k, t: (t, 0)),
        pl.BlockSpec((None, tt, FFS), lambda k, t: (k, t, 0)),
        pl.BlockSpec((None, D, FFS), lambda k, t: (k, 0, 0)),
        SDS((NCHIP, D, FFS), F32), NCHIP, tt, job)


def _dw_down(act, dx2b, job=None):
    tt = min(DW_TOKENS, act.shape[0])
    g, job_out = _dw_call(
        "dw_down", act, dx2b,
        pl.BlockSpec((tt, FFS), lambda k, t: (t, k)),
        pl.BlockSpec((tt, D), lambda k, t: (t, 0)),
        pl.BlockSpec((FFS, D), lambda k, t: (k, 0)),
        SDS((FF, D), F32), 2, tt, job)
    return g.reshape(NCHIP, FF // NCHIP, D), job_out


def _dw_square(name, a, b, job=None):
    tt = min(DW_TOKENS, a.shape[0])
    g, job_out = _dw_call(
        name, a, b,
        pl.BlockSpec((tt, D), lambda k, t: (t, 0)), pl.BlockSpec((tt, D), lambda k, t: (t, 0)),
        pl.BlockSpec((D, D), lambda k, t: (0, 0)), SDS((D, D), F32), 1, tt, job)
    return g.reshape(NCHIP, D // NCHIP, D), job_out


def _place():
    x, y, c = lax.axis_index("x"), lax.axis_index("y"), lax.axis_index("c")
    return x, y, c, 2 * x + y


def _chip_at(x, y, s):
    return x ^ (s >> 1), y ^ (s & 1)


class _Job:
    def __init__(self, ins, out_shapes, sems, start, finish, aliases=None, mid=None):
        self.ins, self.out_shapes, self.sems = list(ins), list(out_shapes), list(sems)
        self.start, self.finish, self.aliases = start, finish, dict(aliases or {})
        self.mid = mid if mid is not None else (lambda ins, outs, sems: None)


def _join_jobs(*jobs):
    def cut(refs, sizes):
        out, at = [], 0
        for n in sizes:
            out.append(refs[at:at + n])
            at += n
        return out

    ni = [len(j.ins) for j in jobs]
    no = [len(j.out_shapes) for j in jobs]
    ns = [len(j.sems) for j in jobs]

    def run(which):
        def go(ins, outs, sems):
            for j, a, b, c in zip(jobs, cut(ins, ni), cut(outs, no), cut(sems, ns)):
                getattr(j, which)(a, b, c)
        return go

    aliases = {}
    for k, j in enumerate(jobs):
        for a, b in j.aliases.items():
            aliases[sum(ni[:k]) + a] = sum(no[:k]) + b
    return _Job([a for j in jobs for a in j.ins], [o for j in jobs for o in j.out_shapes],
                [s for j in jobs for s in j.sems], run("start"), run("finish"), aliases, run("mid"))


def _call(body, *, name, grid, in_specs, out_specs, out_shape, args, scratch_shapes=(), aliases=None,
          job=None, prefetch=None):
    n_in, n_out, n_scr = len(in_specs), len(out_specs), len(scratch_shapes)
    npf = 0 if prefetch is None else 1
    job = job if job is not None else _Job([], [], [], lambda *a: None, lambda *a: None)
    ji, jo = len(job.ins), len(job.out_shapes)
    steps = math.prod(grid)

    def wrapped(*refs):
        pf, refs = refs[:npf], refs[npf:]
        ins, jin = refs[:n_in], refs[n_in:n_in + ji]
        o0 = n_in + ji
        outs, jout = refs[o0:o0 + n_out], refs[o0 + n_out:o0 + n_out + jo]
        s0 = o0 + n_out + jo
        scr, jsem = refs[s0:s0 + n_scr], refs[s0 + n_scr:]
        step = functools.reduce(lambda acc, ag: acc * ag[1] + pl.program_id(ag[0]), enumerate(grid), 0)
        if ji or jo:
            @pl.when(step == 0)
            def _():
                job.start(jin, jout, jsem)

        body(*pf, *ins, *outs, *scr)

        if ji or jo:
            @pl.when(step == steps // 2)
            def _():
                job.mid(jin, jout, jsem)

            @pl.when(step == steps - 1)
            def _():
                job.finish(jin, jout, jsem)

    io = {npf + a: b for a, b in dict(aliases or {}).items()}
    io.update({npf + n_in + a: n_out + b for a, b in job.aliases.items()})
    kw = dict(in_specs=list(in_specs) + [ANY] * ji, out_specs=list(out_specs) + [ANY] * jo,
              scratch_shapes=list(scratch_shapes) + job.sems)
    if npf:
        kw = dict(grid_spec=pltpu.PrefetchScalarGridSpec(num_scalar_prefetch=1, grid=grid, **kw))
    else:
        kw["grid"] = grid
    res = pl.pallas_call(
        wrapped, name=name, out_shape=list(out_shape) + job.out_shapes, input_output_aliases=io,
        compiler_params=_cparams(has_side_effects=bool(ji or jo)), **kw,
    )(*(() if prefetch is None else (prefetch,)), *args, *job.ins)
    return list(res[:n_out]), list(res[n_out:])


def _run_job(job, name):
    ji, jo = len(job.ins), len(job.out_shapes)

    def body(*refs):
        jin, jout, jsem = refs[:ji], refs[ji:ji + jo], refs[ji + jo:]
        job.start(jin, jout, jsem)
        job.finish(jin, jout, jsem)

    return list(pl.pallas_call(
        body, name=name, in_specs=[ANY] * ji, out_specs=[ANY] * jo, out_shape=job.out_shapes,
        scratch_shapes=job.sems, input_output_aliases=job.aliases,
        compiler_params=pltpu.CompilerParams(has_side_effects=True))(*job.ins))


def _cast_shard(name, place, w):
    rows, cols = w.shape
    tr = 352 if rows % 352 == 0 else 256

    def body(pc_ref, w_ref, o_ref):
        del pc_ref
        o_ref[...] = w_ref[...].astype(BF16)

    return pl.pallas_call(
        body, name=name,
        grid_spec=pltpu.PrefetchScalarGridSpec(
            num_scalar_prefetch=1, grid=(rows // tr,),
            in_specs=[pl.BlockSpec((tr, cols), lambda i, pc: (i, 0))],
            out_specs=pl.BlockSpec((None, tr, cols), lambda i, pc: (pc[0], i, 0))),
        out_shape=SDS((NCHIP, rows, cols), BF16),
        compiler_params=_cparams(),
    )(place, w)


def _sibling_copy(ref, send_sem, recv_sem):
    x, y, c, _ = _place()
    return pltpu.make_async_remote_copy(src_ref=ref, dst_ref=ref, send_sem=send_sem, recv_sem=recv_sem,
                                        device_id=(x, y, 1 - c), device_id_type=MESH)


def _half_rows(arr, slot, core):
    half = arr.shape[1] // 2
    return arr.at[slot, pl.ds(pl.multiple_of(core * half, 16), half)]


def _quarter_rows(arr, slot, core, q):
    quarter = arr.shape[1] // 4
    return arr.at[slot, pl.ds(pl.multiple_of((2 * core + q) * quarter, 16), quarter)]


def _chip_copy(ref, dist, send_sem, recv_sem):
    x, y, c, _ = _place()
    cx, cy = _chip_at(x, y, dist)
    return pltpu.make_async_remote_copy(src_ref=ref, dst_ref=ref, send_sem=send_sem, recv_sem=recv_sem,
                                        device_id=(cx, cy, c), device_id_type=MESH)


def _gather_sems(n):
    dma = pltpu.SemaphoreType.DMA
    return [dma((n, 2))] * 4 + [dma((n, 4))] * 2


def _gather_start(arrs, sems):
    dsend, drecv = sems[0], sems[1]
    _, _, c, j = _place()
    for w, arr in enumerate(arrs):
        for dist in (1, 2):
            _chip_copy(_half_rows(arr, j, c), dist, dsend.at[w, dist - 1], drecv.at[w, dist - 1]).start()


def _gather_land(arrs, sems, dist, first=0):
    dsend, drecv, rsend, rrecv, fsend, frecv = sems
    _, _, c, j = _place()
    if dist < 3:
        other = 3 - dist
        for w, arr in enumerate(arrs, first):
            landed = _half_rows(arr, j ^ dist, c)
            _chip_copy(landed, dist, dsend.at[w, dist - 1], drecv.at[w, dist - 1]).wait_recv()
            relay = _quarter_rows(arr, j ^ dist, c, other - 1)
            _chip_copy(relay, other, rsend.at[w, other - 1], rrecv.at[w, other - 1]).start()
            _sibling_copy(landed, fsend.at[w, dist - 1], frecv.at[w, dist - 1]).start()
        for w, arr in enumerate(arrs, first):
            theirs = _half_rows(arr, j ^ dist, 1 - c)
            _sibling_copy(theirs, fsend.at[w, dist - 1], frecv.at[w, dist - 1]).wait_recv()
    else:
        for w, arr in enumerate(arrs, first):
            for via in (1, 2):
                piece = _quarter_rows(arr, j ^ 3, c, via - 1)
                _chip_copy(piece, via, rsend.at[w, via - 1], rrecv.at[w, via - 1]).wait_recv()
                _sibling_copy(piece, fsend.at[w, 1 + via], frecv.at[w, 1 + via]).start()
        for w, arr in enumerate(arrs, first):
            for via in (1, 2):
                theirs = _quarter_rows(arr, j ^ 3, 1 - c, via - 1)
                _sibling_copy(theirs, fsend.at[w, 1 + via], frecv.at[w, 1 + via]).wait_recv()


def _gather_drain(arrs, sems):
    dsend, drecv, rsend, rrecv, fsend, frecv = sems
    _, _, c, j = _place()
    for w, arr in enumerate(arrs):
        for dist in (1, 2):
            other = 3 - dist
            _chip_copy(_half_rows(arr, j, c), dist, dsend.at[w, dist - 1], drecv.at[w, dist - 1]).wait_send()
            _chip_copy(_quarter_rows(arr, j ^ dist, c, other - 1), other,
                       rsend.at[w, other - 1], rrecv.at[w, other - 1]).wait_send()
            _sibling_copy(_half_rows(arr, j ^ dist, c), fsend.at[w, dist - 1], frecv.at[w, dist - 1]).wait_send()
            _sibling_copy(_quarter_rows(arr, j ^ 3, c, dist - 1),
                          fsend.at[w, 1 + dist], frecv.at[w, 1 + dist]).wait_send()


def _gather_neighbours(arrs, sems):
    _gather_land(arrs, sems, 1)
    _gather_land(arrs, sems, 2)


def _gather_finish(arrs, sems):
    _gather_land(arrs, sems, 3)
    _gather_drain(arrs, sems)


def _gather_job(arrs):
    n = len(arrs)
    return _Job(arrs, [SDS(a.shape, a.dtype) for a in arrs], _gather_sems(n),
                lambda ins, outs, sems: _gather_start(outs, sems),
                lambda ins, outs, sems: _gather_finish(outs, sems), {k: k for k in range(n)},
                mid=lambda ins, outs, sems: _gather_neighbours(outs, sems))


def _exchange_job(arrs, out_shapes, n, copies):
    def start(ins, outs, sems):
        for cp in copies(ins, outs, sems[0], sems[1]):
            cp.start()

    def finish(ins, outs, sems):
        for cp in copies(ins, outs, sems[0], sems[1]):
            cp.wait()

    return _Job(arrs, out_shapes, [pltpu.SemaphoreType.DMA((n,))] * 2, start, finish)


def _pair_exchange_job(grads):
    def copies(ins, outs, send_sem, recv_sem):
        x, y, c, _ = _place()
        res = []
        for w in range(len(grads)):
            half = ins[w].shape[1] // 2
            theirs = pl.ds(pl.multiple_of((1 - c) * half, 8), half)
            res.append(pltpu.make_async_remote_copy(
                src_ref=ins[w].at[:, theirs, :], dst_ref=outs[w], send_sem=send_sem.at[w],
                recv_sem=recv_sem.at[w], device_id=(x, y, 1 - c), device_id_type=MESH))
        return res

    return _exchange_job(grads, [SDS((NCHIP, g.shape[1] // 2, g.shape[2]), F32) for g in grads],
                         len(grads), copies)


def _row_tile(rows, cols):
    tr = rows
    while tr * cols * 4 > ELEMENTWISE_BLOCK_BYTES and tr % 32 == 0:
        tr //= 2
    return tr


def _pair_sum(name, place, g, sib):
    half, cols = sib.shape[1], sib.shape[2]
    tr = _row_tile(half, cols)
    nt = half // tr
    mine = nt if g.shape[1] == 2 * half else 0

    def body(pc_ref, g_ref, s_ref, own_ref, out_ref):
        del pc_ref
        v = g_ref[...] + s_ref[...]
        out_ref[...] = v.astype(BF16)

        @pl.when(pl.program_id(1) == 0)
        def _():
            own_ref[...] = v

    return pl.pallas_call(
        body, name=name,
        grid_spec=pltpu.PrefetchScalarGridSpec(
            num_scalar_prefetch=1, grid=(nt, NCHIP),
            in_specs=[pl.BlockSpec((None, tr, cols), lambda i, s, pc: (pc[0] ^ s, pc[1] * mine + i, 0)),
                      pl.BlockSpec((None, tr, cols), lambda i, s, pc: (pc[0] ^ s, i, 0))],
            out_specs=[pl.BlockSpec((tr, cols), lambda i, s, pc: (i, 0)),
                       pl.BlockSpec((None, tr, cols), lambda i, s, pc: (s, i, 0))]),
        out_shape=[SDS((half, cols), F32), SDS((NCHIP, half, cols), BF16)],
        compiler_params=_cparams(),
    )(place, g, sib)


def _chip_exchange_job(parts):
    def copies(ins, outs, send_sem, recv_sem):
        x, y, c, _ = _place()
        res = []
        for w in range(len(parts)):
            for s in range(1, NCHIP):
                cx, cy = _chip_at(x, y, s)
                k = w * (NCHIP - 1) + s - 1
                res.append(pltpu.make_async_remote_copy(
                    src_ref=ins[w].at[s], dst_ref=outs[w].at[s - 1], send_sem=send_sem.at[k],
                    recv_sem=recv_sem.at[k], device_id=(cx, cy, c), device_id_type=MESH))
        return res

    return _exchange_job(parts, [SDS((NCHIP - 1,) + p.shape[1:], BF16) for p in parts],
                         len(parts) * (NCHIP - 1), copies)


def _chip_sum(name, own, rem):
    half, cols = own.shape
    tr = _row_tile(half, cols)

    def body(own_ref, rem_ref, out_ref):
        out_ref[...] = ((own_ref[...] + rem_ref[0].astype(F32)) + rem_ref[1].astype(F32)) + rem_ref[2].astype(F32)

    return pl.pallas_call(
        body, name=name, grid=(half // tr,),
        in_specs=[pl.BlockSpec((tr, cols), lambda i: (i, 0)),
                  pl.BlockSpec((NCHIP - 1, tr, cols), lambda i: (0, i, 0))],
        out_specs=pl.BlockSpec((tr, cols), lambda i: (i, 0)),
        out_shape=SDS((half, cols), F32),
        compiler_params=_cparams(),
    )(own, rem)


def _share_halves_job(halves):
    def copies(ins, outs, send_sem, recv_sem):
        x, y, c, _ = _place()
        return [pltpu.make_async_remote_copy(
            src_ref=ins[w], dst_ref=outs[w], send_sem=send_sem.at[w], recv_sem=recv_sem.at[w],
            device_id=(x, y, 1 - c), device_id_type=MESH) for w in range(len(halves))]

    return _exchange_job(halves, [SDS(h.shape, F32) for h in halves], len(halves), copies)


def _adamw_math(w, g, m, v):
    m = B1 * m + (1.0 - B1) * g
    v = B2 * v + (1.0 - B2) * (g * g)
    m_hat = m / (1.0 - B1 ** STEP)
    v_hat = v / (1.0 - B2 ** STEP)
    delta = -LR * (m_hat / (jnp.sqrt(v_hat) + AEPS) + WD * w)
    return delta, m, v


def _adamw(name, place, w, own, sib, m, v):
    rows, cols = w.shape
    by_cols = own.shape[0] == rows
    half, pc_cols = (rows, cols // 2) if by_cols else (rows // 2, cols)
    tr = _row_tile(half, pc_cols)
    nt = half // tr

    def body(pc_ref, w_ref, own_ref, sib_ref, m_ref, v_ref, g_ref, d_ref, mo_ref, vo_ref):
        g = jnp.where(pl.program_id(0) == pc_ref[1], own_ref[...], sib_ref[...])
        d, mn, vn = _adamw_math(w_ref[...], g, m_ref[...], v_ref[...])
        g_ref[...] = g
        d_ref[...] = d
        mo_ref[...] = mn
        vo_ref[...] = vn

    full = pl.BlockSpec((tr, pc_cols), (lambda h, i, pc: (i, h)) if by_cols else (lambda h, i, pc: (h * nt + i, 0)))
    part = pl.BlockSpec((tr, pc_cols), lambda h, i, pc: (i, 0))
    return pl.pallas_call(
        body, name=name,
        grid_spec=pltpu.PrefetchScalarGridSpec(
            num_scalar_prefetch=1, grid=(2, nt),
            in_specs=[full, part, part, full, full], out_specs=[full] * 4),
        out_shape=[SDS((rows, cols), F32)] * 4,
        compiler_params=_cparams(),
    )(place, w, own, sib, m, v)


def _small_allreduce_adamw(sp, w, m, v):
    shape = sp.shape

    def body(sp_ref, w_ref, m_ref, v_ref, g_ref, d_ref, mo_ref, vo_ref,
             sib_s, pair_s, chip_s, send_sem, recv_sem):
        x, y, c, j = _place()
        cp = pltpu.make_async_remote_copy(
            src_ref=sp_ref, dst_ref=sib_s, send_sem=send_sem.at[0], recv_sem=recv_sem.at[0],
            device_id=(x, y, 1 - c), device_id_type=MESH)
        cp.start()
        cp.wait()
        pair_s[...] = sp_ref[...] + sib_s[...]
        cps = []
        for s in range(1, NCHIP):
            cx, cy = _chip_at(x, y, s)
            cp = pltpu.make_async_remote_copy(
                src_ref=pair_s, dst_ref=chip_s.at[s], send_sem=send_sem.at[s], recv_sem=recv_sem.at[s],
                device_id=(cx, cy, c), device_id_type=MESH)
            cp.start()
            cps.append(cp)
        chip_s[0] = pair_s[...]
        for cp in cps:
            cp.wait()
        tot = chip_s[j]
        for k in range(1, NCHIP):
            tot = tot + chip_s[k ^ j]
        g_ref[...] = tot
        d, mn, vn = _adamw_math(w_ref[...], tot, m_ref[...], v_ref[...])
        d_ref[...] = d
        mo_ref[...] = mn
        vo_ref[...] = vn

    vm = pl.BlockSpec(memory_space=pltpu.VMEM)
    return pl.pallas_call(
        body, name="small_allreduce_adamw",
        in_specs=[vm] * 4, out_specs=[vm] * 4, out_shape=[SDS(shape, F32)] * 4,
        scratch_shapes=[pltpu.VMEM(shape, F32), pltpu.VMEM(shape, F32), pltpu.VMEM((NCHIP,) + shape, F32),
                        pltpu.SemaphoreType.DMA((NCHIP,)), pltpu.SemaphoreType.DMA((NCHIP,))],
        compiler_params=pltpu.CompilerParams(has_side_effects=True),
    )(sp, w, m, v)


def _pack_small(first, mix, ln_g, ln_b, b_s, lbt, hn, ffn, fin, w_s):
    rows = [first.reshape(1, D), mix.reshape(1, D), ln_g.reshape(1, D), ln_b.reshape(1, D),
            b_s.reshape(1, D), lbt.reshape(2, D), hn.reshape(1, D), ffn.reshape(1, D), fin.reshape(1, D),
            jnp.zeros((6, D), F32)]
    return jnp.concatenate(rows + [w_s.reshape(NG, GCH, GCH).transpose(1, 0, 2).reshape(GCH, D)], axis=0)


def _unpack_small(p):
    w_s = p[16:].reshape(GCH, NG, GCH).transpose(1, 0, 2).reshape(1, NG, GCH, GCH)
    return dict(norm_mix_g=p[1:2], gmlp_ln_g=p[2:3], gmlp_ln_b=p[3:4], gmlp_b_s=p[4].reshape(1, NG, GCH),
                hgrn_lb_table=p[5:7], hgrn_norm_g=p[7:8], norm_ffn_g=p[8:9], norm_final_g=p[9],
                gmlp_w_s=w_s)


SMALL = ("norm_mix_g", "gmlp_ln_g", "gmlp_ln_b", "gmlp_w_s", "gmlp_b_s", "hgrn_lb_table", "hgrn_norm_g",
         "norm_ffn_g", "norm_final_g")
BIG = ("w_in", "w_gate_up", "w_branch_a", "w_branch_b", "w_out", "w_down")
ORDER = ("norm_mix_g", "w_in", "gmlp_ln_g", "gmlp_ln_b", "gmlp_w_s", "gmlp_b_s", "hgrn_lb_table",
         "hgrn_norm_g", "w_branch_a", "w_branch_b", "w_out", "norm_ffn_g", "w_gate_up", "w_down",
         "norm_final_g")


def kernel(x, norm_mix_g, w_in, gmlp_ln_g, gmlp_ln_b, gmlp_w_s, gmlp_b_s, hgrn_lb_table, hgrn_norm_g, w_branch_a, w_branch_b, w_out, norm_ffn_g, w_gate_up, w_down, norm_final_g, loss_target, m_norm_mix_g, m_w_in, m_gmlp_ln_g, m_gmlp_ln_b, m_gmlp_w_s, m_gmlp_b_s, m_hgrn_lb_table, m_hgrn_norm_g, m_w_branch_a, m_w_branch_b, m_w_out, m_norm_ffn_g, m_w_gate_up, m_w_down, m_norm_final_g, v_norm_mix_g, v_w_in, v_gmlp_ln_g, v_gmlp_ln_b, v_gmlp_w_s, v_gmlp_b_s, v_hgrn_lb_table, v_hgrn_norm_g, v_w_branch_a, v_w_branch_b, v_w_out, v_norm_ffn_g, v_w_gate_up, v_w_down, v_norm_final_g):
    args = dict(locals())
    T = x.shape[1]
    xs = x.reshape(T, D)
    target = loss_target.reshape(T, D)
    big = {n: args[n].reshape(args[n].shape[1:]) for n in BIG}
    big_m = {n: args["m_" + n].reshape(args[n].shape[1:]) for n in BIG}
    big_v = {n: args["v_" + n].reshape(args[n].shape[1:]) for n in BIG}

    x_i, y_i, c_i = lax.axis_index("x"), lax.axis_index("y"), lax.axis_index("c")
    place = jnp.stack([2 * x_i + y_i, c_i]).astype(jnp.int32)
    cast = {n: _cast_shard("cast_" + n, place, big[n]) for n in BIG}
    tril = jnp.tril(jnp.ones((GCH, GCH), bool))
    wm = jnp.where(tril, gmlp_w_s[0], 0.0).astype(BF16)
    wm_t = jnp.swapaxes(wm, 1, 2)
    b_t = gmlp_b_s[0].T

    (proj, hb), w_in4, (w_a4, w_b4, w_out4, w_down4) = _proj_fwd(
        place, xs, norm_mix_g, cast["w_in"], [cast[n] for n in ("w_branch_a", "w_branch_b", "w_out", "w_down")])
    (ab,), _ = _gmlp_fwd(proj, gmlp_ln_g, gmlp_ln_b, wm, b_t)
    (o_raw, obb, st_before), (w_gu4,) = _hgrn_fwd(
        proj, hgrn_lb_table, hgrn_norm_g, job=_gather_job([cast["w_gate_up"]]))
    w_a, w_b, w_o = (w.reshape(D, D) for w in (w_a4, w_b4, w_out4))
    (mgb, x1), _ = _merge_fwd(xs, ab, obb, proj, w_a, w_b, w_o)
    w_dn = w_down4.reshape(FF, D)
    act, dx2b, h2b, dgu4, dx1, dx1b, acc_ffn = _ffn_fwd_bwd(
        x1, target, norm_ffn_g, norm_final_g.reshape(1, D), w_gu4, w_dn)

    grads, owns, parts, halves, sibh = {}, {}, {}, {}, {}

    def pair_sums(names, sibs):
        for n, s in zip(names, sibs):
            owns[n], parts[n] = _pair_sum("rs_pair_sum_" + n, place, grads[n], s)

    def chip_sums(names, got):
        for n, r in zip(names, got):
            halves[n] = _chip_sum("rs_chip_sum_" + n, owns[n], r)

    ffn, mix = ("w_gate_up", "w_down"), ("w_branch_a", "w_branch_b", "w_out")
    grads["w_gate_up"], _ = _dw_gate_up(h2b, dgu4)
    grads["w_down"], _ = _dw_down(act, dx2b)
    (dya, dyb, dproj), got = _merge_bwd(
        dx1b, ab, obb, proj, w_o, w_a, w_b, job=_pair_exchange_job([grads[n] for n in ffn]))
    pair_sums(ffn, got)
    grads["w_branch_a"], _ = _dw_square("dw_branch_a", ab, dya)
    grads["w_branch_b"], _ = _dw_square("dw_branch_b", obb, dyb)
    grads["w_out"], _ = _dw_square("dw_out", mgb, dx1b)
    (dproj, acc_hgrn), got = _hgrn_bwd(
        dproj, dyb, w_b, o_raw, proj, st_before, hgrn_lb_table, hgrn_norm_g,
        job=_join_jobs(_chip_exchange_job([parts[n] for n in ffn]), _pair_exchange_job([grads[n] for n in mix])))
    chip_sums(ffn, got[:2])
    pair_sums(mix, got[2:])
    dproj, acc_ln, dws, dmix = _gmlp_bwd(dproj, dya, w_a, proj, gmlp_ln_g, gmlp_ln_b, wm, wm_t, b_t)
    for_sibling, got = _dw_in_half(
        "dw_in_sibling_half", place, hb, dproj, False,
        job=_join_jobs(_share_halves_job([halves[n] for n in ffn]), _chip_exchange_job([parts[n] for n in mix])))
    sibh.update(zip(ffn, got[:2]))
    chip_sums(mix, got[2:])
    grads["w_in"], got = _dw_in_half(
        "dw_in_own_half", place, hb, dproj, True, job=_share_halves_job([for_sibling]))
    pair_sums(("w_in",), got)
    (grad_x, acc_mix), got = _proj_bwd(
        dproj, w_in4, xs, dx1, norm_mix_g,
        job=_join_jobs(_chip_exchange_job([parts["w_in"]]), _share_halves_job([halves[n] for n in mix])))
    chip_sums(("w_in",), got[:1])
    sibh.update(zip(mix, got[1:]))
    (sibh["w_in"],) = _run_job(_share_halves_job([halves["w_in"]]), "rs_share_halves_w_in")
    out = {}
    for n in BIG:
        g, d, mn, vn = _adamw("adamw_" + n, place, big[n], halves[n], sibh[n], big_m[n], big_v[n])
        shp = args[n].shape
        out[n] = (g.reshape(shp), d.reshape(shp), mn.reshape(shp), vn.reshape(shp))

    lbv = jax.nn.sigmoid(hgrn_lb_table[0] - hgrn_lb_table[1])
    d_t0 = jnp.sum(acc_hgrn[0], axis=0) * lbv * (1.0 - lbv)
    loss_row = jnp.zeros((D,), F32).at[0].set(jnp.sum(acc_ffn[0]))
    dws_m = jnp.where(tril[:, None, :], dws.reshape(GCH, NG, GCH), 0.0).transpose(1, 0, 2)
    db_s = jnp.sum(dmix.reshape(GCH, NG, GCH), axis=-1).T
    sp = _pack_small(loss_row, jnp.sum(acc_mix, 0), jnp.sum(acc_ln[0], 0), jnp.sum(acc_ln[1], 0), db_s,
                     jnp.stack([d_t0, -d_t0]), jnp.sum(acc_hgrn[1], 0), jnp.sum(acc_ffn[2], 0),
                     jnp.sum(acc_ffn[1], 0), dws_m)
    zero = jnp.zeros((D,), F32)

    def pack(prefix):
        a = lambda n: args[prefix + n]
        return _pack_small(zero, a("norm_mix_g"), a("gmlp_ln_g"), a("gmlp_ln_b"), a("gmlp_b_s"),
                           a("hgrn_lb_table"), a("hgrn_norm_g"), a("norm_ffn_g"), a("norm_final_g"),
                           a("gmlp_w_s"))

    packed = _small_allreduce_adamw(sp, pack(""), pack("m_"), pack("v_"))
    loss = packed[0][0, 0]
    small = [_unpack_small(p) for p in packed]
    for n in SMALL:
        out[n] = tuple(s[n] for s in small)
    return (loss, grad_x.reshape(x.shape), *[out[n][0] for n in ORDER], *[out[n][1] for n in ORDER],
            *[out[n][2] for n in ORDER], *[out[n][3] for n in ORDER])
```

```python
import functools
import math

import jax
import jax.numpy as jnp
from jax import lax
from jax.experimental import pallas as pl
from jax.experimental.pallas import tpu as pltpu

F32 = jnp.float32
BF16 = jnp.bfloat16
SDS = jax.ShapeDtypeStruct
MESH = pl.DeviceIdType.MESH
ANY = pl.BlockSpec(memory_space=pl.ANY)

D = 1024
NIN = 8
NG = 8
GCH = 128
NH = 8
HD = 128
HCH = 64
HGRN_HB = 4
HW = HGRN_HB * HD
DW_TOKENS = 2048
ELEMENTWISE_BLOCK_BYTES = 2 * 1024 * 1024
PROJ_OUT_SLOTS = 4
FF = 2816
FFS = 1408
NCHIP = 4
EPS = 1e-6
QSCALE = HD ** -0.5
GELU_C0 = math.sqrt(2.0 / math.pi)
GELU_C1 = 0.044715
LR, B1, B2, AEPS, WD, STEP = 0.001, 0.9, 0.999, 1e-08, 0.01, 10
VMEM_LIMIT_V7X = 56 * 1024 * 1024
SP_ROWS = 144


def _cparams(**kw):
    return pltpu.CompilerParams(vmem_limit_bytes=VMEM_LIMIT_V7X, **kw)


def _mm(a, b):
    return jnp.dot(a, b, preferred_element_type=F32)


def _mm_nt(a, b):
    return lax.dot_general(a, b, (((1,), (1,)), ((), ())), preferred_element_type=F32)


def _mm_tn(a, b):
    return lax.dot_general(a, b, (((0,), (0,)), ((), ())), preferred_element_type=F32)


def _rows8(x):
    r, c = x.shape
    return jnp.sum(x.reshape(r // 8, 8, c), axis=0)


def _mean(x):
    return jnp.mean(x, axis=-1, keepdims=True)


def _sigmoid(x):
    return 1.0 / (1.0 + jnp.exp(-x))


def _gelu(x):
    t = jnp.tanh(GELU_C0 * (x + GELU_C1 * x * x * x))
    return 0.5 * x * (1.0 + t), t


def _gelu_grad(x, t):
    return 0.5 * (1.0 + t) + 0.5 * x * (1.0 - t * t) * (GELU_C0 * (1.0 + 3.0 * GELU_C1 * x * x))


def _component_of(group):
    return jnp.where(group < 6, (group + 4) % 6, group)


def _proj_fwd(place, x, g_mix, w_in4, later):
    T = x.shape[0]
    tm = min(1024, T)
    ni = T // tm
    n = len(later)

    def body(pc_ref, x_ref, g_ref, *rest):
        proj_ref, h_ref, w_all = rest[1 + n:4 + n]
        gathered = rest[4 + n:4 + 2 * n]
        hs, wbuf, wsem, obuf, osem = rest[4 + 2 * n:9 + 2 * n]
        w_sems, later_sems = rest[9 + 2 * n:15 + 2 * n], rest[15 + 2 * n:]
        jp, i = pl.program_id(0), pl.program_id(1)
        w_cols = [w_all.at[:, :, pl.ds(k * D, D)] for k in range(2)]

        def w_copy(blk):
            cols = pl.ds(pl.multiple_of((blk % 2) * D, 128), D)
            return pltpu.make_async_copy(w_all.at[pc_ref[0] ^ (blk // 2), :, cols], wbuf.at[blk % 2],
                                         wsem.at[blk % 2])

        @pl.when((jp == 0) & (i == 0))
        def _():
            _gather_start(w_cols, w_sems)
            _gather_start(gathered, later_sems)
            w_copy(jp).start()

        @pl.when(i == 0)
        def _():
            w_copy(jp).wait()

        @pl.when(jp == 0)
        def _():
            xv = x_ref[...]
            r = lax.rsqrt(_mean(xv * xv) + EPS)
            hb = (xv * r * g_ref[...]).astype(BF16)
            hs[i] = hb
            h_ref[...] = hb

        step = jp * ni + i
        slot = step % PROJ_OUT_SLOTS

        def o_copy(slot_):
            comp = 2 * (pc_ref[0] ^ (jp // 2)) + jp % 2
            return pltpu.make_async_copy(
                obuf.at[slot_], proj_ref.at[comp, pl.ds(pl.multiple_of(i * tm, 8), tm)], osem.at[slot_])

        @pl.when(step >= PROJ_OUT_SLOTS)
        def _():
            o_copy(slot).wait()

        obuf[slot] = _mm(hs[i], wbuf[jp % 2])
        o_copy(slot).start()

        @pl.when(step == NIN * ni - 1)
        def _():
            for k in range(PROJ_OUT_SLOTS):
                o_copy((slot + 1 + k) % PROJ_OUT_SLOTS).wait()

        for nxt in range(1, NIN):
            @pl.when((jp == nxt - 1) & (i == ni - 1))
            def _():
                if nxt >= 2:
                    _gather_land([w_cols[nxt % 2]], w_sems, nxt // 2, first=nxt % 2)
                if nxt == 4:
                    _gather_neighbours(gathered, later_sems)
                w_copy(jp + 1).start()

        @pl.when((jp == NIN - 1) & (i == ni - 1))
        def _():
            _gather_drain(w_cols, w_sems)
            _gather_finish(gathered, later_sems)

    tile = lambda jp, i, pc: (jnp.where(jp == 0, i, ni - 1), 0)
    res = pl.pallas_call(
        body, name="proj_fwd",
        grid_spec=pltpu.PrefetchScalarGridSpec(
            num_scalar_prefetch=1, grid=(NIN, ni),
            in_specs=[pl.BlockSpec((tm, D), tile), pl.BlockSpec((1, D), lambda jp, i, pc: (0, 0))] + [ANY] * (1 + n),
            out_specs=[ANY, pl.BlockSpec((tm, D), tile)] + [ANY] * (1 + n),
            scratch_shapes=[pltpu.VMEM((ni, tm, D), BF16), pltpu.VMEM((2, D, D), BF16),
                            pltpu.SemaphoreType.DMA((2,)), pltpu.VMEM((PROJ_OUT_SLOTS, tm, D), F32),
                            pltpu.SemaphoreType.DMA((PROJ_OUT_SLOTS,))] + _gather_sems(2) + _gather_sems(n)),
        out_shape=[SDS((NIN, T, D), F32), SDS((T, D), BF16), SDS(w_in4.shape, BF16)]
        + [SDS(a.shape, a.dtype) for a in later],
        input_output_aliases={3 + k: 2 + k for k in range(1 + n)},
        compiler_params=_cparams(has_side_effects=True),
    )(place, x, g_mix, w_in4, *later)
    return res[:2], res[2], res[3:]


def _layer_norm_stats(gv):
    mu = _mean(gv)
    xc = gv - mu
    rs = lax.rsqrt(_mean(xc * xc) + EPS)
    return xc * rs, rs


def _gmlp_fwd(proj, ln_g, ln_b, wm, b_t, job=None):
    T = proj.shape[1]
    tm = min(256, T)

    def body(u_ref, v_ref, lg_ref, lb_ref, wm_ref, bt_ref, a_ref, a_s):
        gu, _ = _gelu(u_ref[...])
        gv, _ = _gelu(v_ref[...])
        vhat, _ = _layer_norm_stats(gv)
        vnb = (vhat * lg_ref[...] + lb_ref[...]).astype(BF16)
        for ch in range(tm // GCH):
            rows = slice(GCH * ch, GCH * (ch + 1))
            for g in range(NG):
                cols = slice(128 * g, 128 * (g + 1))
                mixed = _mm(wm_ref[g], vnb[rows, cols]) + bt_ref[:, g:g + 1]
                a_s[rows, cols] = gu[rows, cols] * mixed
        a_ref[...] = a_s[...].astype(BF16)

    row = lambda i: (0, 0)
    return _call(
        body, name="gmlp_fwd", grid=(T // tm,), job=job, args=(proj, proj, ln_g, ln_b, wm, b_t),
        in_specs=[pl.BlockSpec((None, tm, D), lambda i: (0, i, 0)), pl.BlockSpec((None, tm, D), lambda i: (1, i, 0)),
                  pl.BlockSpec((1, D), row), pl.BlockSpec((1, D), row),
                  pl.BlockSpec((NG, GCH, GCH), lambda i: (0, 0, 0)), pl.BlockSpec((GCH, NG), row)],
        out_specs=[pl.BlockSpec((tm, D), lambda i: (i, 0))],
        out_shape=[SDS((T, D), BF16)],
        scratch_shapes=[pltpu.VMEM((tm, D), F32)])


def _cumsum64(x, row):
    for s in (1, 2, 4, 8, 16, 32):
        x = x + jnp.where(row >= s, pltpu.roll(x, s, 0), 0.0)
    return x


def _revcumsum64(x, row):
    n = x.shape[0]
    for s in (1, 2, 4, 8, 16, 32):
        x = x + jnp.where(row < HCH - s, pltpu.roll(x, n - s, 0), 0.0)
    return x


def _head_mean(x):
    parts = [jnp.broadcast_to(_mean(x[:, HD * h:HD * (h + 1)]), (x.shape[0], HD)) for h in range(x.shape[1] // HD)]
    return jnp.concatenate(parts, axis=1)


def _seg_sum(x):
    n, c = x.shape
    s = jnp.sum(x.reshape(n // HCH, HCH, c), axis=1, keepdims=True)
    return jnp.broadcast_to(s, (n // HCH, HCH, c)).reshape(n, c)


def _hgrn_gates(fl, lbv, row):
    s = _sigmoid(fl)
    f = lbv + (1.0 - lbv) * s
    a = _cumsum64(jnp.log(f), row)
    a_mid = _seg_sum(jnp.where(row == HCH // 2 - 1, a, 0.0))
    a_last = _seg_sum(jnp.where(row == HCH - 1, a, 0.0))
    return s, f, a, a_mid, a_last


def _hgrn_fwd(proj, lb_table, norm_g, job=None):
    T = proj.shape[1]
    tb = min(512, T)
    nc = tb // HCH

    def body(q_ref, fl_ref, v_ref, g_ref, lbt_ref, gn_ref, o_ref, ob_ref, stb_ref, st_s, o_s):
        @pl.when(pl.program_id(1) == 0)
        def _():
            st_s[...] = jnp.zeros_like(st_s)

        row = lax.broadcasted_iota(jnp.int32, (tb, HW), 0) & (HCH - 1)
        lbv = _sigmoid(lbt_ref[0:1, :] - lbt_ref[1:2, :])
        _, f, a, a_mid, a_last = _hgrn_gates(fl_ref[...], lbv, row)
        k = 1.0 - f
        qs = q_ref[...] * QSCALE
        q_in = (qs * jnp.exp(a - a_mid)).astype(BF16)
        k_in = (k * jnp.exp(a_mid - a)).astype(BF16)
        q_a = (qs * jnp.exp(a)).astype(BF16)
        k_d = (k * jnp.exp(a_last - a)).astype(BF16)
        dec = jnp.exp(a_last)
        vb = v_ref[...].astype(BF16)
        tri = (lax.broadcasted_iota(jnp.int32, (HCH, HCH), 0)
               >= lax.broadcasted_iota(jnp.int32, (HCH, HCH), 1))
        for c in range(nc):
            sl = slice(HCH * c, HCH * (c + 1))
            for hh in range(HGRN_HB):
                hs = slice(HD * hh, HD * (hh + 1))
                st = st_s[hh]
                stb_ref[hh, c] = st
                sc = jnp.where(tri, _mm_nt(q_in[sl, hs], k_in[sl, hs]), 0.0)
                o_s[sl, hs] = _mm(sc.astype(BF16), vb[sl, hs]) + _mm_nt(q_a[sl, hs], st.astype(BF16))
                d64 = dec[sl, hs]
                st_s[hh] = st * jnp.concatenate([d64, d64], axis=0) + _mm_tn(vb[sl, hs], k_d[sl, hs])
        o = o_s[...]
        r = lax.rsqrt(_head_mean(o * o) + EPS)
        g = g_ref[...]
        o_ref[...] = o
        ob_ref[...] = (o * r * gn_ref[...] * (g * _sigmoid(g))).astype(BF16)

    def col(off):
        return pl.BlockSpec((None, tb, HW), lambda h, cb: (off, cb, h))

    return _call(
        body, name="hgrn_fwd", grid=(NH // HGRN_HB, T // tb), job=job,
        args=(proj, proj, proj, proj, lb_table, norm_g),
        in_specs=[col(2), col(3), col(4), col(5),
                  pl.BlockSpec((2, HW), lambda h, cb: (0, h)), pl.BlockSpec((1, HW), lambda h, cb: (0, h))],
        out_specs=[pl.BlockSpec((tb, HW), lambda h, cb: (cb, h)), pl.BlockSpec((tb, HW), lambda h, cb: (cb, h)),
                   pl.BlockSpec((HGRN_HB, nc, HD, HD), lambda h, cb: (h, cb, 0, 0))],
        out_shape=[SDS((T, D), F32), SDS((T, D), BF16), SDS((NH, T // HCH, HD, HD), F32)],
        scratch_shapes=[pltpu.VMEM((HGRN_HB, HD, HD), F32), pltpu.VMEM((tb, HW), F32)])


def _merge_fwd(x, ab, ob, proj, w_a, w_b, w_out, job=None):
    T = x.shape[0]
    tm = min(512, T)

    def body(x_ref, ab_ref, ob_ref, ga_ref, gb_ref, wa_ref, wb_ref, wo_ref, mg_ref, x1_ref):
        ya = _mm(ab_ref[...], wa_ref[...])
        yb = _mm(ob_ref[...], wb_ref[...])
        merged = (_sigmoid(ga_ref[...]) * ya + _sigmoid(gb_ref[...]) * yb).astype(BF16)
        mg_ref[...] = merged
        x1_ref[...] = x_ref[...] + _mm(merged, wo_ref[...])

    t = lambda i: (i, 0)
    w = lambda i: (0, 0)
    return _call(
        body, name="merge_fwd", grid=(T // tm,), job=job, args=(x, ab, ob, proj, proj, w_a, w_b, w_out),
        in_specs=[pl.BlockSpec((tm, D), t), pl.BlockSpec((tm, D), t), pl.BlockSpec((tm, D), t),
                  pl.BlockSpec((None, tm, D), lambda i: (6, i, 0)), pl.BlockSpec((None, tm, D), lambda i: (7, i, 0)),
                  pl.BlockSpec((D, D), w), pl.BlockSpec((D, D), w), pl.BlockSpec((D, D), w)],
        out_specs=[pl.BlockSpec((tm, D), t)] * 2,
        out_shape=[SDS((T, D), BF16), SDS((T, D), F32)])


def _ffn_fwd_bwd(x1, target, g_ffn, g_fin, w_gu4, w_down):
    T = x1.shape[0]
    tm = min(256, T)
    inv_d = 1.0 / D

    def body(x1_ref, tg_ref, gf_ref, gn_ref, wgu_ref, wd_ref,
             act_ref, dx2b_ref, h2b_ref, dgu_ref, dx1_ref, dx1b_ref, acc_ref):
        @pl.when(pl.program_id(0) == 0)
        def _():
            acc_ref[...] = jnp.zeros_like(acc_ref)

        x1v = x1_ref[...]
        gf = gf_ref[...]
        gn = gn_ref[...]
        rr1 = lax.rsqrt(_mean(x1v * x1v) + EPS)
        x1n = x1v * rr1
        h2b = (x1n * gf).astype(BF16)
        h2b_ref[...] = h2b
        p = [_mm(h2b, wgu_ref[k]) for k in range(NCHIP)]
        sg = [_sigmoid(p[0]), _sigmoid(p[1])]
        si = [p[0] * sg[0], p[1] * sg[1]]
        x2 = x1v
        for k in range(2):
            actk = (si[k] * p[2 + k]).astype(BF16)
            act_ref[:, FFS * k:FFS * (k + 1)] = actk
            x2 = x2 + _mm(actk, wd_ref[FFS * k:FFS * (k + 1), :])
        rr2 = lax.rsqrt(_mean(x2 * x2) + EPS)
        x2n = x2 * rr2
        e = x2n * gn - tg_ref[...]
        acc_ref[0] += _rows8(e * e) * (0.5 * inv_d)
        dy = e * inv_d
        acc_ref[1] += _rows8(dy * x2n)
        dxn = dy * gn
        dx2 = rr2 * (dxn - x2n * _mean(dxn * x2n))
        dx2b = dx2.astype(BF16)
        dx2b_ref[...] = dx2b
        dh2 = None
        for k in range(2):
            dact = _mm_nt(dx2b, wd_ref[FFS * k:FFS * (k + 1), :])
            dgate = (dact * p[2 + k] * (sg[k] * (1.0 + p[k] * (1.0 - sg[k])))).astype(BF16)
            dup = (dact * si[k]).astype(BF16)
            dgu_ref[k] = dgate
            dgu_ref[2 + k] = dup
            part = _mm_nt(dgate, wgu_ref[k]) + _mm_nt(dup, wgu_ref[2 + k])
            dh2 = part if dh2 is None else dh2 + part
        acc_ref[2] += _rows8(dh2 * x1n)
        dxn1 = dh2 * gf
        dx1 = dx2 + rr1 * (dxn1 - x1n * _mean(dxn1 * x1n))
        dx1_ref[...] = dx1
        dx1b_ref[...] = dx1.astype(BF16)

    t = lambda i: (i, 0)
    w = lambda i: (0, 0)
    one = pl.Buffered(1)
    return pl.pallas_call(
        body, name="ffn_fwd_bwd", grid=(T // tm,),
        in_specs=[pl.BlockSpec((tm, D), t), pl.BlockSpec((tm, D), t),
                  pl.BlockSpec((1, D), w), pl.BlockSpec((1, D), w),
                  pl.BlockSpec((NCHIP, D, FFS), lambda i: (0, 0, 0), pipeline_mode=one),
                  pl.BlockSpec((FF, D), w, pipeline_mode=one)],
        out_specs=[pl.BlockSpec((tm, FF), t), pl.BlockSpec((tm, D), t), pl.BlockSpec((tm, D), t),
                   pl.BlockSpec((NCHIP, tm, FFS), lambda i: (0, i, 0)),
                   pl.BlockSpec((tm, D), t), pl.BlockSpec((tm, D), t),
                   pl.BlockSpec((3, 8, D), lambda i: (0, 0, 0))],
        out_shape=[SDS((T, FF), BF16), SDS((T, D), BF16), SDS((T, D), BF16),
                   SDS((NCHIP, T, FFS), BF16), SDS((T, D), F32), SDS((T, D), BF16),
                   SDS((3, 8, D), F32)],
        compiler_params=_cparams(),
    )(x1, target, g_ffn, g_fin, w_gu4, w_down)


def _merge_bwd(dx1b, ab, ob, proj, w_out, w_a, w_b, job=None):
    T = dx1b.shape[0]
    tm = min(512, T)

    def body(dx_ref, ab_ref, ob_ref, ga_ref, gb_ref, wo_ref, wa_ref, wb_ref, dya_ref, dyb_ref, dp_ref):
        dm = _mm_nt(dx_ref[...], wo_ref[...])
        sa = _sigmoid(ga_ref[...])
        sb = _sigmoid(gb_ref[...])
        dya_ref[...] = (dm * sa).astype(BF16)
        dyb_ref[...] = (dm * sb).astype(BF16)
        dp_ref[0] = (dm * _mm(ab_ref[...], wa_ref[...]) * sa * (1.0 - sa)).astype(BF16)
        dp_ref[1] = (dm * _mm(ob_ref[...], wb_ref[...]) * sb * (1.0 - sb)).astype(BF16)

    t = lambda i: (i, 0)
    w = lambda i: (0, 0)
    return _call(
        body, name="merge_bwd", grid=(T // tm,),
        in_specs=[pl.BlockSpec((tm, D), t), pl.BlockSpec((tm, D), t), pl.BlockSpec((tm, D), t),
                  pl.BlockSpec((None, tm, D), lambda i: (6, i, 0)), pl.BlockSpec((None, tm, D), lambda i: (7, i, 0)),
                  pl.BlockSpec((D, D), w), pl.BlockSpec((D, D), w), pl.BlockSpec((D, D), w)],
        out_specs=[pl.BlockSpec((tm, D), t)] * 2 + [pl.BlockSpec((2, tm, D), lambda i: (3, i, 0))],
        out_shape=[SDS((T, D), BF16), SDS((T, D), BF16), SDS((NIN, T, D), BF16)],
        args=(dx1b, ab, ob, proj, proj, w_out, w_a, w_b), job=job)


def _hgrn_bwd(dproj, dyb, w_b, o_raw, proj, st_before, lb_table, norm_g, job=None):
    T = dyb.shape[0]
    tb = min(512, T)
    nc = tb // HCH
    nb = T // tb

    def body(dp_in, dyb_ref, wb_ref, o_ref, q_ref, fl_ref, v_ref, g_ref, stb_ref, lbt_ref, gn_ref,
             dp_ref, acc_ref, dst_s, dqin_s, dqa_s, dkin_s, dkd_s, dv_s, ddec_s):
        del dp_in

        @pl.when(pl.program_id(1) == 0)
        def _():
            dst_s[...] = jnp.zeros_like(dst_s)
            acc_ref[...] = jnp.zeros_like(acc_ref)

        row = lax.broadcasted_iota(jnp.int32, (tb, HW), 0) & (HCH - 1)
        gn = gn_ref[...]
        lbv = _sigmoid(lbt_ref[0:1, :] - lbt_ref[1:2, :])
        o = o_ref[...]
        r = lax.rsqrt(_head_mean(o * o) + EPS)
        on = o * r
        g = g_ref[...]
        sgm = _sigmoid(g)
        dob_v = _mm_nt(dyb_ref[...], wb_ref[...])
        dp_ref[3] = (dob_v * on * gn * (sgm * (1.0 + g * (1.0 - sgm)))).astype(BF16)
        do_n = dob_v * (g * sgm)
        acc_ref[1] += _rows8(do_n * on)
        dxn = do_n * gn
        do = (r * (dxn - on * _head_mean(dxn * on))).astype(BF16)
        s, f, a, a_mid, a_last = _hgrn_gates(fl_ref[...], lbv, row)
        k = 1.0 - f
        qs = q_ref[...] * QSCALE
        e_q = jnp.exp(a - a_mid)
        e_k = jnp.exp(a_mid - a)
        e_a = jnp.exp(a)
        e_l = jnp.exp(a_last - a)
        dec = jnp.exp(a_last)
        q_in = qs * e_q
        k_in = k * e_k
        q_a = qs * e_a
        k_d = k * e_l
        q_inb, k_inb, q_ab, k_db = (z.astype(BF16) for z in (q_in, k_in, q_a, k_d))
        vb = v_ref[...].astype(BF16)
        tri = (lax.broadcasted_iota(jnp.int32, (HCH, HCH), 0)
               >= lax.broadcasted_iota(jnp.int32, (HCH, HCH), 1))
        for c in reversed(range(nc)):
            sl = slice(HCH * c, HCH * (c + 1))
            for hh in range(HGRN_HB):
                hs = slice(HD * hh, HD * (hh + 1))
                stp = stb_ref[hh, c]
                dst = dst_s[hh]
                dstb = dst.astype(BF16)
                do_c = do[sl, hs]
                v_c = vb[sl, hs]
                dqa_s[sl, hs] = _mm(do_c, stp.astype(BF16))
                dkd_s[sl, hs] = _mm(v_c, dstb)
                ddec_s[sl, hs] = jnp.broadcast_to(jnp.sum(dst * stp, axis=0, keepdims=True), (HCH, HD))
                sc = jnp.where(tri, _mm_nt(q_inb[sl, hs], k_inb[sl, hs]), 0.0).astype(BF16)
                dsc = jnp.where(tri, _mm_nt(do_c, v_c), 0.0).astype(BF16)
                dv_s[sl, hs] = _mm_nt(k_db[sl, hs], dstb) + _mm_tn(sc, do_c)
                dqin_s[sl, hs] = _mm(dsc, k_inb[sl, hs])
                dkin_s[sl, hs] = _mm_tn(dsc, q_inb[sl, hs])
                d64 = dec[sl, hs]
                dst_s[hh] = dst * jnp.concatenate([d64, d64], axis=0) + _mm_tn(do_c, q_ab[sl, hs])
        dq_in = dqin_s[...]
        dq_a = dqa_s[...]
        dk_in = dkin_s[...]
        dk_d = dkd_s[...]
        dp_ref[0] = ((dq_in * e_q + dq_a * e_a) * QSCALE).astype(BF16)
        dp_ref[2] = dv_s[...].astype(BF16)
        tq = dq_in * q_in
        tk = dk_in * k_in
        td = dk_d * k_d
        d_a = tq + dq_a * q_a - tk - td
        d_a = d_a + jnp.where(row == HCH // 2 - 1, _seg_sum(tk - tq), 0.0)
        d_a = d_a + jnp.where(row == HCH - 1, _seg_sum(td) + ddec_s[...] * dec, 0.0)
        dlf = _revcumsum64(d_a, row)
        df = dlf / f - (dk_in * e_k + dk_d * e_l)
        dp_ref[1] = (df * (1.0 - lbv) * s * (1.0 - s)).astype(BF16)
        acc_ref[0] += _rows8(df * (1.0 - s))

    def col(off):
        return pl.BlockSpec((None, tb, HW), lambda h, cb: (off, nb - 1 - cb, h))

    hb = lambda h, cb: (nb - 1 - cb, h)
    return _call(
        body, name="hgrn_bwd", grid=(NH // HGRN_HB, nb), job=job,
        args=(dproj, dyb, w_b, o_raw, proj, proj, proj, proj, st_before, lb_table, norm_g),
        in_specs=[ANY, pl.BlockSpec((tb, D), lambda h, cb: (nb - 1 - cb, 0)),
                  pl.BlockSpec((HW, D), lambda h, cb: (h, 0)), pl.BlockSpec((tb, HW), hb),
                  col(2), col(3), col(4), col(5),
                  pl.BlockSpec((HGRN_HB, nc, HD, HD), lambda h, cb: (h, nb - 1 - cb, 0, 0)),
                  pl.BlockSpec((2, HW), lambda h, cb: (0, h)), pl.BlockSpec((1, HW), lambda h, cb: (0, h))],
        out_specs=[pl.BlockSpec((4, tb, HW), lambda h, cb: (0, nb - 1 - cb, h)),
                   pl.BlockSpec((2, 8, HW), lambda h, cb: (0, 0, h))],
        out_shape=[SDS(dproj.shape, BF16), SDS((2, 8, D), F32)],
        scratch_shapes=[pltpu.VMEM((HGRN_HB, HD, HD), F32)] + [pltpu.VMEM((tb, HW), F32)] * 6,
        aliases={0: 0})


def _gmlp_bwd(dproj, dya, w_a, proj, ln_g, ln_b, wm, wm_t, b_t):
    T = dya.shape[0]
    tm = min(256, T)

    def body(dp_in, dya_ref, wa_ref, u_ref, v_ref, lg_ref, lb_ref, wm_ref, wmt_ref, bt_ref,
             dp_ref, acc_ref, dws_ref, dmix_ref, du_s, dvn_s):
        del dp_in

        @pl.when(pl.program_id(0) == 0)
        def _():
            acc_ref[...] = jnp.zeros_like(acc_ref)
            dws_ref[...] = jnp.zeros_like(dws_ref)
            dmix_ref[...] = jnp.zeros_like(dmix_ref)

        u = u_ref[...]
        v = v_ref[...]
        lg = lg_ref[...]
        gu, t_u = _gelu(u)
        gv, t_v = _gelu(v)
        vhat, rs = _layer_norm_stats(gv)
        vnb = (vhat * lg + lb_ref[...]).astype(BF16)
        da_v = _mm_nt(dya_ref[...], wa_ref[...])
        for ch in range(tm // GCH):
            rows = slice(GCH * ch, GCH * (ch + 1))
            for g in range(NG):
                cols = slice(128 * g, 128 * (g + 1))
                vng = vnb[rows, cols]
                mixed = _mm(wm_ref[g], vng) + bt_ref[:, g:g + 1]
                dag = da_v[rows, cols]
                dmx = dag * gu[rows, cols]
                du_s[rows, cols] = dag * mixed
                dmxb = dmx.astype(BF16)
                dws_ref[:, cols] += _mm_nt(dmxb, vng)
                dmix_ref[:, cols] += dmx
                dvn_s[rows, cols] = _mm(wmt_ref[g], dmxb)
        dp_ref[0] = (du_s[...] * _gelu_grad(u, t_u)).astype(BF16)
        dvn = dvn_s[...]
        acc_ref[0] += _rows8(dvn * vhat)
        acc_ref[1] += _rows8(dvn)
        dvh = dvn * lg
        dgv = rs * (dvh - _mean(dvh) - vhat * _mean(dvh * vhat))
        dp_ref[1] = (dgv * _gelu_grad(v, t_v)).astype(BF16)

    row = lambda i: (0, 0)
    w3 = lambda i: (0, 0, 0)
    return pl.pallas_call(
        body, name="gmlp_bwd", grid=(T // tm,),
        in_specs=[ANY, pl.BlockSpec((tm, D), lambda i: (i, 0)), pl.BlockSpec((D, D), row),
                  pl.BlockSpec((None, tm, D), lambda i: (0, i, 0)), pl.BlockSpec((None, tm, D), lambda i: (1, i, 0)),
                  pl.BlockSpec((1, D), row), pl.BlockSpec((1, D), row),
                  pl.BlockSpec((NG, GCH, GCH), w3), pl.BlockSpec((NG, GCH, GCH), w3),
                  pl.BlockSpec((GCH, NG), row)],
        out_specs=[pl.BlockSpec((2, tm, D), lambda i: (2, i, 0)),
                   pl.BlockSpec((2, 8, D), w3), pl.BlockSpec((GCH, D), row), pl.BlockSpec((GCH, D), row)],
        out_shape=[SDS(dproj.shape, BF16), SDS((2, 8, D), F32), SDS((GCH, D), F32), SDS((GCH, D), F32)],
        scratch_shapes=[pltpu.VMEM((tm, D), F32), pltpu.VMEM((tm, D), F32)],
        input_output_aliases={0: 0},
        compiler_params=_cparams(),
    )(dproj, dya, w_a, proj, proj, ln_g, ln_b, wm, wm_t, b_t)


def _proj_bwd(dproj, w_in4, x, dx1, g_mix, job=None):
    T = x.shape[0]
    tm = min(256, T)
    order = (2, 3, 4, 5, 0, 1, 6, 7)

    def body(dp_ref, w_ref, x_ref, dx1_ref, g_ref, gx_ref, acc_ref):
        @pl.when(pl.program_id(0) == 0)
        def _():
            acc_ref[...] = jnp.zeros_like(acc_ref)

        dh = None
        for m, og in enumerate(order):
            part = _mm_nt(dp_ref[m], w_ref[og // 2, :, D * (og % 2):D * (og % 2 + 1)])
            dh = part if dh is None else dh + part
        xv = x_ref[...]
        r = lax.rsqrt(_mean(xv * xv) + EPS)
        xn = xv * r
        acc_ref[...] += _rows8(dh * xn)
        dxn = dh * g_ref[...]
        gx_ref[...] = dx1_ref[...] + r * (dxn - xn * _mean(dxn * xn))

    t = lambda i: (i, 0)
    return _call(
        body, name="proj_bwd", grid=(T // tm,),
        in_specs=[pl.BlockSpec((NIN, tm, D), lambda i: (0, i, 0)),
                  pl.BlockSpec((NCHIP, D, 2 * D), lambda i: (0, 0, 0), pipeline_mode=pl.Buffered(1)),
                  pl.BlockSpec((tm, D), t), pl.BlockSpec((tm, D), t), pl.BlockSpec((1, D), lambda i: (0, 0))],
        out_specs=[pl.BlockSpec((tm, D), t), pl.BlockSpec((8, D), lambda i: (0, 0))],
        out_shape=[SDS((T, D), F32), SDS((8, D), F32)],
        args=(dproj, w_in4, x, dx1, g_mix), job=job)


def _dw_call(name, a, b, a_spec, b_spec, o_spec, out_shape, nblk, tt, job=None, prefetch=None):
    T = a.shape[-2]

    def body(*refs):
        a_ref, b_ref, o_ref = refs[-3:]

        @pl.when(pl.program_id(1) == 0)
        def _():
            o_ref[...] = jnp.zeros_like(o_ref)
        o_ref[...] += _mm_tn(a_ref[...], b_ref[...])

    (out,), job_out = _call(
        body, name=name, grid=(nblk, T // tt), in_specs=[a_spec, b_spec], out_specs=[o_spec],
        out_shape=[out_shape], args=(a, b), job=job, prefetch=prefetch)
    return out, job_out


def _dw_in_half(name, place, hb, dproj, mine, job=None):
    tt = min(DW_TOKENS, hb.shape[0])

    def comp(k, pc):
        return _component_of(2 * k + (pc[1] if mine else 1 - pc[1]))

    return _dw_call(
        name, hb, dproj,
        pl.BlockSpec((tt, D), lambda k, t, pc: (t, 0)),
        pl.BlockSpec((None, tt, D), lambda k, t, pc: (comp(k, pc), t, 0)),
        pl.BlockSpec((None, D, D), lambda k, t, pc: (k, 0, 0)),
        SDS((NCHIP, D, D), F32), NCHIP, tt, job, place)


def _dw_gate_up(h2b, dgu4, job=None):
    tt = min(DW_TOKENS, h2b.shape[0])
    return _dw_call(
        "dw_gate_up", h2b, dgu4,
        pl.BlockSpec((tt, D), lambda k, t: (t, 0)),
        pl.BlockSpec((None, tt, FFS), lambda k, t: (k, t, 0)),
        pl.BlockSpec((None, D, FFS), lambda k, t: (k, 0, 0)),
        SDS((NCHIP, D, FFS), F32), NCHIP, tt, job)


def _dw_down(act, dx2b, job=None):
    tt = min(DW_TOKENS, act.shape[0])
    g, job_out = _dw_call(
        "dw_down", act, dx2b,
        pl.BlockSpec((tt, FFS), lambda k, t: (t, k)),
        pl.BlockSpec((tt, D), lambda k, t: (t, 0)),
        pl.BlockSpec((FFS, D), lambda k, t: (k, 0)),
        SDS((FF, D), F32), 2, tt, job)
    return g.reshape(NCHIP, FF // NCHIP, D), job_out


def _dw_square(name, a, b, job=None):
    tt = min(DW_TOKENS, a.shape[0])
    g, job_out = _dw_call(
        name, a, b,
        pl.BlockSpec((tt, D), lambda k, t: (t, 0)), pl.BlockSpec((tt, D), lambda k, t: (t, 0)),
        pl.BlockSpec((D, D), lambda k, t: (0, 0)), SDS((D, D), F32), 1, tt, job)
    return g.reshape(NCHIP, D // NCHIP, D), job_out


def _place():
    x, y, c = lax.axis_index("x"), lax.axis_index("y"), lax.axis_index("c")
    return x, y, c, 2 * x + y


def _chip_at(x, y, s):
    return x ^ (s >> 1), y ^ (s & 1)


class _Job:
    def __init__(self, ins, out_shapes, sems, start, finish, aliases=None, mid=None):
        self.ins, self.out_shapes, self.sems = list(ins), list(out_shapes), list(sems)
        self.start, self.finish, self.aliases = start, finish, dict(aliases or {})
        self.mid = mid if mid is not None else (lambda ins, outs, sems: None)


def _join_jobs(*jobs):
    def cut(refs, sizes):
        out, at = [], 0
        for n in sizes:
            out.append(refs[at:at + n])
            at += n
        return out

    ni = [len(j.ins) for j in jobs]
    no = [len(j.out_shapes) for j in jobs]
    ns = [len(j.sems) for j in jobs]

    def run(which):
        def go(ins, outs, sems):
            for j, a, b, c in zip(jobs, cut(ins, ni), cut(outs, no), cut(sems, ns)):
                getattr(j, which)(a, b, c)
        return go

    aliases = {}
    for k, j in enumerate(jobs):
        for a, b in j.aliases.items():
            aliases[sum(ni[:k]) + a] = sum(no[:k]) + b
    return _Job([a for j in jobs for a in j.ins], [o for j in jobs for o in j.out_shapes],
                [s for j in jobs for s in j.sems], run("start"), run("finish"), aliases, run("mid"))


def _call(body, *, name, grid, in_specs, out_specs, out_shape, args, scratch_shapes=(), aliases=None,
          job=None, prefetch=None):
    n_in, n_out, n_scr = len(in_specs), len(out_specs), len(scratch_shapes)
    npf = 0 if prefetch is None else 1
    job = job if job is not None else _Job([], [], [], lambda *a: None, lambda *a: None)
    ji, jo = len(job.ins), len(job.out_shapes)
    steps = math.prod(grid)

    def wrapped(*refs):
        pf, refs = refs[:npf], refs[npf:]
        ins, jin = refs[:n_in], refs[n_in:n_in + ji]
        o0 = n_in + ji
        outs, jout = refs[o0:o0 + n_out], refs[o0 + n_out:o0 + n_out + jo]
        s0 = o0 + n_out + jo
        scr, jsem = refs[s0:s0 + n_scr], refs[s0 + n_scr:]
        step = functools.reduce(lambda acc, ag: acc * ag[1] + pl.program_id(ag[0]), enumerate(grid), 0)
        if ji or jo:
            @pl.when(step == 0)
            def _():
                job.start(jin, jout, jsem)

        body(*pf, *ins, *outs, *scr)

        if ji or jo:
            @pl.when(step == steps // 2)
            def _():
                job.mid(jin, jout, jsem)

            @pl.when(step == steps - 1)
            def _():
                job.finish(jin, jout, jsem)

    io = {npf + a: b for a, b in dict(aliases or {}).items()}
    io.update({npf + n_in + a: n_out + b for a, b in job.aliases.items()})
    kw = dict(in_specs=list(in_specs) + [ANY] * ji, out_specs=list(out_specs) + [ANY] * jo,
              scratch_shapes=list(scratch_shapes) + job.sems)
    if npf:
        kw = dict(grid_spec=pltpu.PrefetchScalarGridSpec(num_scalar_prefetch=1, grid=grid, **kw))
    else:
        kw["grid"] = grid
    res = pl.pallas_call(
        wrapped, name=name, out_shape=list(out_shape) + job.out_shapes, input_output_aliases=io,
        compiler_params=_cparams(has_side_effects=bool(ji or jo)), **kw,
    )(*(() if prefetch is None else (prefetch,)), *args, *job.ins)
    return list(res[:n_out]), list(res[n_out:])


def _run_job(job, name):
    ji, jo = len(job.ins), len(job.out_shapes)

    def body(*refs):
        jin, jout, jsem = refs[:ji], refs[ji:ji + jo], refs[ji + jo:]
        job.start(jin, jout, jsem)
        job.finish(jin, jout, jsem)

    return list(pl.pallas_call(
        body, name=name, in_specs=[ANY] * ji, out_specs=[ANY] * jo, out_shape=job.out_shapes,
        scratch_shapes=job.sems, input_output_aliases=job.aliases,
        compiler_params=pltpu.CompilerParams(has_side_effects=True))(*job.ins))


def _cast_shard(name, place, w):
    rows, cols = w.shape
    tr = 352 if rows % 352 == 0 else 256

    def body(pc_ref, w_ref, o_ref):
        del pc_ref
        o_ref[...] = w_ref[...].astype(BF16)

    return pl.pallas_call(
        body, name=name,
        grid_spec=pltpu.PrefetchScalarGridSpec(
            num_scalar_prefetch=1, grid=(rows // tr,),
            in_specs=[pl.BlockSpec((tr, cols), lambda i, pc: (i, 0))],
            out_specs=pl.BlockSpec((None, tr, cols), lambda i, pc: (pc[0], i, 0))),
        out_shape=SDS((NCHIP, rows, cols), BF16),
        compiler_params=_cparams(),
    )(place, w)


def _sibling_copy(ref, send_sem, recv_sem):
    x, y, c, _ = _place()
    return pltpu.make_async_remote_copy(src_ref=ref, dst_ref=ref, send_sem=send_sem, recv_sem=recv_sem,
                                        device_id=(x, y, 1 - c), device_id_type=MESH)


def _half_rows(arr, slot, core):
    half = arr.shape[1] // 2
    return arr.at[slot, pl.ds(pl.multiple_of(core * half, 16), half)]


def _quarter_rows(arr, slot, core, q):
    quarter = arr.shape[1] // 4
    return arr.at[slot, pl.ds(pl.multiple_of((2 * core + q) * quarter, 16), quarter)]


def _chip_copy(ref, dist, send_sem, recv_sem):
    x, y, c, _ = _place()
    cx, cy = _chip_at(x, y, dist)
    return pltpu.make_async_remote_copy(src_ref=ref, dst_ref=ref, send_sem=send_sem, recv_sem=recv_sem,
                                        device_id=(cx, cy, c), device_id_type=MESH)


def _gather_sems(n):
    dma = pltpu.SemaphoreType.DMA
    return [dma((n, 2))] * 4 + [dma((n, 4))] * 2


def _gather_start(arrs, sems):
    dsend, drecv = sems[0], sems[1]
    _, _, c, j = _place()
    for w, arr in enumerate(arrs):
        for dist in (1, 2):
            _chip_copy(_half_rows(arr, j, c), dist, dsend.at[w, dist - 1], drecv.at[w, dist - 1]).start()


def _gather_land(arrs, sems, dist, first=0):
    dsend, drecv, rsend, rrecv, fsend, frecv = sems
    _, _, c, j = _place()
    if dist < 3:
        other = 3 - dist
        for w, arr in enumerate(arrs, first):
            landed = _half_rows(arr, j ^ dist, c)
            _chip_copy(landed, dist, dsend.at[w, dist - 1], drecv.at[w, dist - 1]).wait_recv()
            relay = _quarter_rows(arr, j ^ dist, c, other - 1)
            _chip_copy(relay, other, rsend.at[w, other - 1], rrecv.at[w, other - 1]).start()
            _sibling_copy(landed, fsend.at[w, dist - 1], frecv.at[w, dist - 1]).start()
        for w, arr in enumerate(arrs, first):
            theirs = _half_rows(arr, j ^ dist, 1 - c)
            _sibling_copy(theirs, fsend.at[w, dist - 1], frecv.at[w, dist - 1]).wait_recv()
    else:
        for w, arr in enumerate(arrs, first):
            for via in (1, 2):
                piece = _quarter_rows(arr, j ^ 3, c, via - 1)
                _chip_copy(piece, via, rsend.at[w, via - 1], rrecv.at[w, via - 1]).wait_recv()
                _sibling_copy(piece, fsend.at[w, 1 + via], frecv.at[w, 1 + via]).start()
        for w, arr in enumerate(arrs, first):
            for via in (1, 2):
                theirs = _quarter_rows(arr, j ^ 3, 1 - c, via - 1)
                _sibling_copy(theirs, fsend.at[w, 1 + via], frecv.at[w, 1 + via]).wait_recv()


def _gather_drain(arrs, sems):
    dsend, drecv, rsend, rrecv, fsend, frecv = sems
    _, _, c, j = _place()
    for w, arr in enumerate(arrs):
        for dist in (1, 2):
            other = 3 - dist
            _chip_copy(_half_rows(arr, j, c), dist, dsend.at[w, dist - 1], drecv.at[w, dist - 1]).wait_send()
            _chip_copy(_quarter_rows(arr, j ^ dist, c, other - 1), other,
                       rsend.at[w, other - 1], rrecv.at[w, other - 1]).wait_send()
            _sibling_copy(_half_rows(arr, j ^ dist, c), fsend.at[w, dist - 1], frecv.at[w, dist - 1]).wait_send()
            _sibling_copy(_quarter_rows(arr, j ^ 3, c, dist - 1),
                          fsend.at[w, 1 + dist], frecv.at[w, 1 + dist]).wait_send()


def _gather_neighbours(arrs, sems):
    _gather_land(arrs, sems, 1)
    _gather_land(arrs, sems, 2)


def _gather_finish(arrs, sems):
    _gather_land(arrs, sems, 3)
    _gather_drain(arrs, sems)


def _gather_job(arrs):
    n = len(arrs)
    return _Job(arrs, [SDS(a.shape, a.dtype) for a in arrs], _gather_sems(n),
                lambda ins, outs, sems: _gather_start(outs, sems),
                lambda ins, outs, sems: _gather_finish(outs, sems), {k: k for k in range(n)},
                mid=lambda ins, outs, sems: _gather_neighbours(outs, sems))


def _exchange_job(arrs, out_shapes, n, copies):
    def start(ins, outs, sems):
        for cp in copies(ins, outs, sems[0], sems[1]):
            cp.start()

    def finish(ins, outs, sems):
        for cp in copies(ins, outs, sems[0], sems[1]):
            cp.wait()

    return _Job(arrs, out_shapes, [pltpu.SemaphoreType.DMA((n,))] * 2, start, finish)


def _pair_exchange_job(grads):
    def copies(ins, outs, send_sem, recv_sem):
        x, y, c, _ = _place()
        res = []
        for w in range(len(grads)):
            half = ins[w].shape[1] // 2
            theirs = pl.ds(pl.multiple_of((1 - c) * half, 8), half)
            res.append(pltpu.make_async_remote_copy(
                src_ref=ins[w].at[:, theirs, :], dst_ref=outs[w], send_sem=send_sem.at[w],
                recv_sem=recv_sem.at[w], device_id=(x, y, 1 - c), device_id_type=MESH))
        return res

    return _exchange_job(grads, [SDS((NCHIP, g.shape[1] // 2, g.shape[2]), F32) for g in grads],
                         len(grads), copies)


def _row_tile(rows, cols):
    tr = rows
    while tr * cols * 4 > ELEMENTWISE_BLOCK_BYTES and tr % 32 == 0:
        tr //= 2
    return tr


def _pair_sum(name, place, g, sib):
    half, cols = sib.shape[1], sib.shape[2]
    tr = _row_tile(half, cols)
    nt = half // tr
    mine = nt if g.shape[1] == 2 * half else 0

    def body(pc_ref, g_ref, s_ref, own_ref, out_ref):
        del pc_ref
        v = g_ref[...] + s_ref[...]
        out_ref[...] = v.astype(BF16)

        @pl.when(pl.program_id(1) == 0)
        def _():
            own_ref[...] = v

    return pl.pallas_call(
        body, name=name,
        grid_spec=pltpu.PrefetchScalarGridSpec(
            num_scalar_prefetch=1, grid=(nt, NCHIP),
            in_specs=[pl.BlockSpec((None, tr, cols), lambda i, s, pc: (pc[0] ^ s, pc[1] * mine + i, 0)),
                      pl.BlockSpec((None, tr, cols), lambda i, s, pc: (pc[0] ^ s, i, 0))],
            out_specs=[pl.BlockSpec((tr, cols), lambda i, s, pc: (i, 0)),
                       pl.BlockSpec((None, tr, cols), lambda i, s, pc: (s, i, 0))]),
        out_shape=[SDS((half, cols), F32), SDS((NCHIP, half, cols), BF16)],
        compiler_params=_cparams(),
    )(place, g, sib)


def _chip_exchange_job(parts):
    def copies(ins, outs, send_sem, recv_sem):
        x, y, c, _ = _place()
        res = []
        for w in range(len(parts)):
            for s in range(1, NCHIP):
                cx, cy = _chip_at(x, y, s)
                k = w * (NCHIP - 1) + s - 1
                res.append(pltpu.make_async_remote_copy(
                    src_ref=ins[w].at[s], dst_ref=outs[w].at[s - 1], send_sem=send_sem.at[k],
                    recv_sem=recv_sem.at[k], device_id=(cx, cy, c), device_id_type=MESH))
        return res

    return _exchange_job(parts, [SDS((NCHIP - 1,) + p.shape[1:], BF16) for p in parts],
                         len(parts) * (NCHIP - 1), copies)


def _chip_sum(name, own, rem):
    half, cols = own.shape
    tr = _row_tile(half, cols)

    def body(own_ref, rem_ref, out_ref):
        out_ref[...] = ((own_ref[...] + rem_ref[0].astype(F32)) + rem_ref[1].astype(F32)) + rem_ref[2].astype(F32)

    return pl.pallas_call(
        body, name=name, grid=(half // tr,),
        in_specs=[pl.BlockSpec((tr, cols), lambda i: (i, 0)),
                  pl.BlockSpec((NCHIP - 1, tr, cols), lambda i: (0, i, 0))],
        out_specs=pl.BlockSpec((tr, cols), lambda i: (i, 0)),
        out_shape=SDS((half, cols), F32),
        compiler_params=_cparams(),
    )(own, rem)


def _share_halves_job(halves):
    def copies(ins, outs, send_sem, recv_sem):
        x, y, c, _ = _place()
        return [pltpu.make_async_remote_copy(
            src_ref=ins[w], dst_ref=outs[w], send_sem=send_sem.at[w], recv_sem=recv_sem.at[w],
            device_id=(x, y, 1 - c), device_id_type=MESH) for w in range(len(halves))]

    return _exchange_job(halves, [SDS(h.shape, F32) for h in halves], len(halves), copies)


def _adamw_math(w, g, m, v):
    m = B1 * m + (1.0 - B1) * g
    v = B2 * v + (1.0 - B2) * (g * g)
    m_hat = m / (1.0 - B1 ** STEP)
    v_hat = v / (1.0 - B2 ** STEP)
    delta = -LR * (m_hat / (jnp.sqrt(v_hat) + AEPS) + WD * w)
    return delta, m, v


def _adamw(name, place, w, own, sib, m, v):
    rows, cols = w.shape
    by_cols = own.shape[0] == rows
    half, pc_cols = (rows, cols // 2) if by_cols else (rows // 2, cols)
    tr = _row_tile(half, pc_cols)
    nt = half // tr

    def body(pc_ref, w_ref, own_ref, sib_ref, m_ref, v_ref, g_ref, d_ref, mo_ref, vo_ref):
        g = jnp.where(pl.program_id(0) == pc_ref[1], own_ref[...], sib_ref[...])
        d, mn, vn = _adamw_math(w_ref[...], g, m_ref[...], v_ref[...])
        g_ref[...] = g
        d_ref[...] = d
        mo_ref[...] = mn
        vo_ref[...] = vn

    full = pl.BlockSpec((tr, pc_cols), (lambda h, i, pc: (i, h)) if by_cols else (lambda h, i, pc: (h * nt + i, 0)))
    part = pl.BlockSpec((tr, pc_cols), lambda h, i, pc: (i, 0))
    return pl.pallas_call(
        body, name=name,
        grid_spec=pltpu.PrefetchScalarGridSpec(
            num_scalar_prefetch=1, grid=(2, nt),
            in_specs=[full, part, part, full, full], out_specs=[full] * 4),
        out_shape=[SDS((rows, cols), F32)] * 4,
        compiler_params=_cparams(),
    )(place, w, own, sib, m, v)


def _small_allreduce_adamw(sp, w, m, v):
    shape = sp.shape

    def body(sp_ref, w_ref, m_ref, v_ref, g_ref, d_ref, mo_ref, vo_ref,
             sib_s, pair_s, chip_s, send_sem, recv_sem):
        x, y, c, j = _place()
        cp = pltpu.make_async_remote_copy(
            src_ref=sp_ref, dst_ref=sib_s, send_sem=send_sem.at[0], recv_sem=recv_sem.at[0],
            device_id=(x, y, 1 - c), device_id_type=MESH)
        cp.start()
        cp.wait()
        pair_s[...] = sp_ref[...] + sib_s[...]
        half = shape[0] // 2
        mine = pl.ds(pl.multiple_of(c * half, 8), half)
        cps = []
        for s in range(1, NCHIP):
            cx, cy = _chip_at(x, y, s)
            cp = pltpu.make_async_remote_copy(
                src_ref=pair_s.at[mine], dst_ref=chip_s.at[s, mine], send_sem=send_sem.at[s],
                recv_sem=recv_sem.at[s], device_id=(cx, cy, c), device_id_type=MESH)
            cp.start()
            cps.append(cp)
        chip_s[0] = pair_s[...]
        for cp in cps:
            cp.wait()
        cps = []
        for s in range(1, NCHIP):
            cp = pltpu.make_async_remote_copy(
                src_ref=chip_s.at[s, mine], dst_ref=chip_s.at[s, mine], send_sem=send_sem.at[NCHIP + s],
                recv_sem=recv_sem.at[NCHIP + s], device_id=(x, y, 1 - c), device_id_type=MESH)
            cp.start()
            cps.append(cp)
        for cp in cps:
            cp.wait()
        tot = chip_s[j]
        for k in range(1, NCHIP):
            tot = tot + chip_s[k ^ j]
        g_ref[...] = tot
        d, mn, vn = _adamw_math(w_ref[...], tot, m_ref[...], v_ref[...])
        d_ref[...] = d
        mo_ref[...] = mn
        vo_ref[...] = vn

    vm = pl.BlockSpec(memory_space=pltpu.VMEM)
    return pl.pallas_call(
        body, name="small_allreduce_adamw",
        in_specs=[vm] * 4, out_specs=[vm] * 4, out_shape=[SDS(shape, F32)] * 4,
        scratch_shapes=[pltpu.VMEM(shape, F32), pltpu.VMEM(shape, F32), pltpu.VMEM((NCHIP,) + shape, F32),
                        pltpu.SemaphoreType.DMA((2 * NCHIP,)), pltpu.SemaphoreType.DMA((2 * NCHIP,))],
        compiler_params=pltpu.CompilerParams(has_side_effects=True),
    )(sp, w, m, v)


def _pack_small(first, mix, ln_g, ln_b, b_s, lbt, hn, ffn, fin, w_s):
    rows = [first.reshape(1, D), mix.reshape(1, D), ln_g.reshape(1, D), ln_b.reshape(1, D),
            b_s.reshape(1, D), lbt.reshape(2, D), hn.reshape(1, D), ffn.reshape(1, D), fin.reshape(1, D),
            jnp.zeros((6, D), F32)]
    return jnp.concatenate(rows + [w_s.reshape(NG, GCH, GCH).transpose(1, 0, 2).reshape(GCH, D)], axis=0)


def _unpack_small(p):
    w_s = p[16:].reshape(GCH, NG, GCH).transpose(1, 0, 2).reshape(1, NG, GCH, GCH)
    return dict(norm_mix_g=p[1:2], gmlp_ln_g=p[2:3], gmlp_ln_b=p[3:4], gmlp_b_s=p[4].reshape(1, NG, GCH),
                hgrn_lb_table=p[5:7], hgrn_norm_g=p[7:8], norm_ffn_g=p[8:9], norm_final_g=p[9],
                gmlp_w_s=w_s)


SMALL = ("norm_mix_g", "gmlp_ln_g", "gmlp_ln_b", "gmlp_w_s", "gmlp_b_s", "hgrn_lb_table", "hgrn_norm_g",
         "norm_ffn_g", "norm_final_g")
BIG = ("w_in", "w_gate_up", "w_branch_a", "w_branch_b", "w_out", "w_down")
ORDER = ("norm_mix_g", "w_in", "gmlp_ln_g", "gmlp_ln_b", "gmlp_w_s", "gmlp_b_s", "hgrn_lb_table",
         "hgrn_norm_g", "w_branch_a", "w_branch_b", "w_out", "norm_ffn_g", "w_gate_up", "w_down",
         "norm_final_g")


def kernel(x, norm_mix_g, w_in, gmlp_ln_g, gmlp_ln_b, gmlp_w_s, gmlp_b_s, hgrn_lb_table, hgrn_norm_g, w_branch_a, w_branch_b, w_out, norm_ffn_g, w_gate_up, w_down, norm_final_g, loss_target, m_norm_mix_g, m_w_in, m_gmlp_ln_g, m_gmlp_ln_b, m_gmlp_w_s, m_gmlp_b_s, m_hgrn_lb_table, m_hgrn_norm_g, m_w_branch_a, m_w_branch_b, m_w_out, m_norm_ffn_g, m_w_gate_up, m_w_down, m_norm_final_g, v_norm_mix_g, v_w_in, v_gmlp_ln_g, v_gmlp_ln_b, v_gmlp_w_s, v_gmlp_b_s, v_hgrn_lb_table, v_hgrn_norm_g, v_w_branch_a, v_w_branch_b, v_w_out, v_norm_ffn_g, v_w_gate_up, v_w_down, v_norm_final_g):
    args = dict(locals())
    T = x.shape[1]
    xs = x.reshape(T, D)
    target = loss_target.reshape(T, D)
    big = {n: args[n].reshape(args[n].shape[1:]) for n in BIG}
    big_m = {n: args["m_" + n].reshape(args[n].shape[1:]) for n in BIG}
    big_v = {n: args["v_" + n].reshape(args[n].shape[1:]) for n in BIG}

    x_i, y_i, c_i = lax.axis_index("x"), lax.axis_index("y"), lax.axis_index("c")
    place = jnp.stack([2 * x_i + y_i, c_i]).astype(jnp.int32)
    cast = {n: _cast_shard("cast_" + n, place, big[n]) for n in BIG}
    tril = jnp.tril(jnp.ones((GCH, GCH), bool))
    wm = jnp.where(tril, gmlp_w_s[0], 0.0).astype(BF16)
    wm_t = jnp.swapaxes(wm, 1, 2)
    b_t = gmlp_b_s[0].T

    (proj, hb), w_in4, (w_a4, w_b4, w_out4, w_down4) = _proj_fwd(
        place, xs, norm_mix_g, cast["w_in"], [cast[n] for n in ("w_branch_a", "w_branch_b", "w_out", "w_down")])
    (ab,), _ = _gmlp_fwd(proj, gmlp_ln_g, gmlp_ln_b, wm, b_t)
    (o_raw, obb, st_before), (w_gu4,) = _hgrn_fwd(
        proj, hgrn_lb_table, hgrn_norm_g, job=_gather_job([cast["w_gate_up"]]))
    w_a, w_b, w_o = (w.reshape(D, D) for w in (w_a4, w_b4, w_out4))
    (mgb, x1), _ = _merge_fwd(xs, ab, obb, proj, w_a, w_b, w_o)
    w_dn = w_down4.reshape(FF, D)
    act, dx2b, h2b, dgu4, dx1, dx1b, acc_ffn = _ffn_fwd_bwd(
        x1, target, norm_ffn_g, norm_final_g.reshape(1, D), w_gu4, w_dn)

    grads, owns, parts, halves, sibh = {}, {}, {}, {}, {}

    def pair_sums(names, sibs):
        for n, s in zip(names, sibs):
            owns[n], parts[n] = _pair_sum("rs_pair_sum_" + n, place, grads[n], s)

    def chip_sums(names, got):
        for n, r in zip(names, got):
            halves[n] = _chip_sum("rs_chip_sum_" + n, owns[n], r)

    ffn, mix = ("w_gate_up", "w_down"), ("w_branch_a", "w_branch_b", "w_out")
    grads["w_gate_up"], _ = _dw_gate_up(h2b, dgu4)
    grads["w_down"], _ = _dw_down(act, dx2b)
    (dya, dyb, dproj), got = _merge_bwd(
        dx1b, ab, obb, proj, w_o, w_a, w_b, job=_pair_exchange_job([grads[n] for n in ffn]))
    pair_sums(ffn, got)
    grads["w_branch_a"], _ = _dw_square("dw_branch_a", ab, dya)
    grads["w_branch_b"], _ = _dw_square("dw_branch_b", obb, dyb)
    grads["w_out"], _ = _dw_square("dw_out", mgb, dx1b)
    (dproj, acc_hgrn), got = _hgrn_bwd(
        dproj, dyb, w_b, o_raw, proj, st_before, hgrn_lb_table, hgrn_norm_g,
        job=_join_jobs(_chip_exchange_job([parts[n] for n in ffn]), _pair_exchange_job([grads[n] for n in mix])))
    chip_sums(ffn, got[:2])
    pair_sums(mix, got[2:])
    dproj, acc_ln, dws, dmix = _gmlp_bwd(dproj, dya, w_a, proj, gmlp_ln_g, gmlp_ln_b, wm, wm_t, b_t)
    for_sibling, got = _dw_in_half(
        "dw_in_sibling_half", place, hb, dproj, False,
        job=_join_jobs(_share_halves_job([halves[n] for n in ffn]), _chip_exchange_job([parts[n] for n in mix])))
    sibh.update(zip(ffn, got[:2]))
    chip_sums(mix, got[2:])
    grads["w_in"], got = _dw_in_half(
        "dw_in_own_half", place, hb, dproj, True, job=_share_halves_job([for_sibling]))
    pair_sums(("w_in",), got)
    (grad_x, acc_mix), got = _proj_bwd(
        dproj, w_in4, xs, dx1, norm_mix_g,
        job=_join_jobs(_chip_exchange_job([parts["w_in"]]), _share_halves_job([halves[n] for n in mix])))
    chip_sums(("w_in",), got[:1])
    sibh.update(zip(mix, got[1:]))
    (sibh["w_in"],) = _run_job(_share_halves_job([halves["w_in"]]), "rs_share_halves_w_in")
    out = {}
    for n in BIG:
        g, d, mn, vn = _adamw("adamw_" + n, place, big[n], halves[n], sibh[n], big_m[n], big_v[n])
        shp = args[n].shape
        out[n] = (g.reshape(shp), d.reshape(shp), mn.reshape(shp), vn.reshape(shp))

    lbv = jax.nn.sigmoid(hgrn_lb_table[0] - hgrn_lb_table[1])
    d_t0 = jnp.sum(acc_hgrn[0], axis=0) * lbv * (1.0 - lbv)
    loss_row = jnp.zeros((D,), F32).at[0].set(jnp.sum(acc_ffn[0]))
    dws_m = jnp.where(tril[:, None, :], dws.reshape(GCH, NG, GCH), 0.0).transpose(1, 0, 2)
    db_s = jnp.sum(dmix.reshape(GCH, NG, GCH), axis=-1).T
    sp = _pack_small(loss_row, jnp.sum(acc_mix, 0), jnp.sum(acc_ln[0], 0), jnp.sum(acc_ln[1], 0), db_s,
                     jnp.stack([d_t0, -d_t0]), jnp.sum(acc_hgrn[1], 0), jnp.sum(acc_ffn[2], 0),
                     jnp.sum(acc_ffn[1], 0), dws_m)
    zero = jnp.zeros((D,), F32)

    def pack(prefix):
        a = lambda n: args[prefix + n]
        return _pack_small(zero, a("norm_mix_g"), a("gmlp_ln_g"), a("gmlp_ln_b"), a("gmlp_b_s"),
                           a("hgrn_lb_table"), a("hgrn_norm_g"), a("norm_ffn_g"), a("norm_final_g"),
                           a("gmlp_w_s"))

    packed = _small_allreduce_adamw(sp, pack(""), pack("m_"), pack("v_"))
    loss = packed[0][0, 0]
    small = [_unpack_small(p) for p in packed]
    for n in SMALL:
        out[n] = tuple(s[n] for s in small)
    return (loss, grad_x.reshape(x.shape), *[out[n][0] for n in ORDER], *[out[n][1] for n in ORDER],
            *[out[n][2] for n in ORDER], *[out[n][3] for n in ORDER])
```

```python
import functools
import math

import jax
import jax.numpy as jnp
from jax import lax
from jax.experimental import pallas as pl
from jax.experimental.pallas import tpu as pltpu

F32 = jnp.float32
BF16 = jnp.bfloat16
SDS = jax.ShapeDtypeStruct
MESH = pl.DeviceIdType.MESH
ANY = pl.BlockSpec(memory_space=pl.ANY)

D = 1024
NIN = 8
NG = 8
GCH = 128
NH = 8
HD = 128
HCH = 64
HGRN_HB = 4
HW = HGRN_HB * HD
DW_TOKENS = 2048
ELEMENTWISE_BLOCK_BYTES = 2 * 1024 * 1024
PROJ_OUT_SLOTS = 4
FF = 2816
FFS = 1408
NCHIP = 4
EPS = 1e-6
QSCALE = HD ** -0.5
GELU_C0 = math.sqrt(2.0 / math.pi)
GELU_C1 = 0.044715
LR, B1, B2, AEPS, WD, STEP = 0.001, 0.9, 0.999, 1e-08, 0.01, 10
VMEM_LIMIT_V7X = 56 * 1024 * 1024
SP_ROWS = 144


def _cparams(**kw):
    return pltpu.CompilerParams(vmem_limit_bytes=VMEM_LIMIT_V7X, **kw)


def _mm(a, b):
    return jnp.dot(a, b, preferred_element_type=F32)


def _mm_nt(a, b):
    return lax.dot_general(a, b, (((1,), (1,)), ((), ())), preferred_element_type=F32)


def _mm_tn(a, b):
    return lax.dot_general(a, b, (((0,), (0,)), ((), ())), preferred_element_type=F32)


def _rows8(x):
    r, c = x.shape
    return jnp.sum(x.reshape(r // 8, 8, c), axis=0)


def _mean(x):
    return jnp.mean(x, axis=-1, keepdims=True)


def _sigmoid(x):
    return 1.0 / (1.0 + jnp.exp(-x))


def _gelu(x):
    t = jnp.tanh(GELU_C0 * (x + GELU_C1 * x * x * x))
    return 0.5 * x * (1.0 + t), t


def _gelu_grad(x, t):
    return 0.5 * (1.0 + t) + 0.5 * x * (1.0 - t * t) * (GELU_C0 * (1.0 + 3.0 * GELU_C1 * x * x))


def _component_of(group):
    return jnp.where(group < 6, (group + 4) % 6, group)


def _proj_fwd(place, x, g_mix, w_in4, later):
    T = x.shape[0]
    tm = min(1024, T)
    ni = T // tm
    n = len(later)

    def body(pc_ref, x_ref, g_ref, *rest):
        proj_ref, h_ref, w_all = rest[1 + n:4 + n]
        gathered = rest[4 + n:4 + 2 * n]
        hs, wbuf, wsem, obuf, osem = rest[4 + 2 * n:9 + 2 * n]
        w_sems, later_sems = rest[9 + 2 * n:15 + 2 * n], rest[15 + 2 * n:]
        jp, i = pl.program_id(0), pl.program_id(1)
        w_cols = [w_all.at[:, :, pl.ds(k * D, D)] for k in range(2)]

        def w_copy(blk):
            cols = pl.ds(pl.multiple_of((blk % 2) * D, 128), D)
            return pltpu.make_async_copy(w_all.at[pc_ref[0] ^ (blk // 2), :, cols], wbuf.at[blk % 2],
                                         wsem.at[blk % 2])

        @pl.when((jp == 0) & (i == 0))
        def _():
            _gather_start(w_cols, w_sems)
            _gather_start(gathered, later_sems)
            w_copy(jp).start()

        @pl.when(i == 0)
        def _():
            w_copy(jp).wait()

        @pl.when(jp == 0)
        def _():
            xv = x_ref[...]
            r = lax.rsqrt(_mean(xv * xv) + EPS)
            hb = (xv * r * g_ref[...]).astype(BF16)
            hs[i] = hb
            h_ref[...] = hb

        step = jp * ni + i
        slot = step % PROJ_OUT_SLOTS

        def o_copy(slot_):
            comp = 2 * (pc_ref[0] ^ (jp // 2)) + jp % 2
            return pltpu.make_async_copy(
                obuf.at[slot_], proj_ref.at[comp, pl.ds(pl.multiple_of(i * tm, 8), tm)], osem.at[slot_])

        @pl.when(step >= PROJ_OUT_SLOTS)
        def _():
            o_copy(slot).wait()

        obuf[slot] = _mm(hs[i], wbuf[jp % 2])
        o_copy(slot).start()

        @pl.when(step == NIN * ni - 1)
        def _():
            for k in range(PROJ_OUT_SLOTS):
                o_copy((slot + 1 + k) % PROJ_OUT_SLOTS).wait()

        for nxt in range(1, NIN):
            @pl.when((jp == nxt - 1) & (i == ni - 1))
            def _():
                if nxt >= 2:
                    _gather_land([w_cols[nxt % 2]], w_sems, nxt // 2, first=nxt % 2)
                if nxt == 4:
                    _gather_neighbours(gathered, later_sems)
                w_copy(jp + 1).start()

        @pl.when((jp == NIN - 1) & (i == ni - 1))
        def _():
            _gather_drain(w_cols, w_sems)
            _gather_finish(gathered, later_sems)

    tile = lambda jp, i, pc: (jnp.where(jp == 0, i, ni - 1), 0)
    res = pl.pallas_call(
        body, name="proj_fwd",
        grid_spec=pltpu.PrefetchScalarGridSpec(
            num_scalar_prefetch=1, grid=(NIN, ni),
            in_specs=[pl.BlockSpec((tm, D), tile), pl.BlockSpec((1, D), lambda jp, i, pc: (0, 0))] + [ANY] * (1 + n),
            out_specs=[ANY, pl.BlockSpec((tm, D), tile)] + [ANY] * (1 + n),
            scratch_shapes=[pltpu.VMEM((ni, tm, D), BF16), pltpu.VMEM((2, D, D), BF16),
                            pltpu.SemaphoreType.DMA((2,)), pltpu.VMEM((PROJ_OUT_SLOTS, tm, D), F32),
                            pltpu.SemaphoreType.DMA((PROJ_OUT_SLOTS,))] + _gather_sems(2) + _gather_sems(n)),
        out_shape=[SDS((NIN, T, D), F32), SDS((T, D), BF16), SDS(w_in4.shape, BF16)]
        + [SDS(a.shape, a.dtype) for a in later],
        input_output_aliases={3 + k: 2 + k for k in range(1 + n)},
        compiler_params=_cparams(has_side_effects=True),
    )(place, x, g_mix, w_in4, *later)
    return res[:2], res[2], res[3:]


def _layer_norm_stats(gv):
    mu = _mean(gv)
    xc = gv - mu
    rs = lax.rsqrt(_mean(xc * xc) + EPS)
    return xc * rs, rs


def _gmlp_fwd(proj, ln_g, ln_b, wm, b_t, job=None):
    T = proj.shape[1]
    tm = min(256, T)

    def body(u_ref, v_ref, lg_ref, lb_ref, wm_ref, bt_ref, a_ref, a_s):
        gu, _ = _gelu(u_ref[...])
        gv, _ = _gelu(v_ref[...])
        vhat, _ = _layer_norm_stats(gv)
        vnb = (vhat * lg_ref[...] + lb_ref[...]).astype(BF16)
        for ch in range(tm // GCH):
            rows = slice(GCH * ch, GCH * (ch + 1))
            for g in range(NG):
                cols = slice(128 * g, 128 * (g + 1))
                mixed = _mm(wm_ref[g], vnb[rows, cols]) + bt_ref[:, g:g + 1]
                a_s[rows, cols] = gu[rows, cols] * mixed
        a_ref[...] = a_s[...].astype(BF16)

    row = lambda i: (0, 0)
    return _call(
        body, name="gmlp_fwd", grid=(T // tm,), job=job, args=(proj, proj, ln_g, ln_b, wm, b_t),
        in_specs=[pl.BlockSpec((None, tm, D), lambda i: (0, i, 0)), pl.BlockSpec((None, tm, D), lambda i: (1, i, 0)),
                  pl.BlockSpec((1, D), row), pl.BlockSpec((1, D), row),
                  pl.BlockSpec((NG, GCH, GCH), lambda i: (0, 0, 0)), pl.BlockSpec((GCH, NG), row)],
        out_specs=[pl.BlockSpec((tm, D), lambda i: (i, 0))],
        out_shape=[SDS((T, D), BF16)],
        scratch_shapes=[pltpu.VMEM((tm, D), F32)])


def _segment_scan(x, reverse=False):
    n = x.shape[0]
    t = lax.broadcasted_iota(jnp.int32, (HCH, HCH), 0)
    s = lax.broadcasted_iota(jnp.int32, (HCH, HCH), 1)
    tri = jnp.where((s >= t) if reverse else (s <= t), 1.0, 0.0).astype(BF16)
    hi = x.astype(BF16)
    rest = x - hi.astype(F32)
    mid = rest.astype(BF16)
    lo = (rest - mid.astype(F32)).astype(BF16)
    out = []
    for seg in range(n // HCH):
        sl = slice(HCH * seg, HCH * (seg + 1))
        out.append(_mm(tri, hi[sl]) + _mm(tri, mid[sl]) + _mm(tri, lo[sl]))
    return jnp.concatenate(out, axis=0)


def _head_mean(x):
    parts = [jnp.broadcast_to(_mean(x[:, HD * h:HD * (h + 1)]), (x.shape[0], HD)) for h in range(x.shape[1] // HD)]
    return jnp.concatenate(parts, axis=1)


def _seg_sum(x):
    n, c = x.shape
    s = jnp.sum(x.reshape(n // HCH, HCH, c), axis=1, keepdims=True)
    return jnp.broadcast_to(s, (n // HCH, HCH, c)).reshape(n, c)


def _hgrn_gates(fl, lbv, row):
    s = _sigmoid(fl)
    f = lbv + (1.0 - lbv) * s
    a = _segment_scan(jnp.log(f))
    a_mid = _seg_sum(jnp.where(row == HCH // 2 - 1, a, 0.0))
    a_last = _seg_sum(jnp.where(row == HCH - 1, a, 0.0))
    return s, f, a, a_mid, a_last


def _hgrn_fwd(proj, lb_table, norm_g, job=None):
    T = proj.shape[1]
    tb = min(512, T)
    nc = tb // HCH

    def body(q_ref, fl_ref, v_ref, g_ref, lbt_ref, gn_ref, o_ref, ob_ref, stb_ref, st_s, o_s):
        @pl.when(pl.program_id(1) == 0)
        def _():
            st_s[...] = jnp.zeros_like(st_s)

        row = lax.broadcasted_iota(jnp.int32, (tb, HW), 0) & (HCH - 1)
        lbv = _sigmoid(lbt_ref[0:1, :] - lbt_ref[1:2, :])
        _, f, a, a_mid, a_last = _hgrn_gates(fl_ref[...], lbv, row)
        k = 1.0 - f
        qs = q_ref[...] * QSCALE
        q_in = (qs * jnp.exp(a - a_mid)).astype(BF16)
        k_in = (k * jnp.exp(a_mid - a)).astype(BF16)
        q_a = (qs * jnp.exp(a)).astype(BF16)
        k_d = (k * jnp.exp(a_last - a)).astype(BF16)
        dec = jnp.exp(a_last)
        vb = v_ref[...].astype(BF16)
        tri = (lax.broadcasted_iota(jnp.int32, (HCH, HCH), 0)
               >= lax.broadcasted_iota(jnp.int32, (HCH, HCH), 1))
        for c in range(nc):
            sl = slice(HCH * c, HCH * (c + 1))
            for hh in range(HGRN_HB):
                hs = slice(HD * hh, HD * (hh + 1))
                st = st_s[hh]
                stb_ref[hh, c] = st
                sc = jnp.where(tri, _mm_nt(q_in[sl, hs], k_in[sl, hs]), 0.0)
                o_s[sl, hs] = _mm(sc.astype(BF16), vb[sl, hs]) + _mm_nt(q_a[sl, hs], st.astype(BF16))
                d64 = dec[sl, hs]
                st_s[hh] = st * jnp.concatenate([d64, d64], axis=0) + _mm_tn(vb[sl, hs], k_d[sl, hs])
        o = o_s[...]
        r = lax.rsqrt(_head_mean(o * o) + EPS)
        g = g_ref[...]
        o_ref[...] = o
        ob_ref[...] = (o * r * gn_ref[...] * (g * _sigmoid(g))).astype(BF16)

    def col(off):
        return pl.BlockSpec((None, tb, HW), lambda h, cb: (off, cb, h))

    return _call(
        body, name="hgrn_fwd", grid=(NH // HGRN_HB, T // tb), job=job,
        args=(proj, proj, proj, proj, lb_table, norm_g),
        in_specs=[col(2), col(3), col(4), col(5),
                  pl.BlockSpec((2, HW), lambda h, cb: (0, h)), pl.BlockSpec((1, HW), lambda h, cb: (0, h))],
        out_specs=[pl.BlockSpec((tb, HW), lambda h, cb: (cb, h)), pl.BlockSpec((tb, HW), lambda h, cb: (cb, h)),
                   pl.BlockSpec((HGRN_HB, nc, HD, HD), lambda h, cb: (h, cb, 0, 0))],
        out_shape=[SDS((T, D), F32), SDS((T, D), BF16), SDS((NH, T // HCH, HD, HD), F32)],
        scratch_shapes=[pltpu.VMEM((HGRN_HB, HD, HD), F32), pltpu.VMEM((tb, HW), F32)])


def _merge_fwd(x, ab, ob, proj, w_a, w_b, w_out, job=None):
    T = x.shape[0]
    tm = min(512, T)

    def body(x_ref, ab_ref, ob_ref, ga_ref, gb_ref, wa_ref, wb_ref, wo_ref, mg_ref, x1_ref):
        ya = _mm(ab_ref[...], wa_ref[...])
        yb = _mm(ob_ref[...], wb_ref[...])
        merged = (_sigmoid(ga_ref[...]) * ya + _sigmoid(gb_ref[...]) * yb).astype(BF16)
        mg_ref[...] = merged
        x1_ref[...] = x_ref[...] + _mm(merged, wo_ref[...])

    t = lambda i: (i, 0)
    w = lambda i: (0, 0)
    return _call(
        body, name="merge_fwd", grid=(T // tm,), job=job, args=(x, ab, ob, proj, proj, w_a, w_b, w_out),
        in_specs=[pl.BlockSpec((tm, D), t), pl.BlockSpec((tm, D), t), pl.BlockSpec((tm, D), t),
                  pl.BlockSpec((None, tm, D), lambda i: (6, i, 0)), pl.BlockSpec((None, tm, D), lambda i: (7, i, 0)),
                  pl.BlockSpec((D, D), w), pl.BlockSpec((D, D), w), pl.BlockSpec((D, D), w)],
        out_specs=[pl.BlockSpec((tm, D), t)] * 2,
        out_shape=[SDS((T, D), BF16), SDS((T, D), F32)])


def _ffn_fwd_bwd(x1, target, g_ffn, g_fin, w_gu4, w_down):
    T = x1.shape[0]
    tm = min(256, T)
    inv_d = 1.0 / D

    def body(x1_ref, tg_ref, gf_ref, gn_ref, wgu_ref, wd_ref,
             act_ref, dx2b_ref, h2b_ref, dgu_ref, dx1_ref, dx1b_ref, acc_ref):
        @pl.when(pl.program_id(0) == 0)
        def _():
            acc_ref[...] = jnp.zeros_like(acc_ref)

        x1v = x1_ref[...]
        gf = gf_ref[...]
        gn = gn_ref[...]
        rr1 = lax.rsqrt(_mean(x1v * x1v) + EPS)
        x1n = x1v * rr1
        h2b = (x1n * gf).astype(BF16)
        h2b_ref[...] = h2b
        p = [_mm(h2b, wgu_ref[k]) for k in range(NCHIP)]
        sg = [_sigmoid(p[0]), _sigmoid(p[1])]
        si = [p[0] * sg[0], p[1] * sg[1]]
        x2 = x1v
        for k in range(2):
            actk = (si[k] * p[2 + k]).astype(BF16)
            act_ref[:, FFS * k:FFS * (k + 1)] = actk
            x2 = x2 + _mm(actk, wd_ref[FFS * k:FFS * (k + 1), :])
        rr2 = lax.rsqrt(_mean(x2 * x2) + EPS)
        x2n = x2 * rr2
        e = x2n * gn - tg_ref[...]
        acc_ref[0] += _rows8(e * e) * (0.5 * inv_d)
        dy = e * inv_d
        acc_ref[1] += _rows8(dy * x2n)
        dxn = dy * gn
        dx2 = rr2 * (dxn - x2n * _mean(dxn * x2n))
        dx2b = dx2.astype(BF16)
        dx2b_ref[...] = dx2b
        dh2 = None
        for k in range(2):
            dact = _mm_nt(dx2b, wd_ref[FFS * k:FFS * (k + 1), :])
            dgate = (dact * p[2 + k] * (sg[k] * (1.0 + p[k] * (1.0 - sg[k])))).astype(BF16)
            dup = (dact * si[k]).astype(BF16)
            dgu_ref[k] = dgate
            dgu_ref[2 + k] = dup
            part = _mm_nt(dgate, wgu_ref[k]) + _mm_nt(dup, wgu_ref[2 + k])
            dh2 = part if dh2 is None else dh2 + part
        acc_ref[2] += _rows8(dh2 * x1n)
        dxn1 = dh2 * gf
        dx1 = dx2 + rr1 * (dxn1 - x1n * _mean(dxn1 * x1n))
        dx1_ref[...] = dx1
        dx1b_ref[...] = dx1.astype(BF16)

    t = lambda i: (i, 0)
    w = lambda i: (0, 0)
    one = pl.Buffered(1)
    return pl.pallas_call(
        body, name="ffn_fwd_bwd", grid=(T // tm,),
        in_specs=[pl.BlockSpec((tm, D), t), pl.BlockSpec((tm, D), t),
                  pl.BlockSpec((1, D), w), pl.BlockSpec((1, D), w),
                  pl.BlockSpec((NCHIP, D, FFS), lambda i: (0, 0, 0), pipeline_mode=one),
                  pl.BlockSpec((FF, D), w, pipeline_mode=one)],
        out_specs=[pl.BlockSpec((tm, FF), t), pl.BlockSpec((tm, D), t), pl.BlockSpec((tm, D), t),
                   pl.BlockSpec((NCHIP, tm, FFS), lambda i: (0, i, 0)),
                   pl.BlockSpec((tm, D), t), pl.BlockSpec((tm, D), t),
                   pl.BlockSpec((3, 8, D), lambda i: (0, 0, 0))],
        out_shape=[SDS((T, FF), BF16), SDS((T, D), BF16), SDS((T, D), BF16),
                   SDS((NCHIP, T, FFS), BF16), SDS((T, D), F32), SDS((T, D), BF16),
                   SDS((3, 8, D), F32)],
        compiler_params=_cparams(),
    )(x1, target, g_ffn, g_fin, w_gu4, w_down)


def _merge_bwd(dx1b, ab, ob, proj, w_out, w_a, w_b, job=None):
    T = dx1b.shape[0]
    tm = min(512, T)

    def body(dx_ref, ab_ref, ob_ref, ga_ref, gb_ref, wo_ref, wa_ref, wb_ref, dya_ref, dyb_ref, dp_ref):
        dm = _mm_nt(dx_ref[...], wo_ref[...])
        sa = _sigmoid(ga_ref[...])
        sb = _sigmoid(gb_ref[...])
        dya_ref[...] = (dm * sa).astype(BF16)
        dyb_ref[...] = (dm * sb).astype(BF16)
        dp_ref[0] = (dm * _mm(ab_ref[...], wa_ref[...]) * sa * (1.0 - sa)).astype(BF16)
        dp_ref[1] = (dm * _mm(ob_ref[...], wb_ref[...]) * sb * (1.0 - sb)).astype(BF16)

    t = lambda i: (i, 0)
    w = lambda i: (0, 0)
    return _call(
        body, name="merge_bwd", grid=(T // tm,),
        in_specs=[pl.BlockSpec((tm, D), t), pl.BlockSpec((tm, D), t), pl.BlockSpec((tm, D), t),
                  pl.BlockSpec((None, tm, D), lambda i: (6, i, 0)), pl.BlockSpec((None, tm, D), lambda i: (7, i, 0)),
                  pl.BlockSpec((D, D), w), pl.BlockSpec((D, D), w), pl.BlockSpec((D, D), w)],
        out_specs=[pl.BlockSpec((tm, D), t)] * 2 + [pl.BlockSpec((2, tm, D), lambda i: (3, i, 0))],
        out_shape=[SDS((T, D), BF16), SDS((T, D), BF16), SDS((NIN, T, D), BF16)],
        args=(dx1b, ab, ob, proj, proj, w_out, w_a, w_b), job=job)


def _hgrn_bwd(dproj, dyb, w_b, o_raw, proj, st_before, lb_table, norm_g, job=None):
    T = dyb.shape[0]
    tb = min(512, T)
    nc = tb // HCH
    nb = T // tb

    def body(dp_in, dyb_ref, wb_ref, o_ref, q_ref, fl_ref, v_ref, g_ref, stb_ref, lbt_ref, gn_ref,
             dp_ref, acc_ref, dst_s, dqin_s, dqa_s, dkin_s, dkd_s, dv_s, ddec_s):
        del dp_in

        @pl.when(pl.program_id(1) == 0)
        def _():
            dst_s[...] = jnp.zeros_like(dst_s)
            acc_ref[...] = jnp.zeros_like(acc_ref)

        row = lax.broadcasted_iota(jnp.int32, (tb, HW), 0) & (HCH - 1)
        gn = gn_ref[...]
        lbv = _sigmoid(lbt_ref[0:1, :] - lbt_ref[1:2, :])
        o = o_ref[...]
        r = lax.rsqrt(_head_mean(o * o) + EPS)
        on = o * r
        g = g_ref[...]
        sgm = _sigmoid(g)
        dob_v = _mm_nt(dyb_ref[...], wb_ref[...])
        dp_ref[3] = (dob_v * on * gn * (sgm * (1.0 + g * (1.0 - sgm)))).astype(BF16)
        do_n = dob_v * (g * sgm)
        acc_ref[1] += _rows8(do_n * on)
        dxn = do_n * gn
        do = (r * (dxn - on * _head_mean(dxn * on))).astype(BF16)
        s, f, a, a_mid, a_last = _hgrn_gates(fl_ref[...], lbv, row)
        k = 1.0 - f
        qs = q_ref[...] * QSCALE
        e_q = jnp.exp(a - a_mid)
        e_k = jnp.exp(a_mid - a)
        e_a = jnp.exp(a)
        e_l = jnp.exp(a_last - a)
        dec = jnp.exp(a_last)
        q_in = qs * e_q
        k_in = k * e_k
        q_a = qs * e_a
        k_d = k * e_l
        q_inb, k_inb, q_ab, k_db = (z.astype(BF16) for z in (q_in, k_in, q_a, k_d))
        vb = v_ref[...].astype(BF16)
        tri = (lax.broadcasted_iota(jnp.int32, (HCH, HCH), 0)
               >= lax.broadcasted_iota(jnp.int32, (HCH, HCH), 1))
        for c in reversed(range(nc)):
            sl = slice(HCH * c, HCH * (c + 1))
            for hh in range(HGRN_HB):
                hs = slice(HD * hh, HD * (hh + 1))
                stp = stb_ref[hh, c]
                dst = dst_s[hh]
                dstb = dst.astype(BF16)
                do_c = do[sl, hs]
                v_c = vb[sl, hs]
                dqa_s[sl, hs] = _mm(do_c, stp.astype(BF16))
                dkd_s[sl, hs] = _mm(v_c, dstb)
                ddec_s[sl, hs] = jnp.broadcast_to(jnp.sum(dst * stp, axis=0, keepdims=True), (HCH, HD))
                sc = jnp.where(tri, _mm_nt(q_inb[sl, hs], k_inb[sl, hs]), 0.0).astype(BF16)
                dsc = jnp.where(tri, _mm_nt(do_c, v_c), 0.0).astype(BF16)
                dv_s[sl, hs] = _mm_nt(k_db[sl, hs], dstb) + _mm_tn(sc, do_c)
                dqin_s[sl, hs] = _mm(dsc, k_inb[sl, hs])
                dkin_s[sl, hs] = _mm_tn(dsc, q_inb[sl, hs])
                d64 = dec[sl, hs]
                dst_s[hh] = dst * jnp.concatenate([d64, d64], axis=0) + _mm_tn(do_c, q_ab[sl, hs])
        dq_in = dqin_s[...]
        dq_a = dqa_s[...]
        dk_in = dkin_s[...]
        dk_d = dkd_s[...]
        dp_ref[0] = ((dq_in * e_q + dq_a * e_a) * QSCALE).astype(BF16)
        dp_ref[2] = dv_s[...].astype(BF16)
        tq = dq_in * q_in
        tk = dk_in * k_in
        td = dk_d * k_d
        d_a = tq + dq_a * q_a - tk - td
        d_a = d_a + jnp.where(row == HCH // 2 - 1, _seg_sum(tk - tq), 0.0)
        d_a = d_a + jnp.where(row == HCH - 1, _seg_sum(td) + ddec_s[...] * dec, 0.0)
        dlf = _segment_scan(d_a, reverse=True)
        df = dlf / f - (dk_in * e_k + dk_d * e_l)
        dp_ref[1] = (df * (1.0 - lbv) * s * (1.0 - s)).astype(BF16)
        acc_ref[0] += _rows8(df * (1.0 - s))

    def col(off):
        return pl.BlockSpec((None, tb, HW), lambda h, cb: (off, nb - 1 - cb, h))

    hb = lambda h, cb: (nb - 1 - cb, h)
    return _call(
        body, name="hgrn_bwd", grid=(NH // HGRN_HB, nb), job=job,
        args=(dproj, dyb, w_b, o_raw, proj, proj, proj, proj, st_before, lb_table, norm_g),
        in_specs=[ANY, pl.BlockSpec((tb, D), lambda h, cb: (nb - 1 - cb, 0)),
                  pl.BlockSpec((HW, D), lambda h, cb: (h, 0)), pl.BlockSpec((tb, HW), hb),
                  col(2), col(3), col(4), col(5),
                  pl.BlockSpec((HGRN_HB, nc, HD, HD), lambda h, cb: (h, nb - 1 - cb, 0, 0)),
                  pl.BlockSpec((2, HW), lambda h, cb: (0, h)), pl.BlockSpec((1, HW), lambda h, cb: (0, h))],
        out_specs=[pl.BlockSpec((4, tb, HW), lambda h, cb: (0, nb - 1 - cb, h)),
                   pl.BlockSpec((2, 8, HW), lambda h, cb: (0, 0, h))],
        out_shape=[SDS(dproj.shape, BF16), SDS((2, 8, D), F32)],
        scratch_shapes=[pltpu.VMEM((HGRN_HB, HD, HD), F32)] + [pltpu.VMEM((tb, HW), F32)] * 6,
        aliases={0: 0})


def _gmlp_bwd(dproj, dya, w_a, proj, ln_g, ln_b, wm, wm_t, b_t):
    T = dya.shape[0]
    tm = min(256, T)

    def body(dp_in, dya_ref, wa_ref, u_ref, v_ref, lg_ref, lb_ref, wm_ref, wmt_ref, bt_ref,
             dp_ref, acc_ref, dws_ref, dmix_ref, du_s, dvn_s):
        del dp_in

        @pl.when(pl.program_id(0) == 0)
        def _():
            acc_ref[...] = jnp.zeros_like(acc_ref)
            dws_ref[...] = jnp.zeros_like(dws_ref)
            dmix_ref[...] = jnp.zeros_like(dmix_ref)

        u = u_ref[...]
        v = v_ref[...]
        lg = lg_ref[...]
        gu, t_u = _gelu(u)
        gv, t_v = _gelu(v)
        vhat, rs = _layer_norm_stats(gv)
        vnb = (vhat * lg + lb_ref[...]).astype(BF16)
        da_v = _mm_nt(dya_ref[...], wa_ref[...])
        for ch in range(tm // GCH):
            rows = slice(GCH * ch, GCH * (ch + 1))
            for g in range(NG):
                cols = slice(128 * g, 128 * (g + 1))
                vng = vnb[rows, cols]
                mixed = _mm(wm_ref[g], vng) + bt_ref[:, g:g + 1]
                dag = da_v[rows, cols]
                dmx = dag * gu[rows, cols]
                du_s[rows, cols] = dag * mixed
                dmxb = dmx.astype(BF16)
                dws_ref[:, cols] += _mm_nt(dmxb, vng)
                dmix_ref[:, cols] += dmx
                dvn_s[rows, cols] = _mm(wmt_ref[g], dmxb)
        dp_ref[0] = (du_s[...] * _gelu_grad(u, t_u)).astype(BF16)
        dvn = dvn_s[...]
        acc_ref[0] += _rows8(dvn * vhat)
        acc_ref[1] += _rows8(dvn)
        dvh = dvn * lg
        dgv = rs * (dvh - _mean(dvh) - vhat * _mean(dvh * vhat))
        dp_ref[1] = (dgv * _gelu_grad(v, t_v)).astype(BF16)

    row = lambda i: (0, 0)
    w3 = lambda i: (0, 0, 0)
    return pl.pallas_call(
        body, name="gmlp_bwd", grid=(T // tm,),
        in_specs=[ANY, pl.BlockSpec((tm, D), lambda i: (i, 0)), pl.BlockSpec((D, D), row),
                  pl.BlockSpec((None, tm, D), lambda i: (0, i, 0)), pl.BlockSpec((None, tm, D), lambda i: (1, i, 0)),
                  pl.BlockSpec((1, D), row), pl.BlockSpec((1, D), row),
                  pl.BlockSpec((NG, GCH, GCH), w3), pl.BlockSpec((NG, GCH, GCH), w3),
                  pl.BlockSpec((GCH, NG), row)],
        out_specs=[pl.BlockSpec((2, tm, D), lambda i: (2, i, 0)),
                   pl.BlockSpec((2, 8, D), w3), pl.BlockSpec((GCH, D), row), pl.BlockSpec((GCH, D), row)],
        out_shape=[SDS(dproj.shape, BF16), SDS((2, 8, D), F32), SDS((GCH, D), F32), SDS((GCH, D), F32)],
        scratch_shapes=[pltpu.VMEM((tm, D), F32), pltpu.VMEM((tm, D), F32)],
        input_output_aliases={0: 0},
        compiler_params=_cparams(),
    )(dproj, dya, w_a, proj, proj, ln_g, ln_b, wm, wm_t, b_t)


def _proj_bwd(dproj, w_in4, x, dx1, g_mix, job=None):
    T = x.shape[0]
    tm = min(256, T)
    order = (2, 3, 4, 5, 0, 1, 6, 7)

    def body(dp_ref, w_ref, x_ref, dx1_ref, g_ref, gx_ref, acc_ref):
        @pl.when(pl.program_id(0) == 0)
        def _():
            acc_ref[...] = jnp.zeros_like(acc_ref)

        dh = None
        for m, og in enumerate(order):
            part = _mm_nt(dp_ref[m], w_ref[og // 2, :, D * (og % 2):D * (og % 2 + 1)])
            dh = part if dh is None else dh + part
        xv = x_ref[...]
        r = lax.rsqrt(_mean(xv * xv) + EPS)
        xn = xv * r
        acc_ref[...] += _rows8(dh * xn)
        dxn = dh * g_ref[...]
        gx_ref[...] = dx1_ref[...] + r * (dxn - xn * _mean(dxn * xn))

    t = lambda i: (i, 0)
    return _call(
        body, name="proj_bwd", grid=(T // tm,),
        in_specs=[pl.BlockSpec((NIN, tm, D), lambda i: (0, i, 0)),
                  pl.BlockSpec((NCHIP, D, 2 * D), lambda i: (0, 0, 0), pipeline_mode=pl.Buffered(1)),
                  pl.BlockSpec((tm, D), t), pl.BlockSpec((tm, D), t), pl.BlockSpec((1, D), lambda i: (0, 0))],
        out_specs=[pl.BlockSpec((tm, D), t), pl.BlockSpec((8, D), lambda i: (0, 0))],
        out_shape=[SDS((T, D), F32), SDS((8, D), F32)],
        args=(dproj, w_in4, x, dx1, g_mix), job=job)


def _dw_call(name, a, b, a_spec, b_spec, o_spec, out_shape, nblk, tt, job=None, prefetch=None):
    T = a.shape[-2]

    def body(*refs):
        a_ref, b_ref, o_ref = refs[-3:]

        @pl.when(pl.program_id(1) == 0)
        def _():
            o_ref[...] = jnp.zeros_like(o_ref)
        o_ref[...] += _mm_tn(a_ref[...], b_ref[...])

    (out,), job_out = _call(
        body, name=name, grid=(nblk, T // tt), in_specs=[a_spec, b_spec], out_specs=[o_spec],
        out_shape=[out_shape], args=(a, b), job=job, prefetch=prefetch)
    return out, job_out


def _dw_in_half(name, place, hb, dproj, mine, job=None):
    tt = min(DW_TOKENS, hb.shape[0])

    def comp(k, pc):
        return _component_of(2 * k + (pc[1] if mine else 1 - pc[1]))

    return _dw_call(
        name, hb, dproj,
        pl.BlockSpec((tt, D), lambda k, t, pc: (t, 0)),
        pl.BlockSpec((None, tt, D), lambda k, t, pc: (comp(k, pc), t, 0)),
        pl.BlockSpec((None, D, D), lambda k, t, pc: (k, 0, 0)),
        SDS((NCHIP, D, D), F32), NCHIP, tt, job, place)


def _dw_gate_up(h2b, dgu4, job=None):
    tt = min(DW_TOKENS, h2b.shape[0])
    return _dw_call(
        "dw_gate_up", h2b, dgu4,
        pl.BlockSpec((tt, D), lambda k, t: (t, 0)),
        pl.BlockSpec((None, tt, FFS), lambda k, t: (k, t, 0)),
        pl.BlockSpec((None, D, FFS), lambda k, t: (k, 0, 0)),
        SDS((NCHIP, D, FFS), F32), NCHIP, tt, job)


def _dw_down(act, dx2b, job=None):
    tt = min(DW_TOKENS, act.shape[0])
    g, job_out = _dw_call(
        "dw_down", act, dx2b,
        pl.BlockSpec((tt, FFS), lambda k, t: (t, k)),
        pl.BlockSpec((tt, D), lambda k, t: (t, 0)),
        pl.BlockSpec((FFS, D), lambda k, t: (k, 0)),
        SDS((FF, D), F32), 2, tt, job)
    return g.reshape(NCHIP, FF // NCHIP, D), job_out


def _dw_square(name, a, b, job=None):
    tt = min(DW_TOKENS, a.shape[0])
    g, job_out = _dw_call(
        name, a, b,
        pl.BlockSpec((tt, D), lambda k, t: (t, 0)), pl.BlockSpec((tt, D), lambda k, t: (t, 0)),
        pl.BlockSpec((D, D), lambda k, t: (0, 0)), SDS((D, D), F32), 1, tt, job)
    return g.reshape(NCHIP, D // NCHIP, D), job_out


def _place():
    x, y, c = lax.axis_index("x"), lax.axis_index("y"), lax.axis_index("c")
    return x, y, c, 2 * x + y


def _chip_at(x, y, s):
    return x ^ (s >> 1), y ^ (s & 1)


class _Job:
    def __init__(self, ins, out_shapes, sems, start, finish, aliases=None, mid=None):
        self.ins, self.out_shapes, self.sems = list(ins), list(out_shapes), list(sems)
        self.start, self.finish, self.aliases = start, finish, dict(aliases or {})
        self.mid = mid if mid is not None else (lambda ins, outs, sems: None)


def _join_jobs(*jobs):
    def cut(refs, sizes):
        out, at = [], 0
        for n in sizes:
            out.append(refs[at:at + n])
            at += n
        return out

    ni = [len(j.ins) for j in jobs]
    no = [len(j.out_shapes) for j in jobs]
    ns = [len(j.sems) for j in jobs]

    def run(which):
        def go(ins, outs, sems):
            for j, a, b, c in zip(jobs, cut(ins, ni), cut(outs, no), cut(sems, ns)):
                getattr(j, which)(a, b, c)
        return go

    aliases = {}
    for k, j in enumerate(jobs):
        for a, b in j.aliases.items():
            aliases[sum(ni[:k]) + a] = sum(no[:k]) + b
    return _Job([a for j in jobs for a in j.ins], [o for j in jobs for o in j.out_shapes],
                [s for j in jobs for s in j.sems], run("start"), run("finish"), aliases, run("mid"))


def _call(body, *, name, grid, in_specs, out_specs, out_shape, args, scratch_shapes=(), aliases=None,
          job=None, prefetch=None):
    n_in, n_out, n_scr = len(in_specs), len(out_specs), len(scratch_shapes)
    npf = 0 if prefetch is None else 1
    job = job if job is not None else _Job([], [], [], lambda *a: None, lambda *a: None)
    ji, jo = len(job.ins), len(job.out_shapes)
    steps = math.prod(grid)

    def wrapped(*refs):
        pf, refs = refs[:npf], refs[npf:]
        ins, jin = refs[:n_in], refs[n_in:n_in + ji]
        o0 = n_in + ji
        outs, jout = refs[o0:o0 + n_out], refs[o0 + n_out:o0 + n_out + jo]
        s0 = o0 + n_out + jo
        scr, jsem = refs[s0:s0 + n_scr], refs[s0 + n_scr:]
        step = functools.reduce(lambda acc, ag: acc * ag[1] + pl.program_id(ag[0]), enumerate(grid), 0)
        if ji or jo:
            @pl.when(step == 0)
            def _():
                job.start(jin, jout, jsem)

        body(*pf, *ins, *outs, *scr)

        if ji or jo:
            @pl.when(step == steps // 2)
            def _():
                job.mid(jin, jout, jsem)

            @pl.when(step == steps - 1)
            def _():
                job.finish(jin, jout, jsem)

    io = {npf + a: b for a, b in dict(aliases or {}).items()}
    io.update({npf + n_in + a: n_out + b for a, b in job.aliases.items()})
    kw = dict(in_specs=list(in_specs) + [ANY] * ji, out_specs=list(out_specs) + [ANY] * jo,
              scratch_shapes=list(scratch_shapes) + job.sems)
    if npf:
        kw = dict(grid_spec=pltpu.PrefetchScalarGridSpec(num_scalar_prefetch=1, grid=grid, **kw))
    else:
        kw["grid"] = grid
    res = pl.pallas_call(
        wrapped, name=name, out_shape=list(out_shape) + job.out_shapes, input_output_aliases=io,
        compiler_params=_cparams(has_side_effects=bool(ji or jo)), **kw,
    )(*(() if prefetch is None else (prefetch,)), *args, *job.ins)
    return list(res[:n_out]), list(res[n_out:])


def _run_job(job, name):
    ji, jo = len(job.ins), len(job.out_shapes)

    def body(*refs):
        jin, jout, jsem = refs[:ji], refs[ji:ji + jo], refs[ji + jo:]
        job.start(jin, jout, jsem)
        job.finish(jin, jout, jsem)

    return list(pl.pallas_call(
        body, name=name, in_specs=[ANY] * ji, out_specs=[ANY] * jo, out_shape=job.out_shapes,
        scratch_shapes=job.sems, input_output_aliases=job.aliases,
        compiler_params=pltpu.CompilerParams(has_side_effects=True))(*job.ins))


def _cast_shard(name, place, w):
    rows, cols = w.shape
    tr = 352 if rows % 352 == 0 else 256

    def body(pc_ref, w_ref, o_ref):
        del pc_ref
        o_ref[...] = w_ref[...].astype(BF16)

    return pl.pallas_call(
        body, name=name,
        grid_spec=pltpu.PrefetchScalarGridSpec(
            num_scalar_prefetch=1, grid=(rows // tr,),
            in_specs=[pl.BlockSpec((tr, cols), lambda i, pc: (i, 0))],
            out_specs=pl.BlockSpec((None, tr, cols), lambda i, pc: (pc[0], i, 0))),
        out_shape=SDS((NCHIP, rows, cols), BF16),
        compiler_params=_cparams(),
    )(place, w)


def _sibling_copy(ref, send_sem, recv_sem):
    x, y, c, _ = _place()
    return pltpu.make_async_remote_copy(src_ref=ref, dst_ref=ref, send_sem=send_sem, recv_sem=recv_sem,
                                        device_id=(x, y, 1 - c), device_id_type=MESH)


def _half_rows(arr, slot, core):
    half = arr.shape[1] // 2
    return arr.at[slot, pl.ds(pl.multiple_of(core * half, 16), half)]


def _quarter_rows(arr, slot, core, q):
    quarter = arr.shape[1] // 4
    return arr.at[slot, pl.ds(pl.multiple_of((2 * core + q) * quarter, 16), quarter)]


def _chip_copy(ref, dist, send_sem, recv_sem):
    x, y, c, _ = _place()
    cx, cy = _chip_at(x, y, dist)
    return pltpu.make_async_remote_copy(src_ref=ref, dst_ref=ref, send_sem=send_sem, recv_sem=recv_sem,
                                        device_id=(cx, cy, c), device_id_type=MESH)


def _gather_sems(n):
    dma = pltpu.SemaphoreType.DMA
    return [dma((n, 2))] * 4 + [dma((n, 4))] * 2


def _gather_start(arrs, sems):
    dsend, drecv = sems[0], sems[1]
    _, _, c, j = _place()
    for w, arr in enumerate(arrs):
        for dist in (1, 2):
            _chip_copy(_half_rows(arr, j, c), dist, dsend.at[w, dist - 1], drecv.at[w, dist - 1]).start()


def _gather_land(arrs, sems, dist, first=0):
    dsend, drecv, rsend, rrecv, fsend, frecv = sems
    _, _, c, j = _place()
    if dist < 3:
        other = 3 - dist
        for w, arr in enumerate(arrs, first):
            landed = _half_rows(arr, j ^ dist, c)
            _chip_copy(landed, dist, dsend.at[w, dist - 1], drecv.at[w, dist - 1]).wait_recv()
            relay = _quarter_rows(arr, j ^ dist, c, other - 1)
            _chip_copy(relay, other, rsend.at[w, other - 1], rrecv.at[w, other - 1]).start()
            _sibling_copy(landed, fsend.at[w, dist - 1], frecv.at[w, dist - 1]).start()
        for w, arr in enumerate(arrs, first):
            theirs = _half_rows(arr, j ^ dist, 1 - c)
            _sibling_copy(theirs, fsend.at[w, dist - 1], frecv.at[w, dist - 1]).wait_recv()
    else:
        for w, arr in enumerate(arrs, first):
            for via in (1, 2):
                piece = _quarter_rows(arr, j ^ 3, c, via - 1)
                _chip_copy(piece, via, rsend.at[w, via - 1], rrecv.at[w, via - 1]).wait_recv()
                _sibling_copy(piece, fsend.at[w, 1 + via], frecv.at[w, 1 + via]).start()
        for w, arr in enumerate(arrs, first):
            for via in (1, 2):
                theirs = _quarter_rows(arr, j ^ 3, 1 - c, via - 1)
                _sibling_copy(theirs, fsend.at[w, 1 + via], frecv.at[w, 1 + via]).wait_recv()


def _gather_drain(arrs, sems):
    dsend, drecv, rsend, rrecv, fsend, frecv = sems
    _, _, c, j = _place()
    for w, arr in enumerate(arrs):
        for dist in (1, 2):
            other = 3 - dist
            _chip_copy(_half_rows(arr, j, c), dist, dsend.at[w, dist - 1], drecv.at[w, dist - 1]).wait_send()
            _chip_copy(_quarter_rows(arr, j ^ dist, c, other - 1), other,
                       rsend.at[w, other - 1], rrecv.at[w, other - 1]).wait_send()
            _sibling_copy(_half_rows(arr, j ^ dist, c), fsend.at[w, dist - 1], frecv.at[w, dist - 1]).wait_send()
            _sibling_copy(_quarter_rows(arr, j ^ 3, c, dist - 1),
                          fsend.at[w, 1 + dist], frecv.at[w, 1 + dist]).wait_send()


def _gather_neighbours(arrs, sems):
    _gather_land(arrs, sems, 1)
    _gather_land(arrs, sems, 2)


def _gather_finish(arrs, sems):
    _gather_land(arrs, sems, 3)
    _gather_drain(arrs, sems)


def _gather_job(arrs):
    n = len(arrs)
    return _Job(arrs, [SDS(a.shape, a.dtype) for a in arrs], _gather_sems(n),
                lambda ins, outs, sems: _gather_start(outs, sems),
                lambda ins, outs, sems: _gather_finish(outs, sems), {k: k for k in range(n)},
                mid=lambda ins, outs, sems: _gather_neighbours(outs, sems))


def _exchange_job(arrs, out_shapes, n, copies):
    def start(ins, outs, sems):
        for cp in copies(ins, outs, sems[0], sems[1]):
            cp.start()

    def finish(ins, outs, sems):
        for cp in copies(ins, outs, sems[0], sems[1]):
            cp.wait()

    return _Job(arrs, out_shapes, [pltpu.SemaphoreType.DMA((n,))] * 2, start, finish)


def _pair_exchange_job(grads):
    def copies(ins, outs, send_sem, recv_sem):
        x, y, c, _ = _place()
        res = []
        for w in range(len(grads)):
            half = ins[w].shape[1] // 2
            theirs = pl.ds(pl.multiple_of((1 - c) * half, 8), half)
            res.append(pltpu.make_async_remote_copy(
                src_ref=ins[w].at[:, theirs, :], dst_ref=outs[w], send_sem=send_sem.at[w],
                recv_sem=recv_sem.at[w], device_id=(x, y, 1 - c), device_id_type=MESH))
        return res

    return _exchange_job(grads, [SDS((NCHIP, g.shape[1] // 2, g.shape[2]), F32) for g in grads],
                         len(grads), copies)


def _row_tile(rows, cols):
    tr = rows
    while tr * cols * 4 > ELEMENTWISE_BLOCK_BYTES and tr % 32 == 0:
        tr //= 2
    return tr


def _pair_sum(name, place, g, sib):
    half, cols = sib.shape[1], sib.shape[2]
    tr = _row_tile(half, cols)
    nt = half // tr
    mine = nt if g.shape[1] == 2 * half else 0

    def body(pc_ref, g_ref, s_ref, own_ref, out_ref):
        del pc_ref
        v = g_ref[...] + s_ref[...]
        out_ref[...] = v.astype(BF16)

        @pl.when(pl.program_id(1) == 0)
        def _():
            own_ref[...] = v

    return pl.pallas_call(
        body, name=name,
        grid_spec=pltpu.PrefetchScalarGridSpec(
            num_scalar_prefetch=1, grid=(nt, NCHIP),
            in_specs=[pl.BlockSpec((None, tr, cols), lambda i, s, pc: (pc[0] ^ s, pc[1] * mine + i, 0)),
                      pl.BlockSpec((None, tr, cols), lambda i, s, pc: (pc[0] ^ s, i, 0))],
            out_specs=[pl.BlockSpec((tr, cols), lambda i, s, pc: (i, 0)),
                       pl.BlockSpec((None, tr, cols), lambda i, s, pc: (s, i, 0))]),
        out_shape=[SDS((half, cols), F32), SDS((NCHIP, half, cols), BF16)],
        compiler_params=_cparams(),
    )(place, g, sib)


def _chip_exchange_job(parts):
    def copies(ins, outs, send_sem, recv_sem):
        x, y, c, _ = _place()
        res = []
        for w in range(len(parts)):
            for s in range(1, NCHIP):
                cx, cy = _chip_at(x, y, s)
                k = w * (NCHIP - 1) + s - 1
                res.append(pltpu.make_async_remote_copy(
                    src_ref=ins[w].at[s], dst_ref=outs[w].at[s - 1], send_sem=send_sem.at[k],
                    recv_sem=recv_sem.at[k], device_id=(cx, cy, c), device_id_type=MESH))
        return res

    return _exchange_job(parts, [SDS((NCHIP - 1,) + p.shape[1:], BF16) for p in parts],
                         len(parts) * (NCHIP - 1), copies)


def _chip_sum(name, own, rem):
    half, cols = own.shape
    tr = _row_tile(half, cols)

    def body(own_ref, rem_ref, out_ref):
        out_ref[...] = ((own_ref[...] + rem_ref[0].astype(F32)) + rem_ref[1].astype(F32)) + rem_ref[2].astype(F32)

    return pl.pallas_call(
        body, name=name, grid=(half // tr,),
        in_specs=[pl.BlockSpec((tr, cols), lambda i: (i, 0)),
                  pl.BlockSpec((NCHIP - 1, tr, cols), lambda i: (0, i, 0))],
        out_specs=pl.BlockSpec((tr, cols), lambda i: (i, 0)),
        out_shape=SDS((half, cols), F32),
        compiler_params=_cparams(),
    )(own, rem)


def _share_halves_job(halves):
    def copies(ins, outs, send_sem, recv_sem):
        x, y, c, _ = _place()
        return [pltpu.make_async_remote_copy(
            src_ref=ins[w], dst_ref=outs[w], send_sem=send_sem.at[w], recv_sem=recv_sem.at[w],
            device_id=(x, y, 1 - c), device_id_type=MESH) for w in range(len(halves))]

    return _exchange_job(halves, [SDS(h.shape, F32) for h in halves], len(halves), copies)


def _adamw_math(w, g, m, v):
    m = B1 * m + (1.0 - B1) * g
    v = B2 * v + (1.0 - B2) * (g * g)
    m_hat = m / (1.0 - B1 ** STEP)
    v_hat = v / (1.0 - B2 ** STEP)
    delta = -LR * (m_hat / (jnp.sqrt(v_hat) + AEPS) + WD * w)
    return delta, m, v


def _adamw(name, place, w, own, sib, m, v):
    rows, cols = w.shape
    by_cols = own.shape[0] == rows
    half, pc_cols = (rows, cols // 2) if by_cols else (rows // 2, cols)
    tr = _row_tile(half, pc_cols)
    nt = half // tr

    def body(pc_ref, w_ref, own_ref, sib_ref, m_ref, v_ref, g_ref, d_ref, mo_ref, vo_ref):
        g = jnp.where(pl.program_id(0) == pc_ref[1], own_ref[...], sib_ref[...])
        d, mn, vn = _adamw_math(w_ref[...], g, m_ref[...], v_ref[...])
        g_ref[...] = g
        d_ref[...] = d
        mo_ref[...] = mn
        vo_ref[...] = vn

    full = pl.BlockSpec((tr, pc_cols), (lambda h, i, pc: (i, h)) if by_cols else (lambda h, i, pc: (h * nt + i, 0)))
    part = pl.BlockSpec((tr, pc_cols), lambda h, i, pc: (i, 0))
    return pl.pallas_call(
        body, name=name,
        grid_spec=pltpu.PrefetchScalarGridSpec(
            num_scalar_prefetch=1, grid=(2, nt),
            in_specs=[full, part, part, full, full], out_specs=[full] * 4),
        out_shape=[SDS((rows, cols), F32)] * 4,
        compiler_params=_cparams(),
    )(place, w, own, sib, m, v)


def _small_allreduce_adamw(sp, w, m, v):
    shape = sp.shape

    def body(sp_ref, w_ref, m_ref, v_ref, g_ref, d_ref, mo_ref, vo_ref,
             sib_s, pair_s, chip_s, send_sem, recv_sem):
        x, y, c, j = _place()
        cp = pltpu.make_async_remote_copy(
            src_ref=sp_ref, dst_ref=sib_s, send_sem=send_sem.at[0], recv_sem=recv_sem.at[0],
            device_id=(x, y, 1 - c), device_id_type=MESH)
        cp.start()
        cp.wait()
        pair_s[...] = sp_ref[...] + sib_s[...]
        half = shape[0] // 2
        mine = pl.ds(pl.multiple_of(c * half, 8), half)
        cps = []
        for s in range(1, NCHIP):
            cx, cy = _chip_at(x, y, s)
            cp = pltpu.make_async_remote_copy(
                src_ref=pair_s.at[mine], dst_ref=chip_s.at[s, mine], send_sem=send_sem.at[s],
                recv_sem=recv_sem.at[s], device_id=(cx, cy, c), device_id_type=MESH)
            cp.start()
            cps.append(cp)
        chip_s[0] = pair_s[...]
        for cp in cps:
            cp.wait()
        cps = []
        for s in range(1, NCHIP):
            cp = pltpu.make_async_remote_copy(
                src_ref=chip_s.at[s, mine], dst_ref=chip_s.at[s, mine], send_sem=send_sem.at[NCHIP + s],
                recv_sem=recv_sem.at[NCHIP + s], device_id=(x, y, 1 - c), device_id_type=MESH)
            cp.start()
            cps.append(cp)
        for cp in cps:
            cp.wait()
        tot = chip_s[j]
        for k in range(1, NCHIP):
            tot = tot + chip_s[k ^ j]
        g_ref[...] = tot
        d, mn, vn = _adamw_math(w_ref[...], tot, m_ref[...], v_ref[...])
        d_ref[...] = d
        mo_ref[...] = mn
        vo_ref[...] = vn

    vm = pl.BlockSpec(memory_space=pltpu.VMEM)
    return pl.pallas_call(
        body, name="small_allreduce_adamw",
        in_specs=[vm] * 4, out_specs=[vm] * 4, out_shape=[SDS(shape, F32)] * 4,
        scratch_shapes=[pltpu.VMEM(shape, F32), pltpu.VMEM(shape, F32), pltpu.VMEM((NCHIP,) + shape, F32),
                        pltpu.SemaphoreType.DMA((2 * NCHIP,)), pltpu.SemaphoreType.DMA((2 * NCHIP,))],
        compiler_params=pltpu.CompilerParams(has_side_effects=True),
    )(sp, w, m, v)


def _pack_small(first, mix, ln_g, ln_b, b_s, lbt, hn, ffn, fin, w_s):
    rows = [first.reshape(1, D), mix.reshape(1, D), ln_g.reshape(1, D), ln_b.reshape(1, D),
            b_s.reshape(1, D), lbt.reshape(2, D), hn.reshape(1, D), ffn.reshape(1, D), fin.reshape(1, D),
            jnp.zeros((6, D), F32)]
    return jnp.concatenate(rows + [w_s.reshape(NG, GCH, GCH).transpose(1, 0, 2).reshape(GCH, D)], axis=0)


def _unpack_small(p):
    w_s = p[16:].reshape(GCH, NG, GCH).transpose(1, 0, 2).reshape(1, NG, GCH, GCH)
    return dict(norm_mix_g=p[1:2], gmlp_ln_g=p[2:3], gmlp_ln_b=p[3:4], gmlp_b_s=p[4].reshape(1, NG, GCH),
                hgrn_lb_table=p[5:7], hgrn_norm_g=p[7:8], norm_ffn_g=p[8:9], norm_final_g=p[9],
                gmlp_w_s=w_s)


SMALL = ("norm_mix_g", "gmlp_ln_g", "gmlp_ln_b", "gmlp_w_s", "gmlp_b_s", "hgrn_lb_table", "hgrn_norm_g",
         "norm_ffn_g", "norm_final_g")
BIG = ("w_in", "w_gate_up", "w_branch_a", "w_branch_b", "w_out", "w_down")
ORDER = ("norm_mix_g", "w_in", "gmlp_ln_g", "gmlp_ln_b", "gmlp_w_s", "gmlp_b_s", "hgrn_lb_table",
         "hgrn_norm_g", "w_branch_a", "w_branch_b", "w_out", "norm_ffn_g", "w_gate_up", "w_down",
         "norm_final_g")


def kernel(x, norm_mix_g, w_in, gmlp_ln_g, gmlp_ln_b, gmlp_w_s, gmlp_b_s, hgrn_lb_table, hgrn_norm_g, w_branch_a, w_branch_b, w_out, norm_ffn_g, w_gate_up, w_down, norm_final_g, loss_target, m_norm_mix_g, m_w_in, m_gmlp_ln_g, m_gmlp_ln_b, m_gmlp_w_s, m_gmlp_b_s, m_hgrn_lb_table, m_hgrn_norm_g, m_w_branch_a, m_w_branch_b, m_w_out, m_norm_ffn_g, m_w_gate_up, m_w_down, m_norm_final_g, v_norm_mix_g, v_w_in, v_gmlp_ln_g, v_gmlp_ln_b, v_gmlp_w_s, v_gmlp_b_s, v_hgrn_lb_table, v_hgrn_norm_g, v_w_branch_a, v_w_branch_b, v_w_out, v_norm_ffn_g, v_w_gate_up, v_w_down, v_norm_final_g):
    args = dict(locals())
    T = x.shape[1]
    xs = x.reshape(T, D)
    target = loss_target.reshape(T, D)
    big = {n: args[n].reshape(args[n].shape[1:]) for n in BIG}
    big_m = {n: args["m_" + n].reshape(args[n].shape[1:]) for n in BIG}
    big_v = {n: args["v_" + n].reshape(args[n].shape[1:]) for n in BIG}

    x_i, y_i, c_i = lax.axis_index("x"), lax.axis_index("y"), lax.axis_index("c")
    place = jnp.stack([2 * x_i + y_i, c_i]).astype(jnp.int32)
    cast = {n: _cast_shard("cast_" + n, place, big[n]) for n in BIG}
    tril = jnp.tril(jnp.ones((GCH, GCH), bool))
    wm = jnp.where(tril, gmlp_w_s[0], 0.0).astype(BF16)
    wm_t = jnp.swapaxes(wm, 1, 2)
    b_t = gmlp_b_s[0].T

    (proj, hb), w_in4, (w_a4, w_b4, w_out4, w_down4) = _proj_fwd(
        place, xs, norm_mix_g, cast["w_in"], [cast[n] for n in ("w_branch_a", "w_branch_b", "w_out", "w_down")])
    (ab,), _ = _gmlp_fwd(proj, gmlp_ln_g, gmlp_ln_b, wm, b_t)
    (o_raw, obb, st_before), (w_gu4,) = _hgrn_fwd(
        proj, hgrn_lb_table, hgrn_norm_g, job=_gather_job([cast["w_gate_up"]]))
    w_a, w_b, w_o = (w.reshape(D, D) for w in (w_a4, w_b4, w_out4))
    (mgb, x1), _ = _merge_fwd(xs, ab, obb, proj, w_a, w_b, w_o)
    w_dn = w_down4.reshape(FF, D)
    act, dx2b, h2b, dgu4, dx1, dx1b, acc_ffn = _ffn_fwd_bwd(
        x1, target, norm_ffn_g, norm_final_g.reshape(1, D), w_gu4, w_dn)

    grads, owns, parts, halves, sibh = {}, {}, {}, {}, {}

    def pair_sums(names, sibs):
        for n, s in zip(names, sibs):
            owns[n], parts[n] = _pair_sum("rs_pair_sum_" + n, place, grads[n], s)

    def chip_sums(names, got):
        for n, r in zip(names, got):
            halves[n] = _chip_sum("rs_chip_sum_" + n, owns[n], r)

    ffn, mix = ("w_gate_up", "w_down"), ("w_branch_a", "w_branch_b", "w_out")
    grads["w_gate_up"], _ = _dw_gate_up(h2b, dgu4)
    grads["w_down"], _ = _dw_down(act, dx2b)
    (dya, dyb, dproj), got = _merge_bwd(
        dx1b, ab, obb, proj, w_o, w_a, w_b, job=_pair_exchange_job([grads[n] for n in ffn]))
    pair_sums(ffn, got)
    grads["w_branch_a"], _ = _dw_square("dw_branch_a", ab, dya)
    grads["w_branch_b"], _ = _dw_square("dw_branch_b", obb, dyb)
    grads["w_out"], _ = _dw_square("dw_out", mgb, dx1b)
    (dproj, acc_hgrn), got = _hgrn_bwd(
        dproj, dyb, w_b, o_raw, proj, st_before, hgrn_lb_table, hgrn_norm_g,
        job=_join_jobs(_chip_exchange_job([parts[n] for n in ffn]), _pair_exchange_job([grads[n] for n in mix])))
    chip_sums(ffn, got[:2])
    pair_sums(mix, got[2:])
    dproj, acc_ln, dws, dmix = _gmlp_bwd(dproj, dya, w_a, proj, gmlp_ln_g, gmlp_ln_b, wm, wm_t, b_t)
    for_sibling, got = _dw_in_half(
        "dw_in_sibling_half", place, hb, dproj, False,
        job=_join_jobs(_share_halves_job([halves[n] for n in ffn]), _chip_exchange_job([parts[n] for n in mix])))
    sibh.update(zip(ffn, got[:2]))
    chip_sums(mix, got[2:])
    grads["w_in"], got = _dw_in_half(
        "dw_in_own_half", place, hb, dproj, True, job=_share_halves_job([for_sibling]))
    pair_sums(("w_in",), got)
    (grad_x, acc_mix), got = _proj_bwd(
        dproj, w_in4, xs, dx1, norm_mix_g,
        job=_join_jobs(_chip_exchange_job([parts["w_in"]]), _share_halves_job([halves[n] for n in mix])))
    chip_sums(("w_in",), got[:1])
    sibh.update(zip(mix, got[1:]))
    (sibh["w_in"],) = _run_job(_share_halves_job([halves["w_in"]]), "rs_share_halves_w_in")
    out = {}
    for n in BIG:
        g, d, mn, vn = _adamw("adamw_" + n, place, big[n], halves[n], sibh[n], big_m[n], big_v[n])
        shp = args[n].shape
        out[n] = (g.reshape(shp), d.reshape(shp), mn.reshape(shp), vn.reshape(shp))

    lbv = jax.nn.sigmoid(hgrn_lb_table[0] - hgrn_lb_table[1])
    d_t0 = jnp.sum(acc_hgrn[0], axis=0) * lbv * (1.0 - lbv)
    loss_row = jnp.zeros((D,), F32).at[0].set(jnp.sum(acc_ffn[0]))
    dws_m = jnp.where(tril[:, None, :], dws.reshape(GCH, NG, GCH), 0.0).transpose(1, 0, 2)
    db_s = jnp.sum(dmix.reshape(GCH, NG, GCH), axis=-1).T
    sp = _pack_small(loss_row, jnp.sum(acc_mix, 0), jnp.sum(acc_ln[0], 0), jnp.sum(acc_ln[1], 0), db_s,
                     jnp.stack([d_t0, -d_t0]), jnp.sum(acc_hgrn[1], 0), jnp.sum(acc_ffn[2], 0),
                     jnp.sum(acc_ffn[1], 0), dws_m)
    zero = jnp.zeros((D,), F32)

    def pack(prefix):
        a = lambda n: args[prefix + n]
        return _pack_small(zero, a("norm_mix_g"), a("gmlp_ln_g"), a("gmlp_ln_b"), a("gmlp_b_s"),
                           a("hgrn_lb_table"), a("hgrn_norm_g"), a("norm_ffn_g"), a("norm_final_g"),
                           a("gmlp_w_s"))

    packed = _small_allreduce_adamw(sp, pack(""), pack("m_"), pack("v_"))
    loss = packed[0][0, 0]
    small = [_unpack_small(p) for p in packed]
    for n in SMALL:
        out[n] = tuple(s[n] for s in small)
    return (loss, grad_x.reshape(x.shape), *[out[n][0] for n in ORDER], *[out[n][1] for n in ORDER],
            *[out[n][2] for n in ORDER], *[out[n][3] for n in ORDER])
```

```python
import functools
import math

import jax
import jax.numpy as jnp
from jax import lax
from jax.experimental import pallas as pl
from jax.experimental.pallas import tpu as pltpu

F32 = jnp.float32
BF16 = jnp.bfloat16
SDS = jax.ShapeDtypeStruct
MESH = pl.DeviceIdType.MESH
ANY = pl.BlockSpec(memory_space=pl.ANY)

D = 1024
NIN = 8
NG = 8
GCH = 128
NH = 8
HD = 128
HCH = 64
HGRN_HB = 4
HW = HGRN_HB * HD
DW_TOKENS = 2048
ELEMENTWISE_BLOCK_BYTES = 2 * 1024 * 1024
PROJ_OUT_SLOTS = 4
FF = 2816
FFS = 1408
NCHIP = 4
EPS = 1e-6
QSCALE = HD ** -0.5
GELU_C0 = math.sqrt(2.0 / math.pi)
GELU_C1 = 0.044715
LR, B1, B2, AEPS, WD, STEP = 0.001, 0.9, 0.999, 1e-08, 0.01, 10
VMEM_LIMIT_V7X = 56 * 1024 * 1024
SP_ROWS = 144


def _cparams(**kw):
    return pltpu.CompilerParams(vmem_limit_bytes=VMEM_LIMIT_V7X, **kw)


def _mm(a, b):
    return jnp.dot(a, b, preferred_element_type=F32)


def _mm_nt(a, b):
    return lax.dot_general(a, b, (((1,), (1,)), ((), ())), preferred_element_type=F32)


def _mm_tn(a, b):
    return lax.dot_general(a, b, (((0,), (0,)), ((), ())), preferred_element_type=F32)


def _rows8(x):
    r, c = x.shape
    return jnp.sum(x.reshape(r // 8, 8, c), axis=0)


def _mean(x):
    return jnp.mean(x, axis=-1, keepdims=True)


def _sigmoid(x):
    return 1.0 / (1.0 + jnp.exp(-x))


def _gelu(x):
    t = jnp.tanh(GELU_C0 * (x + GELU_C1 * x * x * x))
    return 0.5 * x * (1.0 + t), t


def _gelu_grad(x, t):
    return 0.5 * (1.0 + t) + 0.5 * x * (1.0 - t * t) * (GELU_C0 * (1.0 + 3.0 * GELU_C1 * x * x))


def _component_of(group):
    return jnp.where(group < 6, (group + 4) % 6, group)


def _proj_fwd(place, x, g_mix, w_in4, later):
    T = x.shape[0]
    tm = min(1024, T)
    ni = T // tm
    n = len(later)

    def body(pc_ref, x_ref, g_ref, *rest):
        proj_ref, h_ref, w_all = rest[1 + n:4 + n]
        gathered = rest[4 + n:4 + 2 * n]
        hs, wbuf, wsem, obuf, osem = rest[4 + 2 * n:9 + 2 * n]
        w_sems, later_sems = rest[9 + 2 * n:15 + 2 * n], rest[15 + 2 * n:]
        jp, i = pl.program_id(0), pl.program_id(1)
        w_cols = [w_all.at[:, :, pl.ds(k * D, D)] for k in range(2)]

        def w_copy(blk):
            cols = pl.ds(pl.multiple_of((blk % 2) * D, 128), D)
            return pltpu.make_async_copy(w_all.at[pc_ref[0] ^ (blk // 2), :, cols], wbuf.at[blk % 2],
                                         wsem.at[blk % 2])

        @pl.when((jp == 0) & (i == 0))
        def _():
            _gather_start(w_cols, w_sems)
            w_copy(jp).start()

        @pl.when(i == 0)
        def _():
            w_copy(jp).wait()

        @pl.when(jp == 0)
        def _():
            xv = x_ref[...]
            r = lax.rsqrt(_mean(xv * xv) + EPS)
            hb = (xv * r * g_ref[...]).astype(BF16)
            hs[i] = hb
            h_ref[...] = hb

        step = jp * ni + i
        slot = step % PROJ_OUT_SLOTS

        def o_copy(slot_):
            comp = 2 * (pc_ref[0] ^ (jp // 2)) + jp % 2
            return pltpu.make_async_copy(
                obuf.at[slot_], proj_ref.at[comp, pl.ds(pl.multiple_of(i * tm, 8), tm)], osem.at[slot_])

        @pl.when(step >= PROJ_OUT_SLOTS)
        def _():
            o_copy(slot).wait()

        obuf[slot] = _mm(hs[i], wbuf[jp % 2])
        o_copy(slot).start()

        @pl.when(step == NIN * ni - 1)
        def _():
            for k in range(PROJ_OUT_SLOTS):
                o_copy((slot + 1 + k) % PROJ_OUT_SLOTS).wait()

        for nxt in range(1, NIN):
            @pl.when((jp == nxt - 1) & (i == ni - 1))
            def _():
                if nxt >= 2:
                    _gather_land([w_cols[nxt % 2]], w_sems, nxt // 2, first=nxt % 2)
                if nxt == 5:
                    _gather_start(gathered, later_sems)
                if nxt == NIN - 1:
                    _gather_neighbours(gathered, later_sems)
                w_copy(jp + 1).start()

        @pl.when((jp == NIN - 1) & (i == ni - 1))
        def _():
            _gather_drain(w_cols, w_sems)
            _gather_finish(gathered, later_sems)

    tile = lambda jp, i, pc: (jnp.where(jp == 0, i, ni - 1), 0)
    res = pl.pallas_call(
        body, name="proj_fwd",
        grid_spec=pltpu.PrefetchScalarGridSpec(
            num_scalar_prefetch=1, grid=(NIN, ni),
            in_specs=[pl.BlockSpec((tm, D), tile), pl.BlockSpec((1, D), lambda jp, i, pc: (0, 0))] + [ANY] * (1 + n),
            out_specs=[ANY, pl.BlockSpec((tm, D), tile)] + [ANY] * (1 + n),
            scratch_shapes=[pltpu.VMEM((ni, tm, D), BF16), pltpu.VMEM((2, D, D), BF16),
                            pltpu.SemaphoreType.DMA((2,)), pltpu.VMEM((PROJ_OUT_SLOTS, tm, D), F32),
                            pltpu.SemaphoreType.DMA((PROJ_OUT_SLOTS,))] + _gather_sems(2) + _gather_sems(n)),
        out_shape=[SDS((NIN, T, D), F32), SDS((T, D), BF16), SDS(w_in4.shape, BF16)]
        + [SDS(a.shape, a.dtype) for a in later],
        input_output_aliases={3 + k: 2 + k for k in range(1 + n)},
        compiler_params=_cparams(has_side_effects=True),
    )(place, x, g_mix, w_in4, *later)
    return res[:2], res[2], res[3:]


def _layer_norm_stats(gv):
    mu = _mean(gv)
    xc = gv - mu
    rs = lax.rsqrt(_mean(xc * xc) + EPS)
    return xc * rs, rs


def _gmlp_fwd(proj, ln_g, ln_b, wm, b_t, job=None):
    T = proj.shape[1]
    tm = min(256, T)

    def body(u_ref, v_ref, lg_ref, lb_ref, wm_ref, bt_ref, a_ref, a_s):
        gu, _ = _gelu(u_ref[...])
        gv, _ = _gelu(v_ref[...])
        vhat, _ = _layer_norm_stats(gv)
        vnb = (vhat * lg_ref[...] + lb_ref[...]).astype(BF16)
        for ch in range(tm // GCH):
            rows = slice(GCH * ch, GCH * (ch + 1))
            for g in range(NG):
                cols = slice(128 * g, 128 * (g + 1))
                mixed = _mm(wm_ref[g], vnb[rows, cols]) + bt_ref[:, g:g + 1]
                a_s[rows, cols] = gu[rows, cols] * mixed
        a_ref[...] = a_s[...].astype(BF16)

    row = lambda i: (0, 0)
    return _call(
        body, name="gmlp_fwd", grid=(T // tm,), job=job, args=(proj, proj, ln_g, ln_b, wm, b_t),
        in_specs=[pl.BlockSpec((None, tm, D), lambda i: (0, i, 0)), pl.BlockSpec((None, tm, D), lambda i: (1, i, 0)),
                  pl.BlockSpec((1, D), row), pl.BlockSpec((1, D), row),
                  pl.BlockSpec((NG, GCH, GCH), lambda i: (0, 0, 0)), pl.BlockSpec((GCH, NG), row)],
        out_specs=[pl.BlockSpec((tm, D), lambda i: (i, 0))],
        out_shape=[SDS((T, D), BF16)],
        scratch_shapes=[pltpu.VMEM((tm, D), F32)])


def _cumsum64(x, row):
    for s in (1, 2, 4, 8, 16, 32):
        x = x + jnp.where(row >= s, pltpu.roll(x, s, 0), 0.0)
    return x


def _revcumsum64(x, row):
    n = x.shape[0]
    for s in (1, 2, 4, 8, 16, 32):
        x = x + jnp.where(row < HCH - s, pltpu.roll(x, n - s, 0), 0.0)
    return x


def _head_mean(x):
    parts = [jnp.broadcast_to(_mean(x[:, HD * h:HD * (h + 1)]), (x.shape[0], HD)) for h in range(x.shape[1] // HD)]
    return jnp.concatenate(parts, axis=1)


def _seg_sum(x):
    n, c = x.shape
    s = jnp.sum(x.reshape(n // HCH, HCH, c), axis=1, keepdims=True)
    return jnp.broadcast_to(s, (n // HCH, HCH, c)).reshape(n, c)


def _hgrn_gates(fl, lbv, row):
    s = _sigmoid(fl)
    f = lbv + (1.0 - lbv) * s
    a = _cumsum64(jnp.log(f), row)
    a_mid = _seg_sum(jnp.where(row == HCH // 2 - 1, a, 0.0))
    a_last = _seg_sum(jnp.where(row == HCH - 1, a, 0.0))
    return s, f, a, a_mid, a_last


def _hgrn_fwd(proj, lb_table, norm_g, job=None):
    T = proj.shape[1]
    tb = min(512, T)
    nc = tb // HCH

    def body(q_ref, fl_ref, v_ref, g_ref, lbt_ref, gn_ref, o_ref, ob_ref, stb_ref, st_s, o_s):
        @pl.when(pl.program_id(1) == 0)
        def _():
            st_s[...] = jnp.zeros_like(st_s)

        row = lax.broadcasted_iota(jnp.int32, (tb, HW), 0) & (HCH - 1)
        lbv = _sigmoid(lbt_ref[0:1, :] - lbt_ref[1:2, :])
        _, f, a, a_mid, a_last = _hgrn_gates(fl_ref[...], lbv, row)
        k = 1.0 - f
        qs = q_ref[...] * QSCALE
        q_in = (qs * jnp.exp(a - a_mid)).astype(BF16)
        k_in = (k * jnp.exp(a_mid - a)).astype(BF16)
        q_a = (qs * jnp.exp(a)).astype(BF16)
        k_d = (k * jnp.exp(a_last - a)).astype(BF16)
        dec = jnp.exp(a_last)
        vb = v_ref[...].astype(BF16)
        tri = (lax.broadcasted_iota(jnp.int32, (HCH, HCH), 0)
               >= lax.broadcasted_iota(jnp.int32, (HCH, HCH), 1))
        for c in range(nc):
            sl = slice(HCH * c, HCH * (c + 1))
            for hh in range(HGRN_HB):
                hs = slice(HD * hh, HD * (hh + 1))
                st = st_s[hh]
                stb_ref[hh, c] = st
                sc = jnp.where(tri, _mm_nt(q_in[sl, hs], k_in[sl, hs]), 0.0)
                o_s[sl, hs] = _mm(sc.astype(BF16), vb[sl, hs]) + _mm_nt(q_a[sl, hs], st.astype(BF16))
                d64 = dec[sl, hs]
                st_s[hh] = st * jnp.concatenate([d64, d64], axis=0) + _mm_tn(vb[sl, hs], k_d[sl, hs])
        o = o_s[...]
        r = lax.rsqrt(_head_mean(o * o) + EPS)
        g = g_ref[...]
        o_ref[...] = o
        ob_ref[...] = (o * r * gn_ref[...] * (g * _sigmoid(g))).astype(BF16)

    def col(off):
        return pl.BlockSpec((None, tb, HW), lambda h, cb: (off, cb, h))

    return _call(
        body, name="hgrn_fwd", grid=(NH // HGRN_HB, T // tb), job=job,
        args=(proj, proj, proj, proj, lb_table, norm_g),
        in_specs=[col(2), col(3), col(4), col(5),
                  pl.BlockSpec((2, HW), lambda h, cb: (0, h)), pl.BlockSpec((1, HW), lambda h, cb: (0, h))],
        out_specs=[pl.BlockSpec((tb, HW), lambda h, cb: (cb, h)), pl.BlockSpec((tb, HW), lambda h, cb: (cb, h)),
                   pl.BlockSpec((HGRN_HB, nc, HD, HD), lambda h, cb: (h, cb, 0, 0))],
        out_shape=[SDS((T, D), F32), SDS((T, D), BF16), SDS((NH, T // HCH, HD, HD), F32)],
        scratch_shapes=[pltpu.VMEM((HGRN_HB, HD, HD), F32), pltpu.VMEM((tb, HW), F32)])


def _merge_fwd(x, ab, ob, proj, w_a, w_b, w_out, job=None):
    T = x.shape[0]
    tm = min(512, T)

    def body(x_ref, ab_ref, ob_ref, ga_ref, gb_ref, wa_ref, wb_ref, wo_ref, mg_ref, x1_ref):
        ya = _mm(ab_ref[...], wa_ref[...])
        yb = _mm(ob_ref[...], wb_ref[...])
        merged = (_sigmoid(ga_ref[...]) * ya + _sigmoid(gb_ref[...]) * yb).astype(BF16)
        mg_ref[...] = merged
        x1_ref[...] = x_ref[...] + _mm(merged, wo_ref[...])

    t = lambda i: (i, 0)
    w = lambda i: (0, 0)
    return _call(
        body, name="merge_fwd", grid=(T // tm,), job=job, args=(x, ab, ob, proj, proj, w_a, w_b, w_out),
        in_specs=[pl.BlockSpec((tm, D), t), pl.BlockSpec((tm, D), t), pl.BlockSpec((tm, D), t),
                  pl.BlockSpec((None, tm, D), lambda i: (6, i, 0)), pl.BlockSpec((None, tm, D), lambda i: (7, i, 0)),
                  pl.BlockSpec((D, D), w), pl.BlockSpec((D, D), w), pl.BlockSpec((D, D), w)],
        out_specs=[pl.BlockSpec((tm, D), t)] * 2,
        out_shape=[SDS((T, D), BF16), SDS((T, D), F32)])


def _ffn_fwd_bwd(x1, target, g_ffn, g_fin, w_gu4, w_down):
    T = x1.shape[0]
    tm = min(256, T)
    inv_d = 1.0 / D

    def body(x1_ref, tg_ref, gf_ref, gn_ref, wgu_ref, wd_ref,
             act_ref, dx2b_ref, h2b_ref, dgu_ref, dx1_ref, dx1b_ref, acc_ref):
        @pl.when(pl.program_id(0) == 0)
        def _():
            acc_ref[...] = jnp.zeros_like(acc_ref)

        x1v = x1_ref[...]
        gf = gf_ref[...]
        gn = gn_ref[...]
        rr1 = lax.rsqrt(_mean(x1v * x1v) + EPS)
        x1n = x1v * rr1
        h2b = (x1n * gf).astype(BF16)
        h2b_ref[...] = h2b
        p = [_mm(h2b, wgu_ref[k]) for k in range(NCHIP)]
        sg = [_sigmoid(p[0]), _sigmoid(p[1])]
        si = [p[0] * sg[0], p[1] * sg[1]]
        x2 = x1v
        for k in range(2):
            actk = (si[k] * p[2 + k]).astype(BF16)
            act_ref[:, FFS * k:FFS * (k + 1)] = actk
            x2 = x2 + _mm(actk, wd_ref[FFS * k:FFS * (k + 1), :])
        rr2 = lax.rsqrt(_mean(x2 * x2) + EPS)
        x2n = x2 * rr2
        e = x2n * gn - tg_ref[...]
        acc_ref[0] += _rows8(e * e) * (0.5 * inv_d)
        dy = e * inv_d
        acc_ref[1] += _rows8(dy * x2n)
        dxn = dy * gn
        dx2 = rr2 * (dxn - x2n * _mean(dxn * x2n))
        dx2b = dx2.astype(BF16)
        dx2b_ref[...] = dx2b
        dh2 = None
        for k in range(2):
            dact = _mm_nt(dx2b, wd_ref[FFS * k:FFS * (k + 1), :])
            dgate = (dact * p[2 + k] * (sg[k] * (1.0 + p[k] * (1.0 - sg[k])))).astype(BF16)
            dup = (dact * si[k]).astype(BF16)
            dgu_ref[k] = dgate
            dgu_ref[2 + k] = dup
            part = _mm_nt(dgate, wgu_ref[k]) + _mm_nt(dup, wgu_ref[2 + k])
            dh2 = part if dh2 is None else dh2 + part
        acc_ref[2] += _rows8(dh2 * x1n)
        dxn1 = dh2 * gf
        dx1 = dx2 + rr1 * (dxn1 - x1n * _mean(dxn1 * x1n))
        dx1_ref[...] = dx1
        dx1b_ref[...] = dx1.astype(BF16)

    t = lambda i: (i, 0)
    w = lambda i: (0, 0)
    one = pl.Buffered(1)
    return pl.pallas_call(
        body, name="ffn_fwd_bwd", grid=(T // tm,),
        in_specs=[pl.BlockSpec((tm, D), t), pl.BlockSpec((tm, D), t),
                  pl.BlockSpec((1, D), w), pl.BlockSpec((1, D), w),
                  pl.BlockSpec((NCHIP, D, FFS), lambda i: (0, 0, 0), pipeline_mode=one),
                  pl.BlockSpec((FF, D), w, pipeline_mode=one)],
        out_specs=[pl.BlockSpec((tm, FF), t), pl.BlockSpec((tm, D), t), pl.BlockSpec((tm, D), t),
                   pl.BlockSpec((NCHIP, tm, FFS), lambda i: (0, i, 0)),
                   pl.BlockSpec((tm, D), t), pl.BlockSpec((tm, D), t),
                   pl.BlockSpec((3, 8, D), lambda i: (0, 0, 0))],
        out_shape=[SDS((T, FF), BF16), SDS((T, D), BF16), SDS((T, D), BF16),
                   SDS((NCHIP, T, FFS), BF16), SDS((T, D), F32), SDS((T, D), BF16),
                   SDS((3, 8, D), F32)],
        compiler_params=_cparams(),
    )(x1, target, g_ffn, g_fin, w_gu4, w_down)


def _merge_bwd(dx1b, ab, ob, proj, w_out, w_a, w_b, job=None):
    T = dx1b.shape[0]
    tm = min(512, T)

    def body(dx_ref, ab_ref, ob_ref, ga_ref, gb_ref, wo_ref, wa_ref, wb_ref, dya_ref, dyb_ref, dp_ref):
        dm = _mm_nt(dx_ref[...], wo_ref[...])
        sa = _sigmoid(ga_ref[...])
        sb = _sigmoid(gb_ref[...])
        dya_ref[...] = (dm * sa).astype(BF16)
        dyb_ref[...] = (dm * sb).astype(BF16)
        dp_ref[0] = (dm * _mm(ab_ref[...], wa_ref[...]) * sa * (1.0 - sa)).astype(BF16)
        dp_ref[1] = (dm * _mm(ob_ref[...], wb_ref[...]) * sb * (1.0 - sb)).astype(BF16)

    t = lambda i: (i, 0)
    w = lambda i: (0, 0)
    return _call(
        body, name="merge_bwd", grid=(T // tm,),
        in_specs=[pl.BlockSpec((tm, D), t), pl.BlockSpec((tm, D), t), pl.BlockSpec((tm, D), t),
                  pl.BlockSpec((None, tm, D), lambda i: (6, i, 0)), pl.BlockSpec((None, tm, D), lambda i: (7, i, 0)),
                  pl.BlockSpec((D, D), w), pl.BlockSpec((D, D), w), pl.BlockSpec((D, D), w)],
        out_specs=[pl.BlockSpec((tm, D), t)] * 2 + [pl.BlockSpec((2, tm, D), lambda i: (3, i, 0))],
        out_shape=[SDS((T, D), BF16), SDS((T, D), BF16), SDS((NIN, T, D), BF16)],
        args=(dx1b, ab, ob, proj, proj, w_out, w_a, w_b), job=job)


def _hgrn_bwd(dproj, dyb, w_b, o_raw, proj, st_before, lb_table, norm_g, job=None):
    T = dyb.shape[0]
    tb = min(512, T)
    nc = tb // HCH
    nb = T // tb

    def body(dp_in, dyb_ref, wb_ref, o_ref, q_ref, fl_ref, v_ref, g_ref, stb_ref, lbt_ref, gn_ref,
             dp_ref, acc_ref, dst_s, dqin_s, dqa_s, dkin_s, dkd_s, dv_s, ddec_s):
        del dp_in

        @pl.when(pl.program_id(1) == 0)
        def _():
            dst_s[...] = jnp.zeros_like(dst_s)
            acc_ref[...] = jnp.zeros_like(acc_ref)

        row = lax.broadcasted_iota(jnp.int32, (tb, HW), 0) & (HCH - 1)
        gn = gn_ref[...]
        lbv = _sigmoid(lbt_ref[0:1, :] - lbt_ref[1:2, :])
        o = o_ref[...]
        r = lax.rsqrt(_head_mean(o * o) + EPS)
        on = o * r
        g = g_ref[...]
        sgm = _sigmoid(g)
        dob_v = _mm_nt(dyb_ref[...], wb_ref[...])
        dp_ref[3] = (dob_v * on * gn * (sgm * (1.0 + g * (1.0 - sgm)))).astype(BF16)
        do_n = dob_v * (g * sgm)
        acc_ref[1] += _rows8(do_n * on)
        dxn = do_n * gn
        do = (r * (dxn - on * _head_mean(dxn * on))).astype(BF16)
        s, f, a, a_mid, a_last = _hgrn_gates(fl_ref[...], lbv, row)
        k = 1.0 - f
        qs = q_ref[...] * QSCALE
        e_q = jnp.exp(a - a_mid)
        e_k = jnp.exp(a_mid - a)
        e_a = jnp.exp(a)
        e_l = jnp.exp(a_last - a)
        dec = jnp.exp(a_last)
        q_in = qs * e_q
        k_in = k * e_k
        q_a = qs * e_a
        k_d = k * e_l
        q_inb, k_inb, q_ab, k_db = (z.astype(BF16) for z in (q_in, k_in, q_a, k_d))
        vb = v_ref[...].astype(BF16)
        tri = (lax.broadcasted_iota(jnp.int32, (HCH, HCH), 0)
               >= lax.broadcasted_iota(jnp.int32, (HCH, HCH), 1))
        for c in reversed(range(nc)):
            sl = slice(HCH * c, HCH * (c + 1))
            for hh in range(HGRN_HB):
                hs = slice(HD * hh, HD * (hh + 1))
                stp = stb_ref[hh, c]
                dst = dst_s[hh]
                dstb = dst.astype(BF16)
                do_c = do[sl, hs]
                v_c = vb[sl, hs]
                dqa_s[sl, hs] = _mm(do_c, stp.astype(BF16))
                dkd_s[sl, hs] = _mm(v_c, dstb)
                ddec_s[sl, hs] = jnp.broadcast_to(jnp.sum(dst * stp, axis=0, keepdims=True), (HCH, HD))
                sc = jnp.where(tri, _mm_nt(q_inb[sl, hs], k_inb[sl, hs]), 0.0).astype(BF16)
                dsc = jnp.where(tri, _mm_nt(do_c, v_c), 0.0).astype(BF16)
                dv_s[sl, hs] = _mm_nt(k_db[sl, hs], dstb) + _mm_tn(sc, do_c)
                dqin_s[sl, hs] = _mm(dsc, k_inb[sl, hs])
                dkin_s[sl, hs] = _mm_tn(dsc, q_inb[sl, hs])
                d64 = dec[sl, hs]
                dst_s[hh] = dst * jnp.concatenate([d64, d64], axis=0) + _mm_tn(do_c, q_ab[sl, hs])
        dq_in = dqin_s[...]
        dq_a = dqa_s[...]
        dk_in = dkin_s[...]
        dk_d = dkd_s[...]
        dp_ref[0] = ((dq_in * e_q + dq_a * e_a) * QSCALE).astype(BF16)
        dp_ref[2] = dv_s[...].astype(BF16)
        tq = dq_in * q_in
        tk = dk_in * k_in
        td = dk_d * k_d
        d_a = tq + dq_a * q_a - tk - td
        d_a = d_a + jnp.where(row == HCH // 2 - 1, _seg_sum(tk - tq), 0.0)
        d_a = d_a + jnp.where(row == HCH - 1, _seg_sum(td) + ddec_s[...] * dec, 0.0)
        dlf = _revcumsum64(d_a, row)
        df = dlf / f - (dk_in * e_k + dk_d * e_l)
        dp_ref[1] = (df * (1.0 - lbv) * s * (1.0 - s)).astype(BF16)
        acc_ref[0] += _rows8(df * (1.0 - s))

    def col(off):
        return pl.BlockSpec((None, tb, HW), lambda h, cb: (off, nb - 1 - cb, h))

    hb = lambda h, cb: (nb - 1 - cb, h)
    return _call(
        body, name="hgrn_bwd", grid=(NH // HGRN_HB, nb), job=job,
        args=(dproj, dyb, w_b, o_raw, proj, proj, proj, proj, st_before, lb_table, norm_g),
        in_specs=[ANY, pl.BlockSpec((tb, D), lambda h, cb: (nb - 1 - cb, 0)),
                  pl.BlockSpec((HW, D), lambda h, cb: (h, 0)), pl.BlockSpec((tb, HW), hb),
                  col(2), col(3), col(4), col(5),
                  pl.BlockSpec((HGRN_HB, nc, HD, HD), lambda h, cb: (h, nb - 1 - cb, 0, 0)),
                  pl.BlockSpec((2, HW), lambda h, cb: (0, h)), pl.BlockSpec((1, HW), lambda h, cb: (0, h))],
        out_specs=[pl.BlockSpec((4, tb, HW), lambda h, cb: (0, nb - 1 - cb, h)),
                   pl.BlockSpec((2, 8, HW), lambda h, cb: (0, 0, h))],
        out_shape=[SDS(dproj.shape, BF16), SDS((2, 8, D), F32)],
        scratch_shapes=[pltpu.VMEM((HGRN_HB, HD, HD), F32)] + [pltpu.VMEM((tb, HW), F32)] * 6,
        aliases={0: 0})


def _gmlp_bwd(dproj, dya, w_a, proj, ln_g, ln_b, wm, wm_t, b_t):
    T = dya.shape[0]
    tm = min(256, T)

    def body(dp_in, dya_ref, wa_ref, u_ref, v_ref, lg_ref, lb_ref, wm_ref, wmt_ref, bt_ref,
             dp_ref, acc_ref, dws_ref, dmix_ref, du_s, dvn_s):
        del dp_in

        @pl.when(pl.program_id(0) == 0)
        def _():
            acc_ref[...] = jnp.zeros_like(acc_ref)
            dws_ref[...] = jnp.zeros_like(dws_ref)
            dmix_ref[...] = jnp.zeros_like(dmix_ref)

        u = u_ref[...]
        v = v_ref[...]
        lg = lg_ref[...]
        gu, t_u = _gelu(u)
        gv, t_v = _gelu(v)
        vhat, rs = _layer_norm_stats(gv)
        vnb = (vhat * lg + lb_ref[...]).astype(BF16)
        da_v = _mm_nt(dya_ref[...], wa_ref[...])
        for ch in range(tm // GCH):
            rows = slice(GCH * ch, GCH * (ch + 1))
            for g in range(NG):
                cols = slice(128 * g, 128 * (g + 1))
                vng = vnb[rows, cols]
                mixed = _mm(wm_ref[g], vng) + bt_ref[:, g:g + 1]
                dag = da_v[rows, cols]
                dmx = dag * gu[rows, cols]
                du_s[rows, cols] = dag * mixed
                dmxb = dmx.astype(BF16)
                dws_ref[:, cols] += _mm_nt(dmxb, vng)
                dmix_ref[:, cols] += dmx
                dvn_s[rows, cols] = _mm(wmt_ref[g], dmxb)
        dp_ref[0] = (du_s[...] * _gelu_grad(u, t_u)).astype(BF16)
        dvn = dvn_s[...]
        acc_ref[0] += _rows8(dvn * vhat)
        acc_ref[1] += _rows8(dvn)
        dvh = dvn * lg
        dgv = rs * (dvh - _mean(dvh) - vhat * _mean(dvh * vhat))
        dp_ref[1] = (dgv * _gelu_grad(v, t_v)).astype(BF16)

    row = lambda i: (0, 0)
    w3 = lambda i: (0, 0, 0)
    return pl.pallas_call(
        body, name="gmlp_bwd", grid=(T // tm,),
        in_specs=[ANY, pl.BlockSpec((tm, D), lambda i: (i, 0)), pl.BlockSpec((D, D), row),
                  pl.BlockSpec((None, tm, D), lambda i: (0, i, 0)), pl.BlockSpec((None, tm, D), lambda i: (1, i, 0)),
                  pl.BlockSpec((1, D), row), pl.BlockSpec((1, D), row),
                  pl.BlockSpec((NG, GCH, GCH), w3), pl.BlockSpec((NG, GCH, GCH), w3),
                  pl.BlockSpec((GCH, NG), row)],
        out_specs=[pl.BlockSpec((2, tm, D), lambda i: (2, i, 0)),
                   pl.BlockSpec((2, 8, D), w3), pl.BlockSpec((GCH, D), row), pl.BlockSpec((GCH, D), row)],
        out_shape=[SDS(dproj.shape, BF16), SDS((2, 8, D), F32), SDS((GCH, D), F32), SDS((GCH, D), F32)],
        scratch_shapes=[pltpu.VMEM((tm, D), F32), pltpu.VMEM((tm, D), F32)],
        input_output_aliases={0: 0},
        compiler_params=_cparams(),
    )(dproj, dya, w_a, proj, proj, ln_g, ln_b, wm, wm_t, b_t)


def _proj_bwd(dproj, w_in4, x, dx1, g_mix, job=None):
    T = x.shape[0]
    tm = min(256, T)
    order = (2, 3, 4, 5, 0, 1, 6, 7)

    def body(dp_ref, w_ref, x_ref, dx1_ref, g_ref, gx_ref, acc_ref):
        @pl.when(pl.program_id(0) == 0)
        def _():
            acc_ref[...] = jnp.zeros_like(acc_ref)

        dh = None
        for m, og in enumerate(order):
            part = _mm_nt(dp_ref[m], w_ref[og // 2, :, D * (og % 2):D * (og % 2 + 1)])
            dh = part if dh is None else dh + part
        xv = x_ref[...]
        r = lax.rsqrt(_mean(xv * xv) + EPS)
        xn = xv * r
        acc_ref[...] += _rows8(dh * xn)
        dxn = dh * g_ref[...]
        gx_ref[...] = dx1_ref[...] + r * (dxn - xn * _mean(dxn * xn))

    t = lambda i: (i, 0)
    return _call(
        body, name="proj_bwd", grid=(T // tm,),
        in_specs=[pl.BlockSpec((NIN, tm, D), lambda i: (0, i, 0)),
                  pl.BlockSpec((NCHIP, D, 2 * D), lambda i: (0, 0, 0), pipeline_mode=pl.Buffered(1)),
                  pl.BlockSpec((tm, D), t), pl.BlockSpec((tm, D), t), pl.BlockSpec((1, D), lambda i: (0, 0))],
        out_specs=[pl.BlockSpec((tm, D), t), pl.BlockSpec((8, D), lambda i: (0, 0))],
        out_shape=[SDS((T, D), F32), SDS((8, D), F32)],
        args=(dproj, w_in4, x, dx1, g_mix), job=job)


def _dw_call(name, a, b, a_spec, b_spec, o_spec, out_shape, nblk, tt, job=None, prefetch=None):
    T = a.shape[-2]

    def body(*refs):
        a_ref, b_ref, o_ref = refs[-3:]

        @pl.when(pl.program_id(1) == 0)
        def _():
            o_ref[...] = jnp.zeros_like(o_ref)
        o_ref[...] += _mm_tn(a_ref[...], b_ref[...])

    (out,), job_out = _call(
        body, name=name, grid=(nblk, T // tt), in_specs=[a_spec, b_spec], out_specs=[o_spec],
        out_shape=[out_shape], args=(a, b), job=job, prefetch=prefetch)
    return out, job_out


def _dw_in_half(name, place, hb, dproj, mine, job=None):
    tt = min(DW_TOKENS, hb.shape[0])

    def comp(k, pc):
        return _component_of(2 * k + (pc[1] if mine else 1 - pc[1]))

    return _dw_call(
        name, hb, dproj,
        pl.BlockSpec((tt, D), lambda k, t, pc: (t, 0)),
        pl.BlockSpec((None, tt, D), lambda k, t, pc: (comp(k, pc), t, 0)),
        pl.BlockSpec((None, D, D), lambda k, t, pc: (k, 0, 0)),
        SDS((NCHIP, D, D), F32), NCHIP, tt, job, place)


def _dw_gate_up(h2b, dgu4, job=None):
    tt = min(DW_TOKENS, h2b.shape[0])
    return _dw_call(
        "dw_gate_up", h2b, dgu4,
        pl.BlockSpec((tt, D), lambda k, t: (t, 0)),
        pl.BlockSpec((None, tt, FFS), lambda k, t: (k, t, 0)),
        pl.BlockSpec((None, D, FFS), lambda k, t: (k, 0, 0)),
        SDS((NCHIP, D, FFS), F32), NCHIP, tt, job)


def _dw_down(act, dx2b, job=None):
    tt = min(DW_TOKENS, act.shape[0])
    g, job_out = _dw_call(
        "dw_down", act, dx2b,
        pl.BlockSpec((tt, FFS), lambda k, t: (t, k)),
        pl.BlockSpec((tt, D), lambda k, t: (t, 0)),
        pl.BlockSpec((FFS, D), lambda k, t: (k, 0)),
        SDS((FF, D), F32), 2, tt, job)
    return g.reshape(NCHIP, FF // NCHIP, D), job_out


def _dw_square(name, a, b, job=None):
    tt = min(DW_TOKENS, a.shape[0])
    g, job_out = _dw_call(
        name, a, b,
        pl.BlockSpec((tt, D), lambda k, t: (t, 0)), pl.BlockSpec((tt, D), lambda k, t: (t, 0)),
        pl.BlockSpec((D, D), lambda k, t: (0, 0)), SDS((D, D), F32), 1, tt, job)
    return g.reshape(NCHIP, D // NCHIP, D), job_out


def _place():
    x, y, c = lax.axis_index("x"), lax.axis_index("y"), lax.axis_index("c")
    return x, y, c, 2 * x + y


def _chip_at(x, y, s):
    return x ^ (s >> 1), y ^ (s & 1)


class _Job:
    def __init__(self, ins, out_shapes, sems, start, finish, aliases=None, mid=None):
        self.ins, self.out_shapes, self.sems = list(ins), list(out_shapes), list(sems)
        self.start, self.finish, self.aliases = start, finish, dict(aliases or {})
        self.mid = mid if mid is not None else (lambda ins, outs, sems: None)


def _join_jobs(*jobs):
    def cut(refs, sizes):
        out, at = [], 0
        for n in sizes:
            out.append(refs[at:at + n])
            at += n
        return out

    ni = [len(j.ins) for j in jobs]
    no = [len(j.out_shapes) for j in jobs]
    ns = [len(j.sems) for j in jobs]

    def run(which):
        def go(ins, outs, sems):
            for j, a, b, c in zip(jobs, cut(ins, ni), cut(outs, no), cut(sems, ns)):
                getattr(j, which)(a, b, c)
        return go

    aliases = {}
    for k, j in enumerate(jobs):
        for a, b in j.aliases.items():
            aliases[sum(ni[:k]) + a] = sum(no[:k]) + b
    return _Job([a for j in jobs for a in j.ins], [o for j in jobs for o in j.out_shapes],
                [s for j in jobs for s in j.sems], run("start"), run("finish"), aliases, run("mid"))


def _call(body, *, name, grid, in_specs, out_specs, out_shape, args, scratch_shapes=(), aliases=None,
          job=None, prefetch=None):
    n_in, n_out, n_scr = len(in_specs), len(out_specs), len(scratch_shapes)
    npf = 0 if prefetch is None else 1
    job = job if job is not None else _Job([], [], [], lambda *a: None, lambda *a: None)
    ji, jo = len(job.ins), len(job.out_shapes)
    steps = math.prod(grid)

    def wrapped(*refs):
        pf, refs = refs[:npf], refs[npf:]
        ins, jin = refs[:n_in], refs[n_in:n_in + ji]
        o0 = n_in + ji
        outs, jout = refs[o0:o0 + n_out], refs[o0 + n_out:o0 + n_out + jo]
        s0 = o0 + n_out + jo
        scr, jsem = refs[s0:s0 + n_scr], refs[s0 + n_scr:]
        step = functools.reduce(lambda acc, ag: acc * ag[1] + pl.program_id(ag[0]), enumerate(grid), 0)
        if ji or jo:
            @pl.when(step == 0)
            def _():
                job.start(jin, jout, jsem)

        body(*pf, *ins, *outs, *scr)

        if ji or jo:
            @pl.when(step == steps // 2)
            def _():
                job.mid(jin, jout, jsem)

            @pl.when(step == steps - 1)
            def _():
                job.finish(jin, jout, jsem)

    io = {npf + a: b for a, b in dict(aliases or {}).items()}
    io.update({npf + n_in + a: n_out + b for a, b in job.aliases.items()})
    kw = dict(in_specs=list(in_specs) + [ANY] * ji, out_specs=list(out_specs) + [ANY] * jo,
              scratch_shapes=list(scratch_shapes) + job.sems)
    if npf:
        kw = dict(grid_spec=pltpu.PrefetchScalarGridSpec(num_scalar_prefetch=1, grid=grid, **kw))
    else:
        kw["grid"] = grid
    res = pl.pallas_call(
        wrapped, name=name, out_shape=list(out_shape) + job.out_shapes, input_output_aliases=io,
        compiler_params=_cparams(has_side_effects=bool(ji or jo)), **kw,
    )(*(() if prefetch is None else (prefetch,)), *args, *job.ins)
    return list(res[:n_out]), list(res[n_out:])


def _run_job(job, name):
    ji, jo = len(job.ins), len(job.out_shapes)

    def body(*refs):
        jin, jout, jsem = refs[:ji], refs[ji:ji + jo], refs[ji + jo:]
        job.start(jin, jout, jsem)
        job.finish(jin, jout, jsem)

    return list(pl.pallas_call(
        body, name=name, in_specs=[ANY] * ji, out_specs=[ANY] * jo, out_shape=job.out_shapes,
        scratch_shapes=job.sems, input_output_aliases=job.aliases,
        compiler_params=pltpu.CompilerParams(has_side_effects=True))(*job.ins))


def _cast_shard(name, place, w):
    rows, cols = w.shape
    tr = 352 if rows % 352 == 0 else 256

    def body(pc_ref, w_ref, o_ref):
        del pc_ref
        o_ref[...] = w_ref[...].astype(BF16)

    return pl.pallas_call(
        body, name=name,
        grid_spec=pltpu.PrefetchScalarGridSpec(
            num_scalar_prefetch=1, grid=(rows // tr,),
            in_specs=[pl.BlockSpec((tr, cols), lambda i, pc: (i, 0))],
            out_specs=pl.BlockSpec((None, tr, cols), lambda i, pc: (pc[0], i, 0))),
        out_shape=SDS((NCHIP, rows, cols), BF16),
        compiler_params=_cparams(),
    )(place, w)


def _sibling_copy(ref, send_sem, recv_sem):
    x, y, c, _ = _place()
    return pltpu.make_async_remote_copy(src_ref=ref, dst_ref=ref, send_sem=send_sem, recv_sem=recv_sem,
                                        device_id=(x, y, 1 - c), device_id_type=MESH)


def _half_rows(arr, slot, core):
    half = arr.shape[1] // 2
    return arr.at[slot, pl.ds(pl.multiple_of(core * half, 16), half)]


def _quarter_rows(arr, slot, core, q):
    quarter = arr.shape[1] // 4
    return arr.at[slot, pl.ds(pl.multiple_of((2 * core + q) * quarter, 16), quarter)]


def _chip_copy(ref, dist, send_sem, recv_sem):
    x, y, c, _ = _place()
    cx, cy = _chip_at(x, y, dist)
    return pltpu.make_async_remote_copy(src_ref=ref, dst_ref=ref, send_sem=send_sem, recv_sem=recv_sem,
                                        device_id=(cx, cy, c), device_id_type=MESH)


def _gather_sems(n):
    dma = pltpu.SemaphoreType.DMA
    return [dma((n, 2))] * 4 + [dma((n, 4))] * 2


def _gather_start(arrs, sems):
    dsend, drecv = sems[0], sems[1]
    _, _, c, j = _place()
    for w, arr in enumerate(arrs):
        for dist in (1, 2):
            _chip_copy(_half_rows(arr, j, c), dist, dsend.at[w, dist - 1], drecv.at[w, dist - 1]).start()


def _gather_land(arrs, sems, dist, first=0):
    dsend, drecv, rsend, rrecv, fsend, frecv = sems
    _, _, c, j = _place()
    if dist < 3:
        other = 3 - dist
        for w, arr in enumerate(arrs, first):
            landed = _half_rows(arr, j ^ dist, c)
            _chip_copy(landed, dist, dsend.at[w, dist - 1], drecv.at[w, dist - 1]).wait_recv()
            relay = _quarter_rows(arr, j ^ dist, c, other - 1)
            _chip_copy(relay, other, rsend.at[w, other - 1], rrecv.at[w, other - 1]).start()
            _sibling_copy(landed, fsend.at[w, dist - 1], frecv.at[w, dist - 1]).start()
        for w, arr in enumerate(arrs, first):
            theirs = _half_rows(arr, j ^ dist, 1 - c)
            _sibling_copy(theirs, fsend.at[w, dist - 1], frecv.at[w, dist - 1]).wait_recv()
    else:
        for w, arr in enumerate(arrs, first):
            for via in (1, 2):
                piece = _quarter_rows(arr, j ^ 3, c, via - 1)
                _chip_copy(piece, via, rsend.at[w, via - 1], rrecv.at[w, via - 1]).wait_recv()
                _sibling_copy(piece, fsend.at[w, 1 + via], frecv.at[w, 1 + via]).start()
        for w, arr in enumerate(arrs, first):
            for via in (1, 2):
                theirs = _quarter_rows(arr, j ^ 3, 1 - c, via - 1)
                _sibling_copy(theirs, fsend.at[w, 1 + via], frecv.at[w, 1 + via]).wait_recv()


def _gather_drain(arrs, sems):
    dsend, drecv, rsend, rrecv, fsend, frecv = sems
    _, _, c, j = _place()
    for w, arr in enumerate(arrs):
        for dist in (1, 2):
            other = 3 - dist
            _chip_copy(_half_rows(arr, j, c), dist, dsend.at[w, dist - 1], drecv.at[w, dist - 1]).wait_send()
            _chip_copy(_quarter_rows(arr, j ^ dist, c, other - 1), other,
                       rsend.at[w, other - 1], rrecv.at[w, other - 1]).wait_send()
            _sibling_copy(_half_rows(arr, j ^ dist, c), fsend.at[w, dist - 1], frecv.at[w, dist - 1]).wait_send()
            _sibling_copy(_quarter_rows(arr, j ^ 3, c, dist - 1),
                          fsend.at[w, 1 + dist], frecv.at[w, 1 + dist]).wait_send()


def _gather_neighbours(arrs, sems):
    _gather_land(arrs, sems, 1)
    _gather_land(arrs, sems, 2)


def _gather_finish(arrs, sems):
    _gather_land(arrs, sems, 3)
    _gather_drain(arrs, sems)


def _gather_job(arrs):
    n = len(arrs)
    return _Job(arrs, [SDS(a.shape, a.dtype) for a in arrs], _gather_sems(n),
                lambda ins, outs, sems: _gather_start(outs, sems),
                lambda ins, outs, sems: _gather_finish(outs, sems), {k: k for k in range(n)},
                mid=lambda ins, outs, sems: _gather_neighbours(outs, sems))


def _exchange_job(arrs, out_shapes, n, copies):
    def start(ins, outs, sems):
        for cp in copies(ins, outs, sems[0], sems[1]):
            cp.start()

    def finish(ins, outs, sems):
        for cp in copies(ins, outs, sems[0], sems[1]):
            cp.wait()

    return _Job(arrs, out_shapes, [pltpu.SemaphoreType.DMA((n,))] * 2, start, finish)


def _pair_exchange_job(grads):
    def copies(ins, outs, send_sem, recv_sem):
        x, y, c, _ = _place()
        res = []
        for w in range(len(grads)):
            half = ins[w].shape[1] // 2
            theirs = pl.ds(pl.multiple_of((1 - c) * half, 8), half)
            res.append(pltpu.make_async_remote_copy(
                src_ref=ins[w].at[:, theirs, :], dst_ref=outs[w], send_sem=send_sem.at[w],
                recv_sem=recv_sem.at[w], device_id=(x, y, 1 - c), device_id_type=MESH))
        return res

    return _exchange_job(grads, [SDS((NCHIP, g.shape[1] // 2, g.shape[2]), F32) for g in grads],
                         len(grads), copies)


def _row_tile(rows, cols):
    tr = rows
    while tr * cols * 4 > ELEMENTWISE_BLOCK_BYTES and tr % 32 == 0:
        tr //= 2
    return tr


def _pair_sum(name, place, g, sib):
    half, cols = sib.shape[1], sib.shape[2]
    tr = _row_tile(half, cols)
    nt = half // tr
    mine = nt if g.shape[1] == 2 * half else 0

    def body(pc_ref, g_ref, s_ref, own_ref, out_ref):
        del pc_ref
        v = g_ref[...] + s_ref[...]
        out_ref[...] = v.astype(BF16)

        @pl.when(pl.program_id(1) == 0)
        def _():
            own_ref[...] = v

    return pl.pallas_call(
        body, name=name,
        grid_spec=pltpu.PrefetchScalarGridSpec(
            num_scalar_prefetch=1, grid=(nt, NCHIP),
            in_specs=[pl.BlockSpec((None, tr, cols), lambda i, s, pc: (pc[0] ^ s, pc[1] * mine + i, 0)),
                      pl.BlockSpec((None, tr, cols), lambda i, s, pc: (pc[0] ^ s, i, 0))],
            out_specs=[pl.BlockSpec((tr, cols), lambda i, s, pc: (i, 0)),
                       pl.BlockSpec((None, tr, cols), lambda i, s, pc: (s, i, 0))]),
        out_shape=[SDS((half, cols), F32), SDS((NCHIP, half, cols), BF16)],
        compiler_params=_cparams(),
    )(place, g, sib)


def _chip_exchange_job(parts):
    def copies(ins, outs, send_sem, recv_sem):
        x, y, c, _ = _place()
        res = []
        for w in range(len(parts)):
            for s in range(1, NCHIP):
                cx, cy = _chip_at(x, y, s)
                k = w * (NCHIP - 1) + s - 1
                res.append(pltpu.make_async_remote_copy(
                    src_ref=ins[w].at[s], dst_ref=outs[w].at[s - 1], send_sem=send_sem.at[k],
                    recv_sem=recv_sem.at[k], device_id=(cx, cy, c), device_id_type=MESH))
        return res

    return _exchange_job(parts, [SDS((NCHIP - 1,) + p.shape[1:], BF16) for p in parts],
                         len(parts) * (NCHIP - 1), copies)


def _chip_sum(name, own, rem):
    half, cols = own.shape
    tr = _row_tile(half, cols)

    def body(own_ref, rem_ref, out_ref):
        out_ref[...] = ((own_ref[...] + rem_ref[0].astype(F32)) + rem_ref[1].astype(F32)) + rem_ref[2].astype(F32)

    return pl.pallas_call(
        body, name=name, grid=(half // tr,),
        in_specs=[pl.BlockSpec((tr, cols), lambda i: (i, 0)),
                  pl.BlockSpec((NCHIP - 1, tr, cols), lambda i: (0, i, 0))],
        out_specs=pl.BlockSpec((tr, cols), lambda i: (i, 0)),
        out_shape=SDS((half, cols), F32),
        compiler_params=_cparams(),
    )(own, rem)


def _share_halves_job(halves):
    def copies(ins, outs, send_sem, recv_sem):
        x, y, c, _ = _place()
        return [pltpu.make_async_remote_copy(
            src_ref=ins[w], dst_ref=outs[w], send_sem=send_sem.at[w], recv_sem=recv_sem.at[w],
            device_id=(x, y, 1 - c), device_id_type=MESH) for w in range(len(halves))]

    return _exchange_job(halves, [SDS(h.shape, F32) for h in halves], len(halves), copies)


def _adamw_math(w, g, m, v):
    m = B1 * m + (1.0 - B1) * g
    v = B2 * v + (1.0 - B2) * (g * g)
    m_hat = m / (1.0 - B1 ** STEP)
    v_hat = v / (1.0 - B2 ** STEP)
    delta = -LR * (m_hat / (jnp.sqrt(v_hat) + AEPS) + WD * w)
    return delta, m, v


def _adamw(name, place, w, own, sib, m, v):
    rows, cols = w.shape
    by_cols = own.shape[0] == rows
    half, pc_cols = (rows, cols // 2) if by_cols else (rows // 2, cols)
    tr = _row_tile(half, pc_cols)
    nt = half // tr

    def body(pc_ref, w_ref, own_ref, sib_ref, m_ref, v_ref, g_ref, d_ref, mo_ref, vo_ref):
        g = jnp.where(pl.program_id(0) == pc_ref[1], own_ref[...], sib_ref[...])
        d, mn, vn = _adamw_math(w_ref[...], g, m_ref[...], v_ref[...])
        g_ref[...] = g
        d_ref[...] = d
        mo_ref[...] = mn
        vo_ref[...] = vn

    full = pl.BlockSpec((tr, pc_cols), (lambda h, i, pc: (i, h)) if by_cols else (lambda h, i, pc: (h * nt + i, 0)))
    part = pl.BlockSpec((tr, pc_cols), lambda h, i, pc: (i, 0))
    return pl.pallas_call(
        body, name=name,
        grid_spec=pltpu.PrefetchScalarGridSpec(
            num_scalar_prefetch=1, grid=(2, nt),
            in_specs=[full, part, part, full, full], out_specs=[full] * 4),
        out_shape=[SDS((rows, cols), F32)] * 4,
        compiler_params=_cparams(),
    )(place, w, own, sib, m, v)


def _small_allreduce_adamw(sp, w, m, v):
    shape = sp.shape

    def body(sp_ref, w_ref, m_ref, v_ref, g_ref, d_ref, mo_ref, vo_ref,
             sib_s, pair_s, chip_s, send_sem, recv_sem):
        x, y, c, j = _place()
        cp = pltpu.make_async_remote_copy(
            src_ref=sp_ref, dst_ref=sib_s, send_sem=send_sem.at[0], recv_sem=recv_sem.at[0],
            device_id=(x, y, 1 - c), device_id_type=MESH)
        cp.start()
        cp.wait()
        pair_s[...] = sp_ref[...] + sib_s[...]
        half = shape[0] // 2
        mine = pl.ds(pl.multiple_of(c * half, 8), half)
        cps = []
        for s in range(1, NCHIP):
            cx, cy = _chip_at(x, y, s)
            cp = pltpu.make_async_remote_copy(
                src_ref=pair_s.at[mine], dst_ref=chip_s.at[s, mine], send_sem=send_sem.at[s],
                recv_sem=recv_sem.at[s], device_id=(cx, cy, c), device_id_type=MESH)
            cp.start()
            cps.append(cp)
        chip_s[0] = pair_s[...]
        for cp in cps:
            cp.wait()
        cps = []
        for s in range(1, NCHIP):
            cp = pltpu.make_async_remote_copy(
                src_ref=chip_s.at[s, mine], dst_ref=chip_s.at[s, mine], send_sem=send_sem.at[NCHIP + s],
                recv_sem=recv_sem.at[NCHIP + s], device_id=(x, y, 1 - c), device_id_type=MESH)
            cp.start()
            cps.append(cp)
        for cp in cps:
            cp.wait()
        tot = chip_s[j]
        for k in range(1, NCHIP):
            tot = tot + chip_s[k ^ j]
        g_ref[...] = tot
        d, mn, vn = _adamw_math(w_ref[...], tot, m_ref[...], v_ref[...])
        d_ref[...] = d
        mo_ref[...] = mn
        vo_ref[...] = vn

    vm = pl.BlockSpec(memory_space=pltpu.VMEM)
    return pl.pallas_call(
        body, name="small_allreduce_adamw",
        in_specs=[vm] * 4, out_specs=[vm] * 4, out_shape=[SDS(shape, F32)] * 4,
        scratch_shapes=[pltpu.VMEM(shape, F32), pltpu.VMEM(shape, F32), pltpu.VMEM((NCHIP,) + shape, F32),
                        pltpu.SemaphoreType.DMA((2 * NCHIP,)), pltpu.SemaphoreType.DMA((2 * NCHIP,))],
        compiler_params=pltpu.CompilerParams(has_side_effects=True),
    )(sp, w, m, v)


def _pack_small(first, mix, ln_g, ln_b, b_s, lbt, hn, ffn, fin, w_s):
    rows = [first.reshape(1, D), mix.reshape(1, D), ln_g.reshape(1, D), ln_b.reshape(1, D),
            b_s.reshape(1, D), lbt.reshape(2, D), hn.reshape(1, D), ffn.reshape(1, D), fin.reshape(1, D),
            jnp.zeros((6, D), F32)]
    return jnp.concatenate(rows + [w_s.reshape(NG, GCH, GCH).transpose(1, 0, 2).reshape(GCH, D)], axis=0)


def _unpack_small(p):
    w_s = p[16:].reshape(GCH, NG, GCH).transpose(1, 0, 2).reshape(1, NG, GCH, GCH)
    return dict(norm_mix_g=p[1:2], gmlp_ln_g=p[2:3], gmlp_ln_b=p[3:4], gmlp_b_s=p[4].reshape(1, NG, GCH),
                hgrn_lb_table=p[5:7], hgrn_norm_g=p[7:8], norm_ffn_g=p[8:9], norm_final_g=p[9],
                gmlp_w_s=w_s)


SMALL = ("norm_mix_g", "gmlp_ln_g", "gmlp_ln_b", "gmlp_w_s", "gmlp_b_s", "hgrn_lb_table", "hgrn_norm_g",
         "norm_ffn_g", "norm_final_g")
BIG = ("w_in", "w_gate_up", "w_branch_a", "w_branch_b", "w_out", "w_down")
ORDER = ("norm_mix_g", "w_in", "gmlp_ln_g", "gmlp_ln_b", "gmlp_w_s", "gmlp_b_s", "hgrn_lb_table",
         "hgrn_norm_g", "w_branch_a", "w_branch_b", "w_out", "norm_ffn_g", "w_gate_up", "w_down",
         "norm_final_g")


def kernel(x, norm_mix_g, w_in, gmlp_ln_g, gmlp_ln_b, gmlp_w_s, gmlp_b_s, hgrn_lb_table, hgrn_norm_g, w_branch_a, w_branch_b, w_out, norm_ffn_g, w_gate_up, w_down, norm_final_g, loss_target, m_norm_mix_g, m_w_in, m_gmlp_ln_g, m_gmlp_ln_b, m_gmlp_w_s, m_gmlp_b_s, m_hgrn_lb_table, m_hgrn_norm_g, m_w_branch_a, m_w_branch_b, m_w_out, m_norm_ffn_g, m_w_gate_up, m_w_down, m_norm_final_g, v_norm_mix_g, v_w_in, v_gmlp_ln_g, v_gmlp_ln_b, v_gmlp_w_s, v_gmlp_b_s, v_hgrn_lb_table, v_hgrn_norm_g, v_w_branch_a, v_w_branch_b, v_w_out, v_norm_ffn_g, v_w_gate_up, v_w_down, v_norm_final_g):
    args = dict(locals())
    T = x.shape[1]
    xs = x.reshape(T, D)
    target = loss_target.reshape(T, D)
    big = {n: args[n].reshape(args[n].shape[1:]) for n in BIG}
    big_m = {n: args["m_" + n].reshape(args[n].shape[1:]) for n in BIG}
    big_v = {n: args["v_" + n].reshape(args[n].shape[1:]) for n in BIG}

    x_i, y_i, c_i = lax.axis_index("x"), lax.axis_index("y"), lax.axis_index("c")
    place = jnp.stack([2 * x_i + y_i, c_i]).astype(jnp.int32)
    cast = {n: _cast_shard("cast_" + n, place, big[n]) for n in BIG}
    tril = jnp.tril(jnp.ones((GCH, GCH), bool))
    wm = jnp.where(tril, gmlp_w_s[0], 0.0).astype(BF16)
    wm_t = jnp.swapaxes(wm, 1, 2)
    b_t = gmlp_b_s[0].T

    (proj, hb), w_in4, (w_a4, w_b4, w_out4, w_down4) = _proj_fwd(
        place, xs, norm_mix_g, cast["w_in"], [cast[n] for n in ("w_branch_a", "w_branch_b", "w_out", "w_down")])
    (ab,), _ = _gmlp_fwd(proj, gmlp_ln_g, gmlp_ln_b, wm, b_t)
    (o_raw, obb, st_before), (w_gu4,) = _hgrn_fwd(
        proj, hgrn_lb_table, hgrn_norm_g, job=_gather_job([cast["w_gate_up"]]))
    w_a, w_b, w_o = (w.reshape(D, D) for w in (w_a4, w_b4, w_out4))
    (mgb, x1), _ = _merge_fwd(xs, ab, obb, proj, w_a, w_b, w_o)
    w_dn = w_down4.reshape(FF, D)
    act, dx2b, h2b, dgu4, dx1, dx1b, acc_ffn = _ffn_fwd_bwd(
        x1, target, norm_ffn_g, norm_final_g.reshape(1, D), w_gu4, w_dn)

    grads, owns, parts, halves, sibh = {}, {}, {}, {}, {}

    def pair_sums(names, sibs):
        for n, s in zip(names, sibs):
            owns[n], parts[n] = _pair_sum("rs_pair_sum_" + n, place, grads[n], s)

    def chip_sums(names, got):
        for n, r in zip(names, got):
            halves[n] = _chip_sum("rs_chip_sum_" + n, owns[n], r)

    ffn, mix = ("w_gate_up", "w_down"), ("w_branch_a", "w_branch_b", "w_out")
    grads["w_gate_up"], _ = _dw_gate_up(h2b, dgu4)
    grads["w_down"], _ = _dw_down(act, dx2b)
    (dya, dyb, dproj), got = _merge_bwd(
        dx1b, ab, obb, proj, w_o, w_a, w_b, job=_pair_exchange_job([grads[n] for n in ffn]))
    pair_sums(ffn, got)
    grads["w_branch_a"], _ = _dw_square("dw_branch_a", ab, dya)
    grads["w_branch_b"], _ = _dw_square("dw_branch_b", obb, dyb)
    grads["w_out"], _ = _dw_square("dw_out", mgb, dx1b)
    (dproj, acc_hgrn), got = _hgrn_bwd(
        dproj, dyb, w_b, o_raw, proj, st_before, hgrn_lb_table, hgrn_norm_g,
        job=_join_jobs(_chip_exchange_job([parts[n] for n in ffn]), _pair_exchange_job([grads[n] for n in mix])))
    chip_sums(ffn, got[:2])
    pair_sums(mix, got[2:])
    dproj, acc_ln, dws, dmix = _gmlp_bwd(dproj, dya, w_a, proj, gmlp_ln_g, gmlp_ln_b, wm, wm_t, b_t)
    for_sibling, got = _dw_in_half(
        "dw_in_sibling_half", place, hb, dproj, False,
        job=_join_jobs(_share_halves_job([halves[n] for n in ffn]), _chip_exchange_job([parts[n] for n in mix])))
    sibh.update(zip(ffn, got[:2]))
    chip_sums(mix, got[2:])
    grads["w_in"], got = _dw_in_half(
        "dw_in_own_half", place, hb, dproj, True, job=_share_halves_job([for_sibling]))
    pair_sums(("w_in",), got)
    (grad_x, acc_mix), got = _proj_bwd(
        dproj, w_in4, xs, dx1, norm_mix_g,
        job=_join_jobs(_chip_exchange_job([parts["w_in"]]), _share_halves_job([halves[n] for n in mix])))
    chip_sums(("w_in",), got[:1])
    sibh.update(zip(mix, got[1:]))
    (sibh["w_in"],) = _run_job(_share_halves_job([halves["w_in"]]), "rs_share_halves_w_in")
    out = {}
    for n in BIG:
        g, d, mn, vn = _adamw("adamw_" + n, place, big[n], halves[n], sibh[n], big_m[n], big_v[n])
        shp = args[n].shape
        out[n] = (g.reshape(shp), d.reshape(shp), mn.reshape(shp), vn.reshape(shp))

    lbv = jax.nn.sigmoid(hgrn_lb_table[0] - hgrn_lb_table[1])
    d_t0 = jnp.sum(acc_hgrn[0], axis=0) * lbv * (1.0 - lbv)
    loss_row = jnp.zeros((D,), F32).at[0].set(jnp.sum(acc_ffn[0]))
    dws_m = jnp.where(tril[:, None, :], dws.reshape(GCH, NG, GCH), 0.0).transpose(1, 0, 2)
    db_s = jnp.sum(dmix.reshape(GCH, NG, GCH), axis=-1).T
    sp = _pack_small(loss_row, jnp.sum(acc_mix, 0), jnp.sum(acc_ln[0], 0), jnp.sum(acc_ln[1], 0), db_s,
                     jnp.stack([d_t0, -d_t0]), jnp.sum(acc_hgrn[1], 0), jnp.sum(acc_ffn[2], 0),
                     jnp.sum(acc_ffn[1], 0), dws_m)
    zero = jnp.zeros((D,), F32)

    def pack(prefix):
        a = lambda n: args[prefix + n]
        return _pack_small(zero, a("norm_mix_g"), a("gmlp_ln_g"), a("gmlp_ln_b"), a("gmlp_b_s"),
                           a("hgrn_lb_table"), a("hgrn_norm_g"), a("norm_ffn_g"), a("norm_final_g"),
                           a("gmlp_w_s"))

    packed = _small_allreduce_adamw(sp, pack(""), pack("m_"), pack("v_"))
    loss = packed[0][0, 0]
    small = [_unpack_small(p) for p in packed]
    for n in SMALL:
        out[n] = tuple(s[n] for s in small)
    return (loss, grad_x.reshape(x.shape), *[out[n][0] for n in ORDER], *[out[n][1] for n in ORDER],
            *[out[n][2] for n in ORDER], *[out[n][3] for n in ORDER])
```

```python
import functools
import math

import jax
import jax.numpy as jnp
from jax import lax
from jax.experimental import pallas as pl
from jax.experimental.pallas import tpu as pltpu

F32 = jnp.float32
BF16 = jnp.bfloat16
SDS = jax.ShapeDtypeStruct
MESH = pl.DeviceIdType.MESH
ANY = pl.BlockSpec(memory_space=pl.ANY)

D = 1024
NIN = 8
NG = 8
GCH = 128
NH = 8
HD = 128
HCH = 64
HGRN_HB = 4
HW = HGRN_HB * HD
DW_TOKENS = 2048
ELEMENTWISE_BLOCK_BYTES = 2 * 1024 * 1024
PROJ_OUT_SLOTS = 4
FF = 2816
FFS = 1408
NCHIP = 4
EPS = 1e-6
QSCALE = HD ** -0.5
GELU_C0 = math.sqrt(2.0 / math.pi)
GELU_C1 = 0.044715
LR, B1, B2, AEPS, WD, STEP = 0.001, 0.9, 0.999, 1e-08, 0.01, 10
VMEM_LIMIT_V7X = 56 * 1024 * 1024
SP_ROWS = 144


def _cparams(**kw):
    return pltpu.CompilerParams(vmem_limit_bytes=VMEM_LIMIT_V7X, **kw)


def _mm(a, b):
    return jnp.dot(a, b, preferred_element_type=F32)


def _mm_nt(a, b):
    return lax.dot_general(a, b, (((1,), (1,)), ((), ())), preferred_element_type=F32)


def _mm_tn(a, b):
    return lax.dot_general(a, b, (((0,), (0,)), ((), ())), preferred_element_type=F32)


def _rows8(x):
    r, c = x.shape
    return jnp.sum(x.reshape(r // 8, 8, c), axis=0)


def _mean(x):
    return jnp.mean(x, axis=-1, keepdims=True)


def _sigmoid(x):
    return 1.0 / (1.0 + jnp.exp(-x))


def _gelu(x):
    t = jnp.tanh(GELU_C0 * (x + GELU_C1 * x * x * x))
    return 0.5 * x * (1.0 + t), t


def _gelu_grad(x, t):
    return 0.5 * (1.0 + t) + 0.5 * x * (1.0 - t * t) * (GELU_C0 * (1.0 + 3.0 * GELU_C1 * x * x))


def _component_of(group):
    return jnp.where(group < 6, (group + 4) % 6, group)


def _proj_fwd(place, x, g_mix, w_in4, later):
    T = x.shape[0]
    tm = min(1024, T)
    ni = T // tm
    n = len(later)

    def body(pc_ref, x_ref, g_ref, *rest):
        proj_ref, h_ref, w_all = rest[1 + n:4 + n]
        gathered = rest[4 + n:4 + 2 * n]
        hs, wbuf, wsem, obuf, osem = rest[4 + 2 * n:9 + 2 * n]
        w_sems, later_sems = rest[9 + 2 * n:15 + 2 * n], rest[15 + 2 * n:]
        jp, i = pl.program_id(0), pl.program_id(1)
        w_cols = [w_all.at[:, :, pl.ds(k * D, D)] for k in range(2)]

        def w_copy(blk):
            cols = pl.ds(pl.multiple_of((blk % 2) * D, 128), D)
            return pltpu.make_async_copy(w_all.at[pc_ref[0] ^ (blk // 2), :, cols], wbuf.at[blk % 2],
                                         wsem.at[blk % 2])

        @pl.when((jp == 0) & (i == 0))
        def _():
            _gather_start(w_cols, w_sems)
            w_copy(jp).start()

        @pl.when(i == 0)
        def _():
            w_copy(jp).wait()

        @pl.when(jp == 0)
        def _():
            xv = x_ref[...]
            r = lax.rsqrt(_mean(xv * xv) + EPS)
            hb = (xv * r * g_ref[...]).astype(BF16)
            hs[i] = hb
            h_ref[...] = hb

        step = jp * ni + i
        slot = step % PROJ_OUT_SLOTS

        def o_copy(slot_):
            comp = 2 * (pc_ref[0] ^ (jp // 2)) + jp % 2
            return pltpu.make_async_copy(
                obuf.at[slot_], proj_ref.at[comp, pl.ds(pl.multiple_of(i * tm, 8), tm)], osem.at[slot_])

        @pl.when(step >= PROJ_OUT_SLOTS)
        def _():
            o_copy(slot).wait()

        obuf[slot] = _mm(hs[i], wbuf[jp % 2])
        o_copy(slot).start()

        @pl.when(step == NIN * ni - 1)
        def _():
            for k in range(PROJ_OUT_SLOTS):
                o_copy((slot + 1 + k) % PROJ_OUT_SLOTS).wait()

        for nxt in range(1, NIN):
            @pl.when((jp == nxt - 1) & (i == ni - 1))
            def _():
                if nxt >= 2:
                    _gather_land([w_cols[nxt % 2]], w_sems, nxt // 2, first=nxt % 2)
                if nxt == 5:
                    _gather_start(gathered, later_sems)
                if nxt == NIN - 1:
                    _gather_neighbours(gathered, later_sems)
                w_copy(jp + 1).start()

        @pl.when((jp == NIN - 1) & (i == ni - 1))
        def _():
            _gather_drain(w_cols, w_sems)
            _gather_finish(gathered, later_sems)

    tile = lambda jp, i, pc: (jnp.where(jp == 0, i, ni - 1), 0)
    res = pl.pallas_call(
        body, name="proj_fwd",
        grid_spec=pltpu.PrefetchScalarGridSpec(
            num_scalar_prefetch=1, grid=(NIN, ni),
            in_specs=[pl.BlockSpec((tm, D), tile), pl.BlockSpec((1, D), lambda jp, i, pc: (0, 0))] + [ANY] * (1 + n),
            out_specs=[ANY, pl.BlockSpec((tm, D), tile)] + [ANY] * (1 + n),
            scratch_shapes=[pltpu.VMEM((ni, tm, D), BF16), pltpu.VMEM((2, D, D), BF16),
                            pltpu.SemaphoreType.DMA((2,)), pltpu.VMEM((PROJ_OUT_SLOTS, tm, D), F32),
                            pltpu.SemaphoreType.DMA((PROJ_OUT_SLOTS,))] + _gather_sems(2) + _gather_sems(n)),
        out_shape=[SDS((NIN, T, D), F32), SDS((T, D), BF16), SDS(w_in4.shape, BF16)]
        + [SDS(a.shape, a.dtype) for a in later],
        input_output_aliases={3 + k: 2 + k for k in range(1 + n)},
        compiler_params=_cparams(has_side_effects=True),
    )(place, x, g_mix, w_in4, *later)
    return res[:2], res[2], res[3:]


def _layer_norm_stats(gv):
    mu = _mean(gv)
    xc = gv - mu
    rs = lax.rsqrt(_mean(xc * xc) + EPS)
    return xc * rs, rs


def _gmlp_fwd(proj, ln_g, ln_b, wm, b_t, job=None):
    T = proj.shape[1]
    tm = min(256, T)

    def body(u_ref, v_ref, lg_ref, lb_ref, wm_ref, bt_ref, a_ref, a_s):
        gu, _ = _gelu(u_ref[...])
        gv, _ = _gelu(v_ref[...])
        vhat, _ = _layer_norm_stats(gv)
        vnb = (vhat * lg_ref[...] + lb_ref[...]).astype(BF16)
        for ch in range(tm // GCH):
            rows = slice(GCH * ch, GCH * (ch + 1))
            for g in range(NG):
                cols = slice(128 * g, 128 * (g + 1))
                mixed = _mm(wm_ref[g], vnb[rows, cols]) + bt_ref[:, g:g + 1]
                a_s[rows, cols] = gu[rows, cols] * mixed
        a_ref[...] = a_s[...].astype(BF16)

    row = lambda i: (0, 0)
    return _call(
        body, name="gmlp_fwd", grid=(T // tm,), job=job, args=(proj, proj, ln_g, ln_b, wm, b_t),
        in_specs=[pl.BlockSpec((None, tm, D), lambda i: (0, i, 0)), pl.BlockSpec((None, tm, D), lambda i: (1, i, 0)),
                  pl.BlockSpec((1, D), row), pl.BlockSpec((1, D), row),
                  pl.BlockSpec((NG, GCH, GCH), lambda i: (0, 0, 0)), pl.BlockSpec((GCH, NG), row)],
        out_specs=[pl.BlockSpec((tm, D), lambda i: (i, 0))],
        out_shape=[SDS((T, D), BF16)],
        scratch_shapes=[pltpu.VMEM((tm, D), F32)])


def _cumsum64(x, row):
    for s in (1, 2, 4, 8, 16, 32):
        x = x + jnp.where(row >= s, pltpu.roll(x, s, 0), 0.0)
    return x


def _revcumsum64(x, row):
    n = x.shape[0]
    for s in (1, 2, 4, 8, 16, 32):
        x = x + jnp.where(row < HCH - s, pltpu.roll(x, n - s, 0), 0.0)
    return x


def _head_mean(x):
    parts = [jnp.broadcast_to(_mean(x[:, HD * h:HD * (h + 1)]), (x.shape[0], HD)) for h in range(x.shape[1] // HD)]
    return jnp.concatenate(parts, axis=1)


def _seg_sum(x):
    n, c = x.shape
    s = jnp.sum(x.reshape(n // HCH, HCH, c), axis=1, keepdims=True)
    return jnp.broadcast_to(s, (n // HCH, HCH, c)).reshape(n, c)


def _hgrn_gates(fl, lbv, row):
    s = _sigmoid(fl)
    f = lbv + (1.0 - lbv) * s
    a = _cumsum64(jnp.log(f), row)
    a_mid = _seg_sum(jnp.where(row == HCH // 2 - 1, a, 0.0))
    a_last = _seg_sum(jnp.where(row == HCH - 1, a, 0.0))
    return s, f, a, a_mid, a_last


def _hgrn_fwd(proj, lb_table, norm_g, job=None):
    T = proj.shape[1]
    tb = min(512, T)
    nc = tb // HCH

    def body(q_ref, fl_ref, v_ref, g_ref, lbt_ref, gn_ref, o_ref, ob_ref, stb_ref, st_s, o_s):
        @pl.when(pl.program_id(1) == 0)
        def _():
            st_s[...] = jnp.zeros_like(st_s)

        row = lax.broadcasted_iota(jnp.int32, (tb, HW), 0) & (HCH - 1)
        lbv = _sigmoid(lbt_ref[0:1, :] - lbt_ref[1:2, :])
        _, f, a, a_mid, a_last = _hgrn_gates(fl_ref[...], lbv, row)
        k = 1.0 - f
        qs = q_ref[...] * QSCALE
        q_in = (qs * jnp.exp(a - a_mid)).astype(BF16)
        k_in = (k * jnp.exp(a_mid - a)).astype(BF16)
        q_a = (qs * jnp.exp(a)).astype(BF16)
        k_d = (k * jnp.exp(a_last - a)).astype(BF16)
        dec = jnp.exp(a_last)
        vb = v_ref[...].astype(BF16)
        tri = (lax.broadcasted_iota(jnp.int32, (HCH, HCH), 0)
               >= lax.broadcasted_iota(jnp.int32, (HCH, HCH), 1))
        for c in range(nc):
            sl = slice(HCH * c, HCH * (c + 1))
            for hh in range(HGRN_HB):
                hs = slice(HD * hh, HD * (hh + 1))
                st = st_s[hh]
                stb_ref[hh, c] = st
                sc = jnp.where(tri, _mm_nt(q_in[sl, hs], k_in[sl, hs]), 0.0)
                o_s[sl, hs] = _mm(sc.astype(BF16), vb[sl, hs]) + _mm_nt(q_a[sl, hs], st.astype(BF16))
                d64 = dec[sl, hs]
                st_s[hh] = st * jnp.concatenate([d64, d64], axis=0) + _mm_tn(vb[sl, hs], k_d[sl, hs])
        o = o_s[...]
        r = lax.rsqrt(_head_mean(o * o) + EPS)
        g = g_ref[...]
        o_ref[...] = o
        ob_ref[...] = (o * r * gn_ref[...] * (g * _sigmoid(g))).astype(BF16)

    def col(off):
        return pl.BlockSpec((None, tb, HW), lambda h, cb: (off, cb, h))

    return _call(
        body, name="hgrn_fwd", grid=(NH // HGRN_HB, T // tb), job=job,
        args=(proj, proj, proj, proj, lb_table, norm_g),
        in_specs=[col(2), col(3), col(4), col(5),
                  pl.BlockSpec((2, HW), lambda h, cb: (0, h)), pl.BlockSpec((1, HW), lambda h, cb: (0, h))],
        out_specs=[pl.BlockSpec((tb, HW), lambda h, cb: (cb, h)), pl.BlockSpec((tb, HW), lambda h, cb: (cb, h)),
                   pl.BlockSpec((HGRN_HB, nc, HD, HD), lambda h, cb: (h, cb, 0, 0))],
        out_shape=[SDS((T, D), F32), SDS((T, D), BF16), SDS((NH, T // HCH, HD, HD), F32)],
        scratch_shapes=[pltpu.VMEM((HGRN_HB, HD, HD), F32), pltpu.VMEM((tb, HW), F32)])


def _merge_fwd(x, ab, ob, proj, w_a, w_b, w_out, job=None):
    T = x.shape[0]
    tm = min(512, T)

    def body(x_ref, ab_ref, ob_ref, ga_ref, gb_ref, wa_ref, wb_ref, wo_ref, mg_ref, x1_ref):
        ya = _mm(ab_ref[...], wa_ref[...])
        yb = _mm(ob_ref[...], wb_ref[...])
        merged = (_sigmoid(ga_ref[...]) * ya + _sigmoid(gb_ref[...]) * yb).astype(BF16)
        mg_ref[...] = merged
        x1_ref[...] = x_ref[...] + _mm(merged, wo_ref[...])

    t = lambda i: (i, 0)
    w = lambda i: (0, 0)
    return _call(
        body, name="merge_fwd", grid=(T // tm,), job=job, args=(x, ab, ob, proj, proj, w_a, w_b, w_out),
        in_specs=[pl.BlockSpec((tm, D), t), pl.BlockSpec((tm, D), t), pl.BlockSpec((tm, D), t),
                  pl.BlockSpec((None, tm, D), lambda i: (6, i, 0)), pl.BlockSpec((None, tm, D), lambda i: (7, i, 0)),
                  pl.BlockSpec((D, D), w), pl.BlockSpec((D, D), w), pl.BlockSpec((D, D), w)],
        out_specs=[pl.BlockSpec((tm, D), t)] * 2,
        out_shape=[SDS((T, D), BF16), SDS((T, D), F32)])


def _ffn_fwd_bwd(x1, target, g_ffn, g_fin, w_gu4, w_down):
    T = x1.shape[0]
    tm = min(256, T)
    inv_d = 1.0 / D

    def body(x1_ref, tg_ref, gf_ref, gn_ref, wgu_ref, wd_ref,
             act_ref, dx2b_ref, h2b_ref, dgu_ref, dx1_ref, dx1b_ref, acc_ref):
        @pl.when(pl.program_id(0) == 0)
        def _():
            acc_ref[...] = jnp.zeros_like(acc_ref)

        x1v = x1_ref[...]
        gf = gf_ref[...]
        gn = gn_ref[...]
        rr1 = lax.rsqrt(_mean(x1v * x1v) + EPS)
        x1n = x1v * rr1
        h2b = (x1n * gf).astype(BF16)
        h2b_ref[...] = h2b
        p = [_mm(h2b, wgu_ref[k]) for k in range(NCHIP)]
        sg = [_sigmoid(p[0]), _sigmoid(p[1])]
        si = [p[0] * sg[0], p[1] * sg[1]]
        x2 = x1v
        for k in range(2):
            actk = (si[k] * p[2 + k]).astype(BF16)
            act_ref[:, FFS * k:FFS * (k + 1)] = actk
            x2 = x2 + _mm(actk, wd_ref[FFS * k:FFS * (k + 1), :])
        rr2 = lax.rsqrt(_mean(x2 * x2) + EPS)
        x2n = x2 * rr2
        e = x2n * gn - tg_ref[...]
        acc_ref[0] += _rows8(e * e) * (0.5 * inv_d)
        dy = e * inv_d
        acc_ref[1] += _rows8(dy * x2n)
        dxn = dy * gn
        dx2 = rr2 * (dxn - x2n * _mean(dxn * x2n))
        dx2b = dx2.astype(BF16)
        dx2b_ref[...] = dx2b
        dh2 = None
        for k in range(2):
            dact = _mm_nt(dx2b, wd_ref[FFS * k:FFS * (k + 1), :])
            dgate = (dact * p[2 + k] * (sg[k] * (1.0 + p[k] * (1.0 - sg[k])))).astype(BF16)
            dup = (dact * si[k]).astype(BF16)
            dgu_ref[k] = dgate
            dgu_ref[2 + k] = dup
            part = _mm_nt(dgate, wgu_ref[k]) + _mm_nt(dup, wgu_ref[2 + k])
            dh2 = part if dh2 is None else dh2 + part
        acc_ref[2] += _rows8(dh2 * x1n)
        dxn1 = dh2 * gf
        dx1 = dx2 + rr1 * (dxn1 - x1n * _mean(dxn1 * x1n))
        dx1_ref[...] = dx1
        dx1b_ref[...] = dx1.astype(BF16)

    t = lambda i: (i, 0)
    w = lambda i: (0, 0)
    one = pl.Buffered(1)
    return pl.pallas_call(
        body, name="ffn_fwd_bwd", grid=(T // tm,),
        in_specs=[pl.BlockSpec((tm, D), t), pl.BlockSpec((tm, D), t),
                  pl.BlockSpec((1, D), w), pl.BlockSpec((1, D), w),
                  pl.BlockSpec((NCHIP, D, FFS), lambda i: (0, 0, 0), pipeline_mode=one),
                  pl.BlockSpec((FF, D), w, pipeline_mode=one)],
        out_specs=[pl.BlockSpec((tm, FF), t), pl.BlockSpec((tm, D), t), pl.BlockSpec((tm, D), t),
                   pl.BlockSpec((NCHIP, tm, FFS), lambda i: (0, i, 0)),
                   pl.BlockSpec((tm, D), t), pl.BlockSpec((tm, D), t),
                   pl.BlockSpec((3, 8, D), lambda i: (0, 0, 0))],
        out_shape=[SDS((T, FF), BF16), SDS((T, D), BF16), SDS((T, D), BF16),
                   SDS((NCHIP, T, FFS), BF16), SDS((T, D), F32), SDS((T, D), BF16),
                   SDS((3, 8, D), F32)],
        compiler_params=_cparams(),
    )(x1, target, g_ffn, g_fin, w_gu4, w_down)


def _merge_bwd(dx1b, ab, ob, proj, w_out, w_a, w_b, job=None):
    T = dx1b.shape[0]
    tm = min(512, T)

    def body(dx_ref, ab_ref, ob_ref, ga_ref, gb_ref, wo_ref, wa_ref, wb_ref, dya_ref, dyb_ref, dp_ref):
        dm = _mm_nt(dx_ref[...], wo_ref[...])
        sa = _sigmoid(ga_ref[...])
        sb = _sigmoid(gb_ref[...])
        dya_ref[...] = (dm * sa).astype(BF16)
        dyb_ref[...] = (dm * sb).astype(BF16)
        dp_ref[0] = (dm * _mm(ab_ref[...], wa_ref[...]) * sa * (1.0 - sa)).astype(BF16)
        dp_ref[1] = (dm * _mm(ob_ref[...], wb_ref[...]) * sb * (1.0 - sb)).astype(BF16)

    t = lambda i: (i, 0)
    w = lambda i: (0, 0)
    return _call(
        body, name="merge_bwd", grid=(T // tm,),
        in_specs=[pl.BlockSpec((tm, D), t), pl.BlockSpec((tm, D), t), pl.BlockSpec((tm, D), t),
                  pl.BlockSpec((None, tm, D), lambda i: (6, i, 0)), pl.BlockSpec((None, tm, D), lambda i: (7, i, 0)),
                  pl.BlockSpec((D, D), w), pl.BlockSpec((D, D), w), pl.BlockSpec((D, D), w)],
        out_specs=[pl.BlockSpec((tm, D), t)] * 2 + [pl.BlockSpec((2, tm, D), lambda i: (3, i, 0))],
        out_shape=[SDS((T, D), BF16), SDS((T, D), BF16), SDS((NIN, T, D), BF16)],
        args=(dx1b, ab, ob, proj, proj, w_out, w_a, w_b), job=job)


def _hgrn_bwd(dproj, dyb, w_b, o_raw, proj, st_before, lb_table, norm_g, job=None):
    T = dyb.shape[0]
    tb = min(512, T)
    nc = tb // HCH
    nb = T // tb

    def body(dp_in, dyb_ref, wb_ref, o_ref, q_ref, fl_ref, v_ref, g_ref, stb_ref, lbt_ref, gn_ref,
             dp_ref, acc_ref, dst_s, dqin_s, dqa_s, dkin_s, dkd_s, dv_s, ddec_s):
        del dp_in

        @pl.when(pl.program_id(1) == 0)
        def _():
            dst_s[...] = jnp.zeros_like(dst_s)
            acc_ref[...] = jnp.zeros_like(acc_ref)

        row = lax.broadcasted_iota(jnp.int32, (tb, HW), 0) & (HCH - 1)
        gn = gn_ref[...]
        lbv = _sigmoid(lbt_ref[0:1, :] - lbt_ref[1:2, :])
        o = o_ref[...]
        r = lax.rsqrt(_head_mean(o * o) + EPS)
        on = o * r
        g = g_ref[...]
        sgm = _sigmoid(g)
        dob_v = _mm_nt(dyb_ref[...], wb_ref[...])
        dp_ref[3] = (dob_v * on * gn * (sgm * (1.0 + g * (1.0 - sgm)))).astype(BF16)
        do_n = dob_v * (g * sgm)
        acc_ref[1] += _rows8(do_n * on)
        dxn = do_n * gn
        do = (r * (dxn - on * _head_mean(dxn * on))).astype(BF16)
        s, f, a, a_mid, a_last = _hgrn_gates(fl_ref[...], lbv, row)
        k = 1.0 - f
        qs = q_ref[...] * QSCALE
        e_q = jnp.exp(a - a_mid)
        e_k = jnp.exp(a_mid - a)
        e_a = jnp.exp(a)
        e_l = jnp.exp(a_last - a)
        dec = jnp.exp(a_last)
        q_in = qs * e_q
        k_in = k * e_k
        q_a = qs * e_a
        k_d = k * e_l
        q_inb, k_inb, q_ab, k_db = (z.astype(BF16) for z in (q_in, k_in, q_a, k_d))
        vb = v_ref[...].astype(BF16)
        tri = (lax.broadcasted_iota(jnp.int32, (HCH, HCH), 0)
               >= lax.broadcasted_iota(jnp.int32, (HCH, HCH), 1))
        for c in reversed(range(nc)):
            sl = slice(HCH * c, HCH * (c + 1))
            for hh in range(HGRN_HB):
                hs = slice(HD * hh, HD * (hh + 1))
                stp = stb_ref[hh, c]
                dst = dst_s[hh]
                dstb = dst.astype(BF16)
                do_c = do[sl, hs]
                v_c = vb[sl, hs]
                dqa_s[sl, hs] = _mm(do_c, stp.astype(BF16))
                dkd_s[sl, hs] = _mm(v_c, dstb)
                ddec_s[sl, hs] = jnp.broadcast_to(jnp.sum(dst * stp, axis=0, keepdims=True), (HCH, HD))
                sc = jnp.where(tri, _mm_nt(q_inb[sl, hs], k_inb[sl, hs]), 0.0).astype(BF16)
                dsc = jnp.where(tri, _mm_nt(do_c, v_c), 0.0).astype(BF16)
                dv_s[sl, hs] = _mm_nt(k_db[sl, hs], dstb) + _mm_tn(sc, do_c)
                dqin_s[sl, hs] = _mm(dsc, k_inb[sl, hs])
                dkin_s[sl, hs] = _mm_tn(dsc, q_inb[sl, hs])
                d64 = dec[sl, hs]
                dst_s[hh] = dst * jnp.concatenate([d64, d64], axis=0) + _mm_tn(do_c, q_ab[sl, hs])
        dq_in = dqin_s[...]
        dq_a = dqa_s[...]
        dk_in = dkin_s[...]
        dk_d = dkd_s[...]
        dp_ref[0] = ((dq_in * e_q + dq_a * e_a) * QSCALE).astype(BF16)
        dp_ref[2] = dv_s[...].astype(BF16)
        tq = dq_in * q_in
        tk = dk_in * k_in
        td = dk_d * k_d
        d_a = tq + dq_a * q_a - tk - td
        d_a = d_a + jnp.where(row == HCH // 2 - 1, _seg_sum(tk - tq), 0.0)
        d_a = d_a + jnp.where(row == HCH - 1, _seg_sum(td) + ddec_s[...] * dec, 0.0)
        dlf = _revcumsum64(d_a, row)
        df = dlf / f - (dk_in * e_k + dk_d * e_l)
        dp_ref[1] = (df * (1.0 - lbv) * s * (1.0 - s)).astype(BF16)
        acc_ref[0] += _rows8(df * (1.0 - s))

    def col(off):
        return pl.BlockSpec((None, tb, HW), lambda h, cb: (off, nb - 1 - cb, h))

    hb = lambda h, cb: (nb - 1 - cb, h)
    return _call(
        body, name="hgrn_bwd", grid=(NH // HGRN_HB, nb), job=job,
        args=(dproj, dyb, w_b, o_raw, proj, proj, proj, proj, st_before, lb_table, norm_g),
        in_specs=[ANY, pl.BlockSpec((tb, D), lambda h, cb: (nb - 1 - cb, 0)),
                  pl.BlockSpec((HW, D), lambda h, cb: (h, 0)), pl.BlockSpec((tb, HW), hb),
                  col(2), col(3), col(4), col(5),
                  pl.BlockSpec((HGRN_HB, nc, HD, HD), lambda h, cb: (h, nb - 1 - cb, 0, 0)),
                  pl.BlockSpec((2, HW), lambda h, cb: (0, h)), pl.BlockSpec((1, HW), lambda h, cb: (0, h))],
        out_specs=[pl.BlockSpec((4, tb, HW), lambda h, cb: (0, nb - 1 - cb, h)),
                   pl.BlockSpec((2, 8, HW), lambda h, cb: (0, 0, h))],
        out_shape=[SDS(dproj.shape, BF16), SDS((2, 8, D), F32)],
        scratch_shapes=[pltpu.VMEM((HGRN_HB, HD, HD), F32)] + [pltpu.VMEM((tb, HW), F32)] * 6,
        aliases={0: 0})


def _gmlp_bwd(dproj, dya, w_a, proj, ln_g, ln_b, wm, wm_t, b_t):
    T = dya.shape[0]
    tm = min(256, T)

    def body(dp_in, dya_ref, wa_ref, u_ref, v_ref, lg_ref, lb_ref, wm_ref, wmt_ref, bt_ref,
             dp_ref, acc_ref, dws_ref, dmix_ref, du_s, dvn_s):
        del dp_in

        @pl.when(pl.program_id(0) == 0)
        def _():
            acc_ref[...] = jnp.zeros_like(acc_ref)
            dws_ref[...] = jnp.zeros_like(dws_ref)
            dmix_ref[...] = jnp.zeros_like(dmix_ref)

        u = u_ref[...]
        v = v_ref[...]
        lg = lg_ref[...]
        gu, t_u = _gelu(u)
        gv, t_v = _gelu(v)
        vhat, rs = _layer_norm_stats(gv)
        vnb = (vhat * lg + lb_ref[...]).astype(BF16)
        da_v = _mm_nt(dya_ref[...], wa_ref[...])
        for ch in range(tm // GCH):
            rows = slice(GCH * ch, GCH * (ch + 1))
            for g in range(NG):
                cols = slice(128 * g, 128 * (g + 1))
                vng = vnb[rows, cols]
                mixed = _mm(wm_ref[g], vng) + bt_ref[:, g:g + 1]
                dag = da_v[rows, cols]
                dmx = dag * gu[rows, cols]
                du_s[rows, cols] = dag * mixed
                dmxb = dmx.astype(BF16)
                dws_ref[:, cols] += _mm_nt(dmxb, vng)
                dmix_ref[:, cols] += dmx
                dvn_s[rows, cols] = _mm(wmt_ref[g], dmxb)
        dp_ref[0] = (du_s[...] * _gelu_grad(u, t_u)).astype(BF16)
        dvn = dvn_s[...]
        acc_ref[0] += _rows8(dvn * vhat)
        acc_ref[1] += _rows8(dvn)
        dvh = dvn * lg
        dgv = rs * (dvh - _mean(dvh) - vhat * _mean(dvh * vhat))
        dp_ref[1] = (dgv * _gelu_grad(v, t_v)).astype(BF16)

    row = lambda i: (0, 0)
    w3 = lambda i: (0, 0, 0)
    return pl.pallas_call(
        body, name="gmlp_bwd", grid=(T // tm,),
        in_specs=[ANY, pl.BlockSpec((tm, D), lambda i: (i, 0)), pl.BlockSpec((D, D), row),
                  pl.BlockSpec((None, tm, D), lambda i: (0, i, 0)), pl.BlockSpec((None, tm, D), lambda i: (1, i, 0)),
                  pl.BlockSpec((1, D), row), pl.BlockSpec((1, D), row),
                  pl.BlockSpec((NG, GCH, GCH), w3), pl.BlockSpec((NG, GCH, GCH), w3),
                  pl.BlockSpec((GCH, NG), row)],
        out_specs=[pl.BlockSpec((2, tm, D), lambda i: (2, i, 0)),
                   pl.BlockSpec((2, 8, D), w3), pl.BlockSpec((GCH, D), row), pl.BlockSpec((GCH, D), row)],
        out_shape=[SDS(dproj.shape, BF16), SDS((2, 8, D), F32), SDS((GCH, D), F32), SDS((GCH, D), F32)],
        scratch_shapes=[pltpu.VMEM((tm, D), F32), pltpu.VMEM((tm, D), F32)],
        input_output_aliases={0: 0},
        compiler_params=_cparams(),
    )(dproj, dya, w_a, proj, proj, ln_g, ln_b, wm, wm_t, b_t)


def _proj_bwd(dproj, w_in4, x, dx1, g_mix, job=None):
    T = x.shape[0]
    tm = min(256, T)
    order = (2, 3, 4, 5, 0, 1, 6, 7)

    def body(dp_ref, w_ref, x_ref, dx1_ref, g_ref, gx_ref, acc_ref):
        @pl.when(pl.program_id(0) == 0)
        def _():
            acc_ref[...] = jnp.zeros_like(acc_ref)

        dh = None
        for m, og in enumerate(order):
            part = _mm_nt(dp_ref[m], w_ref[og // 2, :, D * (og % 2):D * (og % 2 + 1)])
            dh = part if dh is None else dh + part
        xv = x_ref[...]
        r = lax.rsqrt(_mean(xv * xv) + EPS)
        xn = xv * r
        acc_ref[...] += _rows8(dh * xn)
        dxn = dh * g_ref[...]
        gx_ref[...] = dx1_ref[...] + r * (dxn - xn * _mean(dxn * xn))

    t = lambda i: (i, 0)
    return _call(
        body, name="proj_bwd", grid=(T // tm,),
        in_specs=[pl.BlockSpec((NIN, tm, D), lambda i: (0, i, 0)),
                  pl.BlockSpec((NCHIP, D, 2 * D), lambda i: (0, 0, 0), pipeline_mode=pl.Buffered(1)),
                  pl.BlockSpec((tm, D), t), pl.BlockSpec((tm, D), t), pl.BlockSpec((1, D), lambda i: (0, 0))],
        out_specs=[pl.BlockSpec((tm, D), t), pl.BlockSpec((8, D), lambda i: (0, 0))],
        out_shape=[SDS((T, D), F32), SDS((8, D), F32)],
        args=(dproj, w_in4, x, dx1, g_mix), job=job)


def _dw_call(name, a, b, a_spec, b_spec, o_spec, out_shape, nblk, tt, job=None, prefetch=None):
    T = a.shape[-2]

    def body(*refs):
        a_ref, b_ref, o_ref = refs[-3:]

        @pl.when(pl.program_id(1) == 0)
        def _():
            o_ref[...] = jnp.zeros_like(o_ref)
        o_ref[...] += _mm_tn(a_ref[...], b_ref[...])

    (out,), job_out = _call(
        body, name=name, grid=(nblk, T // tt), in_specs=[a_spec, b_spec], out_specs=[o_spec],
        out_shape=[out_shape], args=(a, b), job=job, prefetch=prefetch)
    return out, job_out


def _dw_in_half(name, place, hb, dproj, mine, job=None):
    tt = min(DW_TOKENS, hb.shape[0])

    def comp(k, pc):
        return _component_of(2 * k + (pc[1] if mine else 1 - pc[1]))

    return _dw_call(
        name, hb, dproj,
        pl.BlockSpec((tt, D), lambda k, t, pc: (t, 0)),
        pl.BlockSpec((None, tt, D), lambda k, t, pc: (comp(k, pc), t, 0)),
        pl.BlockSpec((None, D, D), lambda k, t, pc: (k, 0, 0)),
        SDS((NCHIP, D, D), F32), NCHIP, tt, job, place)


def _dw_gate_up(h2b, dgu4, job=None):
    tt = min(DW_TOKENS, h2b.shape[0])
    return _dw_call(
        "dw_gate_up", h2b, dgu4,
        pl.BlockSpec((tt, D), lambda k, t: (t, 0)),
        pl.BlockSpec((None, tt, FFS), lambda k, t: (k, t, 0)),
        pl.BlockSpec((None, D, FFS), lambda k, t: (k, 0, 0)),
        SDS((NCHIP, D, FFS), F32), NCHIP, tt, job)


def _dw_down(act, dx2b, job=None):
    tt = min(DW_TOKENS, act.shape[0])
    g, job_out = _dw_call(
        "dw_down", act, dx2b,
        pl.BlockSpec((tt, FFS), lambda k, t: (t, k)),
        pl.BlockSpec((tt, D), lambda k, t: (t, 0)),
        pl.BlockSpec((FFS, D), lambda k, t: (k, 0)),
        SDS((FF, D), F32), 2, tt, job)
    return g.reshape(NCHIP, FF // NCHIP, D), job_out


def _dw_square(name, a, b, job=None):
    tt = min(DW_TOKENS, a.shape[0])
    g, job_out = _dw_call(
        name, a, b,
        pl.BlockSpec((tt, D), lambda k, t: (t, 0)), pl.BlockSpec((tt, D), lambda k, t: (t, 0)),
        pl.BlockSpec((D, D), lambda k, t: (0, 0)), SDS((D, D), F32), 1, tt, job)
    return g.reshape(NCHIP, D // NCHIP, D), job_out


def _place():
    x, y, c = lax.axis_index("x"), lax.axis_index("y"), lax.axis_index("c")
    return x, y, c, 2 * x + y


def _chip_at(x, y, s):
    return x ^ (s >> 1), y ^ (s & 1)


class _Job:
    def __init__(self, ins, out_shapes, sems, start, finish, aliases=None, mid=None):
        self.ins, self.out_shapes, self.sems = list(ins), list(out_shapes), list(sems)
        self.start, self.finish, self.aliases = start, finish, dict(aliases or {})
        self.mid = mid if mid is not None else (lambda ins, outs, sems: None)


def _join_jobs(*jobs):
    def cut(refs, sizes):
        out, at = [], 0
        for n in sizes:
            out.append(refs[at:at + n])
            at += n
        return out

    ni = [len(j.ins) for j in jobs]
    no = [len(j.out_shapes) for j in jobs]
    ns = [len(j.sems) for j in jobs]

    def run(which):
        def go(ins, outs, sems):
            for j, a, b, c in zip(jobs, cut(ins, ni), cut(outs, no), cut(sems, ns)):
                getattr(j, which)(a, b, c)
        return go

    aliases = {}
    for k, j in enumerate(jobs):
        for a, b in j.aliases.items():
            aliases[sum(ni[:k]) + a] = sum(no[:k]) + b
    return _Job([a for j in jobs for a in j.ins], [o for j in jobs for o in j.out_shapes],
                [s for j in jobs for s in j.sems], run("start"), run("finish"), aliases, run("mid"))


def _call(body, *, name, grid, in_specs, out_specs, out_shape, args, scratch_shapes=(), aliases=None,
          job=None, prefetch=None):
    n_in, n_out, n_scr = len(in_specs), len(out_specs), len(scratch_shapes)
    npf = 0 if prefetch is None else 1
    job = job if job is not None else _Job([], [], [], lambda *a: None, lambda *a: None)
    ji, jo = len(job.ins), len(job.out_shapes)
    steps = math.prod(grid)

    def wrapped(*refs):
        pf, refs = refs[:npf], refs[npf:]
        ins, jin = refs[:n_in], refs[n_in:n_in + ji]
        o0 = n_in + ji
        outs, jout = refs[o0:o0 + n_out], refs[o0 + n_out:o0 + n_out + jo]
        s0 = o0 + n_out + jo
        scr, jsem = refs[s0:s0 + n_scr], refs[s0 + n_scr:]
        step = functools.reduce(lambda acc, ag: acc * ag[1] + pl.program_id(ag[0]), enumerate(grid), 0)
        if ji or jo:
            @pl.when(step == 0)
            def _():
                job.start(jin, jout, jsem)

        body(*pf, *ins, *outs, *scr)

        if ji or jo:
            @pl.when(step == steps // 2)
            def _():
                job.mid(jin, jout, jsem)

            @pl.when(step == steps - 1)
            def _():
                job.finish(jin, jout, jsem)

    io = {npf + a: b for a, b in dict(aliases or {}).items()}
    io.update({npf + n_in + a: n_out + b for a, b in job.aliases.items()})
    kw = dict(in_specs=list(in_specs) + [ANY] * ji, out_specs=list(out_specs) + [ANY] * jo,
              scratch_shapes=list(scratch_shapes) + job.sems)
    if npf:
        kw = dict(grid_spec=pltpu.PrefetchScalarGridSpec(num_scalar_prefetch=1, grid=grid, **kw))
    else:
        kw["grid"] = grid
    res = pl.pallas_call(
        wrapped, name=name, out_shape=list(out_shape) + job.out_shapes, input_output_aliases=io,
        compiler_params=_cparams(has_side_effects=bool(ji or jo)), **kw,
    )(*(() if prefetch is None else (prefetch,)), *args, *job.ins)
    return list(res[:n_out]), list(res[n_out:])


def _cast_shard(name, place, w):
    rows, cols = w.shape
    tr = 352 if rows % 352 == 0 else 256

    def body(pc_ref, w_ref, o_ref):
        del pc_ref
        o_ref[...] = w_ref[...].astype(BF16)

    return pl.pallas_call(
        body, name=name,
        grid_spec=pltpu.PrefetchScalarGridSpec(
            num_scalar_prefetch=1, grid=(rows // tr,),
            in_specs=[pl.BlockSpec((tr, cols), lambda i, pc: (i, 0))],
            out_specs=pl.BlockSpec((None, tr, cols), lambda i, pc: (pc[0], i, 0))),
        out_shape=SDS((NCHIP, rows, cols), BF16),
        compiler_params=_cparams(),
    )(place, w)


def _sibling_copy(ref, send_sem, recv_sem):
    x, y, c, _ = _place()
    return pltpu.make_async_remote_copy(src_ref=ref, dst_ref=ref, send_sem=send_sem, recv_sem=recv_sem,
                                        device_id=(x, y, 1 - c), device_id_type=MESH)


def _half_rows(arr, slot, core):
    half = arr.shape[1] // 2
    return arr.at[slot, pl.ds(pl.multiple_of(core * half, 16), half)]


def _quarter_rows(arr, slot, core, q):
    quarter = arr.shape[1] // 4
    return arr.at[slot, pl.ds(pl.multiple_of((2 * core + q) * quarter, 16), quarter)]


def _chip_copy(ref, dist, send_sem, recv_sem):
    x, y, c, _ = _place()
    cx, cy = _chip_at(x, y, dist)
    return pltpu.make_async_remote_copy(src_ref=ref, dst_ref=ref, send_sem=send_sem, recv_sem=recv_sem,
                                        device_id=(cx, cy, c), device_id_type=MESH)


def _gather_sems(n):
    dma = pltpu.SemaphoreType.DMA
    return [dma((n, 2))] * 4 + [dma((n, 4))] * 2


def _gather_start(arrs, sems):
    dsend, drecv = sems[0], sems[1]
    _, _, c, j = _place()
    for w, arr in enumerate(arrs):
        for dist in (1, 2):
            _chip_copy(_half_rows(arr, j, c), dist, dsend.at[w, dist - 1], drecv.at[w, dist - 1]).start()


def _gather_land(arrs, sems, dist, first=0):
    dsend, drecv, rsend, rrecv, fsend, frecv = sems
    _, _, c, j = _place()
    if dist < 3:
        other = 3 - dist
        for w, arr in enumerate(arrs, first):
            landed = _half_rows(arr, j ^ dist, c)
            _chip_copy(landed, dist, dsend.at[w, dist - 1], drecv.at[w, dist - 1]).wait_recv()
            relay = _quarter_rows(arr, j ^ dist, c, other - 1)
            _chip_copy(relay, other, rsend.at[w, other - 1], rrecv.at[w, other - 1]).start()
            _sibling_copy(landed, fsend.at[w, dist - 1], frecv.at[w, dist - 1]).start()
        for w, arr in enumerate(arrs, first):
            theirs = _half_rows(arr, j ^ dist, 1 - c)
            _sibling_copy(theirs, fsend.at[w, dist - 1], frecv.at[w, dist - 1]).wait_recv()
    else:
        for w, arr in enumerate(arrs, first):
            for via in (1, 2):
                piece = _quarter_rows(arr, j ^ 3, c, via - 1)
                _chip_copy(piece, via, rsend.at[w, via - 1], rrecv.at[w, via - 1]).wait_recv()
                _sibling_copy(piece, fsend.at[w, 1 + via], frecv.at[w, 1 + via]).start()
        for w, arr in enumerate(arrs, first):
            for via in (1, 2):
                theirs = _quarter_rows(arr, j ^ 3, 1 - c, via - 1)
                _sibling_copy(theirs, fsend.at[w, 1 + via], frecv.at[w, 1 + via]).wait_recv()


def _gather_drain(arrs, sems):
    dsend, drecv, rsend, rrecv, fsend, frecv = sems
    _, _, c, j = _place()
    for w, arr in enumerate(arrs):
        for dist in (1, 2):
            other = 3 - dist
            _chip_copy(_half_rows(arr, j, c), dist, dsend.at[w, dist - 1], drecv.at[w, dist - 1]).wait_send()
            _chip_copy(_quarter_rows(arr, j ^ dist, c, other - 1), other,
                       rsend.at[w, other - 1], rrecv.at[w, other - 1]).wait_send()
            _sibling_copy(_half_rows(arr, j ^ dist, c), fsend.at[w, dist - 1], frecv.at[w, dist - 1]).wait_send()
            _sibling_copy(_quarter_rows(arr, j ^ 3, c, dist - 1),
                          fsend.at[w, 1 + dist], frecv.at[w, 1 + dist]).wait_send()


def _gather_neighbours(arrs, sems):
    _gather_land(arrs, sems, 1)
    _gather_land(arrs, sems, 2)


def _gather_finish(arrs, sems):
    _gather_land(arrs, sems, 3)
    _gather_drain(arrs, sems)


def _gather_job(arrs):
    n = len(arrs)
    return _Job(arrs, [SDS(a.shape, a.dtype) for a in arrs], _gather_sems(n),
                lambda ins, outs, sems: _gather_start(outs, sems),
                lambda ins, outs, sems: _gather_finish(outs, sems), {k: k for k in range(n)},
                mid=lambda ins, outs, sems: _gather_neighbours(outs, sems))


def _exchange_job(arrs, out_shapes, n, copies):
    def start(ins, outs, sems):
        for cp in copies(ins, outs, sems[0], sems[1]):
            cp.start()

    def finish(ins, outs, sems):
        for cp in copies(ins, outs, sems[0], sems[1]):
            cp.wait()

    return _Job(arrs, out_shapes, [pltpu.SemaphoreType.DMA((n,))] * 2, start, finish)


def _pair_exchange_job(grads):
    def copies(ins, outs, send_sem, recv_sem):
        x, y, c, _ = _place()
        res = []
        for w in range(len(grads)):
            half = ins[w].shape[1] // 2
            theirs = pl.ds(pl.multiple_of((1 - c) * half, 8), half)
            res.append(pltpu.make_async_remote_copy(
                src_ref=ins[w].at[:, theirs, :], dst_ref=outs[w], send_sem=send_sem.at[w],
                recv_sem=recv_sem.at[w], device_id=(x, y, 1 - c), device_id_type=MESH))
        return res

    return _exchange_job(grads, [SDS((NCHIP, g.shape[1] // 2, g.shape[2]), F32) for g in grads],
                         len(grads), copies)


def _row_tile(rows, cols):
    tr = rows
    while tr * cols * 4 > ELEMENTWISE_BLOCK_BYTES and tr % 32 == 0:
        tr //= 2
    return tr


def _pair_sum(name, place, g, sib):
    half, cols = sib.shape[1], sib.shape[2]
    tr = _row_tile(half, cols)
    nt = half // tr
    mine = nt if g.shape[1] == 2 * half else 0

    def body(pc_ref, g_ref, s_ref, own_ref, out_ref):
        del pc_ref
        v = g_ref[...] + s_ref[...]
        out_ref[...] = v.astype(BF16)

        @pl.when(pl.program_id(1) == 0)
        def _():
            own_ref[...] = v

    return pl.pallas_call(
        body, name=name,
        grid_spec=pltpu.PrefetchScalarGridSpec(
            num_scalar_prefetch=1, grid=(nt, NCHIP),
            in_specs=[pl.BlockSpec((None, tr, cols), lambda i, s, pc: (pc[0] ^ s, pc[1] * mine + i, 0)),
                      pl.BlockSpec((None, tr, cols), lambda i, s, pc: (pc[0] ^ s, i, 0))],
            out_specs=[pl.BlockSpec((tr, cols), lambda i, s, pc: (i, 0)),
                       pl.BlockSpec((None, tr, cols), lambda i, s, pc: (s, i, 0))]),
        out_shape=[SDS((half, cols), F32), SDS((NCHIP, half, cols), BF16)],
        compiler_params=_cparams(),
    )(place, g, sib)


def _chip_exchange_job(parts):
    def copies(ins, outs, send_sem, recv_sem):
        x, y, c, _ = _place()
        res = []
        for w in range(len(parts)):
            for s in range(1, NCHIP):
                cx, cy = _chip_at(x, y, s)
                k = w * (NCHIP - 1) + s - 1
                res.append(pltpu.make_async_remote_copy(
                    src_ref=ins[w].at[s], dst_ref=outs[w].at[s - 1], send_sem=send_sem.at[k],
                    recv_sem=recv_sem.at[k], device_id=(cx, cy, c), device_id_type=MESH))
        return res

    return _exchange_job(parts, [SDS((NCHIP - 1,) + p.shape[1:], BF16) for p in parts],
                         len(parts) * (NCHIP - 1), copies)


def _chip_sum(name, own, rem):
    half, cols = own.shape
    tr = _row_tile(half, cols)

    def body(own_ref, rem_ref, out_ref):
        out_ref[...] = ((own_ref[...] + rem_ref[0].astype(F32)) + rem_ref[1].astype(F32)) + rem_ref[2].astype(F32)

    return pl.pallas_call(
        body, name=name, grid=(half // tr,),
        in_specs=[pl.BlockSpec((tr, cols), lambda i: (i, 0)),
                  pl.BlockSpec((NCHIP - 1, tr, cols), lambda i: (0, i, 0))],
        out_specs=pl.BlockSpec((tr, cols), lambda i: (i, 0)),
        out_shape=SDS((half, cols), F32),
        compiler_params=_cparams(),
    )(own, rem)


def _share_halves_job(halves):
    def copies(ins, outs, send_sem, recv_sem):
        x, y, c, _ = _place()
        return [pltpu.make_async_remote_copy(
            src_ref=ins[w], dst_ref=outs[w], send_sem=send_sem.at[w], recv_sem=recv_sem.at[w],
            device_id=(x, y, 1 - c), device_id_type=MESH) for w in range(len(halves))]

    return _exchange_job(halves, [SDS(h.shape, F32) for h in halves], len(halves), copies)


def _adamw_math(w, g, m, v):
    m = B1 * m + (1.0 - B1) * g
    v = B2 * v + (1.0 - B2) * (g * g)
    m_hat = m / (1.0 - B1 ** STEP)
    v_hat = v / (1.0 - B2 ** STEP)
    delta = -LR * (m_hat / (jnp.sqrt(v_hat) + AEPS) + WD * w)
    return delta, m, v


def _adamw(name, place, w, own, sib, m, v):
    rows, cols = w.shape
    by_cols = own.shape[0] == rows
    half, pc_cols = (rows, cols // 2) if by_cols else (rows // 2, cols)
    tr = _row_tile(half, pc_cols)
    nt = half // tr

    def body(pc_ref, w_ref, own_ref, sib_ref, m_ref, v_ref, g_ref, d_ref, mo_ref, vo_ref):
        g = jnp.where(pl.program_id(0) == pc_ref[1], own_ref[...], sib_ref[...])
        d, mn, vn = _adamw_math(w_ref[...], g, m_ref[...], v_ref[...])
        g_ref[...] = g
        d_ref[...] = d
        mo_ref[...] = mn
        vo_ref[...] = vn

    full = pl.BlockSpec((tr, pc_cols), (lambda h, i, pc: (i, h)) if by_cols else (lambda h, i, pc: (h * nt + i, 0)))
    part = pl.BlockSpec((tr, pc_cols), lambda h, i, pc: (i, 0))
    return pl.pallas_call(
        body, name=name,
        grid_spec=pltpu.PrefetchScalarGridSpec(
            num_scalar_prefetch=1, grid=(2, nt),
            in_specs=[full, part, part, full, full], out_specs=[full] * 4),
        out_shape=[SDS((rows, cols), F32)] * 4,
        compiler_params=_cparams(),
    )(place, w, own, sib, m, v)


def _small_allreduce_adamw(sp, w, m, v, job):
    shape = sp.shape
    ji, jo = len(job.ins), len(job.out_shapes)

    def body(sp_ref, w_ref, m_ref, v_ref, *rest):
        jin, (g_ref, d_ref, mo_ref, vo_ref), jout = rest[:ji], rest[ji:ji + 4], rest[ji + 4:ji + 4 + jo]
        sib_s, pair_s, chip_s, send_sem, recv_sem = rest[ji + 4 + jo:ji + 9 + jo]
        jsem = rest[ji + 9 + jo:]
        job.start(jin, jout, jsem)
        x, y, c, j = _place()
        cp = pltpu.make_async_remote_copy(
            src_ref=sp_ref, dst_ref=sib_s, send_sem=send_sem.at[0], recv_sem=recv_sem.at[0],
            device_id=(x, y, 1 - c), device_id_type=MESH)
        cp.start()
        cp.wait()
        pair_s[...] = sp_ref[...] + sib_s[...]
        half = shape[0] // 2
        mine = pl.ds(pl.multiple_of(c * half, 8), half)
        cps = []
        for s in range(1, NCHIP):
            cx, cy = _chip_at(x, y, s)
            cp = pltpu.make_async_remote_copy(
                src_ref=pair_s.at[mine], dst_ref=chip_s.at[s, mine], send_sem=send_sem.at[s],
                recv_sem=recv_sem.at[s], device_id=(cx, cy, c), device_id_type=MESH)
            cp.start()
            cps.append(cp)
        chip_s[0] = pair_s[...]
        for cp in cps:
            cp.wait()
        cps = []
        for s in range(1, NCHIP):
            cp = pltpu.make_async_remote_copy(
                src_ref=chip_s.at[s, mine], dst_ref=chip_s.at[s, mine], send_sem=send_sem.at[NCHIP + s],
                recv_sem=recv_sem.at[NCHIP + s], device_id=(x, y, 1 - c), device_id_type=MESH)
            cp.start()
            cps.append(cp)
        for cp in cps:
            cp.wait()
        tot = chip_s[j]
        for k in range(1, NCHIP):
            tot = tot + chip_s[k ^ j]
        g_ref[...] = tot
        d, mn, vn = _adamw_math(w_ref[...], tot, m_ref[...], v_ref[...])
        d_ref[...] = d
        mo_ref[...] = mn
        vo_ref[...] = vn
        job.mid(jin, jout, jsem)
        job.finish(jin, jout, jsem)

    vm = pl.BlockSpec(memory_space=pltpu.VMEM)
    res = pl.pallas_call(
        body, name="small_allreduce_adamw",
        in_specs=[vm] * 4 + [ANY] * ji, out_specs=[vm] * 4 + [ANY] * jo,
        out_shape=[SDS(shape, F32)] * 4 + job.out_shapes,
        scratch_shapes=[pltpu.VMEM(shape, F32), pltpu.VMEM(shape, F32), pltpu.VMEM((NCHIP,) + shape, F32),
                        pltpu.SemaphoreType.DMA((2 * NCHIP,)), pltpu.SemaphoreType.DMA((2 * NCHIP,))] + job.sems,
        input_output_aliases={4 + a: 4 + b for a, b in job.aliases.items()},
        compiler_params=pltpu.CompilerParams(has_side_effects=True),
    )(sp, w, m, v, *job.ins)
    return res[:4], res[4:]


def _pack_small(first, mix, ln_g, ln_b, b_s, lbt, hn, ffn, fin, w_s):
    rows = [first.reshape(1, D), mix.reshape(1, D), ln_g.reshape(1, D), ln_b.reshape(1, D),
            b_s.reshape(1, D), lbt.reshape(2, D), hn.reshape(1, D), ffn.reshape(1, D), fin.reshape(1, D),
            jnp.zeros((6, D), F32)]
    return jnp.concatenate(rows + [w_s.reshape(NG, GCH, GCH).transpose(1, 0, 2).reshape(GCH, D)], axis=0)


def _unpack_small(p):
    w_s = p[16:].reshape(GCH, NG, GCH).transpose(1, 0, 2).reshape(1, NG, GCH, GCH)
    return dict(norm_mix_g=p[1:2], gmlp_ln_g=p[2:3], gmlp_ln_b=p[3:4], gmlp_b_s=p[4].reshape(1, NG, GCH),
                hgrn_lb_table=p[5:7], hgrn_norm_g=p[7:8], norm_ffn_g=p[8:9], norm_final_g=p[9],
                gmlp_w_s=w_s)


SMALL = ("norm_mix_g", "gmlp_ln_g", "gmlp_ln_b", "gmlp_w_s", "gmlp_b_s", "hgrn_lb_table", "hgrn_norm_g",
         "norm_ffn_g", "norm_final_g")
BIG = ("w_in", "w_gate_up", "w_branch_a", "w_branch_b", "w_out", "w_down")
ORDER = ("norm_mix_g", "w_in", "gmlp_ln_g", "gmlp_ln_b", "gmlp_w_s", "gmlp_b_s", "hgrn_lb_table",
         "hgrn_norm_g", "w_branch_a", "w_branch_b", "w_out", "norm_ffn_g", "w_gate_up", "w_down",
         "norm_final_g")


def kernel(x, norm_mix_g, w_in, gmlp_ln_g, gmlp_ln_b, gmlp_w_s, gmlp_b_s, hgrn_lb_table, hgrn_norm_g, w_branch_a, w_branch_b, w_out, norm_ffn_g, w_gate_up, w_down, norm_final_g, loss_target, m_norm_mix_g, m_w_in, m_gmlp_ln_g, m_gmlp_ln_b, m_gmlp_w_s, m_gmlp_b_s, m_hgrn_lb_table, m_hgrn_norm_g, m_w_branch_a, m_w_branch_b, m_w_out, m_norm_ffn_g, m_w_gate_up, m_w_down, m_norm_final_g, v_norm_mix_g, v_w_in, v_gmlp_ln_g, v_gmlp_ln_b, v_gmlp_w_s, v_gmlp_b_s, v_hgrn_lb_table, v_hgrn_norm_g, v_w_branch_a, v_w_branch_b, v_w_out, v_norm_ffn_g, v_w_gate_up, v_w_down, v_norm_final_g):
    args = dict(locals())
    T = x.shape[1]
    xs = x.reshape(T, D)
    target = loss_target.reshape(T, D)
    big = {n: args[n].reshape(args[n].shape[1:]) for n in BIG}
    big_m = {n: args["m_" + n].reshape(args[n].shape[1:]) for n in BIG}
    big_v = {n: args["v_" + n].reshape(args[n].shape[1:]) for n in BIG}

    x_i, y_i, c_i = lax.axis_index("x"), lax.axis_index("y"), lax.axis_index("c")
    place = jnp.stack([2 * x_i + y_i, c_i]).astype(jnp.int32)
    cast = {n: _cast_shard("cast_" + n, place, big[n]) for n in BIG}
    tril = jnp.tril(jnp.ones((GCH, GCH), bool))
    wm = jnp.where(tril, gmlp_w_s[0], 0.0).astype(BF16)
    wm_t = jnp.swapaxes(wm, 1, 2)
    b_t = gmlp_b_s[0].T

    (proj, hb), w_in4, (w_a4, w_b4, w_out4, w_down4) = _proj_fwd(
        place, xs, norm_mix_g, cast["w_in"], [cast[n] for n in ("w_branch_a", "w_branch_b", "w_out", "w_down")])
    (ab,), _ = _gmlp_fwd(proj, gmlp_ln_g, gmlp_ln_b, wm, b_t)
    (o_raw, obb, st_before), (w_gu4,) = _hgrn_fwd(
        proj, hgrn_lb_table, hgrn_norm_g, job=_gather_job([cast["w_gate_up"]]))
    w_a, w_b, w_o = (w.reshape(D, D) for w in (w_a4, w_b4, w_out4))
    (mgb, x1), _ = _merge_fwd(xs, ab, obb, proj, w_a, w_b, w_o)
    w_dn = w_down4.reshape(FF, D)
    act, dx2b, h2b, dgu4, dx1, dx1b, acc_ffn = _ffn_fwd_bwd(
        x1, target, norm_ffn_g, norm_final_g.reshape(1, D), w_gu4, w_dn)

    grads, owns, parts, halves, sibh = {}, {}, {}, {}, {}

    def pair_sums(names, sibs):
        for n, s in zip(names, sibs):
            owns[n], parts[n] = _pair_sum("rs_pair_sum_" + n, place, grads[n], s)

    def chip_sums(names, got):
        for n, r in zip(names, got):
            halves[n] = _chip_sum("rs_chip_sum_" + n, owns[n], r)

    ffn, mix = ("w_gate_up", "w_down"), ("w_branch_a", "w_branch_b", "w_out")
    grads["w_gate_up"], _ = _dw_gate_up(h2b, dgu4)
    grads["w_down"], _ = _dw_down(act, dx2b)
    (dya, dyb, dproj), got = _merge_bwd(
        dx1b, ab, obb, proj, w_o, w_a, w_b, job=_pair_exchange_job([grads[n] for n in ffn]))
    pair_sums(ffn, got)
    grads["w_branch_a"], _ = _dw_square("dw_branch_a", ab, dya)
    grads["w_branch_b"], _ = _dw_square("dw_branch_b", obb, dyb)
    grads["w_out"], _ = _dw_square("dw_out", mgb, dx1b)
    (dproj, acc_hgrn), got = _hgrn_bwd(
        dproj, dyb, w_b, o_raw, proj, st_before, hgrn_lb_table, hgrn_norm_g,
        job=_join_jobs(_chip_exchange_job([parts[n] for n in ffn]), _pair_exchange_job([grads[n] for n in mix])))
    chip_sums(ffn, got[:2])
    pair_sums(mix, got[2:])
    dproj, acc_ln, dws, dmix = _gmlp_bwd(dproj, dya, w_a, proj, gmlp_ln_g, gmlp_ln_b, wm, wm_t, b_t)
    for_sibling, got = _dw_in_half(
        "dw_in_sibling_half", place, hb, dproj, False,
        job=_join_jobs(_share_halves_job([halves[n] for n in ffn]), _chip_exchange_job([parts[n] for n in mix])))
    sibh.update(zip(ffn, got[:2]))
    chip_sums(mix, got[2:])
    grads["w_in"], got = _dw_in_half(
        "dw_in_own_half", place, hb, dproj, True, job=_share_halves_job([for_sibling]))
    pair_sums(("w_in",), got)
    (grad_x, acc_mix), got = _proj_bwd(
        dproj, w_in4, xs, dx1, norm_mix_g,
        job=_join_jobs(_chip_exchange_job([parts["w_in"]]), _share_halves_job([halves[n] for n in mix])))
    chip_sums(("w_in",), got[:1])
    sibh.update(zip(mix, got[1:]))

    lbv = jax.nn.sigmoid(hgrn_lb_table[0] - hgrn_lb_table[1])
    d_t0 = jnp.sum(acc_hgrn[0], axis=0) * lbv * (1.0 - lbv)
    loss_row = jnp.zeros((D,), F32).at[0].set(jnp.sum(acc_ffn[0]))
    dws_m = jnp.where(tril[:, None, :], dws.reshape(GCH, NG, GCH), 0.0).transpose(1, 0, 2)
    db_s = jnp.sum(dmix.reshape(GCH, NG, GCH), axis=-1).T
    sp = _pack_small(loss_row, jnp.sum(acc_mix, 0), jnp.sum(acc_ln[0], 0), jnp.sum(acc_ln[1], 0), db_s,
                     jnp.stack([d_t0, -d_t0]), jnp.sum(acc_hgrn[1], 0), jnp.sum(acc_ffn[2], 0),
                     jnp.sum(acc_ffn[1], 0), dws_m)
    zero = jnp.zeros((D,), F32)

    def pack(prefix):
        a = lambda n: args[prefix + n]
        return _pack_small(zero, a("norm_mix_g"), a("gmlp_ln_g"), a("gmlp_ln_b"), a("gmlp_b_s"),
                           a("hgrn_lb_table"), a("hgrn_norm_g"), a("norm_ffn_g"), a("norm_final_g"),
                           a("gmlp_w_s"))

    packed, (sibh["w_in"],) = _small_allreduce_adamw(
        sp, pack(""), pack("m_"), pack("v_"), _share_halves_job([halves["w_in"]]))
    loss = packed[0][0, 0]
    small = [_unpack_small(p) for p in packed]
    out = {n: tuple(s[n] for s in small) for n in SMALL}
    for n in BIG:
        g, d, mn, vn = _adamw("adamw_" + n, place, big[n], halves[n], sibh[n], big_m[n], big_v[n])
        shp = args[n].shape
        out[n] = (g.reshape(shp), d.reshape(shp), mn.reshape(shp), vn.reshape(shp))
    return (loss, grad_x.reshape(x.shape), *[out[n][0] for n in ORDER], *[out[n][1] for n in ORDER],
            *[out[n][2] for n in ORDER], *[out[n][3] for n in ORDER])
```

```python
import functools
import math

import jax
import jax.numpy as jnp
from jax import lax
from jax.experimental import pallas as pl
from jax.experimental.pallas import tpu as pltpu

F32 = jnp.float32
BF16 = jnp.bfloat16
SDS = jax.ShapeDtypeStruct
MESH = pl.DeviceIdType.MESH
ANY = pl.BlockSpec(memory_space=pl.ANY)

D = 1024
NIN = 8
NG = 8
GCH = 128
NH = 8
HD = 128
HCH = 64
HGRN_HB = 4
HW = HGRN_HB * HD
DW_TOKENS = 2048
ELEMENTWISE_BLOCK_BYTES = 2 * 1024 * 1024
PROJ_OUT_SLOTS = 4
FF = 2816
FFS = 1408
NCHIP = 4
EPS = 1e-6
QSCALE = HD ** -0.5
GELU_C0 = math.sqrt(2.0 / math.pi)
GELU_C1 = 0.044715
LR, B1, B2, AEPS, WD, STEP = 0.001, 0.9, 0.999, 1e-08, 0.01, 10
VMEM_LIMIT_V7X = 56 * 1024 * 1024
SP_ROWS = 144


def _cparams(**kw):
    return pltpu.CompilerParams(vmem_limit_bytes=VMEM_LIMIT_V7X, **kw)


def _mm(a, b):
    return jnp.dot(a, b, preferred_element_type=F32)


def _mm_nt(a, b):
    return lax.dot_general(a, b, (((1,), (1,)), ((), ())), preferred_element_type=F32)


def _mm_tn(a, b):
    return lax.dot_general(a, b, (((0,), (0,)), ((), ())), preferred_element_type=F32)


def _rows8(x):
    r, c = x.shape
    return jnp.sum(x.reshape(r // 8, 8, c), axis=0)


def _mean(x):
    return jnp.mean(x, axis=-1, keepdims=True)


def _sigmoid(x):
    return 1.0 / (1.0 + jnp.exp(-x))


def _gelu(x):
    t = jnp.tanh(GELU_C0 * (x + GELU_C1 * x * x * x))
    return 0.5 * x * (1.0 + t), t


def _gelu_grad(x, t):
    return 0.5 * (1.0 + t) + 0.5 * x * (1.0 - t * t) * (GELU_C0 * (1.0 + 3.0 * GELU_C1 * x * x))


def _component_of(group):
    return jnp.where(group < 6, (group + 4) % 6, group)


def _proj_fwd(place, x, g_mix, w_in4, later):
    T = x.shape[0]
    tm = min(1024, T)
    ni = T // tm
    n = len(later)

    def body(pc_ref, x_ref, g_ref, *rest):
        proj_ref, h_ref, w_all = rest[1 + n:4 + n]
        gathered = rest[4 + n:4 + 2 * n]
        hs, wbuf, wsem, obuf, osem = rest[4 + 2 * n:9 + 2 * n]
        w_sems, later_sems = rest[9 + 2 * n:15 + 2 * n], rest[15 + 2 * n:]
        jp, i = pl.program_id(0), pl.program_id(1)
        w_cols = [w_all.at[:, :, pl.ds(k * D, D)] for k in range(2)]

        def w_copy(blk):
            cols = pl.ds(pl.multiple_of((blk % 2) * D, 128), D)
            return pltpu.make_async_copy(w_all.at[pc_ref[0] ^ (blk // 2), :, cols], wbuf.at[blk % 2],
                                         wsem.at[blk % 2])

        @pl.when((jp == 0) & (i == 0))
        def _():
            _gather_start(w_cols, w_sems)
            w_copy(jp).start()

        @pl.when(i == 0)
        def _():
            w_copy(jp).wait()

        @pl.when(jp == 0)
        def _():
            xv = x_ref[...]
            r = lax.rsqrt(_mean(xv * xv) + EPS)
            hb = (xv * r * g_ref[...]).astype(BF16)
            hs[i] = hb
            h_ref[...] = hb

        step = jp * ni + i
        slot = step % PROJ_OUT_SLOTS

        def o_copy(slot_):
            comp = 2 * (pc_ref[0] ^ (jp // 2)) + jp % 2
            return pltpu.make_async_copy(
                obuf.at[slot_], proj_ref.at[comp, pl.ds(pl.multiple_of(i * tm, 8), tm)], osem.at[slot_])

        @pl.when(step >= PROJ_OUT_SLOTS)
        def _():
            o_copy(slot).wait()

        obuf[slot] = _mm(hs[i], wbuf[jp % 2])
        o_copy(slot).start()

        @pl.when(step == NIN * ni - 1)
        def _():
            for k in range(PROJ_OUT_SLOTS):
                o_copy((slot + 1 + k) % PROJ_OUT_SLOTS).wait()

        for nxt in range(1, NIN):
            @pl.when((jp == nxt - 1) & (i == ni - 1))
            def _():
                if nxt >= 2:
                    _gather_land([w_cols[nxt % 2]], w_sems, nxt // 2, first=nxt % 2)
                if nxt == 5:
                    _gather_start(gathered, later_sems)
                if nxt == NIN - 1:
                    _gather_neighbours(gathered, later_sems)
                w_copy(jp + 1).start()

        @pl.when((jp == NIN - 1) & (i == ni - 1))
        def _():
            _gather_drain(w_cols, w_sems)
            _gather_finish(gathered, later_sems)

    tile = lambda jp, i, pc: (jnp.where(jp == 0, i, ni - 1), 0)
    res = pl.pallas_call(
        body, name="proj_fwd",
        grid_spec=pltpu.PrefetchScalarGridSpec(
            num_scalar_prefetch=1, grid=(NIN, ni),
            in_specs=[pl.BlockSpec((tm, D), tile), pl.BlockSpec((1, D), lambda jp, i, pc: (0, 0))] + [ANY] * (1 + n),
            out_specs=[ANY, pl.BlockSpec((tm, D), tile)] + [ANY] * (1 + n),
            scratch_shapes=[pltpu.VMEM((ni, tm, D), BF16), pltpu.VMEM((2, D, D), BF16),
                            pltpu.SemaphoreType.DMA((2,)), pltpu.VMEM((PROJ_OUT_SLOTS, tm, D), F32),
                            pltpu.SemaphoreType.DMA((PROJ_OUT_SLOTS,))] + _gather_sems(2) + _gather_sems(n)),
        out_shape=[SDS((NIN, T, D), F32), SDS((T, D), BF16), SDS(w_in4.shape, BF16)]
        + [SDS(a.shape, a.dtype) for a in later],
        input_output_aliases={3 + k: 2 + k for k in range(1 + n)},
        compiler_params=_cparams(has_side_effects=True),
    )(place, x, g_mix, w_in4, *later)
    return res[:2], res[2], res[3:]


def _layer_norm_stats(gv):
    mu = _mean(gv)
    xc = gv - mu
    rs = lax.rsqrt(_mean(xc * xc) + EPS)
    return xc * rs, rs


def _gmlp_fwd(proj, ln_g, ln_b, wm, b_t, job=None):
    T = proj.shape[1]
    tm = min(256, T)

    def body(u_ref, v_ref, lg_ref, lb_ref, wm_ref, bt_ref, a_ref, a_s):
        gu, _ = _gelu(u_ref[...])
        gv, _ = _gelu(v_ref[...])
        vhat, _ = _layer_norm_stats(gv)
        vnb = (vhat * lg_ref[...] + lb_ref[...]).astype(BF16)
        for ch in range(tm // GCH):
            rows = slice(GCH * ch, GCH * (ch + 1))
            for g in range(NG):
                cols = slice(128 * g, 128 * (g + 1))
                mixed = _mm(wm_ref[g], vnb[rows, cols]) + bt_ref[:, g:g + 1]
                a_s[rows, cols] = gu[rows, cols] * mixed
        a_ref[...] = a_s[...].astype(BF16)

    row = lambda i: (0, 0)
    return _call(
        body, name="gmlp_fwd", grid=(T // tm,), job=job, args=(proj, proj, ln_g, ln_b, wm, b_t),
        in_specs=[pl.BlockSpec((None, tm, D), lambda i: (0, i, 0)), pl.BlockSpec((None, tm, D), lambda i: (1, i, 0)),
                  pl.BlockSpec((1, D), row), pl.BlockSpec((1, D), row),
                  pl.BlockSpec((NG, GCH, GCH), lambda i: (0, 0, 0)), pl.BlockSpec((GCH, NG), row)],
        out_specs=[pl.BlockSpec((tm, D), lambda i: (i, 0))],
        out_shape=[SDS((T, D), BF16)],
        scratch_shapes=[pltpu.VMEM((tm, D), F32)])


def _cumsum64(x, row):
    for s in (1, 2, 4, 8, 16, 32):
        x = x + jnp.where(row >= s, pltpu.roll(x, s, 0), 0.0)
    return x


def _revcumsum64(x, row):
    n = x.shape[0]
    for s in (1, 2, 4, 8, 16, 32):
        x = x + jnp.where(row < HCH - s, pltpu.roll(x, n - s, 0), 0.0)
    return x


def _head_mean(x):
    parts = [jnp.broadcast_to(_mean(x[:, HD * h:HD * (h + 1)]), (x.shape[0], HD)) for h in range(x.shape[1] // HD)]
    return jnp.concatenate(parts, axis=1)


def _seg_sum(x):
    n, c = x.shape
    s = jnp.sum(x.reshape(n // HCH, HCH, c), axis=1, keepdims=True)
    return jnp.broadcast_to(s, (n // HCH, HCH, c)).reshape(n, c)


def _hgrn_gates(fl, lbv, row):
    s = _sigmoid(fl)
    f = lbv + (1.0 - lbv) * s
    a = _cumsum64(jnp.log(f), row)
    a_mid = _seg_sum(jnp.where(row == HCH // 2 - 1, a, 0.0))
    a_last = _seg_sum(jnp.where(row == HCH - 1, a, 0.0))
    return s, f, a, a_mid, a_last


def _hgrn_fwd(proj, lb_table, norm_g, job=None):
    T = proj.shape[1]
    tb = min(512, T)
    nc = tb // HCH

    def body(q_ref, fl_ref, v_ref, g_ref, lbt_ref, gn_ref, o_ref, ob_ref, stb_ref, st_s, o_s):
        @pl.when(pl.program_id(1) == 0)
        def _():
            st_s[...] = jnp.zeros_like(st_s)

        row = lax.broadcasted_iota(jnp.int32, (tb, HW), 0) & (HCH - 1)
        lbv = _sigmoid(lbt_ref[0:1, :] - lbt_ref[1:2, :])
        _, f, a, a_mid, a_last = _hgrn_gates(fl_ref[...], lbv, row)
        k = 1.0 - f
        qs = q_ref[...] * QSCALE
        q_in = (qs * jnp.exp(a - a_mid)).astype(BF16)
        k_in = (k * jnp.exp(a_mid - a)).astype(BF16)
        q_a = (qs * jnp.exp(a)).astype(BF16)
        k_d = (k * jnp.exp(a_last - a)).astype(BF16)
        dec = jnp.exp(a_last)
        vb = v_ref[...].astype(BF16)
        tri = (lax.broadcasted_iota(jnp.int32, (HCH, HCH), 0)
               >= lax.broadcasted_iota(jnp.int32, (HCH, HCH), 1))
        for c in range(nc):
            sl = slice(HCH * c, HCH * (c + 1))
            for hh in range(HGRN_HB):
                hs = slice(HD * hh, HD * (hh + 1))
                st = st_s[hh]
                stb_ref[hh, c] = st
                sc = jnp.where(tri, _mm_nt(q_in[sl, hs], k_in[sl, hs]), 0.0)
                o_s[sl, hs] = _mm(sc.astype(BF16), vb[sl, hs]) + _mm_nt(q_a[sl, hs], st.astype(BF16))
                d64 = dec[sl, hs]
                st_s[hh] = st * jnp.concatenate([d64, d64], axis=0) + _mm_tn(vb[sl, hs], k_d[sl, hs])
        o = o_s[...]
        r = lax.rsqrt(_head_mean(o * o) + EPS)
        g = g_ref[...]
        o_ref[...] = o
        ob_ref[...] = (o * r * gn_ref[...] * (g * _sigmoid(g))).astype(BF16)

    def col(off):
        return pl.BlockSpec((None, tb, HW), lambda h, cb: (off, cb, h))

    return _call(
        body, name="hgrn_fwd", grid=(NH // HGRN_HB, T // tb), job=job,
        args=(proj, proj, proj, proj, lb_table, norm_g),
        in_specs=[col(2), col(3), col(4), col(5),
                  pl.BlockSpec((2, HW), lambda h, cb: (0, h)), pl.BlockSpec((1, HW), lambda h, cb: (0, h))],
        out_specs=[pl.BlockSpec((tb, HW), lambda h, cb: (cb, h)), pl.BlockSpec((tb, HW), lambda h, cb: (cb, h)),
                   pl.BlockSpec((HGRN_HB, nc, HD, HD), lambda h, cb: (h, cb, 0, 0))],
        out_shape=[SDS((T, D), F32), SDS((T, D), BF16), SDS((NH, T // HCH, HD, HD), F32)],
        scratch_shapes=[pltpu.VMEM((HGRN_HB, HD, HD), F32), pltpu.VMEM((tb, HW), F32)])


def _merge_fwd(x, ab, ob, proj, w_a, w_b, w_out, job=None):
    T = x.shape[0]
    tm = min(512, T)

    def body(x_ref, ab_ref, ob_ref, ga_ref, gb_ref, wa_ref, wb_ref, wo_ref, mg_ref, x1_ref):
        ya = _mm(ab_ref[...], wa_ref[...])
        yb = _mm(ob_ref[...], wb_ref[...])
        merged = (_sigmoid(ga_ref[...]) * ya + _sigmoid(gb_ref[...]) * yb).astype(BF16)
        mg_ref[...] = merged
        x1_ref[...] = x_ref[...] + _mm(merged, wo_ref[...])

    t = lambda i: (i, 0)
    w = lambda i: (0, 0)
    return _call(
        body, name="merge_fwd", grid=(T // tm,), job=job, args=(x, ab, ob, proj, proj, w_a, w_b, w_out),
        in_specs=[pl.BlockSpec((tm, D), t), pl.BlockSpec((tm, D), t), pl.BlockSpec((tm, D), t),
                  pl.BlockSpec((None, tm, D), lambda i: (6, i, 0)), pl.BlockSpec((None, tm, D), lambda i: (7, i, 0)),
                  pl.BlockSpec((D, D), w), pl.BlockSpec((D, D), w), pl.BlockSpec((D, D), w)],
        out_specs=[pl.BlockSpec((tm, D), t)] * 2,
        out_shape=[SDS((T, D), BF16), SDS((T, D), F32)])


def _ffn_fwd_bwd(x1, target, g_ffn, g_fin, w_gu4, w_down):
    T = x1.shape[0]
    tm = min(256, T)
    inv_d = 1.0 / D

    def body(x1_ref, tg_ref, gf_ref, gn_ref, wgu_ref, wd_ref,
             act_ref, dx2b_ref, h2b_ref, dgu_ref, dx1_ref, dx1b_ref, acc_ref):
        @pl.when(pl.program_id(0) == 0)
        def _():
            acc_ref[...] = jnp.zeros_like(acc_ref)

        x1v = x1_ref[...]
        gf = gf_ref[...]
        gn = gn_ref[...]
        rr1 = lax.rsqrt(_mean(x1v * x1v) + EPS)
        x1n = x1v * rr1
        h2b = (x1n * gf).astype(BF16)
        h2b_ref[...] = h2b
        p = [_mm(h2b, wgu_ref[k]) for k in range(NCHIP)]
        sg = [_sigmoid(p[0]), _sigmoid(p[1])]
        si = [p[0] * sg[0], p[1] * sg[1]]
        x2 = x1v
        for k in range(2):
            actk = (si[k] * p[2 + k]).astype(BF16)
            act_ref[:, FFS * k:FFS * (k + 1)] = actk
            x2 = x2 + _mm(actk, wd_ref[FFS * k:FFS * (k + 1), :])
        rr2 = lax.rsqrt(_mean(x2 * x2) + EPS)
        x2n = x2 * rr2
        e = x2n * gn - tg_ref[...]
        acc_ref[0] += _rows8(e * e) * (0.5 * inv_d)
        dy = e * inv_d
        acc_ref[1] += _rows8(dy * x2n)
        dxn = dy * gn
        dx2 = rr2 * (dxn - x2n * _mean(dxn * x2n))
        dx2b = dx2.astype(BF16)
        dx2b_ref[...] = dx2b
        dh2 = None
        for k in range(2):
            dact = _mm_nt(dx2b, wd_ref[FFS * k:FFS * (k + 1), :])
            dgate = (dact * p[2 + k] * (sg[k] * (1.0 + p[k] * (1.0 - sg[k])))).astype(BF16)
            dup = (dact * si[k]).astype(BF16)
            dgu_ref[k] = dgate
            dgu_ref[2 + k] = dup
            part = _mm_nt(dgate, wgu_ref[k]) + _mm_nt(dup, wgu_ref[2 + k])
            dh2 = part if dh2 is None else dh2 + part
        acc_ref[2] += _rows8(dh2 * x1n)
        dxn1 = dh2 * gf
        dx1 = dx2 + rr1 * (dxn1 - x1n * _mean(dxn1 * x1n))
        dx1_ref[...] = dx1
        dx1b_ref[...] = dx1.astype(BF16)

    t = lambda i: (i, 0)
    w = lambda i: (0, 0)
    one = pl.Buffered(1)
    return pl.pallas_call(
        body, name="ffn_fwd_bwd", grid=(T // tm,),
        in_specs=[pl.BlockSpec((tm, D), t), pl.BlockSpec((tm, D), t),
                  pl.BlockSpec((1, D), w), pl.BlockSpec((1, D), w),
                  pl.BlockSpec((NCHIP, D, FFS), lambda i: (0, 0, 0), pipeline_mode=one),
                  pl.BlockSpec((FF, D), w, pipeline_mode=one)],
        out_specs=[pl.BlockSpec((tm, FF), t), pl.BlockSpec((tm, D), t), pl.BlockSpec((tm, D), t),
                   pl.BlockSpec((NCHIP, tm, FFS), lambda i: (0, i, 0)),
                   pl.BlockSpec((tm, D), t), pl.BlockSpec((tm, D), t),
                   pl.BlockSpec((3, 8, D), lambda i: (0, 0, 0))],
        out_shape=[SDS((T, FF), BF16), SDS((T, D), BF16), SDS((T, D), BF16),
                   SDS((NCHIP, T, FFS), BF16), SDS((T, D), F32), SDS((T, D), BF16),
                   SDS((3, 8, D), F32)],
        compiler_params=_cparams(),
    )(x1, target, g_ffn, g_fin, w_gu4, w_down)


def _merge_bwd(dx1b, ab, ob, proj, w_out, w_a, w_b, job=None):
    T = dx1b.shape[0]
    tm = min(512, T)

    def body(dx_ref, ab_ref, ob_ref, ga_ref, gb_ref, wo_ref, wa_ref, wb_ref, dya_ref, dyb_ref, dp_ref):
        dm = _mm_nt(dx_ref[...], wo_ref[...])
        sa = _sigmoid(ga_ref[...])
        sb = _sigmoid(gb_ref[...])
        dya_ref[...] = (dm * sa).astype(BF16)
        dyb_ref[...] = (dm * sb).astype(BF16)
        dp_ref[0] = (dm * _mm(ab_ref[...], wa_ref[...]) * sa * (1.0 - sa)).astype(BF16)
        dp_ref[1] = (dm * _mm(ob_ref[...], wb_ref[...]) * sb * (1.0 - sb)).astype(BF16)

    t = lambda i: (i, 0)
    w = lambda i: (0, 0)
    return _call(
        body, name="merge_bwd", grid=(T // tm,),
        in_specs=[pl.BlockSpec((tm, D), t), pl.BlockSpec((tm, D), t), pl.BlockSpec((tm, D), t),
                  pl.BlockSpec((None, tm, D), lambda i: (6, i, 0)), pl.BlockSpec((None, tm, D), lambda i: (7, i, 0)),
                  pl.BlockSpec((D, D), w), pl.BlockSpec((D, D), w), pl.BlockSpec((D, D), w)],
        out_specs=[pl.BlockSpec((tm, D), t)] * 2 + [pl.BlockSpec((2, tm, D), lambda i: (3, i, 0))],
        out_shape=[SDS((T, D), BF16), SDS((T, D), BF16), SDS((NIN, T, D), BF16)],
        args=(dx1b, ab, ob, proj, proj, w_out, w_a, w_b), job=job)


def _hgrn_bwd(dproj, dyb, w_b, o_raw, proj, st_before, lb_table, norm_g, job=None):
    T = dyb.shape[0]
    tb = min(512, T)
    nc = tb // HCH
    nb = T // tb

    def body(dp_in, dyb_ref, wb_ref, o_ref, q_ref, fl_ref, v_ref, g_ref, stb_ref, lbt_ref, gn_ref,
             dp_ref, acc_ref, dst_s, dqin_s, dqa_s, dkin_s, dkd_s, dv_s, ddec_s):
        del dp_in

        @pl.when(pl.program_id(1) == 0)
        def _():
            dst_s[...] = jnp.zeros_like(dst_s)
            acc_ref[...] = jnp.zeros_like(acc_ref)

        row = lax.broadcasted_iota(jnp.int32, (tb, HW), 0) & (HCH - 1)
        gn = gn_ref[...]
        lbv = _sigmoid(lbt_ref[0:1, :] - lbt_ref[1:2, :])
        o = o_ref[...]
        r = lax.rsqrt(_head_mean(o * o) + EPS)
        on = o * r
        g = g_ref[...]
        sgm = _sigmoid(g)
        dob_v = _mm_nt(dyb_ref[...], wb_ref[...])
        dp_ref[3] = (dob_v * on * gn * (sgm * (1.0 + g * (1.0 - sgm)))).astype(BF16)
        do_n = dob_v * (g * sgm)
        acc_ref[1] += _rows8(do_n * on)
        dxn = do_n * gn
        do = (r * (dxn - on * _head_mean(dxn * on))).astype(BF16)
        s, f, a, a_mid, a_last = _hgrn_gates(fl_ref[...], lbv, row)
        k = 1.0 - f
        qs = q_ref[...] * QSCALE
        e_q = jnp.exp(a - a_mid)
        e_k = jnp.exp(a_mid - a)
        e_a = jnp.exp(a)
        e_l = jnp.exp(a_last - a)
        dec = jnp.exp(a_last)
        q_in = qs * e_q
        k_in = k * e_k
        q_a = qs * e_a
        k_d = k * e_l
        q_inb, k_inb, q_ab, k_db = (z.astype(BF16) for z in (q_in, k_in, q_a, k_d))
        vb = v_ref[...].astype(BF16)
        tri = (lax.broadcasted_iota(jnp.int32, (HCH, HCH), 0)
               >= lax.broadcasted_iota(jnp.int32, (HCH, HCH), 1))
        for c in reversed(range(nc)):
            sl = slice(HCH * c, HCH * (c + 1))
            for hh in range(HGRN_HB):
                hs = slice(HD * hh, HD * (hh + 1))
                stp = stb_ref[hh, c]
                dst = dst_s[hh]
                dstb = dst.astype(BF16)
                do_c = do[sl, hs]
                v_c = vb[sl, hs]
                dqa_s[sl, hs] = _mm(do_c, stp.astype(BF16))
                dkd_s[sl, hs] = _mm(v_c, dstb)
                ddec_s[sl, hs] = jnp.broadcast_to(jnp.sum(dst * stp, axis=0, keepdims=True), (HCH, HD))
                sc = jnp.where(tri, _mm_nt(q_inb[sl, hs], k_inb[sl, hs]), 0.0).astype(BF16)
                dsc = jnp.where(tri, _mm_nt(do_c, v_c), 0.0).astype(BF16)
                dv_s[sl, hs] = _mm_nt(k_db[sl, hs], dstb) + _mm_tn(sc, do_c)
                dqin_s[sl, hs] = _mm(dsc, k_inb[sl, hs])
                dkin_s[sl, hs] = _mm_tn(dsc, q_inb[sl, hs])
                d64 = dec[sl, hs]
                dst_s[hh] = dst * jnp.concatenate([d64, d64], axis=0) + _mm_tn(do_c, q_ab[sl, hs])
        dq_in = dqin_s[...]
        dq_a = dqa_s[...]
        dk_in = dkin_s[...]
        dk_d = dkd_s[...]
        dp_ref[0] = ((dq_in * e_q + dq_a * e_a) * QSCALE).astype(BF16)
        dp_ref[2] = dv_s[...].astype(BF16)
        tq = dq_in * q_in
        tk = dk_in * k_in
        td = dk_d * k_d
        d_a = tq + dq_a * q_a - tk - td
        d_a = d_a + jnp.where(row == HCH // 2 - 1, _seg_sum(tk - tq), 0.0)
        d_a = d_a + jnp.where(row == HCH - 1, _seg_sum(td) + ddec_s[...] * dec, 0.0)
        dlf = _revcumsum64(d_a, row)
        df = dlf / f - (dk_in * e_k + dk_d * e_l)
        dp_ref[1] = (df * (1.0 - lbv) * s * (1.0 - s)).astype(BF16)
        acc_ref[0] += _rows8(df * (1.0 - s))

    def col(off):
        return pl.BlockSpec((None, tb, HW), lambda h, cb: (off, nb - 1 - cb, h))

    hb = lambda h, cb: (nb - 1 - cb, h)
    return _call(
        body, name="hgrn_bwd", grid=(NH // HGRN_HB, nb), job=job,
        args=(dproj, dyb, w_b, o_raw, proj, proj, proj, proj, st_before, lb_table, norm_g),
        in_specs=[ANY, pl.BlockSpec((tb, D), lambda h, cb: (nb - 1 - cb, 0)),
                  pl.BlockSpec((HW, D), lambda h, cb: (h, 0)), pl.BlockSpec((tb, HW), hb),
                  col(2), col(3), col(4), col(5),
                  pl.BlockSpec((HGRN_HB, nc, HD, HD), lambda h, cb: (h, nb - 1 - cb, 0, 0)),
                  pl.BlockSpec((2, HW), lambda h, cb: (0, h)), pl.BlockSpec((1, HW), lambda h, cb: (0, h))],
        out_specs=[pl.BlockSpec((4, tb, HW), lambda h, cb: (0, nb - 1 - cb, h)),
                   pl.BlockSpec((2, 8, HW), lambda h, cb: (0, 0, h))],
        out_shape=[SDS(dproj.shape, BF16), SDS((2, 8, D), F32)],
        scratch_shapes=[pltpu.VMEM((HGRN_HB, HD, HD), F32)] + [pltpu.VMEM((tb, HW), F32)] * 6,
        aliases={0: 0})


def _gmlp_bwd(dproj, dya, w_a, proj, ln_g, ln_b, wm, wm_t, b_t):
    T = dya.shape[0]
    tm = min(256, T)

    def body(dp_in, dya_ref, wa_ref, u_ref, v_ref, lg_ref, lb_ref, wm_ref, wmt_ref, bt_ref,
             dp_ref, acc_ref, dws_ref, dmix_ref, du_s, dvn_s):
        del dp_in

        @pl.when(pl.program_id(0) == 0)
        def _():
            acc_ref[...] = jnp.zeros_like(acc_ref)
            dws_ref[...] = jnp.zeros_like(dws_ref)
            dmix_ref[...] = jnp.zeros_like(dmix_ref)

        u = u_ref[...]
        v = v_ref[...]
        lg = lg_ref[...]
        gu, t_u = _gelu(u)
        gv, t_v = _gelu(v)
        vhat, rs = _layer_norm_stats(gv)
        vnb = (vhat * lg + lb_ref[...]).astype(BF16)
        da_v = _mm_nt(dya_ref[...], wa_ref[...])
        for ch in range(tm // GCH):
            rows = slice(GCH * ch, GCH * (ch + 1))
            for g in range(NG):
                cols = slice(128 * g, 128 * (g + 1))
                vng = vnb[rows, cols]
                mixed = _mm(wm_ref[g], vng) + bt_ref[:, g:g + 1]
                dag = da_v[rows, cols]
                dmx = dag * gu[rows, cols]
                du_s[rows, cols] = dag * mixed
                dmxb = dmx.astype(BF16)
                dws_ref[:, cols] += _mm_nt(dmxb, vng)
                dmix_ref[:, cols] += dmx
                dvn_s[rows, cols] = _mm(wmt_ref[g], dmxb)
        dp_ref[0] = (du_s[...] * _gelu_grad(u, t_u)).astype(BF16)
        dvn = dvn_s[...]
        acc_ref[0] += _rows8(dvn * vhat)
        acc_ref[1] += _rows8(dvn)
        dvh = dvn * lg
        dgv = rs * (dvh - _mean(dvh) - vhat * _mean(dvh * vhat))
        dp_ref[1] = (dgv * _gelu_grad(v, t_v)).astype(BF16)

    row = lambda i: (0, 0)
    w3 = lambda i: (0, 0, 0)
    return pl.pallas_call(
        body, name="gmlp_bwd", grid=(T // tm,),
        in_specs=[ANY, pl.BlockSpec((tm, D), lambda i: (i, 0)), pl.BlockSpec((D, D), row),
                  pl.BlockSpec((None, tm, D), lambda i: (0, i, 0)), pl.BlockSpec((None, tm, D), lambda i: (1, i, 0)),
                  pl.BlockSpec((1, D), row), pl.BlockSpec((1, D), row),
                  pl.BlockSpec((NG, GCH, GCH), w3), pl.BlockSpec((NG, GCH, GCH), w3),
                  pl.BlockSpec((GCH, NG), row)],
        out_specs=[pl.BlockSpec((2, tm, D), lambda i: (2, i, 0)),
                   pl.BlockSpec((2, 8, D), w3), pl.BlockSpec((GCH, D), row), pl.BlockSpec((GCH, D), row)],
        out_shape=[SDS(dproj.shape, BF16), SDS((2, 8, D), F32), SDS((GCH, D), F32), SDS((GCH, D), F32)],
        scratch_shapes=[pltpu.VMEM((tm, D), F32), pltpu.VMEM((tm, D), F32)],
        input_output_aliases={0: 0},
        compiler_params=_cparams(),
    )(dproj, dya, w_a, proj, proj, ln_g, ln_b, wm, wm_t, b_t)


def _proj_bwd(dproj, w_in4, x, dx1, g_mix, job=None):
    T = x.shape[0]
    tm = min(256, T)
    order = (2, 3, 4, 5, 0, 1, 6, 7)

    def body(dp_ref, w_ref, x_ref, dx1_ref, g_ref, gx_ref, acc_ref):
        @pl.when(pl.program_id(0) == 0)
        def _():
            acc_ref[...] = jnp.zeros_like(acc_ref)

        dh = None
        for m, og in enumerate(order):
            part = _mm_nt(dp_ref[m], w_ref[og // 2, :, D * (og % 2):D * (og % 2 + 1)])
            dh = part if dh is None else dh + part
        xv = x_ref[...]
        r = lax.rsqrt(_mean(xv * xv) + EPS)
        xn = xv * r
        acc_ref[...] += _rows8(dh * xn)
        dxn = dh * g_ref[...]
        gx_ref[...] = dx1_ref[...] + r * (dxn - xn * _mean(dxn * xn))

    t = lambda i: (i, 0)
    return _call(
        body, name="proj_bwd", grid=(T // tm,),
        in_specs=[pl.BlockSpec((NIN, tm, D), lambda i: (0, i, 0)),
                  pl.BlockSpec((NCHIP, D, 2 * D), lambda i: (0, 0, 0), pipeline_mode=pl.Buffered(1)),
                  pl.BlockSpec((tm, D), t), pl.BlockSpec((tm, D), t), pl.BlockSpec((1, D), lambda i: (0, 0))],
        out_specs=[pl.BlockSpec((tm, D), t), pl.BlockSpec((8, D), lambda i: (0, 0))],
        out_shape=[SDS((T, D), F32), SDS((8, D), F32)],
        args=(dproj, w_in4, x, dx1, g_mix), job=job)


def _dw_call(name, a, b, a_spec, b_spec, o_spec, out_shape, nblk, tt, job=None, prefetch=None):
    T = a.shape[-2]

    def body(*refs):
        a_ref, b_ref, o_ref = refs[-3:]

        @pl.when(pl.program_id(1) == 0)
        def _():
            o_ref[...] = jnp.zeros_like(o_ref)
        o_ref[...] += _mm_tn(a_ref[...], b_ref[...])

    (out,), job_out = _call(
        body, name=name, grid=(nblk, T // tt), in_specs=[a_spec, b_spec], out_specs=[o_spec],
        out_shape=[out_shape], args=(a, b), job=job, prefetch=prefetch)
    return out, job_out


def _dw_in_half(name, place, hb, dproj, mine, job=None):
    tt = min(DW_TOKENS, hb.shape[0])

    def comp(k, pc):
        return _component_of(2 * k + (pc[1] if mine else 1 - pc[1]))

    return _dw_call(
        name, hb, dproj,
        pl.BlockSpec((tt, D), lambda k, t, pc: (t, 0)),
        pl.BlockSpec((None, tt, D), lambda k, t, pc: (comp(k, pc), t, 0)),
        pl.BlockSpec((None, D, D), lambda k, t, pc: (k, 0, 0)),
        SDS((NCHIP, D, D), F32), NCHIP, tt, job, place)


def _dw_gate_up(h2b, dgu4, job=None):
    tt = min(DW_TOKENS, h2b.shape[0])
    return _dw_call(
        "dw_gate_up", h2b, dgu4,
        pl.BlockSpec((tt, D), lambda k, t: (t, 0)),
        pl.BlockSpec((None, tt, FFS), lambda k, t: (k, t, 0)),
        pl.BlockSpec((None, D, FFS), lambda k, t: (k, 0, 0)),
        SDS((NCHIP, D, FFS), F32), NCHIP, tt, job)


def _dw_down(act, dx2b, job=None):
    tt = min(DW_TOKENS, act.shape[0])
    g, job_out = _dw_call(
        "dw_down", act, dx2b,
        pl.BlockSpec((tt, FFS), lambda k, t: (t, k)),
        pl.BlockSpec((tt, D), lambda k, t: (t, 0)),
        pl.BlockSpec((FFS, D), lambda k, t: (k, 0)),
        SDS((FF, D), F32), 2, tt, job)
    return g.reshape(NCHIP, FF // NCHIP, D), job_out


def _dw_square(name, a, b, job=None):
    tt = min(DW_TOKENS, a.shape[0])
    g, job_out = _dw_call(
        name, a, b,
        pl.BlockSpec((tt, D), lambda k, t: (t, 0)), pl.BlockSpec((tt, D), lambda k, t: (t, 0)),
        pl.BlockSpec((D, D), lambda k, t: (0, 0)), SDS((D, D), F32), 1, tt, job)
    return g.reshape(NCHIP, D // NCHIP, D), job_out


def _place():
    x, y, c = lax.axis_index("x"), lax.axis_index("y"), lax.axis_index("c")
    return x, y, c, 2 * x + y


def _chip_at(x, y, s):
    return x ^ (s >> 1), y ^ (s & 1)


class _Job:
    def __init__(self, ins, out_shapes, sems, start, finish, aliases=None, mid=None):
        self.ins, self.out_shapes, self.sems = list(ins), list(out_shapes), list(sems)
        self.start, self.finish, self.aliases = start, finish, dict(aliases or {})
        self.mid = mid if mid is not None else (lambda ins, outs, sems: None)


def _join_jobs(*jobs):
    def cut(refs, sizes):
        out, at = [], 0
        for n in sizes:
            out.append(refs[at:at + n])
            at += n
        return out

    ni = [len(j.ins) for j in jobs]
    no = [len(j.out_shapes) for j in jobs]
    ns = [len(j.sems) for j in jobs]

    def run(which):
        def go(ins, outs, sems):
            for j, a, b, c in zip(jobs, cut(ins, ni), cut(outs, no), cut(sems, ns)):
                getattr(j, which)(a, b, c)
        return go

    aliases = {}
    for k, j in enumerate(jobs):
        for a, b in j.aliases.items():
            aliases[sum(ni[:k]) + a] = sum(no[:k]) + b
    return _Job([a for j in jobs for a in j.ins], [o for j in jobs for o in j.out_shapes],
                [s for j in jobs for s in j.sems], run("start"), run("finish"), aliases, run("mid"))


def _call(body, *, name, grid, in_specs, out_specs, out_shape, args, scratch_shapes=(), aliases=None,
          job=None, prefetch=None):
    n_in, n_out, n_scr = len(in_specs), len(out_specs), len(scratch_shapes)
    npf = 0 if prefetch is None else 1
    job = job if job is not None else _Job([], [], [], lambda *a: None, lambda *a: None)
    ji, jo = len(job.ins), len(job.out_shapes)
    steps = math.prod(grid)

    def wrapped(*refs):
        pf, refs = refs[:npf], refs[npf:]
        ins, jin = refs[:n_in], refs[n_in:n_in + ji]
        o0 = n_in + ji
        outs, jout = refs[o0:o0 + n_out], refs[o0 + n_out:o0 + n_out + jo]
        s0 = o0 + n_out + jo
        scr, jsem = refs[s0:s0 + n_scr], refs[s0 + n_scr:]
        step = functools.reduce(lambda acc, ag: acc * ag[1] + pl.program_id(ag[0]), enumerate(grid), 0)
        if ji or jo:
            @pl.when(step == 0)
            def _():
                job.start(jin, jout, jsem)

        body(*pf, *ins, *outs, *scr)

        if ji or jo:
            @pl.when(step == steps // 2)
            def _():
                job.mid(jin, jout, jsem)

            @pl.when(step == steps - 1)
            def _():
                job.finish(jin, jout, jsem)

    io = {npf + a: b for a, b in dict(aliases or {}).items()}
    io.update({npf + n_in + a: n_out + b for a, b in job.aliases.items()})
    kw = dict(in_specs=list(in_specs) + [ANY] * ji, out_specs=list(out_specs) + [ANY] * jo,
              scratch_shapes=list(scratch_shapes) + job.sems)
    if npf:
        kw = dict(grid_spec=pltpu.PrefetchScalarGridSpec(num_scalar_prefetch=1, grid=grid, **kw))
    else:
        kw["grid"] = grid
    res = pl.pallas_call(
        wrapped, name=name, out_shape=list(out_shape) + job.out_shapes, input_output_aliases=io,
        compiler_params=_cparams(has_side_effects=bool(ji or jo)), **kw,
    )(*(() if prefetch is None else (prefetch,)), *args, *job.ins)
    return list(res[:n_out]), list(res[n_out:])


def _cast_shards(name, place, ws):
    n = len(ws)
    rows, cols = ws[0].shape
    tr = 352 if rows % 352 == 0 else 256

    def body(pc_ref, *refs):
        del pc_ref
        for w_ref, o_ref in zip(refs[:n], refs[n:]):
            o_ref[...] = w_ref[...].astype(BF16)

    return pl.pallas_call(
        body, name=name,
        grid_spec=pltpu.PrefetchScalarGridSpec(
            num_scalar_prefetch=1, grid=(rows // tr,),
            in_specs=[pl.BlockSpec((tr, cols), lambda i, pc: (i, 0))] * n,
            out_specs=[pl.BlockSpec((None, tr, cols), lambda i, pc: (pc[0], i, 0))] * n),
        out_shape=[SDS((NCHIP, rows, cols), BF16)] * n,
        compiler_params=_cparams(),
    )(place, *ws)


def _sibling_copy(ref, send_sem, recv_sem):
    x, y, c, _ = _place()
    return pltpu.make_async_remote_copy(src_ref=ref, dst_ref=ref, send_sem=send_sem, recv_sem=recv_sem,
                                        device_id=(x, y, 1 - c), device_id_type=MESH)


def _half_rows(arr, slot, core):
    half = arr.shape[1] // 2
    return arr.at[slot, pl.ds(pl.multiple_of(core * half, 16), half)]


def _quarter_rows(arr, slot, core, q):
    quarter = arr.shape[1] // 4
    return arr.at[slot, pl.ds(pl.multiple_of((2 * core + q) * quarter, 16), quarter)]


def _chip_copy(ref, dist, send_sem, recv_sem):
    x, y, c, _ = _place()
    cx, cy = _chip_at(x, y, dist)
    return pltpu.make_async_remote_copy(src_ref=ref, dst_ref=ref, send_sem=send_sem, recv_sem=recv_sem,
                                        device_id=(cx, cy, c), device_id_type=MESH)


def _gather_sems(n):
    dma = pltpu.SemaphoreType.DMA
    return [dma((n, 2))] * 4 + [dma((n, 4))] * 2


def _gather_start(arrs, sems):
    dsend, drecv = sems[0], sems[1]
    _, _, c, j = _place()
    for w, arr in enumerate(arrs):
        for dist in (1, 2):
            _chip_copy(_half_rows(arr, j, c), dist, dsend.at[w, dist - 1], drecv.at[w, dist - 1]).start()


def _gather_land(arrs, sems, dist, first=0):
    dsend, drecv, rsend, rrecv, fsend, frecv = sems
    _, _, c, j = _place()
    if dist < 3:
        other = 3 - dist
        for w, arr in enumerate(arrs, first):
            landed = _half_rows(arr, j ^ dist, c)
            _chip_copy(landed, dist, dsend.at[w, dist - 1], drecv.at[w, dist - 1]).wait_recv()
            relay = _quarter_rows(arr, j ^ dist, c, other - 1)
            _chip_copy(relay, other, rsend.at[w, other - 1], rrecv.at[w, other - 1]).start()
            _sibling_copy(landed, fsend.at[w, dist - 1], frecv.at[w, dist - 1]).start()
        for w, arr in enumerate(arrs, first):
            theirs = _half_rows(arr, j ^ dist, 1 - c)
            _sibling_copy(theirs, fsend.at[w, dist - 1], frecv.at[w, dist - 1]).wait_recv()
    else:
        for w, arr in enumerate(arrs, first):
            for via in (1, 2):
                piece = _quarter_rows(arr, j ^ 3, c, via - 1)
                _chip_copy(piece, via, rsend.at[w, via - 1], rrecv.at[w, via - 1]).wait_recv()
                _sibling_copy(piece, fsend.at[w, 1 + via], frecv.at[w, 1 + via]).start()
        for w, arr in enumerate(arrs, first):
            for via in (1, 2):
                theirs = _quarter_rows(arr, j ^ 3, 1 - c, via - 1)
                _sibling_copy(theirs, fsend.at[w, 1 + via], frecv.at[w, 1 + via]).wait_recv()


def _gather_drain(arrs, sems):
    dsend, drecv, rsend, rrecv, fsend, frecv = sems
    _, _, c, j = _place()
    for w, arr in enumerate(arrs):
        for dist in (1, 2):
            other = 3 - dist
            _chip_copy(_half_rows(arr, j, c), dist, dsend.at[w, dist - 1], drecv.at[w, dist - 1]).wait_send()
            _chip_copy(_quarter_rows(arr, j ^ dist, c, other - 1), other,
                       rsend.at[w, other - 1], rrecv.at[w, other - 1]).wait_send()
            _sibling_copy(_half_rows(arr, j ^ dist, c), fsend.at[w, dist - 1], frecv.at[w, dist - 1]).wait_send()
            _sibling_copy(_quarter_rows(arr, j ^ 3, c, dist - 1),
                          fsend.at[w, 1 + dist], frecv.at[w, 1 + dist]).wait_send()


def _gather_neighbours(arrs, sems):
    _gather_land(arrs, sems, 1)
    _gather_land(arrs, sems, 2)


def _gather_finish(arrs, sems):
    _gather_land(arrs, sems, 3)
    _gather_drain(arrs, sems)


def _gather_job(arrs):
    n = len(arrs)
    return _Job(arrs, [SDS(a.shape, a.dtype) for a in arrs], _gather_sems(n),
                lambda ins, outs, sems: _gather_start(outs, sems),
                lambda ins, outs, sems: _gather_finish(outs, sems), {k: k for k in range(n)},
                mid=lambda ins, outs, sems: _gather_neighbours(outs, sems))


def _exchange_job(arrs, out_shapes, n, copies):
    def start(ins, outs, sems):
        for cp in copies(ins, outs, sems[0], sems[1]):
            cp.start()

    def finish(ins, outs, sems):
        for cp in copies(ins, outs, sems[0], sems[1]):
            cp.wait()

    return _Job(arrs, out_shapes, [pltpu.SemaphoreType.DMA((n,))] * 2, start, finish)


def _pair_exchange_job(grads):
    def copies(ins, outs, send_sem, recv_sem):
        x, y, c, _ = _place()
        res = []
        for w in range(len(grads)):
            half = ins[w].shape[1] // 2
            theirs = pl.ds(pl.multiple_of((1 - c) * half, 8), half)
            res.append(pltpu.make_async_remote_copy(
                src_ref=ins[w].at[:, theirs, :], dst_ref=outs[w], send_sem=send_sem.at[w],
                recv_sem=recv_sem.at[w], device_id=(x, y, 1 - c), device_id_type=MESH))
        return res

    return _exchange_job(grads, [SDS((NCHIP, g.shape[1] // 2, g.shape[2]), F32) for g in grads],
                         len(grads), copies)


def _row_tile(rows, cols):
    tr = rows
    while tr * cols * 4 > ELEMENTWISE_BLOCK_BYTES and tr % 32 == 0:
        tr //= 2
    return tr


def _pair_sums(name, place, gs, sibs):
    n = len(gs)
    half, cols = sibs[0].shape[1], sibs[0].shape[2]
    tr = _row_tile(half, cols)
    nt = half // tr
    mine = nt if gs[0].shape[1] == 2 * half else 0

    def body(pc_ref, *refs):
        del pc_ref
        for g_ref, s_ref, own_ref, out_ref in zip(refs[:n], refs[n:2 * n], refs[2 * n:3 * n], refs[3 * n:]):
            v = g_ref[...] + s_ref[...]
            out_ref[...] = v.astype(BF16)

            @pl.when(pl.program_id(1) == 0)
            def _():
                own_ref[...] = v

    res = pl.pallas_call(
        body, name=name,
        grid_spec=pltpu.PrefetchScalarGridSpec(
            num_scalar_prefetch=1, grid=(nt, NCHIP),
            in_specs=[pl.BlockSpec((None, tr, cols), lambda i, s, pc: (pc[0] ^ s, pc[1] * mine + i, 0))] * n
            + [pl.BlockSpec((None, tr, cols), lambda i, s, pc: (pc[0] ^ s, i, 0))] * n,
            out_specs=[pl.BlockSpec((tr, cols), lambda i, s, pc: (i, 0))] * n
            + [pl.BlockSpec((None, tr, cols), lambda i, s, pc: (s, i, 0))] * n),
        out_shape=[SDS((half, cols), F32)] * n + [SDS((NCHIP, half, cols), BF16)] * n,
        compiler_params=_cparams(),
    )(place, *gs, *sibs)
    return res[:n], res[n:]


def _chip_exchange_job(parts):
    def copies(ins, outs, send_sem, recv_sem):
        x, y, c, _ = _place()
        res = []
        for w in range(len(parts)):
            for s in range(1, NCHIP):
                cx, cy = _chip_at(x, y, s)
                k = w * (NCHIP - 1) + s - 1
                res.append(pltpu.make_async_remote_copy(
                    src_ref=ins[w].at[s], dst_ref=outs[w].at[s - 1], send_sem=send_sem.at[k],
                    recv_sem=recv_sem.at[k], device_id=(cx, cy, c), device_id_type=MESH))
        return res

    return _exchange_job(parts, [SDS((NCHIP - 1,) + p.shape[1:], BF16) for p in parts],
                         len(parts) * (NCHIP - 1), copies)


def _chip_sums(name, owns, rems):
    n = len(owns)
    half, cols = owns[0].shape
    tr = _row_tile(half, cols)

    def body(*refs):
        for own_ref, rem_ref, out_ref in zip(refs[:n], refs[n:2 * n], refs[2 * n:]):
            out_ref[...] = (((own_ref[...] + rem_ref[0].astype(F32)) + rem_ref[1].astype(F32))
                            + rem_ref[2].astype(F32))

    return pl.pallas_call(
        body, name=name, grid=(half // tr,),
        in_specs=[pl.BlockSpec((tr, cols), lambda i: (i, 0))] * n
        + [pl.BlockSpec((NCHIP - 1, tr, cols), lambda i: (0, i, 0))] * n,
        out_specs=[pl.BlockSpec((tr, cols), lambda i: (i, 0))] * n,
        out_shape=[SDS((half, cols), F32)] * n,
        compiler_params=_cparams(),
    )(*owns, *rems)


def _share_halves_job(halves):
    def copies(ins, outs, send_sem, recv_sem):
        x, y, c, _ = _place()
        return [pltpu.make_async_remote_copy(
            src_ref=ins[w], dst_ref=outs[w], send_sem=send_sem.at[w], recv_sem=recv_sem.at[w],
            device_id=(x, y, 1 - c), device_id_type=MESH) for w in range(len(halves))]

    return _exchange_job(halves, [SDS(h.shape, F32) for h in halves], len(halves), copies)


def _adamw_math(w, g, m, v):
    m = B1 * m + (1.0 - B1) * g
    v = B2 * v + (1.0 - B2) * (g * g)
    m_hat = m / (1.0 - B1 ** STEP)
    v_hat = v / (1.0 - B2 ** STEP)
    delta = -LR * (m_hat / (jnp.sqrt(v_hat) + AEPS) + WD * w)
    return delta, m, v


def _adamws(name, place, ws, owns, sibs, ms, vs):
    n = len(ws)
    rows, cols = ws[0].shape
    by_cols = owns[0].shape[0] == rows
    half, pc_cols = (rows, cols // 2) if by_cols else (rows // 2, cols)
    tr = _row_tile(half, pc_cols)
    nt = half // tr

    def body(pc_ref, *refs):
        ins, outs = refs[:5 * n], refs[5 * n:]
        for k in range(n):
            w_ref, own_ref, sib_ref, m_ref, v_ref = ins[5 * k:5 * k + 5]
            g = jnp.where(pl.program_id(0) == pc_ref[1], own_ref[...], sib_ref[...])
            d, mn, vn = _adamw_math(w_ref[...], g, m_ref[...], v_ref[...])
            for ref, val in zip(outs[4 * k:4 * k + 4], (g, d, mn, vn)):
                ref[...] = val

    full = pl.BlockSpec((tr, pc_cols), (lambda h, i, pc: (i, h)) if by_cols else (lambda h, i, pc: (h * nt + i, 0)))
    part = pl.BlockSpec((tr, pc_cols), lambda h, i, pc: (i, 0))
    res = pl.pallas_call(
        body, name=name,
        grid_spec=pltpu.PrefetchScalarGridSpec(
            num_scalar_prefetch=1, grid=(2, nt),
            in_specs=[full, part, part, full, full] * n, out_specs=[full] * (4 * n)),
        out_shape=[SDS((rows, cols), F32)] * (4 * n),
        compiler_params=_cparams(),
    )(place, *[a for group in zip(ws, owns, sibs, ms, vs) for a in group])
    return [tuple(res[4 * k:4 * k + 4]) for k in range(n)]


def _small_allreduce_adamw(sp, w, m, v, job):
    shape = sp.shape
    ji, jo = len(job.ins), len(job.out_shapes)

    def body(sp_ref, w_ref, m_ref, v_ref, *rest):
        jin, (g_ref, d_ref, mo_ref, vo_ref), jout = rest[:ji], rest[ji:ji + 4], rest[ji + 4:ji + 4 + jo]
        sib_s, pair_s, chip_s, send_sem, recv_sem = rest[ji + 4 + jo:ji + 9 + jo]
        jsem = rest[ji + 9 + jo:]
        job.start(jin, jout, jsem)
        x, y, c, j = _place()
        cp = pltpu.make_async_remote_copy(
            src_ref=sp_ref, dst_ref=sib_s, send_sem=send_sem.at[0], recv_sem=recv_sem.at[0],
            device_id=(x, y, 1 - c), device_id_type=MESH)
        cp.start()
        cp.wait()
        pair_s[...] = sp_ref[...] + sib_s[...]
        half = shape[0] // 2
        mine = pl.ds(pl.multiple_of(c * half, 8), half)
        cps = []
        for s in range(1, NCHIP):
            cx, cy = _chip_at(x, y, s)
            cp = pltpu.make_async_remote_copy(
                src_ref=pair_s.at[mine], dst_ref=chip_s.at[s, mine], send_sem=send_sem.at[s],
                recv_sem=recv_sem.at[s], device_id=(cx, cy, c), device_id_type=MESH)
            cp.start()
            cps.append(cp)
        chip_s[0] = pair_s[...]
        for cp in cps:
            cp.wait()
        cps = []
        for s in range(1, NCHIP):
            cp = pltpu.make_async_remote_copy(
                src_ref=chip_s.at[s, mine], dst_ref=chip_s.at[s, mine], send_sem=send_sem.at[NCHIP + s],
                recv_sem=recv_sem.at[NCHIP + s], device_id=(x, y, 1 - c), device_id_type=MESH)
            cp.start()
            cps.append(cp)
        for cp in cps:
            cp.wait()
        tot = chip_s[j]
        for k in range(1, NCHIP):
            tot = tot + chip_s[k ^ j]
        g_ref[...] = tot
        d, mn, vn = _adamw_math(w_ref[...], tot, m_ref[...], v_ref[...])
        d_ref[...] = d
        mo_ref[...] = mn
        vo_ref[...] = vn
        job.mid(jin, jout, jsem)
        job.finish(jin, jout, jsem)

    vm = pl.BlockSpec(memory_space=pltpu.VMEM)
    res = pl.pallas_call(
        body, name="small_allreduce_adamw",
        in_specs=[vm] * 4 + [ANY] * ji, out_specs=[vm] * 4 + [ANY] * jo,
        out_shape=[SDS(shape, F32)] * 4 + job.out_shapes,
        scratch_shapes=[pltpu.VMEM(shape, F32), pltpu.VMEM(shape, F32), pltpu.VMEM((NCHIP,) + shape, F32),
                        pltpu.SemaphoreType.DMA((2 * NCHIP,)), pltpu.SemaphoreType.DMA((2 * NCHIP,))] + job.sems,
        input_output_aliases={4 + a: 4 + b for a, b in job.aliases.items()},
        compiler_params=pltpu.CompilerParams(has_side_effects=True),
    )(sp, w, m, v, *job.ins)
    return res[:4], res[4:]


def _pack_small(first, mix, ln_g, ln_b, b_s, lbt, hn, ffn, fin, w_s):
    rows = [first.reshape(1, D), mix.reshape(1, D), ln_g.reshape(1, D), ln_b.reshape(1, D),
            b_s.reshape(1, D), lbt.reshape(2, D), hn.reshape(1, D), ffn.reshape(1, D), fin.reshape(1, D),
            jnp.zeros((6, D), F32)]
    return jnp.concatenate(rows + [w_s.reshape(NG, GCH, GCH).transpose(1, 0, 2).reshape(GCH, D)], axis=0)


def _unpack_small(p):
    w_s = p[16:].reshape(GCH, NG, GCH).transpose(1, 0, 2).reshape(1, NG, GCH, GCH)
    return dict(norm_mix_g=p[1:2], gmlp_ln_g=p[2:3], gmlp_ln_b=p[3:4], gmlp_b_s=p[4].reshape(1, NG, GCH),
                hgrn_lb_table=p[5:7], hgrn_norm_g=p[7:8], norm_ffn_g=p[8:9], norm_final_g=p[9],
                gmlp_w_s=w_s)


SMALL = ("norm_mix_g", "gmlp_ln_g", "gmlp_ln_b", "gmlp_w_s", "gmlp_b_s", "hgrn_lb_table", "hgrn_norm_g",
         "norm_ffn_g", "norm_final_g")
BIG = ("w_in", "w_gate_up", "w_branch_a", "w_branch_b", "w_out", "w_down")
ORDER = ("norm_mix_g", "w_in", "gmlp_ln_g", "gmlp_ln_b", "gmlp_w_s", "gmlp_b_s", "hgrn_lb_table",
         "hgrn_norm_g", "w_branch_a", "w_branch_b", "w_out", "norm_ffn_g", "w_gate_up", "w_down",
         "norm_final_g")


def kernel(x, norm_mix_g, w_in, gmlp_ln_g, gmlp_ln_b, gmlp_w_s, gmlp_b_s, hgrn_lb_table, hgrn_norm_g, w_branch_a, w_branch_b, w_out, norm_ffn_g, w_gate_up, w_down, norm_final_g, loss_target, m_norm_mix_g, m_w_in, m_gmlp_ln_g, m_gmlp_ln_b, m_gmlp_w_s, m_gmlp_b_s, m_hgrn_lb_table, m_hgrn_norm_g, m_w_branch_a, m_w_branch_b, m_w_out, m_norm_ffn_g, m_w_gate_up, m_w_down, m_norm_final_g, v_norm_mix_g, v_w_in, v_gmlp_ln_g, v_gmlp_ln_b, v_gmlp_w_s, v_gmlp_b_s, v_hgrn_lb_table, v_hgrn_norm_g, v_w_branch_a, v_w_branch_b, v_w_out, v_norm_ffn_g, v_w_gate_up, v_w_down, v_norm_final_g):
    args = dict(locals())
    T = x.shape[1]
    xs = x.reshape(T, D)
    target = loss_target.reshape(T, D)
    big = {n: args[n].reshape(args[n].shape[1:]) for n in BIG}
    big_m = {n: args["m_" + n].reshape(args[n].shape[1:]) for n in BIG}
    big_v = {n: args["v_" + n].reshape(args[n].shape[1:]) for n in BIG}

    x_i, y_i, c_i = lax.axis_index("x"), lax.axis_index("y"), lax.axis_index("c")
    place = jnp.stack([2 * x_i + y_i, c_i]).astype(jnp.int32)
    def by_shape(names):
        groups = []
        for n in names:
            if groups and big[groups[-1][0]].shape == big[n].shape:
                groups[-1].append(n)
            else:
                groups.append([n])
        return groups

    cast = {}
    for grp in by_shape(BIG):
        cast.update(zip(grp, _cast_shards("cast_" + grp[0], place, [big[n] for n in grp])))
    tril = jnp.tril(jnp.ones((GCH, GCH), bool))
    wm = jnp.where(tril, gmlp_w_s[0], 0.0).astype(BF16)
    wm_t = jnp.swapaxes(wm, 1, 2)
    b_t = gmlp_b_s[0].T

    (proj, hb), w_in4, (w_a4, w_b4, w_out4, w_down4) = _proj_fwd(
        place, xs, norm_mix_g, cast["w_in"], [cast[n] for n in ("w_branch_a", "w_branch_b", "w_out", "w_down")])
    (ab,), _ = _gmlp_fwd(proj, gmlp_ln_g, gmlp_ln_b, wm, b_t)
    (o_raw, obb, st_before), (w_gu4,) = _hgrn_fwd(
        proj, hgrn_lb_table, hgrn_norm_g, job=_gather_job([cast["w_gate_up"]]))
    w_a, w_b, w_o = (w.reshape(D, D) for w in (w_a4, w_b4, w_out4))
    (mgb, x1), _ = _merge_fwd(xs, ab, obb, proj, w_a, w_b, w_o)
    w_dn = w_down4.reshape(FF, D)
    act, dx2b, h2b, dgu4, dx1, dx1b, acc_ffn = _ffn_fwd_bwd(
        x1, target, norm_ffn_g, norm_final_g.reshape(1, D), w_gu4, w_dn)

    grads, owns, parts, halves, sibh = {}, {}, {}, {}, {}

    def pair_sums(names, sibs):
        sib_of = dict(zip(names, sibs))
        for grp in by_shape(names):
            o, p = _pair_sums("rs_pair_sum_" + grp[0], place, [grads[n] for n in grp], [sib_of[n] for n in grp])
            owns.update(zip(grp, o))
            parts.update(zip(grp, p))

    def chip_sums(names, got):
        rem_of = dict(zip(names, got))
        for grp in by_shape(names):
            h = _chip_sums("rs_chip_sum_" + grp[0], [owns[n] for n in grp], [rem_of[n] for n in grp])
            halves.update(zip(grp, h))

    ffn, mix = ("w_gate_up", "w_down"), ("w_branch_a", "w_branch_b", "w_out")
    grads["w_gate_up"], _ = _dw_gate_up(h2b, dgu4)
    grads["w_down"], _ = _dw_down(act, dx2b)
    (dya, dyb, dproj), got = _merge_bwd(
        dx1b, ab, obb, proj, w_o, w_a, w_b, job=_pair_exchange_job([grads[n] for n in ffn]))
    pair_sums(ffn, got)
    grads["w_branch_a"], _ = _dw_square("dw_branch_a", ab, dya)
    grads["w_branch_b"], _ = _dw_square("dw_branch_b", obb, dyb)
    grads["w_out"], _ = _dw_square("dw_out", mgb, dx1b)
    (dproj, acc_hgrn), got = _hgrn_bwd(
        dproj, dyb, w_b, o_raw, proj, st_before, hgrn_lb_table, hgrn_norm_g,
        job=_join_jobs(_chip_exchange_job([parts[n] for n in ffn]), _pair_exchange_job([grads[n] for n in mix])))
    chip_sums(ffn, got[:2])
    pair_sums(mix, got[2:])
    dproj, acc_ln, dws, dmix = _gmlp_bwd(dproj, dya, w_a, proj, gmlp_ln_g, gmlp_ln_b, wm, wm_t, b_t)
    for_sibling, got = _dw_in_half(
        "dw_in_sibling_half", place, hb, dproj, False,
        job=_join_jobs(_share_halves_job([halves[n] for n in ffn]), _chip_exchange_job([parts[n] for n in mix])))
    sibh.update(zip(ffn, got[:2]))
    chip_sums(mix, got[2:])
    grads["w_in"], got = _dw_in_half(
        "dw_in_own_half", place, hb, dproj, True, job=_share_halves_job([for_sibling]))
    pair_sums(("w_in",), got)
    (grad_x, acc_mix), got = _proj_bwd(
        dproj, w_in4, xs, dx1, norm_mix_g,
        job=_join_jobs(_chip_exchange_job([parts["w_in"]]), _share_halves_job([halves[n] for n in mix])))
    chip_sums(("w_in",), got[:1])
    sibh.update(zip(mix, got[1:]))

    lbv = jax.nn.sigmoid(hgrn_lb_table[0] - hgrn_lb_table[1])
    d_t0 = jnp.sum(acc_hgrn[0], axis=0) * lbv * (1.0 - lbv)
    loss_row = jnp.zeros((D,), F32).at[0].set(jnp.sum(acc_ffn[0]))
    dws_m = jnp.where(tril[:, None, :], dws.reshape(GCH, NG, GCH), 0.0).transpose(1, 0, 2)
    db_s = jnp.sum(dmix.reshape(GCH, NG, GCH), axis=-1).T
    sp = _pack_small(loss_row, jnp.sum(acc_mix, 0), jnp.sum(acc_ln[0], 0), jnp.sum(acc_ln[1], 0), db_s,
                     jnp.stack([d_t0, -d_t0]), jnp.sum(acc_hgrn[1], 0), jnp.sum(acc_ffn[2], 0),
                     jnp.sum(acc_ffn[1], 0), dws_m)
    zero = jnp.zeros((D,), F32)

    def pack(prefix):
        a = lambda n: args[prefix + n]
        return _pack_small(zero, a("norm_mix_g"), a("gmlp_ln_g"), a("gmlp_ln_b"), a("gmlp_b_s"),
                           a("hgrn_lb_table"), a("hgrn_norm_g"), a("norm_ffn_g"), a("norm_final_g"),
                           a("gmlp_w_s"))

    packed, (sibh["w_in"],) = _small_allreduce_adamw(
        sp, pack(""), pack("m_"), pack("v_"), _share_halves_job([halves["w_in"]]))
    loss = packed[0][0, 0]
    small = [_unpack_small(p) for p in packed]
    out = {n: tuple(s[n] for s in small) for n in SMALL}
    for grp in by_shape(BIG):
        res = _adamws("adamw_" + grp[0], place, *[[d[n] for n in grp] for d in (big, halves, sibh, big_m, big_v)])
        for n, quad in zip(grp, res):
            out[n] = tuple(a.reshape(args[n].shape) for a in quad)
    return (loss, grad_x.reshape(x.shape), *[out[n][0] for n in ORDER], *[out[n][1] for n in ORDER],
            *[out[n][2] for n in ORDER], *[out[n][3] for n in ORDER])
```

```python
import functools
import math

import jax
import jax.numpy as jnp
from jax import lax
from jax.experimental import pallas as pl
from jax.experimental.pallas import tpu as pltpu

F32 = jnp.float32
BF16 = jnp.bfloat16
SDS = jax.ShapeDtypeStruct
MESH = pl.DeviceIdType.MESH
ANY = pl.BlockSpec(memory_space=pl.ANY)

D = 1024
NIN = 8
NG = 8
GCH = 128
NH = 8
HD = 128
HCH = 64
HGRN_HB = 8
HGRN_TOKENS = 256
HW = HGRN_HB * HD
DW_TOKENS = 2048
ELEMENTWISE_BLOCK_BYTES = 2 * 1024 * 1024
PROJ_OUT_SLOTS = 4
FF = 2816
FFS = 1408
NCHIP = 4
EPS = 1e-6
QSCALE = HD ** -0.5
GELU_C0 = math.sqrt(2.0 / math.pi)
GELU_C1 = 0.044715
LR, B1, B2, AEPS, WD, STEP = 0.001, 0.9, 0.999, 1e-08, 0.01, 10
VMEM_LIMIT_V7X = 56 * 1024 * 1024
SP_ROWS = 144


def _cparams(**kw):
    return pltpu.CompilerParams(vmem_limit_bytes=VMEM_LIMIT_V7X, **kw)


def _mm(a, b):
    return jnp.dot(a, b, preferred_element_type=F32)


def _mm_nt(a, b):
    return lax.dot_general(a, b, (((1,), (1,)), ((), ())), preferred_element_type=F32)


def _mm_tn(a, b):
    return lax.dot_general(a, b, (((0,), (0,)), ((), ())), preferred_element_type=F32)


def _rows8(x):
    r, c = x.shape
    return jnp.sum(x.reshape(r // 8, 8, c), axis=0)


def _mean(x):
    return jnp.mean(x, axis=-1, keepdims=True)


def _sigmoid(x):
    return 1.0 / (1.0 + jnp.exp(-x))


def _gelu(x):
    t = jnp.tanh(GELU_C0 * (x + GELU_C1 * x * x * x))
    return 0.5 * x * (1.0 + t), t


def _gelu_grad(x, t):
    return 0.5 * (1.0 + t) + 0.5 * x * (1.0 - t * t) * (GELU_C0 * (1.0 + 3.0 * GELU_C1 * x * x))


def _component_of(group):
    return jnp.where(group < 6, (group + 4) % 6, group)


def _proj_fwd(place, x, g_mix, w_in4, later):
    T = x.shape[0]
    tm = min(1024, T)
    ni = T // tm
    n = len(later)

    def body(pc_ref, x_ref, g_ref, *rest):
        proj_ref, h_ref, w_all = rest[1 + n:4 + n]
        gathered = rest[4 + n:4 + 2 * n]
        hs, wbuf, wsem, obuf, osem = rest[4 + 2 * n:9 + 2 * n]
        w_sems, later_sems = rest[9 + 2 * n:15 + 2 * n], rest[15 + 2 * n:]
        jp, i = pl.program_id(0), pl.program_id(1)
        w_cols = [w_all.at[:, :, pl.ds(k * D, D)] for k in range(2)]

        def w_copy(blk):
            cols = pl.ds(pl.multiple_of((blk % 2) * D, 128), D)
            return pltpu.make_async_copy(w_all.at[pc_ref[0] ^ (blk // 2), :, cols], wbuf.at[blk % 2],
                                         wsem.at[blk % 2])

        @pl.when((jp == 0) & (i == 0))
        def _():
            _gather_start(w_cols, w_sems)
            w_copy(jp).start()

        @pl.when(i == 0)
        def _():
            w_copy(jp).wait()

        @pl.when(jp == 0)
        def _():
            xv = x_ref[...]
            r = lax.rsqrt(_mean(xv * xv) + EPS)
            hb = (xv * r * g_ref[...]).astype(BF16)
            hs[i] = hb
            h_ref[...] = hb

        step = jp * ni + i
        slot = step % PROJ_OUT_SLOTS

        def o_copy(slot_):
            comp = 2 * (pc_ref[0] ^ (jp // 2)) + jp % 2
            return pltpu.make_async_copy(
                obuf.at[slot_], proj_ref.at[comp, pl.ds(pl.multiple_of(i * tm, 8), tm)], osem.at[slot_])

        @pl.when(step >= PROJ_OUT_SLOTS)
        def _():
            o_copy(slot).wait()

        obuf[slot] = _mm(hs[i], wbuf[jp % 2])
        o_copy(slot).start()

        @pl.when(step == NIN * ni - 1)
        def _():
            for k in range(PROJ_OUT_SLOTS):
                o_copy((slot + 1 + k) % PROJ_OUT_SLOTS).wait()

        for nxt in range(1, NIN):
            @pl.when((jp == nxt - 1) & (i == ni - 1))
            def _():
                if nxt >= 2:
                    _gather_land([w_cols[nxt % 2]], w_sems, nxt // 2, first=nxt % 2)
                if nxt == 5:
                    _gather_start(gathered, later_sems)
                if nxt == NIN - 1:
                    _gather_neighbours(gathered, later_sems)
                w_copy(jp + 1).start()

        @pl.when((jp == NIN - 1) & (i == ni - 1))
        def _():
            _gather_drain(w_cols, w_sems)
            _gather_finish(gathered, later_sems)

    tile = lambda jp, i, pc: (jnp.where(jp == 0, i, ni - 1), 0)
    res = pl.pallas_call(
        body, name="proj_fwd",
        grid_spec=pltpu.PrefetchScalarGridSpec(
            num_scalar_prefetch=1, grid=(NIN, ni),
            in_specs=[pl.BlockSpec((tm, D), tile), pl.BlockSpec((1, D), lambda jp, i, pc: (0, 0))] + [ANY] * (1 + n),
            out_specs=[ANY, pl.BlockSpec((tm, D), tile)] + [ANY] * (1 + n),
            scratch_shapes=[pltpu.VMEM((ni, tm, D), BF16), pltpu.VMEM((2, D, D), BF16),
                            pltpu.SemaphoreType.DMA((2,)), pltpu.VMEM((PROJ_OUT_SLOTS, tm, D), F32),
                            pltpu.SemaphoreType.DMA((PROJ_OUT_SLOTS,))] + _gather_sems(2) + _gather_sems(n)),
        out_shape=[SDS((NIN, T, D), F32), SDS((T, D), BF16), SDS(w_in4.shape, BF16)]
        + [SDS(a.shape, a.dtype) for a in later],
        input_output_aliases={3 + k: 2 + k for k in range(1 + n)},
        compiler_params=_cparams(has_side_effects=True),
    )(place, x, g_mix, w_in4, *later)
    return res[:2], res[2], res[3:]


def _layer_norm_stats(gv):
    mu = _mean(gv)
    xc = gv - mu
    rs = lax.rsqrt(_mean(xc * xc) + EPS)
    return xc * rs, rs


def _gmlp_fwd(proj, ln_g, ln_b, wm, b_t, job=None):
    T = proj.shape[1]
    tm = min(256, T)

    def body(u_ref, v_ref, lg_ref, lb_ref, wm_ref, bt_ref, a_ref, a_s):
        gu, _ = _gelu(u_ref[...])
        gv, _ = _gelu(v_ref[...])
        vhat, _ = _layer_norm_stats(gv)
        vnb = (vhat * lg_ref[...] + lb_ref[...]).astype(BF16)
        for ch in range(tm // GCH):
            rows = slice(GCH * ch, GCH * (ch + 1))
            for g in range(NG):
                cols = slice(128 * g, 128 * (g + 1))
                mixed = _mm(wm_ref[g], vnb[rows, cols]) + bt_ref[:, g:g + 1]
                a_s[rows, cols] = gu[rows, cols] * mixed
        a_ref[...] = a_s[...].astype(BF16)

    row = lambda i: (0, 0)
    return _call(
        body, name="gmlp_fwd", grid=(T // tm,), job=job, args=(proj, proj, ln_g, ln_b, wm, b_t),
        in_specs=[pl.BlockSpec((None, tm, D), lambda i: (0, i, 0)), pl.BlockSpec((None, tm, D), lambda i: (1, i, 0)),
                  pl.BlockSpec((1, D), row), pl.BlockSpec((1, D), row),
                  pl.BlockSpec((NG, GCH, GCH), lambda i: (0, 0, 0)), pl.BlockSpec((GCH, NG), row)],
        out_specs=[pl.BlockSpec((tm, D), lambda i: (i, 0))],
        out_shape=[SDS((T, D), BF16)],
        scratch_shapes=[pltpu.VMEM((tm, D), F32)])


def _cumsum64(x, row):
    for s in (1, 2, 4, 8, 16, 32):
        x = x + jnp.where(row >= s, pltpu.roll(x, s, 0), 0.0)
    return x


def _revcumsum64(x, row):
    n = x.shape[0]
    for s in (1, 2, 4, 8, 16, 32):
        x = x + jnp.where(row < HCH - s, pltpu.roll(x, n - s, 0), 0.0)
    return x


def _head_mean(x):
    parts = [jnp.broadcast_to(_mean(x[:, HD * h:HD * (h + 1)]), (x.shape[0], HD)) for h in range(x.shape[1] // HD)]
    return jnp.concatenate(parts, axis=1)


def _seg_sum(x):
    n, c = x.shape
    s = jnp.sum(x.reshape(n // HCH, HCH, c), axis=1, keepdims=True)
    return jnp.broadcast_to(s, (n // HCH, HCH, c)).reshape(n, c)


def _hgrn_gates(fl, lbv, row):
    s = _sigmoid(fl)
    f = lbv + (1.0 - lbv) * s
    a = _cumsum64(jnp.log(f), row)
    a_mid = _seg_sum(jnp.where(row == HCH // 2 - 1, a, 0.0))
    a_last = _seg_sum(jnp.where(row == HCH - 1, a, 0.0))
    return s, f, a, a_mid, a_last


def _hgrn_fwd(proj, lb_table, norm_g, job=None):
    T = proj.shape[1]
    tb = min(HGRN_TOKENS, T)
    nc = tb // HCH

    def body(q_ref, fl_ref, v_ref, g_ref, lbt_ref, gn_ref, o_ref, ob_ref, stb_ref, st_s, o_s):
        @pl.when(pl.program_id(1) == 0)
        def _():
            st_s[...] = jnp.zeros_like(st_s)

        row = lax.broadcasted_iota(jnp.int32, (tb, HW), 0) & (HCH - 1)
        lbv = _sigmoid(lbt_ref[0:1, :] - lbt_ref[1:2, :])
        _, f, a, a_mid, a_last = _hgrn_gates(fl_ref[...], lbv, row)
        k = 1.0 - f
        qs = q_ref[...] * QSCALE
        q_in = (qs * jnp.exp(a - a_mid)).astype(BF16)
        k_in = (k * jnp.exp(a_mid - a)).astype(BF16)
        q_a = (qs * jnp.exp(a)).astype(BF16)
        k_d = (k * jnp.exp(a_last - a)).astype(BF16)
        dec = jnp.exp(a_last)
        vb = v_ref[...].astype(BF16)
        tri = (lax.broadcasted_iota(jnp.int32, (HCH, HCH), 0)
               >= lax.broadcasted_iota(jnp.int32, (HCH, HCH), 1))
        for c in range(nc):
            sl = slice(HCH * c, HCH * (c + 1))
            for hh in range(HGRN_HB):
                hs = slice(HD * hh, HD * (hh + 1))
                st = st_s[hh]
                stb_ref[hh, c] = st
                sc = jnp.where(tri, _mm_nt(q_in[sl, hs], k_in[sl, hs]), 0.0)
                o_s[sl, hs] = _mm(sc.astype(BF16), vb[sl, hs]) + _mm_nt(q_a[sl, hs], st.astype(BF16))
                d64 = dec[sl, hs]
                st_s[hh] = st * jnp.concatenate([d64, d64], axis=0) + _mm_tn(vb[sl, hs], k_d[sl, hs])
        o = o_s[...]
        r = lax.rsqrt(_head_mean(o * o) + EPS)
        g = g_ref[...]
        o_ref[...] = o
        ob_ref[...] = (o * r * gn_ref[...] * (g * _sigmoid(g))).astype(BF16)

    def col(off):
        return pl.BlockSpec((None, tb, HW), lambda h, cb: (off, cb, h))

    return _call(
        body, name="hgrn_fwd", grid=(NH // HGRN_HB, T // tb), job=job,
        args=(proj, proj, proj, proj, lb_table, norm_g),
        in_specs=[col(2), col(3), col(4), col(5),
                  pl.BlockSpec((2, HW), lambda h, cb: (0, h)), pl.BlockSpec((1, HW), lambda h, cb: (0, h))],
        out_specs=[pl.BlockSpec((tb, HW), lambda h, cb: (cb, h)), pl.BlockSpec((tb, HW), lambda h, cb: (cb, h)),
                   pl.BlockSpec((HGRN_HB, nc, HD, HD), lambda h, cb: (h, cb, 0, 0))],
        out_shape=[SDS((T, D), F32), SDS((T, D), BF16), SDS((NH, T // HCH, HD, HD), F32)],
        scratch_shapes=[pltpu.VMEM((HGRN_HB, HD, HD), F32), pltpu.VMEM((tb, HW), F32)])


def _merge_fwd(x, ab, ob, proj, w_a, w_b, w_out, job=None):
    T = x.shape[0]
    tm = min(512, T)

    def body(x_ref, ab_ref, ob_ref, ga_ref, gb_ref, wa_ref, wb_ref, wo_ref, mg_ref, x1_ref):
        ya = _mm(ab_ref[...], wa_ref[...])
        yb = _mm(ob_ref[...], wb_ref[...])
        merged = (_sigmoid(ga_ref[...]) * ya + _sigmoid(gb_ref[...]) * yb).astype(BF16)
        mg_ref[...] = merged
        x1_ref[...] = x_ref[...] + _mm(merged, wo_ref[...])

    t = lambda i: (i, 0)
    w = lambda i: (0, 0)
    return _call(
        body, name="merge_fwd", grid=(T // tm,), job=job, args=(x, ab, ob, proj, proj, w_a, w_b, w_out),
        in_specs=[pl.BlockSpec((tm, D), t), pl.BlockSpec((tm, D), t), pl.BlockSpec((tm, D), t),
                  pl.BlockSpec((None, tm, D), lambda i: (6, i, 0)), pl.BlockSpec((None, tm, D), lambda i: (7, i, 0)),
                  pl.BlockSpec((D, D), w), pl.BlockSpec((D, D), w), pl.BlockSpec((D, D), w)],
        out_specs=[pl.BlockSpec((tm, D), t)] * 2,
        out_shape=[SDS((T, D), BF16), SDS((T, D), F32)])


def _ffn_fwd_bwd(x1, target, g_ffn, g_fin, w_gu4, w_down):
    T = x1.shape[0]
    tm = min(256, T)
    inv_d = 1.0 / D

    def body(x1_ref, tg_ref, gf_ref, gn_ref, wgu_ref, wd_ref,
             act_ref, dx2b_ref, h2b_ref, dgu_ref, dx1_ref, dx1b_ref, acc_ref):
        @pl.when(pl.program_id(0) == 0)
        def _():
            acc_ref[...] = jnp.zeros_like(acc_ref)

        x1v = x1_ref[...]
        gf = gf_ref[...]
        gn = gn_ref[...]
        rr1 = lax.rsqrt(_mean(x1v * x1v) + EPS)
        x1n = x1v * rr1
        h2b = (x1n * gf).astype(BF16)
        h2b_ref[...] = h2b
        p = [_mm(h2b, wgu_ref[k]) for k in range(NCHIP)]
        sg = [_sigmoid(p[0]), _sigmoid(p[1])]
        si = [p[0] * sg[0], p[1] * sg[1]]
        x2 = x1v
        for k in range(2):
            actk = (si[k] * p[2 + k]).astype(BF16)
            act_ref[:, FFS * k:FFS * (k + 1)] = actk
            x2 = x2 + _mm(actk, wd_ref[FFS * k:FFS * (k + 1), :])
        rr2 = lax.rsqrt(_mean(x2 * x2) + EPS)
        x2n = x2 * rr2
        e = x2n * gn - tg_ref[...]
        acc_ref[0] += _rows8(e * e) * (0.5 * inv_d)
        dy = e * inv_d
        acc_ref[1] += _rows8(dy * x2n)
        dxn = dy * gn
        dx2 = rr2 * (dxn - x2n * _mean(dxn * x2n))
        dx2b = dx2.astype(BF16)
        dx2b_ref[...] = dx2b
        dh2 = None
        for k in range(2):
            dact = _mm_nt(dx2b, wd_ref[FFS * k:FFS * (k + 1), :])
            dgate = (dact * p[2 + k] * (sg[k] * (1.0 + p[k] * (1.0 - sg[k])))).astype(BF16)
            dup = (dact * si[k]).astype(BF16)
            dgu_ref[k] = dgate
            dgu_ref[2 + k] = dup
            part = _mm_nt(dgate, wgu_ref[k]) + _mm_nt(dup, wgu_ref[2 + k])
            dh2 = part if dh2 is None else dh2 + part
        acc_ref[2] += _rows8(dh2 * x1n)
        dxn1 = dh2 * gf
        dx1 = dx2 + rr1 * (dxn1 - x1n * _mean(dxn1 * x1n))
        dx1_ref[...] = dx1
        dx1b_ref[...] = dx1.astype(BF16)

    t = lambda i: (i, 0)
    w = lambda i: (0, 0)
    one = pl.Buffered(1)
    return pl.pallas_call(
        body, name="ffn_fwd_bwd", grid=(T // tm,),
        in_specs=[pl.BlockSpec((tm, D), t), pl.BlockSpec((tm, D), t),
                  pl.BlockSpec((1, D), w), pl.BlockSpec((1, D), w),
                  pl.BlockSpec((NCHIP, D, FFS), lambda i: (0, 0, 0), pipeline_mode=one),
                  pl.BlockSpec((FF, D), w, pipeline_mode=one)],
        out_specs=[pl.BlockSpec((tm, FF), t), pl.BlockSpec((tm, D), t), pl.BlockSpec((tm, D), t),
                   pl.BlockSpec((NCHIP, tm, FFS), lambda i: (0, i, 0)),
                   pl.BlockSpec((tm, D), t), pl.BlockSpec((tm, D), t),
                   pl.BlockSpec((3, 8, D), lambda i: (0, 0, 0))],
        out_shape=[SDS((T, FF), BF16), SDS((T, D), BF16), SDS((T, D), BF16),
                   SDS((NCHIP, T, FFS), BF16), SDS((T, D), F32), SDS((T, D), BF16),
                   SDS((3, 8, D), F32)],
        compiler_params=_cparams(),
    )(x1, target, g_ffn, g_fin, w_gu4, w_down)


def _merge_bwd(dx1b, ab, ob, proj, w_out, w_a, w_b, job=None):
    T = dx1b.shape[0]
    tm = min(512, T)

    def body(dx_ref, ab_ref, ob_ref, ga_ref, gb_ref, wo_ref, wa_ref, wb_ref, dya_ref, dyb_ref, dp_ref):
        dm = _mm_nt(dx_ref[...], wo_ref[...])
        sa = _sigmoid(ga_ref[...])
        sb = _sigmoid(gb_ref[...])
        dya_ref[...] = (dm * sa).astype(BF16)
        dyb_ref[...] = (dm * sb).astype(BF16)
        dp_ref[0] = (dm * _mm(ab_ref[...], wa_ref[...]) * sa * (1.0 - sa)).astype(BF16)
        dp_ref[1] = (dm * _mm(ob_ref[...], wb_ref[...]) * sb * (1.0 - sb)).astype(BF16)

    t = lambda i: (i, 0)
    w = lambda i: (0, 0)
    return _call(
        body, name="merge_bwd", grid=(T // tm,),
        in_specs=[pl.BlockSpec((tm, D), t), pl.BlockSpec((tm, D), t), pl.BlockSpec((tm, D), t),
                  pl.BlockSpec((None, tm, D), lambda i: (6, i, 0)), pl.BlockSpec((None, tm, D), lambda i: (7, i, 0)),
                  pl.BlockSpec((D, D), w), pl.BlockSpec((D, D), w), pl.BlockSpec((D, D), w)],
        out_specs=[pl.BlockSpec((tm, D), t)] * 2 + [pl.BlockSpec((2, tm, D), lambda i: (3, i, 0))],
        out_shape=[SDS((T, D), BF16), SDS((T, D), BF16), SDS((NIN, T, D), BF16)],
        args=(dx1b, ab, ob, proj, proj, w_out, w_a, w_b), job=job)


def _hgrn_bwd(dproj, dyb, w_b, o_raw, proj, st_before, lb_table, norm_g, job=None):
    T = dyb.shape[0]
    tb = min(HGRN_TOKENS, T)
    nc = tb // HCH
    nb = T // tb

    def body(dp_in, dyb_ref, wb_ref, o_ref, q_ref, fl_ref, v_ref, g_ref, stb_ref, lbt_ref, gn_ref,
             dp_ref, acc_ref, dst_s, dqin_s, dqa_s, dkin_s, dkd_s, dv_s, ddec_s):
        del dp_in

        @pl.when(pl.program_id(1) == 0)
        def _():
            dst_s[...] = jnp.zeros_like(dst_s)
            acc_ref[...] = jnp.zeros_like(acc_ref)

        row = lax.broadcasted_iota(jnp.int32, (tb, HW), 0) & (HCH - 1)
        gn = gn_ref[...]
        lbv = _sigmoid(lbt_ref[0:1, :] - lbt_ref[1:2, :])
        o = o_ref[...]
        r = lax.rsqrt(_head_mean(o * o) + EPS)
        on = o * r
        g = g_ref[...]
        sgm = _sigmoid(g)
        dob_v = _mm_nt(dyb_ref[...], wb_ref[...])
        dp_ref[3] = (dob_v * on * gn * (sgm * (1.0 + g * (1.0 - sgm)))).astype(BF16)
        do_n = dob_v * (g * sgm)
        acc_ref[1] += _rows8(do_n * on)
        dxn = do_n * gn
        do = (r * (dxn - on * _head_mean(dxn * on))).astype(BF16)
        s, f, a, a_mid, a_last = _hgrn_gates(fl_ref[...], lbv, row)
        k = 1.0 - f
        qs = q_ref[...] * QSCALE
        e_q = jnp.exp(a - a_mid)
        e_k = jnp.exp(a_mid - a)
        e_a = jnp.exp(a)
        e_l = jnp.exp(a_last - a)
        dec = jnp.exp(a_last)
        q_in = qs * e_q
        k_in = k * e_k
        q_a = qs * e_a
        k_d = k * e_l
        q_inb, k_inb, q_ab, k_db = (z.astype(BF16) for z in (q_in, k_in, q_a, k_d))
        vb = v_ref[...].astype(BF16)
        tri = (lax.broadcasted_iota(jnp.int32, (HCH, HCH), 0)
               >= lax.broadcasted_iota(jnp.int32, (HCH, HCH), 1))
        for c in reversed(range(nc)):
            sl = slice(HCH * c, HCH * (c + 1))
            for hh in range(HGRN_HB):
                hs = slice(HD * hh, HD * (hh + 1))
                stp = stb_ref[hh, c]
                dst = dst_s[hh]
                dstb = dst.astype(BF16)
                do_c = do[sl, hs]
                v_c = vb[sl, hs]
                dqa_s[sl, hs] = _mm(do_c, stp.astype(BF16))
                dkd_s[sl, hs] = _mm(v_c, dstb)
                ddec_s[sl, hs] = jnp.broadcast_to(jnp.sum(dst * stp, axis=0, keepdims=True), (HCH, HD))
                sc = jnp.where(tri, _mm_nt(q_inb[sl, hs], k_inb[sl, hs]), 0.0).astype(BF16)
                dsc = jnp.where(tri, _mm_nt(do_c, v_c), 0.0).astype(BF16)
                dv_s[sl, hs] = _mm_nt(k_db[sl, hs], dstb) + _mm_tn(sc, do_c)
                dqin_s[sl, hs] = _mm(dsc, k_inb[sl, hs])
                dkin_s[sl, hs] = _mm_tn(dsc, q_inb[sl, hs])
                d64 = dec[sl, hs]
                dst_s[hh] = dst * jnp.concatenate([d64, d64], axis=0) + _mm_tn(do_c, q_ab[sl, hs])
        dq_in = dqin_s[...]
        dq_a = dqa_s[...]
        dk_in = dkin_s[...]
        dk_d = dkd_s[...]
        dp_ref[0] = ((dq_in * e_q + dq_a * e_a) * QSCALE).astype(BF16)
        dp_ref[2] = dv_s[...].astype(BF16)
        tq = dq_in * q_in
        tk = dk_in * k_in
        td = dk_d * k_d
        d_a = tq + dq_a * q_a - tk - td
        d_a = d_a + jnp.where(row == HCH // 2 - 1, _seg_sum(tk - tq), 0.0)
        d_a = d_a + jnp.where(row == HCH - 1, _seg_sum(td) + ddec_s[...] * dec, 0.0)
        dlf = _revcumsum64(d_a, row)
        df = dlf / f - (dk_in * e_k + dk_d * e_l)
        dp_ref[1] = (df * (1.0 - lbv) * s * (1.0 - s)).astype(BF16)
        acc_ref[0] += _rows8(df * (1.0 - s))

    def col(off):
        return pl.BlockSpec((None, tb, HW), lambda h, cb: (off, nb - 1 - cb, h))

    hb = lambda h, cb: (nb - 1 - cb, h)
    return _call(
        body, name="hgrn_bwd", grid=(NH // HGRN_HB, nb), job=job,
        args=(dproj, dyb, w_b, o_raw, proj, proj, proj, proj, st_before, lb_table, norm_g),
        in_specs=[ANY, pl.BlockSpec((tb, D), lambda h, cb: (nb - 1 - cb, 0)),
                  pl.BlockSpec((HW, D), lambda h, cb: (h, 0)), pl.BlockSpec((tb, HW), hb),
                  col(2), col(3), col(4), col(5),
                  pl.BlockSpec((HGRN_HB, nc, HD, HD), lambda h, cb: (h, nb - 1 - cb, 0, 0)),
                  pl.BlockSpec((2, HW), lambda h, cb: (0, h)), pl.BlockSpec((1, HW), lambda h, cb: (0, h))],
        out_specs=[pl.BlockSpec((4, tb, HW), lambda h, cb: (0, nb - 1 - cb, h)),
                   pl.BlockSpec((2, 8, HW), lambda h, cb: (0, 0, h))],
        out_shape=[SDS(dproj.shape, BF16), SDS((2, 8, D), F32)],
        scratch_shapes=[pltpu.VMEM((HGRN_HB, HD, HD), F32)] + [pltpu.VMEM((tb, HW), F32)] * 6,
        aliases={0: 0})


def _gmlp_bwd(dproj, dya, w_a, proj, ln_g, ln_b, wm, wm_t, b_t):
    T = dya.shape[0]
    tm = min(256, T)

    def body(dp_in, dya_ref, wa_ref, u_ref, v_ref, lg_ref, lb_ref, wm_ref, wmt_ref, bt_ref,
             dp_ref, acc_ref, dws_ref, dmix_ref, du_s, dvn_s):
        del dp_in

        @pl.when(pl.program_id(0) == 0)
        def _():
            acc_ref[...] = jnp.zeros_like(acc_ref)
            dws_ref[...] = jnp.zeros_like(dws_ref)
            dmix_ref[...] = jnp.zeros_like(dmix_ref)

        u = u_ref[...]
        v = v_ref[...]
        lg = lg_ref[...]
        gu, t_u = _gelu(u)
        gv, t_v = _gelu(v)
        vhat, rs = _layer_norm_stats(gv)
        vnb = (vhat * lg + lb_ref[...]).astype(BF16)
        da_v = _mm_nt(dya_ref[...], wa_ref[...])
        for ch in range(tm // GCH):
            rows = slice(GCH * ch, GCH * (ch + 1))
            for g in range(NG):
                cols = slice(128 * g, 128 * (g + 1))
                vng = vnb[rows, cols]
                mixed = _mm(wm_ref[g], vng) + bt_ref[:, g:g + 1]
                dag = da_v[rows, cols]
                dmx = dag * gu[rows, cols]
                du_s[rows, cols] = dag * mixed
                dmxb = dmx.astype(BF16)
                dws_ref[:, cols] += _mm_nt(dmxb, vng)
                dmix_ref[:, cols] += dmx
                dvn_s[rows, cols] = _mm(wmt_ref[g], dmxb)
        dp_ref[0] = (du_s[...] * _gelu_grad(u, t_u)).astype(BF16)
        dvn = dvn_s[...]
        acc_ref[0] += _rows8(dvn * vhat)
        acc_ref[1] += _rows8(dvn)
        dvh = dvn * lg
        dgv = rs * (dvh - _mean(dvh) - vhat * _mean(dvh * vhat))
        dp_ref[1] = (dgv * _gelu_grad(v, t_v)).astype(BF16)

    row = lambda i: (0, 0)
    w3 = lambda i: (0, 0, 0)
    return pl.pallas_call(
        body, name="gmlp_bwd", grid=(T // tm,),
        in_specs=[ANY, pl.BlockSpec((tm, D), lambda i: (i, 0)), pl.BlockSpec((D, D), row),
                  pl.BlockSpec((None, tm, D), lambda i: (0, i, 0)), pl.BlockSpec((None, tm, D), lambda i: (1, i, 0)),
                  pl.BlockSpec((1, D), row), pl.BlockSpec((1, D), row),
                  pl.BlockSpec((NG, GCH, GCH), w3), pl.BlockSpec((NG, GCH, GCH), w3),
                  pl.BlockSpec((GCH, NG), row)],
        out_specs=[pl.BlockSpec((2, tm, D), lambda i: (2, i, 0)),
                   pl.BlockSpec((2, 8, D), w3), pl.BlockSpec((GCH, D), row), pl.BlockSpec((GCH, D), row)],
        out_shape=[SDS(dproj.shape, BF16), SDS((2, 8, D), F32), SDS((GCH, D), F32), SDS((GCH, D), F32)],
        scratch_shapes=[pltpu.VMEM((tm, D), F32), pltpu.VMEM((tm, D), F32)],
        input_output_aliases={0: 0},
        compiler_params=_cparams(),
    )(dproj, dya, w_a, proj, proj, ln_g, ln_b, wm, wm_t, b_t)


def _proj_bwd(dproj, w_in4, x, dx1, g_mix, job=None):
    T = x.shape[0]
    tm = min(256, T)
    order = (2, 3, 4, 5, 0, 1, 6, 7)

    def body(dp_ref, w_ref, x_ref, dx1_ref, g_ref, gx_ref, acc_ref):
        @pl.when(pl.program_id(0) == 0)
        def _():
            acc_ref[...] = jnp.zeros_like(acc_ref)

        dh = None
        for m, og in enumerate(order):
            part = _mm_nt(dp_ref[m], w_ref[og // 2, :, D * (og % 2):D * (og % 2 + 1)])
            dh = part if dh is None else dh + part
        xv = x_ref[...]
        r = lax.rsqrt(_mean(xv * xv) + EPS)
        xn = xv * r
        acc_ref[...] += _rows8(dh * xn)
        dxn = dh * g_ref[...]
        gx_ref[...] = dx1_ref[...] + r * (dxn - xn * _mean(dxn * xn))

    t = lambda i: (i, 0)
    return _call(
        body, name="proj_bwd", grid=(T // tm,),
        in_specs=[pl.BlockSpec((NIN, tm, D), lambda i: (0, i, 0)),
                  pl.BlockSpec((NCHIP, D, 2 * D), lambda i: (0, 0, 0), pipeline_mode=pl.Buffered(1)),
                  pl.BlockSpec((tm, D), t), pl.BlockSpec((tm, D), t), pl.BlockSpec((1, D), lambda i: (0, 0))],
        out_specs=[pl.BlockSpec((tm, D), t), pl.BlockSpec((8, D), lambda i: (0, 0))],
        out_shape=[SDS((T, D), F32), SDS((8, D), F32)],
        args=(dproj, w_in4, x, dx1, g_mix), job=job)


def _dw_call(name, a, b, a_spec, b_spec, o_spec, out_shape, nblk, tt, job=None, prefetch=None):
    T = a.shape[-2]

    def body(*refs):
        a_ref, b_ref, o_ref = refs[-3:]

        @pl.when(pl.program_id(1) == 0)
        def _():
            o_ref[...] = jnp.zeros_like(o_ref)
        o_ref[...] += _mm_tn(a_ref[...], b_ref[...])

    (out,), job_out = _call(
        body, name=name, grid=(nblk, T // tt), in_specs=[a_spec, b_spec], out_specs=[o_spec],
        out_shape=[out_shape], args=(a, b), job=job, prefetch=prefetch)
    return out, job_out


def _dw_in_half(name, place, hb, dproj, mine, job=None):
    tt = min(DW_TOKENS, hb.shape[0])

    def comp(k, pc):
        return _component_of(2 * k + (pc[1] if mine else 1 - pc[1]))

    return _dw_call(
        name, hb, dproj,
        pl.BlockSpec((tt, D), lambda k, t, pc: (t, 0)),
        pl.BlockSpec((None, tt, D), lambda k, t, pc: (comp(k, pc), t, 0)),
        pl.BlockSpec((None, D, D), lambda k, t, pc: (k, 0, 0)),
        SDS((NCHIP, D, D), F32), NCHIP, tt, job, place)


def _dw_gate_up(h2b, dgu4, job=None):
    tt = min(DW_TOKENS, h2b.shape[0])
    return _dw_call(
        "dw_gate_up", h2b, dgu4,
        pl.BlockSpec((tt, D), lambda k, t: (t, 0)),
        pl.BlockSpec((None, tt, FFS), lambda k, t: (k, t, 0)),
        pl.BlockSpec((None, D, FFS), lambda k, t: (k, 0, 0)),
        SDS((NCHIP, D, FFS), F32), NCHIP, tt, job)


def _dw_down(act, dx2b, job=None):
    tt = min(DW_TOKENS, act.shape[0])
    g, job_out = _dw_call(
        "dw_down", act, dx2b,
        pl.BlockSpec((tt, FFS), lambda k, t: (t, k)),
        pl.BlockSpec((tt, D), lambda k, t: (t, 0)),
        pl.BlockSpec((FFS, D), lambda k, t: (k, 0)),
        SDS((FF, D), F32), 2, tt, job)
    return g.reshape(NCHIP, FF // NCHIP, D), job_out


def _dw_square(name, a, b, job=None):
    tt = min(DW_TOKENS, a.shape[0])
    g, job_out = _dw_call(
        name, a, b,
        pl.BlockSpec((tt, D), lambda k, t: (t, 0)), pl.BlockSpec((tt, D), lambda k, t: (t, 0)),
        pl.BlockSpec((D, D), lambda k, t: (0, 0)), SDS((D, D), F32), 1, tt, job)
    return g.reshape(NCHIP, D // NCHIP, D), job_out


def _place():
    x, y, c = lax.axis_index("x"), lax.axis_index("y"), lax.axis_index("c")
    return x, y, c, 2 * x + y


def _chip_at(x, y, s):
    return x ^ (s >> 1), y ^ (s & 1)


class _Job:
    def __init__(self, ins, out_shapes, sems, start, finish, aliases=None, mid=None):
        self.ins, self.out_shapes, self.sems = list(ins), list(out_shapes), list(sems)
        self.start, self.finish, self.aliases = start, finish, dict(aliases or {})
        self.mid = mid if mid is not None else (lambda ins, outs, sems: None)


def _join_jobs(*jobs):
    def cut(refs, sizes):
        out, at = [], 0
        for n in sizes:
            out.append(refs[at:at + n])
            at += n
        return out

    ni = [len(j.ins) for j in jobs]
    no = [len(j.out_shapes) for j in jobs]
    ns = [len(j.sems) for j in jobs]

    def run(which):
        def go(ins, outs, sems):
            for j, a, b, c in zip(jobs, cut(ins, ni), cut(outs, no), cut(sems, ns)):
                getattr(j, which)(a, b, c)
        return go

    aliases = {}
    for k, j in enumerate(jobs):
        for a, b in j.aliases.items():
            aliases[sum(ni[:k]) + a] = sum(no[:k]) + b
    return _Job([a for j in jobs for a in j.ins], [o for j in jobs for o in j.out_shapes],
                [s for j in jobs for s in j.sems], run("start"), run("finish"), aliases, run("mid"))


def _call(body, *, name, grid, in_specs, out_specs, out_shape, args, scratch_shapes=(), aliases=None,
          job=None, prefetch=None):
    n_in, n_out, n_scr = len(in_specs), len(out_specs), len(scratch_shapes)
    npf = 0 if prefetch is None else 1
    job = job if job is not None else _Job([], [], [], lambda *a: None, lambda *a: None)
    ji, jo = len(job.ins), len(job.out_shapes)
    steps = math.prod(grid)

    def wrapped(*refs):
        pf, refs = refs[:npf], refs[npf:]
        ins, jin = refs[:n_in], refs[n_in:n_in + ji]
        o0 = n_in + ji
        outs, jout = refs[o0:o0 + n_out], refs[o0 + n_out:o0 + n_out + jo]
        s0 = o0 + n_out + jo
        scr, jsem = refs[s0:s0 + n_scr], refs[s0 + n_scr:]
        step = functools.reduce(lambda acc, ag: acc * ag[1] + pl.program_id(ag[0]), enumerate(grid), 0)
        if ji or jo:
            @pl.when(step == 0)
            def _():
                job.start(jin, jout, jsem)

        body(*pf, *ins, *outs, *scr)

        if ji or jo:
            @pl.when(step == steps // 2)
            def _():
                job.mid(jin, jout, jsem)

            @pl.when(step == steps - 1)
            def _():
                job.finish(jin, jout, jsem)

    io = {npf + a: b for a, b in dict(aliases or {}).items()}
    io.update({npf + n_in + a: n_out + b for a, b in job.aliases.items()})
    kw = dict(in_specs=list(in_specs) + [ANY] * ji, out_specs=list(out_specs) + [ANY] * jo,
              scratch_shapes=list(scratch_shapes) + job.sems)
    if npf:
        kw = dict(grid_spec=pltpu.PrefetchScalarGridSpec(num_scalar_prefetch=1, grid=grid, **kw))
    else:
        kw["grid"] = grid
    res = pl.pallas_call(
        wrapped, name=name, out_shape=list(out_shape) + job.out_shapes, input_output_aliases=io,
        compiler_params=_cparams(has_side_effects=bool(ji or jo)), **kw,
    )(*(() if prefetch is None else (prefetch,)), *args, *job.ins)
    return list(res[:n_out]), list(res[n_out:])


def _cast_shards(name, place, ws):
    n = len(ws)
    rows, cols = ws[0].shape
    tr = 352 if rows % 352 == 0 else 256

    def body(pc_ref, *refs):
        del pc_ref
        for w_ref, o_ref in zip(refs[:n], refs[n:]):
            o_ref[...] = w_ref[...].astype(BF16)

    return pl.pallas_call(
        body, name=name,
        grid_spec=pltpu.PrefetchScalarGridSpec(
            num_scalar_prefetch=1, grid=(rows // tr,),
            in_specs=[pl.BlockSpec((tr, cols), lambda i, pc: (i, 0))] * n,
            out_specs=[pl.BlockSpec((None, tr, cols), lambda i, pc: (pc[0], i, 0))] * n),
        out_shape=[SDS((NCHIP, rows, cols), BF16)] * n,
        compiler_params=_cparams(),
    )(place, *ws)


def _sibling_copy(ref, send_sem, recv_sem):
    x, y, c, _ = _place()
    return pltpu.make_async_remote_copy(src_ref=ref, dst_ref=ref, send_sem=send_sem, recv_sem=recv_sem,
                                        device_id=(x, y, 1 - c), device_id_type=MESH)


def _half_rows(arr, slot, core):
    half = arr.shape[1] // 2
    return arr.at[slot, pl.ds(pl.multiple_of(core * half, 16), half)]


def _quarter_rows(arr, slot, core, q):
    quarter = arr.shape[1] // 4
    return arr.at[slot, pl.ds(pl.multiple_of((2 * core + q) * quarter, 16), quarter)]


def _chip_copy(ref, dist, send_sem, recv_sem):
    x, y, c, _ = _place()
    cx, cy = _chip_at(x, y, dist)
    return pltpu.make_async_remote_copy(src_ref=ref, dst_ref=ref, send_sem=send_sem, recv_sem=recv_sem,
                                        device_id=(cx, cy, c), device_id_type=MESH)


def _gather_sems(n):
    dma = pltpu.SemaphoreType.DMA
    return [dma((n, 2))] * 4 + [dma((n, 4))] * 2


def _gather_start(arrs, sems):
    dsend, drecv = sems[0], sems[1]
    _, _, c, j = _place()
    for w, arr in enumerate(arrs):
        for dist in (1, 2):
            _chip_copy(_half_rows(arr, j, c), dist, dsend.at[w, dist - 1], drecv.at[w, dist - 1]).start()


def _gather_land(arrs, sems, dist, first=0):
    dsend, drecv, rsend, rrecv, fsend, frecv = sems
    _, _, c, j = _place()
    if dist < 3:
        other = 3 - dist
        for w, arr in enumerate(arrs, first):
            landed = _half_rows(arr, j ^ dist, c)
            _chip_copy(landed, dist, dsend.at[w, dist - 1], drecv.at[w, dist - 1]).wait_recv()
            relay = _quarter_rows(arr, j ^ dist, c, other - 1)
            _chip_copy(relay, other, rsend.at[w, other - 1], rrecv.at[w, other - 1]).start()
            _sibling_copy(landed, fsend.at[w, dist - 1], frecv.at[w, dist - 1]).start()
        for w, arr in enumerate(arrs, first):
            theirs = _half_rows(arr, j ^ dist, 1 - c)
            _sibling_copy(theirs, fsend.at[w, dist - 1], frecv.at[w, dist - 1]).wait_recv()
    else:
        for w, arr in enumerate(arrs, first):
            for via in (1, 2):
                piece = _quarter_rows(arr, j ^ 3, c, via - 1)
                _chip_copy(piece, via, rsend.at[w, via - 1], rrecv.at[w, via - 1]).wait_recv()
                _sibling_copy(piece, fsend.at[w, 1 + via], frecv.at[w, 1 + via]).start()
        for w, arr in enumerate(arrs, first):
            for via in (1, 2):
                theirs = _quarter_rows(arr, j ^ 3, 1 - c, via - 1)
                _sibling_copy(theirs, fsend.at[w, 1 + via], frecv.at[w, 1 + via]).wait_recv()


def _gather_drain(arrs, sems):
    dsend, drecv, rsend, rrecv, fsend, frecv = sems
    _, _, c, j = _place()
    for w, arr in enumerate(arrs):
        for dist in (1, 2):
            other = 3 - dist
            _chip_copy(_half_rows(arr, j, c), dist, dsend.at[w, dist - 1], drecv.at[w, dist - 1]).wait_send()
            _chip_copy(_quarter_rows(arr, j ^ dist, c, other - 1), other,
                       rsend.at[w, other - 1], rrecv.at[w, other - 1]).wait_send()
            _sibling_copy(_half_rows(arr, j ^ dist, c), fsend.at[w, dist - 1], frecv.at[w, dist - 1]).wait_send()
            _sibling_copy(_quarter_rows(arr, j ^ 3, c, dist - 1),
                          fsend.at[w, 1 + dist], frecv.at[w, 1 + dist]).wait_send()


def _gather_neighbours(arrs, sems):
    _gather_land(arrs, sems, 1)
    _gather_land(arrs, sems, 2)


def _gather_finish(arrs, sems):
    _gather_land(arrs, sems, 3)
    _gather_drain(arrs, sems)


def _gather_job(arrs):
    n = len(arrs)
    return _Job(arrs, [SDS(a.shape, a.dtype) for a in arrs], _gather_sems(n),
                lambda ins, outs, sems: _gather_start(outs, sems),
                lambda ins, outs, sems: _gather_finish(outs, sems), {k: k for k in range(n)},
                mid=lambda ins, outs, sems: _gather_neighbours(outs, sems))


def _exchange_job(arrs, out_shapes, n, copies):
    def start(ins, outs, sems):
        for cp in copies(ins, outs, sems[0], sems[1]):
            cp.start()

    def finish(ins, outs, sems):
        for cp in copies(ins, outs, sems[0], sems[1]):
            cp.wait()

    return _Job(arrs, out_shapes, [pltpu.SemaphoreType.DMA((n,))] * 2, start, finish)


def _pair_exchange_job(grads):
    def copies(ins, outs, send_sem, recv_sem):
        x, y, c, _ = _place()
        res = []
        for w in range(len(grads)):
            half = ins[w].shape[1] // 2
            theirs = pl.ds(pl.multiple_of((1 - c) * half, 8), half)
            res.append(pltpu.make_async_remote_copy(
                src_ref=ins[w].at[:, theirs, :], dst_ref=outs[w], send_sem=send_sem.at[w],
                recv_sem=recv_sem.at[w], device_id=(x, y, 1 - c), device_id_type=MESH))
        return res

    return _exchange_job(grads, [SDS((NCHIP, g.shape[1] // 2, g.shape[2]), F32) for g in grads],
                         len(grads), copies)


def _row_tile(rows, cols):
    tr = rows
    while tr * cols * 4 > ELEMENTWISE_BLOCK_BYTES and tr % 32 == 0:
        tr //= 2
    return tr


def _pair_sums(name, place, gs, sibs):
    n = len(gs)
    half, cols = sibs[0].shape[1], sibs[0].shape[2]
    tr = _row_tile(half, cols)
    nt = half // tr
    mine = nt if gs[0].shape[1] == 2 * half else 0

    def body(pc_ref, *refs):
        del pc_ref
        for g_ref, s_ref, own_ref, out_ref in zip(refs[:n], refs[n:2 * n], refs[2 * n:3 * n], refs[3 * n:]):
            v = g_ref[...] + s_ref[...]
            out_ref[...] = v.astype(BF16)

            @pl.when(pl.program_id(1) == 0)
            def _():
                own_ref[...] = v

    res = pl.pallas_call(
        body, name=name,
        grid_spec=pltpu.PrefetchScalarGridSpec(
            num_scalar_prefetch=1, grid=(nt, NCHIP),
            in_specs=[pl.BlockSpec((None, tr, cols), lambda i, s, pc: (pc[0] ^ s, pc[1] * mine + i, 0))] * n
            + [pl.BlockSpec((None, tr, cols), lambda i, s, pc: (pc[0] ^ s, i, 0))] * n,
            out_specs=[pl.BlockSpec((tr, cols), lambda i, s, pc: (i, 0))] * n
            + [pl.BlockSpec((None, tr, cols), lambda i, s, pc: (s, i, 0))] * n),
        out_shape=[SDS((half, cols), F32)] * n + [SDS((NCHIP, half, cols), BF16)] * n,
        compiler_params=_cparams(),
    )(place, *gs, *sibs)
    return res[:n], res[n:]


def _chip_exchange_job(parts):
    def copies(ins, outs, send_sem, recv_sem):
        x, y, c, _ = _place()
        res = []
        for w in range(len(parts)):
            for s in range(1, NCHIP):
                cx, cy = _chip_at(x, y, s)
                k = w * (NCHIP - 1) + s - 1
                res.append(pltpu.make_async_remote_copy(
                    src_ref=ins[w].at[s], dst_ref=outs[w].at[s - 1], send_sem=send_sem.at[k],
                    recv_sem=recv_sem.at[k], device_id=(cx, cy, c), device_id_type=MESH))
        return res

    return _exchange_job(parts, [SDS((NCHIP - 1,) + p.shape[1:], BF16) for p in parts],
                         len(parts) * (NCHIP - 1), copies)


def _chip_sums(name, owns, rems):
    n = len(owns)
    half, cols = owns[0].shape
    tr = _row_tile(half, cols)

    def body(*refs):
        for own_ref, rem_ref, out_ref in zip(refs[:n], refs[n:2 * n], refs[2 * n:]):
            out_ref[...] = (((own_ref[...] + rem_ref[0].astype(F32)) + rem_ref[1].astype(F32))
                            + rem_ref[2].astype(F32))

    return pl.pallas_call(
        body, name=name, grid=(half // tr,),
        in_specs=[pl.BlockSpec((tr, cols), lambda i: (i, 0))] * n
        + [pl.BlockSpec((NCHIP - 1, tr, cols), lambda i: (0, i, 0))] * n,
        out_specs=[pl.BlockSpec((tr, cols), lambda i: (i, 0))] * n,
        out_shape=[SDS((half, cols), F32)] * n,
        compiler_params=_cparams(),
    )(*owns, *rems)


def _share_halves_job(halves):
    def copies(ins, outs, send_sem, recv_sem):
        x, y, c, _ = _place()
        return [pltpu.make_async_remote_copy(
            src_ref=ins[w], dst_ref=outs[w], send_sem=send_sem.at[w], recv_sem=recv_sem.at[w],
            device_id=(x, y, 1 - c), device_id_type=MESH) for w in range(len(halves))]

    return _exchange_job(halves, [SDS(h.shape, F32) for h in halves], len(halves), copies)


def _adamw_math(w, g, m, v):
    m = B1 * m + (1.0 - B1) * g
    v = B2 * v + (1.0 - B2) * (g * g)
    m_hat = m / (1.0 - B1 ** STEP)
    v_hat = v / (1.0 - B2 ** STEP)
    delta = -LR * (m_hat / (jnp.sqrt(v_hat) + AEPS) + WD * w)
    return delta, m, v


def _adamws(name, place, ws, owns, sibs, ms, vs):
    n = len(ws)
    rows, cols = ws[0].shape
    by_cols = owns[0].shape[0] == rows
    half, pc_cols = (rows, cols // 2) if by_cols else (rows // 2, cols)
    tr = _row_tile(half, pc_cols)
    nt = half // tr

    def body(pc_ref, *refs):
        ins, outs = refs[:5 * n], refs[5 * n:]
        for k in range(n):
            w_ref, own_ref, sib_ref, m_ref, v_ref = ins[5 * k:5 * k + 5]
            g = jnp.where(pl.program_id(0) == pc_ref[1], own_ref[...], sib_ref[...])
            d, mn, vn = _adamw_math(w_ref[...], g, m_ref[...], v_ref[...])
            for ref, val in zip(outs[4 * k:4 * k + 4], (g, d, mn, vn)):
                ref[...] = val

    full = pl.BlockSpec((tr, pc_cols), (lambda h, i, pc: (i, h)) if by_cols else (lambda h, i, pc: (h * nt + i, 0)))
    part = pl.BlockSpec((tr, pc_cols), lambda h, i, pc: (i, 0))
    res = pl.pallas_call(
        body, name=name,
        grid_spec=pltpu.PrefetchScalarGridSpec(
            num_scalar_prefetch=1, grid=(2, nt),
            in_specs=[full, part, part, full, full] * n, out_specs=[full] * (4 * n)),
        out_shape=[SDS((rows, cols), F32)] * (4 * n),
        compiler_params=_cparams(),
    )(place, *[a for group in zip(ws, owns, sibs, ms, vs) for a in group])
    return [tuple(res[4 * k:4 * k + 4]) for k in range(n)]


def _small_allreduce_adamw(sp, w, m, v, job):
    shape = sp.shape
    ji, jo = len(job.ins), len(job.out_shapes)

    def body(sp_ref, w_ref, m_ref, v_ref, *rest):
        jin, (g_ref, d_ref, mo_ref, vo_ref), jout = rest[:ji], rest[ji:ji + 4], rest[ji + 4:ji + 4 + jo]
        sib_s, pair_s, chip_s, send_sem, recv_sem = rest[ji + 4 + jo:ji + 9 + jo]
        jsem = rest[ji + 9 + jo:]
        job.start(jin, jout, jsem)
        x, y, c, j = _place()
        cp = pltpu.make_async_remote_copy(
            src_ref=sp_ref, dst_ref=sib_s, send_sem=send_sem.at[0], recv_sem=recv_sem.at[0],
            device_id=(x, y, 1 - c), device_id_type=MESH)
        cp.start()
        cp.wait()
        pair_s[...] = sp_ref[...] + sib_s[...]
        half = shape[0] // 2
        mine = pl.ds(pl.multiple_of(c * half, 8), half)
        cps = []
        for s in range(1, NCHIP):
            cx, cy = _chip_at(x, y, s)
            cp = pltpu.make_async_remote_copy(
                src_ref=pair_s.at[mine], dst_ref=chip_s.at[s, mine], send_sem=send_sem.at[s],
                recv_sem=recv_sem.at[s], device_id=(cx, cy, c), device_id_type=MESH)
            cp.start()
            cps.append(cp)
        chip_s[0] = pair_s[...]
        for cp in cps:
            cp.wait()
        cps = []
        for s in range(1, NCHIP):
            cp = pltpu.make_async_remote_copy(
                src_ref=chip_s.at[s, mine], dst_ref=chip_s.at[s, mine], send_sem=send_sem.at[NCHIP + s],
                recv_sem=recv_sem.at[NCHIP + s], device_id=(x, y, 1 - c), device_id_type=MESH)
            cp.start()
            cps.append(cp)
        for cp in cps:
            cp.wait()
        tot = chip_s[j]
        for k in range(1, NCHIP):
            tot = tot + chip_s[k ^ j]
        g_ref[...] = tot
        d, mn, vn = _adamw_math(w_ref[...], tot, m_ref[...], v_ref[...])
        d_ref[...] = d
        mo_ref[...] = mn
        vo_ref[...] = vn
        job.mid(jin, jout, jsem)
        job.finish(jin, jout, jsem)

    vm = pl.BlockSpec(memory_space=pltpu.VMEM)
    res = pl.pallas_call(
        body, name="small_allreduce_adamw",
        in_specs=[vm] * 4 + [ANY] * ji, out_specs=[vm] * 4 + [ANY] * jo,
        out_shape=[SDS(shape, F32)] * 4 + job.out_shapes,
        scratch_shapes=[pltpu.VMEM(shape, F32), pltpu.VMEM(shape, F32), pltpu.VMEM((NCHIP,) + shape, F32),
                        pltpu.SemaphoreType.DMA((2 * NCHIP,)), pltpu.SemaphoreType.DMA((2 * NCHIP,))] + job.sems,
        input_output_aliases={4 + a: 4 + b for a, b in job.aliases.items()},
        compiler_params=pltpu.CompilerParams(has_side_effects=True),
    )(sp, w, m, v, *job.ins)
    return res[:4], res[4:]


def _pack_small(first, mix, ln_g, ln_b, b_s, lbt, hn, ffn, fin, w_s):
    rows = [first.reshape(1, D), mix.reshape(1, D), ln_g.reshape(1, D), ln_b.reshape(1, D),
            b_s.reshape(1, D), lbt.reshape(2, D), hn.reshape(1, D), ffn.reshape(1, D), fin.reshape(1, D),
            jnp.zeros((6, D), F32)]
    return jnp.concatenate(rows + [w_s.reshape(NG, GCH, GCH).transpose(1, 0, 2).reshape(GCH, D)], axis=0)


def _unpack_small(p):
    w_s = p[16:].reshape(GCH, NG, GCH).transpose(1, 0, 2).reshape(1, NG, GCH, GCH)
    return dict(norm_mix_g=p[1:2], gmlp_ln_g=p[2:3], gmlp_ln_b=p[3:4], gmlp_b_s=p[4].reshape(1, NG, GCH),
                hgrn_lb_table=p[5:7], hgrn_norm_g=p[7:8], norm_ffn_g=p[8:9], norm_final_g=p[9],
                gmlp_w_s=w_s)


SMALL = ("norm_mix_g", "gmlp_ln_g", "gmlp_ln_b", "gmlp_w_s", "gmlp_b_s", "hgrn_lb_table", "hgrn_norm_g",
         "norm_ffn_g", "norm_final_g")
BIG = ("w_in", "w_gate_up", "w_branch_a", "w_branch_b", "w_out", "w_down")
ORDER = ("norm_mix_g", "w_in", "gmlp_ln_g", "gmlp_ln_b", "gmlp_w_s", "gmlp_b_s", "hgrn_lb_table",
         "hgrn_norm_g", "w_branch_a", "w_branch_b", "w_out", "norm_ffn_g", "w_gate_up", "w_down",
         "norm_final_g")


def kernel(x, norm_mix_g, w_in, gmlp_ln_g, gmlp_ln_b, gmlp_w_s, gmlp_b_s, hgrn_lb_table, hgrn_norm_g, w_branch_a, w_branch_b, w_out, norm_ffn_g, w_gate_up, w_down, norm_final_g, loss_target, m_norm_mix_g, m_w_in, m_gmlp_ln_g, m_gmlp_ln_b, m_gmlp_w_s, m_gmlp_b_s, m_hgrn_lb_table, m_hgrn_norm_g, m_w_branch_a, m_w_branch_b, m_w_out, m_norm_ffn_g, m_w_gate_up, m_w_down, m_norm_final_g, v_norm_mix_g, v_w_in, v_gmlp_ln_g, v_gmlp_ln_b, v_gmlp_w_s, v_gmlp_b_s, v_hgrn_lb_table, v_hgrn_norm_g, v_w_branch_a, v_w_branch_b, v_w_out, v_norm_ffn_g, v_w_gate_up, v_w_down, v_norm_final_g):
    args = dict(locals())
    T = x.shape[1]
    xs = x.reshape(T, D)
    target = loss_target.reshape(T, D)
    big = {n: args[n].reshape(args[n].shape[1:]) for n in BIG}
    big_m = {n: args["m_" + n].reshape(args[n].shape[1:]) for n in BIG}
    big_v = {n: args["v_" + n].reshape(args[n].shape[1:]) for n in BIG}

    x_i, y_i, c_i = lax.axis_index("x"), lax.axis_index("y"), lax.axis_index("c")
    place = jnp.stack([2 * x_i + y_i, c_i]).astype(jnp.int32)
    def by_shape(names):
        groups = []
        for n in names:
            if groups and big[groups[-1][0]].shape == big[n].shape:
                groups[-1].append(n)
            else:
                groups.append([n])
        return groups

    cast = {}
    for grp in by_shape(BIG):
        cast.update(zip(grp, _cast_shards("cast_" + grp[0], place, [big[n] for n in grp])))
    tril = jnp.tril(jnp.ones((GCH, GCH), bool))
    wm = jnp.where(tril, gmlp_w_s[0], 0.0).astype(BF16)
    wm_t = jnp.swapaxes(wm, 1, 2)
    b_t = gmlp_b_s[0].T

    (proj, hb), w_in4, (w_a4, w_b4, w_out4, w_down4) = _proj_fwd(
        place, xs, norm_mix_g, cast["w_in"], [cast[n] for n in ("w_branch_a", "w_branch_b", "w_out", "w_down")])
    (ab,), _ = _gmlp_fwd(proj, gmlp_ln_g, gmlp_ln_b, wm, b_t)
    (o_raw, obb, st_before), (w_gu4,) = _hgrn_fwd(
        proj, hgrn_lb_table, hgrn_norm_g, job=_gather_job([cast["w_gate_up"]]))
    w_a, w_b, w_o = (w.reshape(D, D) for w in (w_a4, w_b4, w_out4))
    (mgb, x1), _ = _merge_fwd(xs, ab, obb, proj, w_a, w_b, w_o)
    w_dn = w_down4.reshape(FF, D)
    act, dx2b, h2b, dgu4, dx1, dx1b, acc_ffn = _ffn_fwd_bwd(
        x1, target, norm_ffn_g, norm_final_g.reshape(1, D), w_gu4, w_dn)

    grads, owns, parts, halves, sibh = {}, {}, {}, {}, {}

    def pair_sums(names, sibs):
        sib_of = dict(zip(names, sibs))
        for grp in by_shape(names):
            o, p = _pair_sums("rs_pair_sum_" + grp[0], place, [grads[n] for n in grp], [sib_of[n] for n in grp])
            owns.update(zip(grp, o))
            parts.update(zip(grp, p))

    def chip_sums(names, got):
        rem_of = dict(zip(names, got))
        for grp in by_shape(names):
            h = _chip_sums("rs_chip_sum_" + grp[0], [owns[n] for n in grp], [rem_of[n] for n in grp])
            halves.update(zip(grp, h))

    ffn, mix = ("w_gate_up", "w_down"), ("w_branch_a", "w_branch_b", "w_out")
    grads["w_gate_up"], _ = _dw_gate_up(h2b, dgu4)
    grads["w_down"], _ = _dw_down(act, dx2b)
    (dya, dyb, dproj), got = _merge_bwd(
        dx1b, ab, obb, proj, w_o, w_a, w_b, job=_pair_exchange_job([grads[n] for n in ffn]))
    pair_sums(ffn, got)
    grads["w_branch_a"], _ = _dw_square("dw_branch_a", ab, dya)
    grads["w_branch_b"], _ = _dw_square("dw_branch_b", obb, dyb)
    grads["w_out"], _ = _dw_square("dw_out", mgb, dx1b)
    (dproj, acc_hgrn), got = _hgrn_bwd(
        dproj, dyb, w_b, o_raw, proj, st_before, hgrn_lb_table, hgrn_norm_g,
        job=_join_jobs(_chip_exchange_job([parts[n] for n in ffn]), _pair_exchange_job([grads[n] for n in mix])))
    chip_sums(ffn, got[:2])
    pair_sums(mix, got[2:])
    dproj, acc_ln, dws, dmix = _gmlp_bwd(dproj, dya, w_a, proj, gmlp_ln_g, gmlp_ln_b, wm, wm_t, b_t)
    for_sibling, got = _dw_in_half(
        "dw_in_sibling_half", place, hb, dproj, False,
        job=_join_jobs(_share_halves_job([halves[n] for n in ffn]), _chip_exchange_job([parts[n] for n in mix])))
    sibh.update(zip(ffn, got[:2]))
    chip_sums(mix, got[2:])
    grads["w_in"], got = _dw_in_half(
        "dw_in_own_half", place, hb, dproj, True, job=_share_halves_job([for_sibling]))
    pair_sums(("w_in",), got)
    (grad_x, acc_mix), got = _proj_bwd(
        dproj, w_in4, xs, dx1, norm_mix_g,
        job=_join_jobs(_chip_exchange_job([parts["w_in"]]), _share_halves_job([halves[n] for n in mix])))
    chip_sums(("w_in",), got[:1])
    sibh.update(zip(mix, got[1:]))

    lbv = jax.nn.sigmoid(hgrn_lb_table[0] - hgrn_lb_table[1])
    d_t0 = jnp.sum(acc_hgrn[0], axis=0) * lbv * (1.0 - lbv)
    loss_row = jnp.zeros((D,), F32).at[0].set(jnp.sum(acc_ffn[0]))
    dws_m = jnp.where(tril[:, None, :], dws.reshape(GCH, NG, GCH), 0.0).transpose(1, 0, 2)
    db_s = jnp.sum(dmix.reshape(GCH, NG, GCH), axis=-1).T
    sp = _pack_small(loss_row, jnp.sum(acc_mix, 0), jnp.sum(acc_ln[0], 0), jnp.sum(acc_ln[1], 0), db_s,
                     jnp.stack([d_t0, -d_t0]), jnp.sum(acc_hgrn[1], 0), jnp.sum(acc_ffn[2], 0),
                     jnp.sum(acc_ffn[1], 0), dws_m)
    zero = jnp.zeros((D,), F32)

    def pack(prefix):
        a = lambda n: args[prefix + n]
        return _pack_small(zero, a("norm_mix_g"), a("gmlp_ln_g"), a("gmlp_ln_b"), a("gmlp_b_s"),
                           a("hgrn_lb_table"), a("hgrn_norm_g"), a("norm_ffn_g"), a("norm_final_g"),
                           a("gmlp_w_s"))

    packed, (sibh["w_in"],) = _small_allreduce_adamw(
        sp, pack(""), pack("m_"), pack("v_"), _share_halves_job([halves["w_in"]]))
    loss = packed[0][0, 0]
    small = [_unpack_small(p) for p in packed]
    out = {n: tuple(s[n] for s in small) for n in SMALL}
    for grp in by_shape(BIG):
        res = _adamws("adamw_" + grp[0], place, *[[d[n] for n in grp] for d in (big, halves, sibh, big_m, big_v)])
        for n, quad in zip(grp, res):
            out[n] = tuple(a.reshape(args[n].shape) for a in quad)
    return (loss, grad_x.reshape(x.shape), *[out[n][0] for n in ORDER], *[out[n][1] for n in ORDER],
            *[out[n][2] for n in ORDER], *[out[n][3] for n in ORDER])
```

```python
import functools
import math

import jax
import jax.numpy as jnp
from jax import lax
from jax.experimental import pallas as pl
from jax.experimental.pallas import tpu as pltpu

F32 = jnp.float32
BF16 = jnp.bfloat16
SDS = jax.ShapeDtypeStruct
MESH = pl.DeviceIdType.MESH
ANY = pl.BlockSpec(memory_space=pl.ANY)

D = 1024
NIN = 8
NG = 8
GCH = 128
NH = 8
HD = 128
HCH = 64
HGRN_HB = 8
HGRN_TOKENS = 256
HW = HGRN_HB * HD
DW_TOKENS = 2048
ELEMENTWISE_BLOCK_BYTES = 2 * 1024 * 1024
PROJ_OUT_SLOTS = 4
FF = 2816
FFS = 1408
NCHIP = 4
EPS = 1e-6
QSCALE = HD ** -0.5
GELU_C0 = math.sqrt(2.0 / math.pi)
GELU_C1 = 0.044715
LR, B1, B2, AEPS, WD, STEP = 0.001, 0.9, 0.999, 1e-08, 0.01, 10
VMEM_LIMIT_V7X = 56 * 1024 * 1024
SP_ROWS = 144


def _cparams(**kw):
    return pltpu.CompilerParams(vmem_limit_bytes=VMEM_LIMIT_V7X, **kw)


def _mm(a, b):
    return jnp.dot(a, b, preferred_element_type=F32)


def _mm_nt(a, b):
    return lax.dot_general(a, b, (((1,), (1,)), ((), ())), preferred_element_type=F32)


def _mm_tn(a, b):
    return lax.dot_general(a, b, (((0,), (0,)), ((), ())), preferred_element_type=F32)


def _rows8(x):
    r, c = x.shape
    return jnp.sum(x.reshape(r // 8, 8, c), axis=0)


def _mean(x):
    return jnp.mean(x, axis=-1, keepdims=True)


def _sigmoid(x):
    return 1.0 / (1.0 + jnp.exp(-x))


def _gelu(x):
    t = jnp.tanh(GELU_C0 * (x + GELU_C1 * x * x * x))
    return 0.5 * x * (1.0 + t), t


def _gelu_grad(x, t):
    return 0.5 * (1.0 + t) + 0.5 * x * (1.0 - t * t) * (GELU_C0 * (1.0 + 3.0 * GELU_C1 * x * x))


def _component_of(group):
    return jnp.where(group < 6, (group + 4) % 6, group)


def _proj_fwd(place, x, g_mix, w_in4, later):
    T = x.shape[0]
    tm = min(1024, T)
    ni = T // tm
    n = len(later)

    def body(pc_ref, x_ref, g_ref, *rest):
        proj_ref, h_ref, w_all = rest[1 + n:4 + n]
        gathered = rest[4 + n:4 + 2 * n]
        hs, wbuf, wsem, obuf, osem = rest[4 + 2 * n:9 + 2 * n]
        w_sems, later_sems = rest[9 + 2 * n:15 + 2 * n], rest[15 + 2 * n:]
        jp, i = pl.program_id(0), pl.program_id(1)
        w_cols = [w_all.at[:, :, pl.ds(k * D, D)] for k in range(2)]

        def w_copy(blk):
            cols = pl.ds(pl.multiple_of((blk % 2) * D, 128), D)
            return pltpu.make_async_copy(w_all.at[pc_ref[0] ^ (blk // 2), :, cols], wbuf.at[blk % 2],
                                         wsem.at[blk % 2])

        @pl.when((jp == 0) & (i == 0))
        def _():
            _gather_start(w_cols, w_sems)
            w_copy(jp).start()

        @pl.when(i == 0)
        def _():
            w_copy(jp).wait()

        @pl.when(jp == 0)
        def _():
            xv = x_ref[...]
            r = lax.rsqrt(_mean(xv * xv) + EPS)
            hb = (xv * r * g_ref[...]).astype(BF16)
            hs[i] = hb
            h_ref[...] = hb

        step = jp * ni + i
        slot = step % PROJ_OUT_SLOTS

        def o_copy(slot_):
            comp = 2 * (pc_ref[0] ^ (jp // 2)) + jp % 2
            return pltpu.make_async_copy(
                obuf.at[slot_], proj_ref.at[comp, pl.ds(pl.multiple_of(i * tm, 8), tm)], osem.at[slot_])

        @pl.when(step >= PROJ_OUT_SLOTS)
        def _():
            o_copy(slot).wait()

        obuf[slot] = _mm(hs[i], wbuf[jp % 2])
        o_copy(slot).start()

        @pl.when(step == NIN * ni - 1)
        def _():
            for k in range(PROJ_OUT_SLOTS):
                o_copy((slot + 1 + k) % PROJ_OUT_SLOTS).wait()

        for nxt in range(1, NIN):
            @pl.when((jp == nxt - 1) & (i == ni - 1))
            def _():
                if nxt >= 2:
                    _gather_land([w_cols[nxt % 2]], w_sems, nxt // 2, first=nxt % 2)
                if nxt == 5:
                    _gather_start(gathered, later_sems)
                if nxt == NIN - 1:
                    _gather_neighbours(gathered, later_sems)
                w_copy(jp + 1).start()

        @pl.when((jp == NIN - 1) & (i == ni - 1))
        def _():
            _gather_drain(w_cols, w_sems)
            _gather_finish(gathered, later_sems)

    tile = lambda jp, i, pc: (jnp.where(jp == 0, i, ni - 1), 0)
    res = pl.pallas_call(
        body, name="proj_fwd",
        grid_spec=pltpu.PrefetchScalarGridSpec(
            num_scalar_prefetch=1, grid=(NIN, ni),
            in_specs=[pl.BlockSpec((tm, D), tile), pl.BlockSpec((1, D), lambda jp, i, pc: (0, 0))] + [ANY] * (1 + n),
            out_specs=[ANY, pl.BlockSpec((tm, D), tile)] + [ANY] * (1 + n),
            scratch_shapes=[pltpu.VMEM((ni, tm, D), BF16), pltpu.VMEM((2, D, D), BF16),
                            pltpu.SemaphoreType.DMA((2,)), pltpu.VMEM((PROJ_OUT_SLOTS, tm, D), F32),
                            pltpu.SemaphoreType.DMA((PROJ_OUT_SLOTS,))] + _gather_sems(2) + _gather_sems(n)),
        out_shape=[SDS((NIN, T, D), F32), SDS((T, D), BF16), SDS(w_in4.shape, BF16)]
        + [SDS(a.shape, a.dtype) for a in later],
        input_output_aliases={3 + k: 2 + k for k in range(1 + n)},
        compiler_params=_cparams(has_side_effects=True),
    )(place, x, g_mix, w_in4, *later)
    return res[:2], res[2], res[3:]


def _layer_norm_stats(gv):
    mu = _mean(gv)
    xc = gv - mu
    rs = lax.rsqrt(_mean(xc * xc) + EPS)
    return xc * rs, rs


def _gmlp_fwd(proj, ln_g, ln_b, wm, b_t, job=None):
    T = proj.shape[1]
    tm = min(256, T)

    def body(u_ref, v_ref, lg_ref, lb_ref, wm_ref, bt_ref, a_ref, a_s):
        gu, _ = _gelu(u_ref[...])
        gv, _ = _gelu(v_ref[...])
        vhat, _ = _layer_norm_stats(gv)
        vnb = (vhat * lg_ref[...] + lb_ref[...]).astype(BF16)
        for ch in range(tm // GCH):
            rows = slice(GCH * ch, GCH * (ch + 1))
            for g in range(NG):
                cols = slice(128 * g, 128 * (g + 1))
                mixed = _mm(wm_ref[g], vnb[rows, cols]) + bt_ref[:, g:g + 1]
                a_s[rows, cols] = gu[rows, cols] * mixed
        a_ref[...] = a_s[...].astype(BF16)

    row = lambda i: (0, 0)
    return _call(
        body, name="gmlp_fwd", grid=(T // tm,), job=job, args=(proj, proj, ln_g, ln_b, wm, b_t),
        in_specs=[pl.BlockSpec((None, tm, D), lambda i: (0, i, 0)), pl.BlockSpec((None, tm, D), lambda i: (1, i, 0)),
                  pl.BlockSpec((1, D), row), pl.BlockSpec((1, D), row),
                  pl.BlockSpec((NG, GCH, GCH), lambda i: (0, 0, 0)), pl.BlockSpec((GCH, NG), row)],
        out_specs=[pl.BlockSpec((tm, D), lambda i: (i, 0))],
        out_shape=[SDS((T, D), BF16)],
        scratch_shapes=[pltpu.VMEM((tm, D), F32)])


def _cumsum64(x, row):
    for s in (1, 2, 4, 8, 16, 32):
        x = x + jnp.where(row >= s, pltpu.roll(x, s, 0), 0.0)
    return x


def _revcumsum64(x, row):
    n = x.shape[0]
    for s in (1, 2, 4, 8, 16, 32):
        x = x + jnp.where(row < HCH - s, pltpu.roll(x, n - s, 0), 0.0)
    return x


def _head_mean(x):
    parts = [jnp.broadcast_to(_mean(x[:, HD * h:HD * (h + 1)]), (x.shape[0], HD)) for h in range(x.shape[1] // HD)]
    return jnp.concatenate(parts, axis=1)


def _seg_sum(x):
    n, c = x.shape
    s = jnp.sum(x.reshape(n // HCH, HCH, c), axis=1, keepdims=True)
    return jnp.broadcast_to(s, (n // HCH, HCH, c)).reshape(n, c)


def _hgrn_gates(fl, lbv, row):
    s = _sigmoid(fl)
    f = lbv + (1.0 - lbv) * s
    a = _cumsum64(jnp.log(f), row)
    a_mid = _seg_sum(jnp.where(row == HCH // 2 - 1, a, 0.0))
    a_last = _seg_sum(jnp.where(row == HCH - 1, a, 0.0))
    return s, f, a, a_mid, a_last


def _hgrn_fwd(proj, lb_table, norm_g, job=None):
    T = proj.shape[1]
    tb = min(HGRN_TOKENS, T)
    nc = tb // HCH

    def body(q_ref, fl_ref, v_ref, g_ref, lbt_ref, gn_ref, o_ref, ob_ref, stb_ref, st_s, o_s):
        @pl.when(pl.program_id(1) == 0)
        def _():
            st_s[...] = jnp.zeros_like(st_s)

        row = lax.broadcasted_iota(jnp.int32, (tb, HW), 0) & (HCH - 1)
        lbv = _sigmoid(lbt_ref[0:1, :] - lbt_ref[1:2, :])
        _, f, a, a_mid, a_last = _hgrn_gates(fl_ref[...], lbv, row)
        k = 1.0 - f
        qs = q_ref[...] * QSCALE
        q_in = (qs * jnp.exp(a - a_mid)).astype(BF16)
        k_in = (k * jnp.exp(a_mid - a)).astype(BF16)
        q_a = (qs * jnp.exp(a)).astype(BF16)
        k_d = (k * jnp.exp(a_last - a)).astype(BF16)
        dec = jnp.exp(a_last)
        vb = v_ref[...].astype(BF16)
        tri = (lax.broadcasted_iota(jnp.int32, (HCH, HCH), 0)
               >= lax.broadcasted_iota(jnp.int32, (HCH, HCH), 1))
        for c in range(nc):
            sl = slice(HCH * c, HCH * (c + 1))
            for hh in range(HGRN_HB):
                hs = slice(HD * hh, HD * (hh + 1))
                st = st_s[hh]
                stb_ref[hh, c] = st
                sc = jnp.where(tri, _mm_nt(q_in[sl, hs], k_in[sl, hs]), 0.0)
                o_s[sl, hs] = _mm(sc.astype(BF16), vb[sl, hs]) + _mm_nt(q_a[sl, hs], st.astype(BF16))
                d64 = dec[sl, hs]
                st_s[hh] = st * jnp.concatenate([d64, d64], axis=0) + _mm_tn(vb[sl, hs], k_d[sl, hs])
        o = o_s[...]
        r = lax.rsqrt(_head_mean(o * o) + EPS)
        g = g_ref[...]
        o_ref[...] = o
        ob_ref[...] = (o * r * gn_ref[...] * (g * _sigmoid(g))).astype(BF16)

    def col(off):
        return pl.BlockSpec((None, tb, HW), lambda h, cb: (off, cb, h))

    return _call(
        body, name="hgrn_fwd", grid=(NH // HGRN_HB, T // tb), job=job,
        args=(proj, proj, proj, proj, lb_table, norm_g),
        in_specs=[col(2), col(3), col(4), col(5),
                  pl.BlockSpec((2, HW), lambda h, cb: (0, h)), pl.BlockSpec((1, HW), lambda h, cb: (0, h))],
        out_specs=[pl.BlockSpec((tb, HW), lambda h, cb: (cb, h)), pl.BlockSpec((tb, HW), lambda h, cb: (cb, h)),
                   pl.BlockSpec((HGRN_HB, nc, HD, HD), lambda h, cb: (h, cb, 0, 0))],
        out_shape=[SDS((T, D), F32), SDS((T, D), BF16), SDS((NH, T // HCH, HD, HD), F32)],
        scratch_shapes=[pltpu.VMEM((HGRN_HB, HD, HD), F32), pltpu.VMEM((tb, HW), F32)])


def _merge_fwd(x, ab, ob, proj, w_a, w_b, w_out, job=None):
    T = x.shape[0]
    tm = min(512, T)

    def body(x_ref, ab_ref, ob_ref, ga_ref, gb_ref, wa_ref, wb_ref, wo_ref, mg_ref, x1_ref):
        ya = _mm(ab_ref[...], wa_ref[...])
        yb = _mm(ob_ref[...], wb_ref[...])
        merged = (_sigmoid(ga_ref[...]) * ya + _sigmoid(gb_ref[...]) * yb).astype(BF16)
        mg_ref[...] = merged
        x1_ref[...] = x_ref[...] + _mm(merged, wo_ref[...])

    t = lambda i: (i, 0)
    w = lambda i: (0, 0)
    return _call(
        body, name="merge_fwd", grid=(T // tm,), job=job, args=(x, ab, ob, proj, proj, w_a, w_b, w_out),
        in_specs=[pl.BlockSpec((tm, D), t), pl.BlockSpec((tm, D), t), pl.BlockSpec((tm, D), t),
                  pl.BlockSpec((None, tm, D), lambda i: (6, i, 0)), pl.BlockSpec((None, tm, D), lambda i: (7, i, 0)),
                  pl.BlockSpec((D, D), w), pl.BlockSpec((D, D), w), pl.BlockSpec((D, D), w)],
        out_specs=[pl.BlockSpec((tm, D), t)] * 2,
        out_shape=[SDS((T, D), BF16), SDS((T, D), F32)])


def _ffn_fwd_bwd(x1, target, g_ffn, g_fin, w_gu, w_down):
    T = x1.shape[0]
    tm = min(256, T)
    inv_d = 1.0 / D

    def body(x1_ref, tg_ref, gf_ref, gn_ref, wgu_ref, wd_ref,
             act_ref, dx2b_ref, h2b_ref, dgu_ref, dx1_ref, dx1b_ref, acc_ref):
        @pl.when(pl.program_id(0) == 0)
        def _():
            acc_ref[...] = jnp.zeros_like(acc_ref)

        x1v = x1_ref[...]
        gf = gf_ref[...]
        gn = gn_ref[...]
        rr1 = lax.rsqrt(_mean(x1v * x1v) + EPS)
        x1n = x1v * rr1
        h2b = (x1n * gf).astype(BF16)
        h2b_ref[...] = h2b
        gate = _mm(h2b, wgu_ref[0])
        up = _mm(h2b, wgu_ref[1])
        sg = _sigmoid(gate)
        si = gate * sg
        act = (si * up).astype(BF16)
        act_ref[...] = act
        x2 = x1v + _mm(act, wd_ref[...])
        rr2 = lax.rsqrt(_mean(x2 * x2) + EPS)
        x2n = x2 * rr2
        e = x2n * gn - tg_ref[...]
        acc_ref[0] += _rows8(e * e) * (0.5 * inv_d)
        dy = e * inv_d
        acc_ref[1] += _rows8(dy * x2n)
        dxn = dy * gn
        dx2 = rr2 * (dxn - x2n * _mean(dxn * x2n))
        dx2b = dx2.astype(BF16)
        dx2b_ref[...] = dx2b
        dact = _mm_nt(dx2b, wd_ref[...])
        dgate = (dact * up * (sg * (1.0 + gate * (1.0 - sg)))).astype(BF16)
        dup = (dact * si).astype(BF16)
        dgu_ref[0] = dgate
        dgu_ref[1] = dup
        dh2 = _mm_nt(dgate, wgu_ref[0]) + _mm_nt(dup, wgu_ref[1])
        acc_ref[2] += _rows8(dh2 * x1n)
        dxn1 = dh2 * gf
        dx1 = dx2 + rr1 * (dxn1 - x1n * _mean(dxn1 * x1n))
        dx1_ref[...] = dx1
        dx1b_ref[...] = dx1.astype(BF16)

    t = lambda i: (i, 0)
    w = lambda i: (0, 0)
    one = pl.Buffered(1)
    return pl.pallas_call(
        body, name="ffn_fwd_bwd", grid=(T // tm,),
        in_specs=[pl.BlockSpec((tm, D), t), pl.BlockSpec((tm, D), t),
                  pl.BlockSpec((1, D), w), pl.BlockSpec((1, D), w),
                  pl.BlockSpec((2, D, FF), lambda i: (0, 0, 0), pipeline_mode=one),
                  pl.BlockSpec((FF, D), w, pipeline_mode=one)],
        out_specs=[pl.BlockSpec((tm, FF), t), pl.BlockSpec((tm, D), t), pl.BlockSpec((tm, D), t),
                   pl.BlockSpec((2, tm, FF), lambda i: (0, i, 0)),
                   pl.BlockSpec((tm, D), t), pl.BlockSpec((tm, D), t),
                   pl.BlockSpec((3, 8, D), lambda i: (0, 0, 0))],
        out_shape=[SDS((T, FF), BF16), SDS((T, D), BF16), SDS((T, D), BF16),
                   SDS((2, T, FF), BF16), SDS((T, D), F32), SDS((T, D), BF16),
                   SDS((3, 8, D), F32)],
        compiler_params=_cparams(),
    )(x1, target, g_ffn, g_fin, w_gu, w_down)


def _merge_bwd(dx1b, ab, ob, proj, w_out, w_a, w_b, job=None):
    T = dx1b.shape[0]
    tm = min(512, T)

    def body(dx_ref, ab_ref, ob_ref, ga_ref, gb_ref, wo_ref, wa_ref, wb_ref, dya_ref, dyb_ref, dp_ref):
        dm = _mm_nt(dx_ref[...], wo_ref[...])
        sa = _sigmoid(ga_ref[...])
        sb = _sigmoid(gb_ref[...])
        dya_ref[...] = (dm * sa).astype(BF16)
        dyb_ref[...] = (dm * sb).astype(BF16)
        dp_ref[0] = (dm * _mm(ab_ref[...], wa_ref[...]) * sa * (1.0 - sa)).astype(BF16)
        dp_ref[1] = (dm * _mm(ob_ref[...], wb_ref[...]) * sb * (1.0 - sb)).astype(BF16)

    t = lambda i: (i, 0)
    w = lambda i: (0, 0)
    return _call(
        body, name="merge_bwd", grid=(T // tm,),
        in_specs=[pl.BlockSpec((tm, D), t), pl.BlockSpec((tm, D), t), pl.BlockSpec((tm, D), t),
                  pl.BlockSpec((None, tm, D), lambda i: (6, i, 0)), pl.BlockSpec((None, tm, D), lambda i: (7, i, 0)),
                  pl.BlockSpec((D, D), w), pl.BlockSpec((D, D), w), pl.BlockSpec((D, D), w)],
        out_specs=[pl.BlockSpec((tm, D), t)] * 2 + [pl.BlockSpec((2, tm, D), lambda i: (3, i, 0))],
        out_shape=[SDS((T, D), BF16), SDS((T, D), BF16), SDS((NIN, T, D), BF16)],
        args=(dx1b, ab, ob, proj, proj, w_out, w_a, w_b), job=job)


def _hgrn_bwd(dproj, dyb, w_b, o_raw, proj, st_before, lb_table, norm_g, job=None):
    T = dyb.shape[0]
    tb = min(HGRN_TOKENS, T)
    nc = tb // HCH
    nb = T // tb

    def body(dp_in, dyb_ref, wb_ref, o_ref, q_ref, fl_ref, v_ref, g_ref, stb_ref, lbt_ref, gn_ref,
             dp_ref, acc_ref, dst_s, dqin_s, dqa_s, dkin_s, dkd_s, dv_s, ddec_s):
        del dp_in

        @pl.when(pl.program_id(1) == 0)
        def _():
            dst_s[...] = jnp.zeros_like(dst_s)
            acc_ref[...] = jnp.zeros_like(acc_ref)

        row = lax.broadcasted_iota(jnp.int32, (tb, HW), 0) & (HCH - 1)
        gn = gn_ref[...]
        lbv = _sigmoid(lbt_ref[0:1, :] - lbt_ref[1:2, :])
        o = o_ref[...]
        r = lax.rsqrt(_head_mean(o * o) + EPS)
        on = o * r
        g = g_ref[...]
        sgm = _sigmoid(g)
        dob_v = _mm_nt(dyb_ref[...], wb_ref[...])
        dp_ref[3] = (dob_v * on * gn * (sgm * (1.0 + g * (1.0 - sgm)))).astype(BF16)
        do_n = dob_v * (g * sgm)
        acc_ref[1] += _rows8(do_n * on)
        dxn = do_n * gn
        do = (r * (dxn - on * _head_mean(dxn * on))).astype(BF16)
        s, f, a, a_mid, a_last = _hgrn_gates(fl_ref[...], lbv, row)
        k = 1.0 - f
        qs = q_ref[...] * QSCALE
        e_q = jnp.exp(a - a_mid)
        e_k = jnp.exp(a_mid - a)
        e_a = jnp.exp(a)
        e_l = jnp.exp(a_last - a)
        dec = jnp.exp(a_last)
        q_in = qs * e_q
        k_in = k * e_k
        q_a = qs * e_a
        k_d = k * e_l
        q_inb, k_inb, q_ab, k_db = (z.astype(BF16) for z in (q_in, k_in, q_a, k_d))
        vb = v_ref[...].astype(BF16)
        tri = (lax.broadcasted_iota(jnp.int32, (HCH, HCH), 0)
               >= lax.broadcasted_iota(jnp.int32, (HCH, HCH), 1))
        for c in reversed(range(nc)):
            sl = slice(HCH * c, HCH * (c + 1))
            for hh in range(HGRN_HB):
                hs = slice(HD * hh, HD * (hh + 1))
                stp = stb_ref[hh, c]
                dst = dst_s[hh]
                dstb = dst.astype(BF16)
                do_c = do[sl, hs]
                v_c = vb[sl, hs]
                dqa_s[sl, hs] = _mm(do_c, stp.astype(BF16))
                dkd_s[sl, hs] = _mm(v_c, dstb)
                ddec_s[sl, hs] = jnp.broadcast_to(jnp.sum(dst * stp, axis=0, keepdims=True), (HCH, HD))
                sc = jnp.where(tri, _mm_nt(q_inb[sl, hs], k_inb[sl, hs]), 0.0).astype(BF16)
                dsc = jnp.where(tri, _mm_nt(do_c, v_c), 0.0).astype(BF16)
                dv_s[sl, hs] = _mm_nt(k_db[sl, hs], dstb) + _mm_tn(sc, do_c)
                dqin_s[sl, hs] = _mm(dsc, k_inb[sl, hs])
                dkin_s[sl, hs] = _mm_tn(dsc, q_inb[sl, hs])
                d64 = dec[sl, hs]
                dst_s[hh] = dst * jnp.concatenate([d64, d64], axis=0) + _mm_tn(do_c, q_ab[sl, hs])
        dq_in = dqin_s[...]
        dq_a = dqa_s[...]
        dk_in = dkin_s[...]
        dk_d = dkd_s[...]
        dp_ref[0] = ((dq_in * e_q + dq_a * e_a) * QSCALE).astype(BF16)
        dp_ref[2] = dv_s[...].astype(BF16)
        tq = dq_in * q_in
        tk = dk_in * k_in
        td = dk_d * k_d
        d_a = tq + dq_a * q_a - tk - td
        d_a = d_a + jnp.where(row == HCH // 2 - 1, _seg_sum(tk - tq), 0.0)
        d_a = d_a + jnp.where(row == HCH - 1, _seg_sum(td) + ddec_s[...] * dec, 0.0)
        dlf = _revcumsum64(d_a, row)
        df = dlf / f - (dk_in * e_k + dk_d * e_l)
        dp_ref[1] = (df * (1.0 - lbv) * s * (1.0 - s)).astype(BF16)
        acc_ref[0] += _rows8(df * (1.0 - s))

    def col(off):
        return pl.BlockSpec((None, tb, HW), lambda h, cb: (off, nb - 1 - cb, h))

    hb = lambda h, cb: (nb - 1 - cb, h)
    return _call(
        body, name="hgrn_bwd", grid=(NH // HGRN_HB, nb), job=job,
        args=(dproj, dyb, w_b, o_raw, proj, proj, proj, proj, st_before, lb_table, norm_g),
        in_specs=[ANY, pl.BlockSpec((tb, D), lambda h, cb: (nb - 1 - cb, 0)),
                  pl.BlockSpec((HW, D), lambda h, cb: (h, 0)), pl.BlockSpec((tb, HW), hb),
                  col(2), col(3), col(4), col(5),
                  pl.BlockSpec((HGRN_HB, nc, HD, HD), lambda h, cb: (h, nb - 1 - cb, 0, 0)),
                  pl.BlockSpec((2, HW), lambda h, cb: (0, h)), pl.BlockSpec((1, HW), lambda h, cb: (0, h))],
        out_specs=[pl.BlockSpec((4, tb, HW), lambda h, cb: (0, nb - 1 - cb, h)),
                   pl.BlockSpec((2, 8, HW), lambda h, cb: (0, 0, h))],
        out_shape=[SDS(dproj.shape, BF16), SDS((2, 8, D), F32)],
        scratch_shapes=[pltpu.VMEM((HGRN_HB, HD, HD), F32)] + [pltpu.VMEM((tb, HW), F32)] * 6,
        aliases={0: 0})


def _gmlp_bwd(dproj, dya, w_a, proj, ln_g, ln_b, wm, wm_t, b_t):
    T = dya.shape[0]
    tm = min(256, T)

    def body(dp_in, dya_ref, wa_ref, u_ref, v_ref, lg_ref, lb_ref, wm_ref, wmt_ref, bt_ref,
             dp_ref, acc_ref, dws_ref, dmix_ref, du_s, dvn_s):
        del dp_in

        @pl.when(pl.program_id(0) == 0)
        def _():
            acc_ref[...] = jnp.zeros_like(acc_ref)
            dws_ref[...] = jnp.zeros_like(dws_ref)
            dmix_ref[...] = jnp.zeros_like(dmix_ref)

        u = u_ref[...]
        v = v_ref[...]
        lg = lg_ref[...]
        gu, t_u = _gelu(u)
        gv, t_v = _gelu(v)
        vhat, rs = _layer_norm_stats(gv)
        vnb = (vhat * lg + lb_ref[...]).astype(BF16)
        da_v = _mm_nt(dya_ref[...], wa_ref[...])
        for ch in range(tm // GCH):
            rows = slice(GCH * ch, GCH * (ch + 1))
            for g in range(NG):
                cols = slice(128 * g, 128 * (g + 1))
                vng = vnb[rows, cols]
                mixed = _mm(wm_ref[g], vng) + bt_ref[:, g:g + 1]
                dag = da_v[rows, cols]
                dmx = dag * gu[rows, cols]
                du_s[rows, cols] = dag * mixed
                dmxb = dmx.astype(BF16)
                dws_ref[:, cols] += _mm_nt(dmxb, vng)
                dmix_ref[:, cols] += dmx
                dvn_s[rows, cols] = _mm(wmt_ref[g], dmxb)
        dp_ref[0] = (du_s[...] * _gelu_grad(u, t_u)).astype(BF16)
        dvn = dvn_s[...]
        acc_ref[0] += _rows8(dvn * vhat)
        acc_ref[1] += _rows8(dvn)
        dvh = dvn * lg
        dgv = rs * (dvh - _mean(dvh) - vhat * _mean(dvh * vhat))
        dp_ref[1] = (dgv * _gelu_grad(v, t_v)).astype(BF16)

    row = lambda i: (0, 0)
    w3 = lambda i: (0, 0, 0)
    return pl.pallas_call(
        body, name="gmlp_bwd", grid=(T // tm,),
        in_specs=[ANY, pl.BlockSpec((tm, D), lambda i: (i, 0)), pl.BlockSpec((D, D), row),
                  pl.BlockSpec((None, tm, D), lambda i: (0, i, 0)), pl.BlockSpec((None, tm, D), lambda i: (1, i, 0)),
                  pl.BlockSpec((1, D), row), pl.BlockSpec((1, D), row),
                  pl.BlockSpec((NG, GCH, GCH), w3), pl.BlockSpec((NG, GCH, GCH), w3),
                  pl.BlockSpec((GCH, NG), row)],
        out_specs=[pl.BlockSpec((2, tm, D), lambda i: (2, i, 0)),
                   pl.BlockSpec((2, 8, D), w3), pl.BlockSpec((GCH, D), row), pl.BlockSpec((GCH, D), row)],
        out_shape=[SDS(dproj.shape, BF16), SDS((2, 8, D), F32), SDS((GCH, D), F32), SDS((GCH, D), F32)],
        scratch_shapes=[pltpu.VMEM((tm, D), F32), pltpu.VMEM((tm, D), F32)],
        input_output_aliases={0: 0},
        compiler_params=_cparams(),
    )(dproj, dya, w_a, proj, proj, ln_g, ln_b, wm, wm_t, b_t)


def _proj_bwd(dproj, w_in4, x, dx1, g_mix, job=None):
    T = x.shape[0]
    tm = min(256, T)
    order = (2, 3, 4, 5, 0, 1, 6, 7)

    def body(dp_ref, w_ref, x_ref, dx1_ref, g_ref, gx_ref, acc_ref):
        @pl.when(pl.program_id(0) == 0)
        def _():
            acc_ref[...] = jnp.zeros_like(acc_ref)

        dh = None
        for m, og in enumerate(order):
            part = _mm_nt(dp_ref[m], w_ref[og // 2, :, D * (og % 2):D * (og % 2 + 1)])
            dh = part if dh is None else dh + part
        xv = x_ref[...]
        r = lax.rsqrt(_mean(xv * xv) + EPS)
        xn = xv * r
        acc_ref[...] += _rows8(dh * xn)
        dxn = dh * g_ref[...]
        gx_ref[...] = dx1_ref[...] + r * (dxn - xn * _mean(dxn * xn))

    t = lambda i: (i, 0)
    return _call(
        body, name="proj_bwd", grid=(T // tm,),
        in_specs=[pl.BlockSpec((NIN, tm, D), lambda i: (0, i, 0)),
                  pl.BlockSpec((NCHIP, D, 2 * D), lambda i: (0, 0, 0), pipeline_mode=pl.Buffered(1)),
                  pl.BlockSpec((tm, D), t), pl.BlockSpec((tm, D), t), pl.BlockSpec((1, D), lambda i: (0, 0))],
        out_specs=[pl.BlockSpec((tm, D), t), pl.BlockSpec((8, D), lambda i: (0, 0))],
        out_shape=[SDS((T, D), F32), SDS((8, D), F32)],
        args=(dproj, w_in4, x, dx1, g_mix), job=job)


def _dw_call(name, a, b, a_spec, b_spec, o_spec, out_shape, nblk, tt, job=None, prefetch=None):
    T = a.shape[-2]

    def body(*refs):
        a_ref, b_ref, o_ref = refs[-3:]

        @pl.when(pl.program_id(1) == 0)
        def _():
            o_ref[...] = jnp.zeros_like(o_ref)
        o_ref[...] += _mm_tn(a_ref[...], b_ref[...])

    (out,), job_out = _call(
        body, name=name, grid=(nblk, T // tt), in_specs=[a_spec, b_spec], out_specs=[o_spec],
        out_shape=[out_shape], args=(a, b), job=job, prefetch=prefetch)
    return out, job_out


def _dw_in_half(name, place, hb, dproj, mine, job=None):
    tt = min(DW_TOKENS, hb.shape[0])

    def comp(k, pc):
        return _component_of(2 * k + (pc[1] if mine else 1 - pc[1]))

    return _dw_call(
        name, hb, dproj,
        pl.BlockSpec((tt, D), lambda k, t, pc: (t, 0)),
        pl.BlockSpec((None, tt, D), lambda k, t, pc: (comp(k, pc), t, 0)),
        pl.BlockSpec((None, D, D), lambda k, t, pc: (k, 0, 0)),
        SDS((NCHIP, D, D), F32), NCHIP, tt, job, place)


def _dw_gate_up(h2b, dgu, job=None):
    tt = min(DW_TOKENS, h2b.shape[0])
    return _dw_call(
        "dw_gate_up", h2b, dgu,
        pl.BlockSpec((tt, D), lambda k, t: (t, 0)),
        pl.BlockSpec((None, tt, FFS), lambda k, t: (k // 2, t, k % 2)),
        pl.BlockSpec((None, D, FFS), lambda k, t: (k, 0, 0)),
        SDS((NCHIP, D, FFS), F32), NCHIP, tt, job)


def _dw_down(act, dx2b, job=None):
    tt = min(DW_TOKENS, act.shape[0])
    g, job_out = _dw_call(
        "dw_down", act, dx2b,
        pl.BlockSpec((tt, FFS), lambda k, t: (t, k)),
        pl.BlockSpec((tt, D), lambda k, t: (t, 0)),
        pl.BlockSpec((FFS, D), lambda k, t: (k, 0)),
        SDS((FF, D), F32), 2, tt, job)
    return g.reshape(NCHIP, FF // NCHIP, D), job_out


def _dw_square(name, a, b, job=None):
    tt = min(DW_TOKENS, a.shape[0])
    g, job_out = _dw_call(
        name, a, b,
        pl.BlockSpec((tt, D), lambda k, t: (t, 0)), pl.BlockSpec((tt, D), lambda k, t: (t, 0)),
        pl.BlockSpec((D, D), lambda k, t: (0, 0)), SDS((D, D), F32), 1, tt, job)
    return g.reshape(NCHIP, D // NCHIP, D), job_out


def _place():
    x, y, c = lax.axis_index("x"), lax.axis_index("y"), lax.axis_index("c")
    return x, y, c, 2 * x + y


def _chip_at(x, y, s):
    return x ^ (s >> 1), y ^ (s & 1)


class _Job:
    def __init__(self, ins, out_shapes, sems, start, finish, aliases=None, mid=None):
        self.ins, self.out_shapes, self.sems = list(ins), list(out_shapes), list(sems)
        self.start, self.finish, self.aliases = start, finish, dict(aliases or {})
        self.mid = mid if mid is not None else (lambda ins, outs, sems: None)


def _join_jobs(*jobs):
    def cut(refs, sizes):
        out, at = [], 0
        for n in sizes:
            out.append(refs[at:at + n])
            at += n
        return out

    ni = [len(j.ins) for j in jobs]
    no = [len(j.out_shapes) for j in jobs]
    ns = [len(j.sems) for j in jobs]

    def run(which):
        def go(ins, outs, sems):
            for j, a, b, c in zip(jobs, cut(ins, ni), cut(outs, no), cut(sems, ns)):
                getattr(j, which)(a, b, c)
        return go

    aliases = {}
    for k, j in enumerate(jobs):
        for a, b in j.aliases.items():
            aliases[sum(ni[:k]) + a] = sum(no[:k]) + b
    return _Job([a for j in jobs for a in j.ins], [o for j in jobs for o in j.out_shapes],
                [s for j in jobs for s in j.sems], run("start"), run("finish"), aliases, run("mid"))


def _call(body, *, name, grid, in_specs, out_specs, out_shape, args, scratch_shapes=(), aliases=None,
          job=None, prefetch=None):
    n_in, n_out, n_scr = len(in_specs), len(out_specs), len(scratch_shapes)
    npf = 0 if prefetch is None else 1
    job = job if job is not None else _Job([], [], [], lambda *a: None, lambda *a: None)
    ji, jo = len(job.ins), len(job.out_shapes)
    steps = math.prod(grid)

    def wrapped(*refs):
        pf, refs = refs[:npf], refs[npf:]
        ins, jin = refs[:n_in], refs[n_in:n_in + ji]
        o0 = n_in + ji
        outs, jout = refs[o0:o0 + n_out], refs[o0 + n_out:o0 + n_out + jo]
        s0 = o0 + n_out + jo
        scr, jsem = refs[s0:s0 + n_scr], refs[s0 + n_scr:]
        step = functools.reduce(lambda acc, ag: acc * ag[1] + pl.program_id(ag[0]), enumerate(grid), 0)
        if ji or jo:
            @pl.when(step == 0)
            def _():
                job.start(jin, jout, jsem)

        body(*pf, *ins, *outs, *scr)

        if ji or jo:
            @pl.when(step == steps // 2)
            def _():
                job.mid(jin, jout, jsem)

            @pl.when(step == steps - 1)
            def _():
                job.finish(jin, jout, jsem)

    io = {npf + a: b for a, b in dict(aliases or {}).items()}
    io.update({npf + n_in + a: n_out + b for a, b in job.aliases.items()})
    kw = dict(in_specs=list(in_specs) + [ANY] * ji, out_specs=list(out_specs) + [ANY] * jo,
              scratch_shapes=list(scratch_shapes) + job.sems)
    if npf:
        kw = dict(grid_spec=pltpu.PrefetchScalarGridSpec(num_scalar_prefetch=1, grid=grid, **kw))
    else:
        kw["grid"] = grid
    res = pl.pallas_call(
        wrapped, name=name, out_shape=list(out_shape) + job.out_shapes, input_output_aliases=io,
        compiler_params=_cparams(has_side_effects=bool(ji or jo)), **kw,
    )(*(() if prefetch is None else (prefetch,)), *args, *job.ins)
    return list(res[:n_out]), list(res[n_out:])


def _cast_shards(name, place, ws, paired=False):
    n = len(ws)
    rows, cols = ws[0].shape
    tr = 352 if rows % 352 == 0 else 256
    shape = (2, rows, 2 * cols) if paired else (NCHIP, rows, cols)
    mine = (lambda i, pc: (pc[0] // 2, i, pc[0] % 2)) if paired else (lambda i, pc: (pc[0], i, 0))

    def body(pc_ref, *refs):
        del pc_ref
        for w_ref, o_ref in zip(refs[:n], refs[n:]):
            o_ref[...] = w_ref[...].astype(BF16)

    return pl.pallas_call(
        body, name=name,
        grid_spec=pltpu.PrefetchScalarGridSpec(
            num_scalar_prefetch=1, grid=(rows // tr,),
            in_specs=[pl.BlockSpec((tr, cols), lambda i, pc: (i, 0))] * n,
            out_specs=[pl.BlockSpec((None, tr, cols), mine)] * n),
        out_shape=[SDS(shape, BF16)] * n,
        compiler_params=_cparams(),
    )(place, *ws)


def _sibling_copy(ref, send_sem, recv_sem):
    x, y, c, _ = _place()
    return pltpu.make_async_remote_copy(src_ref=ref, dst_ref=ref, send_sem=send_sem, recv_sem=recv_sem,
                                        device_id=(x, y, 1 - c), device_id_type=MESH)


def _slot(arr, chip):
    if arr.shape[0] == NCHIP:
        return arr.at[chip]
    cols = arr.shape[2] // 2
    return arr.at[chip // 2, :, pl.ds(pl.multiple_of((chip % 2) * cols, 128), cols)]


def _half_rows(arr, slot, core):
    half = arr.shape[1] // 2
    return _slot(arr, slot).at[pl.ds(pl.multiple_of(core * half, 16), half)]


def _quarter_rows(arr, slot, core, q):
    quarter = arr.shape[1] // 4
    return _slot(arr, slot).at[pl.ds(pl.multiple_of((2 * core + q) * quarter, 16), quarter)]


def _chip_copy(ref, dist, send_sem, recv_sem):
    x, y, c, _ = _place()
    cx, cy = _chip_at(x, y, dist)
    return pltpu.make_async_remote_copy(src_ref=ref, dst_ref=ref, send_sem=send_sem, recv_sem=recv_sem,
                                        device_id=(cx, cy, c), device_id_type=MESH)


def _gather_sems(n):
    dma = pltpu.SemaphoreType.DMA
    return [dma((n, 2))] * 4 + [dma((n, 4))] * 2


def _gather_start(arrs, sems):
    dsend, drecv = sems[0], sems[1]
    _, _, c, j = _place()
    for w, arr in enumerate(arrs):
        for dist in (1, 2):
            _chip_copy(_half_rows(arr, j, c), dist, dsend.at[w, dist - 1], drecv.at[w, dist - 1]).start()


def _gather_land(arrs, sems, dist, first=0):
    dsend, drecv, rsend, rrecv, fsend, frecv = sems
    _, _, c, j = _place()
    if dist < 3:
        other = 3 - dist
        for w, arr in enumerate(arrs, first):
            landed = _half_rows(arr, j ^ dist, c)
            _chip_copy(landed, dist, dsend.at[w, dist - 1], drecv.at[w, dist - 1]).wait_recv()
            relay = _quarter_rows(arr, j ^ dist, c, other - 1)
            _chip_copy(relay, other, rsend.at[w, other - 1], rrecv.at[w, other - 1]).start()
            _sibling_copy(landed, fsend.at[w, dist - 1], frecv.at[w, dist - 1]).start()
        for w, arr in enumerate(arrs, first):
            theirs = _half_rows(arr, j ^ dist, 1 - c)
            _sibling_copy(theirs, fsend.at[w, dist - 1], frecv.at[w, dist - 1]).wait_recv()
    else:
        for w, arr in enumerate(arrs, first):
            for via in (1, 2):
                piece = _quarter_rows(arr, j ^ 3, c, via - 1)
                _chip_copy(piece, via, rsend.at[w, via - 1], rrecv.at[w, via - 1]).wait_recv()
                _sibling_copy(piece, fsend.at[w, 1 + via], frecv.at[w, 1 + via]).start()
        for w, arr in enumerate(arrs, first):
            for via in (1, 2):
                theirs = _quarter_rows(arr, j ^ 3, 1 - c, via - 1)
                _sibling_copy(theirs, fsend.at[w, 1 + via], frecv.at[w, 1 + via]).wait_recv()


def _gather_drain(arrs, sems):
    dsend, drecv, rsend, rrecv, fsend, frecv = sems
    _, _, c, j = _place()
    for w, arr in enumerate(arrs):
        for dist in (1, 2):
            other = 3 - dist
            _chip_copy(_half_rows(arr, j, c), dist, dsend.at[w, dist - 1], drecv.at[w, dist - 1]).wait_send()
            _chip_copy(_quarter_rows(arr, j ^ dist, c, other - 1), other,
                       rsend.at[w, other - 1], rrecv.at[w, other - 1]).wait_send()
            _sibling_copy(_half_rows(arr, j ^ dist, c), fsend.at[w, dist - 1], frecv.at[w, dist - 1]).wait_send()
            _sibling_copy(_quarter_rows(arr, j ^ 3, c, dist - 1),
                          fsend.at[w, 1 + dist], frecv.at[w, 1 + dist]).wait_send()


def _gather_neighbours(arrs, sems):
    _gather_land(arrs, sems, 1)
    _gather_land(arrs, sems, 2)


def _gather_finish(arrs, sems):
    _gather_land(arrs, sems, 3)
    _gather_drain(arrs, sems)


def _gather_job(arrs):
    n = len(arrs)
    return _Job(arrs, [SDS(a.shape, a.dtype) for a in arrs], _gather_sems(n),
                lambda ins, outs, sems: _gather_start(outs, sems),
                lambda ins, outs, sems: _gather_finish(outs, sems), {k: k for k in range(n)},
                mid=lambda ins, outs, sems: _gather_neighbours(outs, sems))


def _exchange_job(arrs, out_shapes, n, copies):
    def start(ins, outs, sems):
        for cp in copies(ins, outs, sems[0], sems[1]):
            cp.start()

    def finish(ins, outs, sems):
        for cp in copies(ins, outs, sems[0], sems[1]):
            cp.wait()

    return _Job(arrs, out_shapes, [pltpu.SemaphoreType.DMA((n,))] * 2, start, finish)


def _pair_exchange_job(grads):
    def copies(ins, outs, send_sem, recv_sem):
        x, y, c, _ = _place()
        res = []
        for w in range(len(grads)):
            half = ins[w].shape[1] // 2
            theirs = pl.ds(pl.multiple_of((1 - c) * half, 8), half)
            res.append(pltpu.make_async_remote_copy(
                src_ref=ins[w].at[:, theirs, :], dst_ref=outs[w], send_sem=send_sem.at[w],
                recv_sem=recv_sem.at[w], device_id=(x, y, 1 - c), device_id_type=MESH))
        return res

    return _exchange_job(grads, [SDS((NCHIP, g.shape[1] // 2, g.shape[2]), F32) for g in grads],
                         len(grads), copies)


def _row_tile(rows, cols):
    tr = rows
    while tr * cols * 4 > ELEMENTWISE_BLOCK_BYTES and tr % 32 == 0:
        tr //= 2
    return tr


def _pair_sums(name, place, gs, sibs):
    n = len(gs)
    half, cols = sibs[0].shape[1], sibs[0].shape[2]
    tr = _row_tile(half, cols)
    nt = half // tr
    mine = nt if gs[0].shape[1] == 2 * half else 0

    def body(pc_ref, *refs):
        del pc_ref
        for g_ref, s_ref, own_ref, out_ref in zip(refs[:n], refs[n:2 * n], refs[2 * n:3 * n], refs[3 * n:]):
            v = g_ref[...] + s_ref[...]
            out_ref[...] = v.astype(BF16)

            @pl.when(pl.program_id(1) == 0)
            def _():
                own_ref[...] = v

    res = pl.pallas_call(
        body, name=name,
        grid_spec=pltpu.PrefetchScalarGridSpec(
            num_scalar_prefetch=1, grid=(nt, NCHIP),
            in_specs=[pl.BlockSpec((None, tr, cols), lambda i, s, pc: (pc[0] ^ s, pc[1] * mine + i, 0))] * n
            + [pl.BlockSpec((None, tr, cols), lambda i, s, pc: (pc[0] ^ s, i, 0))] * n,
            out_specs=[pl.BlockSpec((tr, cols), lambda i, s, pc: (i, 0))] * n
            + [pl.BlockSpec((None, tr, cols), lambda i, s, pc: (s, i, 0))] * n),
        out_shape=[SDS((half, cols), F32)] * n + [SDS((NCHIP, half, cols), BF16)] * n,
        compiler_params=_cparams(),
    )(place, *gs, *sibs)
    return res[:n], res[n:]


def _chip_exchange_job(parts):
    def copies(ins, outs, send_sem, recv_sem):
        x, y, c, _ = _place()
        res = []
        for w in range(len(parts)):
            for s in range(1, NCHIP):
                cx, cy = _chip_at(x, y, s)
                k = w * (NCHIP - 1) + s - 1
                res.append(pltpu.make_async_remote_copy(
                    src_ref=ins[w].at[s], dst_ref=outs[w].at[s - 1], send_sem=send_sem.at[k],
                    recv_sem=recv_sem.at[k], device_id=(cx, cy, c), device_id_type=MESH))
        return res

    return _exchange_job(parts, [SDS((NCHIP - 1,) + p.shape[1:], BF16) for p in parts],
                         len(parts) * (NCHIP - 1), copies)


def _chip_sums(name, owns, rems):
    n = len(owns)
    half, cols = owns[0].shape
    tr = _row_tile(half, cols)

    def body(*refs):
        for own_ref, rem_ref, out_ref in zip(refs[:n], refs[n:2 * n], refs[2 * n:]):
            out_ref[...] = (((own_ref[...] + rem_ref[0].astype(F32)) + rem_ref[1].astype(F32))
                            + rem_ref[2].astype(F32))

    return pl.pallas_call(
        body, name=name, grid=(half // tr,),
        in_specs=[pl.BlockSpec((tr, cols), lambda i: (i, 0))] * n
        + [pl.BlockSpec((NCHIP - 1, tr, cols), lambda i: (0, i, 0))] * n,
        out_specs=[pl.BlockSpec((tr, cols), lambda i: (i, 0))] * n,
        out_shape=[SDS((half, cols), F32)] * n,
        compiler_params=_cparams(),
    )(*owns, *rems)


def _share_halves_job(halves):
    def copies(ins, outs, send_sem, recv_sem):
        x, y, c, _ = _place()
        return [pltpu.make_async_remote_copy(
            src_ref=ins[w], dst_ref=outs[w], send_sem=send_sem.at[w], recv_sem=recv_sem.at[w],
            device_id=(x, y, 1 - c), device_id_type=MESH) for w in range(len(halves))]

    return _exchange_job(halves, [SDS(h.shape, F32) for h in halves], len(halves), copies)


def _adamw_math(w, g, m, v):
    m = B1 * m + (1.0 - B1) * g
    v = B2 * v + (1.0 - B2) * (g * g)
    m_hat = m / (1.0 - B1 ** STEP)
    v_hat = v / (1.0 - B2 ** STEP)
    delta = -LR * (m_hat / (jnp.sqrt(v_hat) + AEPS) + WD * w)
    return delta, m, v


def _adamws(name, place, ws, owns, sibs, ms, vs):
    n = len(ws)
    rows, cols = ws[0].shape
    by_cols = owns[0].shape[0] == rows
    half, pc_cols = (rows, cols // 2) if by_cols else (rows // 2, cols)
    tr = _row_tile(half, pc_cols)
    nt = half // tr

    def body(pc_ref, *refs):
        ins, outs = refs[:5 * n], refs[5 * n:]
        for k in range(n):
            w_ref, own_ref, sib_ref, m_ref, v_ref = ins[5 * k:5 * k + 5]
            g = jnp.where(pl.program_id(0) == pc_ref[1], own_ref[...], sib_ref[...])
            d, mn, vn = _adamw_math(w_ref[...], g, m_ref[...], v_ref[...])
            for ref, val in zip(outs[4 * k:4 * k + 4], (g, d, mn, vn)):
                ref[...] = val

    full = pl.BlockSpec((tr, pc_cols), (lambda h, i, pc: (i, h)) if by_cols else (lambda h, i, pc: (h * nt + i, 0)))
    part = pl.BlockSpec((tr, pc_cols), lambda h, i, pc: (i, 0))
    res = pl.pallas_call(
        body, name=name,
        grid_spec=pltpu.PrefetchScalarGridSpec(
            num_scalar_prefetch=1, grid=(2, nt),
            in_specs=[full, part, part, full, full] * n, out_specs=[full] * (4 * n)),
        out_shape=[SDS((rows, cols), F32)] * (4 * n),
        compiler_params=_cparams(),
    )(place, *[a for group in zip(ws, owns, sibs, ms, vs) for a in group])
    return [tuple(res[4 * k:4 * k + 4]) for k in range(n)]


def _small_allreduce_adamw(sp, w, m, v, job):
    shape = sp.shape
    ji, jo = len(job.ins), len(job.out_shapes)

    def body(sp_ref, w_ref, m_ref, v_ref, *rest):
        jin, (g_ref, d_ref, mo_ref, vo_ref), jout = rest[:ji], rest[ji:ji + 4], rest[ji + 4:ji + 4 + jo]
        sib_s, pair_s, chip_s, send_sem, recv_sem = rest[ji + 4 + jo:ji + 9 + jo]
        jsem = rest[ji + 9 + jo:]
        job.start(jin, jout, jsem)
        x, y, c, j = _place()
        cp = pltpu.make_async_remote_copy(
            src_ref=sp_ref, dst_ref=sib_s, send_sem=send_sem.at[0], recv_sem=recv_sem.at[0],
            device_id=(x, y, 1 - c), device_id_type=MESH)
        cp.start()
        cp.wait()
        pair_s[...] = sp_ref[...] + sib_s[...]
        half = shape[0] // 2
        mine = pl.ds(pl.multiple_of(c * half, 8), half)
        cps = []
        for s in range(1, NCHIP):
            cx, cy = _chip_at(x, y, s)
            cp = pltpu.make_async_remote_copy(
                src_ref=pair_s.at[mine], dst_ref=chip_s.at[s, mine], send_sem=send_sem.at[s],
                recv_sem=recv_sem.at[s], device_id=(cx, cy, c), device_id_type=MESH)
            cp.start()
            cps.append(cp)
        chip_s[0] = pair_s[...]
        for cp in cps:
            cp.wait()
        cps = []
        for s in range(1, NCHIP):
            cp = pltpu.make_async_remote_copy(
                src_ref=chip_s.at[s, mine], dst_ref=chip_s.at[s, mine], send_sem=send_sem.at[NCHIP + s],
                recv_sem=recv_sem.at[NCHIP + s], device_id=(x, y, 1 - c), device_id_type=MESH)
            cp.start()
            cps.append(cp)
        for cp in cps:
            cp.wait()
        tot = chip_s[j]
        for k in range(1, NCHIP):
            tot = tot + chip_s[k ^ j]
        g_ref[...] = tot
        d, mn, vn = _adamw_math(w_ref[...], tot, m_ref[...], v_ref[...])
        d_ref[...] = d
        mo_ref[...] = mn
        vo_ref[...] = vn
        job.mid(jin, jout, jsem)
        job.finish(jin, jout, jsem)

    vm = pl.BlockSpec(memory_space=pltpu.VMEM)
    res = pl.pallas_call(
        body, name="small_allreduce_adamw",
        in_specs=[vm] * 4 + [ANY] * ji, out_specs=[vm] * 4 + [ANY] * jo,
        out_shape=[SDS(shape, F32)] * 4 + job.out_shapes,
        scratch_shapes=[pltpu.VMEM(shape, F32), pltpu.VMEM(shape, F32), pltpu.VMEM((NCHIP,) + shape, F32),
                        pltpu.SemaphoreType.DMA((2 * NCHIP,)), pltpu.SemaphoreType.DMA((2 * NCHIP,))] + job.sems,
        input_output_aliases={4 + a: 4 + b for a, b in job.aliases.items()},
        compiler_params=pltpu.CompilerParams(has_side_effects=True),
    )(sp, w, m, v, *job.ins)
    return res[:4], res[4:]


def _pack_small(first, mix, ln_g, ln_b, b_s, lbt, hn, ffn, fin, w_s):
    rows = [first.reshape(1, D), mix.reshape(1, D), ln_g.reshape(1, D), ln_b.reshape(1, D),
            b_s.reshape(1, D), lbt.reshape(2, D), hn.reshape(1, D), ffn.reshape(1, D), fin.reshape(1, D),
            jnp.zeros((6, D), F32)]
    return jnp.concatenate(rows + [w_s.reshape(NG, GCH, GCH).transpose(1, 0, 2).reshape(GCH, D)], axis=0)


def _unpack_small(p):
    w_s = p[16:].reshape(GCH, NG, GCH).transpose(1, 0, 2).reshape(1, NG, GCH, GCH)
    return dict(norm_mix_g=p[1:2], gmlp_ln_g=p[2:3], gmlp_ln_b=p[3:4], gmlp_b_s=p[4].reshape(1, NG, GCH),
                hgrn_lb_table=p[5:7], hgrn_norm_g=p[7:8], norm_ffn_g=p[8:9], norm_final_g=p[9],
                gmlp_w_s=w_s)


SMALL = ("norm_mix_g", "gmlp_ln_g", "gmlp_ln_b", "gmlp_w_s", "gmlp_b_s", "hgrn_lb_table", "hgrn_norm_g",
         "norm_ffn_g", "norm_final_g")
BIG = ("w_in", "w_gate_up", "w_branch_a", "w_branch_b", "w_out", "w_down")
ORDER = ("norm_mix_g", "w_in", "gmlp_ln_g", "gmlp_ln_b", "gmlp_w_s", "gmlp_b_s", "hgrn_lb_table",
         "hgrn_norm_g", "w_branch_a", "w_branch_b", "w_out", "norm_ffn_g", "w_gate_up", "w_down",
         "norm_final_g")


def kernel(x, norm_mix_g, w_in, gmlp_ln_g, gmlp_ln_b, gmlp_w_s, gmlp_b_s, hgrn_lb_table, hgrn_norm_g, w_branch_a, w_branch_b, w_out, norm_ffn_g, w_gate_up, w_down, norm_final_g, loss_target, m_norm_mix_g, m_w_in, m_gmlp_ln_g, m_gmlp_ln_b, m_gmlp_w_s, m_gmlp_b_s, m_hgrn_lb_table, m_hgrn_norm_g, m_w_branch_a, m_w_branch_b, m_w_out, m_norm_ffn_g, m_w_gate_up, m_w_down, m_norm_final_g, v_norm_mix_g, v_w_in, v_gmlp_ln_g, v_gmlp_ln_b, v_gmlp_w_s, v_gmlp_b_s, v_hgrn_lb_table, v_hgrn_norm_g, v_w_branch_a, v_w_branch_b, v_w_out, v_norm_ffn_g, v_w_gate_up, v_w_down, v_norm_final_g):
    args = dict(locals())
    T = x.shape[1]
    xs = x.reshape(T, D)
    target = loss_target.reshape(T, D)
    big = {n: args[n].reshape(args[n].shape[1:]) for n in BIG}
    big_m = {n: args["m_" + n].reshape(args[n].shape[1:]) for n in BIG}
    big_v = {n: args["v_" + n].reshape(args[n].shape[1:]) for n in BIG}

    x_i, y_i, c_i = lax.axis_index("x"), lax.axis_index("y"), lax.axis_index("c")
    place = jnp.stack([2 * x_i + y_i, c_i]).astype(jnp.int32)
    def by_shape(names):
        groups = []
        for n in names:
            if groups and big[groups[-1][0]].shape == big[n].shape:
                groups[-1].append(n)
            else:
                groups.append([n])
        return groups

    cast = {}
    for grp in by_shape(BIG):
        cast.update(zip(grp, _cast_shards("cast_" + grp[0], place, [big[n] for n in grp],
                                          paired=grp[0] == "w_gate_up")))
    tril = jnp.tril(jnp.ones((GCH, GCH), bool))
    wm = jnp.where(tril, gmlp_w_s[0], 0.0).astype(BF16)
    wm_t = jnp.swapaxes(wm, 1, 2)
    b_t = gmlp_b_s[0].T

    (proj, hb), w_in4, (w_a4, w_b4, w_out4, w_down4) = _proj_fwd(
        place, xs, norm_mix_g, cast["w_in"], [cast[n] for n in ("w_branch_a", "w_branch_b", "w_out", "w_down")])
    (ab,), _ = _gmlp_fwd(proj, gmlp_ln_g, gmlp_ln_b, wm, b_t)
    (o_raw, obb, st_before), (w_gu,) = _hgrn_fwd(
        proj, hgrn_lb_table, hgrn_norm_g, job=_gather_job([cast["w_gate_up"]]))
    w_a, w_b, w_o = (w.reshape(D, D) for w in (w_a4, w_b4, w_out4))
    (mgb, x1), _ = _merge_fwd(xs, ab, obb, proj, w_a, w_b, w_o)
    w_dn = w_down4.reshape(FF, D)
    act, dx2b, h2b, dgu, dx1, dx1b, acc_ffn = _ffn_fwd_bwd(
        x1, target, norm_ffn_g, norm_final_g.reshape(1, D), w_gu, w_dn)

    grads, owns, parts, halves, sibh = {}, {}, {}, {}, {}

    def pair_sums(names, sibs):
        sib_of = dict(zip(names, sibs))
        for grp in by_shape(names):
            o, p = _pair_sums("rs_pair_sum_" + grp[0], place, [grads[n] for n in grp], [sib_of[n] for n in grp])
            owns.update(zip(grp, o))
            parts.update(zip(grp, p))

    def chip_sums(names, got):
        rem_of = dict(zip(names, got))
        for grp in by_shape(names):
            h = _chip_sums("rs_chip_sum_" + grp[0], [owns[n] for n in grp], [rem_of[n] for n in grp])
            halves.update(zip(grp, h))

    ffn, mix = ("w_gate_up", "w_down"), ("w_branch_a", "w_branch_b", "w_out")
    grads["w_gate_up"], _ = _dw_gate_up(h2b, dgu)
    grads["w_down"], _ = _dw_down(act, dx2b)
    (dya, dyb, dproj), got = _merge_bwd(
        dx1b, ab, obb, proj, w_o, w_a, w_b, job=_pair_exchange_job([grads[n] for n in ffn]))
    pair_sums(ffn, got)
    grads["w_branch_a"], _ = _dw_square("dw_branch_a", ab, dya)
    grads["w_branch_b"], _ = _dw_square("dw_branch_b", obb, dyb)
    grads["w_out"], _ = _dw_square("dw_out", mgb, dx1b)
    (dproj, acc_hgrn), got = _hgrn_bwd(
        dproj, dyb, w_b, o_raw, proj, st_before, hgrn_lb_table, hgrn_norm_g,
        job=_join_jobs(_chip_exchange_job([parts[n] for n in ffn]), _pair_exchange_job([grads[n] for n in mix])))
    chip_sums(ffn, got[:2])
    pair_sums(mix, got[2:])
    dproj, acc_ln, dws, dmix = _gmlp_bwd(dproj, dya, w_a, proj, gmlp_ln_g, gmlp_ln_b, wm, wm_t, b_t)
    for_sibling, got = _dw_in_half(
        "dw_in_sibling_half", place, hb, dproj, False,
        job=_join_jobs(_share_halves_job([halves[n] for n in ffn]), _chip_exchange_job([parts[n] for n in mix])))
    sibh.update(zip(ffn, got[:2]))
    chip_sums(mix, got[2:])
    grads["w_in"], got = _dw_in_half(
        "dw_in_own_half", place, hb, dproj, True, job=_share_halves_job([for_sibling]))
    pair_sums(("w_in",), got)
    (grad_x, acc_mix), got = _proj_bwd(
        dproj, w_in4, xs, dx1, norm_mix_g,
        job=_join_jobs(_chip_exchange_job([parts["w_in"]]), _share_halves_job([halves[n] for n in mix])))
    chip_sums(("w_in",), got[:1])
    sibh.update(zip(mix, got[1:]))

    lbv = jax.nn.sigmoid(hgrn_lb_table[0] - hgrn_lb_table[1])
    d_t0 = jnp.sum(acc_hgrn[0], axis=0) * lbv * (1.0 - lbv)
    loss_row = jnp.zeros((D,), F32).at[0].set(jnp.sum(acc_ffn[0]))
    dws_m = jnp.where(tril[:, None, :], dws.reshape(GCH, NG, GCH), 0.0).transpose(1, 0, 2)
    db_s = jnp.sum(dmix.reshape(GCH, NG, GCH), axis=-1).T
    sp = _pack_small(loss_row, jnp.sum(acc_mix, 0), jnp.sum(acc_ln[0], 0), jnp.sum(acc_ln[1], 0), db_s,
                     jnp.stack([d_t0, -d_t0]), jnp.sum(acc_hgrn[1], 0), jnp.sum(acc_ffn[2], 0),
                     jnp.sum(acc_ffn[1], 0), dws_m)
    zero = jnp.zeros((D,), F32)

    def pack(prefix):
        a = lambda n: args[prefix + n]
        return _pack_small(zero, a("norm_mix_g"), a("gmlp_ln_g"), a("gmlp_ln_b"), a("gmlp_b_s"),
                           a("hgrn_lb_table"), a("hgrn_norm_g"), a("norm_ffn_g"), a("norm_final_g"),
                           a("gmlp_w_s"))

    packed, (sibh["w_in"],) = _small_allreduce_adamw(
        sp, pack(""), pack("m_"), pack("v_"), _share_halves_job([halves["w_in"]]))
    loss = packed[0][0, 0]
    small = [_unpack_small(p) for p in packed]
    out = {n: tuple(s[n] for s in small) for n in SMALL}
    for grp in by_shape(BIG):
        res = _adamws("adamw_" + grp[0], place, *[[d[n] for n in grp] for d in (big, halves, sibh, big_m, big_v)])
        for n, quad in zip(grp, res):
            out[n] = tuple(a.reshape(args[n].shape) for a in quad)
    return (loss, grad_x.reshape(x.shape), *[out[n][0] for n in ORDER], *[out[n][1] for n in ORDER],
            *[out[n][2] for n in ORDER], *[out[n][3] for n in ORDER])
```

```python
import functools
import math

import jax
import jax.numpy as jnp
from jax import lax
from jax.experimental import pallas as pl
from jax.experimental.pallas import tpu as pltpu

F32 = jnp.float32
BF16 = jnp.bfloat16
SDS = jax.ShapeDtypeStruct
MESH = pl.DeviceIdType.MESH
ANY = pl.BlockSpec(memory_space=pl.ANY)

D = 1024
NIN = 8
NG = 8
GCH = 128
NH = 8
HD = 128
HCH = 64
HGRN_HB = 8
HGRN_TOKENS = 256
HW = HGRN_HB * HD
DW_TOKENS = 2048
ELEMENTWISE_BLOCK_BYTES = 2 * 1024 * 1024
PROJ_OUT_SLOTS = 4
FF = 2816
FFS = 1408
NCHIP = 4
EPS = 1e-6
QSCALE = HD ** -0.5
GELU_C0 = math.sqrt(2.0 / math.pi)
GELU_C1 = 0.044715
LR, B1, B2, AEPS, WD, STEP = 0.001, 0.9, 0.999, 1e-08, 0.01, 10
VMEM_LIMIT_V7X = 56 * 1024 * 1024
SP_ROWS = 144


def _cparams(**kw):
    return pltpu.CompilerParams(vmem_limit_bytes=VMEM_LIMIT_V7X, **kw)


def _mm(a, b):
    return jnp.dot(a, b, preferred_element_type=F32)


def _mm_nt(a, b):
    return lax.dot_general(a, b, (((1,), (1,)), ((), ())), preferred_element_type=F32)


def _mm_tn(a, b):
    return lax.dot_general(a, b, (((0,), (0,)), ((), ())), preferred_element_type=F32)


def _rows8(x):
    r, c = x.shape
    return jnp.sum(x.reshape(r // 8, 8, c), axis=0)


def _mean(x):
    return jnp.mean(x, axis=-1, keepdims=True)


def _sigmoid(x):
    return 1.0 / (1.0 + jnp.exp(-x))


def _gelu(x):
    t = jnp.tanh(GELU_C0 * (x + GELU_C1 * x * x * x))
    return 0.5 * x * (1.0 + t), t


def _gelu_grad(x, t):
    return 0.5 * (1.0 + t) + 0.5 * x * (1.0 - t * t) * (GELU_C0 * (1.0 + 3.0 * GELU_C1 * x * x))


def _component_of(group):
    return jnp.where(group < 6, (group + 4) % 6, group)


def _proj_fwd(place, x, g_mix, w_in4, later):
    T = x.shape[0]
    tm = min(1024, T)
    ni = T // tm
    n = len(later)

    def body(pc_ref, x_ref, g_ref, *rest):
        proj_ref, h_ref, w_all = rest[1 + n:4 + n]
        gathered = rest[4 + n:4 + 2 * n]
        hs, wbuf, wsem, obuf, osem = rest[4 + 2 * n:9 + 2 * n]
        w_sems, later_sems = rest[9 + 2 * n:15 + 2 * n], rest[15 + 2 * n:]
        jp, i = pl.program_id(0), pl.program_id(1)
        w_cols = [w_all.at[:, :, pl.ds(k * D, D)] for k in range(2)]

        def w_copy(blk):
            cols = pl.ds(pl.multiple_of((blk % 2) * D, 128), D)
            return pltpu.make_async_copy(w_all.at[pc_ref[0] ^ (blk // 2), :, cols], wbuf.at[blk % 2],
                                         wsem.at[blk % 2])

        @pl.when((jp == 0) & (i == 0))
        def _():
            _gather_start(w_cols, w_sems)
            w_copy(jp).start()

        @pl.when(i == 0)
        def _():
            w_copy(jp).wait()

        @pl.when(jp == 0)
        def _():
            xv = x_ref[...]
            r = lax.rsqrt(_mean(xv * xv) + EPS)
            hb = (xv * r * g_ref[...]).astype(BF16)
            hs[i] = hb
            h_ref[...] = hb

        step = jp * ni + i
        slot = step % PROJ_OUT_SLOTS

        def o_copy(slot_):
            comp = 2 * (pc_ref[0] ^ (jp // 2)) + jp % 2
            return pltpu.make_async_copy(
                obuf.at[slot_], proj_ref.at[comp, pl.ds(pl.multiple_of(i * tm, 8), tm)], osem.at[slot_])

        @pl.when(step >= PROJ_OUT_SLOTS)
        def _():
            o_copy(slot).wait()

        obuf[slot] = _mm(hs[i], wbuf[jp % 2])
        o_copy(slot).start()

        @pl.when(step == NIN * ni - 1)
        def _():
            for k in range(PROJ_OUT_SLOTS):
                o_copy((slot + 1 + k) % PROJ_OUT_SLOTS).wait()

        for nxt in range(1, NIN):
            @pl.when((jp == nxt - 1) & (i == ni - 1))
            def _():
                if nxt >= 2:
                    _gather_land([w_cols[nxt % 2]], w_sems, nxt // 2, first=nxt % 2)
                if nxt == 5:
                    _gather_start(gathered, later_sems)
                if nxt == NIN - 1:
                    _gather_neighbours(gathered, later_sems)
                w_copy(jp + 1).start()

        @pl.when((jp == NIN - 1) & (i == ni - 1))
        def _():
            _gather_drain(w_cols, w_sems)
            _gather_finish(gathered, later_sems)

    tile = lambda jp, i, pc: (jnp.where(jp == 0, i, ni - 1), 0)
    res = pl.pallas_call(
        body, name="proj_fwd",
        grid_spec=pltpu.PrefetchScalarGridSpec(
            num_scalar_prefetch=1, grid=(NIN, ni),
            in_specs=[pl.BlockSpec((tm, D), tile), pl.BlockSpec((1, D), lambda jp, i, pc: (0, 0))] + [ANY] * (1 + n),
            out_specs=[ANY, pl.BlockSpec((tm, D), tile)] + [ANY] * (1 + n),
            scratch_shapes=[pltpu.VMEM((ni, tm, D), BF16), pltpu.VMEM((2, D, D), BF16),
                            pltpu.SemaphoreType.DMA((2,)), pltpu.VMEM((PROJ_OUT_SLOTS, tm, D), F32),
                            pltpu.SemaphoreType.DMA((PROJ_OUT_SLOTS,))] + _gather_sems(2) + _gather_sems(n)),
        out_shape=[SDS((NIN, T, D), F32), SDS((T, D), BF16), SDS(w_in4.shape, BF16)]
        + [SDS(a.shape, a.dtype) for a in later],
        input_output_aliases={3 + k: 2 + k for k in range(1 + n)},
        compiler_params=_cparams(has_side_effects=True),
    )(place, x, g_mix, w_in4, *later)
    return res[:2], res[2], res[3:]


def _chunks_abreast(x):
    return jnp.concatenate([x[GCH * ch:GCH * (ch + 1)] for ch in range(x.shape[0] // GCH)], axis=1)


def _chunks_stacked(x):
    return jnp.concatenate([x[:, GCH * ch:GCH * (ch + 1)] for ch in range(x.shape[1] // GCH)], axis=0)


def _layer_norm_stats(gv):
    mu = _mean(gv)
    xc = gv - mu
    rs = lax.rsqrt(_mean(xc * xc) + EPS)
    return xc * rs, rs


def _gmlp_fwd(proj, ln_g, ln_b, wm, b_t, job=None):
    T = proj.shape[1]
    tm = min(256, T)

    def body(u_ref, v_ref, lg_ref, lb_ref, wm_ref, bt_ref, a_ref, a_s):
        gu, _ = _gelu(u_ref[...])
        gv, _ = _gelu(v_ref[...])
        vhat, _ = _layer_norm_stats(gv)
        vnb = (vhat * lg_ref[...] + lb_ref[...]).astype(BF16)
        for g in range(NG):
            cols = slice(128 * g, 128 * (g + 1))
            mixed = _mm(wm_ref[g], _chunks_abreast(vnb[:, cols])) + bt_ref[:, g:g + 1]
            a_s[:, cols] = gu[:, cols] * _chunks_stacked(mixed)
        a_ref[...] = a_s[...].astype(BF16)

    row = lambda i: (0, 0)
    return _call(
        body, name="gmlp_fwd", grid=(T // tm,), job=job, args=(proj, proj, ln_g, ln_b, wm, b_t),
        in_specs=[pl.BlockSpec((None, tm, D), lambda i: (0, i, 0)), pl.BlockSpec((None, tm, D), lambda i: (1, i, 0)),
                  pl.BlockSpec((1, D), row), pl.BlockSpec((1, D), row),
                  pl.BlockSpec((NG, GCH, GCH), lambda i: (0, 0, 0)), pl.BlockSpec((GCH, NG), row)],
        out_specs=[pl.BlockSpec((tm, D), lambda i: (i, 0))],
        out_shape=[SDS((T, D), BF16)],
        scratch_shapes=[pltpu.VMEM((tm, D), F32)])


def _cumsum64(x, row):
    for s in (1, 2, 4, 8, 16, 32):
        x = x + jnp.where(row >= s, pltpu.roll(x, s, 0), 0.0)
    return x


def _revcumsum64(x, row):
    n = x.shape[0]
    for s in (1, 2, 4, 8, 16, 32):
        x = x + jnp.where(row < HCH - s, pltpu.roll(x, n - s, 0), 0.0)
    return x


def _head_mean(x):
    parts = [jnp.broadcast_to(_mean(x[:, HD * h:HD * (h + 1)]), (x.shape[0], HD)) for h in range(x.shape[1] // HD)]
    return jnp.concatenate(parts, axis=1)


def _seg_sum(x):
    n, c = x.shape
    s = jnp.sum(x.reshape(n // HCH, HCH, c), axis=1, keepdims=True)
    return jnp.broadcast_to(s, (n // HCH, HCH, c)).reshape(n, c)


def _hgrn_gates(fl, lbv, row):
    s = _sigmoid(fl)
    f = lbv + (1.0 - lbv) * s
    a = _cumsum64(jnp.log(f), row)
    a_mid = _seg_sum(jnp.where(row == HCH // 2 - 1, a, 0.0))
    a_last = _seg_sum(jnp.where(row == HCH - 1, a, 0.0))
    return s, f, a, a_mid, a_last


def _hgrn_fwd(proj, lb_table, norm_g, job=None):
    T = proj.shape[1]
    tb = min(HGRN_TOKENS, T)
    nc = tb // HCH

    def body(q_ref, fl_ref, v_ref, g_ref, lbt_ref, gn_ref, o_ref, ob_ref, stb_ref, st_s, o_s):
        @pl.when(pl.program_id(1) == 0)
        def _():
            st_s[...] = jnp.zeros_like(st_s)

        row = lax.broadcasted_iota(jnp.int32, (tb, HW), 0) & (HCH - 1)
        lbv = _sigmoid(lbt_ref[0:1, :] - lbt_ref[1:2, :])
        _, f, a, a_mid, a_last = _hgrn_gates(fl_ref[...], lbv, row)
        k = 1.0 - f
        qs = q_ref[...] * QSCALE
        q_in = (qs * jnp.exp(a - a_mid)).astype(BF16)
        k_in = (k * jnp.exp(a_mid - a)).astype(BF16)
        q_a = (qs * jnp.exp(a)).astype(BF16)
        k_d = (k * jnp.exp(a_last - a)).astype(BF16)
        dec = jnp.exp(a_last)
        vb = v_ref[...].astype(BF16)
        tri = (lax.broadcasted_iota(jnp.int32, (HCH, HCH), 0)
               >= lax.broadcasted_iota(jnp.int32, (HCH, HCH), 1))
        for c in range(nc):
            sl = slice(HCH * c, HCH * (c + 1))
            for hh in range(HGRN_HB):
                hs = slice(HD * hh, HD * (hh + 1))
                st = st_s[hh]
                stb_ref[hh, c] = st
                sc = jnp.where(tri, _mm_nt(q_in[sl, hs], k_in[sl, hs]), 0.0)
                o_s[sl, hs] = _mm(sc.astype(BF16), vb[sl, hs]) + _mm_nt(q_a[sl, hs], st.astype(BF16))
                d64 = dec[sl, hs]
                st_s[hh] = st * jnp.concatenate([d64, d64], axis=0) + _mm_tn(vb[sl, hs], k_d[sl, hs])
        o = o_s[...]
        r = lax.rsqrt(_head_mean(o * o) + EPS)
        g = g_ref[...]
        o_ref[...] = o
        ob_ref[...] = (o * r * gn_ref[...] * (g * _sigmoid(g))).astype(BF16)

    def col(off):
        return pl.BlockSpec((None, tb, HW), lambda h, cb: (off, cb, h))

    return _call(
        body, name="hgrn_fwd", grid=(NH // HGRN_HB, T // tb), job=job,
        args=(proj, proj, proj, proj, lb_table, norm_g),
        in_specs=[col(2), col(3), col(4), col(5),
                  pl.BlockSpec((2, HW), lambda h, cb: (0, h)), pl.BlockSpec((1, HW), lambda h, cb: (0, h))],
        out_specs=[pl.BlockSpec((tb, HW), lambda h, cb: (cb, h)), pl.BlockSpec((tb, HW), lambda h, cb: (cb, h)),
                   pl.BlockSpec((HGRN_HB, nc, HD, HD), lambda h, cb: (h, cb, 0, 0))],
        out_shape=[SDS((T, D), F32), SDS((T, D), BF16), SDS((NH, T // HCH, HD, HD), F32)],
        scratch_shapes=[pltpu.VMEM((HGRN_HB, HD, HD), F32), pltpu.VMEM((tb, HW), F32)])


def _merge_fwd(x, ab, ob, proj, w_a, w_b, w_out, job=None):
    T = x.shape[0]
    tm = min(512, T)

    def body(x_ref, ab_ref, ob_ref, ga_ref, gb_ref, wa_ref, wb_ref, wo_ref, mg_ref, x1_ref):
        ya = _mm(ab_ref[...], wa_ref[...])
        yb = _mm(ob_ref[...], wb_ref[...])
        merged = (_sigmoid(ga_ref[...]) * ya + _sigmoid(gb_ref[...]) * yb).astype(BF16)
        mg_ref[...] = merged
        x1_ref[...] = x_ref[...] + _mm(merged, wo_ref[...])

    t = lambda i: (i, 0)
    w = lambda i: (0, 0)
    return _call(
        body, name="merge_fwd", grid=(T // tm,), job=job, args=(x, ab, ob, proj, proj, w_a, w_b, w_out),
        in_specs=[pl.BlockSpec((tm, D), t), pl.BlockSpec((tm, D), t), pl.BlockSpec((tm, D), t),
                  pl.BlockSpec((None, tm, D), lambda i: (6, i, 0)), pl.BlockSpec((None, tm, D), lambda i: (7, i, 0)),
                  pl.BlockSpec((D, D), w), pl.BlockSpec((D, D), w), pl.BlockSpec((D, D), w)],
        out_specs=[pl.BlockSpec((tm, D), t)] * 2,
        out_shape=[SDS((T, D), BF16), SDS((T, D), F32)])


def _ffn_fwd_bwd(x1, target, g_ffn, g_fin, w_gu, w_down):
    T = x1.shape[0]
    tm = min(256, T)
    inv_d = 1.0 / D

    def body(x1_ref, tg_ref, gf_ref, gn_ref, wgu_ref, wd_ref,
             act_ref, dx2b_ref, h2b_ref, dgu_ref, dx1_ref, dx1b_ref, acc_ref):
        @pl.when(pl.program_id(0) == 0)
        def _():
            acc_ref[...] = jnp.zeros_like(acc_ref)

        x1v = x1_ref[...]
        gf = gf_ref[...]
        gn = gn_ref[...]
        rr1 = lax.rsqrt(_mean(x1v * x1v) + EPS)
        x1n = x1v * rr1
        h2b = (x1n * gf).astype(BF16)
        h2b_ref[...] = h2b
        gate = _mm(h2b, wgu_ref[0])
        up = _mm(h2b, wgu_ref[1])
        sg = _sigmoid(gate)
        si = gate * sg
        act = (si * up).astype(BF16)
        act_ref[...] = act
        x2 = x1v + _mm(act, wd_ref[...])
        rr2 = lax.rsqrt(_mean(x2 * x2) + EPS)
        x2n = x2 * rr2
        e = x2n * gn - tg_ref[...]
        acc_ref[0] += _rows8(e * e) * (0.5 * inv_d)
        dy = e * inv_d
        acc_ref[1] += _rows8(dy * x2n)
        dxn = dy * gn
        dx2 = rr2 * (dxn - x2n * _mean(dxn * x2n))
        dx2b = dx2.astype(BF16)
        dx2b_ref[...] = dx2b
        dact = _mm_nt(dx2b, wd_ref[...])
        dgate = (dact * up * (sg * (1.0 + gate * (1.0 - sg)))).astype(BF16)
        dup = (dact * si).astype(BF16)
        dgu_ref[0] = dgate
        dgu_ref[1] = dup
        dh2 = _mm_nt(dgate, wgu_ref[0]) + _mm_nt(dup, wgu_ref[1])
        acc_ref[2] += _rows8(dh2 * x1n)
        dxn1 = dh2 * gf
        dx1 = dx2 + rr1 * (dxn1 - x1n * _mean(dxn1 * x1n))
        dx1_ref[...] = dx1
        dx1b_ref[...] = dx1.astype(BF16)

    t = lambda i: (i, 0)
    w = lambda i: (0, 0)
    one = pl.Buffered(1)
    return pl.pallas_call(
        body, name="ffn_fwd_bwd", grid=(T // tm,),
        in_specs=[pl.BlockSpec((tm, D), t), pl.BlockSpec((tm, D), t),
                  pl.BlockSpec((1, D), w), pl.BlockSpec((1, D), w),
                  pl.BlockSpec((2, D, FF), lambda i: (0, 0, 0), pipeline_mode=one),
                  pl.BlockSpec((FF, D), w, pipeline_mode=one)],
        out_specs=[pl.BlockSpec((tm, FF), t), pl.BlockSpec((tm, D), t), pl.BlockSpec((tm, D), t),
                   pl.BlockSpec((2, tm, FF), lambda i: (0, i, 0)),
                   pl.BlockSpec((tm, D), t), pl.BlockSpec((tm, D), t),
                   pl.BlockSpec((3, 8, D), lambda i: (0, 0, 0))],
        out_shape=[SDS((T, FF), BF16), SDS((T, D), BF16), SDS((T, D), BF16),
                   SDS((2, T, FF), BF16), SDS((T, D), F32), SDS((T, D), BF16),
                   SDS((3, 8, D), F32)],
        compiler_params=_cparams(),
    )(x1, target, g_ffn, g_fin, w_gu, w_down)


def _merge_bwd(dx1b, ab, ob, proj, w_out, w_a, w_b, job=None):
    T = dx1b.shape[0]
    tm = min(512, T)

    def body(dx_ref, ab_ref, ob_ref, ga_ref, gb_ref, wo_ref, wa_ref, wb_ref, dya_ref, dyb_ref, dp_ref):
        dm = _mm_nt(dx_ref[...], wo_ref[...])
        sa = _sigmoid(ga_ref[...])
        sb = _sigmoid(gb_ref[...])
        dya_ref[...] = (dm * sa).astype(BF16)
        dyb_ref[...] = (dm * sb).astype(BF16)
        dp_ref[0] = (dm * _mm(ab_ref[...], wa_ref[...]) * sa * (1.0 - sa)).astype(BF16)
        dp_ref[1] = (dm * _mm(ob_ref[...], wb_ref[...]) * sb * (1.0 - sb)).astype(BF16)

    t = lambda i: (i, 0)
    w = lambda i: (0, 0)
    return _call(
        body, name="merge_bwd", grid=(T // tm,),
        in_specs=[pl.BlockSpec((tm, D), t), pl.BlockSpec((tm, D), t), pl.BlockSpec((tm, D), t),
                  pl.BlockSpec((None, tm, D), lambda i: (6, i, 0)), pl.BlockSpec((None, tm, D), lambda i: (7, i, 0)),
                  pl.BlockSpec((D, D), w), pl.BlockSpec((D, D), w), pl.BlockSpec((D, D), w)],
        out_specs=[pl.BlockSpec((tm, D), t)] * 2 + [pl.BlockSpec((2, tm, D), lambda i: (3, i, 0))],
        out_shape=[SDS((T, D), BF16), SDS((T, D), BF16), SDS((NIN, T, D), BF16)],
        args=(dx1b, ab, ob, proj, proj, w_out, w_a, w_b), job=job)


def _hgrn_bwd(dproj, dyb, w_b, o_raw, proj, st_before, lb_table, norm_g, job=None):
    T = dyb.shape[0]
    tb = min(HGRN_TOKENS, T)
    nc = tb // HCH
    nb = T // tb

    def body(dp_in, dyb_ref, wb_ref, o_ref, q_ref, fl_ref, v_ref, g_ref, stb_ref, lbt_ref, gn_ref,
             dp_ref, acc_ref, dst_s, dqin_s, dqa_s, dkin_s, dkd_s, dv_s, ddec_s):
        del dp_in

        @pl.when(pl.program_id(1) == 0)
        def _():
            dst_s[...] = jnp.zeros_like(dst_s)
            acc_ref[...] = jnp.zeros_like(acc_ref)

        row = lax.broadcasted_iota(jnp.int32, (tb, HW), 0) & (HCH - 1)
        gn = gn_ref[...]
        lbv = _sigmoid(lbt_ref[0:1, :] - lbt_ref[1:2, :])
        o = o_ref[...]
        r = lax.rsqrt(_head_mean(o * o) + EPS)
        on = o * r
        g = g_ref[...]
        sgm = _sigmoid(g)
        dob_v = _mm_nt(dyb_ref[...], wb_ref[...])
        dp_ref[3] = (dob_v * on * gn * (sgm * (1.0 + g * (1.0 - sgm)))).astype(BF16)
        do_n = dob_v * (g * sgm)
        acc_ref[1] += _rows8(do_n * on)
        dxn = do_n * gn
        do = (r * (dxn - on * _head_mean(dxn * on))).astype(BF16)
        s, f, a, a_mid, a_last = _hgrn_gates(fl_ref[...], lbv, row)
        k = 1.0 - f
        qs = q_ref[...] * QSCALE
        e_q = jnp.exp(a - a_mid)
        e_k = jnp.exp(a_mid - a)
        e_a = jnp.exp(a)
        e_l = jnp.exp(a_last - a)
        dec = jnp.exp(a_last)
        q_in = qs * e_q
        k_in = k * e_k
        q_a = qs * e_a
        k_d = k * e_l
        q_inb, k_inb, q_ab, k_db = (z.astype(BF16) for z in (q_in, k_in, q_a, k_d))
        vb = v_ref[...].astype(BF16)
        tri = (lax.broadcasted_iota(jnp.int32, (HCH, HCH), 0)
               >= lax.broadcasted_iota(jnp.int32, (HCH, HCH), 1))
        for c in reversed(range(nc)):
            sl = slice(HCH * c, HCH * (c + 1))
            for hh in range(HGRN_HB):
                hs = slice(HD * hh, HD * (hh + 1))
                stp = stb_ref[hh, c]
                dst = dst_s[hh]
                dstb = dst.astype(BF16)
                do_c = do[sl, hs]
                v_c = vb[sl, hs]
                dqa_s[sl, hs] = _mm(do_c, stp.astype(BF16))
                dkd_s[sl, hs] = _mm(v_c, dstb)
                ddec_s[sl, hs] = jnp.broadcast_to(jnp.sum(dst * stp, axis=0, keepdims=True), (HCH, HD))
                sc = jnp.where(tri, _mm_nt(q_inb[sl, hs], k_inb[sl, hs]), 0.0).astype(BF16)
                dsc = jnp.where(tri, _mm_nt(do_c, v_c), 0.0).astype(BF16)
                dv_s[sl, hs] = _mm_nt(k_db[sl, hs], dstb) + _mm_tn(sc, do_c)
                dqin_s[sl, hs] = _mm(dsc, k_inb[sl, hs])
                dkin_s[sl, hs] = _mm_tn(dsc, q_inb[sl, hs])
                d64 = dec[sl, hs]
                dst_s[hh] = dst * jnp.concatenate([d64, d64], axis=0) + _mm_tn(do_c, q_ab[sl, hs])
        dq_in = dqin_s[...]
        dq_a = dqa_s[...]
        dk_in = dkin_s[...]
        dk_d = dkd_s[...]
        dp_ref[0] = ((dq_in * e_q + dq_a * e_a) * QSCALE).astype(BF16)
        dp_ref[2] = dv_s[...].astype(BF16)
        tq = dq_in * q_in
        tk = dk_in * k_in
        td = dk_d * k_d
        d_a = tq + dq_a * q_a - tk - td
        d_a = d_a + jnp.where(row == HCH // 2 - 1, _seg_sum(tk - tq), 0.0)
        d_a = d_a + jnp.where(row == HCH - 1, _seg_sum(td) + ddec_s[...] * dec, 0.0)
        dlf = _revcumsum64(d_a, row)
        df = dlf / f - (dk_in * e_k + dk_d * e_l)
        dp_ref[1] = (df * (1.0 - lbv) * s * (1.0 - s)).astype(BF16)
        acc_ref[0] += _rows8(df * (1.0 - s))

    def col(off):
        return pl.BlockSpec((None, tb, HW), lambda h, cb: (off, nb - 1 - cb, h))

    hb = lambda h, cb: (nb - 1 - cb, h)
    return _call(
        body, name="hgrn_bwd", grid=(NH // HGRN_HB, nb), job=job,
        args=(dproj, dyb, w_b, o_raw, proj, proj, proj, proj, st_before, lb_table, norm_g),
        in_specs=[ANY, pl.BlockSpec((tb, D), lambda h, cb: (nb - 1 - cb, 0)),
                  pl.BlockSpec((HW, D), lambda h, cb: (h, 0)), pl.BlockSpec((tb, HW), hb),
                  col(2), col(3), col(4), col(5),
                  pl.BlockSpec((HGRN_HB, nc, HD, HD), lambda h, cb: (h, nb - 1 - cb, 0, 0)),
                  pl.BlockSpec((2, HW), lambda h, cb: (0, h)), pl.BlockSpec((1, HW), lambda h, cb: (0, h))],
        out_specs=[pl.BlockSpec((4, tb, HW), lambda h, cb: (0, nb - 1 - cb, h)),
                   pl.BlockSpec((2, 8, HW), lambda h, cb: (0, 0, h))],
        out_shape=[SDS(dproj.shape, BF16), SDS((2, 8, D), F32)],
        scratch_shapes=[pltpu.VMEM((HGRN_HB, HD, HD), F32)] + [pltpu.VMEM((tb, HW), F32)] * 6,
        aliases={0: 0})


def _gmlp_bwd(dproj, dya, w_a, proj, ln_g, ln_b, wm, wm_t, b_t):
    T = dya.shape[0]
    tm = min(256, T)

    def body(dp_in, dya_ref, wa_ref, u_ref, v_ref, lg_ref, lb_ref, wm_ref, wmt_ref, bt_ref,
             dp_ref, acc_ref, dws_ref, dmix_ref, du_s, dvn_s):
        del dp_in

        @pl.when(pl.program_id(0) == 0)
        def _():
            acc_ref[...] = jnp.zeros_like(acc_ref)
            dws_ref[...] = jnp.zeros_like(dws_ref)
            dmix_ref[...] = jnp.zeros_like(dmix_ref)

        u = u_ref[...]
        v = v_ref[...]
        lg = lg_ref[...]
        gu, t_u = _gelu(u)
        gv, t_v = _gelu(v)
        vhat, rs = _layer_norm_stats(gv)
        vnb = (vhat * lg + lb_ref[...]).astype(BF16)
        da_v = _mm_nt(dya_ref[...], wa_ref[...])
        for g in range(NG):
            cols = slice(128 * g, 128 * (g + 1))
            vng = _chunks_abreast(vnb[:, cols])
            mixed = _mm(wm_ref[g], vng) + bt_ref[:, g:g + 1]
            dag = _chunks_abreast(da_v[:, cols])
            dmx = dag * _chunks_abreast(gu[:, cols])
            du_s[:, cols] = _chunks_stacked(dag * mixed)
            dmxb = dmx.astype(BF16)
            dws_ref[:, cols] += _mm_nt(dmxb, vng)
            dmix_ref[:, cols] += sum(dmx[:, GCH * ch:GCH * (ch + 1)] for ch in range(tm // GCH))
            dvn_s[:, cols] = _chunks_stacked(_mm(wmt_ref[g], dmxb))
        dp_ref[0] = (du_s[...] * _gelu_grad(u, t_u)).astype(BF16)
        dvn = dvn_s[...]
        acc_ref[0] += _rows8(dvn * vhat)
        acc_ref[1] += _rows8(dvn)
        dvh = dvn * lg
        dgv = rs * (dvh - _mean(dvh) - vhat * _mean(dvh * vhat))
        dp_ref[1] = (dgv * _gelu_grad(v, t_v)).astype(BF16)

    row = lambda i: (0, 0)
    w3 = lambda i: (0, 0, 0)
    return pl.pallas_call(
        body, name="gmlp_bwd", grid=(T // tm,),
        in_specs=[ANY, pl.BlockSpec((tm, D), lambda i: (i, 0)), pl.BlockSpec((D, D), row),
                  pl.BlockSpec((None, tm, D), lambda i: (0, i, 0)), pl.BlockSpec((None, tm, D), lambda i: (1, i, 0)),
                  pl.BlockSpec((1, D), row), pl.BlockSpec((1, D), row),
                  pl.BlockSpec((NG, GCH, GCH), w3), pl.BlockSpec((NG, GCH, GCH), w3),
                  pl.BlockSpec((GCH, NG), row)],
        out_specs=[pl.BlockSpec((2, tm, D), lambda i: (2, i, 0)),
                   pl.BlockSpec((2, 8, D), w3), pl.BlockSpec((GCH, D), row), pl.BlockSpec((GCH, D), row)],
        out_shape=[SDS(dproj.shape, BF16), SDS((2, 8, D), F32), SDS((GCH, D), F32), SDS((GCH, D), F32)],
        scratch_shapes=[pltpu.VMEM((tm, D), F32), pltpu.VMEM((tm, D), F32)],
        input_output_aliases={0: 0},
        compiler_params=_cparams(),
    )(dproj, dya, w_a, proj, proj, ln_g, ln_b, wm, wm_t, b_t)


def _proj_bwd(dproj, w_in4, x, dx1, g_mix, job=None):
    T = x.shape[0]
    tm = min(256, T)
    order = (2, 3, 4, 5, 0, 1, 6, 7)

    def body(dp_ref, w_ref, x_ref, dx1_ref, g_ref, gx_ref, acc_ref):
        @pl.when(pl.program_id(0) == 0)
        def _():
            acc_ref[...] = jnp.zeros_like(acc_ref)

        dh = None
        for m, og in enumerate(order):
            part = _mm_nt(dp_ref[m], w_ref[og // 2, :, D * (og % 2):D * (og % 2 + 1)])
            dh = part if dh is None else dh + part
        xv = x_ref[...]
        r = lax.rsqrt(_mean(xv * xv) + EPS)
        xn = xv * r
        acc_ref[...] += _rows8(dh * xn)
        dxn = dh * g_ref[...]
        gx_ref[...] = dx1_ref[...] + r * (dxn - xn * _mean(dxn * xn))

    t = lambda i: (i, 0)
    return _call(
        body, name="proj_bwd", grid=(T // tm,),
        in_specs=[pl.BlockSpec((NIN, tm, D), lambda i: (0, i, 0)),
                  pl.BlockSpec((NCHIP, D, 2 * D), lambda i: (0, 0, 0), pipeline_mode=pl.Buffered(1)),
                  pl.BlockSpec((tm, D), t), pl.BlockSpec((tm, D), t), pl.BlockSpec((1, D), lambda i: (0, 0))],
        out_specs=[pl.BlockSpec((tm, D), t), pl.BlockSpec((8, D), lambda i: (0, 0))],
        out_shape=[SDS((T, D), F32), SDS((8, D), F32)],
        args=(dproj, w_in4, x, dx1, g_mix), job=job)


def _dw_call(name, a, b, a_spec, b_spec, o_spec, out_shape, nblk, tt, job=None, prefetch=None):
    T = a.shape[-2]

    def body(*refs):
        a_ref, b_ref, o_ref = refs[-3:]

        @pl.when(pl.program_id(1) == 0)
        def _():
            o_ref[...] = jnp.zeros_like(o_ref)
        o_ref[...] += _mm_tn(a_ref[...], b_ref[...])

    (out,), job_out = _call(
        body, name=name, grid=(nblk, T // tt), in_specs=[a_spec, b_spec], out_specs=[o_spec],
        out_shape=[out_shape], args=(a, b), job=job, prefetch=prefetch)
    return out, job_out


def _dw_in_half(name, place, hb, dproj, mine, job=None):
    tt = min(DW_TOKENS, hb.shape[0])

    def comp(k, pc):
        return _component_of(2 * k + (pc[1] if mine else 1 - pc[1]))

    return _dw_call(
        name, hb, dproj,
        pl.BlockSpec((tt, D), lambda k, t, pc: (t, 0)),
        pl.BlockSpec((None, tt, D), lambda k, t, pc: (comp(k, pc), t, 0)),
        pl.BlockSpec((None, D, D), lambda k, t, pc: (k, 0, 0)),
        SDS((NCHIP, D, D), F32), NCHIP, tt, job, place)


def _dw_gate_up(h2b, dgu, job=None):
    tt = min(DW_TOKENS, h2b.shape[0])
    return _dw_call(
        "dw_gate_up", h2b, dgu,
        pl.BlockSpec((tt, D), lambda k, t: (t, 0)),
        pl.BlockSpec((None, tt, FFS), lambda k, t: (k // 2, t, k % 2)),
        pl.BlockSpec((None, D, FFS), lambda k, t: (k, 0, 0)),
        SDS((NCHIP, D, FFS), F32), NCHIP, tt, job)


def _dw_down(act, dx2b, job=None):
    tt = min(DW_TOKENS, act.shape[0])
    g, job_out = _dw_call(
        "dw_down", act, dx2b,
        pl.BlockSpec((tt, FFS), lambda k, t: (t, k)),
        pl.BlockSpec((tt, D), lambda k, t: (t, 0)),
        pl.BlockSpec((FFS, D), lambda k, t: (k, 0)),
        SDS((FF, D), F32), 2, tt, job)
    return g.reshape(NCHIP, FF // NCHIP, D), job_out


def _dw_square(name, a, b, job=None):
    tt = min(DW_TOKENS, a.shape[0])
    g, job_out = _dw_call(
        name, a, b,
        pl.BlockSpec((tt, D), lambda k, t: (t, 0)), pl.BlockSpec((tt, D), lambda k, t: (t, 0)),
        pl.BlockSpec((D, D), lambda k, t: (0, 0)), SDS((D, D), F32), 1, tt, job)
    return g.reshape(NCHIP, D // NCHIP, D), job_out


def _place():
    x, y, c = lax.axis_index("x"), lax.axis_index("y"), lax.axis_index("c")
    return x, y, c, 2 * x + y


def _chip_at(x, y, s):
    return x ^ (s >> 1), y ^ (s & 1)


class _Job:
    def __init__(self, ins, out_shapes, sems, start, finish, aliases=None, mid=None):
        self.ins, self.out_shapes, self.sems = list(ins), list(out_shapes), list(sems)
        self.start, self.finish, self.aliases = start, finish, dict(aliases or {})
        self.mid = mid if mid is not None else (lambda ins, outs, sems: None)


def _join_jobs(*jobs):
    def cut(refs, sizes):
        out, at = [], 0
        for n in sizes:
            out.append(refs[at:at + n])
            at += n
        return out

    ni = [len(j.ins) for j in jobs]
    no = [len(j.out_shapes) for j in jobs]
    ns = [len(j.sems) for j in jobs]

    def run(which):
        def go(ins, outs, sems):
            for j, a, b, c in zip(jobs, cut(ins, ni), cut(outs, no), cut(sems, ns)):
                getattr(j, which)(a, b, c)
        return go

    aliases = {}
    for k, j in enumerate(jobs):
        for a, b in j.aliases.items():
            aliases[sum(ni[:k]) + a] = sum(no[:k]) + b
    return _Job([a for j in jobs for a in j.ins], [o for j in jobs for o in j.out_shapes],
                [s for j in jobs for s in j.sems], run("start"), run("finish"), aliases, run("mid"))


def _call(body, *, name, grid, in_specs, out_specs, out_shape, args, scratch_shapes=(), aliases=None,
          job=None, prefetch=None):
    n_in, n_out, n_scr = len(in_specs), len(out_specs), len(scratch_shapes)
    npf = 0 if prefetch is None else 1
    job = job if job is not None else _Job([], [], [], lambda *a: None, lambda *a: None)
    ji, jo = len(job.ins), len(job.out_shapes)
    steps = math.prod(grid)

    def wrapped(*refs):
        pf, refs = refs[:npf], refs[npf:]
        ins, jin = refs[:n_in], refs[n_in:n_in + ji]
        o0 = n_in + ji
        outs, jout = refs[o0:o0 + n_out], refs[o0 + n_out:o0 + n_out + jo]
        s0 = o0 + n_out + jo
        scr, jsem = refs[s0:s0 + n_scr], refs[s0 + n_scr:]
        step = functools.reduce(lambda acc, ag: acc * ag[1] + pl.program_id(ag[0]), enumerate(grid), 0)
        if ji or jo:
            @pl.when(step == 0)
            def _():
                job.start(jin, jout, jsem)

        body(*pf, *ins, *outs, *scr)

        if ji or jo:
            @pl.when(step == steps // 2)
            def _():
                job.mid(jin, jout, jsem)

            @pl.when(step == steps - 1)
            def _():
                job.finish(jin, jout, jsem)

    io = {npf + a: b for a, b in dict(aliases or {}).items()}
    io.update({npf + n_in + a: n_out + b for a, b in job.aliases.items()})
    kw = dict(in_specs=list(in_specs) + [ANY] * ji, out_specs=list(out_specs) + [ANY] * jo,
              scratch_shapes=list(scratch_shapes) + job.sems)
    if npf:
        kw = dict(grid_spec=pltpu.PrefetchScalarGridSpec(num_scalar_prefetch=1, grid=grid, **kw))
    else:
        kw["grid"] = grid
    res = pl.pallas_call(
        wrapped, name=name, out_shape=list(out_shape) + job.out_shapes, input_output_aliases=io,
        compiler_params=_cparams(has_side_effects=bool(ji or jo)), **kw,
    )(*(() if prefetch is None else (prefetch,)), *args, *job.ins)
    return list(res[:n_out]), list(res[n_out:])


def _cast_shards(name, place, ws, paired=False):
    n = len(ws)
    rows, cols = ws[0].shape
    tr = 352 if rows % 352 == 0 else 256
    shape = (2, rows, 2 * cols) if paired else (NCHIP, rows, cols)
    mine = (lambda i, pc: (pc[0] // 2, i, pc[0] % 2)) if paired else (lambda i, pc: (pc[0], i, 0))

    def body(pc_ref, *refs):
        del pc_ref
        for w_ref, o_ref in zip(refs[:n], refs[n:]):
            o_ref[...] = w_ref[...].astype(BF16)

    return pl.pallas_call(
        body, name=name,
        grid_spec=pltpu.PrefetchScalarGridSpec(
            num_scalar_prefetch=1, grid=(rows // tr,),
            in_specs=[pl.BlockSpec((tr, cols), lambda i, pc: (i, 0))] * n,
            out_specs=[pl.BlockSpec((None, tr, cols), mine)] * n),
        out_shape=[SDS(shape, BF16)] * n,
        compiler_params=_cparams(),
    )(place, *ws)


def _sibling_copy(ref, send_sem, recv_sem):
    x, y, c, _ = _place()
    return pltpu.make_async_remote_copy(src_ref=ref, dst_ref=ref, send_sem=send_sem, recv_sem=recv_sem,
                                        device_id=(x, y, 1 - c), device_id_type=MESH)


def _slot(arr, chip):
    if arr.shape[0] == NCHIP:
        return arr.at[chip]
    cols = arr.shape[2] // 2
    return arr.at[chip // 2, :, pl.ds(pl.multiple_of((chip % 2) * cols, 128), cols)]


def _half_rows(arr, slot, core):
    half = arr.shape[1] // 2
    return _slot(arr, slot).at[pl.ds(pl.multiple_of(core * half, 16), half)]


def _quarter_rows(arr, slot, core, q):
    quarter = arr.shape[1] // 4
    return _slot(arr, slot).at[pl.ds(pl.multiple_of((2 * core + q) * quarter, 16), quarter)]


def _chip_copy(ref, dist, send_sem, recv_sem):
    x, y, c, _ = _place()
    cx, cy = _chip_at(x, y, dist)
    return pltpu.make_async_remote_copy(src_ref=ref, dst_ref=ref, send_sem=send_sem, recv_sem=recv_sem,
                                        device_id=(cx, cy, c), device_id_type=MESH)


def _gather_sems(n):
    dma = pltpu.SemaphoreType.DMA
    return [dma((n, 2))] * 4 + [dma((n, 4))] * 2


def _gather_start(arrs, sems):
    dsend, drecv = sems[0], sems[1]
    _, _, c, j = _place()
    for w, arr in enumerate(arrs):
        for dist in (1, 2):
            _chip_copy(_half_rows(arr, j, c), dist, dsend.at[w, dist - 1], drecv.at[w, dist - 1]).start()


def _gather_land(arrs, sems, dist, first=0):
    dsend, drecv, rsend, rrecv, fsend, frecv = sems
    _, _, c, j = _place()
    if dist < 3:
        other = 3 - dist
        for w, arr in enumerate(arrs, first):
            landed = _half_rows(arr, j ^ dist, c)
            _chip_copy(landed, dist, dsend.at[w, dist - 1], drecv.at[w, dist - 1]).wait_recv()
            relay = _quarter_rows(arr, j ^ dist, c, other - 1)
            _chip_copy(relay, other, rsend.at[w, other - 1], rrecv.at[w, other - 1]).start()
            _sibling_copy(landed, fsend.at[w, dist - 1], frecv.at[w, dist - 1]).start()
        for w, arr in enumerate(arrs, first):
            theirs = _half_rows(arr, j ^ dist, 1 - c)
            _sibling_copy(theirs, fsend.at[w, dist - 1], frecv.at[w, dist - 1]).wait_recv()
    else:
        for w, arr in enumerate(arrs, first):
            for via in (1, 2):
                piece = _quarter_rows(arr, j ^ 3, c, via - 1)
                _chip_copy(piece, via, rsend.at[w, via - 1], rrecv.at[w, via - 1]).wait_recv()
                _sibling_copy(piece, fsend.at[w, 1 + via], frecv.at[w, 1 + via]).start()
        for w, arr in enumerate(arrs, first):
            for via in (1, 2):
                theirs = _quarter_rows(arr, j ^ 3, 1 - c, via - 1)
                _sibling_copy(theirs, fsend.at[w, 1 + via], frecv.at[w, 1 + via]).wait_recv()


def _gather_drain(arrs, sems):
    dsend, drecv, rsend, rrecv, fsend, frecv = sems
    _, _, c, j = _place()
    for w, arr in enumerate(arrs):
        for dist in (1, 2):
            other = 3 - dist
            _chip_copy(_half_rows(arr, j, c), dist, dsend.at[w, dist - 1], drecv.at[w, dist - 1]).wait_send()
            _chip_copy(_quarter_rows(arr, j ^ dist, c, other - 1), other,
                       rsend.at[w, other - 1], rrecv.at[w, other - 1]).wait_send()
            _sibling_copy(_half_rows(arr, j ^ dist, c), fsend.at[w, dist - 1], frecv.at[w, dist - 1]).wait_send()
            _sibling_copy(_quarter_rows(arr, j ^ 3, c, dist - 1),
                          fsend.at[w, 1 + dist], frecv.at[w, 1 + dist]).wait_send()


def _gather_neighbours(arrs, sems):
    _gather_land(arrs, sems, 1)
    _gather_land(arrs, sems, 2)


def _gather_finish(arrs, sems):
    _gather_land(arrs, sems, 3)
    _gather_drain(arrs, sems)


def _gather_job(arrs):
    n = len(arrs)
    return _Job(arrs, [SDS(a.shape, a.dtype) for a in arrs], _gather_sems(n),
                lambda ins, outs, sems: _gather_start(outs, sems),
                lambda ins, outs, sems: _gather_finish(outs, sems), {k: k for k in range(n)},
                mid=lambda ins, outs, sems: _gather_neighbours(outs, sems))


def _exchange_job(arrs, out_shapes, n, copies):
    def start(ins, outs, sems):
        for cp in copies(ins, outs, sems[0], sems[1]):
            cp.start()

    def finish(ins, outs, sems):
        for cp in copies(ins, outs, sems[0], sems[1]):
            cp.wait()

    return _Job(arrs, out_shapes, [pltpu.SemaphoreType.DMA((n,))] * 2, start, finish)


def _pair_exchange_job(grads):
    def copies(ins, outs, send_sem, recv_sem):
        x, y, c, _ = _place()
        res = []
        for w in range(len(grads)):
            half = ins[w].shape[1] // 2
            theirs = pl.ds(pl.multiple_of((1 - c) * half, 8), half)
            res.append(pltpu.make_async_remote_copy(
                src_ref=ins[w].at[:, theirs, :], dst_ref=outs[w], send_sem=send_sem.at[w],
                recv_sem=recv_sem.at[w], device_id=(x, y, 1 - c), device_id_type=MESH))
        return res

    return _exchange_job(grads, [SDS((NCHIP, g.shape[1] // 2, g.shape[2]), F32) for g in grads],
                         len(grads), copies)


def _row_tile(rows, cols):
    tr = rows
    while tr * cols * 4 > ELEMENTWISE_BLOCK_BYTES and tr % 32 == 0:
        tr //= 2
    return tr


def _pair_sums(name, place, gs, sibs):
    n = len(gs)
    half, cols = sibs[0].shape[1], sibs[0].shape[2]
    tr = _row_tile(half, cols)
    nt = half // tr
    mine = nt if gs[0].shape[1] == 2 * half else 0

    def body(pc_ref, *refs):
        del pc_ref
        for g_ref, s_ref, own_ref, out_ref in zip(refs[:n], refs[n:2 * n], refs[2 * n:3 * n], refs[3 * n:]):
            v = g_ref[...] + s_ref[...]
            out_ref[...] = v.astype(BF16)

            @pl.when(pl.program_id(1) == 0)
            def _():
                own_ref[...] = v

    res = pl.pallas_call(
        body, name=name,
        grid_spec=pltpu.PrefetchScalarGridSpec(
            num_scalar_prefetch=1, grid=(nt, NCHIP),
            in_specs=[pl.BlockSpec((None, tr, cols), lambda i, s, pc: (pc[0] ^ s, pc[1] * mine + i, 0))] * n
            + [pl.BlockSpec((None, tr, cols), lambda i, s, pc: (pc[0] ^ s, i, 0))] * n,
            out_specs=[pl.BlockSpec((tr, cols), lambda i, s, pc: (i, 0))] * n
            + [pl.BlockSpec((None, tr, cols), lambda i, s, pc: (s, i, 0))] * n),
        out_shape=[SDS((half, cols), F32)] * n + [SDS((NCHIP, half, cols), BF16)] * n,
        compiler_params=_cparams(),
    )(place, *gs, *sibs)
    return res[:n], res[n:]


def _chip_exchange_job(parts):
    def copies(ins, outs, send_sem, recv_sem):
        x, y, c, _ = _place()
        res = []
        for w in range(len(parts)):
            for s in range(1, NCHIP):
                cx, cy = _chip_at(x, y, s)
                k = w * (NCHIP - 1) + s - 1
                res.append(pltpu.make_async_remote_copy(
                    src_ref=ins[w].at[s], dst_ref=outs[w].at[s - 1], send_sem=send_sem.at[k],
                    recv_sem=recv_sem.at[k], device_id=(cx, cy, c), device_id_type=MESH))
        return res

    return _exchange_job(parts, [SDS((NCHIP - 1,) + p.shape[1:], BF16) for p in parts],
                         len(parts) * (NCHIP - 1), copies)


def _chip_sums(name, owns, rems):
    n = len(owns)
    half, cols = owns[0].shape
    tr = _row_tile(half, cols)

    def body(*refs):
        for own_ref, rem_ref, out_ref in zip(refs[:n], refs[n:2 * n], refs[2 * n:]):
            out_ref[...] = (((own_ref[...] + rem_ref[0].astype(F32)) + rem_ref[1].astype(F32))
                            + rem_ref[2].astype(F32))

    return pl.pallas_call(
        body, name=name, grid=(half // tr,),
        in_specs=[pl.BlockSpec((tr, cols), lambda i: (i, 0))] * n
        + [pl.BlockSpec((NCHIP - 1, tr, cols), lambda i: (0, i, 0))] * n,
        out_specs=[pl.BlockSpec((tr, cols), lambda i: (i, 0))] * n,
        out_shape=[SDS((half, cols), F32)] * n,
        compiler_params=_cparams(),
    )(*owns, *rems)


def _share_halves_job(halves):
    def copies(ins, outs, send_sem, recv_sem):
        x, y, c, _ = _place()
        return [pltpu.make_async_remote_copy(
            src_ref=ins[w], dst_ref=outs[w], send_sem=send_sem.at[w], recv_sem=recv_sem.at[w],
            device_id=(x, y, 1 - c), device_id_type=MESH) for w in range(len(halves))]

    return _exchange_job(halves, [SDS(h.shape, F32) for h in halves], len(halves), copies)


def _adamw_math(w, g, m, v):
    m = B1 * m + (1.0 - B1) * g
    v = B2 * v + (1.0 - B2) * (g * g)
    m_hat = m / (1.0 - B1 ** STEP)
    v_hat = v / (1.0 - B2 ** STEP)
    delta = -LR * (m_hat / (jnp.sqrt(v_hat) + AEPS) + WD * w)
    return delta, m, v


def _adamws(name, place, ws, owns, sibs, ms, vs):
    n = len(ws)
    rows, cols = ws[0].shape
    by_cols = owns[0].shape[0] == rows
    half, pc_cols = (rows, cols // 2) if by_cols else (rows // 2, cols)
    tr = _row_tile(half, pc_cols)
    nt = half // tr

    def body(pc_ref, *refs):
        ins, outs = refs[:5 * n], refs[5 * n:]
        for k in range(n):
            w_ref, own_ref, sib_ref, m_ref, v_ref = ins[5 * k:5 * k + 5]
            g = jnp.where(pl.program_id(0) == pc_ref[1], own_ref[...], sib_ref[...])
            d, mn, vn = _adamw_math(w_ref[...], g, m_ref[...], v_ref[...])
            for ref, val in zip(outs[4 * k:4 * k + 4], (g, d, mn, vn)):
                ref[...] = val

    full = pl.BlockSpec((tr, pc_cols), (lambda h, i, pc: (i, h)) if by_cols else (lambda h, i, pc: (h * nt + i, 0)))
    part = pl.BlockSpec((tr, pc_cols), lambda h, i, pc: (i, 0))
    res = pl.pallas_call(
        body, name=name,
        grid_spec=pltpu.PrefetchScalarGridSpec(
            num_scalar_prefetch=1, grid=(2, nt),
            in_specs=[full, part, part, full, full] * n, out_specs=[full] * (4 * n)),
        out_shape=[SDS((rows, cols), F32)] * (4 * n),
        compiler_params=_cparams(),
    )(place, *[a for group in zip(ws, owns, sibs, ms, vs) for a in group])
    return [tuple(res[4 * k:4 * k + 4]) for k in range(n)]


def _small_allreduce_adamw(sp, w, m, v, job):
    shape = sp.shape
    ji, jo = len(job.ins), len(job.out_shapes)

    def body(sp_ref, w_ref, m_ref, v_ref, *rest):
        jin, (g_ref, d_ref, mo_ref, vo_ref), jout = rest[:ji], rest[ji:ji + 4], rest[ji + 4:ji + 4 + jo]
        sib_s, pair_s, chip_s, send_sem, recv_sem = rest[ji + 4 + jo:ji + 9 + jo]
        jsem = rest[ji + 9 + jo:]
        job.start(jin, jout, jsem)
        x, y, c, j = _place()
        cp = pltpu.make_async_remote_copy(
            src_ref=sp_ref, dst_ref=sib_s, send_sem=send_sem.at[0], recv_sem=recv_sem.at[0],
            device_id=(x, y, 1 - c), device_id_type=MESH)
        cp.start()
        cp.wait()
        pair_s[...] = sp_ref[...] + sib_s[...]
        half = shape[0] // 2
        mine = pl.ds(pl.multiple_of(c * half, 8), half)
        cps = []
        for s in range(1, NCHIP):
            cx, cy = _chip_at(x, y, s)
            cp = pltpu.make_async_remote_copy(
                src_ref=pair_s.at[mine], dst_ref=chip_s.at[s, mine], send_sem=send_sem.at[s],
                recv_sem=recv_sem.at[s], device_id=(cx, cy, c), device_id_type=MESH)
            cp.start()
            cps.append(cp)
        chip_s[0] = pair_s[...]
        for cp in cps:
            cp.wait()
        cps = []
        for s in range(1, NCHIP):
            cp = pltpu.make_async_remote_copy(
                src_ref=chip_s.at[s, mine], dst_ref=chip_s.at[s, mine], send_sem=send_sem.at[NCHIP + s],
                recv_sem=recv_sem.at[NCHIP + s], device_id=(x, y, 1 - c), device_id_type=MESH)
            cp.start()
            cps.append(cp)
        for cp in cps:
            cp.wait()
        tot = chip_s[j]
        for k in range(1, NCHIP):
            tot = tot + chip_s[k ^ j]
        g_ref[...] = tot
        d, mn, vn = _adamw_math(w_ref[...], tot, m_ref[...], v_ref[...])
        d_ref[...] = d
        mo_ref[...] = mn
        vo_ref[...] = vn
        job.mid(jin, jout, jsem)
        job.finish(jin, jout, jsem)

    vm = pl.BlockSpec(memory_space=pltpu.VMEM)
    res = pl.pallas_call(
        body, name="small_allreduce_adamw",
        in_specs=[vm] * 4 + [ANY] * ji, out_specs=[vm] * 4 + [ANY] * jo,
        out_shape=[SDS(shape, F32)] * 4 + job.out_shapes,
        scratch_shapes=[pltpu.VMEM(shape, F32), pltpu.VMEM(shape, F32), pltpu.VMEM((NCHIP,) + shape, F32),
                        pltpu.SemaphoreType.DMA((2 * NCHIP,)), pltpu.SemaphoreType.DMA((2 * NCHIP,))] + job.sems,
        input_output_aliases={4 + a: 4 + b for a, b in job.aliases.items()},
        compiler_params=pltpu.CompilerParams(has_side_effects=True),
    )(sp, w, m, v, *job.ins)
    return res[:4], res[4:]


def _pack_small(first, mix, ln_g, ln_b, b_s, lbt, hn, ffn, fin, w_s):
    rows = [first.reshape(1, D), mix.reshape(1, D), ln_g.reshape(1, D), ln_b.reshape(1, D),
            b_s.reshape(1, D), lbt.reshape(2, D), hn.reshape(1, D), ffn.reshape(1, D), fin.reshape(1, D),
            jnp.zeros((6, D), F32)]
    return jnp.concatenate(rows + [w_s.reshape(NG, GCH, GCH).transpose(1, 0, 2).reshape(GCH, D)], axis=0)


def _unpack_small(p):
    w_s = p[16:].reshape(GCH, NG, GCH).transpose(1, 0, 2).reshape(1, NG, GCH, GCH)
    return dict(norm_mix_g=p[1:2], gmlp_ln_g=p[2:3], gmlp_ln_b=p[3:4], gmlp_b_s=p[4].reshape(1, NG, GCH),
                hgrn_lb_table=p[5:7], hgrn_norm_g=p[7:8], norm_ffn_g=p[8:9], norm_final_g=p[9],
                gmlp_w_s=w_s)


SMALL = ("norm_mix_g", "gmlp_ln_g", "gmlp_ln_b", "gmlp_w_s", "gmlp_b_s", "hgrn_lb_table", "hgrn_norm_g",
         "norm_ffn_g", "norm_final_g")
BIG = ("w_in", "w_gate_up", "w_branch_a", "w_branch_b", "w_out", "w_down")
ORDER = ("norm_mix_g", "w_in", "gmlp_ln_g", "gmlp_ln_b", "gmlp_w_s", "gmlp_b_s", "hgrn_lb_table",
         "hgrn_norm_g", "w_branch_a", "w_branch_b", "w_out", "norm_ffn_g", "w_gate_up", "w_down",
         "norm_final_g")


def kernel(x, norm_mix_g, w_in, gmlp_ln_g, gmlp_ln_b, gmlp_w_s, gmlp_b_s, hgrn_lb_table, hgrn_norm_g, w_branch_a, w_branch_b, w_out, norm_ffn_g, w_gate_up, w_down, norm_final_g, loss_target, m_norm_mix_g, m_w_in, m_gmlp_ln_g, m_gmlp_ln_b, m_gmlp_w_s, m_gmlp_b_s, m_hgrn_lb_table, m_hgrn_norm_g, m_w_branch_a, m_w_branch_b, m_w_out, m_norm_ffn_g, m_w_gate_up, m_w_down, m_norm_final_g, v_norm_mix_g, v_w_in, v_gmlp_ln_g, v_gmlp_ln_b, v_gmlp_w_s, v_gmlp_b_s, v_hgrn_lb_table, v_hgrn_norm_g, v_w_branch_a, v_w_branch_b, v_w_out, v_norm_ffn_g, v_w_gate_up, v_w_down, v_norm_final_g):
    args = dict(locals())
    T = x.shape[1]
    xs = x.reshape(T, D)
    target = loss_target.reshape(T, D)
    big = {n: args[n].reshape(args[n].shape[1:]) for n in BIG}
    big_m = {n: args["m_" + n].reshape(args[n].shape[1:]) for n in BIG}
    big_v = {n: args["v_" + n].reshape(args[n].shape[1:]) for n in BIG}

    x_i, y_i, c_i = lax.axis_index("x"), lax.axis_index("y"), lax.axis_index("c")
    place = jnp.stack([2 * x_i + y_i, c_i]).astype(jnp.int32)
    def by_shape(names):
        groups = []
        for n in names:
            if groups and big[groups[-1][0]].shape == big[n].shape:
                groups[-1].append(n)
            else:
                groups.append([n])
        return groups

    cast = {}
    for grp in by_shape(BIG):
        cast.update(zip(grp, _cast_shards("cast_" + grp[0], place, [big[n] for n in grp],
                                          paired=grp[0] == "w_gate_up")))
    tril = jnp.tril(jnp.ones((GCH, GCH), bool))
    wm = jnp.where(tril, gmlp_w_s[0], 0.0).astype(BF16)
    wm_t = jnp.swapaxes(wm, 1, 2)
    b_t = gmlp_b_s[0].T

    (proj, hb), w_in4, (w_a4, w_b4, w_out4, w_down4) = _proj_fwd(
        place, xs, norm_mix_g, cast["w_in"], [cast[n] for n in ("w_branch_a", "w_branch_b", "w_out", "w_down")])
    (ab,), _ = _gmlp_fwd(proj, gmlp_ln_g, gmlp_ln_b, wm, b_t)
    (o_raw, obb, st_before), (w_gu,) = _hgrn_fwd(
        proj, hgrn_lb_table, hgrn_norm_g, job=_gather_job([cast["w_gate_up"]]))
    w_a, w_b, w_o = (w.reshape(D, D) for w in (w_a4, w_b4, w_out4))
    (mgb, x1), _ = _merge_fwd(xs, ab, obb, proj, w_a, w_b, w_o)
    w_dn = w_down4.reshape(FF, D)
    act, dx2b, h2b, dgu, dx1, dx1b, acc_ffn = _ffn_fwd_bwd(
        x1, target, norm_ffn_g, norm_final_g.reshape(1, D), w_gu, w_dn)

    grads, owns, parts, halves, sibh = {}, {}, {}, {}, {}

    def pair_sums(names, sibs):
        sib_of = dict(zip(names, sibs))
        for grp in by_shape(names):
            o, p = _pair_sums("rs_pair_sum_" + grp[0], place, [grads[n] for n in grp], [sib_of[n] for n in grp])
            owns.update(zip(grp, o))
            parts.update(zip(grp, p))

    def chip_sums(names, got):
        rem_of = dict(zip(names, got))
        for grp in by_shape(names):
            h = _chip_sums("rs_chip_sum_" + grp[0], [owns[n] for n in grp], [rem_of[n] for n in grp])
            halves.update(zip(grp, h))

    ffn, mix = ("w_gate_up", "w_down"), ("w_branch_a", "w_branch_b", "w_out")
    grads["w_gate_up"], _ = _dw_gate_up(h2b, dgu)
    grads["w_down"], _ = _dw_down(act, dx2b)
    (dya, dyb, dproj), got = _merge_bwd(
        dx1b, ab, obb, proj, w_o, w_a, w_b, job=_pair_exchange_job([grads[n] for n in ffn]))
    pair_sums(ffn, got)
    grads["w_branch_a"], _ = _dw_square("dw_branch_a", ab, dya)
    grads["w_branch_b"], _ = _dw_square("dw_branch_b", obb, dyb)
    grads["w_out"], _ = _dw_square("dw_out", mgb, dx1b)
    (dproj, acc_hgrn), got = _hgrn_bwd(
        dproj, dyb, w_b, o_raw, proj, st_before, hgrn_lb_table, hgrn_norm_g,
        job=_join_jobs(_chip_exchange_job([parts[n] for n in ffn]), _pair_exchange_job([grads[n] for n in mix])))
    chip_sums(ffn, got[:2])
    pair_sums(mix, got[2:])
    dproj, acc_ln, dws, dmix = _gmlp_bwd(dproj, dya, w_a, proj, gmlp_ln_g, gmlp_ln_b, wm, wm_t, b_t)
    for_sibling, got = _dw_in_half(
        "dw_in_sibling_half", place, hb, dproj, False,
        job=_join_jobs(_share_halves_job([halves[n] for n in ffn]), _chip_exchange_job([parts[n] for n in mix])))
    sibh.update(zip(ffn, got[:2]))
    chip_sums(mix, got[2:])
    grads["w_in"], got = _dw_in_half(
        "dw_in_own_half", place, hb, dproj, True, job=_share_halves_job([for_sibling]))
    pair_sums(("w_in",), got)
    (grad_x, acc_mix), got = _proj_bwd(
        dproj, w_in4, xs, dx1, norm_mix_g,
        job=_join_jobs(_chip_exchange_job([parts["w_in"]]), _share_halves_job([halves[n] for n in mix])))
    chip_sums(("w_in",), got[:1])
    sibh.update(zip(mix, got[1:]))

    lbv = jax.nn.sigmoid(hgrn_lb_table[0] - hgrn_lb_table[1])
    d_t0 = jnp.sum(acc_hgrn[0], axis=0) * lbv * (1.0 - lbv)
    loss_row = jnp.zeros((D,), F32).at[0].set(jnp.sum(acc_ffn[0]))
    dws_m = jnp.where(tril[:, None, :], dws.reshape(GCH, NG, GCH), 0.0).transpose(1, 0, 2)
    db_s = jnp.sum(dmix.reshape(GCH, NG, GCH), axis=-1).T
    sp = _pack_small(loss_row, jnp.sum(acc_mix, 0), jnp.sum(acc_ln[0], 0), jnp.sum(acc_ln[1], 0), db_s,
                     jnp.stack([d_t0, -d_t0]), jnp.sum(acc_hgrn[1], 0), jnp.sum(acc_ffn[2], 0),
                     jnp.sum(acc_ffn[1], 0), dws_m)
    zero = jnp.zeros((D,), F32)

    def pack(prefix):
        a = lambda n: args[prefix + n]
        return _pack_small(zero, a("norm_mix_g"), a("gmlp_ln_g"), a("gmlp_ln_b"), a("gmlp_b_s"),
                           a("hgrn_lb_table"), a("hgrn_norm_g"), a("norm_ffn_g"), a("norm_final_g"),
                           a("gmlp_w_s"))

    packed, (sibh["w_in"],) = _small_allreduce_adamw(
        sp, pack(""), pack("m_"), pack("v_"), _share_halves_job([halves["w_in"]]))
    loss = packed[0][0, 0]
    small = [_unpack_small(p) for p in packed]
    out = {n: tuple(s[n] for s in small) for n in SMALL}
    for grp in by_shape(BIG):
        res = _adamws("adamw_" + grp[0], place, *[[d[n] for n in grp] for d in (big, halves, sibh, big_m, big_v)])
        for n, quad in zip(grp, res):
            out[n] = tuple(a.reshape(args[n].shape) for a in quad)
    return (loss, grad_x.reshape(x.shape), *[out[n][0] for n in ORDER], *[out[n][1] for n in ORDER],
            *[out[n][2] for n in ORDER], *[out[n][3] for n in ORDER])
```

```python
import functools
import math

import jax
import jax.numpy as jnp
from jax import lax
from jax.experimental import pallas as pl
from jax.experimental.pallas import tpu as pltpu

F32 = jnp.float32
BF16 = jnp.bfloat16
SDS = jax.ShapeDtypeStruct
MESH = pl.DeviceIdType.MESH
ANY = pl.BlockSpec(memory_space=pl.ANY)

D = 1024
NIN = 8
NG = 8
GCH = 128
NH = 8
HD = 128
HCH = 64
HGRN_HB = 8
HGRN_TOKENS = 256
GMLP_TOKENS = 512
HW = HGRN_HB * HD
DW_TOKENS = 2048
ELEMENTWISE_BLOCK_BYTES = 2 * 1024 * 1024
PROJ_OUT_SLOTS = 4
FF = 2816
FFS = 1408
NCHIP = 4
EPS = 1e-6
QSCALE = HD ** -0.5
GELU_C0 = math.sqrt(2.0 / math.pi)
GELU_C1 = 0.044715
LR, B1, B2, AEPS, WD, STEP = 0.001, 0.9, 0.999, 1e-08, 0.01, 10
VMEM_LIMIT_V7X = 56 * 1024 * 1024
SP_ROWS = 144


def _cparams(**kw):
    return pltpu.CompilerParams(vmem_limit_bytes=VMEM_LIMIT_V7X, **kw)


def _mm(a, b):
    return jnp.dot(a, b, preferred_element_type=F32)


def _mm_nt(a, b):
    return lax.dot_general(a, b, (((1,), (1,)), ((), ())), preferred_element_type=F32)


def _mm_tn(a, b):
    return lax.dot_general(a, b, (((0,), (0,)), ((), ())), preferred_element_type=F32)


def _rows8(x):
    r, c = x.shape
    return jnp.sum(x.reshape(r // 8, 8, c), axis=0)


def _mean(x):
    return jnp.mean(x, axis=-1, keepdims=True)


def _sigmoid(x):
    return 1.0 / (1.0 + jnp.exp(-x))


def _gelu(x):
    t = jnp.tanh(GELU_C0 * (x + GELU_C1 * x * x * x))
    return 0.5 * x * (1.0 + t), t


def _gelu_grad(x, t):
    return 0.5 * (1.0 + t) + 0.5 * x * (1.0 - t * t) * (GELU_C0 * (1.0 + 3.0 * GELU_C1 * x * x))


def _component_of(group):
    return jnp.where(group < 6, (group + 4) % 6, group)


def _proj_fwd(place, x, g_mix, w_in4, later):
    T = x.shape[0]
    tm = min(1024, T)
    ni = T // tm
    n = len(later)

    def body(pc_ref, x_ref, g_ref, *rest):
        proj_ref, h_ref, w_all = rest[1 + n:4 + n]
        gathered = rest[4 + n:4 + 2 * n]
        hs, wbuf, wsem, obuf, osem = rest[4 + 2 * n:9 + 2 * n]
        w_sems, later_sems = rest[9 + 2 * n:15 + 2 * n], rest[15 + 2 * n:]
        jp, i = pl.program_id(0), pl.program_id(1)
        w_cols = [w_all.at[:, :, pl.ds(k * D, D)] for k in range(2)]

        def w_copy(blk):
            cols = pl.ds(pl.multiple_of((blk % 2) * D, 128), D)
            return pltpu.make_async_copy(w_all.at[pc_ref[0] ^ (blk // 2), :, cols], wbuf.at[blk % 2],
                                         wsem.at[blk % 2])

        @pl.when((jp == 0) & (i == 0))
        def _():
            _gather_start(w_cols, w_sems)
            w_copy(jp).start()

        @pl.when(i == 0)
        def _():
            w_copy(jp).wait()

        @pl.when(jp == 0)
        def _():
            xv = x_ref[...]
            r = lax.rsqrt(_mean(xv * xv) + EPS)
            hb = (xv * r * g_ref[...]).astype(BF16)
            hs[i] = hb
            h_ref[...] = hb

        step = jp * ni + i
        slot = step % PROJ_OUT_SLOTS

        def o_copy(slot_):
            comp = 2 * (pc_ref[0] ^ (jp // 2)) + jp % 2
            return pltpu.make_async_copy(
                obuf.at[slot_], proj_ref.at[comp, pl.ds(pl.multiple_of(i * tm, 8), tm)], osem.at[slot_])

        @pl.when(step >= PROJ_OUT_SLOTS)
        def _():
            o_copy(slot).wait()

        obuf[slot] = _mm(hs[i], wbuf[jp % 2])
        o_copy(slot).start()

        @pl.when(step == NIN * ni - 1)
        def _():
            for k in range(PROJ_OUT_SLOTS):
                o_copy((slot + 1 + k) % PROJ_OUT_SLOTS).wait()

        for nxt in range(1, NIN):
            @pl.when((jp == nxt - 1) & (i == ni - 1))
            def _():
                if nxt >= 2:
                    _gather_land([w_cols[nxt % 2]], w_sems, nxt // 2, first=nxt % 2)
                if nxt == 5:
                    _gather_start(gathered, later_sems)
                if nxt == NIN - 1:
                    _gather_neighbours(gathered, later_sems)
                w_copy(jp + 1).start()

        @pl.when((jp == NIN - 1) & (i == ni - 1))
        def _():
            _gather_drain(w_cols, w_sems)
            _gather_finish(gathered, later_sems)

    tile = lambda jp, i, pc: (jnp.where(jp == 0, i, ni - 1), 0)
    res = pl.pallas_call(
        body, name="proj_fwd",
        grid_spec=pltpu.PrefetchScalarGridSpec(
            num_scalar_prefetch=1, grid=(NIN, ni),
            in_specs=[pl.BlockSpec((tm, D), tile), pl.BlockSpec((1, D), lambda jp, i, pc: (0, 0))] + [ANY] * (1 + n),
            out_specs=[ANY, pl.BlockSpec((tm, D), tile)] + [ANY] * (1 + n),
            scratch_shapes=[pltpu.VMEM((ni, tm, D), BF16), pltpu.VMEM((2, D, D), BF16),
                            pltpu.SemaphoreType.DMA((2,)), pltpu.VMEM((PROJ_OUT_SLOTS, tm, D), F32),
                            pltpu.SemaphoreType.DMA((PROJ_OUT_SLOTS,))] + _gather_sems(2) + _gather_sems(n)),
        out_shape=[SDS((NIN, T, D), F32), SDS((T, D), BF16), SDS(w_in4.shape, BF16)]
        + [SDS(a.shape, a.dtype) for a in later],
        input_output_aliases={3 + k: 2 + k for k in range(1 + n)},
        compiler_params=_cparams(has_side_effects=True),
    )(place, x, g_mix, w_in4, *later)
    return res[:2], res[2], res[3:]


def _chunks_abreast(x):
    return jnp.concatenate([x[GCH * ch:GCH * (ch + 1)] for ch in range(x.shape[0] // GCH)], axis=1)


def _chunks_stacked(x):
    return jnp.concatenate([x[:, GCH * ch:GCH * (ch + 1)] for ch in range(x.shape[1] // GCH)], axis=0)


def _layer_norm_stats(gv):
    mu = _mean(gv)
    xc = gv - mu
    rs = lax.rsqrt(_mean(xc * xc) + EPS)
    return xc * rs, rs


def _gmlp_fwd(proj, ln_g, ln_b, wm, b_t, job=None):
    T = proj.shape[1]
    tm = min(GMLP_TOKENS, T)

    def body(u_ref, v_ref, lg_ref, lb_ref, wm_ref, bt_ref, a_ref, a_s):
        gu, _ = _gelu(u_ref[...])
        gv, _ = _gelu(v_ref[...])
        vhat, _ = _layer_norm_stats(gv)
        vnb = (vhat * lg_ref[...] + lb_ref[...]).astype(BF16)
        for g in range(NG):
            cols = slice(128 * g, 128 * (g + 1))
            mixed = _mm(wm_ref[g], _chunks_abreast(vnb[:, cols])) + bt_ref[:, g:g + 1]
            a_s[:, cols] = gu[:, cols] * _chunks_stacked(mixed)
        a_ref[...] = a_s[...].astype(BF16)

    row = lambda i: (0, 0)
    return _call(
        body, name="gmlp_fwd", grid=(T // tm,), job=job, args=(proj, proj, ln_g, ln_b, wm, b_t),
        in_specs=[pl.BlockSpec((None, tm, D), lambda i: (0, i, 0)), pl.BlockSpec((None, tm, D), lambda i: (1, i, 0)),
                  pl.BlockSpec((1, D), row), pl.BlockSpec((1, D), row),
                  pl.BlockSpec((NG, GCH, GCH), lambda i: (0, 0, 0)), pl.BlockSpec((GCH, NG), row)],
        out_specs=[pl.BlockSpec((tm, D), lambda i: (i, 0))],
        out_shape=[SDS((T, D), BF16)],
        scratch_shapes=[pltpu.VMEM((tm, D), F32)])


def _cumsum64(x, row):
    for s in (1, 2, 4, 8, 16, 32):
        x = x + jnp.where(row >= s, pltpu.roll(x, s, 0), 0.0)
    return x


def _revcumsum64(x, row):
    n = x.shape[0]
    for s in (1, 2, 4, 8, 16, 32):
        x = x + jnp.where(row < HCH - s, pltpu.roll(x, n - s, 0), 0.0)
    return x


def _head_mean(x):
    parts = [jnp.broadcast_to(_mean(x[:, HD * h:HD * (h + 1)]), (x.shape[0], HD)) for h in range(x.shape[1] // HD)]
    return jnp.concatenate(parts, axis=1)


def _seg_sum(x):
    n, c = x.shape
    s = jnp.sum(x.reshape(n // HCH, HCH, c), axis=1, keepdims=True)
    return jnp.broadcast_to(s, (n // HCH, HCH, c)).reshape(n, c)


def _hgrn_gates(fl, lbv, row):
    s = _sigmoid(fl)
    f = lbv + (1.0 - lbv) * s
    a = _cumsum64(jnp.log(f), row)
    a_mid = _seg_sum(jnp.where(row == HCH // 2 - 1, a, 0.0))
    a_last = _seg_sum(jnp.where(row == HCH - 1, a, 0.0))
    return s, f, a, a_mid, a_last


def _hgrn_fwd(proj, lb_table, norm_g, job=None):
    T = proj.shape[1]
    tb = min(HGRN_TOKENS, T)
    nc = tb // HCH

    def body(q_ref, fl_ref, v_ref, g_ref, lbt_ref, gn_ref, o_ref, ob_ref, stb_ref, st_s, o_s):
        @pl.when(pl.program_id(1) == 0)
        def _():
            st_s[...] = jnp.zeros_like(st_s)

        row = lax.broadcasted_iota(jnp.int32, (tb, HW), 0) & (HCH - 1)
        lbv = _sigmoid(lbt_ref[0:1, :] - lbt_ref[1:2, :])
        _, f, a, a_mid, a_last = _hgrn_gates(fl_ref[...], lbv, row)
        k = 1.0 - f
        qs = q_ref[...] * QSCALE
        q_in = (qs * jnp.exp(a - a_mid)).astype(BF16)
        k_in = (k * jnp.exp(a_mid - a)).astype(BF16)
        q_a = (qs * jnp.exp(a)).astype(BF16)
        k_d = (k * jnp.exp(a_last - a)).astype(BF16)
        dec = jnp.exp(a_last)
        vb = v_ref[...].astype(BF16)
        tri = (lax.broadcasted_iota(jnp.int32, (HCH, HCH), 0)
               >= lax.broadcasted_iota(jnp.int32, (HCH, HCH), 1))
        for c in range(nc):
            sl = slice(HCH * c, HCH * (c + 1))
            for hh in range(HGRN_HB):
                hs = slice(HD * hh, HD * (hh + 1))
                st = st_s[hh]
                stb_ref[hh, c] = st
                sc = jnp.where(tri, _mm_nt(q_in[sl, hs], k_in[sl, hs]), 0.0)
                o_s[sl, hs] = _mm(sc.astype(BF16), vb[sl, hs]) + _mm_nt(q_a[sl, hs], st.astype(BF16))
                d64 = dec[sl, hs]
                st_s[hh] = st * jnp.concatenate([d64, d64], axis=0) + _mm_tn(vb[sl, hs], k_d[sl, hs])
        o = o_s[...]
        r = lax.rsqrt(_head_mean(o * o) + EPS)
        g = g_ref[...]
        o_ref[...] = o
        ob_ref[...] = (o * r * gn_ref[...] * (g * _sigmoid(g))).astype(BF16)

    def col(off):
        return pl.BlockSpec((None, tb, HW), lambda h, cb: (off, cb, h))

    return _call(
        body, name="hgrn_fwd", grid=(NH // HGRN_HB, T // tb), job=job,
        args=(proj, proj, proj, proj, lb_table, norm_g),
        in_specs=[col(2), col(3), col(4), col(5),
                  pl.BlockSpec((2, HW), lambda h, cb: (0, h)), pl.BlockSpec((1, HW), lambda h, cb: (0, h))],
        out_specs=[pl.BlockSpec((tb, HW), lambda h, cb: (cb, h)), pl.BlockSpec((tb, HW), lambda h, cb: (cb, h)),
                   pl.BlockSpec((HGRN_HB, nc, HD, HD), lambda h, cb: (h, cb, 0, 0))],
        out_shape=[SDS((T, D), F32), SDS((T, D), BF16), SDS((NH, T // HCH, HD, HD), F32)],
        scratch_shapes=[pltpu.VMEM((HGRN_HB, HD, HD), F32), pltpu.VMEM((tb, HW), F32)])


def _merge_fwd(x, ab, ob, proj, w_a, w_b, w_out, job=None):
    T = x.shape[0]
    tm = min(512, T)

    def body(x_ref, ab_ref, ob_ref, ga_ref, gb_ref, wa_ref, wb_ref, wo_ref, mg_ref, x1_ref):
        ya = _mm(ab_ref[...], wa_ref[...])
        yb = _mm(ob_ref[...], wb_ref[...])
        merged = (_sigmoid(ga_ref[...]) * ya + _sigmoid(gb_ref[...]) * yb).astype(BF16)
        mg_ref[...] = merged
        x1_ref[...] = x_ref[...] + _mm(merged, wo_ref[...])

    t = lambda i: (i, 0)
    w = lambda i: (0, 0)
    return _call(
        body, name="merge_fwd", grid=(T // tm,), job=job, args=(x, ab, ob, proj, proj, w_a, w_b, w_out),
        in_specs=[pl.BlockSpec((tm, D), t), pl.BlockSpec((tm, D), t), pl.BlockSpec((tm, D), t),
                  pl.BlockSpec((None, tm, D), lambda i: (6, i, 0)), pl.BlockSpec((None, tm, D), lambda i: (7, i, 0)),
                  pl.BlockSpec((D, D), w), pl.BlockSpec((D, D), w), pl.BlockSpec((D, D), w)],
        out_specs=[pl.BlockSpec((tm, D), t)] * 2,
        out_shape=[SDS((T, D), BF16), SDS((T, D), F32)])


def _ffn_fwd_bwd(x1, target, g_ffn, g_fin, w_gu, w_down):
    T = x1.shape[0]
    tm = min(256, T)
    inv_d = 1.0 / D

    def body(x1_ref, tg_ref, gf_ref, gn_ref, wgu_ref, wd_ref,
             act_ref, dx2b_ref, h2b_ref, dgu_ref, dx1_ref, dx1b_ref, acc_ref):
        @pl.when(pl.program_id(0) == 0)
        def _():
            acc_ref[...] = jnp.zeros_like(acc_ref)

        x1v = x1_ref[...]
        gf = gf_ref[...]
        gn = gn_ref[...]
        rr1 = lax.rsqrt(_mean(x1v * x1v) + EPS)
        x1n = x1v * rr1
        h2b = (x1n * gf).astype(BF16)
        h2b_ref[...] = h2b
        gate = _mm(h2b, wgu_ref[0])
        up = _mm(h2b, wgu_ref[1])
        sg = _sigmoid(gate)
        si = gate * sg
        act = (si * up).astype(BF16)
        act_ref[...] = act
        x2 = x1v + _mm(act, wd_ref[...])
        rr2 = lax.rsqrt(_mean(x2 * x2) + EPS)
        x2n = x2 * rr2
        e = x2n * gn - tg_ref[...]
        acc_ref[0] += _rows8(e * e) * (0.5 * inv_d)
        dy = e * inv_d
        acc_ref[1] += _rows8(dy * x2n)
        dxn = dy * gn
        dx2 = rr2 * (dxn - x2n * _mean(dxn * x2n))
        dx2b = dx2.astype(BF16)
        dx2b_ref[...] = dx2b
        dact = _mm_nt(dx2b, wd_ref[...])
        dgate = (dact * up * (sg * (1.0 + gate * (1.0 - sg)))).astype(BF16)
        dup = (dact * si).astype(BF16)
        dgu_ref[0] = dgate
        dgu_ref[1] = dup
        dh2 = _mm_nt(dgate, wgu_ref[0]) + _mm_nt(dup, wgu_ref[1])
        acc_ref[2] += _rows8(dh2 * x1n)
        dxn1 = dh2 * gf
        dx1 = dx2 + rr1 * (dxn1 - x1n * _mean(dxn1 * x1n))
        dx1_ref[...] = dx1
        dx1b_ref[...] = dx1.astype(BF16)

    t = lambda i: (i, 0)
    w = lambda i: (0, 0)
    one = pl.Buffered(1)
    return pl.pallas_call(
        body, name="ffn_fwd_bwd", grid=(T // tm,),
        in_specs=[pl.BlockSpec((tm, D), t), pl.BlockSpec((tm, D), t),
                  pl.BlockSpec((1, D), w), pl.BlockSpec((1, D), w),
                  pl.BlockSpec((2, D, FF), lambda i: (0, 0, 0), pipeline_mode=one),
                  pl.BlockSpec((FF, D), w, pipeline_mode=one)],
        out_specs=[pl.BlockSpec((tm, FF), t), pl.BlockSpec((tm, D), t), pl.BlockSpec((tm, D), t),
                   pl.BlockSpec((2, tm, FF), lambda i: (0, i, 0)),
                   pl.BlockSpec((tm, D), t), pl.BlockSpec((tm, D), t),
                   pl.BlockSpec((3, 8, D), lambda i: (0, 0, 0))],
        out_shape=[SDS((T, FF), BF16), SDS((T, D), BF16), SDS((T, D), BF16),
                   SDS((2, T, FF), BF16), SDS((T, D), F32), SDS((T, D), BF16),
                   SDS((3, 8, D), F32)],
        compiler_params=_cparams(),
    )(x1, target, g_ffn, g_fin, w_gu, w_down)


def _merge_bwd(dx1b, ab, ob, proj, w_out, w_a, w_b, job=None):
    T = dx1b.shape[0]
    tm = min(512, T)

    def body(dx_ref, ab_ref, ob_ref, ga_ref, gb_ref, wo_ref, wa_ref, wb_ref, dya_ref, dyb_ref, dp_ref):
        dm = _mm_nt(dx_ref[...], wo_ref[...])
        sa = _sigmoid(ga_ref[...])
        sb = _sigmoid(gb_ref[...])
        dya_ref[...] = (dm * sa).astype(BF16)
        dyb_ref[...] = (dm * sb).astype(BF16)
        dp_ref[0] = (dm * _mm(ab_ref[...], wa_ref[...]) * sa * (1.0 - sa)).astype(BF16)
        dp_ref[1] = (dm * _mm(ob_ref[...], wb_ref[...]) * sb * (1.0 - sb)).astype(BF16)

    t = lambda i: (i, 0)
    w = lambda i: (0, 0)
    return _call(
        body, name="merge_bwd", grid=(T // tm,),
        in_specs=[pl.BlockSpec((tm, D), t), pl.BlockSpec((tm, D), t), pl.BlockSpec((tm, D), t),
                  pl.BlockSpec((None, tm, D), lambda i: (6, i, 0)), pl.BlockSpec((None, tm, D), lambda i: (7, i, 0)),
                  pl.BlockSpec((D, D), w), pl.BlockSpec((D, D), w), pl.BlockSpec((D, D), w)],
        out_specs=[pl.BlockSpec((tm, D), t)] * 2 + [pl.BlockSpec((2, tm, D), lambda i: (3, i, 0))],
        out_shape=[SDS((T, D), BF16), SDS((T, D), BF16), SDS((NIN, T, D), BF16)],
        args=(dx1b, ab, ob, proj, proj, w_out, w_a, w_b), job=job)


def _hgrn_bwd(dproj, dyb, w_b, o_raw, proj, st_before, lb_table, norm_g, job=None):
    T = dyb.shape[0]
    tb = min(HGRN_TOKENS, T)
    nc = tb // HCH
    nb = T // tb

    def body(dp_in, dyb_ref, wb_ref, o_ref, q_ref, fl_ref, v_ref, g_ref, stb_ref, lbt_ref, gn_ref,
             dp_ref, acc_ref, dst_s, dqin_s, dqa_s, dkin_s, dkd_s, dv_s, ddec_s):
        del dp_in

        @pl.when(pl.program_id(1) == 0)
        def _():
            dst_s[...] = jnp.zeros_like(dst_s)
            acc_ref[...] = jnp.zeros_like(acc_ref)

        row = lax.broadcasted_iota(jnp.int32, (tb, HW), 0) & (HCH - 1)
        gn = gn_ref[...]
        lbv = _sigmoid(lbt_ref[0:1, :] - lbt_ref[1:2, :])
        o = o_ref[...]
        r = lax.rsqrt(_head_mean(o * o) + EPS)
        on = o * r
        g = g_ref[...]
        sgm = _sigmoid(g)
        dob_v = _mm_nt(dyb_ref[...], wb_ref[...])
        dp_ref[3] = (dob_v * on * gn * (sgm * (1.0 + g * (1.0 - sgm)))).astype(BF16)
        do_n = dob_v * (g * sgm)
        acc_ref[1] += _rows8(do_n * on)
        dxn = do_n * gn
        do = (r * (dxn - on * _head_mean(dxn * on))).astype(BF16)
        s, f, a, a_mid, a_last = _hgrn_gates(fl_ref[...], lbv, row)
        k = 1.0 - f
        qs = q_ref[...] * QSCALE
        e_q = jnp.exp(a - a_mid)
        e_k = jnp.exp(a_mid - a)
        e_a = jnp.exp(a)
        e_l = jnp.exp(a_last - a)
        dec = jnp.exp(a_last)
        q_in = qs * e_q
        k_in = k * e_k
        q_a = qs * e_a
        k_d = k * e_l
        q_inb, k_inb, q_ab, k_db = (z.astype(BF16) for z in (q_in, k_in, q_a, k_d))
        vb = v_ref[...].astype(BF16)
        tri = (lax.broadcasted_iota(jnp.int32, (HCH, HCH), 0)
               >= lax.broadcasted_iota(jnp.int32, (HCH, HCH), 1))
        for c in reversed(range(nc)):
            sl = slice(HCH * c, HCH * (c + 1))
            for hh in range(HGRN_HB):
                hs = slice(HD * hh, HD * (hh + 1))
                stp = stb_ref[hh, c]
                dst = dst_s[hh]
                dstb = dst.astype(BF16)
                do_c = do[sl, hs]
                v_c = vb[sl, hs]
                dqa_s[sl, hs] = _mm(do_c, stp.astype(BF16))
                dkd_s[sl, hs] = _mm(v_c, dstb)
                ddec_s[sl, hs] = jnp.broadcast_to(jnp.sum(dst * stp, axis=0, keepdims=True), (HCH, HD))
                sc = jnp.where(tri, _mm_nt(q_inb[sl, hs], k_inb[sl, hs]), 0.0).astype(BF16)
                dsc = jnp.where(tri, _mm_nt(do_c, v_c), 0.0).astype(BF16)
                dv_s[sl, hs] = _mm_nt(k_db[sl, hs], dstb) + _mm_tn(sc, do_c)
                dqin_s[sl, hs] = _mm(dsc, k_inb[sl, hs])
                dkin_s[sl, hs] = _mm_tn(dsc, q_inb[sl, hs])
                d64 = dec[sl, hs]
                dst_s[hh] = dst * jnp.concatenate([d64, d64], axis=0) + _mm_tn(do_c, q_ab[sl, hs])
        dq_in = dqin_s[...]
        dq_a = dqa_s[...]
        dk_in = dkin_s[...]
        dk_d = dkd_s[...]
        dp_ref[0] = ((dq_in * e_q + dq_a * e_a) * QSCALE).astype(BF16)
        dp_ref[2] = dv_s[...].astype(BF16)
        tq = dq_in * q_in
        tk = dk_in * k_in
        td = dk_d * k_d
        d_a = tq + dq_a * q_a - tk - td
        d_a = d_a + jnp.where(row == HCH // 2 - 1, _seg_sum(tk - tq), 0.0)
        d_a = d_a + jnp.where(row == HCH - 1, _seg_sum(td) + ddec_s[...] * dec, 0.0)
        dlf = _revcumsum64(d_a, row)
        df = dlf / f - (dk_in * e_k + dk_d * e_l)
        dp_ref[1] = (df * (1.0 - lbv) * s * (1.0 - s)).astype(BF16)
        acc_ref[0] += _rows8(df * (1.0 - s))

    def col(off):
        return pl.BlockSpec((None, tb, HW), lambda h, cb: (off, nb - 1 - cb, h))

    hb = lambda h, cb: (nb - 1 - cb, h)
    return _call(
        body, name="hgrn_bwd", grid=(NH // HGRN_HB, nb), job=job,
        args=(dproj, dyb, w_b, o_raw, proj, proj, proj, proj, st_before, lb_table, norm_g),
        in_specs=[ANY, pl.BlockSpec((tb, D), lambda h, cb: (nb - 1 - cb, 0)),
                  pl.BlockSpec((HW, D), lambda h, cb: (h, 0)), pl.BlockSpec((tb, HW), hb),
                  col(2), col(3), col(4), col(5),
                  pl.BlockSpec((HGRN_HB, nc, HD, HD), lambda h, cb: (h, nb - 1 - cb, 0, 0)),
                  pl.BlockSpec((2, HW), lambda h, cb: (0, h)), pl.BlockSpec((1, HW), lambda h, cb: (0, h))],
        out_specs=[pl.BlockSpec((4, tb, HW), lambda h, cb: (0, nb - 1 - cb, h)),
                   pl.BlockSpec((2, 8, HW), lambda h, cb: (0, 0, h))],
        out_shape=[SDS(dproj.shape, BF16), SDS((2, 8, D), F32)],
        scratch_shapes=[pltpu.VMEM((HGRN_HB, HD, HD), F32)] + [pltpu.VMEM((tb, HW), F32)] * 6,
        aliases={0: 0})


def _gmlp_bwd(dproj, dya, w_a, proj, ln_g, ln_b, wm, wm_t, b_t):
    T = dya.shape[0]
    tm = min(GMLP_TOKENS, T)

    def body(dp_in, dya_ref, wa_ref, u_ref, v_ref, lg_ref, lb_ref, wm_ref, wmt_ref, bt_ref,
             dp_ref, acc_ref, dws_ref, dmix_ref, du_s, dvn_s):
        del dp_in

        @pl.when(pl.program_id(0) == 0)
        def _():
            acc_ref[...] = jnp.zeros_like(acc_ref)
            dws_ref[...] = jnp.zeros_like(dws_ref)
            dmix_ref[...] = jnp.zeros_like(dmix_ref)

        u = u_ref[...]
        v = v_ref[...]
        lg = lg_ref[...]
        gu, t_u = _gelu(u)
        gv, t_v = _gelu(v)
        vhat, rs = _layer_norm_stats(gv)
        vnb = (vhat * lg + lb_ref[...]).astype(BF16)
        da_v = _mm_nt(dya_ref[...], wa_ref[...])
        for g in range(NG):
            cols = slice(128 * g, 128 * (g + 1))
            vng = _chunks_abreast(vnb[:, cols])
            mixed = _mm(wm_ref[g], vng) + bt_ref[:, g:g + 1]
            dag = _chunks_abreast(da_v[:, cols])
            dmx = dag * _chunks_abreast(gu[:, cols])
            du_s[:, cols] = _chunks_stacked(dag * mixed)
            dmxb = dmx.astype(BF16)
            dws_ref[:, cols] += _mm_nt(dmxb, vng)
            dmix_ref[:, cols] += sum(dmx[:, GCH * ch:GCH * (ch + 1)] for ch in range(tm // GCH))
            dvn_s[:, cols] = _chunks_stacked(_mm(wmt_ref[g], dmxb))
        dp_ref[0] = (du_s[...] * _gelu_grad(u, t_u)).astype(BF16)
        dvn = dvn_s[...]
        acc_ref[0] += _rows8(dvn * vhat)
        acc_ref[1] += _rows8(dvn)
        dvh = dvn * lg
        dgv = rs * (dvh - _mean(dvh) - vhat * _mean(dvh * vhat))
        dp_ref[1] = (dgv * _gelu_grad(v, t_v)).astype(BF16)

    row = lambda i: (0, 0)
    w3 = lambda i: (0, 0, 0)
    return pl.pallas_call(
        body, name="gmlp_bwd", grid=(T // tm,),
        in_specs=[ANY, pl.BlockSpec((tm, D), lambda i: (i, 0)), pl.BlockSpec((D, D), row),
                  pl.BlockSpec((None, tm, D), lambda i: (0, i, 0)), pl.BlockSpec((None, tm, D), lambda i: (1, i, 0)),
                  pl.BlockSpec((1, D), row), pl.BlockSpec((1, D), row),
                  pl.BlockSpec((NG, GCH, GCH), w3), pl.BlockSpec((NG, GCH, GCH), w3),
                  pl.BlockSpec((GCH, NG), row)],
        out_specs=[pl.BlockSpec((2, tm, D), lambda i: (2, i, 0)),
                   pl.BlockSpec((2, 8, D), w3), pl.BlockSpec((GCH, D), row), pl.BlockSpec((GCH, D), row)],
        out_shape=[SDS(dproj.shape, BF16), SDS((2, 8, D), F32), SDS((GCH, D), F32), SDS((GCH, D), F32)],
        scratch_shapes=[pltpu.VMEM((tm, D), F32), pltpu.VMEM((tm, D), F32)],
        input_output_aliases={0: 0},
        compiler_params=_cparams(),
    )(dproj, dya, w_a, proj, proj, ln_g, ln_b, wm, wm_t, b_t)


def _proj_bwd(dproj, w_in4, x, dx1, g_mix, job=None):
    T = x.shape[0]
    tm = min(256, T)
    order = (2, 3, 4, 5, 0, 1, 6, 7)

    def body(dp_ref, w_ref, x_ref, dx1_ref, g_ref, gx_ref, acc_ref):
        @pl.when(pl.program_id(0) == 0)
        def _():
            acc_ref[...] = jnp.zeros_like(acc_ref)

        dh = None
        for m, og in enumerate(order):
            part = _mm_nt(dp_ref[m], w_ref[og // 2, :, D * (og % 2):D * (og % 2 + 1)])
            dh = part if dh is None else dh + part
        xv = x_ref[...]
        r = lax.rsqrt(_mean(xv * xv) + EPS)
        xn = xv * r
        acc_ref[...] += _rows8(dh * xn)
        dxn = dh * g_ref[...]
        gx_ref[...] = dx1_ref[...] + r * (dxn - xn * _mean(dxn * xn))

    t = lambda i: (i, 0)
    return _call(
        body, name="proj_bwd", grid=(T // tm,),
        in_specs=[pl.BlockSpec((NIN, tm, D), lambda i: (0, i, 0)),
                  pl.BlockSpec((NCHIP, D, 2 * D), lambda i: (0, 0, 0), pipeline_mode=pl.Buffered(1)),
                  pl.BlockSpec((tm, D), t), pl.BlockSpec((tm, D), t), pl.BlockSpec((1, D), lambda i: (0, 0))],
        out_specs=[pl.BlockSpec((tm, D), t), pl.BlockSpec((8, D), lambda i: (0, 0))],
        out_shape=[SDS((T, D), F32), SDS((8, D), F32)],
        args=(dproj, w_in4, x, dx1, g_mix), job=job)


def _dw_call(name, a, b, a_spec, b_spec, o_spec, out_shape, nblk, tt, job=None, prefetch=None):
    T = a.shape[-2]

    def body(*refs):
        a_ref, b_ref, o_ref = refs[-3:]

        @pl.when(pl.program_id(1) == 0)
        def _():
            o_ref[...] = jnp.zeros_like(o_ref)
        o_ref[...] += _mm_tn(a_ref[...], b_ref[...])

    (out,), job_out = _call(
        body, name=name, grid=(nblk, T // tt), in_specs=[a_spec, b_spec], out_specs=[o_spec],
        out_shape=[out_shape], args=(a, b), job=job, prefetch=prefetch)
    return out, job_out


def _dw_in_half(name, place, hb, dproj, mine, job=None):
    tt = min(DW_TOKENS, hb.shape[0])

    def comp(k, pc):
        return _component_of(2 * k + (pc[1] if mine else 1 - pc[1]))

    return _dw_call(
        name, hb, dproj,
        pl.BlockSpec((tt, D), lambda k, t, pc: (t, 0)),
        pl.BlockSpec((None, tt, D), lambda k, t, pc: (comp(k, pc), t, 0)),
        pl.BlockSpec((None, D, D), lambda k, t, pc: (k, 0, 0)),
        SDS((NCHIP, D, D), F32), NCHIP, tt, job, place)


def _dw_gate_up(h2b, dgu, job=None):
    tt = min(DW_TOKENS, h2b.shape[0])
    return _dw_call(
        "dw_gate_up", h2b, dgu,
        pl.BlockSpec((tt, D), lambda k, t: (t, 0)),
        pl.BlockSpec((None, tt, FFS), lambda k, t: (k // 2, t, k % 2)),
        pl.BlockSpec((None, D, FFS), lambda k, t: (k, 0, 0)),
        SDS((NCHIP, D, FFS), F32), NCHIP, tt, job)


def _dw_down(act, dx2b, job=None):
    tt = min(DW_TOKENS, act.shape[0])
    g, job_out = _dw_call(
        "dw_down", act, dx2b,
        pl.BlockSpec((tt, FFS), lambda k, t: (t, k)),
        pl.BlockSpec((tt, D), lambda k, t: (t, 0)),
        pl.BlockSpec((FFS, D), lambda k, t: (k, 0)),
        SDS((FF, D), F32), 2, tt, job)
    return g.reshape(NCHIP, FF // NCHIP, D), job_out


def _dw_square(name, a, b, job=None):
    tt = min(DW_TOKENS, a.shape[0])
    g, job_out = _dw_call(
        name, a, b,
        pl.BlockSpec((tt, D), lambda k, t: (t, 0)), pl.BlockSpec((tt, D), lambda k, t: (t, 0)),
        pl.BlockSpec((D, D), lambda k, t: (0, 0)), SDS((D, D), F32), 1, tt, job)
    return g.reshape(NCHIP, D // NCHIP, D), job_out


def _place():
    x, y, c = lax.axis_index("x"), lax.axis_index("y"), lax.axis_index("c")
    return x, y, c, 2 * x + y


def _chip_at(x, y, s):
    return x ^ (s >> 1), y ^ (s & 1)


class _Job:
    def __init__(self, ins, out_shapes, sems, start, finish, aliases=None, mid=None):
        self.ins, self.out_shapes, self.sems = list(ins), list(out_shapes), list(sems)
        self.start, self.finish, self.aliases = start, finish, dict(aliases or {})
        self.mid = mid if mid is not None else (lambda ins, outs, sems: None)


def _join_jobs(*jobs):
    def cut(refs, sizes):
        out, at = [], 0
        for n in sizes:
            out.append(refs[at:at + n])
            at += n
        return out

    ni = [len(j.ins) for j in jobs]
    no = [len(j.out_shapes) for j in jobs]
    ns = [len(j.sems) for j in jobs]

    def run(which):
        def go(ins, outs, sems):
            for j, a, b, c in zip(jobs, cut(ins, ni), cut(outs, no), cut(sems, ns)):
                getattr(j, which)(a, b, c)
        return go

    aliases = {}
    for k, j in enumerate(jobs):
        for a, b in j.aliases.items():
            aliases[sum(ni[:k]) + a] = sum(no[:k]) + b
    return _Job([a for j in jobs for a in j.ins], [o for j in jobs for o in j.out_shapes],
                [s for j in jobs for s in j.sems], run("start"), run("finish"), aliases, run("mid"))


def _call(body, *, name, grid, in_specs, out_specs, out_shape, args, scratch_shapes=(), aliases=None,
          job=None, prefetch=None):
    n_in, n_out, n_scr = len(in_specs), len(out_specs), len(scratch_shapes)
    npf = 0 if prefetch is None else 1
    job = job if job is not None else _Job([], [], [], lambda *a: None, lambda *a: None)
    ji, jo = len(job.ins), len(job.out_shapes)
    steps = math.prod(grid)

    def wrapped(*refs):
        pf, refs = refs[:npf], refs[npf:]
        ins, jin = refs[:n_in], refs[n_in:n_in + ji]
        o0 = n_in + ji
        outs, jout = refs[o0:o0 + n_out], refs[o0 + n_out:o0 + n_out + jo]
        s0 = o0 + n_out + jo
        scr, jsem = refs[s0:s0 + n_scr], refs[s0 + n_scr:]
        step = functools.reduce(lambda acc, ag: acc * ag[1] + pl.program_id(ag[0]), enumerate(grid), 0)
        if ji or jo:
            @pl.when(step == 0)
            def _():
                job.start(jin, jout, jsem)

        body(*pf, *ins, *outs, *scr)

        if ji or jo:
            @pl.when(step == steps // 2)
            def _():
                job.mid(jin, jout, jsem)

            @pl.when(step == steps - 1)
            def _():
                job.finish(jin, jout, jsem)

    io = {npf + a: b for a, b in dict(aliases or {}).items()}
    io.update({npf + n_in + a: n_out + b for a, b in job.aliases.items()})
    kw = dict(in_specs=list(in_specs) + [ANY] * ji, out_specs=list(out_specs) + [ANY] * jo,
              scratch_shapes=list(scratch_shapes) + job.sems)
    if npf:
        kw = dict(grid_spec=pltpu.PrefetchScalarGridSpec(num_scalar_prefetch=1, grid=grid, **kw))
    else:
        kw["grid"] = grid
    res = pl.pallas_call(
        wrapped, name=name, out_shape=list(out_shape) + job.out_shapes, input_output_aliases=io,
        compiler_params=_cparams(has_side_effects=bool(ji or jo)), **kw,
    )(*(() if prefetch is None else (prefetch,)), *args, *job.ins)
    return list(res[:n_out]), list(res[n_out:])


def _cast_shards(name, place, ws, paired=False):
    n = len(ws)
    rows, cols = ws[0].shape
    tr = 352 if rows % 352 == 0 else 256
    shape = (2, rows, 2 * cols) if paired else (NCHIP, rows, cols)
    mine = (lambda i, pc: (pc[0] // 2, i, pc[0] % 2)) if paired else (lambda i, pc: (pc[0], i, 0))

    def body(pc_ref, *refs):
        del pc_ref
        for w_ref, o_ref in zip(refs[:n], refs[n:]):
            o_ref[...] = w_ref[...].astype(BF16)

    return pl.pallas_call(
        body, name=name,
        grid_spec=pltpu.PrefetchScalarGridSpec(
            num_scalar_prefetch=1, grid=(rows // tr,),
            in_specs=[pl.BlockSpec((tr, cols), lambda i, pc: (i, 0))] * n,
            out_specs=[pl.BlockSpec((None, tr, cols), mine)] * n),
        out_shape=[SDS(shape, BF16)] * n,
        compiler_params=_cparams(),
    )(place, *ws)


def _sibling_copy(ref, send_sem, recv_sem):
    x, y, c, _ = _place()
    return pltpu.make_async_remote_copy(src_ref=ref, dst_ref=ref, send_sem=send_sem, recv_sem=recv_sem,
                                        device_id=(x, y, 1 - c), device_id_type=MESH)


def _slot(arr, chip):
    if arr.shape[0] == NCHIP:
        return arr.at[chip]
    cols = arr.shape[2] // 2
    return arr.at[chip // 2, :, pl.ds(pl.multiple_of((chip % 2) * cols, 128), cols)]


def _half_rows(arr, slot, core):
    half = arr.shape[1] // 2
    return _slot(arr, slot).at[pl.ds(pl.multiple_of(core * half, 16), half)]


def _quarter_rows(arr, slot, core, q):
    quarter = arr.shape[1] // 4
    return _slot(arr, slot).at[pl.ds(pl.multiple_of((2 * core + q) * quarter, 16), quarter)]


def _chip_copy(ref, dist, send_sem, recv_sem):
    x, y, c, _ = _place()
    cx, cy = _chip_at(x, y, dist)
    return pltpu.make_async_remote_copy(src_ref=ref, dst_ref=ref, send_sem=send_sem, recv_sem=recv_sem,
                                        device_id=(cx, cy, c), device_id_type=MESH)


def _gather_sems(n):
    dma = pltpu.SemaphoreType.DMA
    return [dma((n, 2))] * 4 + [dma((n, 4))] * 2


def _gather_start(arrs, sems):
    dsend, drecv = sems[0], sems[1]
    _, _, c, j = _place()
    for w, arr in enumerate(arrs):
        for dist in (1, 2):
            _chip_copy(_half_rows(arr, j, c), dist, dsend.at[w, dist - 1], drecv.at[w, dist - 1]).start()


def _gather_land(arrs, sems, dist, first=0):
    dsend, drecv, rsend, rrecv, fsend, frecv = sems
    _, _, c, j = _place()
    if dist < 3:
        other = 3 - dist
        for w, arr in enumerate(arrs, first):
            landed = _half_rows(arr, j ^ dist, c)
            _chip_copy(landed, dist, dsend.at[w, dist - 1], drecv.at[w, dist - 1]).wait_recv()
            relay = _quarter_rows(arr, j ^ dist, c, other - 1)
            _chip_copy(relay, other, rsend.at[w, other - 1], rrecv.at[w, other - 1]).start()
            _sibling_copy(landed, fsend.at[w, dist - 1], frecv.at[w, dist - 1]).start()
        for w, arr in enumerate(arrs, first):
            theirs = _half_rows(arr, j ^ dist, 1 - c)
            _sibling_copy(theirs, fsend.at[w, dist - 1], frecv.at[w, dist - 1]).wait_recv()
    else:
        for w, arr in enumerate(arrs, first):
            for via in (1, 2):
                piece = _quarter_rows(arr, j ^ 3, c, via - 1)
                _chip_copy(piece, via, rsend.at[w, via - 1], rrecv.at[w, via - 1]).wait_recv()
                _sibling_copy(piece, fsend.at[w, 1 + via], frecv.at[w, 1 + via]).start()
        for w, arr in enumerate(arrs, first):
            for via in (1, 2):
                theirs = _quarter_rows(arr, j ^ 3, 1 - c, via - 1)
                _sibling_copy(theirs, fsend.at[w, 1 + via], frecv.at[w, 1 + via]).wait_recv()


def _gather_drain(arrs, sems):
    dsend, drecv, rsend, rrecv, fsend, frecv = sems
    _, _, c, j = _place()
    for w, arr in enumerate(arrs):
        for dist in (1, 2):
            other = 3 - dist
            _chip_copy(_half_rows(arr, j, c), dist, dsend.at[w, dist - 1], drecv.at[w, dist - 1]).wait_send()
            _chip_copy(_quarter_rows(arr, j ^ dist, c, other - 1), other,
                       rsend.at[w, other - 1], rrecv.at[w, other - 1]).wait_send()
            _sibling_copy(_half_rows(arr, j ^ dist, c), fsend.at[w, dist - 1], frecv.at[w, dist - 1]).wait_send()
            _sibling_copy(_quarter_rows(arr, j ^ 3, c, dist - 1),
                          fsend.at[w, 1 + dist], frecv.at[w, 1 + dist]).wait_send()


def _gather_neighbours(arrs, sems):
    _gather_land(arrs, sems, 1)
    _gather_land(arrs, sems, 2)


def _gather_finish(arrs, sems):
    _gather_land(arrs, sems, 3)
    _gather_drain(arrs, sems)


def _gather_job(arrs):
    n = len(arrs)
    return _Job(arrs, [SDS(a.shape, a.dtype) for a in arrs], _gather_sems(n),
                lambda ins, outs, sems: _gather_start(outs, sems),
                lambda ins, outs, sems: _gather_finish(outs, sems), {k: k for k in range(n)},
                mid=lambda ins, outs, sems: _gather_neighbours(outs, sems))


def _exchange_job(arrs, out_shapes, n, copies):
    def start(ins, outs, sems):
        for cp in copies(ins, outs, sems[0], sems[1]):
            cp.start()

    def finish(ins, outs, sems):
        for cp in copies(ins, outs, sems[0], sems[1]):
            cp.wait()

    return _Job(arrs, out_shapes, [pltpu.SemaphoreType.DMA((n,))] * 2, start, finish)


def _pair_exchange_job(grads):
    def copies(ins, outs, send_sem, recv_sem):
        x, y, c, _ = _place()
        res = []
        for w in range(len(grads)):
            half = ins[w].shape[1] // 2
            theirs = pl.ds(pl.multiple_of((1 - c) * half, 8), half)
            res.append(pltpu.make_async_remote_copy(
                src_ref=ins[w].at[:, theirs, :], dst_ref=outs[w], send_sem=send_sem.at[w],
                recv_sem=recv_sem.at[w], device_id=(x, y, 1 - c), device_id_type=MESH))
        return res

    return _exchange_job(grads, [SDS((NCHIP, g.shape[1] // 2, g.shape[2]), F32) for g in grads],
                         len(grads), copies)


def _row_tile(rows, cols):
    tr = rows
    while tr * cols * 4 > ELEMENTWISE_BLOCK_BYTES and tr % 32 == 0:
        tr //= 2
    return tr


def _pair_sums(name, place, gs, sibs):
    n = len(gs)
    half, cols = sibs[0].shape[1], sibs[0].shape[2]
    tr = _row_tile(half, cols)
    nt = half // tr
    mine = nt if gs[0].shape[1] == 2 * half else 0

    def body(pc_ref, *refs):
        del pc_ref
        for g_ref, s_ref, own_ref, out_ref in zip(refs[:n], refs[n:2 * n], refs[2 * n:3 * n], refs[3 * n:]):
            v = g_ref[...] + s_ref[...]
            out_ref[...] = v.astype(BF16)

            @pl.when(pl.program_id(1) == 0)
            def _():
                own_ref[...] = v

    res = pl.pallas_call(
        body, name=name,
        grid_spec=pltpu.PrefetchScalarGridSpec(
            num_scalar_prefetch=1, grid=(nt, NCHIP),
            in_specs=[pl.BlockSpec((None, tr, cols), lambda i, s, pc: (pc[0] ^ s, pc[1] * mine + i, 0))] * n
            + [pl.BlockSpec((None, tr, cols), lambda i, s, pc: (pc[0] ^ s, i, 0))] * n,
            out_specs=[pl.BlockSpec((tr, cols), lambda i, s, pc: (i, 0))] * n
            + [pl.BlockSpec((None, tr, cols), lambda i, s, pc: (s, i, 0))] * n),
        out_shape=[SDS((half, cols), F32)] * n + [SDS((NCHIP, half, cols), BF16)] * n,
        compiler_params=_cparams(),
    )(place, *gs, *sibs)
    return res[:n], res[n:]


def _chip_exchange_job(parts):
    def copies(ins, outs, send_sem, recv_sem):
        x, y, c, _ = _place()
        res = []
        for w in range(len(parts)):
            for s in range(1, NCHIP):
                cx, cy = _chip_at(x, y, s)
                k = w * (NCHIP - 1) + s - 1
                res.append(pltpu.make_async_remote_copy(
                    src_ref=ins[w].at[s], dst_ref=outs[w].at[s - 1], send_sem=send_sem.at[k],
                    recv_sem=recv_sem.at[k], device_id=(cx, cy, c), device_id_type=MESH))
        return res

    return _exchange_job(parts, [SDS((NCHIP - 1,) + p.shape[1:], BF16) for p in parts],
                         len(parts) * (NCHIP - 1), copies)


def _chip_sums(name, owns, rems):
    n = len(owns)
    half, cols = owns[0].shape
    tr = _row_tile(half, cols)

    def body(*refs):
        for own_ref, rem_ref, out_ref in zip(refs[:n], refs[n:2 * n], refs[2 * n:]):
            out_ref[...] = (((own_ref[...] + rem_ref[0].astype(F32)) + rem_ref[1].astype(F32))
                            + rem_ref[2].astype(F32))

    return pl.pallas_call(
        body, name=name, grid=(half // tr,),
        in_specs=[pl.BlockSpec((tr, cols), lambda i: (i, 0))] * n
        + [pl.BlockSpec((NCHIP - 1, tr, cols), lambda i: (0, i, 0))] * n,
        out_specs=[pl.BlockSpec((tr, cols), lambda i: (i, 0))] * n,
        out_shape=[SDS((half, cols), F32)] * n,
        compiler_params=_cparams(),
    )(*owns, *rems)


def _share_halves_job(halves):
    def copies(ins, outs, send_sem, recv_sem):
        x, y, c, _ = _place()
        return [pltpu.make_async_remote_copy(
            src_ref=ins[w], dst_ref=outs[w], send_sem=send_sem.at[w], recv_sem=recv_sem.at[w],
            device_id=(x, y, 1 - c), device_id_type=MESH) for w in range(len(halves))]

    return _exchange_job(halves, [SDS(h.shape, F32) for h in halves], len(halves), copies)


def _adamw_math(w, g, m, v):
    m = B1 * m + (1.0 - B1) * g
    v = B2 * v + (1.0 - B2) * (g * g)
    m_hat = m / (1.0 - B1 ** STEP)
    v_hat = v / (1.0 - B2 ** STEP)
    delta = -LR * (m_hat / (jnp.sqrt(v_hat) + AEPS) + WD * w)
    return delta, m, v


def _adamws(name, place, ws, owns, sibs, ms, vs):
    n = len(ws)
    rows, cols = ws[0].shape
    by_cols = owns[0].shape[0] == rows
    half, pc_cols = (rows, cols // 2) if by_cols else (rows // 2, cols)
    tr = _row_tile(half, pc_cols)
    nt = half // tr

    def body(pc_ref, *refs):
        ins, outs = refs[:5 * n], refs[5 * n:]
        for k in range(n):
            w_ref, own_ref, sib_ref, m_ref, v_ref = ins[5 * k:5 * k + 5]
            g = jnp.where(pl.program_id(0) == pc_ref[1], own_ref[...], sib_ref[...])
            d, mn, vn = _adamw_math(w_ref[...], g, m_ref[...], v_ref[...])
            for ref, val in zip(outs[4 * k:4 * k + 4], (g, d, mn, vn)):
                ref[...] = val

    full = pl.BlockSpec((tr, pc_cols), (lambda h, i, pc: (i, h)) if by_cols else (lambda h, i, pc: (h * nt + i, 0)))
    part = pl.BlockSpec((tr, pc_cols), lambda h, i, pc: (i, 0))
    res = pl.pallas_call(
        body, name=name,
        grid_spec=pltpu.PrefetchScalarGridSpec(
            num_scalar_prefetch=1, grid=(2, nt),
            in_specs=[full, part, part, full, full] * n, out_specs=[full] * (4 * n)),
        out_shape=[SDS((rows, cols), F32)] * (4 * n),
        compiler_params=_cparams(),
    )(place, *[a for group in zip(ws, owns, sibs, ms, vs) for a in group])
    return [tuple(res[4 * k:4 * k + 4]) for k in range(n)]


def _small_allreduce_adamw(sp, w, m, v, job):
    shape = sp.shape
    ji, jo = len(job.ins), len(job.out_shapes)

    def body(sp_ref, w_ref, m_ref, v_ref, *rest):
        jin, (g_ref, d_ref, mo_ref, vo_ref), jout = rest[:ji], rest[ji:ji + 4], rest[ji + 4:ji + 4 + jo]
        sib_s, pair_s, chip_s, send_sem, recv_sem = rest[ji + 4 + jo:ji + 9 + jo]
        jsem = rest[ji + 9 + jo:]
        job.start(jin, jout, jsem)
        x, y, c, j = _place()
        cp = pltpu.make_async_remote_copy(
            src_ref=sp_ref, dst_ref=sib_s, send_sem=send_sem.at[0], recv_sem=recv_sem.at[0],
            device_id=(x, y, 1 - c), device_id_type=MESH)
        cp.start()
        cp.wait()
        pair_s[...] = sp_ref[...] + sib_s[...]
        half = shape[0] // 2
        mine = pl.ds(pl.multiple_of(c * half, 8), half)
        cps = []
        for s in range(1, NCHIP):
            cx, cy = _chip_at(x, y, s)
            cp = pltpu.make_async_remote_copy(
                src_ref=pair_s.at[mine], dst_ref=chip_s.at[s, mine], send_sem=send_sem.at[s],
                recv_sem=recv_sem.at[s], device_id=(cx, cy, c), device_id_type=MESH)
            cp.start()
            cps.append(cp)
        chip_s[0] = pair_s[...]
        for cp in cps:
            cp.wait()
        cps = []
        for s in range(1, NCHIP):
            cp = pltpu.make_async_remote_copy(
                src_ref=chip_s.at[s, mine], dst_ref=chip_s.at[s, mine], send_sem=send_sem.at[NCHIP + s],
                recv_sem=recv_sem.at[NCHIP + s], device_id=(x, y, 1 - c), device_id_type=MESH)
            cp.start()
            cps.append(cp)
        for cp in cps:
            cp.wait()
        tot = chip_s[j]
        for k in range(1, NCHIP):
            tot = tot + chip_s[k ^ j]
        g_ref[...] = tot
        d, mn, vn = _adamw_math(w_ref[...], tot, m_ref[...], v_ref[...])
        d_ref[...] = d
        mo_ref[...] = mn
        vo_ref[...] = vn
        job.mid(jin, jout, jsem)
        job.finish(jin, jout, jsem)

    vm = pl.BlockSpec(memory_space=pltpu.VMEM)
    res = pl.pallas_call(
        body, name="small_allreduce_adamw",
        in_specs=[vm] * 4 + [ANY] * ji, out_specs=[vm] * 4 + [ANY] * jo,
        out_shape=[SDS(shape, F32)] * 4 + job.out_shapes,
        scratch_shapes=[pltpu.VMEM(shape, F32), pltpu.VMEM(shape, F32), pltpu.VMEM((NCHIP,) + shape, F32),
                        pltpu.SemaphoreType.DMA((2 * NCHIP,)), pltpu.SemaphoreType.DMA((2 * NCHIP,))] + job.sems,
        input_output_aliases={4 + a: 4 + b for a, b in job.aliases.items()},
        compiler_params=pltpu.CompilerParams(has_side_effects=True),
    )(sp, w, m, v, *job.ins)
    return res[:4], res[4:]


def _pack_small(first, mix, ln_g, ln_b, b_s, lbt, hn, ffn, fin, w_s):
    rows = [first.reshape(1, D), mix.reshape(1, D), ln_g.reshape(1, D), ln_b.reshape(1, D),
            b_s.reshape(1, D), lbt.reshape(2, D), hn.reshape(1, D), ffn.reshape(1, D), fin.reshape(1, D),
            jnp.zeros((6, D), F32)]
    return jnp.concatenate(rows + [w_s.reshape(NG, GCH, GCH).transpose(1, 0, 2).reshape(GCH, D)], axis=0)


def _unpack_small(p):
    w_s = p[16:].reshape(GCH, NG, GCH).transpose(1, 0, 2).reshape(1, NG, GCH, GCH)
    return dict(norm_mix_g=p[1:2], gmlp_ln_g=p[2:3], gmlp_ln_b=p[3:4], gmlp_b_s=p[4].reshape(1, NG, GCH),
                hgrn_lb_table=p[5:7], hgrn_norm_g=p[7:8], norm_ffn_g=p[8:9], norm_final_g=p[9],
                gmlp_w_s=w_s)


SMALL = ("norm_mix_g", "gmlp_ln_g", "gmlp_ln_b", "gmlp_w_s", "gmlp_b_s", "hgrn_lb_table", "hgrn_norm_g",
         "norm_ffn_g", "norm_final_g")
BIG = ("w_in", "w_gate_up", "w_branch_a", "w_branch_b", "w_out", "w_down")
ORDER = ("norm_mix_g", "w_in", "gmlp_ln_g", "gmlp_ln_b", "gmlp_w_s", "gmlp_b_s", "hgrn_lb_table",
         "hgrn_norm_g", "w_branch_a", "w_branch_b", "w_out", "norm_ffn_g", "w_gate_up", "w_down",
         "norm_final_g")


def kernel(x, norm_mix_g, w_in, gmlp_ln_g, gmlp_ln_b, gmlp_w_s, gmlp_b_s, hgrn_lb_table, hgrn_norm_g, w_branch_a, w_branch_b, w_out, norm_ffn_g, w_gate_up, w_down, norm_final_g, loss_target, m_norm_mix_g, m_w_in, m_gmlp_ln_g, m_gmlp_ln_b, m_gmlp_w_s, m_gmlp_b_s, m_hgrn_lb_table, m_hgrn_norm_g, m_w_branch_a, m_w_branch_b, m_w_out, m_norm_ffn_g, m_w_gate_up, m_w_down, m_norm_final_g, v_norm_mix_g, v_w_in, v_gmlp_ln_g, v_gmlp_ln_b, v_gmlp_w_s, v_gmlp_b_s, v_hgrn_lb_table, v_hgrn_norm_g, v_w_branch_a, v_w_branch_b, v_w_out, v_norm_ffn_g, v_w_gate_up, v_w_down, v_norm_final_g):
    args = dict(locals())
    T = x.shape[1]
    xs = x.reshape(T, D)
    target = loss_target.reshape(T, D)
    big = {n: args[n].reshape(args[n].shape[1:]) for n in BIG}
    big_m = {n: args["m_" + n].reshape(args[n].shape[1:]) for n in BIG}
    big_v = {n: args["v_" + n].reshape(args[n].shape[1:]) for n in BIG}

    x_i, y_i, c_i = lax.axis_index("x"), lax.axis_index("y"), lax.axis_index("c")
    place = jnp.stack([2 * x_i + y_i, c_i]).astype(jnp.int32)
    def by_shape(names):
        groups = []
        for n in names:
            if groups and big[groups[-1][0]].shape == big[n].shape:
                groups[-1].append(n)
            else:
                groups.append([n])
        return groups

    cast = {}
    for grp in by_shape(BIG):
        cast.update(zip(grp, _cast_shards("cast_" + grp[0], place, [big[n] for n in grp],
                                          paired=grp[0] == "w_gate_up")))
    tril = jnp.tril(jnp.ones((GCH, GCH), bool))
    wm = jnp.where(tril, gmlp_w_s[0], 0.0).astype(BF16)
    wm_t = jnp.swapaxes(wm, 1, 2)
    b_t = gmlp_b_s[0].T

    (proj, hb), w_in4, (w_a4, w_b4, w_out4, w_down4) = _proj_fwd(
        place, xs, norm_mix_g, cast["w_in"], [cast[n] for n in ("w_branch_a", "w_branch_b", "w_out", "w_down")])
    (ab,), _ = _gmlp_fwd(proj, gmlp_ln_g, gmlp_ln_b, wm, b_t)
    (o_raw, obb, st_before), (w_gu,) = _hgrn_fwd(
        proj, hgrn_lb_table, hgrn_norm_g, job=_gather_job([cast["w_gate_up"]]))
    w_a, w_b, w_o = (w.reshape(D, D) for w in (w_a4, w_b4, w_out4))
    (mgb, x1), _ = _merge_fwd(xs, ab, obb, proj, w_a, w_b, w_o)
    w_dn = w_down4.reshape(FF, D)
    act, dx2b, h2b, dgu, dx1, dx1b, acc_ffn = _ffn_fwd_bwd(
        x1, target, norm_ffn_g, norm_final_g.reshape(1, D), w_gu, w_dn)

    grads, owns, parts, halves, sibh = {}, {}, {}, {}, {}

    def pair_sums(names, sibs):
        sib_of = dict(zip(names, sibs))
        for grp in by_shape(names):
            o, p = _pair_sums("rs_pair_sum_" + grp[0], place, [grads[n] for n in grp], [sib_of[n] for n in grp])
            owns.update(zip(grp, o))
            parts.update(zip(grp, p))

    def chip_sums(names, got):
        rem_of = dict(zip(names, got))
        for grp in by_shape(names):
            h = _chip_sums("rs_chip_sum_" + grp[0], [owns[n] for n in grp], [rem_of[n] for n in grp])
            halves.update(zip(grp, h))

    ffn, mix = ("w_gate_up", "w_down"), ("w_branch_a", "w_branch_b", "w_out")
    grads["w_gate_up"], _ = _dw_gate_up(h2b, dgu)
    grads["w_down"], _ = _dw_down(act, dx2b)
    (dya, dyb, dproj), got = _merge_bwd(
        dx1b, ab, obb, proj, w_o, w_a, w_b, job=_pair_exchange_job([grads[n] for n in ffn]))
    pair_sums(ffn, got)
    grads["w_branch_a"], _ = _dw_square("dw_branch_a", ab, dya)
    grads["w_branch_b"], _ = _dw_square("dw_branch_b", obb, dyb)
    grads["w_out"], _ = _dw_square("dw_out", mgb, dx1b)
    (dproj, acc_hgrn), got = _hgrn_bwd(
        dproj, dyb, w_b, o_raw, proj, st_before, hgrn_lb_table, hgrn_norm_g,
        job=_join_jobs(_chip_exchange_job([parts[n] for n in ffn]), _pair_exchange_job([grads[n] for n in mix])))
    chip_sums(ffn, got[:2])
    pair_sums(mix, got[2:])
    dproj, acc_ln, dws, dmix = _gmlp_bwd(dproj, dya, w_a, proj, gmlp_ln_g, gmlp_ln_b, wm, wm_t, b_t)
    for_sibling, got = _dw_in_half(
        "dw_in_sibling_half", place, hb, dproj, False,
        job=_join_jobs(_share_halves_job([halves[n] for n in ffn]), _chip_exchange_job([parts[n] for n in mix])))
    sibh.update(zip(ffn, got[:2]))
    chip_sums(mix, got[2:])
    grads["w_in"], got = _dw_in_half(
        "dw_in_own_half", place, hb, dproj, True, job=_share_halves_job([for_sibling]))
    pair_sums(("w_in",), got)
    (grad_x, acc_mix), got = _proj_bwd(
        dproj, w_in4, xs, dx1, norm_mix_g,
        job=_join_jobs(_chip_exchange_job([parts["w_in"]]), _share_halves_job([halves[n] for n in mix])))
    chip_sums(("w_in",), got[:1])
    sibh.update(zip(mix, got[1:]))

    lbv = jax.nn.sigmoid(hgrn_lb_table[0] - hgrn_lb_table[1])
    d_t0 = jnp.sum(acc_hgrn[0], axis=0) * lbv * (1.0 - lbv)
    loss_row = jnp.zeros((D,), F32).at[0].set(jnp.sum(acc_ffn[0]))
    dws_m = jnp.where(tril[:, None, :], dws.reshape(GCH, NG, GCH), 0.0).transpose(1, 0, 2)
    db_s = jnp.sum(dmix.reshape(GCH, NG, GCH), axis=-1).T
    sp = _pack_small(loss_row, jnp.sum(acc_mix, 0), jnp.sum(acc_ln[0], 0), jnp.sum(acc_ln[1], 0), db_s,
                     jnp.stack([d_t0, -d_t0]), jnp.sum(acc_hgrn[1], 0), jnp.sum(acc_ffn[2], 0),
                     jnp.sum(acc_ffn[1], 0), dws_m)
    zero = jnp.zeros((D,), F32)

    def pack(prefix):
        a = lambda n: args[prefix + n]
        return _pack_small(zero, a("norm_mix_g"), a("gmlp_ln_g"), a("gmlp_ln_b"), a("gmlp_b_s"),
                           a("hgrn_lb_table"), a("hgrn_norm_g"), a("norm_ffn_g"), a("norm_final_g"),
                           a("gmlp_w_s"))

    packed, (sibh["w_in"],) = _small_allreduce_adamw(
        sp, pack(""), pack("m_"), pack("v_"), _share_halves_job([halves["w_in"]]))
    loss = packed[0][0, 0]
    small = [_unpack_small(p) for p in packed]
    out = {n: tuple(s[n] for s in small) for n in SMALL}
    for grp in by_shape(BIG):
        res = _adamws("adamw_" + grp[0], place, *[[d[n] for n in grp] for d in (big, halves, sibh, big_m, big_v)])
        for n, quad in zip(grp, res):
            out[n] = tuple(a.reshape(args[n].shape) for a in quad)
    return (loss, grad_x.reshape(x.shape), *[out[n][0] for n in ORDER], *[out[n][1] for n in ORDER],
            *[out[n][2] for n in ORDER], *[out[n][3] for n in ORDER])
```

```python
import functools
import math

import jax
import jax.numpy as jnp
from jax import lax
from jax.experimental import pallas as pl
from jax.experimental.pallas import tpu as pltpu

F32 = jnp.float32
BF16 = jnp.bfloat16
SDS = jax.ShapeDtypeStruct
MESH = pl.DeviceIdType.MESH
ANY = pl.BlockSpec(memory_space=pl.ANY)

D = 1024
NIN = 8
NG = 8
GCH = 128
NH = 8
HD = 128
HCH = 64
HGRN_HB = 8
HGRN_TOKENS = 256
GMLP_FWD_TOKENS = 512
GMLP_BWD_TOKENS = 256
HW = HGRN_HB * HD
DW_TOKENS = 2048
ELEMENTWISE_BLOCK_BYTES = 2 * 1024 * 1024
PROJ_OUT_SLOTS = 4
FF = 2816
FFS = 1408
NCHIP = 4
EPS = 1e-6
QSCALE = HD ** -0.5
GELU_C0 = math.sqrt(2.0 / math.pi)
GELU_C1 = 0.044715
LR, B1, B2, AEPS, WD, STEP = 0.001, 0.9, 0.999, 1e-08, 0.01, 10
VMEM_LIMIT_V7X = 56 * 1024 * 1024
SP_ROWS = 144


def _cparams(**kw):
    return pltpu.CompilerParams(vmem_limit_bytes=VMEM_LIMIT_V7X, **kw)


def _mm(a, b):
    return jnp.dot(a, b, preferred_element_type=F32)


def _mm_nt(a, b):
    return lax.dot_general(a, b, (((1,), (1,)), ((), ())), preferred_element_type=F32)


def _mm_tn(a, b):
    return lax.dot_general(a, b, (((0,), (0,)), ((), ())), preferred_element_type=F32)


def _rows8(x):
    r, c = x.shape
    return jnp.sum(x.reshape(r // 8, 8, c), axis=0)


def _mean(x):
    return jnp.mean(x, axis=-1, keepdims=True)


def _sigmoid(x):
    return 1.0 / (1.0 + jnp.exp(-x))


def _gelu(x):
    t = jnp.tanh(GELU_C0 * (x + GELU_C1 * x * x * x))
    return 0.5 * x * (1.0 + t), t


def _gelu_grad(x, t):
    return 0.5 * (1.0 + t) + 0.5 * x * (1.0 - t * t) * (GELU_C0 * (1.0 + 3.0 * GELU_C1 * x * x))


def _component_of(group):
    return jnp.where(group < 6, (group + 4) % 6, group)


def _proj_fwd(place, x, g_mix, w_in4, later):
    T = x.shape[0]
    tm = min(1024, T)
    ni = T // tm
    n = len(later)

    def body(pc_ref, x_ref, g_ref, *rest):
        proj_ref, h_ref, w_all = rest[1 + n:4 + n]
        gathered = rest[4 + n:4 + 2 * n]
        hs, wbuf, wsem, obuf, osem = rest[4 + 2 * n:9 + 2 * n]
        w_sems, later_sems = rest[9 + 2 * n:15 + 2 * n], rest[15 + 2 * n:]
        jp, i = pl.program_id(0), pl.program_id(1)
        w_cols = [w_all.at[:, :, pl.ds(k * D, D)] for k in range(2)]

        def w_copy(blk):
            cols = pl.ds(pl.multiple_of((blk % 2) * D, 128), D)
            return pltpu.make_async_copy(w_all.at[pc_ref[0] ^ (blk // 2), :, cols], wbuf.at[blk % 2],
                                         wsem.at[blk % 2])

        @pl.when((jp == 0) & (i == 0))
        def _():
            _gather_start(w_cols, w_sems)
            w_copy(jp).start()

        @pl.when(i == 0)
        def _():
            w_copy(jp).wait()

        @pl.when(jp == 0)
        def _():
            xv = x_ref[...]
            r = lax.rsqrt(_mean(xv * xv) + EPS)
            hb = (xv * r * g_ref[...]).astype(BF16)
            hs[i] = hb
            h_ref[...] = hb

        step = jp * ni + i
        slot = step % PROJ_OUT_SLOTS

        def o_copy(slot_):
            comp = 2 * (pc_ref[0] ^ (jp // 2)) + jp % 2
            return pltpu.make_async_copy(
                obuf.at[slot_], proj_ref.at[comp, pl.ds(pl.multiple_of(i * tm, 8), tm)], osem.at[slot_])

        @pl.when(step >= PROJ_OUT_SLOTS)
        def _():
            o_copy(slot).wait()

        obuf[slot] = _mm(hs[i], wbuf[jp % 2])
        o_copy(slot).start()

        @pl.when(step == NIN * ni - 1)
        def _():
            for k in range(PROJ_OUT_SLOTS):
                o_copy((slot + 1 + k) % PROJ_OUT_SLOTS).wait()

        for nxt in range(1, NIN):
            @pl.when((jp == nxt - 1) & (i == ni - 1))
            def _():
                if nxt >= 2:
                    _gather_land([w_cols[nxt % 2]], w_sems, nxt // 2, first=nxt % 2)
                if nxt == 5:
                    _gather_start(gathered, later_sems)
                if nxt == NIN - 1:
                    _gather_neighbours(gathered, later_sems)
                w_copy(jp + 1).start()

        @pl.when((jp == NIN - 1) & (i == ni - 1))
        def _():
            _gather_drain(w_cols, w_sems)
            _gather_finish(gathered, later_sems)

    tile = lambda jp, i, pc: (jnp.where(jp == 0, i, ni - 1), 0)
    res = pl.pallas_call(
        body, name="proj_fwd",
        grid_spec=pltpu.PrefetchScalarGridSpec(
            num_scalar_prefetch=1, grid=(NIN, ni),
            in_specs=[pl.BlockSpec((tm, D), tile), pl.BlockSpec((1, D), lambda jp, i, pc: (0, 0))] + [ANY] * (1 + n),
            out_specs=[ANY, pl.BlockSpec((tm, D), tile)] + [ANY] * (1 + n),
            scratch_shapes=[pltpu.VMEM((ni, tm, D), BF16), pltpu.VMEM((2, D, D), BF16),
                            pltpu.SemaphoreType.DMA((2,)), pltpu.VMEM((PROJ_OUT_SLOTS, tm, D), F32),
                            pltpu.SemaphoreType.DMA((PROJ_OUT_SLOTS,))] + _gather_sems(2) + _gather_sems(n)),
        out_shape=[SDS((NIN, T, D), F32), SDS((T, D), BF16), SDS(w_in4.shape, BF16)]
        + [SDS(a.shape, a.dtype) for a in later],
        input_output_aliases={3 + k: 2 + k for k in range(1 + n)},
        compiler_params=_cparams(has_side_effects=True),
    )(place, x, g_mix, w_in4, *later)
    return res[:2], res[2], res[3:]


def _chunks_abreast(x):
    return jnp.concatenate([x[GCH * ch:GCH * (ch + 1)] for ch in range(x.shape[0] // GCH)], axis=1)


def _chunks_stacked(x):
    return jnp.concatenate([x[:, GCH * ch:GCH * (ch + 1)] for ch in range(x.shape[1] // GCH)], axis=0)


def _layer_norm_stats(gv):
    mu = _mean(gv)
    xc = gv - mu
    rs = lax.rsqrt(_mean(xc * xc) + EPS)
    return xc * rs, rs


def _gmlp_fwd(proj, ln_g, ln_b, wm, b_t, job=None):
    T = proj.shape[1]
    tm = min(GMLP_FWD_TOKENS, T)

    def body(u_ref, v_ref, lg_ref, lb_ref, wm_ref, bt_ref, a_ref, a_s):
        gu, _ = _gelu(u_ref[...])
        gv, _ = _gelu(v_ref[...])
        vhat, _ = _layer_norm_stats(gv)
        vnb = (vhat * lg_ref[...] + lb_ref[...]).astype(BF16)
        for g in range(NG):
            cols = slice(128 * g, 128 * (g + 1))
            mixed = _mm(wm_ref[g], _chunks_abreast(vnb[:, cols])) + bt_ref[:, g:g + 1]
            a_s[:, cols] = gu[:, cols] * _chunks_stacked(mixed)
        a_ref[...] = a_s[...].astype(BF16)

    row = lambda i: (0, 0)
    return _call(
        body, name="gmlp_fwd", grid=(T // tm,), job=job, args=(proj, proj, ln_g, ln_b, wm, b_t),
        in_specs=[pl.BlockSpec((None, tm, D), lambda i: (0, i, 0)), pl.BlockSpec((None, tm, D), lambda i: (1, i, 0)),
                  pl.BlockSpec((1, D), row), pl.BlockSpec((1, D), row),
                  pl.BlockSpec((NG, GCH, GCH), lambda i: (0, 0, 0)), pl.BlockSpec((GCH, NG), row)],
        out_specs=[pl.BlockSpec((tm, D), lambda i: (i, 0))],
        out_shape=[SDS((T, D), BF16)],
        scratch_shapes=[pltpu.VMEM((tm, D), F32)])


def _cumsum64(x, row):
    for s in (1, 2, 4, 8, 16, 32):
        x = x + jnp.where(row >= s, pltpu.roll(x, s, 0), 0.0)
    return x


def _revcumsum64(x, row):
    n = x.shape[0]
    for s in (1, 2, 4, 8, 16, 32):
        x = x + jnp.where(row < HCH - s, pltpu.roll(x, n - s, 0), 0.0)
    return x


def _head_mean(x):
    parts = [jnp.broadcast_to(_mean(x[:, HD * h:HD * (h + 1)]), (x.shape[0], HD)) for h in range(x.shape[1] // HD)]
    return jnp.concatenate(parts, axis=1)


def _seg_sum(x):
    n, c = x.shape
    s = jnp.sum(x.reshape(n // HCH, HCH, c), axis=1, keepdims=True)
    return jnp.broadcast_to(s, (n // HCH, HCH, c)).reshape(n, c)


def _hgrn_gates(fl, lbv, row):
    s = _sigmoid(fl)
    f = lbv + (1.0 - lbv) * s
    a = _cumsum64(jnp.log(f), row)
    a_mid = _seg_sum(jnp.where(row == HCH // 2 - 1, a, 0.0))
    a_last = _seg_sum(jnp.where(row == HCH - 1, a, 0.0))
    return s, f, a, a_mid, a_last


def _hgrn_fwd(proj, lb_table, norm_g, job=None):
    T = proj.shape[1]
    tb = min(HGRN_TOKENS, T)
    nc = tb // HCH

    def body(q_ref, fl_ref, v_ref, g_ref, lbt_ref, gn_ref, o_ref, ob_ref, stb_ref, st_s, o_s):
        @pl.when(pl.program_id(1) == 0)
        def _():
            st_s[...] = jnp.zeros_like(st_s)

        row = lax.broadcasted_iota(jnp.int32, (tb, HW), 0) & (HCH - 1)
        lbv = _sigmoid(lbt_ref[0:1, :] - lbt_ref[1:2, :])
        _, f, a, a_mid, a_last = _hgrn_gates(fl_ref[...], lbv, row)
        k = 1.0 - f
        qs = q_ref[...] * QSCALE
        q_in = (qs * jnp.exp(a - a_mid)).astype(BF16)
        k_in = (k * jnp.exp(a_mid - a)).astype(BF16)
        q_a = (qs * jnp.exp(a)).astype(BF16)
        k_d = (k * jnp.exp(a_last - a)).astype(BF16)
        dec = jnp.exp(a_last)
        vb = v_ref[...].astype(BF16)
        tri = (lax.broadcasted_iota(jnp.int32, (HCH, HCH), 0)
               >= lax.broadcasted_iota(jnp.int32, (HCH, HCH), 1))
        for c in range(nc):
            sl = slice(HCH * c, HCH * (c + 1))
            for hh in range(HGRN_HB):
                hs = slice(HD * hh, HD * (hh + 1))
                st = st_s[hh]
                stb_ref[hh, c] = st
                sc = jnp.where(tri, _mm_nt(q_in[sl, hs], k_in[sl, hs]), 0.0)
                o_s[sl, hs] = _mm(sc.astype(BF16), vb[sl, hs]) + _mm_nt(q_a[sl, hs], st.astype(BF16))
                d64 = dec[sl, hs]
                st_s[hh] = st * jnp.concatenate([d64, d64], axis=0) + _mm_tn(vb[sl, hs], k_d[sl, hs])
        o = o_s[...]
        r = lax.rsqrt(_head_mean(o * o) + EPS)
        g = g_ref[...]
        o_ref[...] = o
        ob_ref[...] = (o * r * gn_ref[...] * (g * _sigmoid(g))).astype(BF16)

    def col(off):
        return pl.BlockSpec((None, tb, HW), lambda h, cb: (off, cb, h))

    return _call(
        body, name="hgrn_fwd", grid=(NH // HGRN_HB, T // tb), job=job,
        args=(proj, proj, proj, proj, lb_table, norm_g),
        in_specs=[col(2), col(3), col(4), col(5),
                  pl.BlockSpec((2, HW), lambda h, cb: (0, h)), pl.BlockSpec((1, HW), lambda h, cb: (0, h))],
        out_specs=[pl.BlockSpec((tb, HW), lambda h, cb: (cb, h)), pl.BlockSpec((tb, HW), lambda h, cb: (cb, h)),
                   pl.BlockSpec((HGRN_HB, nc, HD, HD), lambda h, cb: (h, cb, 0, 0))],
        out_shape=[SDS((T, D), F32), SDS((T, D), BF16), SDS((NH, T // HCH, HD, HD), F32)],
        scratch_shapes=[pltpu.VMEM((HGRN_HB, HD, HD), F32), pltpu.VMEM((tb, HW), F32)])


def _merge_fwd(x, ab, ob, proj, w_a, w_b, w_out, job=None):
    T = x.shape[0]
    tm = min(512, T)

    def body(x_ref, ab_ref, ob_ref, ga_ref, gb_ref, wa_ref, wb_ref, wo_ref, mg_ref, x1_ref):
        ya = _mm(ab_ref[...], wa_ref[...])
        yb = _mm(ob_ref[...], wb_ref[...])
        merged = (_sigmoid(ga_ref[...]) * ya + _sigmoid(gb_ref[...]) * yb).astype(BF16)
        mg_ref[...] = merged
        x1_ref[...] = x_ref[...] + _mm(merged, wo_ref[...])

    t = lambda i: (i, 0)
    w = lambda i: (0, 0)
    return _call(
        body, name="merge_fwd", grid=(T // tm,), job=job, args=(x, ab, ob, proj, proj, w_a, w_b, w_out),
        in_specs=[pl.BlockSpec((tm, D), t), pl.BlockSpec((tm, D), t), pl.BlockSpec((tm, D), t),
                  pl.BlockSpec((None, tm, D), lambda i: (6, i, 0)), pl.BlockSpec((None, tm, D), lambda i: (7, i, 0)),
                  pl.BlockSpec((D, D), w), pl.BlockSpec((D, D), w), pl.BlockSpec((D, D), w)],
        out_specs=[pl.BlockSpec((tm, D), t)] * 2,
        out_shape=[SDS((T, D), BF16), SDS((T, D), F32)])


def _ffn_fwd_bwd(x1, target, g_ffn, g_fin, w_gu, w_down):
    T = x1.shape[0]
    tm = min(256, T)
    inv_d = 1.0 / D

    def body(x1_ref, tg_ref, gf_ref, gn_ref, wgu_ref, wd_ref,
             act_ref, dx2b_ref, h2b_ref, dgu_ref, dx1_ref, dx1b_ref, acc_ref):
        @pl.when(pl.program_id(0) == 0)
        def _():
            acc_ref[...] = jnp.zeros_like(acc_ref)

        x1v = x1_ref[...]
        gf = gf_ref[...]
        gn = gn_ref[...]
        rr1 = lax.rsqrt(_mean(x1v * x1v) + EPS)
        x1n = x1v * rr1
        h2b = (x1n * gf).astype(BF16)
        h2b_ref[...] = h2b
        gate = _mm(h2b, wgu_ref[0])
        up = _mm(h2b, wgu_ref[1])
        sg = _sigmoid(gate)
        si = gate * sg
        act = (si * up).astype(BF16)
        act_ref[...] = act
        x2 = x1v + _mm(act, wd_ref[...])
        rr2 = lax.rsqrt(_mean(x2 * x2) + EPS)
        x2n = x2 * rr2
        e = x2n * gn - tg_ref[...]
        acc_ref[0] += _rows8(e * e) * (0.5 * inv_d)
        dy = e * inv_d
        acc_ref[1] += _rows8(dy * x2n)
        dxn = dy * gn
        dx2 = rr2 * (dxn - x2n * _mean(dxn * x2n))
        dx2b = dx2.astype(BF16)
        dx2b_ref[...] = dx2b
        dact = _mm_nt(dx2b, wd_ref[...])
        dgate = (dact * up * (sg * (1.0 + gate * (1.0 - sg)))).astype(BF16)
        dup = (dact * si).astype(BF16)
        dgu_ref[0] = dgate
        dgu_ref[1] = dup
        dh2 = _mm_nt(dgate, wgu_ref[0]) + _mm_nt(dup, wgu_ref[1])
        acc_ref[2] += _rows8(dh2 * x1n)
        dxn1 = dh2 * gf
        dx1 = dx2 + rr1 * (dxn1 - x1n * _mean(dxn1 * x1n))
        dx1_ref[...] = dx1
        dx1b_ref[...] = dx1.astype(BF16)

    t = lambda i: (i, 0)
    w = lambda i: (0, 0)
    one = pl.Buffered(1)
    return pl.pallas_call(
        body, name="ffn_fwd_bwd", grid=(T // tm,),
        in_specs=[pl.BlockSpec((tm, D), t), pl.BlockSpec((tm, D), t),
                  pl.BlockSpec((1, D), w), pl.BlockSpec((1, D), w),
                  pl.BlockSpec((2, D, FF), lambda i: (0, 0, 0), pipeline_mode=one),
                  pl.BlockSpec((FF, D), w, pipeline_mode=one)],
        out_specs=[pl.BlockSpec((tm, FF), t), pl.BlockSpec((tm, D), t), pl.BlockSpec((tm, D), t),
                   pl.BlockSpec((2, tm, FF), lambda i: (0, i, 0)),
                   pl.BlockSpec((tm, D), t), pl.BlockSpec((tm, D), t),
                   pl.BlockSpec((3, 8, D), lambda i: (0, 0, 0))],
        out_shape=[SDS((T, FF), BF16), SDS((T, D), BF16), SDS((T, D), BF16),
                   SDS((2, T, FF), BF16), SDS((T, D), F32), SDS((T, D), BF16),
                   SDS((3, 8, D), F32)],
        compiler_params=_cparams(),
    )(x1, target, g_ffn, g_fin, w_gu, w_down)


def _merge_bwd(dx1b, ab, ob, proj, w_out, w_a, w_b, job=None):
    T = dx1b.shape[0]
    tm = min(512, T)

    def body(dx_ref, ab_ref, ob_ref, ga_ref, gb_ref, wo_ref, wa_ref, wb_ref, dya_ref, dyb_ref, dp_ref):
        dm = _mm_nt(dx_ref[...], wo_ref[...])
        sa = _sigmoid(ga_ref[...])
        sb = _sigmoid(gb_ref[...])
        dya_ref[...] = (dm * sa).astype(BF16)
        dyb_ref[...] = (dm * sb).astype(BF16)
        dp_ref[0] = (dm * _mm(ab_ref[...], wa_ref[...]) * sa * (1.0 - sa)).astype(BF16)
        dp_ref[1] = (dm * _mm(ob_ref[...], wb_ref[...]) * sb * (1.0 - sb)).astype(BF16)

    t = lambda i: (i, 0)
    w = lambda i: (0, 0)
    return _call(
        body, name="merge_bwd", grid=(T // tm,),
        in_specs=[pl.BlockSpec((tm, D), t), pl.BlockSpec((tm, D), t), pl.BlockSpec((tm, D), t),
                  pl.BlockSpec((None, tm, D), lambda i: (6, i, 0)), pl.BlockSpec((None, tm, D), lambda i: (7, i, 0)),
                  pl.BlockSpec((D, D), w), pl.BlockSpec((D, D), w), pl.BlockSpec((D, D), w)],
        out_specs=[pl.BlockSpec((tm, D), t)] * 2 + [pl.BlockSpec((2, tm, D), lambda i: (3, i, 0))],
        out_shape=[SDS((T, D), BF16), SDS((T, D), BF16), SDS((NIN, T, D), BF16)],
        args=(dx1b, ab, ob, proj, proj, w_out, w_a, w_b), job=job)


def _hgrn_bwd(dproj, dyb, w_b, o_raw, proj, st_before, lb_table, norm_g, job=None):
    T = dyb.shape[0]
    tb = min(HGRN_TOKENS, T)
    nc = tb // HCH
    nb = T // tb

    def body(dp_in, dyb_ref, wb_ref, o_ref, q_ref, fl_ref, v_ref, g_ref, stb_ref, lbt_ref, gn_ref,
             dp_ref, acc_ref, dst_s, dqin_s, dqa_s, dkin_s, dkd_s, dv_s, ddec_s):
        del dp_in

        @pl.when(pl.program_id(1) == 0)
        def _():
            dst_s[...] = jnp.zeros_like(dst_s)
            acc_ref[...] = jnp.zeros_like(acc_ref)

        row = lax.broadcasted_iota(jnp.int32, (tb, HW), 0) & (HCH - 1)
        gn = gn_ref[...]
        lbv = _sigmoid(lbt_ref[0:1, :] - lbt_ref[1:2, :])
        o = o_ref[...]
        r = lax.rsqrt(_head_mean(o * o) + EPS)
        on = o * r
        g = g_ref[...]
        sgm = _sigmoid(g)
        dob_v = _mm_nt(dyb_ref[...], wb_ref[...])
        dp_ref[3] = (dob_v * on * gn * (sgm * (1.0 + g * (1.0 - sgm)))).astype(BF16)
        do_n = dob_v * (g * sgm)
        acc_ref[1] += _rows8(do_n * on)
        dxn = do_n * gn
        do = (r * (dxn - on * _head_mean(dxn * on))).astype(BF16)
        s, f, a, a_mid, a_last = _hgrn_gates(fl_ref[...], lbv, row)
        k = 1.0 - f
        qs = q_ref[...] * QSCALE
        e_q = jnp.exp(a - a_mid)
        e_k = jnp.exp(a_mid - a)
        e_a = jnp.exp(a)
        e_l = jnp.exp(a_last - a)
        dec = jnp.exp(a_last)
        q_in = qs * e_q
        k_in = k * e_k
        q_a = qs * e_a
        k_d = k * e_l
        q_inb, k_inb, q_ab, k_db = (z.astype(BF16) for z in (q_in, k_in, q_a, k_d))
        vb = v_ref[...].astype(BF16)
        tri = (lax.broadcasted_iota(jnp.int32, (HCH, HCH), 0)
               >= lax.broadcasted_iota(jnp.int32, (HCH, HCH), 1))
        for c in reversed(range(nc)):
            sl = slice(HCH * c, HCH * (c + 1))
            for hh in range(HGRN_HB):
                hs = slice(HD * hh, HD * (hh + 1))
                stp = stb_ref[hh, c]
                dst = dst_s[hh]
                dstb = dst.astype(BF16)
                do_c = do[sl, hs]
                v_c = vb[sl, hs]
                dqa_s[sl, hs] = _mm(do_c, stp.astype(BF16))
                dkd_s[sl, hs] = _mm(v_c, dstb)
                ddec_s[sl, hs] = jnp.broadcast_to(jnp.sum(dst * stp, axis=0, keepdims=True), (HCH, HD))
                sc = jnp.where(tri, _mm_nt(q_inb[sl, hs], k_inb[sl, hs]), 0.0).astype(BF16)
                dsc = jnp.where(tri, _mm_nt(do_c, v_c), 0.0).astype(BF16)
                dv_s[sl, hs] = _mm_nt(k_db[sl, hs], dstb) + _mm_tn(sc, do_c)
                dqin_s[sl, hs] = _mm(dsc, k_inb[sl, hs])
                dkin_s[sl, hs] = _mm_tn(dsc, q_inb[sl, hs])
                d64 = dec[sl, hs]
                dst_s[hh] = dst * jnp.concatenate([d64, d64], axis=0) + _mm_tn(do_c, q_ab[sl, hs])
        dq_in = dqin_s[...]
        dq_a = dqa_s[...]
        dk_in = dkin_s[...]
        dk_d = dkd_s[...]
        dp_ref[0] = ((dq_in * e_q + dq_a * e_a) * QSCALE).astype(BF16)
        dp_ref[2] = dv_s[...].astype(BF16)
        tq = dq_in * q_in
        tk = dk_in * k_in
        td = dk_d * k_d
        d_a = tq + dq_a * q_a - tk - td
        d_a = d_a + jnp.where(row == HCH // 2 - 1, _seg_sum(tk - tq), 0.0)
        d_a = d_a + jnp.where(row == HCH - 1, _seg_sum(td) + ddec_s[...] * dec, 0.0)
        dlf = _revcumsum64(d_a, row)
        df = dlf / f - (dk_in * e_k + dk_d * e_l)
        dp_ref[1] = (df * (1.0 - lbv) * s * (1.0 - s)).astype(BF16)
        acc_ref[0] += _rows8(df * (1.0 - s))

    def col(off):
        return pl.BlockSpec((None, tb, HW), lambda h, cb: (off, nb - 1 - cb, h))

    hb = lambda h, cb: (nb - 1 - cb, h)
    return _call(
        body, name="hgrn_bwd", grid=(NH // HGRN_HB, nb), job=job,
        args=(dproj, dyb, w_b, o_raw, proj, proj, proj, proj, st_before, lb_table, norm_g),
        in_specs=[ANY, pl.BlockSpec((tb, D), lambda h, cb: (nb - 1 - cb, 0)),
                  pl.BlockSpec((HW, D), lambda h, cb: (h, 0)), pl.BlockSpec((tb, HW), hb),
                  col(2), col(3), col(4), col(5),
                  pl.BlockSpec((HGRN_HB, nc, HD, HD), lambda h, cb: (h, nb - 1 - cb, 0, 0)),
                  pl.BlockSpec((2, HW), lambda h, cb: (0, h)), pl.BlockSpec((1, HW), lambda h, cb: (0, h))],
        out_specs=[pl.BlockSpec((4, tb, HW), lambda h, cb: (0, nb - 1 - cb, h)),
                   pl.BlockSpec((2, 8, HW), lambda h, cb: (0, 0, h))],
        out_shape=[SDS(dproj.shape, BF16), SDS((2, 8, D), F32)],
        scratch_shapes=[pltpu.VMEM((HGRN_HB, HD, HD), F32)] + [pltpu.VMEM((tb, HW), F32)] * 6,
        aliases={0: 0})


def _gmlp_bwd(dproj, dya, w_a, proj, ln_g, ln_b, wm, wm_t, b_t):
    T = dya.shape[0]
    tm = min(GMLP_BWD_TOKENS, T)

    def body(dp_in, dya_ref, wa_ref, u_ref, v_ref, lg_ref, lb_ref, wm_ref, wmt_ref, bt_ref,
             dp_ref, acc_ref, dws_ref, dmix_ref, du_s, dvn_s):
        del dp_in

        @pl.when(pl.program_id(0) == 0)
        def _():
            acc_ref[...] = jnp.zeros_like(acc_ref)
            dws_ref[...] = jnp.zeros_like(dws_ref)
            dmix_ref[...] = jnp.zeros_like(dmix_ref)

        u = u_ref[...]
        v = v_ref[...]
        lg = lg_ref[...]
        gu, t_u = _gelu(u)
        gv, t_v = _gelu(v)
        vhat, rs = _layer_norm_stats(gv)
        vnb = (vhat * lg + lb_ref[...]).astype(BF16)
        da_v = _mm_nt(dya_ref[...], wa_ref[...])
        for g in range(NG):
            cols = slice(128 * g, 128 * (g + 1))
            vng = _chunks_abreast(vnb[:, cols])
            mixed = _mm(wm_ref[g], vng) + bt_ref[:, g:g + 1]
            dag = _chunks_abreast(da_v[:, cols])
            dmx = dag * _chunks_abreast(gu[:, cols])
            du_s[:, cols] = _chunks_stacked(dag * mixed)
            dmxb = dmx.astype(BF16)
            dws_ref[:, cols] += _mm_nt(dmxb, vng)
            dmix_ref[:, cols] += sum(dmx[:, GCH * ch:GCH * (ch + 1)] for ch in range(tm // GCH))
            dvn_s[:, cols] = _chunks_stacked(_mm(wmt_ref[g], dmxb))
        dp_ref[0] = (du_s[...] * _gelu_grad(u, t_u)).astype(BF16)
        dvn = dvn_s[...]
        acc_ref[0] += _rows8(dvn * vhat)
        acc_ref[1] += _rows8(dvn)
        dvh = dvn * lg
        dgv = rs * (dvh - _mean(dvh) - vhat * _mean(dvh * vhat))
        dp_ref[1] = (dgv * _gelu_grad(v, t_v)).astype(BF16)

    row = lambda i: (0, 0)
    w3 = lambda i: (0, 0, 0)
    return pl.pallas_call(
        body, name="gmlp_bwd", grid=(T // tm,),
        in_specs=[ANY, pl.BlockSpec((tm, D), lambda i: (i, 0)), pl.BlockSpec((D, D), row),
                  pl.BlockSpec((None, tm, D), lambda i: (0, i, 0)), pl.BlockSpec((None, tm, D), lambda i: (1, i, 0)),
                  pl.BlockSpec((1, D), row), pl.BlockSpec((1, D), row),
                  pl.BlockSpec((NG, GCH, GCH), w3), pl.BlockSpec((NG, GCH, GCH), w3),
                  pl.BlockSpec((GCH, NG), row)],
        out_specs=[pl.BlockSpec((2, tm, D), lambda i: (2, i, 0)),
                   pl.BlockSpec((2, 8, D), w3), pl.BlockSpec((GCH, D), row), pl.BlockSpec((GCH, D), row)],
        out_shape=[SDS(dproj.shape, BF16), SDS((2, 8, D), F32), SDS((GCH, D), F32), SDS((GCH, D), F32)],
        scratch_shapes=[pltpu.VMEM((tm, D), F32), pltpu.VMEM((tm, D), F32)],
        input_output_aliases={0: 0},
        compiler_params=_cparams(),
    )(dproj, dya, w_a, proj, proj, ln_g, ln_b, wm, wm_t, b_t)


def _proj_bwd(dproj, w_in4, x, dx1, g_mix, job=None):
    T = x.shape[0]
    tm = min(256, T)
    order = (2, 3, 4, 5, 0, 1, 6, 7)

    def body(dp_ref, w_ref, x_ref, dx1_ref, g_ref, gx_ref, acc_ref):
        @pl.when(pl.program_id(0) == 0)
        def _():
            acc_ref[...] = jnp.zeros_like(acc_ref)

        dh = None
        for m, og in enumerate(order):
            part = _mm_nt(dp_ref[m], w_ref[og // 2, :, D * (og % 2):D * (og % 2 + 1)])
            dh = part if dh is None else dh + part
        xv = x_ref[...]
        r = lax.rsqrt(_mean(xv * xv) + EPS)
        xn = xv * r
        acc_ref[...] += _rows8(dh * xn)
        dxn = dh * g_ref[...]
        gx_ref[...] = dx1_ref[...] + r * (dxn - xn * _mean(dxn * xn))

    t = lambda i: (i, 0)
    return _call(
        body, name="proj_bwd", grid=(T // tm,),
        in_specs=[pl.BlockSpec((NIN, tm, D), lambda i: (0, i, 0)),
                  pl.BlockSpec((NCHIP, D, 2 * D), lambda i: (0, 0, 0), pipeline_mode=pl.Buffered(1)),
                  pl.BlockSpec((tm, D), t), pl.BlockSpec((tm, D), t), pl.BlockSpec((1, D), lambda i: (0, 0))],
        out_specs=[pl.BlockSpec((tm, D), t), pl.BlockSpec((8, D), lambda i: (0, 0))],
        out_shape=[SDS((T, D), F32), SDS((8, D), F32)],
        args=(dproj, w_in4, x, dx1, g_mix), job=job)


def _dw_call(name, a, b, a_spec, b_spec, o_spec, out_shape, nblk, tt, job=None, prefetch=None):
    T = a.shape[-2]

    def body(*refs):
        a_ref, b_ref, o_ref = refs[-3:]

        @pl.when(pl.program_id(1) == 0)
        def _():
            o_ref[...] = jnp.zeros_like(o_ref)
        o_ref[...] += _mm_tn(a_ref[...], b_ref[...])

    (out,), job_out = _call(
        body, name=name, grid=(nblk, T // tt), in_specs=[a_spec, b_spec], out_specs=[o_spec],
        out_shape=[out_shape], args=(a, b), job=job, prefetch=prefetch)
    return out, job_out


def _dw_in_half(name, place, hb, dproj, mine, job=None):
    tt = min(DW_TOKENS, hb.shape[0])

    def comp(k, pc):
        return _component_of(2 * k + (pc[1] if mine else 1 - pc[1]))

    return _dw_call(
        name, hb, dproj,
        pl.BlockSpec((tt, D), lambda k, t, pc: (t, 0)),
        pl.BlockSpec((None, tt, D), lambda k, t, pc: (comp(k, pc), t, 0)),
        pl.BlockSpec((None, D, D), lambda k, t, pc: (k, 0, 0)),
        SDS((NCHIP, D, D), F32), NCHIP, tt, job, place)


def _dw_gate_up(h2b, dgu, job=None):
    tt = min(DW_TOKENS, h2b.shape[0])
    return _dw_call(
        "dw_gate_up", h2b, dgu,
        pl.BlockSpec((tt, D), lambda k, t: (t, 0)),
        pl.BlockSpec((None, tt, FFS), lambda k, t: (k // 2, t, k % 2)),
        pl.BlockSpec((None, D, FFS), lambda k, t: (k, 0, 0)),
        SDS((NCHIP, D, FFS), F32), NCHIP, tt, job)


def _dw_down(act, dx2b, job=None):
    tt = min(DW_TOKENS, act.shape[0])
    g, job_out = _dw_call(
        "dw_down", act, dx2b,
        pl.BlockSpec((tt, FFS), lambda k, t: (t, k)),
        pl.BlockSpec((tt, D), lambda k, t: (t, 0)),
        pl.BlockSpec((FFS, D), lambda k, t: (k, 0)),
        SDS((FF, D), F32), 2, tt, job)
    return g.reshape(NCHIP, FF // NCHIP, D), job_out


def _dw_square(name, a, b, job=None):
    tt = min(DW_TOKENS, a.shape[0])
    g, job_out = _dw_call(
        name, a, b,
        pl.BlockSpec((tt, D), lambda k, t: (t, 0)), pl.BlockSpec((tt, D), lambda k, t: (t, 0)),
        pl.BlockSpec((D, D), lambda k, t: (0, 0)), SDS((D, D), F32), 1, tt, job)
    return g.reshape(NCHIP, D // NCHIP, D), job_out


def _place():
    x, y, c = lax.axis_index("x"), lax.axis_index("y"), lax.axis_index("c")
    return x, y, c, 2 * x + y


def _chip_at(x, y, s):
    return x ^ (s >> 1), y ^ (s & 1)


class _Job:
    def __init__(self, ins, out_shapes, sems, start, finish, aliases=None, mid=None):
        self.ins, self.out_shapes, self.sems = list(ins), list(out_shapes), list(sems)
        self.start, self.finish, self.aliases = start, finish, dict(aliases or {})
        self.mid = mid if mid is not None else (lambda ins, outs, sems: None)


def _join_jobs(*jobs):
    def cut(refs, sizes):
        out, at = [], 0
        for n in sizes:
            out.append(refs[at:at + n])
            at += n
        return out

    ni = [len(j.ins) for j in jobs]
    no = [len(j.out_shapes) for j in jobs]
    ns = [len(j.sems) for j in jobs]

    def run(which):
        def go(ins, outs, sems):
            for j, a, b, c in zip(jobs, cut(ins, ni), cut(outs, no), cut(sems, ns)):
                getattr(j, which)(a, b, c)
        return go

    aliases = {}
    for k, j in enumerate(jobs):
        for a, b in j.aliases.items():
            aliases[sum(ni[:k]) + a] = sum(no[:k]) + b
    return _Job([a for j in jobs for a in j.ins], [o for j in jobs for o in j.out_shapes],
                [s for j in jobs for s in j.sems], run("start"), run("finish"), aliases, run("mid"))


def _call(body, *, name, grid, in_specs, out_specs, out_shape, args, scratch_shapes=(), aliases=None,
          job=None, prefetch=None):
    n_in, n_out, n_scr = len(in_specs), len(out_specs), len(scratch_shapes)
    npf = 0 if prefetch is None else 1
    job = job if job is not None else _Job([], [], [], lambda *a: None, lambda *a: None)
    ji, jo = len(job.ins), len(job.out_shapes)
    steps = math.prod(grid)

    def wrapped(*refs):
        pf, refs = refs[:npf], refs[npf:]
        ins, jin = refs[:n_in], refs[n_in:n_in + ji]
        o0 = n_in + ji
        outs, jout = refs[o0:o0 + n_out], refs[o0 + n_out:o0 + n_out + jo]
        s0 = o0 + n_out + jo
        scr, jsem = refs[s0:s0 + n_scr], refs[s0 + n_scr:]
        step = functools.reduce(lambda acc, ag: acc * ag[1] + pl.program_id(ag[0]), enumerate(grid), 0)
        if ji or jo:
            @pl.when(step == 0)
            def _():
                job.start(jin, jout, jsem)

        body(*pf, *ins, *outs, *scr)

        if ji or jo:
            @pl.when(step == steps // 2)
            def _():
                job.mid(jin, jout, jsem)

            @pl.when(step == steps - 1)
            def _():
                job.finish(jin, jout, jsem)

    io = {npf + a: b for a, b in dict(aliases or {}).items()}
    io.update({npf + n_in + a: n_out + b for a, b in job.aliases.items()})
    kw = dict(in_specs=list(in_specs) + [ANY] * ji, out_specs=list(out_specs) + [ANY] * jo,
              scratch_shapes=list(scratch_shapes) + job.sems)
    if npf:
        kw = dict(grid_spec=pltpu.PrefetchScalarGridSpec(num_scalar_prefetch=1, grid=grid, **kw))
    else:
        kw["grid"] = grid
    res = pl.pallas_call(
        wrapped, name=name, out_shape=list(out_shape) + job.out_shapes, input_output_aliases=io,
        compiler_params=_cparams(has_side_effects=bool(ji or jo)), **kw,
    )(*(() if prefetch is None else (prefetch,)), *args, *job.ins)
    return list(res[:n_out]), list(res[n_out:])


def _cast_shards(name, place, ws, paired=False):
    n = len(ws)
    rows, cols = ws[0].shape
    tr = 352 if rows % 352 == 0 else 256
    shape = (2, rows, 2 * cols) if paired else (NCHIP, rows, cols)
    mine = (lambda i, pc: (pc[0] // 2, i, pc[0] % 2)) if paired else (lambda i, pc: (pc[0], i, 0))

    def body(pc_ref, *refs):
        del pc_ref
        for w_ref, o_ref in zip(refs[:n], refs[n:]):
            o_ref[...] = w_ref[...].astype(BF16)

    return pl.pallas_call(
        body, name=name,
        grid_spec=pltpu.PrefetchScalarGridSpec(
            num_scalar_prefetch=1, grid=(rows // tr,),
            in_specs=[pl.BlockSpec((tr, cols), lambda i, pc: (i, 0))] * n,
            out_specs=[pl.BlockSpec((None, tr, cols), mine)] * n),
        out_shape=[SDS(shape, BF16)] * n,
        compiler_params=_cparams(),
    )(place, *ws)


def _sibling_copy(ref, send_sem, recv_sem):
    x, y, c, _ = _place()
    return pltpu.make_async_remote_copy(src_ref=ref, dst_ref=ref, send_sem=send_sem, recv_sem=recv_sem,
                                        device_id=(x, y, 1 - c), device_id_type=MESH)


def _slot(arr, chip):
    if arr.shape[0] == NCHIP:
        return arr.at[chip]
    cols = arr.shape[2] // 2
    return arr.at[chip // 2, :, pl.ds(pl.multiple_of((chip % 2) * cols, 128), cols)]


def _half_rows(arr, slot, core):
    half = arr.shape[1] // 2
    return _slot(arr, slot).at[pl.ds(pl.multiple_of(core * half, 16), half)]


def _quarter_rows(arr, slot, core, q):
    quarter = arr.shape[1] // 4
    return _slot(arr, slot).at[pl.ds(pl.multiple_of((2 * core + q) * quarter, 16), quarter)]


def _chip_copy(ref, dist, send_sem, recv_sem):
    x, y, c, _ = _place()
    cx, cy = _chip_at(x, y, dist)
    return pltpu.make_async_remote_copy(src_ref=ref, dst_ref=ref, send_sem=send_sem, recv_sem=recv_sem,
                                        device_id=(cx, cy, c), device_id_type=MESH)


def _gather_sems(n):
    dma = pltpu.SemaphoreType.DMA
    return [dma((n, 2))] * 4 + [dma((n, 4))] * 2


def _gather_start(arrs, sems):
    dsend, drecv = sems[0], sems[1]
    _, _, c, j = _place()
    for w, arr in enumerate(arrs):
        for dist in (1, 2):
            _chip_copy(_half_rows(arr, j, c), dist, dsend.at[w, dist - 1], drecv.at[w, dist - 1]).start()


def _gather_land(arrs, sems, dist, first=0):
    dsend, drecv, rsend, rrecv, fsend, frecv = sems
    _, _, c, j = _place()
    if dist < 3:
        other = 3 - dist
        for w, arr in enumerate(arrs, first):
            landed = _half_rows(arr, j ^ dist, c)
            _chip_copy(landed, dist, dsend.at[w, dist - 1], drecv.at[w, dist - 1]).wait_recv()
            relay = _quarter_rows(arr, j ^ dist, c, other - 1)
            _chip_copy(relay, other, rsend.at[w, other - 1], rrecv.at[w, other - 1]).start()
            _sibling_copy(landed, fsend.at[w, dist - 1], frecv.at[w, dist - 1]).start()
        for w, arr in enumerate(arrs, first):
            theirs = _half_rows(arr, j ^ dist, 1 - c)
            _sibling_copy(theirs, fsend.at[w, dist - 1], frecv.at[w, dist - 1]).wait_recv()
    else:
        for w, arr in enumerate(arrs, first):
            for via in (1, 2):
                piece = _quarter_rows(arr, j ^ 3, c, via - 1)
                _chip_copy(piece, via, rsend.at[w, via - 1], rrecv.at[w, via - 1]).wait_recv()
                _sibling_copy(piece, fsend.at[w, 1 + via], frecv.at[w, 1 + via]).start()
        for w, arr in enumerate(arrs, first):
            for via in (1, 2):
                theirs = _quarter_rows(arr, j ^ 3, 1 - c, via - 1)
                _sibling_copy(theirs, fsend.at[w, 1 + via], frecv.at[w, 1 + via]).wait_recv()


def _gather_drain(arrs, sems):
    dsend, drecv, rsend, rrecv, fsend, frecv = sems
    _, _, c, j = _place()
    for w, arr in enumerate(arrs):
        for dist in (1, 2):
            other = 3 - dist
            _chip_copy(_half_rows(arr, j, c), dist, dsend.at[w, dist - 1], drecv.at[w, dist - 1]).wait_send()
            _chip_copy(_quarter_rows(arr, j ^ dist, c, other - 1), other,
                       rsend.at[w, other - 1], rrecv.at[w, other - 1]).wait_send()
            _sibling_copy(_half_rows(arr, j ^ dist, c), fsend.at[w, dist - 1], frecv.at[w, dist - 1]).wait_send()
            _sibling_copy(_quarter_rows(arr, j ^ 3, c, dist - 1),
                          fsend.at[w, 1 + dist], frecv.at[w, 1 + dist]).wait_send()


def _gather_neighbours(arrs, sems):
    _gather_land(arrs, sems, 1)
    _gather_land(arrs, sems, 2)


def _gather_finish(arrs, sems):
    _gather_land(arrs, sems, 3)
    _gather_drain(arrs, sems)


def _gather_job(arrs):
    n = len(arrs)
    return _Job(arrs, [SDS(a.shape, a.dtype) for a in arrs], _gather_sems(n),
                lambda ins, outs, sems: _gather_start(outs, sems),
                lambda ins, outs, sems: _gather_finish(outs, sems), {k: k for k in range(n)},
                mid=lambda ins, outs, sems: _gather_neighbours(outs, sems))


def _exchange_job(arrs, out_shapes, n, copies):
    def start(ins, outs, sems):
        for cp in copies(ins, outs, sems[0], sems[1]):
            cp.start()

    def finish(ins, outs, sems):
        for cp in copies(ins, outs, sems[0], sems[1]):
            cp.wait()

    return _Job(arrs, out_shapes, [pltpu.SemaphoreType.DMA((n,))] * 2, start, finish)


def _pair_exchange_job(grads):
    def copies(ins, outs, send_sem, recv_sem):
        x, y, c, _ = _place()
        res = []
        for w in range(len(grads)):
            half = ins[w].shape[1] // 2
            theirs = pl.ds(pl.multiple_of((1 - c) * half, 8), half)
            res.append(pltpu.make_async_remote_copy(
                src_ref=ins[w].at[:, theirs, :], dst_ref=outs[w], send_sem=send_sem.at[w],
                recv_sem=recv_sem.at[w], device_id=(x, y, 1 - c), device_id_type=MESH))
        return res

    return _exchange_job(grads, [SDS((NCHIP, g.shape[1] // 2, g.shape[2]), F32) for g in grads],
                         len(grads), copies)


def _row_tile(rows, cols):
    tr = rows
    while tr * cols * 4 > ELEMENTWISE_BLOCK_BYTES and tr % 32 == 0:
        tr //= 2
    return tr


def _pair_sums(name, place, gs, sibs):
    n = len(gs)
    half, cols = sibs[0].shape[1], sibs[0].shape[2]
    tr = _row_tile(half, cols)
    nt = half // tr
    mine = nt if gs[0].shape[1] == 2 * half else 0

    def body(pc_ref, *refs):
        del pc_ref
        for g_ref, s_ref, own_ref, out_ref in zip(refs[:n], refs[n:2 * n], refs[2 * n:3 * n], refs[3 * n:]):
            v = g_ref[...] + s_ref[...]

            @pl.when(pl.program_id(1) == 0)
            def _():
                own_ref[...] = v

            @pl.when(pl.program_id(1) > 0)
            def _():
                out_ref[...] = v.astype(BF16)

    res = pl.pallas_call(
        body, name=name,
        grid_spec=pltpu.PrefetchScalarGridSpec(
            num_scalar_prefetch=1, grid=(nt, NCHIP),
            in_specs=[pl.BlockSpec((None, tr, cols), lambda i, s, pc: (pc[0] ^ s, pc[1] * mine + i, 0))] * n
            + [pl.BlockSpec((None, tr, cols), lambda i, s, pc: (pc[0] ^ s, i, 0))] * n,
            out_specs=[pl.BlockSpec((tr, cols), lambda i, s, pc: (i, 0))] * n
            + [pl.BlockSpec((None, tr, cols), lambda i, s, pc: (jnp.maximum(s - 1, 0), i, 0))] * n),
        out_shape=[SDS((half, cols), F32)] * n + [SDS((NCHIP - 1, half, cols), BF16)] * n,
        compiler_params=_cparams(),
    )(place, *gs, *sibs)
    return res[:n], res[n:]


def _chip_exchange_job(parts):
    def copies(ins, outs, send_sem, recv_sem):
        x, y, c, _ = _place()
        res = []
        for w in range(len(parts)):
            for s in range(1, NCHIP):
                cx, cy = _chip_at(x, y, s)
                k = w * (NCHIP - 1) + s - 1
                res.append(pltpu.make_async_remote_copy(
                    src_ref=ins[w].at[s - 1], dst_ref=outs[w].at[s - 1], send_sem=send_sem.at[k],
                    recv_sem=recv_sem.at[k], device_id=(cx, cy, c), device_id_type=MESH))
        return res

    return _exchange_job(parts, [SDS((NCHIP - 1,) + p.shape[1:], BF16) for p in parts],
                         len(parts) * (NCHIP - 1), copies)


def _chip_sums(name, owns, rems):
    n = len(owns)
    half, cols = owns[0].shape
    tr = _row_tile(half, cols)

    def body(*refs):
        for own_ref, rem_ref, out_ref in zip(refs[:n], refs[n:2 * n], refs[2 * n:]):
            out_ref[...] = (((own_ref[...] + rem_ref[0].astype(F32)) + rem_ref[1].astype(F32))
                            + rem_ref[2].astype(F32))

    return pl.pallas_call(
        body, name=name, grid=(half // tr,),
        in_specs=[pl.BlockSpec((tr, cols), lambda i: (i, 0))] * n
        + [pl.BlockSpec((NCHIP - 1, tr, cols), lambda i: (0, i, 0))] * n,
        out_specs=[pl.BlockSpec((tr, cols), lambda i: (i, 0))] * n,
        out_shape=[SDS((half, cols), F32)] * n,
        compiler_params=_cparams(),
    )(*owns, *rems)


def _share_halves_job(halves):
    def copies(ins, outs, send_sem, recv_sem):
        x, y, c, _ = _place()
        return [pltpu.make_async_remote_copy(
            src_ref=ins[w], dst_ref=outs[w], send_sem=send_sem.at[w], recv_sem=recv_sem.at[w],
            device_id=(x, y, 1 - c), device_id_type=MESH) for w in range(len(halves))]

    return _exchange_job(halves, [SDS(h.shape, F32) for h in halves], len(halves), copies)


def _adamw_math(w, g, m, v):
    m = B1 * m + (1.0 - B1) * g
    v = B2 * v + (1.0 - B2) * (g * g)
    m_hat = m / (1.0 - B1 ** STEP)
    v_hat = v / (1.0 - B2 ** STEP)
    delta = -LR * (m_hat / (jnp.sqrt(v_hat) + AEPS) + WD * w)
    return delta, m, v


def _adamws(name, place, ws, owns, sibs, ms, vs):
    n = len(ws)
    rows, cols = ws[0].shape
    by_cols = owns[0].shape[0] == rows
    half, pc_cols = (rows, cols // 2) if by_cols else (rows // 2, cols)
    tr = _row_tile(half, pc_cols)
    nt = half // tr

    def body(pc_ref, *refs):
        ins, outs = refs[:5 * n], refs[5 * n:]
        for k in range(n):
            w_ref, own_ref, sib_ref, m_ref, v_ref = ins[5 * k:5 * k + 5]
            g = jnp.where(pl.program_id(0) == pc_ref[1], own_ref[...], sib_ref[...])
            d, mn, vn = _adamw_math(w_ref[...], g, m_ref[...], v_ref[...])
            for ref, val in zip(outs[4 * k:4 * k + 4], (g, d, mn, vn)):
                ref[...] = val

    full = pl.BlockSpec((tr, pc_cols), (lambda h, i, pc: (i, h)) if by_cols else (lambda h, i, pc: (h * nt + i, 0)))
    part = pl.BlockSpec((tr, pc_cols), lambda h, i, pc: (i, 0))
    res = pl.pallas_call(
        body, name=name,
        grid_spec=pltpu.PrefetchScalarGridSpec(
            num_scalar_prefetch=1, grid=(2, nt),
            in_specs=[full, part, part, full, full] * n, out_specs=[full] * (4 * n)),
        out_shape=[SDS((rows, cols), F32)] * (4 * n),
        compiler_params=_cparams(),
    )(place, *[a for group in zip(ws, owns, sibs, ms, vs) for a in group])
    return [tuple(res[4 * k:4 * k + 4]) for k in range(n)]


def _small_allreduce_adamw(sp, w, m, v, job):
    shape = sp.shape
    ji, jo = len(job.ins), len(job.out_shapes)

    def body(sp_ref, w_ref, m_ref, v_ref, *rest):
        jin, (g_ref, d_ref, mo_ref, vo_ref), jout = rest[:ji], rest[ji:ji + 4], rest[ji + 4:ji + 4 + jo]
        sib_s, pair_s, chip_s, send_sem, recv_sem = rest[ji + 4 + jo:ji + 9 + jo]
        jsem = rest[ji + 9 + jo:]
        job.start(jin, jout, jsem)
        x, y, c, j = _place()
        cp = pltpu.make_async_remote_copy(
            src_ref=sp_ref, dst_ref=sib_s, send_sem=send_sem.at[0], recv_sem=recv_sem.at[0],
            device_id=(x, y, 1 - c), device_id_type=MESH)
        cp.start()
        cp.wait()
        pair_s[...] = sp_ref[...] + sib_s[...]
        half = shape[0] // 2
        mine = pl.ds(pl.multiple_of(c * half, 8), half)
        cps = []
        for s in range(1, NCHIP):
            cx, cy = _chip_at(x, y, s)
            cp = pltpu.make_async_remote_copy(
                src_ref=pair_s.at[mine], dst_ref=chip_s.at[s, mine], send_sem=send_sem.at[s],
                recv_sem=recv_sem.at[s], device_id=(cx, cy, c), device_id_type=MESH)
            cp.start()
            cps.append(cp)
        chip_s[0] = pair_s[...]
        for cp in cps:
            cp.wait()
        cps = []
        for s in range(1, NCHIP):
            cp = pltpu.make_async_remote_copy(
                src_ref=chip_s.at[s, mine], dst_ref=chip_s.at[s, mine], send_sem=send_sem.at[NCHIP + s],
                recv_sem=recv_sem.at[NCHIP + s], device_id=(x, y, 1 - c), device_id_type=MESH)
            cp.start()
            cps.append(cp)
        for cp in cps:
            cp.wait()
        tot = chip_s[j]
        for k in range(1, NCHIP):
            tot = tot + chip_s[k ^ j]
        g_ref[...] = tot
        d, mn, vn = _adamw_math(w_ref[...], tot, m_ref[...], v_ref[...])
        d_ref[...] = d
        mo_ref[...] = mn
        vo_ref[...] = vn
        job.mid(jin, jout, jsem)
        job.finish(jin, jout, jsem)

    vm = pl.BlockSpec(memory_space=pltpu.VMEM)
    res = pl.pallas_call(
        body, name="small_allreduce_adamw",
        in_specs=[vm] * 4 + [ANY] * ji, out_specs=[vm] * 4 + [ANY] * jo,
        out_shape=[SDS(shape, F32)] * 4 + job.out_shapes,
        scratch_shapes=[pltpu.VMEM(shape, F32), pltpu.VMEM(shape, F32), pltpu.VMEM((NCHIP,) + shape, F32),
                        pltpu.SemaphoreType.DMA((2 * NCHIP,)), pltpu.SemaphoreType.DMA((2 * NCHIP,))] + job.sems,
        input_output_aliases={4 + a: 4 + b for a, b in job.aliases.items()},
        compiler_params=pltpu.CompilerParams(has_side_effects=True),
    )(sp, w, m, v, *job.ins)
    return res[:4], res[4:]


def _pack_small(first, mix, ln_g, ln_b, b_s, lbt, hn, ffn, fin, w_s):
    rows = [first.reshape(1, D), mix.reshape(1, D), ln_g.reshape(1, D), ln_b.reshape(1, D),
            b_s.reshape(1, D), lbt.reshape(2, D), hn.reshape(1, D), ffn.reshape(1, D), fin.reshape(1, D),
            jnp.zeros((6, D), F32)]
    return jnp.concatenate(rows + [w_s.reshape(NG, GCH, GCH).transpose(1, 0, 2).reshape(GCH, D)], axis=0)


def _unpack_small(p):
    w_s = p[16:].reshape(GCH, NG, GCH).transpose(1, 0, 2).reshape(1, NG, GCH, GCH)
    return dict(norm_mix_g=p[1:2], gmlp_ln_g=p[2:3], gmlp_ln_b=p[3:4], gmlp_b_s=p[4].reshape(1, NG, GCH),
                hgrn_lb_table=p[5:7], hgrn_norm_g=p[7:8], norm_ffn_g=p[8:9], norm_final_g=p[9],
                gmlp_w_s=w_s)


SMALL = ("norm_mix_g", "gmlp_ln_g", "gmlp_ln_b", "gmlp_w_s", "gmlp_b_s", "hgrn_lb_table", "hgrn_norm_g",
         "norm_ffn_g", "norm_final_g")
BIG = ("w_in", "w_gate_up", "w_branch_a", "w_branch_b", "w_out", "w_down")
ORDER = ("norm_mix_g", "w_in", "gmlp_ln_g", "gmlp_ln_b", "gmlp_w_s", "gmlp_b_s", "hgrn_lb_table",
         "hgrn_norm_g", "w_branch_a", "w_branch_b", "w_out", "norm_ffn_g", "w_gate_up", "w_down",
         "norm_final_g")


def kernel(x, norm_mix_g, w_in, gmlp_ln_g, gmlp_ln_b, gmlp_w_s, gmlp_b_s, hgrn_lb_table, hgrn_norm_g, w_branch_a, w_branch_b, w_out, norm_ffn_g, w_gate_up, w_down, norm_final_g, loss_target, m_norm_mix_g, m_w_in, m_gmlp_ln_g, m_gmlp_ln_b, m_gmlp_w_s, m_gmlp_b_s, m_hgrn_lb_table, m_hgrn_norm_g, m_w_branch_a, m_w_branch_b, m_w_out, m_norm_ffn_g, m_w_gate_up, m_w_down, m_norm_final_g, v_norm_mix_g, v_w_in, v_gmlp_ln_g, v_gmlp_ln_b, v_gmlp_w_s, v_gmlp_b_s, v_hgrn_lb_table, v_hgrn_norm_g, v_w_branch_a, v_w_branch_b, v_w_out, v_norm_ffn_g, v_w_gate_up, v_w_down, v_norm_final_g):
    args = dict(locals())
    T = x.shape[1]
    xs = x.reshape(T, D)
    target = loss_target.reshape(T, D)
    big = {n: args[n].reshape(args[n].shape[1:]) for n in BIG}
    big_m = {n: args["m_" + n].reshape(args[n].shape[1:]) for n in BIG}
    big_v = {n: args["v_" + n].reshape(args[n].shape[1:]) for n in BIG}

    x_i, y_i, c_i = lax.axis_index("x"), lax.axis_index("y"), lax.axis_index("c")
    place = jnp.stack([2 * x_i + y_i, c_i]).astype(jnp.int32)
    def by_shape(names):
        groups = []
        for n in names:
            if groups and big[groups[-1][0]].shape == big[n].shape:
                groups[-1].append(n)
            else:
                groups.append([n])
        return groups

    cast = {}
    for grp in by_shape(BIG):
        cast.update(zip(grp, _cast_shards("cast_" + grp[0], place, [big[n] for n in grp],
                                          paired=grp[0] == "w_gate_up")))
    tril = jnp.tril(jnp.ones((GCH, GCH), bool))
    wm = jnp.where(tril, gmlp_w_s[0], 0.0).astype(BF16)
    wm_t = jnp.swapaxes(wm, 1, 2)
    b_t = gmlp_b_s[0].T

    (proj, hb), w_in4, (w_a4, w_b4, w_out4, w_down4) = _proj_fwd(
        place, xs, norm_mix_g, cast["w_in"], [cast[n] for n in ("w_branch_a", "w_branch_b", "w_out", "w_down")])
    (ab,), _ = _gmlp_fwd(proj, gmlp_ln_g, gmlp_ln_b, wm, b_t)
    (o_raw, obb, st_before), (w_gu,) = _hgrn_fwd(
        proj, hgrn_lb_table, hgrn_norm_g, job=_gather_job([cast["w_gate_up"]]))
    w_a, w_b, w_o = (w.reshape(D, D) for w in (w_a4, w_b4, w_out4))
    (mgb, x1), _ = _merge_fwd(xs, ab, obb, proj, w_a, w_b, w_o)
    w_dn = w_down4.reshape(FF, D)
    act, dx2b, h2b, dgu, dx1, dx1b, acc_ffn = _ffn_fwd_bwd(
        x1, target, norm_ffn_g, norm_final_g.reshape(1, D), w_gu, w_dn)

    grads, owns, parts, halves, sibh = {}, {}, {}, {}, {}

    def pair_sums(names, sibs):
        sib_of = dict(zip(names, sibs))
        for grp in by_shape(names):
            o, p = _pair_sums("rs_pair_sum_" + grp[0], place, [grads[n] for n in grp], [sib_of[n] for n in grp])
            owns.update(zip(grp, o))
            parts.update(zip(grp, p))

    def chip_sums(names, got):
        rem_of = dict(zip(names, got))
        for grp in by_shape(names):
            h = _chip_sums("rs_chip_sum_" + grp[0], [owns[n] for n in grp], [rem_of[n] for n in grp])
            halves.update(zip(grp, h))

    ffn, mix = ("w_gate_up", "w_down"), ("w_branch_a", "w_branch_b", "w_out")
    grads["w_gate_up"], _ = _dw_gate_up(h2b, dgu)
    grads["w_down"], _ = _dw_down(act, dx2b)
    (dya, dyb, dproj), got = _merge_bwd(
        dx1b, ab, obb, proj, w_o, w_a, w_b, job=_pair_exchange_job([grads[n] for n in ffn]))
    pair_sums(ffn, got)
    grads["w_branch_a"], _ = _dw_square("dw_branch_a", ab, dya)
    grads["w_branch_b"], _ = _dw_square("dw_branch_b", obb, dyb)
    grads["w_out"], _ = _dw_square("dw_out", mgb, dx1b)
    (dproj, acc_hgrn), got = _hgrn_bwd(
        dproj, dyb, w_b, o_raw, proj, st_before, hgrn_lb_table, hgrn_norm_g,
        job=_join_jobs(_chip_exchange_job([parts[n] for n in ffn]), _pair_exchange_job([grads[n] for n in mix])))
    chip_sums(ffn, got[:2])
    pair_sums(mix, got[2:])
    dproj, acc_ln, dws, dmix = _gmlp_bwd(dproj, dya, w_a, proj, gmlp_ln_g, gmlp_ln_b, wm, wm_t, b_t)
    for_sibling, got = _dw_in_half(
        "dw_in_sibling_half", place, hb, dproj, False,
        job=_join_jobs(_share_halves_job([halves[n] for n in ffn]), _chip_exchange_job([parts[n] for n in mix])))
    sibh.update(zip(ffn, got[:2]))
    chip_sums(mix, got[2:])
    grads["w_in"], got = _dw_in_half(
        "dw_in_own_half", place, hb, dproj, True, job=_share_halves_job([for_sibling]))
    pair_sums(("w_in",), got)
    (grad_x, acc_mix), got = _proj_bwd(
        dproj, w_in4, xs, dx1, norm_mix_g,
        job=_join_jobs(_chip_exchange_job([parts["w_in"]]), _share_halves_job([halves[n] for n in mix])))
    chip_sums(("w_in",), got[:1])
    sibh.update(zip(mix, got[1:]))

    lbv = jax.nn.sigmoid(hgrn_lb_table[0] - hgrn_lb_table[1])
    d_t0 = jnp.sum(acc_hgrn[0], axis=0) * lbv * (1.0 - lbv)
    loss_row = jnp.zeros((D,), F32).at[0].set(jnp.sum(acc_ffn[0]))
    dws_m = jnp.where(tril[:, None, :], dws.reshape(GCH, NG, GCH), 0.0).transpose(1, 0, 2)
    db_s = jnp.sum(dmix.reshape(GCH, NG, GCH), axis=-1).T
    sp = _pack_small(loss_row, jnp.sum(acc_mix, 0), jnp.sum(acc_ln[0], 0), jnp.sum(acc_ln[1], 0), db_s,
                     jnp.stack([d_t0, -d_t0]), jnp.sum(acc_hgrn[1], 0), jnp.sum(acc_ffn[2], 0),
                     jnp.sum(acc_ffn[1], 0), dws_m)
    zero = jnp.zeros((D,), F32)

    def pack(prefix):
        a = lambda n: args[prefix + n]
        return _pack_small(zero, a("norm_mix_g"), a("gmlp_ln_g"), a("gmlp_ln_b"), a("gmlp_b_s"),
                           a("hgrn_lb_table"), a("hgrn_norm_g"), a("norm_ffn_g"), a("norm_final_g"),
                           a("gmlp_w_s"))

    packed, (sibh["w_in"],) = _small_allreduce_adamw(
        sp, pack(""), pack("m_"), pack("v_"), _share_halves_job([halves["w_in"]]))
    loss = packed[0][0, 0]
    small = [_unpack_small(p) for p in packed]
    out = {n: tuple(s[n] for s in small) for n in SMALL}
    for grp in by_shape(BIG):
        res = _adamws("adamw_" + grp[0], place, *[[d[n] for n in grp] for d in (big, halves, sibh, big_m, big_v)])
        for n, quad in zip(grp, res):
            out[n] = tuple(a.reshape(args[n].shape) for a in quad)
    return (loss, grad_x.reshape(x.shape), *[out[n][0] for n in ORDER], *[out[n][1] for n in ORDER],
            *[out[n][2] for n in ORDER], *[out[n][3] for n in ORDER])
```

```python
import functools
import math

import jax
import jax.numpy as jnp
from jax import lax
from jax.experimental import pallas as pl
from jax.experimental.pallas import tpu as pltpu

F32 = jnp.float32
BF16 = jnp.bfloat16
SDS = jax.ShapeDtypeStruct
MESH = pl.DeviceIdType.MESH
ANY = pl.BlockSpec(memory_space=pl.ANY)

D = 1024
NIN = 8
NG = 8
GCH = 128
NH = 8
HD = 128
HCH = 64
HGRN_HB = 8
HGRN_TOKENS = 256
GMLP_FWD_TOKENS = 512
GMLP_BWD_TOKENS = 256
HW = HGRN_HB * HD
DW_TOKENS = 2048
ELEMENTWISE_BLOCK_BYTES = 2 * 1024 * 1024
PROJ_OUT_SLOTS = 4
FF = 2816
FFS = 1408
NCHIP = 4
EPS = 1e-6
QSCALE = HD ** -0.5
GELU_C0 = math.sqrt(2.0 / math.pi)
GELU_C1 = 0.044715
LR, B1, B2, AEPS, WD, STEP = 0.001, 0.9, 0.999, 1e-08, 0.01, 10
VMEM_LIMIT_V7X = 56 * 1024 * 1024
SP_ROWS = 144


def _cparams(**kw):
    return pltpu.CompilerParams(vmem_limit_bytes=VMEM_LIMIT_V7X, **kw)


def _mm(a, b):
    return jnp.dot(a, b, preferred_element_type=F32)


def _mm_nt(a, b):
    return lax.dot_general(a, b, (((1,), (1,)), ((), ())), preferred_element_type=F32)


def _mm_tn(a, b):
    return lax.dot_general(a, b, (((0,), (0,)), ((), ())), preferred_element_type=F32)


def _rows8(x):
    r, c = x.shape
    return jnp.sum(x.reshape(r // 8, 8, c), axis=0)


def _mean(x):
    return jnp.mean(x, axis=-1, keepdims=True)


def _sigmoid(x):
    return 1.0 / (1.0 + jnp.exp(-x))


def _gelu(x):
    t = jnp.tanh(GELU_C0 * (x + GELU_C1 * x * x * x))
    return 0.5 * x * (1.0 + t), t


def _gelu_grad(x, t):
    return 0.5 * (1.0 + t) + 0.5 * x * (1.0 - t * t) * (GELU_C0 * (1.0 + 3.0 * GELU_C1 * x * x))


def _component_of(group):
    return jnp.where(group < 6, (group + 4) % 6, group)


def _proj_fwd(place, x, g_mix, w_in4, later):
    T = x.shape[0]
    tm = min(1024, T)
    ni = T // tm
    n = len(later)

    def body(pc_ref, x_ref, g_ref, *rest):
        proj_ref, h_ref, w_all = rest[1 + n:4 + n]
        gathered = rest[4 + n:4 + 2 * n]
        hs, wbuf, wsem, obuf, osem = rest[4 + 2 * n:9 + 2 * n]
        w_sems, later_sems = rest[9 + 2 * n:15 + 2 * n], rest[15 + 2 * n:]
        jp, i = pl.program_id(0), pl.program_id(1)
        w_cols = [w_all.at[:, :, pl.ds(k * D, D)] for k in range(2)]

        def w_copy(blk):
            cols = pl.ds(pl.multiple_of((blk % 2) * D, 128), D)
            return pltpu.make_async_copy(w_all.at[pc_ref[0] ^ (blk // 2), :, cols], wbuf.at[blk % 2],
                                         wsem.at[blk % 2])

        @pl.when((jp == 0) & (i == 0))
        def _():
            _gather_start(w_cols, w_sems)
            w_copy(jp).start()

        @pl.when(i == 0)
        def _():
            w_copy(jp).wait()

        @pl.when(jp == 0)
        def _():
            xv = x_ref[...]
            r = lax.rsqrt(_mean(xv * xv) + EPS)
            hb = (xv * r * g_ref[...]).astype(BF16)
            hs[i] = hb
            h_ref[...] = hb

        step = jp * ni + i
        slot = step % PROJ_OUT_SLOTS

        def o_copy(slot_):
            comp = 2 * (pc_ref[0] ^ (jp // 2)) + jp % 2
            return pltpu.make_async_copy(
                obuf.at[slot_], proj_ref.at[comp, pl.ds(pl.multiple_of(i * tm, 8), tm)], osem.at[slot_])

        @pl.when(step >= PROJ_OUT_SLOTS)
        def _():
            o_copy(slot).wait()

        obuf[slot] = _mm(hs[i], wbuf[jp % 2])
        o_copy(slot).start()

        @pl.when(step == NIN * ni - 1)
        def _():
            for k in range(PROJ_OUT_SLOTS):
                o_copy((slot + 1 + k) % PROJ_OUT_SLOTS).wait()

        for nxt in range(1, NIN):
            @pl.when((jp == nxt - 1) & (i == ni - 1))
            def _():
                if nxt >= 2:
                    _gather_land([w_cols[nxt % 2]], w_sems, nxt // 2, first=nxt % 2)
                if nxt == 5:
                    _gather_start(gathered, later_sems)
                if nxt == NIN - 1:
                    _gather_neighbours(gathered, later_sems)
                w_copy(jp + 1).start()

        @pl.when((jp == NIN - 1) & (i == ni - 1))
        def _():
            _gather_drain(w_cols, w_sems)
            _gather_finish(gathered, later_sems)

    tile = lambda jp, i, pc: (jnp.where(jp == 0, i, ni - 1), 0)
    res = pl.pallas_call(
        body, name="proj_fwd",
        grid_spec=pltpu.PrefetchScalarGridSpec(
            num_scalar_prefetch=1, grid=(NIN, ni),
            in_specs=[pl.BlockSpec((tm, D), tile), pl.BlockSpec((1, D), lambda jp, i, pc: (0, 0))] + [ANY] * (1 + n),
            out_specs=[ANY, pl.BlockSpec((tm, D), tile)] + [ANY] * (1 + n),
            scratch_shapes=[pltpu.VMEM((ni, tm, D), BF16), pltpu.VMEM((2, D, D), BF16),
                            pltpu.SemaphoreType.DMA((2,)), pltpu.VMEM((PROJ_OUT_SLOTS, tm, D), F32),
                            pltpu.SemaphoreType.DMA((PROJ_OUT_SLOTS,))] + _gather_sems(2) + _gather_sems(n)),
        out_shape=[SDS((NIN, T, D), F32), SDS((T, D), BF16), SDS(w_in4.shape, BF16)]
        + [SDS(a.shape, a.dtype) for a in later],
        input_output_aliases={3 + k: 2 + k for k in range(1 + n)},
        compiler_params=_cparams(has_side_effects=True),
    )(place, x, g_mix, w_in4, *later)
    return res[:2], res[2], res[3:]


def _chunks_abreast(x):
    return jnp.concatenate([x[GCH * ch:GCH * (ch + 1)] for ch in range(x.shape[0] // GCH)], axis=1)


def _chunks_stacked(x):
    return jnp.concatenate([x[:, GCH * ch:GCH * (ch + 1)] for ch in range(x.shape[1] // GCH)], axis=0)


def _layer_norm_stats(gv):
    mu = _mean(gv)
    xc = gv - mu
    rs = lax.rsqrt(_mean(xc * xc) + EPS)
    return xc * rs, rs


def _gmlp_fwd(proj, ln_g, ln_b, wm, b_t, job=None):
    T = proj.shape[1]
    tm = min(GMLP_FWD_TOKENS, T)

    def body(u_ref, v_ref, lg_ref, lb_ref, wm_ref, bt_ref, a_ref, a_s):
        gu, _ = _gelu(u_ref[...])
        gv, _ = _gelu(v_ref[...])
        vhat, _ = _layer_norm_stats(gv)
        vnb = (vhat * lg_ref[...] + lb_ref[...]).astype(BF16)
        for g in range(NG):
            cols = slice(128 * g, 128 * (g + 1))
            mixed = _mm(wm_ref[g], _chunks_abreast(vnb[:, cols])) + bt_ref[:, g:g + 1]
            a_s[:, cols] = gu[:, cols] * _chunks_stacked(mixed)
        a_ref[...] = a_s[...].astype(BF16)

    row = lambda i: (0, 0)
    return _call(
        body, name="gmlp_fwd", grid=(T // tm,), job=job, args=(proj, proj, ln_g, ln_b, wm, b_t),
        in_specs=[pl.BlockSpec((None, tm, D), lambda i: (0, i, 0)), pl.BlockSpec((None, tm, D), lambda i: (1, i, 0)),
                  pl.BlockSpec((1, D), row), pl.BlockSpec((1, D), row),
                  pl.BlockSpec((NG, GCH, GCH), lambda i: (0, 0, 0)), pl.BlockSpec((GCH, NG), row)],
        out_specs=[pl.BlockSpec((tm, D), lambda i: (i, 0))],
        out_shape=[SDS((T, D), BF16)],
        scratch_shapes=[pltpu.VMEM((tm, D), F32)])


def _cumsum64(x, row):
    for s in (1, 2, 4, 8, 16, 32):
        x = x + jnp.where(row >= s, pltpu.roll(x, s, 0), 0.0)
    return x


def _revcumsum64(x, row):
    n = x.shape[0]
    for s in (1, 2, 4, 8, 16, 32):
        x = x + jnp.where(row < HCH - s, pltpu.roll(x, n - s, 0), 0.0)
    return x


def _head_mean(x):
    parts = [jnp.broadcast_to(_mean(x[:, HD * h:HD * (h + 1)]), (x.shape[0], HD)) for h in range(x.shape[1] // HD)]
    return jnp.concatenate(parts, axis=1)


def _seg_sum(x):
    n, c = x.shape
    s = jnp.sum(x.reshape(n // HCH, HCH, c), axis=1, keepdims=True)
    return jnp.broadcast_to(s, (n // HCH, HCH, c)).reshape(n, c)


def _seg_row(x, idx):
    n, c = x.shape
    x3 = x.reshape(n // HCH, HCH, c)
    return jnp.broadcast_to(x3[:, idx:idx + 1, :], x3.shape).reshape(n, c)


def _hgrn_gates(fl, lbv, row):
    s = _sigmoid(fl)
    f = lbv + (1.0 - lbv) * s
    a = _cumsum64(jnp.log(f), row)
    return s, f, a, _seg_row(a, HCH // 2 - 1), _seg_row(a, HCH - 1)


def _hgrn_fwd(proj, lb_table, norm_g, job=None):
    T = proj.shape[1]
    tb = min(HGRN_TOKENS, T)
    nc = tb // HCH

    def body(q_ref, fl_ref, v_ref, g_ref, lbt_ref, gn_ref, o_ref, ob_ref, stb_ref, st_s, o_s):
        @pl.when(pl.program_id(1) == 0)
        def _():
            st_s[...] = jnp.zeros_like(st_s)

        row = lax.broadcasted_iota(jnp.int32, (tb, HW), 0) & (HCH - 1)
        lbv = _sigmoid(lbt_ref[0:1, :] - lbt_ref[1:2, :])
        _, f, a, a_mid, a_last = _hgrn_gates(fl_ref[...], lbv, row)
        k = 1.0 - f
        qs = q_ref[...] * QSCALE
        q_in = (qs * jnp.exp(a - a_mid)).astype(BF16)
        k_in = (k * jnp.exp(a_mid - a)).astype(BF16)
        q_a = (qs * jnp.exp(a)).astype(BF16)
        k_d = (k * jnp.exp(a_last - a)).astype(BF16)
        dec = jnp.exp(a_last)
        vb = v_ref[...].astype(BF16)
        tri = (lax.broadcasted_iota(jnp.int32, (HCH, HCH), 0)
               >= lax.broadcasted_iota(jnp.int32, (HCH, HCH), 1))
        for c in range(nc):
            sl = slice(HCH * c, HCH * (c + 1))
            for hh in range(HGRN_HB):
                hs = slice(HD * hh, HD * (hh + 1))
                st = st_s[hh]
                stb_ref[hh, c] = st
                sc = jnp.where(tri, _mm_nt(q_in[sl, hs], k_in[sl, hs]), 0.0)
                o_s[sl, hs] = _mm(sc.astype(BF16), vb[sl, hs]) + _mm_nt(q_a[sl, hs], st.astype(BF16))
                d64 = dec[sl, hs]
                st_s[hh] = st * jnp.concatenate([d64, d64], axis=0) + _mm_tn(vb[sl, hs], k_d[sl, hs])
        o = o_s[...]
        r = lax.rsqrt(_head_mean(o * o) + EPS)
        g = g_ref[...]
        o_ref[...] = o
        ob_ref[...] = (o * r * gn_ref[...] * (g * _sigmoid(g))).astype(BF16)

    def col(off):
        return pl.BlockSpec((None, tb, HW), lambda h, cb: (off, cb, h))

    return _call(
        body, name="hgrn_fwd", grid=(NH // HGRN_HB, T // tb), job=job,
        args=(proj, proj, proj, proj, lb_table, norm_g),
        in_specs=[col(2), col(3), col(4), col(5),
                  pl.BlockSpec((2, HW), lambda h, cb: (0, h)), pl.BlockSpec((1, HW), lambda h, cb: (0, h))],
        out_specs=[pl.BlockSpec((tb, HW), lambda h, cb: (cb, h)), pl.BlockSpec((tb, HW), lambda h, cb: (cb, h)),
                   pl.BlockSpec((HGRN_HB, nc, HD, HD), lambda h, cb: (h, cb, 0, 0))],
        out_shape=[SDS((T, D), F32), SDS((T, D), BF16), SDS((NH, T // HCH, HD, HD), F32)],
        scratch_shapes=[pltpu.VMEM((HGRN_HB, HD, HD), F32), pltpu.VMEM((tb, HW), F32)])


def _merge_fwd(x, ab, ob, proj, w_a, w_b, w_out, job=None):
    T = x.shape[0]
    tm = min(512, T)

    def body(x_ref, ab_ref, ob_ref, ga_ref, gb_ref, wa_ref, wb_ref, wo_ref, mg_ref, x1_ref):
        ya = _mm(ab_ref[...], wa_ref[...])
        yb = _mm(ob_ref[...], wb_ref[...])
        merged = (_sigmoid(ga_ref[...]) * ya + _sigmoid(gb_ref[...]) * yb).astype(BF16)
        mg_ref[...] = merged
        x1_ref[...] = x_ref[...] + _mm(merged, wo_ref[...])

    t = lambda i: (i, 0)
    w = lambda i: (0, 0)
    return _call(
        body, name="merge_fwd", grid=(T // tm,), job=job, args=(x, ab, ob, proj, proj, w_a, w_b, w_out),
        in_specs=[pl.BlockSpec((tm, D), t), pl.BlockSpec((tm, D), t), pl.BlockSpec((tm, D), t),
                  pl.BlockSpec((None, tm, D), lambda i: (6, i, 0)), pl.BlockSpec((None, tm, D), lambda i: (7, i, 0)),
                  pl.BlockSpec((D, D), w), pl.BlockSpec((D, D), w), pl.BlockSpec((D, D), w)],
        out_specs=[pl.BlockSpec((tm, D), t)] * 2,
        out_shape=[SDS((T, D), BF16), SDS((T, D), F32)])


def _ffn_fwd_bwd(x1, target, g_ffn, g_fin, w_gu, w_down):
    T = x1.shape[0]
    tm = min(256, T)
    inv_d = 1.0 / D

    def body(x1_ref, tg_ref, gf_ref, gn_ref, wgu_ref, wd_ref,
             act_ref, dx2b_ref, h2b_ref, dgu_ref, dx1_ref, dx1b_ref, acc_ref):
        @pl.when(pl.program_id(0) == 0)
        def _():
            acc_ref[...] = jnp.zeros_like(acc_ref)

        x1v = x1_ref[...]
        gf = gf_ref[...]
        gn = gn_ref[...]
        rr1 = lax.rsqrt(_mean(x1v * x1v) + EPS)
        x1n = x1v * rr1
        h2b = (x1n * gf).astype(BF16)
        h2b_ref[...] = h2b
        gate = _mm(h2b, wgu_ref[0])
        up = _mm(h2b, wgu_ref[1])
        sg = _sigmoid(gate)
        si = gate * sg
        act = (si * up).astype(BF16)
        act_ref[...] = act
        x2 = x1v + _mm(act, wd_ref[...])
        rr2 = lax.rsqrt(_mean(x2 * x2) + EPS)
        x2n = x2 * rr2
        e = x2n * gn - tg_ref[...]
        acc_ref[0] += _rows8(e * e) * (0.5 * inv_d)
        dy = e * inv_d
        acc_ref[1] += _rows8(dy * x2n)
        dxn = dy * gn
        dx2 = rr2 * (dxn - x2n * _mean(dxn * x2n))
        dx2b = dx2.astype(BF16)
        dx2b_ref[...] = dx2b
        dact = _mm_nt(dx2b, wd_ref[...])
        dgate = (dact * up * (sg * (1.0 + gate * (1.0 - sg)))).astype(BF16)
        dup = (dact * si).astype(BF16)
        dgu_ref[0] = dgate
        dgu_ref[1] = dup
        dh2 = _mm_nt(dgate, wgu_ref[0]) + _mm_nt(dup, wgu_ref[1])
        acc_ref[2] += _rows8(dh2 * x1n)
        dxn1 = dh2 * gf
        dx1 = dx2 + rr1 * (dxn1 - x1n * _mean(dxn1 * x1n))
        dx1_ref[...] = dx1
        dx1b_ref[...] = dx1.astype(BF16)

    t = lambda i: (i, 0)
    w = lambda i: (0, 0)
    one = pl.Buffered(1)
    return pl.pallas_call(
        body, name="ffn_fwd_bwd", grid=(T // tm,),
        in_specs=[pl.BlockSpec((tm, D), t), pl.BlockSpec((tm, D), t),
                  pl.BlockSpec((1, D), w), pl.BlockSpec((1, D), w),
                  pl.BlockSpec((2, D, FF), lambda i: (0, 0, 0), pipeline_mode=one),
                  pl.BlockSpec((FF, D), w, pipeline_mode=one)],
        out_specs=[pl.BlockSpec((tm, FF), t), pl.BlockSpec((tm, D), t), pl.BlockSpec((tm, D), t),
                   pl.BlockSpec((2, tm, FF), lambda i: (0, i, 0)),
                   pl.BlockSpec((tm, D), t), pl.BlockSpec((tm, D), t),
                   pl.BlockSpec((3, 8, D), lambda i: (0, 0, 0))],
        out_shape=[SDS((T, FF), BF16), SDS((T, D), BF16), SDS((T, D), BF16),
                   SDS((2, T, FF), BF16), SDS((T, D), F32), SDS((T, D), BF16),
                   SDS((3, 8, D), F32)],
        compiler_params=_cparams(),
    )(x1, target, g_ffn, g_fin, w_gu, w_down)


def _merge_bwd(dx1b, ab, ob, proj, w_out, w_a, w_b, job=None):
    T = dx1b.shape[0]
    tm = min(512, T)

    def body(dx_ref, ab_ref, ob_ref, ga_ref, gb_ref, wo_ref, wa_ref, wb_ref, dya_ref, dyb_ref, dp_ref):
        dm = _mm_nt(dx_ref[...], wo_ref[...])
        sa = _sigmoid(ga_ref[...])
        sb = _sigmoid(gb_ref[...])
        dya_ref[...] = (dm * sa).astype(BF16)
        dyb_ref[...] = (dm * sb).astype(BF16)
        dp_ref[0] = (dm * _mm(ab_ref[...], wa_ref[...]) * sa * (1.0 - sa)).astype(BF16)
        dp_ref[1] = (dm * _mm(ob_ref[...], wb_ref[...]) * sb * (1.0 - sb)).astype(BF16)

    t = lambda i: (i, 0)
    w = lambda i: (0, 0)
    return _call(
        body, name="merge_bwd", grid=(T // tm,),
        in_specs=[pl.BlockSpec((tm, D), t), pl.BlockSpec((tm, D), t), pl.BlockSpec((tm, D), t),
                  pl.BlockSpec((None, tm, D), lambda i: (6, i, 0)), pl.BlockSpec((None, tm, D), lambda i: (7, i, 0)),
                  pl.BlockSpec((D, D), w), pl.BlockSpec((D, D), w), pl.BlockSpec((D, D), w)],
        out_specs=[pl.BlockSpec((tm, D), t)] * 2 + [pl.BlockSpec((2, tm, D), lambda i: (3, i, 0))],
        out_shape=[SDS((T, D), BF16), SDS((T, D), BF16), SDS((NIN, T, D), BF16)],
        args=(dx1b, ab, ob, proj, proj, w_out, w_a, w_b), job=job)


def _hgrn_bwd(dproj, dyb, w_b, o_raw, proj, st_before, lb_table, norm_g, job=None):
    T = dyb.shape[0]
    tb = min(HGRN_TOKENS, T)
    nc = tb // HCH
    nb = T // tb

    def body(dp_in, dyb_ref, wb_ref, o_ref, q_ref, fl_ref, v_ref, g_ref, stb_ref, lbt_ref, gn_ref,
             dp_ref, acc_ref, dst_s, dqin_s, dqa_s, dkin_s, dkd_s, dv_s, ddec_s):
        del dp_in

        @pl.when(pl.program_id(1) == 0)
        def _():
            dst_s[...] = jnp.zeros_like(dst_s)
            acc_ref[...] = jnp.zeros_like(acc_ref)

        row = lax.broadcasted_iota(jnp.int32, (tb, HW), 0) & (HCH - 1)
        gn = gn_ref[...]
        lbv = _sigmoid(lbt_ref[0:1, :] - lbt_ref[1:2, :])
        o = o_ref[...]
        r = lax.rsqrt(_head_mean(o * o) + EPS)
        on = o * r
        g = g_ref[...]
        sgm = _sigmoid(g)
        dob_v = _mm_nt(dyb_ref[...], wb_ref[...])
        dp_ref[3] = (dob_v * on * gn * (sgm * (1.0 + g * (1.0 - sgm)))).astype(BF16)
        do_n = dob_v * (g * sgm)
        acc_ref[1] += _rows8(do_n * on)
        dxn = do_n * gn
        do = (r * (dxn - on * _head_mean(dxn * on))).astype(BF16)
        s, f, a, a_mid, a_last = _hgrn_gates(fl_ref[...], lbv, row)
        k = 1.0 - f
        qs = q_ref[...] * QSCALE
        e_q = jnp.exp(a - a_mid)
        e_k = jnp.exp(a_mid - a)
        e_a = jnp.exp(a)
        e_l = jnp.exp(a_last - a)
        dec = jnp.exp(a_last)
        q_in = qs * e_q
        k_in = k * e_k
        q_a = qs * e_a
        k_d = k * e_l
        q_inb, k_inb, q_ab, k_db = (z.astype(BF16) for z in (q_in, k_in, q_a, k_d))
        vb = v_ref[...].astype(BF16)
        tri = (lax.broadcasted_iota(jnp.int32, (HCH, HCH), 0)
               >= lax.broadcasted_iota(jnp.int32, (HCH, HCH), 1))
        for c in reversed(range(nc)):
            sl = slice(HCH * c, HCH * (c + 1))
            for hh in range(HGRN_HB):
                hs = slice(HD * hh, HD * (hh + 1))
                stp = stb_ref[hh, c]
                dst = dst_s[hh]
                dstb = dst.astype(BF16)
                do_c = do[sl, hs]
                v_c = vb[sl, hs]
                dqa_s[sl, hs] = _mm(do_c, stp.astype(BF16))
                dkd_s[sl, hs] = _mm(v_c, dstb)
                ddec_s[sl, hs] = jnp.broadcast_to(jnp.sum(dst * stp, axis=0, keepdims=True), (HCH, HD))
                sc = jnp.where(tri, _mm_nt(q_inb[sl, hs], k_inb[sl, hs]), 0.0).astype(BF16)
                dsc = jnp.where(tri, _mm_nt(do_c, v_c), 0.0).astype(BF16)
                dv_s[sl, hs] = _mm_nt(k_db[sl, hs], dstb) + _mm_tn(sc, do_c)
                dqin_s[sl, hs] = _mm(dsc, k_inb[sl, hs])
                dkin_s[sl, hs] = _mm_tn(dsc, q_inb[sl, hs])
                d64 = dec[sl, hs]
                dst_s[hh] = dst * jnp.concatenate([d64, d64], axis=0) + _mm_tn(do_c, q_ab[sl, hs])
        dq_in = dqin_s[...]
        dq_a = dqa_s[...]
        dk_in = dkin_s[...]
        dk_d = dkd_s[...]
        dp_ref[0] = ((dq_in * e_q + dq_a * e_a) * QSCALE).astype(BF16)
        dp_ref[2] = dv_s[...].astype(BF16)
        tq = dq_in * q_in
        tk = dk_in * k_in
        td = dk_d * k_d
        d_a = tq + dq_a * q_a - tk - td
        d_a = d_a + jnp.where(row == HCH // 2 - 1, _seg_sum(tk - tq), 0.0)
        d_a = d_a + jnp.where(row == HCH - 1, _seg_sum(td) + ddec_s[...] * dec, 0.0)
        dlf = _revcumsum64(d_a, row)
        df = dlf / f - (dk_in * e_k + dk_d * e_l)
        dp_ref[1] = (df * (1.0 - lbv) * s * (1.0 - s)).astype(BF16)
        acc_ref[0] += _rows8(df * (1.0 - s))

    def col(off):
        return pl.BlockSpec((None, tb, HW), lambda h, cb: (off, nb - 1 - cb, h))

    hb = lambda h, cb: (nb - 1 - cb, h)
    return _call(
        body, name="hgrn_bwd", grid=(NH // HGRN_HB, nb), job=job,
        args=(dproj, dyb, w_b, o_raw, proj, proj, proj, proj, st_before, lb_table, norm_g),
        in_specs=[ANY, pl.BlockSpec((tb, D), lambda h, cb: (nb - 1 - cb, 0)),
                  pl.BlockSpec((HW, D), lambda h, cb: (h, 0)), pl.BlockSpec((tb, HW), hb),
                  col(2), col(3), col(4), col(5),
                  pl.BlockSpec((HGRN_HB, nc, HD, HD), lambda h, cb: (h, nb - 1 - cb, 0, 0)),
                  pl.BlockSpec((2, HW), lambda h, cb: (0, h)), pl.BlockSpec((1, HW), lambda h, cb: (0, h))],
        out_specs=[pl.BlockSpec((4, tb, HW), lambda h, cb: (0, nb - 1 - cb, h)),
                   pl.BlockSpec((2, 8, HW), lambda h, cb: (0, 0, h))],
        out_shape=[SDS(dproj.shape, BF16), SDS((2, 8, D), F32)],
        scratch_shapes=[pltpu.VMEM((HGRN_HB, HD, HD), F32)] + [pltpu.VMEM((tb, HW), F32)] * 6,
        aliases={0: 0})


def _gmlp_bwd(dproj, dya, w_a, proj, ln_g, ln_b, wm, wm_t, b_t):
    T = dya.shape[0]
    tm = min(GMLP_BWD_TOKENS, T)

    def body(dp_in, dya_ref, wa_ref, u_ref, v_ref, lg_ref, lb_ref, wm_ref, wmt_ref, bt_ref,
             dp_ref, acc_ref, dws_ref, dmix_ref, du_s, dvn_s):
        del dp_in

        @pl.when(pl.program_id(0) == 0)
        def _():
            acc_ref[...] = jnp.zeros_like(acc_ref)
            dws_ref[...] = jnp.zeros_like(dws_ref)
            dmix_ref[...] = jnp.zeros_like(dmix_ref)

        u = u_ref[...]
        v = v_ref[...]
        lg = lg_ref[...]
        gu, t_u = _gelu(u)
        gv, t_v = _gelu(v)
        vhat, rs = _layer_norm_stats(gv)
        vnb = (vhat * lg + lb_ref[...]).astype(BF16)
        da_v = _mm_nt(dya_ref[...], wa_ref[...])
        for g in range(NG):
            cols = slice(128 * g, 128 * (g + 1))
            vng = _chunks_abreast(vnb[:, cols])
            mixed = _mm(wm_ref[g], vng) + bt_ref[:, g:g + 1]
            dag = _chunks_abreast(da_v[:, cols])
            dmx = dag * _chunks_abreast(gu[:, cols])
            du_s[:, cols] = _chunks_stacked(dag * mixed)
            dmxb = dmx.astype(BF16)
            dws_ref[:, cols] += _mm_nt(dmxb, vng)
            dmix_ref[:, cols] += sum(dmx[:, GCH * ch:GCH * (ch + 1)] for ch in range(tm // GCH))
            dvn_s[:, cols] = _chunks_stacked(_mm(wmt_ref[g], dmxb))
        dp_ref[0] = (du_s[...] * _gelu_grad(u, t_u)).astype(BF16)
        dvn = dvn_s[...]
        acc_ref[0] += _rows8(dvn * vhat)
        acc_ref[1] += _rows8(dvn)
        dvh = dvn * lg
        dgv = rs * (dvh - _mean(dvh) - vhat * _mean(dvh * vhat))
        dp_ref[1] = (dgv * _gelu_grad(v, t_v)).astype(BF16)

    row = lambda i: (0, 0)
    w3 = lambda i: (0, 0, 0)
    return pl.pallas_call(
        body, name="gmlp_bwd", grid=(T // tm,),
        in_specs=[ANY, pl.BlockSpec((tm, D), lambda i: (i, 0)), pl.BlockSpec((D, D), row),
                  pl.BlockSpec((None, tm, D), lambda i: (0, i, 0)), pl.BlockSpec((None, tm, D), lambda i: (1, i, 0)),
                  pl.BlockSpec((1, D), row), pl.BlockSpec((1, D), row),
                  pl.BlockSpec((NG, GCH, GCH), w3), pl.BlockSpec((NG, GCH, GCH), w3),
                  pl.BlockSpec((GCH, NG), row)],
        out_specs=[pl.BlockSpec((2, tm, D), lambda i: (2, i, 0)),
                   pl.BlockSpec((2, 8, D), w3), pl.BlockSpec((GCH, D), row), pl.BlockSpec((GCH, D), row)],
        out_shape=[SDS(dproj.shape, BF16), SDS((2, 8, D), F32), SDS((GCH, D), F32), SDS((GCH, D), F32)],
        scratch_shapes=[pltpu.VMEM((tm, D), F32), pltpu.VMEM((tm, D), F32)],
        input_output_aliases={0: 0},
        compiler_params=_cparams(),
    )(dproj, dya, w_a, proj, proj, ln_g, ln_b, wm, wm_t, b_t)


def _proj_bwd(dproj, w_in4, x, dx1, g_mix, job=None):
    T = x.shape[0]
    tm = min(256, T)
    order = (2, 3, 4, 5, 0, 1, 6, 7)

    def body(dp_ref, w_ref, x_ref, dx1_ref, g_ref, gx_ref, acc_ref):
        @pl.when(pl.program_id(0) == 0)
        def _():
            acc_ref[...] = jnp.zeros_like(acc_ref)

        dh = None
        for m, og in enumerate(order):
            part = _mm_nt(dp_ref[m], w_ref[og // 2, :, D * (og % 2):D * (og % 2 + 1)])
            dh = part if dh is None else dh + part
        xv = x_ref[...]
        r = lax.rsqrt(_mean(xv * xv) + EPS)
        xn = xv * r
        acc_ref[...] += _rows8(dh * xn)
        dxn = dh * g_ref[...]
        gx_ref[...] = dx1_ref[...] + r * (dxn - xn * _mean(dxn * xn))

    t = lambda i: (i, 0)
    return _call(
        body, name="proj_bwd", grid=(T // tm,),
        in_specs=[pl.BlockSpec((NIN, tm, D), lambda i: (0, i, 0)),
                  pl.BlockSpec((NCHIP, D, 2 * D), lambda i: (0, 0, 0), pipeline_mode=pl.Buffered(1)),
                  pl.BlockSpec((tm, D), t), pl.BlockSpec((tm, D), t), pl.BlockSpec((1, D), lambda i: (0, 0))],
        out_specs=[pl.BlockSpec((tm, D), t), pl.BlockSpec((8, D), lambda i: (0, 0))],
        out_shape=[SDS((T, D), F32), SDS((8, D), F32)],
        args=(dproj, w_in4, x, dx1, g_mix), job=job)


def _dw_call(name, a, b, a_spec, b_spec, o_spec, out_shape, nblk, tt, job=None, prefetch=None):
    T = a.shape[-2]

    def body(*refs):
        a_ref, b_ref, o_ref = refs[-3:]

        @pl.when(pl.program_id(1) == 0)
        def _():
            o_ref[...] = jnp.zeros_like(o_ref)
        o_ref[...] += _mm_tn(a_ref[...], b_ref[...])

    (out,), job_out = _call(
        body, name=name, grid=(nblk, T // tt), in_specs=[a_spec, b_spec], out_specs=[o_spec],
        out_shape=[out_shape], args=(a, b), job=job, prefetch=prefetch)
    return out, job_out


def _dw_in_half(name, place, hb, dproj, mine, job=None):
    tt = min(DW_TOKENS, hb.shape[0])

    def comp(k, pc):
        return _component_of(2 * k + (pc[1] if mine else 1 - pc[1]))

    return _dw_call(
        name, hb, dproj,
        pl.BlockSpec((tt, D), lambda k, t, pc: (t, 0)),
        pl.BlockSpec((None, tt, D), lambda k, t, pc: (comp(k, pc), t, 0)),
        pl.BlockSpec((None, D, D), lambda k, t, pc: (k, 0, 0)),
        SDS((NCHIP, D, D), F32), NCHIP, tt, job, place)


def _dw_gate_up(h2b, dgu, job=None):
    tt = min(DW_TOKENS, h2b.shape[0])
    return _dw_call(
        "dw_gate_up", h2b, dgu,
        pl.BlockSpec((tt, D), lambda k, t: (t, 0)),
        pl.BlockSpec((None, tt, FFS), lambda k, t: (k // 2, t, k % 2)),
        pl.BlockSpec((None, D, FFS), lambda k, t: (k, 0, 0)),
        SDS((NCHIP, D, FFS), F32), NCHIP, tt, job)


def _dw_down(act, dx2b, job=None):
    tt = min(DW_TOKENS, act.shape[0])
    g, job_out = _dw_call(
        "dw_down", act, dx2b,
        pl.BlockSpec((tt, FFS), lambda k, t: (t, k)),
        pl.BlockSpec((tt, D), lambda k, t: (t, 0)),
        pl.BlockSpec((FFS, D), lambda k, t: (k, 0)),
        SDS((FF, D), F32), 2, tt, job)
    return g.reshape(NCHIP, FF // NCHIP, D), job_out


def _dw_square(name, a, b, job=None):
    tt = min(DW_TOKENS, a.shape[0])
    g, job_out = _dw_call(
        name, a, b,
        pl.BlockSpec((tt, D), lambda k, t: (t, 0)), pl.BlockSpec((tt, D), lambda k, t: (t, 0)),
        pl.BlockSpec((D, D), lambda k, t: (0, 0)), SDS((D, D), F32), 1, tt, job)
    return g.reshape(NCHIP, D // NCHIP, D), job_out


def _place():
    x, y, c = lax.axis_index("x"), lax.axis_index("y"), lax.axis_index("c")
    return x, y, c, 2 * x + y


def _chip_at(x, y, s):
    return x ^ (s >> 1), y ^ (s & 1)


class _Job:
    def __init__(self, ins, out_shapes, sems, start, finish, aliases=None, mid=None):
        self.ins, self.out_shapes, self.sems = list(ins), list(out_shapes), list(sems)
        self.start, self.finish, self.aliases = start, finish, dict(aliases or {})
        self.mid = mid if mid is not None else (lambda ins, outs, sems: None)


def _join_jobs(*jobs):
    def cut(refs, sizes):
        out, at = [], 0
        for n in sizes:
            out.append(refs[at:at + n])
            at += n
        return out

    ni = [len(j.ins) for j in jobs]
    no = [len(j.out_shapes) for j in jobs]
    ns = [len(j.sems) for j in jobs]

    def run(which):
        def go(ins, outs, sems):
            for j, a, b, c in zip(jobs, cut(ins, ni), cut(outs, no), cut(sems, ns)):
                getattr(j, which)(a, b, c)
        return go

    aliases = {}
    for k, j in enumerate(jobs):
        for a, b in j.aliases.items():
            aliases[sum(ni[:k]) + a] = sum(no[:k]) + b
    return _Job([a for j in jobs for a in j.ins], [o for j in jobs for o in j.out_shapes],
                [s for j in jobs for s in j.sems], run("start"), run("finish"), aliases, run("mid"))


def _call(body, *, name, grid, in_specs, out_specs, out_shape, args, scratch_shapes=(), aliases=None,
          job=None, prefetch=None):
    n_in, n_out, n_scr = len(in_specs), len(out_specs), len(scratch_shapes)
    npf = 0 if prefetch is None else 1
    job = job if job is not None else _Job([], [], [], lambda *a: None, lambda *a: None)
    ji, jo = len(job.ins), len(job.out_shapes)
    steps = math.prod(grid)

    def wrapped(*refs):
        pf, refs = refs[:npf], refs[npf:]
        ins, jin = refs[:n_in], refs[n_in:n_in + ji]
        o0 = n_in + ji
        outs, jout = refs[o0:o0 + n_out], refs[o0 + n_out:o0 + n_out + jo]
        s0 = o0 + n_out + jo
        scr, jsem = refs[s0:s0 + n_scr], refs[s0 + n_scr:]
        step = functools.reduce(lambda acc, ag: acc * ag[1] + pl.program_id(ag[0]), enumerate(grid), 0)
        if ji or jo:
            @pl.when(step == 0)
            def _():
                job.start(jin, jout, jsem)

        body(*pf, *ins, *outs, *scr)

        if ji or jo:
            @pl.when(step == steps // 2)
            def _():
                job.mid(jin, jout, jsem)

            @pl.when(step == steps - 1)
            def _():
                job.finish(jin, jout, jsem)

    io = {npf + a: b for a, b in dict(aliases or {}).items()}
    io.update({npf + n_in + a: n_out + b for a, b in job.aliases.items()})
    kw = dict(in_specs=list(in_specs) + [ANY] * ji, out_specs=list(out_specs) + [ANY] * jo,
              scratch_shapes=list(scratch_shapes) + job.sems)
    if npf:
        kw = dict(grid_spec=pltpu.PrefetchScalarGridSpec(num_scalar_prefetch=1, grid=grid, **kw))
    else:
        kw["grid"] = grid
    res = pl.pallas_call(
        wrapped, name=name, out_shape=list(out_shape) + job.out_shapes, input_output_aliases=io,
        compiler_params=_cparams(has_side_effects=bool(ji or jo)), **kw,
    )(*(() if prefetch is None else (prefetch,)), *args, *job.ins)
    return list(res[:n_out]), list(res[n_out:])


def _cast_shards(name, place, ws, paired=False):
    n = len(ws)
    rows, cols = ws[0].shape
    tr = 352 if rows % 352 == 0 else 256
    shape = (2, rows, 2 * cols) if paired else (NCHIP, rows, cols)
    mine = (lambda i, pc: (pc[0] // 2, i, pc[0] % 2)) if paired else (lambda i, pc: (pc[0], i, 0))

    def body(pc_ref, *refs):
        del pc_ref
        for w_ref, o_ref in zip(refs[:n], refs[n:]):
            o_ref[...] = w_ref[...].astype(BF16)

    return pl.pallas_call(
        body, name=name,
        grid_spec=pltpu.PrefetchScalarGridSpec(
            num_scalar_prefetch=1, grid=(rows // tr,),
            in_specs=[pl.BlockSpec((tr, cols), lambda i, pc: (i, 0))] * n,
            out_specs=[pl.BlockSpec((None, tr, cols), mine)] * n),
        out_shape=[SDS(shape, BF16)] * n,
        compiler_params=_cparams(),
    )(place, *ws)


def _sibling_copy(ref, send_sem, recv_sem):
    x, y, c, _ = _place()
    return pltpu.make_async_remote_copy(src_ref=ref, dst_ref=ref, send_sem=send_sem, recv_sem=recv_sem,
                                        device_id=(x, y, 1 - c), device_id_type=MESH)


def _slot(arr, chip):
    if arr.shape[0] == NCHIP:
        return arr.at[chip]
    cols = arr.shape[2] // 2
    return arr.at[chip // 2, :, pl.ds(pl.multiple_of((chip % 2) * cols, 128), cols)]


def _half_rows(arr, slot, core):
    half = arr.shape[1] // 2
    return _slot(arr, slot).at[pl.ds(pl.multiple_of(core * half, 16), half)]


def _quarter_rows(arr, slot, core, q):
    quarter = arr.shape[1] // 4
    return _slot(arr, slot).at[pl.ds(pl.multiple_of((2 * core + q) * quarter, 16), quarter)]


def _chip_copy(ref, dist, send_sem, recv_sem):
    x, y, c, _ = _place()
    cx, cy = _chip_at(x, y, dist)
    return pltpu.make_async_remote_copy(src_ref=ref, dst_ref=ref, send_sem=send_sem, recv_sem=recv_sem,
                                        device_id=(cx, cy, c), device_id_type=MESH)


def _gather_sems(n):
    dma = pltpu.SemaphoreType.DMA
    return [dma((n, 2))] * 4 + [dma((n, 4))] * 2


def _gather_start(arrs, sems):
    dsend, drecv = sems[0], sems[1]
    _, _, c, j = _place()
    for w, arr in enumerate(arrs):
        for dist in (1, 2):
            _chip_copy(_half_rows(arr, j, c), dist, dsend.at[w, dist - 1], drecv.at[w, dist - 1]).start()


def _gather_land(arrs, sems, dist, first=0):
    dsend, drecv, rsend, rrecv, fsend, frecv = sems
    _, _, c, j = _place()
    if dist < 3:
        other = 3 - dist
        for w, arr in enumerate(arrs, first):
            landed = _half_rows(arr, j ^ dist, c)
            _chip_copy(landed, dist, dsend.at[w, dist - 1], drecv.at[w, dist - 1]).wait_recv()
            relay = _quarter_rows(arr, j ^ dist, c, other - 1)
            _chip_copy(relay, other, rsend.at[w, other - 1], rrecv.at[w, other - 1]).start()
            _sibling_copy(landed, fsend.at[w, dist - 1], frecv.at[w, dist - 1]).start()
        for w, arr in enumerate(arrs, first):
            theirs = _half_rows(arr, j ^ dist, 1 - c)
            _sibling_copy(theirs, fsend.at[w, dist - 1], frecv.at[w, dist - 1]).wait_recv()
    else:
        for w, arr in enumerate(arrs, first):
            for via in (1, 2):
                piece = _quarter_rows(arr, j ^ 3, c, via - 1)
                _chip_copy(piece, via, rsend.at[w, via - 1], rrecv.at[w, via - 1]).wait_recv()
                _sibling_copy(piece, fsend.at[w, 1 + via], frecv.at[w, 1 + via]).start()
        for w, arr in enumerate(arrs, first):
            for via in (1, 2):
                theirs = _quarter_rows(arr, j ^ 3, 1 - c, via - 1)
                _sibling_copy(theirs, fsend.at[w, 1 + via], frecv.at[w, 1 + via]).wait_recv()


def _gather_drain(arrs, sems):
    dsend, drecv, rsend, rrecv, fsend, frecv = sems
    _, _, c, j = _place()
    for w, arr in enumerate(arrs):
        for dist in (1, 2):
            other = 3 - dist
            _chip_copy(_half_rows(arr, j, c), dist, dsend.at[w, dist - 1], drecv.at[w, dist - 1]).wait_send()
            _chip_copy(_quarter_rows(arr, j ^ dist, c, other - 1), other,
                       rsend.at[w, other - 1], rrecv.at[w, other - 1]).wait_send()
            _sibling_copy(_half_rows(arr, j ^ dist, c), fsend.at[w, dist - 1], frecv.at[w, dist - 1]).wait_send()
            _sibling_copy(_quarter_rows(arr, j ^ 3, c, dist - 1),
                          fsend.at[w, 1 + dist], frecv.at[w, 1 + dist]).wait_send()


def _gather_neighbours(arrs, sems):
    _gather_land(arrs, sems, 1)
    _gather_land(arrs, sems, 2)


def _gather_finish(arrs, sems):
    _gather_land(arrs, sems, 3)
    _gather_drain(arrs, sems)


def _gather_job(arrs):
    n = len(arrs)
    return _Job(arrs, [SDS(a.shape, a.dtype) for a in arrs], _gather_sems(n),
                lambda ins, outs, sems: _gather_start(outs, sems),
                lambda ins, outs, sems: _gather_finish(outs, sems), {k: k for k in range(n)},
                mid=lambda ins, outs, sems: _gather_neighbours(outs, sems))


def _exchange_job(arrs, out_shapes, n, copies):
    def start(ins, outs, sems):
        for cp in copies(ins, outs, sems[0], sems[1]):
            cp.start()

    def finish(ins, outs, sems):
        for cp in copies(ins, outs, sems[0], sems[1]):
            cp.wait()

    return _Job(arrs, out_shapes, [pltpu.SemaphoreType.DMA((n,))] * 2, start, finish)


def _pair_exchange_job(grads):
    def copies(ins, outs, send_sem, recv_sem):
        x, y, c, _ = _place()
        res = []
        for w in range(len(grads)):
            half = ins[w].shape[1] // 2
            theirs = pl.ds(pl.multiple_of((1 - c) * half, 8), half)
            res.append(pltpu.make_async_remote_copy(
                src_ref=ins[w].at[:, theirs, :], dst_ref=outs[w], send_sem=send_sem.at[w],
                recv_sem=recv_sem.at[w], device_id=(x, y, 1 - c), device_id_type=MESH))
        return res

    return _exchange_job(grads, [SDS((NCHIP, g.shape[1] // 2, g.shape[2]), F32) for g in grads],
                         len(grads), copies)


def _row_tile(rows, cols):
    tr = rows
    while tr * cols * 4 > ELEMENTWISE_BLOCK_BYTES and tr % 32 == 0:
        tr //= 2
    return tr


def _pair_sums(name, place, gs, sibs):
    n = len(gs)
    half, cols = sibs[0].shape[1], sibs[0].shape[2]
    tr = _row_tile(half, cols)
    nt = half // tr
    mine = nt if gs[0].shape[1] == 2 * half else 0

    def body(pc_ref, *refs):
        del pc_ref
        for g_ref, s_ref, own_ref, out_ref in zip(refs[:n], refs[n:2 * n], refs[2 * n:3 * n], refs[3 * n:]):
            v = g_ref[...] + s_ref[...]

            @pl.when(pl.program_id(1) == 0)
            def _():
                own_ref[...] = v

            @pl.when(pl.program_id(1) > 0)
            def _():
                out_ref[...] = v.astype(BF16)

    res = pl.pallas_call(
        body, name=name,
        grid_spec=pltpu.PrefetchScalarGridSpec(
            num_scalar_prefetch=1, grid=(nt, NCHIP),
            in_specs=[pl.BlockSpec((None, tr, cols), lambda i, s, pc: (pc[0] ^ s, pc[1] * mine + i, 0))] * n
            + [pl.BlockSpec((None, tr, cols), lambda i, s, pc: (pc[0] ^ s, i, 0))] * n,
            out_specs=[pl.BlockSpec((tr, cols), lambda i, s, pc: (i, 0))] * n
            + [pl.BlockSpec((None, tr, cols), lambda i, s, pc: (jnp.maximum(s - 1, 0), i, 0))] * n),
        out_shape=[SDS((half, cols), F32)] * n + [SDS((NCHIP - 1, half, cols), BF16)] * n,
        compiler_params=_cparams(),
    )(place, *gs, *sibs)
    return res[:n], res[n:]


def _chip_exchange_job(parts):
    def copies(ins, outs, send_sem, recv_sem):
        x, y, c, _ = _place()
        res = []
        for w in range(len(parts)):
            for s in range(1, NCHIP):
                cx, cy = _chip_at(x, y, s)
                k = w * (NCHIP - 1) + s - 1
                res.append(pltpu.make_async_remote_copy(
                    src_ref=ins[w].at[s - 1], dst_ref=outs[w].at[s - 1], send_sem=send_sem.at[k],
                    recv_sem=recv_sem.at[k], device_id=(cx, cy, c), device_id_type=MESH))
        return res

    return _exchange_job(parts, [SDS((NCHIP - 1,) + p.shape[1:], BF16) for p in parts],
                         len(parts) * (NCHIP - 1), copies)


def _chip_sums(name, owns, rems):
    n = len(owns)
    half, cols = owns[0].shape
    tr = _row_tile(half, cols)

    def body(*refs):
        for own_ref, rem_ref, out_ref in zip(refs[:n], refs[n:2 * n], refs[2 * n:]):
            out_ref[...] = (((own_ref[...] + rem_ref[0].astype(F32)) + rem_ref[1].astype(F32))
                            + rem_ref[2].astype(F32))

    return pl.pallas_call(
        body, name=name, grid=(half // tr,),
        in_specs=[pl.BlockSpec((tr, cols), lambda i: (i, 0))] * n
        + [pl.BlockSpec((NCHIP - 1, tr, cols), lambda i: (0, i, 0))] * n,
        out_specs=[pl.BlockSpec((tr, cols), lambda i: (i, 0))] * n,
        out_shape=[SDS((half, cols), F32)] * n,
        compiler_params=_cparams(),
    )(*owns, *rems)


def _share_halves_job(halves):
    def copies(ins, outs, send_sem, recv_sem):
        x, y, c, _ = _place()
        return [pltpu.make_async_remote_copy(
            src_ref=ins[w], dst_ref=outs[w], send_sem=send_sem.at[w], recv_sem=recv_sem.at[w],
            device_id=(x, y, 1 - c), device_id_type=MESH) for w in range(len(halves))]

    return _exchange_job(halves, [SDS(h.shape, F32) for h in halves], len(halves), copies)


def _adamw_math(w, g, m, v):
    m = B1 * m + (1.0 - B1) * g
    v = B2 * v + (1.0 - B2) * (g * g)
    m_hat = m / (1.0 - B1 ** STEP)
    v_hat = v / (1.0 - B2 ** STEP)
    delta = -LR * (m_hat / (jnp.sqrt(v_hat) + AEPS) + WD * w)
    return delta, m, v


def _adamws(name, place, ws, owns, sibs, ms, vs):
    n = len(ws)
    rows, cols = ws[0].shape
    by_cols = owns[0].shape[0] == rows
    half, pc_cols = (rows, cols // 2) if by_cols else (rows // 2, cols)
    tr = _row_tile(half, pc_cols)
    nt = half // tr

    def body(pc_ref, *refs):
        ins, outs = refs[:5 * n], refs[5 * n:]
        for k in range(n):
            w_ref, own_ref, sib_ref, m_ref, v_ref = ins[5 * k:5 * k + 5]
            g = jnp.where(pl.program_id(0) == pc_ref[1], own_ref[...], sib_ref[...])
            d, mn, vn = _adamw_math(w_ref[...], g, m_ref[...], v_ref[...])
            for ref, val in zip(outs[4 * k:4 * k + 4], (g, d, mn, vn)):
                ref[...] = val

    full = pl.BlockSpec((tr, pc_cols), (lambda h, i, pc: (i, h)) if by_cols else (lambda h, i, pc: (h * nt + i, 0)))
    part = pl.BlockSpec((tr, pc_cols), lambda h, i, pc: (i, 0))
    res = pl.pallas_call(
        body, name=name,
        grid_spec=pltpu.PrefetchScalarGridSpec(
            num_scalar_prefetch=1, grid=(2, nt),
            in_specs=[full, part, part, full, full] * n, out_specs=[full] * (4 * n)),
        out_shape=[SDS((rows, cols), F32)] * (4 * n),
        compiler_params=_cparams(),
    )(place, *[a for group in zip(ws, owns, sibs, ms, vs) for a in group])
    return [tuple(res[4 * k:4 * k + 4]) for k in range(n)]


def _small_allreduce_adamw(sp, w, m, v, job):
    shape = sp.shape
    ji, jo = len(job.ins), len(job.out_shapes)

    def body(sp_ref, w_ref, m_ref, v_ref, *rest):
        jin, (g_ref, d_ref, mo_ref, vo_ref), jout = rest[:ji], rest[ji:ji + 4], rest[ji + 4:ji + 4 + jo]
        sib_s, pair_s, chip_s, send_sem, recv_sem = rest[ji + 4 + jo:ji + 9 + jo]
        jsem = rest[ji + 9 + jo:]
        job.start(jin, jout, jsem)
        x, y, c, j = _place()
        cp = pltpu.make_async_remote_copy(
            src_ref=sp_ref, dst_ref=sib_s, send_sem=send_sem.at[0], recv_sem=recv_sem.at[0],
            device_id=(x, y, 1 - c), device_id_type=MESH)
        cp.start()
        cp.wait()
        pair_s[...] = sp_ref[...] + sib_s[...]
        half = shape[0] // 2
        mine = pl.ds(pl.multiple_of(c * half, 8), half)
        cps = []
        for s in range(1, NCHIP):
            cx, cy = _chip_at(x, y, s)
            cp = pltpu.make_async_remote_copy(
                src_ref=pair_s.at[mine], dst_ref=chip_s.at[s, mine], send_sem=send_sem.at[s],
                recv_sem=recv_sem.at[s], device_id=(cx, cy, c), device_id_type=MESH)
            cp.start()
            cps.append(cp)
        chip_s[0] = pair_s[...]
        for cp in cps:
            cp.wait()
        cps = []
        for s in range(1, NCHIP):
            cp = pltpu.make_async_remote_copy(
                src_ref=chip_s.at[s, mine], dst_ref=chip_s.at[s, mine], send_sem=send_sem.at[NCHIP + s],
                recv_sem=recv_sem.at[NCHIP + s], device_id=(x, y, 1 - c), device_id_type=MESH)
            cp.start()
            cps.append(cp)
        for cp in cps:
            cp.wait()
        tot = chip_s[j]
        for k in range(1, NCHIP):
            tot = tot + chip_s[k ^ j]
        g_ref[...] = tot
        d, mn, vn = _adamw_math(w_ref[...], tot, m_ref[...], v_ref[...])
        d_ref[...] = d
        mo_ref[...] = mn
        vo_ref[...] = vn
        job.mid(jin, jout, jsem)
        job.finish(jin, jout, jsem)

    vm = pl.BlockSpec(memory_space=pltpu.VMEM)
    res = pl.pallas_call(
        body, name="small_allreduce_adamw",
        in_specs=[vm] * 4 + [ANY] * ji, out_specs=[vm] * 4 + [ANY] * jo,
        out_shape=[SDS(shape, F32)] * 4 + job.out_shapes,
        scratch_shapes=[pltpu.VMEM(shape, F32), pltpu.VMEM(shape, F32), pltpu.VMEM((NCHIP,) + shape, F32),
                        pltpu.SemaphoreType.DMA((2 * NCHIP,)), pltpu.SemaphoreType.DMA((2 * NCHIP,))] + job.sems,
        input_output_aliases={4 + a: 4 + b for a, b in job.aliases.items()},
        compiler_params=pltpu.CompilerParams(has_side_effects=True),
    )(sp, w, m, v, *job.ins)
    return res[:4], res[4:]


def _pack_small(first, mix, ln_g, ln_b, b_s, lbt, hn, ffn, fin, w_s):
    rows = [first.reshape(1, D), mix.reshape(1, D), ln_g.reshape(1, D), ln_b.reshape(1, D),
            b_s.reshape(1, D), lbt.reshape(2, D), hn.reshape(1, D), ffn.reshape(1, D), fin.reshape(1, D),
            jnp.zeros((6, D), F32)]
    return jnp.concatenate(rows + [w_s.reshape(NG, GCH, GCH).transpose(1, 0, 2).reshape(GCH, D)], axis=0)


def _unpack_small(p):
    w_s = p[16:].reshape(GCH, NG, GCH).transpose(1, 0, 2).reshape(1, NG, GCH, GCH)
    return dict(norm_mix_g=p[1:2], gmlp_ln_g=p[2:3], gmlp_ln_b=p[3:4], gmlp_b_s=p[4].reshape(1, NG, GCH),
                hgrn_lb_table=p[5:7], hgrn_norm_g=p[7:8], norm_ffn_g=p[8:9], norm_final_g=p[9],
                gmlp_w_s=w_s)


SMALL = ("norm_mix_g", "gmlp_ln_g", "gmlp_ln_b", "gmlp_w_s", "gmlp_b_s", "hgrn_lb_table", "hgrn_norm_g",
         "norm_ffn_g", "norm_final_g")
BIG = ("w_in", "w_gate_up", "w_branch_a", "w_branch_b", "w_out", "w_down")
ORDER = ("norm_mix_g", "w_in", "gmlp_ln_g", "gmlp_ln_b", "gmlp_w_s", "gmlp_b_s", "hgrn_lb_table",
         "hgrn_norm_g", "w_branch_a", "w_branch_b", "w_out", "norm_ffn_g", "w_gate_up", "w_down",
         "norm_final_g")


def kernel(x, norm_mix_g, w_in, gmlp_ln_g, gmlp_ln_b, gmlp_w_s, gmlp_b_s, hgrn_lb_table, hgrn_norm_g, w_branch_a, w_branch_b, w_out, norm_ffn_g, w_gate_up, w_down, norm_final_g, loss_target, m_norm_mix_g, m_w_in, m_gmlp_ln_g, m_gmlp_ln_b, m_gmlp_w_s, m_gmlp_b_s, m_hgrn_lb_table, m_hgrn_norm_g, m_w_branch_a, m_w_branch_b, m_w_out, m_norm_ffn_g, m_w_gate_up, m_w_down, m_norm_final_g, v_norm_mix_g, v_w_in, v_gmlp_ln_g, v_gmlp_ln_b, v_gmlp_w_s, v_gmlp_b_s, v_hgrn_lb_table, v_hgrn_norm_g, v_w_branch_a, v_w_branch_b, v_w_out, v_norm_ffn_g, v_w_gate_up, v_w_down, v_norm_final_g):
    args = dict(locals())
    T = x.shape[1]
    xs = x.reshape(T, D)
    target = loss_target.reshape(T, D)
    big = {n: args[n].reshape(args[n].shape[1:]) for n in BIG}
    big_m = {n: args["m_" + n].reshape(args[n].shape[1:]) for n in BIG}
    big_v = {n: args["v_" + n].reshape(args[n].shape[1:]) for n in BIG}

    x_i, y_i, c_i = lax.axis_index("x"), lax.axis_index("y"), lax.axis_index("c")
    place = jnp.stack([2 * x_i + y_i, c_i]).astype(jnp.int32)
    def by_shape(names):
        groups = []
        for n in names:
            if groups and big[groups[-1][0]].shape == big[n].shape:
                groups[-1].append(n)
            else:
                groups.append([n])
        return groups

    cast = {}
    for grp in by_shape(BIG):
        cast.update(zip(grp, _cast_shards("cast_" + grp[0], place, [big[n] for n in grp],
                                          paired=grp[0] == "w_gate_up")))
    tril = jnp.tril(jnp.ones((GCH, GCH), bool))
    wm = jnp.where(tril, gmlp_w_s[0], 0.0).astype(BF16)
    wm_t = jnp.swapaxes(wm, 1, 2)
    b_t = gmlp_b_s[0].T

    (proj, hb), w_in4, (w_a4, w_b4, w_out4, w_down4) = _proj_fwd(
        place, xs, norm_mix_g, cast["w_in"], [cast[n] for n in ("w_branch_a", "w_branch_b", "w_out", "w_down")])
    (ab,), _ = _gmlp_fwd(proj, gmlp_ln_g, gmlp_ln_b, wm, b_t)
    (o_raw, obb, st_before), (w_gu,) = _hgrn_fwd(
        proj, hgrn_lb_table, hgrn_norm_g, job=_gather_job([cast["w_gate_up"]]))
    w_a, w_b, w_o = (w.reshape(D, D) for w in (w_a4, w_b4, w_out4))
    (mgb, x1), _ = _merge_fwd(xs, ab, obb, proj, w_a, w_b, w_o)
    w_dn = w_down4.reshape(FF, D)
    act, dx2b, h2b, dgu, dx1, dx1b, acc_ffn = _ffn_fwd_bwd(
        x1, target, norm_ffn_g, norm_final_g.reshape(1, D), w_gu, w_dn)

    grads, owns, parts, halves, sibh = {}, {}, {}, {}, {}

    def pair_sums(names, sibs):
        sib_of = dict(zip(names, sibs))
        for grp in by_shape(names):
            o, p = _pair_sums("rs_pair_sum_" + grp[0], place, [grads[n] for n in grp], [sib_of[n] for n in grp])
            owns.update(zip(grp, o))
            parts.update(zip(grp, p))

    def chip_sums(names, got):
        rem_of = dict(zip(names, got))
        for grp in by_shape(names):
            h = _chip_sums("rs_chip_sum_" + grp[0], [owns[n] for n in grp], [rem_of[n] for n in grp])
            halves.update(zip(grp, h))

    ffn, mix = ("w_gate_up", "w_down"), ("w_branch_a", "w_branch_b", "w_out")
    grads["w_gate_up"], _ = _dw_gate_up(h2b, dgu)
    grads["w_down"], _ = _dw_down(act, dx2b)
    (dya, dyb, dproj), got = _merge_bwd(
        dx1b, ab, obb, proj, w_o, w_a, w_b, job=_pair_exchange_job([grads[n] for n in ffn]))
    pair_sums(ffn, got)
    grads["w_branch_a"], _ = _dw_square("dw_branch_a", ab, dya)
    grads["w_branch_b"], _ = _dw_square("dw_branch_b", obb, dyb)
    grads["w_out"], _ = _dw_square("dw_out", mgb, dx1b)
    (dproj, acc_hgrn), got = _hgrn_bwd(
        dproj, dyb, w_b, o_raw, proj, st_before, hgrn_lb_table, hgrn_norm_g,
        job=_join_jobs(_chip_exchange_job([parts[n] for n in ffn]), _pair_exchange_job([grads[n] for n in mix])))
    chip_sums(ffn, got[:2])
    pair_sums(mix, got[2:])
    dproj, acc_ln, dws, dmix = _gmlp_bwd(dproj, dya, w_a, proj, gmlp_ln_g, gmlp_ln_b, wm, wm_t, b_t)
    for_sibling, got = _dw_in_half(
        "dw_in_sibling_half", place, hb, dproj, False,
        job=_join_jobs(_share_halves_job([halves[n] for n in ffn]), _chip_exchange_job([parts[n] for n in mix])))
    sibh.update(zip(ffn, got[:2]))
    chip_sums(mix, got[2:])
    grads["w_in"], got = _dw_in_half(
        "dw_in_own_half", place, hb, dproj, True, job=_share_halves_job([for_sibling]))
    pair_sums(("w_in",), got)
    (grad_x, acc_mix), got = _proj_bwd(
        dproj, w_in4, xs, dx1, norm_mix_g,
        job=_join_jobs(_chip_exchange_job([parts["w_in"]]), _share_halves_job([halves[n] for n in mix])))
    chip_sums(("w_in",), got[:1])
    sibh.update(zip(mix, got[1:]))

    lbv = jax.nn.sigmoid(hgrn_lb_table[0] - hgrn_lb_table[1])
    d_t0 = jnp.sum(acc_hgrn[0], axis=0) * lbv * (1.0 - lbv)
    loss_row = jnp.zeros((D,), F32).at[0].set(jnp.sum(acc_ffn[0]))
    dws_m = jnp.where(tril[:, None, :], dws.reshape(GCH, NG, GCH), 0.0).transpose(1, 0, 2)
    db_s = jnp.sum(dmix.reshape(GCH, NG, GCH), axis=-1).T
    sp = _pack_small(loss_row, jnp.sum(acc_mix, 0), jnp.sum(acc_ln[0], 0), jnp.sum(acc_ln[1], 0), db_s,
                     jnp.stack([d_t0, -d_t0]), jnp.sum(acc_hgrn[1], 0), jnp.sum(acc_ffn[2], 0),
                     jnp.sum(acc_ffn[1], 0), dws_m)
    zero = jnp.zeros((D,), F32)

    def pack(prefix):
        a = lambda n: args[prefix + n]
        return _pack_small(zero, a("norm_mix_g"), a("gmlp_ln_g"), a("gmlp_ln_b"), a("gmlp_b_s"),
                           a("hgrn_lb_table"), a("hgrn_norm_g"), a("norm_ffn_g"), a("norm_final_g"),
                           a("gmlp_w_s"))

    packed, (sibh["w_in"],) = _small_allreduce_adamw(
        sp, pack(""), pack("m_"), pack("v_"), _share_halves_job([halves["w_in"]]))
    loss = packed[0][0, 0]
    small = [_unpack_small(p) for p in packed]
    out = {n: tuple(s[n] for s in small) for n in SMALL}
    for grp in by_shape(BIG):
        res = _adamws("adamw_" + grp[0], place, *[[d[n] for n in grp] for d in (big, halves, sibh, big_m, big_v)])
        for n, quad in zip(grp, res):
            out[n] = tuple(a.reshape(args[n].shape) for a in quad)
    return (loss, grad_x.reshape(x.shape), *[out[n][0] for n in ORDER], *[out[n][1] for n in ORDER],
            *[out[n][2] for n in ORDER], *[out[n][3] for n in ORDER])
```

```python
import functools
import math

import jax
import jax.numpy as jnp
from jax import lax
from jax.experimental import pallas as pl
from jax.experimental.pallas import tpu as pltpu

F32 = jnp.float32
BF16 = jnp.bfloat16
SDS = jax.ShapeDtypeStruct
MESH = pl.DeviceIdType.MESH
ANY = pl.BlockSpec(memory_space=pl.ANY)

D = 1024
NIN = 8
NG = 8
GCH = 128
NH = 8
HD = 128
HCH = 64
HGRN_HB = 8
HGRN_TOKENS = 256
GMLP_FWD_TOKENS = 512
GMLP_BWD_TOKENS = 256
HW = HGRN_HB * HD
DW_TOKENS = 2048
ELEMENTWISE_BLOCK_BYTES = 2 * 1024 * 1024
PROJ_OUT_SLOTS = 4
FF = 2816
FFS = 1408
NCHIP = 4
EPS = 1e-6
QSCALE = HD ** -0.5
GELU_C0 = math.sqrt(2.0 / math.pi)
GELU_C1 = 0.044715
LR, B1, B2, AEPS, WD, STEP = 0.001, 0.9, 0.999, 1e-08, 0.01, 10
VMEM_LIMIT_V7X = 56 * 1024 * 1024
SP_ROWS = 144


def _cparams(**kw):
    return pltpu.CompilerParams(vmem_limit_bytes=VMEM_LIMIT_V7X, **kw)


def _mm(a, b):
    return jnp.dot(a, b, preferred_element_type=F32)


def _mm_nt(a, b):
    return lax.dot_general(a, b, (((1,), (1,)), ((), ())), preferred_element_type=F32)


def _mm_tn(a, b):
    return lax.dot_general(a, b, (((0,), (0,)), ((), ())), preferred_element_type=F32)


def _rows8(x):
    r, c = x.shape
    return jnp.sum(x.reshape(r // 8, 8, c), axis=0)


def _mean(x):
    return jnp.mean(x, axis=-1, keepdims=True)


def _sigmoid(x):
    return 1.0 / (1.0 + jnp.exp(-x))


def _gelu(x):
    t = jnp.tanh(GELU_C0 * (x + GELU_C1 * x * x * x))
    return 0.5 * x * (1.0 + t), t


def _gelu_grad(x, t):
    return 0.5 * (1.0 + t) + 0.5 * x * (1.0 - t * t) * (GELU_C0 * (1.0 + 3.0 * GELU_C1 * x * x))


def _component_of(group):
    return jnp.where(group < 6, (group + 4) % 6, group)


def _proj_fwd(place, x, g_mix, w_in4, later):
    T = x.shape[0]
    tm = min(1024, T)
    ni = T // tm
    n = len(later)

    def body(pc_ref, x_ref, g_ref, *rest):
        proj_ref, h_ref, w_all = rest[1 + n:4 + n]
        gathered = rest[4 + n:4 + 2 * n]
        hs, wbuf, wsem, obuf, osem = rest[4 + 2 * n:9 + 2 * n]
        w_sems, later_sems = rest[9 + 2 * n:15 + 2 * n], rest[15 + 2 * n:]
        jp, i = pl.program_id(0), pl.program_id(1)
        w_cols = [w_all.at[:, :, pl.ds(k * D, D)] for k in range(2)]

        def w_copy(blk):
            cols = pl.ds(pl.multiple_of((blk % 2) * D, 128), D)
            return pltpu.make_async_copy(w_all.at[pc_ref[0] ^ (blk // 2), :, cols], wbuf.at[blk % 2],
                                         wsem.at[blk % 2])

        @pl.when((jp == 0) & (i == 0))
        def _():
            _gather_start(w_cols, w_sems)
            w_copy(jp).start()

        @pl.when(i == 0)
        def _():
            w_copy(jp).wait()

        @pl.when(jp == 0)
        def _():
            xv = x_ref[...]
            r = lax.rsqrt(_mean(xv * xv) + EPS)
            hb = (xv * r * g_ref[...]).astype(BF16)
            hs[i] = hb
            h_ref[...] = hb

        step = jp * ni + i
        slot = step % PROJ_OUT_SLOTS

        def o_copy(slot_):
            comp = 2 * (pc_ref[0] ^ (jp // 2)) + jp % 2
            return pltpu.make_async_copy(
                obuf.at[slot_], proj_ref.at[comp, pl.ds(pl.multiple_of(i * tm, 8), tm)], osem.at[slot_])

        @pl.when(step >= PROJ_OUT_SLOTS)
        def _():
            o_copy(slot).wait()

        obuf[slot] = _mm(hs[i], wbuf[jp % 2])
        o_copy(slot).start()

        @pl.when(step == NIN * ni - 1)
        def _():
            for k in range(PROJ_OUT_SLOTS):
                o_copy((slot + 1 + k) % PROJ_OUT_SLOTS).wait()

        for nxt in range(1, NIN):
            @pl.when((jp == nxt - 1) & (i == ni - 1))
            def _():
                if nxt >= 2:
                    _gather_land([w_cols[nxt % 2]], w_sems, nxt // 2, first=nxt % 2)
                if nxt == 5:
                    _gather_start(gathered, later_sems)
                if nxt == NIN - 1:
                    _gather_neighbours(gathered, later_sems)
                w_copy(jp + 1).start()

        @pl.when((jp == NIN - 1) & (i == ni - 1))
        def _():
            _gather_drain(w_cols, w_sems)
            _gather_finish(gathered, later_sems)

    tile = lambda jp, i, pc: (jnp.where(jp == 0, i, ni - 1), 0)
    res = pl.pallas_call(
        body, name="proj_fwd",
        grid_spec=pltpu.PrefetchScalarGridSpec(
            num_scalar_prefetch=1, grid=(NIN, ni),
            in_specs=[pl.BlockSpec((tm, D), tile), pl.BlockSpec((1, D), lambda jp, i, pc: (0, 0))] + [ANY] * (1 + n),
            out_specs=[ANY, pl.BlockSpec((tm, D), tile)] + [ANY] * (1 + n),
            scratch_shapes=[pltpu.VMEM((ni, tm, D), BF16), pltpu.VMEM((2, D, D), BF16),
                            pltpu.SemaphoreType.DMA((2,)), pltpu.VMEM((PROJ_OUT_SLOTS, tm, D), F32),
                            pltpu.SemaphoreType.DMA((PROJ_OUT_SLOTS,))] + _gather_sems(2) + _gather_sems(n)),
        out_shape=[SDS((NIN, T, D), F32), SDS((T, D), BF16), SDS(w_in4.shape, BF16)]
        + [SDS(a.shape, a.dtype) for a in later],
        input_output_aliases={3 + k: 2 + k for k in range(1 + n)},
        compiler_params=_cparams(has_side_effects=True),
    )(place, x, g_mix, w_in4, *later)
    return res[:2], res[2], res[3:]


def _chunks_abreast(x):
    return jnp.concatenate([x[GCH * ch:GCH * (ch + 1)] for ch in range(x.shape[0] // GCH)], axis=1)


def _chunks_stacked(x):
    return jnp.concatenate([x[:, GCH * ch:GCH * (ch + 1)] for ch in range(x.shape[1] // GCH)], axis=0)


def _layer_norm_stats(gv):
    mu = _mean(gv)
    xc = gv - mu
    rs = lax.rsqrt(_mean(xc * xc) + EPS)
    return xc * rs, rs


def _gmlp_fwd(proj, ln_g, ln_b, wm, b_t, job=None):
    T = proj.shape[1]
    tm = min(GMLP_FWD_TOKENS, T)

    def body(u_ref, v_ref, lg_ref, lb_ref, wm_ref, bt_ref, a_ref, a_s):
        gu, _ = _gelu(u_ref[...])
        gv, _ = _gelu(v_ref[...])
        vhat, _ = _layer_norm_stats(gv)
        vnb = (vhat * lg_ref[...] + lb_ref[...]).astype(BF16)
        for g in range(NG):
            cols = slice(128 * g, 128 * (g + 1))
            mixed = _mm(wm_ref[g], _chunks_abreast(vnb[:, cols])) + bt_ref[:, g:g + 1]
            a_s[:, cols] = gu[:, cols] * _chunks_stacked(mixed)
        a_ref[...] = a_s[...].astype(BF16)

    row = lambda i: (0, 0)
    return _call(
        body, name="gmlp_fwd", grid=(T // tm,), job=job, args=(proj, proj, ln_g, ln_b, wm, b_t),
        in_specs=[pl.BlockSpec((None, tm, D), lambda i: (0, i, 0)), pl.BlockSpec((None, tm, D), lambda i: (1, i, 0)),
                  pl.BlockSpec((1, D), row), pl.BlockSpec((1, D), row),
                  pl.BlockSpec((NG, GCH, GCH), lambda i: (0, 0, 0)), pl.BlockSpec((GCH, NG), row)],
        out_specs=[pl.BlockSpec((tm, D), lambda i: (i, 0))],
        out_shape=[SDS((T, D), BF16)],
        scratch_shapes=[pltpu.VMEM((tm, D), F32)])


def _cumsum64(x, row):
    for s in (1, 2, 4, 8, 16, 32):
        x = x + jnp.where(row >= s, pltpu.roll(x, s, 0), 0.0)
    return x


def _revcumsum64(x, row):
    n = x.shape[0]
    for s in (1, 2, 4, 8, 16, 32):
        x = x + jnp.where(row < HCH - s, pltpu.roll(x, n - s, 0), 0.0)
    return x


def _head_mean(x):
    parts = [jnp.broadcast_to(_mean(x[:, HD * h:HD * (h + 1)]), (x.shape[0], HD)) for h in range(x.shape[1] // HD)]
    return jnp.concatenate(parts, axis=1)


def _seg_sum(x):
    n, c = x.shape
    s = jnp.sum(x.reshape(n // HCH, HCH, c), axis=1, keepdims=True)
    return jnp.broadcast_to(s, (n // HCH, HCH, c)).reshape(n, c)


def _seg_row(x, idx):
    n, c = x.shape
    x3 = x.reshape(n // HCH, HCH, c)
    return jnp.broadcast_to(x3[:, idx:idx + 1, :], x3.shape).reshape(n, c)


def _hgrn_gates(fl, lbv, row):
    s = _sigmoid(fl)
    f = lbv + (1.0 - lbv) * s
    a = _cumsum64(jnp.log(f), row)
    return s, f, a, _seg_row(a, HCH // 2 - 1), _seg_row(a, HCH - 1)


def _hgrn_fwd(proj, lb_table, norm_g, job=None):
    T = proj.shape[1]
    tb = min(HGRN_TOKENS, T)
    nc = tb // HCH

    def body(q_ref, fl_ref, v_ref, g_ref, lbt_ref, gn_ref, o_ref, ob_ref, stb_ref, st_s, o_s):
        @pl.when(pl.program_id(1) == 0)
        def _():
            st_s[...] = jnp.zeros_like(st_s)

        row = lax.broadcasted_iota(jnp.int32, (tb, HW), 0) & (HCH - 1)
        lbv = _sigmoid(lbt_ref[0:1, :] - lbt_ref[1:2, :])
        _, f, a, a_mid, a_last = _hgrn_gates(fl_ref[...], lbv, row)
        k = 1.0 - f
        qs = q_ref[...] * QSCALE
        q_in = (qs * jnp.exp(a - a_mid)).astype(BF16)
        k_in = (k * jnp.exp(a_mid - a)).astype(BF16)
        q_a = (qs * jnp.exp(a)).astype(BF16)
        k_d = (k * jnp.exp(a_last - a)).astype(BF16)
        dec = jnp.exp(a_last)
        vb = v_ref[...].astype(BF16)
        tri = (lax.broadcasted_iota(jnp.int32, (HCH, HCH), 0)
               >= lax.broadcasted_iota(jnp.int32, (HCH, HCH), 1))
        for c in range(nc):
            sl = slice(HCH * c, HCH * (c + 1))
            for hh in range(HGRN_HB):
                hs = slice(HD * hh, HD * (hh + 1))
                st = st_s[hh]
                stb_ref[hh, c] = st
                sc = jnp.where(tri, _mm_nt(q_in[sl, hs], k_in[sl, hs]), 0.0)
                o_s[sl, hs] = _mm(sc.astype(BF16), vb[sl, hs]) + _mm_nt(q_a[sl, hs], st.astype(BF16))
                d64 = dec[sl, hs]
                st_s[hh] = st * jnp.concatenate([d64, d64], axis=0) + _mm_tn(vb[sl, hs], k_d[sl, hs])
        o = o_s[...]
        r = lax.rsqrt(_head_mean(o * o) + EPS)
        g = g_ref[...]
        o_ref[...] = o
        ob_ref[...] = (o * r * gn_ref[...] * (g * _sigmoid(g))).astype(BF16)

    def col(off):
        return pl.BlockSpec((None, tb, HW), lambda h, cb: (off, cb, h))

    return _call(
        body, name="hgrn_fwd", grid=(NH // HGRN_HB, T // tb), job=job,
        args=(proj, proj, proj, proj, lb_table, norm_g),
        in_specs=[col(2), col(3), col(4), col(5),
                  pl.BlockSpec((2, HW), lambda h, cb: (0, h)), pl.BlockSpec((1, HW), lambda h, cb: (0, h))],
        out_specs=[pl.BlockSpec((tb, HW), lambda h, cb: (cb, h)), pl.BlockSpec((tb, HW), lambda h, cb: (cb, h)),
                   pl.BlockSpec((HGRN_HB, nc, HD, HD), lambda h, cb: (h, cb, 0, 0))],
        out_shape=[SDS((T, D), F32), SDS((T, D), BF16), SDS((NH, T // HCH, HD, HD), F32)],
        scratch_shapes=[pltpu.VMEM((HGRN_HB, HD, HD), F32), pltpu.VMEM((tb, HW), F32)])


def _merge_fwd(x, ab, ob, proj, w_a, w_b, w_out, job=None):
    T = x.shape[0]
    tm = min(512, T)

    def body(x_ref, ab_ref, ob_ref, ga_ref, gb_ref, wa_ref, wb_ref, wo_ref, mg_ref, x1_ref):
        ya = _mm(ab_ref[...], wa_ref[...])
        yb = _mm(ob_ref[...], wb_ref[...])
        merged = (_sigmoid(ga_ref[...]) * ya + _sigmoid(gb_ref[...]) * yb).astype(BF16)
        mg_ref[...] = merged
        x1_ref[...] = x_ref[...] + _mm(merged, wo_ref[...])

    t = lambda i: (i, 0)
    w = lambda i: (0, 0)
    return _call(
        body, name="merge_fwd", grid=(T // tm,), job=job, args=(x, ab, ob, proj, proj, w_a, w_b, w_out),
        in_specs=[pl.BlockSpec((tm, D), t), pl.BlockSpec((tm, D), t), pl.BlockSpec((tm, D), t),
                  pl.BlockSpec((None, tm, D), lambda i: (6, i, 0)), pl.BlockSpec((None, tm, D), lambda i: (7, i, 0)),
                  pl.BlockSpec((D, D), w), pl.BlockSpec((D, D), w), pl.BlockSpec((D, D), w)],
        out_specs=[pl.BlockSpec((tm, D), t)] * 2,
        out_shape=[SDS((T, D), BF16), SDS((T, D), F32)])


def _ffn_fwd_bwd(x1, target, g_ffn, g_fin, w_gu, w_down):
    T = x1.shape[0]
    tm = min(256, T)
    inv_d = 1.0 / D

    def body(x1_ref, tg_ref, gf_ref, gn_ref, wgu_ref, wd_ref,
             act_ref, dx2b_ref, h2b_ref, dgu_ref, dx1_ref, dx1b_ref, acc_ref):
        @pl.when(pl.program_id(0) == 0)
        def _():
            acc_ref[...] = jnp.zeros_like(acc_ref)

        x1v = x1_ref[...]
        gf = gf_ref[...]
        gn = gn_ref[...]
        rr1 = lax.rsqrt(_mean(x1v * x1v) + EPS)
        x1n = x1v * rr1
        h2b = (x1n * gf).astype(BF16)
        h2b_ref[...] = h2b
        gate = _mm(h2b, wgu_ref[0])
        up = _mm(h2b, wgu_ref[1])
        sg = _sigmoid(gate)
        si = gate * sg
        act = (si * up).astype(BF16)
        act_ref[...] = act
        x2 = x1v + _mm(act, wd_ref[...])
        rr2 = lax.rsqrt(_mean(x2 * x2) + EPS)
        x2n = x2 * rr2
        e = x2n * gn - tg_ref[...]
        acc_ref[0] += _rows8(e * e) * (0.5 * inv_d)
        dy = e * inv_d
        acc_ref[1] += _rows8(dy * x2n)
        dxn = dy * gn
        dx2 = rr2 * (dxn - x2n * _mean(dxn * x2n))
        dx2b = dx2.astype(BF16)
        dx2b_ref[...] = dx2b
        dact = _mm_nt(dx2b, wd_ref[...])
        dgate = (dact * up * (sg * (1.0 + gate * (1.0 - sg)))).astype(BF16)
        dup = (dact * si).astype(BF16)
        dgu_ref[0] = dgate
        dgu_ref[1] = dup
        dh2 = _mm_nt(dgate, wgu_ref[0]) + _mm_nt(dup, wgu_ref[1])
        acc_ref[2] += _rows8(dh2 * x1n)
        dxn1 = dh2 * gf
        dx1 = dx2 + rr1 * (dxn1 - x1n * _mean(dxn1 * x1n))
        dx1_ref[...] = dx1
        dx1b_ref[...] = dx1.astype(BF16)

    t = lambda i: (i, 0)
    w = lambda i: (0, 0)
    one = pl.Buffered(1)
    return pl.pallas_call(
        body, name="ffn_fwd_bwd", grid=(T // tm,),
        in_specs=[pl.BlockSpec((tm, D), t), pl.BlockSpec((tm, D), t),
                  pl.BlockSpec((1, D), w), pl.BlockSpec((1, D), w),
                  pl.BlockSpec((2, D, FF), lambda i: (0, 0, 0), pipeline_mode=one),
                  pl.BlockSpec((FF, D), w, pipeline_mode=one)],
        out_specs=[pl.BlockSpec((tm, FF), t), pl.BlockSpec((tm, D), t), pl.BlockSpec((tm, D), t),
                   pl.BlockSpec((2, tm, FF), lambda i: (0, i, 0)),
                   pl.BlockSpec((tm, D), t), pl.BlockSpec((tm, D), t),
                   pl.BlockSpec((3, 8, D), lambda i: (0, 0, 0))],
        out_shape=[SDS((T, FF), BF16), SDS((T, D), BF16), SDS((T, D), BF16),
                   SDS((2, T, FF), BF16), SDS((T, D), F32), SDS((T, D), BF16),
                   SDS((3, 8, D), F32)],
        compiler_params=_cparams(),
    )(x1, target, g_ffn, g_fin, w_gu, w_down)


def _merge_bwd(dx1b, ab, ob, proj, w_out, w_a, w_b, job=None):
    T = dx1b.shape[0]
    tm = min(512, T)

    def body(dx_ref, ab_ref, ob_ref, ga_ref, gb_ref, wo_ref, wa_ref, wb_ref, dya_ref, dyb_ref, dp_ref):
        dm = _mm_nt(dx_ref[...], wo_ref[...])
        sa = _sigmoid(ga_ref[...])
        sb = _sigmoid(gb_ref[...])
        dya_ref[...] = (dm * sa).astype(BF16)
        dyb_ref[...] = (dm * sb).astype(BF16)
        dp_ref[0] = (dm * _mm(ab_ref[...], wa_ref[...]) * sa * (1.0 - sa)).astype(BF16)
        dp_ref[1] = (dm * _mm(ob_ref[...], wb_ref[...]) * sb * (1.0 - sb)).astype(BF16)

    t = lambda i: (i, 0)
    w = lambda i: (0, 0)
    return _call(
        body, name="merge_bwd", grid=(T // tm,),
        in_specs=[pl.BlockSpec((tm, D), t), pl.BlockSpec((tm, D), t), pl.BlockSpec((tm, D), t),
                  pl.BlockSpec((None, tm, D), lambda i: (6, i, 0)), pl.BlockSpec((None, tm, D), lambda i: (7, i, 0)),
                  pl.BlockSpec((D, D), w), pl.BlockSpec((D, D), w), pl.BlockSpec((D, D), w)],
        out_specs=[pl.BlockSpec((tm, D), t)] * 2 + [pl.BlockSpec((2, tm, D), lambda i: (3, i, 0))],
        out_shape=[SDS((T, D), BF16), SDS((T, D), BF16), SDS((NIN, T, D), BF16)],
        args=(dx1b, ab, ob, proj, proj, w_out, w_a, w_b), job=job)


def _hgrn_bwd(dproj, dyb, w_b, o_raw, proj, st_before, lb_table, norm_g, job=None):
    T = dyb.shape[0]
    tb = min(HGRN_TOKENS, T)
    nc = tb // HCH
    nb = T // tb

    def body(dp_in, dyb_ref, wb_ref, o_ref, q_ref, fl_ref, v_ref, g_ref, stb_ref, lbt_ref, gn_ref,
             dp_ref, acc_ref, dst_s, dqin_s, dqa_s, dkin_s, dkd_s, dv_s, ddec_s):
        del dp_in

        @pl.when(pl.program_id(1) == 0)
        def _():
            dst_s[...] = jnp.zeros_like(dst_s)
            acc_ref[...] = jnp.zeros_like(acc_ref)

        row = lax.broadcasted_iota(jnp.int32, (tb, HW), 0) & (HCH - 1)
        gn = gn_ref[...]
        lbv = _sigmoid(lbt_ref[0:1, :] - lbt_ref[1:2, :])
        o = o_ref[...]
        r = lax.rsqrt(_head_mean(o * o) + EPS)
        on = o * r
        g = g_ref[...]
        sgm = _sigmoid(g)
        dob_v = _mm_nt(dyb_ref[...], wb_ref[...])
        dp_ref[3] = (dob_v * on * gn * (sgm * (1.0 + g * (1.0 - sgm)))).astype(BF16)
        do_n = dob_v * (g * sgm)
        acc_ref[1] += _rows8(do_n * on)
        dxn = do_n * gn
        do = (r * (dxn - on * _head_mean(dxn * on))).astype(BF16)
        s, f, a, a_mid, a_last = _hgrn_gates(fl_ref[...], lbv, row)
        k = 1.0 - f
        qs = q_ref[...] * QSCALE
        e_q = jnp.exp(a - a_mid)
        e_k = jnp.exp(a_mid - a)
        e_a = jnp.exp(a)
        e_l = jnp.exp(a_last - a)
        dec = jnp.exp(a_last)
        q_in = qs * e_q
        k_in = k * e_k
        q_a = qs * e_a
        k_d = k * e_l
        q_inb, k_inb, q_ab, k_db = (z.astype(BF16) for z in (q_in, k_in, q_a, k_d))
        vb = v_ref[...].astype(BF16)
        tri = (lax.broadcasted_iota(jnp.int32, (HCH, HCH), 0)
               >= lax.broadcasted_iota(jnp.int32, (HCH, HCH), 1))
        for c in reversed(range(nc)):
            sl = slice(HCH * c, HCH * (c + 1))
            for hh in range(HGRN_HB):
                hs = slice(HD * hh, HD * (hh + 1))
                stp = stb_ref[hh, c]
                dst = dst_s[hh]
                dstb = dst.astype(BF16)
                do_c = do[sl, hs]
                v_c = vb[sl, hs]
                dqa_s[sl, hs] = _mm(do_c, stp.astype(BF16))
                dkd_s[sl, hs] = _mm(v_c, dstb)
                ddec_s[sl, hs] = jnp.broadcast_to(jnp.sum(dst * stp, axis=0, keepdims=True), (HCH, HD))
                sc = jnp.where(tri, _mm_nt(q_inb[sl, hs], k_inb[sl, hs]), 0.0).astype(BF16)
                dsc = jnp.where(tri, _mm_nt(do_c, v_c), 0.0).astype(BF16)
                dv_s[sl, hs] = _mm_nt(k_db[sl, hs], dstb) + _mm_tn(sc, do_c)
                dqin_s[sl, hs] = _mm(dsc, k_inb[sl, hs])
                dkin_s[sl, hs] = _mm_tn(dsc, q_inb[sl, hs])
                d64 = dec[sl, hs]
                dst_s[hh] = dst * jnp.concatenate([d64, d64], axis=0) + _mm_tn(do_c, q_ab[sl, hs])
        dq_in = dqin_s[...]
        dq_a = dqa_s[...]
        dk_in = dkin_s[...]
        dk_d = dkd_s[...]
        dp_ref[0] = ((dq_in * e_q + dq_a * e_a) * QSCALE).astype(BF16)
        dp_ref[2] = dv_s[...].astype(BF16)
        tq = dq_in * q_in
        tk = dk_in * k_in
        td = dk_d * k_d
        d_a = tq + dq_a * q_a - tk - td
        d_a = d_a + jnp.where(row == HCH // 2 - 1, _seg_sum(tk - tq), 0.0)
        d_a = d_a + jnp.where(row == HCH - 1, _seg_sum(td) + ddec_s[...] * dec, 0.0)
        dlf = _revcumsum64(d_a, row)
        df = dlf / f - (dk_in * e_k + dk_d * e_l)
        dp_ref[1] = (df * (1.0 - lbv) * s * (1.0 - s)).astype(BF16)
        acc_ref[0] += _rows8(df * (1.0 - s))

    def col(off):
        return pl.BlockSpec((None, tb, HW), lambda h, cb: (off, nb - 1 - cb, h))

    hb = lambda h, cb: (nb - 1 - cb, h)
    return _call(
        body, name="hgrn_bwd", grid=(NH // HGRN_HB, nb), job=job,
        args=(dproj, dyb, w_b, o_raw, proj, proj, proj, proj, st_before, lb_table, norm_g),
        in_specs=[ANY, pl.BlockSpec((tb, D), lambda h, cb: (nb - 1 - cb, 0)),
                  pl.BlockSpec((HW, D), lambda h, cb: (h, 0)), pl.BlockSpec((tb, HW), hb),
                  col(2), col(3), col(4), col(5),
                  pl.BlockSpec((HGRN_HB, nc, HD, HD), lambda h, cb: (h, nb - 1 - cb, 0, 0)),
                  pl.BlockSpec((2, HW), lambda h, cb: (0, h)), pl.BlockSpec((1, HW), lambda h, cb: (0, h))],
        out_specs=[pl.BlockSpec((4, tb, HW), lambda h, cb: (0, nb - 1 - cb, h)),
                   pl.BlockSpec((2, 8, HW), lambda h, cb: (0, 0, h))],
        out_shape=[SDS(dproj.shape, BF16), SDS((2, 8, D), F32)],
        scratch_shapes=[pltpu.VMEM((HGRN_HB, HD, HD), F32)] + [pltpu.VMEM((tb, HW), F32)] * 6,
        aliases={0: 0})


def _gmlp_bwd(dproj, dya, w_a, proj, ln_g, ln_b, wm, wm_t, b_t):
    T = dya.shape[0]
    tm = min(GMLP_BWD_TOKENS, T)

    def body(dp_in, dya_ref, wa_ref, u_ref, v_ref, lg_ref, lb_ref, wm_ref, wmt_ref, bt_ref,
             dp_ref, acc_ref, dws_ref, dmix_ref, du_s, dvn_s):
        del dp_in

        @pl.when(pl.program_id(0) == 0)
        def _():
            acc_ref[...] = jnp.zeros_like(acc_ref)
            dws_ref[...] = jnp.zeros_like(dws_ref)
            dmix_ref[...] = jnp.zeros_like(dmix_ref)

        u = u_ref[...]
        v = v_ref[...]
        lg = lg_ref[...]
        gu, t_u = _gelu(u)
        gv, t_v = _gelu(v)
        vhat, rs = _layer_norm_stats(gv)
        vnb = (vhat * lg + lb_ref[...]).astype(BF16)
        da_v = _mm_nt(dya_ref[...], wa_ref[...])
        for g in range(NG):
            cols = slice(128 * g, 128 * (g + 1))
            vng = _chunks_abreast(vnb[:, cols])
            mixed = _mm(wm_ref[g], vng) + bt_ref[:, g:g + 1]
            dag = _chunks_abreast(da_v[:, cols])
            dmx = dag * _chunks_abreast(gu[:, cols])
            du_s[:, cols] = _chunks_stacked(dag * mixed)
            dmxb = dmx.astype(BF16)
            dws_ref[:, cols] += _mm_nt(dmxb, vng)
            dmix_ref[:, cols] += sum(dmx[:, GCH * ch:GCH * (ch + 1)] for ch in range(tm // GCH))
            dvn_s[:, cols] = _chunks_stacked(_mm(wmt_ref[g], dmxb))
        dp_ref[0] = (du_s[...] * _gelu_grad(u, t_u)).astype(BF16)
        dvn = dvn_s[...]
        acc_ref[0] += _rows8(dvn * vhat)
        acc_ref[1] += _rows8(dvn)
        dvh = dvn * lg
        dgv = rs * (dvh - _mean(dvh) - vhat * _mean(dvh * vhat))
        dp_ref[1] = (dgv * _gelu_grad(v, t_v)).astype(BF16)

    row = lambda i: (0, 0)
    w3 = lambda i: (0, 0, 0)
    return pl.pallas_call(
        body, name="gmlp_bwd", grid=(T // tm,),
        in_specs=[ANY, pl.BlockSpec((tm, D), lambda i: (i, 0)), pl.BlockSpec((D, D), row),
                  pl.BlockSpec((None, tm, D), lambda i: (0, i, 0)), pl.BlockSpec((None, tm, D), lambda i: (1, i, 0)),
                  pl.BlockSpec((1, D), row), pl.BlockSpec((1, D), row),
                  pl.BlockSpec((NG, GCH, GCH), w3), pl.BlockSpec((NG, GCH, GCH), w3),
                  pl.BlockSpec((GCH, NG), row)],
        out_specs=[pl.BlockSpec((2, tm, D), lambda i: (2, i, 0)),
                   pl.BlockSpec((2, 8, D), w3), pl.BlockSpec((GCH, D), row), pl.BlockSpec((GCH, D), row)],
        out_shape=[SDS(dproj.shape, BF16), SDS((2, 8, D), F32), SDS((GCH, D), F32), SDS((GCH, D), F32)],
        scratch_shapes=[pltpu.VMEM((tm, D), F32), pltpu.VMEM((tm, D), F32)],
        input_output_aliases={0: 0},
        compiler_params=_cparams(),
    )(dproj, dya, w_a, proj, proj, ln_g, ln_b, wm, wm_t, b_t)


def _proj_bwd(dproj, w_in4, x, dx1, g_mix, job=None):
    T = x.shape[0]
    tm = min(256, T)
    order = (2, 3, 4, 5, 0, 1, 6, 7)

    def body(dp_ref, w_ref, x_ref, dx1_ref, g_ref, gx_ref, acc_ref):
        @pl.when(pl.program_id(0) == 0)
        def _():
            acc_ref[...] = jnp.zeros_like(acc_ref)

        dh = None
        for m, og in enumerate(order):
            part = _mm_nt(dp_ref[m], w_ref[og // 2, :, D * (og % 2):D * (og % 2 + 1)])
            dh = part if dh is None else dh + part
        xv = x_ref[...]
        r = lax.rsqrt(_mean(xv * xv) + EPS)
        xn = xv * r
        acc_ref[...] += _rows8(dh * xn)
        dxn = dh * g_ref[...]
        gx_ref[...] = dx1_ref[...] + r * (dxn - xn * _mean(dxn * xn))

    t = lambda i: (i, 0)
    return _call(
        body, name="proj_bwd", grid=(T // tm,),
        in_specs=[pl.BlockSpec((NIN, tm, D), lambda i: (0, i, 0)),
                  pl.BlockSpec((NCHIP, D, 2 * D), lambda i: (0, 0, 0), pipeline_mode=pl.Buffered(1)),
                  pl.BlockSpec((tm, D), t), pl.BlockSpec((tm, D), t), pl.BlockSpec((1, D), lambda i: (0, 0))],
        out_specs=[pl.BlockSpec((tm, D), t), pl.BlockSpec((8, D), lambda i: (0, 0))],
        out_shape=[SDS((T, D), F32), SDS((8, D), F32)],
        args=(dproj, w_in4, x, dx1, g_mix), job=job)


def _dw_call(name, a, b, a_spec, b_spec, o_spec, out_shape, nblk, tt, job=None, prefetch=None):
    T = a.shape[-2]

    def body(*refs):
        a_ref, b_ref, o_ref = refs[-3:]

        @pl.when(pl.program_id(1) == 0)
        def _():
            o_ref[...] = jnp.zeros_like(o_ref)
        o_ref[...] += _mm_tn(a_ref[...], b_ref[...])

    (out,), job_out = _call(
        body, name=name, grid=(nblk, T // tt), in_specs=[a_spec, b_spec], out_specs=[o_spec],
        out_shape=[out_shape], args=(a, b), job=job, prefetch=prefetch)
    return out, job_out


def _dw_in_half(name, place, hb, dproj, mine, job=None):
    tt = min(DW_TOKENS, hb.shape[0])

    def comp(k, pc):
        return _component_of(2 * k + (pc[1] if mine else 1 - pc[1]))

    return _dw_call(
        name, hb, dproj,
        pl.BlockSpec((tt, D), lambda k, t, pc: (t, 0)),
        pl.BlockSpec((None, tt, D), lambda k, t, pc: (comp(k, pc), t, 0)),
        pl.BlockSpec((None, D, D), lambda k, t, pc: (k, 0, 0)),
        SDS((NCHIP, D, D), F32), NCHIP, tt, job, place)


def _dw_gate_up(h2b, dgu, job=None):
    tt = min(DW_TOKENS, h2b.shape[0])
    return _dw_call(
        "dw_gate_up", h2b, dgu,
        pl.BlockSpec((tt, D), lambda k, t: (t, 0)),
        pl.BlockSpec((None, tt, FFS), lambda k, t: (k // 2, t, k % 2)),
        pl.BlockSpec((None, D, FFS), lambda k, t: (k, 0, 0)),
        SDS((NCHIP, D, FFS), F32), NCHIP, tt, job)


def _dw_down(act, dx2b, job=None):
    tt = min(DW_TOKENS, act.shape[0])
    g, job_out = _dw_call(
        "dw_down", act, dx2b,
        pl.BlockSpec((tt, FFS), lambda k, t: (t, k)),
        pl.BlockSpec((tt, D), lambda k, t: (t, 0)),
        pl.BlockSpec((FFS, D), lambda k, t: (k, 0)),
        SDS((FF, D), F32), 2, tt, job)
    return g.reshape(NCHIP, FF // NCHIP, D), job_out


def _dw_square(name, a, b, job=None):
    tt = min(DW_TOKENS, a.shape[0])
    g, job_out = _dw_call(
        name, a, b,
        pl.BlockSpec((tt, D), lambda k, t: (t, 0)), pl.BlockSpec((tt, D), lambda k, t: (t, 0)),
        pl.BlockSpec((D, D), lambda k, t: (0, 0)), SDS((D, D), F32), 1, tt, job)
    return g.reshape(NCHIP, D // NCHIP, D), job_out


def _place():
    x, y, c = lax.axis_index("x"), lax.axis_index("y"), lax.axis_index("c")
    return x, y, c, 2 * x + y


def _chip_at(x, y, s):
    return x ^ (s >> 1), y ^ (s & 1)


class _Job:
    def __init__(self, ins, out_shapes, sems, start, finish, aliases=None, mid=None):
        self.ins, self.out_shapes, self.sems = list(ins), list(out_shapes), list(sems)
        self.start, self.finish, self.aliases = start, finish, dict(aliases or {})
        self.mid = mid if mid is not None else (lambda ins, outs, sems: None)


def _join_jobs(*jobs):
    def cut(refs, sizes):
        out, at = [], 0
        for n in sizes:
            out.append(refs[at:at + n])
            at += n
        return out

    ni = [len(j.ins) for j in jobs]
    no = [len(j.out_shapes) for j in jobs]
    ns = [len(j.sems) for j in jobs]

    def run(which):
        def go(ins, outs, sems):
            for j, a, b, c in zip(jobs, cut(ins, ni), cut(outs, no), cut(sems, ns)):
                getattr(j, which)(a, b, c)
        return go

    aliases = {}
    for k, j in enumerate(jobs):
        for a, b in j.aliases.items():
            aliases[sum(ni[:k]) + a] = sum(no[:k]) + b
    return _Job([a for j in jobs for a in j.ins], [o for j in jobs for o in j.out_shapes],
                [s for j in jobs for s in j.sems], run("start"), run("finish"), aliases, run("mid"))


def _call(body, *, name, grid, in_specs, out_specs, out_shape, args, scratch_shapes=(), aliases=None,
          job=None, prefetch=None):
    n_in, n_out, n_scr = len(in_specs), len(out_specs), len(scratch_shapes)
    npf = 0 if prefetch is None else 1
    job = job if job is not None else _Job([], [], [], lambda *a: None, lambda *a: None)
    ji, jo = len(job.ins), len(job.out_shapes)
    steps = math.prod(grid)

    def wrapped(*refs):
        pf, refs = refs[:npf], refs[npf:]
        ins, jin = refs[:n_in], refs[n_in:n_in + ji]
        o0 = n_in + ji
        outs, jout = refs[o0:o0 + n_out], refs[o0 + n_out:o0 + n_out + jo]
        s0 = o0 + n_out + jo
        scr, jsem = refs[s0:s0 + n_scr], refs[s0 + n_scr:]
        step = functools.reduce(lambda acc, ag: acc * ag[1] + pl.program_id(ag[0]), enumerate(grid), 0)
        if ji or jo:
            @pl.when(step == 0)
            def _():
                job.start(jin, jout, jsem)

        body(*pf, *ins, *outs, *scr)

        if ji or jo:
            @pl.when(step == steps // 2)
            def _():
                job.mid(jin, jout, jsem)

            @pl.when(step == steps - 1)
            def _():
                job.finish(jin, jout, jsem)

    io = {npf + a: b for a, b in dict(aliases or {}).items()}
    io.update({npf + n_in + a: n_out + b for a, b in job.aliases.items()})
    kw = dict(in_specs=list(in_specs) + [ANY] * ji, out_specs=list(out_specs) + [ANY] * jo,
              scratch_shapes=list(scratch_shapes) + job.sems)
    if npf:
        kw = dict(grid_spec=pltpu.PrefetchScalarGridSpec(num_scalar_prefetch=1, grid=grid, **kw))
    else:
        kw["grid"] = grid
    res = pl.pallas_call(
        wrapped, name=name, out_shape=list(out_shape) + job.out_shapes, input_output_aliases=io,
        compiler_params=_cparams(has_side_effects=bool(ji or jo)), **kw,
    )(*(() if prefetch is None else (prefetch,)), *args, *job.ins)
    return list(res[:n_out]), list(res[n_out:])


def _cast_shards(name, place, ws, paired=False):
    n = len(ws)
    rows, cols = ws[0].shape
    tr = 352 if rows % 352 == 0 else 256
    shape = (2, rows, 2 * cols) if paired else (NCHIP, rows, cols)
    mine = (lambda i, pc: (pc[0] // 2, i, pc[0] % 2)) if paired else (lambda i, pc: (pc[0], i, 0))

    def body(pc_ref, *refs):
        del pc_ref
        for w_ref, o_ref in zip(refs[:n], refs[n:]):
            o_ref[...] = w_ref[...].astype(BF16)

    return pl.pallas_call(
        body, name=name,
        grid_spec=pltpu.PrefetchScalarGridSpec(
            num_scalar_prefetch=1, grid=(rows // tr,),
            in_specs=[pl.BlockSpec((tr, cols), lambda i, pc: (i, 0))] * n,
            out_specs=[pl.BlockSpec((None, tr, cols), mine)] * n),
        out_shape=[SDS(shape, BF16)] * n,
        compiler_params=_cparams(),
    )(place, *ws)


def _sibling_copy(ref, send_sem, recv_sem):
    x, y, c, _ = _place()
    return pltpu.make_async_remote_copy(src_ref=ref, dst_ref=ref, send_sem=send_sem, recv_sem=recv_sem,
                                        device_id=(x, y, 1 - c), device_id_type=MESH)


def _slot(arr, chip):
    if arr.shape[0] == NCHIP:
        return arr.at[chip]
    cols = arr.shape[2] // 2
    return arr.at[chip // 2, :, pl.ds(pl.multiple_of((chip % 2) * cols, 128), cols)]


def _half_rows(arr, slot, core):
    half = arr.shape[1] // 2
    return _slot(arr, slot).at[pl.ds(pl.multiple_of(core * half, 16), half)]


def _quarter_rows(arr, slot, core, q):
    quarter = arr.shape[1] // 4
    return _slot(arr, slot).at[pl.ds(pl.multiple_of((2 * core + q) * quarter, 16), quarter)]


def _chip_copy(ref, dist, send_sem, recv_sem):
    x, y, c, _ = _place()
    cx, cy = _chip_at(x, y, dist)
    return pltpu.make_async_remote_copy(src_ref=ref, dst_ref=ref, send_sem=send_sem, recv_sem=recv_sem,
                                        device_id=(cx, cy, c), device_id_type=MESH)


def _gather_sems(n):
    dma = pltpu.SemaphoreType.DMA
    return [dma((n, 2))] * 4 + [dma((n, 4))] * 2


def _gather_start(arrs, sems):
    dsend, drecv = sems[0], sems[1]
    _, _, c, j = _place()
    for w, arr in enumerate(arrs):
        for dist in (1, 2):
            _chip_copy(_half_rows(arr, j, c), dist, dsend.at[w, dist - 1], drecv.at[w, dist - 1]).start()


def _gather_land(arrs, sems, dist, first=0):
    dsend, drecv, rsend, rrecv, fsend, frecv = sems
    _, _, c, j = _place()
    if dist < 3:
        other = 3 - dist
        for w, arr in enumerate(arrs, first):
            landed = _half_rows(arr, j ^ dist, c)
            _chip_copy(landed, dist, dsend.at[w, dist - 1], drecv.at[w, dist - 1]).wait_recv()
            relay = _quarter_rows(arr, j ^ dist, c, other - 1)
            _chip_copy(relay, other, rsend.at[w, other - 1], rrecv.at[w, other - 1]).start()
            _sibling_copy(landed, fsend.at[w, dist - 1], frecv.at[w, dist - 1]).start()
        for w, arr in enumerate(arrs, first):
            theirs = _half_rows(arr, j ^ dist, 1 - c)
            _sibling_copy(theirs, fsend.at[w, dist - 1], frecv.at[w, dist - 1]).wait_recv()
    else:
        for w, arr in enumerate(arrs, first):
            for via in (1, 2):
                piece = _quarter_rows(arr, j ^ 3, c, via - 1)
                _chip_copy(piece, via, rsend.at[w, via - 1], rrecv.at[w, via - 1]).wait_recv()
                _sibling_copy(piece, fsend.at[w, 1 + via], frecv.at[w, 1 + via]).start()
        for w, arr in enumerate(arrs, first):
            for via in (1, 2):
                theirs = _quarter_rows(arr, j ^ 3, 1 - c, via - 1)
                _sibling_copy(theirs, fsend.at[w, 1 + via], frecv.at[w, 1 + via]).wait_recv()


def _gather_drain(arrs, sems):
    dsend, drecv, rsend, rrecv, fsend, frecv = sems
    _, _, c, j = _place()
    for w, arr in enumerate(arrs):
        for dist in (1, 2):
            other = 3 - dist
            _chip_copy(_half_rows(arr, j, c), dist, dsend.at[w, dist - 1], drecv.at[w, dist - 1]).wait_send()
            _chip_copy(_quarter_rows(arr, j ^ dist, c, other - 1), other,
                       rsend.at[w, other - 1], rrecv.at[w, other - 1]).wait_send()
            _sibling_copy(_half_rows(arr, j ^ dist, c), fsend.at[w, dist - 1], frecv.at[w, dist - 1]).wait_send()
            _sibling_copy(_quarter_rows(arr, j ^ 3, c, dist - 1),
                          fsend.at[w, 1 + dist], frecv.at[w, 1 + dist]).wait_send()


def _gather_neighbours(arrs, sems):
    _gather_land(arrs, sems, 1)
    _gather_land(arrs, sems, 2)


def _gather_finish(arrs, sems):
    _gather_land(arrs, sems, 3)
    _gather_drain(arrs, sems)


def _gather_job(arrs):
    n = len(arrs)
    return _Job(arrs, [SDS(a.shape, a.dtype) for a in arrs], _gather_sems(n),
                lambda ins, outs, sems: _gather_start(outs, sems),
                lambda ins, outs, sems: _gather_finish(outs, sems), {k: k for k in range(n)},
                mid=lambda ins, outs, sems: _gather_neighbours(outs, sems))


def _exchange_job(arrs, out_shapes, n, copies):
    def start(ins, outs, sems):
        for cp in copies(ins, outs, sems[0], sems[1]):
            cp.start()

    def finish(ins, outs, sems):
        for cp in copies(ins, outs, sems[0], sems[1]):
            cp.wait()

    return _Job(arrs, out_shapes, [pltpu.SemaphoreType.DMA((n,))] * 2, start, finish)


def _pair_exchange_job(grads):
    def copies(ins, outs, send_sem, recv_sem):
        x, y, c, _ = _place()
        res = []
        for w in range(len(grads)):
            half = ins[w].shape[1] // 2
            theirs = pl.ds(pl.multiple_of((1 - c) * half, 8), half)
            res.append(pltpu.make_async_remote_copy(
                src_ref=ins[w].at[:, theirs, :], dst_ref=outs[w], send_sem=send_sem.at[w],
                recv_sem=recv_sem.at[w], device_id=(x, y, 1 - c), device_id_type=MESH))
        return res

    return _exchange_job(grads, [SDS((NCHIP, g.shape[1] // 2, g.shape[2]), F32) for g in grads],
                         len(grads), copies)


def _row_tile(rows, cols):
    tr = rows
    while tr * cols * 4 > ELEMENTWISE_BLOCK_BYTES and tr % 32 == 0:
        tr //= 2
    return tr


def _pair_sums(name, place, gs, sibs):
    n = len(gs)
    half, cols = sibs[0].shape[1], sibs[0].shape[2]
    tr = _row_tile(half, cols)
    nt = half // tr
    mine = nt if gs[0].shape[1] == 2 * half else 0

    def body(pc_ref, *refs):
        del pc_ref
        for g_ref, s_ref, own_ref, out_ref in zip(refs[:n], refs[n:2 * n], refs[2 * n:3 * n], refs[3 * n:]):
            v = g_ref[...] + s_ref[...]

            @pl.when(pl.program_id(1) == 0)
            def _():
                own_ref[...] = v

            @pl.when(pl.program_id(1) > 0)
            def _():
                out_ref[...] = v.astype(BF16)

    res = pl.pallas_call(
        body, name=name,
        grid_spec=pltpu.PrefetchScalarGridSpec(
            num_scalar_prefetch=1, grid=(nt, NCHIP),
            in_specs=[pl.BlockSpec((None, tr, cols), lambda i, s, pc: (pc[0] ^ s, pc[1] * mine + i, 0))] * n
            + [pl.BlockSpec((None, tr, cols), lambda i, s, pc: (pc[0] ^ s, i, 0))] * n,
            out_specs=[pl.BlockSpec((tr, cols), lambda i, s, pc: (i, 0))] * n
            + [pl.BlockSpec((None, tr, cols), lambda i, s, pc: (jnp.maximum(s - 1, 0), i, 0))] * n),
        out_shape=[SDS((half, cols), F32)] * n + [SDS((NCHIP - 1, half, cols), BF16)] * n,
        compiler_params=_cparams(),
    )(place, *gs, *sibs)
    return res[:n], res[n:]


def _chip_exchange_job(parts):
    def copies(ins, outs, send_sem, recv_sem):
        x, y, c, _ = _place()
        res = []
        for w in range(len(parts)):
            for s in range(1, NCHIP):
                cx, cy = _chip_at(x, y, s)
                k = w * (NCHIP - 1) + s - 1
                res.append(pltpu.make_async_remote_copy(
                    src_ref=ins[w].at[s - 1], dst_ref=outs[w].at[s - 1], send_sem=send_sem.at[k],
                    recv_sem=recv_sem.at[k], device_id=(cx, cy, c), device_id_type=MESH))
        return res

    return _exchange_job(parts, [SDS((NCHIP - 1,) + p.shape[1:], BF16) for p in parts],
                         len(parts) * (NCHIP - 1), copies)


def _chip_sums(name, owns, rems):
    n = len(owns)
    half, cols = owns[0].shape
    tr = _row_tile(half, cols)

    def body(*refs):
        for own_ref, rem_ref, out_ref in zip(refs[:n], refs[n:2 * n], refs[2 * n:]):
            out_ref[...] = (((own_ref[...] + rem_ref[0].astype(F32)) + rem_ref[1].astype(F32))
                            + rem_ref[2].astype(F32))

    return pl.pallas_call(
        body, name=name, grid=(half // tr,),
        in_specs=[pl.BlockSpec((tr, cols), lambda i: (i, 0))] * n
        + [pl.BlockSpec((NCHIP - 1, tr, cols), lambda i: (0, i, 0))] * n,
        out_specs=[pl.BlockSpec((tr, cols), lambda i: (i, 0))] * n,
        out_shape=[SDS((half, cols), F32)] * n,
        compiler_params=_cparams(),
    )(*owns, *rems)


def _share_halves_job(halves):
    def copies(ins, outs, send_sem, recv_sem):
        x, y, c, _ = _place()
        return [pltpu.make_async_remote_copy(
            src_ref=ins[w], dst_ref=outs[w], send_sem=send_sem.at[w], recv_sem=recv_sem.at[w],
            device_id=(x, y, 1 - c), device_id_type=MESH) for w in range(len(halves))]

    return _exchange_job(halves, [SDS(h.shape, F32) for h in halves], len(halves), copies)


def _adamw_math(w, g, m, v):
    m = B1 * m + (1.0 - B1) * g
    v = B2 * v + (1.0 - B2) * (g * g)
    m_hat = m / (1.0 - B1 ** STEP)
    v_hat = v / (1.0 - B2 ** STEP)
    delta = -LR * (m_hat / (jnp.sqrt(v_hat) + AEPS) + WD * w)
    return delta, m, v


def _adamws(name, place, ws, owns, sibs, ms, vs):
    n = len(ws)
    rows, cols = ws[0].shape
    by_cols = owns[0].shape[0] == rows
    half, pc_cols = (rows, cols // 2) if by_cols else (rows // 2, cols)
    tr = _row_tile(half, pc_cols)
    nt = half // tr

    def body(pc_ref, *refs):
        ins, outs = refs[:5 * n], refs[5 * n:]
        for k in range(n):
            w_ref, own_ref, sib_ref, m_ref, v_ref = ins[5 * k:5 * k + 5]
            g = jnp.where(pl.program_id(0) == pc_ref[1], own_ref[...], sib_ref[...])
            d, mn, vn = _adamw_math(w_ref[...], g, m_ref[...], v_ref[...])
            for ref, val in zip(outs[4 * k:4 * k + 4], (g, d, mn, vn)):
                ref[...] = val

    full = pl.BlockSpec((tr, pc_cols), (lambda h, i, pc: (i, h)) if by_cols else (lambda h, i, pc: (h * nt + i, 0)))
    part = pl.BlockSpec((tr, pc_cols), lambda h, i, pc: (i, 0))
    res = pl.pallas_call(
        body, name=name,
        grid_spec=pltpu.PrefetchScalarGridSpec(
            num_scalar_prefetch=1, grid=(2, nt),
            in_specs=[full, part, part, full, full] * n, out_specs=[full] * (4 * n)),
        out_shape=[SDS((rows, cols), F32)] * (4 * n),
        compiler_params=_cparams(),
    )(place, *[a for group in zip(ws, owns, sibs, ms, vs) for a in group])
    return [tuple(res[4 * k:4 * k + 4]) for k in range(n)]


def _small_allreduce_adamw(sp, wmv, job):
    shape = sp.shape
    ji, jo = len(job.ins), len(job.out_shapes)

    def body(sp_ref, wmv_ref, *rest):
        jin, (g_ref, d_ref, mo_ref, vo_ref), jout = rest[:ji], rest[ji:ji + 4], rest[ji + 4:ji + 4 + jo]
        sib_s, pair_s, chip_s, send_sem, recv_sem = rest[ji + 4 + jo:ji + 9 + jo]
        jsem = rest[ji + 9 + jo:]
        job.start(jin, jout, jsem)
        x, y, c, j = _place()
        cp = pltpu.make_async_remote_copy(
            src_ref=sp_ref, dst_ref=sib_s, send_sem=send_sem.at[0], recv_sem=recv_sem.at[0],
            device_id=(x, y, 1 - c), device_id_type=MESH)
        cp.start()
        cp.wait()
        pair_s[...] = sp_ref[...] + sib_s[...]
        half = shape[0] // 2
        mine = pl.ds(pl.multiple_of(c * half, 8), half)
        cps = []
        for s in range(1, NCHIP):
            cx, cy = _chip_at(x, y, s)
            cp = pltpu.make_async_remote_copy(
                src_ref=pair_s.at[mine], dst_ref=chip_s.at[s, mine], send_sem=send_sem.at[s],
                recv_sem=recv_sem.at[s], device_id=(cx, cy, c), device_id_type=MESH)
            cp.start()
            cps.append(cp)
        chip_s[0] = pair_s[...]
        for cp in cps:
            cp.wait()
        cps = []
        for s in range(1, NCHIP):
            cp = pltpu.make_async_remote_copy(
                src_ref=chip_s.at[s, mine], dst_ref=chip_s.at[s, mine], send_sem=send_sem.at[NCHIP + s],
                recv_sem=recv_sem.at[NCHIP + s], device_id=(x, y, 1 - c), device_id_type=MESH)
            cp.start()
            cps.append(cp)
        for cp in cps:
            cp.wait()
        tot = chip_s[j]
        for k in range(1, NCHIP):
            tot = tot + chip_s[k ^ j]
        g_ref[...] = tot
        d, mn, vn = _adamw_math(wmv_ref[0], tot, wmv_ref[1], wmv_ref[2])
        d_ref[...] = d
        mo_ref[...] = mn
        vo_ref[...] = vn
        job.mid(jin, jout, jsem)
        job.finish(jin, jout, jsem)

    vm = pl.BlockSpec(memory_space=pltpu.VMEM)
    res = pl.pallas_call(
        body, name="small_allreduce_adamw",
        in_specs=[vm] * 2 + [ANY] * ji, out_specs=[vm] * 4 + [ANY] * jo,
        out_shape=[SDS(shape, F32)] * 4 + job.out_shapes,
        scratch_shapes=[pltpu.VMEM(shape, F32), pltpu.VMEM(shape, F32), pltpu.VMEM((NCHIP,) + shape, F32),
                        pltpu.SemaphoreType.DMA((2 * NCHIP,)), pltpu.SemaphoreType.DMA((2 * NCHIP,))] + job.sems,
        input_output_aliases={2 + a: 4 + b for a, b in job.aliases.items()},
        compiler_params=pltpu.CompilerParams(has_side_effects=True),
    )(sp, wmv, *job.ins)
    return res[:4], res[4:]


def _pack_small(first, mix, ln_g, ln_b, b_s, lbt, hn, ffn, fin, w_s):
    rows = [first.reshape(1, D), mix.reshape(1, D), ln_g.reshape(1, D), ln_b.reshape(1, D),
            b_s.reshape(1, D), lbt.reshape(2, D), hn.reshape(1, D), ffn.reshape(1, D), fin.reshape(1, D),
            jnp.zeros((6, D), F32)]
    return jnp.concatenate(rows + [w_s.reshape(NG, GCH, GCH).transpose(1, 0, 2).reshape(GCH, D)], axis=0)


def _unpack_small(p):
    w_s = p[16:].reshape(GCH, NG, GCH).transpose(1, 0, 2).reshape(1, NG, GCH, GCH)
    return dict(norm_mix_g=p[1:2], gmlp_ln_g=p[2:3], gmlp_ln_b=p[3:4], gmlp_b_s=p[4].reshape(1, NG, GCH),
                hgrn_lb_table=p[5:7], hgrn_norm_g=p[7:8], norm_ffn_g=p[8:9], norm_final_g=p[9],
                gmlp_w_s=w_s)


SMALL = ("norm_mix_g", "gmlp_ln_g", "gmlp_ln_b", "gmlp_w_s", "gmlp_b_s", "hgrn_lb_table", "hgrn_norm_g",
         "norm_ffn_g", "norm_final_g")
BIG = ("w_in", "w_gate_up", "w_branch_a", "w_branch_b", "w_out", "w_down")
ORDER = ("norm_mix_g", "w_in", "gmlp_ln_g", "gmlp_ln_b", "gmlp_w_s", "gmlp_b_s", "hgrn_lb_table",
         "hgrn_norm_g", "w_branch_a", "w_branch_b", "w_out", "norm_ffn_g", "w_gate_up", "w_down",
         "norm_final_g")


def kernel(x, norm_mix_g, w_in, gmlp_ln_g, gmlp_ln_b, gmlp_w_s, gmlp_b_s, hgrn_lb_table, hgrn_norm_g, w_branch_a, w_branch_b, w_out, norm_ffn_g, w_gate_up, w_down, norm_final_g, loss_target, m_norm_mix_g, m_w_in, m_gmlp_ln_g, m_gmlp_ln_b, m_gmlp_w_s, m_gmlp_b_s, m_hgrn_lb_table, m_hgrn_norm_g, m_w_branch_a, m_w_branch_b, m_w_out, m_norm_ffn_g, m_w_gate_up, m_w_down, m_norm_final_g, v_norm_mix_g, v_w_in, v_gmlp_ln_g, v_gmlp_ln_b, v_gmlp_w_s, v_gmlp_b_s, v_hgrn_lb_table, v_hgrn_norm_g, v_w_branch_a, v_w_branch_b, v_w_out, v_norm_ffn_g, v_w_gate_up, v_w_down, v_norm_final_g):
    args = dict(locals())
    T = x.shape[1]
    xs = x.reshape(T, D)
    target = loss_target.reshape(T, D)
    big = {n: args[n].reshape(args[n].shape[1:]) for n in BIG}
    big_m = {n: args["m_" + n].reshape(args[n].shape[1:]) for n in BIG}
    big_v = {n: args["v_" + n].reshape(args[n].shape[1:]) for n in BIG}

    x_i, y_i, c_i = lax.axis_index("x"), lax.axis_index("y"), lax.axis_index("c")
    place = jnp.stack([2 * x_i + y_i, c_i]).astype(jnp.int32)
    def by_shape(names):
        groups = []
        for n in names:
            if groups and big[groups[-1][0]].shape == big[n].shape:
                groups[-1].append(n)
            else:
                groups.append([n])
        return groups

    cast = {}
    for grp in by_shape(BIG):
        cast.update(zip(grp, _cast_shards("cast_" + grp[0], place, [big[n] for n in grp],
                                          paired=grp[0] == "w_gate_up")))
    tril = jnp.tril(jnp.ones((GCH, GCH), bool))
    wm = jnp.where(tril, gmlp_w_s[0], 0.0).astype(BF16)
    wm_t = jnp.swapaxes(wm, 1, 2)
    b_t = gmlp_b_s[0].T

    (proj, hb), w_in4, (w_a4, w_b4, w_out4, w_down4) = _proj_fwd(
        place, xs, norm_mix_g, cast["w_in"], [cast[n] for n in ("w_branch_a", "w_branch_b", "w_out", "w_down")])
    (ab,), _ = _gmlp_fwd(proj, gmlp_ln_g, gmlp_ln_b, wm, b_t)
    (o_raw, obb, st_before), (w_gu,) = _hgrn_fwd(
        proj, hgrn_lb_table, hgrn_norm_g, job=_gather_job([cast["w_gate_up"]]))
    w_a, w_b, w_o = (w.reshape(D, D) for w in (w_a4, w_b4, w_out4))
    (mgb, x1), _ = _merge_fwd(xs, ab, obb, proj, w_a, w_b, w_o)
    w_dn = w_down4.reshape(FF, D)
    act, dx2b, h2b, dgu, dx1, dx1b, acc_ffn = _ffn_fwd_bwd(
        x1, target, norm_ffn_g, norm_final_g.reshape(1, D), w_gu, w_dn)

    grads, owns, parts, halves, sibh = {}, {}, {}, {}, {}

    def pair_sums(names, sibs):
        sib_of = dict(zip(names, sibs))
        for grp in by_shape(names):
            o, p = _pair_sums("rs_pair_sum_" + grp[0], place, [grads[n] for n in grp], [sib_of[n] for n in grp])
            owns.update(zip(grp, o))
            parts.update(zip(grp, p))

    def chip_sums(names, got):
        rem_of = dict(zip(names, got))
        for grp in by_shape(names):
            h = _chip_sums("rs_chip_sum_" + grp[0], [owns[n] for n in grp], [rem_of[n] for n in grp])
            halves.update(zip(grp, h))

    ffn, mix = ("w_gate_up", "w_down"), ("w_branch_a", "w_branch_b", "w_out")
    grads["w_gate_up"], _ = _dw_gate_up(h2b, dgu)
    grads["w_down"], _ = _dw_down(act, dx2b)
    (dya, dyb, dproj), got = _merge_bwd(
        dx1b, ab, obb, proj, w_o, w_a, w_b, job=_pair_exchange_job([grads[n] for n in ffn]))
    pair_sums(ffn, got)
    grads["w_branch_a"], _ = _dw_square("dw_branch_a", ab, dya)
    grads["w_branch_b"], _ = _dw_square("dw_branch_b", obb, dyb)
    grads["w_out"], _ = _dw_square("dw_out", mgb, dx1b)
    (dproj, acc_hgrn), got = _hgrn_bwd(
        dproj, dyb, w_b, o_raw, proj, st_before, hgrn_lb_table, hgrn_norm_g,
        job=_join_jobs(_chip_exchange_job([parts[n] for n in ffn]), _pair_exchange_job([grads[n] for n in mix])))
    chip_sums(ffn, got[:2])
    pair_sums(mix, got[2:])
    dproj, acc_ln, dws, dmix = _gmlp_bwd(dproj, dya, w_a, proj, gmlp_ln_g, gmlp_ln_b, wm, wm_t, b_t)
    for_sibling, got = _dw_in_half(
        "dw_in_sibling_half", place, hb, dproj, False,
        job=_join_jobs(_share_halves_job([halves[n] for n in ffn]), _chip_exchange_job([parts[n] for n in mix])))
    sibh.update(zip(ffn, got[:2]))
    chip_sums(mix, got[2:])
    grads["w_in"], got = _dw_in_half(
        "dw_in_own_half", place, hb, dproj, True, job=_share_halves_job([for_sibling]))
    pair_sums(("w_in",), got)
    (grad_x, acc_mix), got = _proj_bwd(
        dproj, w_in4, xs, dx1, norm_mix_g,
        job=_join_jobs(_chip_exchange_job([parts["w_in"]]), _share_halves_job([halves[n] for n in mix])))
    chip_sums(("w_in",), got[:1])
    sibh.update(zip(mix, got[1:]))

    lbv = jax.nn.sigmoid(hgrn_lb_table[0] - hgrn_lb_table[1])
    d_t0 = jnp.sum(acc_hgrn[0], axis=0) * lbv * (1.0 - lbv)
    loss_row = jnp.zeros((D,), F32).at[0].set(jnp.sum(acc_ffn[0]))
    dws_m = jnp.where(tril[:, None, :], dws.reshape(GCH, NG, GCH), 0.0).transpose(1, 0, 2)
    db_s = jnp.sum(dmix.reshape(GCH, NG, GCH), axis=-1).T
    sp = _pack_small(loss_row, jnp.sum(acc_mix, 0), jnp.sum(acc_ln[0], 0), jnp.sum(acc_ln[1], 0), db_s,
                     jnp.stack([d_t0, -d_t0]), jnp.sum(acc_hgrn[1], 0), jnp.sum(acc_ffn[2], 0),
                     jnp.sum(acc_ffn[1], 0), dws_m)
    zero = jnp.zeros((D,), F32)

    def pack(prefix):
        a = lambda n: args[prefix + n]
        return _pack_small(zero, a("norm_mix_g"), a("gmlp_ln_g"), a("gmlp_ln_b"), a("gmlp_b_s"),
                           a("hgrn_lb_table"), a("hgrn_norm_g"), a("norm_ffn_g"), a("norm_final_g"),
                           a("gmlp_w_s"))

    packed, (sibh["w_in"],) = _small_allreduce_adamw(
        sp, jnp.stack([pack(""), pack("m_"), pack("v_")]), _share_halves_job([halves["w_in"]]))
    loss = packed[0][0, 0]
    small = [_unpack_small(p) for p in packed]
    out = {n: tuple(s[n] for s in small) for n in SMALL}
    for grp in by_shape(BIG):
        res = _adamws("adamw_" + grp[0], place, *[[d[n] for n in grp] for d in (big, halves, sibh, big_m, big_v)])
        for n, quad in zip(grp, res):
            out[n] = tuple(a.reshape(args[n].shape) for a in quad)
    return (loss, grad_x.reshape(x.shape), *[out[n][0] for n in ORDER], *[out[n][1] for n in ORDER],
            *[out[n][2] for n in ORDER], *[out[n][3] for n in ORDER])
```

```python
import functools
import math

import jax
import jax.numpy as jnp
from jax import lax
from jax.experimental import pallas as pl
from jax.experimental.pallas import tpu as pltpu

F32 = jnp.float32
BF16 = jnp.bfloat16
SDS = jax.ShapeDtypeStruct
MESH = pl.DeviceIdType.MESH
ANY = pl.BlockSpec(memory_space=pl.ANY)

D = 1024
NIN = 8
NG = 8
GCH = 128
NH = 8
HD = 128
HCH = 64
HGRN_HB = 8
HGRN_TOKENS = 256
GMLP_FWD_TOKENS = 512
GMLP_BWD_TOKENS = 256
HW = HGRN_HB * HD
DW_TOKENS = 2048
ELEMENTWISE_BLOCK_BYTES = 2 * 1024 * 1024
PROJ_OUT_SLOTS = 4
FF = 2816
FFS = 1408
NCHIP = 4
EPS = 1e-6
QSCALE = HD ** -0.5
GELU_C0 = math.sqrt(2.0 / math.pi)
GELU_C1 = 0.044715
LR, B1, B2, AEPS, WD, STEP = 0.001, 0.9, 0.999, 1e-08, 0.01, 10
VMEM_LIMIT_V7X = 56 * 1024 * 1024
SP_ROWS = 144


def _cparams(**kw):
    return pltpu.CompilerParams(vmem_limit_bytes=VMEM_LIMIT_V7X, **kw)


def _mm(a, b):
    return jnp.dot(a, b, preferred_element_type=F32)


def _mm_nt(a, b):
    return lax.dot_general(a, b, (((1,), (1,)), ((), ())), preferred_element_type=F32)


def _mm_tn(a, b):
    return lax.dot_general(a, b, (((0,), (0,)), ((), ())), preferred_element_type=F32)


def _rows8(x):
    r, c = x.shape
    return jnp.sum(x.reshape(r // 8, 8, c), axis=0)


def _mean(x):
    return jnp.mean(x, axis=-1, keepdims=True)


def _sigmoid(x):
    return 1.0 / (1.0 + jnp.exp(-x))


def _gelu(x):
    t = jnp.tanh(GELU_C0 * (x + GELU_C1 * x * x * x))
    return 0.5 * x * (1.0 + t), t


def _gelu_grad(x, t):
    return 0.5 * (1.0 + t) + 0.5 * x * (1.0 - t * t) * (GELU_C0 * (1.0 + 3.0 * GELU_C1 * x * x))


def _component_of(group):
    return jnp.where(group < 6, (group + 4) % 6, group)


def _proj_fwd(place, x, g_mix, w_in4, later):
    T = x.shape[0]
    tm = min(1024, T)
    ni = T // tm
    n = len(later)

    def body(pc_ref, x_ref, g_ref, *rest):
        proj_ref, h_ref, w_all = rest[1 + n:4 + n]
        gathered = rest[4 + n:4 + 2 * n]
        hs, wbuf, wsem, obuf, osem = rest[4 + 2 * n:9 + 2 * n]
        w_sems, later_sems = rest[9 + 2 * n:15 + 2 * n], rest[15 + 2 * n:]
        jp, i = pl.program_id(0), pl.program_id(1)
        w_cols = [w_all.at[:, :, pl.ds(k * D, D)] for k in range(2)]

        def w_copy(blk):
            cols = pl.ds(pl.multiple_of((blk % 2) * D, 128), D)
            return pltpu.make_async_copy(w_all.at[pc_ref[0] ^ (blk // 2), :, cols], wbuf.at[blk % 2],
                                         wsem.at[blk % 2])

        @pl.when((jp == 0) & (i == 0))
        def _():
            _gather_start(w_cols, w_sems)
            w_copy(jp).start()

        @pl.when(i == 0)
        def _():
            w_copy(jp).wait()

        @pl.when(jp == 0)
        def _():
            xv = x_ref[...]
            r = lax.rsqrt(_mean(xv * xv) + EPS)
            hb = (xv * r * g_ref[...]).astype(BF16)
            hs[i] = hb
            h_ref[...] = hb

        step = jp * ni + i
        slot = step % PROJ_OUT_SLOTS

        def o_copy(slot_):
            comp = 2 * (pc_ref[0] ^ (jp // 2)) + jp % 2
            return pltpu.make_async_copy(
                obuf.at[slot_], proj_ref.at[comp, pl.ds(pl.multiple_of(i * tm, 8), tm)], osem.at[slot_])

        @pl.when(step >= PROJ_OUT_SLOTS)
        def _():
            o_copy(slot).wait()

        obuf[slot] = _mm(hs[i], wbuf[jp % 2])
        o_copy(slot).start()

        @pl.when(step == NIN * ni - 1)
        def _():
            for k in range(PROJ_OUT_SLOTS):
                o_copy((slot + 1 + k) % PROJ_OUT_SLOTS).wait()

        for nxt in range(1, NIN):
            @pl.when((jp == nxt - 1) & (i == ni - 1))
            def _():
                if nxt >= 2:
                    _gather_land([w_cols[nxt % 2]], w_sems, nxt // 2, first=nxt % 2)
                if nxt == 5:
                    _gather_start(gathered, later_sems)
                if nxt == NIN - 1:
                    _gather_neighbours(gathered, later_sems)
                w_copy(jp + 1).start()

        @pl.when((jp == NIN - 1) & (i == ni - 1))
        def _():
            _gather_drain(w_cols, w_sems)
            _gather_finish(gathered, later_sems)

    tile = lambda jp, i, pc: (jnp.where(jp == 0, i, ni - 1), 0)
    res = pl.pallas_call(
        body, name="proj_fwd",
        grid_spec=pltpu.PrefetchScalarGridSpec(
            num_scalar_prefetch=1, grid=(NIN, ni),
            in_specs=[pl.BlockSpec((tm, D), tile), pl.BlockSpec((1, D), lambda jp, i, pc: (0, 0))] + [ANY] * (1 + n),
            out_specs=[ANY, pl.BlockSpec((tm, D), tile)] + [ANY] * (1 + n),
            scratch_shapes=[pltpu.VMEM((ni, tm, D), BF16), pltpu.VMEM((2, D, D), BF16),
                            pltpu.SemaphoreType.DMA((2,)), pltpu.VMEM((PROJ_OUT_SLOTS, tm, D), F32),
                            pltpu.SemaphoreType.DMA((PROJ_OUT_SLOTS,))] + _gather_sems(2) + _gather_sems(n)),
        out_shape=[SDS((NIN, T, D), F32), SDS((T, D), BF16), SDS(w_in4.shape, BF16)]
        + [SDS(a.shape, a.dtype) for a in later],
        input_output_aliases={3 + k: 2 + k for k in range(1 + n)},
        compiler_params=_cparams(has_side_effects=True),
    )(place, x, g_mix, w_in4, *later)
    return res[:2], res[2], res[3:]


def _chunks_abreast(x):
    return jnp.concatenate([x[GCH * ch:GCH * (ch + 1)] for ch in range(x.shape[0] // GCH)], axis=1)


def _chunks_stacked(x):
    return jnp.concatenate([x[:, GCH * ch:GCH * (ch + 1)] for ch in range(x.shape[1] // GCH)], axis=0)


def _layer_norm_stats(gv):
    mu = _mean(gv)
    xc = gv - mu
    rs = lax.rsqrt(_mean(xc * xc) + EPS)
    return xc * rs, rs


def _gmlp_fwd(proj, ln_g, ln_b, wm, b_t, job=None):
    T = proj.shape[1]
    tm = min(GMLP_FWD_TOKENS, T)

    def body(u_ref, v_ref, lg_ref, lb_ref, wm_ref, bt_ref, a_ref, a_s):
        gu, _ = _gelu(u_ref[...])
        gv, _ = _gelu(v_ref[...])
        vhat, _ = _layer_norm_stats(gv)
        vnb = (vhat * lg_ref[...] + lb_ref[...]).astype(BF16)
        for g in range(NG):
            cols = slice(128 * g, 128 * (g + 1))
            mixed = _mm(wm_ref[g], _chunks_abreast(vnb[:, cols])) + bt_ref[:, g:g + 1]
            a_s[:, cols] = gu[:, cols] * _chunks_stacked(mixed)
        a_ref[...] = a_s[...].astype(BF16)

    row = lambda i: (0, 0)
    return _call(
        body, name="gmlp_fwd", grid=(T // tm,), job=job, args=(proj, proj, ln_g, ln_b, wm, b_t),
        in_specs=[pl.BlockSpec((None, tm, D), lambda i: (0, i, 0)), pl.BlockSpec((None, tm, D), lambda i: (1, i, 0)),
                  pl.BlockSpec((1, D), row), pl.BlockSpec((1, D), row),
                  pl.BlockSpec((NG, GCH, GCH), lambda i: (0, 0, 0)), pl.BlockSpec((GCH, NG), row)],
        out_specs=[pl.BlockSpec((tm, D), lambda i: (i, 0))],
        out_shape=[SDS((T, D), BF16)],
        scratch_shapes=[pltpu.VMEM((tm, D), F32)])


def _cumsum64(x, row):
    for s in (1, 2, 4, 8, 16, 32):
        x = x + jnp.where(row >= s, pltpu.roll(x, s, 0), 0.0)
    return x


def _revcumsum64(x, row):
    n = x.shape[0]
    for s in (1, 2, 4, 8, 16, 32):
        x = x + jnp.where(row < HCH - s, pltpu.roll(x, n - s, 0), 0.0)
    return x


def _head_mean(x):
    parts = [jnp.broadcast_to(_mean(x[:, HD * h:HD * (h + 1)]), (x.shape[0], HD)) for h in range(x.shape[1] // HD)]
    return jnp.concatenate(parts, axis=1)


def _seg_sum(x):
    n, c = x.shape
    s = jnp.sum(x.reshape(n // HCH, HCH, c), axis=1, keepdims=True)
    return jnp.broadcast_to(s, (n // HCH, HCH, c)).reshape(n, c)


def _seg_row(x, idx):
    n, c = x.shape
    x3 = x.reshape(n // HCH, HCH, c)
    return jnp.broadcast_to(x3[:, idx:idx + 1, :], x3.shape).reshape(n, c)


def _hgrn_gates(fl, lbv, row):
    s = _sigmoid(fl)
    f = lbv + (1.0 - lbv) * s
    a = _cumsum64(jnp.log(f), row)
    return s, f, a, _seg_row(a, HCH // 2 - 1), _seg_row(a, HCH - 1)


def _hgrn_fwd(proj, lb_table, norm_g, job=None):
    T = proj.shape[1]
    tb = min(HGRN_TOKENS, T)
    nc = tb // HCH

    def body(q_ref, fl_ref, v_ref, g_ref, lbt_ref, gn_ref, o_ref, ob_ref, stb_ref, st_s, o_s):
        @pl.when(pl.program_id(1) == 0)
        def _():
            st_s[...] = jnp.zeros_like(st_s)

        row = lax.broadcasted_iota(jnp.int32, (tb, HW), 0) & (HCH - 1)
        lbv = _sigmoid(lbt_ref[0:1, :] - lbt_ref[1:2, :])
        _, f, a, a_mid, a_last = _hgrn_gates(fl_ref[...], lbv, row)
        k = 1.0 - f
        qs = q_ref[...] * QSCALE
        q_in = (qs * jnp.exp(a - a_mid)).astype(BF16)
        k_in = (k * jnp.exp(a_mid - a)).astype(BF16)
        q_a = (qs * jnp.exp(a)).astype(BF16)
        k_d = (k * jnp.exp(a_last - a)).astype(BF16)
        dec = jnp.exp(a_last)
        vb = v_ref[...].astype(BF16)
        tri = (lax.broadcasted_iota(jnp.int32, (HCH, HCH), 0)
               >= lax.broadcasted_iota(jnp.int32, (HCH, HCH), 1))
        for c in range(nc):
            sl = slice(HCH * c, HCH * (c + 1))
            for hh in range(HGRN_HB):
                hs = slice(HD * hh, HD * (hh + 1))
                st = st_s[hh]
                stb_ref[hh, c] = st
                sc = jnp.where(tri, _mm_nt(q_in[sl, hs], k_in[sl, hs]), 0.0)
                o_s[sl, hs] = _mm(sc.astype(BF16), vb[sl, hs]) + _mm_nt(q_a[sl, hs], st.astype(BF16))
                d64 = dec[sl, hs]
                st_s[hh] = st * jnp.concatenate([d64, d64], axis=0) + _mm_tn(vb[sl, hs], k_d[sl, hs])
        o = o_s[...]
        r = lax.rsqrt(_head_mean(o * o) + EPS)
        g = g_ref[...]
        o_ref[...] = o
        ob_ref[...] = (o * r * gn_ref[...] * (g * _sigmoid(g))).astype(BF16)

    def col(off):
        return pl.BlockSpec((None, tb, HW), lambda h, cb: (off, cb, h))

    return _call(
        body, name="hgrn_fwd", grid=(NH // HGRN_HB, T // tb), job=job,
        args=(proj, proj, proj, proj, lb_table, norm_g),
        in_specs=[col(2), col(3), col(4), col(5),
                  pl.BlockSpec((2, HW), lambda h, cb: (0, h)), pl.BlockSpec((1, HW), lambda h, cb: (0, h))],
        out_specs=[pl.BlockSpec((tb, HW), lambda h, cb: (cb, h)), pl.BlockSpec((tb, HW), lambda h, cb: (cb, h)),
                   pl.BlockSpec((HGRN_HB, nc, HD, HD), lambda h, cb: (h, cb, 0, 0))],
        out_shape=[SDS((T, D), F32), SDS((T, D), BF16), SDS((NH, T // HCH, HD, HD), F32)],
        scratch_shapes=[pltpu.VMEM((HGRN_HB, HD, HD), F32), pltpu.VMEM((tb, HW), F32)])


def _merge_fwd(x, ab, ob, proj, w_a, w_b, w_out, job=None):
    T = x.shape[0]
    tm = min(512, T)

    def body(x_ref, ab_ref, ob_ref, ga_ref, gb_ref, wa_ref, wb_ref, wo_ref, mg_ref, x1_ref):
        ya = _mm(ab_ref[...], wa_ref[...])
        yb = _mm(ob_ref[...], wb_ref[...])
        merged = (_sigmoid(ga_ref[...]) * ya + _sigmoid(gb_ref[...]) * yb).astype(BF16)
        mg_ref[...] = merged
        x1_ref[...] = x_ref[...] + _mm(merged, wo_ref[...])

    t = lambda i: (i, 0)
    w = lambda i: (0, 0)
    return _call(
        body, name="merge_fwd", grid=(T // tm,), job=job, args=(x, ab, ob, proj, proj, w_a, w_b, w_out),
        in_specs=[pl.BlockSpec((tm, D), t), pl.BlockSpec((tm, D), t), pl.BlockSpec((tm, D), t),
                  pl.BlockSpec((None, tm, D), lambda i: (6, i, 0)), pl.BlockSpec((None, tm, D), lambda i: (7, i, 0)),
                  pl.BlockSpec((D, D), w), pl.BlockSpec((D, D), w), pl.BlockSpec((D, D), w)],
        out_specs=[pl.BlockSpec((tm, D), t)] * 2,
        out_shape=[SDS((T, D), BF16), SDS((T, D), F32)])


def _ffn_fwd_bwd(x1, target, g_ffn, g_fin, w_gu, w_down):
    T = x1.shape[0]
    tm = min(256, T)
    inv_d = 1.0 / D

    def body(x1_ref, tg_ref, gf_ref, gn_ref, wgu_ref, wd_ref,
             act_ref, dx2b_ref, h2b_ref, dgu_ref, dx1_ref, dx1b_ref, acc_ref):
        @pl.when(pl.program_id(0) == 0)
        def _():
            acc_ref[...] = jnp.zeros_like(acc_ref)

        x1v = x1_ref[...]
        gf = gf_ref[...]
        gn = gn_ref[...]
        rr1 = lax.rsqrt(_mean(x1v * x1v) + EPS)
        x1n = x1v * rr1
        h2b = (x1n * gf).astype(BF16)
        h2b_ref[...] = h2b
        gate = _mm(h2b, wgu_ref[0])
        up = _mm(h2b, wgu_ref[1])
        sg = _sigmoid(gate)
        si = gate * sg
        act = (si * up).astype(BF16)
        act_ref[...] = act
        x2 = x1v + _mm(act, wd_ref[...])
        rr2 = lax.rsqrt(_mean(x2 * x2) + EPS)
        x2n = x2 * rr2
        e = x2n * gn - tg_ref[...]
        acc_ref[0] += _rows8(e * e) * (0.5 * inv_d)
        dy = e * inv_d
        acc_ref[1] += _rows8(dy * x2n)
        dxn = dy * gn
        dx2 = rr2 * (dxn - x2n * _mean(dxn * x2n))
        dx2b = dx2.astype(BF16)
        dx2b_ref[...] = dx2b
        dact = _mm_nt(dx2b, wd_ref[...])
        dgate = (dact * up * (sg * (1.0 + gate * (1.0 - sg)))).astype(BF16)
        dup = (dact * si).astype(BF16)
        dgu_ref[0] = dgate
        dgu_ref[1] = dup
        dh2 = _mm_nt(dgate, wgu_ref[0]) + _mm_nt(dup, wgu_ref[1])
        acc_ref[2] += _rows8(dh2 * x1n)
        dxn1 = dh2 * gf
        dx1 = dx2 + rr1 * (dxn1 - x1n * _mean(dxn1 * x1n))
        dx1_ref[...] = dx1
        dx1b_ref[...] = dx1.astype(BF16)

    t = lambda i: (i, 0)
    w = lambda i: (0, 0)
    one = pl.Buffered(1)
    return pl.pallas_call(
        body, name="ffn_fwd_bwd", grid=(T // tm,),
        in_specs=[pl.BlockSpec((tm, D), t), pl.BlockSpec((tm, D), t),
                  pl.BlockSpec((1, D), w), pl.BlockSpec((1, D), w),
                  pl.BlockSpec((2, D, FF), lambda i: (0, 0, 0), pipeline_mode=one),
                  pl.BlockSpec((FF, D), w, pipeline_mode=one)],
        out_specs=[pl.BlockSpec((tm, FF), t), pl.BlockSpec((tm, D), t), pl.BlockSpec((tm, D), t),
                   pl.BlockSpec((2, tm, FF), lambda i: (0, i, 0)),
                   pl.BlockSpec((tm, D), t), pl.BlockSpec((tm, D), t),
                   pl.BlockSpec((3, 8, D), lambda i: (0, 0, 0))],
        out_shape=[SDS((T, FF), BF16), SDS((T, D), BF16), SDS((T, D), BF16),
                   SDS((2, T, FF), BF16), SDS((T, D), F32), SDS((T, D), BF16),
                   SDS((3, 8, D), F32)],
        compiler_params=_cparams(),
    )(x1, target, g_ffn, g_fin, w_gu, w_down)


def _merge_bwd(dx1b, ab, ob, proj, w_out, w_a, w_b, job=None):
    T = dx1b.shape[0]
    tm = min(512, T)

    def body(dx_ref, ab_ref, ob_ref, ga_ref, gb_ref, wo_ref, wa_ref, wb_ref, dya_ref, dyb_ref, dp_ref):
        dm = _mm_nt(dx_ref[...], wo_ref[...])
        sa = _sigmoid(ga_ref[...])
        sb = _sigmoid(gb_ref[...])
        dya_ref[...] = (dm * sa).astype(BF16)
        dyb_ref[...] = (dm * sb).astype(BF16)
        dp_ref[0] = (dm * _mm(ab_ref[...], wa_ref[...]) * sa * (1.0 - sa)).astype(BF16)
        dp_ref[1] = (dm * _mm(ob_ref[...], wb_ref[...]) * sb * (1.0 - sb)).astype(BF16)

    t = lambda i: (i, 0)
    w = lambda i: (0, 0)
    return _call(
        body, name="merge_bwd", grid=(T // tm,),
        in_specs=[pl.BlockSpec((tm, D), t), pl.BlockSpec((tm, D), t), pl.BlockSpec((tm, D), t),
                  pl.BlockSpec((None, tm, D), lambda i: (6, i, 0)), pl.BlockSpec((None, tm, D), lambda i: (7, i, 0)),
                  pl.BlockSpec((D, D), w), pl.BlockSpec((D, D), w), pl.BlockSpec((D, D), w)],
        out_specs=[pl.BlockSpec((tm, D), t)] * 2 + [pl.BlockSpec((2, tm, D), lambda i: (3, i, 0))],
        out_shape=[SDS((T, D), BF16), SDS((T, D), BF16), SDS((NIN, T, D), BF16)],
        args=(dx1b, ab, ob, proj, proj, w_out, w_a, w_b), job=job)


def _hgrn_bwd(dproj, dyb, w_b, o_raw, proj, st_before, lb_table, norm_g, job=None):
    T = dyb.shape[0]
    tb = min(HGRN_TOKENS, T)
    nc = tb // HCH
    nb = T // tb

    def body(dp_in, dyb_ref, wb_ref, o_ref, q_ref, fl_ref, v_ref, g_ref, stb_ref, lbt_ref, gn_ref,
             dp_ref, acc_ref, dst_s, dqin_s, dqa_s, dkin_s, dkd_s, dv_s, ddec_s):
        del dp_in

        @pl.when(pl.program_id(1) == 0)
        def _():
            dst_s[...] = jnp.zeros_like(dst_s)
            acc_ref[...] = jnp.zeros_like(acc_ref)

        row = lax.broadcasted_iota(jnp.int32, (tb, HW), 0) & (HCH - 1)
        gn = gn_ref[...]
        lbv = _sigmoid(lbt_ref[0:1, :] - lbt_ref[1:2, :])
        o = o_ref[...]
        r = lax.rsqrt(_head_mean(o * o) + EPS)
        on = o * r
        g = g_ref[...]
        sgm = _sigmoid(g)
        dob_v = _mm_nt(dyb_ref[...], wb_ref[...])
        dp_ref[3] = (dob_v * on * gn * (sgm * (1.0 + g * (1.0 - sgm)))).astype(BF16)
        do_n = dob_v * (g * sgm)
        acc_ref[1] += _rows8(do_n * on)
        dxn = do_n * gn
        do = (r * (dxn - on * _head_mean(dxn * on))).astype(BF16)
        s, f, a, a_mid, a_last = _hgrn_gates(fl_ref[...], lbv, row)
        k = 1.0 - f
        qs = q_ref[...] * QSCALE
        e_q = jnp.exp(a - a_mid)
        e_k = jnp.exp(a_mid - a)
        e_a = jnp.exp(a)
        e_l = jnp.exp(a_last - a)
        dec = jnp.exp(a_last)
        q_in = qs * e_q
        k_in = k * e_k
        q_a = qs * e_a
        k_d = k * e_l
        q_inb, k_inb, q_ab, k_db = (z.astype(BF16) for z in (q_in, k_in, q_a, k_d))
        vb = v_ref[...].astype(BF16)
        tri = (lax.broadcasted_iota(jnp.int32, (HCH, HCH), 0)
               >= lax.broadcasted_iota(jnp.int32, (HCH, HCH), 1))
        for c in reversed(range(nc)):
            sl = slice(HCH * c, HCH * (c + 1))
            for hh in range(HGRN_HB):
                hs = slice(HD * hh, HD * (hh + 1))
                stp = stb_ref[hh, c]
                dst = dst_s[hh]
                dstb = dst.astype(BF16)
                do_c = do[sl, hs]
                v_c = vb[sl, hs]
                dqa_s[sl, hs] = _mm(do_c, stp.astype(BF16))
                dkd_s[sl, hs] = _mm(v_c, dstb)
                ddec_s[sl, hs] = jnp.broadcast_to(jnp.sum(dst * stp, axis=0, keepdims=True), (HCH, HD))
                sc = jnp.where(tri, _mm_nt(q_inb[sl, hs], k_inb[sl, hs]), 0.0).astype(BF16)
                dsc = jnp.where(tri, _mm_nt(do_c, v_c), 0.0).astype(BF16)
                dv_s[sl, hs] = _mm_nt(k_db[sl, hs], dstb) + _mm_tn(sc, do_c)
                dqin_s[sl, hs] = _mm(dsc, k_inb[sl, hs])
                dkin_s[sl, hs] = _mm_tn(dsc, q_inb[sl, hs])
                d64 = dec[sl, hs]
                dst_s[hh] = dst * jnp.concatenate([d64, d64], axis=0) + _mm_tn(do_c, q_ab[sl, hs])
        dq_in = dqin_s[...]
        dq_a = dqa_s[...]
        dk_in = dkin_s[...]
        dk_d = dkd_s[...]
        dp_ref[0] = ((dq_in * e_q + dq_a * e_a) * QSCALE).astype(BF16)
        dp_ref[2] = dv_s[...].astype(BF16)
        tq = dq_in * q_in
        tk = dk_in * k_in
        td = dk_d * k_d
        d_a = tq + dq_a * q_a - tk - td
        d_a = d_a + jnp.where(row == HCH // 2 - 1, _seg_sum(tk - tq), 0.0)
        d_a = d_a + jnp.where(row == HCH - 1, _seg_sum(td) + ddec_s[...] * dec, 0.0)
        dlf = _revcumsum64(d_a, row)
        df = dlf / f - (dk_in * e_k + dk_d * e_l)
        dp_ref[1] = (df * (1.0 - lbv) * s * (1.0 - s)).astype(BF16)
        acc_ref[0] += _rows8(df * (1.0 - s))

    def col(off):
        return pl.BlockSpec((None, tb, HW), lambda h, cb: (off, nb - 1 - cb, h))

    hb = lambda h, cb: (nb - 1 - cb, h)
    return _call(
        body, name="hgrn_bwd", grid=(NH // HGRN_HB, nb), job=job,
        args=(dproj, dyb, w_b, o_raw, proj, proj, proj, proj, st_before, lb_table, norm_g),
        in_specs=[ANY, pl.BlockSpec((tb, D), lambda h, cb: (nb - 1 - cb, 0)),
                  pl.BlockSpec((HW, D), lambda h, cb: (h, 0)), pl.BlockSpec((tb, HW), hb),
                  col(2), col(3), col(4), col(5),
                  pl.BlockSpec((HGRN_HB, nc, HD, HD), lambda h, cb: (h, nb - 1 - cb, 0, 0)),
                  pl.BlockSpec((2, HW), lambda h, cb: (0, h)), pl.BlockSpec((1, HW), lambda h, cb: (0, h))],
        out_specs=[pl.BlockSpec((4, tb, HW), lambda h, cb: (0, nb - 1 - cb, h)),
                   pl.BlockSpec((2, 8, HW), lambda h, cb: (0, 0, h))],
        out_shape=[SDS(dproj.shape, BF16), SDS((2, 8, D), F32)],
        scratch_shapes=[pltpu.VMEM((HGRN_HB, HD, HD), F32)] + [pltpu.VMEM((tb, HW), F32)] * 6,
        aliases={0: 0})


def _gmlp_bwd(dproj, dya, w_a, proj, ln_g, ln_b, wm, wm_t, b_t):
    T = dya.shape[0]
    tm = min(GMLP_BWD_TOKENS, T)

    def body(dp_in, dya_ref, wa_ref, u_ref, v_ref, lg_ref, lb_ref, wm_ref, wmt_ref, bt_ref,
             dp_ref, acc_ref, dws_ref, dmix_ref, du_s, dvn_s):
        del dp_in

        @pl.when(pl.program_id(0) == 0)
        def _():
            acc_ref[...] = jnp.zeros_like(acc_ref)
            dws_ref[...] = jnp.zeros_like(dws_ref)
            dmix_ref[...] = jnp.zeros_like(dmix_ref)

        u = u_ref[...]
        v = v_ref[...]
        lg = lg_ref[...]
        gu, t_u = _gelu(u)
        gv, t_v = _gelu(v)
        vhat, rs = _layer_norm_stats(gv)
        vnb = (vhat * lg + lb_ref[...]).astype(BF16)
        da_v = _mm_nt(dya_ref[...], wa_ref[...])
        for g in range(NG):
            cols = slice(128 * g, 128 * (g + 1))
            vng = _chunks_abreast(vnb[:, cols])
            mixed = _mm(wm_ref[g], vng) + bt_ref[:, g:g + 1]
            dag = _chunks_abreast(da_v[:, cols])
            dmx = dag * _chunks_abreast(gu[:, cols])
            du_s[:, cols] = _chunks_stacked(dag * mixed)
            dmxb = dmx.astype(BF16)
            dws_ref[:, cols] += _mm_nt(dmxb, vng)
            dmix_ref[:, cols] += sum(dmx[:, GCH * ch:GCH * (ch + 1)] for ch in range(tm // GCH))
            dvn_s[:, cols] = _chunks_stacked(_mm(wmt_ref[g], dmxb))
        dp_ref[0] = (du_s[...] * _gelu_grad(u, t_u)).astype(BF16)
        dvn = dvn_s[...]
        acc_ref[0] += _rows8(dvn * vhat)
        acc_ref[1] += _rows8(dvn)
        dvh = dvn * lg
        dgv = rs * (dvh - _mean(dvh) - vhat * _mean(dvh * vhat))
        dp_ref[1] = (dgv * _gelu_grad(v, t_v)).astype(BF16)

    row = lambda i: (0, 0)
    w3 = lambda i: (0, 0, 0)
    return pl.pallas_call(
        body, name="gmlp_bwd", grid=(T // tm,),
        in_specs=[ANY, pl.BlockSpec((tm, D), lambda i: (i, 0)), pl.BlockSpec((D, D), row),
                  pl.BlockSpec((None, tm, D), lambda i: (0, i, 0)), pl.BlockSpec((None, tm, D), lambda i: (1, i, 0)),
                  pl.BlockSpec((1, D), row), pl.BlockSpec((1, D), row),
                  pl.BlockSpec((NG, GCH, GCH), w3), pl.BlockSpec((NG, GCH, GCH), w3),
                  pl.BlockSpec((GCH, NG), row)],
        out_specs=[pl.BlockSpec((2, tm, D), lambda i: (2, i, 0)),
                   pl.BlockSpec((2, 8, D), w3), pl.BlockSpec((GCH, D), row), pl.BlockSpec((GCH, D), row)],
        out_shape=[SDS(dproj.shape, BF16), SDS((2, 8, D), F32), SDS((GCH, D), F32), SDS((GCH, D), F32)],
        scratch_shapes=[pltpu.VMEM((tm, D), F32), pltpu.VMEM((tm, D), F32)],
        input_output_aliases={0: 0},
        compiler_params=_cparams(),
    )(dproj, dya, w_a, proj, proj, ln_g, ln_b, wm, wm_t, b_t)


def _proj_bwd(dproj, w_in4, x, dx1, g_mix, job=None):
    T = x.shape[0]
    tm = min(256, T)
    order = (2, 3, 4, 5, 0, 1, 6, 7)

    def body(dp_ref, w_ref, x_ref, dx1_ref, g_ref, gx_ref, acc_ref):
        @pl.when(pl.program_id(0) == 0)
        def _():
            acc_ref[...] = jnp.zeros_like(acc_ref)

        dh = None
        for m, og in enumerate(order):
            part = _mm_nt(dp_ref[m], w_ref[og // 2, :, D * (og % 2):D * (og % 2 + 1)])
            dh = part if dh is None else dh + part
        xv = x_ref[...]
        r = lax.rsqrt(_mean(xv * xv) + EPS)
        xn = xv * r
        acc_ref[...] += _rows8(dh * xn)
        dxn = dh * g_ref[...]
        gx_ref[...] = dx1_ref[...] + r * (dxn - xn * _mean(dxn * xn))

    t = lambda i: (i, 0)
    return _call(
        body, name="proj_bwd", grid=(T // tm,),
        in_specs=[pl.BlockSpec((NIN, tm, D), lambda i: (0, i, 0)),
                  pl.BlockSpec((NCHIP, D, 2 * D), lambda i: (0, 0, 0), pipeline_mode=pl.Buffered(1)),
                  pl.BlockSpec((tm, D), t), pl.BlockSpec((tm, D), t), pl.BlockSpec((1, D), lambda i: (0, 0))],
        out_specs=[pl.BlockSpec((tm, D), t), pl.BlockSpec((8, D), lambda i: (0, 0))],
        out_shape=[SDS((T, D), F32), SDS((8, D), F32)],
        args=(dproj, w_in4, x, dx1, g_mix), job=job)


def _dw_call(name, a, b, a_spec, b_spec, o_spec, out_shape, nblk, tt, job=None, prefetch=None):
    T = a.shape[-2]

    def body(*refs):
        a_ref, b_ref, o_ref = refs[-3:]

        @pl.when(pl.program_id(1) == 0)
        def _():
            o_ref[...] = jnp.zeros_like(o_ref)
        o_ref[...] += _mm_tn(a_ref[...], b_ref[...])

    (out,), job_out = _call(
        body, name=name, grid=(nblk, T // tt), in_specs=[a_spec, b_spec], out_specs=[o_spec],
        out_shape=[out_shape], args=(a, b), job=job, prefetch=prefetch)
    return out, job_out


def _dw_in_half(name, place, hb, dproj, mine, job=None):
    tt = min(DW_TOKENS, hb.shape[0])

    def comp(k, pc):
        return _component_of(2 * k + (pc[1] if mine else 1 - pc[1]))

    return _dw_call(
        name, hb, dproj,
        pl.BlockSpec((tt, D), lambda k, t, pc: (t, 0)),
        pl.BlockSpec((None, tt, D), lambda k, t, pc: (comp(k, pc), t, 0)),
        pl.BlockSpec((None, D, D), lambda k, t, pc: (k, 0, 0)),
        SDS((NCHIP, D, D), F32), NCHIP, tt, job, place)


def _dw_gate_up(h2b, dgu, job=None):
    tt = min(DW_TOKENS, h2b.shape[0])
    return _dw_call(
        "dw_gate_up", h2b, dgu,
        pl.BlockSpec((tt, D), lambda k, t: (t, 0)),
        pl.BlockSpec((None, tt, FFS), lambda k, t: (k // 2, t, k % 2)),
        pl.BlockSpec((None, D, FFS), lambda k, t: (k, 0, 0)),
        SDS((NCHIP, D, FFS), F32), NCHIP, tt, job)


def _dw_down(act, dx2b, job=None):
    tt = min(DW_TOKENS, act.shape[0])
    g, job_out = _dw_call(
        "dw_down", act, dx2b,
        pl.BlockSpec((tt, FFS), lambda k, t: (t, k)),
        pl.BlockSpec((tt, D), lambda k, t: (t, 0)),
        pl.BlockSpec((FFS, D), lambda k, t: (k, 0)),
        SDS((FF, D), F32), 2, tt, job)
    return g.reshape(NCHIP, FF // NCHIP, D), job_out


def _dw_square(name, a, b, job=None):
    tt = min(DW_TOKENS, a.shape[0])
    g, job_out = _dw_call(
        name, a, b,
        pl.BlockSpec((tt, D), lambda k, t: (t, 0)), pl.BlockSpec((tt, D), lambda k, t: (t, 0)),
        pl.BlockSpec((D, D), lambda k, t: (0, 0)), SDS((D, D), F32), 1, tt, job)
    return g.reshape(NCHIP, D // NCHIP, D), job_out


def _place():
    x, y, c = lax.axis_index("x"), lax.axis_index("y"), lax.axis_index("c")
    return x, y, c, 2 * x + y


def _chip_at(x, y, s):
    return x ^ (s >> 1), y ^ (s & 1)


class _Job:
    def __init__(self, ins, out_shapes, sems, start, finish, aliases=None, mid=None):
        self.ins, self.out_shapes, self.sems = list(ins), list(out_shapes), list(sems)
        self.start, self.finish, self.aliases = start, finish, dict(aliases or {})
        self.mid = mid if mid is not None else (lambda ins, outs, sems: None)


def _join_jobs(*jobs):
    def cut(refs, sizes):
        out, at = [], 0
        for n in sizes:
            out.append(refs[at:at + n])
            at += n
        return out

    ni = [len(j.ins) for j in jobs]
    no = [len(j.out_shapes) for j in jobs]
    ns = [len(j.sems) for j in jobs]

    def run(which):
        def go(ins, outs, sems):
            for j, a, b, c in zip(jobs, cut(ins, ni), cut(outs, no), cut(sems, ns)):
                getattr(j, which)(a, b, c)
        return go

    aliases = {}
    for k, j in enumerate(jobs):
        for a, b in j.aliases.items():
            aliases[sum(ni[:k]) + a] = sum(no[:k]) + b
    return _Job([a for j in jobs for a in j.ins], [o for j in jobs for o in j.out_shapes],
                [s for j in jobs for s in j.sems], run("start"), run("finish"), aliases, run("mid"))


def _call(body, *, name, grid, in_specs, out_specs, out_shape, args, scratch_shapes=(), aliases=None,
          job=None, prefetch=None):
    n_in, n_out, n_scr = len(in_specs), len(out_specs), len(scratch_shapes)
    npf = 0 if prefetch is None else 1
    job = job if job is not None else _Job([], [], [], lambda *a: None, lambda *a: None)
    ji, jo = len(job.ins), len(job.out_shapes)
    steps = math.prod(grid)

    def wrapped(*refs):
        pf, refs = refs[:npf], refs[npf:]
        ins, jin = refs[:n_in], refs[n_in:n_in + ji]
        o0 = n_in + ji
        outs, jout = refs[o0:o0 + n_out], refs[o0 + n_out:o0 + n_out + jo]
        s0 = o0 + n_out + jo
        scr, jsem = refs[s0:s0 + n_scr], refs[s0 + n_scr:]
        step = functools.reduce(lambda acc, ag: acc * ag[1] + pl.program_id(ag[0]), enumerate(grid), 0)
        if ji or jo:
            @pl.when(step == 0)
            def _():
                job.start(jin, jout, jsem)

        body(*pf, *ins, *outs, *scr)

        if ji or jo:
            @pl.when(step == steps // 2)
            def _():
                job.mid(jin, jout, jsem)

            @pl.when(step == steps - 1)
            def _():
                job.finish(jin, jout, jsem)

    io = {npf + a: b for a, b in dict(aliases or {}).items()}
    io.update({npf + n_in + a: n_out + b for a, b in job.aliases.items()})
    kw = dict(in_specs=list(in_specs) + [ANY] * ji, out_specs=list(out_specs) + [ANY] * jo,
              scratch_shapes=list(scratch_shapes) + job.sems)
    if npf:
        kw = dict(grid_spec=pltpu.PrefetchScalarGridSpec(num_scalar_prefetch=1, grid=grid, **kw))
    else:
        kw["grid"] = grid
    res = pl.pallas_call(
        wrapped, name=name, out_shape=list(out_shape) + job.out_shapes, input_output_aliases=io,
        compiler_params=_cparams(has_side_effects=bool(ji or jo)), **kw,
    )(*(() if prefetch is None else (prefetch,)), *args, *job.ins)
    return list(res[:n_out]), list(res[n_out:])


def _cast_shards(name, place, ws, paired=False):
    n = len(ws)
    rows, cols = ws[0].shape
    tr = 352 if rows % 352 == 0 else 256
    shape = (2, rows, 2 * cols) if paired else (NCHIP, rows, cols)
    mine = (lambda i, pc: (pc[0] // 2, i, pc[0] % 2)) if paired else (lambda i, pc: (pc[0], i, 0))

    def body(pc_ref, *refs):
        del pc_ref
        for w_ref, o_ref in zip(refs[:n], refs[n:]):
            o_ref[...] = w_ref[...].astype(BF16)

    return pl.pallas_call(
        body, name=name,
        grid_spec=pltpu.PrefetchScalarGridSpec(
            num_scalar_prefetch=1, grid=(rows // tr,),
            in_specs=[pl.BlockSpec((tr, cols), lambda i, pc: (i, 0))] * n,
            out_specs=[pl.BlockSpec((None, tr, cols), mine)] * n),
        out_shape=[SDS(shape, BF16)] * n,
        compiler_params=_cparams(),
    )(place, *ws)


def _sibling_copy(ref, send_sem, recv_sem):
    x, y, c, _ = _place()
    return pltpu.make_async_remote_copy(src_ref=ref, dst_ref=ref, send_sem=send_sem, recv_sem=recv_sem,
                                        device_id=(x, y, 1 - c), device_id_type=MESH)


def _slot(arr, chip):
    if arr.shape[0] == NCHIP:
        return arr.at[chip]
    cols = arr.shape[2] // 2
    return arr.at[chip // 2, :, pl.ds(pl.multiple_of((chip % 2) * cols, 128), cols)]


def _half_rows(arr, slot, core):
    half = arr.shape[1] // 2
    return _slot(arr, slot).at[pl.ds(pl.multiple_of(core * half, 16), half)]


def _quarter_rows(arr, slot, core, q):
    quarter = arr.shape[1] // 4
    return _slot(arr, slot).at[pl.ds(pl.multiple_of((2 * core + q) * quarter, 16), quarter)]


def _chip_copy(ref, dist, send_sem, recv_sem):
    x, y, c, _ = _place()
    cx, cy = _chip_at(x, y, dist)
    return pltpu.make_async_remote_copy(src_ref=ref, dst_ref=ref, send_sem=send_sem, recv_sem=recv_sem,
                                        device_id=(cx, cy, c), device_id_type=MESH)


def _gather_sems(n):
    dma = pltpu.SemaphoreType.DMA
    return [dma((n, 2))] * 4 + [dma((n, 4))] * 2


def _gather_start(arrs, sems):
    dsend, drecv = sems[0], sems[1]
    _, _, c, j = _place()
    for w, arr in enumerate(arrs):
        for dist in (1, 2):
            _chip_copy(_half_rows(arr, j, c), dist, dsend.at[w, dist - 1], drecv.at[w, dist - 1]).start()


def _gather_land(arrs, sems, dist, first=0):
    dsend, drecv, rsend, rrecv, fsend, frecv = sems
    _, _, c, j = _place()
    if dist < 3:
        other = 3 - dist
        for w, arr in enumerate(arrs, first):
            landed = _half_rows(arr, j ^ dist, c)
            _chip_copy(landed, dist, dsend.at[w, dist - 1], drecv.at[w, dist - 1]).wait_recv()
            relay = _quarter_rows(arr, j ^ dist, c, other - 1)
            _chip_copy(relay, other, rsend.at[w, other - 1], rrecv.at[w, other - 1]).start()
            _sibling_copy(landed, fsend.at[w, dist - 1], frecv.at[w, dist - 1]).start()
        for w, arr in enumerate(arrs, first):
            theirs = _half_rows(arr, j ^ dist, 1 - c)
            _sibling_copy(theirs, fsend.at[w, dist - 1], frecv.at[w, dist - 1]).wait_recv()
    else:
        for w, arr in enumerate(arrs, first):
            for via in (1, 2):
                piece = _quarter_rows(arr, j ^ 3, c, via - 1)
                _chip_copy(piece, via, rsend.at[w, via - 1], rrecv.at[w, via - 1]).wait_recv()
                _sibling_copy(piece, fsend.at[w, 1 + via], frecv.at[w, 1 + via]).start()
        for w, arr in enumerate(arrs, first):
            for via in (1, 2):
                theirs = _quarter_rows(arr, j ^ 3, 1 - c, via - 1)
                _sibling_copy(theirs, fsend.at[w, 1 + via], frecv.at[w, 1 + via]).wait_recv()


def _gather_drain(arrs, sems):
    dsend, drecv, rsend, rrecv, fsend, frecv = sems
    _, _, c, j = _place()
    for w, arr in enumerate(arrs):
        for dist in (1, 2):
            other = 3 - dist
            _chip_copy(_half_rows(arr, j, c), dist, dsend.at[w, dist - 1], drecv.at[w, dist - 1]).wait_send()
            _chip_copy(_quarter_rows(arr, j ^ dist, c, other - 1), other,
                       rsend.at[w, other - 1], rrecv.at[w, other - 1]).wait_send()
            _sibling_copy(_half_rows(arr, j ^ dist, c), fsend.at[w, dist - 1], frecv.at[w, dist - 1]).wait_send()
            _sibling_copy(_quarter_rows(arr, j ^ 3, c, dist - 1),
                          fsend.at[w, 1 + dist], frecv.at[w, 1 + dist]).wait_send()


def _gather_neighbours(arrs, sems):
    _gather_land(arrs, sems, 1)
    _gather_land(arrs, sems, 2)


def _gather_finish(arrs, sems):
    _gather_land(arrs, sems, 3)
    _gather_drain(arrs, sems)


def _gather_job(arrs):
    n = len(arrs)
    return _Job(arrs, [SDS(a.shape, a.dtype) for a in arrs], _gather_sems(n),
                lambda ins, outs, sems: _gather_start(outs, sems),
                lambda ins, outs, sems: _gather_finish(outs, sems), {k: k for k in range(n)},
                mid=lambda ins, outs, sems: _gather_neighbours(outs, sems))


def _exchange_job(arrs, out_shapes, n, copies):
    def start(ins, outs, sems):
        for cp in copies(ins, outs, sems[0], sems[1]):
            cp.start()

    def finish(ins, outs, sems):
        for cp in copies(ins, outs, sems[0], sems[1]):
            cp.wait()

    return _Job(arrs, out_shapes, [pltpu.SemaphoreType.DMA((n,))] * 2, start, finish)


def _pair_exchange_job(grads):
    def copies(ins, outs, send_sem, recv_sem):
        x, y, c, _ = _place()
        res = []
        for w in range(len(grads)):
            half = ins[w].shape[1] // 2
            theirs = pl.ds(pl.multiple_of((1 - c) * half, 8), half)
            res.append(pltpu.make_async_remote_copy(
                src_ref=ins[w].at[:, theirs, :], dst_ref=outs[w], send_sem=send_sem.at[w],
                recv_sem=recv_sem.at[w], device_id=(x, y, 1 - c), device_id_type=MESH))
        return res

    return _exchange_job(grads, [SDS((NCHIP, g.shape[1] // 2, g.shape[2]), F32) for g in grads],
                         len(grads), copies)


def _row_tile(rows, cols):
    tr = rows
    while tr * cols * 4 > ELEMENTWISE_BLOCK_BYTES and tr % 32 == 0:
        tr //= 2
    return tr


def _pair_sums(name, place, gs, sibs):
    n = len(gs)
    half, cols = sibs[0].shape[1], sibs[0].shape[2]
    tr = _row_tile(half, cols)
    nt = half // tr
    mine = nt if gs[0].shape[1] == 2 * half else 0

    def body(pc_ref, *refs):
        del pc_ref
        for g_ref, s_ref, own_ref, out_ref in zip(refs[:n], refs[n:2 * n], refs[2 * n:3 * n], refs[3 * n:]):
            v = g_ref[...] + s_ref[...]

            @pl.when(pl.program_id(1) == 0)
            def _():
                own_ref[...] = v

            @pl.when(pl.program_id(1) > 0)
            def _():
                out_ref[...] = v.astype(BF16)

    res = pl.pallas_call(
        body, name=name,
        grid_spec=pltpu.PrefetchScalarGridSpec(
            num_scalar_prefetch=1, grid=(nt, NCHIP),
            in_specs=[pl.BlockSpec((None, tr, cols), lambda i, s, pc: (pc[0] ^ s, pc[1] * mine + i, 0))] * n
            + [pl.BlockSpec((None, tr, cols), lambda i, s, pc: (pc[0] ^ s, i, 0))] * n,
            out_specs=[pl.BlockSpec((tr, cols), lambda i, s, pc: (i, 0))] * n
            + [pl.BlockSpec((None, tr, cols), lambda i, s, pc: (jnp.maximum(s - 1, 0), i, 0))] * n),
        out_shape=[SDS((half, cols), F32)] * n + [SDS((NCHIP - 1, half, cols), BF16)] * n,
        compiler_params=_cparams(),
    )(place, *gs, *sibs)
    return res[:n], res[n:]


def _chip_exchange_job(parts):
    def copies(ins, outs, send_sem, recv_sem):
        x, y, c, _ = _place()
        res = []
        for w in range(len(parts)):
            for s in range(1, NCHIP):
                cx, cy = _chip_at(x, y, s)
                k = w * (NCHIP - 1) + s - 1
                res.append(pltpu.make_async_remote_copy(
                    src_ref=ins[w].at[s - 1], dst_ref=outs[w].at[s - 1], send_sem=send_sem.at[k],
                    recv_sem=recv_sem.at[k], device_id=(cx, cy, c), device_id_type=MESH))
        return res

    return _exchange_job(parts, [SDS((NCHIP - 1,) + p.shape[1:], BF16) for p in parts],
                         len(parts) * (NCHIP - 1), copies)


def _chip_sums(name, owns, rems):
    n = len(owns)
    half, cols = owns[0].shape
    tr = _row_tile(half, cols)

    def body(*refs):
        for own_ref, rem_ref, out_ref in zip(refs[:n], refs[n:2 * n], refs[2 * n:]):
            out_ref[...] = (((own_ref[...] + rem_ref[0].astype(F32)) + rem_ref[1].astype(F32))
                            + rem_ref[2].astype(F32))

    return pl.pallas_call(
        body, name=name, grid=(half // tr,),
        in_specs=[pl.BlockSpec((tr, cols), lambda i: (i, 0))] * n
        + [pl.BlockSpec((NCHIP - 1, tr, cols), lambda i: (0, i, 0))] * n,
        out_specs=[pl.BlockSpec((tr, cols), lambda i: (i, 0))] * n,
        out_shape=[SDS((half, cols), F32)] * n,
        compiler_params=_cparams(),
    )(*owns, *rems)


def _share_halves_job(halves):
    def copies(ins, outs, send_sem, recv_sem):
        x, y, c, _ = _place()
        return [pltpu.make_async_remote_copy(
            src_ref=ins[w], dst_ref=outs[w], send_sem=send_sem.at[w], recv_sem=recv_sem.at[w],
            device_id=(x, y, 1 - c), device_id_type=MESH) for w in range(len(halves))]

    return _exchange_job(halves, [SDS(h.shape, F32) for h in halves], len(halves), copies)


def _adamw_math(w, g, m, v):
    m = B1 * m + (1.0 - B1) * g
    v = B2 * v + (1.0 - B2) * (g * g)
    m_hat = m / (1.0 - B1 ** STEP)
    v_hat = v / (1.0 - B2 ** STEP)
    delta = -LR * (m_hat / (jnp.sqrt(v_hat) + AEPS) + WD * w)
    return delta, m, v


def _adamws(name, place, ws, owns, sibs, ms, vs):
    n = len(ws)
    rows, cols = ws[0].shape
    by_cols = owns[0].shape[0] == rows
    half, pc_cols = (rows, cols // 2) if by_cols else (rows // 2, cols)
    tr = _row_tile(half, pc_cols)
    nt = half // tr

    def body(pc_ref, *refs):
        ins, outs = refs[:5 * n], refs[5 * n:]
        for k in range(n):
            w_ref, own_ref, sib_ref, m_ref, v_ref = ins[5 * k:5 * k + 5]
            g = jnp.where(pl.program_id(0) == pc_ref[1], own_ref[...], sib_ref[...])
            d, mn, vn = _adamw_math(w_ref[...], g, m_ref[...], v_ref[...])
            for ref, val in zip(outs[4 * k:4 * k + 4], (g, d, mn, vn)):
                ref[...] = val

    full = pl.BlockSpec((tr, pc_cols), (lambda h, i, pc: (i, h)) if by_cols else (lambda h, i, pc: (h * nt + i, 0)))
    part = pl.BlockSpec((tr, pc_cols), lambda h, i, pc: (i, 0))
    res = pl.pallas_call(
        body, name=name,
        grid_spec=pltpu.PrefetchScalarGridSpec(
            num_scalar_prefetch=1, grid=(2, nt),
            in_specs=[full, part, part, full, full] * n, out_specs=[full] * (4 * n)),
        out_shape=[SDS((rows, cols), F32)] * (4 * n),
        compiler_params=_cparams(),
    )(place, *[a for group in zip(ws, owns, sibs, ms, vs) for a in group])
    return [tuple(res[4 * k:4 * k + 4]) for k in range(n)]


def _small_allreduce_adamw(sp, wmv, job):
    shape = sp.shape
    ji, jo = len(job.ins), len(job.out_shapes)

    def body(sp_ref, wmv_ref, *rest):
        jin, (g_ref, d_ref, mo_ref, vo_ref), jout = rest[:ji], rest[ji:ji + 4], rest[ji + 4:ji + 4 + jo]
        sib_s, pair_s, chip_s, send_sem, recv_sem = rest[ji + 4 + jo:ji + 9 + jo]
        jsem = rest[ji + 9 + jo:]
        job.start(jin, jout, jsem)
        x, y, c, j = _place()
        cp = pltpu.make_async_remote_copy(
            src_ref=sp_ref, dst_ref=sib_s, send_sem=send_sem.at[0], recv_sem=recv_sem.at[0],
            device_id=(x, y, 1 - c), device_id_type=MESH)
        cp.start()
        cp.wait()
        pair_s[...] = sp_ref[...] + sib_s[...]
        half = shape[0] // 2
        mine = pl.ds(pl.multiple_of(c * half, 8), half)
        cps = []
        for s in range(1, NCHIP):
            cx, cy = _chip_at(x, y, s)
            cp = pltpu.make_async_remote_copy(
                src_ref=pair_s.at[mine], dst_ref=chip_s.at[s, mine], send_sem=send_sem.at[s],
                recv_sem=recv_sem.at[s], device_id=(cx, cy, c), device_id_type=MESH)
            cp.start()
            cps.append(cp)
        chip_s[0] = pair_s[...]
        for cp in cps:
            cp.wait()
        cps = []
        for s in range(1, NCHIP):
            cp = pltpu.make_async_remote_copy(
                src_ref=chip_s.at[s, mine], dst_ref=chip_s.at[s, mine], send_sem=send_sem.at[NCHIP + s],
                recv_sem=recv_sem.at[NCHIP + s], device_id=(x, y, 1 - c), device_id_type=MESH)
            cp.start()
            cps.append(cp)
        for cp in cps:
            cp.wait()
        tot = chip_s[j]
        for k in range(1, NCHIP):
            tot = tot + chip_s[k ^ j]
        g_ref[...] = tot
        d, mn, vn = _adamw_math(wmv_ref[0], tot, wmv_ref[1], wmv_ref[2])
        d_ref[...] = d
        mo_ref[...] = mn
        vo_ref[...] = vn
        job.mid(jin, jout, jsem)
        job.finish(jin, jout, jsem)

    vm = pl.BlockSpec(memory_space=pltpu.VMEM)
    res = pl.pallas_call(
        body, name="small_allreduce_adamw",
        in_specs=[vm] * 2 + [ANY] * ji, out_specs=[vm] * 4 + [ANY] * jo,
        out_shape=[SDS(shape, F32)] * 4 + job.out_shapes,
        scratch_shapes=[pltpu.VMEM(shape, F32), pltpu.VMEM(shape, F32), pltpu.VMEM((NCHIP,) + shape, F32),
                        pltpu.SemaphoreType.DMA((2 * NCHIP,)), pltpu.SemaphoreType.DMA((2 * NCHIP,))] + job.sems,
        input_output_aliases={2 + a: 4 + b for a, b in job.aliases.items()},
        compiler_params=pltpu.CompilerParams(has_side_effects=True),
    )(sp, wmv, *job.ins)
    return res[:4], res[4:]


def _pack_small(first, mix, ln_g, ln_b, b_s, lbt, hn, ffn, fin, w_s):
    rows = [first.reshape(1, D), mix.reshape(1, D), ln_g.reshape(1, D), ln_b.reshape(1, D),
            b_s.reshape(1, D), lbt.reshape(2, D), hn.reshape(1, D), ffn.reshape(1, D), fin.reshape(1, D),
            jnp.zeros((6, D), F32)]
    return jnp.concatenate(rows + [w_s.reshape(NG, GCH, GCH).transpose(1, 0, 2).reshape(GCH, D)], axis=0)


def _unpack_small(p):
    w_s = p[16:].reshape(GCH, NG, GCH).transpose(1, 0, 2).reshape(1, NG, GCH, GCH)
    return dict(norm_mix_g=p[1:2], gmlp_ln_g=p[2:3], gmlp_ln_b=p[3:4], gmlp_b_s=p[4].reshape(1, NG, GCH),
                hgrn_lb_table=p[5:7], hgrn_norm_g=p[7:8], norm_ffn_g=p[8:9], norm_final_g=p[9],
                gmlp_w_s=w_s)


SMALL = ("norm_mix_g", "gmlp_ln_g", "gmlp_ln_b", "gmlp_w_s", "gmlp_b_s", "hgrn_lb_table", "hgrn_norm_g",
         "norm_ffn_g", "norm_final_g")
BIG = ("w_in", "w_gate_up", "w_branch_a", "w_branch_b", "w_out", "w_down")
ORDER = ("norm_mix_g", "w_in", "gmlp_ln_g", "gmlp_ln_b", "gmlp_w_s", "gmlp_b_s", "hgrn_lb_table",
         "hgrn_norm_g", "w_branch_a", "w_branch_b", "w_out", "norm_ffn_g", "w_gate_up", "w_down",
         "norm_final_g")


def kernel(x, norm_mix_g, w_in, gmlp_ln_g, gmlp_ln_b, gmlp_w_s, gmlp_b_s, hgrn_lb_table, hgrn_norm_g, w_branch_a, w_branch_b, w_out, norm_ffn_g, w_gate_up, w_down, norm_final_g, loss_target, m_norm_mix_g, m_w_in, m_gmlp_ln_g, m_gmlp_ln_b, m_gmlp_w_s, m_gmlp_b_s, m_hgrn_lb_table, m_hgrn_norm_g, m_w_branch_a, m_w_branch_b, m_w_out, m_norm_ffn_g, m_w_gate_up, m_w_down, m_norm_final_g, v_norm_mix_g, v_w_in, v_gmlp_ln_g, v_gmlp_ln_b, v_gmlp_w_s, v_gmlp_b_s, v_hgrn_lb_table, v_hgrn_norm_g, v_w_branch_a, v_w_branch_b, v_w_out, v_norm_ffn_g, v_w_gate_up, v_w_down, v_norm_final_g):
    args = dict(locals())
    T = x.shape[1]
    xs = x.reshape(T, D)
    target = loss_target.reshape(T, D)
    big = {n: args[n].reshape(args[n].shape[1:]) for n in BIG}
    big_m = {n: args["m_" + n].reshape(args[n].shape[1:]) for n in BIG}
    big_v = {n: args["v_" + n].reshape(args[n].shape[1:]) for n in BIG}

    x_i, y_i, c_i = lax.axis_index("x"), lax.axis_index("y"), lax.axis_index("c")
    place = jnp.stack([2 * x_i + y_i, c_i]).astype(jnp.int32)
    def by_shape(names):
        groups = []
        for n in names:
            if groups and big[groups[-1][0]].shape == big[n].shape:
                groups[-1].append(n)
            else:
                groups.append([n])
        return groups

    cast = {}
    for grp in by_shape(BIG):
        cast.update(zip(grp, _cast_shards("cast_" + grp[0], place, [big[n] for n in grp],
                                          paired=grp[0] == "w_gate_up")))
    tril = jnp.tril(jnp.ones((GCH, GCH), bool))
    wm = jnp.where(tril, gmlp_w_s[0], 0.0).astype(BF16)
    wm_t = jnp.swapaxes(wm, 1, 2)
    b_t = gmlp_b_s[0].T

    (proj, hb), w_in4, (w_a4, w_b4, w_out4) = _proj_fwd(
        place, xs, norm_mix_g, cast["w_in"], [cast[n] for n in ("w_branch_a", "w_branch_b", "w_out")])
    (ab,), (w_down4,) = _gmlp_fwd(proj, gmlp_ln_g, gmlp_ln_b, wm, b_t, job=_gather_job([cast["w_down"]]))
    (o_raw, obb, st_before), (w_gu,) = _hgrn_fwd(
        proj, hgrn_lb_table, hgrn_norm_g, job=_gather_job([cast["w_gate_up"]]))
    w_a, w_b, w_o = (w.reshape(D, D) for w in (w_a4, w_b4, w_out4))
    (mgb, x1), _ = _merge_fwd(xs, ab, obb, proj, w_a, w_b, w_o)
    w_dn = w_down4.reshape(FF, D)
    act, dx2b, h2b, dgu, dx1, dx1b, acc_ffn = _ffn_fwd_bwd(
        x1, target, norm_ffn_g, norm_final_g.reshape(1, D), w_gu, w_dn)

    grads, owns, parts, halves, sibh = {}, {}, {}, {}, {}

    def pair_sums(names, sibs):
        sib_of = dict(zip(names, sibs))
        for grp in by_shape(names):
            o, p = _pair_sums("rs_pair_sum_" + grp[0], place, [grads[n] for n in grp], [sib_of[n] for n in grp])
            owns.update(zip(grp, o))
            parts.update(zip(grp, p))

    def chip_sums(names, got):
        rem_of = dict(zip(names, got))
        for grp in by_shape(names):
            h = _chip_sums("rs_chip_sum_" + grp[0], [owns[n] for n in grp], [rem_of[n] for n in grp])
            halves.update(zip(grp, h))

    ffn, mix = ("w_gate_up", "w_down"), ("w_branch_a", "w_branch_b", "w_out")
    grads["w_gate_up"], _ = _dw_gate_up(h2b, dgu)
    grads["w_down"], _ = _dw_down(act, dx2b)
    (dya, dyb, dproj), got = _merge_bwd(
        dx1b, ab, obb, proj, w_o, w_a, w_b, job=_pair_exchange_job([grads[n] for n in ffn]))
    pair_sums(ffn, got)
    grads["w_branch_a"], _ = _dw_square("dw_branch_a", ab, dya)
    grads["w_branch_b"], _ = _dw_square("dw_branch_b", obb, dyb)
    grads["w_out"], _ = _dw_square("dw_out", mgb, dx1b)
    (dproj, acc_hgrn), got = _hgrn_bwd(
        dproj, dyb, w_b, o_raw, proj, st_before, hgrn_lb_table, hgrn_norm_g,
        job=_join_jobs(_chip_exchange_job([parts[n] for n in ffn]), _pair_exchange_job([grads[n] for n in mix])))
    chip_sums(ffn, got[:2])
    pair_sums(mix, got[2:])
    dproj, acc_ln, dws, dmix = _gmlp_bwd(dproj, dya, w_a, proj, gmlp_ln_g, gmlp_ln_b, wm, wm_t, b_t)
    for_sibling, got = _dw_in_half(
        "dw_in_sibling_half", place, hb, dproj, False,
        job=_join_jobs(_share_halves_job([halves[n] for n in ffn]), _chip_exchange_job([parts[n] for n in mix])))
    sibh.update(zip(ffn, got[:2]))
    chip_sums(mix, got[2:])
    grads["w_in"], got = _dw_in_half(
        "dw_in_own_half", place, hb, dproj, True, job=_share_halves_job([for_sibling]))
    pair_sums(("w_in",), got)
    (grad_x, acc_mix), got = _proj_bwd(
        dproj, w_in4, xs, dx1, norm_mix_g,
        job=_join_jobs(_chip_exchange_job([parts["w_in"]]), _share_halves_job([halves[n] for n in mix])))
    chip_sums(("w_in",), got[:1])
    sibh.update(zip(mix, got[1:]))

    lbv = jax.nn.sigmoid(hgrn_lb_table[0] - hgrn_lb_table[1])
    d_t0 = jnp.sum(acc_hgrn[0], axis=0) * lbv * (1.0 - lbv)
    loss_row = jnp.zeros((D,), F32).at[0].set(jnp.sum(acc_ffn[0]))
    dws_m = jnp.where(tril[:, None, :], dws.reshape(GCH, NG, GCH), 0.0).transpose(1, 0, 2)
    db_s = jnp.sum(dmix.reshape(GCH, NG, GCH), axis=-1).T
    sp = _pack_small(loss_row, jnp.sum(acc_mix, 0), jnp.sum(acc_ln[0], 0), jnp.sum(acc_ln[1], 0), db_s,
                     jnp.stack([d_t0, -d_t0]), jnp.sum(acc_hgrn[1], 0), jnp.sum(acc_ffn[2], 0),
                     jnp.sum(acc_ffn[1], 0), dws_m)
    zero = jnp.zeros((D,), F32)

    def pack(prefix):
        a = lambda n: args[prefix + n]
        return _pack_small(zero, a("norm_mix_g"), a("gmlp_ln_g"), a("gmlp_ln_b"), a("gmlp_b_s"),
                           a("hgrn_lb_table"), a("hgrn_norm_g"), a("norm_ffn_g"), a("norm_final_g"),
                           a("gmlp_w_s"))

    packed, (sibh["w_in"],) = _small_allreduce_adamw(
        sp, jnp.stack([pack(""), pack("m_"), pack("v_")]), _share_halves_job([halves["w_in"]]))
    loss = packed[0][0, 0]
    small = [_unpack_small(p) for p in packed]
    out = {n: tuple(s[n] for s in small) for n in SMALL}
    for grp in by_shape(BIG):
        res = _adamws("adamw_" + grp[0], place, *[[d[n] for n in grp] for d in (big, halves, sibh, big_m, big_v)])
        for n, quad in zip(grp, res):
            out[n] = tuple(a.reshape(args[n].shape) for a in quad)
    return (loss, grad_x.reshape(x.shape), *[out[n][0] for n in ORDER], *[out[n][1] for n in ORDER],
            *[out[n][2] for n in ORDER], *[out[n][3] for n in ORDER])
```

```python
import functools
import math

import jax
import jax.numpy as jnp
from jax import lax
from jax.experimental import pallas as pl
from jax.experimental.pallas import tpu as pltpu

F32 = jnp.float32
BF16 = jnp.bfloat16
SDS = jax.ShapeDtypeStruct
MESH = pl.DeviceIdType.MESH
ANY = pl.BlockSpec(memory_space=pl.ANY)

D = 1024
NIN = 8
NG = 8
GCH = 128
NH = 8
HD = 128
HCH = 64
HGRN_HB = 8
HGRN_TOKENS = 256
GMLP_FWD_TOKENS = 512
GMLP_BWD_TOKENS = 256
HW = HGRN_HB * HD
DW_TOKENS = 2048
DW_IN_TOKENS = 4096
ELEMENTWISE_BLOCK_BYTES = 2 * 1024 * 1024
PROJ_OUT_SLOTS = 4
FF = 2816
FFS = 1408
NCHIP = 4
EPS = 1e-6
QSCALE = HD ** -0.5
GELU_C0 = math.sqrt(2.0 / math.pi)
GELU_C1 = 0.044715
LR, B1, B2, AEPS, WD, STEP = 0.001, 0.9, 0.999, 1e-08, 0.01, 10
VMEM_LIMIT_V7X = 56 * 1024 * 1024
SP_ROWS = 144


def _cparams(**kw):
    return pltpu.CompilerParams(vmem_limit_bytes=VMEM_LIMIT_V7X, **kw)


def _mm(a, b):
    return jnp.dot(a, b, preferred_element_type=F32)


def _mm_nt(a, b):
    return lax.dot_general(a, b, (((1,), (1,)), ((), ())), preferred_element_type=F32)


def _mm_tn(a, b):
    return lax.dot_general(a, b, (((0,), (0,)), ((), ())), preferred_element_type=F32)


def _rows8(x):
    r, c = x.shape
    return jnp.sum(x.reshape(r // 8, 8, c), axis=0)


def _mean(x):
    return jnp.mean(x, axis=-1, keepdims=True)


def _sigmoid(x):
    return 1.0 / (1.0 + jnp.exp(-x))


def _gelu(x):
    t = jnp.tanh(GELU_C0 * (x + GELU_C1 * x * x * x))
    return 0.5 * x * (1.0 + t), t


def _gelu_grad(x, t):
    return 0.5 * (1.0 + t) + 0.5 * x * (1.0 - t * t) * (GELU_C0 * (1.0 + 3.0 * GELU_C1 * x * x))


def _component_of(group):
    return jnp.where(group < 6, (group + 4) % 6, group)


def _proj_fwd(place, x, g_mix, w_in4, later):
    T = x.shape[0]
    tm = min(1024, T)
    ni = T // tm
    n = len(later)

    def body(pc_ref, x_ref, g_ref, *rest):
        proj_ref, h_ref, w_all = rest[1 + n:4 + n]
        gathered = rest[4 + n:4 + 2 * n]
        hs, wbuf, wsem, obuf, osem = rest[4 + 2 * n:9 + 2 * n]
        w_sems, later_sems = rest[9 + 2 * n:15 + 2 * n], rest[15 + 2 * n:]
        jp, i = pl.program_id(0), pl.program_id(1)
        w_cols = [w_all.at[:, :, pl.ds(k * D, D)] for k in range(2)]

        def w_copy(blk):
            cols = pl.ds(pl.multiple_of((blk % 2) * D, 128), D)
            return pltpu.make_async_copy(w_all.at[pc_ref[0] ^ (blk // 2), :, cols], wbuf.at[blk % 2],
                                         wsem.at[blk % 2])

        @pl.when((jp == 0) & (i == 0))
        def _():
            _gather_start(w_cols, w_sems)
            w_copy(jp).start()

        @pl.when(i == 0)
        def _():
            w_copy(jp).wait()

        @pl.when(jp == 0)
        def _():
            xv = x_ref[...]
            r = lax.rsqrt(_mean(xv * xv) + EPS)
            hb = (xv * r * g_ref[...]).astype(BF16)
            hs[i] = hb
            h_ref[...] = hb

        step = jp * ni + i
        slot = step % PROJ_OUT_SLOTS

        def o_copy(slot_):
            comp = 2 * (pc_ref[0] ^ (jp // 2)) + jp % 2
            return pltpu.make_async_copy(
                obuf.at[slot_], proj_ref.at[comp, pl.ds(pl.multiple_of(i * tm, 8), tm)], osem.at[slot_])

        @pl.when(step >= PROJ_OUT_SLOTS)
        def _():
            o_copy(slot).wait()

        obuf[slot] = _mm(hs[i], wbuf[jp % 2])
        o_copy(slot).start()

        @pl.when(step == NIN * ni - 1)
        def _():
            for k in range(PROJ_OUT_SLOTS):
                o_copy((slot + 1 + k) % PROJ_OUT_SLOTS).wait()

        for nxt in range(1, NIN):
            @pl.when((jp == nxt - 1) & (i == ni - 1))
            def _():
                if nxt >= 2:
                    _gather_land([w_cols[nxt % 2]], w_sems, nxt // 2, first=nxt % 2)
                if nxt == 5:
                    _gather_start(gathered, later_sems)
                if nxt == NIN - 1:
                    _gather_neighbours(gathered, later_sems)
                w_copy(jp + 1).start()

        @pl.when((jp == NIN - 1) & (i == ni - 1))
        def _():
            _gather_drain(w_cols, w_sems)
            _gather_finish(gathered, later_sems)

    tile = lambda jp, i, pc: (jnp.where(jp == 0, i, ni - 1), 0)
    res = pl.pallas_call(
        body, name="proj_fwd",
        grid_spec=pltpu.PrefetchScalarGridSpec(
            num_scalar_prefetch=1, grid=(NIN, ni),
            in_specs=[pl.BlockSpec((tm, D), tile), pl.BlockSpec((1, D), lambda jp, i, pc: (0, 0))] + [ANY] * (1 + n),
            out_specs=[ANY, pl.BlockSpec((tm, D), tile)] + [ANY] * (1 + n),
            scratch_shapes=[pltpu.VMEM((ni, tm, D), BF16), pltpu.VMEM((2, D, D), BF16),
                            pltpu.SemaphoreType.DMA((2,)), pltpu.VMEM((PROJ_OUT_SLOTS, tm, D), F32),
                            pltpu.SemaphoreType.DMA((PROJ_OUT_SLOTS,))] + _gather_sems(2) + _gather_sems(n)),
        out_shape=[SDS((NIN, T, D), F32), SDS((T, D), BF16), SDS(w_in4.shape, BF16)]
        + [SDS(a.shape, a.dtype) for a in later],
        input_output_aliases={3 + k: 2 + k for k in range(1 + n)},
        compiler_params=_cparams(has_side_effects=True),
    )(place, x, g_mix, w_in4, *later)
    return res[:2], res[2], res[3:]


def _chunks_abreast(x):
    return jnp.concatenate([x[GCH * ch:GCH * (ch + 1)] for ch in range(x.shape[0] // GCH)], axis=1)


def _chunks_stacked(x):
    return jnp.concatenate([x[:, GCH * ch:GCH * (ch + 1)] for ch in range(x.shape[1] // GCH)], axis=0)


def _layer_norm_stats(gv):
    mu = _mean(gv)
    xc = gv - mu
    rs = lax.rsqrt(_mean(xc * xc) + EPS)
    return xc * rs, rs


def _gmlp_fwd(proj, ln_g, ln_b, wm, b_t, job=None):
    T = proj.shape[1]
    tm = min(GMLP_FWD_TOKENS, T)

    def body(u_ref, v_ref, lg_ref, lb_ref, wm_ref, bt_ref, a_ref, a_s):
        gu, _ = _gelu(u_ref[...])
        gv, _ = _gelu(v_ref[...])
        vhat, _ = _layer_norm_stats(gv)
        vnb = (vhat * lg_ref[...] + lb_ref[...]).astype(BF16)
        for g in range(NG):
            cols = slice(128 * g, 128 * (g + 1))
            mixed = _mm(wm_ref[g], _chunks_abreast(vnb[:, cols])) + bt_ref[:, g:g + 1]
            a_s[:, cols] = gu[:, cols] * _chunks_stacked(mixed)
        a_ref[...] = a_s[...].astype(BF16)

    row = lambda i: (0, 0)
    return _call(
        body, name="gmlp_fwd", grid=(T // tm,), job=job, args=(proj, proj, ln_g, ln_b, wm, b_t),
        in_specs=[pl.BlockSpec((None, tm, D), lambda i: (0, i, 0)), pl.BlockSpec((None, tm, D), lambda i: (1, i, 0)),
                  pl.BlockSpec((1, D), row), pl.BlockSpec((1, D), row),
                  pl.BlockSpec((NG, GCH, GCH), lambda i: (0, 0, 0)), pl.BlockSpec((GCH, NG), row)],
        out_specs=[pl.BlockSpec((tm, D), lambda i: (i, 0))],
        out_shape=[SDS((T, D), BF16)],
        scratch_shapes=[pltpu.VMEM((tm, D), F32)])


def _cumsum64(x, row):
    for s in (1, 2, 4, 8, 16, 32):
        x = x + jnp.where(row >= s, pltpu.roll(x, s, 0), 0.0)
    return x


def _revcumsum64(x, row):
    n = x.shape[0]
    for s in (1, 2, 4, 8, 16, 32):
        x = x + jnp.where(row < HCH - s, pltpu.roll(x, n - s, 0), 0.0)
    return x


def _head_mean(x):
    parts = [jnp.broadcast_to(_mean(x[:, HD * h:HD * (h + 1)]), (x.shape[0], HD)) for h in range(x.shape[1] // HD)]
    return jnp.concatenate(parts, axis=1)


def _seg_sum(x):
    n, c = x.shape
    s = jnp.sum(x.reshape(n // HCH, HCH, c), axis=1, keepdims=True)
    return jnp.broadcast_to(s, (n // HCH, HCH, c)).reshape(n, c)


def _seg_row(x, idx):
    n, c = x.shape
    x3 = x.reshape(n // HCH, HCH, c)
    return jnp.broadcast_to(x3[:, idx:idx + 1, :], x3.shape).reshape(n, c)


def _hgrn_gates(fl, lbv, row):
    s = _sigmoid(fl)
    f = lbv + (1.0 - lbv) * s
    a = _cumsum64(jnp.log(f), row)
    return s, f, a, _seg_row(a, HCH // 2 - 1), _seg_row(a, HCH - 1)


def _hgrn_fwd(proj, lb_table, norm_g, job=None):
    T = proj.shape[1]
    tb = min(HGRN_TOKENS, T)
    nc = tb // HCH

    def body(q_ref, fl_ref, v_ref, g_ref, lbt_ref, gn_ref, o_ref, ob_ref, stb_ref, st_s, o_s):
        @pl.when(pl.program_id(1) == 0)
        def _():
            st_s[...] = jnp.zeros_like(st_s)

        row = lax.broadcasted_iota(jnp.int32, (tb, HW), 0) & (HCH - 1)
        lbv = _sigmoid(lbt_ref[0:1, :] - lbt_ref[1:2, :])
        _, f, a, a_mid, a_last = _hgrn_gates(fl_ref[...], lbv, row)
        k = 1.0 - f
        qs = q_ref[...] * QSCALE
        q_in = (qs * jnp.exp(a - a_mid)).astype(BF16)
        k_in = (k * jnp.exp(a_mid - a)).astype(BF16)
        q_a = (qs * jnp.exp(a)).astype(BF16)
        k_d = (k * jnp.exp(a_last - a)).astype(BF16)
        dec = jnp.exp(a_last)
        vb = v_ref[...].astype(BF16)
        tri = (lax.broadcasted_iota(jnp.int32, (HCH, HCH), 0)
               >= lax.broadcasted_iota(jnp.int32, (HCH, HCH), 1))
        for c in range(nc):
            sl = slice(HCH * c, HCH * (c + 1))
            for hh in range(HGRN_HB):
                hs = slice(HD * hh, HD * (hh + 1))
                st = st_s[hh]
                stb_ref[hh, c] = st
                sc = jnp.where(tri, _mm_nt(q_in[sl, hs], k_in[sl, hs]), 0.0)
                o_s[sl, hs] = _mm(sc.astype(BF16), vb[sl, hs]) + _mm_nt(q_a[sl, hs], st.astype(BF16))
                d64 = dec[sl, hs]
                st_s[hh] = st * jnp.concatenate([d64, d64], axis=0) + _mm_tn(vb[sl, hs], k_d[sl, hs])
        o = o_s[...]
        r = lax.rsqrt(_head_mean(o * o) + EPS)
        g = g_ref[...]
        o_ref[...] = o
        ob_ref[...] = (o * r * gn_ref[...] * (g * _sigmoid(g))).astype(BF16)

    def col(off):
        return pl.BlockSpec((None, tb, HW), lambda h, cb: (off, cb, h))

    return _call(
        body, name="hgrn_fwd", grid=(NH // HGRN_HB, T // tb), job=job,
        args=(proj, proj, proj, proj, lb_table, norm_g),
        in_specs=[col(2), col(3), col(4), col(5),
                  pl.BlockSpec((2, HW), lambda h, cb: (0, h)), pl.BlockSpec((1, HW), lambda h, cb: (0, h))],
        out_specs=[pl.BlockSpec((tb, HW), lambda h, cb: (cb, h)), pl.BlockSpec((tb, HW), lambda h, cb: (cb, h)),
                   pl.BlockSpec((HGRN_HB, nc, HD, HD), lambda h, cb: (h, cb, 0, 0))],
        out_shape=[SDS((T, D), F32), SDS((T, D), BF16), SDS((NH, T // HCH, HD, HD), F32)],
        scratch_shapes=[pltpu.VMEM((HGRN_HB, HD, HD), F32), pltpu.VMEM((tb, HW), F32)])


def _merge_fwd(x, ab, ob, proj, w_a, w_b, w_out, job=None):
    T = x.shape[0]
    tm = min(512, T)

    def body(x_ref, ab_ref, ob_ref, ga_ref, gb_ref, wa_ref, wb_ref, wo_ref, mg_ref, x1_ref):
        ya = _mm(ab_ref[...], wa_ref[...])
        yb = _mm(ob_ref[...], wb_ref[...])
        merged = (_sigmoid(ga_ref[...]) * ya + _sigmoid(gb_ref[...]) * yb).astype(BF16)
        mg_ref[...] = merged
        x1_ref[...] = x_ref[...] + _mm(merged, wo_ref[...])

    t = lambda i: (i, 0)
    w = lambda i: (0, 0)
    return _call(
        body, name="merge_fwd", grid=(T // tm,), job=job, args=(x, ab, ob, proj, proj, w_a, w_b, w_out),
        in_specs=[pl.BlockSpec((tm, D), t), pl.BlockSpec((tm, D), t), pl.BlockSpec((tm, D), t),
                  pl.BlockSpec((None, tm, D), lambda i: (6, i, 0)), pl.BlockSpec((None, tm, D), lambda i: (7, i, 0)),
                  pl.BlockSpec((D, D), w), pl.BlockSpec((D, D), w), pl.BlockSpec((D, D), w)],
        out_specs=[pl.BlockSpec((tm, D), t)] * 2,
        out_shape=[SDS((T, D), BF16), SDS((T, D), F32)])


def _ffn_fwd_bwd(x1, target, g_ffn, g_fin, w_gu, w_down):
    T = x1.shape[0]
    tm = min(256, T)
    inv_d = 1.0 / D

    def body(x1_ref, tg_ref, gf_ref, gn_ref, wgu_ref, wd_ref,
             act_ref, dx2b_ref, h2b_ref, dgu_ref, dx1_ref, dx1b_ref, acc_ref):
        @pl.when(pl.program_id(0) == 0)
        def _():
            acc_ref[...] = jnp.zeros_like(acc_ref)

        x1v = x1_ref[...]
        gf = gf_ref[...]
        gn = gn_ref[...]
        rr1 = lax.rsqrt(_mean(x1v * x1v) + EPS)
        x1n = x1v * rr1
        h2b = (x1n * gf).astype(BF16)
        h2b_ref[...] = h2b
        gate = _mm(h2b, wgu_ref[0])
        up = _mm(h2b, wgu_ref[1])
        sg = _sigmoid(gate)
        si = gate * sg
        act = (si * up).astype(BF16)
        act_ref[...] = act
        x2 = x1v + _mm(act, wd_ref[...])
        rr2 = lax.rsqrt(_mean(x2 * x2) + EPS)
        x2n = x2 * rr2
        e = x2n * gn - tg_ref[...]
        acc_ref[0] += _rows8(e * e) * (0.5 * inv_d)
        dy = e * inv_d
        acc_ref[1] += _rows8(dy * x2n)
        dxn = dy * gn
        dx2 = rr2 * (dxn - x2n * _mean(dxn * x2n))
        dx2b = dx2.astype(BF16)
        dx2b_ref[...] = dx2b
        dact = _mm_nt(dx2b, wd_ref[...])
        dgate = (dact * up * (sg * (1.0 + gate * (1.0 - sg)))).astype(BF16)
        dup = (dact * si).astype(BF16)
        dgu_ref[0] = dgate
        dgu_ref[1] = dup
        dh2 = _mm_nt(dgate, wgu_ref[0]) + _mm_nt(dup, wgu_ref[1])
        acc_ref[2] += _rows8(dh2 * x1n)
        dxn1 = dh2 * gf
        dx1 = dx2 + rr1 * (dxn1 - x1n * _mean(dxn1 * x1n))
        dx1_ref[...] = dx1
        dx1b_ref[...] = dx1.astype(BF16)

    t = lambda i: (i, 0)
    w = lambda i: (0, 0)
    one = pl.Buffered(1)
    return pl.pallas_call(
        body, name="ffn_fwd_bwd", grid=(T // tm,),
        in_specs=[pl.BlockSpec((tm, D), t), pl.BlockSpec((tm, D), t),
                  pl.BlockSpec((1, D), w), pl.BlockSpec((1, D), w),
                  pl.BlockSpec((2, D, FF), lambda i: (0, 0, 0), pipeline_mode=one),
                  pl.BlockSpec((FF, D), w, pipeline_mode=one)],
        out_specs=[pl.BlockSpec((tm, FF), t), pl.BlockSpec((tm, D), t), pl.BlockSpec((tm, D), t),
                   pl.BlockSpec((2, tm, FF), lambda i: (0, i, 0)),
                   pl.BlockSpec((tm, D), t), pl.BlockSpec((tm, D), t),
                   pl.BlockSpec((3, 8, D), lambda i: (0, 0, 0))],
        out_shape=[SDS((T, FF), BF16), SDS((T, D), BF16), SDS((T, D), BF16),
                   SDS((2, T, FF), BF16), SDS((T, D), F32), SDS((T, D), BF16),
                   SDS((3, 8, D), F32)],
        compiler_params=_cparams(),
    )(x1, target, g_ffn, g_fin, w_gu, w_down)


def _merge_bwd(dx1b, ab, ob, proj, w_out, w_a, w_b, job=None):
    T = dx1b.shape[0]
    tm = min(512, T)

    def body(dx_ref, ab_ref, ob_ref, ga_ref, gb_ref, wo_ref, wa_ref, wb_ref, dya_ref, dyb_ref, dp_ref):
        dm = _mm_nt(dx_ref[...], wo_ref[...])
        sa = _sigmoid(ga_ref[...])
        sb = _sigmoid(gb_ref[...])
        dya_ref[...] = (dm * sa).astype(BF16)
        dyb_ref[...] = (dm * sb).astype(BF16)
        dp_ref[0] = (dm * _mm(ab_ref[...], wa_ref[...]) * sa * (1.0 - sa)).astype(BF16)
        dp_ref[1] = (dm * _mm(ob_ref[...], wb_ref[...]) * sb * (1.0 - sb)).astype(BF16)

    t = lambda i: (i, 0)
    w = lambda i: (0, 0)
    return _call(
        body, name="merge_bwd", grid=(T // tm,),
        in_specs=[pl.BlockSpec((tm, D), t), pl.BlockSpec((tm, D), t), pl.BlockSpec((tm, D), t),
                  pl.BlockSpec((None, tm, D), lambda i: (6, i, 0)), pl.BlockSpec((None, tm, D), lambda i: (7, i, 0)),
                  pl.BlockSpec((D, D), w), pl.BlockSpec((D, D), w), pl.BlockSpec((D, D), w)],
        out_specs=[pl.BlockSpec((tm, D), t)] * 2 + [pl.BlockSpec((2, tm, D), lambda i: (3, i, 0))],
        out_shape=[SDS((T, D), BF16), SDS((T, D), BF16), SDS((NIN, T, D), BF16)],
        args=(dx1b, ab, ob, proj, proj, w_out, w_a, w_b), job=job)


def _hgrn_bwd(dproj, dyb, w_b, o_raw, proj, st_before, lb_table, norm_g, job=None):
    T = dyb.shape[0]
    tb = min(HGRN_TOKENS, T)
    nc = tb // HCH
    nb = T // tb

    def body(dp_in, dyb_ref, wb_ref, o_ref, q_ref, fl_ref, v_ref, g_ref, stb_ref, lbt_ref, gn_ref,
             dp_ref, acc_ref, dst_s, dqin_s, dqa_s, dkin_s, dkd_s, dv_s, ddec_s):
        del dp_in

        @pl.when(pl.program_id(1) == 0)
        def _():
            dst_s[...] = jnp.zeros_like(dst_s)
            acc_ref[...] = jnp.zeros_like(acc_ref)

        row = lax.broadcasted_iota(jnp.int32, (tb, HW), 0) & (HCH - 1)
        gn = gn_ref[...]
        lbv = _sigmoid(lbt_ref[0:1, :] - lbt_ref[1:2, :])
        o = o_ref[...]
        r = lax.rsqrt(_head_mean(o * o) + EPS)
        on = o * r
        g = g_ref[...]
        sgm = _sigmoid(g)
        dob_v = _mm_nt(dyb_ref[...], wb_ref[...])
        dp_ref[3] = (dob_v * on * gn * (sgm * (1.0 + g * (1.0 - sgm)))).astype(BF16)
        do_n = dob_v * (g * sgm)
        acc_ref[1] += _rows8(do_n * on)
        dxn = do_n * gn
        do = (r * (dxn - on * _head_mean(dxn * on))).astype(BF16)
        s, f, a, a_mid, a_last = _hgrn_gates(fl_ref[...], lbv, row)
        k = 1.0 - f
        qs = q_ref[...] * QSCALE
        e_q = jnp.exp(a - a_mid)
        e_k = jnp.exp(a_mid - a)
        e_a = jnp.exp(a)
        e_l = jnp.exp(a_last - a)
        dec = jnp.exp(a_last)
        q_in = qs * e_q
        k_in = k * e_k
        q_a = qs * e_a
        k_d = k * e_l
        q_inb, k_inb, q_ab, k_db = (z.astype(BF16) for z in (q_in, k_in, q_a, k_d))
        vb = v_ref[...].astype(BF16)
        tri = (lax.broadcasted_iota(jnp.int32, (HCH, HCH), 0)
               >= lax.broadcasted_iota(jnp.int32, (HCH, HCH), 1))
        for c in reversed(range(nc)):
            sl = slice(HCH * c, HCH * (c + 1))
            for hh in range(HGRN_HB):
                hs = slice(HD * hh, HD * (hh + 1))
                stp = stb_ref[hh, c]
                dst = dst_s[hh]
                dstb = dst.astype(BF16)
                do_c = do[sl, hs]
                v_c = vb[sl, hs]
                dqa_s[sl, hs] = _mm(do_c, stp.astype(BF16))
                dkd_s[sl, hs] = _mm(v_c, dstb)
                ddec_s[sl, hs] = jnp.broadcast_to(jnp.sum(dst * stp, axis=0, keepdims=True), (HCH, HD))
                sc = jnp.where(tri, _mm_nt(q_inb[sl, hs], k_inb[sl, hs]), 0.0).astype(BF16)
                dsc = jnp.where(tri, _mm_nt(do_c, v_c), 0.0).astype(BF16)
                dv_s[sl, hs] = _mm_nt(k_db[sl, hs], dstb) + _mm_tn(sc, do_c)
                dqin_s[sl, hs] = _mm(dsc, k_inb[sl, hs])
                dkin_s[sl, hs] = _mm_tn(dsc, q_inb[sl, hs])
                d64 = dec[sl, hs]
                dst_s[hh] = dst * jnp.concatenate([d64, d64], axis=0) + _mm_tn(do_c, q_ab[sl, hs])
        dq_in = dqin_s[...]
        dq_a = dqa_s[...]
        dk_in = dkin_s[...]
        dk_d = dkd_s[...]
        dp_ref[0] = ((dq_in * e_q + dq_a * e_a) * QSCALE).astype(BF16)
        dp_ref[2] = dv_s[...].astype(BF16)
        tq = dq_in * q_in
        tk = dk_in * k_in
        td = dk_d * k_d
        d_a = tq + dq_a * q_a - tk - td
        d_a = d_a + jnp.where(row == HCH // 2 - 1, _seg_sum(tk - tq), 0.0)
        d_a = d_a + jnp.where(row == HCH - 1, _seg_sum(td) + ddec_s[...] * dec, 0.0)
        dlf = _revcumsum64(d_a, row)
        df = dlf / f - (dk_in * e_k + dk_d * e_l)
        dp_ref[1] = (df * (1.0 - lbv) * s * (1.0 - s)).astype(BF16)
        acc_ref[0] += _rows8(df * (1.0 - s))

    def col(off):
        return pl.BlockSpec((None, tb, HW), lambda h, cb: (off, nb - 1 - cb, h))

    hb = lambda h, cb: (nb - 1 - cb, h)
    return _call(
        body, name="hgrn_bwd", grid=(NH // HGRN_HB, nb), job=job,
        args=(dproj, dyb, w_b, o_raw, proj, proj, proj, proj, st_before, lb_table, norm_g),
        in_specs=[ANY, pl.BlockSpec((tb, D), lambda h, cb: (nb - 1 - cb, 0)),
                  pl.BlockSpec((HW, D), lambda h, cb: (h, 0)), pl.BlockSpec((tb, HW), hb),
                  col(2), col(3), col(4), col(5),
                  pl.BlockSpec((HGRN_HB, nc, HD, HD), lambda h, cb: (h, nb - 1 - cb, 0, 0)),
                  pl.BlockSpec((2, HW), lambda h, cb: (0, h)), pl.BlockSpec((1, HW), lambda h, cb: (0, h))],
        out_specs=[pl.BlockSpec((4, tb, HW), lambda h, cb: (0, nb - 1 - cb, h)),
                   pl.BlockSpec((2, 8, HW), lambda h, cb: (0, 0, h))],
        out_shape=[SDS(dproj.shape, BF16), SDS((2, 8, D), F32)],
        scratch_shapes=[pltpu.VMEM((HGRN_HB, HD, HD), F32)] + [pltpu.VMEM((tb, HW), F32)] * 6,
        aliases={0: 0})


def _gmlp_bwd(dproj, dya, w_a, proj, ln_g, ln_b, wm, wm_t, b_t):
    T = dya.shape[0]
    tm = min(GMLP_BWD_TOKENS, T)

    def body(dp_in, dya_ref, wa_ref, u_ref, v_ref, lg_ref, lb_ref, wm_ref, wmt_ref, bt_ref,
             dp_ref, acc_ref, dws_ref, dmix_ref, du_s, dvn_s):
        del dp_in

        @pl.when(pl.program_id(0) == 0)
        def _():
            acc_ref[...] = jnp.zeros_like(acc_ref)
            dws_ref[...] = jnp.zeros_like(dws_ref)
            dmix_ref[...] = jnp.zeros_like(dmix_ref)

        u = u_ref[...]
        v = v_ref[...]
        lg = lg_ref[...]
        gu, t_u = _gelu(u)
        gv, t_v = _gelu(v)
        vhat, rs = _layer_norm_stats(gv)
        vnb = (vhat * lg + lb_ref[...]).astype(BF16)
        da_v = _mm_nt(dya_ref[...], wa_ref[...])
        for g in range(NG):
            cols = slice(128 * g, 128 * (g + 1))
            vng = _chunks_abreast(vnb[:, cols])
            mixed = _mm(wm_ref[g], vng) + bt_ref[:, g:g + 1]
            dag = _chunks_abreast(da_v[:, cols])
            dmx = dag * _chunks_abreast(gu[:, cols])
            du_s[:, cols] = _chunks_stacked(dag * mixed)
            dmxb = dmx.astype(BF16)
            dws_ref[:, cols] += _mm_nt(dmxb, vng)
            dmix_ref[:, cols] += sum(dmx[:, GCH * ch:GCH * (ch + 1)] for ch in range(tm // GCH))
            dvn_s[:, cols] = _chunks_stacked(_mm(wmt_ref[g], dmxb))
        dp_ref[0] = (du_s[...] * _gelu_grad(u, t_u)).astype(BF16)
        dvn = dvn_s[...]
        acc_ref[0] += _rows8(dvn * vhat)
        acc_ref[1] += _rows8(dvn)
        dvh = dvn * lg
        dgv = rs * (dvh - _mean(dvh) - vhat * _mean(dvh * vhat))
        dp_ref[1] = (dgv * _gelu_grad(v, t_v)).astype(BF16)

    row = lambda i: (0, 0)
    w3 = lambda i: (0, 0, 0)
    return pl.pallas_call(
        body, name="gmlp_bwd", grid=(T // tm,),
        in_specs=[ANY, pl.BlockSpec((tm, D), lambda i: (i, 0)), pl.BlockSpec((D, D), row),
                  pl.BlockSpec((None, tm, D), lambda i: (0, i, 0)), pl.BlockSpec((None, tm, D), lambda i: (1, i, 0)),
                  pl.BlockSpec((1, D), row), pl.BlockSpec((1, D), row),
                  pl.BlockSpec((NG, GCH, GCH), w3), pl.BlockSpec((NG, GCH, GCH), w3),
                  pl.BlockSpec((GCH, NG), row)],
        out_specs=[pl.BlockSpec((2, tm, D), lambda i: (2, i, 0)),
                   pl.BlockSpec((2, 8, D), w3), pl.BlockSpec((GCH, D), row), pl.BlockSpec((GCH, D), row)],
        out_shape=[SDS(dproj.shape, BF16), SDS((2, 8, D), F32), SDS((GCH, D), F32), SDS((GCH, D), F32)],
        scratch_shapes=[pltpu.VMEM((tm, D), F32), pltpu.VMEM((tm, D), F32)],
        input_output_aliases={0: 0},
        compiler_params=_cparams(),
    )(dproj, dya, w_a, proj, proj, ln_g, ln_b, wm, wm_t, b_t)


def _proj_bwd(dproj, w_in4, x, dx1, g_mix, job=None):
    T = x.shape[0]
    tm = min(256, T)
    order = (2, 3, 4, 5, 0, 1, 6, 7)

    def body(dp_ref, w_ref, x_ref, dx1_ref, g_ref, gx_ref, acc_ref):
        @pl.when(pl.program_id(0) == 0)
        def _():
            acc_ref[...] = jnp.zeros_like(acc_ref)

        dh = None
        for m, og in enumerate(order):
            part = _mm_nt(dp_ref[m], w_ref[og // 2, :, D * (og % 2):D * (og % 2 + 1)])
            dh = part if dh is None else dh + part
        xv = x_ref[...]
        r = lax.rsqrt(_mean(xv * xv) + EPS)
        xn = xv * r
        acc_ref[...] += _rows8(dh * xn)
        dxn = dh * g_ref[...]
        gx_ref[...] = dx1_ref[...] + r * (dxn - xn * _mean(dxn * xn))

    t = lambda i: (i, 0)
    return _call(
        body, name="proj_bwd", grid=(T // tm,),
        in_specs=[pl.BlockSpec((NIN, tm, D), lambda i: (0, i, 0)),
                  pl.BlockSpec((NCHIP, D, 2 * D), lambda i: (0, 0, 0), pipeline_mode=pl.Buffered(1)),
                  pl.BlockSpec((tm, D), t), pl.BlockSpec((tm, D), t), pl.BlockSpec((1, D), lambda i: (0, 0))],
        out_specs=[pl.BlockSpec((tm, D), t), pl.BlockSpec((8, D), lambda i: (0, 0))],
        out_shape=[SDS((T, D), F32), SDS((8, D), F32)],
        args=(dproj, w_in4, x, dx1, g_mix), job=job)


def _dw_call(name, a, b, a_spec, b_spec, o_spec, out_shape, nblk, tt, job=None, prefetch=None):
    T = a.shape[-2]

    def body(*refs):
        a_ref, b_ref, o_ref = refs[-3:]

        @pl.when(pl.program_id(1) == 0)
        def _():
            o_ref[...] = jnp.zeros_like(o_ref)
        o_ref[...] += _mm_tn(a_ref[...], b_ref[...])

    (out,), job_out = _call(
        body, name=name, grid=(nblk, T // tt), in_specs=[a_spec, b_spec], out_specs=[o_spec],
        out_shape=[out_shape], args=(a, b), job=job, prefetch=prefetch)
    return out, job_out


def _dw_in_half(name, place, hb, dproj, mine, job=None):
    tt = min(DW_IN_TOKENS, hb.shape[0])

    def comp(k, pc):
        return _component_of(2 * k + (pc[1] if mine else 1 - pc[1]))

    return _dw_call(
        name, hb, dproj,
        pl.BlockSpec((tt, D), lambda k, t, pc: (t, 0)),
        pl.BlockSpec((None, tt, D), lambda k, t, pc: (comp(k, pc), t, 0)),
        pl.BlockSpec((None, D, D), lambda k, t, pc: (k, 0, 0)),
        SDS((NCHIP, D, D), F32), NCHIP, tt, job, place)


def _dw_gate_up(h2b, dgu, job=None):
    tt = min(DW_TOKENS, h2b.shape[0])
    return _dw_call(
        "dw_gate_up", h2b, dgu,
        pl.BlockSpec((tt, D), lambda k, t: (t, 0)),
        pl.BlockSpec((None, tt, FFS), lambda k, t: (k // 2, t, k % 2)),
        pl.BlockSpec((None, D, FFS), lambda k, t: (k, 0, 0)),
        SDS((NCHIP, D, FFS), F32), NCHIP, tt, job)


def _dw_down(act, dx2b, job=None):
    tt = min(DW_TOKENS, act.shape[0])
    g, job_out = _dw_call(
        "dw_down", act, dx2b,
        pl.BlockSpec((tt, FFS), lambda k, t: (t, k)),
        pl.BlockSpec((tt, D), lambda k, t: (t, 0)),
        pl.BlockSpec((FFS, D), lambda k, t: (k, 0)),
        SDS((FF, D), F32), 2, tt, job)
    return g.reshape(NCHIP, FF // NCHIP, D), job_out


def _dw_square(name, a, b, job=None):
    tt = min(DW_TOKENS, a.shape[0])
    g, job_out = _dw_call(
        name, a, b,
        pl.BlockSpec((tt, D), lambda k, t: (t, 0)), pl.BlockSpec((tt, D), lambda k, t: (t, 0)),
        pl.BlockSpec((D, D), lambda k, t: (0, 0)), SDS((D, D), F32), 1, tt, job)
    return g.reshape(NCHIP, D // NCHIP, D), job_out


def _place():
    x, y, c = lax.axis_index("x"), lax.axis_index("y"), lax.axis_index("c")
    return x, y, c, 2 * x + y


def _chip_at(x, y, s):
    return x ^ (s >> 1), y ^ (s & 1)


class _Job:
    def __init__(self, ins, out_shapes, sems, start, finish, aliases=None, mid=None):
        self.ins, self.out_shapes, self.sems = list(ins), list(out_shapes), list(sems)
        self.start, self.finish, self.aliases = start, finish, dict(aliases or {})
        self.mid = mid if mid is not None else (lambda ins, outs, sems: None)


def _join_jobs(*jobs):
    def cut(refs, sizes):
        out, at = [], 0
        for n in sizes:
            out.append(refs[at:at + n])
            at += n
        return out

    ni = [len(j.ins) for j in jobs]
    no = [len(j.out_shapes) for j in jobs]
    ns = [len(j.sems) for j in jobs]

    def run(which):
        def go(ins, outs, sems):
            for j, a, b, c in zip(jobs, cut(ins, ni), cut(outs, no), cut(sems, ns)):
                getattr(j, which)(a, b, c)
        return go

    aliases = {}
    for k, j in enumerate(jobs):
        for a, b in j.aliases.items():
            aliases[sum(ni[:k]) + a] = sum(no[:k]) + b
    return _Job([a for j in jobs for a in j.ins], [o for j in jobs for o in j.out_shapes],
                [s for j in jobs for s in j.sems], run("start"), run("finish"), aliases, run("mid"))


def _call(body, *, name, grid, in_specs, out_specs, out_shape, args, scratch_shapes=(), aliases=None,
          job=None, prefetch=None):
    n_in, n_out, n_scr = len(in_specs), len(out_specs), len(scratch_shapes)
    npf = 0 if prefetch is None else 1
    job = job if job is not None else _Job([], [], [], lambda *a: None, lambda *a: None)
    ji, jo = len(job.ins), len(job.out_shapes)
    steps = math.prod(grid)

    def wrapped(*refs):
        pf, refs = refs[:npf], refs[npf:]
        ins, jin = refs[:n_in], refs[n_in:n_in + ji]
        o0 = n_in + ji
        outs, jout = refs[o0:o0 + n_out], refs[o0 + n_out:o0 + n_out + jo]
        s0 = o0 + n_out + jo
        scr, jsem = refs[s0:s0 + n_scr], refs[s0 + n_scr:]
        step = functools.reduce(lambda acc, ag: acc * ag[1] + pl.program_id(ag[0]), enumerate(grid), 0)
        if ji or jo:
            @pl.when(step == 0)
            def _():
                job.start(jin, jout, jsem)

        body(*pf, *ins, *outs, *scr)

        if ji or jo:
            @pl.when(step == steps // 2)
            def _():
                job.mid(jin, jout, jsem)

            @pl.when(step == steps - 1)
            def _():
                job.finish(jin, jout, jsem)

    io = {npf + a: b for a, b in dict(aliases or {}).items()}
    io.update({npf + n_in + a: n_out + b for a, b in job.aliases.items()})
    kw = dict(in_specs=list(in_specs) + [ANY] * ji, out_specs=list(out_specs) + [ANY] * jo,
              scratch_shapes=list(scratch_shapes) + job.sems)
    if npf:
        kw = dict(grid_spec=pltpu.PrefetchScalarGridSpec(num_scalar_prefetch=1, grid=grid, **kw))
    else:
        kw["grid"] = grid
    res = pl.pallas_call(
        wrapped, name=name, out_shape=list(out_shape) + job.out_shapes, input_output_aliases=io,
        compiler_params=_cparams(has_side_effects=bool(ji or jo)), **kw,
    )(*(() if prefetch is None else (prefetch,)), *args, *job.ins)
    return list(res[:n_out]), list(res[n_out:])


def _cast_shards(name, place, ws, paired=False):
    n = len(ws)
    rows, cols = ws[0].shape
    tr = 352 if rows % 352 == 0 else 256
    shape = (2, rows, 2 * cols) if paired else (NCHIP, rows, cols)
    mine = (lambda i, pc: (pc[0] // 2, i, pc[0] % 2)) if paired else (lambda i, pc: (pc[0], i, 0))

    def body(pc_ref, *refs):
        del pc_ref
        for w_ref, o_ref in zip(refs[:n], refs[n:]):
            o_ref[...] = w_ref[...].astype(BF16)

    return pl.pallas_call(
        body, name=name,
        grid_spec=pltpu.PrefetchScalarGridSpec(
            num_scalar_prefetch=1, grid=(rows // tr,),
            in_specs=[pl.BlockSpec((tr, cols), lambda i, pc: (i, 0))] * n,
            out_specs=[pl.BlockSpec((None, tr, cols), mine)] * n),
        out_shape=[SDS(shape, BF16)] * n,
        compiler_params=_cparams(),
    )(place, *ws)


def _sibling_copy(ref, send_sem, recv_sem):
    x, y, c, _ = _place()
    return pltpu.make_async_remote_copy(src_ref=ref, dst_ref=ref, send_sem=send_sem, recv_sem=recv_sem,
                                        device_id=(x, y, 1 - c), device_id_type=MESH)


def _slot(arr, chip):
    if arr.shape[0] == NCHIP:
        return arr.at[chip]
    cols = arr.shape[2] // 2
    return arr.at[chip // 2, :, pl.ds(pl.multiple_of((chip % 2) * cols, 128), cols)]


def _half_rows(arr, slot, core):
    half = arr.shape[1] // 2
    return _slot(arr, slot).at[pl.ds(pl.multiple_of(core * half, 16), half)]


def _quarter_rows(arr, slot, core, q):
    quarter = arr.shape[1] // 4
    return _slot(arr, slot).at[pl.ds(pl.multiple_of((2 * core + q) * quarter, 16), quarter)]


def _chip_copy(ref, dist, send_sem, recv_sem):
    x, y, c, _ = _place()
    cx, cy = _chip_at(x, y, dist)
    return pltpu.make_async_remote_copy(src_ref=ref, dst_ref=ref, send_sem=send_sem, recv_sem=recv_sem,
                                        device_id=(cx, cy, c), device_id_type=MESH)


def _gather_sems(n):
    dma = pltpu.SemaphoreType.DMA
    return [dma((n, 2))] * 4 + [dma((n, 4))] * 2


def _gather_start(arrs, sems):
    dsend, drecv = sems[0], sems[1]
    _, _, c, j = _place()
    for w, arr in enumerate(arrs):
        for dist in (1, 2):
            _chip_copy(_half_rows(arr, j, c), dist, dsend.at[w, dist - 1], drecv.at[w, dist - 1]).start()


def _gather_land(arrs, sems, dist, first=0):
    dsend, drecv, rsend, rrecv, fsend, frecv = sems
    _, _, c, j = _place()
    if dist < 3:
        other = 3 - dist
        for w, arr in enumerate(arrs, first):
            landed = _half_rows(arr, j ^ dist, c)
            _chip_copy(landed, dist, dsend.at[w, dist - 1], drecv.at[w, dist - 1]).wait_recv()
            relay = _quarter_rows(arr, j ^ dist, c, other - 1)
            _chip_copy(relay, other, rsend.at[w, other - 1], rrecv.at[w, other - 1]).start()
            _sibling_copy(landed, fsend.at[w, dist - 1], frecv.at[w, dist - 1]).start()
        for w, arr in enumerate(arrs, first):
            theirs = _half_rows(arr, j ^ dist, 1 - c)
            _sibling_copy(theirs, fsend.at[w, dist - 1], frecv.at[w, dist - 1]).wait_recv()
    else:
        for w, arr in enumerate(arrs, first):
            for via in (1, 2):
                piece = _quarter_rows(arr, j ^ 3, c, via - 1)
                _chip_copy(piece, via, rsend.at[w, via - 1], rrecv.at[w, via - 1]).wait_recv()
                _sibling_copy(piece, fsend.at[w, 1 + via], frecv.at[w, 1 + via]).start()
        for w, arr in enumerate(arrs, first):
            for via in (1, 2):
                theirs = _quarter_rows(arr, j ^ 3, 1 - c, via - 1)
                _sibling_copy(theirs, fsend.at[w, 1 + via], frecv.at[w, 1 + via]).wait_recv()


def _gather_drain(arrs, sems):
    dsend, drecv, rsend, rrecv, fsend, frecv = sems
    _, _, c, j = _place()
    for w, arr in enumerate(arrs):
        for dist in (1, 2):
            other = 3 - dist
            _chip_copy(_half_rows(arr, j, c), dist, dsend.at[w, dist - 1], drecv.at[w, dist - 1]).wait_send()
            _chip_copy(_quarter_rows(arr, j ^ dist, c, other - 1), other,
                       rsend.at[w, other - 1], rrecv.at[w, other - 1]).wait_send()
            _sibling_copy(_half_rows(arr, j ^ dist, c), fsend.at[w, dist - 1], frecv.at[w, dist - 1]).wait_send()
            _sibling_copy(_quarter_rows(arr, j ^ 3, c, dist - 1),
                          fsend.at[w, 1 + dist], frecv.at[w, 1 + dist]).wait_send()


def _gather_neighbours(arrs, sems):
    _gather_land(arrs, sems, 1)
    _gather_land(arrs, sems, 2)


def _gather_finish(arrs, sems):
    _gather_land(arrs, sems, 3)
    _gather_drain(arrs, sems)


def _gather_job(arrs):
    n = len(arrs)
    return _Job(arrs, [SDS(a.shape, a.dtype) for a in arrs], _gather_sems(n),
                lambda ins, outs, sems: _gather_start(outs, sems),
                lambda ins, outs, sems: _gather_finish(outs, sems), {k: k for k in range(n)},
                mid=lambda ins, outs, sems: _gather_neighbours(outs, sems))


def _exchange_job(arrs, out_shapes, n, copies):
    def start(ins, outs, sems):
        for cp in copies(ins, outs, sems[0], sems[1]):
            cp.start()

    def finish(ins, outs, sems):
        for cp in copies(ins, outs, sems[0], sems[1]):
            cp.wait()

    return _Job(arrs, out_shapes, [pltpu.SemaphoreType.DMA((n,))] * 2, start, finish)


def _pair_exchange_job(grads):
    def copies(ins, outs, send_sem, recv_sem):
        x, y, c, _ = _place()
        res = []
        for w in range(len(grads)):
            half = ins[w].shape[1] // 2
            theirs = pl.ds(pl.multiple_of((1 - c) * half, 8), half)
            res.append(pltpu.make_async_remote_copy(
                src_ref=ins[w].at[:, theirs, :], dst_ref=outs[w], send_sem=send_sem.at[w],
                recv_sem=recv_sem.at[w], device_id=(x, y, 1 - c), device_id_type=MESH))
        return res

    return _exchange_job(grads, [SDS((NCHIP, g.shape[1] // 2, g.shape[2]), F32) for g in grads],
                         len(grads), copies)


def _row_tile(rows, cols):
    tr = rows
    while tr * cols * 4 > ELEMENTWISE_BLOCK_BYTES and tr % 32 == 0:
        tr //= 2
    return tr


def _pair_sums(name, place, gs, sibs):
    n = len(gs)
    half, cols = sibs[0].shape[1], sibs[0].shape[2]
    tr = _row_tile(half, cols)
    nt = half // tr
    mine = nt if gs[0].shape[1] == 2 * half else 0

    def body(pc_ref, *refs):
        del pc_ref
        for g_ref, s_ref, own_ref, out_ref in zip(refs[:n], refs[n:2 * n], refs[2 * n:3 * n], refs[3 * n:]):
            v = g_ref[...] + s_ref[...]

            @pl.when(pl.program_id(1) == 0)
            def _():
                own_ref[...] = v

            @pl.when(pl.program_id(1) > 0)
            def _():
                out_ref[...] = v.astype(BF16)

    res = pl.pallas_call(
        body, name=name,
        grid_spec=pltpu.PrefetchScalarGridSpec(
            num_scalar_prefetch=1, grid=(nt, NCHIP),
            in_specs=[pl.BlockSpec((None, tr, cols), lambda i, s, pc: (pc[0] ^ s, pc[1] * mine + i, 0))] * n
            + [pl.BlockSpec((None, tr, cols), lambda i, s, pc: (pc[0] ^ s, i, 0))] * n,
            out_specs=[pl.BlockSpec((tr, cols), lambda i, s, pc: (i, 0))] * n
            + [pl.BlockSpec((None, tr, cols), lambda i, s, pc: (jnp.maximum(s - 1, 0), i, 0))] * n),
        out_shape=[SDS((half, cols), F32)] * n + [SDS((NCHIP - 1, half, cols), BF16)] * n,
        compiler_params=_cparams(),
    )(place, *gs, *sibs)
    return res[:n], res[n:]


def _chip_exchange_job(parts):
    def copies(ins, outs, send_sem, recv_sem):
        x, y, c, _ = _place()
        res = []
        for w in range(len(parts)):
            for s in range(1, NCHIP):
                cx, cy = _chip_at(x, y, s)
                k = w * (NCHIP - 1) + s - 1
                res.append(pltpu.make_async_remote_copy(
                    src_ref=ins[w].at[s - 1], dst_ref=outs[w].at[s - 1], send_sem=send_sem.at[k],
                    recv_sem=recv_sem.at[k], device_id=(cx, cy, c), device_id_type=MESH))
        return res

    return _exchange_job(parts, [SDS((NCHIP - 1,) + p.shape[1:], BF16) for p in parts],
                         len(parts) * (NCHIP - 1), copies)


def _chip_sums(name, owns, rems):
    n = len(owns)
    half, cols = owns[0].shape
    tr = _row_tile(half, cols)

    def body(*refs):
        for own_ref, rem_ref, out_ref in zip(refs[:n], refs[n:2 * n], refs[2 * n:]):
            out_ref[...] = (((own_ref[...] + rem_ref[0].astype(F32)) + rem_ref[1].astype(F32))
                            + rem_ref[2].astype(F32))

    return pl.pallas_call(
        body, name=name, grid=(half // tr,),
        in_specs=[pl.BlockSpec((tr, cols), lambda i: (i, 0))] * n
        + [pl.BlockSpec((NCHIP - 1, tr, cols), lambda i: (0, i, 0))] * n,
        out_specs=[pl.BlockSpec((tr, cols), lambda i: (i, 0))] * n,
        out_shape=[SDS((half, cols), F32)] * n,
        compiler_params=_cparams(),
    )(*owns, *rems)


def _share_halves_job(halves):
    def copies(ins, outs, send_sem, recv_sem):
        x, y, c, _ = _place()
        return [pltpu.make_async_remote_copy(
            src_ref=ins[w], dst_ref=outs[w], send_sem=send_sem.at[w], recv_sem=recv_sem.at[w],
            device_id=(x, y, 1 - c), device_id_type=MESH) for w in range(len(halves))]

    return _exchange_job(halves, [SDS(h.shape, F32) for h in halves], len(halves), copies)


def _adamw_math(w, g, m, v):
    m = B1 * m + (1.0 - B1) * g
    v = B2 * v + (1.0 - B2) * (g * g)
    m_hat = m / (1.0 - B1 ** STEP)
    v_hat = v / (1.0 - B2 ** STEP)
    delta = -LR * (m_hat / (jnp.sqrt(v_hat) + AEPS) + WD * w)
    return delta, m, v


def _adamws(name, place, ws, owns, sibs, ms, vs):
    n = len(ws)
    rows, cols = ws[0].shape
    by_cols = owns[0].shape[0] == rows
    half, pc_cols = (rows, cols // 2) if by_cols else (rows // 2, cols)
    tr = _row_tile(half, pc_cols)
    nt = half // tr

    def body(pc_ref, *refs):
        ins, outs = refs[:5 * n], refs[5 * n:]
        for k in range(n):
            w_ref, own_ref, sib_ref, m_ref, v_ref = ins[5 * k:5 * k + 5]
            g = jnp.where(pl.program_id(0) == pc_ref[1], own_ref[...], sib_ref[...])
            d, mn, vn = _adamw_math(w_ref[...], g, m_ref[...], v_ref[...])
            for ref, val in zip(outs[4 * k:4 * k + 4], (g, d, mn, vn)):
                ref[...] = val

    full = pl.BlockSpec((tr, pc_cols), (lambda h, i, pc: (i, h)) if by_cols else (lambda h, i, pc: (h * nt + i, 0)))
    part = pl.BlockSpec((tr, pc_cols), lambda h, i, pc: (i, 0))
    res = pl.pallas_call(
        body, name=name,
        grid_spec=pltpu.PrefetchScalarGridSpec(
            num_scalar_prefetch=1, grid=(2, nt),
            in_specs=[full, part, part, full, full] * n, out_specs=[full] * (4 * n)),
        out_shape=[SDS((rows, cols), F32)] * (4 * n),
        compiler_params=_cparams(),
    )(place, *[a for group in zip(ws, owns, sibs, ms, vs) for a in group])
    return [tuple(res[4 * k:4 * k + 4]) for k in range(n)]


def _small_allreduce_adamw(sp, wmv, job):
    shape = sp.shape
    ji, jo = len(job.ins), len(job.out_shapes)

    def body(sp_ref, wmv_ref, *rest):
        jin, (g_ref, d_ref, mo_ref, vo_ref), jout = rest[:ji], rest[ji:ji + 4], rest[ji + 4:ji + 4 + jo]
        sib_s, pair_s, chip_s, send_sem, recv_sem = rest[ji + 4 + jo:ji + 9 + jo]
        jsem = rest[ji + 9 + jo:]
        job.start(jin, jout, jsem)
        x, y, c, j = _place()
        cp = pltpu.make_async_remote_copy(
            src_ref=sp_ref, dst_ref=sib_s, send_sem=send_sem.at[0], recv_sem=recv_sem.at[0],
            device_id=(x, y, 1 - c), device_id_type=MESH)
        cp.start()
        cp.wait()
        pair_s[...] = sp_ref[...] + sib_s[...]
        half = shape[0] // 2
        mine = pl.ds(pl.multiple_of(c * half, 8), half)
        cps = []
        for s in range(1, NCHIP):
            cx, cy = _chip_at(x, y, s)
            cp = pltpu.make_async_remote_copy(
                src_ref=pair_s.at[mine], dst_ref=chip_s.at[s, mine], send_sem=send_sem.at[s],
                recv_sem=recv_sem.at[s], device_id=(cx, cy, c), device_id_type=MESH)
            cp.start()
            cps.append(cp)
        chip_s[0] = pair_s[...]
        for cp in cps:
            cp.wait()
        cps = []
        for s in range(1, NCHIP):
            cp = pltpu.make_async_remote_copy(
                src_ref=chip_s.at[s, mine], dst_ref=chip_s.at[s, mine], send_sem=send_sem.at[NCHIP + s],
                recv_sem=recv_sem.at[NCHIP + s], device_id=(x, y, 1 - c), device_id_type=MESH)
            cp.start()
            cps.append(cp)
        for cp in cps:
            cp.wait()
        tot = chip_s[j]
        for k in range(1, NCHIP):
            tot = tot + chip_s[k ^ j]
        g_ref[...] = tot
        d, mn, vn = _adamw_math(wmv_ref[0], tot, wmv_ref[1], wmv_ref[2])
        d_ref[...] = d
        mo_ref[...] = mn
        vo_ref[...] = vn
        job.mid(jin, jout, jsem)
        job.finish(jin, jout, jsem)

    vm = pl.BlockSpec(memory_space=pltpu.VMEM)
    res = pl.pallas_call(
        body, name="small_allreduce_adamw",
        in_specs=[vm] * 2 + [ANY] * ji, out_specs=[vm] * 4 + [ANY] * jo,
        out_shape=[SDS(shape, F32)] * 4 + job.out_shapes,
        scratch_shapes=[pltpu.VMEM(shape, F32), pltpu.VMEM(shape, F32), pltpu.VMEM((NCHIP,) + shape, F32),
                        pltpu.SemaphoreType.DMA((2 * NCHIP,)), pltpu.SemaphoreType.DMA((2 * NCHIP,))] + job.sems,
        input_output_aliases={2 + a: 4 + b for a, b in job.aliases.items()},
        compiler_params=pltpu.CompilerParams(has_side_effects=True),
    )(sp, wmv, *job.ins)
    return res[:4], res[4:]


def _pack_small(first, mix, ln_g, ln_b, b_s, lbt, hn, ffn, fin, w_s):
    rows = [first.reshape(1, D), mix.reshape(1, D), ln_g.reshape(1, D), ln_b.reshape(1, D),
            b_s.reshape(1, D), lbt.reshape(2, D), hn.reshape(1, D), ffn.reshape(1, D), fin.reshape(1, D),
            jnp.zeros((6, D), F32)]
    return jnp.concatenate(rows + [w_s.reshape(NG, GCH, GCH).transpose(1, 0, 2).reshape(GCH, D)], axis=0)


def _unpack_small(p):
    w_s = p[16:].reshape(GCH, NG, GCH).transpose(1, 0, 2).reshape(1, NG, GCH, GCH)
    return dict(norm_mix_g=p[1:2], gmlp_ln_g=p[2:3], gmlp_ln_b=p[3:4], gmlp_b_s=p[4].reshape(1, NG, GCH),
                hgrn_lb_table=p[5:7], hgrn_norm_g=p[7:8], norm_ffn_g=p[8:9], norm_final_g=p[9],
                gmlp_w_s=w_s)


SMALL = ("norm_mix_g", "gmlp_ln_g", "gmlp_ln_b", "gmlp_w_s", "gmlp_b_s", "hgrn_lb_table", "hgrn_norm_g",
         "norm_ffn_g", "norm_final_g")
BIG = ("w_in", "w_gate_up", "w_branch_a", "w_branch_b", "w_out", "w_down")
ORDER = ("norm_mix_g", "w_in", "gmlp_ln_g", "gmlp_ln_b", "gmlp_w_s", "gmlp_b_s", "hgrn_lb_table",
         "hgrn_norm_g", "w_branch_a", "w_branch_b", "w_out", "norm_ffn_g", "w_gate_up", "w_down",
         "norm_final_g")


def kernel(x, norm_mix_g, w_in, gmlp_ln_g, gmlp_ln_b, gmlp_w_s, gmlp_b_s, hgrn_lb_table, hgrn_norm_g, w_branch_a, w_branch_b, w_out, norm_ffn_g, w_gate_up, w_down, norm_final_g, loss_target, m_norm_mix_g, m_w_in, m_gmlp_ln_g, m_gmlp_ln_b, m_gmlp_w_s, m_gmlp_b_s, m_hgrn_lb_table, m_hgrn_norm_g, m_w_branch_a, m_w_branch_b, m_w_out, m_norm_ffn_g, m_w_gate_up, m_w_down, m_norm_final_g, v_norm_mix_g, v_w_in, v_gmlp_ln_g, v_gmlp_ln_b, v_gmlp_w_s, v_gmlp_b_s, v_hgrn_lb_table, v_hgrn_norm_g, v_w_branch_a, v_w_branch_b, v_w_out, v_norm_ffn_g, v_w_gate_up, v_w_down, v_norm_final_g):
    args = dict(locals())
    T = x.shape[1]
    xs = x.reshape(T, D)
    target = loss_target.reshape(T, D)
    big = {n: args[n].reshape(args[n].shape[1:]) for n in BIG}
    big_m = {n: args["m_" + n].reshape(args[n].shape[1:]) for n in BIG}
    big_v = {n: args["v_" + n].reshape(args[n].shape[1:]) for n in BIG}

    x_i, y_i, c_i = lax.axis_index("x"), lax.axis_index("y"), lax.axis_index("c")
    place = jnp.stack([2 * x_i + y_i, c_i]).astype(jnp.int32)
    def by_shape(names):
        groups = []
        for n in names:
            if groups and big[groups[-1][0]].shape == big[n].shape:
                groups[-1].append(n)
            else:
                groups.append([n])
        return groups

    cast = {}
    for grp in by_shape(BIG):
        cast.update(zip(grp, _cast_shards("cast_" + grp[0], place, [big[n] for n in grp],
                                          paired=grp[0] == "w_gate_up")))
    tril = jnp.tril(jnp.ones((GCH, GCH), bool))
    wm = jnp.where(tril, gmlp_w_s[0], 0.0).astype(BF16)
    wm_t = jnp.swapaxes(wm, 1, 2)
    b_t = gmlp_b_s[0].T

    (proj, hb), w_in4, (w_a4, w_b4, w_out4, w_down4) = _proj_fwd(
        place, xs, norm_mix_g, cast["w_in"], [cast[n] for n in ("w_branch_a", "w_branch_b", "w_out", "w_down")])
    (ab,), _ = _gmlp_fwd(proj, gmlp_ln_g, gmlp_ln_b, wm, b_t)
    (o_raw, obb, st_before), (w_gu,) = _hgrn_fwd(
        proj, hgrn_lb_table, hgrn_norm_g, job=_gather_job([cast["w_gate_up"]]))
    w_a, w_b, w_o = (w.reshape(D, D) for w in (w_a4, w_b4, w_out4))
    (mgb, x1), _ = _merge_fwd(xs, ab, obb, proj, w_a, w_b, w_o)
    w_dn = w_down4.reshape(FF, D)
    act, dx2b, h2b, dgu, dx1, dx1b, acc_ffn = _ffn_fwd_bwd(
        x1, target, norm_ffn_g, norm_final_g.reshape(1, D), w_gu, w_dn)

    grads, owns, parts, halves, sibh = {}, {}, {}, {}, {}

    def pair_sums(names, sibs):
        sib_of = dict(zip(names, sibs))
        for grp in by_shape(names):
            o, p = _pair_sums("rs_pair_sum_" + grp[0], place, [grads[n] for n in grp], [sib_of[n] for n in grp])
            owns.update(zip(grp, o))
            parts.update(zip(grp, p))

    def chip_sums(names, got):
        rem_of = dict(zip(names, got))
        for grp in by_shape(names):
            h = _chip_sums("rs_chip_sum_" + grp[0], [owns[n] for n in grp], [rem_of[n] for n in grp])
            halves.update(zip(grp, h))

    ffn, mix = ("w_gate_up", "w_down"), ("w_branch_a", "w_branch_b", "w_out")
    grads["w_gate_up"], _ = _dw_gate_up(h2b, dgu)
    grads["w_down"], _ = _dw_down(act, dx2b)
    (dya, dyb, dproj), got = _merge_bwd(
        dx1b, ab, obb, proj, w_o, w_a, w_b, job=_pair_exchange_job([grads[n] for n in ffn]))
    pair_sums(ffn, got)
    grads["w_branch_a"], _ = _dw_square("dw_branch_a", ab, dya)
    grads["w_branch_b"], _ = _dw_square("dw_branch_b", obb, dyb)
    grads["w_out"], _ = _dw_square("dw_out", mgb, dx1b)
    (dproj, acc_hgrn), got = _hgrn_bwd(
        dproj, dyb, w_b, o_raw, proj, st_before, hgrn_lb_table, hgrn_norm_g,
        job=_join_jobs(_chip_exchange_job([parts[n] for n in ffn]), _pair_exchange_job([grads[n] for n in mix])))
    chip_sums(ffn, got[:2])
    pair_sums(mix, got[2:])
    dproj, acc_ln, dws, dmix = _gmlp_bwd(dproj, dya, w_a, proj, gmlp_ln_g, gmlp_ln_b, wm, wm_t, b_t)
    for_sibling, got = _dw_in_half(
        "dw_in_sibling_half", place, hb, dproj, False,
        job=_join_jobs(_share_halves_job([halves[n] for n in ffn]), _chip_exchange_job([parts[n] for n in mix])))
    sibh.update(zip(ffn, got[:2]))
    chip_sums(mix, got[2:])
    grads["w_in"], got = _dw_in_half(
        "dw_in_own_half", place, hb, dproj, True, job=_share_halves_job([for_sibling]))
    pair_sums(("w_in",), got)
    (grad_x, acc_mix), got = _proj_bwd(
        dproj, w_in4, xs, dx1, norm_mix_g,
        job=_join_jobs(_chip_exchange_job([parts["w_in"]]), _share_halves_job([halves[n] for n in mix])))
    chip_sums(("w_in",), got[:1])
    sibh.update(zip(mix, got[1:]))

    lbv = jax.nn.sigmoid(hgrn_lb_table[0] - hgrn_lb_table[1])
    d_t0 = jnp.sum(acc_hgrn[0], axis=0) * lbv * (1.0 - lbv)
    loss_row = jnp.zeros((D,), F32).at[0].set(jnp.sum(acc_ffn[0]))
    dws_m = jnp.where(tril[:, None, :], dws.reshape(GCH, NG, GCH), 0.0).transpose(1, 0, 2)
    db_s = jnp.sum(dmix.reshape(GCH, NG, GCH), axis=-1).T
    sp = _pack_small(loss_row, jnp.sum(acc_mix, 0), jnp.sum(acc_ln[0], 0), jnp.sum(acc_ln[1], 0), db_s,
                     jnp.stack([d_t0, -d_t0]), jnp.sum(acc_hgrn[1], 0), jnp.sum(acc_ffn[2], 0),
                     jnp.sum(acc_ffn[1], 0), dws_m)
    zero = jnp.zeros((D,), F32)

    def pack(prefix):
        a = lambda n: args[prefix + n]
        return _pack_small(zero, a("norm_mix_g"), a("gmlp_ln_g"), a("gmlp_ln_b"), a("gmlp_b_s"),
                           a("hgrn_lb_table"), a("hgrn_norm_g"), a("norm_ffn_g"), a("norm_final_g"),
                           a("gmlp_w_s"))

    packed, (sibh["w_in"],) = _small_allreduce_adamw(
        sp, jnp.stack([pack(""), pack("m_"), pack("v_")]), _share_halves_job([halves["w_in"]]))
    loss = packed[0][0, 0]
    small = [_unpack_small(p) for p in packed]
    out = {n: tuple(s[n] for s in small) for n in SMALL}
    for grp in by_shape(BIG):
        res = _adamws("adamw_" + grp[0], place, *[[d[n] for n in grp] for d in (big, halves, sibh, big_m, big_v)])
        for n, quad in zip(grp, res):
            out[n] = tuple(a.reshape(args[n].shape) for a in quad)
    return (loss, grad_x.reshape(x.shape), *[out[n][0] for n in ORDER], *[out[n][1] for n in ORDER],
            *[out[n][2] for n in ORDER], *[out[n][3] for n in ORDER])
```

```python
import functools
import math

import jax
import jax.numpy as jnp
from jax import lax
from jax.experimental import pallas as pl
from jax.experimental.pallas import tpu as pltpu

F32 = jnp.float32
BF16 = jnp.bfloat16
SDS = jax.ShapeDtypeStruct
MESH = pl.DeviceIdType.MESH
ANY = pl.BlockSpec(memory_space=pl.ANY)

D = 1024
NIN = 8
NG = 8
GCH = 128
NH = 8
HD = 128
HCH = 64
HGRN_HB = 8
HGRN_TOKENS = 256
GMLP_FWD_TOKENS = 512
GMLP_BWD_TOKENS = 256
HW = HGRN_HB * HD
DW_TOKENS = 2048
DW_IN_TOKENS = 4096
ELEMENTWISE_BLOCK_BYTES = 2 * 1024 * 1024
PROJ_OUT_SLOTS = 4
FF = 2816
FFS = 1408
NCHIP = 4
EPS = 1e-6
QSCALE = HD ** -0.5
GELU_C0 = math.sqrt(2.0 / math.pi)
GELU_C1 = 0.044715
LR, B1, B2, AEPS, WD, STEP = 0.001, 0.9, 0.999, 1e-08, 0.01, 10
VMEM_LIMIT_V7X = 56 * 1024 * 1024
SP_ROWS = 144


def _cparams(**kw):
    return pltpu.CompilerParams(vmem_limit_bytes=VMEM_LIMIT_V7X, **kw)


def _mm(a, b):
    return jnp.dot(a, b, preferred_element_type=F32)


def _mm_nt(a, b):
    return lax.dot_general(a, b, (((1,), (1,)), ((), ())), preferred_element_type=F32)


def _mm_tn(a, b):
    return lax.dot_general(a, b, (((0,), (0,)), ((), ())), preferred_element_type=F32)


def _rows8(x):
    r, c = x.shape
    return jnp.sum(x.reshape(r // 8, 8, c), axis=0)


def _mean(x):
    return jnp.mean(x, axis=-1, keepdims=True)


def _sigmoid(x):
    return 1.0 / (1.0 + jnp.exp(-x))


def _gelu(x):
    t = jnp.tanh(GELU_C0 * (x + GELU_C1 * x * x * x))
    return 0.5 * x * (1.0 + t), t


def _gelu_grad(x, t):
    return 0.5 * (1.0 + t) + 0.5 * x * (1.0 - t * t) * (GELU_C0 * (1.0 + 3.0 * GELU_C1 * x * x))


def _component_of(group):
    return jnp.where(group < 6, (group + 4) % 6, group)


def _proj_fwd(place, x, g_mix, w_in4, later):
    T = x.shape[0]
    tm = min(1024, T)
    ni = T // tm
    n = len(later)

    def body(pc_ref, x_ref, g_ref, *rest):
        proj_ref, h_ref, w_all = rest[1 + n:4 + n]
        gathered = rest[4 + n:4 + 2 * n]
        hs, wbuf, wsem, obuf, osem = rest[4 + 2 * n:9 + 2 * n]
        w_sems, later_sems = rest[9 + 2 * n:15 + 2 * n], rest[15 + 2 * n:]
        jp, i = pl.program_id(0), pl.program_id(1)
        w_cols = [w_all.at[:, :, pl.ds(k * D, D)] for k in range(2)]

        def w_copy(blk):
            cols = pl.ds(pl.multiple_of((blk % 2) * D, 128), D)
            return pltpu.make_async_copy(w_all.at[pc_ref[0] ^ (blk // 2), :, cols], wbuf.at[blk % 2],
                                         wsem.at[blk % 2])

        @pl.when((jp == 0) & (i == 0))
        def _():
            _gather_start(w_cols, w_sems)
            w_copy(jp).start()

        @pl.when(i == 0)
        def _():
            w_copy(jp).wait()

        @pl.when(jp == 0)
        def _():
            xv = x_ref[...]
            r = lax.rsqrt(_mean(xv * xv) + EPS)
            hb = (xv * r * g_ref[...]).astype(BF16)
            hs[i] = hb
            h_ref[...] = hb

        step = jp * ni + i
        slot = step % PROJ_OUT_SLOTS

        def o_copy(slot_):
            comp = 2 * (pc_ref[0] ^ (jp // 2)) + jp % 2
            return pltpu.make_async_copy(
                obuf.at[slot_], proj_ref.at[comp, pl.ds(pl.multiple_of(i * tm, 8), tm)], osem.at[slot_])

        @pl.when(step >= PROJ_OUT_SLOTS)
        def _():
            o_copy(slot).wait()

        obuf[slot] = _mm(hs[i], wbuf[jp % 2])
        o_copy(slot).start()

        @pl.when(step == NIN * ni - 1)
        def _():
            for k in range(PROJ_OUT_SLOTS):
                o_copy((slot + 1 + k) % PROJ_OUT_SLOTS).wait()

        for nxt in range(1, NIN):
            @pl.when((jp == nxt - 1) & (i == ni - 1))
            def _():
                if nxt >= 2:
                    _gather_land([w_cols[nxt % 2]], w_sems, nxt // 2, first=nxt % 2)
                if nxt == 5:
                    _gather_start(gathered, later_sems)
                if nxt == NIN - 1:
                    _gather_neighbours(gathered, later_sems)
                w_copy(jp + 1).start()

        @pl.when((jp == NIN - 1) & (i == ni - 1))
        def _():
            _gather_drain(w_cols, w_sems)
            _gather_finish(gathered, later_sems)

    tile = lambda jp, i, pc: (jnp.where(jp == 0, i, ni - 1), 0)
    res = pl.pallas_call(
        body, name="proj_fwd",
        grid_spec=pltpu.PrefetchScalarGridSpec(
            num_scalar_prefetch=1, grid=(NIN, ni),
            in_specs=[pl.BlockSpec((tm, D), tile), pl.BlockSpec((1, D), lambda jp, i, pc: (0, 0))] + [ANY] * (1 + n),
            out_specs=[ANY, pl.BlockSpec((tm, D), tile)] + [ANY] * (1 + n),
            scratch_shapes=[pltpu.VMEM((ni, tm, D), BF16), pltpu.VMEM((2, D, D), BF16),
                            pltpu.SemaphoreType.DMA((2,)), pltpu.VMEM((PROJ_OUT_SLOTS, tm, D), F32),
                            pltpu.SemaphoreType.DMA((PROJ_OUT_SLOTS,))] + _gather_sems(2) + _gather_sems(n)),
        out_shape=[SDS((NIN, T, D), F32), SDS((T, D), BF16), SDS(w_in4.shape, BF16)]
        + [SDS(a.shape, a.dtype) for a in later],
        input_output_aliases={3 + k: 2 + k for k in range(1 + n)},
        compiler_params=_cparams(has_side_effects=True),
    )(place, x, g_mix, w_in4, *later)
    return res[:2], res[2], res[3:]


def _chunks_abreast(x):
    return jnp.concatenate([x[GCH * ch:GCH * (ch + 1)] for ch in range(x.shape[0] // GCH)], axis=1)


def _chunks_stacked(x):
    return jnp.concatenate([x[:, GCH * ch:GCH * (ch + 1)] for ch in range(x.shape[1] // GCH)], axis=0)


def _layer_norm_stats(gv):
    mu = _mean(gv)
    xc = gv - mu
    rs = lax.rsqrt(_mean(xc * xc) + EPS)
    return xc * rs, rs


def _gmlp_fwd(proj, ln_g, ln_b, wm, b_t, job=None):
    T = proj.shape[1]
    tm = min(GMLP_FWD_TOKENS, T)

    def body(u_ref, v_ref, lg_ref, lb_ref, wm_ref, bt_ref, a_ref, a_s):
        gu, _ = _gelu(u_ref[...])
        gv, _ = _gelu(v_ref[...])
        vhat, _ = _layer_norm_stats(gv)
        vnb = (vhat * lg_ref[...] + lb_ref[...]).astype(BF16)
        for g in range(NG):
            cols = slice(128 * g, 128 * (g + 1))
            mixed = _mm(wm_ref[g], _chunks_abreast(vnb[:, cols])) + bt_ref[:, g:g + 1]
            a_s[:, cols] = gu[:, cols] * _chunks_stacked(mixed)
        a_ref[...] = a_s[...].astype(BF16)

    row = lambda i: (0, 0)
    return _call(
        body, name="gmlp_fwd", grid=(T // tm,), job=job, args=(proj, proj, ln_g, ln_b, wm, b_t),
        in_specs=[pl.BlockSpec((None, tm, D), lambda i: (0, i, 0)), pl.BlockSpec((None, tm, D), lambda i: (1, i, 0)),
                  pl.BlockSpec((1, D), row), pl.BlockSpec((1, D), row),
                  pl.BlockSpec((NG, GCH, GCH), lambda i: (0, 0, 0)), pl.BlockSpec((GCH, NG), row)],
        out_specs=[pl.BlockSpec((tm, D), lambda i: (i, 0))],
        out_shape=[SDS((T, D), BF16)],
        scratch_shapes=[pltpu.VMEM((tm, D), F32)])


def _cumsum64(x, row):
    for s in (1, 2, 4, 8, 16, 32):
        x = x + jnp.where(row >= s, pltpu.roll(x, s, 0), 0.0)
    return x


def _revcumsum64(x, row):
    n = x.shape[0]
    for s in (1, 2, 4, 8, 16, 32):
        x = x + jnp.where(row < HCH - s, pltpu.roll(x, n - s, 0), 0.0)
    return x


def _head_mean(x):
    parts = [jnp.broadcast_to(_mean(x[:, HD * h:HD * (h + 1)]), (x.shape[0], HD)) for h in range(x.shape[1] // HD)]
    return jnp.concatenate(parts, axis=1)


def _seg_sum(x):
    n, c = x.shape
    s = jnp.sum(x.reshape(n // HCH, HCH, c), axis=1, keepdims=True)
    return jnp.broadcast_to(s, (n // HCH, HCH, c)).reshape(n, c)


def _seg_row(x, idx):
    n, c = x.shape
    x3 = x.reshape(n // HCH, HCH, c)
    return jnp.broadcast_to(x3[:, idx:idx + 1, :], x3.shape).reshape(n, c)


def _hgrn_gates(fl, lbv, row):
    s = _sigmoid(fl)
    f = lbv + (1.0 - lbv) * s
    a = _cumsum64(jnp.log(f), row)
    return s, f, a, _seg_row(a, HCH // 2 - 1), _seg_row(a, HCH - 1)


def _hgrn_fwd(proj, lb_table, norm_g, job=None):
    T = proj.shape[1]
    tb = min(HGRN_TOKENS, T)
    nc = tb // HCH

    def body(q_ref, fl_ref, v_ref, g_ref, lbt_ref, gn_ref, o_ref, ob_ref, stb_ref, st_s, o_s):
        @pl.when(pl.program_id(1) == 0)
        def _():
            st_s[...] = jnp.zeros_like(st_s)

        row = lax.broadcasted_iota(jnp.int32, (tb, HW), 0) & (HCH - 1)
        lbv = _sigmoid(lbt_ref[0:1, :] - lbt_ref[1:2, :])
        _, f, a, a_mid, a_last = _hgrn_gates(fl_ref[...], lbv, row)
        k = 1.0 - f
        qs = q_ref[...] * QSCALE
        q_in = (qs * jnp.exp(a - a_mid)).astype(BF16)
        k_in = (k * jnp.exp(a_mid - a)).astype(BF16)
        q_a = (qs * jnp.exp(a)).astype(BF16)
        k_d = (k * jnp.exp(a_last - a)).astype(BF16)
        dec = jnp.exp(a_last)
        vb = v_ref[...].astype(BF16)
        tri = (lax.broadcasted_iota(jnp.int32, (HCH, HCH), 0)
               >= lax.broadcasted_iota(jnp.int32, (HCH, HCH), 1))
        for c in range(nc):
            sl = slice(HCH * c, HCH * (c + 1))
            for hh in range(HGRN_HB):
                hs = slice(HD * hh, HD * (hh + 1))
                st = st_s[hh]
                stb_ref[hh, c] = st
                sc = jnp.where(tri, _mm_nt(q_in[sl, hs], k_in[sl, hs]), 0.0)
                o_s[sl, hs] = _mm(sc.astype(BF16), vb[sl, hs]) + _mm_nt(q_a[sl, hs], st.astype(BF16))
                d64 = dec[sl, hs]
                st_s[hh] = st * jnp.concatenate([d64, d64], axis=0) + _mm_tn(vb[sl, hs], k_d[sl, hs])
        o = o_s[...]
        r = lax.rsqrt(_head_mean(o * o) + EPS)
        g = g_ref[...]
        o_ref[...] = o
        ob_ref[...] = (o * r * gn_ref[...] * (g * _sigmoid(g))).astype(BF16)

    def col(off):
        return pl.BlockSpec((None, tb, HW), lambda h, cb: (off, cb, h))

    return _call(
        body, name="hgrn_fwd", grid=(NH // HGRN_HB, T // tb), job=job,
        args=(proj, proj, proj, proj, lb_table, norm_g),
        in_specs=[col(2), col(3), col(4), col(5),
                  pl.BlockSpec((2, HW), lambda h, cb: (0, h)), pl.BlockSpec((1, HW), lambda h, cb: (0, h))],
        out_specs=[pl.BlockSpec((tb, HW), lambda h, cb: (cb, h)), pl.BlockSpec((tb, HW), lambda h, cb: (cb, h)),
                   pl.BlockSpec((HGRN_HB, nc, HD, HD), lambda h, cb: (h, cb, 0, 0))],
        out_shape=[SDS((T, D), F32), SDS((T, D), BF16), SDS((NH, T // HCH, HD, HD), F32)],
        scratch_shapes=[pltpu.VMEM((HGRN_HB, HD, HD), F32), pltpu.VMEM((tb, HW), F32)])


def _merge_fwd(x, ab, ob, proj, w_a, w_b, w_out, job=None):
    T = x.shape[0]
    tm = min(512, T)

    def body(x_ref, ab_ref, ob_ref, ga_ref, gb_ref, wa_ref, wb_ref, wo_ref, mg_ref, x1_ref):
        ya = _mm(ab_ref[...], wa_ref[...])
        yb = _mm(ob_ref[...], wb_ref[...])
        merged = (_sigmoid(ga_ref[...]) * ya + _sigmoid(gb_ref[...]) * yb).astype(BF16)
        mg_ref[...] = merged
        x1_ref[...] = x_ref[...] + _mm(merged, wo_ref[...])

    t = lambda i: (i, 0)
    w = lambda i: (0, 0)
    return _call(
        body, name="merge_fwd", grid=(T // tm,), job=job, args=(x, ab, ob, proj, proj, w_a, w_b, w_out),
        in_specs=[pl.BlockSpec((tm, D), t), pl.BlockSpec((tm, D), t), pl.BlockSpec((tm, D), t),
                  pl.BlockSpec((None, tm, D), lambda i: (6, i, 0)), pl.BlockSpec((None, tm, D), lambda i: (7, i, 0)),
                  pl.BlockSpec((D, D), w), pl.BlockSpec((D, D), w), pl.BlockSpec((D, D), w)],
        out_specs=[pl.BlockSpec((tm, D), t)] * 2,
        out_shape=[SDS((T, D), BF16), SDS((T, D), F32)])


def _ffn_fwd_bwd(x1, target, g_ffn, g_fin, w_gu, w_down):
    T = x1.shape[0]
    tm = min(256, T)
    inv_d = 1.0 / D

    def body(x1_ref, tg_ref, gf_ref, gn_ref, wgu_ref, wd_ref,
             act_ref, dx2b_ref, h2b_ref, dgu_ref, dx1_ref, dx1b_ref, acc_ref):
        @pl.when(pl.program_id(0) == 0)
        def _():
            acc_ref[...] = jnp.zeros_like(acc_ref)

        x1v = x1_ref[...]
        gf = gf_ref[...]
        gn = gn_ref[...]
        rr1 = lax.rsqrt(_mean(x1v * x1v) + EPS)
        x1n = x1v * rr1
        h2b = (x1n * gf).astype(BF16)
        h2b_ref[...] = h2b
        gate = _mm(h2b, wgu_ref[0])
        up = _mm(h2b, wgu_ref[1])
        sg = _sigmoid(gate)
        si = gate * sg
        act = (si * up).astype(BF16)
        act_ref[...] = act
        x2 = x1v + _mm(act, wd_ref[...])
        rr2 = lax.rsqrt(_mean(x2 * x2) + EPS)
        x2n = x2 * rr2
        e = x2n * gn - tg_ref[...]
        acc_ref[0] += _rows8(e * e) * (0.5 * inv_d)
        dy = e * inv_d
        acc_ref[1] += _rows8(dy * x2n)
        dxn = dy * gn
        dx2 = rr2 * (dxn - x2n * _mean(dxn * x2n))
        dx2b = dx2.astype(BF16)
        dx2b_ref[...] = dx2b
        dact = _mm_nt(dx2b, wd_ref[...])
        dgate = (dact * up * (sg * (1.0 + gate * (1.0 - sg)))).astype(BF16)
        dup = (dact * si).astype(BF16)
        dgu_ref[0] = dgate
        dgu_ref[1] = dup
        dh2 = _mm_nt(dgate, wgu_ref[0]) + _mm_nt(dup, wgu_ref[1])
        acc_ref[2] += _rows8(dh2 * x1n)
        dxn1 = dh2 * gf
        dx1 = dx2 + rr1 * (dxn1 - x1n * _mean(dxn1 * x1n))
        dx1_ref[...] = dx1
        dx1b_ref[...] = dx1.astype(BF16)

    t = lambda i: (i, 0)
    w = lambda i: (0, 0)
    one = pl.Buffered(1)
    return pl.pallas_call(
        body, name="ffn_fwd_bwd", grid=(T // tm,),
        in_specs=[pl.BlockSpec((tm, D), t), pl.BlockSpec((tm, D), t),
                  pl.BlockSpec((1, D), w), pl.BlockSpec((1, D), w),
                  pl.BlockSpec((2, D, FF), lambda i: (0, 0, 0), pipeline_mode=one),
                  pl.BlockSpec((FF, D), w, pipeline_mode=one)],
        out_specs=[pl.BlockSpec((tm, FF), t), pl.BlockSpec((tm, D), t), pl.BlockSpec((tm, D), t),
                   pl.BlockSpec((2, tm, FF), lambda i: (0, i, 0)),
                   pl.BlockSpec((tm, D), t), pl.BlockSpec((tm, D), t),
                   pl.BlockSpec((3, 8, D), lambda i: (0, 0, 0))],
        out_shape=[SDS((T, FF), BF16), SDS((T, D), BF16), SDS((T, D), BF16),
                   SDS((2, T, FF), BF16), SDS((T, D), F32), SDS((T, D), BF16),
                   SDS((3, 8, D), F32)],
        compiler_params=_cparams(),
    )(x1, target, g_ffn, g_fin, w_gu, w_down)


def _merge_bwd(dx1b, ab, ob, proj, w_out, w_a, w_b, job=None):
    T = dx1b.shape[0]
    tm = min(512, T)

    def body(dx_ref, ab_ref, ob_ref, ga_ref, gb_ref, wo_ref, wa_ref, wb_ref, dya_ref, dyb_ref, dp_ref):
        dm = _mm_nt(dx_ref[...], wo_ref[...])
        sa = _sigmoid(ga_ref[...])
        sb = _sigmoid(gb_ref[...])
        dya_ref[...] = (dm * sa).astype(BF16)
        dyb_ref[...] = (dm * sb).astype(BF16)
        dp_ref[0] = (dm * _mm(ab_ref[...], wa_ref[...]) * sa * (1.0 - sa)).astype(BF16)
        dp_ref[1] = (dm * _mm(ob_ref[...], wb_ref[...]) * sb * (1.0 - sb)).astype(BF16)

    t = lambda i: (i, 0)
    w = lambda i: (0, 0)
    return _call(
        body, name="merge_bwd", grid=(T // tm,),
        in_specs=[pl.BlockSpec((tm, D), t), pl.BlockSpec((tm, D), t), pl.BlockSpec((tm, D), t),
                  pl.BlockSpec((None, tm, D), lambda i: (6, i, 0)), pl.BlockSpec((None, tm, D), lambda i: (7, i, 0)),
                  pl.BlockSpec((D, D), w), pl.BlockSpec((D, D), w), pl.BlockSpec((D, D), w)],
        out_specs=[pl.BlockSpec((tm, D), t)] * 2 + [pl.BlockSpec((2, tm, D), lambda i: (3, i, 0))],
        out_shape=[SDS((T, D), BF16), SDS((T, D), BF16), SDS((NIN, T, D), BF16)],
        args=(dx1b, ab, ob, proj, proj, w_out, w_a, w_b), job=job)


def _hgrn_bwd(dproj, dyb, w_b, o_raw, proj, st_before, lb_table, norm_g, job=None):
    T = dyb.shape[0]
    tb = min(HGRN_TOKENS, T)
    nc = tb // HCH
    nb = T // tb

    def body(dp_in, dyb_ref, wb_ref, o_ref, q_ref, fl_ref, v_ref, g_ref, stb_ref, lbt_ref, gn_ref,
             dp_ref, acc_ref, dst_s, dqin_s, dqa_s, dkin_s, dkd_s, dv_s, ddec_s):
        del dp_in

        @pl.when(pl.program_id(1) == 0)
        def _():
            dst_s[...] = jnp.zeros_like(dst_s)
            acc_ref[...] = jnp.zeros_like(acc_ref)

        row = lax.broadcasted_iota(jnp.int32, (tb, HW), 0) & (HCH - 1)
        gn = gn_ref[...]
        lbv = _sigmoid(lbt_ref[0:1, :] - lbt_ref[1:2, :])
        o = o_ref[...]
        r = lax.rsqrt(_head_mean(o * o) + EPS)
        on = o * r
        g = g_ref[...]
        sgm = _sigmoid(g)
        dob_v = _mm_nt(dyb_ref[...], wb_ref[...])
        dp_ref[3] = (dob_v * on * gn * (sgm * (1.0 + g * (1.0 - sgm)))).astype(BF16)
        do_n = dob_v * (g * sgm)
        acc_ref[1] += _rows8(do_n * on)
        dxn = do_n * gn
        do = (r * (dxn - on * _head_mean(dxn * on))).astype(BF16)
        s, f, a, a_mid, a_last = _hgrn_gates(fl_ref[...], lbv, row)
        k = 1.0 - f
        qs = q_ref[...] * QSCALE
        e_q = jnp.exp(a - a_mid)
        e_k = jnp.exp(a_mid - a)
        e_a = jnp.exp(a)
        e_l = jnp.exp(a_last - a)
        dec = jnp.exp(a_last)
        q_in = qs * e_q
        k_in = k * e_k
        q_a = qs * e_a
        k_d = k * e_l
        q_inb, k_inb, q_ab, k_db = (z.astype(BF16) for z in (q_in, k_in, q_a, k_d))
        vb = v_ref[...].astype(BF16)
        tri = (lax.broadcasted_iota(jnp.int32, (HCH, HCH), 0)
               >= lax.broadcasted_iota(jnp.int32, (HCH, HCH), 1))
        for c in reversed(range(nc)):
            sl = slice(HCH * c, HCH * (c + 1))
            for hh in range(HGRN_HB):
                hs = slice(HD * hh, HD * (hh + 1))
                stp = stb_ref[hh, c]
                dst = dst_s[hh]
                dstb = dst.astype(BF16)
                do_c = do[sl, hs]
                v_c = vb[sl, hs]
                dqa_s[sl, hs] = _mm(do_c, stp.astype(BF16))
                dkd_s[sl, hs] = _mm(v_c, dstb)
                ddec_s[sl, hs] = jnp.broadcast_to(jnp.sum(dst * stp, axis=0, keepdims=True), (HCH, HD))
                sc = jnp.where(tri, _mm_nt(q_inb[sl, hs], k_inb[sl, hs]), 0.0).astype(BF16)
                dsc = jnp.where(tri, _mm_nt(do_c, v_c), 0.0).astype(BF16)
                dv_s[sl, hs] = _mm_nt(k_db[sl, hs], dstb) + _mm_tn(sc, do_c)
                dqin_s[sl, hs] = _mm(dsc, k_inb[sl, hs])
                dkin_s[sl, hs] = _mm_tn(dsc, q_inb[sl, hs])
                d64 = dec[sl, hs]
                dst_s[hh] = dst * jnp.concatenate([d64, d64], axis=0) + _mm_tn(do_c, q_ab[sl, hs])
        dq_in = dqin_s[...]
        dq_a = dqa_s[...]
        dk_in = dkin_s[...]
        dk_d = dkd_s[...]
        dp_ref[0] = ((dq_in * e_q + dq_a * e_a) * QSCALE).astype(BF16)
        dp_ref[2] = dv_s[...].astype(BF16)
        tq = dq_in * q_in
        tk = dk_in * k_in
        td = dk_d * k_d
        d_a = tq + dq_a * q_a - tk - td
        d_a = d_a + jnp.where(row == HCH // 2 - 1, _seg_sum(tk - tq), 0.0)
        d_a = d_a + jnp.where(row == HCH - 1, _seg_sum(td) + ddec_s[...] * dec, 0.0)
        dlf = _revcumsum64(d_a, row)
        df = dlf / f - (dk_in * e_k + dk_d * e_l)
        dp_ref[1] = (df * (1.0 - lbv) * s * (1.0 - s)).astype(BF16)
        acc_ref[0] += _rows8(df * (1.0 - s))

    def col(off):
        return pl.BlockSpec((None, tb, HW), lambda h, cb: (off, nb - 1 - cb, h))

    hb = lambda h, cb: (nb - 1 - cb, h)
    return _call(
        body, name="hgrn_bwd", grid=(NH // HGRN_HB, nb), job=job,
        args=(dproj, dyb, w_b, o_raw, proj, proj, proj, proj, st_before, lb_table, norm_g),
        in_specs=[ANY, pl.BlockSpec((tb, D), lambda h, cb: (nb - 1 - cb, 0)),
                  pl.BlockSpec((HW, D), lambda h, cb: (h, 0)), pl.BlockSpec((tb, HW), hb),
                  col(2), col(3), col(4), col(5),
                  pl.BlockSpec((HGRN_HB, nc, HD, HD), lambda h, cb: (h, nb - 1 - cb, 0, 0)),
                  pl.BlockSpec((2, HW), lambda h, cb: (0, h)), pl.BlockSpec((1, HW), lambda h, cb: (0, h))],
        out_specs=[pl.BlockSpec((4, tb, HW), lambda h, cb: (0, nb - 1 - cb, h)),
                   pl.BlockSpec((2, 8, HW), lambda h, cb: (0, 0, h))],
        out_shape=[SDS(dproj.shape, BF16), SDS((2, 8, D), F32)],
        scratch_shapes=[pltpu.VMEM((HGRN_HB, HD, HD), F32)] + [pltpu.VMEM((tb, HW), F32)] * 6,
        aliases={0: 0})


def _gmlp_bwd(dproj, dya, w_a, proj, ln_g, ln_b, wm, wm_t, b_t):
    T = dya.shape[0]
    tm = min(GMLP_BWD_TOKENS, T)

    def body(dp_in, dya_ref, wa_ref, u_ref, v_ref, lg_ref, lb_ref, wm_ref, wmt_ref, bt_ref,
             dp_ref, acc_ref, dws_ref, dmix_ref, du_s, dvn_s):
        del dp_in

        @pl.when(pl.program_id(0) == 0)
        def _():
            acc_ref[...] = jnp.zeros_like(acc_ref)
            dws_ref[...] = jnp.zeros_like(dws_ref)
            dmix_ref[...] = jnp.zeros_like(dmix_ref)

        u = u_ref[...]
        v = v_ref[...]
        lg = lg_ref[...]
        gu, t_u = _gelu(u)
        gv, t_v = _gelu(v)
        vhat, rs = _layer_norm_stats(gv)
        vnb = (vhat * lg + lb_ref[...]).astype(BF16)
        da_v = _mm_nt(dya_ref[...], wa_ref[...])
        for g in range(NG):
            cols = slice(128 * g, 128 * (g + 1))
            vng = _chunks_abreast(vnb[:, cols])
            mixed = _mm(wm_ref[g], vng) + bt_ref[:, g:g + 1]
            dag = _chunks_abreast(da_v[:, cols])
            dmx = dag * _chunks_abreast(gu[:, cols])
            du_s[:, cols] = _chunks_stacked(dag * mixed)
            dmxb = dmx.astype(BF16)
            dws_ref[:, cols] += _mm_nt(dmxb, vng)
            dmix_ref[:, cols] += sum(dmx[:, GCH * ch:GCH * (ch + 1)] for ch in range(tm // GCH))
            dvn_s[:, cols] = _chunks_stacked(_mm(wmt_ref[g], dmxb))
        dp_ref[0] = (du_s[...] * _gelu_grad(u, t_u)).astype(BF16)
        dvn = dvn_s[...]
        acc_ref[0] += _rows8(dvn * vhat)
        acc_ref[1] += _rows8(dvn)
        dvh = dvn * lg
        dgv = rs * (dvh - _mean(dvh) - vhat * _mean(dvh * vhat))
        dp_ref[1] = (dgv * _gelu_grad(v, t_v)).astype(BF16)

    row = lambda i: (0, 0)
    w3 = lambda i: (0, 0, 0)
    return pl.pallas_call(
        body, name="gmlp_bwd", grid=(T // tm,),
        in_specs=[ANY, pl.BlockSpec((tm, D), lambda i: (i, 0)), pl.BlockSpec((D, D), row),
                  pl.BlockSpec((None, tm, D), lambda i: (0, i, 0)), pl.BlockSpec((None, tm, D), lambda i: (1, i, 0)),
                  pl.BlockSpec((1, D), row), pl.BlockSpec((1, D), row),
                  pl.BlockSpec((NG, GCH, GCH), w3), pl.BlockSpec((NG, GCH, GCH), w3),
                  pl.BlockSpec((GCH, NG), row)],
        out_specs=[pl.BlockSpec((2, tm, D), lambda i: (2, i, 0)),
                   pl.BlockSpec((2, 8, D), w3), pl.BlockSpec((GCH, D), row), pl.BlockSpec((GCH, D), row)],
        out_shape=[SDS(dproj.shape, BF16), SDS((2, 8, D), F32), SDS((GCH, D), F32), SDS((GCH, D), F32)],
        scratch_shapes=[pltpu.VMEM((tm, D), F32), pltpu.VMEM((tm, D), F32)],
        input_output_aliases={0: 0},
        compiler_params=_cparams(),
    )(dproj, dya, w_a, proj, proj, ln_g, ln_b, wm, wm_t, b_t)


def _proj_bwd(dproj, w_in4, x, dx1, g_mix, job=None):
    T = x.shape[0]
    tm = min(256, T)
    order = (2, 3, 4, 5, 0, 1, 6, 7)

    def body(dp_ref, w_ref, x_ref, dx1_ref, g_ref, gx_ref, acc_ref):
        @pl.when(pl.program_id(0) == 0)
        def _():
            acc_ref[...] = jnp.zeros_like(acc_ref)

        dh = None
        for m, og in enumerate(order):
            part = _mm_nt(dp_ref[m], w_ref[og // 2, :, D * (og % 2):D * (og % 2 + 1)])
            dh = part if dh is None else dh + part
        xv = x_ref[...]
        r = lax.rsqrt(_mean(xv * xv) + EPS)
        xn = xv * r
        acc_ref[...] += _rows8(dh * xn)
        dxn = dh * g_ref[...]
        gx_ref[...] = dx1_ref[...] + r * (dxn - xn * _mean(dxn * xn))

    t = lambda i: (i, 0)
    return _call(
        body, name="proj_bwd", grid=(T // tm,),
        in_specs=[pl.BlockSpec((NIN, tm, D), lambda i: (0, i, 0)),
                  pl.BlockSpec((NCHIP, D, 2 * D), lambda i: (0, 0, 0), pipeline_mode=pl.Buffered(1)),
                  pl.BlockSpec((tm, D), t), pl.BlockSpec((tm, D), t), pl.BlockSpec((1, D), lambda i: (0, 0))],
        out_specs=[pl.BlockSpec((tm, D), t), pl.BlockSpec((8, D), lambda i: (0, 0))],
        out_shape=[SDS((T, D), F32), SDS((8, D), F32)],
        args=(dproj, w_in4, x, dx1, g_mix), job=job)


def _dw_call(name, a, b, a_spec, b_spec, o_spec, out_shape, nblk, tt, job=None, prefetch=None):
    T = a.shape[-2]

    def body(*refs):
        a_ref, b_ref, o_ref = refs[-3:]

        @pl.when(pl.program_id(1) == 0)
        def _():
            o_ref[...] = jnp.zeros_like(o_ref)
        o_ref[...] += _mm_tn(a_ref[...], b_ref[...])

    (out,), job_out = _call(
        body, name=name, grid=(nblk, T // tt), in_specs=[a_spec, b_spec], out_specs=[o_spec],
        out_shape=[out_shape], args=(a, b), job=job, prefetch=prefetch)
    return out, job_out


def _dw_in_half(name, place, hb, dproj, mine, job=None):
    tt = min(DW_IN_TOKENS if mine else DW_TOKENS, hb.shape[0])

    def comp(k, pc):
        return _component_of(2 * k + (pc[1] if mine else 1 - pc[1]))

    return _dw_call(
        name, hb, dproj,
        pl.BlockSpec((tt, D), lambda k, t, pc: (t, 0)),
        pl.BlockSpec((None, tt, D), lambda k, t, pc: (comp(k, pc), t, 0)),
        pl.BlockSpec((None, D, D), lambda k, t, pc: (k, 0, 0)),
        SDS((NCHIP, D, D), F32), NCHIP, tt, job, place)


def _dw_gate_up(h2b, dgu, job=None):
    tt = min(DW_TOKENS, h2b.shape[0])
    return _dw_call(
        "dw_gate_up", h2b, dgu,
        pl.BlockSpec((tt, D), lambda k, t: (t, 0)),
        pl.BlockSpec((None, tt, FFS), lambda k, t: (k // 2, t, k % 2)),
        pl.BlockSpec((None, D, FFS), lambda k, t: (k, 0, 0)),
        SDS((NCHIP, D, FFS), F32), NCHIP, tt, job)


def _dw_down(act, dx2b, job=None):
    tt = min(DW_TOKENS, act.shape[0])
    g, job_out = _dw_call(
        "dw_down", act, dx2b,
        pl.BlockSpec((tt, FFS), lambda k, t: (t, k)),
        pl.BlockSpec((tt, D), lambda k, t: (t, 0)),
        pl.BlockSpec((FFS, D), lambda k, t: (k, 0)),
        SDS((FF, D), F32), 2, tt, job)
    return g.reshape(NCHIP, FF // NCHIP, D), job_out


def _dw_square(name, a, b, job=None):
    tt = min(DW_TOKENS, a.shape[0])
    g, job_out = _dw_call(
        name, a, b,
        pl.BlockSpec((tt, D), lambda k, t: (t, 0)), pl.BlockSpec((tt, D), lambda k, t: (t, 0)),
        pl.BlockSpec((D, D), lambda k, t: (0, 0)), SDS((D, D), F32), 1, tt, job)
    return g.reshape(NCHIP, D // NCHIP, D), job_out


def _place():
    x, y, c = lax.axis_index("x"), lax.axis_index("y"), lax.axis_index("c")
    return x, y, c, 2 * x + y


def _chip_at(x, y, s):
    return x ^ (s >> 1), y ^ (s & 1)


class _Job:
    def __init__(self, ins, out_shapes, sems, start, finish, aliases=None, mid=None):
        self.ins, self.out_shapes, self.sems = list(ins), list(out_shapes), list(sems)
        self.start, self.finish, self.aliases = start, finish, dict(aliases or {})
        self.mid = mid if mid is not None else (lambda ins, outs, sems: None)


def _join_jobs(*jobs):
    def cut(refs, sizes):
        out, at = [], 0
        for n in sizes:
            out.append(refs[at:at + n])
            at += n
        return out

    ni = [len(j.ins) for j in jobs]
    no = [len(j.out_shapes) for j in jobs]
    ns = [len(j.sems) for j in jobs]

    def run(which):
        def go(ins, outs, sems):
            for j, a, b, c in zip(jobs, cut(ins, ni), cut(outs, no), cut(sems, ns)):
                getattr(j, which)(a, b, c)
        return go

    aliases = {}
    for k, j in enumerate(jobs):
        for a, b in j.aliases.items():
            aliases[sum(ni[:k]) + a] = sum(no[:k]) + b
    return _Job([a for j in jobs for a in j.ins], [o for j in jobs for o in j.out_shapes],
                [s for j in jobs for s in j.sems], run("start"), run("finish"), aliases, run("mid"))


def _call(body, *, name, grid, in_specs, out_specs, out_shape, args, scratch_shapes=(), aliases=None,
          job=None, prefetch=None):
    n_in, n_out, n_scr = len(in_specs), len(out_specs), len(scratch_shapes)
    npf = 0 if prefetch is None else 1
    job = job if job is not None else _Job([], [], [], lambda *a: None, lambda *a: None)
    ji, jo = len(job.ins), len(job.out_shapes)
    steps = math.prod(grid)

    def wrapped(*refs):
        pf, refs = refs[:npf], refs[npf:]
        ins, jin = refs[:n_in], refs[n_in:n_in + ji]
        o0 = n_in + ji
        outs, jout = refs[o0:o0 + n_out], refs[o0 + n_out:o0 + n_out + jo]
        s0 = o0 + n_out + jo
        scr, jsem = refs[s0:s0 + n_scr], refs[s0 + n_scr:]
        step = functools.reduce(lambda acc, ag: acc * ag[1] + pl.program_id(ag[0]), enumerate(grid), 0)
        if ji or jo:
            @pl.when(step == 0)
            def _():
                job.start(jin, jout, jsem)

        body(*pf, *ins, *outs, *scr)

        if ji or jo:
            @pl.when(step == steps // 2)
            def _():
                job.mid(jin, jout, jsem)

            @pl.when(step == steps - 1)
            def _():
                job.finish(jin, jout, jsem)

    io = {npf + a: b for a, b in dict(aliases or {}).items()}
    io.update({npf + n_in + a: n_out + b for a, b in job.aliases.items()})
    kw = dict(in_specs=list(in_specs) + [ANY] * ji, out_specs=list(out_specs) + [ANY] * jo,
              scratch_shapes=list(scratch_shapes) + job.sems)
    if npf:
        kw = dict(grid_spec=pltpu.PrefetchScalarGridSpec(num_scalar_prefetch=1, grid=grid, **kw))
    else:
        kw["grid"] = grid
    res = pl.pallas_call(
        wrapped, name=name, out_shape=list(out_shape) + job.out_shapes, input_output_aliases=io,
        compiler_params=_cparams(has_side_effects=bool(ji or jo)), **kw,
    )(*(() if prefetch is None else (prefetch,)), *args, *job.ins)
    return list(res[:n_out]), list(res[n_out:])


def _cast_shards(name, place, ws, paired=False):
    n = len(ws)
    rows, cols = ws[0].shape
    tr = 352 if rows % 352 == 0 else 256
    shape = (2, rows, 2 * cols) if paired else (NCHIP, rows, cols)
    mine = (lambda i, pc: (pc[0] // 2, i, pc[0] % 2)) if paired else (lambda i, pc: (pc[0], i, 0))

    def body(pc_ref, *refs):
        del pc_ref
        for w_ref, o_ref in zip(refs[:n], refs[n:]):
            o_ref[...] = w_ref[...].astype(BF16)

    return pl.pallas_call(
        body, name=name,
        grid_spec=pltpu.PrefetchScalarGridSpec(
            num_scalar_prefetch=1, grid=(rows // tr,),
            in_specs=[pl.BlockSpec((tr, cols), lambda i, pc: (i, 0))] * n,
            out_specs=[pl.BlockSpec((None, tr, cols), mine)] * n),
        out_shape=[SDS(shape, BF16)] * n,
        compiler_params=_cparams(),
    )(place, *ws)


def _sibling_copy(ref, send_sem, recv_sem):
    x, y, c, _ = _place()
    return pltpu.make_async_remote_copy(src_ref=ref, dst_ref=ref, send_sem=send_sem, recv_sem=recv_sem,
                                        device_id=(x, y, 1 - c), device_id_type=MESH)


def _slot(arr, chip):
    if arr.shape[0] == NCHIP:
        return arr.at[chip]
    cols = arr.shape[2] // 2
    return arr.at[chip // 2, :, pl.ds(pl.multiple_of((chip % 2) * cols, 128), cols)]


def _half_rows(arr, slot, core):
    half = arr.shape[1] // 2
    return _slot(arr, slot).at[pl.ds(pl.multiple_of(core * half, 16), half)]


def _quarter_rows(arr, slot, core, q):
    quarter = arr.shape[1] // 4
    return _slot(arr, slot).at[pl.ds(pl.multiple_of((2 * core + q) * quarter, 16), quarter)]


def _chip_copy(ref, dist, send_sem, recv_sem):
    x, y, c, _ = _place()
    cx, cy = _chip_at(x, y, dist)
    return pltpu.make_async_remote_copy(src_ref=ref, dst_ref=ref, send_sem=send_sem, recv_sem=recv_sem,
                                        device_id=(cx, cy, c), device_id_type=MESH)


def _gather_sems(n):
    dma = pltpu.SemaphoreType.DMA
    return [dma((n, 2))] * 4 + [dma((n, 4))] * 2


def _gather_start(arrs, sems):
    dsend, drecv = sems[0], sems[1]
    _, _, c, j = _place()
    for w, arr in enumerate(arrs):
        for dist in (1, 2):
            _chip_copy(_half_rows(arr, j, c), dist, dsend.at[w, dist - 1], drecv.at[w, dist - 1]).start()


def _gather_land(arrs, sems, dist, first=0):
    dsend, drecv, rsend, rrecv, fsend, frecv = sems
    _, _, c, j = _place()
    if dist < 3:
        other = 3 - dist
        for w, arr in enumerate(arrs, first):
            landed = _half_rows(arr, j ^ dist, c)
            _chip_copy(landed, dist, dsend.at[w, dist - 1], drecv.at[w, dist - 1]).wait_recv()
            relay = _quarter_rows(arr, j ^ dist, c, other - 1)
            _chip_copy(relay, other, rsend.at[w, other - 1], rrecv.at[w, other - 1]).start()
            _sibling_copy(landed, fsend.at[w, dist - 1], frecv.at[w, dist - 1]).start()
        for w, arr in enumerate(arrs, first):
            theirs = _half_rows(arr, j ^ dist, 1 - c)
            _sibling_copy(theirs, fsend.at[w, dist - 1], frecv.at[w, dist - 1]).wait_recv()
    else:
        for w, arr in enumerate(arrs, first):
            for via in (1, 2):
                piece = _quarter_rows(arr, j ^ 3, c, via - 1)
                _chip_copy(piece, via, rsend.at[w, via - 1], rrecv.at[w, via - 1]).wait_recv()
                _sibling_copy(piece, fsend.at[w, 1 + via], frecv.at[w, 1 + via]).start()
        for w, arr in enumerate(arrs, first):
            for via in (1, 2):
                theirs = _quarter_rows(arr, j ^ 3, 1 - c, via - 1)
                _sibling_copy(theirs, fsend.at[w, 1 + via], frecv.at[w, 1 + via]).wait_recv()


def _gather_drain(arrs, sems):
    dsend, drecv, rsend, rrecv, fsend, frecv = sems
    _, _, c, j = _place()
    for w, arr in enumerate(arrs):
        for dist in (1, 2):
            other = 3 - dist
            _chip_copy(_half_rows(arr, j, c), dist, dsend.at[w, dist - 1], drecv.at[w, dist - 1]).wait_send()
            _chip_copy(_quarter_rows(arr, j ^ dist, c, other - 1), other,
                       rsend.at[w, other - 1], rrecv.at[w, other - 1]).wait_send()
            _sibling_copy(_half_rows(arr, j ^ dist, c), fsend.at[w, dist - 1], frecv.at[w, dist - 1]).wait_send()
            _sibling_copy(_quarter_rows(arr, j ^ 3, c, dist - 1),
                          fsend.at[w, 1 + dist], frecv.at[w, 1 + dist]).wait_send()


def _gather_neighbours(arrs, sems):
    _gather_land(arrs, sems, 1)
    _gather_land(arrs, sems, 2)


def _gather_finish(arrs, sems):
    _gather_land(arrs, sems, 3)
    _gather_drain(arrs, sems)


def _gather_job(arrs):
    n = len(arrs)
    return _Job(arrs, [SDS(a.shape, a.dtype) for a in arrs], _gather_sems(n),
                lambda ins, outs, sems: _gather_start(outs, sems),
                lambda ins, outs, sems: _gather_finish(outs, sems), {k: k for k in range(n)},
                mid=lambda ins, outs, sems: _gather_neighbours(outs, sems))


def _exchange_job(arrs, out_shapes, n, copies):
    def start(ins, outs, sems):
        for cp in copies(ins, outs, sems[0], sems[1]):
            cp.start()

    def finish(ins, outs, sems):
        for cp in copies(ins, outs, sems[0], sems[1]):
            cp.wait()

    return _Job(arrs, out_shapes, [pltpu.SemaphoreType.DMA((n,))] * 2, start, finish)


def _pair_exchange_job(grads):
    def copies(ins, outs, send_sem, recv_sem):
        x, y, c, _ = _place()
        res = []
        for w in range(len(grads)):
            half = ins[w].shape[1] // 2
            theirs = pl.ds(pl.multiple_of((1 - c) * half, 8), half)
            res.append(pltpu.make_async_remote_copy(
                src_ref=ins[w].at[:, theirs, :], dst_ref=outs[w], send_sem=send_sem.at[w],
                recv_sem=recv_sem.at[w], device_id=(x, y, 1 - c), device_id_type=MESH))
        return res

    return _exchange_job(grads, [SDS((NCHIP, g.shape[1] // 2, g.shape[2]), F32) for g in grads],
                         len(grads), copies)


def _row_tile(rows, cols):
    tr = rows
    while tr * cols * 4 > ELEMENTWISE_BLOCK_BYTES and tr % 32 == 0:
        tr //= 2
    return tr


def _pair_sums(name, place, gs, sibs):
    n = len(gs)
    half, cols = sibs[0].shape[1], sibs[0].shape[2]
    tr = _row_tile(half, cols)
    nt = half // tr
    mine = nt if gs[0].shape[1] == 2 * half else 0

    def body(pc_ref, *refs):
        del pc_ref
        for g_ref, s_ref, own_ref, out_ref in zip(refs[:n], refs[n:2 * n], refs[2 * n:3 * n], refs[3 * n:]):
            v = g_ref[...] + s_ref[...]

            @pl.when(pl.program_id(1) == 0)
            def _():
                own_ref[...] = v

            @pl.when(pl.program_id(1) > 0)
            def _():
                out_ref[...] = v.astype(BF16)

    res = pl.pallas_call(
        body, name=name,
        grid_spec=pltpu.PrefetchScalarGridSpec(
            num_scalar_prefetch=1, grid=(nt, NCHIP),
            in_specs=[pl.BlockSpec((None, tr, cols), lambda i, s, pc: (pc[0] ^ s, pc[1] * mine + i, 0))] * n
            + [pl.BlockSpec((None, tr, cols), lambda i, s, pc: (pc[0] ^ s, i, 0))] * n,
            out_specs=[pl.BlockSpec((tr, cols), lambda i, s, pc: (i, 0))] * n
            + [pl.BlockSpec((None, tr, cols), lambda i, s, pc: (jnp.maximum(s - 1, 0), i, 0))] * n),
        out_shape=[SDS((half, cols), F32)] * n + [SDS((NCHIP - 1, half, cols), BF16)] * n,
        compiler_params=_cparams(),
    )(place, *gs, *sibs)
    return res[:n], res[n:]


def _chip_exchange_job(parts):
    def copies(ins, outs, send_sem, recv_sem):
        x, y, c, _ = _place()
        res = []
        for w in range(len(parts)):
            for s in range(1, NCHIP):
                cx, cy = _chip_at(x, y, s)
                k = w * (NCHIP - 1) + s - 1
                res.append(pltpu.make_async_remote_copy(
                    src_ref=ins[w].at[s - 1], dst_ref=outs[w].at[s - 1], send_sem=send_sem.at[k],
                    recv_sem=recv_sem.at[k], device_id=(cx, cy, c), device_id_type=MESH))
        return res

    return _exchange_job(parts, [SDS((NCHIP - 1,) + p.shape[1:], BF16) for p in parts],
                         len(parts) * (NCHIP - 1), copies)


def _chip_sums(name, owns, rems):
    n = len(owns)
    half, cols = owns[0].shape
    tr = _row_tile(half, cols)

    def body(*refs):
        for own_ref, rem_ref, out_ref in zip(refs[:n], refs[n:2 * n], refs[2 * n:]):
            out_ref[...] = (((own_ref[...] + rem_ref[0].astype(F32)) + rem_ref[1].astype(F32))
                            + rem_ref[2].astype(F32))

    return pl.pallas_call(
        body, name=name, grid=(half // tr,),
        in_specs=[pl.BlockSpec((tr, cols), lambda i: (i, 0))] * n
        + [pl.BlockSpec((NCHIP - 1, tr, cols), lambda i: (0, i, 0))] * n,
        out_specs=[pl.BlockSpec((tr, cols), lambda i: (i, 0))] * n,
        out_shape=[SDS((half, cols), F32)] * n,
        compiler_params=_cparams(),
    )(*owns, *rems)


def _share_halves_job(halves):
    def copies(ins, outs, send_sem, recv_sem):
        x, y, c, _ = _place()
        return [pltpu.make_async_remote_copy(
            src_ref=ins[w], dst_ref=outs[w], send_sem=send_sem.at[w], recv_sem=recv_sem.at[w],
            device_id=(x, y, 1 - c), device_id_type=MESH) for w in range(len(halves))]

    return _exchange_job(halves, [SDS(h.shape, F32) for h in halves], len(halves), copies)


def _adamw_math(w, g, m, v):
    m = B1 * m + (1.0 - B1) * g
    v = B2 * v + (1.0 - B2) * (g * g)
    m_hat = m / (1.0 - B1 ** STEP)
    v_hat = v / (1.0 - B2 ** STEP)
    delta = -LR * (m_hat / (jnp.sqrt(v_hat) + AEPS) + WD * w)
    return delta, m, v


def _adamws(name, place, ws, owns, sibs, ms, vs):
    n = len(ws)
    rows, cols = ws[0].shape
    by_cols = owns[0].shape[0] == rows
    half, pc_cols = (rows, cols // 2) if by_cols else (rows // 2, cols)
    tr = _row_tile(half, pc_cols)
    nt = half // tr

    def body(pc_ref, *refs):
        ins, outs = refs[:5 * n], refs[5 * n:]
        for k in range(n):
            w_ref, own_ref, sib_ref, m_ref, v_ref = ins[5 * k:5 * k + 5]
            g = jnp.where(pl.program_id(0) == pc_ref[1], own_ref[...], sib_ref[...])
            d, mn, vn = _adamw_math(w_ref[...], g, m_ref[...], v_ref[...])
            for ref, val in zip(outs[4 * k:4 * k + 4], (g, d, mn, vn)):
                ref[...] = val

    full = pl.BlockSpec((tr, pc_cols), (lambda h, i, pc: (i, h)) if by_cols else (lambda h, i, pc: (h * nt + i, 0)))
    part = pl.BlockSpec((tr, pc_cols), lambda h, i, pc: (i, 0))
    res = pl.pallas_call(
        body, name=name,
        grid_spec=pltpu.PrefetchScalarGridSpec(
            num_scalar_prefetch=1, grid=(2, nt),
            in_specs=[full, part, part, full, full] * n, out_specs=[full] * (4 * n)),
        out_shape=[SDS((rows, cols), F32)] * (4 * n),
        compiler_params=_cparams(),
    )(place, *[a for group in zip(ws, owns, sibs, ms, vs) for a in group])
    return [tuple(res[4 * k:4 * k + 4]) for k in range(n)]


def _small_allreduce_adamw(sp, wmv, job):
    shape = sp.shape
    ji, jo = len(job.ins), len(job.out_shapes)

    def body(sp_ref, wmv_ref, *rest):
        jin, (g_ref, d_ref, mo_ref, vo_ref), jout = rest[:ji], rest[ji:ji + 4], rest[ji + 4:ji + 4 + jo]
        sib_s, pair_s, chip_s, send_sem, recv_sem = rest[ji + 4 + jo:ji + 9 + jo]
        jsem = rest[ji + 9 + jo:]
        job.start(jin, jout, jsem)
        x, y, c, j = _place()
        cp = pltpu.make_async_remote_copy(
            src_ref=sp_ref, dst_ref=sib_s, send_sem=send_sem.at[0], recv_sem=recv_sem.at[0],
            device_id=(x, y, 1 - c), device_id_type=MESH)
        cp.start()
        cp.wait()
        pair_s[...] = sp_ref[...] + sib_s[...]
        half = shape[0] // 2
        mine = pl.ds(pl.multiple_of(c * half, 8), half)
        cps = []
        for s in range(1, NCHIP):
            cx, cy = _chip_at(x, y, s)
            cp = pltpu.make_async_remote_copy(
                src_ref=pair_s.at[mine], dst_ref=chip_s.at[s, mine], send_sem=send_sem.at[s],
                recv_sem=recv_sem.at[s], device_id=(cx, cy, c), device_id_type=MESH)
            cp.start()
            cps.append(cp)
        chip_s[0] = pair_s[...]
        for cp in cps:
            cp.wait()
        cps = []
        for s in range(1, NCHIP):
            cp = pltpu.make_async_remote_copy(
                src_ref=chip_s.at[s, mine], dst_ref=chip_s.at[s, mine], send_sem=send_sem.at[NCHIP + s],
                recv_sem=recv_sem.at[NCHIP + s], device_id=(x, y, 1 - c), device_id_type=MESH)
            cp.start()
            cps.append(cp)
        for cp in cps:
            cp.wait()
        tot = chip_s[j]
        for k in range(1, NCHIP):
            tot = tot + chip_s[k ^ j]
        g_ref[...] = tot
        d, mn, vn = _adamw_math(wmv_ref[0], tot, wmv_ref[1], wmv_ref[2])
        d_ref[...] = d
        mo_ref[...] = mn
        vo_ref[...] = vn
        job.mid(jin, jout, jsem)
        job.finish(jin, jout, jsem)

    vm = pl.BlockSpec(memory_space=pltpu.VMEM)
    res = pl.pallas_call(
        body, name="small_allreduce_adamw",
        in_specs=[vm] * 2 + [ANY] * ji, out_specs=[vm] * 4 + [ANY] * jo,
        out_shape=[SDS(shape, F32)] * 4 + job.out_shapes,
        scratch_shapes=[pltpu.VMEM(shape, F32), pltpu.VMEM(shape, F32), pltpu.VMEM((NCHIP,) + shape, F32),
                        pltpu.SemaphoreType.DMA((2 * NCHIP,)), pltpu.SemaphoreType.DMA((2 * NCHIP,))] + job.sems,
        input_output_aliases={2 + a: 4 + b for a, b in job.aliases.items()},
        compiler_params=pltpu.CompilerParams(has_side_effects=True),
    )(sp, wmv, *job.ins)
    return res[:4], res[4:]


def _pack_small(first, mix, ln_g, ln_b, b_s, lbt, hn, ffn, fin, w_s):
    rows = [first.reshape(1, D), mix.reshape(1, D), ln_g.reshape(1, D), ln_b.reshape(1, D),
            b_s.reshape(1, D), lbt.reshape(2, D), hn.reshape(1, D), ffn.reshape(1, D), fin.reshape(1, D),
            jnp.zeros((6, D), F32)]
    return jnp.concatenate(rows + [w_s.reshape(NG, GCH, GCH).transpose(1, 0, 2).reshape(GCH, D)], axis=0)


def _unpack_small(p):
    w_s = p[16:].reshape(GCH, NG, GCH).transpose(1, 0, 2).reshape(1, NG, GCH, GCH)
    return dict(norm_mix_g=p[1:2], gmlp_ln_g=p[2:3], gmlp_ln_b=p[3:4], gmlp_b_s=p[4].reshape(1, NG, GCH),
                hgrn_lb_table=p[5:7], hgrn_norm_g=p[7:8], norm_ffn_g=p[8:9], norm_final_g=p[9],
                gmlp_w_s=w_s)


SMALL = ("norm_mix_g", "gmlp_ln_g", "gmlp_ln_b", "gmlp_w_s", "gmlp_b_s", "hgrn_lb_table", "hgrn_norm_g",
         "norm_ffn_g", "norm_final_g")
BIG = ("w_in", "w_gate_up", "w_branch_a", "w_branch_b", "w_out", "w_down")
ORDER = ("norm_mix_g", "w_in", "gmlp_ln_g", "gmlp_ln_b", "gmlp_w_s", "gmlp_b_s", "hgrn_lb_table",
         "hgrn_norm_g", "w_branch_a", "w_branch_b", "w_out", "norm_ffn_g", "w_gate_up", "w_down",
         "norm_final_g")


def kernel(x, norm_mix_g, w_in, gmlp_ln_g, gmlp_ln_b, gmlp_w_s, gmlp_b_s, hgrn_lb_table, hgrn_norm_g, w_branch_a, w_branch_b, w_out, norm_ffn_g, w_gate_up, w_down, norm_final_g, loss_target, m_norm_mix_g, m_w_in, m_gmlp_ln_g, m_gmlp_ln_b, m_gmlp_w_s, m_gmlp_b_s, m_hgrn_lb_table, m_hgrn_norm_g, m_w_branch_a, m_w_branch_b, m_w_out, m_norm_ffn_g, m_w_gate_up, m_w_down, m_norm_final_g, v_norm_mix_g, v_w_in, v_gmlp_ln_g, v_gmlp_ln_b, v_gmlp_w_s, v_gmlp_b_s, v_hgrn_lb_table, v_hgrn_norm_g, v_w_branch_a, v_w_branch_b, v_w_out, v_norm_ffn_g, v_w_gate_up, v_w_down, v_norm_final_g):
    args = dict(locals())
    T = x.shape[1]
    xs = x.reshape(T, D)
    target = loss_target.reshape(T, D)
    big = {n: args[n].reshape(args[n].shape[1:]) for n in BIG}
    big_m = {n: args["m_" + n].reshape(args[n].shape[1:]) for n in BIG}
    big_v = {n: args["v_" + n].reshape(args[n].shape[1:]) for n in BIG}

    x_i, y_i, c_i = lax.axis_index("x"), lax.axis_index("y"), lax.axis_index("c")
    place = jnp.stack([2 * x_i + y_i, c_i]).astype(jnp.int32)
    def by_shape(names):
        groups = []
        for n in names:
            if groups and big[groups[-1][0]].shape == big[n].shape:
                groups[-1].append(n)
            else:
                groups.append([n])
        return groups

    cast = {}
    for grp in by_shape(BIG):
        cast.update(zip(grp, _cast_shards("cast_" + grp[0], place, [big[n] for n in grp],
                                          paired=grp[0] == "w_gate_up")))
    tril = jnp.tril(jnp.ones((GCH, GCH), bool))
    wm = jnp.where(tril, gmlp_w_s[0], 0.0).astype(BF16)
    wm_t = jnp.swapaxes(wm, 1, 2)
    b_t = gmlp_b_s[0].T

    (proj, hb), w_in4, (w_a4, w_b4, w_out4, w_down4) = _proj_fwd(
        place, xs, norm_mix_g, cast["w_in"], [cast[n] for n in ("w_branch_a", "w_branch_b", "w_out", "w_down")])
    (ab,), _ = _gmlp_fwd(proj, gmlp_ln_g, gmlp_ln_b, wm, b_t)
    (o_raw, obb, st_before), (w_gu,) = _hgrn_fwd(
        proj, hgrn_lb_table, hgrn_norm_g, job=_gather_job([cast["w_gate_up"]]))
    w_a, w_b, w_o = (w.reshape(D, D) for w in (w_a4, w_b4, w_out4))
    (mgb, x1), _ = _merge_fwd(xs, ab, obb, proj, w_a, w_b, w_o)
    w_dn = w_down4.reshape(FF, D)
    act, dx2b, h2b, dgu, dx1, dx1b, acc_ffn = _ffn_fwd_bwd(
        x1, target, norm_ffn_g, norm_final_g.reshape(1, D), w_gu, w_dn)

    grads, owns, parts, halves, sibh = {}, {}, {}, {}, {}

    def pair_sums(names, sibs):
        sib_of = dict(zip(names, sibs))
        for grp in by_shape(names):
            o, p = _pair_sums("rs_pair_sum_" + grp[0], place, [grads[n] for n in grp], [sib_of[n] for n in grp])
            owns.update(zip(grp, o))
            parts.update(zip(grp, p))

    def chip_sums(names, got):
        rem_of = dict(zip(names, got))
        for grp in by_shape(names):
            h = _chip_sums("rs_chip_sum_" + grp[0], [owns[n] for n in grp], [rem_of[n] for n in grp])
            halves.update(zip(grp, h))

    ffn, mix = ("w_gate_up", "w_down"), ("w_branch_a", "w_branch_b", "w_out")
    grads["w_gate_up"], _ = _dw_gate_up(h2b, dgu)
    grads["w_down"], _ = _dw_down(act, dx2b)
    (dya, dyb, dproj), got = _merge_bwd(
        dx1b, ab, obb, proj, w_o, w_a, w_b, job=_pair_exchange_job([grads[n] for n in ffn]))
    pair_sums(ffn, got)
    grads["w_branch_a"], _ = _dw_square("dw_branch_a", ab, dya)
    grads["w_branch_b"], _ = _dw_square("dw_branch_b", obb, dyb)
    grads["w_out"], _ = _dw_square("dw_out", mgb, dx1b)
    (dproj, acc_hgrn), got = _hgrn_bwd(
        dproj, dyb, w_b, o_raw, proj, st_before, hgrn_lb_table, hgrn_norm_g,
        job=_join_jobs(_chip_exchange_job([parts[n] for n in ffn]), _pair_exchange_job([grads[n] for n in mix])))
    chip_sums(ffn, got[:2])
    pair_sums(mix, got[2:])
    dproj, acc_ln, dws, dmix = _gmlp_bwd(dproj, dya, w_a, proj, gmlp_ln_g, gmlp_ln_b, wm, wm_t, b_t)
    for_sibling, got = _dw_in_half(
        "dw_in_sibling_half", place, hb, dproj, False,
        job=_join_jobs(_share_halves_job([halves[n] for n in ffn]), _chip_exchange_job([parts[n] for n in mix])))
    sibh.update(zip(ffn, got[:2]))
    chip_sums(mix, got[2:])
    grads["w_in"], got = _dw_in_half(
        "dw_in_own_half", place, hb, dproj, True, job=_share_halves_job([for_sibling]))
    pair_sums(("w_in",), got)
    (grad_x, acc_mix), got = _proj_bwd(
        dproj, w_in4, xs, dx1, norm_mix_g,
        job=_join_jobs(_chip_exchange_job([parts["w_in"]]), _share_halves_job([halves[n] for n in mix])))
    chip_sums(("w_in",), got[:1])
    sibh.update(zip(mix, got[1:]))

    lbv = jax.nn.sigmoid(hgrn_lb_table[0] - hgrn_lb_table[1])
    d_t0 = jnp.sum(acc_hgrn[0], axis=0) * lbv * (1.0 - lbv)
    loss_row = jnp.zeros((D,), F32).at[0].set(jnp.sum(acc_ffn[0]))
    dws_m = jnp.where(tril[:, None, :], dws.reshape(GCH, NG, GCH), 0.0).transpose(1, 0, 2)
    db_s = jnp.sum(dmix.reshape(GCH, NG, GCH), axis=-1).T
    sp = _pack_small(loss_row, jnp.sum(acc_mix, 0), jnp.sum(acc_ln[0], 0), jnp.sum(acc_ln[1], 0), db_s,
                     jnp.stack([d_t0, -d_t0]), jnp.sum(acc_hgrn[1], 0), jnp.sum(acc_ffn[2], 0),
                     jnp.sum(acc_ffn[1], 0), dws_m)
    zero = jnp.zeros((D,), F32)

    def pack(prefix):
        a = lambda n: args[prefix + n]
        return _pack_small(zero, a("norm_mix_g"), a("gmlp_ln_g"), a("gmlp_ln_b"), a("gmlp_b_s"),
                           a("hgrn_lb_table"), a("hgrn_norm_g"), a("norm_ffn_g"), a("norm_final_g"),
                           a("gmlp_w_s"))

    packed, (sibh["w_in"],) = _small_allreduce_adamw(
        sp, jnp.stack([pack(""), pack("m_"), pack("v_")]), _share_halves_job([halves["w_in"]]))
    loss = packed[0][0, 0]
    small = [_unpack_small(p) for p in packed]
    out = {n: tuple(s[n] for s in small) for n in SMALL}
    for grp in by_shape(BIG):
        res = _adamws("adamw_" + grp[0], place, *[[d[n] for n in grp] for d in (big, halves, sibh, big_m, big_v)])
        for n, quad in zip(grp, res):
            out[n] = tuple(a.reshape(args[n].shape) for a in quad)
    return (loss, grad_x.reshape(x.shape), *[out[n][0] for n in ORDER], *[out[n][1] for n in ORDER],
            *[out[n][2] for n in ORDER], *[out[n][3] for n in ORDER])
```

```python
import functools
import math

import jax
import jax.numpy as jnp
from jax import lax
from jax.experimental import pallas as pl
from jax.experimental.pallas import tpu as pltpu
from jax.experimental.pallas import tpu_sc as plsc

F32 = jnp.float32
BF16 = jnp.bfloat16
SDS = jax.ShapeDtypeStruct
MESH = pl.DeviceIdType.MESH
ANY = pl.BlockSpec(memory_space=pl.ANY)

D = 1024
NIN = 8
NG = 8
GCH = 128
NH = 8
HD = 128
HCH = 64
HGRN_HB = 8
HGRN_TOKENS = 256
GMLP_FWD_TOKENS = 512
GMLP_BWD_TOKENS = 256
HW = HGRN_HB * HD
DW_TOKENS = 2048
DW_IN_TOKENS = 4096
ELEMENTWISE_BLOCK_BYTES = 2 * 1024 * 1024
PROJ_OUT_SLOTS = 4
FF = 2816
FFS = 1408
NCHIP = 4
EPS = 1e-6
QSCALE = HD ** -0.5
GELU_C0 = math.sqrt(2.0 / math.pi)
GELU_C1 = 0.044715
LR, B1, B2, AEPS, WD, STEP = 0.001, 0.9, 0.999, 1e-08, 0.01, 10
VMEM_LIMIT_V7X = 56 * 1024 * 1024
SP_ROWS = 144


def _cparams(**kw):
    return pltpu.CompilerParams(vmem_limit_bytes=VMEM_LIMIT_V7X, **kw)


def _mm(a, b):
    return jnp.dot(a, b, preferred_element_type=F32)


def _mm_nt(a, b):
    return lax.dot_general(a, b, (((1,), (1,)), ((), ())), preferred_element_type=F32)


def _mm_tn(a, b):
    return lax.dot_general(a, b, (((0,), (0,)), ((), ())), preferred_element_type=F32)


def _rows8(x):
    r, c = x.shape
    return jnp.sum(x.reshape(r // 8, 8, c), axis=0)


def _mean(x):
    return jnp.mean(x, axis=-1, keepdims=True)


def _sigmoid(x):
    return 1.0 / (1.0 + jnp.exp(-x))


def _gelu(x):
    t = jnp.tanh(GELU_C0 * (x + GELU_C1 * x * x * x))
    return 0.5 * x * (1.0 + t), t


def _gelu_grad(x, t):
    return 0.5 * (1.0 + t) + 0.5 * x * (1.0 - t * t) * (GELU_C0 * (1.0 + 3.0 * GELU_C1 * x * x))


def _component_of(group):
    return jnp.where(group < 6, (group + 4) % 6, group)


def _proj_fwd(place, x, g_mix, w_in4, later):
    T = x.shape[0]
    tm = min(1024, T)
    ni = T // tm
    n = len(later)

    def body(pc_ref, x_ref, g_ref, *rest):
        proj_ref, h_ref, w_all = rest[1 + n:4 + n]
        gathered = rest[4 + n:4 + 2 * n]
        hs, wbuf, wsem, obuf, osem = rest[4 + 2 * n:9 + 2 * n]
        w_sems, later_sems = rest[9 + 2 * n:15 + 2 * n], rest[15 + 2 * n:]
        jp, i = pl.program_id(0), pl.program_id(1)
        w_cols = [w_all.at[:, :, pl.ds(k * D, D)] for k in range(2)]

        def w_copy(blk):
            cols = pl.ds(pl.multiple_of((blk % 2) * D, 128), D)
            return pltpu.make_async_copy(w_all.at[pc_ref[0] ^ (blk // 2), :, cols], wbuf.at[blk % 2],
                                         wsem.at[blk % 2])

        @pl.when((jp == 0) & (i == 0))
        def _():
            _gather_start(w_cols, w_sems)
            w_copy(jp).start()

        @pl.when(i == 0)
        def _():
            w_copy(jp).wait()

        @pl.when(jp == 0)
        def _():
            xv = x_ref[...]
            r = lax.rsqrt(_mean(xv * xv) + EPS)
            hb = (xv * r * g_ref[...]).astype(BF16)
            hs[i] = hb
            h_ref[...] = hb

        step = jp * ni + i
        slot = step % PROJ_OUT_SLOTS

        def o_copy(slot_):
            comp = 2 * (pc_ref[0] ^ (jp // 2)) + jp % 2
            return pltpu.make_async_copy(
                obuf.at[slot_], proj_ref.at[comp, pl.ds(pl.multiple_of(i * tm, 8), tm)], osem.at[slot_])

        @pl.when(step >= PROJ_OUT_SLOTS)
        def _():
            o_copy(slot).wait()

        obuf[slot] = _mm(hs[i], wbuf[jp % 2])
        o_copy(slot).start()

        @pl.when(step == NIN * ni - 1)
        def _():
            for k in range(PROJ_OUT_SLOTS):
                o_copy((slot + 1 + k) % PROJ_OUT_SLOTS).wait()

        for nxt in range(1, NIN):
            @pl.when((jp == nxt - 1) & (i == ni - 1))
            def _():
                if nxt >= 2:
                    _gather_land([w_cols[nxt % 2]], w_sems, nxt // 2, first=nxt % 2)
                if nxt == 5:
                    _gather_start(gathered, later_sems)
                if nxt == NIN - 1:
                    _gather_neighbours(gathered, later_sems)
                w_copy(jp + 1).start()

        @pl.when((jp == NIN - 1) & (i == ni - 1))
        def _():
            _gather_drain(w_cols, w_sems)
            _gather_finish(gathered, later_sems)

    tile = lambda jp, i, pc: (jnp.where(jp == 0, i, ni - 1), 0)
    res = pl.pallas_call(
        body, name="proj_fwd",
        grid_spec=pltpu.PrefetchScalarGridSpec(
            num_scalar_prefetch=1, grid=(NIN, ni),
            in_specs=[pl.BlockSpec((tm, D), tile), pl.BlockSpec((1, D), lambda jp, i, pc: (0, 0))] + [ANY] * (1 + n),
            out_specs=[ANY, pl.BlockSpec((tm, D), tile)] + [ANY] * (1 + n),
            scratch_shapes=[pltpu.VMEM((ni, tm, D), BF16), pltpu.VMEM((2, D, D), BF16),
                            pltpu.SemaphoreType.DMA((2,)), pltpu.VMEM((PROJ_OUT_SLOTS, tm, D), F32),
                            pltpu.SemaphoreType.DMA((PROJ_OUT_SLOTS,))] + _gather_sems(2) + _gather_sems(n)),
        out_shape=[SDS((NIN, T, D), F32), SDS((T, D), BF16), SDS(w_in4.shape, BF16)]
        + [SDS(a.shape, a.dtype) for a in later],
        input_output_aliases={3 + k: 2 + k for k in range(1 + n)},
        compiler_params=_cparams(has_side_effects=True),
    )(place, x, g_mix, w_in4, *later)
    return res[:2], res[2], res[3:]


def _chunks_abreast(x):
    return jnp.concatenate([x[GCH * ch:GCH * (ch + 1)] for ch in range(x.shape[0] // GCH)], axis=1)


def _chunks_stacked(x):
    return jnp.concatenate([x[:, GCH * ch:GCH * (ch + 1)] for ch in range(x.shape[1] // GCH)], axis=0)


def _layer_norm_stats(gv):
    mu = _mean(gv)
    xc = gv - mu
    rs = lax.rsqrt(_mean(xc * xc) + EPS)
    return xc * rs, rs


def _gmlp_fwd(proj, ln_g, ln_b, wm, b_t, job=None):
    T = proj.shape[1]
    tm = min(GMLP_FWD_TOKENS, T)

    def body(u_ref, v_ref, lg_ref, lb_ref, wm_ref, bt_ref, a_ref, a_s):
        gu, _ = _gelu(u_ref[...])
        gv, _ = _gelu(v_ref[...])
        vhat, _ = _layer_norm_stats(gv)
        vnb = (vhat * lg_ref[...] + lb_ref[...]).astype(BF16)
        for g in range(NG):
            cols = slice(128 * g, 128 * (g + 1))
            mixed = _mm(wm_ref[g], _chunks_abreast(vnb[:, cols])) + bt_ref[:, g:g + 1]
            a_s[:, cols] = gu[:, cols] * _chunks_stacked(mixed)
        a_ref[...] = a_s[...].astype(BF16)

    row = lambda i: (0, 0)
    return _call(
        body, name="gmlp_fwd", grid=(T // tm,), job=job, args=(proj, proj, ln_g, ln_b, wm, b_t),
        in_specs=[pl.BlockSpec((None, tm, D), lambda i: (0, i, 0)), pl.BlockSpec((None, tm, D), lambda i: (1, i, 0)),
                  pl.BlockSpec((1, D), row), pl.BlockSpec((1, D), row),
                  pl.BlockSpec((NG, GCH, GCH), lambda i: (0, 0, 0)), pl.BlockSpec((GCH, NG), row)],
        out_specs=[pl.BlockSpec((tm, D), lambda i: (i, 0))],
        out_shape=[SDS((T, D), BF16)],
        scratch_shapes=[pltpu.VMEM((tm, D), F32)])


def _cumsum64(x, row):
    for s in (1, 2, 4, 8, 16, 32):
        x = x + jnp.where(row >= s, pltpu.roll(x, s, 0), 0.0)
    return x


def _revcumsum64(x, row):
    n = x.shape[0]
    for s in (1, 2, 4, 8, 16, 32):
        x = x + jnp.where(row < HCH - s, pltpu.roll(x, n - s, 0), 0.0)
    return x


def _head_mean(x):
    parts = [jnp.broadcast_to(_mean(x[:, HD * h:HD * (h + 1)]), (x.shape[0], HD)) for h in range(x.shape[1] // HD)]
    return jnp.concatenate(parts, axis=1)


def _seg_sum(x):
    n, c = x.shape
    s = jnp.sum(x.reshape(n // HCH, HCH, c), axis=1, keepdims=True)
    return jnp.broadcast_to(s, (n // HCH, HCH, c)).reshape(n, c)


def _seg_row(x, idx):
    n, c = x.shape
    x3 = x.reshape(n // HCH, HCH, c)
    return jnp.broadcast_to(x3[:, idx:idx + 1, :], x3.shape).reshape(n, c)


def _hgrn_gates(fl, lbv, row):
    s = _sigmoid(fl)
    f = lbv + (1.0 - lbv) * s
    a = _cumsum64(jnp.log(f), row)
    return s, f, a, _seg_row(a, HCH // 2 - 1), _seg_row(a, HCH - 1)


def _hgrn_fwd(proj, lb_table, norm_g, job=None):
    T = proj.shape[1]
    tb = min(HGRN_TOKENS, T)
    nc = tb // HCH

    def body(q_ref, fl_ref, v_ref, g_ref, lbt_ref, gn_ref, o_ref, ob_ref, stb_ref, st_s, o_s):
        @pl.when(pl.program_id(1) == 0)
        def _():
            st_s[...] = jnp.zeros_like(st_s)

        row = lax.broadcasted_iota(jnp.int32, (tb, HW), 0) & (HCH - 1)
        lbv = _sigmoid(lbt_ref[0:1, :] - lbt_ref[1:2, :])
        _, f, a, a_mid, a_last = _hgrn_gates(fl_ref[...], lbv, row)
        k = 1.0 - f
        qs = q_ref[...] * QSCALE
        q_in = (qs * jnp.exp(a - a_mid)).astype(BF16)
        k_in = (k * jnp.exp(a_mid - a)).astype(BF16)
        q_a = (qs * jnp.exp(a)).astype(BF16)
        k_d = (k * jnp.exp(a_last - a)).astype(BF16)
        dec = jnp.exp(a_last)
        vb = v_ref[...].astype(BF16)
        tri = (lax.broadcasted_iota(jnp.int32, (HCH, HCH), 0)
               >= lax.broadcasted_iota(jnp.int32, (HCH, HCH), 1))
        for c in range(nc):
            sl = slice(HCH * c, HCH * (c + 1))
            for hh in range(HGRN_HB):
                hs = slice(HD * hh, HD * (hh + 1))
                st = st_s[hh]
                stb_ref[hh, c] = st
                sc = jnp.where(tri, _mm_nt(q_in[sl, hs], k_in[sl, hs]), 0.0)
                o_s[sl, hs] = _mm(sc.astype(BF16), vb[sl, hs]) + _mm_nt(q_a[sl, hs], st.astype(BF16))
                d64 = dec[sl, hs]
                st_s[hh] = st * jnp.concatenate([d64, d64], axis=0) + _mm_tn(vb[sl, hs], k_d[sl, hs])
        o = o_s[...]
        r = lax.rsqrt(_head_mean(o * o) + EPS)
        g = g_ref[...]
        o_ref[...] = o
        ob_ref[...] = (o * r * gn_ref[...] * (g * _sigmoid(g))).astype(BF16)

    def col(off):
        return pl.BlockSpec((None, tb, HW), lambda h, cb: (off, cb, h))

    return _call(
        body, name="hgrn_fwd", grid=(NH // HGRN_HB, T // tb), job=job,
        args=(proj, proj, proj, proj, lb_table, norm_g),
        in_specs=[col(2), col(3), col(4), col(5),
                  pl.BlockSpec((2, HW), lambda h, cb: (0, h)), pl.BlockSpec((1, HW), lambda h, cb: (0, h))],
        out_specs=[pl.BlockSpec((tb, HW), lambda h, cb: (cb, h)), pl.BlockSpec((tb, HW), lambda h, cb: (cb, h)),
                   pl.BlockSpec((HGRN_HB, nc, HD, HD), lambda h, cb: (h, cb, 0, 0))],
        out_shape=[SDS((T, D), F32), SDS((T, D), BF16), SDS((NH, T // HCH, HD, HD), F32)],
        scratch_shapes=[pltpu.VMEM((HGRN_HB, HD, HD), F32), pltpu.VMEM((tb, HW), F32)])


def _merge_fwd(x, ab, ob, proj, w_a, w_b, w_out, job=None):
    T = x.shape[0]
    tm = min(512, T)

    def body(x_ref, ab_ref, ob_ref, ga_ref, gb_ref, wa_ref, wb_ref, wo_ref, mg_ref, x1_ref):
        ya = _mm(ab_ref[...], wa_ref[...])
        yb = _mm(ob_ref[...], wb_ref[...])
        merged = (_sigmoid(ga_ref[...]) * ya + _sigmoid(gb_ref[...]) * yb).astype(BF16)
        mg_ref[...] = merged
        x1_ref[...] = x_ref[...] + _mm(merged, wo_ref[...])

    t = lambda i: (i, 0)
    w = lambda i: (0, 0)
    return _call(
        body, name="merge_fwd", grid=(T // tm,), job=job, args=(x, ab, ob, proj, proj, w_a, w_b, w_out),
        in_specs=[pl.BlockSpec((tm, D), t), pl.BlockSpec((tm, D), t), pl.BlockSpec((tm, D), t),
                  pl.BlockSpec((None, tm, D), lambda i: (6, i, 0)), pl.BlockSpec((None, tm, D), lambda i: (7, i, 0)),
                  pl.BlockSpec((D, D), w), pl.BlockSpec((D, D), w), pl.BlockSpec((D, D), w)],
        out_specs=[pl.BlockSpec((tm, D), t)] * 2,
        out_shape=[SDS((T, D), BF16), SDS((T, D), F32)])


def _ffn_fwd_bwd(x1, target, g_ffn, g_fin, w_gu, w_down):
    T = x1.shape[0]
    tm = min(256, T)
    inv_d = 1.0 / D

    def body(x1_ref, tg_ref, gf_ref, gn_ref, wgu_ref, wd_ref,
             act_ref, dx2b_ref, h2b_ref, dgu_ref, dx1_ref, dx1b_ref, acc_ref):
        @pl.when(pl.program_id(0) == 0)
        def _():
            acc_ref[...] = jnp.zeros_like(acc_ref)

        x1v = x1_ref[...]
        gf = gf_ref[...]
        gn = gn_ref[...]
        rr1 = lax.rsqrt(_mean(x1v * x1v) + EPS)
        x1n = x1v * rr1
        h2b = (x1n * gf).astype(BF16)
        h2b_ref[...] = h2b
        gate = _mm(h2b, wgu_ref[0])
        up = _mm(h2b, wgu_ref[1])
        sg = _sigmoid(gate)
        si = gate * sg
        act = (si * up).astype(BF16)
        act_ref[...] = act
        x2 = x1v + _mm(act, wd_ref[...])
        rr2 = lax.rsqrt(_mean(x2 * x2) + EPS)
        x2n = x2 * rr2
        e = x2n * gn - tg_ref[...]
        acc_ref[0] += _rows8(e * e) * (0.5 * inv_d)
        dy = e * inv_d
        acc_ref[1] += _rows8(dy * x2n)
        dxn = dy * gn
        dx2 = rr2 * (dxn - x2n * _mean(dxn * x2n))
        dx2b = dx2.astype(BF16)
        dx2b_ref[...] = dx2b
        dact = _mm_nt(dx2b, wd_ref[...])
        dgate = (dact * up * (sg * (1.0 + gate * (1.0 - sg)))).astype(BF16)
        dup = (dact * si).astype(BF16)
        dgu_ref[0] = dgate
        dgu_ref[1] = dup
        dh2 = _mm_nt(dgate, wgu_ref[0]) + _mm_nt(dup, wgu_ref[1])
        acc_ref[2] += _rows8(dh2 * x1n)
        dxn1 = dh2 * gf
        dx1 = dx2 + rr1 * (dxn1 - x1n * _mean(dxn1 * x1n))
        dx1_ref[...] = dx1
        dx1b_ref[...] = dx1.astype(BF16)

    t = lambda i: (i, 0)
    w = lambda i: (0, 0)
    one = pl.Buffered(1)
    return pl.pallas_call(
        body, name="ffn_fwd_bwd", grid=(T // tm,),
        in_specs=[pl.BlockSpec((tm, D), t), pl.BlockSpec((tm, D), t),
                  pl.BlockSpec((1, D), w), pl.BlockSpec((1, D), w),
                  pl.BlockSpec((2, D, FF), lambda i: (0, 0, 0), pipeline_mode=one),
                  pl.BlockSpec((FF, D), w, pipeline_mode=one)],
        out_specs=[pl.BlockSpec((tm, FF), t), pl.BlockSpec((tm, D), t), pl.BlockSpec((tm, D), t),
                   pl.BlockSpec((2, tm, FF), lambda i: (0, i, 0)),
                   pl.BlockSpec((tm, D), t), pl.BlockSpec((tm, D), t),
                   pl.BlockSpec((3, 8, D), lambda i: (0, 0, 0))],
        out_shape=[SDS((T, FF), BF16), SDS((T, D), BF16), SDS((T, D), BF16),
                   SDS((2, T, FF), BF16), SDS((T, D), F32), SDS((T, D), BF16),
                   SDS((3, 8, D), F32)],
        compiler_params=_cparams(),
    )(x1, target, g_ffn, g_fin, w_gu, w_down)


def _merge_bwd(dx1b, ab, ob, proj, w_out, w_a, w_b, job=None):
    T = dx1b.shape[0]
    tm = min(512, T)

    def body(dx_ref, ab_ref, ob_ref, ga_ref, gb_ref, wo_ref, wa_ref, wb_ref, dya_ref, dyb_ref, dp_ref):
        dm = _mm_nt(dx_ref[...], wo_ref[...])
        sa = _sigmoid(ga_ref[...])
        sb = _sigmoid(gb_ref[...])
        dya_ref[...] = (dm * sa).astype(BF16)
        dyb_ref[...] = (dm * sb).astype(BF16)
        dp_ref[0] = (dm * _mm(ab_ref[...], wa_ref[...]) * sa * (1.0 - sa)).astype(BF16)
        dp_ref[1] = (dm * _mm(ob_ref[...], wb_ref[...]) * sb * (1.0 - sb)).astype(BF16)

    t = lambda i: (i, 0)
    w = lambda i: (0, 0)
    return _call(
        body, name="merge_bwd", grid=(T // tm,),
        in_specs=[pl.BlockSpec((tm, D), t), pl.BlockSpec((tm, D), t), pl.BlockSpec((tm, D), t),
                  pl.BlockSpec((None, tm, D), lambda i: (6, i, 0)), pl.BlockSpec((None, tm, D), lambda i: (7, i, 0)),
                  pl.BlockSpec((D, D), w), pl.BlockSpec((D, D), w), pl.BlockSpec((D, D), w)],
        out_specs=[pl.BlockSpec((tm, D), t)] * 2 + [pl.BlockSpec((2, tm, D), lambda i: (3, i, 0))],
        out_shape=[SDS((T, D), BF16), SDS((T, D), BF16), SDS((NIN, T, D), BF16)],
        args=(dx1b, ab, ob, proj, proj, w_out, w_a, w_b), job=job)


def _hgrn_bwd(dproj, dyb, w_b, o_raw, proj, st_before, lb_table, norm_g, job=None):
    T = dyb.shape[0]
    tb = min(HGRN_TOKENS, T)
    nc = tb // HCH
    nb = T // tb

    def body(dp_in, dyb_ref, wb_ref, o_ref, q_ref, fl_ref, v_ref, g_ref, stb_ref, lbt_ref, gn_ref,
             dp_ref, acc_ref, dst_s, dqin_s, dqa_s, dkin_s, dkd_s, dv_s, ddec_s):
        del dp_in

        @pl.when(pl.program_id(1) == 0)
        def _():
            dst_s[...] = jnp.zeros_like(dst_s)
            acc_ref[...] = jnp.zeros_like(acc_ref)

        row = lax.broadcasted_iota(jnp.int32, (tb, HW), 0) & (HCH - 1)
        gn = gn_ref[...]
        lbv = _sigmoid(lbt_ref[0:1, :] - lbt_ref[1:2, :])
        o = o_ref[...]
        r = lax.rsqrt(_head_mean(o * o) + EPS)
        on = o * r
        g = g_ref[...]
        sgm = _sigmoid(g)
        dob_v = _mm_nt(dyb_ref[...], wb_ref[...])
        dp_ref[3] = (dob_v * on * gn * (sgm * (1.0 + g * (1.0 - sgm)))).astype(BF16)
        do_n = dob_v * (g * sgm)
        acc_ref[1] += _rows8(do_n * on)
        dxn = do_n * gn
        do = (r * (dxn - on * _head_mean(dxn * on))).astype(BF16)
        s, f, a, a_mid, a_last = _hgrn_gates(fl_ref[...], lbv, row)
        k = 1.0 - f
        qs = q_ref[...] * QSCALE
        e_q = jnp.exp(a - a_mid)
        e_k = jnp.exp(a_mid - a)
        e_a = jnp.exp(a)
        e_l = jnp.exp(a_last - a)
        dec = jnp.exp(a_last)
        q_in = qs * e_q
        k_in = k * e_k
        q_a = qs * e_a
        k_d = k * e_l
        q_inb, k_inb, q_ab, k_db = (z.astype(BF16) for z in (q_in, k_in, q_a, k_d))
        vb = v_ref[...].astype(BF16)
        tri = (lax.broadcasted_iota(jnp.int32, (HCH, HCH), 0)
               >= lax.broadcasted_iota(jnp.int32, (HCH, HCH), 1))
        for c in reversed(range(nc)):
            sl = slice(HCH * c, HCH * (c + 1))
            for hh in range(HGRN_HB):
                hs = slice(HD * hh, HD * (hh + 1))
                stp = stb_ref[hh, c]
                dst = dst_s[hh]
                dstb = dst.astype(BF16)
                do_c = do[sl, hs]
                v_c = vb[sl, hs]
                dqa_s[sl, hs] = _mm(do_c, stp.astype(BF16))
                dkd_s[sl, hs] = _mm(v_c, dstb)
                ddec_s[sl, hs] = jnp.broadcast_to(jnp.sum(dst * stp, axis=0, keepdims=True), (HCH, HD))
                sc = jnp.where(tri, _mm_nt(q_inb[sl, hs], k_inb[sl, hs]), 0.0).astype(BF16)
                dsc = jnp.where(tri, _mm_nt(do_c, v_c), 0.0).astype(BF16)
                dv_s[sl, hs] = _mm_nt(k_db[sl, hs], dstb) + _mm_tn(sc, do_c)
                dqin_s[sl, hs] = _mm(dsc, k_inb[sl, hs])
                dkin_s[sl, hs] = _mm_tn(dsc, q_inb[sl, hs])
                d64 = dec[sl, hs]
                dst_s[hh] = dst * jnp.concatenate([d64, d64], axis=0) + _mm_tn(do_c, q_ab[sl, hs])
        dq_in = dqin_s[...]
        dq_a = dqa_s[...]
        dk_in = dkin_s[...]
        dk_d = dkd_s[...]
        dp_ref[0] = ((dq_in * e_q + dq_a * e_a) * QSCALE).astype(BF16)
        dp_ref[2] = dv_s[...].astype(BF16)
        tq = dq_in * q_in
        tk = dk_in * k_in
        td = dk_d * k_d
        d_a = tq + dq_a * q_a - tk - td
        d_a = d_a + jnp.where(row == HCH // 2 - 1, _seg_sum(tk - tq), 0.0)
        d_a = d_a + jnp.where(row == HCH - 1, _seg_sum(td) + ddec_s[...] * dec, 0.0)
        dlf = _revcumsum64(d_a, row)
        df = dlf / f - (dk_in * e_k + dk_d * e_l)
        dp_ref[1] = (df * (1.0 - lbv) * s * (1.0 - s)).astype(BF16)
        acc_ref[0] += _rows8(df * (1.0 - s))

    def col(off):
        return pl.BlockSpec((None, tb, HW), lambda h, cb: (off, nb - 1 - cb, h))

    hb = lambda h, cb: (nb - 1 - cb, h)
    return _call(
        body, name="hgrn_bwd", grid=(NH // HGRN_HB, nb), job=job,
        args=(dproj, dyb, w_b, o_raw, proj, proj, proj, proj, st_before, lb_table, norm_g),
        in_specs=[ANY, pl.BlockSpec((tb, D), lambda h, cb: (nb - 1 - cb, 0)),
                  pl.BlockSpec((HW, D), lambda h, cb: (h, 0)), pl.BlockSpec((tb, HW), hb),
                  col(2), col(3), col(4), col(5),
                  pl.BlockSpec((HGRN_HB, nc, HD, HD), lambda h, cb: (h, nb - 1 - cb, 0, 0)),
                  pl.BlockSpec((2, HW), lambda h, cb: (0, h)), pl.BlockSpec((1, HW), lambda h, cb: (0, h))],
        out_specs=[pl.BlockSpec((4, tb, HW), lambda h, cb: (0, nb - 1 - cb, h)),
                   pl.BlockSpec((2, 8, HW), lambda h, cb: (0, 0, h))],
        out_shape=[SDS(dproj.shape, BF16), SDS((2, 8, D), F32)],
        scratch_shapes=[pltpu.VMEM((HGRN_HB, HD, HD), F32)] + [pltpu.VMEM((tb, HW), F32)] * 6,
        aliases={0: 0})


def _gmlp_bwd(dproj, dya, w_a, proj, ln_g, ln_b, wm, wm_t, b_t):
    T = dya.shape[0]
    tm = min(GMLP_BWD_TOKENS, T)

    def body(dp_in, dya_ref, wa_ref, u_ref, v_ref, lg_ref, lb_ref, wm_ref, wmt_ref, bt_ref,
             dp_ref, acc_ref, dws_ref, dmix_ref, du_s, dvn_s):
        del dp_in

        @pl.when(pl.program_id(0) == 0)
        def _():
            acc_ref[...] = jnp.zeros_like(acc_ref)
            dws_ref[...] = jnp.zeros_like(dws_ref)
            dmix_ref[...] = jnp.zeros_like(dmix_ref)

        u = u_ref[...]
        v = v_ref[...]
        lg = lg_ref[...]
        gu, t_u = _gelu(u)
        gv, t_v = _gelu(v)
        vhat, rs = _layer_norm_stats(gv)
        vnb = (vhat * lg + lb_ref[...]).astype(BF16)
        da_v = _mm_nt(dya_ref[...], wa_ref[...])
        for g in range(NG):
            cols = slice(128 * g, 128 * (g + 1))
            vng = _chunks_abreast(vnb[:, cols])
            mixed = _mm(wm_ref[g], vng) + bt_ref[:, g:g + 1]
            dag = _chunks_abreast(da_v[:, cols])
            dmx = dag * _chunks_abreast(gu[:, cols])
            du_s[:, cols] = _chunks_stacked(dag * mixed)
            dmxb = dmx.astype(BF16)
            dws_ref[:, cols] += _mm_nt(dmxb, vng)
            dmix_ref[:, cols] += sum(dmx[:, GCH * ch:GCH * (ch + 1)] for ch in range(tm // GCH))
            dvn_s[:, cols] = _chunks_stacked(_mm(wmt_ref[g], dmxb))
        dp_ref[0] = (du_s[...] * _gelu_grad(u, t_u)).astype(BF16)
        dvn = dvn_s[...]
        acc_ref[0] += _rows8(dvn * vhat)
        acc_ref[1] += _rows8(dvn)
        dvh = dvn * lg
        dgv = rs * (dvh - _mean(dvh) - vhat * _mean(dvh * vhat))
        dp_ref[1] = (dgv * _gelu_grad(v, t_v)).astype(BF16)

    row = lambda i: (0, 0)
    w3 = lambda i: (0, 0, 0)
    return pl.pallas_call(
        body, name="gmlp_bwd", grid=(T // tm,),
        in_specs=[ANY, pl.BlockSpec((tm, D), lambda i: (i, 0)), pl.BlockSpec((D, D), row),
                  pl.BlockSpec((None, tm, D), lambda i: (0, i, 0)), pl.BlockSpec((None, tm, D), lambda i: (1, i, 0)),
                  pl.BlockSpec((1, D), row), pl.BlockSpec((1, D), row),
                  pl.BlockSpec((NG, GCH, GCH), w3), pl.BlockSpec((NG, GCH, GCH), w3),
                  pl.BlockSpec((GCH, NG), row)],
        out_specs=[pl.BlockSpec((2, tm, D), lambda i: (2, i, 0)),
                   pl.BlockSpec((2, 8, D), w3), pl.BlockSpec((GCH, D), row), pl.BlockSpec((GCH, D), row)],
        out_shape=[SDS(dproj.shape, BF16), SDS((2, 8, D), F32), SDS((GCH, D), F32), SDS((GCH, D), F32)],
        scratch_shapes=[pltpu.VMEM((tm, D), F32), pltpu.VMEM((tm, D), F32)],
        input_output_aliases={0: 0},
        compiler_params=_cparams(),
    )(dproj, dya, w_a, proj, proj, ln_g, ln_b, wm, wm_t, b_t)


def _proj_bwd(dproj, w_in4, x, dx1, g_mix, job=None):
    T = x.shape[0]
    tm = min(256, T)
    order = (2, 3, 4, 5, 0, 1, 6, 7)

    def body(dp_ref, w_ref, x_ref, dx1_ref, g_ref, gx_ref, acc_ref):
        @pl.when(pl.program_id(0) == 0)
        def _():
            acc_ref[...] = jnp.zeros_like(acc_ref)

        dh = None
        for m, og in enumerate(order):
            part = _mm_nt(dp_ref[m], w_ref[og // 2, :, D * (og % 2):D * (og % 2 + 1)])
            dh = part if dh is None else dh + part
        xv = x_ref[...]
        r = lax.rsqrt(_mean(xv * xv) + EPS)
        xn = xv * r
        acc_ref[...] += _rows8(dh * xn)
        dxn = dh * g_ref[...]
        gx_ref[...] = dx1_ref[...] + r * (dxn - xn * _mean(dxn * xn))

    t = lambda i: (i, 0)
    return _call(
        body, name="proj_bwd", grid=(T // tm,),
        in_specs=[pl.BlockSpec((NIN, tm, D), lambda i: (0, i, 0)),
                  pl.BlockSpec((NCHIP, D, 2 * D), lambda i: (0, 0, 0), pipeline_mode=pl.Buffered(1)),
                  pl.BlockSpec((tm, D), t), pl.BlockSpec((tm, D), t), pl.BlockSpec((1, D), lambda i: (0, 0))],
        out_specs=[pl.BlockSpec((tm, D), t), pl.BlockSpec((8, D), lambda i: (0, 0))],
        out_shape=[SDS((T, D), F32), SDS((8, D), F32)],
        args=(dproj, w_in4, x, dx1, g_mix), job=job)


def _dw_call(name, a, b, a_spec, b_spec, o_spec, out_shape, nblk, tt, job=None, prefetch=None):
    T = a.shape[-2]

    def body(*refs):
        a_ref, b_ref, o_ref = refs[-3:]

        @pl.when(pl.program_id(1) == 0)
        def _():
            o_ref[...] = jnp.zeros_like(o_ref)
        o_ref[...] += _mm_tn(a_ref[...], b_ref[...])

    (out,), job_out = _call(
        body, name=name, grid=(nblk, T // tt), in_specs=[a_spec, b_spec], out_specs=[o_spec],
        out_shape=[out_shape], args=(a, b), job=job, prefetch=prefetch)
    return out, job_out


def _dw_in_half(name, place, hb, dproj, mine, job=None):
    tt = min(DW_IN_TOKENS, hb.shape[0])

    def comp(k, pc):
        return _component_of(2 * k + (pc[1] if mine else 1 - pc[1]))

    return _dw_call(
        name, hb, dproj,
        pl.BlockSpec((tt, D), lambda k, t, pc: (t, 0)),
        pl.BlockSpec((None, tt, D), lambda k, t, pc: (comp(k, pc), t, 0)),
        pl.BlockSpec((None, D, D), lambda k, t, pc: (k, 0, 0)),
        SDS((NCHIP, D, D), F32), NCHIP, tt, job, place)


def _dw_gate_up(h2b, dgu, job=None):
    tt = min(DW_TOKENS, h2b.shape[0])
    return _dw_call(
        "dw_gate_up", h2b, dgu,
        pl.BlockSpec((tt, D), lambda k, t: (t, 0)),
        pl.BlockSpec((None, tt, FFS), lambda k, t: (k // 2, t, k % 2)),
        pl.BlockSpec((None, D, FFS), lambda k, t: (k, 0, 0)),
        SDS((NCHIP, D, FFS), F32), NCHIP, tt, job)


def _dw_down(act, dx2b, job=None):
    tt = min(DW_TOKENS, act.shape[0])
    g, job_out = _dw_call(
        "dw_down", act, dx2b,
        pl.BlockSpec((tt, FFS), lambda k, t: (t, k)),
        pl.BlockSpec((tt, D), lambda k, t: (t, 0)),
        pl.BlockSpec((FFS, D), lambda k, t: (k, 0)),
        SDS((FF, D), F32), 2, tt, job)
    return g.reshape(NCHIP, FF // NCHIP, D), job_out


def _dw_square(name, a, b, job=None):
    tt = min(DW_TOKENS, a.shape[0])
    g, job_out = _dw_call(
        name, a, b,
        pl.BlockSpec((tt, D), lambda k, t: (t, 0)), pl.BlockSpec((tt, D), lambda k, t: (t, 0)),
        pl.BlockSpec((D, D), lambda k, t: (0, 0)), SDS((D, D), F32), 1, tt, job)
    return g.reshape(NCHIP, D // NCHIP, D), job_out


def _place():
    x, y, c = lax.axis_index("x"), lax.axis_index("y"), lax.axis_index("c")
    return x, y, c, 2 * x + y


def _chip_at(x, y, s):
    return x ^ (s >> 1), y ^ (s & 1)


class _Job:
    def __init__(self, ins, out_shapes, sems, start, finish, aliases=None, mid=None):
        self.ins, self.out_shapes, self.sems = list(ins), list(out_shapes), list(sems)
        self.start, self.finish, self.aliases = start, finish, dict(aliases or {})
        self.mid = mid if mid is not None else (lambda ins, outs, sems: None)


def _join_jobs(*jobs):
    def cut(refs, sizes):
        out, at = [], 0
        for n in sizes:
            out.append(refs[at:at + n])
            at += n
        return out

    ni = [len(j.ins) for j in jobs]
    no = [len(j.out_shapes) for j in jobs]
    ns = [len(j.sems) for j in jobs]

    def run(which):
        def go(ins, outs, sems):
            for j, a, b, c in zip(jobs, cut(ins, ni), cut(outs, no), cut(sems, ns)):
                getattr(j, which)(a, b, c)
        return go

    aliases = {}
    for k, j in enumerate(jobs):
        for a, b in j.aliases.items():
            aliases[sum(ni[:k]) + a] = sum(no[:k]) + b
    return _Job([a for j in jobs for a in j.ins], [o for j in jobs for o in j.out_shapes],
                [s for j in jobs for s in j.sems], run("start"), run("finish"), aliases, run("mid"))


def _call(body, *, name, grid, in_specs, out_specs, out_shape, args, scratch_shapes=(), aliases=None,
          job=None, prefetch=None):
    n_in, n_out, n_scr = len(in_specs), len(out_specs), len(scratch_shapes)
    npf = 0 if prefetch is None else 1
    job = job if job is not None else _Job([], [], [], lambda *a: None, lambda *a: None)
    ji, jo = len(job.ins), len(job.out_shapes)
    steps = math.prod(grid)

    def wrapped(*refs):
        pf, refs = refs[:npf], refs[npf:]
        ins, jin = refs[:n_in], refs[n_in:n_in + ji]
        o0 = n_in + ji
        outs, jout = refs[o0:o0 + n_out], refs[o0 + n_out:o0 + n_out + jo]
        s0 = o0 + n_out + jo
        scr, jsem = refs[s0:s0 + n_scr], refs[s0 + n_scr:]
        step = functools.reduce(lambda acc, ag: acc * ag[1] + pl.program_id(ag[0]), enumerate(grid), 0)
        if ji or jo:
            @pl.when(step == 0)
            def _():
                job.start(jin, jout, jsem)

        body(*pf, *ins, *outs, *scr)

        if ji or jo:
            @pl.when(step == steps // 2)
            def _():
                job.mid(jin, jout, jsem)

            @pl.when(step == steps - 1)
            def _():
                job.finish(jin, jout, jsem)

    io = {npf + a: b for a, b in dict(aliases or {}).items()}
    io.update({npf + n_in + a: n_out + b for a, b in job.aliases.items()})
    kw = dict(in_specs=list(in_specs) + [ANY] * ji, out_specs=list(out_specs) + [ANY] * jo,
              scratch_shapes=list(scratch_shapes) + job.sems)
    if npf:
        kw = dict(grid_spec=pltpu.PrefetchScalarGridSpec(num_scalar_prefetch=1, grid=grid, **kw))
    else:
        kw["grid"] = grid
    res = pl.pallas_call(
        wrapped, name=name, out_shape=list(out_shape) + job.out_shapes, input_output_aliases=io,
        compiler_params=_cparams(has_side_effects=bool(ji or jo)), **kw,
    )(*(() if prefetch is None else (prefetch,)), *args, *job.ins)
    return list(res[:n_out]), list(res[n_out:])


def _cast_shards(name, place, ws, paired=False):
    n = len(ws)
    rows, cols = ws[0].shape
    tr = 352 if rows % 352 == 0 else 256
    shape = (2, rows, 2 * cols) if paired else (NCHIP, rows, cols)
    mine = (lambda i, pc: (pc[0] // 2, i, pc[0] % 2)) if paired else (lambda i, pc: (pc[0], i, 0))

    def body(pc_ref, *refs):
        del pc_ref
        for w_ref, o_ref in zip(refs[:n], refs[n:]):
            o_ref[...] = w_ref[...].astype(BF16)

    return pl.pallas_call(
        body, name=name,
        grid_spec=pltpu.PrefetchScalarGridSpec(
            num_scalar_prefetch=1, grid=(rows // tr,),
            in_specs=[pl.BlockSpec((tr, cols), lambda i, pc: (i, 0))] * n,
            out_specs=[pl.BlockSpec((None, tr, cols), mine)] * n),
        out_shape=[SDS(shape, BF16)] * n,
        compiler_params=_cparams(),
    )(place, *ws)


def _sibling_copy(ref, send_sem, recv_sem):
    x, y, c, _ = _place()
    return pltpu.make_async_remote_copy(src_ref=ref, dst_ref=ref, send_sem=send_sem, recv_sem=recv_sem,
                                        device_id=(x, y, 1 - c), device_id_type=MESH)


def _slot(arr, chip):
    if arr.shape[0] == NCHIP:
        return arr.at[chip]
    cols = arr.shape[2] // 2
    return arr.at[chip // 2, :, pl.ds(pl.multiple_of((chip % 2) * cols, 128), cols)]


def _half_rows(arr, slot, core):
    half = arr.shape[1] // 2
    return _slot(arr, slot).at[pl.ds(pl.multiple_of(core * half, 16), half)]


def _quarter_rows(arr, slot, core, q):
    quarter = arr.shape[1] // 4
    return _slot(arr, slot).at[pl.ds(pl.multiple_of((2 * core + q) * quarter, 16), quarter)]


def _chip_copy(ref, dist, send_sem, recv_sem):
    x, y, c, _ = _place()
    cx, cy = _chip_at(x, y, dist)
    return pltpu.make_async_remote_copy(src_ref=ref, dst_ref=ref, send_sem=send_sem, recv_sem=recv_sem,
                                        device_id=(cx, cy, c), device_id_type=MESH)


def _gather_sems(n):
    dma = pltpu.SemaphoreType.DMA
    return [dma((n, 2))] * 4 + [dma((n, 4))] * 2


def _gather_start(arrs, sems):
    dsend, drecv = sems[0], sems[1]
    _, _, c, j = _place()
    for w, arr in enumerate(arrs):
        for dist in (1, 2):
            _chip_copy(_half_rows(arr, j, c), dist, dsend.at[w, dist - 1], drecv.at[w, dist - 1]).start()


def _gather_land(arrs, sems, dist, first=0):
    dsend, drecv, rsend, rrecv, fsend, frecv = sems
    _, _, c, j = _place()
    if dist < 3:
        other = 3 - dist
        for w, arr in enumerate(arrs, first):
            landed = _half_rows(arr, j ^ dist, c)
            _chip_copy(landed, dist, dsend.at[w, dist - 1], drecv.at[w, dist - 1]).wait_recv()
            relay = _quarter_rows(arr, j ^ dist, c, other - 1)
            _chip_copy(relay, other, rsend.at[w, other - 1], rrecv.at[w, other - 1]).start()
            _sibling_copy(landed, fsend.at[w, dist - 1], frecv.at[w, dist - 1]).start()
        for w, arr in enumerate(arrs, first):
            theirs = _half_rows(arr, j ^ dist, 1 - c)
            _sibling_copy(theirs, fsend.at[w, dist - 1], frecv.at[w, dist - 1]).wait_recv()
    else:
        for w, arr in enumerate(arrs, first):
            for via in (1, 2):
                piece = _quarter_rows(arr, j ^ 3, c, via - 1)
                _chip_copy(piece, via, rsend.at[w, via - 1], rrecv.at[w, via - 1]).wait_recv()
                _sibling_copy(piece, fsend.at[w, 1 + via], frecv.at[w, 1 + via]).start()
        for w, arr in enumerate(arrs, first):
            for via in (1, 2):
                theirs = _quarter_rows(arr, j ^ 3, 1 - c, via - 1)
                _sibling_copy(theirs, fsend.at[w, 1 + via], frecv.at[w, 1 + via]).wait_recv()


def _gather_drain(arrs, sems):
    dsend, drecv, rsend, rrecv, fsend, frecv = sems
    _, _, c, j = _place()
    for w, arr in enumerate(arrs):
        for dist in (1, 2):
            other = 3 - dist
            _chip_copy(_half_rows(arr, j, c), dist, dsend.at[w, dist - 1], drecv.at[w, dist - 1]).wait_send()
            _chip_copy(_quarter_rows(arr, j ^ dist, c, other - 1), other,
                       rsend.at[w, other - 1], rrecv.at[w, other - 1]).wait_send()
            _sibling_copy(_half_rows(arr, j ^ dist, c), fsend.at[w, dist - 1], frecv.at[w, dist - 1]).wait_send()
            _sibling_copy(_quarter_rows(arr, j ^ 3, c, dist - 1),
                          fsend.at[w, 1 + dist], frecv.at[w, 1 + dist]).wait_send()


def _gather_neighbours(arrs, sems):
    _gather_land(arrs, sems, 1)
    _gather_land(arrs, sems, 2)


def _gather_finish(arrs, sems):
    _gather_land(arrs, sems, 3)
    _gather_drain(arrs, sems)


def _gather_job(arrs):
    n = len(arrs)
    return _Job(arrs, [SDS(a.shape, a.dtype) for a in arrs], _gather_sems(n),
                lambda ins, outs, sems: _gather_start(outs, sems),
                lambda ins, outs, sems: _gather_finish(outs, sems), {k: k for k in range(n)},
                mid=lambda ins, outs, sems: _gather_neighbours(outs, sems))


def _exchange_job(arrs, out_shapes, n, copies):
    def start(ins, outs, sems):
        for cp in copies(ins, outs, sems[0], sems[1]):
            cp.start()

    def finish(ins, outs, sems):
        for cp in copies(ins, outs, sems[0], sems[1]):
            cp.wait()

    return _Job(arrs, out_shapes, [pltpu.SemaphoreType.DMA((n,))] * 2, start, finish)


def _pair_exchange_job(grads):
    def copies(ins, outs, send_sem, recv_sem):
        x, y, c, _ = _place()
        res = []
        for w in range(len(grads)):
            half = ins[w].shape[1] // 2
            theirs = pl.ds(pl.multiple_of((1 - c) * half, 8), half)
            res.append(pltpu.make_async_remote_copy(
                src_ref=ins[w].at[:, theirs, :], dst_ref=outs[w], send_sem=send_sem.at[w],
                recv_sem=recv_sem.at[w], device_id=(x, y, 1 - c), device_id_type=MESH))
        return res

    return _exchange_job(grads, [SDS((NCHIP, g.shape[1] // 2, g.shape[2]), F32) for g in grads],
                         len(grads), copies)


def _row_tile(rows, cols):
    tr = rows
    while tr * cols * 4 > ELEMENTWISE_BLOCK_BYTES and tr % 32 == 0:
        tr //= 2
    return tr


def _pair_sums(name, place, gs, sibs):
    n = len(gs)
    half, cols = sibs[0].shape[1], sibs[0].shape[2]
    tr = _row_tile(half, cols)
    nt = half // tr
    mine = nt if gs[0].shape[1] == 2 * half else 0

    def body(pc_ref, *refs):
        del pc_ref
        for g_ref, s_ref, own_ref, out_ref in zip(refs[:n], refs[n:2 * n], refs[2 * n:3 * n], refs[3 * n:]):
            v = g_ref[...] + s_ref[...]

            @pl.when(pl.program_id(1) == 0)
            def _():
                own_ref[...] = v

            @pl.when(pl.program_id(1) > 0)
            def _():
                out_ref[...] = v.astype(BF16)

    res = pl.pallas_call(
        body, name=name,
        grid_spec=pltpu.PrefetchScalarGridSpec(
            num_scalar_prefetch=1, grid=(nt, NCHIP),
            in_specs=[pl.BlockSpec((None, tr, cols), lambda i, s, pc: (pc[0] ^ s, pc[1] * mine + i, 0))] * n
            + [pl.BlockSpec((None, tr, cols), lambda i, s, pc: (pc[0] ^ s, i, 0))] * n,
            out_specs=[pl.BlockSpec((tr, cols), lambda i, s, pc: (i, 0))] * n
            + [pl.BlockSpec((None, tr, cols), lambda i, s, pc: (jnp.maximum(s - 1, 0), i, 0))] * n),
        out_shape=[SDS((half, cols), F32)] * n + [SDS((NCHIP - 1, half, cols), BF16)] * n,
        compiler_params=_cparams(),
    )(place, *gs, *sibs)
    return res[:n], res[n:]


def _chip_exchange_job(parts):
    def copies(ins, outs, send_sem, recv_sem):
        x, y, c, _ = _place()
        res = []
        for w in range(len(parts)):
            for s in range(1, NCHIP):
                cx, cy = _chip_at(x, y, s)
                k = w * (NCHIP - 1) + s - 1
                res.append(pltpu.make_async_remote_copy(
                    src_ref=ins[w].at[s - 1], dst_ref=outs[w].at[s - 1], send_sem=send_sem.at[k],
                    recv_sem=recv_sem.at[k], device_id=(cx, cy, c), device_id_type=MESH))
        return res

    return _exchange_job(parts, [SDS((NCHIP - 1,) + p.shape[1:], BF16) for p in parts],
                         len(parts) * (NCHIP - 1), copies)


def _chip_sums(name, owns, rems):
    n = len(owns)
    half, cols = owns[0].shape
    tr = _row_tile(half, cols)

    def body(*refs):
        for own_ref, rem_ref, out_ref in zip(refs[:n], refs[n:2 * n], refs[2 * n:]):
            out_ref[...] = (((own_ref[...] + rem_ref[0].astype(F32)) + rem_ref[1].astype(F32))
                            + rem_ref[2].astype(F32))

    return pl.pallas_call(
        body, name=name, grid=(half // tr,),
        in_specs=[pl.BlockSpec((tr, cols), lambda i: (i, 0))] * n
        + [pl.BlockSpec((NCHIP - 1, tr, cols), lambda i: (0, i, 0))] * n,
        out_specs=[pl.BlockSpec((tr, cols), lambda i: (i, 0))] * n,
        out_shape=[SDS((half, cols), F32)] * n,
        compiler_params=_cparams(),
    )(*owns, *rems)


def _share_halves_job(halves):
    def copies(ins, outs, send_sem, recv_sem):
        x, y, c, _ = _place()
        return [pltpu.make_async_remote_copy(
            src_ref=ins[w], dst_ref=outs[w], send_sem=send_sem.at[w], recv_sem=recv_sem.at[w],
            device_id=(x, y, 1 - c), device_id_type=MESH) for w in range(len(halves))]

    return _exchange_job(halves, [SDS(h.shape, F32) for h in halves], len(halves), copies)


def _adamw_math(w, g, m, v):
    m = B1 * m + (1.0 - B1) * g
    v = B2 * v + (1.0 - B2) * (g * g)
    m_hat = m / (1.0 - B1 ** STEP)
    v_hat = v / (1.0 - B2 ** STEP)
    delta = -LR * (m_hat / (jnp.sqrt(v_hat) + AEPS) + WD * w)
    return delta, m, v


def _adamws(name, place, ws, owns, sibs, ms, vs):
    n = len(ws)
    rows, cols = ws[0].shape
    by_cols = owns[0].shape[0] == rows
    half, pc_cols = (rows, cols // 2) if by_cols else (rows // 2, cols)
    tr = _row_tile(half, pc_cols)
    nt = half // tr

    def body(pc_ref, *refs):
        ins, outs = refs[:5 * n], refs[5 * n:]
        for k in range(n):
            w_ref, own_ref, sib_ref, m_ref, v_ref = ins[5 * k:5 * k + 5]
            g = jnp.where(pl.program_id(0) == pc_ref[1], own_ref[...], sib_ref[...])
            d, mn, vn = _adamw_math(w_ref[...], g, m_ref[...], v_ref[...])
            for ref, val in zip(outs[4 * k:4 * k + 4], (g, d, mn, vn)):
                ref[...] = val

    full = pl.BlockSpec((tr, pc_cols), (lambda h, i, pc: (i, h)) if by_cols else (lambda h, i, pc: (h * nt + i, 0)))
    part = pl.BlockSpec((tr, pc_cols), lambda h, i, pc: (i, 0))
    res = pl.pallas_call(
        body, name=name,
        grid_spec=pltpu.PrefetchScalarGridSpec(
            num_scalar_prefetch=1, grid=(2, nt),
            in_specs=[full, part, part, full, full] * n, out_specs=[full] * (4 * n)),
        out_shape=[SDS((rows, cols), F32)] * (4 * n),
        compiler_params=_cparams(),
    )(place, *[a for group in zip(ws, owns, sibs, ms, vs) for a in group])
    return [tuple(res[4 * k:4 * k + 4]) for k in range(n)]


ON_SPARSECORE = ("w_down",)
SC_TILES = 32
SC_LANES = 16


def _adamw_sparsecore(name, w, own, sib, m, v):
    rows, cols = w.shape
    groups, half_groups = rows // 8, rows // 16
    rounds = -(-groups // SC_TILES)

    def body(w_hbm, own_hbm, sib_hbm, m_hbm, v_hbm, g_out, d_out, mo_out, vo_out,
             wb, gb, mb, vb, db):
        tile = lax.axis_index("sc_tile") * 2 + lax.axis_index("sc_core")
        c = lax.axis_index("c")
        for k in range(rounds):
            grp = tile + SC_TILES * k

            @pl.when(grp < groups)
            def _():
                rws = pl.ds(pl.multiple_of(grp * 8, 8), 8)
                in_half = pl.ds(pl.multiple_of((grp % half_groups) * 8, 8), 8)
                mine = (grp // half_groups) == c

                @pl.when(mine)
                def _():
                    pltpu.sync_copy(own_hbm.at[in_half], gb)

                @pl.when(jnp.logical_not(mine))
                def _():
                    pltpu.sync_copy(sib_hbm.at[in_half], gb)

                pltpu.sync_copy(w_hbm.at[rws], wb)
                pltpu.sync_copy(m_hbm.at[rws], mb)
                pltpu.sync_copy(v_hbm.at[rws], vb)

                @pl.loop(0, cols, step=SC_LANES)
                def _(j):
                    for r in range(8):
                        at = (r, pl.ds(j, SC_LANES))
                        d, mn, vn = _adamw_math(wb[at], gb[at], mb[at], vb[at])
                        db[at] = d
                        mb[at] = mn
                        vb[at] = vn

                pltpu.sync_copy(gb, g_out.at[rws])
                pltpu.sync_copy(db, d_out.at[rws])
                pltpu.sync_copy(mb, mo_out.at[rws])
                pltpu.sync_copy(vb, vo_out.at[rws])

    return pl.kernel(
        body, name=name, out_type=[SDS((rows, cols), F32)] * 4,
        mesh=plsc.VectorSubcoreMesh(core_axis_name="sc_core", subcore_axis_name="sc_tile"),
        scratch_types=[pltpu.VMEM((8, cols), F32)] * 5,
    )(w, own, sib, m, v)


def _small_allreduce_adamw(sp, wmv, job):
    shape = sp.shape
    ji, jo = len(job.ins), len(job.out_shapes)

    def body(sp_ref, wmv_ref, *rest):
        jin, (g_ref, d_ref, mo_ref, vo_ref), jout = rest[:ji], rest[ji:ji + 4], rest[ji + 4:ji + 4 + jo]
        sib_s, pair_s, chip_s, send_sem, recv_sem = rest[ji + 4 + jo:ji + 9 + jo]
        jsem = rest[ji + 9 + jo:]
        job.start(jin, jout, jsem)
        x, y, c, j = _place()
        cp = pltpu.make_async_remote_copy(
            src_ref=sp_ref, dst_ref=sib_s, send_sem=send_sem.at[0], recv_sem=recv_sem.at[0],
            device_id=(x, y, 1 - c), device_id_type=MESH)
        cp.start()
        cp.wait()
        pair_s[...] = sp_ref[...] + sib_s[...]
        half = shape[0] // 2
        mine = pl.ds(pl.multiple_of(c * half, 8), half)
        cps = []
        for s in range(1, NCHIP):
            cx, cy = _chip_at(x, y, s)
            cp = pltpu.make_async_remote_copy(
                src_ref=pair_s.at[mine], dst_ref=chip_s.at[s, mine], send_sem=send_sem.at[s],
                recv_sem=recv_sem.at[s], device_id=(cx, cy, c), device_id_type=MESH)
            cp.start()
            cps.append(cp)
        chip_s[0] = pair_s[...]
        for cp in cps:
            cp.wait()
        cps = []
        for s in range(1, NCHIP):
            cp = pltpu.make_async_remote_copy(
                src_ref=chip_s.at[s, mine], dst_ref=chip_s.at[s, mine], send_sem=send_sem.at[NCHIP + s],
                recv_sem=recv_sem.at[NCHIP + s], device_id=(x, y, 1 - c), device_id_type=MESH)
            cp.start()
            cps.append(cp)
        for cp in cps:
            cp.wait()
        tot = chip_s[j]
        for k in range(1, NCHIP):
            tot = tot + chip_s[k ^ j]
        g_ref[...] = tot
        d, mn, vn = _adamw_math(wmv_ref[0], tot, wmv_ref[1], wmv_ref[2])
        d_ref[...] = d
        mo_ref[...] = mn
        vo_ref[...] = vn
        job.mid(jin, jout, jsem)
        job.finish(jin, jout, jsem)

    vm = pl.BlockSpec(memory_space=pltpu.VMEM)
    res = pl.pallas_call(
        body, name="small_allreduce_adamw",
        in_specs=[vm] * 2 + [ANY] * ji, out_specs=[vm] * 4 + [ANY] * jo,
        out_shape=[SDS(shape, F32)] * 4 + job.out_shapes,
        scratch_shapes=[pltpu.VMEM(shape, F32), pltpu.VMEM(shape, F32), pltpu.VMEM((NCHIP,) + shape, F32),
                        pltpu.SemaphoreType.DMA((2 * NCHIP,)), pltpu.SemaphoreType.DMA((2 * NCHIP,))] + job.sems,
        input_output_aliases={2 + a: 4 + b for a, b in job.aliases.items()},
        compiler_params=pltpu.CompilerParams(has_side_effects=True),
    )(sp, wmv, *job.ins)
    return res[:4], res[4:]


def _pack_small(first, mix, ln_g, ln_b, b_s, lbt, hn, ffn, fin, w_s):
    rows = [first.reshape(1, D), mix.reshape(1, D), ln_g.reshape(1, D), ln_b.reshape(1, D),
            b_s.reshape(1, D), lbt.reshape(2, D), hn.reshape(1, D), ffn.reshape(1, D), fin.reshape(1, D),
            jnp.zeros((6, D), F32)]
    return jnp.concatenate(rows + [w_s.reshape(NG, GCH, GCH).transpose(1, 0, 2).reshape(GCH, D)], axis=0)


def _unpack_small(p):
    w_s = p[16:].reshape(GCH, NG, GCH).transpose(1, 0, 2).reshape(1, NG, GCH, GCH)
    return dict(norm_mix_g=p[1:2], gmlp_ln_g=p[2:3], gmlp_ln_b=p[3:4], gmlp_b_s=p[4].reshape(1, NG, GCH),
                hgrn_lb_table=p[5:7], hgrn_norm_g=p[7:8], norm_ffn_g=p[8:9], norm_final_g=p[9],
                gmlp_w_s=w_s)


SMALL = ("norm_mix_g", "gmlp_ln_g", "gmlp_ln_b", "gmlp_w_s", "gmlp_b_s", "hgrn_lb_table", "hgrn_norm_g",
         "norm_ffn_g", "norm_final_g")
BIG = ("w_in", "w_gate_up", "w_branch_a", "w_branch_b", "w_out", "w_down")
ORDER = ("norm_mix_g", "w_in", "gmlp_ln_g", "gmlp_ln_b", "gmlp_w_s", "gmlp_b_s", "hgrn_lb_table",
         "hgrn_norm_g", "w_branch_a", "w_branch_b", "w_out", "norm_ffn_g", "w_gate_up", "w_down",
         "norm_final_g")


def kernel(x, norm_mix_g, w_in, gmlp_ln_g, gmlp_ln_b, gmlp_w_s, gmlp_b_s, hgrn_lb_table, hgrn_norm_g, w_branch_a, w_branch_b, w_out, norm_ffn_g, w_gate_up, w_down, norm_final_g, loss_target, m_norm_mix_g, m_w_in, m_gmlp_ln_g, m_gmlp_ln_b, m_gmlp_w_s, m_gmlp_b_s, m_hgrn_lb_table, m_hgrn_norm_g, m_w_branch_a, m_w_branch_b, m_w_out, m_norm_ffn_g, m_w_gate_up, m_w_down, m_norm_final_g, v_norm_mix_g, v_w_in, v_gmlp_ln_g, v_gmlp_ln_b, v_gmlp_w_s, v_gmlp_b_s, v_hgrn_lb_table, v_hgrn_norm_g, v_w_branch_a, v_w_branch_b, v_w_out, v_norm_ffn_g, v_w_gate_up, v_w_down, v_norm_final_g):
    args = dict(locals())
    T = x.shape[1]
    xs = x.reshape(T, D)
    target = loss_target.reshape(T, D)
    big = {n: args[n].reshape(args[n].shape[1:]) for n in BIG}
    big_m = {n: args["m_" + n].reshape(args[n].shape[1:]) for n in BIG}
    big_v = {n: args["v_" + n].reshape(args[n].shape[1:]) for n in BIG}

    x_i, y_i, c_i = lax.axis_index("x"), lax.axis_index("y"), lax.axis_index("c")
    place = jnp.stack([2 * x_i + y_i, c_i]).astype(jnp.int32)
    def by_shape(names):
        groups = []
        for n in names:
            if groups and big[groups[-1][0]].shape == big[n].shape:
                groups[-1].append(n)
            else:
                groups.append([n])
        return groups

    cast = {}
    for grp in by_shape(BIG):
        cast.update(zip(grp, _cast_shards("cast_" + grp[0], place, [big[n] for n in grp],
                                          paired=grp[0] == "w_gate_up")))
    tril = jnp.tril(jnp.ones((GCH, GCH), bool))
    wm = jnp.where(tril, gmlp_w_s[0], 0.0).astype(BF16)
    wm_t = jnp.swapaxes(wm, 1, 2)
    b_t = gmlp_b_s[0].T

    (proj, hb), w_in4, (w_a4, w_b4, w_out4, w_down4) = _proj_fwd(
        place, xs, norm_mix_g, cast["w_in"], [cast[n] for n in ("w_branch_a", "w_branch_b", "w_out", "w_down")])
    (ab,), _ = _gmlp_fwd(proj, gmlp_ln_g, gmlp_ln_b, wm, b_t)
    (o_raw, obb, st_before), (w_gu,) = _hgrn_fwd(
        proj, hgrn_lb_table, hgrn_norm_g, job=_gather_job([cast["w_gate_up"]]))
    w_a, w_b, w_o = (w.reshape(D, D) for w in (w_a4, w_b4, w_out4))
    (mgb, x1), _ = _merge_fwd(xs, ab, obb, proj, w_a, w_b, w_o)
    w_dn = w_down4.reshape(FF, D)
    act, dx2b, h2b, dgu, dx1, dx1b, acc_ffn = _ffn_fwd_bwd(
        x1, target, norm_ffn_g, norm_final_g.reshape(1, D), w_gu, w_dn)

    grads, owns, parts, halves, sibh = {}, {}, {}, {}, {}

    def pair_sums(names, sibs):
        sib_of = dict(zip(names, sibs))
        for grp in by_shape(names):
            o, p = _pair_sums("rs_pair_sum_" + grp[0], place, [grads[n] for n in grp], [sib_of[n] for n in grp])
            owns.update(zip(grp, o))
            parts.update(zip(grp, p))

    def chip_sums(names, got):
        rem_of = dict(zip(names, got))
        for grp in by_shape(names):
            h = _chip_sums("rs_chip_sum_" + grp[0], [owns[n] for n in grp], [rem_of[n] for n in grp])
            halves.update(zip(grp, h))

    ffn, mix = ("w_gate_up", "w_down"), ("w_branch_a", "w_branch_b", "w_out")
    grads["w_gate_up"], _ = _dw_gate_up(h2b, dgu)
    grads["w_down"], _ = _dw_down(act, dx2b)
    (dya, dyb, dproj), got = _merge_bwd(
        dx1b, ab, obb, proj, w_o, w_a, w_b, job=_pair_exchange_job([grads[n] for n in ffn]))
    pair_sums(ffn, got)
    grads["w_branch_a"], _ = _dw_square("dw_branch_a", ab, dya)
    grads["w_branch_b"], _ = _dw_square("dw_branch_b", obb, dyb)
    grads["w_out"], _ = _dw_square("dw_out", mgb, dx1b)
    (dproj, acc_hgrn), got = _hgrn_bwd(
        dproj, dyb, w_b, o_raw, proj, st_before, hgrn_lb_table, hgrn_norm_g,
        job=_join_jobs(_chip_exchange_job([parts[n] for n in ffn]), _pair_exchange_job([grads[n] for n in mix])))
    chip_sums(ffn, got[:2])
    pair_sums(mix, got[2:])
    dproj, acc_ln, dws, dmix = _gmlp_bwd(dproj, dya, w_a, proj, gmlp_ln_g, gmlp_ln_b, wm, wm_t, b_t)
    for_sibling, got = _dw_in_half(
        "dw_in_sibling_half", place, hb, dproj, False,
        job=_join_jobs(_share_halves_job([halves[n] for n in ffn]), _chip_exchange_job([parts[n] for n in mix])))
    sibh.update(zip(ffn, got[:2]))
    chip_sums(mix, got[2:])
    grads["w_in"], got = _dw_in_half(
        "dw_in_own_half", place, hb, dproj, True, job=_share_halves_job([for_sibling]))
    pair_sums(("w_in",), got)
    (grad_x, acc_mix), got = _proj_bwd(
        dproj, w_in4, xs, dx1, norm_mix_g,
        job=_join_jobs(_chip_exchange_job([parts["w_in"]]), _share_halves_job([halves[n] for n in mix])))
    chip_sums(("w_in",), got[:1])
    sibh.update(zip(mix, got[1:]))

    lbv = jax.nn.sigmoid(hgrn_lb_table[0] - hgrn_lb_table[1])
    d_t0 = jnp.sum(acc_hgrn[0], axis=0) * lbv * (1.0 - lbv)
    loss_row = jnp.zeros((D,), F32).at[0].set(jnp.sum(acc_ffn[0]))
    dws_m = jnp.where(tril[:, None, :], dws.reshape(GCH, NG, GCH), 0.0).transpose(1, 0, 2)
    db_s = jnp.sum(dmix.reshape(GCH, NG, GCH), axis=-1).T
    sp = _pack_small(loss_row, jnp.sum(acc_mix, 0), jnp.sum(acc_ln[0], 0), jnp.sum(acc_ln[1], 0), db_s,
                     jnp.stack([d_t0, -d_t0]), jnp.sum(acc_hgrn[1], 0), jnp.sum(acc_ffn[2], 0),
                     jnp.sum(acc_ffn[1], 0), dws_m)
    zero = jnp.zeros((D,), F32)

    def pack(prefix):
        a = lambda n: args[prefix + n]
        return _pack_small(zero, a("norm_mix_g"), a("gmlp_ln_g"), a("gmlp_ln_b"), a("gmlp_b_s"),
                           a("hgrn_lb_table"), a("hgrn_norm_g"), a("norm_ffn_g"), a("norm_final_g"),
                           a("gmlp_w_s"))

    packed, (sibh["w_in"],) = _small_allreduce_adamw(
        sp, jnp.stack([pack(""), pack("m_"), pack("v_")]), _share_halves_job([halves["w_in"]]))
    loss = packed[0][0, 0]
    small = [_unpack_small(p) for p in packed]
    out = {n: tuple(s[n] for s in small) for n in SMALL}
    for grp in by_shape(BIG):
        if grp[0] in ON_SPARSECORE:
            n = grp[0]
            res = [_adamw_sparsecore("adamw_sc_" + n, big[n], halves[n], sibh[n], big_m[n], big_v[n])]
        else:
            res = _adamws("adamw_" + grp[0], place,
                          *[[d[n] for n in grp] for d in (big, halves, sibh, big_m, big_v)])
        for n, quad in zip(grp, res):
            out[n] = tuple(a.reshape(args[n].shape) for a in quad)
    return (loss, grad_x.reshape(x.shape), *[out[n][0] for n in ORDER], *[out[n][1] for n in ORDER],
            *[out[n][2] for n in ORDER], *[out[n][3] for n in ORDER])
```

```python
import functools
import math

import jax
import jax.numpy as jnp
from jax import lax
from jax.experimental import pallas as pl
from jax.experimental.pallas import tpu as pltpu
from jax.experimental.pallas import tpu_sc as plsc

F32 = jnp.float32
BF16 = jnp.bfloat16
SDS = jax.ShapeDtypeStruct
MESH = pl.DeviceIdType.MESH
ANY = pl.BlockSpec(memory_space=pl.ANY)

D = 1024
NIN = 8
NG = 8
GCH = 128
NH = 8
HD = 128
HCH = 64
HGRN_HB = 8
HGRN_TOKENS = 256
GMLP_FWD_TOKENS = 512
GMLP_BWD_TOKENS = 256
HW = HGRN_HB * HD
DW_TOKENS = 2048
DW_IN_TOKENS = 4096
ELEMENTWISE_BLOCK_BYTES = 2 * 1024 * 1024
PROJ_OUT_SLOTS = 4
FF = 2816
FFS = 1408
NCHIP = 4
EPS = 1e-6
QSCALE = HD ** -0.5
GELU_C0 = math.sqrt(2.0 / math.pi)
GELU_C1 = 0.044715
LR, B1, B2, AEPS, WD, STEP = 0.001, 0.9, 0.999, 1e-08, 0.01, 10
VMEM_LIMIT_V7X = 56 * 1024 * 1024
SP_ROWS = 144


def _cparams(**kw):
    return pltpu.CompilerParams(vmem_limit_bytes=VMEM_LIMIT_V7X, **kw)


def _mm(a, b):
    return jnp.dot(a, b, preferred_element_type=F32)


def _mm_nt(a, b):
    return lax.dot_general(a, b, (((1,), (1,)), ((), ())), preferred_element_type=F32)


def _mm_tn(a, b):
    return lax.dot_general(a, b, (((0,), (0,)), ((), ())), preferred_element_type=F32)


def _rows8(x):
    r, c = x.shape
    return jnp.sum(x.reshape(r // 8, 8, c), axis=0)


def _mean(x):
    return jnp.mean(x, axis=-1, keepdims=True)


def _sigmoid(x):
    return 1.0 / (1.0 + jnp.exp(-x))


def _gelu(x):
    t = jnp.tanh(GELU_C0 * (x + GELU_C1 * x * x * x))
    return 0.5 * x * (1.0 + t), t


def _gelu_grad(x, t):
    return 0.5 * (1.0 + t) + 0.5 * x * (1.0 - t * t) * (GELU_C0 * (1.0 + 3.0 * GELU_C1 * x * x))


def _component_of(group):
    return jnp.where(group < 6, (group + 4) % 6, group)


def _proj_fwd(place, x, g_mix, w_in4, later):
    T = x.shape[0]
    tm = min(1024, T)
    ni = T // tm
    n = len(later)

    def body(pc_ref, x_ref, g_ref, *rest):
        proj_ref, h_ref, w_all = rest[1 + n:4 + n]
        gathered = rest[4 + n:4 + 2 * n]
        hs, wbuf, wsem, obuf, osem = rest[4 + 2 * n:9 + 2 * n]
        w_sems, later_sems = rest[9 + 2 * n:15 + 2 * n], rest[15 + 2 * n:]
        jp, i = pl.program_id(0), pl.program_id(1)
        w_cols = [w_all.at[:, :, pl.ds(k * D, D)] for k in range(2)]

        def w_copy(blk):
            cols = pl.ds(pl.multiple_of((blk % 2) * D, 128), D)
            return pltpu.make_async_copy(w_all.at[pc_ref[0] ^ (blk // 2), :, cols], wbuf.at[blk % 2],
                                         wsem.at[blk % 2])

        @pl.when((jp == 0) & (i == 0))
        def _():
            _gather_start(w_cols, w_sems)
            w_copy(jp).start()

        @pl.when(i == 0)
        def _():
            w_copy(jp).wait()

        @pl.when(jp == 0)
        def _():
            xv = x_ref[...]
            r = lax.rsqrt(_mean(xv * xv) + EPS)
            hb = (xv * r * g_ref[...]).astype(BF16)
            hs[i] = hb
            h_ref[...] = hb

        step = jp * ni + i
        slot = step % PROJ_OUT_SLOTS

        def o_copy(slot_):
            comp = 2 * (pc_ref[0] ^ (jp // 2)) + jp % 2
            return pltpu.make_async_copy(
                obuf.at[slot_], proj_ref.at[comp, pl.ds(pl.multiple_of(i * tm, 8), tm)], osem.at[slot_])

        @pl.when(step >= PROJ_OUT_SLOTS)
        def _():
            o_copy(slot).wait()

        obuf[slot] = _mm(hs[i], wbuf[jp % 2])
        o_copy(slot).start()

        @pl.when(step == NIN * ni - 1)
        def _():
            for k in range(PROJ_OUT_SLOTS):
                o_copy((slot + 1 + k) % PROJ_OUT_SLOTS).wait()

        for nxt in range(1, NIN):
            @pl.when((jp == nxt - 1) & (i == ni - 1))
            def _():
                if nxt >= 2:
                    _gather_land([w_cols[nxt % 2]], w_sems, nxt // 2, first=nxt % 2)
                if nxt == 5:
                    _gather_start(gathered, later_sems)
                if nxt == NIN - 1:
                    _gather_neighbours(gathered, later_sems)
                w_copy(jp + 1).start()

        @pl.when((jp == NIN - 1) & (i == ni - 1))
        def _():
            _gather_drain(w_cols, w_sems)
            _gather_finish(gathered, later_sems)

    tile = lambda jp, i, pc: (jnp.where(jp == 0, i, ni - 1), 0)
    res = pl.pallas_call(
        body, name="proj_fwd",
        grid_spec=pltpu.PrefetchScalarGridSpec(
            num_scalar_prefetch=1, grid=(NIN, ni),
            in_specs=[pl.BlockSpec((tm, D), tile), pl.BlockSpec((1, D), lambda jp, i, pc: (0, 0))] + [ANY] * (1 + n),
            out_specs=[ANY, pl.BlockSpec((tm, D), tile)] + [ANY] * (1 + n),
            scratch_shapes=[pltpu.VMEM((ni, tm, D), BF16), pltpu.VMEM((2, D, D), BF16),
                            pltpu.SemaphoreType.DMA((2,)), pltpu.VMEM((PROJ_OUT_SLOTS, tm, D), F32),
                            pltpu.SemaphoreType.DMA((PROJ_OUT_SLOTS,))] + _gather_sems(2) + _gather_sems(n)),
        out_shape=[SDS((NIN, T, D), F32), SDS((T, D), BF16), SDS(w_in4.shape, BF16)]
        + [SDS(a.shape, a.dtype) for a in later],
        input_output_aliases={3 + k: 2 + k for k in range(1 + n)},
        compiler_params=_cparams(has_side_effects=True),
    )(place, x, g_mix, w_in4, *later)
    return res[:2], res[2], res[3:]


def _chunks_abreast(x):
    return jnp.concatenate([x[GCH * ch:GCH * (ch + 1)] for ch in range(x.shape[0] // GCH)], axis=1)


def _chunks_stacked(x):
    return jnp.concatenate([x[:, GCH * ch:GCH * (ch + 1)] for ch in range(x.shape[1] // GCH)], axis=0)


def _layer_norm_stats(gv):
    mu = _mean(gv)
    xc = gv - mu
    rs = lax.rsqrt(_mean(xc * xc) + EPS)
    return xc * rs, rs


def _gmlp_fwd(proj, ln_g, ln_b, wm, b_t, job=None):
    T = proj.shape[1]
    tm = min(GMLP_FWD_TOKENS, T)

    def body(u_ref, v_ref, lg_ref, lb_ref, wm_ref, bt_ref, a_ref, a_s):
        gu, _ = _gelu(u_ref[...])
        gv, _ = _gelu(v_ref[...])
        vhat, _ = _layer_norm_stats(gv)
        vnb = (vhat * lg_ref[...] + lb_ref[...]).astype(BF16)
        for g in range(NG):
            cols = slice(128 * g, 128 * (g + 1))
            mixed = _mm(wm_ref[g], _chunks_abreast(vnb[:, cols])) + bt_ref[:, g:g + 1]
            a_s[:, cols] = gu[:, cols] * _chunks_stacked(mixed)
        a_ref[...] = a_s[...].astype(BF16)

    row = lambda i: (0, 0)
    return _call(
        body, name="gmlp_fwd", grid=(T // tm,), job=job, args=(proj, proj, ln_g, ln_b, wm, b_t),
        in_specs=[pl.BlockSpec((None, tm, D), lambda i: (0, i, 0)), pl.BlockSpec((None, tm, D), lambda i: (1, i, 0)),
                  pl.BlockSpec((1, D), row), pl.BlockSpec((1, D), row),
                  pl.BlockSpec((NG, GCH, GCH), lambda i: (0, 0, 0)), pl.BlockSpec((GCH, NG), row)],
        out_specs=[pl.BlockSpec((tm, D), lambda i: (i, 0))],
        out_shape=[SDS((T, D), BF16)],
        scratch_shapes=[pltpu.VMEM((tm, D), F32)])


def _cumsum64(x, row):
    for s in (1, 2, 4, 8, 16, 32):
        x = x + jnp.where(row >= s, pltpu.roll(x, s, 0), 0.0)
    return x


def _revcumsum64(x, row):
    n = x.shape[0]
    for s in (1, 2, 4, 8, 16, 32):
        x = x + jnp.where(row < HCH - s, pltpu.roll(x, n - s, 0), 0.0)
    return x


def _head_mean(x):
    parts = [jnp.broadcast_to(_mean(x[:, HD * h:HD * (h + 1)]), (x.shape[0], HD)) for h in range(x.shape[1] // HD)]
    return jnp.concatenate(parts, axis=1)


def _seg_sum(x):
    n, c = x.shape
    s = jnp.sum(x.reshape(n // HCH, HCH, c), axis=1, keepdims=True)
    return jnp.broadcast_to(s, (n // HCH, HCH, c)).reshape(n, c)


def _seg_row(x, idx):
    n, c = x.shape
    x3 = x.reshape(n // HCH, HCH, c)
    return jnp.broadcast_to(x3[:, idx:idx + 1, :], x3.shape).reshape(n, c)


def _hgrn_gates(fl, lbv, row):
    s = _sigmoid(fl)
    f = lbv + (1.0 - lbv) * s
    a = _cumsum64(jnp.log(f), row)
    return s, f, a, _seg_row(a, HCH // 2 - 1), _seg_row(a, HCH - 1)


def _hgrn_fwd(proj, lb_table, norm_g, job=None):
    T = proj.shape[1]
    tb = min(HGRN_TOKENS, T)
    nc = tb // HCH

    def body(q_ref, fl_ref, v_ref, g_ref, lbt_ref, gn_ref, o_ref, ob_ref, stb_ref, st_s, o_s):
        @pl.when(pl.program_id(1) == 0)
        def _():
            st_s[...] = jnp.zeros_like(st_s)

        row = lax.broadcasted_iota(jnp.int32, (tb, HW), 0) & (HCH - 1)
        lbv = _sigmoid(lbt_ref[0:1, :] - lbt_ref[1:2, :])
        _, f, a, a_mid, a_last = _hgrn_gates(fl_ref[...], lbv, row)
        k = 1.0 - f
        qs = q_ref[...] * QSCALE
        q_in = (qs * jnp.exp(a - a_mid)).astype(BF16)
        k_in = (k * jnp.exp(a_mid - a)).astype(BF16)
        q_a = (qs * jnp.exp(a)).astype(BF16)
        k_d = (k * jnp.exp(a_last - a)).astype(BF16)
        dec = jnp.exp(a_last)
        vb = v_ref[...].astype(BF16)
        tri = (lax.broadcasted_iota(jnp.int32, (HCH, HCH), 0)
               >= lax.broadcasted_iota(jnp.int32, (HCH, HCH), 1))
        for c in range(nc):
            sl = slice(HCH * c, HCH * (c + 1))
            for hh in range(HGRN_HB):
                hs = slice(HD * hh, HD * (hh + 1))
                st = st_s[hh]
                stb_ref[hh, c] = st
                sc = jnp.where(tri, _mm_nt(q_in[sl, hs], k_in[sl, hs]), 0.0)
                o_s[sl, hs] = _mm(sc.astype(BF16), vb[sl, hs]) + _mm_nt(q_a[sl, hs], st.astype(BF16))
                d64 = dec[sl, hs]
                st_s[hh] = st * jnp.concatenate([d64, d64], axis=0) + _mm_tn(vb[sl, hs], k_d[sl, hs])
        o = o_s[...]
        r = lax.rsqrt(_head_mean(o * o) + EPS)
        g = g_ref[...]
        o_ref[...] = o
        ob_ref[...] = (o * r * gn_ref[...] * (g * _sigmoid(g))).astype(BF16)

    def col(off):
        return pl.BlockSpec((None, tb, HW), lambda h, cb: (off, cb, h))

    return _call(
        body, name="hgrn_fwd", grid=(NH // HGRN_HB, T // tb), job=job,
        args=(proj, proj, proj, proj, lb_table, norm_g),
        in_specs=[col(2), col(3), col(4), col(5),
                  pl.BlockSpec((2, HW), lambda h, cb: (0, h)), pl.BlockSpec((1, HW), lambda h, cb: (0, h))],
        out_specs=[pl.BlockSpec((tb, HW), lambda h, cb: (cb, h)), pl.BlockSpec((tb, HW), lambda h, cb: (cb, h)),
                   pl.BlockSpec((HGRN_HB, nc, HD, HD), lambda h, cb: (h, cb, 0, 0))],
        out_shape=[SDS((T, D), F32), SDS((T, D), BF16), SDS((NH, T // HCH, HD, HD), F32)],
        scratch_shapes=[pltpu.VMEM((HGRN_HB, HD, HD), F32), pltpu.VMEM((tb, HW), F32)])


def _merge_fwd(x, ab, ob, proj, w_a, w_b, w_out, job=None):
    T = x.shape[0]
    tm = min(512, T)

    def body(x_ref, ab_ref, ob_ref, ga_ref, gb_ref, wa_ref, wb_ref, wo_ref, mg_ref, x1_ref):
        ya = _mm(ab_ref[...], wa_ref[...])
        yb = _mm(ob_ref[...], wb_ref[...])
        merged = (_sigmoid(ga_ref[...]) * ya + _sigmoid(gb_ref[...]) * yb).astype(BF16)
        mg_ref[...] = merged
        x1_ref[...] = x_ref[...] + _mm(merged, wo_ref[...])

    t = lambda i: (i, 0)
    w = lambda i: (0, 0)
    return _call(
        body, name="merge_fwd", grid=(T // tm,), job=job, args=(x, ab, ob, proj, proj, w_a, w_b, w_out),
        in_specs=[pl.BlockSpec((tm, D), t), pl.BlockSpec((tm, D), t), pl.BlockSpec((tm, D), t),
                  pl.BlockSpec((None, tm, D), lambda i: (6, i, 0)), pl.BlockSpec((None, tm, D), lambda i: (7, i, 0)),
                  pl.BlockSpec((D, D), w), pl.BlockSpec((D, D), w), pl.BlockSpec((D, D), w)],
        out_specs=[pl.BlockSpec((tm, D), t)] * 2,
        out_shape=[SDS((T, D), BF16), SDS((T, D), F32)])


def _ffn_fwd_bwd(x1, target, g_ffn, g_fin, w_gu, w_down):
    T = x1.shape[0]
    tm = min(256, T)
    inv_d = 1.0 / D

    def body(x1_ref, tg_ref, gf_ref, gn_ref, wgu_ref, wd_ref,
             act_ref, dx2b_ref, h2b_ref, dgu_ref, dx1_ref, dx1b_ref, acc_ref):
        @pl.when(pl.program_id(0) == 0)
        def _():
            acc_ref[...] = jnp.zeros_like(acc_ref)

        x1v = x1_ref[...]
        gf = gf_ref[...]
        gn = gn_ref[...]
        rr1 = lax.rsqrt(_mean(x1v * x1v) + EPS)
        x1n = x1v * rr1
        h2b = (x1n * gf).astype(BF16)
        h2b_ref[...] = h2b
        gate = _mm(h2b, wgu_ref[0])
        up = _mm(h2b, wgu_ref[1])
        sg = _sigmoid(gate)
        si = gate * sg
        act = (si * up).astype(BF16)
        act_ref[...] = act
        x2 = x1v + _mm(act, wd_ref[...])
        rr2 = lax.rsqrt(_mean(x2 * x2) + EPS)
        x2n = x2 * rr2
        e = x2n * gn - tg_ref[...]
        acc_ref[0] += _rows8(e * e) * (0.5 * inv_d)
        dy = e * inv_d
        acc_ref[1] += _rows8(dy * x2n)
        dxn = dy * gn
        dx2 = rr2 * (dxn - x2n * _mean(dxn * x2n))
        dx2b = dx2.astype(BF16)
        dx2b_ref[...] = dx2b
        dact = _mm_nt(dx2b, wd_ref[...])
        dgate = (dact * up * (sg * (1.0 + gate * (1.0 - sg)))).astype(BF16)
        dup = (dact * si).astype(BF16)
        dgu_ref[0] = dgate
        dgu_ref[1] = dup
        dh2 = _mm_nt(dgate, wgu_ref[0]) + _mm_nt(dup, wgu_ref[1])
        acc_ref[2] += _rows8(dh2 * x1n)
        dxn1 = dh2 * gf
        dx1 = dx2 + rr1 * (dxn1 - x1n * _mean(dxn1 * x1n))
        dx1_ref[...] = dx1
        dx1b_ref[...] = dx1.astype(BF16)

    t = lambda i: (i, 0)
    w = lambda i: (0, 0)
    one = pl.Buffered(1)
    return pl.pallas_call(
        body, name="ffn_fwd_bwd", grid=(T // tm,),
        in_specs=[pl.BlockSpec((tm, D), t), pl.BlockSpec((tm, D), t),
                  pl.BlockSpec((1, D), w), pl.BlockSpec((1, D), w),
                  pl.BlockSpec((2, D, FF), lambda i: (0, 0, 0), pipeline_mode=one),
                  pl.BlockSpec((FF, D), w, pipeline_mode=one)],
        out_specs=[pl.BlockSpec((tm, FF), t), pl.BlockSpec((tm, D), t), pl.BlockSpec((tm, D), t),
                   pl.BlockSpec((2, tm, FF), lambda i: (0, i, 0)),
                   pl.BlockSpec((tm, D), t), pl.BlockSpec((tm, D), t),
                   pl.BlockSpec((3, 8, D), lambda i: (0, 0, 0))],
        out_shape=[SDS((T, FF), BF16), SDS((T, D), BF16), SDS((T, D), BF16),
                   SDS((2, T, FF), BF16), SDS((T, D), F32), SDS((T, D), BF16),
                   SDS((3, 8, D), F32)],
        compiler_params=_cparams(),
    )(x1, target, g_ffn, g_fin, w_gu, w_down)


def _merge_bwd(dx1b, ab, ob, proj, w_out, w_a, w_b, job=None):
    T = dx1b.shape[0]
    tm = min(512, T)

    def body(dx_ref, ab_ref, ob_ref, ga_ref, gb_ref, wo_ref, wa_ref, wb_ref, dya_ref, dyb_ref, dp_ref):
        dm = _mm_nt(dx_ref[...], wo_ref[...])
        sa = _sigmoid(ga_ref[...])
        sb = _sigmoid(gb_ref[...])
        dya_ref[...] = (dm * sa).astype(BF16)
        dyb_ref[...] = (dm * sb).astype(BF16)
        dp_ref[0] = (dm * _mm(ab_ref[...], wa_ref[...]) * sa * (1.0 - sa)).astype(BF16)
        dp_ref[1] = (dm * _mm(ob_ref[...], wb_ref[...]) * sb * (1.0 - sb)).astype(BF16)

    t = lambda i: (i, 0)
    w = lambda i: (0, 0)
    return _call(
        body, name="merge_bwd", grid=(T // tm,),
        in_specs=[pl.BlockSpec((tm, D), t), pl.BlockSpec((tm, D), t), pl.BlockSpec((tm, D), t),
                  pl.BlockSpec((None, tm, D), lambda i: (6, i, 0)), pl.BlockSpec((None, tm, D), lambda i: (7, i, 0)),
                  pl.BlockSpec((D, D), w), pl.BlockSpec((D, D), w), pl.BlockSpec((D, D), w)],
        out_specs=[pl.BlockSpec((tm, D), t)] * 2 + [pl.BlockSpec((2, tm, D), lambda i: (3, i, 0))],
        out_shape=[SDS((T, D), BF16), SDS((T, D), BF16), SDS((NIN, T, D), BF16)],
        args=(dx1b, ab, ob, proj, proj, w_out, w_a, w_b), job=job)


def _hgrn_bwd(dproj, dyb, w_b, o_raw, proj, st_before, lb_table, norm_g, job=None):
    T = dyb.shape[0]
    tb = min(HGRN_TOKENS, T)
    nc = tb // HCH
    nb = T // tb

    def body(dp_in, dyb_ref, wb_ref, o_ref, q_ref, fl_ref, v_ref, g_ref, stb_ref, lbt_ref, gn_ref,
             dp_ref, acc_ref, dst_s, dqin_s, dqa_s, dkin_s, dkd_s, dv_s, ddec_s):
        del dp_in

        @pl.when(pl.program_id(1) == 0)
        def _():
            dst_s[...] = jnp.zeros_like(dst_s)
            acc_ref[...] = jnp.zeros_like(acc_ref)

        row = lax.broadcasted_iota(jnp.int32, (tb, HW), 0) & (HCH - 1)
        gn = gn_ref[...]
        lbv = _sigmoid(lbt_ref[0:1, :] - lbt_ref[1:2, :])
        o = o_ref[...]
        r = lax.rsqrt(_head_mean(o * o) + EPS)
        on = o * r
        g = g_ref[...]
        sgm = _sigmoid(g)
        dob_v = _mm_nt(dyb_ref[...], wb_ref[...])
        dp_ref[3] = (dob_v * on * gn * (sgm * (1.0 + g * (1.0 - sgm)))).astype(BF16)
        do_n = dob_v * (g * sgm)
        acc_ref[1] += _rows8(do_n * on)
        dxn = do_n * gn
        do = (r * (dxn - on * _head_mean(dxn * on))).astype(BF16)
        s, f, a, a_mid, a_last = _hgrn_gates(fl_ref[...], lbv, row)
        k = 1.0 - f
        qs = q_ref[...] * QSCALE
        e_q = jnp.exp(a - a_mid)
        e_k = jnp.exp(a_mid - a)
        e_a = jnp.exp(a)
        e_l = jnp.exp(a_last - a)
        dec = jnp.exp(a_last)
        q_in = qs * e_q
        k_in = k * e_k
        q_a = qs * e_a
        k_d = k * e_l
        q_inb, k_inb, q_ab, k_db = (z.astype(BF16) for z in (q_in, k_in, q_a, k_d))
        vb = v_ref[...].astype(BF16)
        tri = (lax.broadcasted_iota(jnp.int32, (HCH, HCH), 0)
               >= lax.broadcasted_iota(jnp.int32, (HCH, HCH), 1))
        for c in reversed(range(nc)):
            sl = slice(HCH * c, HCH * (c + 1))
            for hh in range(HGRN_HB):
                hs = slice(HD * hh, HD * (hh + 1))
                stp = stb_ref[hh, c]
                dst = dst_s[hh]
                dstb = dst.astype(BF16)
                do_c = do[sl, hs]
                v_c = vb[sl, hs]
                dqa_s[sl, hs] = _mm(do_c, stp.astype(BF16))
                dkd_s[sl, hs] = _mm(v_c, dstb)
                ddec_s[sl, hs] = jnp.broadcast_to(jnp.sum(dst * stp, axis=0, keepdims=True), (HCH, HD))
                sc = jnp.where(tri, _mm_nt(q_inb[sl, hs], k_inb[sl, hs]), 0.0).astype(BF16)
                dsc = jnp.where(tri, _mm_nt(do_c, v_c), 0.0).astype(BF16)
                dv_s[sl, hs] = _mm_nt(k_db[sl, hs], dstb) + _mm_tn(sc, do_c)
                dqin_s[sl, hs] = _mm(dsc, k_inb[sl, hs])
                dkin_s[sl, hs] = _mm_tn(dsc, q_inb[sl, hs])
                d64 = dec[sl, hs]
                dst_s[hh] = dst * jnp.concatenate([d64, d64], axis=0) + _mm_tn(do_c, q_ab[sl, hs])
        dq_in = dqin_s[...]
        dq_a = dqa_s[...]
        dk_in = dkin_s[...]
        dk_d = dkd_s[...]
        dp_ref[0] = ((dq_in * e_q + dq_a * e_a) * QSCALE).astype(BF16)
        dp_ref[2] = dv_s[...].astype(BF16)
        tq = dq_in * q_in
        tk = dk_in * k_in
        td = dk_d * k_d
        d_a = tq + dq_a * q_a - tk - td
        d_a = d_a + jnp.where(row == HCH // 2 - 1, _seg_sum(tk - tq), 0.0)
        d_a = d_a + jnp.where(row == HCH - 1, _seg_sum(td) + ddec_s[...] * dec, 0.0)
        dlf = _revcumsum64(d_a, row)
        df = dlf / f - (dk_in * e_k + dk_d * e_l)
        dp_ref[1] = (df * (1.0 - lbv) * s * (1.0 - s)).astype(BF16)
        acc_ref[0] += _rows8(df * (1.0 - s))

    def col(off):
        return pl.BlockSpec((None, tb, HW), lambda h, cb: (off, nb - 1 - cb, h))

    hb = lambda h, cb: (nb - 1 - cb, h)
    return _call(
        body, name="hgrn_bwd", grid=(NH // HGRN_HB, nb), job=job,
        args=(dproj, dyb, w_b, o_raw, proj, proj, proj, proj, st_before, lb_table, norm_g),
        in_specs=[ANY, pl.BlockSpec((tb, D), lambda h, cb: (nb - 1 - cb, 0)),
                  pl.BlockSpec((HW, D), lambda h, cb: (h, 0)), pl.BlockSpec((tb, HW), hb),
                  col(2), col(3), col(4), col(5),
                  pl.BlockSpec((HGRN_HB, nc, HD, HD), lambda h, cb: (h, nb - 1 - cb, 0, 0)),
                  pl.BlockSpec((2, HW), lambda h, cb: (0, h)), pl.BlockSpec((1, HW), lambda h, cb: (0, h))],
        out_specs=[pl.BlockSpec((4, tb, HW), lambda h, cb: (0, nb - 1 - cb, h)),
                   pl.BlockSpec((2, 8, HW), lambda h, cb: (0, 0, h))],
        out_shape=[SDS(dproj.shape, BF16), SDS((2, 8, D), F32)],
        scratch_shapes=[pltpu.VMEM((HGRN_HB, HD, HD), F32)] + [pltpu.VMEM((tb, HW), F32)] * 6,
        aliases={0: 0})


def _gmlp_bwd(dproj, dya, w_a, proj, ln_g, ln_b, wm, wm_t, b_t):
    T = dya.shape[0]
    tm = min(GMLP_BWD_TOKENS, T)

    def body(dp_in, dya_ref, wa_ref, u_ref, v_ref, lg_ref, lb_ref, wm_ref, wmt_ref, bt_ref,
             dp_ref, acc_ref, dws_ref, dmix_ref, du_s, dvn_s):
        del dp_in

        @pl.when(pl.program_id(0) == 0)
        def _():
            acc_ref[...] = jnp.zeros_like(acc_ref)
            dws_ref[...] = jnp.zeros_like(dws_ref)
            dmix_ref[...] = jnp.zeros_like(dmix_ref)

        u = u_ref[...]
        v = v_ref[...]
        lg = lg_ref[...]
        gu, t_u = _gelu(u)
        gv, t_v = _gelu(v)
        vhat, rs = _layer_norm_stats(gv)
        vnb = (vhat * lg + lb_ref[...]).astype(BF16)
        da_v = _mm_nt(dya_ref[...], wa_ref[...])
        for g in range(NG):
            cols = slice(128 * g, 128 * (g + 1))
            vng = _chunks_abreast(vnb[:, cols])
            mixed = _mm(wm_ref[g], vng) + bt_ref[:, g:g + 1]
            dag = _chunks_abreast(da_v[:, cols])
            dmx = dag * _chunks_abreast(gu[:, cols])
            du_s[:, cols] = _chunks_stacked(dag * mixed)
            dmxb = dmx.astype(BF16)
            dws_ref[:, cols] += _mm_nt(dmxb, vng)
            dmix_ref[:, cols] += sum(dmx[:, GCH * ch:GCH * (ch + 1)] for ch in range(tm // GCH))
            dvn_s[:, cols] = _chunks_stacked(_mm(wmt_ref[g], dmxb))
        dp_ref[0] = (du_s[...] * _gelu_grad(u, t_u)).astype(BF16)
        dvn = dvn_s[...]
        acc_ref[0] += _rows8(dvn * vhat)
        acc_ref[1] += _rows8(dvn)
        dvh = dvn * lg
        dgv = rs * (dvh - _mean(dvh) - vhat * _mean(dvh * vhat))
        dp_ref[1] = (dgv * _gelu_grad(v, t_v)).astype(BF16)

    row = lambda i: (0, 0)
    w3 = lambda i: (0, 0, 0)
    return pl.pallas_call(
        body, name="gmlp_bwd", grid=(T // tm,),
        in_specs=[ANY, pl.BlockSpec((tm, D), lambda i: (i, 0)), pl.BlockSpec((D, D), row),
                  pl.BlockSpec((None, tm, D), lambda i: (0, i, 0)), pl.BlockSpec((None, tm, D), lambda i: (1, i, 0)),
                  pl.BlockSpec((1, D), row), pl.BlockSpec((1, D), row),
                  pl.BlockSpec((NG, GCH, GCH), w3), pl.BlockSpec((NG, GCH, GCH), w3),
                  pl.BlockSpec((GCH, NG), row)],
        out_specs=[pl.BlockSpec((2, tm, D), lambda i: (2, i, 0)),
                   pl.BlockSpec((2, 8, D), w3), pl.BlockSpec((GCH, D), row), pl.BlockSpec((GCH, D), row)],
        out_shape=[SDS(dproj.shape, BF16), SDS((2, 8, D), F32), SDS((GCH, D), F32), SDS((GCH, D), F32)],
        scratch_shapes=[pltpu.VMEM((tm, D), F32), pltpu.VMEM((tm, D), F32)],
        input_output_aliases={0: 0},
        compiler_params=_cparams(),
    )(dproj, dya, w_a, proj, proj, ln_g, ln_b, wm, wm_t, b_t)


def _proj_bwd(dproj, w_in4, x, dx1, g_mix, job=None):
    T = x.shape[0]
    tm = min(256, T)
    order = (2, 3, 4, 5, 0, 1, 6, 7)

    def body(dp_ref, w_ref, x_ref, dx1_ref, g_ref, gx_ref, acc_ref):
        @pl.when(pl.program_id(0) == 0)
        def _():
            acc_ref[...] = jnp.zeros_like(acc_ref)

        dh = None
        for m, og in enumerate(order):
            part = _mm_nt(dp_ref[m], w_ref[og // 2, :, D * (og % 2):D * (og % 2 + 1)])
            dh = part if dh is None else dh + part
        xv = x_ref[...]
        r = lax.rsqrt(_mean(xv * xv) + EPS)
        xn = xv * r
        acc_ref[...] += _rows8(dh * xn)
        dxn = dh * g_ref[...]
        gx_ref[...] = dx1_ref[...] + r * (dxn - xn * _mean(dxn * xn))

    t = lambda i: (i, 0)
    return _call(
        body, name="proj_bwd", grid=(T // tm,),
        in_specs=[pl.BlockSpec((NIN, tm, D), lambda i: (0, i, 0)),
                  pl.BlockSpec((NCHIP, D, 2 * D), lambda i: (0, 0, 0), pipeline_mode=pl.Buffered(1)),
                  pl.BlockSpec((tm, D), t), pl.BlockSpec((tm, D), t), pl.BlockSpec((1, D), lambda i: (0, 0))],
        out_specs=[pl.BlockSpec((tm, D), t), pl.BlockSpec((8, D), lambda i: (0, 0))],
        out_shape=[SDS((T, D), F32), SDS((8, D), F32)],
        args=(dproj, w_in4, x, dx1, g_mix), job=job)


def _dw_call(name, a, b, a_spec, b_spec, o_spec, out_shape, nblk, tt, job=None, prefetch=None):
    T = a.shape[-2]

    def body(*refs):
        a_ref, b_ref, o_ref = refs[-3:]

        @pl.when(pl.program_id(1) == 0)
        def _():
            o_ref[...] = jnp.zeros_like(o_ref)
        o_ref[...] += _mm_tn(a_ref[...], b_ref[...])

    (out,), job_out = _call(
        body, name=name, grid=(nblk, T // tt), in_specs=[a_spec, b_spec], out_specs=[o_spec],
        out_shape=[out_shape], args=(a, b), job=job, prefetch=prefetch)
    return out, job_out


def _dw_in_half(name, place, hb, dproj, mine, job=None):
    tt = min(DW_IN_TOKENS, hb.shape[0])

    def comp(k, pc):
        return _component_of(2 * k + (pc[1] if mine else 1 - pc[1]))

    return _dw_call(
        name, hb, dproj,
        pl.BlockSpec((tt, D), lambda k, t, pc: (t, 0)),
        pl.BlockSpec((None, tt, D), lambda k, t, pc: (comp(k, pc), t, 0)),
        pl.BlockSpec((None, D, D), lambda k, t, pc: (k, 0, 0)),
        SDS((NCHIP, D, D), F32), NCHIP, tt, job, place)


def _dw_gate_up(h2b, dgu, job=None):
    tt = min(DW_TOKENS, h2b.shape[0])
    return _dw_call(
        "dw_gate_up", h2b, dgu,
        pl.BlockSpec((tt, D), lambda k, t: (t, 0)),
        pl.BlockSpec((None, tt, FFS), lambda k, t: (k // 2, t, k % 2)),
        pl.BlockSpec((None, D, FFS), lambda k, t: (k, 0, 0)),
        SDS((NCHIP, D, FFS), F32), NCHIP, tt, job)


def _dw_down(act, dx2b, job=None):
    tt = min(DW_TOKENS, act.shape[0])
    g, job_out = _dw_call(
        "dw_down", act, dx2b,
        pl.BlockSpec((tt, FFS), lambda k, t: (t, k)),
        pl.BlockSpec((tt, D), lambda k, t: (t, 0)),
        pl.BlockSpec((FFS, D), lambda k, t: (k, 0)),
        SDS((FF, D), F32), 2, tt, job)
    return g.reshape(NCHIP, FF // NCHIP, D), job_out


def _dw_square(name, a, b, job=None):
    tt = min(DW_TOKENS, a.shape[0])
    g, job_out = _dw_call(
        name, a, b,
        pl.BlockSpec((tt, D), lambda k, t: (t, 0)), pl.BlockSpec((tt, D), lambda k, t: (t, 0)),
        pl.BlockSpec((D, D), lambda k, t: (0, 0)), SDS((D, D), F32), 1, tt, job)
    return g.reshape(NCHIP, D // NCHIP, D), job_out


def _place():
    x, y, c = lax.axis_index("x"), lax.axis_index("y"), lax.axis_index("c")
    return x, y, c, 2 * x + y


def _chip_at(x, y, s):
    return x ^ (s >> 1), y ^ (s & 1)


class _Job:
    def __init__(self, ins, out_shapes, sems, start, finish, aliases=None, mid=None):
        self.ins, self.out_shapes, self.sems = list(ins), list(out_shapes), list(sems)
        self.start, self.finish, self.aliases = start, finish, dict(aliases or {})
        self.mid = mid if mid is not None else (lambda ins, outs, sems: None)


def _join_jobs(*jobs):
    def cut(refs, sizes):
        out, at = [], 0
        for n in sizes:
            out.append(refs[at:at + n])
            at += n
        return out

    ni = [len(j.ins) for j in jobs]
    no = [len(j.out_shapes) for j in jobs]
    ns = [len(j.sems) for j in jobs]

    def run(which):
        def go(ins, outs, sems):
            for j, a, b, c in zip(jobs, cut(ins, ni), cut(outs, no), cut(sems, ns)):
                getattr(j, which)(a, b, c)
        return go

    aliases = {}
    for k, j in enumerate(jobs):
        for a, b in j.aliases.items():
            aliases[sum(ni[:k]) + a] = sum(no[:k]) + b
    return _Job([a for j in jobs for a in j.ins], [o for j in jobs for o in j.out_shapes],
                [s for j in jobs for s in j.sems], run("start"), run("finish"), aliases, run("mid"))


def _call(body, *, name, grid, in_specs, out_specs, out_shape, args, scratch_shapes=(), aliases=None,
          job=None, prefetch=None):
    n_in, n_out, n_scr = len(in_specs), len(out_specs), len(scratch_shapes)
    npf = 0 if prefetch is None else 1
    job = job if job is not None else _Job([], [], [], lambda *a: None, lambda *a: None)
    ji, jo = len(job.ins), len(job.out_shapes)
    steps = math.prod(grid)

    def wrapped(*refs):
        pf, refs = refs[:npf], refs[npf:]
        ins, jin = refs[:n_in], refs[n_in:n_in + ji]
        o0 = n_in + ji
        outs, jout = refs[o0:o0 + n_out], refs[o0 + n_out:o0 + n_out + jo]
        s0 = o0 + n_out + jo
        scr, jsem = refs[s0:s0 + n_scr], refs[s0 + n_scr:]
        step = functools.reduce(lambda acc, ag: acc * ag[1] + pl.program_id(ag[0]), enumerate(grid), 0)
        if ji or jo:
            @pl.when(step == 0)
            def _():
                job.start(jin, jout, jsem)

        body(*pf, *ins, *outs, *scr)

        if ji or jo:
            @pl.when(step == steps // 2)
            def _():
                job.mid(jin, jout, jsem)

            @pl.when(step == steps - 1)
            def _():
                job.finish(jin, jout, jsem)

    io = {npf + a: b for a, b in dict(aliases or {}).items()}
    io.update({npf + n_in + a: n_out + b for a, b in job.aliases.items()})
    kw = dict(in_specs=list(in_specs) + [ANY] * ji, out_specs=list(out_specs) + [ANY] * jo,
              scratch_shapes=list(scratch_shapes) + job.sems)
    if npf:
        kw = dict(grid_spec=pltpu.PrefetchScalarGridSpec(num_scalar_prefetch=1, grid=grid, **kw))
    else:
        kw["grid"] = grid
    res = pl.pallas_call(
        wrapped, name=name, out_shape=list(out_shape) + job.out_shapes, input_output_aliases=io,
        compiler_params=_cparams(has_side_effects=bool(ji or jo)), **kw,
    )(*(() if prefetch is None else (prefetch,)), *args, *job.ins)
    return list(res[:n_out]), list(res[n_out:])


def _cast_shards(name, place, ws, paired=False):
    n = len(ws)
    rows, cols = ws[0].shape
    tr = 352 if rows % 352 == 0 else 256
    shape = (2, rows, 2 * cols) if paired else (NCHIP, rows, cols)
    mine = (lambda i, pc: (pc[0] // 2, i, pc[0] % 2)) if paired else (lambda i, pc: (pc[0], i, 0))

    def body(pc_ref, *refs):
        del pc_ref
        for w_ref, o_ref in zip(refs[:n], refs[n:]):
            o_ref[...] = w_ref[...].astype(BF16)

    return pl.pallas_call(
        body, name=name,
        grid_spec=pltpu.PrefetchScalarGridSpec(
            num_scalar_prefetch=1, grid=(rows // tr,),
            in_specs=[pl.BlockSpec((tr, cols), lambda i, pc: (i, 0))] * n,
            out_specs=[pl.BlockSpec((None, tr, cols), mine)] * n),
        out_shape=[SDS(shape, BF16)] * n,
        compiler_params=_cparams(),
    )(place, *ws)


def _sibling_copy(ref, send_sem, recv_sem):
    x, y, c, _ = _place()
    return pltpu.make_async_remote_copy(src_ref=ref, dst_ref=ref, send_sem=send_sem, recv_sem=recv_sem,
                                        device_id=(x, y, 1 - c), device_id_type=MESH)


def _slot(arr, chip):
    if arr.shape[0] == NCHIP:
        return arr.at[chip]
    cols = arr.shape[2] // 2
    return arr.at[chip // 2, :, pl.ds(pl.multiple_of((chip % 2) * cols, 128), cols)]


def _half_rows(arr, slot, core):
    half = arr.shape[1] // 2
    return _slot(arr, slot).at[pl.ds(pl.multiple_of(core * half, 16), half)]


def _quarter_rows(arr, slot, core, q):
    quarter = arr.shape[1] // 4
    return _slot(arr, slot).at[pl.ds(pl.multiple_of((2 * core + q) * quarter, 16), quarter)]


def _chip_copy(ref, dist, send_sem, recv_sem):
    x, y, c, _ = _place()
    cx, cy = _chip_at(x, y, dist)
    return pltpu.make_async_remote_copy(src_ref=ref, dst_ref=ref, send_sem=send_sem, recv_sem=recv_sem,
                                        device_id=(cx, cy, c), device_id_type=MESH)


def _gather_sems(n):
    dma = pltpu.SemaphoreType.DMA
    return [dma((n, 2))] * 4 + [dma((n, 4))] * 2


def _gather_start(arrs, sems):
    dsend, drecv = sems[0], sems[1]
    _, _, c, j = _place()
    for w, arr in enumerate(arrs):
        for dist in (1, 2):
            _chip_copy(_half_rows(arr, j, c), dist, dsend.at[w, dist - 1], drecv.at[w, dist - 1]).start()


def _gather_land(arrs, sems, dist, first=0):
    dsend, drecv, rsend, rrecv, fsend, frecv = sems
    _, _, c, j = _place()
    if dist < 3:
        other = 3 - dist
        for w, arr in enumerate(arrs, first):
            landed = _half_rows(arr, j ^ dist, c)
            _chip_copy(landed, dist, dsend.at[w, dist - 1], drecv.at[w, dist - 1]).wait_recv()
            relay = _quarter_rows(arr, j ^ dist, c, other - 1)
            _chip_copy(relay, other, rsend.at[w, other - 1], rrecv.at[w, other - 1]).start()
            _sibling_copy(landed, fsend.at[w, dist - 1], frecv.at[w, dist - 1]).start()
        for w, arr in enumerate(arrs, first):
            theirs = _half_rows(arr, j ^ dist, 1 - c)
            _sibling_copy(theirs, fsend.at[w, dist - 1], frecv.at[w, dist - 1]).wait_recv()
    else:
        for w, arr in enumerate(arrs, first):
            for via in (1, 2):
                piece = _quarter_rows(arr, j ^ 3, c, via - 1)
                _chip_copy(piece, via, rsend.at[w, via - 1], rrecv.at[w, via - 1]).wait_recv()
                _sibling_copy(piece, fsend.at[w, 1 + via], frecv.at[w, 1 + via]).start()
        for w, arr in enumerate(arrs, first):
            for via in (1, 2):
                theirs = _quarter_rows(arr, j ^ 3, 1 - c, via - 1)
                _sibling_copy(theirs, fsend.at[w, 1 + via], frecv.at[w, 1 + via]).wait_recv()


def _gather_drain(arrs, sems):
    dsend, drecv, rsend, rrecv, fsend, frecv = sems
    _, _, c, j = _place()
    for w, arr in enumerate(arrs):
        for dist in (1, 2):
            other = 3 - dist
            _chip_copy(_half_rows(arr, j, c), dist, dsend.at[w, dist - 1], drecv.at[w, dist - 1]).wait_send()
            _chip_copy(_quarter_rows(arr, j ^ dist, c, other - 1), other,
                       rsend.at[w, other - 1], rrecv.at[w, other - 1]).wait_send()
            _sibling_copy(_half_rows(arr, j ^ dist, c), fsend.at[w, dist - 1], frecv.at[w, dist - 1]).wait_send()
            _sibling_copy(_quarter_rows(arr, j ^ 3, c, dist - 1),
                          fsend.at[w, 1 + dist], frecv.at[w, 1 + dist]).wait_send()


def _gather_neighbours(arrs, sems):
    _gather_land(arrs, sems, 1)
    _gather_land(arrs, sems, 2)


def _gather_finish(arrs, sems):
    _gather_land(arrs, sems, 3)
    _gather_drain(arrs, sems)


def _gather_job(arrs):
    n = len(arrs)
    return _Job(arrs, [SDS(a.shape, a.dtype) for a in arrs], _gather_sems(n),
                lambda ins, outs, sems: _gather_start(outs, sems),
                lambda ins, outs, sems: _gather_finish(outs, sems), {k: k for k in range(n)},
                mid=lambda ins, outs, sems: _gather_neighbours(outs, sems))


def _exchange_job(arrs, out_shapes, n, copies):
    def start(ins, outs, sems):
        for cp in copies(ins, outs, sems[0], sems[1]):
            cp.start()

    def finish(ins, outs, sems):
        for cp in copies(ins, outs, sems[0], sems[1]):
            cp.wait()

    return _Job(arrs, out_shapes, [pltpu.SemaphoreType.DMA((n,))] * 2, start, finish)


def _pair_exchange_job(grads):
    def copies(ins, outs, send_sem, recv_sem):
        x, y, c, _ = _place()
        res = []
        for w in range(len(grads)):
            half = ins[w].shape[1] // 2
            theirs = pl.ds(pl.multiple_of((1 - c) * half, 8), half)
            res.append(pltpu.make_async_remote_copy(
                src_ref=ins[w].at[:, theirs, :], dst_ref=outs[w], send_sem=send_sem.at[w],
                recv_sem=recv_sem.at[w], device_id=(x, y, 1 - c), device_id_type=MESH))
        return res

    return _exchange_job(grads, [SDS((NCHIP, g.shape[1] // 2, g.shape[2]), F32) for g in grads],
                         len(grads), copies)


def _row_tile(rows, cols):
    tr = rows
    while tr * cols * 4 > ELEMENTWISE_BLOCK_BYTES and tr % 32 == 0:
        tr //= 2
    return tr


def _pair_sums(name, place, gs, sibs):
    n = len(gs)
    half, cols = sibs[0].shape[1], sibs[0].shape[2]
    tr = _row_tile(half, cols)
    nt = half // tr
    mine = nt if gs[0].shape[1] == 2 * half else 0

    def body(pc_ref, *refs):
        del pc_ref
        for g_ref, s_ref, own_ref, out_ref in zip(refs[:n], refs[n:2 * n], refs[2 * n:3 * n], refs[3 * n:]):
            v = g_ref[...] + s_ref[...]

            @pl.when(pl.program_id(1) == 0)
            def _():
                own_ref[...] = v

            @pl.when(pl.program_id(1) > 0)
            def _():
                out_ref[...] = v.astype(BF16)

    res = pl.pallas_call(
        body, name=name,
        grid_spec=pltpu.PrefetchScalarGridSpec(
            num_scalar_prefetch=1, grid=(nt, NCHIP),
            in_specs=[pl.BlockSpec((None, tr, cols), lambda i, s, pc: (pc[0] ^ s, pc[1] * mine + i, 0))] * n
            + [pl.BlockSpec((None, tr, cols), lambda i, s, pc: (pc[0] ^ s, i, 0))] * n,
            out_specs=[pl.BlockSpec((tr, cols), lambda i, s, pc: (i, 0))] * n
            + [pl.BlockSpec((None, tr, cols), lambda i, s, pc: (jnp.maximum(s - 1, 0), i, 0))] * n),
        out_shape=[SDS((half, cols), F32)] * n + [SDS((NCHIP - 1, half, cols), BF16)] * n,
        compiler_params=_cparams(),
    )(place, *gs, *sibs)
    return res[:n], res[n:]


def _chip_exchange_job(parts):
    def copies(ins, outs, send_sem, recv_sem):
        x, y, c, _ = _place()
        res = []
        for w in range(len(parts)):
            for s in range(1, NCHIP):
                cx, cy = _chip_at(x, y, s)
                k = w * (NCHIP - 1) + s - 1
                res.append(pltpu.make_async_remote_copy(
                    src_ref=ins[w].at[s - 1], dst_ref=outs[w].at[s - 1], send_sem=send_sem.at[k],
                    recv_sem=recv_sem.at[k], device_id=(cx, cy, c), device_id_type=MESH))
        return res

    return _exchange_job(parts, [SDS((NCHIP - 1,) + p.shape[1:], BF16) for p in parts],
                         len(parts) * (NCHIP - 1), copies)


def _chip_sums(name, owns, rems):
    n = len(owns)
    half, cols = owns[0].shape
    tr = _row_tile(half, cols)

    def body(*refs):
        for own_ref, rem_ref, out_ref in zip(refs[:n], refs[n:2 * n], refs[2 * n:]):
            out_ref[...] = (((own_ref[...] + rem_ref[0].astype(F32)) + rem_ref[1].astype(F32))
                            + rem_ref[2].astype(F32))

    return pl.pallas_call(
        body, name=name, grid=(half // tr,),
        in_specs=[pl.BlockSpec((tr, cols), lambda i: (i, 0))] * n
        + [pl.BlockSpec((NCHIP - 1, tr, cols), lambda i: (0, i, 0))] * n,
        out_specs=[pl.BlockSpec((tr, cols), lambda i: (i, 0))] * n,
        out_shape=[SDS((half, cols), F32)] * n,
        compiler_params=_cparams(),
    )(*owns, *rems)


def _share_halves_job(halves):
    def copies(ins, outs, send_sem, recv_sem):
        x, y, c, _ = _place()
        return [pltpu.make_async_remote_copy(
            src_ref=ins[w], dst_ref=outs[w], send_sem=send_sem.at[w], recv_sem=recv_sem.at[w],
            device_id=(x, y, 1 - c), device_id_type=MESH) for w in range(len(halves))]

    return _exchange_job(halves, [SDS(h.shape, F32) for h in halves], len(halves), copies)


def _adamw_math(w, g, m, v):
    m = B1 * m + (1.0 - B1) * g
    v = B2 * v + (1.0 - B2) * (g * g)
    m_hat = m / (1.0 - B1 ** STEP)
    v_hat = v / (1.0 - B2 ** STEP)
    delta = -LR * (m_hat / (jnp.sqrt(v_hat) + AEPS) + WD * w)
    return delta, m, v


def _adamws(name, place, ws, owns, sibs, ms, vs):
    n = len(ws)
    rows, cols = ws[0].shape
    by_cols = owns[0].shape[0] == rows
    half, pc_cols = (rows, cols // 2) if by_cols else (rows // 2, cols)
    tr = _row_tile(half, pc_cols)
    nt = half // tr

    def body(pc_ref, *refs):
        ins, outs = refs[:5 * n], refs[5 * n:]
        for k in range(n):
            w_ref, own_ref, sib_ref, m_ref, v_ref = ins[5 * k:5 * k + 5]
            g = jnp.where(pl.program_id(0) == pc_ref[1], own_ref[...], sib_ref[...])
            d, mn, vn = _adamw_math(w_ref[...], g, m_ref[...], v_ref[...])
            for ref, val in zip(outs[4 * k:4 * k + 4], (g, d, mn, vn)):
                ref[...] = val

    full = pl.BlockSpec((tr, pc_cols), (lambda h, i, pc: (i, h)) if by_cols else (lambda h, i, pc: (h * nt + i, 0)))
    part = pl.BlockSpec((tr, pc_cols), lambda h, i, pc: (i, 0))
    res = pl.pallas_call(
        body, name=name,
        grid_spec=pltpu.PrefetchScalarGridSpec(
            num_scalar_prefetch=1, grid=(2, nt),
            in_specs=[full, part, part, full, full] * n, out_specs=[full] * (4 * n)),
        out_shape=[SDS((rows, cols), F32)] * (4 * n),
        compiler_params=_cparams(),
    )(place, *[a for group in zip(ws, owns, sibs, ms, vs) for a in group])
    return [tuple(res[4 * k:4 * k + 4]) for k in range(n)]


ON_SPARSECORE = ("w_down", "w_gate_up", "w_branch_a")
SC_TILES = 32
SC_LANES = 16


def _adamw_sparsecore(name, w, own, sib, m, v):
    rows, cols = w.shape
    groups, half_groups = rows // 8, rows // 16
    rounds = -(-groups // SC_TILES)

    def body(w_hbm, own_hbm, sib_hbm, m_hbm, v_hbm, g_out, d_out, mo_out, vo_out,
             wb, gb, mb, vb, db):
        tile = lax.axis_index("sc_tile") * 2 + lax.axis_index("sc_core")
        c = lax.axis_index("c")
        for k in range(rounds):
            grp = tile + SC_TILES * k

            @pl.when(grp < groups)
            def _():
                rws = pl.ds(pl.multiple_of(grp * 8, 8), 8)
                in_half = pl.ds(pl.multiple_of((grp % half_groups) * 8, 8), 8)
                mine = (grp // half_groups) == c

                @pl.when(mine)
                def _():
                    pltpu.sync_copy(own_hbm.at[in_half], gb)

                @pl.when(jnp.logical_not(mine))
                def _():
                    pltpu.sync_copy(sib_hbm.at[in_half], gb)

                pltpu.sync_copy(w_hbm.at[rws], wb)
                pltpu.sync_copy(m_hbm.at[rws], mb)
                pltpu.sync_copy(v_hbm.at[rws], vb)

                @pl.loop(0, cols, step=SC_LANES)
                def _(j):
                    for r in range(8):
                        at = (r, pl.ds(j, SC_LANES))
                        d, mn, vn = _adamw_math(wb[at], gb[at], mb[at], vb[at])
                        db[at] = d
                        mb[at] = mn
                        vb[at] = vn

                pltpu.sync_copy(gb, g_out.at[rws])
                pltpu.sync_copy(db, d_out.at[rws])
                pltpu.sync_copy(mb, mo_out.at[rws])
                pltpu.sync_copy(vb, vo_out.at[rws])

    return pl.kernel(
        body, name=name, out_type=[SDS((rows, cols), F32)] * 4,
        mesh=plsc.VectorSubcoreMesh(core_axis_name="sc_core", subcore_axis_name="sc_tile"),
        scratch_types=[pltpu.VMEM((8, cols), F32)] * 5,
    )(w, own, sib, m, v)


def _small_allreduce_adamw(sp, wmv, job):
    shape = sp.shape
    ji, jo = len(job.ins), len(job.out_shapes)

    def body(sp_ref, wmv_ref, *rest):
        jin, (g_ref, d_ref, mo_ref, vo_ref), jout = rest[:ji], rest[ji:ji + 4], rest[ji + 4:ji + 4 + jo]
        sib_s, pair_s, chip_s, send_sem, recv_sem = rest[ji + 4 + jo:ji + 9 + jo]
        jsem = rest[ji + 9 + jo:]
        job.start(jin, jout, jsem)
        x, y, c, j = _place()
        cp = pltpu.make_async_remote_copy(
            src_ref=sp_ref, dst_ref=sib_s, send_sem=send_sem.at[0], recv_sem=recv_sem.at[0],
            device_id=(x, y, 1 - c), device_id_type=MESH)
        cp.start()
        cp.wait()
        pair_s[...] = sp_ref[...] + sib_s[...]
        half = shape[0] // 2
        mine = pl.ds(pl.multiple_of(c * half, 8), half)
        cps = []
        for s in range(1, NCHIP):
            cx, cy = _chip_at(x, y, s)
            cp = pltpu.make_async_remote_copy(
                src_ref=pair_s.at[mine], dst_ref=chip_s.at[s, mine], send_sem=send_sem.at[s],
                recv_sem=recv_sem.at[s], device_id=(cx, cy, c), device_id_type=MESH)
            cp.start()
            cps.append(cp)
        chip_s[0] = pair_s[...]
        for cp in cps:
            cp.wait()
        cps = []
        for s in range(1, NCHIP):
            cp = pltpu.make_async_remote_copy(
                src_ref=chip_s.at[s, mine], dst_ref=chip_s.at[s, mine], send_sem=send_sem.at[NCHIP + s],
                recv_sem=recv_sem.at[NCHIP + s], device_id=(x, y, 1 - c), device_id_type=MESH)
            cp.start()
            cps.append(cp)
        for cp in cps:
            cp.wait()
        tot = chip_s[j]
        for k in range(1, NCHIP):
            tot = tot + chip_s[k ^ j]
        g_ref[...] = tot
        d, mn, vn = _adamw_math(wmv_ref[0], tot, wmv_ref[1], wmv_ref[2])
        d_ref[...] = d
        mo_ref[...] = mn
        vo_ref[...] = vn
        job.mid(jin, jout, jsem)
        job.finish(jin, jout, jsem)

    vm = pl.BlockSpec(memory_space=pltpu.VMEM)
    res = pl.pallas_call(
        body, name="small_allreduce_adamw",
        in_specs=[vm] * 2 + [ANY] * ji, out_specs=[vm] * 4 + [ANY] * jo,
        out_shape=[SDS(shape, F32)] * 4 + job.out_shapes,
        scratch_shapes=[pltpu.VMEM(shape, F32), pltpu.VMEM(shape, F32), pltpu.VMEM((NCHIP,) + shape, F32),
                        pltpu.SemaphoreType.DMA((2 * NCHIP,)), pltpu.SemaphoreType.DMA((2 * NCHIP,))] + job.sems,
        input_output_aliases={2 + a: 4 + b for a, b in job.aliases.items()},
        compiler_params=pltpu.CompilerParams(has_side_effects=True),
    )(sp, wmv, *job.ins)
    return res[:4], res[4:]


def _pack_small(first, mix, ln_g, ln_b, b_s, lbt, hn, ffn, fin, w_s):
    rows = [first.reshape(1, D), mix.reshape(1, D), ln_g.reshape(1, D), ln_b.reshape(1, D),
            b_s.reshape(1, D), lbt.reshape(2, D), hn.reshape(1, D), ffn.reshape(1, D), fin.reshape(1, D),
            jnp.zeros((6, D), F32)]
    return jnp.concatenate(rows + [w_s.reshape(NG, GCH, GCH).transpose(1, 0, 2).reshape(GCH, D)], axis=0)


def _unpack_small(p):
    w_s = p[16:].reshape(GCH, NG, GCH).transpose(1, 0, 2).reshape(1, NG, GCH, GCH)
    return dict(norm_mix_g=p[1:2], gmlp_ln_g=p[2:3], gmlp_ln_b=p[3:4], gmlp_b_s=p[4].reshape(1, NG, GCH),
                hgrn_lb_table=p[5:7], hgrn_norm_g=p[7:8], norm_ffn_g=p[8:9], norm_final_g=p[9],
                gmlp_w_s=w_s)


SMALL = ("norm_mix_g", "gmlp_ln_g", "gmlp_ln_b", "gmlp_w_s", "gmlp_b_s", "hgrn_lb_table", "hgrn_norm_g",
         "norm_ffn_g", "norm_final_g")
BIG = ("w_in", "w_gate_up", "w_branch_a", "w_branch_b", "w_out", "w_down")
ORDER = ("norm_mix_g", "w_in", "gmlp_ln_g", "gmlp_ln_b", "gmlp_w_s", "gmlp_b_s", "hgrn_lb_table",
         "hgrn_norm_g", "w_branch_a", "w_branch_b", "w_out", "norm_ffn_g", "w_gate_up", "w_down",
         "norm_final_g")


def kernel(x, norm_mix_g, w_in, gmlp_ln_g, gmlp_ln_b, gmlp_w_s, gmlp_b_s, hgrn_lb_table, hgrn_norm_g, w_branch_a, w_branch_b, w_out, norm_ffn_g, w_gate_up, w_down, norm_final_g, loss_target, m_norm_mix_g, m_w_in, m_gmlp_ln_g, m_gmlp_ln_b, m_gmlp_w_s, m_gmlp_b_s, m_hgrn_lb_table, m_hgrn_norm_g, m_w_branch_a, m_w_branch_b, m_w_out, m_norm_ffn_g, m_w_gate_up, m_w_down, m_norm_final_g, v_norm_mix_g, v_w_in, v_gmlp_ln_g, v_gmlp_ln_b, v_gmlp_w_s, v_gmlp_b_s, v_hgrn_lb_table, v_hgrn_norm_g, v_w_branch_a, v_w_branch_b, v_w_out, v_norm_ffn_g, v_w_gate_up, v_w_down, v_norm_final_g):
    args = dict(locals())
    T = x.shape[1]
    xs = x.reshape(T, D)
    target = loss_target.reshape(T, D)
    big = {n: args[n].reshape(args[n].shape[1:]) for n in BIG}
    big_m = {n: args["m_" + n].reshape(args[n].shape[1:]) for n in BIG}
    big_v = {n: args["v_" + n].reshape(args[n].shape[1:]) for n in BIG}

    x_i, y_i, c_i = lax.axis_index("x"), lax.axis_index("y"), lax.axis_index("c")
    place = jnp.stack([2 * x_i + y_i, c_i]).astype(jnp.int32)
    def by_shape(names):
        groups = []
        for n in names:
            if groups and big[groups[-1][0]].shape == big[n].shape:
                groups[-1].append(n)
            else:
                groups.append([n])
        return groups

    cast = {}
    for grp in by_shape(BIG):
        cast.update(zip(grp, _cast_shards("cast_" + grp[0], place, [big[n] for n in grp],
                                          paired=grp[0] == "w_gate_up")))
    tril = jnp.tril(jnp.ones((GCH, GCH), bool))
    wm = jnp.where(tril, gmlp_w_s[0], 0.0).astype(BF16)
    wm_t = jnp.swapaxes(wm, 1, 2)
    b_t = gmlp_b_s[0].T

    (proj, hb), w_in4, (w_a4, w_b4, w_out4, w_down4) = _proj_fwd(
        place, xs, norm_mix_g, cast["w_in"], [cast[n] for n in ("w_branch_a", "w_branch_b", "w_out", "w_down")])
    (ab,), _ = _gmlp_fwd(proj, gmlp_ln_g, gmlp_ln_b, wm, b_t)
    (o_raw, obb, st_before), (w_gu,) = _hgrn_fwd(
        proj, hgrn_lb_table, hgrn_norm_g, job=_gather_job([cast["w_gate_up"]]))
    w_a, w_b, w_o = (w.reshape(D, D) for w in (w_a4, w_b4, w_out4))
    (mgb, x1), _ = _merge_fwd(xs, ab, obb, proj, w_a, w_b, w_o)
    w_dn = w_down4.reshape(FF, D)
    act, dx2b, h2b, dgu, dx1, dx1b, acc_ffn = _ffn_fwd_bwd(
        x1, target, norm_ffn_g, norm_final_g.reshape(1, D), w_gu, w_dn)

    grads, owns, parts, halves, sibh = {}, {}, {}, {}, {}

    def pair_sums(names, sibs):
        sib_of = dict(zip(names, sibs))
        for grp in by_shape(names):
            o, p = _pair_sums("rs_pair_sum_" + grp[0], place, [grads[n] for n in grp], [sib_of[n] for n in grp])
            owns.update(zip(grp, o))
            parts.update(zip(grp, p))

    def chip_sums(names, got):
        rem_of = dict(zip(names, got))
        for grp in by_shape(names):
            h = _chip_sums("rs_chip_sum_" + grp[0], [owns[n] for n in grp], [rem_of[n] for n in grp])
            halves.update(zip(grp, h))

    ffn, mix = ("w_gate_up", "w_down"), ("w_branch_a", "w_branch_b", "w_out")
    grads["w_gate_up"], _ = _dw_gate_up(h2b, dgu)
    grads["w_down"], _ = _dw_down(act, dx2b)
    (dya, dyb, dproj), got = _merge_bwd(
        dx1b, ab, obb, proj, w_o, w_a, w_b, job=_pair_exchange_job([grads[n] for n in ffn]))
    pair_sums(ffn, got)
    grads["w_branch_a"], _ = _dw_square("dw_branch_a", ab, dya)
    grads["w_branch_b"], _ = _dw_square("dw_branch_b", obb, dyb)
    grads["w_out"], _ = _dw_square("dw_out", mgb, dx1b)
    (dproj, acc_hgrn), got = _hgrn_bwd(
        dproj, dyb, w_b, o_raw, proj, st_before, hgrn_lb_table, hgrn_norm_g,
        job=_join_jobs(_chip_exchange_job([parts[n] for n in ffn]), _pair_exchange_job([grads[n] for n in mix])))
    chip_sums(ffn, got[:2])
    pair_sums(mix, got[2:])
    dproj, acc_ln, dws, dmix = _gmlp_bwd(dproj, dya, w_a, proj, gmlp_ln_g, gmlp_ln_b, wm, wm_t, b_t)
    for_sibling, got = _dw_in_half(
        "dw_in_sibling_half", place, hb, dproj, False,
        job=_join_jobs(_share_halves_job([halves[n] for n in ffn]), _chip_exchange_job([parts[n] for n in mix])))
    sibh.update(zip(ffn, got[:2]))
    chip_sums(mix, got[2:])
    grads["w_in"], got = _dw_in_half(
        "dw_in_own_half", place, hb, dproj, True, job=_share_halves_job([for_sibling]))
    pair_sums(("w_in",), got)
    (grad_x, acc_mix), got = _proj_bwd(
        dproj, w_in4, xs, dx1, norm_mix_g,
        job=_join_jobs(_chip_exchange_job([parts["w_in"]]), _share_halves_job([halves[n] for n in mix])))
    chip_sums(("w_in",), got[:1])
    sibh.update(zip(mix, got[1:]))

    lbv = jax.nn.sigmoid(hgrn_lb_table[0] - hgrn_lb_table[1])
    d_t0 = jnp.sum(acc_hgrn[0], axis=0) * lbv * (1.0 - lbv)
    loss_row = jnp.zeros((D,), F32).at[0].set(jnp.sum(acc_ffn[0]))
    dws_m = jnp.where(tril[:, None, :], dws.reshape(GCH, NG, GCH), 0.0).transpose(1, 0, 2)
    db_s = jnp.sum(dmix.reshape(GCH, NG, GCH), axis=-1).T
    sp = _pack_small(loss_row, jnp.sum(acc_mix, 0), jnp.sum(acc_ln[0], 0), jnp.sum(acc_ln[1], 0), db_s,
                     jnp.stack([d_t0, -d_t0]), jnp.sum(acc_hgrn[1], 0), jnp.sum(acc_ffn[2], 0),
                     jnp.sum(acc_ffn[1], 0), dws_m)
    zero = jnp.zeros((D,), F32)

    def pack(prefix):
        a = lambda n: args[prefix + n]
        return _pack_small(zero, a("norm_mix_g"), a("gmlp_ln_g"), a("gmlp_ln_b"), a("gmlp_b_s"),
                           a("hgrn_lb_table"), a("hgrn_norm_g"), a("norm_ffn_g"), a("norm_final_g"),
                           a("gmlp_w_s"))

    packed, (sibh["w_in"],) = _small_allreduce_adamw(
        sp, jnp.stack([pack(""), pack("m_"), pack("v_")]), _share_halves_job([halves["w_in"]]))
    loss = packed[0][0, 0]
    small = [_unpack_small(p) for p in packed]
    out = {n: tuple(s[n] for s in small) for n in SMALL}
    for grp in by_shape(BIG):
        if grp[0] in ON_SPARSECORE:
            res = [_adamw_sparsecore("adamw_sc_" + n, big[n], halves[n], sibh[n], big_m[n], big_v[n]) for n in grp]
        else:
            res = _adamws("adamw_" + grp[0], place,
                          *[[d[n] for n in grp] for d in (big, halves, sibh, big_m, big_v)])
        for n, quad in zip(grp, res):
            out[n] = tuple(a.reshape(args[n].shape) for a in quad)
    return (loss, grad_x.reshape(x.shape), *[out[n][0] for n in ORDER], *[out[n][1] for n in ORDER],
            *[out[n][2] for n in ORDER], *[out[n][3] for n in ORDER])
```

```python
import functools
import math

import jax
import jax.numpy as jnp
from jax import lax
from jax.experimental import pallas as pl
from jax.experimental.pallas import tpu as pltpu
from jax.experimental.pallas import tpu_sc as plsc

F32 = jnp.float32
BF16 = jnp.bfloat16
SDS = jax.ShapeDtypeStruct
MESH = pl.DeviceIdType.MESH
ANY = pl.BlockSpec(memory_space=pl.ANY)

D = 1024
NIN = 8
NG = 8
GCH = 128
NH = 8
HD = 128
HCH = 64
HGRN_HB = 8
HGRN_TOKENS = 256
GMLP_FWD_TOKENS = 512
GMLP_BWD_TOKENS = 256
HW = HGRN_HB * HD
DW_TOKENS = 2048
DW_IN_TOKENS = 4096
ELEMENTWISE_BLOCK_BYTES = 2 * 1024 * 1024
PROJ_OUT_SLOTS = 4
FF = 2816
FFS = 1408
NCHIP = 4
EPS = 1e-6
QSCALE = HD ** -0.5
GELU_C0 = math.sqrt(2.0 / math.pi)
GELU_C1 = 0.044715
LR, B1, B2, AEPS, WD, STEP = 0.001, 0.9, 0.999, 1e-08, 0.01, 10
VMEM_LIMIT_V7X = 56 * 1024 * 1024
SP_ROWS = 144


def _cparams(**kw):
    return pltpu.CompilerParams(vmem_limit_bytes=VMEM_LIMIT_V7X, **kw)


def _mm(a, b):
    return jnp.dot(a, b, preferred_element_type=F32)


def _mm_nt(a, b):
    return lax.dot_general(a, b, (((1,), (1,)), ((), ())), preferred_element_type=F32)


def _mm_tn(a, b):
    return lax.dot_general(a, b, (((0,), (0,)), ((), ())), preferred_element_type=F32)


def _rows8(x):
    r, c = x.shape
    return jnp.sum(x.reshape(r // 8, 8, c), axis=0)


def _mean(x):
    return jnp.mean(x, axis=-1, keepdims=True)


def _sigmoid(x):
    return 1.0 / (1.0 + jnp.exp(-x))


def _gelu(x):
    t = jnp.tanh(GELU_C0 * (x + GELU_C1 * x * x * x))
    return 0.5 * x * (1.0 + t), t


def _gelu_grad(x, t):
    return 0.5 * (1.0 + t) + 0.5 * x * (1.0 - t * t) * (GELU_C0 * (1.0 + 3.0 * GELU_C1 * x * x))


def _component_of(group):
    return jnp.where(group < 6, (group + 4) % 6, group)


def _proj_fwd(place, x, g_mix, w_in4, later):
    T = x.shape[0]
    tm = min(1024, T)
    ni = T // tm
    n = len(later)

    def body(pc_ref, x_ref, g_ref, *rest):
        proj_ref, h_ref, w_all = rest[1 + n:4 + n]
        gathered = rest[4 + n:4 + 2 * n]
        hs, wbuf, wsem, obuf, osem = rest[4 + 2 * n:9 + 2 * n]
        w_sems, later_sems = rest[9 + 2 * n:15 + 2 * n], rest[15 + 2 * n:]
        jp, i = pl.program_id(0), pl.program_id(1)
        w_cols = [w_all.at[:, :, pl.ds(k * D, D)] for k in range(2)]

        def w_copy(blk):
            cols = pl.ds(pl.multiple_of((blk % 2) * D, 128), D)
            return pltpu.make_async_copy(w_all.at[pc_ref[0] ^ (blk // 2), :, cols], wbuf.at[blk % 2],
                                         wsem.at[blk % 2])

        @pl.when((jp == 0) & (i == 0))
        def _():
            _gather_start(w_cols, w_sems)
            w_copy(jp).start()

        @pl.when(i == 0)
        def _():
            w_copy(jp).wait()

        @pl.when(jp == 0)
        def _():
            xv = x_ref[...]
            r = lax.rsqrt(_mean(xv * xv) + EPS)
            hb = (xv * r * g_ref[...]).astype(BF16)
            hs[i] = hb
            h_ref[...] = hb

        step = jp * ni + i
        slot = step % PROJ_OUT_SLOTS

        def o_copy(slot_):
            comp = 2 * (pc_ref[0] ^ (jp // 2)) + jp % 2
            return pltpu.make_async_copy(
                obuf.at[slot_], proj_ref.at[comp, pl.ds(pl.multiple_of(i * tm, 8), tm)], osem.at[slot_])

        @pl.when(step >= PROJ_OUT_SLOTS)
        def _():
            o_copy(slot).wait()

        obuf[slot] = _mm(hs[i], wbuf[jp % 2])
        o_copy(slot).start()

        @pl.when(step == NIN * ni - 1)
        def _():
            for k in range(PROJ_OUT_SLOTS):
                o_copy((slot + 1 + k) % PROJ_OUT_SLOTS).wait()

        for nxt in range(1, NIN):
            @pl.when((jp == nxt - 1) & (i == ni - 1))
            def _():
                if nxt >= 2:
                    _gather_land([w_cols[nxt % 2]], w_sems, nxt // 2, first=nxt % 2)
                if nxt == 5:
                    _gather_start(gathered, later_sems)
                if nxt == NIN - 1:
                    _gather_neighbours(gathered, later_sems)
                w_copy(jp + 1).start()

        @pl.when((jp == NIN - 1) & (i == ni - 1))
        def _():
            _gather_drain(w_cols, w_sems)
            _gather_finish(gathered, later_sems)

    tile = lambda jp, i, pc: (jnp.where(jp == 0, i, ni - 1), 0)
    res = pl.pallas_call(
        body, name="proj_fwd",
        grid_spec=pltpu.PrefetchScalarGridSpec(
            num_scalar_prefetch=1, grid=(NIN, ni),
            in_specs=[pl.BlockSpec((tm, D), tile), pl.BlockSpec((1, D), lambda jp, i, pc: (0, 0))] + [ANY] * (1 + n),
            out_specs=[ANY, pl.BlockSpec((tm, D), tile)] + [ANY] * (1 + n),
            scratch_shapes=[pltpu.VMEM((ni, tm, D), BF16), pltpu.VMEM((2, D, D), BF16),
                            pltpu.SemaphoreType.DMA((2,)), pltpu.VMEM((PROJ_OUT_SLOTS, tm, D), F32),
                            pltpu.SemaphoreType.DMA((PROJ_OUT_SLOTS,))] + _gather_sems(2) + _gather_sems(n)),
        out_shape=[SDS((NIN, T, D), F32), SDS((T, D), BF16), SDS(w_in4.shape, BF16)]
        + [SDS(a.shape, a.dtype) for a in later],
        input_output_aliases={3 + k: 2 + k for k in range(1 + n)},
        compiler_params=_cparams(has_side_effects=True),
    )(place, x, g_mix, w_in4, *later)
    return res[:2], res[2], res[3:]


def _chunks_abreast(x):
    return jnp.concatenate([x[GCH * ch:GCH * (ch + 1)] for ch in range(x.shape[0] // GCH)], axis=1)


def _chunks_stacked(x):
    return jnp.concatenate([x[:, GCH * ch:GCH * (ch + 1)] for ch in range(x.shape[1] // GCH)], axis=0)


def _layer_norm_stats(gv):
    mu = _mean(gv)
    xc = gv - mu
    rs = lax.rsqrt(_mean(xc * xc) + EPS)
    return xc * rs, rs


def _gmlp_fwd(proj, ln_g, ln_b, wm, b_t, job=None):
    T = proj.shape[1]
    tm = min(GMLP_FWD_TOKENS, T)

    def body(u_ref, v_ref, lg_ref, lb_ref, wm_ref, bt_ref, a_ref, a_s):
        gu, _ = _gelu(u_ref[...])
        gv, _ = _gelu(v_ref[...])
        vhat, _ = _layer_norm_stats(gv)
        vnb = (vhat * lg_ref[...] + lb_ref[...]).astype(BF16)
        for g in range(NG):
            cols = slice(128 * g, 128 * (g + 1))
            mixed = _mm(wm_ref[g], _chunks_abreast(vnb[:, cols])) + bt_ref[:, g:g + 1]
            a_s[:, cols] = gu[:, cols] * _chunks_stacked(mixed)
        a_ref[...] = a_s[...].astype(BF16)

    row = lambda i: (0, 0)
    return _call(
        body, name="gmlp_fwd", grid=(T // tm,), job=job, args=(proj, proj, ln_g, ln_b, wm, b_t),
        in_specs=[pl.BlockSpec((None, tm, D), lambda i: (0, i, 0)), pl.BlockSpec((None, tm, D), lambda i: (1, i, 0)),
                  pl.BlockSpec((1, D), row), pl.BlockSpec((1, D), row),
                  pl.BlockSpec((NG, GCH, GCH), lambda i: (0, 0, 0)), pl.BlockSpec((GCH, NG), row)],
        out_specs=[pl.BlockSpec((tm, D), lambda i: (i, 0))],
        out_shape=[SDS((T, D), BF16)],
        scratch_shapes=[pltpu.VMEM((tm, D), F32)])


def _cumsum64(x, row):
    for s in (1, 2, 4, 8, 16, 32):
        x = x + jnp.where(row >= s, pltpu.roll(x, s, 0), 0.0)
    return x


def _revcumsum64(x, row):
    n = x.shape[0]
    for s in (1, 2, 4, 8, 16, 32):
        x = x + jnp.where(row < HCH - s, pltpu.roll(x, n - s, 0), 0.0)
    return x


def _head_mean(x):
    parts = [jnp.broadcast_to(_mean(x[:, HD * h:HD * (h + 1)]), (x.shape[0], HD)) for h in range(x.shape[1] // HD)]
    return jnp.concatenate(parts, axis=1)


def _seg_sum(x):
    n, c = x.shape
    s = jnp.sum(x.reshape(n // HCH, HCH, c), axis=1, keepdims=True)
    return jnp.broadcast_to(s, (n // HCH, HCH, c)).reshape(n, c)


def _seg_row(x, idx):
    n, c = x.shape
    x3 = x.reshape(n // HCH, HCH, c)
    return jnp.broadcast_to(x3[:, idx:idx + 1, :], x3.shape).reshape(n, c)


def _hgrn_gates(fl, lbv, row):
    s = _sigmoid(fl)
    f = lbv + (1.0 - lbv) * s
    a = _cumsum64(jnp.log(f), row)
    return s, f, a, _seg_row(a, HCH // 2 - 1), _seg_row(a, HCH - 1)


def _hgrn_fwd(proj, lb_table, norm_g, job=None):
    T = proj.shape[1]
    tb = min(HGRN_TOKENS, T)
    nc = tb // HCH

    def body(q_ref, fl_ref, v_ref, g_ref, lbt_ref, gn_ref, o_ref, ob_ref, stb_ref, st_s, o_s):
        @pl.when(pl.program_id(1) == 0)
        def _():
            st_s[...] = jnp.zeros_like(st_s)

        row = lax.broadcasted_iota(jnp.int32, (tb, HW), 0) & (HCH - 1)
        lbv = _sigmoid(lbt_ref[0:1, :] - lbt_ref[1:2, :])
        _, f, a, a_mid, a_last = _hgrn_gates(fl_ref[...], lbv, row)
        k = 1.0 - f
        qs = q_ref[...] * QSCALE
        q_in = (qs * jnp.exp(a - a_mid)).astype(BF16)
        k_in = (k * jnp.exp(a_mid - a)).astype(BF16)
        q_a = (qs * jnp.exp(a)).astype(BF16)
        k_d = (k * jnp.exp(a_last - a)).astype(BF16)
        dec = jnp.exp(a_last)
        vb = v_ref[...].astype(BF16)
        tri = (lax.broadcasted_iota(jnp.int32, (HCH, HCH), 0)
               >= lax.broadcasted_iota(jnp.int32, (HCH, HCH), 1))
        for c in range(nc):
            sl = slice(HCH * c, HCH * (c + 1))
            for hh in range(HGRN_HB):
                hs = slice(HD * hh, HD * (hh + 1))
                st = st_s[hh]
                stb_ref[hh, c] = st
                sc = jnp.where(tri, _mm_nt(q_in[sl, hs], k_in[sl, hs]), 0.0)
                o_s[sl, hs] = _mm(sc.astype(BF16), vb[sl, hs]) + _mm_nt(q_a[sl, hs], st.astype(BF16))
                d64 = dec[sl, hs]
                st_s[hh] = st * jnp.concatenate([d64, d64], axis=0) + _mm_tn(vb[sl, hs], k_d[sl, hs])
        o = o_s[...]
        r = lax.rsqrt(_head_mean(o * o) + EPS)
        g = g_ref[...]
        o_ref[...] = o
        ob_ref[...] = (o * r * gn_ref[...] * (g * _sigmoid(g))).astype(BF16)

    def col(off):
        return pl.BlockSpec((None, tb, HW), lambda h, cb: (off, cb, h))

    return _call(
        body, name="hgrn_fwd", grid=(NH // HGRN_HB, T // tb), job=job,
        args=(proj, proj, proj, proj, lb_table, norm_g),
        in_specs=[col(2), col(3), col(4), col(5),
                  pl.BlockSpec((2, HW), lambda h, cb: (0, h)), pl.BlockSpec((1, HW), lambda h, cb: (0, h))],
        out_specs=[pl.BlockSpec((tb, HW), lambda h, cb: (cb, h)), pl.BlockSpec((tb, HW), lambda h, cb: (cb, h)),
                   pl.BlockSpec((HGRN_HB, nc, HD, HD), lambda h, cb: (h, cb, 0, 0))],
        out_shape=[SDS((T, D), F32), SDS((T, D), BF16), SDS((NH, T // HCH, HD, HD), F32)],
        scratch_shapes=[pltpu.VMEM((HGRN_HB, HD, HD), F32), pltpu.VMEM((tb, HW), F32)])


def _merge_fwd(x, ab, ob, proj, w_a, w_b, w_out, job=None):
    T = x.shape[0]
    tm = min(512, T)

    def body(x_ref, ab_ref, ob_ref, ga_ref, gb_ref, wa_ref, wb_ref, wo_ref, mg_ref, x1_ref):
        ya = _mm(ab_ref[...], wa_ref[...])
        yb = _mm(ob_ref[...], wb_ref[...])
        merged = (_sigmoid(ga_ref[...]) * ya + _sigmoid(gb_ref[...]) * yb).astype(BF16)
        mg_ref[...] = merged
        x1_ref[...] = x_ref[...] + _mm(merged, wo_ref[...])

    t = lambda i: (i, 0)
    w = lambda i: (0, 0)
    return _call(
        body, name="merge_fwd", grid=(T // tm,), job=job, args=(x, ab, ob, proj, proj, w_a, w_b, w_out),
        in_specs=[pl.BlockSpec((tm, D), t), pl.BlockSpec((tm, D), t), pl.BlockSpec((tm, D), t),
                  pl.BlockSpec((None, tm, D), lambda i: (6, i, 0)), pl.BlockSpec((None, tm, D), lambda i: (7, i, 0)),
                  pl.BlockSpec((D, D), w), pl.BlockSpec((D, D), w), pl.BlockSpec((D, D), w)],
        out_specs=[pl.BlockSpec((tm, D), t)] * 2,
        out_shape=[SDS((T, D), BF16), SDS((T, D), F32)])


def _ffn_fwd_bwd(x1, target, g_ffn, g_fin, w_gu, w_down):
    T = x1.shape[0]
    tm = min(256, T)
    inv_d = 1.0 / D

    def body(x1_ref, tg_ref, gf_ref, gn_ref, wgu_ref, wd_ref,
             act_ref, dx2b_ref, h2b_ref, dgu_ref, dx1_ref, dx1b_ref, acc_ref):
        @pl.when(pl.program_id(0) == 0)
        def _():
            acc_ref[...] = jnp.zeros_like(acc_ref)

        x1v = x1_ref[...]
        gf = gf_ref[...]
        gn = gn_ref[...]
        rr1 = lax.rsqrt(_mean(x1v * x1v) + EPS)
        x1n = x1v * rr1
        h2b = (x1n * gf).astype(BF16)
        h2b_ref[...] = h2b
        gate = _mm(h2b, wgu_ref[0])
        up = _mm(h2b, wgu_ref[1])
        sg = _sigmoid(gate)
        si = gate * sg
        act = (si * up).astype(BF16)
        act_ref[...] = act
        x2 = x1v + _mm(act, wd_ref[...])
        rr2 = lax.rsqrt(_mean(x2 * x2) + EPS)
        x2n = x2 * rr2
        e = x2n * gn - tg_ref[...]
        acc_ref[0] += _rows8(e * e) * (0.5 * inv_d)
        dy = e * inv_d
        acc_ref[1] += _rows8(dy * x2n)
        dxn = dy * gn
        dx2 = rr2 * (dxn - x2n * _mean(dxn * x2n))
        dx2b = dx2.astype(BF16)
        dx2b_ref[...] = dx2b
        dact = _mm_nt(dx2b, wd_ref[...])
        dgate = (dact * up * (sg * (1.0 + gate * (1.0 - sg)))).astype(BF16)
        dup = (dact * si).astype(BF16)
        dgu_ref[0] = dgate
        dgu_ref[1] = dup
        dh2 = _mm_nt(dgate, wgu_ref[0]) + _mm_nt(dup, wgu_ref[1])
        acc_ref[2] += _rows8(dh2 * x1n)
        dxn1 = dh2 * gf
        dx1 = dx2 + rr1 * (dxn1 - x1n * _mean(dxn1 * x1n))
        dx1_ref[...] = dx1
        dx1b_ref[...] = dx1.astype(BF16)

    t = lambda i: (i, 0)
    w = lambda i: (0, 0)
    one = pl.Buffered(1)
    return pl.pallas_call(
        body, name="ffn_fwd_bwd", grid=(T // tm,),
        in_specs=[pl.BlockSpec((tm, D), t), pl.BlockSpec((tm, D), t),
                  pl.BlockSpec((1, D), w), pl.BlockSpec((1, D), w),
                  pl.BlockSpec((2, D, FF), lambda i: (0, 0, 0), pipeline_mode=one),
                  pl.BlockSpec((FF, D), w, pipeline_mode=one)],
        out_specs=[pl.BlockSpec((tm, FF), t), pl.BlockSpec((tm, D), t), pl.BlockSpec((tm, D), t),
                   pl.BlockSpec((2, tm, FF), lambda i: (0, i, 0)),
                   pl.BlockSpec((tm, D), t), pl.BlockSpec((tm, D), t),
                   pl.BlockSpec((3, 8, D), lambda i: (0, 0, 0))],
        out_shape=[SDS((T, FF), BF16), SDS((T, D), BF16), SDS((T, D), BF16),
                   SDS((2, T, FF), BF16), SDS((T, D), F32), SDS((T, D), BF16),
                   SDS((3, 8, D), F32)],
        compiler_params=_cparams(),
    )(x1, target, g_ffn, g_fin, w_gu, w_down)


def _merge_bwd(dx1b, ab, ob, proj, w_out, w_a, w_b, job=None):
    T = dx1b.shape[0]
    tm = min(512, T)

    def body(dx_ref, ab_ref, ob_ref, ga_ref, gb_ref, wo_ref, wa_ref, wb_ref, dya_ref, dyb_ref, dp_ref):
        dm = _mm_nt(dx_ref[...], wo_ref[...])
        sa = _sigmoid(ga_ref[...])
        sb = _sigmoid(gb_ref[...])
        dya_ref[...] = (dm * sa).astype(BF16)
        dyb_ref[...] = (dm * sb).astype(BF16)
        dp_ref[0] = (dm * _mm(ab_ref[...], wa_ref[...]) * sa * (1.0 - sa)).astype(BF16)
        dp_ref[1] = (dm * _mm(ob_ref[...], wb_ref[...]) * sb * (1.0 - sb)).astype(BF16)

    t = lambda i: (i, 0)
    w = lambda i: (0, 0)
    return _call(
        body, name="merge_bwd", grid=(T // tm,),
        in_specs=[pl.BlockSpec((tm, D), t), pl.BlockSpec((tm, D), t), pl.BlockSpec((tm, D), t),
                  pl.BlockSpec((None, tm, D), lambda i: (6, i, 0)), pl.BlockSpec((None, tm, D), lambda i: (7, i, 0)),
                  pl.BlockSpec((D, D), w), pl.BlockSpec((D, D), w), pl.BlockSpec((D, D), w)],
        out_specs=[pl.BlockSpec((tm, D), t)] * 2 + [pl.BlockSpec((2, tm, D), lambda i: (3, i, 0))],
        out_shape=[SDS((T, D), BF16), SDS((T, D), BF16), SDS((NIN, T, D), BF16)],
        args=(dx1b, ab, ob, proj, proj, w_out, w_a, w_b), job=job)


def _hgrn_bwd(dproj, dyb, w_b, o_raw, proj, st_before, lb_table, norm_g, job=None):
    T = dyb.shape[0]
    tb = min(HGRN_TOKENS, T)
    nc = tb // HCH
    nb = T // tb

    def body(dp_in, dyb_ref, wb_ref, o_ref, q_ref, fl_ref, v_ref, g_ref, stb_ref, lbt_ref, gn_ref,
             dp_ref, acc_ref, dst_s, dqin_s, dqa_s, dkin_s, dkd_s, dv_s, ddec_s):
        del dp_in

        @pl.when(pl.program_id(1) == 0)
        def _():
            dst_s[...] = jnp.zeros_like(dst_s)
            acc_ref[...] = jnp.zeros_like(acc_ref)

        row = lax.broadcasted_iota(jnp.int32, (tb, HW), 0) & (HCH - 1)
        gn = gn_ref[...]
        lbv = _sigmoid(lbt_ref[0:1, :] - lbt_ref[1:2, :])
        o = o_ref[...]
        r = lax.rsqrt(_head_mean(o * o) + EPS)
        on = o * r
        g = g_ref[...]
        sgm = _sigmoid(g)
        dob_v = _mm_nt(dyb_ref[...], wb_ref[...])
        dp_ref[3] = (dob_v * on * gn * (sgm * (1.0 + g * (1.0 - sgm)))).astype(BF16)
        do_n = dob_v * (g * sgm)
        acc_ref[1] += _rows8(do_n * on)
        dxn = do_n * gn
        do = (r * (dxn - on * _head_mean(dxn * on))).astype(BF16)
        s, f, a, a_mid, a_last = _hgrn_gates(fl_ref[...], lbv, row)
        k = 1.0 - f
        qs = q_ref[...] * QSCALE
        e_q = jnp.exp(a - a_mid)
        e_k = jnp.exp(a_mid - a)
        e_a = jnp.exp(a)
        e_l = jnp.exp(a_last - a)
        dec = jnp.exp(a_last)
        q_in = qs * e_q
        k_in = k * e_k
        q_a = qs * e_a
        k_d = k * e_l
        q_inb, k_inb, q_ab, k_db = (z.astype(BF16) for z in (q_in, k_in, q_a, k_d))
        vb = v_ref[...].astype(BF16)
        tri = (lax.broadcasted_iota(jnp.int32, (HCH, HCH), 0)
               >= lax.broadcasted_iota(jnp.int32, (HCH, HCH), 1))
        for c in reversed(range(nc)):
            sl = slice(HCH * c, HCH * (c + 1))
            for hh in range(HGRN_HB):
                hs = slice(HD * hh, HD * (hh + 1))
                stp = stb_ref[hh, c]
                dst = dst_s[hh]
                dstb = dst.astype(BF16)
                do_c = do[sl, hs]
                v_c = vb[sl, hs]
                dqa_s[sl, hs] = _mm(do_c, stp.astype(BF16))
                dkd_s[sl, hs] = _mm(v_c, dstb)
                ddec_s[sl, hs] = jnp.broadcast_to(jnp.sum(dst * stp, axis=0, keepdims=True), (HCH, HD))
                sc = jnp.where(tri, _mm_nt(q_inb[sl, hs], k_inb[sl, hs]), 0.0).astype(BF16)
                dsc = jnp.where(tri, _mm_nt(do_c, v_c), 0.0).astype(BF16)
                dv_s[sl, hs] = _mm_nt(k_db[sl, hs], dstb) + _mm_tn(sc, do_c)
                dqin_s[sl, hs] = _mm(dsc, k_inb[sl, hs])
                dkin_s[sl, hs] = _mm_tn(dsc, q_inb[sl, hs])
                d64 = dec[sl, hs]
                dst_s[hh] = dst * jnp.concatenate([d64, d64], axis=0) + _mm_tn(do_c, q_ab[sl, hs])
        dq_in = dqin_s[...]
        dq_a = dqa_s[...]
        dk_in = dkin_s[...]
        dk_d = dkd_s[...]
        dp_ref[0] = ((dq_in * e_q + dq_a * e_a) * QSCALE).astype(BF16)
        dp_ref[2] = dv_s[...].astype(BF16)
        tq = dq_in * q_in
        tk = dk_in * k_in
        td = dk_d * k_d
        d_a = tq + dq_a * q_a - tk - td
        d_a = d_a + jnp.where(row == HCH // 2 - 1, _seg_sum(tk - tq), 0.0)
        d_a = d_a + jnp.where(row == HCH - 1, _seg_sum(td) + ddec_s[...] * dec, 0.0)
        dlf = _revcumsum64(d_a, row)
        df = dlf / f - (dk_in * e_k + dk_d * e_l)
        dp_ref[1] = (df * (1.0 - lbv) * s * (1.0 - s)).astype(BF16)
        acc_ref[0] += _rows8(df * (1.0 - s))

    def col(off):
        return pl.BlockSpec((None, tb, HW), lambda h, cb: (off, nb - 1 - cb, h))

    hb = lambda h, cb: (nb - 1 - cb, h)
    return _call(
        body, name="hgrn_bwd", grid=(NH // HGRN_HB, nb), job=job,
        args=(dproj, dyb, w_b, o_raw, proj, proj, proj, proj, st_before, lb_table, norm_g),
        in_specs=[ANY, pl.BlockSpec((tb, D), lambda h, cb: (nb - 1 - cb, 0)),
                  pl.BlockSpec((HW, D), lambda h, cb: (h, 0)), pl.BlockSpec((tb, HW), hb),
                  col(2), col(3), col(4), col(5),
                  pl.BlockSpec((HGRN_HB, nc, HD, HD), lambda h, cb: (h, nb - 1 - cb, 0, 0)),
                  pl.BlockSpec((2, HW), lambda h, cb: (0, h)), pl.BlockSpec((1, HW), lambda h, cb: (0, h))],
        out_specs=[pl.BlockSpec((4, tb, HW), lambda h, cb: (0, nb - 1 - cb, h)),
                   pl.BlockSpec((2, 8, HW), lambda h, cb: (0, 0, h))],
        out_shape=[SDS(dproj.shape, BF16), SDS((2, 8, D), F32)],
        scratch_shapes=[pltpu.VMEM((HGRN_HB, HD, HD), F32)] + [pltpu.VMEM((tb, HW), F32)] * 6,
        aliases={0: 0})


def _gmlp_bwd(dproj, dya, w_a, proj, ln_g, ln_b, wm, wm_t, b_t):
    T = dya.shape[0]
    tm = min(GMLP_BWD_TOKENS, T)

    def body(dp_in, dya_ref, wa_ref, u_ref, v_ref, lg_ref, lb_ref, wm_ref, wmt_ref, bt_ref,
             dp_ref, acc_ref, dws_ref, dmix_ref, du_s, dvn_s):
        del dp_in

        @pl.when(pl.program_id(0) == 0)
        def _():
            acc_ref[...] = jnp.zeros_like(acc_ref)
            dws_ref[...] = jnp.zeros_like(dws_ref)
            dmix_ref[...] = jnp.zeros_like(dmix_ref)

        u = u_ref[...]
        v = v_ref[...]
        lg = lg_ref[...]
        gu, t_u = _gelu(u)
        gv, t_v = _gelu(v)
        vhat, rs = _layer_norm_stats(gv)
        vnb = (vhat * lg + lb_ref[...]).astype(BF16)
        da_v = _mm_nt(dya_ref[...], wa_ref[...])
        for g in range(NG):
            cols = slice(128 * g, 128 * (g + 1))
            vng = _chunks_abreast(vnb[:, cols])
            mixed = _mm(wm_ref[g], vng) + bt_ref[:, g:g + 1]
            dag = _chunks_abreast(da_v[:, cols])
            dmx = dag * _chunks_abreast(gu[:, cols])
            du_s[:, cols] = _chunks_stacked(dag * mixed)
            dmxb = dmx.astype(BF16)
            dws_ref[:, cols] += _mm_nt(dmxb, vng)
            dmix_ref[:, cols] += sum(dmx[:, GCH * ch:GCH * (ch + 1)] for ch in range(tm // GCH))
            dvn_s[:, cols] = _chunks_stacked(_mm(wmt_ref[g], dmxb))
        dp_ref[0] = (du_s[...] * _gelu_grad(u, t_u)).astype(BF16)
        dvn = dvn_s[...]
        acc_ref[0] += _rows8(dvn * vhat)
        acc_ref[1] += _rows8(dvn)
        dvh = dvn * lg
        dgv = rs * (dvh - _mean(dvh) - vhat * _mean(dvh * vhat))
        dp_ref[1] = (dgv * _gelu_grad(v, t_v)).astype(BF16)

    row = lambda i: (0, 0)
    w3 = lambda i: (0, 0, 0)
    return pl.pallas_call(
        body, name="gmlp_bwd", grid=(T // tm,),
        in_specs=[ANY, pl.BlockSpec((tm, D), lambda i: (i, 0)), pl.BlockSpec((D, D), row),
                  pl.BlockSpec((None, tm, D), lambda i: (0, i, 0)), pl.BlockSpec((None, tm, D), lambda i: (1, i, 0)),
                  pl.BlockSpec((1, D), row), pl.BlockSpec((1, D), row),
                  pl.BlockSpec((NG, GCH, GCH), w3), pl.BlockSpec((NG, GCH, GCH), w3),
                  pl.BlockSpec((GCH, NG), row)],
        out_specs=[pl.BlockSpec((2, tm, D), lambda i: (2, i, 0)),
                   pl.BlockSpec((2, 8, D), w3), pl.BlockSpec((GCH, D), row), pl.BlockSpec((GCH, D), row)],
        out_shape=[SDS(dproj.shape, BF16), SDS((2, 8, D), F32), SDS((GCH, D), F32), SDS((GCH, D), F32)],
        scratch_shapes=[pltpu.VMEM((tm, D), F32), pltpu.VMEM((tm, D), F32)],
        input_output_aliases={0: 0},
        compiler_params=_cparams(),
    )(dproj, dya, w_a, proj, proj, ln_g, ln_b, wm, wm_t, b_t)


def _proj_bwd(dproj, w_in4, x, dx1, g_mix, job=None):
    T = x.shape[0]
    tm = min(256, T)
    order = (2, 3, 4, 5, 0, 1, 6, 7)

    def body(dp_ref, w_ref, x_ref, dx1_ref, g_ref, gx_ref, acc_ref):
        @pl.when(pl.program_id(0) == 0)
        def _():
            acc_ref[...] = jnp.zeros_like(acc_ref)

        dh = None
        for m, og in enumerate(order):
            part = _mm_nt(dp_ref[m], w_ref[og // 2, :, D * (og % 2):D * (og % 2 + 1)])
            dh = part if dh is None else dh + part
        xv = x_ref[...]
        r = lax.rsqrt(_mean(xv * xv) + EPS)
        xn = xv * r
        acc_ref[...] += _rows8(dh * xn)
        dxn = dh * g_ref[...]
        gx_ref[...] = dx1_ref[...] + r * (dxn - xn * _mean(dxn * xn))

    t = lambda i: (i, 0)
    return _call(
        body, name="proj_bwd", grid=(T // tm,),
        in_specs=[pl.BlockSpec((NIN, tm, D), lambda i: (0, i, 0)),
                  pl.BlockSpec((NCHIP, D, 2 * D), lambda i: (0, 0, 0), pipeline_mode=pl.Buffered(1)),
                  pl.BlockSpec((tm, D), t), pl.BlockSpec((tm, D), t), pl.BlockSpec((1, D), lambda i: (0, 0))],
        out_specs=[pl.BlockSpec((tm, D), t), pl.BlockSpec((8, D), lambda i: (0, 0))],
        out_shape=[SDS((T, D), F32), SDS((8, D), F32)],
        args=(dproj, w_in4, x, dx1, g_mix), job=job)


def _dw_call(name, a, b, a_spec, b_spec, o_spec, out_shape, nblk, tt, job=None, prefetch=None):
    T = a.shape[-2]

    def body(*refs):
        a_ref, b_ref, o_ref = refs[-3:]

        @pl.when(pl.program_id(1) == 0)
        def _():
            o_ref[...] = jnp.zeros_like(o_ref)
        o_ref[...] += _mm_tn(a_ref[...], b_ref[...])

    (out,), job_out = _call(
        body, name=name, grid=(nblk, T // tt), in_specs=[a_spec, b_spec], out_specs=[o_spec],
        out_shape=[out_shape], args=(a, b), job=job, prefetch=prefetch)
    return out, job_out


def _dw_in_half(name, place, hb, dproj, mine, job=None):
    tt = min(DW_IN_TOKENS, hb.shape[0])

    def comp(k, pc):
        return _component_of(2 * k + (pc[1] if mine else 1 - pc[1]))

    return _dw_call(
        name, hb, dproj,
        pl.BlockSpec((tt, D), lambda k, t, pc: (t, 0)),
        pl.BlockSpec((None, tt, D), lambda k, t, pc: (comp(k, pc), t, 0)),
        pl.BlockSpec((None, D, D), lambda k, t, pc: (k, 0, 0)),
        SDS((NCHIP, D, D), F32), NCHIP, tt, job, place)


def _dw_gate_up(h2b, dgu, job=None):
    tt = min(DW_TOKENS, h2b.shape[0])
    return _dw_call(
        "dw_gate_up", h2b, dgu,
        pl.BlockSpec((tt, D), lambda k, t: (t, 0)),
        pl.BlockSpec((None, tt, FFS), lambda k, t: (k // 2, t, k % 2)),
        pl.BlockSpec((None, D, FFS), lambda k, t: (k, 0, 0)),
        SDS((NCHIP, D, FFS), F32), NCHIP, tt, job)


def _dw_down(act, dx2b, job=None):
    tt = min(DW_TOKENS, act.shape[0])
    g, job_out = _dw_call(
        "dw_down", act, dx2b,
        pl.BlockSpec((tt, FFS), lambda k, t: (t, k)),
        pl.BlockSpec((tt, D), lambda k, t: (t, 0)),
        pl.BlockSpec((FFS, D), lambda k, t: (k, 0)),
        SDS((FF, D), F32), 2, tt, job)
    return g.reshape(NCHIP, FF // NCHIP, D), job_out


def _dw_square(name, a, b, job=None):
    tt = min(DW_TOKENS, a.shape[0])
    g, job_out = _dw_call(
        name, a, b,
        pl.BlockSpec((tt, D), lambda k, t: (t, 0)), pl.BlockSpec((tt, D), lambda k, t: (t, 0)),
        pl.BlockSpec((D, D), lambda k, t: (0, 0)), SDS((D, D), F32), 1, tt, job)
    return g.reshape(NCHIP, D // NCHIP, D), job_out


def _place():
    x, y, c = lax.axis_index("x"), lax.axis_index("y"), lax.axis_index("c")
    return x, y, c, 2 * x + y


def _chip_at(x, y, s):
    return x ^ (s >> 1), y ^ (s & 1)


class _Job:
    def __init__(self, ins, out_shapes, sems, start, finish, aliases=None, mid=None):
        self.ins, self.out_shapes, self.sems = list(ins), list(out_shapes), list(sems)
        self.start, self.finish, self.aliases = start, finish, dict(aliases or {})
        self.mid = mid if mid is not None else (lambda ins, outs, sems: None)


def _join_jobs(*jobs):
    def cut(refs, sizes):
        out, at = [], 0
        for n in sizes:
            out.append(refs[at:at + n])
            at += n
        return out

    ni = [len(j.ins) for j in jobs]
    no = [len(j.out_shapes) for j in jobs]
    ns = [len(j.sems) for j in jobs]

    def run(which):
        def go(ins, outs, sems):
            for j, a, b, c in zip(jobs, cut(ins, ni), cut(outs, no), cut(sems, ns)):
                getattr(j, which)(a, b, c)
        return go

    aliases = {}
    for k, j in enumerate(jobs):
        for a, b in j.aliases.items():
            aliases[sum(ni[:k]) + a] = sum(no[:k]) + b
    return _Job([a for j in jobs for a in j.ins], [o for j in jobs for o in j.out_shapes],
                [s for j in jobs for s in j.sems], run("start"), run("finish"), aliases, run("mid"))


def _call(body, *, name, grid, in_specs, out_specs, out_shape, args, scratch_shapes=(), aliases=None,
          job=None, prefetch=None):
    n_in, n_out, n_scr = len(in_specs), len(out_specs), len(scratch_shapes)
    npf = 0 if prefetch is None else 1
    job = job if job is not None else _Job([], [], [], lambda *a: None, lambda *a: None)
    ji, jo = len(job.ins), len(job.out_shapes)
    steps = math.prod(grid)

    def wrapped(*refs):
        pf, refs = refs[:npf], refs[npf:]
        ins, jin = refs[:n_in], refs[n_in:n_in + ji]
        o0 = n_in + ji
        outs, jout = refs[o0:o0 + n_out], refs[o0 + n_out:o0 + n_out + jo]
        s0 = o0 + n_out + jo
        scr, jsem = refs[s0:s0 + n_scr], refs[s0 + n_scr:]
        step = functools.reduce(lambda acc, ag: acc * ag[1] + pl.program_id(ag[0]), enumerate(grid), 0)
        if ji or jo:
            @pl.when(step == 0)
            def _():
                job.start(jin, jout, jsem)

        body(*pf, *ins, *outs, *scr)

        if ji or jo:
            @pl.when(step == steps // 2)
            def _():
                job.mid(jin, jout, jsem)

            @pl.when(step == steps - 1)
            def _():
                job.finish(jin, jout, jsem)

    io = {npf + a: b for a, b in dict(aliases or {}).items()}
    io.update({npf + n_in + a: n_out + b for a, b in job.aliases.items()})
    kw = dict(in_specs=list(in_specs) + [ANY] * ji, out_specs=list(out_specs) + [ANY] * jo,
              scratch_shapes=list(scratch_shapes) + job.sems)
    if npf:
        kw = dict(grid_spec=pltpu.PrefetchScalarGridSpec(num_scalar_prefetch=1, grid=grid, **kw))
    else:
        kw["grid"] = grid
    res = pl.pallas_call(
        wrapped, name=name, out_shape=list(out_shape) + job.out_shapes, input_output_aliases=io,
        compiler_params=_cparams(has_side_effects=bool(ji or jo)), **kw,
    )(*(() if prefetch is None else (prefetch,)), *args, *job.ins)
    return list(res[:n_out]), list(res[n_out:])


def _cast_shards(name, place, ws, paired=False):
    n = len(ws)
    rows, cols = ws[0].shape
    tr = 352 if rows % 352 == 0 else 256
    shape = (2, rows, 2 * cols) if paired else (NCHIP, rows, cols)
    mine = (lambda i, pc: (pc[0] // 2, i, pc[0] % 2)) if paired else (lambda i, pc: (pc[0], i, 0))

    def body(pc_ref, *refs):
        del pc_ref
        for w_ref, o_ref in zip(refs[:n], refs[n:]):
            o_ref[...] = w_ref[...].astype(BF16)

    return pl.pallas_call(
        body, name=name,
        grid_spec=pltpu.PrefetchScalarGridSpec(
            num_scalar_prefetch=1, grid=(rows // tr,),
            in_specs=[pl.BlockSpec((tr, cols), lambda i, pc: (i, 0))] * n,
            out_specs=[pl.BlockSpec((None, tr, cols), mine)] * n),
        out_shape=[SDS(shape, BF16)] * n,
        compiler_params=_cparams(),
    )(place, *ws)


def _sibling_copy(ref, send_sem, recv_sem):
    x, y, c, _ = _place()
    return pltpu.make_async_remote_copy(src_ref=ref, dst_ref=ref, send_sem=send_sem, recv_sem=recv_sem,
                                        device_id=(x, y, 1 - c), device_id_type=MESH)


def _slot(arr, chip):
    if arr.shape[0] == NCHIP:
        return arr.at[chip]
    cols = arr.shape[2] // 2
    return arr.at[chip // 2, :, pl.ds(pl.multiple_of((chip % 2) * cols, 128), cols)]


def _half_rows(arr, slot, core):
    half = arr.shape[1] // 2
    return _slot(arr, slot).at[pl.ds(pl.multiple_of(core * half, 16), half)]


def _quarter_rows(arr, slot, core, q):
    quarter = arr.shape[1] // 4
    return _slot(arr, slot).at[pl.ds(pl.multiple_of((2 * core + q) * quarter, 16), quarter)]


def _chip_copy(ref, dist, send_sem, recv_sem):
    x, y, c, _ = _place()
    cx, cy = _chip_at(x, y, dist)
    return pltpu.make_async_remote_copy(src_ref=ref, dst_ref=ref, send_sem=send_sem, recv_sem=recv_sem,
                                        device_id=(cx, cy, c), device_id_type=MESH)


def _gather_sems(n):
    dma = pltpu.SemaphoreType.DMA
    return [dma((n, 2))] * 4 + [dma((n, 4))] * 2


def _gather_start(arrs, sems):
    dsend, drecv = sems[0], sems[1]
    _, _, c, j = _place()
    for w, arr in enumerate(arrs):
        for dist in (1, 2):
            _chip_copy(_half_rows(arr, j, c), dist, dsend.at[w, dist - 1], drecv.at[w, dist - 1]).start()


def _gather_land(arrs, sems, dist, first=0):
    dsend, drecv, rsend, rrecv, fsend, frecv = sems
    _, _, c, j = _place()
    if dist < 3:
        other = 3 - dist
        for w, arr in enumerate(arrs, first):
            landed = _half_rows(arr, j ^ dist, c)
            _chip_copy(landed, dist, dsend.at[w, dist - 1], drecv.at[w, dist - 1]).wait_recv()
            relay = _quarter_rows(arr, j ^ dist, c, other - 1)
            _chip_copy(relay, other, rsend.at[w, other - 1], rrecv.at[w, other - 1]).start()
            _sibling_copy(landed, fsend.at[w, dist - 1], frecv.at[w, dist - 1]).start()
        for w, arr in enumerate(arrs, first):
            theirs = _half_rows(arr, j ^ dist, 1 - c)
            _sibling_copy(theirs, fsend.at[w, dist - 1], frecv.at[w, dist - 1]).wait_recv()
    else:
        for w, arr in enumerate(arrs, first):
            for via in (1, 2):
                piece = _quarter_rows(arr, j ^ 3, c, via - 1)
                _chip_copy(piece, via, rsend.at[w, via - 1], rrecv.at[w, via - 1]).wait_recv()
                _sibling_copy(piece, fsend.at[w, 1 + via], frecv.at[w, 1 + via]).start()
        for w, arr in enumerate(arrs, first):
            for via in (1, 2):
                theirs = _quarter_rows(arr, j ^ 3, 1 - c, via - 1)
                _sibling_copy(theirs, fsend.at[w, 1 + via], frecv.at[w, 1 + via]).wait_recv()


def _gather_drain(arrs, sems):
    dsend, drecv, rsend, rrecv, fsend, frecv = sems
    _, _, c, j = _place()
    for w, arr in enumerate(arrs):
        for dist in (1, 2):
            other = 3 - dist
            _chip_copy(_half_rows(arr, j, c), dist, dsend.at[w, dist - 1], drecv.at[w, dist - 1]).wait_send()
            _chip_copy(_quarter_rows(arr, j ^ dist, c, other - 1), other,
                       rsend.at[w, other - 1], rrecv.at[w, other - 1]).wait_send()
            _sibling_copy(_half_rows(arr, j ^ dist, c), fsend.at[w, dist - 1], frecv.at[w, dist - 1]).wait_send()
            _sibling_copy(_quarter_rows(arr, j ^ 3, c, dist - 1),
                          fsend.at[w, 1 + dist], frecv.at[w, 1 + dist]).wait_send()


def _gather_neighbours(arrs, sems):
    _gather_land(arrs, sems, 1)
    _gather_land(arrs, sems, 2)


def _gather_finish(arrs, sems):
    _gather_land(arrs, sems, 3)
    _gather_drain(arrs, sems)


def _gather_job(arrs):
    n = len(arrs)
    return _Job(arrs, [SDS(a.shape, a.dtype) for a in arrs], _gather_sems(n),
                lambda ins, outs, sems: _gather_start(outs, sems),
                lambda ins, outs, sems: _gather_finish(outs, sems), {k: k for k in range(n)},
                mid=lambda ins, outs, sems: _gather_neighbours(outs, sems))


def _exchange_job(arrs, out_shapes, n, copies):
    def start(ins, outs, sems):
        for cp in copies(ins, outs, sems[0], sems[1]):
            cp.start()

    def finish(ins, outs, sems):
        for cp in copies(ins, outs, sems[0], sems[1]):
            cp.wait()

    return _Job(arrs, out_shapes, [pltpu.SemaphoreType.DMA((n,))] * 2, start, finish)


def _pair_exchange_job(grads):
    def copies(ins, outs, send_sem, recv_sem):
        x, y, c, _ = _place()
        res = []
        for w in range(len(grads)):
            half = ins[w].shape[1] // 2
            theirs = pl.ds(pl.multiple_of((1 - c) * half, 8), half)
            res.append(pltpu.make_async_remote_copy(
                src_ref=ins[w].at[:, theirs, :], dst_ref=outs[w], send_sem=send_sem.at[w],
                recv_sem=recv_sem.at[w], device_id=(x, y, 1 - c), device_id_type=MESH))
        return res

    return _exchange_job(grads, [SDS((NCHIP, g.shape[1] // 2, g.shape[2]), F32) for g in grads],
                         len(grads), copies)


def _row_tile(rows, cols):
    tr = rows
    while tr * cols * 4 > ELEMENTWISE_BLOCK_BYTES and tr % 32 == 0:
        tr //= 2
    return tr


def _pair_sums(name, place, gs, sibs):
    n = len(gs)
    half, cols = sibs[0].shape[1], sibs[0].shape[2]
    tr = _row_tile(half, cols)
    nt = half // tr
    mine = nt if gs[0].shape[1] == 2 * half else 0

    def body(pc_ref, *refs):
        del pc_ref
        for g_ref, s_ref, own_ref, out_ref in zip(refs[:n], refs[n:2 * n], refs[2 * n:3 * n], refs[3 * n:]):
            v = g_ref[...] + s_ref[...]

            @pl.when(pl.program_id(1) == 0)
            def _():
                own_ref[...] = v

            @pl.when(pl.program_id(1) > 0)
            def _():
                out_ref[...] = v.astype(BF16)

    res = pl.pallas_call(
        body, name=name,
        grid_spec=pltpu.PrefetchScalarGridSpec(
            num_scalar_prefetch=1, grid=(nt, NCHIP),
            in_specs=[pl.BlockSpec((None, tr, cols), lambda i, s, pc: (pc[0] ^ s, pc[1] * mine + i, 0))] * n
            + [pl.BlockSpec((None, tr, cols), lambda i, s, pc: (pc[0] ^ s, i, 0))] * n,
            out_specs=[pl.BlockSpec((tr, cols), lambda i, s, pc: (i, 0))] * n
            + [pl.BlockSpec((None, tr, cols), lambda i, s, pc: (jnp.maximum(s - 1, 0), i, 0))] * n),
        out_shape=[SDS((half, cols), F32)] * n + [SDS((NCHIP - 1, half, cols), BF16)] * n,
        compiler_params=_cparams(),
    )(place, *gs, *sibs)
    return res[:n], res[n:]


def _chip_exchange_job(parts):
    def copies(ins, outs, send_sem, recv_sem):
        x, y, c, _ = _place()
        res = []
        for w in range(len(parts)):
            for s in range(1, NCHIP):
                cx, cy = _chip_at(x, y, s)
                k = w * (NCHIP - 1) + s - 1
                res.append(pltpu.make_async_remote_copy(
                    src_ref=ins[w].at[s - 1], dst_ref=outs[w].at[s - 1], send_sem=send_sem.at[k],
                    recv_sem=recv_sem.at[k], device_id=(cx, cy, c), device_id_type=MESH))
        return res

    return _exchange_job(parts, [SDS((NCHIP - 1,) + p.shape[1:], BF16) for p in parts],
                         len(parts) * (NCHIP - 1), copies)


def _chip_sums(name, owns, rems):
    n = len(owns)
    half, cols = owns[0].shape
    tr = _row_tile(half, cols)

    def body(*refs):
        for own_ref, rem_ref, out_ref in zip(refs[:n], refs[n:2 * n], refs[2 * n:]):
            out_ref[...] = (((own_ref[...] + rem_ref[0].astype(F32)) + rem_ref[1].astype(F32))
                            + rem_ref[2].astype(F32))

    return pl.pallas_call(
        body, name=name, grid=(half // tr,),
        in_specs=[pl.BlockSpec((tr, cols), lambda i: (i, 0))] * n
        + [pl.BlockSpec((NCHIP - 1, tr, cols), lambda i: (0, i, 0))] * n,
        out_specs=[pl.BlockSpec((tr, cols), lambda i: (i, 0))] * n,
        out_shape=[SDS((half, cols), F32)] * n,
        compiler_params=_cparams(),
    )(*owns, *rems)


def _share_halves_job(halves):
    def copies(ins, outs, send_sem, recv_sem):
        x, y, c, _ = _place()
        return [pltpu.make_async_remote_copy(
            src_ref=ins[w], dst_ref=outs[w], send_sem=send_sem.at[w], recv_sem=recv_sem.at[w],
            device_id=(x, y, 1 - c), device_id_type=MESH) for w in range(len(halves))]

    return _exchange_job(halves, [SDS(h.shape, F32) for h in halves], len(halves), copies)


def _adamw_math(w, g, m, v):
    m = B1 * m + (1.0 - B1) * g
    v = B2 * v + (1.0 - B2) * (g * g)
    m_hat = m / (1.0 - B1 ** STEP)
    v_hat = v / (1.0 - B2 ** STEP)
    delta = -LR * (m_hat / (jnp.sqrt(v_hat) + AEPS) + WD * w)
    return delta, m, v


def _adamws(name, place, ws, owns, sibs, ms, vs):
    n = len(ws)
    rows, cols = ws[0].shape
    by_cols = owns[0].shape[0] == rows
    half, pc_cols = (rows, cols // 2) if by_cols else (rows // 2, cols)
    tr = _row_tile(half, pc_cols)
    nt = half // tr

    def body(pc_ref, *refs):
        ins, outs = refs[:5 * n], refs[5 * n:]
        for k in range(n):
            w_ref, own_ref, sib_ref, m_ref, v_ref = ins[5 * k:5 * k + 5]
            g = jnp.where(pl.program_id(0) == pc_ref[1], own_ref[...], sib_ref[...])
            d, mn, vn = _adamw_math(w_ref[...], g, m_ref[...], v_ref[...])
            for ref, val in zip(outs[4 * k:4 * k + 4], (g, d, mn, vn)):
                ref[...] = val

    full = pl.BlockSpec((tr, pc_cols), (lambda h, i, pc: (i, h)) if by_cols else (lambda h, i, pc: (h * nt + i, 0)))
    part = pl.BlockSpec((tr, pc_cols), lambda h, i, pc: (i, 0))
    res = pl.pallas_call(
        body, name=name,
        grid_spec=pltpu.PrefetchScalarGridSpec(
            num_scalar_prefetch=1, grid=(2, nt),
            in_specs=[full, part, part, full, full] * n, out_specs=[full] * (4 * n)),
        out_shape=[SDS((rows, cols), F32)] * (4 * n),
        compiler_params=_cparams(),
    )(place, *[a for group in zip(ws, owns, sibs, ms, vs) for a in group])
    return [tuple(res[4 * k:4 * k + 4]) for k in range(n)]


ON_SPARSECORE = ("w_down", "w_gate_up")
SC_TILES = 32
SC_LANES = 16


def _adamw_sparsecore(name, w, own, sib, m, v):
    rows, cols = w.shape
    groups, half_groups = rows // 8, rows // 16
    rounds = -(-groups // SC_TILES)

    def body(w_hbm, own_hbm, sib_hbm, m_hbm, v_hbm, g_out, d_out, mo_out, vo_out,
             wb, gb, mb, vb, db):
        tile = lax.axis_index("sc_tile") * 2 + lax.axis_index("sc_core")
        c = lax.axis_index("c")
        for k in range(rounds):
            grp = tile + SC_TILES * k

            @pl.when(grp < groups)
            def _():
                rws = pl.ds(pl.multiple_of(grp * 8, 8), 8)
                in_half = pl.ds(pl.multiple_of((grp % half_groups) * 8, 8), 8)
                mine = (grp // half_groups) == c

                @pl.when(mine)
                def _():
                    pltpu.sync_copy(own_hbm.at[in_half], gb)

                @pl.when(jnp.logical_not(mine))
                def _():
                    pltpu.sync_copy(sib_hbm.at[in_half], gb)

                pltpu.sync_copy(w_hbm.at[rws], wb)
                pltpu.sync_copy(m_hbm.at[rws], mb)
                pltpu.sync_copy(v_hbm.at[rws], vb)

                @pl.loop(0, cols, step=SC_LANES)
                def _(j):
                    for r in range(8):
                        at = (r, pl.ds(j, SC_LANES))
                        d, mn, vn = _adamw_math(wb[at], gb[at], mb[at], vb[at])
                        db[at] = d
                        mb[at] = mn
                        vb[at] = vn

                pltpu.sync_copy(gb, g_out.at[rws])
                pltpu.sync_copy(db, d_out.at[rws])
                pltpu.sync_copy(mb, mo_out.at[rws])
                pltpu.sync_copy(vb, vo_out.at[rws])

    return pl.kernel(
        body, name=name, out_type=[SDS((rows, cols), F32)] * 4,
        mesh=plsc.VectorSubcoreMesh(core_axis_name="sc_core", subcore_axis_name="sc_tile"),
        scratch_types=[pltpu.VMEM((8, cols), F32)] * 5,
    )(w, own, sib, m, v)


def _small_allreduce_adamw(sp, wmv, job):
    shape = sp.shape
    ji, jo = len(job.ins), len(job.out_shapes)

    def body(sp_ref, wmv_ref, *rest):
        jin, (g_ref, d_ref, mo_ref, vo_ref), jout = rest[:ji], rest[ji:ji + 4], rest[ji + 4:ji + 4 + jo]
        sib_s, pair_s, chip_s, send_sem, recv_sem = rest[ji + 4 + jo:ji + 9 + jo]
        jsem = rest[ji + 9 + jo:]
        job.start(jin, jout, jsem)
        x, y, c, j = _place()
        cp = pltpu.make_async_remote_copy(
            src_ref=sp_ref, dst_ref=sib_s, send_sem=send_sem.at[0], recv_sem=recv_sem.at[0],
            device_id=(x, y, 1 - c), device_id_type=MESH)
        cp.start()
        cp.wait()
        pair_s[...] = sp_ref[...] + sib_s[...]
        half = shape[0] // 2
        mine = pl.ds(pl.multiple_of(c * half, 8), half)
        cps = []
        for s in range(1, NCHIP):
            cx, cy = _chip_at(x, y, s)
            cp = pltpu.make_async_remote_copy(
                src_ref=pair_s.at[mine], dst_ref=chip_s.at[s, mine], send_sem=send_sem.at[s],
                recv_sem=recv_sem.at[s], device_id=(cx, cy, c), device_id_type=MESH)
            cp.start()
            cps.append(cp)
        chip_s[0] = pair_s[...]
        for cp in cps:
            cp.wait()
        cps = []
        for s in range(1, NCHIP):
            cp = pltpu.make_async_remote_copy(
                src_ref=chip_s.at[s, mine], dst_ref=chip_s.at[s, mine], send_sem=send_sem.at[NCHIP + s],
                recv_sem=recv_sem.at[NCHIP + s], device_id=(x, y, 1 - c), device_id_type=MESH)
            cp.start()
            cps.append(cp)
        for cp in cps:
            cp.wait()
        tot = chip_s[j]
        for k in range(1, NCHIP):
            tot = tot + chip_s[k ^ j]
        g_ref[...] = tot
        d, mn, vn = _adamw_math(wmv_ref[0], tot, wmv_ref[1], wmv_ref[2])
        d_ref[...] = d
        mo_ref[...] = mn
        vo_ref[...] = vn
        job.mid(jin, jout, jsem)
        job.finish(jin, jout, jsem)

    vm = pl.BlockSpec(memory_space=pltpu.VMEM)
    res = pl.pallas_call(
        body, name="small_allreduce_adamw",
        in_specs=[vm] * 2 + [ANY] * ji, out_specs=[vm] * 4 + [ANY] * jo,
        out_shape=[SDS(shape, F32)] * 4 + job.out_shapes,
        scratch_shapes=[pltpu.VMEM(shape, F32), pltpu.VMEM(shape, F32), pltpu.VMEM((NCHIP,) + shape, F32),
                        pltpu.SemaphoreType.DMA((2 * NCHIP,)), pltpu.SemaphoreType.DMA((2 * NCHIP,))] + job.sems,
        input_output_aliases={2 + a: 4 + b for a, b in job.aliases.items()},
        compiler_params=pltpu.CompilerParams(has_side_effects=True),
    )(sp, wmv, *job.ins)
    return res[:4], res[4:]


def _pack_small(first, mix, ln_g, ln_b, b_s, lbt, hn, ffn, fin, w_s):
    rows = [first.reshape(1, D), mix.reshape(1, D), ln_g.reshape(1, D), ln_b.reshape(1, D),
            b_s.reshape(1, D), lbt.reshape(2, D), hn.reshape(1, D), ffn.reshape(1, D), fin.reshape(1, D),
            jnp.zeros((6, D), F32)]
    return jnp.concatenate(rows + [w_s.reshape(NG, GCH, GCH).transpose(1, 0, 2).reshape(GCH, D)], axis=0)


def _unpack_small(p):
    w_s = p[16:].reshape(GCH, NG, GCH).transpose(1, 0, 2).reshape(1, NG, GCH, GCH)
    return dict(norm_mix_g=p[1:2], gmlp_ln_g=p[2:3], gmlp_ln_b=p[3:4], gmlp_b_s=p[4].reshape(1, NG, GCH),
                hgrn_lb_table=p[5:7], hgrn_norm_g=p[7:8], norm_ffn_g=p[8:9], norm_final_g=p[9],
                gmlp_w_s=w_s)


SMALL = ("norm_mix_g", "gmlp_ln_g", "gmlp_ln_b", "gmlp_w_s", "gmlp_b_s", "hgrn_lb_table", "hgrn_norm_g",
         "norm_ffn_g", "norm_final_g")
BIG = ("w_in", "w_gate_up", "w_branch_a", "w_branch_b", "w_out", "w_down")
ORDER = ("norm_mix_g", "w_in", "gmlp_ln_g", "gmlp_ln_b", "gmlp_w_s", "gmlp_b_s", "hgrn_lb_table",
         "hgrn_norm_g", "w_branch_a", "w_branch_b", "w_out", "norm_ffn_g", "w_gate_up", "w_down",
         "norm_final_g")


def kernel(x, norm_mix_g, w_in, gmlp_ln_g, gmlp_ln_b, gmlp_w_s, gmlp_b_s, hgrn_lb_table, hgrn_norm_g, w_branch_a, w_branch_b, w_out, norm_ffn_g, w_gate_up, w_down, norm_final_g, loss_target, m_norm_mix_g, m_w_in, m_gmlp_ln_g, m_gmlp_ln_b, m_gmlp_w_s, m_gmlp_b_s, m_hgrn_lb_table, m_hgrn_norm_g, m_w_branch_a, m_w_branch_b, m_w_out, m_norm_ffn_g, m_w_gate_up, m_w_down, m_norm_final_g, v_norm_mix_g, v_w_in, v_gmlp_ln_g, v_gmlp_ln_b, v_gmlp_w_s, v_gmlp_b_s, v_hgrn_lb_table, v_hgrn_norm_g, v_w_branch_a, v_w_branch_b, v_w_out, v_norm_ffn_g, v_w_gate_up, v_w_down, v_norm_final_g):
    args = dict(locals())
    T = x.shape[1]
    xs = x.reshape(T, D)
    target = loss_target.reshape(T, D)
    big = {n: args[n].reshape(args[n].shape[1:]) for n in BIG}
    big_m = {n: args["m_" + n].reshape(args[n].shape[1:]) for n in BIG}
    big_v = {n: args["v_" + n].reshape(args[n].shape[1:]) for n in BIG}

    x_i, y_i, c_i = lax.axis_index("x"), lax.axis_index("y"), lax.axis_index("c")
    place = jnp.stack([2 * x_i + y_i, c_i]).astype(jnp.int32)
    def by_shape(names):
        groups = []
        for n in names:
            if groups and big[groups[-1][0]].shape == big[n].shape:
                groups[-1].append(n)
            else:
                groups.append([n])
        return groups

    cast = {}
    for grp in by_shape(BIG):
        cast.update(zip(grp, _cast_shards("cast_" + grp[0], place, [big[n] for n in grp],
                                          paired=grp[0] == "w_gate_up")))
    tril = jnp.tril(jnp.ones((GCH, GCH), bool))
    wm = jnp.where(tril, gmlp_w_s[0], 0.0).astype(BF16)
    wm_t = jnp.swapaxes(wm, 1, 2)
    b_t = gmlp_b_s[0].T

    (proj, hb), w_in4, (w_a4, w_b4, w_out4, w_down4) = _proj_fwd(
        place, xs, norm_mix_g, cast["w_in"], [cast[n] for n in ("w_branch_a", "w_branch_b", "w_out", "w_down")])
    (ab,), _ = _gmlp_fwd(proj, gmlp_ln_g, gmlp_ln_b, wm, b_t)
    (o_raw, obb, st_before), (w_gu,) = _hgrn_fwd(
        proj, hgrn_lb_table, hgrn_norm_g, job=_gather_job([cast["w_gate_up"]]))
    w_a, w_b, w_o = (w.reshape(D, D) for w in (w_a4, w_b4, w_out4))
    (mgb, x1), _ = _merge_fwd(xs, ab, obb, proj, w_a, w_b, w_o)
    w_dn = w_down4.reshape(FF, D)
    act, dx2b, h2b, dgu, dx1, dx1b, acc_ffn = _ffn_fwd_bwd(
        x1, target, norm_ffn_g, norm_final_g.reshape(1, D), w_gu, w_dn)

    grads, owns, parts, halves, sibh = {}, {}, {}, {}, {}

    def pair_sums(names, sibs):
        sib_of = dict(zip(names, sibs))
        for grp in by_shape(names):
            o, p = _pair_sums("rs_pair_sum_" + grp[0], place, [grads[n] for n in grp], [sib_of[n] for n in grp])
            owns.update(zip(grp, o))
            parts.update(zip(grp, p))

    def chip_sums(names, got):
        rem_of = dict(zip(names, got))
        for grp in by_shape(names):
            h = _chip_sums("rs_chip_sum_" + grp[0], [owns[n] for n in grp], [rem_of[n] for n in grp])
            halves.update(zip(grp, h))

    ffn, mix = ("w_gate_up", "w_down"), ("w_branch_a", "w_branch_b", "w_out")
    grads["w_gate_up"], _ = _dw_gate_up(h2b, dgu)
    grads["w_down"], _ = _dw_down(act, dx2b)
    (dya, dyb, dproj), got = _merge_bwd(
        dx1b, ab, obb, proj, w_o, w_a, w_b, job=_pair_exchange_job([grads[n] for n in ffn]))
    pair_sums(ffn, got)
    grads["w_branch_a"], _ = _dw_square("dw_branch_a", ab, dya)
    grads["w_branch_b"], _ = _dw_square("dw_branch_b", obb, dyb)
    grads["w_out"], _ = _dw_square("dw_out", mgb, dx1b)
    (dproj, acc_hgrn), got = _hgrn_bwd(
        dproj, dyb, w_b, o_raw, proj, st_before, hgrn_lb_table, hgrn_norm_g,
        job=_join_jobs(_chip_exchange_job([parts[n] for n in ffn]), _pair_exchange_job([grads[n] for n in mix])))
    chip_sums(ffn, got[:2])
    pair_sums(mix, got[2:])
    dproj, acc_ln, dws, dmix = _gmlp_bwd(dproj, dya, w_a, proj, gmlp_ln_g, gmlp_ln_b, wm, wm_t, b_t)
    for_sibling, got = _dw_in_half(
        "dw_in_sibling_half", place, hb, dproj, False,
        job=_join_jobs(_share_halves_job([halves[n] for n in ffn]), _chip_exchange_job([parts[n] for n in mix])))
    sibh.update(zip(ffn, got[:2]))
    chip_sums(mix, got[2:])
    grads["w_in"], got = _dw_in_half(
        "dw_in_own_half", place, hb, dproj, True, job=_share_halves_job([for_sibling]))
    pair_sums(("w_in",), got)
    (grad_x, acc_mix), got = _proj_bwd(
        dproj, w_in4, xs, dx1, norm_mix_g,
        job=_join_jobs(_chip_exchange_job([parts["w_in"]]), _share_halves_job([halves[n] for n in mix])))
    chip_sums(("w_in",), got[:1])
    sibh.update(zip(mix, got[1:]))

    lbv = jax.nn.sigmoid(hgrn_lb_table[0] - hgrn_lb_table[1])
    d_t0 = jnp.sum(acc_hgrn[0], axis=0) * lbv * (1.0 - lbv)
    loss_row = jnp.zeros((D,), F32).at[0].set(jnp.sum(acc_ffn[0]))
    dws_m = jnp.where(tril[:, None, :], dws.reshape(GCH, NG, GCH), 0.0).transpose(1, 0, 2)
    db_s = jnp.sum(dmix.reshape(GCH, NG, GCH), axis=-1).T
    sp = _pack_small(loss_row, jnp.sum(acc_mix, 0), jnp.sum(acc_ln[0], 0), jnp.sum(acc_ln[1], 0), db_s,
                     jnp.stack([d_t0, -d_t0]), jnp.sum(acc_hgrn[1], 0), jnp.sum(acc_ffn[2], 0),
                     jnp.sum(acc_ffn[1], 0), dws_m)
    zero = jnp.zeros((D,), F32)

    def pack(prefix):
        a = lambda n: args[prefix + n]
        return _pack_small(zero, a("norm_mix_g"), a("gmlp_ln_g"), a("gmlp_ln_b"), a("gmlp_b_s"),
                           a("hgrn_lb_table"), a("hgrn_norm_g"), a("norm_ffn_g"), a("norm_final_g"),
                           a("gmlp_w_s"))

    packed, (sibh["w_in"],) = _small_allreduce_adamw(
        sp, jnp.stack([pack(""), pack("m_"), pack("v_")]), _share_halves_job([halves["w_in"]]))
    loss = packed[0][0, 0]
    small = [_unpack_small(p) for p in packed]
    out = {n: tuple(s[n] for s in small) for n in SMALL}
    for grp in by_shape(BIG):
        if grp[0] in ON_SPARSECORE:
            res = [_adamw_sparsecore("adamw_sc_" + n, big[n], halves[n], sibh[n], big_m[n], big_v[n]) for n in grp]
        else:
            res = _adamws("adamw_" + grp[0], place,
                          *[[d[n] for n in grp] for d in (big, halves, sibh, big_m, big_v)])
        for n, quad in zip(grp, res):
            out[n] = tuple(a.reshape(args[n].shape) for a in quad)
    return (loss, grad_x.reshape(x.shape), *[out[n][0] for n in ORDER], *[out[n][1] for n in ORDER],
            *[out[n][2] for n in ORDER], *[out[n][3] for n in ORDER])
```

```python
import functools
import math

import jax
import jax.numpy as jnp
from jax import lax
from jax.experimental import pallas as pl
from jax.experimental.pallas import tpu as pltpu
from jax.experimental.pallas import tpu_sc as plsc

F32 = jnp.float32
BF16 = jnp.bfloat16
SDS = jax.ShapeDtypeStruct
MESH = pl.DeviceIdType.MESH
ANY = pl.BlockSpec(memory_space=pl.ANY)

D = 1024
NIN = 8
NG = 8
GCH = 128
NH = 8
HD = 128
HCH = 64
HGRN_HB = 8
HGRN_TOKENS = 256
GMLP_FWD_TOKENS = 512
GMLP_BWD_TOKENS = 256
HW = HGRN_HB * HD
DW_TOKENS = 2048
DW_IN_TOKENS = 4096
ELEMENTWISE_BLOCK_BYTES = 2 * 1024 * 1024
PROJ_OUT_SLOTS = 4
FF = 2816
FFS = 1408
NCHIP = 4
EPS = 1e-6
QSCALE = HD ** -0.5
GELU_C0 = math.sqrt(2.0 / math.pi)
GELU_C1 = 0.044715
LR, B1, B2, AEPS, WD, STEP = 0.001, 0.9, 0.999, 1e-08, 0.01, 10
VMEM_LIMIT_V7X = 56 * 1024 * 1024
SP_ROWS = 144


def _cparams(**kw):
    return pltpu.CompilerParams(vmem_limit_bytes=VMEM_LIMIT_V7X, **kw)


def _mm(a, b):
    return jnp.dot(a, b, preferred_element_type=F32)


def _mm_nt(a, b):
    return lax.dot_general(a, b, (((1,), (1,)), ((), ())), preferred_element_type=F32)


def _mm_tn(a, b):
    return lax.dot_general(a, b, (((0,), (0,)), ((), ())), preferred_element_type=F32)


def _rows8(x):
    r, c = x.shape
    return jnp.sum(x.reshape(r // 8, 8, c), axis=0)


def _mean(x):
    return jnp.mean(x, axis=-1, keepdims=True)


def _sigmoid(x):
    return 1.0 / (1.0 + jnp.exp(-x))


def _gelu(x):
    t = jnp.tanh(GELU_C0 * (x + GELU_C1 * x * x * x))
    return 0.5 * x * (1.0 + t), t


def _gelu_grad(x, t):
    return 0.5 * (1.0 + t) + 0.5 * x * (1.0 - t * t) * (GELU_C0 * (1.0 + 3.0 * GELU_C1 * x * x))


def _component_of(group):
    return jnp.where(group < 6, (group + 4) % 6, group)


def _proj_fwd(place, x, g_mix, w_in4, later):
    T = x.shape[0]
    tm = min(1024, T)
    ni = T // tm
    n = len(later)

    def body(pc_ref, x_ref, g_ref, *rest):
        proj_ref, h_ref, w_all = rest[1 + n:4 + n]
        gathered = rest[4 + n:4 + 2 * n]
        hs, wbuf, wsem, obuf, osem = rest[4 + 2 * n:9 + 2 * n]
        w_sems, later_sems = rest[9 + 2 * n:15 + 2 * n], rest[15 + 2 * n:]
        jp, i = pl.program_id(0), pl.program_id(1)
        w_cols = [w_all.at[:, :, pl.ds(k * D, D)] for k in range(2)]

        def w_copy(blk):
            cols = pl.ds(pl.multiple_of((blk % 2) * D, 128), D)
            return pltpu.make_async_copy(w_all.at[pc_ref[0] ^ (blk // 2), :, cols], wbuf.at[blk % 2],
                                         wsem.at[blk % 2])

        @pl.when((jp == 0) & (i == 0))
        def _():
            _gather_start(w_cols, w_sems)
            w_copy(jp).start()

        @pl.when(i == 0)
        def _():
            w_copy(jp).wait()

        @pl.when(jp == 0)
        def _():
            xv = x_ref[...]
            r = lax.rsqrt(_mean(xv * xv) + EPS)
            hb = (xv * r * g_ref[...]).astype(BF16)
            hs[i] = hb
            h_ref[...] = hb

        step = jp * ni + i
        slot = step % PROJ_OUT_SLOTS

        def o_copy(slot_):
            comp = 2 * (pc_ref[0] ^ (jp // 2)) + jp % 2
            return pltpu.make_async_copy(
                obuf.at[slot_], proj_ref.at[comp, pl.ds(pl.multiple_of(i * tm, 8), tm)], osem.at[slot_])

        @pl.when(step >= PROJ_OUT_SLOTS)
        def _():
            o_copy(slot).wait()

        obuf[slot] = _mm(hs[i], wbuf[jp % 2])
        o_copy(slot).start()

        @pl.when(step == NIN * ni - 1)
        def _():
            for k in range(PROJ_OUT_SLOTS):
                o_copy((slot + 1 + k) % PROJ_OUT_SLOTS).wait()

        for nxt in range(1, NIN):
            @pl.when((jp == nxt - 1) & (i == ni - 1))
            def _():
                if nxt >= 2:
                    _gather_land([w_cols[nxt % 2]], w_sems, nxt // 2, first=nxt % 2)
                if nxt == 5:
                    _gather_start(gathered, later_sems)
                if nxt == NIN - 1:
                    _gather_neighbours(gathered, later_sems)
                w_copy(jp + 1).start()

        @pl.when((jp == NIN - 1) & (i == ni - 1))
        def _():
            _gather_drain(w_cols, w_sems)
            _gather_finish(gathered, later_sems)

    tile = lambda jp, i, pc: (jnp.where(jp == 0, i, ni - 1), 0)
    res = pl.pallas_call(
        body, name="proj_fwd",
        grid_spec=pltpu.PrefetchScalarGridSpec(
            num_scalar_prefetch=1, grid=(NIN, ni),
            in_specs=[pl.BlockSpec((tm, D), tile), pl.BlockSpec((1, D), lambda jp, i, pc: (0, 0))] + [ANY] * (1 + n),
            out_specs=[ANY, pl.BlockSpec((tm, D), tile)] + [ANY] * (1 + n),
            scratch_shapes=[pltpu.VMEM((ni, tm, D), BF16), pltpu.VMEM((2, D, D), BF16),
                            pltpu.SemaphoreType.DMA((2,)), pltpu.VMEM((PROJ_OUT_SLOTS, tm, D), F32),
                            pltpu.SemaphoreType.DMA((PROJ_OUT_SLOTS,))] + _gather_sems(2) + _gather_sems(n)),
        out_shape=[SDS((NIN, T, D), F32), SDS((T, D), BF16), SDS(w_in4.shape, BF16)]
        + [SDS(a.shape, a.dtype) for a in later],
        input_output_aliases={3 + k: 2 + k for k in range(1 + n)},
        compiler_params=_cparams(has_side_effects=True),
    )(place, x, g_mix, w_in4, *later)
    return res[:2], res[2], res[3:]


def _chunks_abreast(x):
    return jnp.concatenate([x[GCH * ch:GCH * (ch + 1)] for ch in range(x.shape[0] // GCH)], axis=1)


def _chunks_stacked(x):
    return jnp.concatenate([x[:, GCH * ch:GCH * (ch + 1)] for ch in range(x.shape[1] // GCH)], axis=0)


def _layer_norm_stats(gv):
    mu = _mean(gv)
    xc = gv - mu
    rs = lax.rsqrt(_mean(xc * xc) + EPS)
    return xc * rs, rs


def _gmlp_fwd(proj, ln_g, ln_b, wm, b_t, job=None):
    T = proj.shape[1]
    tm = min(GMLP_FWD_TOKENS, T)

    def body(u_ref, v_ref, lg_ref, lb_ref, wm_ref, bt_ref, a_ref, a_s):
        gu, _ = _gelu(u_ref[...])
        gv, _ = _gelu(v_ref[...])
        vhat, _ = _layer_norm_stats(gv)
        vnb = (vhat * lg_ref[...] + lb_ref[...]).astype(BF16)
        for g in range(NG):
            cols = slice(128 * g, 128 * (g + 1))
            mixed = _mm(wm_ref[g], _chunks_abreast(vnb[:, cols])) + bt_ref[:, g:g + 1]
            a_s[:, cols] = gu[:, cols] * _chunks_stacked(mixed)
        a_ref[...] = a_s[...].astype(BF16)

    row = lambda i: (0, 0)
    return _call(
        body, name="gmlp_fwd", grid=(T // tm,), job=job, args=(proj, proj, ln_g, ln_b, wm, b_t),
        in_specs=[pl.BlockSpec((None, tm, D), lambda i: (0, i, 0)), pl.BlockSpec((None, tm, D), lambda i: (1, i, 0)),
                  pl.BlockSpec((1, D), row), pl.BlockSpec((1, D), row),
                  pl.BlockSpec((NG, GCH, GCH), lambda i: (0, 0, 0)), pl.BlockSpec((GCH, NG), row)],
        out_specs=[pl.BlockSpec((tm, D), lambda i: (i, 0))],
        out_shape=[SDS((T, D), BF16)],
        scratch_shapes=[pltpu.VMEM((tm, D), F32)])


def _cumsum64(x, row):
    for s in (1, 2, 4, 8, 16, 32):
        x = x + jnp.where(row >= s, pltpu.roll(x, s, 0), 0.0)
    return x


def _revcumsum64(x, row):
    n = x.shape[0]
    for s in (1, 2, 4, 8, 16, 32):
        x = x + jnp.where(row < HCH - s, pltpu.roll(x, n - s, 0), 0.0)
    return x


def _head_mean(x):
    parts = [jnp.broadcast_to(_mean(x[:, HD * h:HD * (h + 1)]), (x.shape[0], HD)) for h in range(x.shape[1] // HD)]
    return jnp.concatenate(parts, axis=1)


def _seg_sum(x):
    n, c = x.shape
    s = jnp.sum(x.reshape(n // HCH, HCH, c), axis=1, keepdims=True)
    return jnp.broadcast_to(s, (n // HCH, HCH, c)).reshape(n, c)


def _seg_row(x, idx):
    n, c = x.shape
    x3 = x.reshape(n // HCH, HCH, c)
    return jnp.broadcast_to(x3[:, idx:idx + 1, :], x3.shape).reshape(n, c)


def _hgrn_gates(fl, lbv, row):
    s = _sigmoid(fl)
    f = lbv + (1.0 - lbv) * s
    a = _cumsum64(jnp.log(f), row)
    return s, f, a, _seg_row(a, HCH // 2 - 1), _seg_row(a, HCH - 1)


def _hgrn_fwd(proj, lb_table, norm_g, job=None):
    T = proj.shape[1]
    tb = min(HGRN_TOKENS, T)
    nc = tb // HCH

    def body(q_ref, fl_ref, v_ref, g_ref, lbt_ref, gn_ref, o_ref, ob_ref, stb_ref, st_s, o_s):
        @pl.when(pl.program_id(1) == 0)
        def _():
            st_s[...] = jnp.zeros_like(st_s)

        row = lax.broadcasted_iota(jnp.int32, (tb, HW), 0) & (HCH - 1)
        lbv = _sigmoid(lbt_ref[0:1, :] - lbt_ref[1:2, :])
        _, f, a, a_mid, a_last = _hgrn_gates(fl_ref[...], lbv, row)
        k = 1.0 - f
        qs = q_ref[...] * QSCALE
        q_in = (qs * jnp.exp(a - a_mid)).astype(BF16)
        k_in = (k * jnp.exp(a_mid - a)).astype(BF16)
        q_a = (qs * jnp.exp(a)).astype(BF16)
        k_d = (k * jnp.exp(a_last - a)).astype(BF16)
        dec = jnp.exp(a_last)
        vb = v_ref[...].astype(BF16)
        tri = (lax.broadcasted_iota(jnp.int32, (HCH, HCH), 0)
               >= lax.broadcasted_iota(jnp.int32, (HCH, HCH), 1))
        for c in range(nc):
            sl = slice(HCH * c, HCH * (c + 1))
            for hh in range(HGRN_HB):
                hs = slice(HD * hh, HD * (hh + 1))
                st = st_s[hh]
                stb_ref[hh, c] = st
                sc = jnp.where(tri, _mm_nt(q_in[sl, hs], k_in[sl, hs]), 0.0)
                o_s[sl, hs] = _mm(sc.astype(BF16), vb[sl, hs]) + _mm_nt(q_a[sl, hs], st.astype(BF16))
                d64 = dec[sl, hs]
                st_s[hh] = st * jnp.concatenate([d64, d64], axis=0) + _mm_tn(vb[sl, hs], k_d[sl, hs])
        o = o_s[...]
        r = lax.rsqrt(_head_mean(o * o) + EPS)
        g = g_ref[...]
        o_ref[...] = o
        ob_ref[...] = (o * r * gn_ref[...] * (g * _sigmoid(g))).astype(BF16)

    def col(off):
        return pl.BlockSpec((None, tb, HW), lambda h, cb: (off, cb, h))

    return _call(
        body, name="hgrn_fwd", grid=(NH // HGRN_HB, T // tb), job=job,
        args=(proj, proj, proj, proj, lb_table, norm_g),
        in_specs=[col(2), col(3), col(4), col(5),
                  pl.BlockSpec((2, HW), lambda h, cb: (0, h)), pl.BlockSpec((1, HW), lambda h, cb: (0, h))],
        out_specs=[pl.BlockSpec((tb, HW), lambda h, cb: (cb, h)), pl.BlockSpec((tb, HW), lambda h, cb: (cb, h)),
                   pl.BlockSpec((HGRN_HB, nc, HD, HD), lambda h, cb: (h, cb, 0, 0))],
        out_shape=[SDS((T, D), F32), SDS((T, D), BF16), SDS((NH, T // HCH, HD, HD), F32)],
        scratch_shapes=[pltpu.VMEM((HGRN_HB, HD, HD), F32), pltpu.VMEM((tb, HW), F32)])


def _merge_fwd(x, ab, ob, proj, w_a, w_b, w_out, job=None):
    T = x.shape[0]
    tm = min(512, T)

    def body(x_ref, ab_ref, ob_ref, ga_ref, gb_ref, wa_ref, wb_ref, wo_ref, mg_ref, x1_ref):
        ya = _mm(ab_ref[...], wa_ref[...])
        yb = _mm(ob_ref[...], wb_ref[...])
        merged = (_sigmoid(ga_ref[...]) * ya + _sigmoid(gb_ref[...]) * yb).astype(BF16)
        mg_ref[...] = merged
        x1_ref[...] = x_ref[...] + _mm(merged, wo_ref[...])

    t = lambda i: (i, 0)
    w = lambda i: (0, 0)
    return _call(
        body, name="merge_fwd", grid=(T // tm,), job=job, args=(x, ab, ob, proj, proj, w_a, w_b, w_out),
        in_specs=[pl.BlockSpec((tm, D), t), pl.BlockSpec((tm, D), t), pl.BlockSpec((tm, D), t),
                  pl.BlockSpec((None, tm, D), lambda i: (6, i, 0)), pl.BlockSpec((None, tm, D), lambda i: (7, i, 0)),
                  pl.BlockSpec((D, D), w), pl.BlockSpec((D, D), w), pl.BlockSpec((D, D), w)],
        out_specs=[pl.BlockSpec((tm, D), t)] * 2,
        out_shape=[SDS((T, D), BF16), SDS((T, D), F32)])


def _ffn_fwd_bwd(x1, target, g_ffn, g_fin, w_gu, w_down):
    T = x1.shape[0]
    tm = min(256, T)
    inv_d = 1.0 / D

    def body(x1_ref, tg_ref, gf_ref, gn_ref, wgu_ref, wd_ref,
             act_ref, dx2b_ref, h2b_ref, dgu_ref, dx1_ref, dx1b_ref, acc_ref):
        @pl.when(pl.program_id(0) == 0)
        def _():
            acc_ref[...] = jnp.zeros_like(acc_ref)

        x1v = x1_ref[...]
        gf = gf_ref[...]
        gn = gn_ref[...]
        rr1 = lax.rsqrt(_mean(x1v * x1v) + EPS)
        x1n = x1v * rr1
        h2b = (x1n * gf).astype(BF16)
        h2b_ref[...] = h2b
        gate = _mm(h2b, wgu_ref[0])
        up = _mm(h2b, wgu_ref[1])
        sg = _sigmoid(gate)
        si = gate * sg
        act = (si * up).astype(BF16)
        act_ref[...] = act
        x2 = x1v + _mm(act, wd_ref[...])
        rr2 = lax.rsqrt(_mean(x2 * x2) + EPS)
        x2n = x2 * rr2
        e = x2n * gn - tg_ref[...]
        acc_ref[0] += _rows8(e * e) * (0.5 * inv_d)
        dy = e * inv_d
        acc_ref[1] += _rows8(dy * x2n)
        dxn = dy * gn
        dx2 = rr2 * (dxn - x2n * _mean(dxn * x2n))
        dx2b = dx2.astype(BF16)
        dx2b_ref[...] = dx2b
        dact = _mm_nt(dx2b, wd_ref[...])
        dgate = (dact * up * (sg * (1.0 + gate * (1.0 - sg)))).astype(BF16)
        dup = (dact * si).astype(BF16)
        dgu_ref[0] = dgate
        dgu_ref[1] = dup
        dh2 = _mm_nt(dgate, wgu_ref[0]) + _mm_nt(dup, wgu_ref[1])
        acc_ref[2] += _rows8(dh2 * x1n)
        dxn1 = dh2 * gf
        dx1 = dx2 + rr1 * (dxn1 - x1n * _mean(dxn1 * x1n))
        dx1_ref[...] = dx1
        dx1b_ref[...] = dx1.astype(BF16)

    t = lambda i: (i, 0)
    w = lambda i: (0, 0)
    one = pl.Buffered(1)
    return pl.pallas_call(
        body, name="ffn_fwd_bwd", grid=(T // tm,),
        in_specs=[pl.BlockSpec((tm, D), t), pl.BlockSpec((tm, D), t),
                  pl.BlockSpec((1, D), w), pl.BlockSpec((1, D), w),
                  pl.BlockSpec((2, D, FF), lambda i: (0, 0, 0), pipeline_mode=one),
                  pl.BlockSpec((FF, D), w, pipeline_mode=one)],
        out_specs=[pl.BlockSpec((tm, FF), t), pl.BlockSpec((tm, D), t), pl.BlockSpec((tm, D), t),
                   pl.BlockSpec((2, tm, FF), lambda i: (0, i, 0)),
                   pl.BlockSpec((tm, D), t), pl.BlockSpec((tm, D), t),
                   pl.BlockSpec((3, 8, D), lambda i: (0, 0, 0))],
        out_shape=[SDS((T, FF), BF16), SDS((T, D), BF16), SDS((T, D), BF16),
                   SDS((2, T, FF), BF16), SDS((T, D), F32), SDS((T, D), BF16),
                   SDS((3, 8, D), F32)],
        compiler_params=_cparams(),
    )(x1, target, g_ffn, g_fin, w_gu, w_down)


def _merge_bwd(dx1b, ab, ob, proj, w_out, w_a, w_b, job=None):
    T = dx1b.shape[0]
    tm = min(512, T)

    def body(dx_ref, ab_ref, ob_ref, ga_ref, gb_ref, wo_ref, wa_ref, wb_ref, dya_ref, dyb_ref, dp_ref):
        dm = _mm_nt(dx_ref[...], wo_ref[...])
        sa = _sigmoid(ga_ref[...])
        sb = _sigmoid(gb_ref[...])
        dya_ref[...] = (dm * sa).astype(BF16)
        dyb_ref[...] = (dm * sb).astype(BF16)
        dp_ref[0] = (dm * _mm(ab_ref[...], wa_ref[...]) * sa * (1.0 - sa)).astype(BF16)
        dp_ref[1] = (dm * _mm(ob_ref[...], wb_ref[...]) * sb * (1.0 - sb)).astype(BF16)

    t = lambda i: (i, 0)
    w = lambda i: (0, 0)
    return _call(
        body, name="merge_bwd", grid=(T // tm,),
        in_specs=[pl.BlockSpec((tm, D), t), pl.BlockSpec((tm, D), t), pl.BlockSpec((tm, D), t),
                  pl.BlockSpec((None, tm, D), lambda i: (6, i, 0)), pl.BlockSpec((None, tm, D), lambda i: (7, i, 0)),
                  pl.BlockSpec((D, D), w), pl.BlockSpec((D, D), w), pl.BlockSpec((D, D), w)],
        out_specs=[pl.BlockSpec((tm, D), t)] * 2 + [pl.BlockSpec((2, tm, D), lambda i: (3, i, 0))],
        out_shape=[SDS((T, D), BF16), SDS((T, D), BF16), SDS((NIN, T, D), BF16)],
        args=(dx1b, ab, ob, proj, proj, w_out, w_a, w_b), job=job)


def _hgrn_bwd(dproj, dyb, w_b, o_raw, proj, st_before, lb_table, norm_g, job=None):
    T = dyb.shape[0]
    tb = min(HGRN_TOKENS, T)
    nc = tb // HCH
    nb = T // tb

    def body(dp_in, dyb_ref, wb_ref, o_ref, q_ref, fl_ref, v_ref, g_ref, stb_ref, lbt_ref, gn_ref,
             dp_ref, acc_ref, dst_s, dqin_s, dqa_s, dkin_s, dkd_s, dv_s, ddec_s):
        del dp_in

        @pl.when(pl.program_id(1) == 0)
        def _():
            dst_s[...] = jnp.zeros_like(dst_s)
            acc_ref[...] = jnp.zeros_like(acc_ref)

        row = lax.broadcasted_iota(jnp.int32, (tb, HW), 0) & (HCH - 1)
        gn = gn_ref[...]
        lbv = _sigmoid(lbt_ref[0:1, :] - lbt_ref[1:2, :])
        o = o_ref[...]
        r = lax.rsqrt(_head_mean(o * o) + EPS)
        on = o * r
        g = g_ref[...]
        sgm = _sigmoid(g)
        dob_v = _mm_nt(dyb_ref[...], wb_ref[...])
        dp_ref[3] = (dob_v * on * gn * (sgm * (1.0 + g * (1.0 - sgm)))).astype(BF16)
        do_n = dob_v * (g * sgm)
        acc_ref[1] += _rows8(do_n * on)
        dxn = do_n * gn
        do = (r * (dxn - on * _head_mean(dxn * on))).astype(BF16)
        s, f, a, a_mid, a_last = _hgrn_gates(fl_ref[...], lbv, row)
        k = 1.0 - f
        qs = q_ref[...] * QSCALE
        e_q = jnp.exp(a - a_mid)
        e_k = jnp.exp(a_mid - a)
        e_a = jnp.exp(a)
        e_l = jnp.exp(a_last - a)
        dec = jnp.exp(a_last)
        q_in = qs * e_q
        k_in = k * e_k
        q_a = qs * e_a
        k_d = k * e_l
        q_inb, k_inb, q_ab, k_db = (z.astype(BF16) for z in (q_in, k_in, q_a, k_d))
        vb = v_ref[...].astype(BF16)
        tri = (lax.broadcasted_iota(jnp.int32, (HCH, HCH), 0)
               >= lax.broadcasted_iota(jnp.int32, (HCH, HCH), 1))
        for c in reversed(range(nc)):
            sl = slice(HCH * c, HCH * (c + 1))
            for hh in range(HGRN_HB):
                hs = slice(HD * hh, HD * (hh + 1))
                stp = stb_ref[hh, c]
                dst = dst_s[hh]
                dstb = dst.astype(BF16)
                do_c = do[sl, hs]
                v_c = vb[sl, hs]
                dqa_s[sl, hs] = _mm(do_c, stp.astype(BF16))
                dkd_s[sl, hs] = _mm(v_c, dstb)
                ddec_s[sl, hs] = jnp.broadcast_to(jnp.sum(dst * stp, axis=0, keepdims=True), (HCH, HD))
                sc = jnp.where(tri, _mm_nt(q_inb[sl, hs], k_inb[sl, hs]), 0.0).astype(BF16)
                dsc = jnp.where(tri, _mm_nt(do_c, v_c), 0.0).astype(BF16)
                dv_s[sl, hs] = _mm_nt(k_db[sl, hs], dstb) + _mm_tn(sc, do_c)
                dqin_s[sl, hs] = _mm(dsc, k_inb[sl, hs])
                dkin_s[sl, hs] = _mm_tn(dsc, q_inb[sl, hs])
                d64 = dec[sl, hs]
                dst_s[hh] = dst * jnp.concatenate([d64, d64], axis=0) + _mm_tn(do_c, q_ab[sl, hs])
        dq_in = dqin_s[...]
        dq_a = dqa_s[...]
        dk_in = dkin_s[...]
        dk_d = dkd_s[...]
        dp_ref[0] = ((dq_in * e_q + dq_a * e_a) * QSCALE).astype(BF16)
        dp_ref[2] = dv_s[...].astype(BF16)
        tq = dq_in * q_in
        tk = dk_in * k_in
        td = dk_d * k_d
        d_a = tq + dq_a * q_a - tk - td
        d_a = d_a + jnp.where(row == HCH // 2 - 1, _seg_sum(tk - tq), 0.0)
        d_a = d_a + jnp.where(row == HCH - 1, _seg_sum(td) + ddec_s[...] * dec, 0.0)
        dlf = _revcumsum64(d_a, row)
        df = dlf / f - (dk_in * e_k + dk_d * e_l)
        dp_ref[1] = (df * (1.0 - lbv) * s * (1.0 - s)).astype(BF16)
        acc_ref[0] += _rows8(df * (1.0 - s))

    def col(off):
        return pl.BlockSpec((None, tb, HW), lambda h, cb: (off, nb - 1 - cb, h))

    hb = lambda h, cb: (nb - 1 - cb, h)
    return _call(
        body, name="hgrn_bwd", grid=(NH // HGRN_HB, nb), job=job,
        args=(dproj, dyb, w_b, o_raw, proj, proj, proj, proj, st_before, lb_table, norm_g),
        in_specs=[ANY, pl.BlockSpec((tb, D), lambda h, cb: (nb - 1 - cb, 0)),
                  pl.BlockSpec((HW, D), lambda h, cb: (h, 0)), pl.BlockSpec((tb, HW), hb),
                  col(2), col(3), col(4), col(5),
                  pl.BlockSpec((HGRN_HB, nc, HD, HD), lambda h, cb: (h, nb - 1 - cb, 0, 0)),
                  pl.BlockSpec((2, HW), lambda h, cb: (0, h)), pl.BlockSpec((1, HW), lambda h, cb: (0, h))],
        out_specs=[pl.BlockSpec((4, tb, HW), lambda h, cb: (0, nb - 1 - cb, h)),
                   pl.BlockSpec((2, 8, HW), lambda h, cb: (0, 0, h))],
        out_shape=[SDS(dproj.shape, BF16), SDS((2, 8, D), F32)],
        scratch_shapes=[pltpu.VMEM((HGRN_HB, HD, HD), F32)] + [pltpu.VMEM((tb, HW), F32)] * 6,
        aliases={0: 0})


def _gmlp_bwd(dproj, dya, w_a, proj, ln_g, ln_b, wm, wm_t, b_t):
    T = dya.shape[0]
    tm = min(GMLP_BWD_TOKENS, T)

    def body(dp_in, dya_ref, wa_ref, u_ref, v_ref, lg_ref, lb_ref, wm_ref, wmt_ref, bt_ref,
             dp_ref, acc_ref, dws_ref, dmix_ref, du_s, dvn_s):
        del dp_in

        @pl.when(pl.program_id(0) == 0)
        def _():
            acc_ref[...] = jnp.zeros_like(acc_ref)
            dws_ref[...] = jnp.zeros_like(dws_ref)
            dmix_ref[...] = jnp.zeros_like(dmix_ref)

        u = u_ref[...]
        v = v_ref[...]
        lg = lg_ref[...]
        gu, t_u = _gelu(u)
        gv, t_v = _gelu(v)
        vhat, rs = _layer_norm_stats(gv)
        vnb = (vhat * lg + lb_ref[...]).astype(BF16)
        da_v = _mm_nt(dya_ref[...], wa_ref[...])
        for g in range(NG):
            cols = slice(128 * g, 128 * (g + 1))
            vng = _chunks_abreast(vnb[:, cols])
            mixed = _mm(wm_ref[g], vng) + bt_ref[:, g:g + 1]
            dag = _chunks_abreast(da_v[:, cols])
            dmx = dag * _chunks_abreast(gu[:, cols])
            du_s[:, cols] = _chunks_stacked(dag * mixed)
            dmxb = dmx.astype(BF16)
            dws_ref[:, cols] += _mm_nt(dmxb, vng)
            dmix_ref[:, cols] += sum(dmx[:, GCH * ch:GCH * (ch + 1)] for ch in range(tm // GCH))
            dvn_s[:, cols] = _chunks_stacked(_mm(wmt_ref[g], dmxb))
        dp_ref[0] = (du_s[...] * _gelu_grad(u, t_u)).astype(BF16)
        dvn = dvn_s[...]
        acc_ref[0] += _rows8(dvn * vhat)
        acc_ref[1] += _rows8(dvn)
        dvh = dvn * lg
        dgv = rs * (dvh - _mean(dvh) - vhat * _mean(dvh * vhat))
        dp_ref[1] = (dgv * _gelu_grad(v, t_v)).astype(BF16)

    row = lambda i: (0, 0)
    w3 = lambda i: (0, 0, 0)
    return pl.pallas_call(
        body, name="gmlp_bwd", grid=(T // tm,),
        in_specs=[ANY, pl.BlockSpec((tm, D), lambda i: (i, 0)), pl.BlockSpec((D, D), row),
                  pl.BlockSpec((None, tm, D), lambda i: (0, i, 0)), pl.BlockSpec((None, tm, D), lambda i: (1, i, 0)),
                  pl.BlockSpec((1, D), row), pl.BlockSpec((1, D), row),
                  pl.BlockSpec((NG, GCH, GCH), w3), pl.BlockSpec((NG, GCH, GCH), w3),
                  pl.BlockSpec((GCH, NG), row)],
        out_specs=[pl.BlockSpec((2, tm, D), lambda i: (2, i, 0)),
                   pl.BlockSpec((2, 8, D), w3), pl.BlockSpec((GCH, D), row), pl.BlockSpec((GCH, D), row)],
        out_shape=[SDS(dproj.shape, BF16), SDS((2, 8, D), F32), SDS((GCH, D), F32), SDS((GCH, D), F32)],
        scratch_shapes=[pltpu.VMEM((tm, D), F32), pltpu.VMEM((tm, D), F32)],
        input_output_aliases={0: 0},
        compiler_params=_cparams(),
    )(dproj, dya, w_a, proj, proj, ln_g, ln_b, wm, wm_t, b_t)


def _proj_bwd(dproj, w_in4, x, dx1, g_mix, job=None):
    T = x.shape[0]
    tm = min(256, T)
    order = (2, 3, 4, 5, 0, 1, 6, 7)

    def body(dp_ref, w_ref, x_ref, dx1_ref, g_ref, gx_ref, acc_ref):
        @pl.when(pl.program_id(0) == 0)
        def _():
            acc_ref[...] = jnp.zeros_like(acc_ref)

        dh = None
        for m, og in enumerate(order):
            part = _mm_nt(dp_ref[m], w_ref[og // 2, :, D * (og % 2):D * (og % 2 + 1)])
            dh = part if dh is None else dh + part
        xv = x_ref[...]
        r = lax.rsqrt(_mean(xv * xv) + EPS)
        xn = xv * r
        acc_ref[...] += _rows8(dh * xn)
        dxn = dh * g_ref[...]
        gx_ref[...] = dx1_ref[...] + r * (dxn - xn * _mean(dxn * xn))

    t = lambda i: (i, 0)
    return _call(
        body, name="proj_bwd", grid=(T // tm,),
        in_specs=[pl.BlockSpec((NIN, tm, D), lambda i: (0, i, 0)),
                  pl.BlockSpec((NCHIP, D, 2 * D), lambda i: (0, 0, 0), pipeline_mode=pl.Buffered(1)),
                  pl.BlockSpec((tm, D), t), pl.BlockSpec((tm, D), t), pl.BlockSpec((1, D), lambda i: (0, 0))],
        out_specs=[pl.BlockSpec((tm, D), t), pl.BlockSpec((8, D), lambda i: (0, 0))],
        out_shape=[SDS((T, D), F32), SDS((8, D), F32)],
        args=(dproj, w_in4, x, dx1, g_mix), job=job)


def _dw_call(name, a, b, a_spec, b_spec, o_spec, out_shape, nblk, tt, job=None, prefetch=None):
    T = a.shape[-2]

    def body(*refs):
        a_ref, b_ref, o_ref = refs[-3:]

        @pl.when(pl.program_id(1) == 0)
        def _():
            o_ref[...] = jnp.zeros_like(o_ref)
        o_ref[...] += _mm_tn(a_ref[...], b_ref[...])

    (out,), job_out = _call(
        body, name=name, grid=(nblk, T // tt), in_specs=[a_spec, b_spec], out_specs=[o_spec],
        out_shape=[out_shape], args=(a, b), job=job, prefetch=prefetch)
    return out, job_out


def _dw_in_half(name, place, hb, dproj, mine, job=None):
    tt = min(DW_IN_TOKENS, hb.shape[0])

    def comp(k, pc):
        return _component_of(2 * k + (pc[1] if mine else 1 - pc[1]))

    return _dw_call(
        name, hb, dproj,
        pl.BlockSpec((tt, D), lambda k, t, pc: (t, 0)),
        pl.BlockSpec((None, tt, D), lambda k, t, pc: (comp(k, pc), t, 0)),
        pl.BlockSpec((None, D, D), lambda k, t, pc: (k, 0, 0)),
        SDS((NCHIP, D, D), F32), NCHIP, tt, job, place)


def _dw_gate_up(h2b, dgu, job=None):
    tt = min(DW_TOKENS, h2b.shape[0])
    return _dw_call(
        "dw_gate_up", h2b, dgu,
        pl.BlockSpec((tt, D), lambda k, t: (t, 0)),
        pl.BlockSpec((None, tt, FFS), lambda k, t: (k // 2, t, k % 2)),
        pl.BlockSpec((None, D, FFS), lambda k, t: (k, 0, 0)),
        SDS((NCHIP, D, FFS), F32), NCHIP, tt, job)


def _dw_down(act, dx2b, job=None):
    tt = min(DW_TOKENS, act.shape[0])
    g, job_out = _dw_call(
        "dw_down", act, dx2b,
        pl.BlockSpec((tt, FFS), lambda k, t: (t, k)),
        pl.BlockSpec((tt, D), lambda k, t: (t, 0)),
        pl.BlockSpec((FFS, D), lambda k, t: (k, 0)),
        SDS((FF, D), F32), 2, tt, job)
    return g.reshape(NCHIP, FF // NCHIP, D), job_out


def _dw_square(name, a, b, job=None):
    tt = min(DW_TOKENS, a.shape[0])
    g, job_out = _dw_call(
        name, a, b,
        pl.BlockSpec((tt, D), lambda k, t: (t, 0)), pl.BlockSpec((tt, D), lambda k, t: (t, 0)),
        pl.BlockSpec((D, D), lambda k, t: (0, 0)), SDS((D, D), F32), 1, tt, job)
    return g.reshape(NCHIP, D // NCHIP, D), job_out


def _place():
    x, y, c = lax.axis_index("x"), lax.axis_index("y"), lax.axis_index("c")
    return x, y, c, 2 * x + y


def _chip_at(x, y, s):
    return x ^ (s >> 1), y ^ (s & 1)


class _Job:
    def __init__(self, ins, out_shapes, sems, start, finish, aliases=None, mid=None):
        self.ins, self.out_shapes, self.sems = list(ins), list(out_shapes), list(sems)
        self.start, self.finish, self.aliases = start, finish, dict(aliases or {})
        self.mid = mid if mid is not None else (lambda ins, outs, sems: None)


def _join_jobs(*jobs):
    def cut(refs, sizes):
        out, at = [], 0
        for n in sizes:
            out.append(refs[at:at + n])
            at += n
        return out

    ni = [len(j.ins) for j in jobs]
    no = [len(j.out_shapes) for j in jobs]
    ns = [len(j.sems) for j in jobs]

    def run(which):
        def go(ins, outs, sems):
            for j, a, b, c in zip(jobs, cut(ins, ni), cut(outs, no), cut(sems, ns)):
                getattr(j, which)(a, b, c)
        return go

    aliases = {}
    for k, j in enumerate(jobs):
        for a, b in j.aliases.items():
            aliases[sum(ni[:k]) + a] = sum(no[:k]) + b
    return _Job([a for j in jobs for a in j.ins], [o for j in jobs for o in j.out_shapes],
                [s for j in jobs for s in j.sems], run("start"), run("finish"), aliases, run("mid"))


def _call(body, *, name, grid, in_specs, out_specs, out_shape, args, scratch_shapes=(), aliases=None,
          job=None, prefetch=None):
    n_in, n_out, n_scr = len(in_specs), len(out_specs), len(scratch_shapes)
    npf = 0 if prefetch is None else 1
    job = job if job is not None else _Job([], [], [], lambda *a: None, lambda *a: None)
    ji, jo = len(job.ins), len(job.out_shapes)
    steps = math.prod(grid)

    def wrapped(*refs):
        pf, refs = refs[:npf], refs[npf:]
        ins, jin = refs[:n_in], refs[n_in:n_in + ji]
        o0 = n_in + ji
        outs, jout = refs[o0:o0 + n_out], refs[o0 + n_out:o0 + n_out + jo]
        s0 = o0 + n_out + jo
        scr, jsem = refs[s0:s0 + n_scr], refs[s0 + n_scr:]
        step = functools.reduce(lambda acc, ag: acc * ag[1] + pl.program_id(ag[0]), enumerate(grid), 0)
        if ji or jo:
            @pl.when(step == 0)
            def _():
                job.start(jin, jout, jsem)

        body(*pf, *ins, *outs, *scr)

        if ji or jo:
            @pl.when(step == steps // 2)
            def _():
                job.mid(jin, jout, jsem)

            @pl.when(step == steps - 1)
            def _():
                job.finish(jin, jout, jsem)

    io = {npf + a: b for a, b in dict(aliases or {}).items()}
    io.update({npf + n_in + a: n_out + b for a, b in job.aliases.items()})
    kw = dict(in_specs=list(in_specs) + [ANY] * ji, out_specs=list(out_specs) + [ANY] * jo,
              scratch_shapes=list(scratch_shapes) + job.sems)
    if npf:
        kw = dict(grid_spec=pltpu.PrefetchScalarGridSpec(num_scalar_prefetch=1, grid=grid, **kw))
    else:
        kw["grid"] = grid
    res = pl.pallas_call(
        wrapped, name=name, out_shape=list(out_shape) + job.out_shapes, input_output_aliases=io,
        compiler_params=_cparams(has_side_effects=bool(ji or jo)), **kw,
    )(*(() if prefetch is None else (prefetch,)), *args, *job.ins)
    return list(res[:n_out]), list(res[n_out:])


def _cast_shards(name, place, ws, paired=False):
    n = len(ws)
    rows, cols = ws[0].shape
    tr = 352 if rows % 352 == 0 else 256
    shape = (2, rows, 2 * cols) if paired else (NCHIP, rows, cols)
    mine = (lambda i, pc: (pc[0] // 2, i, pc[0] % 2)) if paired else (lambda i, pc: (pc[0], i, 0))

    def body(pc_ref, *refs):
        del pc_ref
        for w_ref, o_ref in zip(refs[:n], refs[n:]):
            o_ref[...] = w_ref[...].astype(BF16)

    return pl.pallas_call(
        body, name=name,
        grid_spec=pltpu.PrefetchScalarGridSpec(
            num_scalar_prefetch=1, grid=(rows // tr,),
            in_specs=[pl.BlockSpec((tr, cols), lambda i, pc: (i, 0))] * n,
            out_specs=[pl.BlockSpec((None, tr, cols), mine)] * n),
        out_shape=[SDS(shape, BF16)] * n,
        compiler_params=_cparams(),
    )(place, *ws)


def _sibling_copy(ref, send_sem, recv_sem):
    x, y, c, _ = _place()
    return pltpu.make_async_remote_copy(src_ref=ref, dst_ref=ref, send_sem=send_sem, recv_sem=recv_sem,
                                        device_id=(x, y, 1 - c), device_id_type=MESH)


def _slot(arr, chip):
    if arr.shape[0] == NCHIP:
        return arr.at[chip]
    cols = arr.shape[2] // 2
    return arr.at[chip // 2, :, pl.ds(pl.multiple_of((chip % 2) * cols, 128), cols)]


def _half_rows(arr, slot, core):
    half = arr.shape[1] // 2
    return _slot(arr, slot).at[pl.ds(pl.multiple_of(core * half, 16), half)]


def _quarter_rows(arr, slot, core, q):
    quarter = arr.shape[1] // 4
    return _slot(arr, slot).at[pl.ds(pl.multiple_of((2 * core + q) * quarter, 16), quarter)]


def _chip_copy(ref, dist, send_sem, recv_sem):
    x, y, c, _ = _place()
    cx, cy = _chip_at(x, y, dist)
    return pltpu.make_async_remote_copy(src_ref=ref, dst_ref=ref, send_sem=send_sem, recv_sem=recv_sem,
                                        device_id=(cx, cy, c), device_id_type=MESH)


def _gather_sems(n):
    dma = pltpu.SemaphoreType.DMA
    return [dma((n, 2))] * 4 + [dma((n, 4))] * 2


def _gather_start(arrs, sems):
    dsend, drecv = sems[0], sems[1]
    _, _, c, j = _place()
    for w, arr in enumerate(arrs):
        for dist in (1, 2):
            _chip_copy(_half_rows(arr, j, c), dist, dsend.at[w, dist - 1], drecv.at[w, dist - 1]).start()


def _gather_land(arrs, sems, dist, first=0):
    dsend, drecv, rsend, rrecv, fsend, frecv = sems
    _, _, c, j = _place()
    if dist < 3:
        other = 3 - dist
        for w, arr in enumerate(arrs, first):
            landed = _half_rows(arr, j ^ dist, c)
            _chip_copy(landed, dist, dsend.at[w, dist - 1], drecv.at[w, dist - 1]).wait_recv()
            relay = _quarter_rows(arr, j ^ dist, c, other - 1)
            _chip_copy(relay, other, rsend.at[w, other - 1], rrecv.at[w, other - 1]).start()
            _sibling_copy(landed, fsend.at[w, dist - 1], frecv.at[w, dist - 1]).start()
        for w, arr in enumerate(arrs, first):
            theirs = _half_rows(arr, j ^ dist, 1 - c)
            _sibling_copy(theirs, fsend.at[w, dist - 1], frecv.at[w, dist - 1]).wait_recv()
    else:
        for w, arr in enumerate(arrs, first):
            for via in (1, 2):
                piece = _quarter_rows(arr, j ^ 3, c, via - 1)
                _chip_copy(piece, via, rsend.at[w, via - 1], rrecv.at[w, via - 1]).wait_recv()
                _sibling_copy(piece, fsend.at[w, 1 + via], frecv.at[w, 1 + via]).start()
        for w, arr in enumerate(arrs, first):
            for via in (1, 2):
                theirs = _quarter_rows(arr, j ^ 3, 1 - c, via - 1)
                _sibling_copy(theirs, fsend.at[w, 1 + via], frecv.at[w, 1 + via]).wait_recv()


def _gather_drain(arrs, sems):
    dsend, drecv, rsend, rrecv, fsend, frecv = sems
    _, _, c, j = _place()
    for w, arr in enumerate(arrs):
        for dist in (1, 2):
            other = 3 - dist
            _chip_copy(_half_rows(arr, j, c), dist, dsend.at[w, dist - 1], drecv.at[w, dist - 1]).wait_send()
            _chip_copy(_quarter_rows(arr, j ^ dist, c, other - 1), other,
                       rsend.at[w, other - 1], rrecv.at[w, other - 1]).wait_send()
            _sibling_copy(_half_rows(arr, j ^ dist, c), fsend.at[w, dist - 1], frecv.at[w, dist - 1]).wait_send()
            _sibling_copy(_quarter_rows(arr, j ^ 3, c, dist - 1),
                          fsend.at[w, 1 + dist], frecv.at[w, 1 + dist]).wait_send()


def _gather_neighbours(arrs, sems):
    _gather_land(arrs, sems, 1)
    _gather_land(arrs, sems, 2)


def _gather_finish(arrs, sems):
    _gather_land(arrs, sems, 3)
    _gather_drain(arrs, sems)


def _gather_job(arrs):
    n = len(arrs)
    return _Job(arrs, [SDS(a.shape, a.dtype) for a in arrs], _gather_sems(n),
                lambda ins, outs, sems: _gather_start(outs, sems),
                lambda ins, outs, sems: _gather_finish(outs, sems), {k: k for k in range(n)},
                mid=lambda ins, outs, sems: _gather_neighbours(outs, sems))


def _exchange_job(arrs, out_shapes, n, copies):
    def start(ins, outs, sems):
        for cp in copies(ins, outs, sems[0], sems[1]):
            cp.start()

    def finish(ins, outs, sems):
        for cp in copies(ins, outs, sems[0], sems[1]):
            cp.wait()

    return _Job(arrs, out_shapes, [pltpu.SemaphoreType.DMA((n,))] * 2, start, finish)


def _pair_exchange_job(grads):
    def copies(ins, outs, send_sem, recv_sem):
        x, y, c, _ = _place()
        res = []
        for w in range(len(grads)):
            half = ins[w].shape[1] // 2
            theirs = pl.ds(pl.multiple_of((1 - c) * half, 8), half)
            res.append(pltpu.make_async_remote_copy(
                src_ref=ins[w].at[:, theirs, :], dst_ref=outs[w], send_sem=send_sem.at[w],
                recv_sem=recv_sem.at[w], device_id=(x, y, 1 - c), device_id_type=MESH))
        return res

    return _exchange_job(grads, [SDS((NCHIP, g.shape[1] // 2, g.shape[2]), F32) for g in grads],
                         len(grads), copies)


def _row_tile(rows, cols):
    tr = rows
    while tr * cols * 4 > ELEMENTWISE_BLOCK_BYTES and tr % 32 == 0:
        tr //= 2
    return tr


def _pair_sums(name, place, gs, sibs):
    n = len(gs)
    half, cols = sibs[0].shape[1], sibs[0].shape[2]
    tr = _row_tile(half, cols)
    nt = half // tr
    mine = nt if gs[0].shape[1] == 2 * half else 0

    def body(pc_ref, *refs):
        del pc_ref
        for g_ref, s_ref, own_ref, out_ref in zip(refs[:n], refs[n:2 * n], refs[2 * n:3 * n], refs[3 * n:]):
            v = g_ref[...] + s_ref[...]

            @pl.when(pl.program_id(1) == 0)
            def _():
                own_ref[...] = v

            @pl.when(pl.program_id(1) > 0)
            def _():
                out_ref[...] = v.astype(BF16)

    res = pl.pallas_call(
        body, name=name,
        grid_spec=pltpu.PrefetchScalarGridSpec(
            num_scalar_prefetch=1, grid=(nt, NCHIP),
            in_specs=[pl.BlockSpec((None, tr, cols), lambda i, s, pc: (pc[0] ^ s, pc[1] * mine + i, 0))] * n
            + [pl.BlockSpec((None, tr, cols), lambda i, s, pc: (pc[0] ^ s, i, 0))] * n,
            out_specs=[pl.BlockSpec((tr, cols), lambda i, s, pc: (i, 0))] * n
            + [pl.BlockSpec((None, tr, cols), lambda i, s, pc: (jnp.maximum(s - 1, 0), i, 0))] * n),
        out_shape=[SDS((half, cols), F32)] * n + [SDS((NCHIP - 1, half, cols), BF16)] * n,
        compiler_params=_cparams(),
    )(place, *gs, *sibs)
    return res[:n], res[n:]


def _chip_exchange_job(parts):
    def copies(ins, outs, send_sem, recv_sem):
        x, y, c, _ = _place()
        res = []
        for w in range(len(parts)):
            for s in range(1, NCHIP):
                cx, cy = _chip_at(x, y, s)
                k = w * (NCHIP - 1) + s - 1
                res.append(pltpu.make_async_remote_copy(
                    src_ref=ins[w].at[s - 1], dst_ref=outs[w].at[s - 1], send_sem=send_sem.at[k],
                    recv_sem=recv_sem.at[k], device_id=(cx, cy, c), device_id_type=MESH))
        return res

    return _exchange_job(parts, [SDS((NCHIP - 1,) + p.shape[1:], BF16) for p in parts],
                         len(parts) * (NCHIP - 1), copies)


def _chip_sums(name, owns, rems):
    n = len(owns)
    half, cols = owns[0].shape
    tr = _row_tile(half, cols)

    def body(*refs):
        for own_ref, rem_ref, out_ref in zip(refs[:n], refs[n:2 * n], refs[2 * n:]):
            out_ref[...] = (((own_ref[...] + rem_ref[0].astype(F32)) + rem_ref[1].astype(F32))
                            + rem_ref[2].astype(F32))

    return pl.pallas_call(
        body, name=name, grid=(half // tr,),
        in_specs=[pl.BlockSpec((tr, cols), lambda i: (i, 0))] * n
        + [pl.BlockSpec((NCHIP - 1, tr, cols), lambda i: (0, i, 0))] * n,
        out_specs=[pl.BlockSpec((tr, cols), lambda i: (i, 0))] * n,
        out_shape=[SDS((half, cols), F32)] * n,
        compiler_params=_cparams(),
    )(*owns, *rems)


def _share_halves_job(halves):
    def copies(ins, outs, send_sem, recv_sem):
        x, y, c, _ = _place()
        return [pltpu.make_async_remote_copy(
            src_ref=ins[w], dst_ref=outs[w], send_sem=send_sem.at[w], recv_sem=recv_sem.at[w],
            device_id=(x, y, 1 - c), device_id_type=MESH) for w in range(len(halves))]

    return _exchange_job(halves, [SDS(h.shape, F32) for h in halves], len(halves), copies)


def _adamw_math(w, g, m, v):
    m = B1 * m + (1.0 - B1) * g
    v = B2 * v + (1.0 - B2) * (g * g)
    m_hat = m / (1.0 - B1 ** STEP)
    v_hat = v / (1.0 - B2 ** STEP)
    delta = -LR * (m_hat / (jnp.sqrt(v_hat) + AEPS) + WD * w)
    return delta, m, v


def _adamws(name, place, ws, owns, sibs, ms, vs):
    n = len(ws)
    rows, cols = ws[0].shape
    by_cols = owns[0].shape[0] == rows
    half, pc_cols = (rows, cols // 2) if by_cols else (rows // 2, cols)
    tr = _row_tile(half, pc_cols)
    nt = half // tr

    def body(pc_ref, *refs):
        ins, outs = refs[:5 * n], refs[5 * n:]
        for k in range(n):
            w_ref, own_ref, sib_ref, m_ref, v_ref = ins[5 * k:5 * k + 5]
            g = jnp.where(pl.program_id(0) == pc_ref[1], own_ref[...], sib_ref[...])
            d, mn, vn = _adamw_math(w_ref[...], g, m_ref[...], v_ref[...])
            for ref, val in zip(outs[4 * k:4 * k + 4], (g, d, mn, vn)):
                ref[...] = val

    full = pl.BlockSpec((tr, pc_cols), (lambda h, i, pc: (i, h)) if by_cols else (lambda h, i, pc: (h * nt + i, 0)))
    part = pl.BlockSpec((tr, pc_cols), lambda h, i, pc: (i, 0))
    res = pl.pallas_call(
        body, name=name,
        grid_spec=pltpu.PrefetchScalarGridSpec(
            num_scalar_prefetch=1, grid=(2, nt),
            in_specs=[full, part, part, full, full] * n, out_specs=[full] * (4 * n)),
        out_shape=[SDS((rows, cols), F32)] * (4 * n),
        compiler_params=_cparams(),
    )(place, *[a for group in zip(ws, owns, sibs, ms, vs) for a in group])
    return [tuple(res[4 * k:4 * k + 4]) for k in range(n)]


ON_SPARSECORE = ("w_gate_up", "w_down", "w_branch_a")
SC_TILES = 32
SC_LANES = 16


def _adamw_sparsecore(name, w, own, sib, m, v, after):
    rows, cols = w.shape
    groups, half_groups = rows // 8, rows // 16
    rounds = -(-groups // SC_TILES)

    def body(w_hbm, own_hbm, sib_hbm, m_hbm, v_hbm, *rest):
        g_out, d_out, mo_out, vo_out, wb, gb, mb, vb, db = rest[len(after):]
        tile = lax.axis_index("sc_tile") * 2 + lax.axis_index("sc_core")
        c = lax.axis_index("c")
        for k in range(rounds):
            grp = tile + SC_TILES * k

            @pl.when(grp < groups)
            def _():
                rws = pl.ds(pl.multiple_of(grp * 8, 8), 8)
                in_half = pl.ds(pl.multiple_of((grp % half_groups) * 8, 8), 8)
                mine = (grp // half_groups) == c

                @pl.when(mine)
                def _():
                    pltpu.sync_copy(own_hbm.at[in_half], gb)

                @pl.when(jnp.logical_not(mine))
                def _():
                    pltpu.sync_copy(sib_hbm.at[in_half], gb)

                pltpu.sync_copy(w_hbm.at[rws], wb)
                pltpu.sync_copy(m_hbm.at[rws], mb)
                pltpu.sync_copy(v_hbm.at[rws], vb)

                @pl.loop(0, cols, step=SC_LANES)
                def _(j):
                    for r in range(8):
                        at = (r, pl.ds(j, SC_LANES))
                        d, mn, vn = _adamw_math(wb[at], gb[at], mb[at], vb[at])
                        db[at] = d
                        mb[at] = mn
                        vb[at] = vn

                pltpu.sync_copy(gb, g_out.at[rws])
                pltpu.sync_copy(db, d_out.at[rws])
                pltpu.sync_copy(mb, mo_out.at[rws])
                pltpu.sync_copy(vb, vo_out.at[rws])

    return pl.kernel(
        body, name=name, out_type=[SDS((rows, cols), F32)] * 4,
        mesh=plsc.VectorSubcoreMesh(core_axis_name="sc_core", subcore_axis_name="sc_tile"),
        scratch_types=[pltpu.VMEM((8, cols), F32)] * 5,
    )(w, own, sib, m, v, *after)


def _small_allreduce_adamw(sp, wmv, job):
    shape = sp.shape
    ji, jo = len(job.ins), len(job.out_shapes)

    def body(sp_ref, wmv_ref, *rest):
        jin, (g_ref, d_ref, mo_ref, vo_ref), jout = rest[:ji], rest[ji:ji + 4], rest[ji + 4:ji + 4 + jo]
        sib_s, pair_s, chip_s, send_sem, recv_sem = rest[ji + 4 + jo:ji + 9 + jo]
        jsem = rest[ji + 9 + jo:]
        job.start(jin, jout, jsem)
        x, y, c, j = _place()
        cp = pltpu.make_async_remote_copy(
            src_ref=sp_ref, dst_ref=sib_s, send_sem=send_sem.at[0], recv_sem=recv_sem.at[0],
            device_id=(x, y, 1 - c), device_id_type=MESH)
        cp.start()
        cp.wait()
        pair_s[...] = sp_ref[...] + sib_s[...]
        half = shape[0] // 2
        mine = pl.ds(pl.multiple_of(c * half, 8), half)
        cps = []
        for s in range(1, NCHIP):
            cx, cy = _chip_at(x, y, s)
            cp = pltpu.make_async_remote_copy(
                src_ref=pair_s.at[mine], dst_ref=chip_s.at[s, mine], send_sem=send_sem.at[s],
                recv_sem=recv_sem.at[s], device_id=(cx, cy, c), device_id_type=MESH)
            cp.start()
            cps.append(cp)
        chip_s[0] = pair_s[...]
        for cp in cps:
            cp.wait()
        cps = []
        for s in range(1, NCHIP):
            cp = pltpu.make_async_remote_copy(
                src_ref=chip_s.at[s, mine], dst_ref=chip_s.at[s, mine], send_sem=send_sem.at[NCHIP + s],
                recv_sem=recv_sem.at[NCHIP + s], device_id=(x, y, 1 - c), device_id_type=MESH)
            cp.start()
            cps.append(cp)
        for cp in cps:
            cp.wait()
        tot = chip_s[j]
        for k in range(1, NCHIP):
            tot = tot + chip_s[k ^ j]
        g_ref[...] = tot
        d, mn, vn = _adamw_math(wmv_ref[0], tot, wmv_ref[1], wmv_ref[2])
        d_ref[...] = d
        mo_ref[...] = mn
        vo_ref[...] = vn
        job.mid(jin, jout, jsem)
        job.finish(jin, jout, jsem)

    vm = pl.BlockSpec(memory_space=pltpu.VMEM)
    res = pl.pallas_call(
        body, name="small_allreduce_adamw",
        in_specs=[vm] * 2 + [ANY] * ji, out_specs=[vm] * 4 + [ANY] * jo,
        out_shape=[SDS(shape, F32)] * 4 + job.out_shapes,
        scratch_shapes=[pltpu.VMEM(shape, F32), pltpu.VMEM(shape, F32), pltpu.VMEM((NCHIP,) + shape, F32),
                        pltpu.SemaphoreType.DMA((2 * NCHIP,)), pltpu.SemaphoreType.DMA((2 * NCHIP,))] + job.sems,
        input_output_aliases={2 + a: 4 + b for a, b in job.aliases.items()},
        compiler_params=pltpu.CompilerParams(has_side_effects=True),
    )(sp, wmv, *job.ins)
    return res[:4], res[4:]


def _pack_small(first, mix, ln_g, ln_b, b_s, lbt, hn, ffn, fin, w_s):
    rows = [first.reshape(1, D), mix.reshape(1, D), ln_g.reshape(1, D), ln_b.reshape(1, D),
            b_s.reshape(1, D), lbt.reshape(2, D), hn.reshape(1, D), ffn.reshape(1, D), fin.reshape(1, D),
            jnp.zeros((6, D), F32)]
    return jnp.concatenate(rows + [w_s.reshape(NG, GCH, GCH).transpose(1, 0, 2).reshape(GCH, D)], axis=0)


def _unpack_small(p):
    w_s = p[16:].reshape(GCH, NG, GCH).transpose(1, 0, 2).reshape(1, NG, GCH, GCH)
    return dict(norm_mix_g=p[1:2], gmlp_ln_g=p[2:3], gmlp_ln_b=p[3:4], gmlp_b_s=p[4].reshape(1, NG, GCH),
                hgrn_lb_table=p[5:7], hgrn_norm_g=p[7:8], norm_ffn_g=p[8:9], norm_final_g=p[9],
                gmlp_w_s=w_s)


SMALL = ("norm_mix_g", "gmlp_ln_g", "gmlp_ln_b", "gmlp_w_s", "gmlp_b_s", "hgrn_lb_table", "hgrn_norm_g",
         "norm_ffn_g", "norm_final_g")
BIG = ("w_in", "w_gate_up", "w_branch_a", "w_branch_b", "w_out", "w_down")
ORDER = ("norm_mix_g", "w_in", "gmlp_ln_g", "gmlp_ln_b", "gmlp_w_s", "gmlp_b_s", "hgrn_lb_table",
         "hgrn_norm_g", "w_branch_a", "w_branch_b", "w_out", "norm_ffn_g", "w_gate_up", "w_down",
         "norm_final_g")


def kernel(x, norm_mix_g, w_in, gmlp_ln_g, gmlp_ln_b, gmlp_w_s, gmlp_b_s, hgrn_lb_table, hgrn_norm_g, w_branch_a, w_branch_b, w_out, norm_ffn_g, w_gate_up, w_down, norm_final_g, loss_target, m_norm_mix_g, m_w_in, m_gmlp_ln_g, m_gmlp_ln_b, m_gmlp_w_s, m_gmlp_b_s, m_hgrn_lb_table, m_hgrn_norm_g, m_w_branch_a, m_w_branch_b, m_w_out, m_norm_ffn_g, m_w_gate_up, m_w_down, m_norm_final_g, v_norm_mix_g, v_w_in, v_gmlp_ln_g, v_gmlp_ln_b, v_gmlp_w_s, v_gmlp_b_s, v_hgrn_lb_table, v_hgrn_norm_g, v_w_branch_a, v_w_branch_b, v_w_out, v_norm_ffn_g, v_w_gate_up, v_w_down, v_norm_final_g):
    args = dict(locals())
    T = x.shape[1]
    xs = x.reshape(T, D)
    target = loss_target.reshape(T, D)
    big = {n: args[n].reshape(args[n].shape[1:]) for n in BIG}
    big_m = {n: args["m_" + n].reshape(args[n].shape[1:]) for n in BIG}
    big_v = {n: args["v_" + n].reshape(args[n].shape[1:]) for n in BIG}

    x_i, y_i, c_i = lax.axis_index("x"), lax.axis_index("y"), lax.axis_index("c")
    place = jnp.stack([2 * x_i + y_i, c_i]).astype(jnp.int32)
    def by_shape(names):
        groups = []
        for n in names:
            if groups and big[groups[-1][0]].shape == big[n].shape:
                groups[-1].append(n)
            else:
                groups.append([n])
        return groups

    cast = {}
    for grp in by_shape(BIG):
        cast.update(zip(grp, _cast_shards("cast_" + grp[0], place, [big[n] for n in grp],
                                          paired=grp[0] == "w_gate_up")))
    tril = jnp.tril(jnp.ones((GCH, GCH), bool))
    wm = jnp.where(tril, gmlp_w_s[0], 0.0).astype(BF16)
    wm_t = jnp.swapaxes(wm, 1, 2)
    b_t = gmlp_b_s[0].T

    (proj, hb), w_in4, (w_a4, w_b4, w_out4, w_down4) = _proj_fwd(
        place, xs, norm_mix_g, cast["w_in"], [cast[n] for n in ("w_branch_a", "w_branch_b", "w_out", "w_down")])
    (ab,), _ = _gmlp_fwd(proj, gmlp_ln_g, gmlp_ln_b, wm, b_t)
    (o_raw, obb, st_before), (w_gu,) = _hgrn_fwd(
        proj, hgrn_lb_table, hgrn_norm_g, job=_gather_job([cast["w_gate_up"]]))
    w_a, w_b, w_o = (w.reshape(D, D) for w in (w_a4, w_b4, w_out4))
    (mgb, x1), _ = _merge_fwd(xs, ab, obb, proj, w_a, w_b, w_o)
    w_dn = w_down4.reshape(FF, D)
    act, dx2b, h2b, dgu, dx1, dx1b, acc_ffn = _ffn_fwd_bwd(
        x1, target, norm_ffn_g, norm_final_g.reshape(1, D), w_gu, w_dn)

    grads, owns, parts, halves, sibh = {}, {}, {}, {}, {}

    def pair_sums(names, sibs):
        sib_of = dict(zip(names, sibs))
        for grp in by_shape(names):
            o, p = _pair_sums("rs_pair_sum_" + grp[0], place, [grads[n] for n in grp], [sib_of[n] for n in grp])
            owns.update(zip(grp, o))
            parts.update(zip(grp, p))

    def chip_sums(names, got):
        rem_of = dict(zip(names, got))
        for grp in by_shape(names):
            h = _chip_sums("rs_chip_sum_" + grp[0], [owns[n] for n in grp], [rem_of[n] for n in grp])
            halves.update(zip(grp, h))

    ffn, mix = ("w_gate_up", "w_down"), ("w_branch_a", "w_branch_b", "w_out")
    grads["w_gate_up"], _ = _dw_gate_up(h2b, dgu)
    grads["w_down"], _ = _dw_down(act, dx2b)
    (dya, dyb, dproj), got = _merge_bwd(
        dx1b, ab, obb, proj, w_o, w_a, w_b, job=_pair_exchange_job([grads[n] for n in ffn]))
    pair_sums(ffn, got)
    grads["w_branch_a"], _ = _dw_square("dw_branch_a", ab, dya)
    grads["w_branch_b"], _ = _dw_square("dw_branch_b", obb, dyb)
    grads["w_out"], _ = _dw_square("dw_out", mgb, dx1b)
    (dproj, acc_hgrn), got = _hgrn_bwd(
        dproj, dyb, w_b, o_raw, proj, st_before, hgrn_lb_table, hgrn_norm_g,
        job=_join_jobs(_chip_exchange_job([parts[n] for n in ffn]), _pair_exchange_job([grads[n] for n in mix])))
    chip_sums(ffn, got[:2])
    pair_sums(mix, got[2:])
    dproj, acc_ln, dws, dmix = _gmlp_bwd(dproj, dya, w_a, proj, gmlp_ln_g, gmlp_ln_b, wm, wm_t, b_t)
    for_sibling, got = _dw_in_half(
        "dw_in_sibling_half", place, hb, dproj, False,
        job=_join_jobs(_share_halves_job([halves[n] for n in ffn]), _chip_exchange_job([parts[n] for n in mix])))
    sibh.update(zip(ffn, got[:2]))
    chip_sums(mix, got[2:])
    grads["w_in"], got = _dw_in_half(
        "dw_in_own_half", place, hb, dproj, True, job=_share_halves_job([for_sibling]))
    pair_sums(("w_in",), got)
    (grad_x, acc_mix), got = _proj_bwd(
        dproj, w_in4, xs, dx1, norm_mix_g,
        job=_join_jobs(_chip_exchange_job([parts["w_in"]]), _share_halves_job([halves[n] for n in mix])))
    chip_sums(("w_in",), got[:1])
    sibh.update(zip(mix, got[1:]))

    lbv = jax.nn.sigmoid(hgrn_lb_table[0] - hgrn_lb_table[1])
    d_t0 = jnp.sum(acc_hgrn[0], axis=0) * lbv * (1.0 - lbv)
    loss_row = jnp.zeros((D,), F32).at[0].set(jnp.sum(acc_ffn[0]))
    dws_m = jnp.where(tril[:, None, :], dws.reshape(GCH, NG, GCH), 0.0).transpose(1, 0, 2)
    db_s = jnp.sum(dmix.reshape(GCH, NG, GCH), axis=-1).T
    sp = _pack_small(loss_row, jnp.sum(acc_mix, 0), jnp.sum(acc_ln[0], 0), jnp.sum(acc_ln[1], 0), db_s,
                     jnp.stack([d_t0, -d_t0]), jnp.sum(acc_hgrn[1], 0), jnp.sum(acc_ffn[2], 0),
                     jnp.sum(acc_ffn[1], 0), dws_m)
    zero = jnp.zeros((D,), F32)

    def pack(prefix):
        a = lambda n: args[prefix + n]
        return _pack_small(zero, a("norm_mix_g"), a("gmlp_ln_g"), a("gmlp_ln_b"), a("gmlp_b_s"),
                           a("hgrn_lb_table"), a("hgrn_norm_g"), a("norm_ffn_g"), a("norm_final_g"),
                           a("gmlp_w_s"))

    packed, (sibh["w_in"],) = _small_allreduce_adamw(
        sp, jnp.stack([pack(""), pack("m_"), pack("v_")]), _share_halves_job([halves["w_in"]]))
    loss = packed[0][0, 0]
    small = [_unpack_small(p) for p in packed]
    out = {n: tuple(s[n] for s in small) for n in SMALL}
    queued = []
    for grp in sorted(by_shape(BIG), key=lambda g: ON_SPARSECORE.index(g[0]) if g[0] in ON_SPARSECORE else -1):
        if grp[0] in ON_SPARSECORE:
            res = []
            for n in grp:
                res.append(_adamw_sparsecore("adamw_sc_" + n, big[n], halves[n], sibh[n], big_m[n], big_v[n],
                                             queued[-1:]))
                queued.append(res[-1][0])
        else:
            res = _adamws("adamw_" + grp[0], place,
                          *[[d[n] for n in grp] for d in (big, halves, sibh, big_m, big_v)])
        for n, quad in zip(grp, res):
            out[n] = tuple(a.reshape(args[n].shape) for a in quad)
    return (loss, grad_x.reshape(x.shape), *[out[n][0] for n in ORDER], *[out[n][1] for n in ORDER],
            *[out[n][2] for n in ORDER], *[out[n][3] for n in ORDER])
```

```python
import functools
import math

import jax
import jax.numpy as jnp
from jax import lax
from jax.experimental import pallas as pl
from jax.experimental.pallas import tpu as pltpu
from jax.experimental.pallas import tpu_sc as plsc

F32 = jnp.float32
BF16 = jnp.bfloat16
SDS = jax.ShapeDtypeStruct
MESH = pl.DeviceIdType.MESH
ANY = pl.BlockSpec(memory_space=pl.ANY)

D = 1024
NIN = 8
NG = 8
GCH = 128
NH = 8
HD = 128
HCH = 64
HGRN_HB = 8
HGRN_TOKENS = 256
GMLP_FWD_TOKENS = 512
GMLP_BWD_TOKENS = 256
HW = HGRN_HB * HD
DW_TOKENS = 2048
DW_IN_TOKENS = 4096
ELEMENTWISE_BLOCK_BYTES = 2 * 1024 * 1024
PROJ_OUT_SLOTS = 4
FF = 2816
FFS = 1408
NCHIP = 4
EPS = 1e-6
QSCALE = HD ** -0.5
GELU_C0 = math.sqrt(2.0 / math.pi)
GELU_C1 = 0.044715
LR, B1, B2, AEPS, WD, STEP = 0.001, 0.9, 0.999, 1e-08, 0.01, 10
VMEM_LIMIT_V7X = 56 * 1024 * 1024
SP_ROWS = 144


def _cparams(**kw):
    return pltpu.CompilerParams(vmem_limit_bytes=VMEM_LIMIT_V7X, **kw)


def _mm(a, b):
    return jnp.dot(a, b, preferred_element_type=F32)


def _mm_nt(a, b):
    return lax.dot_general(a, b, (((1,), (1,)), ((), ())), preferred_element_type=F32)


def _mm_tn(a, b):
    return lax.dot_general(a, b, (((0,), (0,)), ((), ())), preferred_element_type=F32)


def _rows8(x):
    r, c = x.shape
    return jnp.sum(x.reshape(r // 8, 8, c), axis=0)


def _mean(x):
    return jnp.mean(x, axis=-1, keepdims=True)


def _sigmoid(x):
    return 1.0 / (1.0 + jnp.exp(-x))


def _gelu(x):
    t = jnp.tanh(GELU_C0 * (x + GELU_C1 * x * x * x))
    return 0.5 * x * (1.0 + t), t


def _gelu_grad(x, t):
    return 0.5 * (1.0 + t) + 0.5 * x * (1.0 - t * t) * (GELU_C0 * (1.0 + 3.0 * GELU_C1 * x * x))


def _component_of(group):
    return jnp.where(group < 6, (group + 4) % 6, group)


def _proj_fwd(place, x, g_mix, w_in4, later):
    T = x.shape[0]
    tm = min(1024, T)
    ni = T // tm
    n = len(later)

    def body(pc_ref, x_ref, g_ref, *rest):
        proj_ref, h_ref, w_all = rest[1 + n:4 + n]
        gathered = rest[4 + n:4 + 2 * n]
        hs, wbuf, wsem, obuf, osem = rest[4 + 2 * n:9 + 2 * n]
        w_sems, later_sems = rest[9 + 2 * n:15 + 2 * n], rest[15 + 2 * n:]
        jp, i = pl.program_id(0), pl.program_id(1)
        w_cols = [w_all.at[:, :, pl.ds(k * D, D)] for k in range(2)]

        def w_copy(blk):
            cols = pl.ds(pl.multiple_of((blk % 2) * D, 128), D)
            return pltpu.make_async_copy(w_all.at[pc_ref[0] ^ (blk // 2), :, cols], wbuf.at[blk % 2],
                                         wsem.at[blk % 2])

        @pl.when((jp == 0) & (i == 0))
        def _():
            _gather_start(w_cols, w_sems)
            w_copy(jp).start()

        @pl.when(i == 0)
        def _():
            w_copy(jp).wait()

        @pl.when(jp == 0)
        def _():
            xv = x_ref[...]
            r = lax.rsqrt(_mean(xv * xv) + EPS)
            hb = (xv * r * g_ref[...]).astype(BF16)
            hs[i] = hb
            h_ref[...] = hb

        step = jp * ni + i
        slot = step % PROJ_OUT_SLOTS

        def o_copy(slot_):
            comp = 2 * (pc_ref[0] ^ (jp // 2)) + jp % 2
            return pltpu.make_async_copy(
                obuf.at[slot_], proj_ref.at[comp, pl.ds(pl.multiple_of(i * tm, 8), tm)], osem.at[slot_])

        @pl.when(step >= PROJ_OUT_SLOTS)
        def _():
            o_copy(slot).wait()

        obuf[slot] = _mm(hs[i], wbuf[jp % 2])
        o_copy(slot).start()

        @pl.when(step == NIN * ni - 1)
        def _():
            for k in range(PROJ_OUT_SLOTS):
                o_copy((slot + 1 + k) % PROJ_OUT_SLOTS).wait()

        for nxt in range(1, NIN):
            @pl.when((jp == nxt - 1) & (i == ni - 1))
            def _():
                if nxt >= 2:
                    _gather_land([w_cols[nxt % 2]], w_sems, nxt // 2, first=nxt % 2)
                if nxt == 5:
                    _gather_start(gathered, later_sems)
                if nxt == NIN - 1:
                    _gather_neighbours(gathered, later_sems)
                w_copy(jp + 1).start()

        @pl.when((jp == NIN - 1) & (i == ni - 1))
        def _():
            _gather_drain(w_cols, w_sems)
            _gather_finish(gathered, later_sems)

    tile = lambda jp, i, pc: (jnp.where(jp == 0, i, ni - 1), 0)
    res = pl.pallas_call(
        body, name="proj_fwd",
        grid_spec=pltpu.PrefetchScalarGridSpec(
            num_scalar_prefetch=1, grid=(NIN, ni),
            in_specs=[pl.BlockSpec((tm, D), tile), pl.BlockSpec((1, D), lambda jp, i, pc: (0, 0))] + [ANY] * (1 + n),
            out_specs=[ANY, pl.BlockSpec((tm, D), tile)] + [ANY] * (1 + n),
            scratch_shapes=[pltpu.VMEM((ni, tm, D), BF16), pltpu.VMEM((2, D, D), BF16),
                            pltpu.SemaphoreType.DMA((2,)), pltpu.VMEM((PROJ_OUT_SLOTS, tm, D), F32),
                            pltpu.SemaphoreType.DMA((PROJ_OUT_SLOTS,))] + _gather_sems(2) + _gather_sems(n)),
        out_shape=[SDS((NIN, T, D), F32), SDS((T, D), BF16), SDS(w_in4.shape, BF16)]
        + [SDS(a.shape, a.dtype) for a in later],
        input_output_aliases={3 + k: 2 + k for k in range(1 + n)},
        compiler_params=_cparams(has_side_effects=True),
    )(place, x, g_mix, w_in4, *later)
    return res[:2], res[2], res[3:]


def _chunks_abreast(x):
    return jnp.concatenate([x[GCH * ch:GCH * (ch + 1)] for ch in range(x.shape[0] // GCH)], axis=1)


def _chunks_stacked(x):
    return jnp.concatenate([x[:, GCH * ch:GCH * (ch + 1)] for ch in range(x.shape[1] // GCH)], axis=0)


def _layer_norm_stats(gv):
    mu = _mean(gv)
    xc = gv - mu
    rs = lax.rsqrt(_mean(xc * xc) + EPS)
    return xc * rs, rs


def _gmlp_fwd(proj, ln_g, ln_b, wm, b_t, job=None):
    T = proj.shape[1]
    tm = min(GMLP_FWD_TOKENS, T)

    def body(u_ref, v_ref, lg_ref, lb_ref, wm_ref, bt_ref, a_ref, a_s):
        gu, _ = _gelu(u_ref[...])
        gv, _ = _gelu(v_ref[...])
        vhat, _ = _layer_norm_stats(gv)
        vnb = (vhat * lg_ref[...] + lb_ref[...]).astype(BF16)
        for g in range(NG):
            cols = slice(128 * g, 128 * (g + 1))
            mixed = _mm(wm_ref[g], _chunks_abreast(vnb[:, cols])) + bt_ref[:, g:g + 1]
            a_s[:, cols] = gu[:, cols] * _chunks_stacked(mixed)
        a_ref[...] = a_s[...].astype(BF16)

    row = lambda i: (0, 0)
    return _call(
        body, name="gmlp_fwd", grid=(T // tm,), job=job, args=(proj, proj, ln_g, ln_b, wm, b_t),
        in_specs=[pl.BlockSpec((None, tm, D), lambda i: (0, i, 0)), pl.BlockSpec((None, tm, D), lambda i: (1, i, 0)),
                  pl.BlockSpec((1, D), row), pl.BlockSpec((1, D), row),
                  pl.BlockSpec((NG, GCH, GCH), lambda i: (0, 0, 0)), pl.BlockSpec((GCH, NG), row)],
        out_specs=[pl.BlockSpec((tm, D), lambda i: (i, 0))],
        out_shape=[SDS((T, D), BF16)],
        scratch_shapes=[pltpu.VMEM((tm, D), F32)])


def _cumsum64(x, row):
    for s in (1, 2, 4, 8, 16, 32):
        x = x + jnp.where(row >= s, pltpu.roll(x, s, 0), 0.0)
    return x


def _revcumsum64(x, row):
    n = x.shape[0]
    for s in (1, 2, 4, 8, 16, 32):
        x = x + jnp.where(row < HCH - s, pltpu.roll(x, n - s, 0), 0.0)
    return x


def _head_mean(x):
    parts = [jnp.broadcast_to(_mean(x[:, HD * h:HD * (h + 1)]), (x.shape[0], HD)) for h in range(x.shape[1] // HD)]
    return jnp.concatenate(parts, axis=1)


def _seg_sum(x):
    n, c = x.shape
    s = jnp.sum(x.reshape(n // HCH, HCH, c), axis=1, keepdims=True)
    return jnp.broadcast_to(s, (n // HCH, HCH, c)).reshape(n, c)


def _seg_row(x, idx):
    n, c = x.shape
    x3 = x.reshape(n // HCH, HCH, c)
    return jnp.broadcast_to(x3[:, idx:idx + 1, :], x3.shape).reshape(n, c)


def _hgrn_gates(fl, lbv, row):
    s = _sigmoid(fl)
    f = lbv + (1.0 - lbv) * s
    a = _cumsum64(jnp.log(f), row)
    return s, f, a, _seg_row(a, HCH // 2 - 1), _seg_row(a, HCH - 1)


def _hgrn_fwd(proj, lb_table, norm_g, job=None):
    T = proj.shape[1]
    tb = min(HGRN_TOKENS, T)
    nc = tb // HCH

    def body(q_ref, fl_ref, v_ref, g_ref, lbt_ref, gn_ref, o_ref, ob_ref, stb_ref, st_s, o_s):
        @pl.when(pl.program_id(1) == 0)
        def _():
            st_s[...] = jnp.zeros_like(st_s)

        row = lax.broadcasted_iota(jnp.int32, (tb, HW), 0) & (HCH - 1)
        lbv = _sigmoid(lbt_ref[0:1, :] - lbt_ref[1:2, :])
        _, f, a, a_mid, a_last = _hgrn_gates(fl_ref[...], lbv, row)
        k = 1.0 - f
        qs = q_ref[...] * QSCALE
        q_in = (qs * jnp.exp(a - a_mid)).astype(BF16)
        k_in = (k * jnp.exp(a_mid - a)).astype(BF16)
        q_a = (qs * jnp.exp(a)).astype(BF16)
        k_d = (k * jnp.exp(a_last - a)).astype(BF16)
        dec = jnp.exp(a_last)
        vb = v_ref[...].astype(BF16)
        tri = (lax.broadcasted_iota(jnp.int32, (HCH, HCH), 0)
               >= lax.broadcasted_iota(jnp.int32, (HCH, HCH), 1))
        for c in range(nc):
            sl = slice(HCH * c, HCH * (c + 1))
            for hh in range(HGRN_HB):
                hs = slice(HD * hh, HD * (hh + 1))
                st = st_s[hh]
                stb_ref[hh, c] = st
                sc = jnp.where(tri, _mm_nt(q_in[sl, hs], k_in[sl, hs]), 0.0)
                o_s[sl, hs] = _mm(sc.astype(BF16), vb[sl, hs]) + _mm_nt(q_a[sl, hs], st.astype(BF16))
                d64 = dec[sl, hs]
                st_s[hh] = st * jnp.concatenate([d64, d64], axis=0) + _mm_tn(vb[sl, hs], k_d[sl, hs])
        o = o_s[...]
        r = lax.rsqrt(_head_mean(o * o) + EPS)
        g = g_ref[...]
        o_ref[...] = o
        ob_ref[...] = (o * r * gn_ref[...] * (g * _sigmoid(g))).astype(BF16)

    def col(off):
        return pl.BlockSpec((None, tb, HW), lambda h, cb: (off, cb, h))

    return _call(
        body, name="hgrn_fwd", grid=(NH // HGRN_HB, T // tb), job=job,
        args=(proj, proj, proj, proj, lb_table, norm_g),
        in_specs=[col(2), col(3), col(4), col(5),
                  pl.BlockSpec((2, HW), lambda h, cb: (0, h)), pl.BlockSpec((1, HW), lambda h, cb: (0, h))],
        out_specs=[pl.BlockSpec((tb, HW), lambda h, cb: (cb, h)), pl.BlockSpec((tb, HW), lambda h, cb: (cb, h)),
                   pl.BlockSpec((HGRN_HB, nc, HD, HD), lambda h, cb: (h, cb, 0, 0))],
        out_shape=[SDS((T, D), F32), SDS((T, D), BF16), SDS((NH, T // HCH, HD, HD), F32)],
        scratch_shapes=[pltpu.VMEM((HGRN_HB, HD, HD), F32), pltpu.VMEM((tb, HW), F32)])


def _merge_fwd(x, ab, ob, proj, w_a, w_b, w_out, job=None):
    T = x.shape[0]
    tm = min(512, T)

    def body(x_ref, ab_ref, ob_ref, ga_ref, gb_ref, wa_ref, wb_ref, wo_ref, mg_ref, x1_ref):
        ya = _mm(ab_ref[...], wa_ref[...])
        yb = _mm(ob_ref[...], wb_ref[...])
        merged = (_sigmoid(ga_ref[...]) * ya + _sigmoid(gb_ref[...]) * yb).astype(BF16)
        mg_ref[...] = merged
        x1_ref[...] = x_ref[...] + _mm(merged, wo_ref[...])

    t = lambda i: (i, 0)
    w = lambda i: (0, 0)
    return _call(
        body, name="merge_fwd", grid=(T // tm,), job=job, args=(x, ab, ob, proj, proj, w_a, w_b, w_out),
        in_specs=[pl.BlockSpec((tm, D), t), pl.BlockSpec((tm, D), t), pl.BlockSpec((tm, D), t),
                  pl.BlockSpec((None, tm, D), lambda i: (6, i, 0)), pl.BlockSpec((None, tm, D), lambda i: (7, i, 0)),
                  pl.BlockSpec((D, D), w), pl.BlockSpec((D, D), w), pl.BlockSpec((D, D), w)],
        out_specs=[pl.BlockSpec((tm, D), t)] * 2,
        out_shape=[SDS((T, D), BF16), SDS((T, D), F32)])


def _ffn_fwd_bwd(x1, target, g_ffn, g_fin, w_gu, w_down):
    T = x1.shape[0]
    tm = min(256, T)
    inv_d = 1.0 / D

    def body(x1_ref, tg_ref, gf_ref, gn_ref, wgu_ref, wd_ref,
             act_ref, dx2b_ref, h2b_ref, dgu_ref, dx1_ref, dx1b_ref, acc_ref):
        @pl.when(pl.program_id(0) == 0)
        def _():
            acc_ref[...] = jnp.zeros_like(acc_ref)

        x1v = x1_ref[...]
        gf = gf_ref[...]
        gn = gn_ref[...]
        rr1 = lax.rsqrt(_mean(x1v * x1v) + EPS)
        x1n = x1v * rr1
        h2b = (x1n * gf).astype(BF16)
        h2b_ref[...] = h2b
        gate = _mm(h2b, wgu_ref[0])
        up = _mm(h2b, wgu_ref[1])
        sg = _sigmoid(gate)
        si = gate * sg
        act = (si * up).astype(BF16)
        act_ref[...] = act
        x2 = x1v + _mm(act, wd_ref[...])
        rr2 = lax.rsqrt(_mean(x2 * x2) + EPS)
        x2n = x2 * rr2
        e = x2n * gn - tg_ref[...]
        acc_ref[0] += _rows8(e * e) * (0.5 * inv_d)
        dy = e * inv_d
        acc_ref[1] += _rows8(dy * x2n)
        dxn = dy * gn
        dx2 = rr2 * (dxn - x2n * _mean(dxn * x2n))
        dx2b = dx2.astype(BF16)
        dx2b_ref[...] = dx2b
        dact = _mm_nt(dx2b, wd_ref[...])
        dgate = (dact * up * (sg * (1.0 + gate * (1.0 - sg)))).astype(BF16)
        dup = (dact * si).astype(BF16)
        dgu_ref[0] = dgate
        dgu_ref[1] = dup
        dh2 = _mm_nt(dgate, wgu_ref[0]) + _mm_nt(dup, wgu_ref[1])
        acc_ref[2] += _rows8(dh2 * x1n)
        dxn1 = dh2 * gf
        dx1 = dx2 + rr1 * (dxn1 - x1n * _mean(dxn1 * x1n))
        dx1_ref[...] = dx1
        dx1b_ref[...] = dx1.astype(BF16)

    t = lambda i: (i, 0)
    w = lambda i: (0, 0)
    one = pl.Buffered(1)
    return pl.pallas_call(
        body, name="ffn_fwd_bwd", grid=(T // tm,),
        in_specs=[pl.BlockSpec((tm, D), t), pl.BlockSpec((tm, D), t),
                  pl.BlockSpec((1, D), w), pl.BlockSpec((1, D), w),
                  pl.BlockSpec((2, D, FF), lambda i: (0, 0, 0), pipeline_mode=one),
                  pl.BlockSpec((FF, D), w, pipeline_mode=one)],
        out_specs=[pl.BlockSpec((tm, FF), t), pl.BlockSpec((tm, D), t), pl.BlockSpec((tm, D), t),
                   pl.BlockSpec((2, tm, FF), lambda i: (0, i, 0)),
                   pl.BlockSpec((tm, D), t), pl.BlockSpec((tm, D), t),
                   pl.BlockSpec((3, 8, D), lambda i: (0, 0, 0))],
        out_shape=[SDS((T, FF), BF16), SDS((T, D), BF16), SDS((T, D), BF16),
                   SDS((2, T, FF), BF16), SDS((T, D), F32), SDS((T, D), BF16),
                   SDS((3, 8, D), F32)],
        compiler_params=_cparams(),
    )(x1, target, g_ffn, g_fin, w_gu, w_down)


def _merge_bwd(dx1b, ab, ob, proj, w_out, w_a, w_b, job=None):
    T = dx1b.shape[0]
    tm = min(512, T)

    def body(dx_ref, ab_ref, ob_ref, ga_ref, gb_ref, wo_ref, wa_ref, wb_ref, dya_ref, dyb_ref, dp_ref):
        dm = _mm_nt(dx_ref[...], wo_ref[...])
        sa = _sigmoid(ga_ref[...])
        sb = _sigmoid(gb_ref[...])
        dya_ref[...] = (dm * sa).astype(BF16)
        dyb_ref[...] = (dm * sb).astype(BF16)
        dp_ref[0] = (dm * _mm(ab_ref[...], wa_ref[...]) * sa * (1.0 - sa)).astype(BF16)
        dp_ref[1] = (dm * _mm(ob_ref[...], wb_ref[...]) * sb * (1.0 - sb)).astype(BF16)

    t = lambda i: (i, 0)
    w = lambda i: (0, 0)
    return _call(
        body, name="merge_bwd", grid=(T // tm,),
        in_specs=[pl.BlockSpec((tm, D), t), pl.BlockSpec((tm, D), t), pl.BlockSpec((tm, D), t),
                  pl.BlockSpec((None, tm, D), lambda i: (6, i, 0)), pl.BlockSpec((None, tm, D), lambda i: (7, i, 0)),
                  pl.BlockSpec((D, D), w), pl.BlockSpec((D, D), w), pl.BlockSpec((D, D), w)],
        out_specs=[pl.BlockSpec((tm, D), t)] * 2 + [pl.BlockSpec((2, tm, D), lambda i: (3, i, 0))],
        out_shape=[SDS((T, D), BF16), SDS((T, D), BF16), SDS((NIN, T, D), BF16)],
        args=(dx1b, ab, ob, proj, proj, w_out, w_a, w_b), job=job)


def _hgrn_bwd(dproj, dyb, w_b, o_raw, proj, st_before, lb_table, norm_g, job=None):
    T = dyb.shape[0]
    tb = min(HGRN_TOKENS, T)
    nc = tb // HCH
    nb = T // tb

    def body(dp_in, dyb_ref, wb_ref, o_ref, q_ref, fl_ref, v_ref, g_ref, stb_ref, lbt_ref, gn_ref,
             dp_ref, acc_ref, dst_s, dqin_s, dqa_s, dkin_s, dkd_s, dv_s, ddec_s):
        del dp_in

        @pl.when(pl.program_id(1) == 0)
        def _():
            dst_s[...] = jnp.zeros_like(dst_s)
            acc_ref[...] = jnp.zeros_like(acc_ref)

        row = lax.broadcasted_iota(jnp.int32, (tb, HW), 0) & (HCH - 1)
        gn = gn_ref[...]
        lbv = _sigmoid(lbt_ref[0:1, :] - lbt_ref[1:2, :])
        o = o_ref[...]
        r = lax.rsqrt(_head_mean(o * o) + EPS)
        on = o * r
        g = g_ref[...]
        sgm = _sigmoid(g)
        dob_v = _mm_nt(dyb_ref[...], wb_ref[...])
        dp_ref[3] = (dob_v * on * gn * (sgm * (1.0 + g * (1.0 - sgm)))).astype(BF16)
        do_n = dob_v * (g * sgm)
        acc_ref[1] += _rows8(do_n * on)
        dxn = do_n * gn
        do = (r * (dxn - on * _head_mean(dxn * on))).astype(BF16)
        s, f, a, a_mid, a_last = _hgrn_gates(fl_ref[...], lbv, row)
        k = 1.0 - f
        qs = q_ref[...] * QSCALE
        e_q = jnp.exp(a - a_mid)
        e_k = jnp.exp(a_mid - a)
        e_a = jnp.exp(a)
        e_l = jnp.exp(a_last - a)
        dec = jnp.exp(a_last)
        q_in = qs * e_q
        k_in = k * e_k
        q_a = qs * e_a
        k_d = k * e_l
        q_inb, k_inb, q_ab, k_db = (z.astype(BF16) for z in (q_in, k_in, q_a, k_d))
        vb = v_ref[...].astype(BF16)
        tri = (lax.broadcasted_iota(jnp.int32, (HCH, HCH), 0)
               >= lax.broadcasted_iota(jnp.int32, (HCH, HCH), 1))
        for c in reversed(range(nc)):
            sl = slice(HCH * c, HCH * (c + 1))
            for hh in range(HGRN_HB):
                hs = slice(HD * hh, HD * (hh + 1))
                stp = stb_ref[hh, c]
                dst = dst_s[hh]
                dstb = dst.astype(BF16)
                do_c = do[sl, hs]
                v_c = vb[sl, hs]
                dqa_s[sl, hs] = _mm(do_c, stp.astype(BF16))
                dkd_s[sl, hs] = _mm(v_c, dstb)
                ddec_s[sl, hs] = jnp.broadcast_to(jnp.sum(dst * stp, axis=0, keepdims=True), (HCH, HD))
                sc = jnp.where(tri, _mm_nt(q_inb[sl, hs], k_inb[sl, hs]), 0.0).astype(BF16)
                dsc = jnp.where(tri, _mm_nt(do_c, v_c), 0.0).astype(BF16)
                dv_s[sl, hs] = _mm_nt(k_db[sl, hs], dstb) + _mm_tn(sc, do_c)
                dqin_s[sl, hs] = _mm(dsc, k_inb[sl, hs])
                dkin_s[sl, hs] = _mm_tn(dsc, q_inb[sl, hs])
                d64 = dec[sl, hs]
                dst_s[hh] = dst * jnp.concatenate([d64, d64], axis=0) + _mm_tn(do_c, q_ab[sl, hs])
        dq_in = dqin_s[...]
        dq_a = dqa_s[...]
        dk_in = dkin_s[...]
        dk_d = dkd_s[...]
        dp_ref[0] = ((dq_in * e_q + dq_a * e_a) * QSCALE).astype(BF16)
        dp_ref[2] = dv_s[...].astype(BF16)
        tq = dq_in * q_in
        tk = dk_in * k_in
        td = dk_d * k_d
        d_a = tq + dq_a * q_a - tk - td
        d_a = d_a + jnp.where(row == HCH // 2 - 1, _seg_sum(tk - tq), 0.0)
        d_a = d_a + jnp.where(row == HCH - 1, _seg_sum(td) + ddec_s[...] * dec, 0.0)
        dlf = _revcumsum64(d_a, row)
        df = dlf / f - (dk_in * e_k + dk_d * e_l)
        dp_ref[1] = (df * (1.0 - lbv) * s * (1.0 - s)).astype(BF16)
        acc_ref[0] += _rows8(df * (1.0 - s))

    def col(off):
        return pl.BlockSpec((None, tb, HW), lambda h, cb: (off, nb - 1 - cb, h))

    hb = lambda h, cb: (nb - 1 - cb, h)
    return _call(
        body, name="hgrn_bwd", grid=(NH // HGRN_HB, nb), job=job,
        args=(dproj, dyb, w_b, o_raw, proj, proj, proj, proj, st_before, lb_table, norm_g),
        in_specs=[ANY, pl.BlockSpec((tb, D), lambda h, cb: (nb - 1 - cb, 0)),
                  pl.BlockSpec((HW, D), lambda h, cb: (h, 0)), pl.BlockSpec((tb, HW), hb),
                  col(2), col(3), col(4), col(5),
                  pl.BlockSpec((HGRN_HB, nc, HD, HD), lambda h, cb: (h, nb - 1 - cb, 0, 0)),
                  pl.BlockSpec((2, HW), lambda h, cb: (0, h)), pl.BlockSpec((1, HW), lambda h, cb: (0, h))],
        out_specs=[pl.BlockSpec((4, tb, HW), lambda h, cb: (0, nb - 1 - cb, h)),
                   pl.BlockSpec((2, 8, HW), lambda h, cb: (0, 0, h))],
        out_shape=[SDS(dproj.shape, BF16), SDS((2, 8, D), F32)],
        scratch_shapes=[pltpu.VMEM((HGRN_HB, HD, HD), F32)] + [pltpu.VMEM((tb, HW), F32)] * 6,
        aliases={0: 0})


def _gmlp_bwd(dproj, dya, w_a, proj, ln_g, ln_b, wm, wm_t, b_t):
    T = dya.shape[0]
    tm = min(GMLP_BWD_TOKENS, T)

    def body(dp_in, dya_ref, wa_ref, u_ref, v_ref, lg_ref, lb_ref, wm_ref, wmt_ref, bt_ref,
             dp_ref, acc_ref, dws_ref, dmix_ref, du_s, dvn_s):
        del dp_in

        @pl.when(pl.program_id(0) == 0)
        def _():
            acc_ref[...] = jnp.zeros_like(acc_ref)
            dws_ref[...] = jnp.zeros_like(dws_ref)
            dmix_ref[...] = jnp.zeros_like(dmix_ref)

        u = u_ref[...]
        v = v_ref[...]
        lg = lg_ref[...]
        gu, t_u = _gelu(u)
        gv, t_v = _gelu(v)
        vhat, rs = _layer_norm_stats(gv)
        vnb = (vhat * lg + lb_ref[...]).astype(BF16)
        da_v = _mm_nt(dya_ref[...], wa_ref[...])
        for g in range(NG):
            cols = slice(128 * g, 128 * (g + 1))
            vng = _chunks_abreast(vnb[:, cols])
            mixed = _mm(wm_ref[g], vng) + bt_ref[:, g:g + 1]
            dag = _chunks_abreast(da_v[:, cols])
            dmx = dag * _chunks_abreast(gu[:, cols])
            du_s[:, cols] = _chunks_stacked(dag * mixed)
            dmxb = dmx.astype(BF16)
            dws_ref[:, cols] += _mm_nt(dmxb, vng)
            dmix_ref[:, cols] += sum(dmx[:, GCH * ch:GCH * (ch + 1)] for ch in range(tm // GCH))
            dvn_s[:, cols] = _chunks_stacked(_mm(wmt_ref[g], dmxb))
        dp_ref[0] = (du_s[...] * _gelu_grad(u, t_u)).astype(BF16)
        dvn = dvn_s[...]
        acc_ref[0] += _rows8(dvn * vhat)
        acc_ref[1] += _rows8(dvn)
        dvh = dvn * lg
        dgv = rs * (dvh - _mean(dvh) - vhat * _mean(dvh * vhat))
        dp_ref[1] = (dgv * _gelu_grad(v, t_v)).astype(BF16)

    row = lambda i: (0, 0)
    w3 = lambda i: (0, 0, 0)
    return pl.pallas_call(
        body, name="gmlp_bwd", grid=(T // tm,),
        in_specs=[ANY, pl.BlockSpec((tm, D), lambda i: (i, 0)), pl.BlockSpec((D, D), row),
                  pl.BlockSpec((None, tm, D), lambda i: (0, i, 0)), pl.BlockSpec((None, tm, D), lambda i: (1, i, 0)),
                  pl.BlockSpec((1, D), row), pl.BlockSpec((1, D), row),
                  pl.BlockSpec((NG, GCH, GCH), w3), pl.BlockSpec((NG, GCH, GCH), w3),
                  pl.BlockSpec((GCH, NG), row)],
        out_specs=[pl.BlockSpec((2, tm, D), lambda i: (2, i, 0)),
                   pl.BlockSpec((2, 8, D), w3), pl.BlockSpec((GCH, D), row), pl.BlockSpec((GCH, D), row)],
        out_shape=[SDS(dproj.shape, BF16), SDS((2, 8, D), F32), SDS((GCH, D), F32), SDS((GCH, D), F32)],
        scratch_shapes=[pltpu.VMEM((tm, D), F32), pltpu.VMEM((tm, D), F32)],
        input_output_aliases={0: 0},
        compiler_params=_cparams(),
    )(dproj, dya, w_a, proj, proj, ln_g, ln_b, wm, wm_t, b_t)


def _proj_bwd(dproj, w_in4, x, dx1, g_mix, job=None):
    T = x.shape[0]
    tm = min(256, T)
    order = (2, 3, 4, 5, 0, 1, 6, 7)

    def body(dp_ref, w_ref, x_ref, dx1_ref, g_ref, gx_ref, acc_ref):
        @pl.when(pl.program_id(0) == 0)
        def _():
            acc_ref[...] = jnp.zeros_like(acc_ref)

        dh = None
        for m, og in enumerate(order):
            part = _mm_nt(dp_ref[m], w_ref[og // 2, :, D * (og % 2):D * (og % 2 + 1)])
            dh = part if dh is None else dh + part
        xv = x_ref[...]
        r = lax.rsqrt(_mean(xv * xv) + EPS)
        xn = xv * r
        acc_ref[...] += _rows8(dh * xn)
        dxn = dh * g_ref[...]
        gx_ref[...] = dx1_ref[...] + r * (dxn - xn * _mean(dxn * xn))

    t = lambda i: (i, 0)
    return _call(
        body, name="proj_bwd", grid=(T // tm,),
        in_specs=[pl.BlockSpec((NIN, tm, D), lambda i: (0, i, 0)),
                  pl.BlockSpec((NCHIP, D, 2 * D), lambda i: (0, 0, 0), pipeline_mode=pl.Buffered(1)),
                  pl.BlockSpec((tm, D), t), pl.BlockSpec((tm, D), t), pl.BlockSpec((1, D), lambda i: (0, 0))],
        out_specs=[pl.BlockSpec((tm, D), t), pl.BlockSpec((8, D), lambda i: (0, 0))],
        out_shape=[SDS((T, D), F32), SDS((8, D), F32)],
        args=(dproj, w_in4, x, dx1, g_mix), job=job)


def _dw_call(name, a, b, a_spec, b_spec, o_spec, out_shape, nblk, tt, job=None, prefetch=None):
    T = a.shape[-2]

    def body(*refs):
        a_ref, b_ref, o_ref = refs[-3:]

        @pl.when(pl.program_id(1) == 0)
        def _():
            o_ref[...] = jnp.zeros_like(o_ref)
        o_ref[...] += _mm_tn(a_ref[...], b_ref[...])

    (out,), job_out = _call(
        body, name=name, grid=(nblk, T // tt), in_specs=[a_spec, b_spec], out_specs=[o_spec],
        out_shape=[out_shape], args=(a, b), job=job, prefetch=prefetch)
    return out, job_out


def _dw_in_half(name, place, hb, dproj, mine, job=None):
    tt = min(DW_IN_TOKENS, hb.shape[0])

    def comp(k, pc):
        return _component_of(2 * k + (pc[1] if mine else 1 - pc[1]))

    return _dw_call(
        name, hb, dproj,
        pl.BlockSpec((tt, D), lambda k, t, pc: (t, 0)),
        pl.BlockSpec((None, tt, D), lambda k, t, pc: (comp(k, pc), t, 0)),
        pl.BlockSpec((None, D, D), lambda k, t, pc: (k, 0, 0)),
        SDS((NCHIP, D, D), F32), NCHIP, tt, job, place)


def _dw_gate_up(h2b, dgu, job=None):
    tt = min(DW_TOKENS, h2b.shape[0])
    return _dw_call(
        "dw_gate_up", h2b, dgu,
        pl.BlockSpec((tt, D), lambda k, t: (t, 0)),
        pl.BlockSpec((None, tt, FFS), lambda k, t: (k // 2, t, k % 2)),
        pl.BlockSpec((None, D, FFS), lambda k, t: (k, 0, 0)),
        SDS((NCHIP, D, FFS), F32), NCHIP, tt, job)


def _dw_down(act, dx2b, job=None):
    tt = min(DW_TOKENS, act.shape[0])
    g, job_out = _dw_call(
        "dw_down", act, dx2b,
        pl.BlockSpec((tt, FFS), lambda k, t: (t, k)),
        pl.BlockSpec((tt, D), lambda k, t: (t, 0)),
        pl.BlockSpec((FFS, D), lambda k, t: (k, 0)),
        SDS((FF, D), F32), 2, tt, job)
    return g.reshape(NCHIP, FF // NCHIP, D), job_out


def _dw_square(name, a, b, job=None):
    tt = min(DW_TOKENS, a.shape[0])
    g, job_out = _dw_call(
        name, a, b,
        pl.BlockSpec((tt, D), lambda k, t: (t, 0)), pl.BlockSpec((tt, D), lambda k, t: (t, 0)),
        pl.BlockSpec((D, D), lambda k, t: (0, 0)), SDS((D, D), F32), 1, tt, job)
    return g.reshape(NCHIP, D // NCHIP, D), job_out


def _place():
    x, y, c = lax.axis_index("x"), lax.axis_index("y"), lax.axis_index("c")
    return x, y, c, 2 * x + y


def _chip_at(x, y, s):
    return x ^ (s >> 1), y ^ (s & 1)


class _Job:
    def __init__(self, ins, out_shapes, sems, start, finish, aliases=None, mid=None):
        self.ins, self.out_shapes, self.sems = list(ins), list(out_shapes), list(sems)
        self.start, self.finish, self.aliases = start, finish, dict(aliases or {})
        self.mid = mid if mid is not None else (lambda ins, outs, sems: None)


def _join_jobs(*jobs):
    def cut(refs, sizes):
        out, at = [], 0
        for n in sizes:
            out.append(refs[at:at + n])
            at += n
        return out

    ni = [len(j.ins) for j in jobs]
    no = [len(j.out_shapes) for j in jobs]
    ns = [len(j.sems) for j in jobs]

    def run(which):
        def go(ins, outs, sems):
            for j, a, b, c in zip(jobs, cut(ins, ni), cut(outs, no), cut(sems, ns)):
                getattr(j, which)(a, b, c)
        return go

    aliases = {}
    for k, j in enumerate(jobs):
        for a, b in j.aliases.items():
            aliases[sum(ni[:k]) + a] = sum(no[:k]) + b
    return _Job([a for j in jobs for a in j.ins], [o for j in jobs for o in j.out_shapes],
                [s for j in jobs for s in j.sems], run("start"), run("finish"), aliases, run("mid"))


def _call(body, *, name, grid, in_specs, out_specs, out_shape, args, scratch_shapes=(), aliases=None,
          job=None, prefetch=None):
    n_in, n_out, n_scr = len(in_specs), len(out_specs), len(scratch_shapes)
    npf = 0 if prefetch is None else 1
    job = job if job is not None else _Job([], [], [], lambda *a: None, lambda *a: None)
    ji, jo = len(job.ins), len(job.out_shapes)
    steps = math.prod(grid)

    def wrapped(*refs):
        pf, refs = refs[:npf], refs[npf:]
        ins, jin = refs[:n_in], refs[n_in:n_in + ji]
        o0 = n_in + ji
        outs, jout = refs[o0:o0 + n_out], refs[o0 + n_out:o0 + n_out + jo]
        s0 = o0 + n_out + jo
        scr, jsem = refs[s0:s0 + n_scr], refs[s0 + n_scr:]
        step = functools.reduce(lambda acc, ag: acc * ag[1] + pl.program_id(ag[0]), enumerate(grid), 0)
        if ji or jo:
            @pl.when(step == 0)
            def _():
                job.start(jin, jout, jsem)

        body(*pf, *ins, *outs, *scr)

        if ji or jo:
            @pl.when(step == steps // 2)
            def _():
                job.mid(jin, jout, jsem)

            @pl.when(step == steps - 1)
            def _():
                job.finish(jin, jout, jsem)

    io = {npf + a: b for a, b in dict(aliases or {}).items()}
    io.update({npf + n_in + a: n_out + b for a, b in job.aliases.items()})
    kw = dict(in_specs=list(in_specs) + [ANY] * ji, out_specs=list(out_specs) + [ANY] * jo,
              scratch_shapes=list(scratch_shapes) + job.sems)
    if npf:
        kw = dict(grid_spec=pltpu.PrefetchScalarGridSpec(num_scalar_prefetch=1, grid=grid, **kw))
    else:
        kw["grid"] = grid
    res = pl.pallas_call(
        wrapped, name=name, out_shape=list(out_shape) + job.out_shapes, input_output_aliases=io,
        compiler_params=_cparams(has_side_effects=bool(ji or jo)), **kw,
    )(*(() if prefetch is None else (prefetch,)), *args, *job.ins)
    return list(res[:n_out]), list(res[n_out:])


def _cast_shards(name, place, ws, paired=False):
    n = len(ws)
    rows, cols = ws[0].shape
    tr = 352 if rows % 352 == 0 else 256
    shape = (2, rows, 2 * cols) if paired else (NCHIP, rows, cols)
    mine = (lambda i, pc: (pc[0] // 2, i, pc[0] % 2)) if paired else (lambda i, pc: (pc[0], i, 0))

    def body(pc_ref, *refs):
        del pc_ref
        for w_ref, o_ref in zip(refs[:n], refs[n:]):
            o_ref[...] = w_ref[...].astype(BF16)

    return pl.pallas_call(
        body, name=name,
        grid_spec=pltpu.PrefetchScalarGridSpec(
            num_scalar_prefetch=1, grid=(rows // tr,),
            in_specs=[pl.BlockSpec((tr, cols), lambda i, pc: (i, 0))] * n,
            out_specs=[pl.BlockSpec((None, tr, cols), mine)] * n),
        out_shape=[SDS(shape, BF16)] * n,
        compiler_params=_cparams(),
    )(place, *ws)


def _sibling_copy(ref, send_sem, recv_sem):
    x, y, c, _ = _place()
    return pltpu.make_async_remote_copy(src_ref=ref, dst_ref=ref, send_sem=send_sem, recv_sem=recv_sem,
                                        device_id=(x, y, 1 - c), device_id_type=MESH)


def _slot(arr, chip):
    if arr.shape[0] == NCHIP:
        return arr.at[chip]
    cols = arr.shape[2] // 2
    return arr.at[chip // 2, :, pl.ds(pl.multiple_of((chip % 2) * cols, 128), cols)]


def _half_rows(arr, slot, core):
    half = arr.shape[1] // 2
    return _slot(arr, slot).at[pl.ds(pl.multiple_of(core * half, 16), half)]


def _quarter_rows(arr, slot, core, q):
    quarter = arr.shape[1] // 4
    return _slot(arr, slot).at[pl.ds(pl.multiple_of((2 * core + q) * quarter, 16), quarter)]


def _chip_copy(ref, dist, send_sem, recv_sem):
    x, y, c, _ = _place()
    cx, cy = _chip_at(x, y, dist)
    return pltpu.make_async_remote_copy(src_ref=ref, dst_ref=ref, send_sem=send_sem, recv_sem=recv_sem,
                                        device_id=(cx, cy, c), device_id_type=MESH)


def _gather_sems(n):
    dma = pltpu.SemaphoreType.DMA
    return [dma((n, 2))] * 4 + [dma((n, 4))] * 2


def _gather_start(arrs, sems):
    dsend, drecv = sems[0], sems[1]
    _, _, c, j = _place()
    for w, arr in enumerate(arrs):
        for dist in (1, 2):
            _chip_copy(_half_rows(arr, j, c), dist, dsend.at[w, dist - 1], drecv.at[w, dist - 1]).start()


def _gather_land(arrs, sems, dist, first=0):
    dsend, drecv, rsend, rrecv, fsend, frecv = sems
    _, _, c, j = _place()
    if dist < 3:
        other = 3 - dist
        for w, arr in enumerate(arrs, first):
            landed = _half_rows(arr, j ^ dist, c)
            _chip_copy(landed, dist, dsend.at[w, dist - 1], drecv.at[w, dist - 1]).wait_recv()
            relay = _quarter_rows(arr, j ^ dist, c, other - 1)
            _chip_copy(relay, other, rsend.at[w, other - 1], rrecv.at[w, other - 1]).start()
            _sibling_copy(landed, fsend.at[w, dist - 1], frecv.at[w, dist - 1]).start()
        for w, arr in enumerate(arrs, first):
            theirs = _half_rows(arr, j ^ dist, 1 - c)
            _sibling_copy(theirs, fsend.at[w, dist - 1], frecv.at[w, dist - 1]).wait_recv()
    else:
        for w, arr in enumerate(arrs, first):
            for via in (1, 2):
                piece = _quarter_rows(arr, j ^ 3, c, via - 1)
                _chip_copy(piece, via, rsend.at[w, via - 1], rrecv.at[w, via - 1]).wait_recv()
                _sibling_copy(piece, fsend.at[w, 1 + via], frecv.at[w, 1 + via]).start()
        for w, arr in enumerate(arrs, first):
            for via in (1, 2):
                theirs = _quarter_rows(arr, j ^ 3, 1 - c, via - 1)
                _sibling_copy(theirs, fsend.at[w, 1 + via], frecv.at[w, 1 + via]).wait_recv()


def _gather_drain(arrs, sems):
    dsend, drecv, rsend, rrecv, fsend, frecv = sems
    _, _, c, j = _place()
    for w, arr in enumerate(arrs):
        for dist in (1, 2):
            other = 3 - dist
            _chip_copy(_half_rows(arr, j, c), dist, dsend.at[w, dist - 1], drecv.at[w, dist - 1]).wait_send()
            _chip_copy(_quarter_rows(arr, j ^ dist, c, other - 1), other,
                       rsend.at[w, other - 1], rrecv.at[w, other - 1]).wait_send()
            _sibling_copy(_half_rows(arr, j ^ dist, c), fsend.at[w, dist - 1], frecv.at[w, dist - 1]).wait_send()
            _sibling_copy(_quarter_rows(arr, j ^ 3, c, dist - 1),
                          fsend.at[w, 1 + dist], frecv.at[w, 1 + dist]).wait_send()


def _gather_neighbours(arrs, sems):
    _gather_land(arrs, sems, 1)
    _gather_land(arrs, sems, 2)


def _gather_finish(arrs, sems):
    _gather_land(arrs, sems, 3)
    _gather_drain(arrs, sems)


def _gather_job(arrs):
    n = len(arrs)
    return _Job(arrs, [SDS(a.shape, a.dtype) for a in arrs], _gather_sems(n),
                lambda ins, outs, sems: _gather_start(outs, sems),
                lambda ins, outs, sems: _gather_finish(outs, sems), {k: k for k in range(n)},
                mid=lambda ins, outs, sems: _gather_neighbours(outs, sems))


def _exchange_job(arrs, out_shapes, n, copies):
    def start(ins, outs, sems):
        for cp in copies(ins, outs, sems[0], sems[1]):
            cp.start()

    def finish(ins, outs, sems):
        for cp in copies(ins, outs, sems[0], sems[1]):
            cp.wait()

    return _Job(arrs, out_shapes, [pltpu.SemaphoreType.DMA((n,))] * 2, start, finish)


def _pair_exchange_job(grads):
    def copies(ins, outs, send_sem, recv_sem):
        x, y, c, _ = _place()
        res = []
        for w in range(len(grads)):
            half = ins[w].shape[1] // 2
            theirs = pl.ds(pl.multiple_of((1 - c) * half, 8), half)
            res.append(pltpu.make_async_remote_copy(
                src_ref=ins[w].at[:, theirs, :], dst_ref=outs[w], send_sem=send_sem.at[w],
                recv_sem=recv_sem.at[w], device_id=(x, y, 1 - c), device_id_type=MESH))
        return res

    return _exchange_job(grads, [SDS((NCHIP, g.shape[1] // 2, g.shape[2]), F32) for g in grads],
                         len(grads), copies)


def _row_tile(rows, cols):
    tr = rows
    while tr * cols * 4 > ELEMENTWISE_BLOCK_BYTES and tr % 32 == 0:
        tr //= 2
    return tr


def _pair_sums(name, place, gs, sibs):
    n = len(gs)
    half, cols = sibs[0].shape[1], sibs[0].shape[2]
    tr = _row_tile(half, cols)
    nt = half // tr
    mine = nt if gs[0].shape[1] == 2 * half else 0

    def body(pc_ref, *refs):
        del pc_ref
        for g_ref, s_ref, own_ref, out_ref in zip(refs[:n], refs[n:2 * n], refs[2 * n:3 * n], refs[3 * n:]):
            v = g_ref[...] + s_ref[...]

            @pl.when(pl.program_id(1) == 0)
            def _():
                own_ref[...] = v

            @pl.when(pl.program_id(1) > 0)
            def _():
                out_ref[...] = v.astype(BF16)

    res = pl.pallas_call(
        body, name=name,
        grid_spec=pltpu.PrefetchScalarGridSpec(
            num_scalar_prefetch=1, grid=(nt, NCHIP),
            in_specs=[pl.BlockSpec((None, tr, cols), lambda i, s, pc: (pc[0] ^ s, pc[1] * mine + i, 0))] * n
            + [pl.BlockSpec((None, tr, cols), lambda i, s, pc: (pc[0] ^ s, i, 0))] * n,
            out_specs=[pl.BlockSpec((tr, cols), lambda i, s, pc: (i, 0))] * n
            + [pl.BlockSpec((None, tr, cols), lambda i, s, pc: (jnp.maximum(s - 1, 0), i, 0))] * n),
        out_shape=[SDS((half, cols), F32)] * n + [SDS((NCHIP - 1, half, cols), BF16)] * n,
        compiler_params=_cparams(),
    )(place, *gs, *sibs)
    return res[:n], res[n:]


def _chip_exchange_job(parts):
    def copies(ins, outs, send_sem, recv_sem):
        x, y, c, _ = _place()
        res = []
        for w in range(len(parts)):
            for s in range(1, NCHIP):
                cx, cy = _chip_at(x, y, s)
                k = w * (NCHIP - 1) + s - 1
                res.append(pltpu.make_async_remote_copy(
                    src_ref=ins[w].at[s - 1], dst_ref=outs[w].at[s - 1], send_sem=send_sem.at[k],
                    recv_sem=recv_sem.at[k], device_id=(cx, cy, c), device_id_type=MESH))
        return res

    return _exchange_job(parts, [SDS((NCHIP - 1,) + p.shape[1:], BF16) for p in parts],
                         len(parts) * (NCHIP - 1), copies)


def _chip_sums(name, owns, rems):
    n = len(owns)
    half, cols = owns[0].shape
    tr = _row_tile(half, cols)

    def body(*refs):
        for own_ref, rem_ref, out_ref in zip(refs[:n], refs[n:2 * n], refs[2 * n:]):
            out_ref[...] = (((own_ref[...] + rem_ref[0].astype(F32)) + rem_ref[1].astype(F32))
                            + rem_ref[2].astype(F32))

    return pl.pallas_call(
        body, name=name, grid=(half // tr,),
        in_specs=[pl.BlockSpec((tr, cols), lambda i: (i, 0))] * n
        + [pl.BlockSpec((NCHIP - 1, tr, cols), lambda i: (0, i, 0))] * n,
        out_specs=[pl.BlockSpec((tr, cols), lambda i: (i, 0))] * n,
        out_shape=[SDS((half, cols), F32)] * n,
        compiler_params=_cparams(),
    )(*owns, *rems)


def _share_halves_job(halves):
    def copies(ins, outs, send_sem, recv_sem):
        x, y, c, _ = _place()
        return [pltpu.make_async_remote_copy(
            src_ref=ins[w], dst_ref=outs[w], send_sem=send_sem.at[w], recv_sem=recv_sem.at[w],
            device_id=(x, y, 1 - c), device_id_type=MESH) for w in range(len(halves))]

    return _exchange_job(halves, [SDS(h.shape, F32) for h in halves], len(halves), copies)


def _adamw_math(w, g, m, v):
    m = B1 * m + (1.0 - B1) * g
    v = B2 * v + (1.0 - B2) * (g * g)
    m_hat = m / (1.0 - B1 ** STEP)
    v_hat = v / (1.0 - B2 ** STEP)
    delta = -LR * (m_hat / (jnp.sqrt(v_hat) + AEPS) + WD * w)
    return delta, m, v


def _adamws(name, place, ws, owns, sibs, ms, vs):
    n = len(ws)
    rows, cols = ws[0].shape
    by_cols = owns[0].shape[0] == rows
    half, pc_cols = (rows, cols // 2) if by_cols else (rows // 2, cols)
    tr = _row_tile(half, pc_cols)
    nt = half // tr

    def body(pc_ref, *refs):
        ins, outs = refs[:5 * n], refs[5 * n:]
        for k in range(n):
            w_ref, own_ref, sib_ref, m_ref, v_ref = ins[5 * k:5 * k + 5]
            g = jnp.where(pl.program_id(0) == pc_ref[1], own_ref[...], sib_ref[...])
            d, mn, vn = _adamw_math(w_ref[...], g, m_ref[...], v_ref[...])
            for ref, val in zip(outs[4 * k:4 * k + 4], (g, d, mn, vn)):
                ref[...] = val

    full = pl.BlockSpec((tr, pc_cols), (lambda h, i, pc: (i, h)) if by_cols else (lambda h, i, pc: (h * nt + i, 0)))
    part = pl.BlockSpec((tr, pc_cols), lambda h, i, pc: (i, 0))
    res = pl.pallas_call(
        body, name=name,
        grid_spec=pltpu.PrefetchScalarGridSpec(
            num_scalar_prefetch=1, grid=(2, nt),
            in_specs=[full, part, part, full, full] * n, out_specs=[full] * (4 * n)),
        out_shape=[SDS((rows, cols), F32)] * (4 * n),
        compiler_params=_cparams(),
    )(place, *[a for group in zip(ws, owns, sibs, ms, vs) for a in group])
    return [tuple(res[4 * k:4 * k + 4]) for k in range(n)]


ON_SPARSECORE = ("w_gate_up", "w_down")
SC_TILES = 32
SC_LANES = 16


def _adamw_sparsecore(name, shards):
    n = len(shards)

    def body(*refs):
        tile = lax.axis_index("sc_tile") * 2 + lax.axis_index("sc_core")
        c = lax.axis_index("c")
        for s in range(n):
            one_shard(tile, c, shards[s][0].shape, refs[5 * s:5 * s + 5],
                      refs[5 * n + 4 * s:5 * n + 4 * s + 4], refs[9 * n + 5 * s:9 * n + 5 * s + 5])

    def one_shard(tile, c, shape, ins, outs, bufs):
        rows, cols = shape
        groups, half_groups = rows // 8, rows // 16
        w_hbm, own_hbm, sib_hbm, m_hbm, v_hbm = ins
        g_out, d_out, mo_out, vo_out = outs
        wb, gb, mb, vb, db = bufs
        for k in range(-(-groups // SC_TILES)):
            grp = tile + SC_TILES * k

            @pl.when(grp < groups)
            def _():
                rws = pl.ds(pl.multiple_of(grp * 8, 8), 8)
                in_half = pl.ds(pl.multiple_of((grp % half_groups) * 8, 8), 8)
                mine = (grp // half_groups) == c

                @pl.when(mine)
                def _():
                    pltpu.sync_copy(own_hbm.at[in_half], gb)

                @pl.when(jnp.logical_not(mine))
                def _():
                    pltpu.sync_copy(sib_hbm.at[in_half], gb)

                pltpu.sync_copy(w_hbm.at[rws], wb)
                pltpu.sync_copy(m_hbm.at[rws], mb)
                pltpu.sync_copy(v_hbm.at[rws], vb)

                @pl.loop(0, cols, step=SC_LANES)
                def _(j):
                    for r in range(8):
                        at = (r, pl.ds(j, SC_LANES))
                        d, mn, vn = _adamw_math(wb[at], gb[at], mb[at], vb[at])
                        db[at] = d
                        mb[at] = mn
                        vb[at] = vn

                pltpu.sync_copy(gb, g_out.at[rws])
                pltpu.sync_copy(db, d_out.at[rws])
                pltpu.sync_copy(mb, mo_out.at[rws])
                pltpu.sync_copy(vb, vo_out.at[rws])

    res = pl.kernel(
        body, name=name, out_type=[SDS(sh[0].shape, F32) for sh in shards for _ in range(4)],
        mesh=plsc.VectorSubcoreMesh(core_axis_name="sc_core", subcore_axis_name="sc_tile"),
        scratch_types=[pltpu.VMEM((8, sh[0].shape[1]), F32) for sh in shards for _ in range(5)],
    )(*[a for sh in shards for a in sh])
    return [tuple(res[4 * s:4 * s + 4]) for s in range(n)]


def _small_allreduce_adamw(sp, wmv, job):
    shape = sp.shape
    ji, jo = len(job.ins), len(job.out_shapes)

    def body(sp_ref, wmv_ref, *rest):
        jin, (g_ref, d_ref, mo_ref, vo_ref), jout = rest[:ji], rest[ji:ji + 4], rest[ji + 4:ji + 4 + jo]
        sib_s, pair_s, chip_s, send_sem, recv_sem = rest[ji + 4 + jo:ji + 9 + jo]
        jsem = rest[ji + 9 + jo:]
        job.start(jin, jout, jsem)
        x, y, c, j = _place()
        cp = pltpu.make_async_remote_copy(
            src_ref=sp_ref, dst_ref=sib_s, send_sem=send_sem.at[0], recv_sem=recv_sem.at[0],
            device_id=(x, y, 1 - c), device_id_type=MESH)
        cp.start()
        cp.wait()
        pair_s[...] = sp_ref[...] + sib_s[...]
        half = shape[0] // 2
        mine = pl.ds(pl.multiple_of(c * half, 8), half)
        cps = []
        for s in range(1, NCHIP):
            cx, cy = _chip_at(x, y, s)
            cp = pltpu.make_async_remote_copy(
                src_ref=pair_s.at[mine], dst_ref=chip_s.at[s, mine], send_sem=send_sem.at[s],
                recv_sem=recv_sem.at[s], device_id=(cx, cy, c), device_id_type=MESH)
            cp.start()
            cps.append(cp)
        chip_s[0] = pair_s[...]
        for cp in cps:
            cp.wait()
        cps = []
        for s in range(1, NCHIP):
            cp = pltpu.make_async_remote_copy(
                src_ref=chip_s.at[s, mine], dst_ref=chip_s.at[s, mine], send_sem=send_sem.at[NCHIP + s],
                recv_sem=recv_sem.at[NCHIP + s], device_id=(x, y, 1 - c), device_id_type=MESH)
            cp.start()
            cps.append(cp)
        for cp in cps:
            cp.wait()
        tot = chip_s[j]
        for k in range(1, NCHIP):
            tot = tot + chip_s[k ^ j]
        g_ref[...] = tot
        d, mn, vn = _adamw_math(wmv_ref[0], tot, wmv_ref[1], wmv_ref[2])
        d_ref[...] = d
        mo_ref[...] = mn
        vo_ref[...] = vn
        job.mid(jin, jout, jsem)
        job.finish(jin, jout, jsem)

    vm = pl.BlockSpec(memory_space=pltpu.VMEM)
    res = pl.pallas_call(
        body, name="small_allreduce_adamw",
        in_specs=[vm] * 2 + [ANY] * ji, out_specs=[vm] * 4 + [ANY] * jo,
        out_shape=[SDS(shape, F32)] * 4 + job.out_shapes,
        scratch_shapes=[pltpu.VMEM(shape, F32), pltpu.VMEM(shape, F32), pltpu.VMEM((NCHIP,) + shape, F32),
                        pltpu.SemaphoreType.DMA((2 * NCHIP,)), pltpu.SemaphoreType.DMA((2 * NCHIP,))] + job.sems,
        input_output_aliases={2 + a: 4 + b for a, b in job.aliases.items()},
        compiler_params=pltpu.CompilerParams(has_side_effects=True),
    )(sp, wmv, *job.ins)
    return res[:4], res[4:]


def _pack_small(first, mix, ln_g, ln_b, b_s, lbt, hn, ffn, fin, w_s):
    rows = [first.reshape(1, D), mix.reshape(1, D), ln_g.reshape(1, D), ln_b.reshape(1, D),
            b_s.reshape(1, D), lbt.reshape(2, D), hn.reshape(1, D), ffn.reshape(1, D), fin.reshape(1, D),
            jnp.zeros((6, D), F32)]
    return jnp.concatenate(rows + [w_s.reshape(NG, GCH, GCH).transpose(1, 0, 2).reshape(GCH, D)], axis=0)


def _unpack_small(p):
    w_s = p[16:].reshape(GCH, NG, GCH).transpose(1, 0, 2).reshape(1, NG, GCH, GCH)
    return dict(norm_mix_g=p[1:2], gmlp_ln_g=p[2:3], gmlp_ln_b=p[3:4], gmlp_b_s=p[4].reshape(1, NG, GCH),
                hgrn_lb_table=p[5:7], hgrn_norm_g=p[7:8], norm_ffn_g=p[8:9], norm_final_g=p[9],
                gmlp_w_s=w_s)


SMALL = ("norm_mix_g", "gmlp_ln_g", "gmlp_ln_b", "gmlp_w_s", "gmlp_b_s", "hgrn_lb_table", "hgrn_norm_g",
         "norm_ffn_g", "norm_final_g")
BIG = ("w_in", "w_gate_up", "w_branch_a", "w_branch_b", "w_out", "w_down")
ORDER = ("norm_mix_g", "w_in", "gmlp_ln_g", "gmlp_ln_b", "gmlp_w_s", "gmlp_b_s", "hgrn_lb_table",
         "hgrn_norm_g", "w_branch_a", "w_branch_b", "w_out", "norm_ffn_g", "w_gate_up", "w_down",
         "norm_final_g")


def kernel(x, norm_mix_g, w_in, gmlp_ln_g, gmlp_ln_b, gmlp_w_s, gmlp_b_s, hgrn_lb_table, hgrn_norm_g, w_branch_a, w_branch_b, w_out, norm_ffn_g, w_gate_up, w_down, norm_final_g, loss_target, m_norm_mix_g, m_w_in, m_gmlp_ln_g, m_gmlp_ln_b, m_gmlp_w_s, m_gmlp_b_s, m_hgrn_lb_table, m_hgrn_norm_g, m_w_branch_a, m_w_branch_b, m_w_out, m_norm_ffn_g, m_w_gate_up, m_w_down, m_norm_final_g, v_norm_mix_g, v_w_in, v_gmlp_ln_g, v_gmlp_ln_b, v_gmlp_w_s, v_gmlp_b_s, v_hgrn_lb_table, v_hgrn_norm_g, v_w_branch_a, v_w_branch_b, v_w_out, v_norm_ffn_g, v_w_gate_up, v_w_down, v_norm_final_g):
    args = dict(locals())
    T = x.shape[1]
    xs = x.reshape(T, D)
    target = loss_target.reshape(T, D)
    big = {n: args[n].reshape(args[n].shape[1:]) for n in BIG}
    big_m = {n: args["m_" + n].reshape(args[n].shape[1:]) for n in BIG}
    big_v = {n: args["v_" + n].reshape(args[n].shape[1:]) for n in BIG}

    x_i, y_i, c_i = lax.axis_index("x"), lax.axis_index("y"), lax.axis_index("c")
    place = jnp.stack([2 * x_i + y_i, c_i]).astype(jnp.int32)
    def by_shape(names):
        groups = []
        for n in names:
            if groups and big[groups[-1][0]].shape == big[n].shape:
                groups[-1].append(n)
            else:
                groups.append([n])
        return groups

    cast = {}
    for grp in by_shape(BIG):
        cast.update(zip(grp, _cast_shards("cast_" + grp[0], place, [big[n] for n in grp],
                                          paired=grp[0] == "w_gate_up")))
    tril = jnp.tril(jnp.ones((GCH, GCH), bool))
    wm = jnp.where(tril, gmlp_w_s[0], 0.0).astype(BF16)
    wm_t = jnp.swapaxes(wm, 1, 2)
    b_t = gmlp_b_s[0].T

    (proj, hb), w_in4, (w_a4, w_b4, w_out4, w_down4) = _proj_fwd(
        place, xs, norm_mix_g, cast["w_in"], [cast[n] for n in ("w_branch_a", "w_branch_b", "w_out", "w_down")])
    (ab,), _ = _gmlp_fwd(proj, gmlp_ln_g, gmlp_ln_b, wm, b_t)
    (o_raw, obb, st_before), (w_gu,) = _hgrn_fwd(
        proj, hgrn_lb_table, hgrn_norm_g, job=_gather_job([cast["w_gate_up"]]))
    w_a, w_b, w_o = (w.reshape(D, D) for w in (w_a4, w_b4, w_out4))
    (mgb, x1), _ = _merge_fwd(xs, ab, obb, proj, w_a, w_b, w_o)
    w_dn = w_down4.reshape(FF, D)
    act, dx2b, h2b, dgu, dx1, dx1b, acc_ffn = _ffn_fwd_bwd(
        x1, target, norm_ffn_g, norm_final_g.reshape(1, D), w_gu, w_dn)

    grads, owns, parts, halves, sibh = {}, {}, {}, {}, {}

    def pair_sums(names, sibs):
        sib_of = dict(zip(names, sibs))
        for grp in by_shape(names):
            o, p = _pair_sums("rs_pair_sum_" + grp[0], place, [grads[n] for n in grp], [sib_of[n] for n in grp])
            owns.update(zip(grp, o))
            parts.update(zip(grp, p))

    def chip_sums(names, got):
        rem_of = dict(zip(names, got))
        for grp in by_shape(names):
            h = _chip_sums("rs_chip_sum_" + grp[0], [owns[n] for n in grp], [rem_of[n] for n in grp])
            halves.update(zip(grp, h))

    ffn, mix = ("w_gate_up", "w_down"), ("w_branch_a", "w_branch_b", "w_out")
    grads["w_gate_up"], _ = _dw_gate_up(h2b, dgu)
    grads["w_down"], _ = _dw_down(act, dx2b)
    (dya, dyb, dproj), got = _merge_bwd(
        dx1b, ab, obb, proj, w_o, w_a, w_b, job=_pair_exchange_job([grads[n] for n in ffn]))
    pair_sums(ffn, got)
    grads["w_branch_a"], _ = _dw_square("dw_branch_a", ab, dya)
    grads["w_branch_b"], _ = _dw_square("dw_branch_b", obb, dyb)
    grads["w_out"], _ = _dw_square("dw_out", mgb, dx1b)
    (dproj, acc_hgrn), got = _hgrn_bwd(
        dproj, dyb, w_b, o_raw, proj, st_before, hgrn_lb_table, hgrn_norm_g,
        job=_join_jobs(_chip_exchange_job([parts[n] for n in ffn]), _pair_exchange_job([grads[n] for n in mix])))
    chip_sums(ffn, got[:2])
    pair_sums(mix, got[2:])
    dproj, acc_ln, dws, dmix = _gmlp_bwd(dproj, dya, w_a, proj, gmlp_ln_g, gmlp_ln_b, wm, wm_t, b_t)
    for_sibling, got = _dw_in_half(
        "dw_in_sibling_half", place, hb, dproj, False,
        job=_join_jobs(_share_halves_job([halves[n] for n in ffn]), _chip_exchange_job([parts[n] for n in mix])))
    sibh.update(zip(ffn, got[:2]))
    chip_sums(mix, got[2:])
    grads["w_in"], got = _dw_in_half(
        "dw_in_own_half", place, hb, dproj, True, job=_share_halves_job([for_sibling]))
    pair_sums(("w_in",), got)
    (grad_x, acc_mix), got = _proj_bwd(
        dproj, w_in4, xs, dx1, norm_mix_g,
        job=_join_jobs(_chip_exchange_job([parts["w_in"]]), _share_halves_job([halves[n] for n in mix])))
    chip_sums(("w_in",), got[:1])
    sibh.update(zip(mix, got[1:]))

    lbv = jax.nn.sigmoid(hgrn_lb_table[0] - hgrn_lb_table[1])
    d_t0 = jnp.sum(acc_hgrn[0], axis=0) * lbv * (1.0 - lbv)
    loss_row = jnp.zeros((D,), F32).at[0].set(jnp.sum(acc_ffn[0]))
    dws_m = jnp.where(tril[:, None, :], dws.reshape(GCH, NG, GCH), 0.0).transpose(1, 0, 2)
    db_s = jnp.sum(dmix.reshape(GCH, NG, GCH), axis=-1).T
    sp = _pack_small(loss_row, jnp.sum(acc_mix, 0), jnp.sum(acc_ln[0], 0), jnp.sum(acc_ln[1], 0), db_s,
                     jnp.stack([d_t0, -d_t0]), jnp.sum(acc_hgrn[1], 0), jnp.sum(acc_ffn[2], 0),
                     jnp.sum(acc_ffn[1], 0), dws_m)
    zero = jnp.zeros((D,), F32)

    def pack(prefix):
        a = lambda n: args[prefix + n]
        return _pack_small(zero, a("norm_mix_g"), a("gmlp_ln_g"), a("gmlp_ln_b"), a("gmlp_b_s"),
                           a("hgrn_lb_table"), a("hgrn_norm_g"), a("norm_ffn_g"), a("norm_final_g"),
                           a("gmlp_w_s"))

    packed, (sibh["w_in"],) = _small_allreduce_adamw(
        sp, jnp.stack([pack(""), pack("m_"), pack("v_")]), _share_halves_job([halves["w_in"]]))
    loss = packed[0][0, 0]
    small = [_unpack_small(p) for p in packed]
    out = {n: tuple(s[n] for s in small) for n in SMALL}
    results = {}
    for grp in by_shape([n for n in BIG if n not in ON_SPARSECORE]):
        results.update(zip(grp, _adamws("adamw_" + grp[0], place,
                                        *[[d[n] for n in grp] for d in (big, halves, sibh, big_m, big_v)])))
    results.update(zip(ON_SPARSECORE, _adamw_sparsecore(
        "adamw_sparsecore", [(big[n], halves[n], sibh[n], big_m[n], big_v[n]) for n in ON_SPARSECORE])))
    for n, quad in results.items():
        out[n] = tuple(a.reshape(args[n].shape) for a in quad)
    return (loss, grad_x.reshape(x.shape), *[out[n][0] for n in ORDER], *[out[n][1] for n in ORDER],
            *[out[n][2] for n in ORDER], *[out[n][3] for n in ORDER])
```

```python
import functools
import math

import jax
import jax.numpy as jnp
from jax import lax
from jax.experimental import pallas as pl
from jax.experimental.pallas import tpu as pltpu
from jax.experimental.pallas import tpu_sc as plsc

F32 = jnp.float32
BF16 = jnp.bfloat16
SDS = jax.ShapeDtypeStruct
MESH = pl.DeviceIdType.MESH
ANY = pl.BlockSpec(memory_space=pl.ANY)

D = 1024
NIN = 8
NG = 8
GCH = 128
NH = 8
HD = 128
HCH = 64
HGRN_HB = 8
HGRN_TOKENS = 256
GMLP_FWD_TOKENS = 512
GMLP_BWD_TOKENS = 256
HW = HGRN_HB * HD
DW_TOKENS = 2048
DW_IN_TOKENS = 4096
ELEMENTWISE_BLOCK_BYTES = 2 * 1024 * 1024
PROJ_OUT_SLOTS = 4
FF = 2816
FFS = 1408
NCHIP = 4
EPS = 1e-6
QSCALE = HD ** -0.5
GELU_C0 = math.sqrt(2.0 / math.pi)
GELU_C1 = 0.044715
LR, B1, B2, AEPS, WD, STEP = 0.001, 0.9, 0.999, 1e-08, 0.01, 10
VMEM_LIMIT_V7X = 56 * 1024 * 1024
SP_ROWS = 144


def _cparams(**kw):
    return pltpu.CompilerParams(vmem_limit_bytes=VMEM_LIMIT_V7X, **kw)


def _mm(a, b):
    return jnp.dot(a, b, preferred_element_type=F32)


def _mm_nt(a, b):
    return lax.dot_general(a, b, (((1,), (1,)), ((), ())), preferred_element_type=F32)


def _mm_tn(a, b):
    return lax.dot_general(a, b, (((0,), (0,)), ((), ())), preferred_element_type=F32)


def _rows8(x):
    r, c = x.shape
    return jnp.sum(x.reshape(r // 8, 8, c), axis=0)


def _mean(x):
    return jnp.mean(x, axis=-1, keepdims=True)


def _sigmoid(x):
    return 1.0 / (1.0 + jnp.exp(-x))


def _gelu(x):
    t = jnp.tanh(GELU_C0 * (x + GELU_C1 * x * x * x))
    return 0.5 * x * (1.0 + t), t


def _gelu_grad(x, t):
    return 0.5 * (1.0 + t) + 0.5 * x * (1.0 - t * t) * (GELU_C0 * (1.0 + 3.0 * GELU_C1 * x * x))


def _component_of(group):
    return jnp.where(group < 6, (group + 4) % 6, group)


def _proj_fwd(place, x, g_mix, w_in4, later):
    T = x.shape[0]
    tm = min(1024, T)
    ni = T // tm
    n = len(later)

    def body(pc_ref, x_ref, g_ref, *rest):
        proj_ref, h_ref, w_all = rest[1 + n:4 + n]
        gathered = rest[4 + n:4 + 2 * n]
        hs, wbuf, wsem, obuf, osem = rest[4 + 2 * n:9 + 2 * n]
        w_sems, later_sems = rest[9 + 2 * n:15 + 2 * n], rest[15 + 2 * n:]
        jp, i = pl.program_id(0), pl.program_id(1)
        w_cols = [w_all.at[:, :, pl.ds(k * D, D)] for k in range(2)]

        def w_copy(blk):
            cols = pl.ds(pl.multiple_of((blk % 2) * D, 128), D)
            return pltpu.make_async_copy(w_all.at[pc_ref[0] ^ (blk // 2), :, cols], wbuf.at[blk % 2],
                                         wsem.at[blk % 2])

        @pl.when((jp == 0) & (i == 0))
        def _():
            _gather_start(w_cols, w_sems)
            w_copy(jp).start()

        @pl.when(i == 0)
        def _():
            w_copy(jp).wait()

        @pl.when(jp == 0)
        def _():
            xv = x_ref[...]
            r = lax.rsqrt(_mean(xv * xv) + EPS)
            hb = (xv * r * g_ref[...]).astype(BF16)
            hs[i] = hb
            h_ref[...] = hb

        step = jp * ni + i
        slot = step % PROJ_OUT_SLOTS

        def o_copy(slot_):
            comp = 2 * (pc_ref[0] ^ (jp // 2)) + jp % 2
            return pltpu.make_async_copy(
                obuf.at[slot_], proj_ref.at[comp, pl.ds(pl.multiple_of(i * tm, 8), tm)], osem.at[slot_])

        @pl.when(step >= PROJ_OUT_SLOTS)
        def _():
            o_copy(slot).wait()

        obuf[slot] = _mm(hs[i], wbuf[jp % 2])
        o_copy(slot).start()

        @pl.when(step == NIN * ni - 1)
        def _():
            for k in range(PROJ_OUT_SLOTS):
                o_copy((slot + 1 + k) % PROJ_OUT_SLOTS).wait()

        for nxt in range(1, NIN):
            @pl.when((jp == nxt - 1) & (i == ni - 1))
            def _():
                if nxt >= 2:
                    _gather_land([w_cols[nxt % 2]], w_sems, nxt // 2, first=nxt % 2)
                if nxt == 5:
                    _gather_start(gathered, later_sems)
                if nxt == NIN - 1:
                    _gather_neighbours(gathered, later_sems)
                w_copy(jp + 1).start()

        @pl.when((jp == NIN - 1) & (i == ni - 1))
        def _():
            _gather_drain(w_cols, w_sems)
            _gather_finish(gathered, later_sems)

    tile = lambda jp, i, pc: (jnp.where(jp == 0, i, ni - 1), 0)
    res = pl.pallas_call(
        body, name="proj_fwd",
        grid_spec=pltpu.PrefetchScalarGridSpec(
            num_scalar_prefetch=1, grid=(NIN, ni),
            in_specs=[pl.BlockSpec((tm, D), tile), pl.BlockSpec((1, D), lambda jp, i, pc: (0, 0))] + [ANY] * (1 + n),
            out_specs=[ANY, pl.BlockSpec((tm, D), tile)] + [ANY] * (1 + n),
            scratch_shapes=[pltpu.VMEM((ni, tm, D), BF16), pltpu.VMEM((2, D, D), BF16),
                            pltpu.SemaphoreType.DMA((2,)), pltpu.VMEM((PROJ_OUT_SLOTS, tm, D), F32),
                            pltpu.SemaphoreType.DMA((PROJ_OUT_SLOTS,))] + _gather_sems(2) + _gather_sems(n)),
        out_shape=[SDS((NIN, T, D), F32), SDS((T, D), BF16), SDS(w_in4.shape, BF16)]
        + [SDS(a.shape, a.dtype) for a in later],
        input_output_aliases={3 + k: 2 + k for k in range(1 + n)},
        compiler_params=_cparams(has_side_effects=True),
    )(place, x, g_mix, w_in4, *later)
    return res[:2], res[2], res[3:]


def _chunks_abreast(x):
    return jnp.concatenate([x[GCH * ch:GCH * (ch + 1)] for ch in range(x.shape[0] // GCH)], axis=1)


def _chunks_stacked(x):
    return jnp.concatenate([x[:, GCH * ch:GCH * (ch + 1)] for ch in range(x.shape[1] // GCH)], axis=0)


def _layer_norm_stats(gv):
    mu = _mean(gv)
    xc = gv - mu
    rs = lax.rsqrt(_mean(xc * xc) + EPS)
    return xc * rs, rs


def _gmlp_fwd(proj, ln_g, ln_b, wm, b_t, job=None):
    T = proj.shape[1]
    tm = min(GMLP_FWD_TOKENS, T)

    def body(u_ref, v_ref, lg_ref, lb_ref, wm_ref, bt_ref, a_ref, a_s):
        gu, _ = _gelu(u_ref[...])
        gv, _ = _gelu(v_ref[...])
        vhat, _ = _layer_norm_stats(gv)
        vnb = (vhat * lg_ref[...] + lb_ref[...]).astype(BF16)
        for g in range(NG):
            cols = slice(128 * g, 128 * (g + 1))
            mixed = _mm(wm_ref[g], _chunks_abreast(vnb[:, cols])) + bt_ref[:, g:g + 1]
            a_s[:, cols] = gu[:, cols] * _chunks_stacked(mixed)
        a_ref[...] = a_s[...].astype(BF16)

    row = lambda i: (0, 0)
    return _call(
        body, name="gmlp_fwd", grid=(T // tm,), job=job, args=(proj, proj, ln_g, ln_b, wm, b_t),
        in_specs=[pl.BlockSpec((None, tm, D), lambda i: (0, i, 0)), pl.BlockSpec((None, tm, D), lambda i: (1, i, 0)),
                  pl.BlockSpec((1, D), row), pl.BlockSpec((1, D), row),
                  pl.BlockSpec((NG, GCH, GCH), lambda i: (0, 0, 0)), pl.BlockSpec((GCH, NG), row)],
        out_specs=[pl.BlockSpec((tm, D), lambda i: (i, 0))],
        out_shape=[SDS((T, D), BF16)],
        scratch_shapes=[pltpu.VMEM((tm, D), F32)])


def _cumsum64(x, row):
    for s in (1, 2, 4, 8, 16, 32):
        x = x + jnp.where(row >= s, pltpu.roll(x, s, 0), 0.0)
    return x


def _revcumsum64(x, row):
    n = x.shape[0]
    for s in (1, 2, 4, 8, 16, 32):
        x = x + jnp.where(row < HCH - s, pltpu.roll(x, n - s, 0), 0.0)
    return x


def _head_mean(x):
    parts = [jnp.broadcast_to(_mean(x[:, HD * h:HD * (h + 1)]), (x.shape[0], HD)) for h in range(x.shape[1] // HD)]
    return jnp.concatenate(parts, axis=1)


def _seg_sum(x):
    n, c = x.shape
    s = jnp.sum(x.reshape(n // HCH, HCH, c), axis=1, keepdims=True)
    return jnp.broadcast_to(s, (n // HCH, HCH, c)).reshape(n, c)


def _seg_row(x, idx):
    n, c = x.shape
    x3 = x.reshape(n // HCH, HCH, c)
    return jnp.broadcast_to(x3[:, idx:idx + 1, :], x3.shape).reshape(n, c)


def _hgrn_gates(fl, lbv, row):
    s = _sigmoid(fl)
    f = lbv + (1.0 - lbv) * s
    a = _cumsum64(jnp.log(f), row)
    return s, f, a, _seg_row(a, HCH // 2 - 1), _seg_row(a, HCH - 1)


def _hgrn_fwd(proj, lb_table, norm_g, job=None):
    T = proj.shape[1]
    tb = min(HGRN_TOKENS, T)
    nc = tb // HCH

    def body(q_ref, fl_ref, v_ref, g_ref, lbt_ref, gn_ref, o_ref, ob_ref, stb_ref, st_s, o_s):
        @pl.when(pl.program_id(1) == 0)
        def _():
            st_s[...] = jnp.zeros_like(st_s)

        row = lax.broadcasted_iota(jnp.int32, (tb, HW), 0) & (HCH - 1)
        lbv = _sigmoid(lbt_ref[0:1, :] - lbt_ref[1:2, :])
        _, f, a, a_mid, a_last = _hgrn_gates(fl_ref[...], lbv, row)
        k = 1.0 - f
        qs = q_ref[...] * QSCALE
        q_in = (qs * jnp.exp(a - a_mid)).astype(BF16)
        k_in = (k * jnp.exp(a_mid - a)).astype(BF16)
        q_a = (qs * jnp.exp(a)).astype(BF16)
        k_d = (k * jnp.exp(a_last - a)).astype(BF16)
        dec = jnp.exp(a_last)
        vb = v_ref[...].astype(BF16)
        tri = (lax.broadcasted_iota(jnp.int32, (HCH, HCH), 0)
               >= lax.broadcasted_iota(jnp.int32, (HCH, HCH), 1))
        for c in range(nc):
            sl = slice(HCH * c, HCH * (c + 1))
            for hh in range(HGRN_HB):
                hs = slice(HD * hh, HD * (hh + 1))
                st = st_s[hh]
                stb_ref[hh, c] = st
                sc = jnp.where(tri, _mm_nt(q_in[sl, hs], k_in[sl, hs]), 0.0)
                o_s[sl, hs] = _mm(sc.astype(BF16), vb[sl, hs]) + _mm_nt(q_a[sl, hs], st.astype(BF16))
                d64 = dec[sl, hs]
                st_s[hh] = st * jnp.concatenate([d64, d64], axis=0) + _mm_tn(vb[sl, hs], k_d[sl, hs])
        o = o_s[...]
        r = lax.rsqrt(_head_mean(o * o) + EPS)
        g = g_ref[...]
        o_ref[...] = o
        ob_ref[...] = (o * r * gn_ref[...] * (g * _sigmoid(g))).astype(BF16)

    def col(off):
        return pl.BlockSpec((None, tb, HW), lambda h, cb: (off, cb, h))

    return _call(
        body, name="hgrn_fwd", grid=(NH // HGRN_HB, T // tb), job=job,
        args=(proj, proj, proj, proj, lb_table, norm_g),
        in_specs=[col(2), col(3), col(4), col(5),
                  pl.BlockSpec((2, HW), lambda h, cb: (0, h)), pl.BlockSpec((1, HW), lambda h, cb: (0, h))],
        out_specs=[pl.BlockSpec((tb, HW), lambda h, cb: (cb, h)), pl.BlockSpec((tb, HW), lambda h, cb: (cb, h)),
                   pl.BlockSpec((HGRN_HB, nc, HD, HD), lambda h, cb: (h, cb, 0, 0))],
        out_shape=[SDS((T, D), F32), SDS((T, D), BF16), SDS((NH, T // HCH, HD, HD), F32)],
        scratch_shapes=[pltpu.VMEM((HGRN_HB, HD, HD), F32), pltpu.VMEM((tb, HW), F32)])


def _merge_fwd(x, ab, ob, proj, w_a, w_b, w_out, job=None):
    T = x.shape[0]
    tm = min(512, T)

    def body(x_ref, ab_ref, ob_ref, ga_ref, gb_ref, wa_ref, wb_ref, wo_ref, mg_ref, x1_ref):
        ya = _mm(ab_ref[...], wa_ref[...])
        yb = _mm(ob_ref[...], wb_ref[...])
        merged = (_sigmoid(ga_ref[...]) * ya + _sigmoid(gb_ref[...]) * yb).astype(BF16)
        mg_ref[...] = merged
        x1_ref[...] = x_ref[...] + _mm(merged, wo_ref[...])

    t = lambda i: (i, 0)
    w = lambda i: (0, 0)
    return _call(
        body, name="merge_fwd", grid=(T // tm,), job=job, args=(x, ab, ob, proj, proj, w_a, w_b, w_out),
        in_specs=[pl.BlockSpec((tm, D), t), pl.BlockSpec((tm, D), t), pl.BlockSpec((tm, D), t),
                  pl.BlockSpec((None, tm, D), lambda i: (6, i, 0)), pl.BlockSpec((None, tm, D), lambda i: (7, i, 0)),
                  pl.BlockSpec((D, D), w), pl.BlockSpec((D, D), w), pl.BlockSpec((D, D), w)],
        out_specs=[pl.BlockSpec((tm, D), t)] * 2,
        out_shape=[SDS((T, D), BF16), SDS((T, D), F32)])


def _ffn_fwd_bwd(x1, target, g_ffn, g_fin, w_gu, w_down):
    T = x1.shape[0]
    tm = min(256, T)
    inv_d = 1.0 / D

    def body(x1_ref, tg_ref, gf_ref, gn_ref, wgu_ref, wd_ref,
             act_ref, dx2b_ref, h2b_ref, dgu_ref, dx1_ref, dx1b_ref, acc_ref):
        @pl.when(pl.program_id(0) == 0)
        def _():
            acc_ref[...] = jnp.zeros_like(acc_ref)

        x1v = x1_ref[...]
        gf = gf_ref[...]
        gn = gn_ref[...]
        rr1 = lax.rsqrt(_mean(x1v * x1v) + EPS)
        x1n = x1v * rr1
        h2b = (x1n * gf).astype(BF16)
        h2b_ref[...] = h2b
        gate = _mm(h2b, wgu_ref[0])
        up = _mm(h2b, wgu_ref[1])
        sg = _sigmoid(gate)
        si = gate * sg
        act = (si * up).astype(BF16)
        act_ref[...] = act
        x2 = x1v + _mm(act, wd_ref[...])
        rr2 = lax.rsqrt(_mean(x2 * x2) + EPS)
        x2n = x2 * rr2
        e = x2n * gn - tg_ref[...]
        acc_ref[0] += _rows8(e * e) * (0.5 * inv_d)
        dy = e * inv_d
        acc_ref[1] += _rows8(dy * x2n)
        dxn = dy * gn
        dx2 = rr2 * (dxn - x2n * _mean(dxn * x2n))
        dx2b = dx2.astype(BF16)
        dx2b_ref[...] = dx2b
        dact = _mm_nt(dx2b, wd_ref[...])
        dgate = (dact * up * (sg * (1.0 + gate * (1.0 - sg)))).astype(BF16)
        dup = (dact * si).astype(BF16)
        dgu_ref[0] = dgate
        dgu_ref[1] = dup
        dh2 = _mm_nt(dgate, wgu_ref[0]) + _mm_nt(dup, wgu_ref[1])
        acc_ref[2] += _rows8(dh2 * x1n)
        dxn1 = dh2 * gf
        dx1 = dx2 + rr1 * (dxn1 - x1n * _mean(dxn1 * x1n))
        dx1_ref[...] = dx1
        dx1b_ref[...] = dx1.astype(BF16)

    t = lambda i: (i, 0)
    w = lambda i: (0, 0)
    one = pl.Buffered(1)
    return pl.pallas_call(
        body, name="ffn_fwd_bwd", grid=(T // tm,),
        in_specs=[pl.BlockSpec((tm, D), t), pl.BlockSpec((tm, D), t),
                  pl.BlockSpec((1, D), w), pl.BlockSpec((1, D), w),
                  pl.BlockSpec((2, D, FF), lambda i: (0, 0, 0), pipeline_mode=one),
                  pl.BlockSpec((FF, D), w, pipeline_mode=one)],
        out_specs=[pl.BlockSpec((tm, FF), t), pl.BlockSpec((tm, D), t), pl.BlockSpec((tm, D), t),
                   pl.BlockSpec((2, tm, FF), lambda i: (0, i, 0)),
                   pl.BlockSpec((tm, D), t), pl.BlockSpec((tm, D), t),
                   pl.BlockSpec((3, 8, D), lambda i: (0, 0, 0))],
        out_shape=[SDS((T, FF), BF16), SDS((T, D), BF16), SDS((T, D), BF16),
                   SDS((2, T, FF), BF16), SDS((T, D), F32), SDS((T, D), BF16),
                   SDS((3, 8, D), F32)],
        compiler_params=_cparams(),
    )(x1, target, g_ffn, g_fin, w_gu, w_down)


def _merge_bwd(dx1b, ab, ob, proj, w_out, w_a, w_b, job=None):
    T = dx1b.shape[0]
    tm = min(512, T)

    def body(dx_ref, ab_ref, ob_ref, ga_ref, gb_ref, wo_ref, wa_ref, wb_ref, dya_ref, dyb_ref, dp_ref):
        dm = _mm_nt(dx_ref[...], wo_ref[...])
        sa = _sigmoid(ga_ref[...])
        sb = _sigmoid(gb_ref[...])
        dya_ref[...] = (dm * sa).astype(BF16)
        dyb_ref[...] = (dm * sb).astype(BF16)
        dp_ref[0] = (dm * _mm(ab_ref[...], wa_ref[...]) * sa * (1.0 - sa)).astype(BF16)
        dp_ref[1] = (dm * _mm(ob_ref[...], wb_ref[...]) * sb * (1.0 - sb)).astype(BF16)

    t = lambda i: (i, 0)
    w = lambda i: (0, 0)
    return _call(
        body, name="merge_bwd", grid=(T // tm,),
        in_specs=[pl.BlockSpec((tm, D), t), pl.BlockSpec((tm, D), t), pl.BlockSpec((tm, D), t),
                  pl.BlockSpec((None, tm, D), lambda i: (6, i, 0)), pl.BlockSpec((None, tm, D), lambda i: (7, i, 0)),
                  pl.BlockSpec((D, D), w), pl.BlockSpec((D, D), w), pl.BlockSpec((D, D), w)],
        out_specs=[pl.BlockSpec((tm, D), t)] * 2 + [pl.BlockSpec((2, tm, D), lambda i: (3, i, 0))],
        out_shape=[SDS((T, D), BF16), SDS((T, D), BF16), SDS((NIN, T, D), BF16)],
        args=(dx1b, ab, ob, proj, proj, w_out, w_a, w_b), job=job)


def _hgrn_bwd(dproj, dyb, w_b, o_raw, proj, st_before, lb_table, norm_g, job=None):
    T = dyb.shape[0]
    tb = min(HGRN_TOKENS, T)
    nc = tb // HCH
    nb = T // tb

    def body(dp_in, dyb_ref, wb_ref, o_ref, q_ref, fl_ref, v_ref, g_ref, stb_ref, lbt_ref, gn_ref,
             dp_ref, acc_ref, dst_s, dqin_s, dqa_s, dkin_s, dkd_s, dv_s, ddec_s):
        del dp_in

        @pl.when(pl.program_id(1) == 0)
        def _():
            dst_s[...] = jnp.zeros_like(dst_s)
            acc_ref[...] = jnp.zeros_like(acc_ref)

        row = lax.broadcasted_iota(jnp.int32, (tb, HW), 0) & (HCH - 1)
        gn = gn_ref[...]
        lbv = _sigmoid(lbt_ref[0:1, :] - lbt_ref[1:2, :])
        o = o_ref[...]
        r = lax.rsqrt(_head_mean(o * o) + EPS)
        on = o * r
        g = g_ref[...]
        sgm = _sigmoid(g)
        dob_v = _mm_nt(dyb_ref[...], wb_ref[...])
        dp_ref[3] = (dob_v * on * gn * (sgm * (1.0 + g * (1.0 - sgm)))).astype(BF16)
        do_n = dob_v * (g * sgm)
        acc_ref[1] += _rows8(do_n * on)
        dxn = do_n * gn
        do = (r * (dxn - on * _head_mean(dxn * on))).astype(BF16)
        s, f, a, a_mid, a_last = _hgrn_gates(fl_ref[...], lbv, row)
        k = 1.0 - f
        qs = q_ref[...] * QSCALE
        e_q = jnp.exp(a - a_mid)
        e_k = jnp.exp(a_mid - a)
        e_a = jnp.exp(a)
        e_l = jnp.exp(a_last - a)
        dec = jnp.exp(a_last)
        q_in = qs * e_q
        k_in = k * e_k
        q_a = qs * e_a
        k_d = k * e_l
        q_inb, k_inb, q_ab, k_db = (z.astype(BF16) for z in (q_in, k_in, q_a, k_d))
        vb = v_ref[...].astype(BF16)
        tri = (lax.broadcasted_iota(jnp.int32, (HCH, HCH), 0)
               >= lax.broadcasted_iota(jnp.int32, (HCH, HCH), 1))
        for c in reversed(range(nc)):
            sl = slice(HCH * c, HCH * (c + 1))
            for hh in range(HGRN_HB):
                hs = slice(HD * hh, HD * (hh + 1))
                stp = stb_ref[hh, c]
                dst = dst_s[hh]
                dstb = dst.astype(BF16)
                do_c = do[sl, hs]
                v_c = vb[sl, hs]
                dqa_s[sl, hs] = _mm(do_c, stp.astype(BF16))
                dkd_s[sl, hs] = _mm(v_c, dstb)
                ddec_s[sl, hs] = jnp.broadcast_to(jnp.sum(dst * stp, axis=0, keepdims=True), (HCH, HD))
                sc = jnp.where(tri, _mm_nt(q_inb[sl, hs], k_inb[sl, hs]), 0.0).astype(BF16)
                dsc = jnp.where(tri, _mm_nt(do_c, v_c), 0.0).astype(BF16)
                dv_s[sl, hs] = _mm_nt(k_db[sl, hs], dstb) + _mm_tn(sc, do_c)
                dqin_s[sl, hs] = _mm(dsc, k_inb[sl, hs])
                dkin_s[sl, hs] = _mm_tn(dsc, q_inb[sl, hs])
                d64 = dec[sl, hs]
                dst_s[hh] = dst * jnp.concatenate([d64, d64], axis=0) + _mm_tn(do_c, q_ab[sl, hs])
        dq_in = dqin_s[...]
        dq_a = dqa_s[...]
        dk_in = dkin_s[...]
        dk_d = dkd_s[...]
        dp_ref[0] = ((dq_in * e_q + dq_a * e_a) * QSCALE).astype(BF16)
        dp_ref[2] = dv_s[...].astype(BF16)
        tq = dq_in * q_in
        tk = dk_in * k_in
        td = dk_d * k_d
        d_a = tq + dq_a * q_a - tk - td
        d_a = d_a + jnp.where(row == HCH // 2 - 1, _seg_sum(tk - tq), 0.0)
        d_a = d_a + jnp.where(row == HCH - 1, _seg_sum(td) + ddec_s[...] * dec, 0.0)
        dlf = _revcumsum64(d_a, row)
        df = dlf / f - (dk_in * e_k + dk_d * e_l)
        dp_ref[1] = (df * (1.0 - lbv) * s * (1.0 - s)).astype(BF16)
        acc_ref[0] += _rows8(df * (1.0 - s))

    def col(off):
        return pl.BlockSpec((None, tb, HW), lambda h, cb: (off, nb - 1 - cb, h))

    hb = lambda h, cb: (nb - 1 - cb, h)
    return _call(
        body, name="hgrn_bwd", grid=(NH // HGRN_HB, nb), job=job,
        args=(dproj, dyb, w_b, o_raw, proj, proj, proj, proj, st_before, lb_table, norm_g),
        in_specs=[ANY, pl.BlockSpec((tb, D), lambda h, cb: (nb - 1 - cb, 0)),
                  pl.BlockSpec((HW, D), lambda h, cb: (h, 0)), pl.BlockSpec((tb, HW), hb),
                  col(2), col(3), col(4), col(5),
                  pl.BlockSpec((HGRN_HB, nc, HD, HD), lambda h, cb: (h, nb - 1 - cb, 0, 0)),
                  pl.BlockSpec((2, HW), lambda h, cb: (0, h)), pl.BlockSpec((1, HW), lambda h, cb: (0, h))],
        out_specs=[pl.BlockSpec((4, tb, HW), lambda h, cb: (0, nb - 1 - cb, h)),
                   pl.BlockSpec((2, 8, HW), lambda h, cb: (0, 0, h))],
        out_shape=[SDS(dproj.shape, BF16), SDS((2, 8, D), F32)],
        scratch_shapes=[pltpu.VMEM((HGRN_HB, HD, HD), F32)] + [pltpu.VMEM((tb, HW), F32)] * 6,
        aliases={0: 0})


def _gmlp_bwd(dproj, dya, w_a, proj, ln_g, ln_b, wm, wm_t, b_t):
    T = dya.shape[0]
    tm = min(GMLP_BWD_TOKENS, T)

    def body(dp_in, dya_ref, wa_ref, u_ref, v_ref, lg_ref, lb_ref, wm_ref, wmt_ref, bt_ref,
             dp_ref, acc_ref, dws_ref, dmix_ref, du_s, dvn_s):
        del dp_in

        @pl.when(pl.program_id(0) == 0)
        def _():
            acc_ref[...] = jnp.zeros_like(acc_ref)
            dws_ref[...] = jnp.zeros_like(dws_ref)
            dmix_ref[...] = jnp.zeros_like(dmix_ref)

        u = u_ref[...]
        v = v_ref[...]
        lg = lg_ref[...]
        gu, t_u = _gelu(u)
        gv, t_v = _gelu(v)
        vhat, rs = _layer_norm_stats(gv)
        vnb = (vhat * lg + lb_ref[...]).astype(BF16)
        da_v = _mm_nt(dya_ref[...], wa_ref[...])
        for g in range(NG):
            cols = slice(128 * g, 128 * (g + 1))
            vng = _chunks_abreast(vnb[:, cols])
            mixed = _mm(wm_ref[g], vng) + bt_ref[:, g:g + 1]
            dag = _chunks_abreast(da_v[:, cols])
            dmx = dag * _chunks_abreast(gu[:, cols])
            du_s[:, cols] = _chunks_stacked(dag * mixed)
            dmxb = dmx.astype(BF16)
            dws_ref[:, cols] += _mm_nt(dmxb, vng)
            dmix_ref[:, cols] += sum(dmx[:, GCH * ch:GCH * (ch + 1)] for ch in range(tm // GCH))
            dvn_s[:, cols] = _chunks_stacked(_mm(wmt_ref[g], dmxb))
        dp_ref[0] = (du_s[...] * _gelu_grad(u, t_u)).astype(BF16)
        dvn = dvn_s[...]
        acc_ref[0] += _rows8(dvn * vhat)
        acc_ref[1] += _rows8(dvn)
        dvh = dvn * lg
        dgv = rs * (dvh - _mean(dvh) - vhat * _mean(dvh * vhat))
        dp_ref[1] = (dgv * _gelu_grad(v, t_v)).astype(BF16)

    row = lambda i: (0, 0)
    w3 = lambda i: (0, 0, 0)
    return pl.pallas_call(
        body, name="gmlp_bwd", grid=(T // tm,),
        in_specs=[ANY, pl.BlockSpec((tm, D), lambda i: (i, 0)), pl.BlockSpec((D, D), row),
                  pl.BlockSpec((None, tm, D), lambda i: (0, i, 0)), pl.BlockSpec((None, tm, D), lambda i: (1, i, 0)),
                  pl.BlockSpec((1, D), row), pl.BlockSpec((1, D), row),
                  pl.BlockSpec((NG, GCH, GCH), w3), pl.BlockSpec((NG, GCH, GCH), w3),
                  pl.BlockSpec((GCH, NG), row)],
        out_specs=[pl.BlockSpec((2, tm, D), lambda i: (2, i, 0)),
                   pl.BlockSpec((2, 8, D), w3), pl.BlockSpec((GCH, D), row), pl.BlockSpec((GCH, D), row)],
        out_shape=[SDS(dproj.shape, BF16), SDS((2, 8, D), F32), SDS((GCH, D), F32), SDS((GCH, D), F32)],
        scratch_shapes=[pltpu.VMEM((tm, D), F32), pltpu.VMEM((tm, D), F32)],
        input_output_aliases={0: 0},
        compiler_params=_cparams(),
    )(dproj, dya, w_a, proj, proj, ln_g, ln_b, wm, wm_t, b_t)


def _proj_bwd(dproj, w_in4, x, dx1, g_mix, job=None):
    T = x.shape[0]
    tm = min(256, T)
    order = (2, 3, 4, 5, 0, 1, 6, 7)

    def body(dp_ref, w_ref, x_ref, dx1_ref, g_ref, gx_ref, acc_ref):
        @pl.when(pl.program_id(0) == 0)
        def _():
            acc_ref[...] = jnp.zeros_like(acc_ref)

        dh = None
        for m, og in enumerate(order):
            part = _mm_nt(dp_ref[m], w_ref[og // 2, :, D * (og % 2):D * (og % 2 + 1)])
            dh = part if dh is None else dh + part
        xv = x_ref[...]
        r = lax.rsqrt(_mean(xv * xv) + EPS)
        xn = xv * r
        acc_ref[...] += _rows8(dh * xn)
        dxn = dh * g_ref[...]
        gx_ref[...] = dx1_ref[...] + r * (dxn - xn * _mean(dxn * xn))

    t = lambda i: (i, 0)
    return _call(
        body, name="proj_bwd", grid=(T // tm,),
        in_specs=[pl.BlockSpec((NIN, tm, D), lambda i: (0, i, 0)),
                  pl.BlockSpec((NCHIP, D, 2 * D), lambda i: (0, 0, 0), pipeline_mode=pl.Buffered(1)),
                  pl.BlockSpec((tm, D), t), pl.BlockSpec((tm, D), t), pl.BlockSpec((1, D), lambda i: (0, 0))],
        out_specs=[pl.BlockSpec((tm, D), t), pl.BlockSpec((8, D), lambda i: (0, 0))],
        out_shape=[SDS((T, D), F32), SDS((8, D), F32)],
        args=(dproj, w_in4, x, dx1, g_mix), job=job)


def _dw_call(name, a, b, a_spec, b_spec, o_spec, out_shape, nblk, tt, job=None, prefetch=None):
    T = a.shape[-2]

    def body(*refs):
        a_ref, b_ref, o_ref = refs[-3:]

        @pl.when(pl.program_id(1) == 0)
        def _():
            o_ref[...] = jnp.zeros_like(o_ref)
        o_ref[...] += _mm_tn(a_ref[...], b_ref[...])

    (out,), job_out = _call(
        body, name=name, grid=(nblk, T // tt), in_specs=[a_spec, b_spec], out_specs=[o_spec],
        out_shape=[out_shape], args=(a, b), job=job, prefetch=prefetch)
    return out, job_out


def _dw_in_half(name, place, hb, dproj, mine, job=None):
    tt = min(DW_IN_TOKENS, hb.shape[0])

    def comp(k, pc):
        return _component_of(2 * k + (pc[1] if mine else 1 - pc[1]))

    return _dw_call(
        name, hb, dproj,
        pl.BlockSpec((tt, D), lambda k, t, pc: (t, 0)),
        pl.BlockSpec((None, tt, D), lambda k, t, pc: (comp(k, pc), t, 0)),
        pl.BlockSpec((None, D, D), lambda k, t, pc: (k, 0, 0)),
        SDS((NCHIP, D, D), F32), NCHIP, tt, job, place)


def _dw_gate_up(h2b, dgu, job=None):
    tt = min(DW_TOKENS, h2b.shape[0])
    return _dw_call(
        "dw_gate_up", h2b, dgu,
        pl.BlockSpec((tt, D), lambda k, t: (t, 0)),
        pl.BlockSpec((None, tt, FFS), lambda k, t: (k // 2, t, k % 2)),
        pl.BlockSpec((None, D, FFS), lambda k, t: (k, 0, 0)),
        SDS((NCHIP, D, FFS), F32), NCHIP, tt, job)


def _dw_down(act, dx2b, job=None):
    tt = min(DW_TOKENS, act.shape[0])
    g, job_out = _dw_call(
        "dw_down", act, dx2b,
        pl.BlockSpec((tt, FFS), lambda k, t: (t, k)),
        pl.BlockSpec((tt, D), lambda k, t: (t, 0)),
        pl.BlockSpec((FFS, D), lambda k, t: (k, 0)),
        SDS((FF, D), F32), 2, tt, job)
    return g.reshape(NCHIP, FF // NCHIP, D), job_out


def _dw_square(name, a, b, job=None):
    tt = min(DW_TOKENS, a.shape[0])
    g, job_out = _dw_call(
        name, a, b,
        pl.BlockSpec((tt, D), lambda k, t: (t, 0)), pl.BlockSpec((tt, D), lambda k, t: (t, 0)),
        pl.BlockSpec((D, D), lambda k, t: (0, 0)), SDS((D, D), F32), 1, tt, job)
    return g.reshape(NCHIP, D // NCHIP, D), job_out


def _place():
    x, y, c = lax.axis_index("x"), lax.axis_index("y"), lax.axis_index("c")
    return x, y, c, 2 * x + y


def _chip_at(x, y, s):
    return x ^ (s >> 1), y ^ (s & 1)


class _Job:
    def __init__(self, ins, out_shapes, sems, start, finish, aliases=None, mid=None):
        self.ins, self.out_shapes, self.sems = list(ins), list(out_shapes), list(sems)
        self.start, self.finish, self.aliases = start, finish, dict(aliases or {})
        self.mid = mid if mid is not None else (lambda ins, outs, sems: None)


def _join_jobs(*jobs):
    def cut(refs, sizes):
        out, at = [], 0
        for n in sizes:
            out.append(refs[at:at + n])
            at += n
        return out

    ni = [len(j.ins) for j in jobs]
    no = [len(j.out_shapes) for j in jobs]
    ns = [len(j.sems) for j in jobs]

    def run(which):
        def go(ins, outs, sems):
            for j, a, b, c in zip(jobs, cut(ins, ni), cut(outs, no), cut(sems, ns)):
                getattr(j, which)(a, b, c)
        return go

    aliases = {}
    for k, j in enumerate(jobs):
        for a, b in j.aliases.items():
            aliases[sum(ni[:k]) + a] = sum(no[:k]) + b
    return _Job([a for j in jobs for a in j.ins], [o for j in jobs for o in j.out_shapes],
                [s for j in jobs for s in j.sems], run("start"), run("finish"), aliases, run("mid"))


def _call(body, *, name, grid, in_specs, out_specs, out_shape, args, scratch_shapes=(), aliases=None,
          job=None, prefetch=None):
    n_in, n_out, n_scr = len(in_specs), len(out_specs), len(scratch_shapes)
    npf = 0 if prefetch is None else 1
    job = job if job is not None else _Job([], [], [], lambda *a: None, lambda *a: None)
    ji, jo = len(job.ins), len(job.out_shapes)
    steps = math.prod(grid)

    def wrapped(*refs):
        pf, refs = refs[:npf], refs[npf:]
        ins, jin = refs[:n_in], refs[n_in:n_in + ji]
        o0 = n_in + ji
        outs, jout = refs[o0:o0 + n_out], refs[o0 + n_out:o0 + n_out + jo]
        s0 = o0 + n_out + jo
        scr, jsem = refs[s0:s0 + n_scr], refs[s0 + n_scr:]
        step = functools.reduce(lambda acc, ag: acc * ag[1] + pl.program_id(ag[0]), enumerate(grid), 0)
        if ji or jo:
            @pl.when(step == 0)
            def _():
                job.start(jin, jout, jsem)

        body(*pf, *ins, *outs, *scr)

        if ji or jo:
            @pl.when(step == steps // 2)
            def _():
                job.mid(jin, jout, jsem)

            @pl.when(step == steps - 1)
            def _():
                job.finish(jin, jout, jsem)

    io = {npf + a: b for a, b in dict(aliases or {}).items()}
    io.update({npf + n_in + a: n_out + b for a, b in job.aliases.items()})
    kw = dict(in_specs=list(in_specs) + [ANY] * ji, out_specs=list(out_specs) + [ANY] * jo,
              scratch_shapes=list(scratch_shapes) + job.sems)
    if npf:
        kw = dict(grid_spec=pltpu.PrefetchScalarGridSpec(num_scalar_prefetch=1, grid=grid, **kw))
    else:
        kw["grid"] = grid
    res = pl.pallas_call(
        wrapped, name=name, out_shape=list(out_shape) + job.out_shapes, input_output_aliases=io,
        compiler_params=_cparams(has_side_effects=bool(ji or jo)), **kw,
    )(*(() if prefetch is None else (prefetch,)), *args, *job.ins)
    return list(res[:n_out]), list(res[n_out:])


def _cast_shards(name, place, ws, paired=False):
    n = len(ws)
    rows, cols = ws[0].shape
    tr = 352 if rows % 352 == 0 else 256
    shape = (2, rows, 2 * cols) if paired else (NCHIP, rows, cols)
    mine = (lambda i, pc: (pc[0] // 2, i, pc[0] % 2)) if paired else (lambda i, pc: (pc[0], i, 0))

    def body(pc_ref, *refs):
        del pc_ref
        for w_ref, o_ref in zip(refs[:n], refs[n:]):
            o_ref[...] = w_ref[...].astype(BF16)

    return pl.pallas_call(
        body, name=name,
        grid_spec=pltpu.PrefetchScalarGridSpec(
            num_scalar_prefetch=1, grid=(rows // tr,),
            in_specs=[pl.BlockSpec((tr, cols), lambda i, pc: (i, 0))] * n,
            out_specs=[pl.BlockSpec((None, tr, cols), mine)] * n),
        out_shape=[SDS(shape, BF16)] * n,
        compiler_params=_cparams(),
    )(place, *ws)


def _sibling_copy(ref, send_sem, recv_sem):
    x, y, c, _ = _place()
    return pltpu.make_async_remote_copy(src_ref=ref, dst_ref=ref, send_sem=send_sem, recv_sem=recv_sem,
                                        device_id=(x, y, 1 - c), device_id_type=MESH)


def _slot(arr, chip):
    if arr.shape[0] == NCHIP:
        return arr.at[chip]
    cols = arr.shape[2] // 2
    return arr.at[chip // 2, :, pl.ds(pl.multiple_of((chip % 2) * cols, 128), cols)]


def _half_rows(arr, slot, core):
    half = arr.shape[1] // 2
    return _slot(arr, slot).at[pl.ds(pl.multiple_of(core * half, 16), half)]


def _quarter_rows(arr, slot, core, q):
    quarter = arr.shape[1] // 4
    return _slot(arr, slot).at[pl.ds(pl.multiple_of((2 * core + q) * quarter, 16), quarter)]


def _chip_copy(ref, dist, send_sem, recv_sem):
    x, y, c, _ = _place()
    cx, cy = _chip_at(x, y, dist)
    return pltpu.make_async_remote_copy(src_ref=ref, dst_ref=ref, send_sem=send_sem, recv_sem=recv_sem,
                                        device_id=(cx, cy, c), device_id_type=MESH)


def _gather_sems(n):
    dma = pltpu.SemaphoreType.DMA
    return [dma((n, 2))] * 4 + [dma((n, 4))] * 2


def _gather_start(arrs, sems):
    dsend, drecv = sems[0], sems[1]
    _, _, c, j = _place()
    for w, arr in enumerate(arrs):
        for dist in (1, 2):
            _chip_copy(_half_rows(arr, j, c), dist, dsend.at[w, dist - 1], drecv.at[w, dist - 1]).start()


def _gather_land(arrs, sems, dist, first=0):
    dsend, drecv, rsend, rrecv, fsend, frecv = sems
    _, _, c, j = _place()
    if dist < 3:
        other = 3 - dist
        for w, arr in enumerate(arrs, first):
            landed = _half_rows(arr, j ^ dist, c)
            _chip_copy(landed, dist, dsend.at[w, dist - 1], drecv.at[w, dist - 1]).wait_recv()
            relay = _quarter_rows(arr, j ^ dist, c, other - 1)
            _chip_copy(relay, other, rsend.at[w, other - 1], rrecv.at[w, other - 1]).start()
            _sibling_copy(landed, fsend.at[w, dist - 1], frecv.at[w, dist - 1]).start()
        for w, arr in enumerate(arrs, first):
            theirs = _half_rows(arr, j ^ dist, 1 - c)
            _sibling_copy(theirs, fsend.at[w, dist - 1], frecv.at[w, dist - 1]).wait_recv()
    else:
        for w, arr in enumerate(arrs, first):
            for via in (1, 2):
                piece = _quarter_rows(arr, j ^ 3, c, via - 1)
                _chip_copy(piece, via, rsend.at[w, via - 1], rrecv.at[w, via - 1]).wait_recv()
                _sibling_copy(piece, fsend.at[w, 1 + via], frecv.at[w, 1 + via]).start()
        for w, arr in enumerate(arrs, first):
            for via in (1, 2):
                theirs = _quarter_rows(arr, j ^ 3, 1 - c, via - 1)
                _sibling_copy(theirs, fsend.at[w, 1 + via], frecv.at[w, 1 + via]).wait_recv()


def _gather_drain(arrs, sems):
    dsend, drecv, rsend, rrecv, fsend, frecv = sems
    _, _, c, j = _place()
    for w, arr in enumerate(arrs):
        for dist in (1, 2):
            other = 3 - dist
            _chip_copy(_half_rows(arr, j, c), dist, dsend.at[w, dist - 1], drecv.at[w, dist - 1]).wait_send()
            _chip_copy(_quarter_rows(arr, j ^ dist, c, other - 1), other,
                       rsend.at[w, other - 1], rrecv.at[w, other - 1]).wait_send()
            _sibling_copy(_half_rows(arr, j ^ dist, c), fsend.at[w, dist - 1], frecv.at[w, dist - 1]).wait_send()
            _sibling_copy(_quarter_rows(arr, j ^ 3, c, dist - 1),
                          fsend.at[w, 1 + dist], frecv.at[w, 1 + dist]).wait_send()


def _gather_neighbours(arrs, sems):
    _gather_land(arrs, sems, 1)
    _gather_land(arrs, sems, 2)


def _gather_finish(arrs, sems):
    _gather_land(arrs, sems, 3)
    _gather_drain(arrs, sems)


def _gather_job(arrs):
    n = len(arrs)
    return _Job(arrs, [SDS(a.shape, a.dtype) for a in arrs], _gather_sems(n),
                lambda ins, outs, sems: _gather_start(outs, sems),
                lambda ins, outs, sems: _gather_finish(outs, sems), {k: k for k in range(n)},
                mid=lambda ins, outs, sems: _gather_neighbours(outs, sems))


def _exchange_job(arrs, out_shapes, n, copies):
    def start(ins, outs, sems):
        for cp in copies(ins, outs, sems[0], sems[1]):
            cp.start()

    def finish(ins, outs, sems):
        for cp in copies(ins, outs, sems[0], sems[1]):
            cp.wait()

    return _Job(arrs, out_shapes, [pltpu.SemaphoreType.DMA((n,))] * 2, start, finish)


def _pair_exchange_job(grads):
    def copies(ins, outs, send_sem, recv_sem):
        x, y, c, _ = _place()
        res = []
        for w in range(len(grads)):
            half = ins[w].shape[1] // 2
            theirs = pl.ds(pl.multiple_of((1 - c) * half, 8), half)
            res.append(pltpu.make_async_remote_copy(
                src_ref=ins[w].at[:, theirs, :], dst_ref=outs[w], send_sem=send_sem.at[w],
                recv_sem=recv_sem.at[w], device_id=(x, y, 1 - c), device_id_type=MESH))
        return res

    return _exchange_job(grads, [SDS((NCHIP, g.shape[1] // 2, g.shape[2]), F32) for g in grads],
                         len(grads), copies)


def _row_tile(rows, cols):
    tr = rows
    while tr * cols * 4 > ELEMENTWISE_BLOCK_BYTES and tr % 32 == 0:
        tr //= 2
    return tr


def _pair_sums(name, place, gs, sibs):
    n = len(gs)
    half, cols = sibs[0].shape[1], sibs[0].shape[2]
    tr = _row_tile(half, cols)
    nt = half // tr
    mine = nt if gs[0].shape[1] == 2 * half else 0

    def body(pc_ref, *refs):
        del pc_ref
        for g_ref, s_ref, own_ref, out_ref in zip(refs[:n], refs[n:2 * n], refs[2 * n:3 * n], refs[3 * n:]):
            v = g_ref[...] + s_ref[...]

            @pl.when(pl.program_id(1) == 0)
            def _():
                own_ref[...] = v

            @pl.when(pl.program_id(1) > 0)
            def _():
                out_ref[...] = v.astype(BF16)

    res = pl.pallas_call(
        body, name=name,
        grid_spec=pltpu.PrefetchScalarGridSpec(
            num_scalar_prefetch=1, grid=(nt, NCHIP),
            in_specs=[pl.BlockSpec((None, tr, cols), lambda i, s, pc: (pc[0] ^ s, pc[1] * mine + i, 0))] * n
            + [pl.BlockSpec((None, tr, cols), lambda i, s, pc: (pc[0] ^ s, i, 0))] * n,
            out_specs=[pl.BlockSpec((tr, cols), lambda i, s, pc: (i, 0))] * n
            + [pl.BlockSpec((None, tr, cols), lambda i, s, pc: (jnp.maximum(s - 1, 0), i, 0))] * n),
        out_shape=[SDS((half, cols), F32)] * n + [SDS((NCHIP - 1, half, cols), BF16)] * n,
        compiler_params=_cparams(),
    )(place, *gs, *sibs)
    return res[:n], res[n:]


def _chip_exchange_job(parts):
    def copies(ins, outs, send_sem, recv_sem):
        x, y, c, _ = _place()
        res = []
        for w in range(len(parts)):
            for s in range(1, NCHIP):
                cx, cy = _chip_at(x, y, s)
                k = w * (NCHIP - 1) + s - 1
                res.append(pltpu.make_async_remote_copy(
                    src_ref=ins[w].at[s - 1], dst_ref=outs[w].at[s - 1], send_sem=send_sem.at[k],
                    recv_sem=recv_sem.at[k], device_id=(cx, cy, c), device_id_type=MESH))
        return res

    return _exchange_job(parts, [SDS((NCHIP - 1,) + p.shape[1:], BF16) for p in parts],
                         len(parts) * (NCHIP - 1), copies)


def _chip_sums(name, owns, rems):
    n = len(owns)
    half, cols = owns[0].shape
    tr = _row_tile(half, cols)

    def body(*refs):
        for own_ref, rem_ref, out_ref in zip(refs[:n], refs[n:2 * n], refs[2 * n:]):
            out_ref[...] = (((own_ref[...] + rem_ref[0].astype(F32)) + rem_ref[1].astype(F32))
                            + rem_ref[2].astype(F32))

    return pl.pallas_call(
        body, name=name, grid=(half // tr,),
        in_specs=[pl.BlockSpec((tr, cols), lambda i: (i, 0))] * n
        + [pl.BlockSpec((NCHIP - 1, tr, cols), lambda i: (0, i, 0))] * n,
        out_specs=[pl.BlockSpec((tr, cols), lambda i: (i, 0))] * n,
        out_shape=[SDS((half, cols), F32)] * n,
        compiler_params=_cparams(),
    )(*owns, *rems)


def _share_halves_job(halves):
    def copies(ins, outs, send_sem, recv_sem):
        x, y, c, _ = _place()
        return [pltpu.make_async_remote_copy(
            src_ref=ins[w], dst_ref=outs[w], send_sem=send_sem.at[w], recv_sem=recv_sem.at[w],
            device_id=(x, y, 1 - c), device_id_type=MESH) for w in range(len(halves))]

    return _exchange_job(halves, [SDS(h.shape, F32) for h in halves], len(halves), copies)


def _adamw_math(w, g, m, v):
    m = B1 * m + (1.0 - B1) * g
    v = B2 * v + (1.0 - B2) * (g * g)
    m_hat = m / (1.0 - B1 ** STEP)
    v_hat = v / (1.0 - B2 ** STEP)
    delta = -LR * (m_hat / (jnp.sqrt(v_hat) + AEPS) + WD * w)
    return delta, m, v


def _adamws(name, place, ws, owns, sibs, ms, vs):
    n = len(ws)
    rows, cols = ws[0].shape
    by_cols = owns[0].shape[0] == rows
    half, pc_cols = (rows, cols // 2) if by_cols else (rows // 2, cols)
    tr = _row_tile(half, pc_cols)
    nt = half // tr

    def body(pc_ref, *refs):
        ins, outs = refs[:5 * n], refs[5 * n:]
        for k in range(n):
            w_ref, own_ref, sib_ref, m_ref, v_ref = ins[5 * k:5 * k + 5]
            g = jnp.where(pl.program_id(0) == pc_ref[1], own_ref[...], sib_ref[...])
            d, mn, vn = _adamw_math(w_ref[...], g, m_ref[...], v_ref[...])
            for ref, val in zip(outs[4 * k:4 * k + 4], (g, d, mn, vn)):
                ref[...] = val

    full = pl.BlockSpec((tr, pc_cols), (lambda h, i, pc: (i, h)) if by_cols else (lambda h, i, pc: (h * nt + i, 0)))
    part = pl.BlockSpec((tr, pc_cols), lambda h, i, pc: (i, 0))
    res = pl.pallas_call(
        body, name=name,
        grid_spec=pltpu.PrefetchScalarGridSpec(
            num_scalar_prefetch=1, grid=(2, nt),
            in_specs=[full, part, part, full, full] * n, out_specs=[full] * (4 * n)),
        out_shape=[SDS((rows, cols), F32)] * (4 * n),
        compiler_params=_cparams(),
    )(place, *[a for group in zip(ws, owns, sibs, ms, vs) for a in group])
    return [tuple(res[4 * k:4 * k + 4]) for k in range(n)]


ON_SPARSECORE = ("w_down", "w_gate_up")
SC_TILES = 32
SC_LANES = 16


def _cast_gate_up_sparsecore(w):
    rows, cols = w.shape
    groups = rows // 16

    def body(w_hbm, o_hbm, wb, ob):
        tile = lax.axis_index("sc_tile") * 2 + lax.axis_index("sc_core")
        chip = 2 * lax.axis_index("x") + lax.axis_index("y")
        for k in range(-(-groups // SC_TILES)):
            grp = tile + SC_TILES * k

            @pl.when(grp < groups)
            def _():
                rws = pl.ds(pl.multiple_of(grp * 16, 16), 16)
                pltpu.sync_copy(w_hbm.at[rws], wb)

                @pl.loop(0, cols, step=SC_LANES)
                def _(j):
                    for r in range(16):
                        at = (r, pl.ds(j, SC_LANES))
                        ob[at] = wb[at].astype(BF16)

                pltpu.sync_copy(ob, o_hbm.at[chip // 2, rws, pl.ds(pl.multiple_of((chip % 2) * cols, 128), cols)])

    return pl.kernel(
        body, name="cast_sc_w_gate_up", out_type=SDS((2, rows, 2 * cols), BF16),
        mesh=plsc.VectorSubcoreMesh(core_axis_name="sc_core", subcore_axis_name="sc_tile"),
        scratch_types=[pltpu.VMEM((16, cols), F32), pltpu.VMEM((16, cols), BF16)],
    )(w)


def _adamw_sparsecore(name, w, own, sib, m, v):
    rows, cols = w.shape
    groups, half_groups = rows // 8, rows // 16
    rounds = -(-groups // SC_TILES)

    def body(w_hbm, own_hbm, sib_hbm, m_hbm, v_hbm, g_out, d_out, mo_out, vo_out,
             wb, gb, mb, vb, db):
        tile = lax.axis_index("sc_tile") * 2 + lax.axis_index("sc_core")
        c = lax.axis_index("c")
        for k in range(rounds):
            grp = tile + SC_TILES * k

            @pl.when(grp < groups)
            def _():
                rws = pl.ds(pl.multiple_of(grp * 8, 8), 8)
                in_half = pl.ds(pl.multiple_of((grp % half_groups) * 8, 8), 8)
                mine = (grp // half_groups) == c

                @pl.when(mine)
                def _():
                    pltpu.sync_copy(own_hbm.at[in_half], gb)

                @pl.when(jnp.logical_not(mine))
                def _():
                    pltpu.sync_copy(sib_hbm.at[in_half], gb)

                pltpu.sync_copy(w_hbm.at[rws], wb)
                pltpu.sync_copy(m_hbm.at[rws], mb)
                pltpu.sync_copy(v_hbm.at[rws], vb)

                @pl.loop(0, cols, step=SC_LANES)
                def _(j):
                    for r in range(8):
                        at = (r, pl.ds(j, SC_LANES))
                        d, mn, vn = _adamw_math(wb[at], gb[at], mb[at], vb[at])
                        db[at] = d
                        mb[at] = mn
                        vb[at] = vn

                pltpu.sync_copy(gb, g_out.at[rws])
                pltpu.sync_copy(db, d_out.at[rws])
                pltpu.sync_copy(mb, mo_out.at[rws])
                pltpu.sync_copy(vb, vo_out.at[rws])

    return pl.kernel(
        body, name=name, out_type=[SDS((rows, cols), F32)] * 4,
        mesh=plsc.VectorSubcoreMesh(core_axis_name="sc_core", subcore_axis_name="sc_tile"),
        scratch_types=[pltpu.VMEM((8, cols), F32)] * 5,
    )(w, own, sib, m, v)


def _small_allreduce_adamw(sp, wmv, job):
    shape = sp.shape
    ji, jo = len(job.ins), len(job.out_shapes)

    def body(sp_ref, wmv_ref, *rest):
        jin, (g_ref, d_ref, mo_ref, vo_ref), jout = rest[:ji], rest[ji:ji + 4], rest[ji + 4:ji + 4 + jo]
        sib_s, pair_s, chip_s, send_sem, recv_sem = rest[ji + 4 + jo:ji + 9 + jo]
        jsem = rest[ji + 9 + jo:]
        job.start(jin, jout, jsem)
        x, y, c, j = _place()
        cp = pltpu.make_async_remote_copy(
            src_ref=sp_ref, dst_ref=sib_s, send_sem=send_sem.at[0], recv_sem=recv_sem.at[0],
            device_id=(x, y, 1 - c), device_id_type=MESH)
        cp.start()
        cp.wait()
        pair_s[...] = sp_ref[...] + sib_s[...]
        half = shape[0] // 2
        mine = pl.ds(pl.multiple_of(c * half, 8), half)
        cps = []
        for s in range(1, NCHIP):
            cx, cy = _chip_at(x, y, s)
            cp = pltpu.make_async_remote_copy(
                src_ref=pair_s.at[mine], dst_ref=chip_s.at[s, mine], send_sem=send_sem.at[s],
                recv_sem=recv_sem.at[s], device_id=(cx, cy, c), device_id_type=MESH)
            cp.start()
            cps.append(cp)
        chip_s[0] = pair_s[...]
        for cp in cps:
            cp.wait()
        cps = []
        for s in range(1, NCHIP):
            cp = pltpu.make_async_remote_copy(
                src_ref=chip_s.at[s, mine], dst_ref=chip_s.at[s, mine], send_sem=send_sem.at[NCHIP + s],
                recv_sem=recv_sem.at[NCHIP + s], device_id=(x, y, 1 - c), device_id_type=MESH)
            cp.start()
            cps.append(cp)
        for cp in cps:
            cp.wait()
        tot = chip_s[j]
        for k in range(1, NCHIP):
            tot = tot + chip_s[k ^ j]
        g_ref[...] = tot
        d, mn, vn = _adamw_math(wmv_ref[0], tot, wmv_ref[1], wmv_ref[2])
        d_ref[...] = d
        mo_ref[...] = mn
        vo_ref[...] = vn
        job.mid(jin, jout, jsem)
        job.finish(jin, jout, jsem)

    vm = pl.BlockSpec(memory_space=pltpu.VMEM)
    res = pl.pallas_call(
        body, name="small_allreduce_adamw",
        in_specs=[vm] * 2 + [ANY] * ji, out_specs=[vm] * 4 + [ANY] * jo,
        out_shape=[SDS(shape, F32)] * 4 + job.out_shapes,
        scratch_shapes=[pltpu.VMEM(shape, F32), pltpu.VMEM(shape, F32), pltpu.VMEM((NCHIP,) + shape, F32),
                        pltpu.SemaphoreType.DMA((2 * NCHIP,)), pltpu.SemaphoreType.DMA((2 * NCHIP,))] + job.sems,
        input_output_aliases={2 + a: 4 + b for a, b in job.aliases.items()},
        compiler_params=pltpu.CompilerParams(has_side_effects=True),
    )(sp, wmv, *job.ins)
    return res[:4], res[4:]


def _pack_small(first, mix, ln_g, ln_b, b_s, lbt, hn, ffn, fin, w_s):
    rows = [first.reshape(1, D), mix.reshape(1, D), ln_g.reshape(1, D), ln_b.reshape(1, D),
            b_s.reshape(1, D), lbt.reshape(2, D), hn.reshape(1, D), ffn.reshape(1, D), fin.reshape(1, D),
            jnp.zeros((6, D), F32)]
    return jnp.concatenate(rows + [w_s.reshape(NG, GCH, GCH).transpose(1, 0, 2).reshape(GCH, D)], axis=0)


def _unpack_small(p):
    w_s = p[16:].reshape(GCH, NG, GCH).transpose(1, 0, 2).reshape(1, NG, GCH, GCH)
    return dict(norm_mix_g=p[1:2], gmlp_ln_g=p[2:3], gmlp_ln_b=p[3:4], gmlp_b_s=p[4].reshape(1, NG, GCH),
                hgrn_lb_table=p[5:7], hgrn_norm_g=p[7:8], norm_ffn_g=p[8:9], norm_final_g=p[9],
                gmlp_w_s=w_s)


SMALL = ("norm_mix_g", "gmlp_ln_g", "gmlp_ln_b", "gmlp_w_s", "gmlp_b_s", "hgrn_lb_table", "hgrn_norm_g",
         "norm_ffn_g", "norm_final_g")
BIG = ("w_in", "w_gate_up", "w_branch_a", "w_branch_b", "w_out", "w_down")
ORDER = ("norm_mix_g", "w_in", "gmlp_ln_g", "gmlp_ln_b", "gmlp_w_s", "gmlp_b_s", "hgrn_lb_table",
         "hgrn_norm_g", "w_branch_a", "w_branch_b", "w_out", "norm_ffn_g", "w_gate_up", "w_down",
         "norm_final_g")


def kernel(x, norm_mix_g, w_in, gmlp_ln_g, gmlp_ln_b, gmlp_w_s, gmlp_b_s, hgrn_lb_table, hgrn_norm_g, w_branch_a, w_branch_b, w_out, norm_ffn_g, w_gate_up, w_down, norm_final_g, loss_target, m_norm_mix_g, m_w_in, m_gmlp_ln_g, m_gmlp_ln_b, m_gmlp_w_s, m_gmlp_b_s, m_hgrn_lb_table, m_hgrn_norm_g, m_w_branch_a, m_w_branch_b, m_w_out, m_norm_ffn_g, m_w_gate_up, m_w_down, m_norm_final_g, v_norm_mix_g, v_w_in, v_gmlp_ln_g, v_gmlp_ln_b, v_gmlp_w_s, v_gmlp_b_s, v_hgrn_lb_table, v_hgrn_norm_g, v_w_branch_a, v_w_branch_b, v_w_out, v_norm_ffn_g, v_w_gate_up, v_w_down, v_norm_final_g):
    args = dict(locals())
    T = x.shape[1]
    xs = x.reshape(T, D)
    target = loss_target.reshape(T, D)
    big = {n: args[n].reshape(args[n].shape[1:]) for n in BIG}
    big_m = {n: args["m_" + n].reshape(args[n].shape[1:]) for n in BIG}
    big_v = {n: args["v_" + n].reshape(args[n].shape[1:]) for n in BIG}

    x_i, y_i, c_i = lax.axis_index("x"), lax.axis_index("y"), lax.axis_index("c")
    place = jnp.stack([2 * x_i + y_i, c_i]).astype(jnp.int32)
    def by_shape(names):
        groups = []
        for n in names:
            if groups and big[groups[-1][0]].shape == big[n].shape:
                groups[-1].append(n)
            else:
                groups.append([n])
        return groups

    cast = {}
    for grp in by_shape([n for n in BIG if n != "w_gate_up"]):
        cast.update(zip(grp, _cast_shards("cast_" + grp[0], place, [big[n] for n in grp])))
    cast["w_gate_up"] = _cast_gate_up_sparsecore(big["w_gate_up"])
    tril = jnp.tril(jnp.ones((GCH, GCH), bool))
    wm = jnp.where(tril, gmlp_w_s[0], 0.0).astype(BF16)
    wm_t = jnp.swapaxes(wm, 1, 2)
    b_t = gmlp_b_s[0].T

    (proj, hb), w_in4, (w_a4, w_b4, w_out4, w_down4) = _proj_fwd(
        place, xs, norm_mix_g, cast["w_in"], [cast[n] for n in ("w_branch_a", "w_branch_b", "w_out", "w_down")])
    (ab,), _ = _gmlp_fwd(proj, gmlp_ln_g, gmlp_ln_b, wm, b_t)
    (o_raw, obb, st_before), (w_gu,) = _hgrn_fwd(
        proj, hgrn_lb_table, hgrn_norm_g, job=_gather_job([cast["w_gate_up"]]))
    w_a, w_b, w_o = (w.reshape(D, D) for w in (w_a4, w_b4, w_out4))
    (mgb, x1), _ = _merge_fwd(xs, ab, obb, proj, w_a, w_b, w_o)
    w_dn = w_down4.reshape(FF, D)
    act, dx2b, h2b, dgu, dx1, dx1b, acc_ffn = _ffn_fwd_bwd(
        x1, target, norm_ffn_g, norm_final_g.reshape(1, D), w_gu, w_dn)

    grads, owns, parts, halves, sibh = {}, {}, {}, {}, {}

    def pair_sums(names, sibs):
        sib_of = dict(zip(names, sibs))
        for grp in by_shape(names):
            o, p = _pair_sums("rs_pair_sum_" + grp[0], place, [grads[n] for n in grp], [sib_of[n] for n in grp])
            owns.update(zip(grp, o))
            parts.update(zip(grp, p))

    def chip_sums(names, got):
        rem_of = dict(zip(names, got))
        for grp in by_shape(names):
            h = _chip_sums("rs_chip_sum_" + grp[0], [owns[n] for n in grp], [rem_of[n] for n in grp])
            halves.update(zip(grp, h))

    ffn, mix = ("w_gate_up", "w_down"), ("w_branch_a", "w_branch_b", "w_out")
    grads["w_gate_up"], _ = _dw_gate_up(h2b, dgu)
    grads["w_down"], _ = _dw_down(act, dx2b)
    (dya, dyb, dproj), got = _merge_bwd(
        dx1b, ab, obb, proj, w_o, w_a, w_b, job=_pair_exchange_job([grads[n] for n in ffn]))
    pair_sums(ffn, got)
    grads["w_branch_a"], _ = _dw_square("dw_branch_a", ab, dya)
    grads["w_branch_b"], _ = _dw_square("dw_branch_b", obb, dyb)
    grads["w_out"], _ = _dw_square("dw_out", mgb, dx1b)
    (dproj, acc_hgrn), got = _hgrn_bwd(
        dproj, dyb, w_b, o_raw, proj, st_before, hgrn_lb_table, hgrn_norm_g,
        job=_join_jobs(_chip_exchange_job([parts[n] for n in ffn]), _pair_exchange_job([grads[n] for n in mix])))
    chip_sums(ffn, got[:2])
    pair_sums(mix, got[2:])
    dproj, acc_ln, dws, dmix = _gmlp_bwd(dproj, dya, w_a, proj, gmlp_ln_g, gmlp_ln_b, wm, wm_t, b_t)
    for_sibling, got = _dw_in_half(
        "dw_in_sibling_half", place, hb, dproj, False,
        job=_join_jobs(_share_halves_job([halves[n] for n in ffn]), _chip_exchange_job([parts[n] for n in mix])))
    sibh.update(zip(ffn, got[:2]))
    chip_sums(mix, got[2:])
    grads["w_in"], got = _dw_in_half(
        "dw_in_own_half", place, hb, dproj, True, job=_share_halves_job([for_sibling]))
    pair_sums(("w_in",), got)
    (grad_x, acc_mix), got = _proj_bwd(
        dproj, w_in4, xs, dx1, norm_mix_g,
        job=_join_jobs(_chip_exchange_job([parts["w_in"]]), _share_halves_job([halves[n] for n in mix])))
    chip_sums(("w_in",), got[:1])
    sibh.update(zip(mix, got[1:]))

    lbv = jax.nn.sigmoid(hgrn_lb_table[0] - hgrn_lb_table[1])
    d_t0 = jnp.sum(acc_hgrn[0], axis=0) * lbv * (1.0 - lbv)
    loss_row = jnp.zeros((D,), F32).at[0].set(jnp.sum(acc_ffn[0]))
    dws_m = jnp.where(tril[:, None, :], dws.reshape(GCH, NG, GCH), 0.0).transpose(1, 0, 2)
    db_s = jnp.sum(dmix.reshape(GCH, NG, GCH), axis=-1).T
    sp = _pack_small(loss_row, jnp.sum(acc_mix, 0), jnp.sum(acc_ln[0], 0), jnp.sum(acc_ln[1], 0), db_s,
                     jnp.stack([d_t0, -d_t0]), jnp.sum(acc_hgrn[1], 0), jnp.sum(acc_ffn[2], 0),
                     jnp.sum(acc_ffn[1], 0), dws_m)
    zero = jnp.zeros((D,), F32)

    def pack(prefix):
        a = lambda n: args[prefix + n]
        return _pack_small(zero, a("norm_mix_g"), a("gmlp_ln_g"), a("gmlp_ln_b"), a("gmlp_b_s"),
                           a("hgrn_lb_table"), a("hgrn_norm_g"), a("norm_ffn_g"), a("norm_final_g"),
                           a("gmlp_w_s"))

    packed, (sibh["w_in"],) = _small_allreduce_adamw(
        sp, jnp.stack([pack(""), pack("m_"), pack("v_")]), _share_halves_job([halves["w_in"]]))
    loss = packed[0][0, 0]
    small = [_unpack_small(p) for p in packed]
    out = {n: tuple(s[n] for s in small) for n in SMALL}
    for grp in by_shape(BIG):
        if grp[0] in ON_SPARSECORE:
            res = [_adamw_sparsecore("adamw_sc_" + n, big[n], halves[n], sibh[n], big_m[n], big_v[n]) for n in grp]
        else:
            res = _adamws("adamw_" + grp[0], place,
                          *[[d[n] for n in grp] for d in (big, halves, sibh, big_m, big_v)])
        for n, quad in zip(grp, res):
            out[n] = tuple(a.reshape(args[n].shape) for a in quad)
    return (loss, grad_x.reshape(x.shape), *[out[n][0] for n in ORDER], *[out[n][1] for n in ORDER],
            *[out[n][2] for n in ORDER], *[out[n][3] for n in ORDER])
```
